```python
import jax, jax.numpy as jnp
from jax import lax
import numpy as np

D_MODEL = 1024
BATCH = 8
SEQ = 4096
DEPTH = 1

CHUNK = 64
MIX_WIDTH = D_MODEL
ATTN_WIDTH = MIX_WIDTH // 2
N_HEADS = 8
HEAD_DIM = ATTN_WIDTH // N_HEADS
POOL_WIDTH = MIX_WIDTH - ATTN_WIDTH
POOL_WINDOWS = (2, 4, 8, 16)
N_POOL_GROUPS = len(POOL_WINDOWS)
POOL_GROUP = POOL_WIDTH // N_POOL_GROUPS
IN_WIDTH = 3 * ATTN_WIDTH + N_HEADS + POOL_WIDTH
D_FF = -(-(8 * D_MODEL) // (3 * 256)) * 256
Q_BLOCK = 128
EPS = 1e-6

kernel_name = "fox_pool_hybrid_block"


def rmsnorm(x, g):
    xf = x.astype(jnp.float32)
    y = xf * lax.rsqrt(jnp.mean(xf * xf, axis=-1, keepdims=True) + EPS)
    return (y * g.astype(jnp.float32)).astype(x.dtype)


def forgetting_attention(q, k, v, log_f):
    S, Dh = q.shape[2], q.shape[3]
    c = jnp.cumsum(log_f, axis=-1)
    qf = q.astype(jnp.float32) * (Dh ** -0.5)
    kf = k.astype(jnp.float32)
    vf = v.astype(jnp.float32)
    outs = []
    for i in range(S // Q_BLOCK):
        q0, q1 = i * Q_BLOCK, (i + 1) * Q_BLOCK
        logits = jnp.einsum('bhqd,bhkd->bhqk', qf[:, :, q0:q1], kf[:, :, :q1])
        logits = logits + c[:, :, q0:q1, None] - c[:, :, None, :q1]
        t_pos = jnp.arange(q0, q1)[:, None]
        s_pos = jnp.arange(q1)[None, :]
        logits = jnp.where(s_pos <= t_pos, logits, -jnp.inf)
        p = jax.nn.softmax(logits, axis=-1)
        outs.append(jnp.einsum('bhqk,bhkd->bhqd', p, vf[:, :, :q1]))
    return jnp.concatenate(outs, axis=2).astype(q.dtype)


def multiscale_pool(u, w_pool, pool_scale):
    B, S, C = u.shape
    uf = u.astype(jnp.float32)
    cs = jnp.concatenate([jnp.zeros((B, 1, C), jnp.float32), jnp.cumsum(uf, axis=1)], axis=1)
    t = jnp.arange(S)
    groups = []
    for g, w in enumerate(POOL_WINDOWS):
        lo, hi = g * POOL_GROUP, (g + 1) * POOL_GROUP
        start = jnp.maximum(t + 1 - w, 0)
        csg = cs[:, :, lo:hi]
        window_sum = csg[:, 1:] - csg[:, start]
        count = (t + 1 - start).astype(jnp.float32)[None, :, None]
        groups.append(window_sum / count - uf[:, :, lo:hi])
    pooled = jnp.stack(groups, axis=2)
    mixed = jnp.einsum('bsgc,gcd->bsgd', pooled, w_pool.astype(jnp.float32)).reshape(B, S, C)
    return (mixed * pool_scale.astype(jnp.float32)).astype(u.dtype)


def _fwd_setup_inputs(seed: int = 0) -> dict:
    key = jax.random.key(seed)
    ks = jax.random.split(key, 13)
    f32 = jnp.float32
    nrm = lambda k, shape, scale: jax.random.normal(k, shape, f32) * scale
    return {
        "x": jax.random.normal(ks[0], (BATCH, SEQ, D_MODEL), f32),
        "norm1_g": 1.0 + nrm(ks[1], (DEPTH, D_MODEL), 0.05),
        "w_in": nrm(ks[2], (DEPTH, D_MODEL, IN_WIDTH), D_MODEL ** -0.5),
        "b_forget": 2.0 + nrm(ks[3], (DEPTH, N_HEADS), 0.5),
        "w_pool": nrm(ks[4], (DEPTH, N_POOL_GROUPS, POOL_GROUP, POOL_GROUP), POOL_GROUP ** -0.5),
        "pool_scale": 1.0 + nrm(ks[5], (DEPTH, POOL_WIDTH), 0.1),
        "w_out": nrm(ks[6], (DEPTH, MIX_WIDTH, D_MODEL), MIX_WIDTH ** -0.5),
        "norm2_g": 1.0 + nrm(ks[7], (DEPTH, D_MODEL), 0.05),
        "w_gate": nrm(ks[8], (DEPTH, D_MODEL, D_FF), D_MODEL ** -0.5),
        "w_up": nrm(ks[9], (DEPTH, D_MODEL, D_FF), D_MODEL ** -0.5),
        "w_down": nrm(ks[10], (DEPTH, D_FF, D_MODEL), D_FF ** -0.5),
        "final_g": 1.0 + nrm(ks[11], (D_MODEL,), 0.05),
    }


def _fwd_reference(x, norm1_g, w_in, b_forget, w_pool, pool_scale, w_out, norm2_g, w_gate, w_up, w_down, final_g):
    B, S, _ = x.shape
    for layer in range(DEPTH):
        h = rmsnorm(x, norm1_g[layer])
        proj = jnp.einsum('bsd,de->bse', h, w_in[layer])
        a0 = ATTN_WIDTH
        q = proj[..., 0:a0]
        k = proj[..., a0:2 * a0]
        v = proj[..., 2 * a0:3 * a0]
        f_logit = proj[..., 3 * a0:3 * a0 + N_HEADS]
        u = proj[..., 3 * a0 + N_HEADS:]
        to_heads = lambda t: t.reshape(B, S, N_HEADS, HEAD_DIM).transpose(0, 2, 1, 3)
        log_f = jax.nn.log_sigmoid(f_logit.astype(jnp.float32) + b_forget[layer].astype(jnp.float32))
        log_f = log_f.transpose(0, 2, 1)
        attn = forgetting_attention(to_heads(q), to_heads(k), to_heads(v), log_f)
        attn = attn.transpose(0, 2, 1, 3).reshape(B, S, ATTN_WIDTH)
        pool = multiscale_pool(u, w_pool[layer], pool_scale[layer])
        mixed = jnp.concatenate([attn, pool], axis=-1)
        x = x + jnp.einsum('bse,ed->bsd', mixed, w_out[layer])
        h2 = rmsnorm(x, norm2_g[layer])
        gate = jnp.einsum('bsd,df->bsf', h2, w_gate[layer])
        up = jnp.einsum('bsd,df->bsf', h2, w_up[layer])
        x = x + jnp.einsum('bsf,fd->bsd', jax.nn.silu(gate) * up, w_down[layer])
    return rmsnorm(x, final_g)


import jax as _jax
import jax.numpy as _jnp

TWIN_FORMAT = 'train_step'
FWD_PARAMS = ['x', 'norm1_g', 'w_in', 'b_forget', 'w_pool', 'pool_scale', 'w_out', 'norm2_g', 'w_gate', 'w_up', 'w_down', 'final_g']
TWIN_WEIGHTS = ['norm1_g', 'w_in', 'b_forget', 'w_pool', 'pool_scale', 'w_out', 'norm2_g', 'w_gate', 'w_up', 'w_down', 'final_g']
TWIN_DIFF_INPUT = 'x'
TWIN_INPUTS = ['x', 'norm1_g', 'w_in', 'b_forget', 'w_pool', 'pool_scale', 'w_out', 'norm2_g', 'w_gate', 'w_up', 'w_down', 'final_g', 'loss_target', 'm_norm1_g', 'm_w_in', 'm_b_forget', 'm_w_pool', 'm_pool_scale', 'm_w_out', 'm_norm2_g', 'm_w_gate', 'm_w_up', 'm_w_down', 'm_final_g', 'v_norm1_g', 'v_w_in', 'v_b_forget', 'v_w_pool', 'v_pool_scale', 'v_w_out', 'v_norm2_g', 'v_w_gate', 'v_w_up', 'v_w_down', 'v_final_g']
TWIN_OUTPUTS = ['loss', 'grad_x', 'grad_norm1_g', 'grad_w_in', 'grad_b_forget', 'grad_w_pool', 'grad_pool_scale', 'grad_w_out', 'grad_norm2_g', 'grad_w_gate', 'grad_w_up', 'grad_w_down', 'grad_final_g', 'delta_norm1_g', 'delta_w_in', 'delta_b_forget', 'delta_w_pool', 'delta_pool_scale', 'delta_w_out', 'delta_norm2_g', 'delta_w_gate', 'delta_w_up', 'delta_w_down', 'delta_final_g', 'new_m_norm1_g', 'new_m_w_in', 'new_m_b_forget', 'new_m_w_pool', 'new_m_pool_scale', 'new_m_w_out', 'new_m_norm2_g', 'new_m_w_gate', 'new_m_w_up', 'new_m_w_down', 'new_m_final_g', 'new_v_norm1_g', 'new_v_w_in', 'new_v_b_forget', 'new_v_w_pool', 'new_v_pool_scale', 'new_v_w_out', 'new_v_norm2_g', 'new_v_w_gate', 'new_v_w_up', 'new_v_w_down', 'new_v_final_g']
TWIN_LEAF_KINDS = {'loss': 'loss', 'grad_x': 'grad_x', 'grad_norm1_g': 'grad_w', 'grad_w_in': 'grad_w', 'grad_b_forget': 'grad_w', 'grad_w_pool': 'grad_w', 'grad_pool_scale': 'grad_w', 'grad_w_out': 'grad_w', 'grad_norm2_g': 'grad_w', 'grad_w_gate': 'grad_w', 'grad_w_up': 'grad_w', 'grad_w_down': 'grad_w', 'grad_final_g': 'grad_w', 'delta_norm1_g': 'delta_w', 'delta_w_in': 'delta_w', 'delta_b_forget': 'delta_w', 'delta_w_pool': 'delta_w', 'delta_pool_scale': 'delta_w', 'delta_w_out': 'delta_w', 'delta_norm2_g': 'delta_w', 'delta_w_gate': 'delta_w', 'delta_w_up': 'delta_w', 'delta_w_down': 'delta_w', 'delta_final_g': 'delta_w', 'new_m_norm1_g': 'new_m', 'new_m_w_in': 'new_m', 'new_m_b_forget': 'new_m', 'new_m_w_pool': 'new_m', 'new_m_pool_scale': 'new_m', 'new_m_w_out': 'new_m', 'new_m_norm2_g': 'new_m', 'new_m_w_gate': 'new_m', 'new_m_w_up': 'new_m', 'new_m_w_down': 'new_m', 'new_m_final_g': 'new_m', 'new_v_norm1_g': 'new_v', 'new_v_w_in': 'new_v', 'new_v_b_forget': 'new_v', 'new_v_w_pool': 'new_v', 'new_v_pool_scale': 'new_v', 'new_v_w_out': 'new_v', 'new_v_norm2_g': 'new_v', 'new_v_w_gate': 'new_v', 'new_v_w_up': 'new_v', 'new_v_w_down': 'new_v', 'new_v_final_g': 'new_v'}


def _forward(args):
    return _fwd_reference(*[args[k] for k in FWD_PARAMS])


def _output_shape():
    out = _jax.eval_shape(lambda: _forward(_fwd_setup_inputs(0)))
    return out.shape, out.dtype

N_MICROBATCH = 1
ADAM_LR = 0.001
ADAM_B1 = 0.9
ADAM_B2 = 0.999
ADAM_EPS = 1e-08
ADAM_WD = 0.01
ADAM_STEP = 10
PER_EXAMPLE_BATCH_AXIS = {'x': 0, 'loss_target': 0}
SHARED_INPUTS = []
_WEIGHT_DTYPES = {'norm1_g': _jnp.float32, 'w_in': _jnp.float32, 'b_forget': _jnp.float32, 'w_pool': _jnp.float32, 'pool_scale': _jnp.float32, 'w_out': _jnp.float32, 'norm2_g': _jnp.float32, 'w_gate': _jnp.float32, 'w_up': _jnp.float32, 'w_down': _jnp.float32, 'final_g': _jnp.float32}
MOMENT_SCALE = {'norm1_g': 1.212740e-01, 'w_in': 8.693172e-02, 'b_forget': 6.418034e-01, 'w_pool': 1.438518e-01, 'pool_scale': 1.576581e-01, 'w_out': 1.139927e-01, 'norm2_g': 1.133142e-01, 'w_gate': 4.904316e-02, 'w_up': 4.766971e-02, 'w_down': 7.945436e-02, 'final_g': 3.208134e+01}


def _to_microbatches(a, axis):
    t = _jnp.moveaxis(a, axis, 0)
    t = t.reshape((N_MICROBATCH, t.shape[0] // N_MICROBATCH) + t.shape[1:])
    return _jnp.moveaxis(t, 1, axis + 1)


def setup_inputs(seed: int = 0) -> dict:
    inp = _fwd_setup_inputs(seed)
    key = _jax.random.fold_in(_jax.random.key(seed), 7919)
    shape, _ = _output_shape()
    out = dict(inp)
    out["loss_target"] = _jax.random.normal(_jax.random.fold_in(key, 0), shape, _jnp.float32)
    for i, name in enumerate(TWIN_WEIGHTS):
        w = inp[name].astype(_jnp.float32)
        if MOMENT_SCALE is None:
            s = _jnp.sqrt(_jnp.mean(_jnp.square(w)) + 1e-30)
        else:
            s = MOMENT_SCALE[name]
        km, kv = _jax.random.split(_jax.random.fold_in(key, i + 1))
        out[name] = w
        out["m_" + name] = s * _jax.random.normal(km, w.shape, _jnp.float32)
        out["v_" + name] = (s * s) * _jax.random.uniform(kv, w.shape, _jnp.float32, 0.5, 1.5)
    if N_MICROBATCH > 1:
        for name, axis in PER_EXAMPLE_BATCH_AXIS.items():
            out[name] = _to_microbatches(out[name], axis)
    return {'x': out['x'], 'norm1_g': out['norm1_g'], 'w_in': out['w_in'], 'b_forget': out['b_forget'], 'w_pool': out['w_pool'], 'pool_scale': out['pool_scale'], 'w_out': out['w_out'], 'norm2_g': out['norm2_g'], 'w_gate': out['w_gate'], 'w_up': out['w_up'], 'w_down': out['w_down'], 'final_g': out['final_g'], 'loss_target': out['loss_target'], 'm_norm1_g': out['m_norm1_g'], 'm_w_in': out['m_w_in'], 'm_b_forget': out['m_b_forget'], 'm_w_pool': out['m_w_pool'], 'm_pool_scale': out['m_pool_scale'], 'm_w_out': out['m_w_out'], 'm_norm2_g': out['m_norm2_g'], 'm_w_gate': out['m_w_gate'], 'm_w_up': out['m_w_up'], 'm_w_down': out['m_w_down'], 'm_final_g': out['m_final_g'], 'v_norm1_g': out['v_norm1_g'], 'v_w_in': out['v_w_in'], 'v_b_forget': out['v_b_forget'], 'v_w_pool': out['v_w_pool'], 'v_pool_scale': out['v_pool_scale'], 'v_w_out': out['v_w_out'], 'v_norm2_g': out['v_norm2_g'], 'v_w_gate': out['v_w_gate'], 'v_w_up': out['v_w_up'], 'v_w_down': out['v_w_down'], 'v_final_g': out['v_final_g']}


def _loss(weights, diff, rest, loss_target):
    with _jax.named_scope("forward"):
        args = {**rest, TWIN_DIFF_INPUT: diff, **{k: w.astype(_WEIGHT_DTYPES[k]) for k, w in weights.items()}}
        y = _forward(args)
    with _jax.named_scope("loss_head"):
        err = _jnp.square(y.astype(_jnp.float32) - loss_target)
        return 0.5 * _jnp.sum(_jnp.mean(err, axis=-1)) if err.ndim else 0.5 * err


def _adamw(w, g, m, v):
    m = ADAM_B1 * m + (1.0 - ADAM_B1) * g
    v = ADAM_B2 * v + (1.0 - ADAM_B2) * _jnp.square(g)
    m_hat = m / (1.0 - ADAM_B1 ** ADAM_STEP)
    v_hat = v / (1.0 - ADAM_B2 ** ADAM_STEP)
    delta = -ADAM_LR * (m_hat / (_jnp.sqrt(v_hat) + ADAM_EPS) + ADAM_WD * w)
    return delta, m, v


def reference(x, norm1_g, w_in, b_forget, w_pool, pool_scale, w_out, norm2_g, w_gate, w_up, w_down, final_g, loss_target, m_norm1_g, m_w_in, m_b_forget, m_w_pool, m_pool_scale, m_w_out, m_norm2_g, m_w_gate, m_w_up, m_w_down, m_final_g, v_norm1_g, v_w_in, v_b_forget, v_w_pool, v_pool_scale, v_w_out, v_norm2_g, v_w_gate, v_w_up, v_w_down, v_final_g):
    given = dict(x=x, norm1_g=norm1_g, w_in=w_in, b_forget=b_forget, w_pool=w_pool, pool_scale=pool_scale, w_out=w_out, norm2_g=norm2_g, w_gate=w_gate, w_up=w_up, w_down=w_down, final_g=final_g, loss_target=loss_target, m_norm1_g=m_norm1_g, m_w_in=m_w_in, m_b_forget=m_b_forget, m_w_pool=m_w_pool, m_pool_scale=m_pool_scale, m_w_out=m_w_out, m_norm2_g=m_norm2_g, m_w_gate=m_w_gate, m_w_up=m_w_up, m_w_down=m_w_down, m_final_g=m_final_g, v_norm1_g=v_norm1_g, v_w_in=v_w_in, v_b_forget=v_b_forget, v_w_pool=v_w_pool, v_pool_scale=v_pool_scale, v_w_out=v_w_out, v_norm2_g=v_norm2_g, v_w_gate=v_w_gate, v_w_up=v_w_up, v_w_down=v_w_down, v_final_g=v_final_g)
    weights = {n: given[n] for n in TWIN_WEIGHTS}
    shared = {n: given[n] for n in SHARED_INPUTS}
    per_example = {n: given[n] for n in ['x']}
    grad_fn = _jax.value_and_grad(_loss, argnums=(0, 1))

    def one_microbatch(ex, loss_target):
        ex = dict(ex)
        diff = ex.pop(TWIN_DIFF_INPUT)
        return grad_fn(weights, diff, {**shared, **ex}, loss_target)

    if N_MICROBATCH == 1:
        loss, (grad_w, grad_x) = one_microbatch(per_example, given["loss_target"])
    else:
        def body(carry, xs):
            loss_sum, grad_sum = carry
            l_k, (gw_k, gx_k) = one_microbatch(xs[0], xs[1])
            with _jax.named_scope("update"):
                return (loss_sum + l_k, _jax.tree.map(_jnp.add, grad_sum, gw_k)), gx_k

        init = (_jnp.zeros((), _jnp.float32), _jax.tree.map(_jnp.zeros_like, weights))
        (loss, grad_w), grad_x = _jax.lax.scan(body, init, (per_example, given["loss_target"]))
    with _jax.named_scope("update"):
        delta_w, new_m, new_v = {}, {}, {}
        for n in TWIN_WEIGHTS:
            delta_w[n], new_m[n], new_v[n] = _adamw(weights[n], grad_w[n], given["m_" + n], given["v_" + n])
    return (loss, grad_x, *[grad_w[n] for n in TWIN_WEIGHTS], *[delta_w[n] for n in TWIN_WEIGHTS],
            *[new_m[n] for n in TWIN_WEIGHTS], *[new_v[n] for n in TWIN_WEIGHTS])
```

```python
import functools

import jax
import jax.numpy as jnp
import numpy as np
from jax import lax
from jax.experimental import pallas as pl
from jax.experimental.pallas import tpu as pltpu

f32 = jnp.float32
bf16 = jnp.bfloat16

T = 4096
D = 1024
NSH = 4
IN_W = 2056
IN_S = IN_W // NSH
AW = 512
PAIRS = 4
FF = 2816
FS = FF // NSH
WINDOWS = (2, 4, 8, 16)
HALO = 16
EPS = 1e-6
NEG = -1e30
LR, B1, B2, AEPS, WD, STEP = 0.001, 0.9, 0.999, 1e-08, 0.01, 10
SMALL_ROWS = 552

NT = (((1,), (1,)), ((), ()))
TN = (((0,), (0,)), ((), ()))

MESH = pl.DeviceIdType.MESH


def _cp(*sem):
    return pltpu.CompilerParams(dimension_semantics=sem)


def _full(shape):
    n = len(shape)
    return pl.BlockSpec(shape, lambda *_: (0,) * n)


def _rms_inproj(x, g1, wm, wf):
    tm = 512

    def body(x_ref, g_ref, wm_ref, wf_ref, h_ref, qkv_ref, u_ref, fl_ref):
        xv = x_ref[...]
        r = lax.rsqrt(jnp.mean(xv * xv, axis=-1, keepdims=True) + EPS)
        h = (xv * r * g_ref[...]).astype(bf16)
        h_ref[...] = h
        qkv_ref[...] = jnp.dot(h, wm_ref[:, 0:3 * AW], preferred_element_type=f32).astype(bf16)
        u_ref[...] = jnp.dot(h, wm_ref[:, 3 * AW:4 * AW], preferred_element_type=f32)
        fl_ref[...] = jnp.dot(h, wf_ref[...], preferred_element_type=f32)

    return pl.pallas_call(
        body, name="rms_inproj", grid=(T // tm,),
        in_specs=[pl.BlockSpec((tm, D), lambda i: (i, 0)), _full((1, D)), _full((D, 4 * AW)), _full((D, 128))],
        out_specs=[pl.BlockSpec((tm, D), lambda i: (i, 0)), pl.BlockSpec((tm, 3 * AW), lambda i: (i, 0)),
                   pl.BlockSpec((tm, AW), lambda i: (i, 0)), pl.BlockSpec((tm, 128), lambda i: (i, 0))],
        out_shape=[jax.ShapeDtypeStruct((T, D), bf16), jax.ShapeDtypeStruct((T, 3 * AW), bf16),
                   jax.ShapeDtypeStruct((T, AW), f32), jax.ShapeDtypeStruct((T, 128), f32)],
        compiler_params=_cp("parallel"),
    )(x, g1, wm, wf)


def _log_sigmoid(z):
    return jnp.minimum(z, 0.0) - jnp.log(1.0 + jnp.exp(-jnp.abs(z)))


def _fox_cumsum(fl, bfp):
    nb = T // 128

    def body(fl_ref, b_ref, ccol_ref, crow_ref, carry):
        i = pl.program_id(0)

        @pl.when(i == 0)
        def _():
            carry[...] = jnp.zeros_like(carry)

        lf = _log_sigmoid(fl_ref[...] + b_ref[...])
        r = lax.broadcasted_iota(jnp.int32, (128, 128), 0)
        cc = lax.broadcasted_iota(jnp.int32, (128, 128), 1)
        ltri = (cc <= r).astype(f32)
        cb = jnp.dot(ltri, lf, precision=lax.Precision.HIGHEST, preferred_element_type=f32) + carry[0:1, :]
        carry[...] = jnp.broadcast_to(cb[127:128, :], (8, 128))
        for p in range(PAIRS):
            ce = jnp.broadcast_to(cb[:, 2 * p:2 * p + 1], (128, 128))
            co = jnp.broadcast_to(cb[:, 2 * p + 1:2 * p + 2], (128, 128))
            ccol_ref[p] = jnp.where(cc < 64, ce, co)
        crow_ref[...] = cb.T[0:8, :]

    return pl.pallas_call(
        body, name="fox_cumsum", grid=(nb,),
        in_specs=[pl.BlockSpec((128, 128), lambda i: (i, 0)), _full((1, 128))],
        out_specs=[pl.BlockSpec((PAIRS, 128, 128), lambda i: (0, i, 0)), pl.BlockSpec((8, 128), lambda i: (0, i))],
        out_shape=[jax.ShapeDtypeStruct((PAIRS, T, 128), f32), jax.ShapeDtypeStruct((8, T), f32)],
        scratch_shapes=[pltpu.VMEM((8, 128), f32)],
        compiler_params=_cp("arbitrary"),
    )(fl, bfp)


ATT_T = 512


def _causal_steps(key_major):
    n = T // ATT_T
    if key_major:
        pairs = [(i, j) for j in range(n) for i in range(j, n)]
    else:
        pairs = [(i, j) for i in range(n) for j in range(i + 1)]
    it = np.array([p[0] for p in pairs], np.int32)
    jt = np.array([p[1] for p in pairs], np.int32)
    return jnp.asarray(it), jnp.asarray(jt)


def _attn_fwd(qkv, ccol, crow2):
    tq = tk = ATT_T
    it, jt = _causal_steps(False)
    nsteps = it.shape[0]

    def body(it_ref, jt_ref, q_ref, k_ref, v_ref, cq_ref, ck_ref, o_ref, rb_ref, m_sc, l_sc, acc_sc):
        t = pl.program_id(1)
        i = it_ref[t]
        j = jt_ref[t]

        @pl.when(j == 0)
        def _():
            m_sc[...] = jnp.full_like(m_sc, NEG)
            l_sc[...] = jnp.zeros_like(l_sc)
            acc_sc[...] = jnp.zeros_like(acc_sc)

        lane = lax.broadcasted_iota(jnp.int32, (tq, 128), 1)
        row = lax.broadcasted_iota(jnp.int32, (tq, tk), 0)
        col = lax.broadcasted_iota(jnp.int32, (tq, tk), 1)
        causal = jnp.logical_or(col <= row, j < i)
        q = q_ref[...] * 0.125
        k = k_ref[...]
        v = v_ref[...]
        for e in range(2):
            hm = (lane >= 64) if e else (lane < 64)
            qe = jnp.where(hm, q, jnp.zeros_like(q))
            s = lax.dot_general(qe, k, NT, preferred_element_type=f32)
            s = s + cq_ref[:, 64 * e:64 * e + 1] - ck_ref[e:e + 1, :]
            s = jnp.where(causal, s, NEG)
            m_prev = m_sc[e]
            m_new = jnp.maximum(m_prev, jnp.max(s, axis=1, keepdims=True))
            alpha = jnp.exp(m_prev - m_new)
            p = jnp.exp(s - m_new)
            l_sc[e] = alpha * l_sc[e] + jnp.sum(p, axis=1, keepdims=True)
            acc_sc[e] = alpha * acc_sc[e] + jnp.dot(p.astype(bf16), v, preferred_element_type=f32)
            m_sc[e] = m_new

        @pl.when(j == i)
        def _():
            o0 = acc_sc[0] / l_sc[0]
            o1 = acc_sc[1] / l_sc[1]
            o_ref[...] = jnp.where(lane < 64, o0, o1).astype(bf16)
            rb0 = cq_ref[:, 0:1] - (m_sc[0] + jnp.log(l_sc[0]))
            rb1 = cq_ref[:, 64:65] - (m_sc[1] + jnp.log(l_sc[1]))
            rb_ref[...] = jnp.where(lane < 64, rb0, rb1)

    grid_spec = pltpu.PrefetchScalarGridSpec(
        num_scalar_prefetch=2, grid=(PAIRS, nsteps),
        in_specs=[pl.BlockSpec((tq, 128), lambda p, t, it, jt: (it[t], p)),
                  pl.BlockSpec((tk, 128), lambda p, t, it, jt: (jt[t], PAIRS + p)),
                  pl.BlockSpec((tk, 128), lambda p, t, it, jt: (jt[t], 2 * PAIRS + p)),
                  pl.BlockSpec((None, tq, 128), lambda p, t, it, jt: (p, it[t], 0)),
                  pl.BlockSpec((None, 2, tk), lambda p, t, it, jt: (p, 0, jt[t]))],
        out_specs=[pl.BlockSpec((tq, 128), lambda p, t, it, jt: (it[t], p)),
                   pl.BlockSpec((None, tq, 128), lambda p, t, it, jt: (p, it[t], 0))],
        scratch_shapes=[pltpu.VMEM((2, tq, 1), f32), pltpu.VMEM((2, tq, 1), f32), pltpu.VMEM((2, tq, 128), f32)],
    )
    return pl.pallas_call(
        body, name="fox_attn_fwd", grid_spec=grid_spec,
        out_shape=[jax.ShapeDtypeStruct((T, AW), bf16), jax.ShapeDtypeStruct((PAIRS, T, 128), f32)],
        compiler_params=_cp("parallel", "arbitrary"),
    )(it, jt, qkv, qkv, qkv, ccol, crow2)


def _pool_fwd(u, wp, scale):
    tm = 512

    def body(u_ref, wp_ref, sc_ref, pooled_ref, pool_ref, ext):
        i = pl.program_id(0)

        @pl.when(i == 0)
        def _():
            ext[0:HALO, :] = jnp.zeros((HALO, AW), f32)

        uv = u_ref[...]
        ext[HALO:HALO + tm, :] = uv
        t_idx = i * tm + lax.broadcasted_iota(jnp.int32, (tm, 1), 0)
        for g, w in enumerate(WINDOWS):
            lo, hi = 128 * g, 128 * (g + 1)
            ug = uv[:, lo:hi]
            acc = ug
            for d in range(1, w):
                acc = acc + ext[HALO - d:HALO - d + tm, lo:hi]
            cnt = jnp.minimum(t_idx + 1, w).astype(f32)
            pb = (acc / cnt - ug).astype(bf16)
            pooled_ref[:, lo:hi] = pb
            mixed = jnp.dot(pb, wp_ref[g], preferred_element_type=f32)
            pool_ref[:, lo:hi] = (mixed * sc_ref[:, lo:hi]).astype(bf16)
        ext[0:HALO, :] = uv[tm - HALO:tm, :]

    return pl.pallas_call(
        body, name="pool_fwd", grid=(T // tm,),
        in_specs=[pl.BlockSpec((tm, AW), lambda i: (i, 0)), _full((4, 128, 128)), _full((1, AW))],
        out_specs=[pl.BlockSpec((tm, AW), lambda i: (i, 0)), pl.BlockSpec((tm, AW), lambda i: (i, 0))],
        out_shape=[jax.ShapeDtypeStruct((T, AW), bf16), jax.ShapeDtypeStruct((T, AW), bf16)],
        scratch_shapes=[pltpu.VMEM((tm + HALO, AW), f32)],
        compiler_params=_cp("arbitrary"),
    )(u, wp, scale)


def _outproj(x, attn, pool, wo, g2):
    tm = 512

    def body(x_ref, a_ref, p_ref, wo_ref, g_ref, x1_ref, h2_ref):
        x1 = x_ref[...] + jnp.dot(a_ref[...], wo_ref[0:AW, :], preferred_element_type=f32)
        x1 = x1 + jnp.dot(p_ref[...], wo_ref[AW:2 * AW, :], preferred_element_type=f32)
        x1_ref[...] = x1
        r = lax.rsqrt(jnp.mean(x1 * x1, axis=-1, keepdims=True) + EPS)
        h2_ref[...] = (x1 * r * g_ref[...]).astype(bf16)

    return pl.pallas_call(
        body, name="outproj", grid=(T // tm,),
        in_specs=[pl.BlockSpec((tm, D), lambda i: (i, 0)), pl.BlockSpec((tm, AW), lambda i: (i, 0)),
                  pl.BlockSpec((tm, AW), lambda i: (i, 0)), _full((D, D)), _full((1, D))],
        out_specs=[pl.BlockSpec((tm, D), lambda i: (i, 0)), pl.BlockSpec((tm, D), lambda i: (i, 0))],
        out_shape=[jax.ShapeDtypeStruct((T, D), f32), jax.ShapeDtypeStruct((T, D), bf16)],
        compiler_params=_cp("parallel"),
    )(x, attn, pool, wo, g2)


def _mlp_fwd(h2, x1, wg, wu, wd):
    tm = 512

    def body(h_ref, x1_ref, wg_ref, wu_ref, wd_ref, x2_ref, gate_ref, up_ref):
        s = pl.program_id(1)
        h = h_ref[...]
        gate = jnp.dot(h, wg_ref[...], preferred_element_type=f32)
        up = jnp.dot(h, wu_ref[...], preferred_element_type=f32)
        gate_ref[...] = gate
        up_ref[...] = up
        a = (gate * jax.nn.sigmoid(gate) * up).astype(bf16)
        part = jnp.dot(a, wd_ref[...], preferred_element_type=f32)

        @pl.when(s == 0)
        def _():
            x2_ref[...] = x1_ref[...] + part

        @pl.when(s > 0)
        def _():
            x2_ref[...] += part

    return pl.pallas_call(
        body, name="mlp_fwd", grid=(T // tm, NSH),
        in_specs=[pl.BlockSpec((tm, D), lambda i, s: (i, 0)), pl.BlockSpec((tm, D), lambda i, s: (i, 0)),
                  pl.BlockSpec((None, D, FS), lambda i, s: (s, 0, 0)), pl.BlockSpec((None, D, FS), lambda i, s: (s, 0, 0)),
                  pl.BlockSpec((None, FS, D), lambda i, s: (s, 0, 0))],
        out_specs=[pl.BlockSpec((tm, D), lambda i, s: (i, 0)), pl.BlockSpec((None, tm, FS), lambda i, s: (s, i, 0)),
                   pl.BlockSpec((None, tm, FS), lambda i, s: (s, i, 0))],
        out_shape=[jax.ShapeDtypeStruct((T, D), f32), jax.ShapeDtypeStruct((NSH, T, FS), f32),
                   jax.ShapeDtypeStruct((NSH, T, FS), f32)],
        compiler_params=_cp("parallel", "arbitrary"),
    )(h2, x1, wg, wu, wd)


def _final_loss(x2, tgt, gf):
    tm = 512

    def body(x_ref, t_ref, g_ref, loss_ref, dg_ref, dx_ref, dxb_ref):
        i = pl.program_id(0)

        @pl.when(i == 0)
        def _():
            loss_ref[...] = jnp.zeros_like(loss_ref)
            dg_ref[...] = jnp.zeros_like(dg_ref)

        xv = x_ref[...]
        g = g_ref[...]
        r = lax.rsqrt(jnp.mean(xv * xv, axis=-1, keepdims=True) + EPS)
        xhat = xv * r
        e = xhat * g - t_ref[...]
        loss_ref[...] += 0.5 * jnp.sum(jnp.mean(e * e, axis=-1, keepdims=True))
        dy = e * (1.0 / D)
        dg_ref[...] += jnp.sum(dy * xhat, axis=0, keepdims=True)
        z = dy * g
        dx = r * (z - xhat * jnp.mean(z * xhat, axis=-1, keepdims=True))
        dx_ref[...] = dx
        dxb_ref[...] = dx.astype(bf16)

    return pl.pallas_call(
        body, name="final_loss", grid=(T // tm,),
        in_specs=[pl.BlockSpec((tm, D), lambda i: (i, 0)), pl.BlockSpec((tm, D), lambda i: (i, 0)), _full((1, D))],
        out_specs=[_full((8, 128)), _full((1, D)), pl.BlockSpec((tm, D), lambda i: (i, 0)),
                   pl.BlockSpec((tm, D), lambda i: (i, 0))],
        out_shape=[jax.ShapeDtypeStruct((8, 128), f32), jax.ShapeDtypeStruct((1, D), f32),
                   jax.ShapeDtypeStruct((T, D), f32), jax.ShapeDtypeStruct((T, D), bf16)],
        compiler_params=_cp("arbitrary"),
    )(x2, tgt, gf)


def _mlp_bwd(dx2b, dx2, gate, up, wg, wu, wd, x1, g2):
    tm = 512

    def body(dxb_ref, dx_ref, gate_ref, up_ref, wg_ref, wu_ref, wd_ref, x1_ref, g_ref,
             a_ref, dg_ref, du_ref, dx1_ref, dx1b_ref, dn_ref, acc):
        i = pl.program_id(0)
        s = pl.program_id(1)

        @pl.when(jnp.logical_and(i == 0, s == 0))
        def _():
            dn_ref[...] = jnp.zeros_like(dn_ref)

        da = lax.dot_general(dxb_ref[...], wd_ref[...], NT, preferred_element_type=f32)
        gate = gate_ref[...]
        upv = up_ref[...]
        sg = jax.nn.sigmoid(gate)
        silu = gate * sg
        a_ref[...] = (silu * upv).astype(bf16)
        dgate = (da * upv * (sg * (1.0 + gate * (1.0 - sg)))).astype(bf16)
        dup = (da * silu).astype(bf16)
        dg_ref[...] = dgate
        du_ref[...] = dup
        part = lax.dot_general(dgate, wg_ref[...], NT, preferred_element_type=f32)
        part = part + lax.dot_general(dup, wu_ref[...], NT, preferred_element_type=f32)

        @pl.when(s == 0)
        def _():
            acc[...] = part

        @pl.when(s > 0)
        def _():
            acc[...] += part

        @pl.when(s == NSH - 1)
        def _():
            xv = x1_ref[...]
            r = lax.rsqrt(jnp.mean(xv * xv, axis=-1, keepdims=True) + EPS)
            xhat = xv * r
            dh = acc[...]
            dn_ref[...] += jnp.sum(dh * xhat, axis=0, keepdims=True)
            z = dh * g_ref[...]
            dx1 = dx_ref[...] + r * (z - xhat * jnp.mean(z * xhat, axis=-1, keepdims=True))
            dx1_ref[...] = dx1
            dx1b_ref[...] = dx1.astype(bf16)

    row = lambda i, s: (i, 0)
    sl = lambda i, s: (s, i, 0)
    wsl = lambda i, s: (s, 0, 0)
    return pl.pallas_call(
        body, name="mlp_bwd", grid=(T // tm, NSH),
        in_specs=[pl.BlockSpec((tm, D), row), pl.BlockSpec((tm, D), row),
                  pl.BlockSpec((None, tm, FS), sl), pl.BlockSpec((None, tm, FS), sl),
                  pl.BlockSpec((None, D, FS), wsl), pl.BlockSpec((None, D, FS), wsl), pl.BlockSpec((None, FS, D), wsl),
                  pl.BlockSpec((tm, D), row), pl.BlockSpec((1, D), lambda i, s: (0, 0))],
        out_specs=[pl.BlockSpec((None, tm, FS), sl), pl.BlockSpec((None, tm, FS), sl), pl.BlockSpec((None, tm, FS), sl),
                   pl.BlockSpec((tm, D), row), pl.BlockSpec((tm, D), row), pl.BlockSpec((1, D), lambda i, s: (0, 0))],
        out_shape=[jax.ShapeDtypeStruct((NSH, T, FS), bf16)] * 3
        + [jax.ShapeDtypeStruct((T, D), f32), jax.ShapeDtypeStruct((T, D), bf16), jax.ShapeDtypeStruct((1, D), f32)],
        scratch_shapes=[pltpu.VMEM((tm, D), f32)],
        compiler_params=_cp("arbitrary", "arbitrary"),
    )(dx2b, dx2, gate, up, wg, wu, wd, x1, g2)


def _mm_tn(a, bs, name, a_sharded=False, b_sharded=False, tk=512):
    nb = len(bs)
    sh = NSH if (a_sharded or b_sharded) else 1
    m = a.shape[-1]

    def body(a_ref, *refs):
        kk = pl.program_id(1)
        av = a_ref[...]
        for b_ref, o_ref in zip(refs[:nb], refs[nb:]):
            upd = lax.dot_general(av, b_ref[...], TN, preferred_element_type=f32)

            @pl.when(kk == 0)
            def _():
                o_ref[...] = upd

            @pl.when(kk > 0)
            def _():
                o_ref[...] += upd

    a_spec = (pl.BlockSpec((None, tk, m), lambda s, k: (s, k, 0)) if a_sharded
              else pl.BlockSpec((tk, m), lambda s, k: (k, 0)))
    b_specs, o_specs, o_shapes = [], [], []
    for b in bs:
        n = b.shape[-1]
        b_specs.append(pl.BlockSpec((None, tk, n), lambda s, k: (s, k, 0)) if b_sharded
                       else pl.BlockSpec((tk, n), lambda s, k: (k, 0)))
        if sh > 1:
            o_specs.append(pl.BlockSpec((None, m, n), lambda s, k: (s, 0, 0)))
            o_shapes.append(jax.ShapeDtypeStruct((sh, m, n), f32))
        else:
            o_specs.append(pl.BlockSpec((m, n), lambda s, k: (0, 0)))
            o_shapes.append(jax.ShapeDtypeStruct((m, n), f32))
    return pl.pallas_call(
        body, name=name, grid=(sh, T // tk), in_specs=[a_spec] + b_specs, out_specs=o_specs, out_shape=o_shapes,
        compiler_params=_cp("parallel", "arbitrary"),
    )(a, *bs)


def _outproj_bwd(dx1b, wo):
    tm = 512

    def body(dx_ref, wo_ref, da_ref, dp_ref):
        dx = dx_ref[...]
        da_ref[...] = lax.dot_general(dx, wo_ref[0:AW, :], NT, preferred_element_type=f32).astype(bf16)
        dp_ref[...] = lax.dot_general(dx, wo_ref[AW:2 * AW, :], NT, preferred_element_type=f32)

    return pl.pallas_call(
        body, name="outproj_bwd", grid=(T // tm,),
        in_specs=[pl.BlockSpec((tm, D), lambda i: (i, 0)), _full((D, D))],
        out_specs=[pl.BlockSpec((tm, AW), lambda i: (i, 0)), pl.BlockSpec((tm, AW), lambda i: (i, 0))],
        out_shape=[jax.ShapeDtypeStruct((T, AW), bf16), jax.ShapeDtypeStruct((T, AW), f32)],
        compiler_params=_cp("parallel"),
    )(dx1b, wo)


def _pool_bwd(dpool, pooled, wp, scale):
    tm = 512
    n = T // tm

    def body(dp_ref, pb_ref, wp_ref, sc_ref, du_ref, dsc_ref, dwp_ref, ext):
        i = pl.program_id(0)

        @pl.when(i == 0)
        def _():
            ext[tm:tm + HALO, :] = jnp.zeros((HALO, AW), f32)
            dsc_ref[...] = jnp.zeros_like(dsc_ref)
            dwp_ref[...] = jnp.zeros_like(dwp_ref)

        t_idx = (n - 1 - i) * tm + lax.broadcasted_iota(jnp.int32, (tm, 1), 0)
        for g, w in enumerate(WINDOWS):
            lo, hi = 128 * g, 128 * (g + 1)
            pb = pb_ref[:, lo:hi]
            mixed = jnp.dot(pb, wp_ref[g], preferred_element_type=f32)
            dpo = dp_ref[:, lo:hi]
            dsc_ref[:, lo:hi] += jnp.sum(dpo * mixed, axis=0, keepdims=True)
            dmr = (dpo * sc_ref[:, lo:hi]).astype(bf16)
            dwp_ref[g] += lax.dot_general(pb, dmr, TN, preferred_element_type=f32)
            dpl = lax.dot_general(dmr, wp_ref[g], NT, preferred_element_type=f32)
            cnt = jnp.minimum(t_idx + 1, w).astype(f32)
            dpn = dpl / cnt
            ext[0:tm, lo:hi] = dpn
            acc = dpn
            for d in range(1, w):
                acc = acc + ext[d:d + tm, lo:hi]
            du_ref[:, lo:hi] = (acc - dpl).astype(bf16)
        ext[tm:tm + HALO, :] = ext[0:HALO, :]

    rev = lambda i: (n - 1 - i, 0)
    return pl.pallas_call(
        body, name="pool_bwd", grid=(n,),
        in_specs=[pl.BlockSpec((tm, AW), rev), pl.BlockSpec((tm, AW), rev), _full((4, 128, 128)), _full((1, AW))],
        out_specs=[pl.BlockSpec((tm, AW), rev), _full((1, AW)), _full((4, 128, 128))],
        out_shape=[jax.ShapeDtypeStruct((T, AW), bf16), jax.ShapeDtypeStruct((1, AW), f32),
                   jax.ShapeDtypeStruct((4, 128, 128), f32)],
        scratch_shapes=[pltpu.VMEM((tm + HALO, AW), f32)],
        compiler_params=_cp("arbitrary"),
    )(dpool, pooled, wp, scale)


def _attn_bwd(qkv, attn, dattn, rb, crow2):
    tq = tk = ATT_T
    n = T // tq
    it, jt = _causal_steps(True)
    nsteps = it.shape[0]

    def body(it_ref, jt_ref, q_ref, k_ref, v_ref, o_ref, do_ref, rb_ref, ck_ref,
             dq_ref, dqs_ref, dk_ref, dks_ref, dv_ref, dq_acc, dk_acc, dv_acc):
        t = pl.program_id(1)
        i = it_ref[t]
        j = jt_ref[t]

        @pl.when(t == 0)
        def _():
            dq_acc[...] = jnp.zeros_like(dq_acc)

        @pl.when(i == j)
        def _():
            dk_acc[...] = jnp.zeros_like(dk_acc)
            dv_acc[...] = jnp.zeros_like(dv_acc)

        lane = lax.broadcasted_iota(jnp.int32, (tq, 128), 1)
        row = lax.broadcasted_iota(jnp.int32, (tq, tk), 0)
        col = lax.broadcasted_iota(jnp.int32, (tq, tk), 1)
        causal = jnp.logical_or(col <= row, i > j)
        q = q_ref[...] * 0.125
        k = k_ref[...]
        v = v_ref[...]
        do = do_ref[...]
        dd = do.astype(f32) * o_ref[...].astype(f32)
        r0 = pl.multiple_of(i * tq, tq)
        for e in range(2):
            hm = (lane >= 64) if e else (lane < 64)
            ones_lane = lane == (0 if e else 64)
            qe = jnp.where(hm, q, jnp.zeros_like(q))
            doe = jnp.where(hm, do, jnp.zeros_like(do))
            ke = jnp.where(hm, k, jnp.zeros_like(k))
            delta = jnp.sum(jnp.where(hm, dd, 0.0), axis=1, keepdims=True)
            s = lax.dot_general(qe, k, NT, preferred_element_type=f32)
            s = s + rb_ref[:, 64 * e:64 * e + 1] - ck_ref[e:e + 1, :]
            p = jnp.exp(jnp.where(causal, s, NEG))
            dv_acc[...] += lax.dot_general(p.astype(bf16), doe, TN, preferred_element_type=f32)
            dp = lax.dot_general(doe, v, NT, preferred_element_type=f32)
            dsb = (p * (dp - delta)).astype(bf16)
            q1 = jnp.where(ones_lane, jnp.ones_like(qe), qe)
            k1 = jnp.where(ones_lane, jnp.ones_like(ke), ke)
            dk_acc[e] += lax.dot_general(dsb, q1, TN, preferred_element_type=f32)
            dq_acc[e, pl.ds(r0, tq), :] += jnp.dot(dsb, k1, preferred_element_type=f32)

        @pl.when(i == n - 1)
        def _():
            dk_ref[...] = jnp.where(lane < 64, dk_acc[0], dk_acc[1]).astype(bf16)
            dks_ref[...] = jnp.where(lane < 64, dk_acc[1], dk_acc[0])
            dv_ref[...] = dv_acc[...].astype(bf16)

        @pl.when(t == nsteps - 1)
        def _():
            lane_t = lax.broadcasted_iota(jnp.int32, (T, 128), 1)
            dq_ref[...] = (jnp.where(lane_t < 64, dq_acc[0], dq_acc[1]) * 0.125).astype(bf16)
            dqs_ref[...] = jnp.where(lane_t < 64, dq_acc[1], dq_acc[0])

    qmap = lambda p, t, it, jt: (it[t], p)
    grid_spec = pltpu.PrefetchScalarGridSpec(
        num_scalar_prefetch=2, grid=(PAIRS, nsteps),
        in_specs=[pl.BlockSpec((tq, 128), qmap),
                  pl.BlockSpec((tk, 128), lambda p, t, it, jt: (jt[t], PAIRS + p)),
                  pl.BlockSpec((tk, 128), lambda p, t, it, jt: (jt[t], 2 * PAIRS + p)),
                  pl.BlockSpec((tq, 128), qmap), pl.BlockSpec((tq, 128), qmap),
                  pl.BlockSpec((None, tq, 128), lambda p, t, it, jt: (p, it[t], 0)),
                  pl.BlockSpec((None, 2, tk), lambda p, t, it, jt: (p, 0, jt[t]))],
        out_specs=[pl.BlockSpec((T, 128), lambda p, t, it, jt: (0, p)),
                   pl.BlockSpec((None, T, 128), lambda p, t, it, jt: (p, 0, 0)),
                   pl.BlockSpec((tk, 128), lambda p, t, it, jt: (jt[t], p)),
                   pl.BlockSpec((None, tk, 128), lambda p, t, it, jt: (p, jt[t], 0)),
                   pl.BlockSpec((tk, 128), lambda p, t, it, jt: (jt[t], p))],
        scratch_shapes=[pltpu.VMEM((2, T, 128), f32), pltpu.VMEM((2, tk, 128), f32), pltpu.VMEM((tk, 128), f32)],
    )
    return pl.pallas_call(
        body, name="fox_attn_bwd", grid_spec=grid_spec,
        out_shape=[jax.ShapeDtypeStruct((T, AW), bf16), jax.ShapeDtypeStruct((PAIRS, T, 128), f32),
                   jax.ShapeDtypeStruct((T, AW), bf16), jax.ShapeDtypeStruct((PAIRS, T, 128), f32),
                   jax.ShapeDtypeStruct((T, AW), bf16)],
        compiler_params=_cp("parallel", "arbitrary"),
    )(it, jt, qkv, qkv, qkv, attn, dattn, rb, crow2)


def _fox_cumsum_bwd(dccol, fl, bfp):
    nb = T // 128

    def body(dc_ref, fl_ref, b_ref, df_ref, db_ref, carry):
        i = pl.program_id(0)

        @pl.when(i == 0)
        def _():
            carry[...] = jnp.zeros_like(carry)
            db_ref[...] = jnp.zeros_like(db_ref)

        r = lax.broadcasted_iota(jnp.int32, (128, 128), 0)
        cc = lax.broadcasted_iota(jnp.int32, (128, 128), 1)
        utri = (cc >= r).astype(f32)
        dl = jnp.dot(utri, dc_ref[...], precision=lax.Precision.HIGHEST, preferred_element_type=f32) + carry[0:1, :]
        carry[...] = jnp.broadcast_to(dl[0:1, :], (8, 128))
        z = fl_ref[...] + b_ref[...]
        df = dl * jax.nn.sigmoid(-z)
        df_ref[...] = df.astype(bf16)
        db_ref[...] += jnp.sum(df, axis=0, keepdims=True)

    rev = lambda i: (nb - 1 - i, 0)
    return pl.pallas_call(
        body, name="fox_cumsum_bwd", grid=(nb,),
        in_specs=[pl.BlockSpec((128, 128), rev), pl.BlockSpec((128, 128), rev), _full((1, 128))],
        out_specs=[pl.BlockSpec((128, 128), rev), _full((1, 128))],
        out_shape=[jax.ShapeDtypeStruct((T, 128), bf16), jax.ShapeDtypeStruct((1, 128), f32)],
        scratch_shapes=[pltpu.VMEM((8, 128), f32)],
        compiler_params=_cp("arbitrary"),
    )(dccol, fl, bfp)


def _inproj_bwd(dq, dk, dv, du, df, wm, wf, x, dx1, g1):
    tm = 512

    def body(dq_ref, dk_ref, dv_ref, du_ref, df_ref, wm_ref, wf_ref, x_ref, dx1_ref, g_ref, dx_ref, dn_ref):
        i = pl.program_id(0)

        @pl.when(i == 0)
        def _():
            dn_ref[...] = jnp.zeros_like(dn_ref)

        dh = lax.dot_general(dq_ref[...], wm_ref[:, 0:AW], NT, preferred_element_type=f32)
        dh = dh + lax.dot_general(dk_ref[...], wm_ref[:, AW:2 * AW], NT, preferred_element_type=f32)
        dh = dh + lax.dot_general(dv_ref[...], wm_ref[:, 2 * AW:3 * AW], NT, preferred_element_type=f32)
        dh = dh + lax.dot_general(du_ref[...], wm_ref[:, 3 * AW:4 * AW], NT, preferred_element_type=f32)
        dh = dh + lax.dot_general(df_ref[...], wf_ref[...], NT, preferred_element_type=f32)
        xv = x_ref[...]
        r = lax.rsqrt(jnp.mean(xv * xv, axis=-1, keepdims=True) + EPS)
        xhat = xv * r
        dn_ref[...] += jnp.sum(dh * xhat, axis=0, keepdims=True)
        z = dh * g_ref[...]
        dx_ref[...] = dx1_ref[...] + r * (z - xhat * jnp.mean(z * xhat, axis=-1, keepdims=True))

    row = lambda i: (i, 0)
    return pl.pallas_call(
        body, name="inproj_bwd", grid=(T // tm,),
        in_specs=[pl.BlockSpec((tm, AW), row)] * 4 + [pl.BlockSpec((tm, 128), row), _full((D, 4 * AW)), _full((D, 128)),
                                                       pl.BlockSpec((tm, D), row), pl.BlockSpec((tm, D), row), _full((1, D))],
        out_specs=[pl.BlockSpec((tm, D), row), _full((1, D))],
        out_shape=[jax.ShapeDtypeStruct((T, D), f32), jax.ShapeDtypeStruct((1, D), f32)],
        compiler_params=_cp("arbitrary"),
    )(dq, dk, dv, du, df, wm, wf, x, dx1, g1)


def _adamw_math(w, g, m, v):
    m = B1 * m + (1.0 - B1) * g
    v = B2 * v + (1.0 - B2) * (g * g)
    m_hat = m / (1.0 - B1 ** STEP)
    v_hat = v / (1.0 - B2 ** STEP)
    delta = -LR * (m_hat / (jnp.sqrt(v_hat) + AEPS) + WD * w)
    return delta, m, v


def _adamw_shard(w, m, v, p_mine, p_other, name):
    rows, cols = w.shape
    tr = 256 if rows % 256 == 0 else 176

    def body(w_ref, m_ref, v_ref, a_ref, b_ref, g_ref, d_ref, nm_ref, nv_ref):
        g = a_ref[...] + b_ref[...]
        g_ref[...] = g
        d_ref[...], nm_ref[...], nv_ref[...] = _adamw_math(w_ref[...], g, m_ref[...], v_ref[...])

    spec = pl.BlockSpec((tr, cols), lambda i: (i, 0))
    return pl.pallas_call(
        body, name=name, grid=(rows // tr,), in_specs=[spec] * 5, out_specs=[spec] * 4,
        out_shape=[jax.ShapeDtypeStruct((rows, cols), f32)] * 4, compiler_params=_cp("parallel"),
    )(w, m, v, p_mine, p_other)


def _adamw_small(w, m, v, parts):
    def body(w_ref, m_ref, v_ref, p_ref, g_ref, d_ref, nm_ref, nv_ref):
        g = p_ref[0]
        for k in range(1, 8):
            g = g + p_ref[k]
        g_ref[...] = g
        d_ref[...], nm_ref[...], nv_ref[...] = _adamw_math(w_ref[...], g, m_ref[...], v_ref[...])

    return pl.pallas_call(
        body, name="adamw_small", out_shape=[jax.ShapeDtypeStruct((SMALL_ROWS, 128), f32)] * 4,
    )(w, m, v, parts)


def _sum4(recv, name):
    _, rows, cols = recv.shape
    tr = 256 if rows % 256 == 0 else 176

    def body(r_ref, o_ref):
        o_ref[...] = (r_ref[0] + r_ref[1]) + (r_ref[2] + r_ref[3])

    return pl.pallas_call(
        body, name=name, grid=(rows // tr,),
        in_specs=[pl.BlockSpec((4, tr, cols), lambda i: (0, i, 0))], out_specs=pl.BlockSpec((tr, cols), lambda i: (i, 0)),
        out_shape=jax.ShapeDtypeStruct((rows, cols), f32), compiler_params=_cp("parallel"),
    )(recv)


_HBM = pl.BlockSpec(memory_space=pltpu.HBM)


def _mesh_pos():
    return lax.axis_index("x"), lax.axis_index("y"), lax.axis_index("c")


def _all_gather_shards(parts):
    n = len(parts)

    def body(*refs):
        srcs, dsts = refs[:n], refs[n:2 * n]
        send_sems, recv_sems, loc_sems = refs[2 * n:]
        x, y, c = _mesh_pos()
        mine = 2 * x + y
        chips = [(1 - x, y), (x, 1 - y), (1 - x, 1 - y)]

        def rcopy(a, k, slot):
            cx, cy = chips[k]
            return pltpu.make_async_remote_copy(
                src_ref=srcs[a], dst_ref=dsts[a].at[slot], send_sem=send_sems.at[3 * a + k],
                recv_sem=recv_sems.at[3 * a + k], device_id=(cx, cy, c), device_id_type=MESH)

        local = [pltpu.make_async_copy(srcs[a], dsts[a].at[mine], loc_sems.at[a]) for a in range(n)]
        for lc in local:
            lc.start()
        sends = [rcopy(a, k, mine) for a in range(n) for k in range(3)]
        for cp in sends:
            cp.start()
        for a in range(n):
            for k in range(3):
                rcopy(a, k, 2 * chips[k][0] + chips[k][1]).wait_recv()
        for cp in sends:
            cp.wait_send()
        for lc in local:
            lc.wait()

    return pl.pallas_call(
        body, name="all_gather_weights",
        in_specs=[_HBM] * n, out_specs=[_HBM] * n,
        out_shape=[jax.ShapeDtypeStruct((NSH,) + p.shape, p.dtype) for p in parts],
        scratch_shapes=[pltpu.SemaphoreType.DMA((3 * n,)), pltpu.SemaphoreType.DMA((3 * n,)), pltpu.SemaphoreType.DMA((n,))],
    )(*parts)


def _reduce_scatter_send(grads, small):
    n = len(grads)

    def body(*refs):
        srcs, small_src = refs[:n], refs[n]
        dsts, small_dst = refs[n + 1:2 * n + 1], refs[2 * n + 1]
        send_sems, recv_sems, loc_sems, ssend, srecv = refs[2 * n + 2:]
        x, y, c = _mesh_pos()
        mine = 2 * x + y
        me = 4 * x + 2 * y + c
        chips = [(1 - x, y), (x, 1 - y), (1 - x, 1 - y)]

        def rcopy(a, k):
            cx, cy = chips[k]
            return pltpu.make_async_remote_copy(
                src_ref=srcs[a].at[2 * cx + cy], dst_ref=dsts[a].at[k], send_sem=send_sems.at[3 * a + k],
                recv_sem=recv_sems.at[3 * a + k], device_id=(cx, cy, c), device_id_type=MESH)

        def scopy(f):
            px = (x + (f >> 2)) % 2
            py = (y + ((f >> 1) & 1)) % 2
            pc = (c + (f & 1)) % 2
            return pltpu.make_async_remote_copy(
                src_ref=small_src, dst_ref=small_dst.at[me], send_sem=ssend.at[f - 1], recv_sem=srecv.at[f - 1],
                device_id=(px, py, pc), device_id_type=MESH)

        local = [pltpu.make_async_copy(srcs[a].at[mine], dsts[a].at[3], loc_sems.at[a]) for a in range(n)]
        local.append(pltpu.make_async_copy(small_src, small_dst.at[me], loc_sems.at[n]))
        for lc in local:
            lc.start()
        smalls = [scopy(f) for f in range(1, 8)]
        for cp in smalls:
            cp.start()
        sends = [rcopy(a, k) for a in range(n) for k in range(3)]
        for cp in sends:
            cp.start()
        for cp in smalls:
            cp.wait_recv()
        for cp in sends:
            cp.wait_recv()
        for cp in smalls + sends:
            cp.wait_send()
        for lc in local:
            lc.wait()

    return pl.pallas_call(
        body, name="reduce_scatter_send",
        in_specs=[_HBM] * (n + 1), out_specs=[_HBM] * (n + 1),
        out_shape=[jax.ShapeDtypeStruct(g.shape, g.dtype) for g in grads] + [jax.ShapeDtypeStruct((8,) + small.shape, f32)],
        scratch_shapes=[pltpu.SemaphoreType.DMA((3 * n,)), pltpu.SemaphoreType.DMA((3 * n,)), pltpu.SemaphoreType.DMA((n + 1,)),
                        pltpu.SemaphoreType.DMA((7,)), pltpu.SemaphoreType.DMA((7,))],
    )(*grads, small)


def _swap_with_sibling(parts):
    n = len(parts)

    def body(*refs):
        srcs, dsts = refs[:n], refs[n:2 * n]
        send_sems, recv_sems = refs[2 * n:]
        x, y, c = _mesh_pos()
        cps = [pltpu.make_async_remote_copy(src_ref=srcs[a], dst_ref=dsts[a], send_sem=send_sems.at[a],
                                            recv_sem=recv_sems.at[a], device_id=(x, y, 1 - c), device_id_type=MESH)
               for a in range(n)]
        for cp in cps:
            cp.start()
        for cp in cps:
            cp.wait_recv()
        for cp in cps:
            cp.wait_send()

    return pl.pallas_call(
        body, name="swap_with_sibling", in_specs=[_HBM] * n, out_specs=[_HBM] * n,
        out_shape=[jax.ShapeDtypeStruct(p.shape, p.dtype) for p in parts],
        scratch_shapes=[pltpu.SemaphoreType.DMA((n,)), pltpu.SemaphoreType.DMA((n,))],
    )(*parts)


def _local_step(x, tgt, wm, wf, wo, wg, wu, wd, g1, bfp, wp, scale, g2, gf):
    h, qkv, u, fl = _rms_inproj(x, g1, wm, wf)
    ccol, crow = _fox_cumsum(fl, bfp)
    crow2 = crow.reshape(PAIRS, 2, T)
    attn, rb = _attn_fwd(qkv, ccol, crow2)
    pooled, pool = _pool_fwd(u, wp, scale)
    x1, h2 = _outproj(x, attn, pool, wo, g2)
    x2, gate, up = _mlp_fwd(h2, x1, wg, wu, wd)
    loss, dgf, dx2, dx2b = _final_loss(x2, tgt, gf)

    a_b, dgate, dup, dx1, dx1b, dg2 = _mlp_bwd(dx2b, dx2, gate, up, wg, wu, wd, x1, g2)
    (dwd,) = _mm_tn(a_b, [dx2b], "dw_down", a_sharded=True)
    dwg, dwu = _mm_tn(h2, [dgate, dup], "dw_gate_up", b_sharded=True)
    dattn, dpool = _outproj_bwd(dx1b, wo)
    dwo_a, = _mm_tn(attn, [dx1b], "dw_out_attn")
    dwo_p, = _mm_tn(pool, [dx1b], "dw_out_pool")
    du, dscale, dwp = _pool_bwd(dpool, pooled, wp, scale)
    dq, dqs, dk, dks, dv = _attn_bwd(qkv, attn, dattn, rb, crow2)
    side = dqs - dks
    dc = jnp.stack([side[:, :, 64], side[:, :, 0]], axis=-1)
    dccol = jnp.pad(dc.transpose(1, 0, 2).reshape(T, 8), ((0, 0), (0, 120)))
    df, dbf = _fox_cumsum_bwd(dccol, fl, bfp)
    dx, dg1 = _inproj_bwd(dq, dk, dv, du, df, wm, wf, x, dx1, g1)
    dwq, dwk, dwv, dwu_in, dwf = _mm_tn(h, [dq, dk, dv, du, df], "dw_in")

    dwin = jnp.concatenate([dwq, dwk, dwv, dwf[:, 0:8], dwu_in], axis=1)
    dwin = dwin.reshape(D, NSH, IN_S).transpose(1, 0, 2)
    dwo = jnp.concatenate([dwo_a, dwo_p], axis=0).reshape(NSH, D // NSH, D)
    pad8 = lambda r: jnp.pad(r, ((0, 8 - r.shape[0]), (0, 0)))
    small = jnp.concatenate([dg1.reshape(8, 128), dg2.reshape(8, 128), dgf.reshape(8, 128), pad8(dscale.reshape(4, 128)),
                             pad8(dbf), dwp.reshape(512, 128)], axis=0)
    return loss[0, 0], dx, (dwin, dwo, dwg, dwu, dwd), small


def _pack_small(n1, n2, nf, ps, bfv, wp):
    pad8 = lambda r: jnp.pad(r, ((0, 8 - r.shape[0]), (0, 0)))
    return jnp.concatenate([n1.reshape(8, 128), n2.reshape(8, 128), nf.reshape(8, 128), pad8(ps.reshape(4, 128)),
                            pad8(jnp.pad(bfv.reshape(1, 8), ((0, 0), (0, 120)))), wp.reshape(512, 128)], axis=0)


def _unpack_small(p):
    return dict(norm1_g=p[0:8].reshape(1, D), norm2_g=p[8:16].reshape(1, D), final_g=p[16:24].reshape(D),
                pool_scale=p[24:28].reshape(1, AW), b_forget=p[32:33, 0:8], w_pool=p[40:552].reshape(1, 4, 128, 128))


def kernel(x, norm1_g, w_in, b_forget, w_pool, pool_scale, w_out, norm2_g, w_gate, w_up, w_down, final_g, loss_target, m_norm1_g, m_w_in, m_b_forget, m_w_pool, m_pool_scale, m_w_out, m_norm2_g, m_w_gate, m_w_up, m_w_down, m_final_g, v_norm1_g, v_w_in, v_b_forget, v_w_pool, v_pool_scale, v_w_out, v_norm2_g, v_w_gate, v_w_up, v_w_down, v_final_g):
    shards = [w_in[0].astype(bf16), w_out[0].astype(bf16), w_gate[0].astype(bf16), w_up[0].astype(bf16), w_down[0].astype(bf16)]
    win4, wo4, wg, wu, wd = _all_gather_shards(shards)
    win = win4.transpose(1, 0, 2).reshape(D, IN_W)
    wm = jnp.concatenate([win[:, 0:3 * AW], win[:, 3 * AW + 8:]], axis=1)
    wf = jnp.pad(win[:, 3 * AW:3 * AW + 8], ((0, 0), (0, 120)))
    wo = wo4.reshape(D, D)
    bfp = jnp.pad(b_forget, ((0, 0), (0, 120)))

    loss_local, dx, grads, small = _local_step(
        x[0], loss_target[0], wm, wf, wo, wg, wu, wd, norm1_g, bfp, w_pool[0].astype(bf16), pool_scale, norm2_g,
        final_g.reshape(1, D))

    *recv, small_all = _reduce_scatter_send(list(grads), small)
    partial = [_sum4(r, f"sum4_{i}") for i, r in enumerate(recv)]
    other = _swap_with_sibling(partial)

    ws = [w_in[0], w_out[0], w_gate[0], w_up[0], w_down[0]]
    ms = [m_w_in[0], m_w_out[0], m_w_gate[0], m_w_up[0], m_w_down[0]]
    vs = [v_w_in[0], v_w_out[0], v_w_gate[0], v_w_up[0], v_w_down[0]]
    big = [_adamw_shard(ws[i], ms[i], vs[i], partial[i], other[i], f"adamw_{i}") for i in range(5)]
    sm = _adamw_small(_pack_small(norm1_g, norm2_g, final_g, pool_scale, b_forget, w_pool),
                      _pack_small(m_norm1_g, m_norm2_g, m_final_g, m_pool_scale, m_b_forget, m_w_pool),
                      _pack_small(v_norm1_g, v_norm2_g, v_final_g, v_pool_scale, v_b_forget, v_w_pool), small_all)

    loss = lax.psum(loss_local, ("x", "y", "c"))
    order = ["norm1_g", "w_in", "b_forget", "w_pool", "pool_scale", "w_out", "norm2_g", "w_gate", "w_up", "w_down", "final_g"]
    big_idx = {"w_in": 0, "w_out": 1, "w_gate": 2, "w_up": 3, "w_down": 4}
    outs = [loss, dx[None]]
    for kind in range(4):
        small_k = _unpack_small(sm[kind])
        for name in order:
            outs.append(big[big_idx[name]][kind][None] if name in big_idx else small_k[name])
    return tuple(outs)
```

```python
import functools

import jax
import jax.numpy as jnp
import numpy as np
from jax import lax
from jax.experimental import pallas as pl
from jax.experimental.pallas import tpu as pltpu

f32 = jnp.float32
bf16 = jnp.bfloat16

T = 4096
D = 1024
NSH = 4
IN_W = 2056
IN_S = IN_W // NSH
AW = 512
PAIRS = 4
FF = 2816
FS = FF // NSH
WINDOWS = (2, 4, 8, 16)
HALO = 16
EPS = 1e-6
NEG = -1e30
LR, B1, B2, AEPS, WD, STEP = 0.001, 0.9, 0.999, 1e-08, 0.01, 10
SMALL_ROWS = 552

NT = (((1,), (1,)), ((), ()))
TN = (((0,), (0,)), ((), ()))

MESH = pl.DeviceIdType.MESH


def _cp(*sem):
    return pltpu.CompilerParams(dimension_semantics=sem)


def _full(shape):
    n = len(shape)
    return pl.BlockSpec(shape, lambda *_: (0,) * n)


def _rms_inproj(x, g1, wm, wf, dep):
    tm = 512

    def body(x_ref, g_ref, wm_ref, wf_ref, dep_ref, h_ref, qkv_ref, u_ref, fl_ref):
        xv = x_ref[...]
        r = lax.rsqrt(jnp.mean(xv * xv, axis=-1, keepdims=True) + EPS)
        h = (xv * r * g_ref[...]).astype(bf16)
        h_ref[...] = h
        qkv_ref[...] = jnp.dot(h, wm_ref[:, 0:3 * AW], preferred_element_type=f32).astype(bf16)
        u_ref[...] = jnp.dot(h, wm_ref[:, 3 * AW:4 * AW], preferred_element_type=f32)
        fl_ref[...] = jnp.dot(h, wf_ref[...], preferred_element_type=f32)

    return pl.pallas_call(
        body, name="rms_inproj", grid=(T // tm,),
        in_specs=[pl.BlockSpec((tm, D), lambda i: (i, 0)), _full((1, D)), _full((D, 4 * AW)), _full((D, 128)),
                  _full((8, 128))],
        out_specs=[pl.BlockSpec((tm, D), lambda i: (i, 0)), pl.BlockSpec((tm, 3 * AW), lambda i: (i, 0)),
                   pl.BlockSpec((tm, AW), lambda i: (i, 0)), pl.BlockSpec((tm, 128), lambda i: (i, 0))],
        out_shape=[jax.ShapeDtypeStruct((T, D), bf16), jax.ShapeDtypeStruct((T, 3 * AW), bf16),
                   jax.ShapeDtypeStruct((T, AW), f32), jax.ShapeDtypeStruct((T, 128), f32)],
        compiler_params=_cp("parallel"),
    )(x, g1, wm, wf, dep)


def _log_sigmoid(z):
    return jnp.minimum(z, 0.0) - jnp.log(1.0 + jnp.exp(-jnp.abs(z)))


def _fox_cumsum(fl, bfp):
    nb = T // 128

    def body(fl_ref, b_ref, ccol_ref, crow_ref, carry):
        i = pl.program_id(0)

        @pl.when(i == 0)
        def _():
            carry[...] = jnp.zeros_like(carry)

        lf = _log_sigmoid(fl_ref[...] + b_ref[...])
        r = lax.broadcasted_iota(jnp.int32, (128, 128), 0)
        cc = lax.broadcasted_iota(jnp.int32, (128, 128), 1)
        ltri = (cc <= r).astype(f32)
        cb = jnp.dot(ltri, lf, precision=lax.Precision.HIGHEST, preferred_element_type=f32) + carry[0:1, :]
        carry[...] = jnp.broadcast_to(cb[127:128, :], (8, 128))
        for p in range(PAIRS):
            ce = jnp.broadcast_to(cb[:, 2 * p:2 * p + 1], (128, 128))
            co = jnp.broadcast_to(cb[:, 2 * p + 1:2 * p + 2], (128, 128))
            ccol_ref[p] = jnp.where(cc < 64, ce, co)
        crow_ref[...] = cb.T[0:8, :]

    return pl.pallas_call(
        body, name="fox_cumsum", grid=(nb,),
        in_specs=[pl.BlockSpec((128, 128), lambda i: (i, 0)), _full((1, 128))],
        out_specs=[pl.BlockSpec((PAIRS, 128, 128), lambda i: (0, i, 0)), pl.BlockSpec((8, 128), lambda i: (0, i))],
        out_shape=[jax.ShapeDtypeStruct((PAIRS, T, 128), f32), jax.ShapeDtypeStruct((8, T), f32)],
        scratch_shapes=[pltpu.VMEM((8, 128), f32)],
        compiler_params=_cp("arbitrary"),
    )(fl, bfp)


ATT_T = 512


def _causal_steps(key_major):
    n = T // ATT_T
    if key_major:
        pairs = [(i, j) for j in range(n) for i in range(j, n)]
    else:
        pairs = [(i, j) for i in range(n) for j in range(i + 1)]
    it = np.array([p[0] for p in pairs], np.int32)
    jt = np.array([p[1] for p in pairs], np.int32)
    return jnp.asarray(it), jnp.asarray(jt)


def _attn_fwd(qkv, ccol, crow2):
    tq = tk = ATT_T
    it, jt = _causal_steps(False)
    nsteps = it.shape[0]

    def body(it_ref, jt_ref, q_ref, k_ref, v_ref, cq_ref, ck_ref, o_ref, rb_ref, m_sc, l_sc, acc_sc):
        t = pl.program_id(1)
        i = it_ref[t]
        j = jt_ref[t]

        @pl.when(j == 0)
        def _():
            m_sc[...] = jnp.full_like(m_sc, NEG)
            l_sc[...] = jnp.zeros_like(l_sc)
            acc_sc[...] = jnp.zeros_like(acc_sc)

        lane = lax.broadcasted_iota(jnp.int32, (tq, 128), 1)
        row = lax.broadcasted_iota(jnp.int32, (tq, tk), 0)
        col = lax.broadcasted_iota(jnp.int32, (tq, tk), 1)
        causal = jnp.logical_or(col <= row, j < i)
        q = q_ref[...] * 0.125
        k = k_ref[...]
        v = v_ref[...]
        for e in range(2):
            hm = (lane >= 64) if e else (lane < 64)
            qe = jnp.where(hm, q, jnp.zeros_like(q))
            s = lax.dot_general(qe, k, NT, preferred_element_type=f32)
            s = s + cq_ref[:, 64 * e:64 * e + 1] - ck_ref[e:e + 1, :]
            s = jnp.where(causal, s, NEG)
            m_prev = m_sc[e]
            m_new = jnp.maximum(m_prev, jnp.max(s, axis=1, keepdims=True))
            alpha = jnp.exp(m_prev - m_new)
            p = jnp.exp(s - m_new)
            l_sc[e] = alpha * l_sc[e] + jnp.sum(p, axis=1, keepdims=True)
            acc_sc[e] = alpha * acc_sc[e] + jnp.dot(p.astype(bf16), v, preferred_element_type=f32)
            m_sc[e] = m_new

        @pl.when(j == i)
        def _():
            o0 = acc_sc[0] / l_sc[0]
            o1 = acc_sc[1] / l_sc[1]
            o_ref[...] = jnp.where(lane < 64, o0, o1).astype(bf16)
            rb0 = cq_ref[:, 0:1] - (m_sc[0] + jnp.log(l_sc[0]))
            rb1 = cq_ref[:, 64:65] - (m_sc[1] + jnp.log(l_sc[1]))
            rb_ref[...] = jnp.where(lane < 64, rb0, rb1)

    grid_spec = pltpu.PrefetchScalarGridSpec(
        num_scalar_prefetch=2, grid=(PAIRS, nsteps),
        in_specs=[pl.BlockSpec((tq, 128), lambda p, t, it, jt: (it[t], p)),
                  pl.BlockSpec((tk, 128), lambda p, t, it, jt: (jt[t], PAIRS + p)),
                  pl.BlockSpec((tk, 128), lambda p, t, it, jt: (jt[t], 2 * PAIRS + p)),
                  pl.BlockSpec((None, tq, 128), lambda p, t, it, jt: (p, it[t], 0)),
                  pl.BlockSpec((None, 2, tk), lambda p, t, it, jt: (p, 0, jt[t]))],
        out_specs=[pl.BlockSpec((tq, 128), lambda p, t, it, jt: (it[t], p)),
                   pl.BlockSpec((None, tq, 128), lambda p, t, it, jt: (p, it[t], 0))],
        scratch_shapes=[pltpu.VMEM((2, tq, 1), f32), pltpu.VMEM((2, tq, 1), f32), pltpu.VMEM((2, tq, 128), f32)],
    )
    return pl.pallas_call(
        body, name="fox_attn_fwd", grid_spec=grid_spec,
        out_shape=[jax.ShapeDtypeStruct((T, AW), bf16), jax.ShapeDtypeStruct((PAIRS, T, 128), f32)],
        compiler_params=_cp("parallel", "arbitrary"),
    )(it, jt, qkv, qkv, qkv, ccol, crow2)


def _pool_fwd(u, wp, scale):
    tm = 512

    def body(u_ref, wp_ref, sc_ref, pooled_ref, pool_ref, ext):
        i = pl.program_id(0)

        @pl.when(i == 0)
        def _():
            ext[0:HALO, :] = jnp.zeros((HALO, AW), f32)

        uv = u_ref[...]
        ext[HALO:HALO + tm, :] = uv
        t_idx = i * tm + lax.broadcasted_iota(jnp.int32, (tm, 1), 0)
        for g, w in enumerate(WINDOWS):
            lo, hi = 128 * g, 128 * (g + 1)
            ug = uv[:, lo:hi]
            acc = ug
            for d in range(1, w):
                acc = acc + ext[HALO - d:HALO - d + tm, lo:hi]
            cnt = jnp.minimum(t_idx + 1, w).astype(f32)
            pb = (acc / cnt - ug).astype(bf16)
            pooled_ref[:, lo:hi] = pb
            mixed = jnp.dot(pb, wp_ref[g], preferred_element_type=f32)
            pool_ref[:, lo:hi] = (mixed * sc_ref[:, lo:hi]).astype(bf16)
        ext[0:HALO, :] = uv[tm - HALO:tm, :]

    return pl.pallas_call(
        body, name="pool_fwd", grid=(T // tm,),
        in_specs=[pl.BlockSpec((tm, AW), lambda i: (i, 0)), _full((4, 128, 128)), _full((1, AW))],
        out_specs=[pl.BlockSpec((tm, AW), lambda i: (i, 0)), pl.BlockSpec((tm, AW), lambda i: (i, 0))],
        out_shape=[jax.ShapeDtypeStruct((T, AW), bf16), jax.ShapeDtypeStruct((T, AW), bf16)],
        scratch_shapes=[pltpu.VMEM((tm + HALO, AW), f32)],
        compiler_params=_cp("arbitrary"),
    )(u, wp, scale)


def _outproj(x, attn, pool, wo, g2):
    tm = 512

    def body(x_ref, a_ref, p_ref, wo_ref, g_ref, x1_ref, h2_ref):
        x1 = x_ref[...] + jnp.dot(a_ref[...], wo_ref[0:AW, :], preferred_element_type=f32)
        x1 = x1 + jnp.dot(p_ref[...], wo_ref[AW:2 * AW, :], preferred_element_type=f32)
        x1_ref[...] = x1
        r = lax.rsqrt(jnp.mean(x1 * x1, axis=-1, keepdims=True) + EPS)
        h2_ref[...] = (x1 * r * g_ref[...]).astype(bf16)

    return pl.pallas_call(
        body, name="outproj", grid=(T // tm,),
        in_specs=[pl.BlockSpec((tm, D), lambda i: (i, 0)), pl.BlockSpec((tm, AW), lambda i: (i, 0)),
                  pl.BlockSpec((tm, AW), lambda i: (i, 0)), _full((D, D)), _full((1, D))],
        out_specs=[pl.BlockSpec((tm, D), lambda i: (i, 0)), pl.BlockSpec((tm, D), lambda i: (i, 0))],
        out_shape=[jax.ShapeDtypeStruct((T, D), f32), jax.ShapeDtypeStruct((T, D), bf16)],
        compiler_params=_cp("parallel"),
    )(x, attn, pool, wo, g2)


def _mlp_fwd(h2, x1, wg, wu, wd):
    tm = 512

    def body(h_ref, x1_ref, wg_ref, wu_ref, wd_ref, x2_ref, gate_ref, up_ref):
        s = pl.program_id(1)
        h = h_ref[...]
        gate = lax.dot_general(h, wg_ref[...], NT, preferred_element_type=f32)
        up = lax.dot_general(h, wu_ref[...], NT, preferred_element_type=f32)
        gate_ref[...] = gate
        up_ref[...] = up
        a = (gate * jax.nn.sigmoid(gate) * up).astype(bf16)
        part = jnp.dot(a, wd_ref[...], preferred_element_type=f32)

        @pl.when(s == 0)
        def _():
            x2_ref[...] = x1_ref[...] + part

        @pl.when(s > 0)
        def _():
            x2_ref[...] += part

    return pl.pallas_call(
        body, name="mlp_fwd", grid=(T // tm, NSH),
        in_specs=[pl.BlockSpec((tm, D), lambda i, s: (i, 0)), pl.BlockSpec((tm, D), lambda i, s: (i, 0)),
                  pl.BlockSpec((None, FS, D), lambda i, s: (s, 0, 0)), pl.BlockSpec((None, FS, D), lambda i, s: (s, 0, 0)),
                  pl.BlockSpec((None, FS, D), lambda i, s: (s, 0, 0))],
        out_specs=[pl.BlockSpec((tm, D), lambda i, s: (i, 0)), pl.BlockSpec((None, tm, FS), lambda i, s: (s, i, 0)),
                   pl.BlockSpec((None, tm, FS), lambda i, s: (s, i, 0))],
        out_shape=[jax.ShapeDtypeStruct((T, D), f32), jax.ShapeDtypeStruct((NSH, T, FS), f32),
                   jax.ShapeDtypeStruct((NSH, T, FS), f32)],
        compiler_params=_cp("parallel", "arbitrary"),
    )(h2, x1, wg, wu, wd)


def _final_loss(x2, tgt, gf):
    tm = 512

    def body(x_ref, t_ref, g_ref, loss_ref, dg_ref, dx_ref, dxb_ref):
        i = pl.program_id(0)

        @pl.when(i == 0)
        def _():
            loss_ref[...] = jnp.zeros_like(loss_ref)
            dg_ref[...] = jnp.zeros_like(dg_ref)

        xv = x_ref[...]
        g = g_ref[...]
        r = lax.rsqrt(jnp.mean(xv * xv, axis=-1, keepdims=True) + EPS)
        xhat = xv * r
        e = xhat * g - t_ref[...]
        loss_ref[...] += 0.5 * jnp.sum(jnp.mean(e * e, axis=-1, keepdims=True))
        dy = e * (1.0 / D)
        dg_ref[...] += jnp.sum(dy * xhat, axis=0, keepdims=True)
        z = dy * g
        dx = r * (z - xhat * jnp.mean(z * xhat, axis=-1, keepdims=True))
        dx_ref[...] = dx
        dxb_ref[...] = dx.astype(bf16)

    return pl.pallas_call(
        body, name="final_loss", grid=(T // tm,),
        in_specs=[pl.BlockSpec((tm, D), lambda i: (i, 0)), pl.BlockSpec((tm, D), lambda i: (i, 0)), _full((1, D))],
        out_specs=[_full((8, 128)), _full((1, D)), pl.BlockSpec((tm, D), lambda i: (i, 0)),
                   pl.BlockSpec((tm, D), lambda i: (i, 0))],
        out_shape=[jax.ShapeDtypeStruct((8, 128), f32), jax.ShapeDtypeStruct((1, D), f32),
                   jax.ShapeDtypeStruct((T, D), f32), jax.ShapeDtypeStruct((T, D), bf16)],
        compiler_params=_cp("arbitrary"),
    )(x2, tgt, gf)


def _mlp_bwd(dx2b, dx2, gate, up, wg, wu, wd, x1, g2):
    tm = 512

    def body(dxb_ref, dx_ref, gate_ref, up_ref, wg_ref, wu_ref, wd_ref, x1_ref, g_ref,
             a_ref, dg_ref, du_ref, dx1_ref, dx1b_ref, dn_ref, acc):
        i = pl.program_id(0)
        s = pl.program_id(1)

        @pl.when(jnp.logical_and(i == 0, s == 0))
        def _():
            dn_ref[...] = jnp.zeros_like(dn_ref)

        da = lax.dot_general(dxb_ref[...], wd_ref[...], NT, preferred_element_type=f32)
        gate = gate_ref[...]
        upv = up_ref[...]
        sg = jax.nn.sigmoid(gate)
        silu = gate * sg
        a_ref[...] = (silu * upv).astype(bf16)
        dgate = (da * upv * (sg * (1.0 + gate * (1.0 - sg)))).astype(bf16)
        dup = (da * silu).astype(bf16)
        dg_ref[...] = dgate
        du_ref[...] = dup
        part = jnp.dot(dgate, wg_ref[...], preferred_element_type=f32)
        part = part + jnp.dot(dup, wu_ref[...], preferred_element_type=f32)

        @pl.when(s == 0)
        def _():
            acc[...] = part

        @pl.when(s > 0)
        def _():
            acc[...] += part

        @pl.when(s == NSH - 1)
        def _():
            xv = x1_ref[...]
            r = lax.rsqrt(jnp.mean(xv * xv, axis=-1, keepdims=True) + EPS)
            xhat = xv * r
            dh = acc[...]
            dn_ref[...] += jnp.sum(dh * xhat, axis=0, keepdims=True)
            z = dh * g_ref[...]
            dx1 = dx_ref[...] + r * (z - xhat * jnp.mean(z * xhat, axis=-1, keepdims=True))
            dx1_ref[...] = dx1
            dx1b_ref[...] = dx1.astype(bf16)

    row = lambda i, s: (i, 0)
    sl = lambda i, s: (s, i, 0)
    wsl = lambda i, s: (s, 0, 0)
    return pl.pallas_call(
        body, name="mlp_bwd", grid=(T // tm, NSH),
        in_specs=[pl.BlockSpec((tm, D), row), pl.BlockSpec((tm, D), row),
                  pl.BlockSpec((None, tm, FS), sl), pl.BlockSpec((None, tm, FS), sl),
                  pl.BlockSpec((None, FS, D), wsl), pl.BlockSpec((None, FS, D), wsl), pl.BlockSpec((None, FS, D), wsl),
                  pl.BlockSpec((tm, D), row), pl.BlockSpec((1, D), lambda i, s: (0, 0))],
        out_specs=[pl.BlockSpec((None, tm, FS), sl), pl.BlockSpec((None, tm, FS), sl), pl.BlockSpec((None, tm, FS), sl),
                   pl.BlockSpec((tm, D), row), pl.BlockSpec((tm, D), row), pl.BlockSpec((1, D), lambda i, s: (0, 0))],
        out_shape=[jax.ShapeDtypeStruct((NSH, T, FS), bf16)] * 3
        + [jax.ShapeDtypeStruct((T, D), f32), jax.ShapeDtypeStruct((T, D), bf16), jax.ShapeDtypeStruct((1, D), f32)],
        scratch_shapes=[pltpu.VMEM((tm, D), f32)],
        compiler_params=_cp("arbitrary", "arbitrary"),
    )(dx2b, dx2, gate, up, wg, wu, wd, x1, g2)


def _mm_tn(a, bs, name, a_sharded=False, b_sharded=False, tk=512, out_dtype=bf16):
    nb = len(bs)
    sh = NSH if (a_sharded or b_sharded) else 1
    m = a.shape[-1]
    nk = T // tk

    def body(a_ref, *refs):
        kk = pl.program_id(1)
        av = a_ref[...]
        for b_ref, o_ref, acc in zip(refs[:nb], refs[nb:2 * nb], refs[2 * nb:]):
            upd = lax.dot_general(av, b_ref[...], TN, preferred_element_type=f32)

            @pl.when(kk == 0)
            def _():
                acc[...] = upd

            @pl.when(kk > 0)
            def _():
                acc[...] += upd

            @pl.when(kk == nk - 1)
            def _():
                o_ref[...] = acc[...].astype(out_dtype)

    a_spec = (pl.BlockSpec((None, tk, m), lambda s, k: (s, k, 0)) if a_sharded
              else pl.BlockSpec((tk, m), lambda s, k: (k, 0)))
    b_specs, o_specs, o_shapes, scratch = [], [], [], []
    for b in bs:
        n = b.shape[-1]
        b_specs.append(pl.BlockSpec((None, tk, n), lambda s, k: (s, k, 0)) if b_sharded
                       else pl.BlockSpec((tk, n), lambda s, k: (k, 0)))
        scratch.append(pltpu.VMEM((m, n), f32))
        if sh > 1:
            o_specs.append(pl.BlockSpec((None, m, n), lambda s, k: (s, 0, 0)))
            o_shapes.append(jax.ShapeDtypeStruct((sh, m, n), out_dtype))
        else:
            o_specs.append(pl.BlockSpec((m, n), lambda s, k: (0, 0)))
            o_shapes.append(jax.ShapeDtypeStruct((m, n), out_dtype))
    return pl.pallas_call(
        body, name=name, grid=(sh, nk), in_specs=[a_spec] + b_specs, out_specs=o_specs, out_shape=o_shapes,
        scratch_shapes=scratch, compiler_params=_cp("arbitrary", "arbitrary"),
    )(a, *bs)


def _outproj_bwd(dx1b, wo, dep):
    tm = 512

    def body(dx_ref, wo_ref, dep_ref, da_ref, dp_ref):
        dx = dx_ref[...]
        da_ref[...] = lax.dot_general(dx, wo_ref[0:AW, :], NT, preferred_element_type=f32).astype(bf16)
        dp_ref[...] = lax.dot_general(dx, wo_ref[AW:2 * AW, :], NT, preferred_element_type=f32)

    return pl.pallas_call(
        body, name="outproj_bwd", grid=(T // tm,),
        in_specs=[pl.BlockSpec((tm, D), lambda i: (i, 0)), _full((D, D)), _full((8, 128))],
        out_specs=[pl.BlockSpec((tm, AW), lambda i: (i, 0)), pl.BlockSpec((tm, AW), lambda i: (i, 0))],
        out_shape=[jax.ShapeDtypeStruct((T, AW), bf16), jax.ShapeDtypeStruct((T, AW), f32)],
        compiler_params=_cp("parallel"),
    )(dx1b, wo, dep)


def _pool_bwd(dpool, pooled, wp, scale):
    tm = 512
    n = T // tm

    def body(dp_ref, pb_ref, wp_ref, sc_ref, du_ref, dsc_ref, dwp_ref, ext):
        i = pl.program_id(0)

        @pl.when(i == 0)
        def _():
            ext[tm:tm + HALO, :] = jnp.zeros((HALO, AW), f32)
            dsc_ref[...] = jnp.zeros_like(dsc_ref)
            dwp_ref[...] = jnp.zeros_like(dwp_ref)

        t_idx = (n - 1 - i) * tm + lax.broadcasted_iota(jnp.int32, (tm, 1), 0)
        for g, w in enumerate(WINDOWS):
            lo, hi = 128 * g, 128 * (g + 1)
            pb = pb_ref[:, lo:hi]
            mixed = jnp.dot(pb, wp_ref[g], preferred_element_type=f32)
            dpo = dp_ref[:, lo:hi]
            dsc_ref[:, lo:hi] += jnp.sum(dpo * mixed, axis=0, keepdims=True)
            dmr = (dpo * sc_ref[:, lo:hi]).astype(bf16)
            dwp_ref[g] += lax.dot_general(pb, dmr, TN, preferred_element_type=f32)
            dpl = lax.dot_general(dmr, wp_ref[g], NT, preferred_element_type=f32)
            cnt = jnp.minimum(t_idx + 1, w).astype(f32)
            dpn = dpl / cnt
            ext[0:tm, lo:hi] = dpn
            acc = dpn
            for d in range(1, w):
                acc = acc + ext[d:d + tm, lo:hi]
            du_ref[:, lo:hi] = (acc - dpl).astype(bf16)
        ext[tm:tm + HALO, :] = ext[0:HALO, :]

    rev = lambda i: (n - 1 - i, 0)
    return pl.pallas_call(
        body, name="pool_bwd", grid=(n,),
        in_specs=[pl.BlockSpec((tm, AW), rev), pl.BlockSpec((tm, AW), rev), _full((4, 128, 128)), _full((1, AW))],
        out_specs=[pl.BlockSpec((tm, AW), rev), _full((1, AW)), _full((4, 128, 128))],
        out_shape=[jax.ShapeDtypeStruct((T, AW), bf16), jax.ShapeDtypeStruct((1, AW), f32),
                   jax.ShapeDtypeStruct((4, 128, 128), f32)],
        scratch_shapes=[pltpu.VMEM((tm + HALO, AW), f32)],
        compiler_params=_cp("arbitrary"),
    )(dpool, pooled, wp, scale)


def _attn_bwd(qkv, attn, dattn, rb, crow2):
    tq = tk = ATT_T
    n = T // tq
    it, jt = _causal_steps(True)
    nsteps = it.shape[0]

    def body(it_ref, jt_ref, q_ref, k_ref, v_ref, o_ref, do_ref, rb_ref, ck_ref,
             dq_ref, dqs_ref, dk_ref, dks_ref, dv_ref, dq_acc, dk_acc, dv_acc):
        t = pl.program_id(1)
        i = it_ref[t]
        j = jt_ref[t]

        @pl.when(t == 0)
        def _():
            dq_acc[...] = jnp.zeros_like(dq_acc)

        @pl.when(i == j)
        def _():
            dk_acc[...] = jnp.zeros_like(dk_acc)
            dv_acc[...] = jnp.zeros_like(dv_acc)

        lane = lax.broadcasted_iota(jnp.int32, (tq, 128), 1)
        row = lax.broadcasted_iota(jnp.int32, (tq, tk), 0)
        col = lax.broadcasted_iota(jnp.int32, (tq, tk), 1)
        causal = jnp.logical_or(col <= row, i > j)
        q = q_ref[...] * 0.125
        k = k_ref[...]
        v = v_ref[...]
        do = do_ref[...]
        dd = do.astype(f32) * o_ref[...].astype(f32)
        r0 = pl.multiple_of(i * tq, tq)
        for e in range(2):
            hm = (lane >= 64) if e else (lane < 64)
            ones_lane = lane == (0 if e else 64)
            qe = jnp.where(hm, q, jnp.zeros_like(q))
            doe = jnp.where(hm, do, jnp.zeros_like(do))
            ke = jnp.where(hm, k, jnp.zeros_like(k))
            delta = jnp.sum(jnp.where(hm, dd, 0.0), axis=1, keepdims=True)
            s = lax.dot_general(qe, k, NT, preferred_element_type=f32)
            s = s + rb_ref[:, 64 * e:64 * e + 1] - ck_ref[e:e + 1, :]
            p = jnp.exp(jnp.where(causal, s, NEG))
            dv_acc[...] += lax.dot_general(p.astype(bf16), doe, TN, preferred_element_type=f32)
            dp = lax.dot_general(doe, v, NT, preferred_element_type=f32)
            dsb = (p * (dp - delta)).astype(bf16)
            q1 = jnp.where(ones_lane, jnp.ones_like(qe), qe)
            k1 = jnp.where(ones_lane, jnp.ones_like(ke), ke)
            dk_acc[e] += lax.dot_general(dsb, q1, TN, preferred_element_type=f32)
            dq_acc[e, pl.ds(r0, tq), :] += jnp.dot(dsb, k1, preferred_element_type=f32)

        @pl.when(i == n - 1)
        def _():
            dk_ref[...] = jnp.where(lane < 64, dk_acc[0], dk_acc[1]).astype(bf16)
            dks_ref[...] = jnp.where(lane < 64, dk_acc[1], dk_acc[0])
            dv_ref[...] = dv_acc[...].astype(bf16)

        @pl.when(t == nsteps - 1)
        def _():
            lane_t = lax.broadcasted_iota(jnp.int32, (T, 128), 1)
            dq_ref[...] = (jnp.where(lane_t < 64, dq_acc[0], dq_acc[1]) * 0.125).astype(bf16)
            dqs_ref[...] = jnp.where(lane_t < 64, dq_acc[1], dq_acc[0])

    qmap = lambda p, t, it, jt: (it[t], p)
    grid_spec = pltpu.PrefetchScalarGridSpec(
        num_scalar_prefetch=2, grid=(PAIRS, nsteps),
        in_specs=[pl.BlockSpec((tq, 128), qmap),
                  pl.BlockSpec((tk, 128), lambda p, t, it, jt: (jt[t], PAIRS + p)),
                  pl.BlockSpec((tk, 128), lambda p, t, it, jt: (jt[t], 2 * PAIRS + p)),
                  pl.BlockSpec((tq, 128), qmap), pl.BlockSpec((tq, 128), qmap),
                  pl.BlockSpec((None, tq, 128), lambda p, t, it, jt: (p, it[t], 0)),
                  pl.BlockSpec((None, 2, tk), lambda p, t, it, jt: (p, 0, jt[t]))],
        out_specs=[pl.BlockSpec((T, 128), lambda p, t, it, jt: (0, p)),
                   pl.BlockSpec((None, T, 128), lambda p, t, it, jt: (p, 0, 0)),
                   pl.BlockSpec((tk, 128), lambda p, t, it, jt: (jt[t], p)),
                   pl.BlockSpec((None, tk, 128), lambda p, t, it, jt: (p, jt[t], 0)),
                   pl.BlockSpec((tk, 128), lambda p, t, it, jt: (jt[t], p))],
        scratch_shapes=[pltpu.VMEM((2, T, 128), f32), pltpu.VMEM((2, tk, 128), f32), pltpu.VMEM((tk, 128), f32)],
    )
    return pl.pallas_call(
        body, name="fox_attn_bwd", grid_spec=grid_spec,
        out_shape=[jax.ShapeDtypeStruct((T, AW), bf16), jax.ShapeDtypeStruct((PAIRS, T, 128), f32),
                   jax.ShapeDtypeStruct((T, AW), bf16), jax.ShapeDtypeStruct((PAIRS, T, 128), f32),
                   jax.ShapeDtypeStruct((T, AW), bf16)],
        compiler_params=_cp("parallel", "arbitrary"),
    )(it, jt, qkv, qkv, qkv, attn, dattn, rb, crow2)


def _fox_cumsum_bwd(dccol, fl, bfp):
    nb = T // 128

    def body(dc_ref, fl_ref, b_ref, df_ref, db_ref, carry):
        i = pl.program_id(0)

        @pl.when(i == 0)
        def _():
            carry[...] = jnp.zeros_like(carry)
            db_ref[...] = jnp.zeros_like(db_ref)

        r = lax.broadcasted_iota(jnp.int32, (128, 128), 0)
        cc = lax.broadcasted_iota(jnp.int32, (128, 128), 1)
        utri = (cc >= r).astype(f32)
        dl = jnp.dot(utri, dc_ref[...], precision=lax.Precision.HIGHEST, preferred_element_type=f32) + carry[0:1, :]
        carry[...] = jnp.broadcast_to(dl[0:1, :], (8, 128))
        z = fl_ref[...] + b_ref[...]
        df = dl * jax.nn.sigmoid(-z)
        df_ref[...] = df.astype(bf16)
        db_ref[...] += jnp.sum(df, axis=0, keepdims=True)

    rev = lambda i: (nb - 1 - i, 0)
    return pl.pallas_call(
        body, name="fox_cumsum_bwd", grid=(nb,),
        in_specs=[pl.BlockSpec((128, 128), rev), pl.BlockSpec((128, 128), rev), _full((1, 128))],
        out_specs=[pl.BlockSpec((128, 128), rev), _full((1, 128))],
        out_shape=[jax.ShapeDtypeStruct((T, 128), bf16), jax.ShapeDtypeStruct((1, 128), f32)],
        scratch_shapes=[pltpu.VMEM((8, 128), f32)],
        compiler_params=_cp("arbitrary"),
    )(dccol, fl, bfp)


def _inproj_bwd(dq, dk, dv, du, df, wm, wf, x, dx1, g1):
    tm = 512

    def body(dq_ref, dk_ref, dv_ref, du_ref, df_ref, wm_ref, wf_ref, x_ref, dx1_ref, g_ref, dx_ref, dn_ref):
        i = pl.program_id(0)

        @pl.when(i == 0)
        def _():
            dn_ref[...] = jnp.zeros_like(dn_ref)

        dh = lax.dot_general(dq_ref[...], wm_ref[:, 0:AW], NT, preferred_element_type=f32)
        dh = dh + lax.dot_general(dk_ref[...], wm_ref[:, AW:2 * AW], NT, preferred_element_type=f32)
        dh = dh + lax.dot_general(dv_ref[...], wm_ref[:, 2 * AW:3 * AW], NT, preferred_element_type=f32)
        dh = dh + lax.dot_general(du_ref[...], wm_ref[:, 3 * AW:4 * AW], NT, preferred_element_type=f32)
        dh = dh + lax.dot_general(df_ref[...], wf_ref[...], NT, preferred_element_type=f32)
        xv = x_ref[...]
        r = lax.rsqrt(jnp.mean(xv * xv, axis=-1, keepdims=True) + EPS)
        xhat = xv * r
        dn_ref[...] += jnp.sum(dh * xhat, axis=0, keepdims=True)
        z = dh * g_ref[...]
        dx_ref[...] = dx1_ref[...] + r * (z - xhat * jnp.mean(z * xhat, axis=-1, keepdims=True))

    row = lambda i: (i, 0)
    return pl.pallas_call(
        body, name="inproj_bwd", grid=(T // tm,),
        in_specs=[pl.BlockSpec((tm, AW), row)] * 4 + [pl.BlockSpec((tm, 128), row), _full((D, 4 * AW)), _full((D, 128)),
                                                       pl.BlockSpec((tm, D), row), pl.BlockSpec((tm, D), row), _full((1, D))],
        out_specs=[pl.BlockSpec((tm, D), row), _full((1, D))],
        out_shape=[jax.ShapeDtypeStruct((T, D), f32), jax.ShapeDtypeStruct((1, D), f32)],
        compiler_params=_cp("arbitrary"),
    )(dq, dk, dv, du, df, wm, wf, x, dx1, g1)


def _adamw_math(w, g, m, v):
    m = B1 * m + (1.0 - B1) * g
    v = B2 * v + (1.0 - B2) * (g * g)
    m_hat = m / (1.0 - B1 ** STEP)
    v_hat = v / (1.0 - B2 ** STEP)
    delta = -LR * (m_hat / (jnp.sqrt(v_hat) + AEPS) + WD * w)
    return delta, m, v


def _adamw_shard(w, m, v, p_mine, p_other, name):
    rows, cols = w.shape
    tr = 256 if rows % 256 == 0 else 176

    def body(w_ref, m_ref, v_ref, a_ref, b_ref, g_ref, d_ref, nm_ref, nv_ref):
        g = a_ref[...] + b_ref[...]
        g_ref[...] = g
        d_ref[...], nm_ref[...], nv_ref[...] = _adamw_math(w_ref[...], g, m_ref[...], v_ref[...])

    spec = pl.BlockSpec((tr, cols), lambda i: (i, 0))
    return pl.pallas_call(
        body, name=name, grid=(rows // tr,), in_specs=[spec] * 5, out_specs=[spec] * 4,
        out_shape=[jax.ShapeDtypeStruct((rows, cols), f32)] * 4, compiler_params=_cp("parallel"),
    )(w, m, v, p_mine, p_other)


def _adamw_small(w, m, v, parts):
    def body(w_ref, m_ref, v_ref, p_ref, g_ref, d_ref, nm_ref, nv_ref):
        g = p_ref[0]
        for k in range(1, 8):
            g = g + p_ref[k]
        g_ref[...] = g
        d_ref[...], nm_ref[...], nv_ref[...] = _adamw_math(w_ref[...], g, m_ref[...], v_ref[...])

    return pl.pallas_call(
        body, name="adamw_small", out_shape=[jax.ShapeDtypeStruct((SMALL_ROWS, 128), f32)] * 4,
    )(w, m, v, parts)


def _sum4(recv, g, mine, name):
    _, rows, cols = recv.shape
    tr = 256 if rows % 256 == 0 else 176

    def body(mine_ref, r_ref, g_ref, o_ref):
        o_ref[...] = ((g_ref[...].astype(f32) + r_ref[0].astype(f32))
                      + (r_ref[1].astype(f32) + r_ref[2].astype(f32)))

    grid_spec = pltpu.PrefetchScalarGridSpec(
        num_scalar_prefetch=1, grid=(rows // tr,),
        in_specs=[pl.BlockSpec((3, tr, cols), lambda i, m: (0, i, 0)),
                  pl.BlockSpec((None, tr, cols), lambda i, m: (m[0], i, 0))],
        out_specs=pl.BlockSpec((tr, cols), lambda i, m: (i, 0)))
    return pl.pallas_call(
        body, name=name, grid_spec=grid_spec, out_shape=jax.ShapeDtypeStruct((rows, cols), f32),
        compiler_params=_cp("arbitrary"),
    )(mine, recv, g)


_HBM = pl.BlockSpec(memory_space=pltpu.HBM)
_SEM = pl.BlockSpec(memory_space=pltpu.SEMAPHORE)
_EFFECT = pltpu.SideEffectType.DATAFLOW_SIDE_EFFECTING


def _in_hbm(a):
    return pltpu.with_memory_space_constraint(a, pltpu.HBM)


def _mesh_pos():
    return lax.axis_index("x"), lax.axis_index("y"), lax.axis_index("c")


def _other_chips(x, y):
    return [(1 - x, y), (x, 1 - y), (1 - x, 1 - y)]


def _gather_copy(srcs, lands, send_sems, recv_sems, a, k, slot):
    x, y, c = _mesh_pos()
    cx, cy = _other_chips(x, y)[k]
    return pltpu.make_async_remote_copy(
        src_ref=srcs[a], dst_ref=lands[a].at[slot], send_sem=send_sems.at[3 * a + k], recv_sem=recv_sems.at[3 * a + k],
        device_id=(cx, cy, c), device_id_type=MESH)


def _scatter_copy(srcs, lands, send_sems, recv_sems, a, k):
    x, y, c = _mesh_pos()
    cx, cy = _other_chips(x, y)[k]
    return pltpu.make_async_remote_copy(
        src_ref=srcs[a].at[2 * cx + cy], dst_ref=lands[a].at[k], send_sem=send_sems.at[3 * a + k],
        recv_sem=recv_sems.at[3 * a + k], device_id=(cx, cy, c), device_id_type=MESH)


def _all_gather_shards(parts):
    n = len(parts)

    def body(*refs):
        srcs, dsts = refs[:n], refs[n:2 * n]
        send_sems, recv_sems, loc_sems = refs[2 * n:]
        x, y, _ = _mesh_pos()
        mine = 2 * x + y
        chips = _other_chips(x, y)
        local = [pltpu.make_async_copy(srcs[a], dsts[a].at[mine], loc_sems.at[a]) for a in range(n)]
        for lc in local:
            lc.start()
        sends = [_gather_copy(srcs, dsts, send_sems, recv_sems, a, k, mine) for a in range(n) for k in range(3)]
        for cp in sends:
            cp.start()
        for a in range(n):
            for k in range(3):
                _gather_copy(srcs, dsts, send_sems, recv_sems, a, k, 2 * chips[k][0] + chips[k][1]).wait_recv()
        for cp in sends:
            cp.wait_send()
        for lc in local:
            lc.wait()

    return pl.pallas_call(
        body, name="all_gather_w_in",
        in_specs=[_HBM] * n, out_specs=[_HBM] * n,
        out_shape=[jax.ShapeDtypeStruct((NSH,) + p.shape, p.dtype) for p in parts],
        scratch_shapes=[pltpu.SemaphoreType.DMA((3 * n,)), pltpu.SemaphoreType.DMA((3 * n,)), pltpu.SemaphoreType.DMA((n,))],
    )(*parts)


def _split_start(name, srcs, lands, make_copy):
    n = len(srcs)

    def body(*refs):
        src_refs, land_refs = refs[:n], refs[n:2 * n]
        send_sems, recv_sems = refs[2 * n], refs[2 * n + 1]
        token = refs[-1]
        for a in range(n):
            for k in range(3):
                make_copy(src_refs, land_refs, send_sems, recv_sems, a, k).start()
        token[...] = jnp.zeros_like(token)

    outs = pl.pallas_call(
        body, name=name,
        in_specs=[_HBM] * (2 * n),
        out_specs=[_SEM, _SEM] + [_HBM] * (2 * n) + [pl.BlockSpec(memory_space=pltpu.VMEM)],
        out_shape=[pltpu.SemaphoreType.DMA((3 * n,)), pltpu.SemaphoreType.DMA((3 * n,))]
        + [pltpu.HBM(a.shape, a.dtype) for a in list(srcs) + list(lands)] + [jax.ShapeDtypeStruct((8, 128), f32)],
        input_output_aliases={i: 2 + i for i in range(2 * n)},
        compiler_params=pltpu.CompilerParams(has_side_effects=_EFFECT),
    )(*[_in_hbm(a) for a in list(srcs) + list(lands)])
    return outs[0], outs[1], list(outs[2:2 + n]), list(outs[2 + n:2 + 2 * n]), outs[-1]


def _split_wait(name, send_sems, recv_sems, srcs, lands, after, make_send, make_recv):
    n = len(srcs)

    def body(*refs):
        src_refs, land_refs = refs[:n], refs[n:2 * n]
        s_sems, r_sems = refs[2 * n], refs[2 * n + 1]
        for a in range(n):
            for k in range(3):
                make_recv(src_refs, land_refs, s_sems, r_sems, a, k).wait_recv()
        for a in range(n):
            for k in range(3):
                make_send(src_refs, land_refs, s_sems, r_sems, a, k).wait_send()

    outs = pl.pallas_call(
        body, name=name,
        in_specs=[_HBM] * (2 * n) + [_SEM, _SEM, pl.BlockSpec(memory_space=pl.ANY)],
        out_specs=[_HBM] * (2 * n),
        out_shape=[pltpu.HBM(a.shape, a.dtype) for a in list(srcs) + list(lands)],
        input_output_aliases={i: i for i in range(2 * n)},
        compiler_params=pltpu.CompilerParams(has_side_effects=_EFFECT),
    )(*srcs, *lands, send_sems, recv_sems, after)
    return list(outs[n:])


def _gather_send(srcs, lands, ss, rs, a, k):
    x, y, _ = _mesh_pos()
    return _gather_copy(srcs, lands, ss, rs, a, k, 2 * x + y)


def _gather_recv(srcs, lands, ss, rs, a, k):
    x, y, _ = _mesh_pos()
    cx, cy = _other_chips(x, y)[k]
    return _gather_copy(srcs, lands, ss, rs, a, k, 2 * cx + cy)


def _reduce_scatter_tail(grads, small):
    n = len(grads)

    def body(*refs):
        srcs, small_src = refs[:n], refs[n]
        dsts, small_dst = refs[n + 1:2 * n + 1], refs[2 * n + 1]
        send_sems, recv_sems, loc_sem, ssend, srecv = refs[2 * n + 2:]
        x, y, c = _mesh_pos()
        me = 4 * x + 2 * y + c

        def scopy(f):
            px = (x + (f >> 2)) % 2
            py = (y + ((f >> 1) & 1)) % 2
            pc = (c + (f & 1)) % 2
            return pltpu.make_async_remote_copy(
                src_ref=small_src, dst_ref=small_dst.at[me], send_sem=ssend.at[f - 1], recv_sem=srecv.at[f - 1],
                device_id=(px, py, pc), device_id_type=MESH)

        local = pltpu.make_async_copy(small_src, small_dst.at[me], loc_sem.at[0])
        local.start()
        smalls = [scopy(f) for f in range(1, 8)]
        for cp in smalls:
            cp.start()
        sends = [_scatter_copy(srcs, dsts, send_sems, recv_sems, a, k) for a in range(n) for k in range(3)]
        for cp in sends:
            cp.start()
        for cp in smalls:
            cp.wait_recv()
        for cp in sends:
            cp.wait_recv()
        for cp in smalls + sends:
            cp.wait_send()
        local.wait()

    return pl.pallas_call(
        body, name="reduce_scatter_tail",
        in_specs=[_HBM] * (n + 1), out_specs=[_HBM] * (n + 1),
        out_shape=[jax.ShapeDtypeStruct((3,) + g.shape[1:], g.dtype) for g in grads]
        + [jax.ShapeDtypeStruct((8,) + small.shape, f32)],
        scratch_shapes=[pltpu.SemaphoreType.DMA((3 * n,)), pltpu.SemaphoreType.DMA((3 * n,)), pltpu.SemaphoreType.DMA((1,)),
                        pltpu.SemaphoreType.DMA((7,)), pltpu.SemaphoreType.DMA((7,))],
    )(*grads, small)


def _swap_with_sibling(parts):
    n = len(parts)

    def body(*refs):
        srcs, dsts = refs[:n], refs[n:2 * n]
        send_sems, recv_sems = refs[2 * n:]
        x, y, c = _mesh_pos()
        cps = [pltpu.make_async_remote_copy(src_ref=srcs[a], dst_ref=dsts[a], send_sem=send_sems.at[a],
                                            recv_sem=recv_sems.at[a], device_id=(x, y, 1 - c), device_id_type=MESH)
               for a in range(n)]
        for cp in cps:
            cp.start()
        for cp in cps:
            cp.wait_recv()
        for cp in cps:
            cp.wait_send()

    return pl.pallas_call(
        body, name="swap_with_sibling", in_specs=[_HBM] * n, out_specs=[_HBM] * n,
        out_shape=[jax.ShapeDtypeStruct(p.shape, p.dtype) for p in parts],
        scratch_shapes=[pltpu.SemaphoreType.DMA((n,)), pltpu.SemaphoreType.DMA((n,))],
    )(*parts)


def _pack_small(n1, n2, nf, ps, bfv, wp):
    pad8 = lambda r: jnp.pad(r, ((0, 8 - r.shape[0]), (0, 0)))
    return jnp.concatenate([n1.reshape(8, 128), n2.reshape(8, 128), nf.reshape(8, 128), pad8(ps.reshape(4, 128)),
                            pad8(jnp.pad(bfv.reshape(1, 8), ((0, 0), (0, 120)))), wp.reshape(512, 128)], axis=0)


def _unpack_small(p):
    return dict(norm1_g=p[0:8].reshape(1, D), norm2_g=p[8:16].reshape(1, D), final_g=p[16:24].reshape(D),
                pool_scale=p[24:28].reshape(1, AW), b_forget=p[32:33, 0:8], w_pool=p[40:552].reshape(1, 4, 128, 128))


def _forward(x, tgt, wm, wf, mlp_w_fn, g1, bfp, wp, scale, g2, gf, dep):
    h, qkv, u, fl = _rms_inproj(x, g1, wm, wf, dep)
    ccol, crow = _fox_cumsum(fl, bfp)
    crow2 = crow.reshape(PAIRS, 2, T)
    attn, rb = _attn_fwd(qkv, ccol, crow2)
    pooled, pool = _pool_fwd(u, wp, scale)
    wo, wgt, wut, wd = mlp_w_fn(pool)
    x1, h2 = _outproj(x, attn, pool, wo, g2)
    x2, gate, up = _mlp_fwd(h2, x1, wgt, wut, wd)
    loss, dgf, dx2, dx2b = _final_loss(x2, tgt, gf)
    saved = dict(h=h, qkv=qkv, fl=fl, crow2=crow2, attn=attn, rb=rb, pooled=pooled, pool=pool, x1=x1, h2=h2,
                 gate=gate, up=up, wo=wo, wgt=wgt, wut=wut, wd=wd)
    return loss, dgf, dx2, dx2b, saved


def _backward_mlp(sv, dx2, dx2b, g2):
    a_b, dgate, dup, dx1, dx1b, dg2 = _mlp_bwd(dx2b, dx2, sv["gate"], sv["up"], sv["wgt"], sv["wut"], sv["wd"], sv["x1"], g2)
    (dwd,) = _mm_tn(a_b, [dx2b], "dw_down", a_sharded=True)
    (dwgt,) = _mm_tn(dgate, [sv["h2"]], "dw_gate", a_sharded=True)
    (dwut,) = _mm_tn(dup, [sv["h2"]], "dw_up", a_sharded=True)
    return dx1, dx1b, dg2, (dwgt, dwut, dwd)


def _backward_mixer(sv, x, dx1, dx1b, wm, wf, g1, bfp, wp, scale, dep):
    dattn, dpool = _outproj_bwd(dx1b, sv["wo"], dep)
    dwo_a, = _mm_tn(sv["attn"], [dx1b], "dw_out_attn")
    dwo_p, = _mm_tn(sv["pool"], [dx1b], "dw_out_pool")
    du, dscale, dwp = _pool_bwd(dpool, sv["pooled"], wp, scale)
    dq, dqs, dk, dks, dv = _attn_bwd(sv["qkv"], sv["attn"], dattn, sv["rb"], sv["crow2"])
    dc = jnp.stack([dqs[:, :, 64] - dks[:, :, 64], dqs[:, :, 0] - dks[:, :, 0]], axis=-1)
    dccol = jnp.pad(dc.transpose(1, 0, 2).reshape(T, 8), ((0, 0), (0, 120)))
    df, dbf = _fox_cumsum_bwd(dccol, sv["fl"], bfp)
    dx, dg1 = _inproj_bwd(dq, dk, dv, du, df, wm, wf, x, dx1, g1)
    dwq, dwk, dwv, dwu_in, dwf = _mm_tn(sv["h"], [dq, dk, dv, du, df], "dw_in")
    dwin = jnp.concatenate([dwq, dwk, dwv, dwf[:, 0:8], dwu_in], axis=1)
    dwin = dwin.reshape(D, NSH, IN_S).transpose(1, 0, 2)
    dwo = jnp.concatenate([dwo_a, dwo_p], axis=0).reshape(NSH, D // NSH, D)
    return dx, dg1, dscale, dwp, dbf, dwin, dwo


def kernel(x, norm1_g, w_in, b_forget, w_pool, pool_scale, w_out, norm2_g, w_gate, w_up, w_down, final_g, loss_target, m_norm1_g, m_w_in, m_b_forget, m_w_pool, m_pool_scale, m_w_out, m_norm2_g, m_w_gate, m_w_up, m_w_down, m_final_g, v_norm1_g, v_w_in, v_b_forget, v_w_pool, v_pool_scale, v_w_out, v_norm2_g, v_w_gate, v_w_up, v_w_down, v_final_g):
    mine = (2 * lax.axis_index("x") + lax.axis_index("y")).astype(jnp.int32)
    mine1 = mine.reshape(1)
    tr = lambda a: jnp.transpose(a[0])

    (win4,) = _all_gather_shards([w_in[0].astype(bf16)])
    later = [w_out[0].astype(bf16), tr(w_gate).astype(bf16), tr(w_up).astype(bf16), w_down[0].astype(bf16)]
    lands = [lax.dynamic_update_slice(lax.empty((NSH,) + p.shape, bf16), p[None], (mine, 0, 0)) for p in later]
    ag_send, ag_recv, later_thru, lands_thru, ag_token = _split_start("all_gather_start", later, lands, _gather_send)
    win = win4.transpose(1, 0, 2).reshape(D, IN_W)
    wm = jnp.concatenate([win[:, 0:3 * AW], win[:, 3 * AW + 8:]], axis=1)
    wf = jnp.pad(win[:, 3 * AW:3 * AW + 8], ((0, 0), (0, 120)))
    bfp = jnp.pad(b_forget, ((0, 0), (0, 120)))
    wp = w_pool[0].astype(bf16)
    gf = final_g.reshape(1, D)

    def later_weights(after):
        wo4, wgt, wut, wd = _split_wait("all_gather_wait", ag_send, ag_recv, later_thru, lands_thru, after,
                                        _gather_send, _gather_recv)
        return wo4.reshape(D, D), wgt, wut, wd

    xe, tgt = x[0], loss_target[0]
    loss_v, dgf, dx2, dx2b, sv = _forward(xe, tgt, wm, wf, later_weights, norm1_g, bfp, wp, pool_scale, norm2_g, gf, ag_token)
    dx1, dx1b, dg2, mlp_grads = _backward_mlp(sv, dx2, dx2b, norm2_g)
    mlp_lands = [lax.empty((3,) + g.shape[1:], bf16) for g in mlp_grads]
    rs_send, rs_recv, mlp_thru, mlp_lands_thru, rs_token = _split_start("reduce_scatter_start", mlp_grads, mlp_lands,
                                                                        _scatter_copy)
    dx, dg1, dscale, dwp, dbf, dwin, dwo = _backward_mixer(sv, xe, dx1, dx1b, wm, wf, norm1_g, bfp, wp, pool_scale, rs_token)
    mlp_recv = _split_wait("reduce_scatter_wait", rs_send, rs_recv, mlp_thru, mlp_lands_thru, dwin, _scatter_copy, _scatter_copy)

    pad8 = lambda r: jnp.pad(r, ((0, 8 - r.shape[0]), (0, 0)))
    small = jnp.concatenate([dg1.reshape(8, 128), dg2.reshape(8, 128), dgf.reshape(8, 128), pad8(dscale.reshape(4, 128)),
                             pad8(dbf), dwp.reshape(512, 128)], axis=0)
    win_recv, wo_recv, small_all = _reduce_scatter_tail([dwin, dwo], small)
    grads = [dwin, dwo] + list(mlp_thru)
    recvs = [win_recv, wo_recv] + list(mlp_recv)
    partial = [_sum4(r, g, mine1, f"sum4_{i}") for i, (r, g) in enumerate(zip(recvs, grads))]
    other = _swap_with_sibling(partial)

    ws = [w_in[0], w_out[0], tr(w_gate), tr(w_up), w_down[0]]
    ms = [m_w_in[0], m_w_out[0], tr(m_w_gate), tr(m_w_up), m_w_down[0]]
    vs = [v_w_in[0], v_w_out[0], tr(v_w_gate), tr(v_w_up), v_w_down[0]]
    big = [_adamw_shard(ws[i], ms[i], vs[i], partial[i], other[i], f"adamw_{i}") for i in range(5)]
    sm = _adamw_small(_pack_small(norm1_g, norm2_g, final_g, pool_scale, b_forget, w_pool),
                      _pack_small(m_norm1_g, m_norm2_g, m_final_g, m_pool_scale, m_b_forget, m_w_pool),
                      _pack_small(v_norm1_g, v_norm2_g, v_final_g, v_pool_scale, v_b_forget, v_w_pool), small_all)

    loss = lax.psum(loss_v[0, 0], ("x", "y", "c"))
    order = ["norm1_g", "w_in", "b_forget", "w_pool", "pool_scale", "w_out", "norm2_g", "w_gate", "w_up", "w_down", "final_g"]
    big_idx = {"w_in": 0, "w_out": 1, "w_gate": 2, "w_up": 3, "w_down": 4}
    outs = [loss, dx[None]]
    for kind in range(4):
        small_k = _unpack_small(sm[kind])
        for name in order:
            if name in ("w_gate", "w_up"):
                outs.append(jnp.transpose(big[big_idx[name]][kind])[None])
            elif name in big_idx:
                outs.append(big[big_idx[name]][kind][None])
            else:
                outs.append(small_k[name])
    return tuple(outs)
```

```python
import functools

import jax
import jax.numpy as jnp
import numpy as np
from jax import lax
from jax.experimental import pallas as pl
from jax.experimental.pallas import tpu as pltpu

f32 = jnp.float32
bf16 = jnp.bfloat16

T = 4096
D = 1024
NSH = 4
IN_W = 2056
IN_S = IN_W // NSH
AW = 512
PAIRS = 4
FF = 2816
FS = FF // NSH
WINDOWS = (2, 4, 8, 16)
HALO = 16
EPS = 1e-6
NEG = -1e30
LR, B1, B2, AEPS, WD, STEP = 0.001, 0.9, 0.999, 1e-08, 0.01, 10
SMALL_ROWS = 552

NT = (((1,), (1,)), ((), ()))
TN = (((0,), (0,)), ((), ()))

MESH = pl.DeviceIdType.MESH


def _cp(*sem):
    return pltpu.CompilerParams(dimension_semantics=sem)


def _full(shape):
    n = len(shape)
    return pl.BlockSpec(shape, lambda *_: (0,) * n)


def _rms_inproj(x, g1, wm, wf, dep):
    tm = 512

    def body(x_ref, g_ref, wm_ref, wf_ref, dep_ref, h_ref, qkv_ref, u_ref, fl_ref):
        xv = x_ref[...]
        r = lax.rsqrt(jnp.mean(xv * xv, axis=-1, keepdims=True) + EPS)
        h = (xv * r * g_ref[...]).astype(bf16)
        h_ref[...] = h
        qkv_ref[...] = jnp.dot(h, wm_ref[:, 0:3 * AW], preferred_element_type=f32).astype(bf16)
        u_ref[...] = jnp.dot(h, wm_ref[:, 3 * AW:4 * AW], preferred_element_type=f32)
        fl_ref[...] = jnp.dot(h, wf_ref[...], preferred_element_type=f32)

    return pl.pallas_call(
        body, name="rms_inproj", grid=(T // tm,),
        in_specs=[pl.BlockSpec((tm, D), lambda i: (i, 0)), _full((1, D)), _full((D, 4 * AW)), _full((D, 128)),
                  _full((8, 128))],
        out_specs=[pl.BlockSpec((tm, D), lambda i: (i, 0)), pl.BlockSpec((tm, 3 * AW), lambda i: (i, 0)),
                   pl.BlockSpec((tm, AW), lambda i: (i, 0)), pl.BlockSpec((tm, 128), lambda i: (i, 0))],
        out_shape=[jax.ShapeDtypeStruct((T, D), bf16), jax.ShapeDtypeStruct((T, 3 * AW), bf16),
                   jax.ShapeDtypeStruct((T, AW), f32), jax.ShapeDtypeStruct((T, 128), f32)],
        compiler_params=_cp("parallel"),
    )(x, g1, wm, wf, dep)


def _log_sigmoid(z):
    return jnp.minimum(z, 0.0) - jnp.log(1.0 + jnp.exp(-jnp.abs(z)))


def _fox_cumsum(fl, bfp):
    nb = T // 128

    def body(fl_ref, b_ref, qa_ref, ka_ref, carry):
        i = pl.program_id(0)

        @pl.when(i == 0)
        def _():
            carry[...] = jnp.zeros_like(carry)

        lf = _log_sigmoid(fl_ref[...] + b_ref[...])
        r = lax.broadcasted_iota(jnp.int32, (128, 128), 0)
        cc = lax.broadcasted_iota(jnp.int32, (128, 128), 1)
        ltri = (cc <= r).astype(f32)
        cb = jnp.dot(ltri, lf, precision=lax.Precision.HIGHEST, preferred_element_type=f32) + carry[0:1, :]
        carry[...] = jnp.broadcast_to(cb[127:128, :], (8, 128))
        hi = cb.astype(bf16)
        r1 = cb - hi.astype(f32)
        mid = r1.astype(bf16)
        lo = (r1 - mid.astype(f32)).astype(bf16)
        head = lax.broadcasted_iota(jnp.int32, (128, AW), 0)
        col = lax.broadcasted_iota(jnp.int32, (128, AW), 1)
        base = 128 * (head >> 1) + 64 * (1 - (head & 1))
        place = lambda off: jnp.logical_and(col == base + off, head < 8).astype(bf16)
        mm = lambda a, off: jnp.dot(a, place(off), preferred_element_type=f32)
        cq = mm(hi, 0) + mm(mid, 1) + mm(lo, 2)
        ck = mm(hi, 3) + mm(mid, 4) + mm(lo, 5)
        within = jnp.bitwise_and(lax.broadcasted_iota(jnp.int32, (128, AW), 1), 63)
        qa_ref[...] = jnp.where(jnp.logical_and(within >= 3, within <= 5), 1.0, cq).astype(bf16)
        ka_ref[...] = jnp.where(within <= 2, 1.0, -ck).astype(bf16)

    return pl.pallas_call(
        body, name="fox_cumsum", grid=(nb,),
        in_specs=[pl.BlockSpec((128, 128), lambda i: (i, 0)), _full((1, 128))],
        out_specs=[pl.BlockSpec((128, AW), lambda i: (i, 0)), pl.BlockSpec((128, AW), lambda i: (i, 0))],
        out_shape=[jax.ShapeDtypeStruct((T, AW), bf16), jax.ShapeDtypeStruct((T, AW), bf16)],
        scratch_shapes=[pltpu.VMEM((8, 128), f32)],
        compiler_params=_cp("arbitrary"),
    )(fl, bfp)


ATT_T = 512


def _causal_steps(key_major):
    n = T // ATT_T
    if key_major:
        pairs = [(i, j) for j in range(n) for i in range(j, n)]
    else:
        pairs = [(i, j) for i in range(n) for j in range(i + 1)]
    it = np.array([p[0] for p in pairs], np.int32)
    jt = np.array([p[1] for p in pairs], np.int32)
    return jnp.asarray(it), jnp.asarray(jt)


def _attn_fwd(qkv, qaug, kaug):
    tq = tk = ATT_T
    it, jt = _causal_steps(False)
    nsteps = it.shape[0]

    def body(it_ref, jt_ref, q_ref, k_ref, v_ref, qa_ref, ka_ref, o_ref, lse_ref, m_sc, l_sc, acc_sc):
        t = pl.program_id(1)
        i = it_ref[t]
        j = jt_ref[t]

        @pl.when(j == 0)
        def _():
            m_sc[...] = jnp.full_like(m_sc, NEG)
            l_sc[...] = jnp.zeros_like(l_sc)
            acc_sc[...] = jnp.zeros_like(acc_sc)

        lane = lax.broadcasted_iota(jnp.int32, (tq, 128), 1)

        def step(on_diagonal):
            q = q_ref[...] * 0.125
            k = k_ref[...]
            v = v_ref[...]
            qa = qa_ref[...]
            ka = ka_ref[...]
            for e in range(2):
                hm = (lane >= 64) if e else (lane < 64)
                s = lax.dot_general(jnp.where(hm, q, qa), jnp.where(hm, k, ka), NT, preferred_element_type=f32)
                if on_diagonal:
                    row = lax.broadcasted_iota(jnp.int32, (tq, tk), 0)
                    col = lax.broadcasted_iota(jnp.int32, (tq, tk), 1)
                    s = jnp.where(col <= row, s, NEG)
                m_prev = m_sc[e]
                m_new = jnp.maximum(m_prev, jnp.max(s, axis=1, keepdims=True))
                alpha = jnp.exp(m_prev - m_new)
                p = jnp.exp(s - m_new)
                l_sc[e] = alpha * l_sc[e] + jnp.sum(p, axis=1, keepdims=True)
                acc_sc[e] = alpha * acc_sc[e] + jnp.dot(p.astype(bf16), v, preferred_element_type=f32)
                m_sc[e] = m_new

        @pl.when(j < i)
        def _():
            step(False)

        @pl.when(j == i)
        def _():
            step(True)
            o0 = acc_sc[0] / l_sc[0]
            o1 = acc_sc[1] / l_sc[1]
            o_ref[...] = jnp.where(lane < 64, o0, o1).astype(bf16)
            lse_ref[...] = jnp.where(lane < 64, m_sc[0] + jnp.log(l_sc[0]), m_sc[1] + jnp.log(l_sc[1]))

    qmap = lambda p, t, it, jt: (it[t], p)
    kmap = lambda p, t, it, jt: (jt[t], p)
    grid_spec = pltpu.PrefetchScalarGridSpec(
        num_scalar_prefetch=2, grid=(PAIRS, nsteps),
        in_specs=[pl.BlockSpec((tq, 128), qmap),
                  pl.BlockSpec((tk, 128), lambda p, t, it, jt: (jt[t], PAIRS + p)),
                  pl.BlockSpec((tk, 128), lambda p, t, it, jt: (jt[t], 2 * PAIRS + p)),
                  pl.BlockSpec((tq, 128), qmap), pl.BlockSpec((tk, 128), kmap)],
        out_specs=[pl.BlockSpec((tq, 128), qmap),
                   pl.BlockSpec((None, tq, 128), lambda p, t, it, jt: (p, it[t], 0))],
        scratch_shapes=[pltpu.VMEM((2, tq, 1), f32), pltpu.VMEM((2, tq, 1), f32), pltpu.VMEM((2, tq, 128), f32)],
    )
    return pl.pallas_call(
        body, name="fox_attn_fwd", grid_spec=grid_spec,
        out_shape=[jax.ShapeDtypeStruct((T, AW), bf16), jax.ShapeDtypeStruct((PAIRS, T, 128), f32)],
        compiler_params=_cp("parallel", "arbitrary"),
    )(it, jt, qkv, qkv, qkv, qaug, kaug)


def _pool_fwd(u, wp, scale):
    tm = 512

    def body(u_ref, wp_ref, sc_ref, pooled_ref, pool_ref, ext):
        i = pl.program_id(0)

        @pl.when(i == 0)
        def _():
            ext[0:HALO, :] = jnp.zeros((HALO, AW), f32)

        uv = u_ref[...]
        ext[HALO:HALO + tm, :] = uv
        t_idx = i * tm + lax.broadcasted_iota(jnp.int32, (tm, 1), 0)
        for g, w in enumerate(WINDOWS):
            lo, hi = 128 * g, 128 * (g + 1)
            ug = uv[:, lo:hi]
            acc = ug
            for d in range(1, w):
                acc = acc + ext[HALO - d:HALO - d + tm, lo:hi]
            cnt = jnp.minimum(t_idx + 1, w).astype(f32)
            pb = (acc / cnt - ug).astype(bf16)
            pooled_ref[:, lo:hi] = pb
            mixed = jnp.dot(pb, wp_ref[g], preferred_element_type=f32)
            pool_ref[:, lo:hi] = (mixed * sc_ref[:, lo:hi]).astype(bf16)
        ext[0:HALO, :] = uv[tm - HALO:tm, :]

    return pl.pallas_call(
        body, name="pool_fwd", grid=(T // tm,),
        in_specs=[pl.BlockSpec((tm, AW), lambda i: (i, 0)), _full((4, 128, 128)), _full((1, AW))],
        out_specs=[pl.BlockSpec((tm, AW), lambda i: (i, 0)), pl.BlockSpec((tm, AW), lambda i: (i, 0))],
        out_shape=[jax.ShapeDtypeStruct((T, AW), bf16), jax.ShapeDtypeStruct((T, AW), bf16)],
        scratch_shapes=[pltpu.VMEM((tm + HALO, AW), f32)],
        compiler_params=_cp("arbitrary"),
    )(u, wp, scale)


def _outproj(x, attn, pool, wo, g2):
    tm = 512

    def body(x_ref, a_ref, p_ref, wo_ref, g_ref, x1_ref, h2_ref):
        x1 = x_ref[...] + jnp.dot(a_ref[...], wo_ref[0:AW, :], preferred_element_type=f32)
        x1 = x1 + jnp.dot(p_ref[...], wo_ref[AW:2 * AW, :], preferred_element_type=f32)
        x1_ref[...] = x1
        r = lax.rsqrt(jnp.mean(x1 * x1, axis=-1, keepdims=True) + EPS)
        h2_ref[...] = (x1 * r * g_ref[...]).astype(bf16)

    return pl.pallas_call(
        body, name="outproj", grid=(T // tm,),
        in_specs=[pl.BlockSpec((tm, D), lambda i: (i, 0)), pl.BlockSpec((tm, AW), lambda i: (i, 0)),
                  pl.BlockSpec((tm, AW), lambda i: (i, 0)), _full((D, D)), _full((1, D))],
        out_specs=[pl.BlockSpec((tm, D), lambda i: (i, 0)), pl.BlockSpec((tm, D), lambda i: (i, 0))],
        out_shape=[jax.ShapeDtypeStruct((T, D), f32), jax.ShapeDtypeStruct((T, D), bf16)],
        compiler_params=_cp("parallel"),
    )(x, attn, pool, wo, g2)


def _mlp_fwd(h2, x1, wg, wu, wd):
    tm = 512

    def body(h_ref, x1_ref, wg_ref, wu_ref, wd_ref, x2_ref, gate_ref, up_ref):
        s = pl.program_id(1)
        h = h_ref[...]
        gate = lax.dot_general(h, wg_ref[...], NT, preferred_element_type=f32)
        up = lax.dot_general(h, wu_ref[...], NT, preferred_element_type=f32)
        gate_ref[...] = gate
        up_ref[...] = up
        a = (gate * jax.nn.sigmoid(gate) * up).astype(bf16)
        part = jnp.dot(a, wd_ref[...], preferred_element_type=f32)

        @pl.when(s == 0)
        def _():
            x2_ref[...] = x1_ref[...] + part

        @pl.when(s > 0)
        def _():
            x2_ref[...] += part

    return pl.pallas_call(
        body, name="mlp_fwd", grid=(T // tm, NSH),
        in_specs=[pl.BlockSpec((tm, D), lambda i, s: (i, 0)), pl.BlockSpec((tm, D), lambda i, s: (i, 0)),
                  pl.BlockSpec((None, FS, D), lambda i, s: (s, 0, 0)), pl.BlockSpec((None, FS, D), lambda i, s: (s, 0, 0)),
                  pl.BlockSpec((None, FS, D), lambda i, s: (s, 0, 0))],
        out_specs=[pl.BlockSpec((tm, D), lambda i, s: (i, 0)), pl.BlockSpec((None, tm, FS), lambda i, s: (s, i, 0)),
                   pl.BlockSpec((None, tm, FS), lambda i, s: (s, i, 0))],
        out_shape=[jax.ShapeDtypeStruct((T, D), f32), jax.ShapeDtypeStruct((NSH, T, FS), f32),
                   jax.ShapeDtypeStruct((NSH, T, FS), f32)],
        compiler_params=_cp("parallel", "arbitrary"),
    )(h2, x1, wg, wu, wd)


def _final_loss(x2, tgt, gf):
    tm = 512

    def body(x_ref, t_ref, g_ref, loss_ref, dg_ref, dx_ref, dxb_ref):
        i = pl.program_id(0)

        @pl.when(i == 0)
        def _():
            loss_ref[...] = jnp.zeros_like(loss_ref)
            dg_ref[...] = jnp.zeros_like(dg_ref)

        xv = x_ref[...]
        g = g_ref[...]
        r = lax.rsqrt(jnp.mean(xv * xv, axis=-1, keepdims=True) + EPS)
        xhat = xv * r
        e = xhat * g - t_ref[...]
        loss_ref[...] += 0.5 * jnp.sum(jnp.mean(e * e, axis=-1, keepdims=True))
        dy = e * (1.0 / D)
        dg_ref[...] += jnp.sum(dy * xhat, axis=0, keepdims=True)
        z = dy * g
        dx = r * (z - xhat * jnp.mean(z * xhat, axis=-1, keepdims=True))
        dx_ref[...] = dx
        dxb_ref[...] = dx.astype(bf16)

    return pl.pallas_call(
        body, name="final_loss", grid=(T // tm,),
        in_specs=[pl.BlockSpec((tm, D), lambda i: (i, 0)), pl.BlockSpec((tm, D), lambda i: (i, 0)), _full((1, D))],
        out_specs=[_full((8, 128)), _full((1, D)), pl.BlockSpec((tm, D), lambda i: (i, 0)),
                   pl.BlockSpec((tm, D), lambda i: (i, 0))],
        out_shape=[jax.ShapeDtypeStruct((8, 128), f32), jax.ShapeDtypeStruct((1, D), f32),
                   jax.ShapeDtypeStruct((T, D), f32), jax.ShapeDtypeStruct((T, D), bf16)],
        compiler_params=_cp("arbitrary"),
    )(x2, tgt, gf)


def _mlp_bwd(dx2b, dx2, gate, up, wg, wu, wd, x1, g2):
    tm = 512

    def body(dxb_ref, dx_ref, gate_ref, up_ref, wg_ref, wu_ref, wd_ref, x1_ref, g_ref,
             a_ref, dg_ref, du_ref, dx1_ref, dx1b_ref, dn_ref, acc):
        i = pl.program_id(0)
        s = pl.program_id(1)

        @pl.when(jnp.logical_and(i == 0, s == 0))
        def _():
            dn_ref[...] = jnp.zeros_like(dn_ref)

        da = lax.dot_general(dxb_ref[...], wd_ref[...], NT, preferred_element_type=f32)
        gate = gate_ref[...]
        upv = up_ref[...]
        sg = jax.nn.sigmoid(gate)
        silu = gate * sg
        a_ref[...] = (silu * upv).astype(bf16)
        dgate = (da * upv * (sg * (1.0 + gate * (1.0 - sg)))).astype(bf16)
        dup = (da * silu).astype(bf16)
        dg_ref[...] = dgate
        du_ref[...] = dup
        part = jnp.dot(dgate, wg_ref[...], preferred_element_type=f32)
        part = part + jnp.dot(dup, wu_ref[...], preferred_element_type=f32)

        @pl.when(s == 0)
        def _():
            acc[...] = part

        @pl.when(s > 0)
        def _():
            acc[...] += part

        @pl.when(s == NSH - 1)
        def _():
            xv = x1_ref[...]
            r = lax.rsqrt(jnp.mean(xv * xv, axis=-1, keepdims=True) + EPS)
            xhat = xv * r
            dh = acc[...]
            dn_ref[...] += jnp.sum(dh * xhat, axis=0, keepdims=True)
            z = dh * g_ref[...]
            dx1 = dx_ref[...] + r * (z - xhat * jnp.mean(z * xhat, axis=-1, keepdims=True))
            dx1_ref[...] = dx1
            dx1b_ref[...] = dx1.astype(bf16)

    row = lambda i, s: (i, 0)
    sl = lambda i, s: (s, i, 0)
    wsl = lambda i, s: (s, 0, 0)
    return pl.pallas_call(
        body, name="mlp_bwd", grid=(T // tm, NSH),
        in_specs=[pl.BlockSpec((tm, D), row), pl.BlockSpec((tm, D), row),
                  pl.BlockSpec((None, tm, FS), sl), pl.BlockSpec((None, tm, FS), sl),
                  pl.BlockSpec((None, FS, D), wsl), pl.BlockSpec((None, FS, D), wsl), pl.BlockSpec((None, FS, D), wsl),
                  pl.BlockSpec((tm, D), row), pl.BlockSpec((1, D), lambda i, s: (0, 0))],
        out_specs=[pl.BlockSpec((None, tm, FS), sl), pl.BlockSpec((None, tm, FS), sl), pl.BlockSpec((None, tm, FS), sl),
                   pl.BlockSpec((tm, D), row), pl.BlockSpec((tm, D), row), pl.BlockSpec((1, D), lambda i, s: (0, 0))],
        out_shape=[jax.ShapeDtypeStruct((NSH, T, FS), bf16)] * 3
        + [jax.ShapeDtypeStruct((T, D), f32), jax.ShapeDtypeStruct((T, D), bf16), jax.ShapeDtypeStruct((1, D), f32)],
        scratch_shapes=[pltpu.VMEM((tm, D), f32)],
        compiler_params=_cp("arbitrary", "arbitrary"),
    )(dx2b, dx2, gate, up, wg, wu, wd, x1, g2)


def _mm_tn(a, bs, name, a_sharded=False, b_sharded=False, tk=512, out_dtype=bf16):
    nb = len(bs)
    sh = NSH if (a_sharded or b_sharded) else 1
    m = a.shape[-1]
    nk = T // tk

    def body(a_ref, *refs):
        kk = pl.program_id(1)
        av = a_ref[...]
        for b_ref, o_ref, acc in zip(refs[:nb], refs[nb:2 * nb], refs[2 * nb:]):
            upd = lax.dot_general(av, b_ref[...], TN, preferred_element_type=f32)

            @pl.when(kk == 0)
            def _():
                acc[...] = upd

            @pl.when(kk > 0)
            def _():
                acc[...] += upd

            @pl.when(kk == nk - 1)
            def _():
                o_ref[...] = acc[...].astype(out_dtype)

    a_spec = (pl.BlockSpec((None, tk, m), lambda s, k: (s, k, 0)) if a_sharded
              else pl.BlockSpec((tk, m), lambda s, k: (k, 0)))
    b_specs, o_specs, o_shapes, scratch = [], [], [], []
    for b in bs:
        n = b.shape[-1]
        b_specs.append(pl.BlockSpec((None, tk, n), lambda s, k: (s, k, 0)) if b_sharded
                       else pl.BlockSpec((tk, n), lambda s, k: (k, 0)))
        scratch.append(pltpu.VMEM((m, n), f32))
        if sh > 1:
            o_specs.append(pl.BlockSpec((None, m, n), lambda s, k: (s, 0, 0)))
            o_shapes.append(jax.ShapeDtypeStruct((sh, m, n), out_dtype))
        else:
            o_specs.append(pl.BlockSpec((m, n), lambda s, k: (0, 0)))
            o_shapes.append(jax.ShapeDtypeStruct((m, n), out_dtype))
    return pl.pallas_call(
        body, name=name, grid=(sh, nk), in_specs=[a_spec] + b_specs, out_specs=o_specs, out_shape=o_shapes,
        scratch_shapes=scratch, compiler_params=_cp("arbitrary", "arbitrary"),
    )(a, *bs)


def _outproj_bwd(dx1b, wo):
    tm = 512

    def body(dx_ref, wo_ref, da_ref, dp_ref):
        dx = dx_ref[...]
        da_ref[...] = lax.dot_general(dx, wo_ref[0:AW, :], NT, preferred_element_type=f32).astype(bf16)
        dp_ref[...] = lax.dot_general(dx, wo_ref[AW:2 * AW, :], NT, preferred_element_type=f32)

    return pl.pallas_call(
        body, name="outproj_bwd", grid=(T // tm,),
        in_specs=[pl.BlockSpec((tm, D), lambda i: (i, 0)), _full((D, D))],
        out_specs=[pl.BlockSpec((tm, AW), lambda i: (i, 0)), pl.BlockSpec((tm, AW), lambda i: (i, 0))],
        out_shape=[jax.ShapeDtypeStruct((T, AW), bf16), jax.ShapeDtypeStruct((T, AW), f32)],
        compiler_params=_cp("parallel"),
    )(dx1b, wo)


def _pool_bwd(dpool, pooled, wp, scale, dep):
    tm = 512
    n = T // tm

    def body(dp_ref, pb_ref, wp_ref, sc_ref, dep_ref, du_ref, dsc_ref, dwp_ref, ext):
        i = pl.program_id(0)

        @pl.when(i == 0)
        def _():
            ext[tm:tm + HALO, :] = jnp.zeros((HALO, AW), f32)
            dsc_ref[...] = jnp.zeros_like(dsc_ref)
            dwp_ref[...] = jnp.zeros_like(dwp_ref)

        t_idx = (n - 1 - i) * tm + lax.broadcasted_iota(jnp.int32, (tm, 1), 0)
        for g, w in enumerate(WINDOWS):
            lo, hi = 128 * g, 128 * (g + 1)
            pb = pb_ref[:, lo:hi]
            mixed = jnp.dot(pb, wp_ref[g], preferred_element_type=f32)
            dpo = dp_ref[:, lo:hi]
            dsc_ref[:, lo:hi] += jnp.sum(dpo * mixed, axis=0, keepdims=True)
            dmr = (dpo * sc_ref[:, lo:hi]).astype(bf16)
            dwp_ref[g] += lax.dot_general(pb, dmr, TN, preferred_element_type=f32)
            dpl = lax.dot_general(dmr, wp_ref[g], NT, preferred_element_type=f32)
            cnt = jnp.minimum(t_idx + 1, w).astype(f32)
            dpn = dpl / cnt
            ext[0:tm, lo:hi] = dpn
            acc = dpn
            for d in range(1, w):
                acc = acc + ext[d:d + tm, lo:hi]
            du_ref[:, lo:hi] = (acc - dpl).astype(bf16)
        ext[tm:tm + HALO, :] = ext[0:HALO, :]

    rev = lambda i: (n - 1 - i, 0)
    return pl.pallas_call(
        body, name="pool_bwd", grid=(n,),
        in_specs=[pl.BlockSpec((tm, AW), rev), pl.BlockSpec((tm, AW), rev), _full((4, 128, 128)), _full((1, AW)),
                  _full((8, 128))],
        out_specs=[pl.BlockSpec((tm, AW), rev), _full((1, AW)), _full((4, 128, 128))],
        out_shape=[jax.ShapeDtypeStruct((T, AW), bf16), jax.ShapeDtypeStruct((1, AW), f32),
                   jax.ShapeDtypeStruct((4, 128, 128), f32)],
        scratch_shapes=[pltpu.VMEM((tm + HALO, AW), f32)],
        compiler_params=_cp("arbitrary"),
    )(dpool, pooled, wp, scale, dep)


def _attn_bwd(qkv, qaug, kaug, attn, dattn, lse):
    tq = tk = ATT_T
    n = T // tq
    it, jt = _causal_steps(True)
    nsteps = it.shape[0]

    def body(it_ref, jt_ref, q_ref, k_ref, v_ref, qa_ref, ka_ref, o_ref, do_ref, lse_ref,
             dq_ref, dqs_ref, dk_ref, dks_ref, dv_ref, dq_acc, dk_acc, dv_acc):
        t = pl.program_id(1)
        i = it_ref[t]
        j = jt_ref[t]

        @pl.when(t == 0)
        def _():
            dq_acc[...] = jnp.zeros_like(dq_acc)

        @pl.when(i == j)
        def _():
            dk_acc[...] = jnp.zeros_like(dk_acc)
            dv_acc[...] = jnp.zeros_like(dv_acc)

        lane = lax.broadcasted_iota(jnp.int32, (tq, 128), 1)

        def step(on_diagonal):
            q = q_ref[...] * 0.125
            k = k_ref[...]
            v = v_ref[...]
            qa = qa_ref[...]
            ka = ka_ref[...]
            do = do_ref[...]
            dd = do.astype(f32) * o_ref[...].astype(f32)
            r0 = pl.multiple_of(i * tq, tq)
            for e in range(2):
                hm = (lane >= 64) if e else (lane < 64)
                qe = jnp.where(hm, q, qa)
                ke = jnp.where(hm, k, ka)
                doe = jnp.where(hm, do, jnp.zeros_like(do))
                delta = jnp.sum(jnp.where(hm, dd, 0.0), axis=1, keepdims=True)
                s = lax.dot_general(qe, ke, NT, preferred_element_type=f32) - lse_ref[:, 64 * e:64 * e + 1]
                if on_diagonal:
                    row = lax.broadcasted_iota(jnp.int32, (tq, tk), 0)
                    col = lax.broadcasted_iota(jnp.int32, (tq, tk), 1)
                    s = jnp.where(col <= row, s, NEG)
                p = jnp.exp(s)
                dv_acc[...] += lax.dot_general(p.astype(bf16), doe, TN, preferred_element_type=f32)
                dp = lax.dot_general(doe, v, NT, preferred_element_type=f32)
                dsb = (p * (dp - delta)).astype(bf16)
                dk_acc[e] += lax.dot_general(dsb, qe, TN, preferred_element_type=f32)
                dq_acc[e, pl.ds(r0, tq), :] += jnp.dot(dsb, ke, preferred_element_type=f32)

        @pl.when(i > j)
        def _():
            step(False)

        @pl.when(i == j)
        def _():
            step(True)

        @pl.when(i == n - 1)
        def _():
            dk_ref[...] = jnp.where(lane < 64, dk_acc[0], dk_acc[1]).astype(bf16)
            dks_ref[...] = jnp.where(lane < 64, dk_acc[1], dk_acc[0])
            dv_ref[...] = dv_acc[...].astype(bf16)

        @pl.when(t == nsteps - 1)
        def _():
            lane_t = lax.broadcasted_iota(jnp.int32, (T, 128), 1)
            dq_ref[...] = (jnp.where(lane_t < 64, dq_acc[0], dq_acc[1]) * 0.125).astype(bf16)
            dqs_ref[...] = jnp.where(lane_t < 64, dq_acc[1], dq_acc[0])

    qmap = lambda p, t, it, jt: (it[t], p)
    grid_spec = pltpu.PrefetchScalarGridSpec(
        num_scalar_prefetch=2, grid=(PAIRS, nsteps),
        in_specs=[pl.BlockSpec((tq, 128), qmap),
                  pl.BlockSpec((tk, 128), lambda p, t, it, jt: (jt[t], PAIRS + p)),
                  pl.BlockSpec((tk, 128), lambda p, t, it, jt: (jt[t], 2 * PAIRS + p)),
                  pl.BlockSpec((tq, 128), qmap), pl.BlockSpec((tk, 128), lambda p, t, it, jt: (jt[t], p)),
                  pl.BlockSpec((tq, 128), qmap), pl.BlockSpec((tq, 128), qmap),
                  pl.BlockSpec((None, tq, 128), lambda p, t, it, jt: (p, it[t], 0))],
        out_specs=[pl.BlockSpec((T, 128), lambda p, t, it, jt: (0, p)),
                   pl.BlockSpec((None, T, 128), lambda p, t, it, jt: (p, 0, 0)),
                   pl.BlockSpec((tk, 128), lambda p, t, it, jt: (jt[t], p)),
                   pl.BlockSpec((None, tk, 128), lambda p, t, it, jt: (p, jt[t], 0)),
                   pl.BlockSpec((tk, 128), lambda p, t, it, jt: (jt[t], p))],
        scratch_shapes=[pltpu.VMEM((2, T, 128), f32), pltpu.VMEM((2, tk, 128), f32), pltpu.VMEM((tk, 128), f32)],
    )
    return pl.pallas_call(
        body, name="fox_attn_bwd", grid_spec=grid_spec,
        out_shape=[jax.ShapeDtypeStruct((T, AW), bf16), jax.ShapeDtypeStruct((PAIRS, T, 128), f32),
                   jax.ShapeDtypeStruct((T, AW), bf16), jax.ShapeDtypeStruct((PAIRS, T, 128), f32),
                   jax.ShapeDtypeStruct((T, AW), bf16)],
        compiler_params=_cp("parallel", "arbitrary"),
    )(it, jt, qkv, qkv, qkv, qaug, kaug, attn, dattn, lse)


def _fox_cumsum_bwd(dccol, fl, bfp):
    nb = T // 128

    def body(dc_ref, fl_ref, b_ref, df_ref, db_ref, carry):
        i = pl.program_id(0)

        @pl.when(i == 0)
        def _():
            carry[...] = jnp.zeros_like(carry)
            db_ref[...] = jnp.zeros_like(db_ref)

        r = lax.broadcasted_iota(jnp.int32, (128, 128), 0)
        cc = lax.broadcasted_iota(jnp.int32, (128, 128), 1)
        utri = (cc >= r).astype(f32)
        dl = jnp.dot(utri, dc_ref[...], precision=lax.Precision.HIGHEST, preferred_element_type=f32) + carry[0:1, :]
        carry[...] = jnp.broadcast_to(dl[0:1, :], (8, 128))
        z = fl_ref[...] + b_ref[...]
        df = dl * jax.nn.sigmoid(-z)
        df_ref[...] = df.astype(bf16)
        db_ref[...] += jnp.sum(df, axis=0, keepdims=True)

    rev = lambda i: (nb - 1 - i, 0)
    return pl.pallas_call(
        body, name="fox_cumsum_bwd", grid=(nb,),
        in_specs=[pl.BlockSpec((128, 128), rev), pl.BlockSpec((128, 128), rev), _full((1, 128))],
        out_specs=[pl.BlockSpec((128, 128), rev), _full((1, 128))],
        out_shape=[jax.ShapeDtypeStruct((T, 128), bf16), jax.ShapeDtypeStruct((1, 128), f32)],
        scratch_shapes=[pltpu.VMEM((8, 128), f32)],
        compiler_params=_cp("arbitrary"),
    )(dccol, fl, bfp)


def _inproj_bwd(dq, dk, dv, du, df, wm, wf, x, dx1, g1):
    tm = 512

    def body(dq_ref, dk_ref, dv_ref, du_ref, df_ref, wm_ref, wf_ref, x_ref, dx1_ref, g_ref, dx_ref, dn_ref):
        i = pl.program_id(0)

        @pl.when(i == 0)
        def _():
            dn_ref[...] = jnp.zeros_like(dn_ref)

        dh = lax.dot_general(dq_ref[...], wm_ref[:, 0:AW], NT, preferred_element_type=f32)
        dh = dh + lax.dot_general(dk_ref[...], wm_ref[:, AW:2 * AW], NT, preferred_element_type=f32)
        dh = dh + lax.dot_general(dv_ref[...], wm_ref[:, 2 * AW:3 * AW], NT, preferred_element_type=f32)
        dh = dh + lax.dot_general(du_ref[...], wm_ref[:, 3 * AW:4 * AW], NT, preferred_element_type=f32)
        dh = dh + lax.dot_general(df_ref[...], wf_ref[...], NT, preferred_element_type=f32)
        xv = x_ref[...]
        r = lax.rsqrt(jnp.mean(xv * xv, axis=-1, keepdims=True) + EPS)
        xhat = xv * r
        dn_ref[...] += jnp.sum(dh * xhat, axis=0, keepdims=True)
        z = dh * g_ref[...]
        dx_ref[...] = dx1_ref[...] + r * (z - xhat * jnp.mean(z * xhat, axis=-1, keepdims=True))

    row = lambda i: (i, 0)
    return pl.pallas_call(
        body, name="inproj_bwd", grid=(T // tm,),
        in_specs=[pl.BlockSpec((tm, AW), row)] * 4 + [pl.BlockSpec((tm, 128), row), _full((D, 4 * AW)), _full((D, 128)),
                                                       pl.BlockSpec((tm, D), row), pl.BlockSpec((tm, D), row), _full((1, D))],
        out_specs=[pl.BlockSpec((tm, D), row), _full((1, D))],
        out_shape=[jax.ShapeDtypeStruct((T, D), f32), jax.ShapeDtypeStruct((1, D), f32)],
        compiler_params=_cp("arbitrary"),
    )(dq, dk, dv, du, df, wm, wf, x, dx1, g1)


def _adamw_math(w, g, m, v):
    m = B1 * m + (1.0 - B1) * g
    v = B2 * v + (1.0 - B2) * (g * g)
    m_hat = m / (1.0 - B1 ** STEP)
    v_hat = v / (1.0 - B2 ** STEP)
    delta = -LR * (m_hat / (jnp.sqrt(v_hat) + AEPS) + WD * w)
    return delta, m, v


def _adamw_shard(w, m, v, p_mine, p_other, name):
    rows, cols = w.shape
    tr = 256 if rows % 256 == 0 else 176

    def body(w_ref, m_ref, v_ref, a_ref, b_ref, g_ref, d_ref, nm_ref, nv_ref):
        g = a_ref[...] + b_ref[...]
        g_ref[...] = g
        d_ref[...], nm_ref[...], nv_ref[...] = _adamw_math(w_ref[...], g, m_ref[...], v_ref[...])

    spec = pl.BlockSpec((tr, cols), lambda i: (i, 0))
    return pl.pallas_call(
        body, name=name, grid=(rows // tr,), in_specs=[spec] * 5, out_specs=[spec] * 4,
        out_shape=[jax.ShapeDtypeStruct((rows, cols), f32)] * 4, compiler_params=_cp("parallel"),
    )(w, m, v, p_mine, p_other)


def _adamw_small(w, m, v, parts):
    def body(w_ref, m_ref, v_ref, p_ref, g_ref, d_ref, nm_ref, nv_ref):
        g = p_ref[0]
        for k in range(1, 8):
            g = g + p_ref[k]
        g_ref[...] = g
        d_ref[...], nm_ref[...], nv_ref[...] = _adamw_math(w_ref[...], g, m_ref[...], v_ref[...])

    return pl.pallas_call(
        body, name="adamw_small", out_shape=[jax.ShapeDtypeStruct((SMALL_ROWS, 128), f32)] * 4,
    )(w, m, v, parts)


def _sum4(recv, g, mine, name):
    _, rows, cols = recv.shape
    tr = 256 if rows % 256 == 0 else 176

    def body(mine_ref, r_ref, g_ref, o_ref):
        o_ref[...] = ((g_ref[...].astype(f32) + r_ref[0].astype(f32))
                      + (r_ref[1].astype(f32) + r_ref[2].astype(f32)))

    grid_spec = pltpu.PrefetchScalarGridSpec(
        num_scalar_prefetch=1, grid=(rows // tr,),
        in_specs=[pl.BlockSpec((3, tr, cols), lambda i, m: (0, i, 0)),
                  pl.BlockSpec((None, tr, cols), lambda i, m: (m[0], i, 0))],
        out_specs=pl.BlockSpec((tr, cols), lambda i, m: (i, 0)))
    return pl.pallas_call(
        body, name=name, grid_spec=grid_spec, out_shape=jax.ShapeDtypeStruct((rows, cols), f32),
        compiler_params=_cp("arbitrary"),
    )(mine, recv, g)


_HBM = pl.BlockSpec(memory_space=pltpu.HBM)
_SEM = pl.BlockSpec(memory_space=pltpu.SEMAPHORE)
_EFFECT = pltpu.SideEffectType.DATAFLOW_SIDE_EFFECTING


def _in_hbm(a):
    return pltpu.with_memory_space_constraint(a, pltpu.HBM)


def _mesh_pos():
    return lax.axis_index("x"), lax.axis_index("y"), lax.axis_index("c")


def _other_chips(x, y):
    return [(1 - x, y), (x, 1 - y), (1 - x, 1 - y)]


def _gather_copy(srcs, lands, send_sems, recv_sems, a, k, slot):
    x, y, c = _mesh_pos()
    cx, cy = _other_chips(x, y)[k]
    return pltpu.make_async_remote_copy(
        src_ref=srcs[a], dst_ref=lands[a].at[slot], send_sem=send_sems.at[3 * a + k], recv_sem=recv_sems.at[3 * a + k],
        device_id=(cx, cy, c), device_id_type=MESH)


def _scatter_copy(srcs, lands, send_sems, recv_sems, a, k):
    x, y, c = _mesh_pos()
    cx, cy = _other_chips(x, y)[k]
    return pltpu.make_async_remote_copy(
        src_ref=srcs[a].at[2 * cx + cy], dst_ref=lands[a].at[k], send_sem=send_sems.at[3 * a + k],
        recv_sem=recv_sems.at[3 * a + k], device_id=(cx, cy, c), device_id_type=MESH)


def _all_gather_shards(parts):
    n = len(parts)

    def body(*refs):
        srcs, dsts = refs[:n], refs[n:2 * n]
        send_sems, recv_sems, loc_sems = refs[2 * n:]
        x, y, _ = _mesh_pos()
        mine = 2 * x + y
        chips = _other_chips(x, y)
        local = [pltpu.make_async_copy(srcs[a], dsts[a].at[mine], loc_sems.at[a]) for a in range(n)]
        for lc in local:
            lc.start()
        sends = [_gather_copy(srcs, dsts, send_sems, recv_sems, a, k, mine) for a in range(n) for k in range(3)]
        for cp in sends:
            cp.start()
        for a in range(n):
            for k in range(3):
                _gather_copy(srcs, dsts, send_sems, recv_sems, a, k, 2 * chips[k][0] + chips[k][1]).wait_recv()
        for cp in sends:
            cp.wait_send()
        for lc in local:
            lc.wait()

    return pl.pallas_call(
        body, name="all_gather_w_in",
        in_specs=[_HBM] * n, out_specs=[_HBM] * n,
        out_shape=[jax.ShapeDtypeStruct((NSH,) + p.shape, p.dtype) for p in parts],
        scratch_shapes=[pltpu.SemaphoreType.DMA((3 * n,)), pltpu.SemaphoreType.DMA((3 * n,)), pltpu.SemaphoreType.DMA((n,))],
    )(*parts)


def _split_start(name, srcs, lands, make_copy):
    n = len(srcs)

    def body(*refs):
        src_refs, land_refs = refs[:n], refs[n:2 * n]
        send_sems, recv_sems = refs[2 * n], refs[2 * n + 1]
        token = refs[-1]
        for a in range(n):
            for k in range(3):
                make_copy(src_refs, land_refs, send_sems, recv_sems, a, k).start()
        token[...] = jnp.zeros_like(token)

    outs = pl.pallas_call(
        body, name=name,
        in_specs=[_HBM] * (2 * n),
        out_specs=[_SEM, _SEM] + [_HBM] * (2 * n) + [pl.BlockSpec(memory_space=pltpu.VMEM)],
        out_shape=[pltpu.SemaphoreType.DMA((3 * n,)), pltpu.SemaphoreType.DMA((3 * n,))]
        + [pltpu.HBM(a.shape, a.dtype) for a in list(srcs) + list(lands)] + [jax.ShapeDtypeStruct((8, 128), f32)],
        input_output_aliases={i: 2 + i for i in range(2 * n)},
        compiler_params=pltpu.CompilerParams(has_side_effects=_EFFECT),
    )(*[_in_hbm(a) for a in list(srcs) + list(lands)])
    return outs[0], outs[1], list(outs[2:2 + n]), list(outs[2 + n:2 + 2 * n]), outs[-1]


def _split_wait(name, send_sems, recv_sems, srcs, lands, after, make_send, make_recv):
    n = len(srcs)

    def body(*refs):
        src_refs, land_refs = refs[:n], refs[n:2 * n]
        s_sems, r_sems = refs[2 * n], refs[2 * n + 1]
        for a in range(n):
            for k in range(3):
                make_recv(src_refs, land_refs, s_sems, r_sems, a, k).wait_recv()
        for a in range(n):
            for k in range(3):
                make_send(src_refs, land_refs, s_sems, r_sems, a, k).wait_send()

    outs = pl.pallas_call(
        body, name=name,
        in_specs=[_HBM] * (2 * n) + [_SEM, _SEM, pl.BlockSpec(memory_space=pl.ANY)],
        out_specs=[_HBM] * (2 * n),
        out_shape=[pltpu.HBM(a.shape, a.dtype) for a in list(srcs) + list(lands)],
        input_output_aliases={i: i for i in range(2 * n)},
        compiler_params=pltpu.CompilerParams(has_side_effects=_EFFECT),
    )(*srcs, *lands, send_sems, recv_sems, after)
    return list(outs[:n]), list(outs[n:])


def _gather_send(srcs, lands, ss, rs, a, k):
    x, y, _ = _mesh_pos()
    return _gather_copy(srcs, lands, ss, rs, a, k, 2 * x + y)


def _gather_recv(srcs, lands, ss, rs, a, k):
    x, y, _ = _mesh_pos()
    cx, cy = _other_chips(x, y)[k]
    return _gather_copy(srcs, lands, ss, rs, a, k, 2 * cx + cy)


def _small_all_gather(small, dep):
    def body(small_src, dep_ref, small_dst, loc_sem, ssend, srecv):
        x, y, c = _mesh_pos()
        me = 4 * x + 2 * y + c

        def scopy(f):
            px = (x + (f >> 2)) % 2
            py = (y + ((f >> 1) & 1)) % 2
            pc = (c + (f & 1)) % 2
            return pltpu.make_async_remote_copy(
                src_ref=small_src, dst_ref=small_dst.at[me], send_sem=ssend.at[f - 1], recv_sem=srecv.at[f - 1],
                device_id=(px, py, pc), device_id_type=MESH)

        local = pltpu.make_async_copy(small_src, small_dst.at[me], loc_sem.at[0])
        local.start()
        smalls = [scopy(f) for f in range(1, 8)]
        for cp in smalls:
            cp.start()
        for cp in smalls:
            cp.wait_recv()
        for cp in smalls:
            cp.wait_send()
        local.wait()

    return pl.pallas_call(
        body, name="small_all_gather",
        in_specs=[_HBM, _HBM], out_specs=_HBM,
        out_shape=jax.ShapeDtypeStruct((8,) + small.shape, f32),
        scratch_shapes=[pltpu.SemaphoreType.DMA((1,)), pltpu.SemaphoreType.DMA((7,)), pltpu.SemaphoreType.DMA((7,))],
    )(small, dep)


def _swap_with_sibling(parts, name):
    n = len(parts)

    def body(*refs):
        srcs, dsts = refs[:n], refs[n:2 * n]
        send_sems, recv_sems = refs[2 * n:]
        x, y, c = _mesh_pos()
        cps = [pltpu.make_async_remote_copy(src_ref=srcs[a], dst_ref=dsts[a], send_sem=send_sems.at[a],
                                            recv_sem=recv_sems.at[a], device_id=(x, y, 1 - c), device_id_type=MESH)
               for a in range(n)]
        for cp in cps:
            cp.start()
        for cp in cps:
            cp.wait_recv()
        for cp in cps:
            cp.wait_send()

    return pl.pallas_call(
        body, name=name, in_specs=[_HBM] * n, out_specs=[_HBM] * n,
        out_shape=[jax.ShapeDtypeStruct(p.shape, p.dtype) for p in parts],
        scratch_shapes=[pltpu.SemaphoreType.DMA((n,)), pltpu.SemaphoreType.DMA((n,))],
    )(*parts)


def _pack_small(n1, n2, nf, ps, bfv, wp):
    pad8 = lambda r: jnp.pad(r, ((0, 8 - r.shape[0]), (0, 0)))
    return jnp.concatenate([n1.reshape(8, 128), n2.reshape(8, 128), nf.reshape(8, 128), pad8(ps.reshape(4, 128)),
                            pad8(jnp.pad(bfv.reshape(1, 8), ((0, 0), (0, 120)))), wp.reshape(512, 128)], axis=0)


def _unpack_small(p):
    return dict(norm1_g=p[0:8].reshape(1, D), norm2_g=p[8:16].reshape(1, D), final_g=p[16:24].reshape(D),
                pool_scale=p[24:28].reshape(1, AW), b_forget=p[32:33, 0:8], w_pool=p[40:552].reshape(1, 4, 128, 128))


def _forward(x, tgt, wm, wf, mlp_w_fn, g1, bfp, wp, scale, g2, gf, dep):
    h, qkv, u, fl = _rms_inproj(x, g1, wm, wf, dep)
    qaug, kaug = _fox_cumsum(fl, bfp)
    attn, lse = _attn_fwd(qkv, qaug, kaug)
    pooled, pool = _pool_fwd(u, wp, scale)
    wo, wgt, wut, wd = mlp_w_fn(pool)
    x1, h2 = _outproj(x, attn, pool, wo, g2)
    x2, gate, up = _mlp_fwd(h2, x1, wgt, wut, wd)
    loss, dgf, dx2, dx2b = _final_loss(x2, tgt, gf)
    saved = dict(h=h, qkv=qkv, fl=fl, qaug=qaug, kaug=kaug, attn=attn, lse=lse, pooled=pooled, pool=pool, x1=x1, h2=h2,
                 gate=gate, up=up, wo=wo, wgt=wgt, wut=wut, wd=wd)
    return loss, dgf, dx2, dx2b, saved


def _backward_mlp(sv, dx2, dx2b, g2):
    a_b, dgate, dup, dx1, dx1b, dg2 = _mlp_bwd(dx2b, dx2, sv["gate"], sv["up"], sv["wgt"], sv["wut"], sv["wd"], sv["x1"], g2)
    (dwd,) = _mm_tn(a_b, [dx2b], "dw_down", a_sharded=True)
    (dwgt,) = _mm_tn(dgate, [sv["h2"]], "dw_gate", a_sharded=True)
    (dwut,) = _mm_tn(dup, [sv["h2"]], "dw_up", a_sharded=True)
    return dx1, dx1b, dg2, (dwgt, dwut, dwd)


def _backward_outproj(sv, dx1b):
    dattn, dpool = _outproj_bwd(dx1b, sv["wo"])
    dwo_a, = _mm_tn(sv["attn"], [dx1b], "dw_out_attn")
    dwo_p, = _mm_tn(sv["pool"], [dx1b], "dw_out_pool")
    dwo = jnp.concatenate([dwo_a, dwo_p], axis=0).reshape(NSH, D // NSH, D)
    return dattn, dpool, dwo


def _backward_mixer(sv, x, dx1, dattn, dpool, wm, wf, g1, bfp, wp, scale, dep):
    du, dscale, dwp = _pool_bwd(dpool, sv["pooled"], wp, scale, dep)
    dq, dqs, dk, dks, dv = _attn_bwd(sv["qkv"], sv["qaug"], sv["kaug"], sv["attn"], dattn, sv["lse"])
    dc = jnp.stack([dqs[:, :, 64] - dks[:, :, 67], dqs[:, :, 0] - dks[:, :, 3]], axis=-1)
    dccol = jnp.pad(dc.transpose(1, 0, 2).reshape(T, 8), ((0, 0), (0, 120)))
    df, dbf = _fox_cumsum_bwd(dccol, sv["fl"], bfp)
    dx, dg1 = _inproj_bwd(dq, dk, dv, du, df, wm, wf, x, dx1, g1)
    dwq, dwk, dwv, dwu_in, dwf = _mm_tn(sv["h"], [dq, dk, dv, du, df], "dw_in")
    dwin = jnp.concatenate([dwq, dwk, dwv, dwf[:, 0:8], dwu_in], axis=1)
    dwin = dwin.reshape(D, NSH, IN_S).transpose(1, 0, 2)
    return dx, dg1, dscale, dwp, dbf, dwin


def kernel(x, norm1_g, w_in, b_forget, w_pool, pool_scale, w_out, norm2_g, w_gate, w_up, w_down, final_g, loss_target, m_norm1_g, m_w_in, m_b_forget, m_w_pool, m_pool_scale, m_w_out, m_norm2_g, m_w_gate, m_w_up, m_w_down, m_final_g, v_norm1_g, v_w_in, v_b_forget, v_w_pool, v_pool_scale, v_w_out, v_norm2_g, v_w_gate, v_w_up, v_w_down, v_final_g):
    mine = (2 * lax.axis_index("x") + lax.axis_index("y")).astype(jnp.int32)
    mine1 = mine.reshape(1)
    tr = lambda a: jnp.transpose(a[0])

    (win4,) = _all_gather_shards([w_in[0].astype(bf16)])
    later = [w_out[0].astype(bf16), tr(w_gate).astype(bf16), tr(w_up).astype(bf16), w_down[0].astype(bf16)]
    lands = [lax.dynamic_update_slice(lax.empty((NSH,) + p.shape, bf16), p[None], (mine, 0, 0)) for p in later]
    ag_send, ag_recv, later_thru, lands_thru, ag_token = _split_start("all_gather_start", later, lands, _gather_send)
    win = win4.transpose(1, 0, 2).reshape(D, IN_W)
    wm = jnp.concatenate([win[:, 0:3 * AW], win[:, 3 * AW + 8:]], axis=1)
    wf = jnp.pad(win[:, 3 * AW:3 * AW + 8], ((0, 0), (0, 120)))
    bfp = jnp.pad(b_forget, ((0, 0), (0, 120)))
    wp = w_pool[0].astype(bf16)
    gf = final_g.reshape(1, D)

    def later_weights(after):
        _, (wo4, wgt, wut, wd) = _split_wait("all_gather_wait", ag_send, ag_recv, later_thru, lands_thru, after,
                                             _gather_send, _gather_recv)
        return wo4.reshape(D, D), wgt, wut, wd

    xe, tgt = x[0], loss_target[0]
    loss_v, dgf, dx2, dx2b, sv = _forward(xe, tgt, wm, wf, later_weights, norm1_g, bfp, wp, pool_scale, norm2_g, gf, ag_token)
    dx1, dx1b, dg2, mlp_grads = _backward_mlp(sv, dx2, dx2b, norm2_g)
    dattn, dpool, dwo = _backward_outproj(sv, dx1b)
    first = [dwo] + list(mlp_grads)
    first_lands = [lax.empty((3,) + g.shape[1:], bf16) for g in first]
    rs_send, rs_recv, first_thru, first_lands_thru, rs_token = _split_start("reduce_scatter_start", first, first_lands,
                                                                            _scatter_copy)
    dx, dg1, dscale, dwp, dbf, dwin = _backward_mixer(sv, xe, dx1, dattn, dpool, wm, wf, norm1_g, bfp, wp, pool_scale, rs_token)

    in_send, in_recv, in_thru, in_lands_thru, in_token = _split_start(
        "reduce_scatter_in_start", [dwin], [lax.empty((3,) + dwin.shape[1:], bf16)], _scatter_copy)
    pad8 = lambda r: jnp.pad(r, ((0, 8 - r.shape[0]), (0, 0)))
    small = jnp.concatenate([dg1.reshape(8, 128), dg2.reshape(8, 128), dgf.reshape(8, 128), pad8(dscale.reshape(4, 128)),
                             pad8(dbf), dwp.reshape(512, 128)], axis=0)
    small_all = _small_all_gather(small, in_token)
    first_thru, first_recv = _split_wait("reduce_scatter_wait", rs_send, rs_recv, first_thru, first_lands_thru, small_all,
                                         _scatter_copy, _scatter_copy)
    ws = [w_in[0], w_out[0], tr(w_gate), tr(w_up), w_down[0]]
    ms = [m_w_in[0], m_w_out[0], tr(m_w_gate), tr(m_w_up), m_w_down[0]]
    vs = [v_w_in[0], v_w_out[0], tr(v_w_gate), tr(v_w_up), v_w_down[0]]
    partial = [_sum4(r, g, mine1, f"sum4_{i + 1}") for i, (r, g) in enumerate(zip(first_recv, first_thru))]
    other = _swap_with_sibling(partial, "swap_first")
    big = [_adamw_shard(ws[i + 1], ms[i + 1], vs[i + 1], partial[i], other[i], f"adamw_{i + 1}") for i in range(4)]
    (dwin_thru,), (in_recv_land,) = _split_wait("reduce_scatter_in_wait", in_send, in_recv, in_thru, in_lands_thru, big[3][0],
                                                _scatter_copy, _scatter_copy)
    partial_in = _sum4(in_recv_land, dwin_thru, mine1, "sum4_0")
    (other_in,) = _swap_with_sibling([partial_in], "swap_in")
    big = [_adamw_shard(ws[0], ms[0], vs[0], partial_in, other_in, "adamw_0")] + big

    sm = _adamw_small(_pack_small(norm1_g, norm2_g, final_g, pool_scale, b_forget, w_pool),
                      _pack_small(m_norm1_g, m_norm2_g, m_final_g, m_pool_scale, m_b_forget, m_w_pool),
                      _pack_small(v_norm1_g, v_norm2_g, v_final_g, v_pool_scale, v_b_forget, v_w_pool), small_all)

    loss = lax.psum(loss_v[0, 0], ("x", "y", "c"))
    order = ["norm1_g", "w_in", "b_forget", "w_pool", "pool_scale", "w_out", "norm2_g", "w_gate", "w_up", "w_down", "final_g"]
    big_idx = {"w_in": 0, "w_out": 1, "w_gate": 2, "w_up": 3, "w_down": 4}
    outs = [loss, dx[None]]
    for kind in range(4):
        small_k = _unpack_small(sm[kind])
        for name in order:
            if name in ("w_gate", "w_up"):
                outs.append(jnp.transpose(big[big_idx[name]][kind])[None])
            elif name in big_idx:
                outs.append(big[big_idx[name]][kind][None])
            else:
                outs.append(small_k[name])
    return tuple(outs)
```

```python
import functools

import jax
import jax.numpy as jnp
import numpy as np
from jax import lax
from jax.experimental import pallas as pl
from jax.experimental.pallas import tpu as pltpu

f32 = jnp.float32
bf16 = jnp.bfloat16

T = 4096
D = 1024
NSH = 4
IN_W = 2056
IN_S = IN_W // NSH
AW = 512
PAIRS = 4
FF = 2816
FS = FF // NSH
WINDOWS = (2, 4, 8, 16)
HALO = 16
EPS = 1e-6
NEG = -1e30
LR, B1, B2, AEPS, WD, STEP = 0.001, 0.9, 0.999, 1e-08, 0.01, 10
SMALL_ROWS = 552

NT = (((1,), (1,)), ((), ()))
TN = (((0,), (0,)), ((), ()))

MESH = pl.DeviceIdType.MESH


def _cp(*sem):
    return pltpu.CompilerParams(dimension_semantics=sem)


def _full(shape):
    n = len(shape)
    return pl.BlockSpec(shape, lambda *_: (0,) * n)


def _rms_inproj(x, g1, wm, wf, dep):
    tm = 512

    def body(x_ref, g_ref, wm_ref, wf_ref, dep_ref, h_ref, qkv_ref, u_ref, fl_ref):
        xv = x_ref[...]
        r = lax.rsqrt(jnp.mean(xv * xv, axis=-1, keepdims=True) + EPS)
        h = (xv * r * g_ref[...]).astype(bf16)
        h_ref[...] = h
        qkv_ref[...] = jnp.dot(h, wm_ref[:, 0:3 * AW], preferred_element_type=f32).astype(bf16)
        u_ref[...] = jnp.dot(h, wm_ref[:, 3 * AW:4 * AW], preferred_element_type=f32)
        fl_ref[...] = jnp.dot(h, wf_ref[...], preferred_element_type=f32)

    return pl.pallas_call(
        body, name="rms_inproj", grid=(T // tm,),
        in_specs=[pl.BlockSpec((tm, D), lambda i: (i, 0)), _full((1, D)), _full((D, 4 * AW)), _full((D, 128)),
                  _full((8, 128))],
        out_specs=[pl.BlockSpec((tm, D), lambda i: (i, 0)), pl.BlockSpec((tm, 3 * AW), lambda i: (i, 0)),
                   pl.BlockSpec((tm, AW), lambda i: (i, 0)), pl.BlockSpec((tm, 128), lambda i: (i, 0))],
        out_shape=[jax.ShapeDtypeStruct((T, D), bf16), jax.ShapeDtypeStruct((T, 3 * AW), bf16),
                   jax.ShapeDtypeStruct((T, AW), f32), jax.ShapeDtypeStruct((T, 128), f32)],
        compiler_params=_cp("parallel"),
    )(x, g1, wm, wf, dep)


def _log_sigmoid(z):
    return jnp.minimum(z, 0.0) - jnp.log(1.0 + jnp.exp(-jnp.abs(z)))


def _fox_cumsum(fl, bfp):
    nb = T // 128

    def body(fl_ref, b_ref, qa_ref, ka_ref, carry):
        i = pl.program_id(0)

        @pl.when(i == 0)
        def _():
            carry[...] = jnp.zeros_like(carry)

        lf = _log_sigmoid(fl_ref[...] + b_ref[...])
        r = lax.broadcasted_iota(jnp.int32, (128, 128), 0)
        cc = lax.broadcasted_iota(jnp.int32, (128, 128), 1)
        ltri = (cc <= r).astype(f32)
        cb = jnp.dot(ltri, lf, precision=lax.Precision.HIGHEST, preferred_element_type=f32) + carry[0:1, :]
        carry[...] = jnp.broadcast_to(cb[127:128, :], (8, 128))
        hi = cb.astype(bf16)
        r1 = cb - hi.astype(f32)
        mid = r1.astype(bf16)
        lo = (r1 - mid.astype(f32)).astype(bf16)
        head = lax.broadcasted_iota(jnp.int32, (128, AW), 0)
        col = lax.broadcasted_iota(jnp.int32, (128, AW), 1)
        base = 128 * (head >> 1) + 64 * (1 - (head & 1))
        place = lambda off: jnp.logical_and(col == base + off, head < 8).astype(bf16)
        mm = lambda a, off: jnp.dot(a, place(off), preferred_element_type=f32)
        cq = mm(hi, 0) + mm(mid, 1) + mm(lo, 2)
        ck = mm(hi, 3) + mm(mid, 4) + mm(lo, 5)
        within = jnp.bitwise_and(lax.broadcasted_iota(jnp.int32, (128, AW), 1), 63)
        qa_ref[...] = jnp.where(jnp.logical_and(within >= 3, within <= 5), 1.0, cq).astype(bf16)
        ka_ref[...] = jnp.where(within <= 2, 1.0, -ck).astype(bf16)

    return pl.pallas_call(
        body, name="fox_cumsum", grid=(nb,),
        in_specs=[pl.BlockSpec((128, 128), lambda i: (i, 0)), _full((1, 128))],
        out_specs=[pl.BlockSpec((128, AW), lambda i: (i, 0)), pl.BlockSpec((128, AW), lambda i: (i, 0))],
        out_shape=[jax.ShapeDtypeStruct((T, AW), bf16), jax.ShapeDtypeStruct((T, AW), bf16)],
        scratch_shapes=[pltpu.VMEM((8, 128), f32)],
        compiler_params=_cp("arbitrary"),
    )(fl, bfp)


ATT_T = 512


def _causal_steps(key_major):
    n = T // ATT_T
    if key_major:
        pairs = [(i, j) for j in range(n) for i in range(j, n)]
    else:
        pairs = [(i, j) for i in range(n) for j in range(i + 1)]
    it = np.array([p[0] for p in pairs], np.int32)
    jt = np.array([p[1] for p in pairs], np.int32)
    return jnp.asarray(it), jnp.asarray(jt)


def _attn_fwd(qkv, qaug, kaug):
    tq = tk = ATT_T
    it, jt = _causal_steps(False)
    nsteps = it.shape[0]

    def body(it_ref, jt_ref, q_ref, k_ref, v_ref, qa_ref, ka_ref, o_ref, lse_ref, m_sc, l_sc, acc_sc):
        t = pl.program_id(1)
        i = it_ref[t]
        j = jt_ref[t]

        @pl.when(j == 0)
        def _():
            m_sc[...] = jnp.full_like(m_sc, NEG)
            l_sc[...] = jnp.zeros_like(l_sc)
            acc_sc[...] = jnp.zeros_like(acc_sc)

        lane = lax.broadcasted_iota(jnp.int32, (tq, 128), 1)

        def step(on_diagonal):
            q = q_ref[...] * 0.125
            k = k_ref[...]
            v = v_ref[...]
            qa = qa_ref[...]
            ka = ka_ref[...]
            for e in range(2):
                hm = (lane >= 64) if e else (lane < 64)
                s = lax.dot_general(jnp.where(hm, q, qa), jnp.where(hm, k, ka), NT, preferred_element_type=f32)
                if on_diagonal:
                    row = lax.broadcasted_iota(jnp.int32, (tq, tk), 0)
                    col = lax.broadcasted_iota(jnp.int32, (tq, tk), 1)
                    s = jnp.where(col <= row, s, NEG)
                m_prev = m_sc[e]
                m_new = jnp.maximum(m_prev, jnp.max(s, axis=1, keepdims=True))
                alpha = jnp.exp(m_prev - m_new)
                p = jnp.exp(s - m_new)
                l_sc[e] = alpha * l_sc[e] + jnp.sum(p, axis=1, keepdims=True)
                acc_sc[e] = alpha * acc_sc[e] + jnp.dot(p.astype(bf16), v, preferred_element_type=f32)
                m_sc[e] = m_new

        @pl.when(j < i)
        def _():
            step(False)

        @pl.when(j == i)
        def _():
            step(True)
            o0 = acc_sc[0] / l_sc[0]
            o1 = acc_sc[1] / l_sc[1]
            o_ref[...] = jnp.where(lane < 64, o0, o1).astype(bf16)
            lse_ref[...] = jnp.where(lane < 64, m_sc[0] + jnp.log(l_sc[0]), m_sc[1] + jnp.log(l_sc[1]))

    qmap = lambda p, t, it, jt: (it[t], p)
    kmap = lambda p, t, it, jt: (jt[t], p)
    grid_spec = pltpu.PrefetchScalarGridSpec(
        num_scalar_prefetch=2, grid=(PAIRS, nsteps),
        in_specs=[pl.BlockSpec((tq, 128), qmap),
                  pl.BlockSpec((tk, 128), lambda p, t, it, jt: (jt[t], PAIRS + p)),
                  pl.BlockSpec((tk, 128), lambda p, t, it, jt: (jt[t], 2 * PAIRS + p)),
                  pl.BlockSpec((tq, 128), qmap), pl.BlockSpec((tk, 128), kmap)],
        out_specs=[pl.BlockSpec((tq, 128), qmap),
                   pl.BlockSpec((None, tq, 128), lambda p, t, it, jt: (p, it[t], 0))],
        scratch_shapes=[pltpu.VMEM((2, tq, 1), f32), pltpu.VMEM((2, tq, 1), f32), pltpu.VMEM((2, tq, 128), f32)],
    )
    return pl.pallas_call(
        body, name="fox_attn_fwd", grid_spec=grid_spec,
        out_shape=[jax.ShapeDtypeStruct((T, AW), bf16), jax.ShapeDtypeStruct((PAIRS, T, 128), f32)],
        compiler_params=_cp("parallel", "arbitrary"),
    )(it, jt, qkv, qkv, qkv, qaug, kaug)


def _pool_fwd(u, wp, scale):
    tm = 512

    def body(u_ref, wp_ref, sc_ref, pooled_ref, pool_ref, ext):
        i = pl.program_id(0)

        @pl.when(i == 0)
        def _():
            ext[0:HALO, :] = jnp.zeros((HALO, AW), f32)

        uv = u_ref[...]
        ext[HALO:HALO + tm, :] = uv
        t_idx = i * tm + lax.broadcasted_iota(jnp.int32, (tm, 1), 0)
        for g, w in enumerate(WINDOWS):
            lo, hi = 128 * g, 128 * (g + 1)
            ug = uv[:, lo:hi]
            acc = ug
            for d in range(1, w):
                acc = acc + ext[HALO - d:HALO - d + tm, lo:hi]
            cnt = jnp.minimum(t_idx + 1, w).astype(f32)
            pb = (acc / cnt - ug).astype(bf16)
            pooled_ref[:, lo:hi] = pb
            mixed = jnp.dot(pb, wp_ref[g], preferred_element_type=f32)
            pool_ref[:, lo:hi] = (mixed * sc_ref[:, lo:hi]).astype(bf16)
        ext[0:HALO, :] = uv[tm - HALO:tm, :]

    return pl.pallas_call(
        body, name="pool_fwd", grid=(T // tm,),
        in_specs=[pl.BlockSpec((tm, AW), lambda i: (i, 0)), _full((4, 128, 128)), _full((1, AW))],
        out_specs=[pl.BlockSpec((tm, AW), lambda i: (i, 0)), pl.BlockSpec((tm, AW), lambda i: (i, 0))],
        out_shape=[jax.ShapeDtypeStruct((T, AW), bf16), jax.ShapeDtypeStruct((T, AW), bf16)],
        scratch_shapes=[pltpu.VMEM((tm + HALO, AW), f32)],
        compiler_params=_cp("arbitrary"),
    )(u, wp, scale)


def _outproj(x, attn, pool, wo, g2):
    tm = 512

    def body(x_ref, a_ref, p_ref, wo_ref, g_ref, x1_ref, h2_ref):
        x1 = x_ref[...] + jnp.dot(a_ref[...], wo_ref[0:AW, :], preferred_element_type=f32)
        x1 = x1 + jnp.dot(p_ref[...], wo_ref[AW:2 * AW, :], preferred_element_type=f32)
        x1_ref[...] = x1
        r = lax.rsqrt(jnp.mean(x1 * x1, axis=-1, keepdims=True) + EPS)
        h2_ref[...] = (x1 * r * g_ref[...]).astype(bf16)

    return pl.pallas_call(
        body, name="outproj", grid=(T // tm,),
        in_specs=[pl.BlockSpec((tm, D), lambda i: (i, 0)), pl.BlockSpec((tm, AW), lambda i: (i, 0)),
                  pl.BlockSpec((tm, AW), lambda i: (i, 0)), _full((D, D)), _full((1, D))],
        out_specs=[pl.BlockSpec((tm, D), lambda i: (i, 0)), pl.BlockSpec((tm, D), lambda i: (i, 0))],
        out_shape=[jax.ShapeDtypeStruct((T, D), f32), jax.ShapeDtypeStruct((T, D), bf16)],
        compiler_params=_cp("parallel"),
    )(x, attn, pool, wo, g2)


def _mlp_fwd(h2, x1, wg, wu, wd):
    tm = 512

    def body(h_ref, x1_ref, wg_ref, wu_ref, wd_ref, x2_ref, gate_ref, up_ref):
        s = pl.program_id(1)
        h = h_ref[...]
        gate = lax.dot_general(h, wg_ref[...], NT, preferred_element_type=f32)
        up = lax.dot_general(h, wu_ref[...], NT, preferred_element_type=f32)
        gate_ref[...] = gate
        up_ref[...] = up
        a = (gate * jax.nn.sigmoid(gate) * up).astype(bf16)
        part = jnp.dot(a, wd_ref[...], preferred_element_type=f32)

        @pl.when(s == 0)
        def _():
            x2_ref[...] = x1_ref[...] + part

        @pl.when(s > 0)
        def _():
            x2_ref[...] += part

    return pl.pallas_call(
        body, name="mlp_fwd", grid=(T // tm, NSH),
        in_specs=[pl.BlockSpec((tm, D), lambda i, s: (i, 0)), pl.BlockSpec((tm, D), lambda i, s: (i, 0)),
                  pl.BlockSpec((None, FS, D), lambda i, s: (s, 0, 0)), pl.BlockSpec((None, FS, D), lambda i, s: (s, 0, 0)),
                  pl.BlockSpec((None, FS, D), lambda i, s: (s, 0, 0))],
        out_specs=[pl.BlockSpec((tm, D), lambda i, s: (i, 0)), pl.BlockSpec((None, tm, FS), lambda i, s: (s, i, 0)),
                   pl.BlockSpec((None, tm, FS), lambda i, s: (s, i, 0))],
        out_shape=[jax.ShapeDtypeStruct((T, D), f32), jax.ShapeDtypeStruct((NSH, T, FS), f32),
                   jax.ShapeDtypeStruct((NSH, T, FS), f32)],
        compiler_params=_cp("parallel", "arbitrary"),
    )(h2, x1, wg, wu, wd)


def _final_loss(x2, tgt, gf):
    tm = 512

    def body(x_ref, t_ref, g_ref, loss_ref, dg_ref, dx_ref, dxb_ref):
        i = pl.program_id(0)

        @pl.when(i == 0)
        def _():
            loss_ref[...] = jnp.zeros_like(loss_ref)
            dg_ref[...] = jnp.zeros_like(dg_ref)

        xv = x_ref[...]
        g = g_ref[...]
        r = lax.rsqrt(jnp.mean(xv * xv, axis=-1, keepdims=True) + EPS)
        xhat = xv * r
        e = xhat * g - t_ref[...]
        loss_ref[...] += 0.5 * jnp.sum(jnp.mean(e * e, axis=-1, keepdims=True))
        dy = e * (1.0 / D)
        dg_ref[...] += jnp.sum(dy * xhat, axis=0, keepdims=True)
        z = dy * g
        dx = r * (z - xhat * jnp.mean(z * xhat, axis=-1, keepdims=True))
        dx_ref[...] = dx
        dxb_ref[...] = dx.astype(bf16)

    return pl.pallas_call(
        body, name="final_loss", grid=(T // tm,),
        in_specs=[pl.BlockSpec((tm, D), lambda i: (i, 0)), pl.BlockSpec((tm, D), lambda i: (i, 0)), _full((1, D))],
        out_specs=[_full((8, 128)), _full((1, D)), pl.BlockSpec((tm, D), lambda i: (i, 0)),
                   pl.BlockSpec((tm, D), lambda i: (i, 0))],
        out_shape=[jax.ShapeDtypeStruct((8, 128), f32), jax.ShapeDtypeStruct((1, D), f32),
                   jax.ShapeDtypeStruct((T, D), f32), jax.ShapeDtypeStruct((T, D), bf16)],
        compiler_params=_cp("arbitrary"),
    )(x2, tgt, gf)


def _mlp_bwd(dx2b, dx2, gate, up, wg, wu, wd, x1, g2):
    tm = 512

    def body(dxb_ref, dx_ref, gate_ref, up_ref, wg_ref, wu_ref, wd_ref, x1_ref, g_ref,
             a_ref, dg_ref, du_ref, dx1_ref, dx1b_ref, dn_ref, acc):
        i = pl.program_id(0)
        s = pl.program_id(1)

        @pl.when(jnp.logical_and(i == 0, s == 0))
        def _():
            dn_ref[...] = jnp.zeros_like(dn_ref)

        da = lax.dot_general(dxb_ref[...], wd_ref[...], NT, preferred_element_type=f32)
        gate = gate_ref[...]
        upv = up_ref[...]
        sg = jax.nn.sigmoid(gate)
        silu = gate * sg
        a_ref[...] = (silu * upv).astype(bf16)
        dgate = (da * upv * (sg * (1.0 + gate * (1.0 - sg)))).astype(bf16)
        dup = (da * silu).astype(bf16)
        dg_ref[...] = dgate
        du_ref[...] = dup
        part = jnp.dot(dgate, wg_ref[...], preferred_element_type=f32)
        part = part + jnp.dot(dup, wu_ref[...], preferred_element_type=f32)

        @pl.when(s == 0)
        def _():
            acc[...] = part

        @pl.when(s > 0)
        def _():
            acc[...] += part

        @pl.when(s == NSH - 1)
        def _():
            xv = x1_ref[...]
            r = lax.rsqrt(jnp.mean(xv * xv, axis=-1, keepdims=True) + EPS)
            xhat = xv * r
            dh = acc[...]
            dn_ref[...] += jnp.sum(dh * xhat, axis=0, keepdims=True)
            z = dh * g_ref[...]
            dx1 = dx_ref[...] + r * (z - xhat * jnp.mean(z * xhat, axis=-1, keepdims=True))
            dx1_ref[...] = dx1
            dx1b_ref[...] = dx1.astype(bf16)

    row = lambda i, s: (i, 0)
    sl = lambda i, s: (s, i, 0)
    wsl = lambda i, s: (s, 0, 0)
    return pl.pallas_call(
        body, name="mlp_bwd", grid=(T // tm, NSH),
        in_specs=[pl.BlockSpec((tm, D), row), pl.BlockSpec((tm, D), row),
                  pl.BlockSpec((None, tm, FS), sl), pl.BlockSpec((None, tm, FS), sl),
                  pl.BlockSpec((None, FS, D), wsl), pl.BlockSpec((None, FS, D), wsl), pl.BlockSpec((None, FS, D), wsl),
                  pl.BlockSpec((tm, D), row), pl.BlockSpec((1, D), lambda i, s: (0, 0))],
        out_specs=[pl.BlockSpec((None, tm, FS), sl), pl.BlockSpec((None, tm, FS), sl), pl.BlockSpec((None, tm, FS), sl),
                   pl.BlockSpec((tm, D), row), pl.BlockSpec((tm, D), row), pl.BlockSpec((1, D), lambda i, s: (0, 0))],
        out_shape=[jax.ShapeDtypeStruct((NSH, T, FS), bf16)] * 3
        + [jax.ShapeDtypeStruct((T, D), f32), jax.ShapeDtypeStruct((T, D), bf16), jax.ShapeDtypeStruct((1, D), f32)],
        scratch_shapes=[pltpu.VMEM((tm, D), f32)],
        compiler_params=_cp("arbitrary", "arbitrary"),
    )(dx2b, dx2, gate, up, wg, wu, wd, x1, g2)


def _mm_tn(a, bs, name, a_sharded=False, b_sharded=False, tk=512, out_dtype=bf16):
    nb = len(bs)
    sh = NSH if (a_sharded or b_sharded) else 1
    m = a.shape[-1]
    nk = T // tk

    def body(a_ref, *refs):
        kk = pl.program_id(1)
        av = a_ref[...]
        for b_ref, o_ref, acc in zip(refs[:nb], refs[nb:2 * nb], refs[2 * nb:]):
            upd = lax.dot_general(av, b_ref[...], TN, preferred_element_type=f32)

            @pl.when(kk == 0)
            def _():
                acc[...] = upd

            @pl.when(kk > 0)
            def _():
                acc[...] += upd

            @pl.when(kk == nk - 1)
            def _():
                o_ref[...] = acc[...].astype(out_dtype)

    a_spec = (pl.BlockSpec((None, tk, m), lambda s, k: (s, k, 0)) if a_sharded
              else pl.BlockSpec((tk, m), lambda s, k: (k, 0)))
    b_specs, o_specs, o_shapes, scratch = [], [], [], []
    for b in bs:
        n = b.shape[-1]
        b_specs.append(pl.BlockSpec((None, tk, n), lambda s, k: (s, k, 0)) if b_sharded
                       else pl.BlockSpec((tk, n), lambda s, k: (k, 0)))
        scratch.append(pltpu.VMEM((m, n), f32))
        if sh > 1:
            o_specs.append(pl.BlockSpec((None, m, n), lambda s, k: (s, 0, 0)))
            o_shapes.append(jax.ShapeDtypeStruct((sh, m, n), out_dtype))
        else:
            o_specs.append(pl.BlockSpec((m, n), lambda s, k: (0, 0)))
            o_shapes.append(jax.ShapeDtypeStruct((m, n), out_dtype))
    return pl.pallas_call(
        body, name=name, grid=(sh, nk), in_specs=[a_spec] + b_specs, out_specs=o_specs, out_shape=o_shapes,
        scratch_shapes=scratch, compiler_params=_cp("arbitrary", "arbitrary"),
    )(a, *bs)


def _outproj_bwd(dx1b, wo):
    tm = 512

    def body(dx_ref, wo_ref, da_ref, dp_ref):
        dx = dx_ref[...]
        da_ref[...] = lax.dot_general(dx, wo_ref[0:AW, :], NT, preferred_element_type=f32).astype(bf16)
        dp_ref[...] = lax.dot_general(dx, wo_ref[AW:2 * AW, :], NT, preferred_element_type=f32)

    return pl.pallas_call(
        body, name="outproj_bwd", grid=(T // tm,),
        in_specs=[pl.BlockSpec((tm, D), lambda i: (i, 0)), _full((D, D))],
        out_specs=[pl.BlockSpec((tm, AW), lambda i: (i, 0)), pl.BlockSpec((tm, AW), lambda i: (i, 0))],
        out_shape=[jax.ShapeDtypeStruct((T, AW), bf16), jax.ShapeDtypeStruct((T, AW), f32)],
        compiler_params=_cp("parallel"),
    )(dx1b, wo)


def _pool_bwd(dpool, pooled, wp, scale, dep):
    tm = 512
    n = T // tm

    def body(dp_ref, pb_ref, wp_ref, sc_ref, dep_ref, du_ref, dsc_ref, dwp_ref, ext):
        i = pl.program_id(0)

        @pl.when(i == 0)
        def _():
            ext[tm:tm + HALO, :] = jnp.zeros((HALO, AW), f32)
            dsc_ref[...] = jnp.zeros_like(dsc_ref)
            dwp_ref[...] = jnp.zeros_like(dwp_ref)

        t_idx = (n - 1 - i) * tm + lax.broadcasted_iota(jnp.int32, (tm, 1), 0)
        for g, w in enumerate(WINDOWS):
            lo, hi = 128 * g, 128 * (g + 1)
            pb = pb_ref[:, lo:hi]
            mixed = jnp.dot(pb, wp_ref[g], preferred_element_type=f32)
            dpo = dp_ref[:, lo:hi]
            dsc_ref[:, lo:hi] += jnp.sum(dpo * mixed, axis=0, keepdims=True)
            dmr = (dpo * sc_ref[:, lo:hi]).astype(bf16)
            dwp_ref[g] += lax.dot_general(pb, dmr, TN, preferred_element_type=f32)
            dpl = lax.dot_general(dmr, wp_ref[g], NT, preferred_element_type=f32)
            cnt = jnp.minimum(t_idx + 1, w).astype(f32)
            dpn = dpl / cnt
            ext[0:tm, lo:hi] = dpn
            acc = dpn
            for d in range(1, w):
                acc = acc + ext[d:d + tm, lo:hi]
            du_ref[:, lo:hi] = (acc - dpl).astype(bf16)
        ext[tm:tm + HALO, :] = ext[0:HALO, :]

    rev = lambda i: (n - 1 - i, 0)
    return pl.pallas_call(
        body, name="pool_bwd", grid=(n,),
        in_specs=[pl.BlockSpec((tm, AW), rev), pl.BlockSpec((tm, AW), rev), _full((4, 128, 128)), _full((1, AW)),
                  _full((8, 128))],
        out_specs=[pl.BlockSpec((tm, AW), rev), _full((1, AW)), _full((4, 128, 128))],
        out_shape=[jax.ShapeDtypeStruct((T, AW), bf16), jax.ShapeDtypeStruct((1, AW), f32),
                   jax.ShapeDtypeStruct((4, 128, 128), f32)],
        scratch_shapes=[pltpu.VMEM((tm + HALO, AW), f32)],
        compiler_params=_cp("arbitrary"),
    )(dpool, pooled, wp, scale, dep)


def _attn_bwd(qkv, qaug, kaug, attn, dattn, lse):
    tq = tk = ATT_T
    n = T // tq
    it, jt = _causal_steps(True)
    nsteps = it.shape[0]

    def body(it_ref, jt_ref, q_ref, k_ref, v_ref, qa_ref, ka_ref, o_ref, do_ref, lse_ref,
             dq_ref, dqs_ref, dk_ref, dks_ref, dv_ref, dq_acc, dk_acc, dv_acc):
        t = pl.program_id(1)
        i = it_ref[t]
        j = jt_ref[t]

        @pl.when(t == 0)
        def _():
            dq_acc[...] = jnp.zeros_like(dq_acc)

        @pl.when(i == j)
        def _():
            dk_acc[...] = jnp.zeros_like(dk_acc)
            dv_acc[...] = jnp.zeros_like(dv_acc)

        lane = lax.broadcasted_iota(jnp.int32, (tq, 128), 1)

        def step(on_diagonal):
            q = q_ref[...] * 0.125
            k = k_ref[...]
            v = v_ref[...]
            qa = qa_ref[...]
            ka = ka_ref[...]
            do = do_ref[...]
            dd = do.astype(f32) * o_ref[...].astype(f32)
            r0 = pl.multiple_of(i * tq, tq)
            for e in range(2):
                hm = (lane >= 64) if e else (lane < 64)
                qe = jnp.where(hm, q, qa)
                ke = jnp.where(hm, k, ka)
                doe = jnp.where(hm, do, jnp.zeros_like(do))
                delta = jnp.sum(jnp.where(hm, dd, 0.0), axis=1, keepdims=True)
                s = lax.dot_general(qe, ke, NT, preferred_element_type=f32) - lse_ref[:, 64 * e:64 * e + 1]
                if on_diagonal:
                    row = lax.broadcasted_iota(jnp.int32, (tq, tk), 0)
                    col = lax.broadcasted_iota(jnp.int32, (tq, tk), 1)
                    s = jnp.where(col <= row, s, NEG)
                p = jnp.exp(s)
                dv_acc[...] += lax.dot_general(p.astype(bf16), doe, TN, preferred_element_type=f32)
                dp = lax.dot_general(doe, v, NT, preferred_element_type=f32)
                dsb = (p * (dp - delta)).astype(bf16)
                dk_acc[e] += lax.dot_general(dsb, qe, TN, preferred_element_type=f32)
                dq_acc[e, pl.ds(r0, tq), :] += jnp.dot(dsb, ke, preferred_element_type=f32)

        @pl.when(i > j)
        def _():
            step(False)

        @pl.when(i == j)
        def _():
            step(True)

        @pl.when(i == n - 1)
        def _():
            dk_ref[...] = jnp.where(lane < 64, dk_acc[0], dk_acc[1]).astype(bf16)
            dks_ref[...] = jnp.where(lane < 64, dk_acc[1], dk_acc[0])
            dv_ref[...] = dv_acc[...].astype(bf16)

        @pl.when(t == nsteps - 1)
        def _():
            lane_t = lax.broadcasted_iota(jnp.int32, (T, 128), 1)
            dq_ref[...] = (jnp.where(lane_t < 64, dq_acc[0], dq_acc[1]) * 0.125).astype(bf16)
            dqs_ref[...] = jnp.where(lane_t < 64, dq_acc[1], dq_acc[0])

    qmap = lambda p, t, it, jt: (it[t], p)
    grid_spec = pltpu.PrefetchScalarGridSpec(
        num_scalar_prefetch=2, grid=(PAIRS, nsteps),
        in_specs=[pl.BlockSpec((tq, 128), qmap),
                  pl.BlockSpec((tk, 128), lambda p, t, it, jt: (jt[t], PAIRS + p)),
                  pl.BlockSpec((tk, 128), lambda p, t, it, jt: (jt[t], 2 * PAIRS + p)),
                  pl.BlockSpec((tq, 128), qmap), pl.BlockSpec((tk, 128), lambda p, t, it, jt: (jt[t], p)),
                  pl.BlockSpec((tq, 128), qmap), pl.BlockSpec((tq, 128), qmap),
                  pl.BlockSpec((None, tq, 128), lambda p, t, it, jt: (p, it[t], 0))],
        out_specs=[pl.BlockSpec((T, 128), lambda p, t, it, jt: (0, p)),
                   pl.BlockSpec((None, T, 128), lambda p, t, it, jt: (p, 0, 0)),
                   pl.BlockSpec((tk, 128), lambda p, t, it, jt: (jt[t], p)),
                   pl.BlockSpec((None, tk, 128), lambda p, t, it, jt: (p, jt[t], 0)),
                   pl.BlockSpec((tk, 128), lambda p, t, it, jt: (jt[t], p))],
        scratch_shapes=[pltpu.VMEM((2, T, 128), f32), pltpu.VMEM((2, tk, 128), f32), pltpu.VMEM((tk, 128), f32)],
    )
    return pl.pallas_call(
        body, name="fox_attn_bwd", grid_spec=grid_spec,
        out_shape=[jax.ShapeDtypeStruct((T, AW), bf16), jax.ShapeDtypeStruct((PAIRS, T, 128), f32),
                   jax.ShapeDtypeStruct((T, AW), bf16), jax.ShapeDtypeStruct((PAIRS, T, 128), f32),
                   jax.ShapeDtypeStruct((T, AW), bf16)],
        compiler_params=_cp("parallel", "arbitrary"),
    )(it, jt, qkv, qkv, qkv, qaug, kaug, attn, dattn, lse)


def _fox_cumsum_bwd(dqs, dks, fl, bfp):
    nb = T // 128
    hp = lax.Precision.HIGHEST

    def body(dqs_ref, dks_ref, fl_ref, b_ref, df_ref, db_ref, carry):
        i = pl.program_id(0)

        @pl.when(i == 0)
        def _():
            carry[...] = jnp.zeros_like(carry)
            db_ref[...] = jnp.zeros_like(db_ref)

        r = lax.broadcasted_iota(jnp.int32, (128, 128), 0)
        cc = lax.broadcasted_iota(jnp.int32, (128, 128), 1)
        pick = lambda even_lane, odd_lane, p: jnp.logical_or(
            jnp.logical_and(r == even_lane, cc == 2 * p), jnp.logical_and(r == odd_lane, cc == 2 * p + 1)).astype(f32)
        dc = jnp.zeros((128, 128), f32)
        for p in range(PAIRS):
            dc = dc + jnp.dot(dqs_ref[p], pick(64, 0, p), precision=hp, preferred_element_type=f32)
            dc = dc - jnp.dot(dks_ref[p], pick(67, 3, p), precision=hp, preferred_element_type=f32)
        utri = (cc >= r).astype(f32)
        dl = jnp.dot(utri, dc, precision=hp, preferred_element_type=f32) + carry[0:1, :]
        carry[...] = jnp.broadcast_to(dl[0:1, :], (8, 128))
        z = fl_ref[...] + b_ref[...]
        df = dl * jax.nn.sigmoid(-z)
        df_ref[...] = df.astype(bf16)
        db_ref[...] += jnp.sum(df, axis=0, keepdims=True)

    rev = lambda i: (nb - 1 - i, 0)
    return pl.pallas_call(
        body, name="fox_cumsum_bwd", grid=(nb,),
        in_specs=[pl.BlockSpec((PAIRS, 128, 128), lambda i: (0, nb - 1 - i, 0)),
                  pl.BlockSpec((PAIRS, 128, 128), lambda i: (0, nb - 1 - i, 0)),
                  pl.BlockSpec((128, 128), rev), _full((1, 128))],
        out_specs=[pl.BlockSpec((128, 128), rev), _full((1, 128))],
        out_shape=[jax.ShapeDtypeStruct((T, 128), bf16), jax.ShapeDtypeStruct((1, 128), f32)],
        scratch_shapes=[pltpu.VMEM((8, 128), f32)],
        compiler_params=_cp("arbitrary"),
    )(dqs, dks, fl, bfp)


def _inproj_bwd(dq, dk, dv, du, df, wm, wf, x, dx1, g1):
    tm = 512

    def body(dq_ref, dk_ref, dv_ref, du_ref, df_ref, wm_ref, wf_ref, x_ref, dx1_ref, g_ref, dx_ref, dn_ref):
        i = pl.program_id(0)

        @pl.when(i == 0)
        def _():
            dn_ref[...] = jnp.zeros_like(dn_ref)

        dh = lax.dot_general(dq_ref[...], wm_ref[:, 0:AW], NT, preferred_element_type=f32)
        dh = dh + lax.dot_general(dk_ref[...], wm_ref[:, AW:2 * AW], NT, preferred_element_type=f32)
        dh = dh + lax.dot_general(dv_ref[...], wm_ref[:, 2 * AW:3 * AW], NT, preferred_element_type=f32)
        dh = dh + lax.dot_general(du_ref[...], wm_ref[:, 3 * AW:4 * AW], NT, preferred_element_type=f32)
        dh = dh + lax.dot_general(df_ref[...], wf_ref[...], NT, preferred_element_type=f32)
        xv = x_ref[...]
        r = lax.rsqrt(jnp.mean(xv * xv, axis=-1, keepdims=True) + EPS)
        xhat = xv * r
        dn_ref[...] += jnp.sum(dh * xhat, axis=0, keepdims=True)
        z = dh * g_ref[...]
        dx_ref[...] = dx1_ref[...] + r * (z - xhat * jnp.mean(z * xhat, axis=-1, keepdims=True))

    row = lambda i: (i, 0)
    return pl.pallas_call(
        body, name="inproj_bwd", grid=(T // tm,),
        in_specs=[pl.BlockSpec((tm, AW), row)] * 4 + [pl.BlockSpec((tm, 128), row), _full((D, 4 * AW)), _full((D, 128)),
                                                       pl.BlockSpec((tm, D), row), pl.BlockSpec((tm, D), row), _full((1, D))],
        out_specs=[pl.BlockSpec((tm, D), row), _full((1, D))],
        out_shape=[jax.ShapeDtypeStruct((T, D), f32), jax.ShapeDtypeStruct((1, D), f32)],
        compiler_params=_cp("arbitrary"),
    )(dq, dk, dv, du, df, wm, wf, x, dx1, g1)


def _adamw_math(w, g, m, v):
    m = B1 * m + (1.0 - B1) * g
    v = B2 * v + (1.0 - B2) * (g * g)
    m_hat = m / (1.0 - B1 ** STEP)
    v_hat = v / (1.0 - B2 ** STEP)
    delta = -LR * (m_hat / (jnp.sqrt(v_hat) + AEPS) + WD * w)
    return delta, m, v


def _adamw_shard(w, m, v, p_mine, p_other, name):
    rows, cols = w.shape
    tr = 256 if rows % 256 == 0 else 176

    def body(w_ref, m_ref, v_ref, a_ref, b_ref, g_ref, d_ref, nm_ref, nv_ref):
        g = a_ref[...] + b_ref[...]
        g_ref[...] = g
        d_ref[...], nm_ref[...], nv_ref[...] = _adamw_math(w_ref[...], g, m_ref[...], v_ref[...])

    spec = pl.BlockSpec((tr, cols), lambda i: (i, 0))
    return pl.pallas_call(
        body, name=name, grid=(rows // tr,), in_specs=[spec] * 5, out_specs=[spec] * 4,
        out_shape=[jax.ShapeDtypeStruct((rows, cols), f32)] * 4, compiler_params=_cp("parallel"),
    )(w, m, v, p_mine, p_other)


def _adamw_small(w, m, v, parts):
    def body(w_ref, m_ref, v_ref, p_ref, g_ref, d_ref, nm_ref, nv_ref):
        g = p_ref[0]
        for k in range(1, 8):
            g = g + p_ref[k]
        g_ref[...] = g
        d_ref[...], nm_ref[...], nv_ref[...] = _adamw_math(w_ref[...], g, m_ref[...], v_ref[...])

    return pl.pallas_call(
        body, name="adamw_small", out_shape=[jax.ShapeDtypeStruct((SMALL_ROWS, 128), f32)] * 4,
    )(w, m, v, parts)


def _sum4(recv, g, mine, name):
    _, rows, cols = recv.shape
    tr = 256 if rows % 256 == 0 else 176

    def body(mine_ref, r_ref, g_ref, o_ref):
        o_ref[...] = ((g_ref[...].astype(f32) + r_ref[0].astype(f32))
                      + (r_ref[1].astype(f32) + r_ref[2].astype(f32)))

    grid_spec = pltpu.PrefetchScalarGridSpec(
        num_scalar_prefetch=1, grid=(rows // tr,),
        in_specs=[pl.BlockSpec((3, tr, cols), lambda i, m: (0, i, 0)),
                  pl.BlockSpec((None, tr, cols), lambda i, m: (m[0], i, 0))],
        out_specs=pl.BlockSpec((tr, cols), lambda i, m: (i, 0)))
    return pl.pallas_call(
        body, name=name, grid_spec=grid_spec, out_shape=jax.ShapeDtypeStruct((rows, cols), f32),
        compiler_params=_cp("arbitrary"),
    )(mine, recv, g)


_HBM = pl.BlockSpec(memory_space=pltpu.HBM)
_SEM = pl.BlockSpec(memory_space=pltpu.SEMAPHORE)
_EFFECT = pltpu.SideEffectType.DATAFLOW_SIDE_EFFECTING


def _in_hbm(a):
    return pltpu.with_memory_space_constraint(a, pltpu.HBM)


def _mesh_pos():
    return lax.axis_index("x"), lax.axis_index("y"), lax.axis_index("c")


def _other_chips(x, y):
    return [(1 - x, y), (x, 1 - y), (1 - x, 1 - y)]


def _gather_copy(srcs, lands, send_sems, recv_sems, a, k, slot):
    x, y, c = _mesh_pos()
    cx, cy = _other_chips(x, y)[k]
    return pltpu.make_async_remote_copy(
        src_ref=srcs[a], dst_ref=lands[a].at[slot], send_sem=send_sems.at[3 * a + k], recv_sem=recv_sems.at[3 * a + k],
        device_id=(cx, cy, c), device_id_type=MESH)


def _scatter_copy(srcs, lands, send_sems, recv_sems, a, k):
    x, y, c = _mesh_pos()
    cx, cy = _other_chips(x, y)[k]
    return pltpu.make_async_remote_copy(
        src_ref=srcs[a].at[2 * cx + cy], dst_ref=lands[a].at[k], send_sem=send_sems.at[3 * a + k],
        recv_sem=recv_sems.at[3 * a + k], device_id=(cx, cy, c), device_id_type=MESH)


def _all_gather_w_in(part):
    rows = part.shape[0] // 2

    def body(src, dst, send_sems, recv_sems, loc_sem):
        x, y, c = _mesh_pos()
        mine = 2 * x + y
        chips = _other_chips(x, y)
        half = lambda ref, cc: ref.at[pl.ds(pl.multiple_of(cc * rows, rows), rows), :]

        def over_ici(k, slot):
            cx, cy = chips[k]
            return pltpu.make_async_remote_copy(
                src_ref=half(src, c), dst_ref=half(dst.at[slot], c), send_sem=send_sems.at[k], recv_sem=recv_sems.at[k],
                device_id=(cx, cy, c), device_id_type=MESH)

        def to_sibling(k, cc):
            slot = 2 * chips[k][0] + chips[k][1]
            return pltpu.make_async_remote_copy(
                src_ref=half(dst.at[slot], cc), dst_ref=half(dst.at[slot], cc), send_sem=send_sems.at[3 + k],
                recv_sem=recv_sems.at[3 + k], device_id=(x, y, 1 - c), device_id_type=MESH)

        local = pltpu.make_async_copy(src, dst.at[mine], loc_sem.at[0])
        local.start()
        first = [over_ici(k, mine) for k in range(3)]
        for cp in first:
            cp.start()
        passed = [to_sibling(k, c) for k in range(3)]
        for k in range(3):
            over_ici(k, 2 * chips[k][0] + chips[k][1]).wait_recv()
            passed[k].start()
        for k in range(3):
            to_sibling(k, 1 - c).wait_recv()
        for cp in first + passed:
            cp.wait_send()
        local.wait()

    return pl.pallas_call(
        body, name="all_gather_w_in", in_specs=[_HBM], out_specs=_HBM,
        out_shape=jax.ShapeDtypeStruct((NSH,) + part.shape, part.dtype),
        scratch_shapes=[pltpu.SemaphoreType.DMA((6,)), pltpu.SemaphoreType.DMA((6,)), pltpu.SemaphoreType.DMA((1,))],
    )(part)


def _split_start(name, srcs, lands, n_sems, plan):
    n, nl = len(srcs), len(lands)

    def body(*refs):
        src_refs, land_refs = refs[:n], refs[n:n + nl]
        send_sems, recv_sems = refs[n + nl], refs[n + nl + 1]
        token = refs[-1]
        sends, _ = plan(src_refs, land_refs, send_sems, recv_sems)
        for cp in sends:
            cp.start()
        token[...] = jnp.zeros_like(token)

    outs = pl.pallas_call(
        body, name=name,
        in_specs=[_HBM] * (n + nl),
        out_specs=[_SEM, _SEM] + [_HBM] * (n + nl) + [pl.BlockSpec(memory_space=pltpu.VMEM)],
        out_shape=[pltpu.SemaphoreType.DMA((n_sems,)), pltpu.SemaphoreType.DMA((n_sems,))]
        + [pltpu.HBM(a.shape, a.dtype) for a in list(srcs) + list(lands)] + [jax.ShapeDtypeStruct((8, 128), f32)],
        input_output_aliases={i: 2 + i for i in range(n + nl)},
        compiler_params=pltpu.CompilerParams(has_side_effects=_EFFECT),
    )(*[_in_hbm(a) for a in list(srcs) + list(lands)])
    return outs[0], outs[1], list(outs[2:2 + n]), list(outs[2 + n:2 + n + nl]), outs[-1]


def _split_wait(name, send_sems, recv_sems, srcs, lands, after, plan):
    n, nl = len(srcs), len(lands)

    def body(*refs):
        src_refs, land_refs = refs[:n], refs[n:n + nl]
        s_sems, r_sems = refs[n + nl], refs[n + nl + 1]
        sends, recvs = plan(src_refs, land_refs, s_sems, r_sems)
        for cp in recvs:
            cp.wait_recv()
        for cp in sends:
            cp.wait_send()

    outs = pl.pallas_call(
        body, name=name,
        in_specs=[_HBM] * (n + nl) + [_SEM, _SEM, pl.BlockSpec(memory_space=pl.ANY)],
        out_specs=[_HBM] * (n + nl),
        out_shape=[pltpu.HBM(a.shape, a.dtype) for a in list(srcs) + list(lands)],
        input_output_aliases={i: i for i in range(n + nl)},
        compiler_params=pltpu.CompilerParams(has_side_effects=_EFFECT),
    )(*srcs, *lands, send_sems, recv_sems, after)
    return list(outs[:n]), list(outs[n:])


def _gather_plan(srcs, lands, ss, rs):
    x, y, _ = _mesh_pos()
    chips = _other_chips(x, y)
    sends = [_gather_copy(srcs, lands, ss, rs, a, k, 2 * x + y) for a in range(len(srcs)) for k in range(3)]
    recvs = [_gather_copy(srcs, lands, ss, rs, a, k, 2 * chips[k][0] + chips[k][1])
             for a in range(len(srcs)) for k in range(3)]
    return sends, recvs


def _scatter_plan(srcs, lands, ss, rs):
    cps = [_scatter_copy(srcs, lands, ss, rs, a, k) for a in range(len(srcs)) for k in range(3)]
    return cps, cps


def _tail_plan(srcs, lands, ss, rs):
    x, y, c = _mesh_pos()
    me = 4 * x + 2 * y + c
    cps = [_scatter_copy(srcs[:1], lands[:1], ss, rs, 0, k) for k in range(3)]
    for f in range(1, 8):
        peer = ((x + (f >> 2)) % 2, (y + ((f >> 1) & 1)) % 2, (c + (f & 1)) % 2)
        cps.append(pltpu.make_async_remote_copy(
            src_ref=srcs[1], dst_ref=lands[1].at[me], send_sem=ss.at[2 + f], recv_sem=rs.at[2 + f],
            device_id=peer, device_id_type=MESH))
    return cps, cps


def _swap_with_sibling(parts, name):
    n = len(parts)

    def body(*refs):
        srcs, dsts = refs[:n], refs[n:2 * n]
        send_sems, recv_sems = refs[2 * n:]
        x, y, c = _mesh_pos()
        cps = [pltpu.make_async_remote_copy(src_ref=srcs[a], dst_ref=dsts[a], send_sem=send_sems.at[a],
                                            recv_sem=recv_sems.at[a], device_id=(x, y, 1 - c), device_id_type=MESH)
               for a in range(n)]
        for cp in cps:
            cp.start()
        for cp in cps:
            cp.wait_recv()
        for cp in cps:
            cp.wait_send()

    return pl.pallas_call(
        body, name=name, in_specs=[_HBM] * n, out_specs=[_HBM] * n,
        out_shape=[jax.ShapeDtypeStruct(p.shape, p.dtype) for p in parts],
        scratch_shapes=[pltpu.SemaphoreType.DMA((n,)), pltpu.SemaphoreType.DMA((n,))],
    )(*parts)


def _pack_small(n1, n2, nf, ps, bfv, wp):
    pad8 = lambda r: jnp.pad(r, ((0, 8 - r.shape[0]), (0, 0)))
    return jnp.concatenate([n1.reshape(8, 128), n2.reshape(8, 128), nf.reshape(8, 128), pad8(ps.reshape(4, 128)),
                            pad8(jnp.pad(bfv.reshape(1, 8), ((0, 0), (0, 120)))), wp.reshape(512, 128)], axis=0)


def _unpack_small(p):
    return dict(norm1_g=p[0:8].reshape(1, D), norm2_g=p[8:16].reshape(1, D), final_g=p[16:24].reshape(D),
                pool_scale=p[24:28].reshape(1, AW), b_forget=p[32:33, 0:8], w_pool=p[40:552].reshape(1, 4, 128, 128))


def _forward(x, tgt, wm, wf, mlp_w_fn, g1, bfp, wp, scale, g2, gf, dep):
    h, qkv, u, fl = _rms_inproj(x, g1, wm, wf, dep)
    qaug, kaug = _fox_cumsum(fl, bfp)
    attn, lse = _attn_fwd(qkv, qaug, kaug)
    pooled, pool = _pool_fwd(u, wp, scale)
    wo, wgt, wut, wd = mlp_w_fn(pool)
    x1, h2 = _outproj(x, attn, pool, wo, g2)
    x2, gate, up = _mlp_fwd(h2, x1, wgt, wut, wd)
    loss, dgf, dx2, dx2b = _final_loss(x2, tgt, gf)
    saved = dict(h=h, qkv=qkv, fl=fl, qaug=qaug, kaug=kaug, attn=attn, lse=lse, pooled=pooled, pool=pool, x1=x1, h2=h2,
                 gate=gate, up=up, wo=wo, wgt=wgt, wut=wut, wd=wd)
    return loss, dgf, dx2, dx2b, saved


def _backward_mlp(sv, dx2, dx2b, g2):
    a_b, dgate, dup, dx1, dx1b, dg2 = _mlp_bwd(dx2b, dx2, sv["gate"], sv["up"], sv["wgt"], sv["wut"], sv["wd"], sv["x1"], g2)
    (dwd,) = _mm_tn(a_b, [dx2b], "dw_down", a_sharded=True)
    (dwgt,) = _mm_tn(dgate, [sv["h2"]], "dw_gate", a_sharded=True)
    (dwut,) = _mm_tn(dup, [sv["h2"]], "dw_up", a_sharded=True)
    return dx1, dx1b, dg2, (dwgt, dwut, dwd)


def _backward_outproj(sv, dx1b):
    dattn, dpool = _outproj_bwd(dx1b, sv["wo"])
    dwo_a, = _mm_tn(sv["attn"], [dx1b], "dw_out_attn")
    dwo_p, = _mm_tn(sv["pool"], [dx1b], "dw_out_pool")
    dwo = jnp.concatenate([dwo_a, dwo_p], axis=0).reshape(NSH, D // NSH, D)
    return dattn, dpool, dwo


def _backward_mixer(sv, x, dx1, dattn, dpool, wm, wf, g1, bfp, wp, scale, dep):
    du, dscale, dwp = _pool_bwd(dpool, sv["pooled"], wp, scale, dep)
    dq, dqs, dk, dks, dv = _attn_bwd(sv["qkv"], sv["qaug"], sv["kaug"], sv["attn"], dattn, sv["lse"])
    df, dbf = _fox_cumsum_bwd(dqs, dks, sv["fl"], bfp)
    dx, dg1 = _inproj_bwd(dq, dk, dv, du, df, wm, wf, x, dx1, g1)
    dwq, dwk, dwv, dwu_in, dwf = _mm_tn(sv["h"], [dq, dk, dv, du, df], "dw_in")
    dwin = jnp.concatenate([dwq, dwk, dwv, dwf[:, 0:8], dwu_in], axis=1)
    dwin = dwin.reshape(D, NSH, IN_S).transpose(1, 0, 2)
    return dx, dg1, dscale, dwp, dbf, dwin


def kernel(x, norm1_g, w_in, b_forget, w_pool, pool_scale, w_out, norm2_g, w_gate, w_up, w_down, final_g, loss_target, m_norm1_g, m_w_in, m_b_forget, m_w_pool, m_pool_scale, m_w_out, m_norm2_g, m_w_gate, m_w_up, m_w_down, m_final_g, v_norm1_g, v_w_in, v_b_forget, v_w_pool, v_pool_scale, v_w_out, v_norm2_g, v_w_gate, v_w_up, v_w_down, v_final_g):
    mine = (2 * lax.axis_index("x") + lax.axis_index("y")).astype(jnp.int32)
    mine1 = mine.reshape(1)
    tr = lambda a: jnp.transpose(a[0])

    win4 = _all_gather_w_in(w_in[0].astype(bf16))
    later = [w_out[0].astype(bf16), tr(w_gate).astype(bf16), tr(w_up).astype(bf16), w_down[0].astype(bf16)]
    lands = [lax.dynamic_update_slice(lax.empty((NSH,) + p.shape, bf16), p[None], (mine, 0, 0)) for p in later]
    ag_send, ag_recv, later_thru, lands_thru, ag_token = _split_start("all_gather_start", later, lands, 12, _gather_plan)
    win = win4.transpose(1, 0, 2).reshape(D, IN_W)
    wm = jnp.concatenate([win[:, 0:3 * AW], win[:, 3 * AW + 8:]], axis=1)
    wf = jnp.pad(win[:, 3 * AW:3 * AW + 8], ((0, 0), (0, 120)))
    bfp = jnp.pad(b_forget, ((0, 0), (0, 120)))
    wp = w_pool[0].astype(bf16)
    gf = final_g.reshape(1, D)

    def later_weights(after):
        _, (wo4, wgt, wut, wd) = _split_wait("all_gather_wait", ag_send, ag_recv, later_thru, lands_thru, after, _gather_plan)
        return wo4.reshape(D, D), wgt, wut, wd

    xe, tgt = x[0], loss_target[0]
    loss_v, dgf, dx2, dx2b, sv = _forward(xe, tgt, wm, wf, later_weights, norm1_g, bfp, wp, pool_scale, norm2_g, gf, ag_token)
    dx1, dx1b, dg2, mlp_grads = _backward_mlp(sv, dx2, dx2b, norm2_g)
    dattn, dpool, dwo = _backward_outproj(sv, dx1b)
    first = [dwo] + list(mlp_grads)
    first_lands = [lax.empty((3,) + g.shape[1:], bf16) for g in first]
    rs_send, rs_recv, first_thru, first_lands_thru, rs_token = _split_start("reduce_scatter_start", first, first_lands, 12,
                                                                            _scatter_plan)
    dx, dg1, dscale, dwp, dbf, dwin = _backward_mixer(sv, xe, dx1, dattn, dpool, wm, wf, norm1_g, bfp, wp, pool_scale, rs_token)

    me = (4 * lax.axis_index("x") + 2 * lax.axis_index("y") + lax.axis_index("c")).astype(jnp.int32)
    pad8 = lambda r: jnp.pad(r, ((0, 8 - r.shape[0]), (0, 0)))
    loss_rows = jnp.concatenate([dbf, jnp.zeros((6, 128), f32), loss_v[0:1, :]], axis=0)
    small = jnp.concatenate([dg1.reshape(8, 128), dg2.reshape(8, 128), dgf.reshape(8, 128), pad8(dscale.reshape(4, 128)),
                             loss_rows, dwp.reshape(512, 128)], axis=0)
    small_land = lax.dynamic_update_slice(lax.empty((8, SMALL_ROWS, 128), f32), small[None], (me, 0, 0))
    tail_send, tail_recv, tail_thru, tail_lands_thru, tail_token = _split_start(
        "tail_start", [dwin, small], [lax.empty((3,) + dwin.shape[1:], bf16), small_land], 10, _tail_plan)
    first_thru, first_recv = _split_wait("reduce_scatter_wait", rs_send, rs_recv, first_thru, first_lands_thru, tail_token,
                                         _scatter_plan)
    ws = [w_in[0], w_out[0], tr(w_gate), tr(w_up), w_down[0]]
    ms = [m_w_in[0], m_w_out[0], tr(m_w_gate), tr(m_w_up), m_w_down[0]]
    vs = [v_w_in[0], v_w_out[0], tr(v_w_gate), tr(v_w_up), v_w_down[0]]
    partial = [_sum4(r, g, mine1, f"sum4_{i + 1}") for i, (r, g) in enumerate(zip(first_recv, first_thru))]
    other = _swap_with_sibling(partial, "swap_first")
    big = [_adamw_shard(ws[i + 1], ms[i + 1], vs[i + 1], partial[i], other[i], f"adamw_{i + 1}") for i in range(4)]
    (dwin_thru, _), (in_recv_land, small_all) = _split_wait("tail_wait", tail_send, tail_recv, tail_thru, tail_lands_thru,
                                                            big[3][0], _tail_plan)
    partial_in = _sum4(in_recv_land, dwin_thru, mine1, "sum4_0")
    (other_in,) = _swap_with_sibling([partial_in], "swap_in")
    big = [_adamw_shard(ws[0], ms[0], vs[0], partial_in, other_in, "adamw_0")] + big

    sm = _adamw_small(_pack_small(norm1_g, norm2_g, final_g, pool_scale, b_forget, w_pool),
                      _pack_small(m_norm1_g, m_norm2_g, m_final_g, m_pool_scale, m_b_forget, m_w_pool),
                      _pack_small(v_norm1_g, v_norm2_g, v_final_g, v_pool_scale, v_b_forget, v_w_pool), small_all)

    loss = sm[0][39, 0]
    order =["norm1_g", "w_in", "b_forget", "w_pool", "pool_scale", "w_out", "norm2_g", "w_gate", "w_up", "w_down", "final_g"]
    big_idx = {"w_in": 0, "w_out": 1, "w_gate": 2, "w_up": 3, "w_down": 4}
    outs = [loss, dx[None]]
    for kind in range(4):
        small_k = _unpack_small(sm[kind])
        for name in order:
            if name in ("w_gate", "w_up"):
                outs.append(jnp.transpose(big[big_idx[name]][kind])[None])
            elif name in big_idx:
                outs.append(big[big_idx[name]][kind][None])
            else:
                outs.append(small_k[name])
    return tuple(outs)
```

```python
import functools

import jax
import jax.numpy as jnp
import numpy as np
from jax import lax
from jax.experimental import pallas as pl
from jax.experimental.pallas import tpu as pltpu

f32 = jnp.float32
bf16 = jnp.bfloat16

T = 4096
D = 1024
NSH = 4
IN_W = 2056
IN_S = IN_W // NSH
AW = 512
PAIRS = 4
FF = 2816
FS = FF // NSH
WINDOWS = (2, 4, 8, 16)
HALO = 16
EPS = 1e-6
NEG = -1e30
LR, B1, B2, AEPS, WD, STEP = 0.001, 0.9, 0.999, 1e-08, 0.01, 10
SMALL_ROWS = 552

NT = (((1,), (1,)), ((), ()))
TN = (((0,), (0,)), ((), ()))

MESH = pl.DeviceIdType.MESH


def _cp(*sem):
    return pltpu.CompilerParams(dimension_semantics=sem)


def _full(shape):
    n = len(shape)
    return pl.BlockSpec(shape, lambda *_: (0,) * n)


def _rms_inproj(x, g1, wm, wf, dep):
    tm = 512

    def body(x_ref, g_ref, wm_ref, wf_ref, dep_ref, h_ref, qkv_ref, u_ref, fl_ref):
        xv = x_ref[...]
        r = lax.rsqrt(jnp.mean(xv * xv, axis=-1, keepdims=True) + EPS)
        h = (xv * r * g_ref[...]).astype(bf16)
        h_ref[...] = h
        qkv_ref[...] = jnp.dot(h, wm_ref[:, 0:3 * AW], preferred_element_type=f32).astype(bf16)
        u_ref[...] = jnp.dot(h, wm_ref[:, 3 * AW:4 * AW], preferred_element_type=f32)
        fl_ref[...] = jnp.dot(h, wf_ref[...], preferred_element_type=f32)

    return pl.pallas_call(
        body, name="rms_inproj", grid=(T // tm,),
        in_specs=[pl.BlockSpec((tm, D), lambda i: (i, 0)), _full((1, D)), _full((D, 4 * AW)), _full((D, 128)),
                  _full((8, 128))],
        out_specs=[pl.BlockSpec((tm, D), lambda i: (i, 0)), pl.BlockSpec((tm, 3 * AW), lambda i: (i, 0)),
                   pl.BlockSpec((tm, AW), lambda i: (i, 0)), pl.BlockSpec((tm, 128), lambda i: (i, 0))],
        out_shape=[jax.ShapeDtypeStruct((T, D), bf16), jax.ShapeDtypeStruct((T, 3 * AW), bf16),
                   jax.ShapeDtypeStruct((T, AW), f32), jax.ShapeDtypeStruct((T, 128), f32)],
        compiler_params=_cp("parallel"),
    )(x, g1, wm, wf, dep)


def _log_sigmoid(z):
    return jnp.minimum(z, 0.0) - jnp.log(1.0 + jnp.exp(-jnp.abs(z)))


def _fox_cumsum(fl, bfp):
    nb = T // 128

    def body(fl_ref, b_ref, qa_ref, ka_ref, carry):
        i = pl.program_id(0)

        @pl.when(i == 0)
        def _():
            carry[...] = jnp.zeros_like(carry)

        lf = _log_sigmoid(fl_ref[...] + b_ref[...])
        r = lax.broadcasted_iota(jnp.int32, (128, 128), 0)
        cc = lax.broadcasted_iota(jnp.int32, (128, 128), 1)
        ltri = (cc <= r).astype(f32)
        cb = jnp.dot(ltri, lf, precision=lax.Precision.HIGHEST, preferred_element_type=f32) + carry[0:1, :]
        carry[...] = jnp.broadcast_to(cb[127:128, :], (8, 128))
        hi = cb.astype(bf16)
        r1 = cb - hi.astype(f32)
        mid = r1.astype(bf16)
        lo = (r1 - mid.astype(f32)).astype(bf16)
        head = lax.broadcasted_iota(jnp.int32, (128, AW), 0)
        col = lax.broadcasted_iota(jnp.int32, (128, AW), 1)
        base = 128 * (head >> 1) + 64 * (1 - (head & 1))
        place = lambda off: jnp.logical_and(col == base + off, head < 8).astype(bf16)
        mm = lambda a, off: jnp.dot(a, place(off), preferred_element_type=f32)
        cq = mm(hi, 0) + mm(mid, 1) + mm(lo, 2)
        ck = mm(hi, 3) + mm(mid, 4) + mm(lo, 5)
        within = jnp.bitwise_and(lax.broadcasted_iota(jnp.int32, (128, AW), 1), 63)
        qa_ref[...] = jnp.where(jnp.logical_and(within >= 3, within <= 5), 1.0, cq).astype(bf16)
        ka_ref[...] = jnp.where(within <= 2, 1.0, -ck).astype(bf16)

    return pl.pallas_call(
        body, name="fox_cumsum", grid=(nb,),
        in_specs=[pl.BlockSpec((128, 128), lambda i: (i, 0)), _full((1, 128))],
        out_specs=[pl.BlockSpec((128, AW), lambda i: (i, 0)), pl.BlockSpec((128, AW), lambda i: (i, 0))],
        out_shape=[jax.ShapeDtypeStruct((T, AW), bf16), jax.ShapeDtypeStruct((T, AW), bf16)],
        scratch_shapes=[pltpu.VMEM((8, 128), f32)],
        compiler_params=_cp("arbitrary"),
    )(fl, bfp)


ATT_T = 512


def _causal_steps(key_major):
    n = T // ATT_T
    if key_major:
        pairs = [(i, j) for j in range(n) for i in range(j, n)]
    else:
        pairs = [(i, j) for i in range(n) for j in range(i + 1)]
    it = np.array([p[0] for p in pairs], np.int32)
    jt = np.array([p[1] for p in pairs], np.int32)
    return jnp.asarray(it), jnp.asarray(jt)


def _attn_fwd(qkv, qaug, kaug):
    tq = tk = ATT_T
    it, jt = _causal_steps(False)
    nsteps = it.shape[0]

    def body(it_ref, jt_ref, q_ref, k_ref, v_ref, qa_ref, ka_ref, o_ref, lse_ref, m_sc, l_sc, acc_sc):
        t = pl.program_id(1)
        i = it_ref[t]
        j = jt_ref[t]

        @pl.when(j == 0)
        def _():
            m_sc[...] = jnp.full_like(m_sc, NEG)
            l_sc[...] = jnp.zeros_like(l_sc)
            acc_sc[...] = jnp.zeros_like(acc_sc)

        lane = lax.broadcasted_iota(jnp.int32, (tq, 128), 1)

        def step(on_diagonal):
            q = q_ref[...] * 0.125
            k = k_ref[...]
            v = v_ref[...]
            qa = qa_ref[...]
            ka = ka_ref[...]
            for e in range(2):
                hm = (lane >= 64) if e else (lane < 64)
                s = lax.dot_general(jnp.where(hm, q, qa), jnp.where(hm, k, ka), NT, preferred_element_type=f32)
                if on_diagonal:
                    row = lax.broadcasted_iota(jnp.int32, (tq, tk), 0)
                    col = lax.broadcasted_iota(jnp.int32, (tq, tk), 1)
                    s = jnp.where(col <= row, s, NEG)
                m_prev = m_sc[e]
                m_new = jnp.maximum(m_prev, jnp.max(s, axis=1, keepdims=True))
                alpha = jnp.exp(m_prev - m_new)
                p = jnp.exp(s - m_new)
                l_sc[e] = alpha * l_sc[e] + jnp.sum(p, axis=1, keepdims=True)
                acc_sc[e] = alpha * acc_sc[e] + jnp.dot(p.astype(bf16), v, preferred_element_type=f32)
                m_sc[e] = m_new

        @pl.when(j < i)
        def _():
            step(False)

        @pl.when(j == i)
        def _():
            step(True)
            o0 = acc_sc[0] / l_sc[0]
            o1 = acc_sc[1] / l_sc[1]
            o_ref[...] = jnp.where(lane < 64, o0, o1).astype(bf16)
            lse_ref[...] = jnp.where(lane < 64, m_sc[0] + jnp.log(l_sc[0]), m_sc[1] + jnp.log(l_sc[1]))

    qmap = lambda p, t, it, jt: (it[t], p)
    kmap = lambda p, t, it, jt: (jt[t], p)
    grid_spec = pltpu.PrefetchScalarGridSpec(
        num_scalar_prefetch=2, grid=(PAIRS, nsteps),
        in_specs=[pl.BlockSpec((tq, 128), qmap),
                  pl.BlockSpec((tk, 128), lambda p, t, it, jt: (jt[t], PAIRS + p)),
                  pl.BlockSpec((tk, 128), lambda p, t, it, jt: (jt[t], 2 * PAIRS + p)),
                  pl.BlockSpec((tq, 128), qmap), pl.BlockSpec((tk, 128), kmap)],
        out_specs=[pl.BlockSpec((tq, 128), qmap),
                   pl.BlockSpec((None, tq, 128), lambda p, t, it, jt: (p, it[t], 0))],
        scratch_shapes=[pltpu.VMEM((2, tq, 1), f32), pltpu.VMEM((2, tq, 1), f32), pltpu.VMEM((2, tq, 128), f32)],
    )
    return pl.pallas_call(
        body, name="fox_attn_fwd", grid_spec=grid_spec,
        out_shape=[jax.ShapeDtypeStruct((T, AW), bf16), jax.ShapeDtypeStruct((PAIRS, T, 128), f32)],
        compiler_params=_cp("parallel", "arbitrary"),
    )(it, jt, qkv, qkv, qkv, qaug, kaug)


def _pool_fwd(u, wp, scale):
    tm = 512

    def body(u_ref, wp_ref, sc_ref, pooled_ref, pool_ref, ext):
        i = pl.program_id(0)

        @pl.when(i == 0)
        def _():
            ext[0:HALO, :] = jnp.zeros((HALO, AW), f32)

        uv = u_ref[...]
        ext[HALO:HALO + tm, :] = uv
        t_idx = i * tm + lax.broadcasted_iota(jnp.int32, (tm, 1), 0)
        for g, w in enumerate(WINDOWS):
            lo, hi = 128 * g, 128 * (g + 1)
            ug = uv[:, lo:hi]
            acc = ug
            for d in range(1, w):
                acc = acc + ext[HALO - d:HALO - d + tm, lo:hi]
            cnt = jnp.minimum(t_idx + 1, w).astype(f32)
            pb = (acc / cnt - ug).astype(bf16)
            pooled_ref[:, lo:hi] = pb
            mixed = jnp.dot(pb, wp_ref[g], preferred_element_type=f32)
            pool_ref[:, lo:hi] = (mixed * sc_ref[:, lo:hi]).astype(bf16)
        ext[0:HALO, :] = uv[tm - HALO:tm, :]

    return pl.pallas_call(
        body, name="pool_fwd", grid=(T // tm,),
        in_specs=[pl.BlockSpec((tm, AW), lambda i: (i, 0)), _full((4, 128, 128)), _full((1, AW))],
        out_specs=[pl.BlockSpec((tm, AW), lambda i: (i, 0)), pl.BlockSpec((tm, AW), lambda i: (i, 0))],
        out_shape=[jax.ShapeDtypeStruct((T, AW), bf16), jax.ShapeDtypeStruct((T, AW), bf16)],
        scratch_shapes=[pltpu.VMEM((tm + HALO, AW), f32)],
        compiler_params=_cp("arbitrary"),
    )(u, wp, scale)


def _outproj(x, attn, pool, wo, g2):
    tm = 512

    def body(x_ref, a_ref, p_ref, wo_ref, g_ref, x1_ref, h2_ref):
        x1 = x_ref[...] + jnp.dot(a_ref[...], wo_ref[0:AW, :], preferred_element_type=f32)
        x1 = x1 + jnp.dot(p_ref[...], wo_ref[AW:2 * AW, :], preferred_element_type=f32)
        x1_ref[...] = x1
        r = lax.rsqrt(jnp.mean(x1 * x1, axis=-1, keepdims=True) + EPS)
        h2_ref[...] = (x1 * r * g_ref[...]).astype(bf16)

    return pl.pallas_call(
        body, name="outproj", grid=(T // tm,),
        in_specs=[pl.BlockSpec((tm, D), lambda i: (i, 0)), pl.BlockSpec((tm, AW), lambda i: (i, 0)),
                  pl.BlockSpec((tm, AW), lambda i: (i, 0)), _full((D, D)), _full((1, D))],
        out_specs=[pl.BlockSpec((tm, D), lambda i: (i, 0)), pl.BlockSpec((tm, D), lambda i: (i, 0))],
        out_shape=[jax.ShapeDtypeStruct((T, D), f32), jax.ShapeDtypeStruct((T, D), bf16)],
        compiler_params=_cp("parallel"),
    )(x, attn, pool, wo, g2)


def _mlp_fwd(h2, x1, wg, wu, wd):
    tm = 512

    def body(h_ref, x1_ref, wg_ref, wu_ref, wd_ref, x2_ref, gate_ref, up_ref):
        s = pl.program_id(1)
        h = h_ref[...]
        gate = lax.dot_general(h, wg_ref[...], NT, preferred_element_type=f32)
        up = lax.dot_general(h, wu_ref[...], NT, preferred_element_type=f32)
        gate_ref[...] = gate
        up_ref[...] = up
        a = (gate * jax.nn.sigmoid(gate) * up).astype(bf16)
        part = jnp.dot(a, wd_ref[...], preferred_element_type=f32)

        @pl.when(s == 0)
        def _():
            x2_ref[...] = x1_ref[...] + part

        @pl.when(s > 0)
        def _():
            x2_ref[...] += part

    return pl.pallas_call(
        body, name="mlp_fwd", grid=(T // tm, NSH),
        in_specs=[pl.BlockSpec((tm, D), lambda i, s: (i, 0)), pl.BlockSpec((tm, D), lambda i, s: (i, 0)),
                  pl.BlockSpec((None, FS, D), lambda i, s: (s, 0, 0)), pl.BlockSpec((None, FS, D), lambda i, s: (s, 0, 0)),
                  pl.BlockSpec((None, FS, D), lambda i, s: (s, 0, 0))],
        out_specs=[pl.BlockSpec((tm, D), lambda i, s: (i, 0)), pl.BlockSpec((None, tm, FS), lambda i, s: (s, i, 0)),
                   pl.BlockSpec((None, tm, FS), lambda i, s: (s, i, 0))],
        out_shape=[jax.ShapeDtypeStruct((T, D), f32), jax.ShapeDtypeStruct((NSH, T, FS), f32),
                   jax.ShapeDtypeStruct((NSH, T, FS), f32)],
        compiler_params=_cp("parallel", "arbitrary"),
    )(h2, x1, wg, wu, wd)


def _final_loss(x2, tgt, gf):
    tm = 512

    def body(x_ref, t_ref, g_ref, loss_ref, dg_ref, dx_ref, dxb_ref):
        i = pl.program_id(0)

        @pl.when(i == 0)
        def _():
            loss_ref[...] = jnp.zeros_like(loss_ref)
            dg_ref[...] = jnp.zeros_like(dg_ref)

        xv = x_ref[...]
        g = g_ref[...]
        r = lax.rsqrt(jnp.mean(xv * xv, axis=-1, keepdims=True) + EPS)
        xhat = xv * r
        e = xhat * g - t_ref[...]
        loss_ref[...] += 0.5 * jnp.sum(jnp.mean(e * e, axis=-1, keepdims=True))
        dy = e * (1.0 / D)
        dg_ref[...] += jnp.sum(dy * xhat, axis=0, keepdims=True)
        z = dy * g
        dx = r * (z - xhat * jnp.mean(z * xhat, axis=-1, keepdims=True))
        dx_ref[...] = dx
        dxb_ref[...] = dx.astype(bf16)

    return pl.pallas_call(
        body, name="final_loss", grid=(T // tm,),
        in_specs=[pl.BlockSpec((tm, D), lambda i: (i, 0)), pl.BlockSpec((tm, D), lambda i: (i, 0)), _full((1, D))],
        out_specs=[_full((8, 128)), _full((1, D)), pl.BlockSpec((tm, D), lambda i: (i, 0)),
                   pl.BlockSpec((tm, D), lambda i: (i, 0))],
        out_shape=[jax.ShapeDtypeStruct((8, 128), f32), jax.ShapeDtypeStruct((1, D), f32),
                   jax.ShapeDtypeStruct((T, D), f32), jax.ShapeDtypeStruct((T, D), bf16)],
        compiler_params=_cp("arbitrary"),
    )(x2, tgt, gf)


def _mlp_bwd(dx2b, dx2, gate, up, wg, wu, wd, x1, g2):
    tm = 512

    def body(dxb_ref, dx_ref, gate_ref, up_ref, wg_ref, wu_ref, wd_ref, x1_ref, g_ref,
             a_ref, dg_ref, du_ref, dx1_ref, dx1b_ref, dn_ref, acc):
        i = pl.program_id(0)
        s = pl.program_id(1)

        @pl.when(jnp.logical_and(i == 0, s == 0))
        def _():
            dn_ref[...] = jnp.zeros_like(dn_ref)

        da = lax.dot_general(dxb_ref[...], wd_ref[...], NT, preferred_element_type=f32)
        gate = gate_ref[...]
        upv = up_ref[...]
        sg = jax.nn.sigmoid(gate)
        silu = gate * sg
        a_ref[...] = (silu * upv).astype(bf16)
        dgate = (da * upv * (sg * (1.0 + gate * (1.0 - sg)))).astype(bf16)
        dup = (da * silu).astype(bf16)
        dg_ref[...] = dgate
        du_ref[...] = dup
        part = jnp.dot(dgate, wg_ref[...], preferred_element_type=f32)
        part = part + jnp.dot(dup, wu_ref[...], preferred_element_type=f32)

        @pl.when(s == 0)
        def _():
            acc[...] = part

        @pl.when(s > 0)
        def _():
            acc[...] += part

        @pl.when(s == NSH - 1)
        def _():
            xv = x1_ref[...]
            r = lax.rsqrt(jnp.mean(xv * xv, axis=-1, keepdims=True) + EPS)
            xhat = xv * r
            dh = acc[...]
            dn_ref[...] += jnp.sum(dh * xhat, axis=0, keepdims=True)
            z = dh * g_ref[...]
            dx1 = dx_ref[...] + r * (z - xhat * jnp.mean(z * xhat, axis=-1, keepdims=True))
            dx1_ref[...] = dx1
            dx1b_ref[...] = dx1.astype(bf16)

    row = lambda i, s: (i, 0)
    sl = lambda i, s: (s, i, 0)
    wsl = lambda i, s: (s, 0, 0)
    return pl.pallas_call(
        body, name="mlp_bwd", grid=(T // tm, NSH),
        in_specs=[pl.BlockSpec((tm, D), row), pl.BlockSpec((tm, D), row),
                  pl.BlockSpec((None, tm, FS), sl), pl.BlockSpec((None, tm, FS), sl),
                  pl.BlockSpec((None, FS, D), wsl), pl.BlockSpec((None, FS, D), wsl), pl.BlockSpec((None, FS, D), wsl),
                  pl.BlockSpec((tm, D), row), pl.BlockSpec((1, D), lambda i, s: (0, 0))],
        out_specs=[pl.BlockSpec((None, tm, FS), sl), pl.BlockSpec((None, tm, FS), sl), pl.BlockSpec((None, tm, FS), sl),
                   pl.BlockSpec((tm, D), row), pl.BlockSpec((tm, D), row), pl.BlockSpec((1, D), lambda i, s: (0, 0))],
        out_shape=[jax.ShapeDtypeStruct((NSH, T, FS), bf16)] * 3
        + [jax.ShapeDtypeStruct((T, D), f32), jax.ShapeDtypeStruct((T, D), bf16), jax.ShapeDtypeStruct((1, D), f32)],
        scratch_shapes=[pltpu.VMEM((tm, D), f32)],
        compiler_params=_cp("arbitrary", "arbitrary"),
    )(dx2b, dx2, gate, up, wg, wu, wd, x1, g2)


def _mm_tn(a, bs, name, a_sharded=False, b_sharded=False, tk=512, out_dtype=bf16):
    nb = len(bs)
    sh = NSH if (a_sharded or b_sharded) else 1
    m = a.shape[-1]
    nk = T // tk

    def body(a_ref, *refs):
        kk = pl.program_id(1)
        av = a_ref[...]
        for b_ref, o_ref, acc in zip(refs[:nb], refs[nb:2 * nb], refs[2 * nb:]):
            upd = lax.dot_general(av, b_ref[...], TN, preferred_element_type=f32)

            @pl.when(kk == 0)
            def _():
                acc[...] = upd

            @pl.when(kk > 0)
            def _():
                acc[...] += upd

            @pl.when(kk == nk - 1)
            def _():
                o_ref[...] = acc[...].astype(out_dtype)

    a_spec = (pl.BlockSpec((None, tk, m), lambda s, k: (s, k, 0)) if a_sharded
              else pl.BlockSpec((tk, m), lambda s, k: (k, 0)))
    b_specs, o_specs, o_shapes, scratch = [], [], [], []
    for b in bs:
        n = b.shape[-1]
        b_specs.append(pl.BlockSpec((None, tk, n), lambda s, k: (s, k, 0)) if b_sharded
                       else pl.BlockSpec((tk, n), lambda s, k: (k, 0)))
        scratch.append(pltpu.VMEM((m, n), f32))
        if sh > 1:
            o_specs.append(pl.BlockSpec((None, m, n), lambda s, k: (s, 0, 0)))
            o_shapes.append(jax.ShapeDtypeStruct((sh, m, n), out_dtype))
        else:
            o_specs.append(pl.BlockSpec((m, n), lambda s, k: (0, 0)))
            o_shapes.append(jax.ShapeDtypeStruct((m, n), out_dtype))
    return pl.pallas_call(
        body, name=name, grid=(sh, nk), in_specs=[a_spec] + b_specs, out_specs=o_specs, out_shape=o_shapes,
        scratch_shapes=scratch, compiler_params=_cp("arbitrary", "arbitrary"),
    )(a, *bs)


def _outproj_bwd(dx1b, wo):
    tm = 512

    def body(dx_ref, wo_ref, da_ref, dp_ref):
        dx = dx_ref[...]
        da_ref[...] = lax.dot_general(dx, wo_ref[0:AW, :], NT, preferred_element_type=f32).astype(bf16)
        dp_ref[...] = lax.dot_general(dx, wo_ref[AW:2 * AW, :], NT, preferred_element_type=f32)

    return pl.pallas_call(
        body, name="outproj_bwd", grid=(T // tm,),
        in_specs=[pl.BlockSpec((tm, D), lambda i: (i, 0)), _full((D, D))],
        out_specs=[pl.BlockSpec((tm, AW), lambda i: (i, 0)), pl.BlockSpec((tm, AW), lambda i: (i, 0))],
        out_shape=[jax.ShapeDtypeStruct((T, AW), bf16), jax.ShapeDtypeStruct((T, AW), f32)],
        compiler_params=_cp("parallel"),
    )(dx1b, wo)


def _pool_bwd(dpool, pooled, wp, scale, dep):
    tm = 512
    n = T // tm

    def body(dp_ref, pb_ref, wp_ref, sc_ref, dep_ref, du_ref, dsc_ref, dwp_ref, ext):
        i = pl.program_id(0)

        @pl.when(i == 0)
        def _():
            ext[tm:tm + HALO, :] = jnp.zeros((HALO, AW), f32)
            dsc_ref[...] = jnp.zeros_like(dsc_ref)
            dwp_ref[...] = jnp.zeros_like(dwp_ref)

        t_idx = (n - 1 - i) * tm + lax.broadcasted_iota(jnp.int32, (tm, 1), 0)
        for g, w in enumerate(WINDOWS):
            lo, hi = 128 * g, 128 * (g + 1)
            pb = pb_ref[:, lo:hi]
            mixed = jnp.dot(pb, wp_ref[g], preferred_element_type=f32)
            dpo = dp_ref[:, lo:hi]
            dsc_ref[:, lo:hi] += jnp.sum(dpo * mixed, axis=0, keepdims=True)
            dmr = (dpo * sc_ref[:, lo:hi]).astype(bf16)
            dwp_ref[g] += lax.dot_general(pb, dmr, TN, preferred_element_type=f32)
            dpl = lax.dot_general(dmr, wp_ref[g], NT, preferred_element_type=f32)
            cnt = jnp.minimum(t_idx + 1, w).astype(f32)
            dpn = dpl / cnt
            ext[0:tm, lo:hi] = dpn
            acc = dpn
            for d in range(1, w):
                acc = acc + ext[d:d + tm, lo:hi]
            du_ref[:, lo:hi] = (acc - dpl).astype(bf16)
        ext[tm:tm + HALO, :] = ext[0:HALO, :]

    rev = lambda i: (n - 1 - i, 0)
    return pl.pallas_call(
        body, name="pool_bwd", grid=(n,),
        in_specs=[pl.BlockSpec((tm, AW), rev), pl.BlockSpec((tm, AW), rev), _full((4, 128, 128)), _full((1, AW)),
                  _full((8, 128))],
        out_specs=[pl.BlockSpec((tm, AW), rev), _full((1, AW)), _full((4, 128, 128))],
        out_shape=[jax.ShapeDtypeStruct((T, AW), bf16), jax.ShapeDtypeStruct((1, AW), f32),
                   jax.ShapeDtypeStruct((4, 128, 128), f32)],
        scratch_shapes=[pltpu.VMEM((tm + HALO, AW), f32)],
        compiler_params=_cp("arbitrary"),
    )(dpool, pooled, wp, scale, dep)


def _attn_bwd(qkv, qaug, kaug, attn, dattn, lse, dep):
    tq = tk = ATT_T
    n = T // tq
    it, jt = _causal_steps(True)
    nsteps = it.shape[0]

    def body(it_ref, jt_ref, q_ref, k_ref, v_ref, qa_ref, ka_ref, o_ref, do_ref, lse_ref, dep_ref,
             dq_ref, dqs_ref, dk_ref, dks_ref, dv_ref, dq_acc, dk_acc, dv_acc):
        t = pl.program_id(1)
        i = it_ref[t]
        j = jt_ref[t]

        @pl.when(t == 0)
        def _():
            dq_acc[...] = jnp.zeros_like(dq_acc)

        @pl.when(i == j)
        def _():
            dk_acc[...] = jnp.zeros_like(dk_acc)
            dv_acc[...] = jnp.zeros_like(dv_acc)

        lane = lax.broadcasted_iota(jnp.int32, (tq, 128), 1)

        def step(on_diagonal):
            q = q_ref[...] * 0.125
            k = k_ref[...]
            v = v_ref[...]
            qa = qa_ref[...]
            ka = ka_ref[...]
            do = do_ref[...]
            dd = do.astype(f32) * o_ref[...].astype(f32)
            r0 = pl.multiple_of(i * tq, tq)
            for e in range(2):
                hm = (lane >= 64) if e else (lane < 64)
                qe = jnp.where(hm, q, qa)
                ke = jnp.where(hm, k, ka)
                doe = jnp.where(hm, do, jnp.zeros_like(do))
                delta = jnp.sum(jnp.where(hm, dd, 0.0), axis=1, keepdims=True)
                s = lax.dot_general(qe, ke, NT, preferred_element_type=f32) - lse_ref[:, 64 * e:64 * e + 1]
                if on_diagonal:
                    row = lax.broadcasted_iota(jnp.int32, (tq, tk), 0)
                    col = lax.broadcasted_iota(jnp.int32, (tq, tk), 1)
                    s = jnp.where(col <= row, s, NEG)
                p = jnp.exp(s)
                dv_acc[...] += lax.dot_general(p.astype(bf16), doe, TN, preferred_element_type=f32)
                dp = lax.dot_general(doe, v, NT, preferred_element_type=f32)
                dsb = (p * (dp - delta)).astype(bf16)
                dk_acc[e] += lax.dot_general(dsb, qe, TN, preferred_element_type=f32)
                dq_acc[e, pl.ds(r0, tq), :] += jnp.dot(dsb, ke, preferred_element_type=f32)

        @pl.when(i > j)
        def _():
            step(False)

        @pl.when(i == j)
        def _():
            step(True)

        @pl.when(i == n - 1)
        def _():
            dk_ref[...] = jnp.where(lane < 64, dk_acc[0], dk_acc[1]).astype(bf16)
            dks_ref[...] = jnp.where(lane < 64, dk_acc[1], dk_acc[0])
            dv_ref[...] = dv_acc[...].astype(bf16)

        @pl.when(t == nsteps - 1)
        def _():
            lane_t = lax.broadcasted_iota(jnp.int32, (T, 128), 1)
            dq_ref[...] = (jnp.where(lane_t < 64, dq_acc[0], dq_acc[1]) * 0.125).astype(bf16)
            dqs_ref[...] = jnp.where(lane_t < 64, dq_acc[1], dq_acc[0])

    qmap = lambda p, t, it, jt: (it[t], p)
    grid_spec = pltpu.PrefetchScalarGridSpec(
        num_scalar_prefetch=2, grid=(PAIRS, nsteps),
        in_specs=[pl.BlockSpec((tq, 128), qmap),
                  pl.BlockSpec((tk, 128), lambda p, t, it, jt: (jt[t], PAIRS + p)),
                  pl.BlockSpec((tk, 128), lambda p, t, it, jt: (jt[t], 2 * PAIRS + p)),
                  pl.BlockSpec((tq, 128), qmap), pl.BlockSpec((tk, 128), lambda p, t, it, jt: (jt[t], p)),
                  pl.BlockSpec((tq, 128), qmap), pl.BlockSpec((tq, 128), qmap),
                  pl.BlockSpec((None, tq, 128), lambda p, t, it, jt: (p, it[t], 0)),
                  pl.BlockSpec((8, 128), lambda p, t, it, jt: (0, 0))],
        out_specs=[pl.BlockSpec((T, 128), lambda p, t, it, jt: (0, p)),
                   pl.BlockSpec((None, T, 128), lambda p, t, it, jt: (p, 0, 0)),
                   pl.BlockSpec((tk, 128), lambda p, t, it, jt: (jt[t], p)),
                   pl.BlockSpec((None, tk, 128), lambda p, t, it, jt: (p, jt[t], 0)),
                   pl.BlockSpec((tk, 128), lambda p, t, it, jt: (jt[t], p))],
        scratch_shapes=[pltpu.VMEM((2, T, 128), f32), pltpu.VMEM((2, tk, 128), f32), pltpu.VMEM((tk, 128), f32)],
    )
    return pl.pallas_call(
        body, name="fox_attn_bwd", grid_spec=grid_spec,
        out_shape=[jax.ShapeDtypeStruct((T, AW), bf16), jax.ShapeDtypeStruct((PAIRS, T, 128), f32),
                   jax.ShapeDtypeStruct((T, AW), bf16), jax.ShapeDtypeStruct((PAIRS, T, 128), f32),
                   jax.ShapeDtypeStruct((T, AW), bf16)],
        compiler_params=_cp("parallel", "arbitrary"),
    )(it, jt, qkv, qkv, qkv, qaug, kaug, attn, dattn, lse, dep)


def _fox_cumsum_bwd(dqs, dks, fl, bfp):
    nb = T // 128
    hp = lax.Precision.HIGHEST

    def body(dqs_ref, dks_ref, fl_ref, b_ref, df_ref, db_ref, carry):
        i = pl.program_id(0)

        @pl.when(i == 0)
        def _():
            carry[...] = jnp.zeros_like(carry)
            db_ref[...] = jnp.zeros_like(db_ref)

        r = lax.broadcasted_iota(jnp.int32, (128, 128), 0)
        cc = lax.broadcasted_iota(jnp.int32, (128, 128), 1)
        pick = lambda even_lane, odd_lane, p: jnp.logical_or(
            jnp.logical_and(r == even_lane, cc == 2 * p), jnp.logical_and(r == odd_lane, cc == 2 * p + 1)).astype(f32)
        dc = jnp.zeros((128, 128), f32)
        for p in range(PAIRS):
            dc = dc + jnp.dot(dqs_ref[p], pick(64, 0, p), precision=hp, preferred_element_type=f32)
            dc = dc - jnp.dot(dks_ref[p], pick(67, 3, p), precision=hp, preferred_element_type=f32)
        utri = (cc >= r).astype(f32)
        dl = jnp.dot(utri, dc, precision=hp, preferred_element_type=f32) + carry[0:1, :]
        carry[...] = jnp.broadcast_to(dl[0:1, :], (8, 128))
        z = fl_ref[...] + b_ref[...]
        df = dl * jax.nn.sigmoid(-z)
        df_ref[...] = df.astype(bf16)
        db_ref[...] += jnp.sum(df, axis=0, keepdims=True)

    rev = lambda i: (nb - 1 - i, 0)
    return pl.pallas_call(
        body, name="fox_cumsum_bwd", grid=(nb,),
        in_specs=[pl.BlockSpec((PAIRS, 128, 128), lambda i: (0, nb - 1 - i, 0)),
                  pl.BlockSpec((PAIRS, 128, 128), lambda i: (0, nb - 1 - i, 0)),
                  pl.BlockSpec((128, 128), rev), _full((1, 128))],
        out_specs=[pl.BlockSpec((128, 128), rev), _full((1, 128))],
        out_shape=[jax.ShapeDtypeStruct((T, 128), bf16), jax.ShapeDtypeStruct((1, 128), f32)],
        scratch_shapes=[pltpu.VMEM((8, 128), f32)],
        compiler_params=_cp("arbitrary"),
    )(dqs, dks, fl, bfp)


def _inproj_bwd(dq, dk, dv, du, df, wm, wf, x, dx1, g1):
    tm = 512

    def body(dq_ref, dk_ref, dv_ref, du_ref, df_ref, wm_ref, wf_ref, x_ref, dx1_ref, g_ref, dx_ref, dn_ref):
        i = pl.program_id(0)

        @pl.when(i == 0)
        def _():
            dn_ref[...] = jnp.zeros_like(dn_ref)

        dh = lax.dot_general(dq_ref[...], wm_ref[:, 0:AW], NT, preferred_element_type=f32)
        dh = dh + lax.dot_general(dk_ref[...], wm_ref[:, AW:2 * AW], NT, preferred_element_type=f32)
        dh = dh + lax.dot_general(dv_ref[...], wm_ref[:, 2 * AW:3 * AW], NT, preferred_element_type=f32)
        dh = dh + lax.dot_general(du_ref[...], wm_ref[:, 3 * AW:4 * AW], NT, preferred_element_type=f32)
        dh = dh + lax.dot_general(df_ref[...], wf_ref[...], NT, preferred_element_type=f32)
        xv = x_ref[...]
        r = lax.rsqrt(jnp.mean(xv * xv, axis=-1, keepdims=True) + EPS)
        xhat = xv * r
        dn_ref[...] += jnp.sum(dh * xhat, axis=0, keepdims=True)
        z = dh * g_ref[...]
        dx_ref[...] = dx1_ref[...] + r * (z - xhat * jnp.mean(z * xhat, axis=-1, keepdims=True))

    row = lambda i: (i, 0)
    return pl.pallas_call(
        body, name="inproj_bwd", grid=(T // tm,),
        in_specs=[pl.BlockSpec((tm, AW), row)] * 4 + [pl.BlockSpec((tm, 128), row), _full((D, 4 * AW)), _full((D, 128)),
                                                       pl.BlockSpec((tm, D), row), pl.BlockSpec((tm, D), row), _full((1, D))],
        out_specs=[pl.BlockSpec((tm, D), row), _full((1, D))],
        out_shape=[jax.ShapeDtypeStruct((T, D), f32), jax.ShapeDtypeStruct((1, D), f32)],
        compiler_params=_cp("arbitrary"),
    )(dq, dk, dv, du, df, wm, wf, x, dx1, g1)


def _adamw_math(w, g, m, v):
    m = B1 * m + (1.0 - B1) * g
    v = B2 * v + (1.0 - B2) * (g * g)
    m_hat = m / (1.0 - B1 ** STEP)
    v_hat = v / (1.0 - B2 ** STEP)
    delta = -LR * (m_hat / (jnp.sqrt(v_hat) + AEPS) + WD * w)
    return delta, m, v


def _adamw_shard(w, m, v, p_mine, p_other, name):
    rows, cols = w.shape
    tr = 256 if rows % 256 == 0 else 176

    def body(w_ref, m_ref, v_ref, a_ref, b_ref, g_ref, d_ref, nm_ref, nv_ref):
        g = a_ref[...] + b_ref[...]
        g_ref[...] = g
        d_ref[...], nm_ref[...], nv_ref[...] = _adamw_math(w_ref[...], g, m_ref[...], v_ref[...])

    spec = pl.BlockSpec((tr, cols), lambda i: (i, 0))
    return pl.pallas_call(
        body, name=name, grid=(rows // tr,), in_specs=[spec] * 5, out_specs=[spec] * 4,
        out_shape=[jax.ShapeDtypeStruct((rows, cols), f32)] * 4, compiler_params=_cp("parallel"),
    )(w, m, v, p_mine, p_other)


def _adamw_small(w, m, v, parts):
    def body(w_ref, m_ref, v_ref, p_ref, g_ref, d_ref, nm_ref, nv_ref):
        g = p_ref[0]
        for k in range(1, 8):
            g = g + p_ref[k]
        g_ref[...] = g
        d_ref[...], nm_ref[...], nv_ref[...] = _adamw_math(w_ref[...], g, m_ref[...], v_ref[...])

    return pl.pallas_call(
        body, name="adamw_small", out_shape=[jax.ShapeDtypeStruct((SMALL_ROWS, 128), f32)] * 4,
    )(w, m, v, parts)


def _sum4(recv, g, mine, name):
    _, rows, cols = recv.shape
    tr = 256 if rows % 256 == 0 else 176

    def body(mine_ref, r_ref, g_ref, o_ref):
        o_ref[...] = ((g_ref[...].astype(f32) + r_ref[0].astype(f32))
                      + (r_ref[1].astype(f32) + r_ref[2].astype(f32)))

    grid_spec = pltpu.PrefetchScalarGridSpec(
        num_scalar_prefetch=1, grid=(rows // tr,),
        in_specs=[pl.BlockSpec((3, tr, cols), lambda i, m: (0, i, 0)),
                  pl.BlockSpec((None, tr, cols), lambda i, m: (m[0], i, 0))],
        out_specs=pl.BlockSpec((tr, cols), lambda i, m: (i, 0)))
    return pl.pallas_call(
        body, name=name, grid_spec=grid_spec, out_shape=jax.ShapeDtypeStruct((rows, cols), f32),
        compiler_params=_cp("arbitrary"),
    )(mine, recv, g)


_HBM = pl.BlockSpec(memory_space=pltpu.HBM)
_SEM = pl.BlockSpec(memory_space=pltpu.SEMAPHORE)
_EFFECT = pltpu.SideEffectType.DATAFLOW_SIDE_EFFECTING


def _in_hbm(a):
    return pltpu.with_memory_space_constraint(a, pltpu.HBM)


def _mesh_pos():
    return lax.axis_index("x"), lax.axis_index("y"), lax.axis_index("c")


def _other_chips(x, y):
    return [(1 - x, y), (x, 1 - y), (1 - x, 1 - y)]


def _gather_copy(srcs, lands, send_sems, recv_sems, a, k, slot):
    x, y, c = _mesh_pos()
    cx, cy = _other_chips(x, y)[k]
    return pltpu.make_async_remote_copy(
        src_ref=srcs[a], dst_ref=lands[a].at[slot], send_sem=send_sems.at[3 * a + k], recv_sem=recv_sems.at[3 * a + k],
        device_id=(cx, cy, c), device_id_type=MESH)


def _scatter_copy(srcs, lands, send_sems, recv_sems, a, k):
    x, y, c = _mesh_pos()
    cx, cy = _other_chips(x, y)[k]
    return pltpu.make_async_remote_copy(
        src_ref=srcs[a].at[2 * cx + cy], dst_ref=lands[a].at[k], send_sem=send_sems.at[3 * a + k],
        recv_sem=recv_sems.at[3 * a + k], device_id=(cx, cy, c), device_id_type=MESH)


def _all_gather_w_in(part):
    rows = part.shape[0] // 2

    def body(src, dst, send_sems, recv_sems, loc_sem):
        x, y, c = _mesh_pos()
        mine = 2 * x + y
        chips = _other_chips(x, y)
        half = lambda ref, cc: ref.at[pl.ds(pl.multiple_of(cc * rows, rows), rows), :]

        def over_ici(k, slot):
            cx, cy = chips[k]
            return pltpu.make_async_remote_copy(
                src_ref=half(src, c), dst_ref=half(dst.at[slot], c), send_sem=send_sems.at[k], recv_sem=recv_sems.at[k],
                device_id=(cx, cy, c), device_id_type=MESH)

        def to_sibling(k, cc):
            slot = 2 * chips[k][0] + chips[k][1]
            return pltpu.make_async_remote_copy(
                src_ref=half(dst.at[slot], cc), dst_ref=half(dst.at[slot], cc), send_sem=send_sems.at[3 + k],
                recv_sem=recv_sems.at[3 + k], device_id=(x, y, 1 - c), device_id_type=MESH)

        local = pltpu.make_async_copy(src, dst.at[mine], loc_sem.at[0])
        local.start()
        first = [over_ici(k, mine) for k in range(3)]
        for cp in first:
            cp.start()
        passed = [to_sibling(k, c) for k in range(3)]
        for k in range(3):
            over_ici(k, 2 * chips[k][0] + chips[k][1]).wait_recv()
            passed[k].start()
        for k in range(3):
            to_sibling(k, 1 - c).wait_recv()
        for cp in first + passed:
            cp.wait_send()
        local.wait()

    return pl.pallas_call(
        body, name="all_gather_w_in", in_specs=[_HBM], out_specs=_HBM,
        out_shape=jax.ShapeDtypeStruct((NSH,) + part.shape, part.dtype),
        scratch_shapes=[pltpu.SemaphoreType.DMA((6,)), pltpu.SemaphoreType.DMA((6,)), pltpu.SemaphoreType.DMA((1,))],
    )(part)


def _split_start(name, srcs, lands, n_sems, plan, dep):
    n, nl = len(srcs), len(lands)

    def body(*refs):
        src_refs, land_refs = refs[:n], refs[n:n + nl]
        send_sems, recv_sems = refs[n + nl + 1], refs[n + nl + 2]
        token = refs[-1]
        sends, _ = plan(src_refs, land_refs, send_sems, recv_sems)
        for cp in sends:
            cp.start()
        token[...] = jnp.zeros_like(token)

    outs = pl.pallas_call(
        body, name=name,
        in_specs=[_HBM] * (n + nl) + [pl.BlockSpec(memory_space=pl.ANY)],
        out_specs=[_SEM, _SEM] + [_HBM] * (n + nl) + [pl.BlockSpec(memory_space=pltpu.VMEM)],
        out_shape=[pltpu.SemaphoreType.DMA((n_sems,)), pltpu.SemaphoreType.DMA((n_sems,))]
        + [pltpu.HBM(a.shape, a.dtype) for a in list(srcs) + list(lands)] + [jax.ShapeDtypeStruct((8, 128), f32)],
        input_output_aliases={i: 2 + i for i in range(n + nl)},
        compiler_params=pltpu.CompilerParams(has_side_effects=_EFFECT),
    )(*[_in_hbm(a) for a in list(srcs) + list(lands)], dep)
    return outs[0], outs[1], list(outs[2:2 + n]), list(outs[2 + n:2 + n + nl]), outs[-1]


def _split_wait(name, send_sems, recv_sems, srcs, lands, after, plan):
    n, nl = len(srcs), len(lands)

    def body(*refs):
        src_refs, land_refs = refs[:n], refs[n:n + nl]
        s_sems, r_sems = refs[n + nl], refs[n + nl + 1]
        sends, recvs = plan(src_refs, land_refs, s_sems, r_sems)
        for cp in recvs:
            cp.wait_recv()
        for cp in sends:
            cp.wait_send()

    outs = pl.pallas_call(
        body, name=name,
        in_specs=[_HBM] * (n + nl) + [_SEM, _SEM, pl.BlockSpec(memory_space=pl.ANY)],
        out_specs=[_HBM] * (n + nl),
        out_shape=[pltpu.HBM(a.shape, a.dtype) for a in list(srcs) + list(lands)],
        input_output_aliases={i: i for i in range(n + nl)},
        compiler_params=pltpu.CompilerParams(has_side_effects=_EFFECT),
    )(*srcs, *lands, send_sems, recv_sems, after)
    return list(outs[:n]), list(outs[n:])


def _gather_plan(srcs, lands, ss, rs):
    x, y, _ = _mesh_pos()
    chips = _other_chips(x, y)
    sends = [_gather_copy(srcs, lands, ss, rs, a, k, 2 * x + y) for a in range(len(srcs)) for k in range(3)]
    recvs = [_gather_copy(srcs, lands, ss, rs, a, k, 2 * chips[k][0] + chips[k][1])
             for a in range(len(srcs)) for k in range(3)]
    return sends, recvs


def _scatter_plan(srcs, lands, ss, rs):
    cps = [_scatter_copy(srcs, lands, ss, rs, a, k) for a in range(len(srcs)) for k in range(3)]
    return cps, cps


def _tail_plan(srcs, lands, ss, rs):
    x, y, c = _mesh_pos()
    me = 4 * x + 2 * y + c
    cps = [_scatter_copy(srcs[:1], lands[:1], ss, rs, 0, k) for k in range(3)]
    for f in range(1, 8):
        peer = ((x + (f >> 2)) % 2, (y + ((f >> 1) & 1)) % 2, (c + (f & 1)) % 2)
        cps.append(pltpu.make_async_remote_copy(
            src_ref=srcs[1], dst_ref=lands[1].at[me], send_sem=ss.at[2 + f], recv_sem=rs.at[2 + f],
            device_id=peer, device_id_type=MESH))
    return cps, cps


def _swap_with_sibling(parts, name):
    n = len(parts)

    def body(*refs):
        srcs, dsts = refs[:n], refs[n:2 * n]
        send_sems, recv_sems = refs[2 * n:]
        x, y, c = _mesh_pos()
        cps = [pltpu.make_async_remote_copy(src_ref=srcs[a], dst_ref=dsts[a], send_sem=send_sems.at[a],
                                            recv_sem=recv_sems.at[a], device_id=(x, y, 1 - c), device_id_type=MESH)
               for a in range(n)]
        for cp in cps:
            cp.start()
        for cp in cps:
            cp.wait_recv()
        for cp in cps:
            cp.wait_send()

    return pl.pallas_call(
        body, name=name, in_specs=[_HBM] * n, out_specs=[_HBM] * n,
        out_shape=[jax.ShapeDtypeStruct(p.shape, p.dtype) for p in parts],
        scratch_shapes=[pltpu.SemaphoreType.DMA((n,)), pltpu.SemaphoreType.DMA((n,))],
    )(*parts)


def _pack_small(n1, n2, nf, ps, bfv, wp):
    pad8 = lambda r: jnp.pad(r, ((0, 8 - r.shape[0]), (0, 0)))
    return jnp.concatenate([n1.reshape(8, 128), n2.reshape(8, 128), nf.reshape(8, 128), pad8(ps.reshape(4, 128)),
                            pad8(jnp.pad(bfv.reshape(1, 8), ((0, 0), (0, 120)))), wp.reshape(512, 128)], axis=0)


def _unpack_small(p):
    return dict(norm1_g=p[0:8].reshape(1, D), norm2_g=p[8:16].reshape(1, D), final_g=p[16:24].reshape(D),
                pool_scale=p[24:28].reshape(1, AW), b_forget=p[32:33, 0:8], w_pool=p[40:552].reshape(1, 4, 128, 128))


def _forward(x, tgt, wm, wf, mlp_w_fn, g1, bfp, wp, scale, g2, gf, dep):
    h, qkv, u, fl = _rms_inproj(x, g1, wm, wf, dep)
    qaug, kaug = _fox_cumsum(fl, bfp)
    attn, lse = _attn_fwd(qkv, qaug, kaug)
    pooled, pool = _pool_fwd(u, wp, scale)
    wo, wgt, wut, wd = mlp_w_fn(attn)
    x1, h2 = _outproj(x, attn, pool, wo, g2)
    x2, gate, up = _mlp_fwd(h2, x1, wgt, wut, wd)
    loss, dgf, dx2, dx2b = _final_loss(x2, tgt, gf)
    saved = dict(h=h, qkv=qkv, fl=fl, qaug=qaug, kaug=kaug, attn=attn, lse=lse, pooled=pooled, pool=pool, x1=x1, h2=h2,
                 gate=gate, up=up, wo=wo, wgt=wgt, wut=wut, wd=wd)
    return loss, dgf, dx2, dx2b, saved


def _backward_mlp(sv, dx2, dx2b, g2):
    a_b, dgate, dup, dx1, dx1b, dg2 = _mlp_bwd(dx2b, dx2, sv["gate"], sv["up"], sv["wgt"], sv["wut"], sv["wd"], sv["x1"], g2)
    (dwd,) = _mm_tn(a_b, [dx2b], "dw_down", a_sharded=True)
    (dwgt,) = _mm_tn(dgate, [sv["h2"]], "dw_gate", a_sharded=True)
    (dwut,) = _mm_tn(dup, [sv["h2"]], "dw_up", a_sharded=True)
    return dx1, dx1b, dg2, (dwgt, dwut, dwd)


def _backward_outproj(sv, dx1b):
    dattn, dpool = _outproj_bwd(dx1b, sv["wo"])
    dwo_a, = _mm_tn(sv["attn"], [dx1b], "dw_out_attn")
    dwo_p, = _mm_tn(sv["pool"], [dx1b], "dw_out_pool")
    dwo = jnp.concatenate([dwo_a, dwo_p], axis=0).reshape(NSH, D // NSH, D)
    return dattn, dpool, dwo


def _backward_mixer(sv, x, dx1, dattn, dpool, wm, wf, g1, bfp, wp, scale, dep):
    du, dscale, dwp = _pool_bwd(dpool, sv["pooled"], wp, scale, dep)
    dq, dqs, dk, dks, dv = _attn_bwd(sv["qkv"], sv["qaug"], sv["kaug"], sv["attn"], dattn, sv["lse"], dep)
    df, dbf = _fox_cumsum_bwd(dqs, dks, sv["fl"], bfp)
    dx, dg1 = _inproj_bwd(dq, dk, dv, du, df, wm, wf, x, dx1, g1)
    dwq, dwk, dwv, dwu_in, dwf = _mm_tn(sv["h"], [dq, dk, dv, du, df], "dw_in")
    dwin = jnp.concatenate([dwq, dwk, dwv, dwf[:, 0:8], dwu_in], axis=1)
    dwin = dwin.reshape(D, NSH, IN_S).transpose(1, 0, 2)
    return dx, dg1, dscale, dwp, dbf, dwin


def kernel(x, norm1_g, w_in, b_forget, w_pool, pool_scale, w_out, norm2_g, w_gate, w_up, w_down, final_g, loss_target, m_norm1_g, m_w_in, m_b_forget, m_w_pool, m_pool_scale, m_w_out, m_norm2_g, m_w_gate, m_w_up, m_w_down, m_final_g, v_norm1_g, v_w_in, v_b_forget, v_w_pool, v_pool_scale, v_w_out, v_norm2_g, v_w_gate, v_w_up, v_w_down, v_final_g):
    mine = (2 * lax.axis_index("x") + lax.axis_index("y")).astype(jnp.int32)
    mine1 = mine.reshape(1)
    tr = lambda a: jnp.transpose(a[0])

    win4 = _all_gather_w_in(w_in[0].astype(bf16))
    later = [w_out[0].astype(bf16), tr(w_gate).astype(bf16), tr(w_up).astype(bf16), w_down[0].astype(bf16)]
    lands = [lax.dynamic_update_slice(lax.empty((NSH,) + p.shape, bf16), p[None], (mine, 0, 0)) for p in later]
    ag_send, ag_recv, later_thru, lands_thru, ag_token = _split_start("all_gather_start", later, lands, 12, _gather_plan,
                                                                      win4)
    win = win4.transpose(1, 0, 2).reshape(D, IN_W)
    wm = jnp.concatenate([win[:, 0:3 * AW], win[:, 3 * AW + 8:]], axis=1)
    wf = jnp.pad(win[:, 3 * AW:3 * AW + 8], ((0, 0), (0, 120)))
    bfp = jnp.pad(b_forget, ((0, 0), (0, 120)))
    wp = w_pool[0].astype(bf16)
    gf = final_g.reshape(1, D)

    def later_weights(after):
        _, (wo4, wgt, wut, wd) = _split_wait("all_gather_wait", ag_send, ag_recv, later_thru, lands_thru, after, _gather_plan)
        return wo4.reshape(D, D), wgt, wut, wd

    xe, tgt = x[0], loss_target[0]
    loss_v, dgf, dx2, dx2b, sv = _forward(xe, tgt, wm, wf, later_weights, norm1_g, bfp, wp, pool_scale, norm2_g, gf, ag_token)
    dx1, dx1b, dg2, mlp_grads = _backward_mlp(sv, dx2, dx2b, norm2_g)
    dattn, dpool, dwo = _backward_outproj(sv, dx1b)
    first = [dwo] + list(mlp_grads)
    first_lands = [lax.empty((3,) + g.shape[1:], bf16) for g in first]
    rs_send, rs_recv, first_thru, first_lands_thru, rs_token = _split_start("reduce_scatter_start", first, first_lands, 12,
                                                                            _scatter_plan, dattn)
    dx, dg1, dscale, dwp, dbf, dwin = _backward_mixer(sv, xe, dx1, dattn, dpool, wm, wf, norm1_g, bfp, wp, pool_scale, rs_token)

    me = (4 * lax.axis_index("x") + 2 * lax.axis_index("y") + lax.axis_index("c")).astype(jnp.int32)
    pad8 = lambda r: jnp.pad(r, ((0, 8 - r.shape[0]), (0, 0)))
    loss_rows = jnp.concatenate([dbf, jnp.zeros((6, 128), f32), loss_v[0:1, :]], axis=0)
    small = jnp.concatenate([dg1.reshape(8, 128), dg2.reshape(8, 128), dgf.reshape(8, 128), pad8(dscale.reshape(4, 128)),
                             loss_rows, dwp.reshape(512, 128)], axis=0)
    small_land = lax.dynamic_update_slice(lax.empty((8, SMALL_ROWS, 128), f32), small[None], (me, 0, 0))
    tail_send, tail_recv, tail_thru, tail_lands_thru, tail_token = _split_start(
        "tail_start", [dwin, small], [lax.empty((3,) + dwin.shape[1:], bf16), small_land], 10, _tail_plan, dx)
    first_thru, first_recv = _split_wait("reduce_scatter_wait", rs_send, rs_recv, first_thru, first_lands_thru, tail_token,
                                         _scatter_plan)
    ws = [w_in[0], w_out[0], tr(w_gate), tr(w_up), w_down[0]]
    ms = [m_w_in[0], m_w_out[0], tr(m_w_gate), tr(m_w_up), m_w_down[0]]
    vs = [v_w_in[0], v_w_out[0], tr(v_w_gate), tr(v_w_up), v_w_down[0]]
    partial = [_sum4(r, g, mine1, f"sum4_{i + 1}") for i, (r, g) in enumerate(zip(first_recv, first_thru))]
    other = _swap_with_sibling(partial, "swap_first")
    big = [_adamw_shard(ws[i + 1], ms[i + 1], vs[i + 1], partial[i], other[i], f"adamw_{i + 1}") for i in range(4)]
    (dwin_thru, _), (in_recv_land, small_all) = _split_wait("tail_wait", tail_send, tail_recv, tail_thru, tail_lands_thru,
                                                            big[3][0], _tail_plan)
    partial_in = _sum4(in_recv_land, dwin_thru, mine1, "sum4_0")
    (other_in,) = _swap_with_sibling([partial_in], "swap_in")
    big = [_adamw_shard(ws[0], ms[0], vs[0], partial_in, other_in, "adamw_0")] + big

    sm = _adamw_small(_pack_small(norm1_g, norm2_g, final_g, pool_scale, b_forget, w_pool),
                      _pack_small(m_norm1_g, m_norm2_g, m_final_g, m_pool_scale, m_b_forget, m_w_pool),
                      _pack_small(v_norm1_g, v_norm2_g, v_final_g, v_pool_scale, v_b_forget, v_w_pool), small_all)

    loss = sm[0][39, 0]
    order =["norm1_g", "w_in", "b_forget", "w_pool", "pool_scale", "w_out", "norm2_g", "w_gate", "w_up", "w_down", "final_g"]
    big_idx = {"w_in": 0, "w_out": 1, "w_gate": 2, "w_up": 3, "w_down": 4}
    outs = [loss, dx[None]]
    for kind in range(4):
        small_k = _unpack_small(sm[kind])
        for name in order:
            if name in ("w_gate", "w_up"):
                outs.append(jnp.transpose(big[big_idx[name]][kind])[None])
            elif name in big_idx:
                outs.append(big[big_idx[name]][kind][None])
            else:
                outs.append(small_k[name])
    return tuple(outs)
```

```python
import functools

import jax
import jax.numpy as jnp
import numpy as np
from jax import lax
from jax.experimental import pallas as pl
from jax.experimental.pallas import tpu as pltpu

f32 = jnp.float32
bf16 = jnp.bfloat16

T = 4096
D = 1024
NSH = 4
IN_W = 2056
IN_S = IN_W // NSH
AW = 512
PAIRS = 4
FF = 2816
FS = FF // NSH
WINDOWS = (2, 4, 8, 16)
HALO = 16
EPS = 1e-6
NEG = -1e30
LR, B1, B2, AEPS, WD, STEP = 0.001, 0.9, 0.999, 1e-08, 0.01, 10
SMALL_ROWS = 552

NT = (((1,), (1,)), ((), ()))
TN = (((0,), (0,)), ((), ()))

MESH = pl.DeviceIdType.MESH


def _cp(*sem):
    return pltpu.CompilerParams(dimension_semantics=sem)


def _full(shape):
    n = len(shape)
    return pl.BlockSpec(shape, lambda *_: (0,) * n)


def _rms_inproj(x, g1, wm, wf, dep):
    tm = 512

    def body(x_ref, g_ref, wm_ref, wf_ref, dep_ref, h_ref, qkv_ref, u_ref, fl_ref):
        xv = x_ref[...]
        r = lax.rsqrt(jnp.mean(xv * xv, axis=-1, keepdims=True) + EPS)
        h = (xv * r * g_ref[...]).astype(bf16)
        h_ref[...] = h
        qkv_ref[...] = jnp.dot(h, wm_ref[:, 0:3 * AW], preferred_element_type=f32).astype(bf16)
        u_ref[...] = jnp.dot(h, wm_ref[:, 3 * AW:4 * AW], preferred_element_type=f32)
        fl_ref[...] = jnp.dot(h, wf_ref[...], preferred_element_type=f32)

    return pl.pallas_call(
        body, name="rms_inproj", grid=(T // tm,),
        in_specs=[pl.BlockSpec((tm, D), lambda i: (i, 0)), _full((1, D)), _full((D, 4 * AW)), _full((D, 128)),
                  _full((8, 128))],
        out_specs=[pl.BlockSpec((tm, D), lambda i: (i, 0)), pl.BlockSpec((tm, 3 * AW), lambda i: (i, 0)),
                   pl.BlockSpec((tm, AW), lambda i: (i, 0)), pl.BlockSpec((tm, 128), lambda i: (i, 0))],
        out_shape=[jax.ShapeDtypeStruct((T, D), bf16), jax.ShapeDtypeStruct((T, 3 * AW), bf16),
                   jax.ShapeDtypeStruct((T, AW), f32), jax.ShapeDtypeStruct((T, 128), f32)],
        compiler_params=_cp("parallel"),
    )(x, g1, wm, wf, dep)


def _log_sigmoid(z):
    return jnp.minimum(z, 0.0) - jnp.log(1.0 + jnp.exp(-jnp.abs(z)))


def _fox_cumsum(fl, bfp):
    nb = T // 128

    def body(fl_ref, b_ref, qa_ref, ka_ref, carry):
        i = pl.program_id(0)

        @pl.when(i == 0)
        def _():
            carry[...] = jnp.zeros_like(carry)

        lf = _log_sigmoid(fl_ref[...] + b_ref[...])
        r = lax.broadcasted_iota(jnp.int32, (128, 128), 0)
        cc = lax.broadcasted_iota(jnp.int32, (128, 128), 1)
        ltri = (cc <= r).astype(f32)
        cb = jnp.dot(ltri, lf, precision=lax.Precision.HIGHEST, preferred_element_type=f32) + carry[0:1, :]
        carry[...] = jnp.broadcast_to(cb[127:128, :], (8, 128))
        hi = cb.astype(bf16)
        r1 = cb - hi.astype(f32)
        mid = r1.astype(bf16)
        lo = (r1 - mid.astype(f32)).astype(bf16)
        head = lax.broadcasted_iota(jnp.int32, (128, AW), 0)
        col = lax.broadcasted_iota(jnp.int32, (128, AW), 1)
        base = 128 * (head >> 1) + 64 * (1 - (head & 1))
        place = lambda off: jnp.logical_and(col == base + off, head < 8).astype(bf16)
        mm = lambda a, off: jnp.dot(a, place(off), preferred_element_type=f32)
        cq = mm(hi, 0) + mm(mid, 1) + mm(lo, 2)
        ck = mm(hi, 3) + mm(mid, 4) + mm(lo, 5)
        within = jnp.bitwise_and(lax.broadcasted_iota(jnp.int32, (128, AW), 1), 63)
        qa_ref[...] = jnp.where(jnp.logical_and(within >= 3, within <= 5), 1.0, cq).astype(bf16)
        ka_ref[...] = jnp.where(within <= 2, 1.0, -ck).astype(bf16)

    return pl.pallas_call(
        body, name="fox_cumsum", grid=(nb,),
        in_specs=[pl.BlockSpec((128, 128), lambda i: (i, 0)), _full((1, 128))],
        out_specs=[pl.BlockSpec((128, AW), lambda i: (i, 0)), pl.BlockSpec((128, AW), lambda i: (i, 0))],
        out_shape=[jax.ShapeDtypeStruct((T, AW), bf16), jax.ShapeDtypeStruct((T, AW), bf16)],
        scratch_shapes=[pltpu.VMEM((8, 128), f32)],
        compiler_params=_cp("arbitrary"),
    )(fl, bfp)


ATT_T = 512


def _causal_steps(key_major):
    n = T // ATT_T
    if key_major:
        pairs = [(i, j) for j in range(n) for i in range(j, n)]
    else:
        pairs = [(i, j) for i in range(n) for j in range(i + 1)]
    it = np.array([p[0] for p in pairs], np.int32)
    jt = np.array([p[1] for p in pairs], np.int32)
    return jnp.asarray(it), jnp.asarray(jt)


def _attn_fwd(qkv, qaug, kaug):
    tq = tk = ATT_T
    it, jt = _causal_steps(False)
    nsteps = it.shape[0]

    rs = 64

    def body(it_ref, jt_ref, q_ref, k_ref, v_ref, qa_ref, ka_ref, o_ref, lse_ref, m_sc, acc_sc, s_sc, p_sc, alpha_sc):
        t = pl.program_id(1)
        i = it_ref[t]
        j = jt_ref[t]

        @pl.when(j == 0)
        def _():
            m_sc[...] = jnp.full_like(m_sc, NEG)
            acc_sc[...] = jnp.zeros_like(acc_sc)

        lane = lax.broadcasted_iota(jnp.int32, (tq, 128), 1)
        spare = (64, 0)

        def step(on_diagonal):
            q = q_ref[...] * 0.125
            k = k_ref[...]
            v = v_ref[...]
            qa = qa_ref[...]
            ka = ka_ref[...]
            for e in range(2):
                hm = (lane >= 64) if e else (lane < 64)
                s_sc[...] = lax.dot_general(jnp.where(hm, q, qa), jnp.where(hm, k, ka), NT, preferred_element_type=f32)
                for r in range(0, tq, rs):
                    s = s_sc[r:r + rs, :]
                    if on_diagonal:
                        row = lax.broadcasted_iota(jnp.int32, (rs, tk), 0) + r
                        col = lax.broadcasted_iota(jnp.int32, (rs, tk), 1)
                        s = jnp.where(col <= row, s, NEG)
                    m_prev = m_sc[e, r:r + rs, :]
                    m_new = jnp.maximum(m_prev, jnp.max(s, axis=1, keepdims=True))
                    p_sc[r:r + rs, :] = jnp.exp(s - jnp.tile(m_new, (1, tk // 128))).astype(bf16)
                    alpha_sc[r:r + rs, :] = jnp.exp(m_prev - m_new)
                    m_sc[e, r:r + rs, :] = m_new
                ve = jnp.where(hm, v, (lane == spare[e]).astype(bf16))
                acc_sc[e] = alpha_sc[...] * acc_sc[e] + jnp.dot(p_sc[...], ve, preferred_element_type=f32)

        @pl.when(j < i)
        def _():
            step(False)

        @pl.when(j == i)
        def _():
            step(True)
            l0 = acc_sc[0][:, spare[0]:spare[0] + 1]
            l1 = acc_sc[1][:, spare[1]:spare[1] + 1]
            o_ref[...] = jnp.where(lane < 64, acc_sc[0] / l0, acc_sc[1] / l1).astype(bf16)
            lse_ref[...] = jnp.where(lane < 64, m_sc[0] + jnp.log(l0), m_sc[1] + jnp.log(l1))

    qmap = lambda p, t, it, jt: (it[t], p)
    kmap = lambda p, t, it, jt: (jt[t], p)
    grid_spec = pltpu.PrefetchScalarGridSpec(
        num_scalar_prefetch=2, grid=(PAIRS, nsteps),
        in_specs=[pl.BlockSpec((tq, 128), qmap),
                  pl.BlockSpec((tk, 128), lambda p, t, it, jt: (jt[t], PAIRS + p)),
                  pl.BlockSpec((tk, 128), lambda p, t, it, jt: (jt[t], 2 * PAIRS + p)),
                  pl.BlockSpec((tq, 128), qmap), pl.BlockSpec((tk, 128), kmap)],
        out_specs=[pl.BlockSpec((tq, 128), qmap),
                   pl.BlockSpec((None, tq, 128), lambda p, t, it, jt: (p, it[t], 0))],
        scratch_shapes=[pltpu.VMEM((2, tq, 128), f32), pltpu.VMEM((2, tq, 128), f32), pltpu.VMEM((tq, tk), f32),
                        pltpu.VMEM((tq, tk), bf16), pltpu.VMEM((tq, 128), f32)],
    )
    return pl.pallas_call(
        body, name="fox_attn_fwd", grid_spec=grid_spec,
        out_shape=[jax.ShapeDtypeStruct((T, AW), bf16), jax.ShapeDtypeStruct((PAIRS, T, 128), f32)],
        compiler_params=_cp("parallel", "arbitrary"),
    )(it, jt, qkv, qkv, qkv, qaug, kaug)


def _pool_fwd(u, wp, scale):
    tm = 512

    def body(u_ref, wp_ref, sc_ref, pooled_ref, pool_ref, ext):
        i = pl.program_id(0)

        @pl.when(i == 0)
        def _():
            ext[0:HALO, :] = jnp.zeros((HALO, AW), f32)

        uv = u_ref[...]
        ext[HALO:HALO + tm, :] = uv
        t_idx = i * tm + lax.broadcasted_iota(jnp.int32, (tm, 1), 0)
        for g, w in enumerate(WINDOWS):
            lo, hi = 128 * g, 128 * (g + 1)
            ug = uv[:, lo:hi]
            acc = ug
            for d in range(1, w):
                acc = acc + ext[HALO - d:HALO - d + tm, lo:hi]
            cnt = jnp.minimum(t_idx + 1, w).astype(f32)
            pb = (acc / cnt - ug).astype(bf16)
            pooled_ref[:, lo:hi] = pb
            mixed = jnp.dot(pb, wp_ref[g], preferred_element_type=f32)
            pool_ref[:, lo:hi] = (mixed * sc_ref[:, lo:hi]).astype(bf16)
        ext[0:HALO, :] = uv[tm - HALO:tm, :]

    return pl.pallas_call(
        body, name="pool_fwd", grid=(T // tm,),
        in_specs=[pl.BlockSpec((tm, AW), lambda i: (i, 0)), _full((4, 128, 128)), _full((1, AW))],
        out_specs=[pl.BlockSpec((tm, AW), lambda i: (i, 0)), pl.BlockSpec((tm, AW), lambda i: (i, 0))],
        out_shape=[jax.ShapeDtypeStruct((T, AW), bf16), jax.ShapeDtypeStruct((T, AW), bf16)],
        scratch_shapes=[pltpu.VMEM((tm + HALO, AW), f32)],
        compiler_params=_cp("arbitrary"),
    )(u, wp, scale)


def _outproj(x, attn, pool, wo, g2):
    tm = 512

    def body(x_ref, a_ref, p_ref, wo_ref, g_ref, x1_ref, h2_ref):
        x1 = x_ref[...] + jnp.dot(a_ref[...], wo_ref[0:AW, :], preferred_element_type=f32)
        x1 = x1 + jnp.dot(p_ref[...], wo_ref[AW:2 * AW, :], preferred_element_type=f32)
        x1_ref[...] = x1
        r = lax.rsqrt(jnp.mean(x1 * x1, axis=-1, keepdims=True) + EPS)
        h2_ref[...] = (x1 * r * g_ref[...]).astype(bf16)

    return pl.pallas_call(
        body, name="outproj", grid=(T // tm,),
        in_specs=[pl.BlockSpec((tm, D), lambda i: (i, 0)), pl.BlockSpec((tm, AW), lambda i: (i, 0)),
                  pl.BlockSpec((tm, AW), lambda i: (i, 0)), _full((D, D)), _full((1, D))],
        out_specs=[pl.BlockSpec((tm, D), lambda i: (i, 0)), pl.BlockSpec((tm, D), lambda i: (i, 0))],
        out_shape=[jax.ShapeDtypeStruct((T, D), f32), jax.ShapeDtypeStruct((T, D), bf16)],
        compiler_params=_cp("parallel"),
    )(x, attn, pool, wo, g2)


def _mlp_fwd(h2, x1, wg, wu, wd):
    tm = 512

    def body(h_ref, x1_ref, wg_ref, wu_ref, wd_ref, x2_ref, gate_ref, up_ref):
        s = pl.program_id(1)
        h = h_ref[...]
        gate = lax.dot_general(h, wg_ref[...], NT, preferred_element_type=f32)
        up = lax.dot_general(h, wu_ref[...], NT, preferred_element_type=f32)
        gate_ref[...] = gate
        up_ref[...] = up
        a = (gate * jax.nn.sigmoid(gate) * up).astype(bf16)
        part = jnp.dot(a, wd_ref[...], preferred_element_type=f32)

        @pl.when(s == 0)
        def _():
            x2_ref[...] = x1_ref[...] + part

        @pl.when(s > 0)
        def _():
            x2_ref[...] += part

    return pl.pallas_call(
        body, name="mlp_fwd", grid=(T // tm, NSH),
        in_specs=[pl.BlockSpec((tm, D), lambda i, s: (i, 0)), pl.BlockSpec((tm, D), lambda i, s: (i, 0)),
                  pl.BlockSpec((None, FS, D), lambda i, s: (s, 0, 0)), pl.BlockSpec((None, FS, D), lambda i, s: (s, 0, 0)),
                  pl.BlockSpec((None, FS, D), lambda i, s: (s, 0, 0))],
        out_specs=[pl.BlockSpec((tm, D), lambda i, s: (i, 0)), pl.BlockSpec((None, tm, FS), lambda i, s: (s, i, 0)),
                   pl.BlockSpec((None, tm, FS), lambda i, s: (s, i, 0))],
        out_shape=[jax.ShapeDtypeStruct((T, D), f32), jax.ShapeDtypeStruct((NSH, T, FS), f32),
                   jax.ShapeDtypeStruct((NSH, T, FS), f32)],
        compiler_params=_cp("parallel", "arbitrary"),
    )(h2, x1, wg, wu, wd)


def _final_loss(x2, tgt, gf):
    tm = 512

    def body(x_ref, t_ref, g_ref, loss_ref, dg_ref, dx_ref, dxb_ref):
        i = pl.program_id(0)

        @pl.when(i == 0)
        def _():
            loss_ref[...] = jnp.zeros_like(loss_ref)
            dg_ref[...] = jnp.zeros_like(dg_ref)

        xv = x_ref[...]
        g = g_ref[...]
        r = lax.rsqrt(jnp.mean(xv * xv, axis=-1, keepdims=True) + EPS)
        xhat = xv * r
        e = xhat * g - t_ref[...]
        loss_ref[...] += 0.5 * jnp.sum(jnp.mean(e * e, axis=-1, keepdims=True))
        dy = e * (1.0 / D)
        dg_ref[...] += jnp.sum(dy * xhat, axis=0, keepdims=True)
        z = dy * g
        dx = r * (z - xhat * jnp.mean(z * xhat, axis=-1, keepdims=True))
        dx_ref[...] = dx
        dxb_ref[...] = dx.astype(bf16)

    return pl.pallas_call(
        body, name="final_loss", grid=(T // tm,),
        in_specs=[pl.BlockSpec((tm, D), lambda i: (i, 0)), pl.BlockSpec((tm, D), lambda i: (i, 0)), _full((1, D))],
        out_specs=[_full((8, 128)), _full((1, D)), pl.BlockSpec((tm, D), lambda i: (i, 0)),
                   pl.BlockSpec((tm, D), lambda i: (i, 0))],
        out_shape=[jax.ShapeDtypeStruct((8, 128), f32), jax.ShapeDtypeStruct((1, D), f32),
                   jax.ShapeDtypeStruct((T, D), f32), jax.ShapeDtypeStruct((T, D), bf16)],
        compiler_params=_cp("arbitrary"),
    )(x2, tgt, gf)


def _mlp_bwd(dx2b, dx2, gate, up, wg, wu, wd, x1, g2):
    tm = 512

    def body(dxb_ref, dx_ref, gate_ref, up_ref, wg_ref, wu_ref, wd_ref, x1_ref, g_ref,
             a_ref, dg_ref, du_ref, dx1_ref, dx1b_ref, dn_ref, acc):
        i = pl.program_id(0)
        s = pl.program_id(1)

        @pl.when(jnp.logical_and(i == 0, s == 0))
        def _():
            dn_ref[...] = jnp.zeros_like(dn_ref)

        da = lax.dot_general(dxb_ref[...], wd_ref[...], NT, preferred_element_type=f32)
        gate = gate_ref[...]
        upv = up_ref[...]
        sg = jax.nn.sigmoid(gate)
        silu = gate * sg
        a_ref[...] = (silu * upv).astype(bf16)
        dgate = (da * upv * (sg * (1.0 + gate * (1.0 - sg)))).astype(bf16)
        dup = (da * silu).astype(bf16)
        dg_ref[...] = dgate
        du_ref[...] = dup
        part = jnp.dot(dgate, wg_ref[...], preferred_element_type=f32)
        part = part + jnp.dot(dup, wu_ref[...], preferred_element_type=f32)

        @pl.when(s == 0)
        def _():
            acc[...] = part

        @pl.when(s > 0)
        def _():
            acc[...] += part

        @pl.when(s == NSH - 1)
        def _():
            xv = x1_ref[...]
            r = lax.rsqrt(jnp.mean(xv * xv, axis=-1, keepdims=True) + EPS)
            xhat = xv * r
            dh = acc[...]
            dn_ref[...] += jnp.sum(dh * xhat, axis=0, keepdims=True)
            z = dh * g_ref[...]
            dx1 = dx_ref[...] + r * (z - xhat * jnp.mean(z * xhat, axis=-1, keepdims=True))
            dx1_ref[...] = dx1
            dx1b_ref[...] = dx1.astype(bf16)

    row = lambda i, s: (i, 0)
    sl = lambda i, s: (s, i, 0)
    wsl = lambda i, s: (s, 0, 0)
    return pl.pallas_call(
        body, name="mlp_bwd", grid=(T // tm, NSH),
        in_specs=[pl.BlockSpec((tm, D), row), pl.BlockSpec((tm, D), row),
                  pl.BlockSpec((None, tm, FS), sl), pl.BlockSpec((None, tm, FS), sl),
                  pl.BlockSpec((None, FS, D), wsl), pl.BlockSpec((None, FS, D), wsl), pl.BlockSpec((None, FS, D), wsl),
                  pl.BlockSpec((tm, D), row), pl.BlockSpec((1, D), lambda i, s: (0, 0))],
        out_specs=[pl.BlockSpec((None, tm, FS), sl), pl.BlockSpec((None, tm, FS), sl), pl.BlockSpec((None, tm, FS), sl),
                   pl.BlockSpec((tm, D), row), pl.BlockSpec((tm, D), row), pl.BlockSpec((1, D), lambda i, s: (0, 0))],
        out_shape=[jax.ShapeDtypeStruct((NSH, T, FS), bf16)] * 3
        + [jax.ShapeDtypeStruct((T, D), f32), jax.ShapeDtypeStruct((T, D), bf16), jax.ShapeDtypeStruct((1, D), f32)],
        scratch_shapes=[pltpu.VMEM((tm, D), f32)],
        compiler_params=_cp("arbitrary", "arbitrary"),
    )(dx2b, dx2, gate, up, wg, wu, wd, x1, g2)


def _mm_tn(a, bs, name, a_sharded=False, b_sharded=False, tk=512, out_dtype=bf16):
    nb = len(bs)
    sh = NSH if (a_sharded or b_sharded) else 1
    m = a.shape[-1]
    nk = T // tk

    def body(a_ref, *refs):
        kk = pl.program_id(1)
        av = a_ref[...]
        for b_ref, o_ref, acc in zip(refs[:nb], refs[nb:2 * nb], refs[2 * nb:]):
            upd = lax.dot_general(av, b_ref[...], TN, preferred_element_type=f32)

            @pl.when(kk == 0)
            def _():
                acc[...] = upd

            @pl.when(kk > 0)
            def _():
                acc[...] += upd

            @pl.when(kk == nk - 1)
            def _():
                o_ref[...] = acc[...].astype(out_dtype)

    a_spec = (pl.BlockSpec((None, tk, m), lambda s, k: (s, k, 0)) if a_sharded
              else pl.BlockSpec((tk, m), lambda s, k: (k, 0)))
    b_specs, o_specs, o_shapes, scratch = [], [], [], []
    for b in bs:
        n = b.shape[-1]
        b_specs.append(pl.BlockSpec((None, tk, n), lambda s, k: (s, k, 0)) if b_sharded
                       else pl.BlockSpec((tk, n), lambda s, k: (k, 0)))
        scratch.append(pltpu.VMEM((m, n), f32))
        if sh > 1:
            o_specs.append(pl.BlockSpec((None, m, n), lambda s, k: (s, 0, 0)))
            o_shapes.append(jax.ShapeDtypeStruct((sh, m, n), out_dtype))
        else:
            o_specs.append(pl.BlockSpec((m, n), lambda s, k: (0, 0)))
            o_shapes.append(jax.ShapeDtypeStruct((m, n), out_dtype))
    return pl.pallas_call(
        body, name=name, grid=(sh, nk), in_specs=[a_spec] + b_specs, out_specs=o_specs, out_shape=o_shapes,
        scratch_shapes=scratch, compiler_params=_cp("arbitrary", "arbitrary"),
    )(a, *bs)


def _outproj_bwd(dx1b, wo):
    tm = 512

    def body(dx_ref, wo_ref, da_ref, dp_ref):
        dx = dx_ref[...]
        da_ref[...] = lax.dot_general(dx, wo_ref[0:AW, :], NT, preferred_element_type=f32).astype(bf16)
        dp_ref[...] = lax.dot_general(dx, wo_ref[AW:2 * AW, :], NT, preferred_element_type=f32)

    return pl.pallas_call(
        body, name="outproj_bwd", grid=(T // tm,),
        in_specs=[pl.BlockSpec((tm, D), lambda i: (i, 0)), _full((D, D))],
        out_specs=[pl.BlockSpec((tm, AW), lambda i: (i, 0)), pl.BlockSpec((tm, AW), lambda i: (i, 0))],
        out_shape=[jax.ShapeDtypeStruct((T, AW), bf16), jax.ShapeDtypeStruct((T, AW), f32)],
        compiler_params=_cp("parallel"),
    )(dx1b, wo)


def _pool_bwd(dpool, pooled, wp, scale, dep):
    tm = 512
    n = T // tm

    def body(dp_ref, pb_ref, wp_ref, sc_ref, dep_ref, du_ref, dsc_ref, dwp_ref, ext):
        i = pl.program_id(0)

        @pl.when(i == 0)
        def _():
            ext[tm:tm + HALO, :] = jnp.zeros((HALO, AW), f32)
            dsc_ref[...] = jnp.zeros_like(dsc_ref)
            dwp_ref[...] = jnp.zeros_like(dwp_ref)

        t_idx = (n - 1 - i) * tm + lax.broadcasted_iota(jnp.int32, (tm, 1), 0)
        for g, w in enumerate(WINDOWS):
            lo, hi = 128 * g, 128 * (g + 1)
            pb = pb_ref[:, lo:hi]
            mixed = jnp.dot(pb, wp_ref[g], preferred_element_type=f32)
            dpo = dp_ref[:, lo:hi]
            dsc_ref[:, lo:hi] += jnp.sum(dpo * mixed, axis=0, keepdims=True)
            dmr = (dpo * sc_ref[:, lo:hi]).astype(bf16)
            dwp_ref[g] += lax.dot_general(pb, dmr, TN, preferred_element_type=f32)
            dpl = lax.dot_general(dmr, wp_ref[g], NT, preferred_element_type=f32)
            cnt = jnp.minimum(t_idx + 1, w).astype(f32)
            dpn = dpl / cnt
            ext[0:tm, lo:hi] = dpn
            acc = dpn
            for d in range(1, w):
                acc = acc + ext[d:d + tm, lo:hi]
            du_ref[:, lo:hi] = (acc - dpl).astype(bf16)
        ext[tm:tm + HALO, :] = ext[0:HALO, :]

    rev = lambda i: (n - 1 - i, 0)
    return pl.pallas_call(
        body, name="pool_bwd", grid=(n,),
        in_specs=[pl.BlockSpec((tm, AW), rev), pl.BlockSpec((tm, AW), rev), _full((4, 128, 128)), _full((1, AW)),
                  _full((8, 128))],
        out_specs=[pl.BlockSpec((tm, AW), rev), _full((1, AW)), _full((4, 128, 128))],
        out_shape=[jax.ShapeDtypeStruct((T, AW), bf16), jax.ShapeDtypeStruct((1, AW), f32),
                   jax.ShapeDtypeStruct((4, 128, 128), f32)],
        scratch_shapes=[pltpu.VMEM((tm + HALO, AW), f32)],
        compiler_params=_cp("arbitrary"),
    )(dpool, pooled, wp, scale, dep)


def _attn_bwd(qkv, qaug, kaug, attn, dattn, lse, dep):
    tq = tk = ATT_T
    n = T // tq
    it, jt = _causal_steps(True)
    nsteps = it.shape[0]

    rs = 64

    def body(it_ref, jt_ref, q_ref, k_ref, v_ref, qa_ref, ka_ref, o_ref, do_ref, lse_ref, dep_ref,
             dq_ref, dqs_ref, dk_ref, dks_ref, dv_ref, dq_acc, dk_acc, dv_acc, s_sc, dp_sc, p_sc, ds_sc):
        t = pl.program_id(1)
        i = it_ref[t]
        j = jt_ref[t]

        @pl.when(t == 0)
        def _():
            dq_acc[...] = jnp.zeros_like(dq_acc)

        @pl.when(i == j)
        def _():
            dk_acc[...] = jnp.zeros_like(dk_acc)
            dv_acc[...] = jnp.zeros_like(dv_acc)

        lane = lax.broadcasted_iota(jnp.int32, (tq, 128), 1)

        def step(on_diagonal):
            q = q_ref[...] * 0.125
            k = k_ref[...]
            v = v_ref[...]
            qa = qa_ref[...]
            ka = ka_ref[...]
            do = do_ref[...]
            dd = do.astype(f32) * o_ref[...].astype(f32)
            r0 = pl.multiple_of(i * tq, tq)
            for e in range(2):
                hm = (lane >= 64) if e else (lane < 64)
                qe = jnp.where(hm, q, qa)
                ke = jnp.where(hm, k, ka)
                doe = jnp.where(hm, do, jnp.zeros_like(do))
                delta = jnp.sum(jnp.where(hm, dd, 0.0), axis=1, keepdims=True)
                s_sc[...] = lax.dot_general(qe, ke, NT, preferred_element_type=f32)
                dp_sc[...] = lax.dot_general(doe, v, NT, preferred_element_type=f32)
                for r in range(0, tq, rs):
                    s = s_sc[r:r + rs, :] - lse_ref[r:r + rs, 64 * e:64 * e + 1]
                    if on_diagonal:
                        row = lax.broadcasted_iota(jnp.int32, (rs, tk), 0) + r
                        col = lax.broadcasted_iota(jnp.int32, (rs, tk), 1)
                        s = jnp.where(col <= row, s, NEG)
                    p = jnp.exp(s)
                    p_sc[r:r + rs, :] = p.astype(bf16)
                    ds_sc[r:r + rs, :] = (p * (dp_sc[r:r + rs, :] - delta[r:r + rs, :])).astype(bf16)
                dv_acc[...] += lax.dot_general(p_sc[...], doe, TN, preferred_element_type=f32)
                dsb = ds_sc[...]
                dk_acc[e] += lax.dot_general(dsb, qe, TN, preferred_element_type=f32)
                dq_acc[e, pl.ds(r0, tq), :] += jnp.dot(dsb, ke, preferred_element_type=f32)

        @pl.when(i > j)
        def _():
            step(False)

        @pl.when(i == j)
        def _():
            step(True)

        @pl.when(i == n - 1)
        def _():
            dk_ref[...] = jnp.where(lane < 64, dk_acc[0], dk_acc[1]).astype(bf16)
            dks_ref[...] = jnp.where(lane < 64, dk_acc[1], dk_acc[0])
            dv_ref[...] = dv_acc[...].astype(bf16)

        @pl.when(t == nsteps - 1)
        def _():
            lane_t = lax.broadcasted_iota(jnp.int32, (T, 128), 1)
            dq_ref[...] = (jnp.where(lane_t < 64, dq_acc[0], dq_acc[1]) * 0.125).astype(bf16)
            dqs_ref[...] = jnp.where(lane_t < 64, dq_acc[1], dq_acc[0])

    qmap = lambda p, t, it, jt: (it[t], p)
    grid_spec = pltpu.PrefetchScalarGridSpec(
        num_scalar_prefetch=2, grid=(PAIRS, nsteps),
        in_specs=[pl.BlockSpec((tq, 128), qmap),
                  pl.BlockSpec((tk, 128), lambda p, t, it, jt: (jt[t], PAIRS + p)),
                  pl.BlockSpec((tk, 128), lambda p, t, it, jt: (jt[t], 2 * PAIRS + p)),
                  pl.BlockSpec((tq, 128), qmap), pl.BlockSpec((tk, 128), lambda p, t, it, jt: (jt[t], p)),
                  pl.BlockSpec((tq, 128), qmap), pl.BlockSpec((tq, 128), qmap),
                  pl.BlockSpec((None, tq, 128), lambda p, t, it, jt: (p, it[t], 0)),
                  pl.BlockSpec((8, 128), lambda p, t, it, jt: (0, 0))],
        out_specs=[pl.BlockSpec((T, 128), lambda p, t, it, jt: (0, p)),
                   pl.BlockSpec((None, T, 128), lambda p, t, it, jt: (p, 0, 0)),
                   pl.BlockSpec((tk, 128), lambda p, t, it, jt: (jt[t], p)),
                   pl.BlockSpec((None, tk, 128), lambda p, t, it, jt: (p, jt[t], 0)),
                   pl.BlockSpec((tk, 128), lambda p, t, it, jt: (jt[t], p))],
        scratch_shapes=[pltpu.VMEM((2, T, 128), f32), pltpu.VMEM((2, tk, 128), f32), pltpu.VMEM((tk, 128), f32),
                        pltpu.VMEM((tq, tk), f32), pltpu.VMEM((tq, tk), f32), pltpu.VMEM((tq, tk), bf16),
                        pltpu.VMEM((tq, tk), bf16)],
    )
    return pl.pallas_call(
        body, name="fox_attn_bwd", grid_spec=grid_spec,
        out_shape=[jax.ShapeDtypeStruct((T, AW), bf16), jax.ShapeDtypeStruct((PAIRS, T, 128), f32),
                   jax.ShapeDtypeStruct((T, AW), bf16), jax.ShapeDtypeStruct((PAIRS, T, 128), f32),
                   jax.ShapeDtypeStruct((T, AW), bf16)],
        compiler_params=_cp("parallel", "arbitrary"),
    )(it, jt, qkv, qkv, qkv, qaug, kaug, attn, dattn, lse, dep)


def _fox_cumsum_bwd(dqs, dks, fl, bfp):
    nb = T // 128
    hp = lax.Precision.HIGHEST

    def body(dqs_ref, dks_ref, fl_ref, b_ref, df_ref, db_ref, carry):
        i = pl.program_id(0)

        @pl.when(i == 0)
        def _():
            carry[...] = jnp.zeros_like(carry)
            db_ref[...] = jnp.zeros_like(db_ref)

        r = lax.broadcasted_iota(jnp.int32, (128, 128), 0)
        cc = lax.broadcasted_iota(jnp.int32, (128, 128), 1)
        pick = lambda even_lane, odd_lane, p: jnp.logical_or(
            jnp.logical_and(r == even_lane, cc == 2 * p), jnp.logical_and(r == odd_lane, cc == 2 * p + 1)).astype(f32)
        dc = jnp.zeros((128, 128), f32)
        for p in range(PAIRS):
            dc = dc + jnp.dot(dqs_ref[p], pick(64, 0, p), precision=hp, preferred_element_type=f32)
            dc = dc - jnp.dot(dks_ref[p], pick(67, 3, p), precision=hp, preferred_element_type=f32)
        utri = (cc >= r).astype(f32)
        dl = jnp.dot(utri, dc, precision=hp, preferred_element_type=f32) + carry[0:1, :]
        carry[...] = jnp.broadcast_to(dl[0:1, :], (8, 128))
        z = fl_ref[...] + b_ref[...]
        df = dl * jax.nn.sigmoid(-z)
        df_ref[...] = df.astype(bf16)
        db_ref[...] += jnp.sum(df, axis=0, keepdims=True)

    rev = lambda i: (nb - 1 - i, 0)
    return pl.pallas_call(
        body, name="fox_cumsum_bwd", grid=(nb,),
        in_specs=[pl.BlockSpec((PAIRS, 128, 128), lambda i: (0, nb - 1 - i, 0)),
                  pl.BlockSpec((PAIRS, 128, 128), lambda i: (0, nb - 1 - i, 0)),
                  pl.BlockSpec((128, 128), rev), _full((1, 128))],
        out_specs=[pl.BlockSpec((128, 128), rev), _full((1, 128))],
        out_shape=[jax.ShapeDtypeStruct((T, 128), bf16), jax.ShapeDtypeStruct((1, 128), f32)],
        scratch_shapes=[pltpu.VMEM((8, 128), f32)],
        compiler_params=_cp("arbitrary"),
    )(dqs, dks, fl, bfp)


def _inproj_bwd(dq, dk, dv, du, df, wm, wf, x, dx1, g1):
    tm = 512

    def body(dq_ref, dk_ref, dv_ref, du_ref, df_ref, wm_ref, wf_ref, x_ref, dx1_ref, g_ref, dx_ref, dn_ref):
        i = pl.program_id(0)

        @pl.when(i == 0)
        def _():
            dn_ref[...] = jnp.zeros_like(dn_ref)

        dh = lax.dot_general(dq_ref[...], wm_ref[:, 0:AW], NT, preferred_element_type=f32)
        dh = dh + lax.dot_general(dk_ref[...], wm_ref[:, AW:2 * AW], NT, preferred_element_type=f32)
        dh = dh + lax.dot_general(dv_ref[...], wm_ref[:, 2 * AW:3 * AW], NT, preferred_element_type=f32)
        dh = dh + lax.dot_general(du_ref[...], wm_ref[:, 3 * AW:4 * AW], NT, preferred_element_type=f32)
        dh = dh + lax.dot_general(df_ref[...], wf_ref[...], NT, preferred_element_type=f32)
        xv = x_ref[...]
        r = lax.rsqrt(jnp.mean(xv * xv, axis=-1, keepdims=True) + EPS)
        xhat = xv * r
        dn_ref[...] += jnp.sum(dh * xhat, axis=0, keepdims=True)
        z = dh * g_ref[...]
        dx_ref[...] = dx1_ref[...] + r * (z - xhat * jnp.mean(z * xhat, axis=-1, keepdims=True))

    row = lambda i: (i, 0)
    return pl.pallas_call(
        body, name="inproj_bwd", grid=(T // tm,),
        in_specs=[pl.BlockSpec((tm, AW), row)] * 4 + [pl.BlockSpec((tm, 128), row), _full((D, 4 * AW)), _full((D, 128)),
                                                       pl.BlockSpec((tm, D), row), pl.BlockSpec((tm, D), row), _full((1, D))],
        out_specs=[pl.BlockSpec((tm, D), row), _full((1, D))],
        out_shape=[jax.ShapeDtypeStruct((T, D), f32), jax.ShapeDtypeStruct((1, D), f32)],
        compiler_params=_cp("arbitrary"),
    )(dq, dk, dv, du, df, wm, wf, x, dx1, g1)


def _adamw_math(w, g, m, v):
    m = B1 * m + (1.0 - B1) * g
    v = B2 * v + (1.0 - B2) * (g * g)
    m_hat = m / (1.0 - B1 ** STEP)
    v_hat = v / (1.0 - B2 ** STEP)
    delta = -LR * (m_hat / (jnp.sqrt(v_hat) + AEPS) + WD * w)
    return delta, m, v


def _adamw_shard(w, m, v, p_mine, p_other, name):
    rows, cols = w.shape
    tr = 256 if rows % 256 == 0 else 176

    def body(w_ref, m_ref, v_ref, a_ref, b_ref, g_ref, d_ref, nm_ref, nv_ref):
        g = a_ref[...] + b_ref[...]
        g_ref[...] = g
        d_ref[...], nm_ref[...], nv_ref[...] = _adamw_math(w_ref[...], g, m_ref[...], v_ref[...])

    spec = pl.BlockSpec((tr, cols), lambda i: (i, 0))
    return pl.pallas_call(
        body, name=name, grid=(rows // tr,), in_specs=[spec] * 5, out_specs=[spec] * 4,
        out_shape=[jax.ShapeDtypeStruct((rows, cols), f32)] * 4, compiler_params=_cp("parallel"),
    )(w, m, v, p_mine, p_other)


def _adamw_small(w, m, v, parts):
    def body(w_ref, m_ref, v_ref, p_ref, g_ref, d_ref, nm_ref, nv_ref):
        g = p_ref[0]
        for k in range(1, 8):
            g = g + p_ref[k]
        g_ref[...] = g
        d_ref[...], nm_ref[...], nv_ref[...] = _adamw_math(w_ref[...], g, m_ref[...], v_ref[...])

    return pl.pallas_call(
        body, name="adamw_small", out_shape=[jax.ShapeDtypeStruct((SMALL_ROWS, 128), f32)] * 4,
    )(w, m, v, parts)


def _sum4(recv, g, mine, name):
    _, rows, cols = recv.shape
    tr = 256 if rows % 256 == 0 else 176

    def body(mine_ref, r_ref, g_ref, o_ref):
        o_ref[...] = ((g_ref[...].astype(f32) + r_ref[0].astype(f32))
                      + (r_ref[1].astype(f32) + r_ref[2].astype(f32)))

    grid_spec = pltpu.PrefetchScalarGridSpec(
        num_scalar_prefetch=1, grid=(rows // tr,),
        in_specs=[pl.BlockSpec((3, tr, cols), lambda i, m: (0, i, 0)),
                  pl.BlockSpec((None, tr, cols), lambda i, m: (m[0], i, 0))],
        out_specs=pl.BlockSpec((tr, cols), lambda i, m: (i, 0)))
    return pl.pallas_call(
        body, name=name, grid_spec=grid_spec, out_shape=jax.ShapeDtypeStruct((rows, cols), f32),
        compiler_params=_cp("arbitrary"),
    )(mine, recv, g)


_HBM = pl.BlockSpec(memory_space=pltpu.HBM)
_SEM = pl.BlockSpec(memory_space=pltpu.SEMAPHORE)
_EFFECT = pltpu.SideEffectType.DATAFLOW_SIDE_EFFECTING


def _in_hbm(a):
    return pltpu.with_memory_space_constraint(a, pltpu.HBM)


def _mesh_pos():
    return lax.axis_index("x"), lax.axis_index("y"), lax.axis_index("c")


def _other_chips(x, y):
    return [(1 - x, y), (x, 1 - y), (1 - x, 1 - y)]


def _gather_copy(srcs, lands, send_sems, recv_sems, a, k, slot):
    x, y, c = _mesh_pos()
    cx, cy = _other_chips(x, y)[k]
    return pltpu.make_async_remote_copy(
        src_ref=srcs[a], dst_ref=lands[a].at[slot], send_sem=send_sems.at[3 * a + k], recv_sem=recv_sems.at[3 * a + k],
        device_id=(cx, cy, c), device_id_type=MESH)


def _scatter_copy(srcs, lands, send_sems, recv_sems, a, k):
    x, y, c = _mesh_pos()
    cx, cy = _other_chips(x, y)[k]
    return pltpu.make_async_remote_copy(
        src_ref=srcs[a].at[2 * cx + cy], dst_ref=lands[a].at[k], send_sem=send_sems.at[3 * a + k],
        recv_sem=recv_sems.at[3 * a + k], device_id=(cx, cy, c), device_id_type=MESH)


def _all_gather_w_in(part):
    rows = part.shape[0] // 2

    def body(src, dst, send_sems, recv_sems, loc_sem):
        x, y, c = _mesh_pos()
        mine = 2 * x + y
        chips = _other_chips(x, y)
        half = lambda ref, cc: ref.at[pl.ds(pl.multiple_of(cc * rows, rows), rows), :]

        def over_ici(k, slot):
            cx, cy = chips[k]
            return pltpu.make_async_remote_copy(
                src_ref=half(src, c), dst_ref=half(dst.at[slot], c), send_sem=send_sems.at[k], recv_sem=recv_sems.at[k],
                device_id=(cx, cy, c), device_id_type=MESH)

        def to_sibling(k, cc):
            slot = 2 * chips[k][0] + chips[k][1]
            return pltpu.make_async_remote_copy(
                src_ref=half(dst.at[slot], cc), dst_ref=half(dst.at[slot], cc), send_sem=send_sems.at[3 + k],
                recv_sem=recv_sems.at[3 + k], device_id=(x, y, 1 - c), device_id_type=MESH)

        local = pltpu.make_async_copy(src, dst.at[mine], loc_sem.at[0])
        local.start()
        first = [over_ici(k, mine) for k in range(3)]
        for cp in first:
            cp.start()
        passed = [to_sibling(k, c) for k in range(3)]
        for k in range(3):
            over_ici(k, 2 * chips[k][0] + chips[k][1]).wait_recv()
            passed[k].start()
        for k in range(3):
            to_sibling(k, 1 - c).wait_recv()
        for cp in first + passed:
            cp.wait_send()
        local.wait()

    return pl.pallas_call(
        body, name="all_gather_w_in", in_specs=[_HBM], out_specs=_HBM,
        out_shape=jax.ShapeDtypeStruct((NSH,) + part.shape, part.dtype),
        scratch_shapes=[pltpu.SemaphoreType.DMA((6,)), pltpu.SemaphoreType.DMA((6,)), pltpu.SemaphoreType.DMA((1,))],
    )(part)


def _split_start(name, srcs, lands, n_sems, plan, dep):
    n, nl = len(srcs), len(lands)

    def body(*refs):
        src_refs, land_refs = refs[:n], refs[n:n + nl]
        send_sems, recv_sems = refs[n + nl + 1], refs[n + nl + 2]
        token = refs[-1]
        sends, _ = plan(src_refs, land_refs, send_sems, recv_sems)
        for cp in sends:
            cp.start()
        token[...] = jnp.zeros_like(token)

    outs = pl.pallas_call(
        body, name=name,
        in_specs=[_HBM] * (n + nl) + [pl.BlockSpec(memory_space=pl.ANY)],
        out_specs=[_SEM, _SEM] + [_HBM] * (n + nl) + [pl.BlockSpec(memory_space=pltpu.VMEM)],
        out_shape=[pltpu.SemaphoreType.DMA((n_sems,)), pltpu.SemaphoreType.DMA((n_sems,))]
        + [pltpu.HBM(a.shape, a.dtype) for a in list(srcs) + list(lands)] + [jax.ShapeDtypeStruct((8, 128), f32)],
        input_output_aliases={i: 2 + i for i in range(n + nl)},
        compiler_params=pltpu.CompilerParams(has_side_effects=_EFFECT),
    )(*[_in_hbm(a) for a in list(srcs) + list(lands)], dep)
    return outs[0], outs[1], list(outs[2:2 + n]), list(outs[2 + n:2 + n + nl]), outs[-1]


def _split_wait(name, send_sems, recv_sems, srcs, lands, after, plan):
    n, nl = len(srcs), len(lands)

    def body(*refs):
        src_refs, land_refs = refs[:n], refs[n:n + nl]
        s_sems, r_sems = refs[n + nl], refs[n + nl + 1]
        sends, recvs = plan(src_refs, land_refs, s_sems, r_sems)
        for cp in recvs:
            cp.wait_recv()
        for cp in sends:
            cp.wait_send()

    outs = pl.pallas_call(
        body, name=name,
        in_specs=[_HBM] * (n + nl) + [_SEM, _SEM, pl.BlockSpec(memory_space=pl.ANY)],
        out_specs=[_HBM] * (n + nl),
        out_shape=[pltpu.HBM(a.shape, a.dtype) for a in list(srcs) + list(lands)],
        input_output_aliases={i: i for i in range(n + nl)},
        compiler_params=pltpu.CompilerParams(has_side_effects=_EFFECT),
    )(*srcs, *lands, send_sems, recv_sems, after)
    return list(outs[:n]), list(outs[n:])


def _gather_plan(srcs, lands, ss, rs):
    x, y, _ = _mesh_pos()
    chips = _other_chips(x, y)
    sends = [_gather_copy(srcs, lands, ss, rs, a, k, 2 * x + y) for a in range(len(srcs)) for k in range(3)]
    recvs = [_gather_copy(srcs, lands, ss, rs, a, k, 2 * chips[k][0] + chips[k][1])
             for a in range(len(srcs)) for k in range(3)]
    return sends, recvs


def _scatter_plan(srcs, lands, ss, rs):
    cps = [_scatter_copy(srcs, lands, ss, rs, a, k) for a in range(len(srcs)) for k in range(3)]
    return cps, cps


def _tail_plan(srcs, lands, ss, rs):
    x, y, c = _mesh_pos()
    me = 4 * x + 2 * y + c
    cps = [_scatter_copy(srcs[:1], lands[:1], ss, rs, 0, k) for k in range(3)]
    for f in range(1, 8):
        peer = ((x + (f >> 2)) % 2, (y + ((f >> 1) & 1)) % 2, (c + (f & 1)) % 2)
        cps.append(pltpu.make_async_remote_copy(
            src_ref=srcs[1], dst_ref=lands[1].at[me], send_sem=ss.at[2 + f], recv_sem=rs.at[2 + f],
            device_id=peer, device_id_type=MESH))
    return cps, cps


def _swap_with_sibling(parts, name):
    n = len(parts)

    def body(*refs):
        srcs, dsts = refs[:n], refs[n:2 * n]
        send_sems, recv_sems = refs[2 * n:]
        x, y, c = _mesh_pos()
        cps = [pltpu.make_async_remote_copy(src_ref=srcs[a], dst_ref=dsts[a], send_sem=send_sems.at[a],
                                            recv_sem=recv_sems.at[a], device_id=(x, y, 1 - c), device_id_type=MESH)
               for a in range(n)]
        for cp in cps:
            cp.start()
        for cp in cps:
            cp.wait_recv()
        for cp in cps:
            cp.wait_send()

    return pl.pallas_call(
        body, name=name, in_specs=[_HBM] * n, out_specs=[_HBM] * n,
        out_shape=[jax.ShapeDtypeStruct(p.shape, p.dtype) for p in parts],
        scratch_shapes=[pltpu.SemaphoreType.DMA((n,)), pltpu.SemaphoreType.DMA((n,))],
    )(*parts)


def _pack_small(n1, n2, nf, ps, bfv, wp):
    pad8 = lambda r: jnp.pad(r, ((0, 8 - r.shape[0]), (0, 0)))
    return jnp.concatenate([n1.reshape(8, 128), n2.reshape(8, 128), nf.reshape(8, 128), pad8(ps.reshape(4, 128)),
                            pad8(jnp.pad(bfv.reshape(1, 8), ((0, 0), (0, 120)))), wp.reshape(512, 128)], axis=0)


def _unpack_small(p):
    return dict(norm1_g=p[0:8].reshape(1, D), norm2_g=p[8:16].reshape(1, D), final_g=p[16:24].reshape(D),
                pool_scale=p[24:28].reshape(1, AW), b_forget=p[32:33, 0:8], w_pool=p[40:552].reshape(1, 4, 128, 128))


def _forward(x, tgt, wm, wf, mlp_w_fn, g1, bfp, wp, scale, g2, gf, dep):
    h, qkv, u, fl = _rms_inproj(x, g1, wm, wf, dep)
    qaug, kaug = _fox_cumsum(fl, bfp)
    attn, lse = _attn_fwd(qkv, qaug, kaug)
    pooled, pool = _pool_fwd(u, wp, scale)
    wo, wgt, wut, wd = mlp_w_fn(attn)
    x1, h2 = _outproj(x, attn, pool, wo, g2)
    x2, gate, up = _mlp_fwd(h2, x1, wgt, wut, wd)
    loss, dgf, dx2, dx2b = _final_loss(x2, tgt, gf)
    saved = dict(h=h, qkv=qkv, fl=fl, qaug=qaug, kaug=kaug, attn=attn, lse=lse, pooled=pooled, pool=pool, x1=x1, h2=h2,
                 gate=gate, up=up, wo=wo, wgt=wgt, wut=wut, wd=wd)
    return loss, dgf, dx2, dx2b, saved


def _backward_mlp(sv, dx2, dx2b, g2):
    a_b, dgate, dup, dx1, dx1b, dg2 = _mlp_bwd(dx2b, dx2, sv["gate"], sv["up"], sv["wgt"], sv["wut"], sv["wd"], sv["x1"], g2)
    (dwd,) = _mm_tn(a_b, [dx2b], "dw_down", a_sharded=True)
    (dwgt,) = _mm_tn(dgate, [sv["h2"]], "dw_gate", a_sharded=True)
    (dwut,) = _mm_tn(dup, [sv["h2"]], "dw_up", a_sharded=True)
    return dx1, dx1b, dg2, (dwgt, dwut, dwd)


def _backward_outproj(sv, dx1b):
    dattn, dpool = _outproj_bwd(dx1b, sv["wo"])
    dwo_a, = _mm_tn(sv["attn"], [dx1b], "dw_out_attn")
    dwo_p, = _mm_tn(sv["pool"], [dx1b], "dw_out_pool")
    dwo = jnp.concatenate([dwo_a, dwo_p], axis=0).reshape(NSH, D // NSH, D)
    return dattn, dpool, dwo


def _backward_mixer(sv, x, dx1, dattn, dpool, wm, wf, g1, bfp, wp, scale, dep):
    du, dscale, dwp = _pool_bwd(dpool, sv["pooled"], wp, scale, dep)
    dq, dqs, dk, dks, dv = _attn_bwd(sv["qkv"], sv["qaug"], sv["kaug"], sv["attn"], dattn, sv["lse"], dep)
    df, dbf = _fox_cumsum_bwd(dqs, dks, sv["fl"], bfp)
    dx, dg1 = _inproj_bwd(dq, dk, dv, du, df, wm, wf, x, dx1, g1)
    dwq, dwk, dwv, dwu_in, dwf = _mm_tn(sv["h"], [dq, dk, dv, du, df], "dw_in")
    dwin = jnp.concatenate([dwq, dwk, dwv, dwf[:, 0:8], dwu_in], axis=1)
    dwin = dwin.reshape(D, NSH, IN_S).transpose(1, 0, 2)
    return dx, dg1, dscale, dwp, dbf, dwin


def kernel(x, norm1_g, w_in, b_forget, w_pool, pool_scale, w_out, norm2_g, w_gate, w_up, w_down, final_g, loss_target, m_norm1_g, m_w_in, m_b_forget, m_w_pool, m_pool_scale, m_w_out, m_norm2_g, m_w_gate, m_w_up, m_w_down, m_final_g, v_norm1_g, v_w_in, v_b_forget, v_w_pool, v_pool_scale, v_w_out, v_norm2_g, v_w_gate, v_w_up, v_w_down, v_final_g):
    mine = (2 * lax.axis_index("x") + lax.axis_index("y")).astype(jnp.int32)
    mine1 = mine.reshape(1)
    tr = lambda a: jnp.transpose(a[0])

    win4 = _all_gather_w_in(w_in[0].astype(bf16))
    later = [w_out[0].astype(bf16), tr(w_gate).astype(bf16), tr(w_up).astype(bf16), w_down[0].astype(bf16)]
    lands = [lax.dynamic_update_slice(lax.empty((NSH,) + p.shape, bf16), p[None], (mine, 0, 0)) for p in later]
    ag_send, ag_recv, later_thru, lands_thru, ag_token = _split_start("all_gather_start", later, lands, 12, _gather_plan,
                                                                      win4)
    win = win4.transpose(1, 0, 2).reshape(D, IN_W)
    wm = jnp.concatenate([win[:, 0:3 * AW], win[:, 3 * AW + 8:]], axis=1)
    wf = jnp.pad(win[:, 3 * AW:3 * AW + 8], ((0, 0), (0, 120)))
    bfp = jnp.pad(b_forget, ((0, 0), (0, 120)))
    wp = w_pool[0].astype(bf16)
    gf = final_g.reshape(1, D)

    def later_weights(after):
        _, (wo4, wgt, wut, wd) = _split_wait("all_gather_wait", ag_send, ag_recv, later_thru, lands_thru, after, _gather_plan)
        return wo4.reshape(D, D), wgt, wut, wd

    xe, tgt = x[0], loss_target[0]
    loss_v, dgf, dx2, dx2b, sv = _forward(xe, tgt, wm, wf, later_weights, norm1_g, bfp, wp, pool_scale, norm2_g, gf, ag_token)
    dx1, dx1b, dg2, mlp_grads = _backward_mlp(sv, dx2, dx2b, norm2_g)
    dattn, dpool, dwo = _backward_outproj(sv, dx1b)
    first = [dwo] + list(mlp_grads)
    first_lands = [lax.empty((3,) + g.shape[1:], bf16) for g in first]
    rs_send, rs_recv, first_thru, first_lands_thru, rs_token = _split_start("reduce_scatter_start", first, first_lands, 12,
                                                                            _scatter_plan, dattn)
    dx, dg1, dscale, dwp, dbf, dwin = _backward_mixer(sv, xe, dx1, dattn, dpool, wm, wf, norm1_g, bfp, wp, pool_scale, rs_token)

    me = (4 * lax.axis_index("x") + 2 * lax.axis_index("y") + lax.axis_index("c")).astype(jnp.int32)
    pad8 = lambda r: jnp.pad(r, ((0, 8 - r.shape[0]), (0, 0)))
    loss_rows = jnp.concatenate([dbf, jnp.zeros((6, 128), f32), loss_v[0:1, :]], axis=0)
    small = jnp.concatenate([dg1.reshape(8, 128), dg2.reshape(8, 128), dgf.reshape(8, 128), pad8(dscale.reshape(4, 128)),
                             loss_rows, dwp.reshape(512, 128)], axis=0)
    small_land = lax.dynamic_update_slice(lax.empty((8, SMALL_ROWS, 128), f32), small[None], (me, 0, 0))
    tail_send, tail_recv, tail_thru, tail_lands_thru, tail_token = _split_start(
        "tail_start", [dwin, small], [lax.empty((3,) + dwin.shape[1:], bf16), small_land], 10, _tail_plan, dx)
    first_thru, first_recv = _split_wait("reduce_scatter_wait", rs_send, rs_recv, first_thru, first_lands_thru, tail_token,
                                         _scatter_plan)
    ws = [w_in[0], w_out[0], tr(w_gate), tr(w_up), w_down[0]]
    ms = [m_w_in[0], m_w_out[0], tr(m_w_gate), tr(m_w_up), m_w_down[0]]
    vs = [v_w_in[0], v_w_out[0], tr(v_w_gate), tr(v_w_up), v_w_down[0]]
    partial = [_sum4(r, g, mine1, f"sum4_{i + 1}") for i, (r, g) in enumerate(zip(first_recv, first_thru))]
    other = _swap_with_sibling(partial, "swap_first")
    big = [_adamw_shard(ws[i + 1], ms[i + 1], vs[i + 1], partial[i], other[i], f"adamw_{i + 1}") for i in range(4)]
    (dwin_thru, _), (in_recv_land, small_all) = _split_wait("tail_wait", tail_send, tail_recv, tail_thru, tail_lands_thru,
                                                            big[3][0], _tail_plan)
    partial_in = _sum4(in_recv_land, dwin_thru, mine1, "sum4_0")
    (other_in,) = _swap_with_sibling([partial_in], "swap_in")
    big = [_adamw_shard(ws[0], ms[0], vs[0], partial_in, other_in, "adamw_0")] + big

    sm = _adamw_small(_pack_small(norm1_g, norm2_g, final_g, pool_scale, b_forget, w_pool),
                      _pack_small(m_norm1_g, m_norm2_g, m_final_g, m_pool_scale, m_b_forget, m_w_pool),
                      _pack_small(v_norm1_g, v_norm2_g, v_final_g, v_pool_scale, v_b_forget, v_w_pool), small_all)

    loss = sm[0][39, 0]
    order =["norm1_g", "w_in", "b_forget", "w_pool", "pool_scale", "w_out", "norm2_g", "w_gate", "w_up", "w_down", "final_g"]
    big_idx = {"w_in": 0, "w_out": 1, "w_gate": 2, "w_up": 3, "w_down": 4}
    outs = [loss, dx[None]]
    for kind in range(4):
        small_k = _unpack_small(sm[kind])
        for name in order:
            if name in ("w_gate", "w_up"):
                outs.append(jnp.transpose(big[big_idx[name]][kind])[None])
            elif name in big_idx:
                outs.append(big[big_idx[name]][kind][None])
            else:
                outs.append(small_k[name])
    return tuple(outs)
```

```python
import functools

import jax
import jax.numpy as jnp
import numpy as np
from jax import lax
from jax.experimental import pallas as pl
from jax.experimental.pallas import tpu as pltpu

f32 = jnp.float32
bf16 = jnp.bfloat16

T = 4096
D = 1024
NSH = 4
IN_W = 2056
IN_S = IN_W // NSH
AW = 512
PAIRS = 4
FF = 2816
FS = FF // NSH
WINDOWS = (2, 4, 8, 16)
HALO = 16
EPS = 1e-6
NEG = -1e30
LR, B1, B2, AEPS, WD, STEP = 0.001, 0.9, 0.999, 1e-08, 0.01, 10
SMALL_ROWS = 552

NT = (((1,), (1,)), ((), ()))
TN = (((0,), (0,)), ((), ()))

MESH = pl.DeviceIdType.MESH


def _cp(*sem):
    return pltpu.CompilerParams(dimension_semantics=sem)


def _full(shape):
    n = len(shape)
    return pl.BlockSpec(shape, lambda *_: (0,) * n)


def _rms_inproj(x, g1, wm, wf, dep):
    tm = 512

    def body(x_ref, g_ref, wm_ref, wf_ref, dep_ref, h_ref, qkv_ref, u_ref, fl_ref):
        xv = x_ref[...]
        r = lax.rsqrt(jnp.mean(xv * xv, axis=-1, keepdims=True) + EPS)
        h = (xv * r * g_ref[...]).astype(bf16)
        h_ref[...] = h
        qkv_ref[...] = jnp.dot(h, wm_ref[:, 0:3 * AW], preferred_element_type=f32).astype(bf16)
        u_ref[...] = jnp.dot(h, wm_ref[:, 3 * AW:4 * AW], preferred_element_type=f32)
        fl_ref[...] = jnp.dot(h, wf_ref[...], preferred_element_type=f32)

    return pl.pallas_call(
        body, name="rms_inproj", grid=(T // tm,),
        in_specs=[pl.BlockSpec((tm, D), lambda i: (i, 0)), _full((1, D)), _full((D, 4 * AW)), _full((D, 128)),
                  _full((8, 128))],
        out_specs=[pl.BlockSpec((tm, D), lambda i: (i, 0)), pl.BlockSpec((tm, 3 * AW), lambda i: (i, 0)),
                   pl.BlockSpec((tm, AW), lambda i: (i, 0)), pl.BlockSpec((tm, 128), lambda i: (i, 0))],
        out_shape=[jax.ShapeDtypeStruct((T, D), bf16), jax.ShapeDtypeStruct((T, 3 * AW), bf16),
                   jax.ShapeDtypeStruct((T, AW), f32), jax.ShapeDtypeStruct((T, 128), f32)],
        compiler_params=_cp("parallel"),
    )(x, g1, wm, wf, dep)


CUMSUM_ROWS = 512
FS_CHUNKS = ((0, 256), (256, 512), (512, FS))


def _log_sigmoid(z):
    return jnp.minimum(z, 0.0) - jnp.log(1.0 + jnp.exp(-jnp.abs(z)))


def _fox_cumsum(fl, bfp):
    tb = CUMSUM_ROWS
    nb = T // tb

    def body(fl_ref, b_ref, qa_ref, ka_ref, carry):
        i = pl.program_id(0)

        @pl.when(i == 0)
        def _():
            carry[...] = jnp.zeros_like(carry)

        lf = _log_sigmoid(fl_ref[...] + b_ref[...])
        r = lax.broadcasted_iota(jnp.int32, (tb, tb), 0)
        cc = lax.broadcasted_iota(jnp.int32, (tb, tb), 1)
        ltri = (cc <= r).astype(f32)
        cb = jnp.dot(ltri, lf, precision=lax.Precision.HIGHEST, preferred_element_type=f32) + carry[0:1, :]
        carry[...] = jnp.broadcast_to(cb[tb - 1:tb, :], (8, 128))
        hi = cb.astype(bf16)
        r1 = cb - hi.astype(f32)
        mid = r1.astype(bf16)
        lo = (r1 - mid.astype(f32)).astype(bf16)
        head = lax.broadcasted_iota(jnp.int32, (128, AW), 0)
        col = lax.broadcasted_iota(jnp.int32, (128, AW), 1)
        base = 128 * (head >> 1) + 64 * (1 - (head & 1))
        place = lambda off: jnp.logical_and(col == base + off, head < 8).astype(bf16)
        mm = lambda a, off: jnp.dot(a, place(off), preferred_element_type=f32)
        cq = mm(hi, 0) + mm(mid, 1) + mm(lo, 2)
        ck = mm(hi, 3) + mm(mid, 4) + mm(lo, 5)
        within = jnp.bitwise_and(lax.broadcasted_iota(jnp.int32, (tb, AW), 1), 63)
        qa_ref[...] = jnp.where(jnp.logical_and(within >= 3, within <= 5), 1.0, cq).astype(bf16)
        ka_ref[...] = jnp.where(within <= 2, 1.0, -ck).astype(bf16)

    return pl.pallas_call(
        body, name="fox_cumsum", grid=(nb,),
        in_specs=[pl.BlockSpec((tb, 128), lambda i: (i, 0)), _full((1, 128))],
        out_specs=[pl.BlockSpec((tb, AW), lambda i: (i, 0)), pl.BlockSpec((tb, AW), lambda i: (i, 0))],
        out_shape=[jax.ShapeDtypeStruct((T, AW), bf16), jax.ShapeDtypeStruct((T, AW), bf16)],
        scratch_shapes=[pltpu.VMEM((8, 128), f32)],
        compiler_params=_cp("arbitrary"),
    )(fl, bfp)


ATT_T = 512


def _causal_steps(key_major):
    n = T // ATT_T
    if key_major:
        pairs = [(i, j) for j in range(n) for i in range(j, n)]
    else:
        pairs = [(i, j) for i in range(n) for j in range(i + 1)]
    it = np.array([p[0] for p in pairs], np.int32)
    jt = np.array([p[1] for p in pairs], np.int32)
    return jnp.asarray(it), jnp.asarray(jt)


def _attn_fwd(qkv, qaug, kaug):
    tq = tk = ATT_T
    it, jt = _causal_steps(False)
    nsteps = it.shape[0]

    rs = 64

    def body(it_ref, jt_ref, q_ref, k_ref, v_ref, qa_ref, ka_ref, o_ref, lse_ref, m_sc, acc_sc, s_sc, p_sc, alpha_sc):
        t = pl.program_id(1)
        i = it_ref[t]
        j = jt_ref[t]

        @pl.when(j == 0)
        def _():
            m_sc[...] = jnp.full_like(m_sc, NEG)
            acc_sc[...] = jnp.zeros_like(acc_sc)

        lane = lax.broadcasted_iota(jnp.int32, (tq, 128), 1)
        spare = (64, 0)

        def step(on_diagonal):
            q = q_ref[...] * 0.125
            k = k_ref[...]
            v = v_ref[...]
            qa = qa_ref[...]
            ka = ka_ref[...]
            for e in range(2):
                hm = (lane >= 64) if e else (lane < 64)
                s_sc[...] = lax.dot_general(jnp.where(hm, q, qa), jnp.where(hm, k, ka), NT, preferred_element_type=f32)
                for r in range(0, tq, rs):
                    s = s_sc[r:r + rs, :]
                    if on_diagonal:
                        row = lax.broadcasted_iota(jnp.int32, (rs, tk), 0) + r
                        col = lax.broadcasted_iota(jnp.int32, (rs, tk), 1)
                        s = jnp.where(col <= row, s, NEG)
                    m_prev = m_sc[e, r:r + rs, :]
                    m_new = jnp.maximum(m_prev, jnp.max(s, axis=1, keepdims=True))
                    p_sc[r:r + rs, :] = jnp.exp(s - jnp.tile(m_new, (1, tk // 128))).astype(bf16)
                    alpha_sc[r:r + rs, :] = jnp.exp(m_prev - m_new)
                    m_sc[e, r:r + rs, :] = m_new
                ve = jnp.where(hm, v, (lane == spare[e]).astype(bf16))
                acc_sc[e] = alpha_sc[...] * acc_sc[e] + jnp.dot(p_sc[...], ve, preferred_element_type=f32)

        @pl.when(j < i)
        def _():
            step(False)

        @pl.when(j == i)
        def _():
            step(True)
            l0 = acc_sc[0][:, spare[0]:spare[0] + 1]
            l1 = acc_sc[1][:, spare[1]:spare[1] + 1]
            o_ref[...] = jnp.where(lane < 64, acc_sc[0] / l0, acc_sc[1] / l1).astype(bf16)
            lse_ref[...] = jnp.where(lane < 64, m_sc[0] + jnp.log(l0), m_sc[1] + jnp.log(l1))

    qmap = lambda p, t, it, jt: (it[t], p)
    kmap = lambda p, t, it, jt: (jt[t], p)
    grid_spec = pltpu.PrefetchScalarGridSpec(
        num_scalar_prefetch=2, grid=(PAIRS, nsteps),
        in_specs=[pl.BlockSpec((tq, 128), qmap),
                  pl.BlockSpec((tk, 128), lambda p, t, it, jt: (jt[t], PAIRS + p)),
                  pl.BlockSpec((tk, 128), lambda p, t, it, jt: (jt[t], 2 * PAIRS + p)),
                  pl.BlockSpec((tq, 128), qmap), pl.BlockSpec((tk, 128), kmap)],
        out_specs=[pl.BlockSpec((tq, 128), qmap),
                   pl.BlockSpec((None, tq, 128), lambda p, t, it, jt: (p, it[t], 0))],
        scratch_shapes=[pltpu.VMEM((2, tq, 128), f32), pltpu.VMEM((2, tq, 128), f32), pltpu.VMEM((tq, tk), f32),
                        pltpu.VMEM((tq, tk), bf16), pltpu.VMEM((tq, 128), f32)],
    )
    return pl.pallas_call(
        body, name="fox_attn_fwd", grid_spec=grid_spec,
        out_shape=[jax.ShapeDtypeStruct((T, AW), bf16), jax.ShapeDtypeStruct((PAIRS, T, 128), f32)],
        compiler_params=_cp("parallel", "arbitrary"),
    )(it, jt, qkv, qkv, qkv, qaug, kaug)


def _pool_fwd(u, wp, scale):
    tm = 512

    def body(u_ref, wp_ref, sc_ref, pooled_ref, pool_ref, ext):
        i = pl.program_id(0)

        @pl.when(i == 0)
        def _():
            ext[0:HALO, :] = jnp.zeros((HALO, AW), f32)

        uv = u_ref[...]
        ext[HALO:HALO + tm, :] = uv
        t_idx = i * tm + lax.broadcasted_iota(jnp.int32, (tm, 1), 0)
        for g, w in enumerate(WINDOWS):
            lo, hi = 128 * g, 128 * (g + 1)
            ug = uv[:, lo:hi]
            acc = ug
            for d in range(1, w):
                acc = acc + ext[HALO - d:HALO - d + tm, lo:hi]
            cnt = jnp.minimum(t_idx + 1, w).astype(f32)
            pb = (acc / cnt - ug).astype(bf16)
            pooled_ref[:, lo:hi] = pb
            mixed = jnp.dot(pb, wp_ref[g], preferred_element_type=f32)
            pool_ref[:, lo:hi] = (mixed * sc_ref[:, lo:hi]).astype(bf16)
        ext[0:HALO, :] = uv[tm - HALO:tm, :]

    return pl.pallas_call(
        body, name="pool_fwd", grid=(T // tm,),
        in_specs=[pl.BlockSpec((tm, AW), lambda i: (i, 0)), _full((4, 128, 128)), _full((1, AW))],
        out_specs=[pl.BlockSpec((tm, AW), lambda i: (i, 0)), pl.BlockSpec((tm, AW), lambda i: (i, 0))],
        out_shape=[jax.ShapeDtypeStruct((T, AW), bf16), jax.ShapeDtypeStruct((T, AW), bf16)],
        scratch_shapes=[pltpu.VMEM((tm + HALO, AW), f32)],
        compiler_params=_cp("arbitrary"),
    )(u, wp, scale)


def _outproj(x, attn, pool, wo, g2):
    tm = 512

    def body(x_ref, a_ref, p_ref, wo_ref, g_ref, x1_ref, h2_ref):
        x1 = x_ref[...] + jnp.dot(a_ref[...], wo_ref[0:AW, :], preferred_element_type=f32)
        x1 = x1 + jnp.dot(p_ref[...], wo_ref[AW:2 * AW, :], preferred_element_type=f32)
        x1_ref[...] = x1
        r = lax.rsqrt(jnp.mean(x1 * x1, axis=-1, keepdims=True) + EPS)
        h2_ref[...] = (x1 * r * g_ref[...]).astype(bf16)

    return pl.pallas_call(
        body, name="outproj", grid=(T // tm,),
        in_specs=[pl.BlockSpec((tm, D), lambda i: (i, 0)), pl.BlockSpec((tm, AW), lambda i: (i, 0)),
                  pl.BlockSpec((tm, AW), lambda i: (i, 0)), _full((D, D)), _full((1, D))],
        out_specs=[pl.BlockSpec((tm, D), lambda i: (i, 0)), pl.BlockSpec((tm, D), lambda i: (i, 0))],
        out_shape=[jax.ShapeDtypeStruct((T, D), f32), jax.ShapeDtypeStruct((T, D), bf16)],
        compiler_params=_cp("parallel"),
    )(x, attn, pool, wo, g2)


def _mlp_fwd(h2, x1, wg, wu, wd):
    tm = 512

    def body(h_ref, x1_ref, wg_ref, wu_ref, wd_ref, x2_ref, gate_ref, up_ref):
        s = pl.program_id(1)
        h = h_ref[...]
        part = None
        for c0, c1 in FS_CHUNKS:
            gate = lax.dot_general(h, wg_ref[c0:c1, :], NT, preferred_element_type=f32)
            up = lax.dot_general(h, wu_ref[c0:c1, :], NT, preferred_element_type=f32)
            gate_ref[:, c0:c1] = gate
            up_ref[:, c0:c1] = up
            a = (gate * jax.nn.sigmoid(gate) * up).astype(bf16)
            pc = jnp.dot(a, wd_ref[c0:c1, :], preferred_element_type=f32)
            part = pc if part is None else part + pc

        @pl.when(s == 0)
        def _():
            x2_ref[...] = x1_ref[...] + part

        @pl.when(s > 0)
        def _():
            x2_ref[...] += part

    return pl.pallas_call(
        body, name="mlp_fwd", grid=(T // tm, NSH),
        in_specs=[pl.BlockSpec((tm, D), lambda i, s: (i, 0)), pl.BlockSpec((tm, D), lambda i, s: (i, 0)),
                  pl.BlockSpec((None, FS, D), lambda i, s: (s, 0, 0)), pl.BlockSpec((None, FS, D), lambda i, s: (s, 0, 0)),
                  pl.BlockSpec((None, FS, D), lambda i, s: (s, 0, 0))],
        out_specs=[pl.BlockSpec((tm, D), lambda i, s: (i, 0)), pl.BlockSpec((None, tm, FS), lambda i, s: (s, i, 0)),
                   pl.BlockSpec((None, tm, FS), lambda i, s: (s, i, 0))],
        out_shape=[jax.ShapeDtypeStruct((T, D), f32), jax.ShapeDtypeStruct((NSH, T, FS), f32),
                   jax.ShapeDtypeStruct((NSH, T, FS), f32)],
        compiler_params=_cp("parallel", "arbitrary"),
    )(h2, x1, wg, wu, wd)


def _final_loss(x2, tgt, gf):
    tm = 512

    def body(x_ref, t_ref, g_ref, loss_ref, dg_ref, dx_ref, dxb_ref):
        i = pl.program_id(0)

        @pl.when(i == 0)
        def _():
            loss_ref[...] = jnp.zeros_like(loss_ref)
            dg_ref[...] = jnp.zeros_like(dg_ref)

        xv = x_ref[...]
        g = g_ref[...]
        r = lax.rsqrt(jnp.mean(xv * xv, axis=-1, keepdims=True) + EPS)
        xhat = xv * r
        e = xhat * g - t_ref[...]
        loss_ref[...] += 0.5 * jnp.sum(jnp.mean(e * e, axis=-1, keepdims=True))
        dy = e * (1.0 / D)
        dg_ref[...] += jnp.sum(dy * xhat, axis=0, keepdims=True)
        z = dy * g
        dx = r * (z - xhat * jnp.mean(z * xhat, axis=-1, keepdims=True))
        dx_ref[...] = dx
        dxb_ref[...] = dx.astype(bf16)

    return pl.pallas_call(
        body, name="final_loss", grid=(T // tm,),
        in_specs=[pl.BlockSpec((tm, D), lambda i: (i, 0)), pl.BlockSpec((tm, D), lambda i: (i, 0)), _full((1, D))],
        out_specs=[_full((8, 128)), _full((1, D)), pl.BlockSpec((tm, D), lambda i: (i, 0)),
                   pl.BlockSpec((tm, D), lambda i: (i, 0))],
        out_shape=[jax.ShapeDtypeStruct((8, 128), f32), jax.ShapeDtypeStruct((1, D), f32),
                   jax.ShapeDtypeStruct((T, D), f32), jax.ShapeDtypeStruct((T, D), bf16)],
        compiler_params=_cp("arbitrary"),
    )(x2, tgt, gf)


def _mlp_bwd(dx2b, dx2, gate, up, wg, wu, wd, x1, g2):
    tm = 512

    def body(dxb_ref, dx_ref, gate_ref, up_ref, wg_ref, wu_ref, wd_ref, x1_ref, g_ref,
             a_ref, dg_ref, du_ref, dx1_ref, dx1b_ref, dn_ref, acc):
        i = pl.program_id(0)
        s = pl.program_id(1)

        @pl.when(jnp.logical_and(i == 0, s == 0))
        def _():
            dn_ref[...] = jnp.zeros_like(dn_ref)

        dxb = dxb_ref[...]
        part = None
        for c0, c1 in FS_CHUNKS:
            da = lax.dot_general(dxb, wd_ref[c0:c1, :], NT, preferred_element_type=f32)
            gate = gate_ref[:, c0:c1]
            upv = up_ref[:, c0:c1]
            sg = jax.nn.sigmoid(gate)
            silu = gate * sg
            a_ref[:, c0:c1] = (silu * upv).astype(bf16)
            dgate = (da * upv * (sg * (1.0 + gate * (1.0 - sg)))).astype(bf16)
            dup = (da * silu).astype(bf16)
            dg_ref[:, c0:c1] = dgate
            du_ref[:, c0:c1] = dup
            pc = jnp.dot(dgate, wg_ref[c0:c1, :], preferred_element_type=f32)
            pc = pc + jnp.dot(dup, wu_ref[c0:c1, :], preferred_element_type=f32)
            part = pc if part is None else part + pc

        @pl.when(s == 0)
        def _():
            acc[...] = part

        @pl.when(s > 0)
        def _():
            acc[...] += part

        @pl.when(s == NSH - 1)
        def _():
            xv = x1_ref[...]
            r = lax.rsqrt(jnp.mean(xv * xv, axis=-1, keepdims=True) + EPS)
            xhat = xv * r
            dh = acc[...]
            dn_ref[...] += jnp.sum(dh * xhat, axis=0, keepdims=True)
            z = dh * g_ref[...]
            dx1 = dx_ref[...] + r * (z - xhat * jnp.mean(z * xhat, axis=-1, keepdims=True))
            dx1_ref[...] = dx1
            dx1b_ref[...] = dx1.astype(bf16)

    row = lambda i, s: (i, 0)
    sl = lambda i, s: (s, i, 0)
    wsl = lambda i, s: (s, 0, 0)
    return pl.pallas_call(
        body, name="mlp_bwd", grid=(T // tm, NSH),
        in_specs=[pl.BlockSpec((tm, D), row), pl.BlockSpec((tm, D), row),
                  pl.BlockSpec((None, tm, FS), sl), pl.BlockSpec((None, tm, FS), sl),
                  pl.BlockSpec((None, FS, D), wsl), pl.BlockSpec((None, FS, D), wsl), pl.BlockSpec((None, FS, D), wsl),
                  pl.BlockSpec((tm, D), row), pl.BlockSpec((1, D), lambda i, s: (0, 0))],
        out_specs=[pl.BlockSpec((None, tm, FS), sl), pl.BlockSpec((None, tm, FS), sl), pl.BlockSpec((None, tm, FS), sl),
                   pl.BlockSpec((tm, D), row), pl.BlockSpec((tm, D), row), pl.BlockSpec((1, D), lambda i, s: (0, 0))],
        out_shape=[jax.ShapeDtypeStruct((NSH, T, FS), bf16)] * 3
        + [jax.ShapeDtypeStruct((T, D), f32), jax.ShapeDtypeStruct((T, D), bf16), jax.ShapeDtypeStruct((1, D), f32)],
        scratch_shapes=[pltpu.VMEM((tm, D), f32)],
        compiler_params=_cp("arbitrary", "arbitrary"),
    )(dx2b, dx2, gate, up, wg, wu, wd, x1, g2)


def _mm_tn(a, bs, name, a_sharded=False, b_sharded=False, tk=512, out_dtype=bf16):
    nb = len(bs)
    sh = NSH if (a_sharded or b_sharded) else 1
    m = a.shape[-1]
    nk = T // tk

    def body(a_ref, *refs):
        kk = pl.program_id(1)
        av = a_ref[...]
        for b_ref, o_ref, acc in zip(refs[:nb], refs[nb:2 * nb], refs[2 * nb:]):
            upd = lax.dot_general(av, b_ref[...], TN, preferred_element_type=f32)

            @pl.when(kk == 0)
            def _():
                acc[...] = upd

            @pl.when(kk > 0)
            def _():
                acc[...] += upd

            @pl.when(kk == nk - 1)
            def _():
                o_ref[...] = acc[...].astype(out_dtype)

    a_spec = (pl.BlockSpec((None, tk, m), lambda s, k: (s, k, 0)) if a_sharded
              else pl.BlockSpec((tk, m), lambda s, k: (k, 0)))
    b_specs, o_specs, o_shapes, scratch = [], [], [], []
    for b in bs:
        n = b.shape[-1]
        b_specs.append(pl.BlockSpec((None, tk, n), lambda s, k: (s, k, 0)) if b_sharded
                       else pl.BlockSpec((tk, n), lambda s, k: (k, 0)))
        scratch.append(pltpu.VMEM((m, n), f32))
        if sh > 1:
            o_specs.append(pl.BlockSpec((None, m, n), lambda s, k: (s, 0, 0)))
            o_shapes.append(jax.ShapeDtypeStruct((sh, m, n), out_dtype))
        else:
            o_specs.append(pl.BlockSpec((m, n), lambda s, k: (0, 0)))
            o_shapes.append(jax.ShapeDtypeStruct((m, n), out_dtype))
    return pl.pallas_call(
        body, name=name, grid=(sh, nk), in_specs=[a_spec] + b_specs, out_specs=o_specs, out_shape=o_shapes,
        scratch_shapes=scratch, compiler_params=_cp("arbitrary", "arbitrary"),
    )(a, *bs)


def _outproj_bwd(dx1b, wo):
    tm = 512

    def body(dx_ref, wo_ref, da_ref, dp_ref):
        dx = dx_ref[...]
        da_ref[...] = lax.dot_general(dx, wo_ref[0:AW, :], NT, preferred_element_type=f32).astype(bf16)
        dp_ref[...] = lax.dot_general(dx, wo_ref[AW:2 * AW, :], NT, preferred_element_type=f32)

    return pl.pallas_call(
        body, name="outproj_bwd", grid=(T // tm,),
        in_specs=[pl.BlockSpec((tm, D), lambda i: (i, 0)), _full((D, D))],
        out_specs=[pl.BlockSpec((tm, AW), lambda i: (i, 0)), pl.BlockSpec((tm, AW), lambda i: (i, 0))],
        out_shape=[jax.ShapeDtypeStruct((T, AW), bf16), jax.ShapeDtypeStruct((T, AW), f32)],
        compiler_params=_cp("parallel"),
    )(dx1b, wo)


def _pool_bwd(dpool, pooled, wp, scale, dep):
    tm = 512
    n = T // tm

    def body(dp_ref, pb_ref, wp_ref, sc_ref, dep_ref, du_ref, dsc_ref, dwp_ref, ext):
        i = pl.program_id(0)

        @pl.when(i == 0)
        def _():
            ext[tm:tm + HALO, :] = jnp.zeros((HALO, AW), f32)
            dsc_ref[...] = jnp.zeros_like(dsc_ref)
            dwp_ref[...] = jnp.zeros_like(dwp_ref)

        t_idx = (n - 1 - i) * tm + lax.broadcasted_iota(jnp.int32, (tm, 1), 0)
        for g, w in enumerate(WINDOWS):
            lo, hi = 128 * g, 128 * (g + 1)
            pb = pb_ref[:, lo:hi]
            mixed = jnp.dot(pb, wp_ref[g], preferred_element_type=f32)
            dpo = dp_ref[:, lo:hi]
            dsc_ref[:, lo:hi] += jnp.sum(dpo * mixed, axis=0, keepdims=True)
            dmr = (dpo * sc_ref[:, lo:hi]).astype(bf16)
            dwp_ref[g] += lax.dot_general(pb, dmr, TN, preferred_element_type=f32)
            dpl = lax.dot_general(dmr, wp_ref[g], NT, preferred_element_type=f32)
            cnt = jnp.minimum(t_idx + 1, w).astype(f32)
            dpn = dpl / cnt
            ext[0:tm, lo:hi] = dpn
            acc = dpn
            for d in range(1, w):
                acc = acc + ext[d:d + tm, lo:hi]
            du_ref[:, lo:hi] = (acc - dpl).astype(bf16)
        ext[tm:tm + HALO, :] = ext[0:HALO, :]

    rev = lambda i: (n - 1 - i, 0)
    return pl.pallas_call(
        body, name="pool_bwd", grid=(n,),
        in_specs=[pl.BlockSpec((tm, AW), rev), pl.BlockSpec((tm, AW), rev), _full((4, 128, 128)), _full((1, AW)),
                  _full((8, 128))],
        out_specs=[pl.BlockSpec((tm, AW), rev), _full((1, AW)), _full((4, 128, 128))],
        out_shape=[jax.ShapeDtypeStruct((T, AW), bf16), jax.ShapeDtypeStruct((1, AW), f32),
                   jax.ShapeDtypeStruct((4, 128, 128), f32)],
        scratch_shapes=[pltpu.VMEM((tm + HALO, AW), f32)],
        compiler_params=_cp("arbitrary"),
    )(dpool, pooled, wp, scale, dep)


def _attn_bwd(qkv, qaug, kaug, attn, dattn, lse, dep):
    tq = tk = ATT_T
    n = T // tq
    it, jt = _causal_steps(True)
    nsteps = it.shape[0]

    rs = 64

    def body(it_ref, jt_ref, q_ref, k_ref, v_ref, qa_ref, ka_ref, o_ref, do_ref, lse_ref, dep_ref,
             dq_ref, dqs_ref, dk_ref, dks_ref, dv_ref, dq_acc, dk_acc, dv_acc, s_sc, dp_sc, p_sc, ds_sc):
        t = pl.program_id(1)
        i = it_ref[t]
        j = jt_ref[t]

        @pl.when(t == 0)
        def _():
            dq_acc[...] = jnp.zeros_like(dq_acc)

        @pl.when(i == j)
        def _():
            dk_acc[...] = jnp.zeros_like(dk_acc)
            dv_acc[...] = jnp.zeros_like(dv_acc)

        lane = lax.broadcasted_iota(jnp.int32, (tq, 128), 1)

        def step(on_diagonal):
            q = q_ref[...] * 0.125
            k = k_ref[...]
            v = v_ref[...]
            qa = qa_ref[...]
            ka = ka_ref[...]
            do = do_ref[...]
            dd = do.astype(f32) * o_ref[...].astype(f32)
            r0 = pl.multiple_of(i * tq, tq)
            for e in range(2):
                hm = (lane >= 64) if e else (lane < 64)
                qe = jnp.where(hm, q, qa)
                ke = jnp.where(hm, k, ka)
                doe = jnp.where(hm, do, jnp.zeros_like(do))
                delta = jnp.sum(jnp.where(hm, dd, 0.0), axis=1, keepdims=True)
                s_sc[...] = lax.dot_general(qe, ke, NT, preferred_element_type=f32)
                dp_sc[...] = lax.dot_general(doe, v, NT, preferred_element_type=f32)
                for r in range(0, tq, rs):
                    s = s_sc[r:r + rs, :] - lse_ref[r:r + rs, 64 * e:64 * e + 1]
                    if on_diagonal:
                        row = lax.broadcasted_iota(jnp.int32, (rs, tk), 0) + r
                        col = lax.broadcasted_iota(jnp.int32, (rs, tk), 1)
                        s = jnp.where(col <= row, s, NEG)
                    p = jnp.exp(s)
                    p_sc[r:r + rs, :] = p.astype(bf16)
                    ds_sc[r:r + rs, :] = (p * (dp_sc[r:r + rs, :] - delta[r:r + rs, :])).astype(bf16)
                dv_acc[...] += lax.dot_general(p_sc[...], doe, TN, preferred_element_type=f32)
                dsb = ds_sc[...]
                dk_acc[e] += lax.dot_general(dsb, qe, TN, preferred_element_type=f32)
                dq_acc[e, pl.ds(r0, tq), :] += jnp.dot(dsb, ke, preferred_element_type=f32)

        @pl.when(i > j)
        def _():
            step(False)

        @pl.when(i == j)
        def _():
            step(True)

        @pl.when(i == n - 1)
        def _():
            dk_ref[...] = jnp.where(lane < 64, dk_acc[0], dk_acc[1]).astype(bf16)
            dks_ref[...] = jnp.where(lane < 64, dk_acc[1], dk_acc[0])
            dv_ref[...] = dv_acc[...].astype(bf16)

        @pl.when(t == nsteps - 1)
        def _():
            lane_t = lax.broadcasted_iota(jnp.int32, (T, 128), 1)
            dq_ref[...] = (jnp.where(lane_t < 64, dq_acc[0], dq_acc[1]) * 0.125).astype(bf16)
            dqs_ref[...] = jnp.where(lane_t < 64, dq_acc[1], dq_acc[0])

    qmap = lambda p, t, it, jt: (it[t], p)
    grid_spec = pltpu.PrefetchScalarGridSpec(
        num_scalar_prefetch=2, grid=(PAIRS, nsteps),
        in_specs=[pl.BlockSpec((tq, 128), qmap),
                  pl.BlockSpec((tk, 128), lambda p, t, it, jt: (jt[t], PAIRS + p)),
                  pl.BlockSpec((tk, 128), lambda p, t, it, jt: (jt[t], 2 * PAIRS + p)),
                  pl.BlockSpec((tq, 128), qmap), pl.BlockSpec((tk, 128), lambda p, t, it, jt: (jt[t], p)),
                  pl.BlockSpec((tq, 128), qmap), pl.BlockSpec((tq, 128), qmap),
                  pl.BlockSpec((None, tq, 128), lambda p, t, it, jt: (p, it[t], 0)),
                  pl.BlockSpec((8, 128), lambda p, t, it, jt: (0, 0))],
        out_specs=[pl.BlockSpec((T, 128), lambda p, t, it, jt: (0, p)),
                   pl.BlockSpec((None, T, 128), lambda p, t, it, jt: (p, 0, 0)),
                   pl.BlockSpec((tk, 128), lambda p, t, it, jt: (jt[t], p)),
                   pl.BlockSpec((None, tk, 128), lambda p, t, it, jt: (p, jt[t], 0)),
                   pl.BlockSpec((tk, 128), lambda p, t, it, jt: (jt[t], p))],
        scratch_shapes=[pltpu.VMEM((2, T, 128), f32), pltpu.VMEM((2, tk, 128), f32), pltpu.VMEM((tk, 128), f32),
                        pltpu.VMEM((tq, tk), f32), pltpu.VMEM((tq, tk), f32), pltpu.VMEM((tq, tk), bf16),
                        pltpu.VMEM((tq, tk), bf16)],
    )
    return pl.pallas_call(
        body, name="fox_attn_bwd", grid_spec=grid_spec,
        out_shape=[jax.ShapeDtypeStruct((T, AW), bf16), jax.ShapeDtypeStruct((PAIRS, T, 128), f32),
                   jax.ShapeDtypeStruct((T, AW), bf16), jax.ShapeDtypeStruct((PAIRS, T, 128), f32),
                   jax.ShapeDtypeStruct((T, AW), bf16)],
        compiler_params=_cp("parallel", "arbitrary"),
    )(it, jt, qkv, qkv, qkv, qaug, kaug, attn, dattn, lse, dep)


def _fox_cumsum_bwd(dqs, dks, fl, bfp):
    tb = CUMSUM_ROWS
    nb = T // tb
    hp = lax.Precision.HIGHEST

    def body(dqs_ref, dks_ref, fl_ref, b_ref, df_ref, db_ref, carry):
        i = pl.program_id(0)

        @pl.when(i == 0)
        def _():
            carry[...] = jnp.zeros_like(carry)
            db_ref[...] = jnp.zeros_like(db_ref)

        r = lax.broadcasted_iota(jnp.int32, (128, 128), 0)
        cc = lax.broadcasted_iota(jnp.int32, (128, 128), 1)
        pick = lambda even_lane, odd_lane, p: jnp.logical_or(
            jnp.logical_and(r == even_lane, cc == 2 * p), jnp.logical_and(r == odd_lane, cc == 2 * p + 1)).astype(f32)
        dc = jnp.zeros((tb, 128), f32)
        for p in range(PAIRS):
            dc = dc + jnp.dot(dqs_ref[p], pick(64, 0, p), precision=hp, preferred_element_type=f32)
            dc = dc - jnp.dot(dks_ref[p], pick(67, 3, p), precision=hp, preferred_element_type=f32)
        rt = lax.broadcasted_iota(jnp.int32, (tb, tb), 0)
        ct = lax.broadcasted_iota(jnp.int32, (tb, tb), 1)
        utri = (ct >= rt).astype(f32)
        dl = jnp.dot(utri, dc, precision=hp, preferred_element_type=f32) + carry[0:1, :]
        carry[...] = jnp.broadcast_to(dl[0:1, :], (8, 128))
        z = fl_ref[...] + b_ref[...]
        df = dl * jax.nn.sigmoid(-z)
        df_ref[...] = df.astype(bf16)
        db_ref[...] += jnp.sum(df, axis=0, keepdims=True)

    rev = lambda i: (nb - 1 - i, 0)
    return pl.pallas_call(
        body, name="fox_cumsum_bwd", grid=(nb,),
        in_specs=[pl.BlockSpec((PAIRS, tb, 128), lambda i: (0, nb - 1 - i, 0)),
                  pl.BlockSpec((PAIRS, tb, 128), lambda i: (0, nb - 1 - i, 0)),
                  pl.BlockSpec((tb, 128), rev), _full((1, 128))],
        out_specs=[pl.BlockSpec((tb, 128), rev), _full((1, 128))],
        out_shape=[jax.ShapeDtypeStruct((T, 128), bf16), jax.ShapeDtypeStruct((1, 128), f32)],
        scratch_shapes=[pltpu.VMEM((8, 128), f32)],
        compiler_params=_cp("arbitrary"),
    )(dqs, dks, fl, bfp)


def _inproj_bwd(dq, dk, dv, du, df, wm, wf, x, dx1, g1):
    tm = 512

    def body(dq_ref, dk_ref, dv_ref, du_ref, df_ref, wm_ref, wf_ref, x_ref, dx1_ref, g_ref, dx_ref, dn_ref):
        i = pl.program_id(0)

        @pl.when(i == 0)
        def _():
            dn_ref[...] = jnp.zeros_like(dn_ref)

        dh = lax.dot_general(dq_ref[...], wm_ref[:, 0:AW], NT, preferred_element_type=f32)
        dh = dh + lax.dot_general(dk_ref[...], wm_ref[:, AW:2 * AW], NT, preferred_element_type=f32)
        dh = dh + lax.dot_general(dv_ref[...], wm_ref[:, 2 * AW:3 * AW], NT, preferred_element_type=f32)
        dh = dh + lax.dot_general(du_ref[...], wm_ref[:, 3 * AW:4 * AW], NT, preferred_element_type=f32)
        dh = dh + lax.dot_general(df_ref[...], wf_ref[...], NT, preferred_element_type=f32)
        xv = x_ref[...]
        r = lax.rsqrt(jnp.mean(xv * xv, axis=-1, keepdims=True) + EPS)
        xhat = xv * r
        dn_ref[...] += jnp.sum(dh * xhat, axis=0, keepdims=True)
        z = dh * g_ref[...]
        dx_ref[...] = dx1_ref[...] + r * (z - xhat * jnp.mean(z * xhat, axis=-1, keepdims=True))

    row = lambda i: (i, 0)
    return pl.pallas_call(
        body, name="inproj_bwd", grid=(T // tm,),
        in_specs=[pl.BlockSpec((tm, AW), row)] * 4 + [pl.BlockSpec((tm, 128), row), _full((D, 4 * AW)), _full((D, 128)),
                                                       pl.BlockSpec((tm, D), row), pl.BlockSpec((tm, D), row), _full((1, D))],
        out_specs=[pl.BlockSpec((tm, D), row), _full((1, D))],
        out_shape=[jax.ShapeDtypeStruct((T, D), f32), jax.ShapeDtypeStruct((1, D), f32)],
        compiler_params=_cp("arbitrary"),
    )(dq, dk, dv, du, df, wm, wf, x, dx1, g1)


def _adamw_math(w, g, m, v):
    m = B1 * m + (1.0 - B1) * g
    v = B2 * v + (1.0 - B2) * (g * g)
    m_hat = m / (1.0 - B1 ** STEP)
    v_hat = v / (1.0 - B2 ** STEP)
    delta = -LR * (m_hat / (jnp.sqrt(v_hat) + AEPS) + WD * w)
    return delta, m, v


def _adamw_shard(w, m, v, p_mine, p_other, name):
    rows, cols = w.shape
    tr = 256 if rows % 256 == 0 else 176

    def body(w_ref, m_ref, v_ref, a_ref, b_ref, g_ref, d_ref, nm_ref, nv_ref):
        g = a_ref[...] + b_ref[...]
        g_ref[...] = g
        d_ref[...], nm_ref[...], nv_ref[...] = _adamw_math(w_ref[...], g, m_ref[...], v_ref[...])

    spec = pl.BlockSpec((tr, cols), lambda i: (i, 0))
    return pl.pallas_call(
        body, name=name, grid=(rows // tr,), in_specs=[spec] * 5, out_specs=[spec] * 4,
        out_shape=[jax.ShapeDtypeStruct((rows, cols), f32)] * 4, compiler_params=_cp("parallel"),
    )(w, m, v, p_mine, p_other)


def _adamw_small(w, m, v, parts):
    def body(w_ref, m_ref, v_ref, p_ref, g_ref, d_ref, nm_ref, nv_ref):
        g = p_ref[0]
        for k in range(1, 8):
            g = g + p_ref[k]
        g_ref[...] = g
        d_ref[...], nm_ref[...], nv_ref[...] = _adamw_math(w_ref[...], g, m_ref[...], v_ref[...])

    return pl.pallas_call(
        body, name="adamw_small", out_shape=[jax.ShapeDtypeStruct((SMALL_ROWS, 128), f32)] * 4,
    )(w, m, v, parts)


def _sum4(recv, g, mine, name):
    _, rows, cols = recv.shape
    tr = 256 if rows % 256 == 0 else 176

    def body(mine_ref, r_ref, g_ref, o_ref):
        o_ref[...] = ((g_ref[...].astype(f32) + r_ref[0].astype(f32))
                      + (r_ref[1].astype(f32) + r_ref[2].astype(f32)))

    grid_spec = pltpu.PrefetchScalarGridSpec(
        num_scalar_prefetch=1, grid=(rows // tr,),
        in_specs=[pl.BlockSpec((3, tr, cols), lambda i, m: (0, i, 0)),
                  pl.BlockSpec((None, tr, cols), lambda i, m: (m[0], i, 0))],
        out_specs=pl.BlockSpec((tr, cols), lambda i, m: (i, 0)))
    return pl.pallas_call(
        body, name=name, grid_spec=grid_spec, out_shape=jax.ShapeDtypeStruct((rows, cols), f32),
        compiler_params=_cp("arbitrary"),
    )(mine, recv, g)


_HBM = pl.BlockSpec(memory_space=pltpu.HBM)
_SEM = pl.BlockSpec(memory_space=pltpu.SEMAPHORE)
_EFFECT = pltpu.SideEffectType.DATAFLOW_SIDE_EFFECTING


def _in_hbm(a):
    return pltpu.with_memory_space_constraint(a, pltpu.HBM)


def _mesh_pos():
    return lax.axis_index("x"), lax.axis_index("y"), lax.axis_index("c")


def _other_chips(x, y):
    return [(1 - x, y), (x, 1 - y), (1 - x, 1 - y)]


def _gather_copy(srcs, lands, send_sems, recv_sems, a, k, slot):
    x, y, c = _mesh_pos()
    cx, cy = _other_chips(x, y)[k]
    return pltpu.make_async_remote_copy(
        src_ref=srcs[a], dst_ref=lands[a].at[slot], send_sem=send_sems.at[3 * a + k], recv_sem=recv_sems.at[3 * a + k],
        device_id=(cx, cy, c), device_id_type=MESH)


def _scatter_copy(srcs, lands, send_sems, recv_sems, a, k):
    x, y, c = _mesh_pos()
    cx, cy = _other_chips(x, y)[k]
    return pltpu.make_async_remote_copy(
        src_ref=srcs[a].at[2 * cx + cy], dst_ref=lands[a].at[k], send_sem=send_sems.at[3 * a + k],
        recv_sem=recv_sems.at[3 * a + k], device_id=(cx, cy, c), device_id_type=MESH)


def _all_gather_w_in(part):
    rows = part.shape[0] // 2

    def body(src, dst, send_sems, recv_sems, loc_sem):
        x, y, c = _mesh_pos()
        mine = 2 * x + y
        chips = _other_chips(x, y)
        half = lambda ref, cc: ref.at[pl.ds(pl.multiple_of(cc * rows, rows), rows), :]

        def over_ici(k, slot):
            cx, cy = chips[k]
            return pltpu.make_async_remote_copy(
                src_ref=half(src, c), dst_ref=half(dst.at[slot], c), send_sem=send_sems.at[k], recv_sem=recv_sems.at[k],
                device_id=(cx, cy, c), device_id_type=MESH)

        def to_sibling(k, cc):
            slot = 2 * chips[k][0] + chips[k][1]
            return pltpu.make_async_remote_copy(
                src_ref=half(dst.at[slot], cc), dst_ref=half(dst.at[slot], cc), send_sem=send_sems.at[3 + k],
                recv_sem=recv_sems.at[3 + k], device_id=(x, y, 1 - c), device_id_type=MESH)

        local = pltpu.make_async_copy(src, dst.at[mine], loc_sem.at[0])
        local.start()
        first = [over_ici(k, mine) for k in range(3)]
        for cp in first:
            cp.start()
        passed = [to_sibling(k, c) for k in range(3)]
        for k in range(3):
            over_ici(k, 2 * chips[k][0] + chips[k][1]).wait_recv()
            passed[k].start()
        for k in range(3):
            to_sibling(k, 1 - c).wait_recv()
        for cp in first + passed:
            cp.wait_send()
        local.wait()

    return pl.pallas_call(
        body, name="all_gather_w_in", in_specs=[_HBM], out_specs=_HBM,
        out_shape=jax.ShapeDtypeStruct((NSH,) + part.shape, part.dtype),
        scratch_shapes=[pltpu.SemaphoreType.DMA((6,)), pltpu.SemaphoreType.DMA((6,)), pltpu.SemaphoreType.DMA((1,))],
    )(part)


def _split_start(name, srcs, lands, n_sems, plan, dep):
    n, nl = len(srcs), len(lands)

    def body(*refs):
        src_refs, land_refs = refs[:n], refs[n:n + nl]
        send_sems, recv_sems = refs[n + nl + 1], refs[n + nl + 2]
        token = refs[-1]
        sends, _ = plan(src_refs, land_refs, send_sems, recv_sems)
        for cp in sends:
            cp.start()
        token[...] = jnp.zeros_like(token)

    outs = pl.pallas_call(
        body, name=name,
        in_specs=[_HBM] * (n + nl) + [pl.BlockSpec(memory_space=pl.ANY)],
        out_specs=[_SEM, _SEM] + [_HBM] * (n + nl) + [pl.BlockSpec(memory_space=pltpu.VMEM)],
        out_shape=[pltpu.SemaphoreType.DMA((n_sems,)), pltpu.SemaphoreType.DMA((n_sems,))]
        + [pltpu.HBM(a.shape, a.dtype) for a in list(srcs) + list(lands)] + [jax.ShapeDtypeStruct((8, 128), f32)],
        input_output_aliases={i: 2 + i for i in range(n + nl)},
        compiler_params=pltpu.CompilerParams(has_side_effects=_EFFECT),
    )(*[_in_hbm(a) for a in list(srcs) + list(lands)], dep)
    return outs[0], outs[1], list(outs[2:2 + n]), list(outs[2 + n:2 + n + nl]), outs[-1]


def _split_wait(name, send_sems, recv_sems, srcs, lands, after, plan):
    n, nl = len(srcs), len(lands)

    def body(*refs):
        src_refs, land_refs = refs[:n], refs[n:n + nl]
        s_sems, r_sems = refs[n + nl], refs[n + nl + 1]
        sends, recvs = plan(src_refs, land_refs, s_sems, r_sems)
        for cp in recvs:
            cp.wait_recv()
        for cp in sends:
            cp.wait_send()

    outs = pl.pallas_call(
        body, name=name,
        in_specs=[_HBM] * (n + nl) + [_SEM, _SEM, pl.BlockSpec(memory_space=pl.ANY)],
        out_specs=[_HBM] * (n + nl),
        out_shape=[pltpu.HBM(a.shape, a.dtype) for a in list(srcs) + list(lands)],
        input_output_aliases={i: i for i in range(n + nl)},
        compiler_params=pltpu.CompilerParams(has_side_effects=_EFFECT),
    )(*srcs, *lands, send_sems, recv_sems, after)
    return list(outs[:n]), list(outs[n:])


def _gather_plan(srcs, lands, ss, rs):
    x, y, _ = _mesh_pos()
    chips = _other_chips(x, y)
    sends = [_gather_copy(srcs, lands, ss, rs, a, k, 2 * x + y) for a in range(len(srcs)) for k in range(3)]
    recvs = [_gather_copy(srcs, lands, ss, rs, a, k, 2 * chips[k][0] + chips[k][1])
             for a in range(len(srcs)) for k in range(3)]
    return sends, recvs


def _scatter_plan(srcs, lands, ss, rs):
    cps = [_scatter_copy(srcs, lands, ss, rs, a, k) for a in range(len(srcs)) for k in range(3)]
    return cps, cps


def _tail_plan(srcs, lands, ss, rs):
    x, y, c = _mesh_pos()
    me = 4 * x + 2 * y + c
    cps = [_scatter_copy(srcs[:1], lands[:1], ss, rs, 0, k) for k in range(3)]
    for f in range(1, 8):
        peer = ((x + (f >> 2)) % 2, (y + ((f >> 1) & 1)) % 2, (c + (f & 1)) % 2)
        cps.append(pltpu.make_async_remote_copy(
            src_ref=srcs[1], dst_ref=lands[1].at[me], send_sem=ss.at[2 + f], recv_sem=rs.at[2 + f],
            device_id=peer, device_id_type=MESH))
    return cps, cps


def _swap_with_sibling(parts, name):
    n = len(parts)

    def body(*refs):
        srcs, dsts = refs[:n], refs[n:2 * n]
        send_sems, recv_sems = refs[2 * n:]
        x, y, c = _mesh_pos()
        cps = [pltpu.make_async_remote_copy(src_ref=srcs[a], dst_ref=dsts[a], send_sem=send_sems.at[a],
                                            recv_sem=recv_sems.at[a], device_id=(x, y, 1 - c), device_id_type=MESH)
               for a in range(n)]
        for cp in cps:
            cp.start()
        for cp in cps:
            cp.wait_recv()
        for cp in cps:
            cp.wait_send()

    return pl.pallas_call(
        body, name=name, in_specs=[_HBM] * n, out_specs=[_HBM] * n,
        out_shape=[jax.ShapeDtypeStruct(p.shape, p.dtype) for p in parts],
        scratch_shapes=[pltpu.SemaphoreType.DMA((n,)), pltpu.SemaphoreType.DMA((n,))],
    )(*parts)


def _pack_small(n1, n2, nf, ps, bfv, wp):
    pad8 = lambda r: jnp.pad(r, ((0, 8 - r.shape[0]), (0, 0)))
    return jnp.concatenate([n1.reshape(8, 128), n2.reshape(8, 128), nf.reshape(8, 128), pad8(ps.reshape(4, 128)),
                            pad8(jnp.pad(bfv.reshape(1, 8), ((0, 0), (0, 120)))), wp.reshape(512, 128)], axis=0)


def _unpack_small(p):
    return dict(norm1_g=p[0:8].reshape(1, D), norm2_g=p[8:16].reshape(1, D), final_g=p[16:24].reshape(D),
                pool_scale=p[24:28].reshape(1, AW), b_forget=p[32:33, 0:8], w_pool=p[40:552].reshape(1, 4, 128, 128))


def _forward(x, tgt, wm, wf, mlp_w_fn, g1, bfp, wp, scale, g2, gf, dep):
    h, qkv, u, fl = _rms_inproj(x, g1, wm, wf, dep)
    qaug, kaug = _fox_cumsum(fl, bfp)
    attn, lse = _attn_fwd(qkv, qaug, kaug)
    pooled, pool = _pool_fwd(u, wp, scale)
    wo, wgt, wut, wd = mlp_w_fn(attn)
    x1, h2 = _outproj(x, attn, pool, wo, g2)
    x2, gate, up = _mlp_fwd(h2, x1, wgt, wut, wd)
    loss, dgf, dx2, dx2b = _final_loss(x2, tgt, gf)
    saved = dict(h=h, qkv=qkv, fl=fl, qaug=qaug, kaug=kaug, attn=attn, lse=lse, pooled=pooled, pool=pool, x1=x1, h2=h2,
                 gate=gate, up=up, wo=wo, wgt=wgt, wut=wut, wd=wd)
    return loss, dgf, dx2, dx2b, saved


def _backward_mlp(sv, dx2, dx2b, g2):
    a_b, dgate, dup, dx1, dx1b, dg2 = _mlp_bwd(dx2b, dx2, sv["gate"], sv["up"], sv["wgt"], sv["wut"], sv["wd"], sv["x1"], g2)
    (dwd,) = _mm_tn(a_b, [dx2b], "dw_down", a_sharded=True, tk=2048)
    (dwgt,) = _mm_tn(dgate, [sv["h2"]], "dw_gate", a_sharded=True, tk=2048)
    (dwut,) = _mm_tn(dup, [sv["h2"]], "dw_up", a_sharded=True, tk=2048)
    return dx1, dx1b, dg2, (dwgt, dwut, dwd)


def _backward_outproj(sv, dx1b):
    dattn, dpool = _outproj_bwd(dx1b, sv["wo"])
    dwo_a, = _mm_tn(sv["attn"], [dx1b], "dw_out_attn", tk=2048)
    dwo_p, = _mm_tn(sv["pool"], [dx1b], "dw_out_pool", tk=2048)
    dwo = jnp.concatenate([dwo_a, dwo_p], axis=0).reshape(NSH, D // NSH, D)
    return dattn, dpool, dwo


def _backward_mixer(sv, x, dx1, dattn, dpool, wm, wf, g1, bfp, wp, scale, dep):
    du, dscale, dwp = _pool_bwd(dpool, sv["pooled"], wp, scale, dep)
    dq, dqs, dk, dks, dv = _attn_bwd(sv["qkv"], sv["qaug"], sv["kaug"], sv["attn"], dattn, sv["lse"], dep)
    df, dbf = _fox_cumsum_bwd(dqs, dks, sv["fl"], bfp)
    dx, dg1 = _inproj_bwd(dq, dk, dv, du, df, wm, wf, x, dx1, g1)
    dwq, dwk, dwv, dwu_in, dwf = _mm_tn(sv["h"], [dq, dk, dv, du, df], "dw_in", tk=1024)
    dwin = jnp.concatenate([dwq, dwk, dwv, dwf[:, 0:8], dwu_in], axis=1)
    dwin = dwin.reshape(D, NSH, IN_S).transpose(1, 0, 2)
    return dx, dg1, dscale, dwp, dbf, dwin


def kernel(x, norm1_g, w_in, b_forget, w_pool, pool_scale, w_out, norm2_g, w_gate, w_up, w_down, final_g, loss_target, m_norm1_g, m_w_in, m_b_forget, m_w_pool, m_pool_scale, m_w_out, m_norm2_g, m_w_gate, m_w_up, m_w_down, m_final_g, v_norm1_g, v_w_in, v_b_forget, v_w_pool, v_pool_scale, v_w_out, v_norm2_g, v_w_gate, v_w_up, v_w_down, v_final_g):
    mine = (2 * lax.axis_index("x") + lax.axis_index("y")).astype(jnp.int32)
    mine1 = mine.reshape(1)
    tr = lambda a: jnp.transpose(a[0])

    win4 = _all_gather_w_in(w_in[0].astype(bf16))
    later = [w_out[0].astype(bf16), tr(w_gate).astype(bf16), tr(w_up).astype(bf16), w_down[0].astype(bf16)]
    lands = [lax.dynamic_update_slice(lax.empty((NSH,) + p.shape, bf16), p[None], (mine, 0, 0)) for p in later]
    ag_send, ag_recv, later_thru, lands_thru, ag_token = _split_start("all_gather_start", later, lands, 12, _gather_plan,
                                                                      win4)
    win = win4.transpose(1, 0, 2).reshape(D, IN_W)
    wm = jnp.concatenate([win[:, 0:3 * AW], win[:, 3 * AW + 8:]], axis=1)
    wf = jnp.pad(win[:, 3 * AW:3 * AW + 8], ((0, 0), (0, 120)))
    bfp = jnp.pad(b_forget, ((0, 0), (0, 120)))
    wp = w_pool[0].astype(bf16)
    gf = final_g.reshape(1, D)

    def later_weights(after):
        _, (wo4, wgt, wut, wd) = _split_wait("all_gather_wait", ag_send, ag_recv, later_thru, lands_thru, after, _gather_plan)
        return wo4.reshape(D, D), wgt, wut, wd

    xe, tgt = x[0], loss_target[0]
    loss_v, dgf, dx2, dx2b, sv = _forward(xe, tgt, wm, wf, later_weights, norm1_g, bfp, wp, pool_scale, norm2_g, gf, ag_token)
    dx1, dx1b, dg2, mlp_grads = _backward_mlp(sv, dx2, dx2b, norm2_g)
    dattn, dpool, dwo = _backward_outproj(sv, dx1b)
    first = [dwo] + list(mlp_grads)
    first_lands = [lax.empty((3,) + g.shape[1:], bf16) for g in first]
    rs_send, rs_recv, first_thru, first_lands_thru, rs_token = _split_start("reduce_scatter_start", first, first_lands, 12,
                                                                            _scatter_plan, dattn)
    dx, dg1, dscale, dwp, dbf, dwin = _backward_mixer(sv, xe, dx1, dattn, dpool, wm, wf, norm1_g, bfp, wp, pool_scale, rs_token)

    me = (4 * lax.axis_index("x") + 2 * lax.axis_index("y") + lax.axis_index("c")).astype(jnp.int32)
    pad8 = lambda r: jnp.pad(r, ((0, 8 - r.shape[0]), (0, 0)))
    loss_rows = jnp.concatenate([dbf, jnp.zeros((6, 128), f32), loss_v[0:1, :]], axis=0)
    small = jnp.concatenate([dg1.reshape(8, 128), dg2.reshape(8, 128), dgf.reshape(8, 128), pad8(dscale.reshape(4, 128)),
                             loss_rows, dwp.reshape(512, 128)], axis=0)
    small_land = lax.dynamic_update_slice(lax.empty((8, SMALL_ROWS, 128), f32), small[None], (me, 0, 0))
    tail_send, tail_recv, tail_thru, tail_lands_thru, tail_token = _split_start(
        "tail_start", [dwin, small], [lax.empty((3,) + dwin.shape[1:], bf16), small_land], 10, _tail_plan, dx)
    first_thru, first_recv = _split_wait("reduce_scatter_wait", rs_send, rs_recv, first_thru, first_lands_thru, tail_token,
                                         _scatter_plan)
    ws = [w_in[0], w_out[0], tr(w_gate), tr(w_up), w_down[0]]
    ms = [m_w_in[0], m_w_out[0], tr(m_w_gate), tr(m_w_up), m_w_down[0]]
    vs = [v_w_in[0], v_w_out[0], tr(v_w_gate), tr(v_w_up), v_w_down[0]]
    partial = [_sum4(r, g, mine1, f"sum4_{i + 1}") for i, (r, g) in enumerate(zip(first_recv, first_thru))]
    other = _swap_with_sibling(partial, "swap_first")
    big = [_adamw_shard(ws[i + 1], ms[i + 1], vs[i + 1], partial[i], other[i], f"adamw_{i + 1}") for i in range(4)]
    (dwin_thru, _), (in_recv_land, small_all) = _split_wait("tail_wait", tail_send, tail_recv, tail_thru, tail_lands_thru,
                                                            big[3][0], _tail_plan)
    partial_in = _sum4(in_recv_land, dwin_thru, mine1, "sum4_0")
    (other_in,) = _swap_with_sibling([partial_in], "swap_in")
    big = [_adamw_shard(ws[0], ms[0], vs[0], partial_in, other_in, "adamw_0")] + big

    sm = _adamw_small(_pack_small(norm1_g, norm2_g, final_g, pool_scale, b_forget, w_pool),
                      _pack_small(m_norm1_g, m_norm2_g, m_final_g, m_pool_scale, m_b_forget, m_w_pool),
                      _pack_small(v_norm1_g, v_norm2_g, v_final_g, v_pool_scale, v_b_forget, v_w_pool), small_all)

    loss = sm[0][39, 0]
    order =["norm1_g", "w_in", "b_forget", "w_pool", "pool_scale", "w_out", "norm2_g", "w_gate", "w_up", "w_down", "final_g"]
    big_idx = {"w_in": 0, "w_out": 1, "w_gate": 2, "w_up": 3, "w_down": 4}
    outs = [loss, dx[None]]
    for kind in range(4):
        small_k = _unpack_small(sm[kind])
        for name in order:
            if name in ("w_gate", "w_up"):
                outs.append(jnp.transpose(big[big_idx[name]][kind])[None])
            elif name in big_idx:
                outs.append(big[big_idx[name]][kind][None])
            else:
                outs.append(small_k[name])
    return tuple(outs)
```

```python
import functools

import jax
import jax.numpy as jnp
import numpy as np
from jax import lax
from jax.experimental import pallas as pl
from jax.experimental.pallas import tpu as pltpu

f32 = jnp.float32
bf16 = jnp.bfloat16

T = 4096
D = 1024
NSH = 4
IN_W = 2056
IN_S = IN_W // NSH
AW = 512
PAIRS = 4
FF = 2816
FS = FF // NSH
WINDOWS = (2, 4, 8, 16)
HALO = 16
EPS = 1e-6
NEG = -1e30
LR, B1, B2, AEPS, WD, STEP = 0.001, 0.9, 0.999, 1e-08, 0.01, 10
SMALL_ROWS = 552

NT = (((1,), (1,)), ((), ()))
TN = (((0,), (0,)), ((), ()))

MESH = pl.DeviceIdType.MESH


def _cp(*sem):
    return pltpu.CompilerParams(dimension_semantics=sem)


def _full(shape):
    n = len(shape)
    return pl.BlockSpec(shape, lambda *_: (0,) * n)


def _rms_inproj(x, g1, wm, wf, dep):
    tm = 512

    def body(x_ref, g_ref, wm_ref, wf_ref, dep_ref, h_ref, qkv_ref, u_ref, fl_ref):
        xv = x_ref[...]
        r = lax.rsqrt(jnp.mean(xv * xv, axis=-1, keepdims=True) + EPS)
        h = (xv * r * g_ref[...]).astype(bf16)
        h_ref[...] = h
        qkv_ref[...] = jnp.dot(h, wm_ref[:, 0:3 * AW], preferred_element_type=f32).astype(bf16)
        u_ref[...] = jnp.dot(h, wm_ref[:, 3 * AW:4 * AW], preferred_element_type=f32)
        fl_ref[...] = jnp.dot(h, wf_ref[...], preferred_element_type=f32)

    return pl.pallas_call(
        body, name="rms_inproj", grid=(T // tm,),
        in_specs=[pl.BlockSpec((tm, D), lambda i: (i, 0)), _full((1, D)), _full((D, 4 * AW)), _full((D, 128)),
                  _full((8, 128))],
        out_specs=[pl.BlockSpec((tm, D), lambda i: (i, 0)), pl.BlockSpec((tm, 3 * AW), lambda i: (i, 0)),
                   pl.BlockSpec((tm, AW), lambda i: (i, 0)), pl.BlockSpec((tm, 128), lambda i: (i, 0))],
        out_shape=[jax.ShapeDtypeStruct((T, D), bf16), jax.ShapeDtypeStruct((T, 3 * AW), bf16),
                   jax.ShapeDtypeStruct((T, AW), f32), jax.ShapeDtypeStruct((T, 128), f32)],
        compiler_params=_cp("parallel"),
    )(x, g1, wm, wf, dep)


CUMSUM_ROWS = 512
FS_CHUNKS = ((0, 256), (256, 512), (512, FS))


def _log_sigmoid(z):
    return jnp.minimum(z, 0.0) - jnp.log(1.0 + jnp.exp(-jnp.abs(z)))


def _fox_cumsum(fl, bfp):
    tb = CUMSUM_ROWS
    nb = T // tb

    def body(fl_ref, b_ref, qa_ref, ka_ref, carry):
        i = pl.program_id(0)

        @pl.when(i == 0)
        def _():
            carry[...] = jnp.zeros_like(carry)

        lf = _log_sigmoid(fl_ref[...] + b_ref[...])
        r = lax.broadcasted_iota(jnp.int32, (tb, tb), 0)
        cc = lax.broadcasted_iota(jnp.int32, (tb, tb), 1)
        ltri = (cc <= r).astype(f32)
        cb = jnp.dot(ltri, lf, precision=lax.Precision.HIGHEST, preferred_element_type=f32) + carry[0:1, :]
        carry[...] = jnp.broadcast_to(cb[tb - 1:tb, :], (8, 128))
        hi = cb.astype(bf16)
        r1 = cb - hi.astype(f32)
        mid = r1.astype(bf16)
        lo = (r1 - mid.astype(f32)).astype(bf16)
        head = lax.broadcasted_iota(jnp.int32, (128, AW), 0)
        col = lax.broadcasted_iota(jnp.int32, (128, AW), 1)
        base = 128 * (head >> 1) + 64 * (1 - (head & 1))
        place = lambda off: jnp.logical_and(col == base + off, head < 8).astype(bf16)
        mm = lambda a, off: jnp.dot(a, place(off), preferred_element_type=f32)
        cq = mm(hi, 0) + mm(mid, 1) + mm(lo, 2)
        ck = mm(hi, 3) + mm(mid, 4) + mm(lo, 5)
        within = jnp.bitwise_and(lax.broadcasted_iota(jnp.int32, (tb, AW), 1), 63)
        qa_ref[...] = jnp.where(jnp.logical_and(within >= 3, within <= 5), 1.0, cq).astype(bf16)
        ka_ref[...] = jnp.where(within <= 2, 1.0, -ck).astype(bf16)

    return pl.pallas_call(
        body, name="fox_cumsum", grid=(nb,),
        in_specs=[pl.BlockSpec((tb, 128), lambda i: (i, 0)), _full((1, 128))],
        out_specs=[pl.BlockSpec((tb, AW), lambda i: (i, 0)), pl.BlockSpec((tb, AW), lambda i: (i, 0))],
        out_shape=[jax.ShapeDtypeStruct((T, AW), bf16), jax.ShapeDtypeStruct((T, AW), bf16)],
        scratch_shapes=[pltpu.VMEM((8, 128), f32)],
        compiler_params=_cp("arbitrary"),
    )(fl, bfp)


ATT_T = 512


def _causal_steps(key_major):
    n = T // ATT_T
    if key_major:
        pairs = [(i, j) for j in range(n) for i in range(j, n)]
    else:
        pairs = [(i, j) for i in range(n) for j in range(i + 1)]
    it = np.array([p[0] for p in pairs], np.int32)
    jt = np.array([p[1] for p in pairs], np.int32)
    return jnp.asarray(it), jnp.asarray(jt)


def _attn_fwd(qkv, qaug, kaug):
    tq = tk = ATT_T
    it, jt = _causal_steps(False)
    nsteps = it.shape[0]

    rs = 64

    def body(it_ref, jt_ref, q_ref, k_ref, v_ref, qa_ref, ka_ref, o_ref, lse_ref, m_sc, acc_sc, s_sc, p_sc, alpha_sc):
        t = pl.program_id(1)
        i = it_ref[t]
        j = jt_ref[t]

        @pl.when(j == 0)
        def _():
            m_sc[...] = jnp.full_like(m_sc, NEG)
            acc_sc[...] = jnp.zeros_like(acc_sc)

        lane = lax.broadcasted_iota(jnp.int32, (tq, 128), 1)
        spare = (64, 0)

        def step(on_diagonal):
            q = q_ref[...] * 0.125
            k = k_ref[...]
            v = v_ref[...]
            qa = qa_ref[...]
            ka = ka_ref[...]
            for e in range(2):
                hm = (lane >= 64) if e else (lane < 64)
                s_sc[...] = lax.dot_general(jnp.where(hm, q, qa), jnp.where(hm, k, ka), NT, preferred_element_type=f32)
                for r in range(0, tq, rs):
                    s = s_sc[r:r + rs, :]
                    if on_diagonal:
                        row = lax.broadcasted_iota(jnp.int32, (rs, tk), 0) + r
                        col = lax.broadcasted_iota(jnp.int32, (rs, tk), 1)
                        s = jnp.where(col <= row, s, NEG)
                    m_prev = m_sc[e, r:r + rs, :]
                    m_new = jnp.maximum(m_prev, jnp.max(s, axis=1, keepdims=True))
                    p_sc[r:r + rs, :] = jnp.exp(s - jnp.tile(m_new, (1, tk // 128))).astype(bf16)
                    alpha_sc[r:r + rs, :] = jnp.exp(m_prev - m_new)
                    m_sc[e, r:r + rs, :] = m_new
                ve = jnp.where(hm, v, (lane == spare[e]).astype(bf16))
                acc_sc[e] = alpha_sc[...] * acc_sc[e] + jnp.dot(p_sc[...], ve, preferred_element_type=f32)

        @pl.when(j < i)
        def _():
            step(False)

        @pl.when(j == i)
        def _():
            step(True)
            l0 = acc_sc[0][:, spare[0]:spare[0] + 1]
            l1 = acc_sc[1][:, spare[1]:spare[1] + 1]
            o_ref[...] = jnp.where(lane < 64, acc_sc[0] / l0, acc_sc[1] / l1).astype(bf16)
            lse_ref[...] = jnp.where(lane < 64, m_sc[0] + jnp.log(l0), m_sc[1] + jnp.log(l1))

    qmap = lambda p, t, it, jt: (it[t], p)
    kmap = lambda p, t, it, jt: (jt[t], p)
    grid_spec = pltpu.PrefetchScalarGridSpec(
        num_scalar_prefetch=2, grid=(PAIRS, nsteps),
        in_specs=[pl.BlockSpec((tq, 128), qmap),
                  pl.BlockSpec((tk, 128), lambda p, t, it, jt: (jt[t], PAIRS + p)),
                  pl.BlockSpec((tk, 128), lambda p, t, it, jt: (jt[t], 2 * PAIRS + p)),
                  pl.BlockSpec((tq, 128), qmap), pl.BlockSpec((tk, 128), kmap)],
        out_specs=[pl.BlockSpec((tq, 128), qmap),
                   pl.BlockSpec((None, tq, 128), lambda p, t, it, jt: (p, it[t], 0))],
        scratch_shapes=[pltpu.VMEM((2, tq, 128), f32), pltpu.VMEM((2, tq, 128), f32), pltpu.VMEM((tq, tk), f32),
                        pltpu.VMEM((tq, tk), bf16), pltpu.VMEM((tq, 128), f32)],
    )
    return pl.pallas_call(
        body, name="fox_attn_fwd", grid_spec=grid_spec,
        out_shape=[jax.ShapeDtypeStruct((T, AW), bf16), jax.ShapeDtypeStruct((PAIRS, T, 128), f32)],
        compiler_params=_cp("parallel", "arbitrary"),
    )(it, jt, qkv, qkv, qkv, qaug, kaug)


def _pool_fwd(u, wp, scale):
    tm = 512

    def body(u_ref, wp_ref, sc_ref, pooled_ref, pool_ref, ext):
        i = pl.program_id(0)

        @pl.when(i == 0)
        def _():
            ext[0:HALO, :] = jnp.zeros((HALO, AW), f32)

        uv = u_ref[...]
        ext[HALO:HALO + tm, :] = uv
        t_idx = i * tm + lax.broadcasted_iota(jnp.int32, (tm, 1), 0)
        for g, w in enumerate(WINDOWS):
            lo, hi = 128 * g, 128 * (g + 1)
            ug = uv[:, lo:hi]
            acc = ug
            for d in range(1, w):
                acc = acc + ext[HALO - d:HALO - d + tm, lo:hi]
            cnt = jnp.minimum(t_idx + 1, w).astype(f32)
            pb = (acc / cnt - ug).astype(bf16)
            pooled_ref[:, lo:hi] = pb
            mixed = jnp.dot(pb, wp_ref[g], preferred_element_type=f32)
            pool_ref[:, lo:hi] = (mixed * sc_ref[:, lo:hi]).astype(bf16)
        ext[0:HALO, :] = uv[tm - HALO:tm, :]

    return pl.pallas_call(
        body, name="pool_fwd", grid=(T // tm,),
        in_specs=[pl.BlockSpec((tm, AW), lambda i: (i, 0)), _full((4, 128, 128)), _full((1, AW))],
        out_specs=[pl.BlockSpec((tm, AW), lambda i: (i, 0)), pl.BlockSpec((tm, AW), lambda i: (i, 0))],
        out_shape=[jax.ShapeDtypeStruct((T, AW), bf16), jax.ShapeDtypeStruct((T, AW), bf16)],
        scratch_shapes=[pltpu.VMEM((tm + HALO, AW), f32)],
        compiler_params=_cp("arbitrary"),
    )(u, wp, scale)


def _outproj(x, attn, pool, wo, g2):
    tm = 512

    def body(x_ref, a_ref, p_ref, wo_ref, g_ref, x1_ref, h2_ref):
        x1 = x_ref[...] + jnp.dot(a_ref[...], wo_ref[0:AW, :], preferred_element_type=f32)
        x1 = x1 + jnp.dot(p_ref[...], wo_ref[AW:2 * AW, :], preferred_element_type=f32)
        x1_ref[...] = x1
        r = lax.rsqrt(jnp.mean(x1 * x1, axis=-1, keepdims=True) + EPS)
        h2_ref[...] = (x1 * r * g_ref[...]).astype(bf16)

    return pl.pallas_call(
        body, name="outproj", grid=(T // tm,),
        in_specs=[pl.BlockSpec((tm, D), lambda i: (i, 0)), pl.BlockSpec((tm, AW), lambda i: (i, 0)),
                  pl.BlockSpec((tm, AW), lambda i: (i, 0)), _full((D, D)), _full((1, D))],
        out_specs=[pl.BlockSpec((tm, D), lambda i: (i, 0)), pl.BlockSpec((tm, D), lambda i: (i, 0))],
        out_shape=[jax.ShapeDtypeStruct((T, D), f32), jax.ShapeDtypeStruct((T, D), bf16)],
        compiler_params=_cp("parallel"),
    )(x, attn, pool, wo, g2)


def _mlp_fwd_loss(h2, x1, wg, wu, wd, tgt, gf):
    tm = 512

    def body(h_ref, x1_ref, wg_ref, wu_ref, wd_ref, t_ref, g_ref,
             loss_ref, dg_ref, dx_ref, dxb_ref, gate_ref, up_ref, a_ref, x2):
        i = pl.program_id(0)
        s = pl.program_id(1)

        @pl.when(jnp.logical_and(i == 0, s == 0))
        def _():
            loss_ref[...] = jnp.zeros_like(loss_ref)
            dg_ref[...] = jnp.zeros_like(dg_ref)

        h = h_ref[...]
        part = None
        for c0, c1 in FS_CHUNKS:
            gate = lax.dot_general(h, wg_ref[c0:c1, :], NT, preferred_element_type=f32)
            up = lax.dot_general(h, wu_ref[c0:c1, :], NT, preferred_element_type=f32)
            gate_ref[:, c0:c1] = gate.astype(bf16)
            up_ref[:, c0:c1] = up.astype(bf16)
            a = (gate * jax.nn.sigmoid(gate) * up).astype(bf16)
            a_ref[:, c0:c1] = a
            pc = jnp.dot(a, wd_ref[c0:c1, :], preferred_element_type=f32)
            part = pc if part is None else part + pc

        @pl.when(s == 0)
        def _():
            x2[...] = x1_ref[...] + part

        @pl.when(s > 0)
        def _():
            x2[...] += part

        @pl.when(s == NSH - 1)
        def _():
            xv = x2[...]
            g = g_ref[...]
            r = lax.rsqrt(jnp.mean(xv * xv, axis=-1, keepdims=True) + EPS)
            xhat = xv * r
            e = xhat * g - t_ref[...]
            loss_ref[...] += 0.5 * jnp.sum(jnp.mean(e * e, axis=-1, keepdims=True))
            dy = e * (1.0 / D)
            dg_ref[...] += jnp.sum(dy * xhat, axis=0, keepdims=True)
            z = dy * g
            dx = r * (z - xhat * jnp.mean(z * xhat, axis=-1, keepdims=True))
            dx_ref[...] = dx
            dxb_ref[...] = dx.astype(bf16)

    row = lambda i, s: (i, 0)
    sl = lambda i, s: (s, i, 0)
    wsl = lambda i, s: (s, 0, 0)
    return pl.pallas_call(
        body, name="mlp_fwd_loss", grid=(T // tm, NSH),
        in_specs=[pl.BlockSpec((tm, D), row), pl.BlockSpec((tm, D), row),
                  pl.BlockSpec((None, FS, D), wsl), pl.BlockSpec((None, FS, D), wsl), pl.BlockSpec((None, FS, D), wsl),
                  pl.BlockSpec((tm, D), row), pl.BlockSpec((1, D), lambda i, s: (0, 0))],
        out_specs=[pl.BlockSpec((8, 128), lambda i, s: (0, 0)), pl.BlockSpec((1, D), lambda i, s: (0, 0)),
                   pl.BlockSpec((tm, D), row), pl.BlockSpec((tm, D), row),
                   pl.BlockSpec((None, tm, FS), sl), pl.BlockSpec((None, tm, FS), sl), pl.BlockSpec((None, tm, FS), sl)],
        out_shape=[jax.ShapeDtypeStruct((8, 128), f32), jax.ShapeDtypeStruct((1, D), f32),
                   jax.ShapeDtypeStruct((T, D), f32), jax.ShapeDtypeStruct((T, D), bf16)]
        + [jax.ShapeDtypeStruct((NSH, T, FS), bf16)] * 3,
        scratch_shapes=[pltpu.VMEM((tm, D), f32)],
        compiler_params=_cp("arbitrary", "arbitrary"),
    )(h2, x1, wg, wu, wd, tgt, gf)


def _mlp_bwd(dx2b, dx2, gate, up, wg, wu, wd, x1, g2):
    tm = 512

    def body(dxb_ref, dx_ref, gate_ref, up_ref, wg_ref, wu_ref, wd_ref, x1_ref, g_ref,
             dg_ref, du_ref, dx1_ref, dx1b_ref, dn_ref, acc):
        i = pl.program_id(0)
        s = pl.program_id(1)

        @pl.when(jnp.logical_and(i == 0, s == 0))
        def _():
            dn_ref[...] = jnp.zeros_like(dn_ref)

        dxb = dxb_ref[...]
        part = None
        for c0, c1 in FS_CHUNKS:
            da = lax.dot_general(dxb, wd_ref[c0:c1, :], NT, preferred_element_type=f32)
            gate = gate_ref[:, c0:c1].astype(f32)
            upv = up_ref[:, c0:c1].astype(f32)
            sg = jax.nn.sigmoid(gate)
            silu = gate * sg
            dgate =(da * upv * (sg * (1.0 + gate * (1.0 - sg)))).astype(bf16)
            dup = (da * silu).astype(bf16)
            dg_ref[:, c0:c1] = dgate
            du_ref[:, c0:c1] = dup
            pc = jnp.dot(dgate, wg_ref[c0:c1, :], preferred_element_type=f32)
            pc = pc + jnp.dot(dup, wu_ref[c0:c1, :], preferred_element_type=f32)
            part = pc if part is None else part + pc

        @pl.when(s == 0)
        def _():
            acc[...] = part

        @pl.when(s > 0)
        def _():
            acc[...] += part

        @pl.when(s == NSH - 1)
        def _():
            xv = x1_ref[...]
            r = lax.rsqrt(jnp.mean(xv * xv, axis=-1, keepdims=True) + EPS)
            xhat = xv * r
            dh = acc[...]
            dn_ref[...] += jnp.sum(dh * xhat, axis=0, keepdims=True)
            z = dh * g_ref[...]
            dx1 = dx_ref[...] + r * (z - xhat * jnp.mean(z * xhat, axis=-1, keepdims=True))
            dx1_ref[...] = dx1
            dx1b_ref[...] = dx1.astype(bf16)

    row = lambda i, s: (i, 0)
    sl = lambda i, s: (s, i, 0)
    wsl = lambda i, s: (s, 0, 0)
    return pl.pallas_call(
        body, name="mlp_bwd", grid=(T // tm, NSH),
        in_specs=[pl.BlockSpec((tm, D), row), pl.BlockSpec((tm, D), row),
                  pl.BlockSpec((None, tm, FS), sl), pl.BlockSpec((None, tm, FS), sl),
                  pl.BlockSpec((None, FS, D), wsl), pl.BlockSpec((None, FS, D), wsl), pl.BlockSpec((None, FS, D), wsl),
                  pl.BlockSpec((tm, D), row), pl.BlockSpec((1, D), lambda i, s: (0, 0))],
        out_specs=[pl.BlockSpec((None, tm, FS), sl), pl.BlockSpec((None, tm, FS), sl),
                   pl.BlockSpec((tm, D), row), pl.BlockSpec((tm, D), row), pl.BlockSpec((1, D), lambda i, s: (0, 0))],
        out_shape=[jax.ShapeDtypeStruct((NSH, T, FS), bf16)] * 2
        + [jax.ShapeDtypeStruct((T, D), f32), jax.ShapeDtypeStruct((T, D), bf16), jax.ShapeDtypeStruct((1, D), f32)],
        scratch_shapes=[pltpu.VMEM((tm, D), f32)],
        compiler_params=_cp("arbitrary", "arbitrary"),
    )(dx2b, dx2, gate, up, wg, wu, wd, x1, g2)


def _mm_tn(a, bs, name, a_sharded=False, b_sharded=False, tk=512, out_dtype=bf16):
    nb = len(bs)
    sh = NSH if (a_sharded or b_sharded) else 1
    m = a.shape[-1]
    nk = T // tk

    def body(a_ref, *refs):
        kk = pl.program_id(1)
        av = a_ref[...]
        for b_ref, o_ref, acc in zip(refs[:nb], refs[nb:2 * nb], refs[2 * nb:]):
            upd = lax.dot_general(av, b_ref[...], TN, preferred_element_type=f32)

            @pl.when(kk == 0)
            def _():
                acc[...] = upd

            @pl.when(kk > 0)
            def _():
                acc[...] += upd

            @pl.when(kk == nk - 1)
            def _():
                o_ref[...] = acc[...].astype(out_dtype)

    a_spec = (pl.BlockSpec((None, tk, m), lambda s, k: (s, k, 0)) if a_sharded
              else pl.BlockSpec((tk, m), lambda s, k: (k, 0)))
    b_specs, o_specs, o_shapes, scratch = [], [], [], []
    for b in bs:
        n = b.shape[-1]
        b_specs.append(pl.BlockSpec((None, tk, n), lambda s, k: (s, k, 0)) if b_sharded
                       else pl.BlockSpec((tk, n), lambda s, k: (k, 0)))
        scratch.append(pltpu.VMEM((m, n), f32))
        if sh > 1:
            o_specs.append(pl.BlockSpec((None, m, n), lambda s, k: (s, 0, 0)))
            o_shapes.append(jax.ShapeDtypeStruct((sh, m, n), out_dtype))
        else:
            o_specs.append(pl.BlockSpec((m, n), lambda s, k: (0, 0)))
            o_shapes.append(jax.ShapeDtypeStruct((m, n), out_dtype))
    return pl.pallas_call(
        body, name=name, grid=(sh, nk), in_specs=[a_spec] + b_specs, out_specs=o_specs, out_shape=o_shapes,
        scratch_shapes=scratch, compiler_params=_cp("arbitrary", "arbitrary"),
    )(a, *bs)


def _outproj_bwd(dx1b, wo):
    tm = 512

    def body(dx_ref, wo_ref, da_ref, dp_ref):
        dx = dx_ref[...]
        da_ref[...] = lax.dot_general(dx, wo_ref[0:AW, :], NT, preferred_element_type=f32).astype(bf16)
        dp_ref[...] = lax.dot_general(dx, wo_ref[AW:2 * AW, :], NT, preferred_element_type=f32)

    return pl.pallas_call(
        body, name="outproj_bwd", grid=(T // tm,),
        in_specs=[pl.BlockSpec((tm, D), lambda i: (i, 0)), _full((D, D))],
        out_specs=[pl.BlockSpec((tm, AW), lambda i: (i, 0)), pl.BlockSpec((tm, AW), lambda i: (i, 0))],
        out_shape=[jax.ShapeDtypeStruct((T, AW), bf16), jax.ShapeDtypeStruct((T, AW), f32)],
        compiler_params=_cp("parallel"),
    )(dx1b, wo)


def _pool_bwd(dpool, pooled, wp, scale, dep):
    tm = 512
    n = T // tm

    def body(dp_ref, pb_ref, wp_ref, sc_ref, dep_ref, du_ref, dsc_ref, dwp_ref, ext):
        i = pl.program_id(0)

        @pl.when(i == 0)
        def _():
            ext[tm:tm + HALO, :] = jnp.zeros((HALO, AW), f32)
            dsc_ref[...] = jnp.zeros_like(dsc_ref)
            dwp_ref[...] = jnp.zeros_like(dwp_ref)

        t_idx = (n - 1 - i) * tm + lax.broadcasted_iota(jnp.int32, (tm, 1), 0)
        for g, w in enumerate(WINDOWS):
            lo, hi = 128 * g, 128 * (g + 1)
            pb = pb_ref[:, lo:hi]
            mixed = jnp.dot(pb, wp_ref[g], preferred_element_type=f32)
            dpo = dp_ref[:, lo:hi]
            dsc_ref[:, lo:hi] += jnp.sum(dpo * mixed, axis=0, keepdims=True)
            dmr = (dpo * sc_ref[:, lo:hi]).astype(bf16)
            dwp_ref[g] += lax.dot_general(pb, dmr, TN, preferred_element_type=f32)
            dpl = lax.dot_general(dmr, wp_ref[g], NT, preferred_element_type=f32)
            cnt = jnp.minimum(t_idx + 1, w).astype(f32)
            dpn = dpl / cnt
            ext[0:tm, lo:hi] = dpn
            acc = dpn
            for d in range(1, w):
                acc = acc + ext[d:d + tm, lo:hi]
            du_ref[:, lo:hi] = (acc - dpl).astype(bf16)
        ext[tm:tm + HALO, :] = ext[0:HALO, :]

    rev = lambda i: (n - 1 - i, 0)
    return pl.pallas_call(
        body, name="pool_bwd", grid=(n,),
        in_specs=[pl.BlockSpec((tm, AW), rev), pl.BlockSpec((tm, AW), rev), _full((4, 128, 128)), _full((1, AW)),
                  _full((8, 128))],
        out_specs=[pl.BlockSpec((tm, AW), rev), _full((1, AW)), _full((4, 128, 128))],
        out_shape=[jax.ShapeDtypeStruct((T, AW), bf16), jax.ShapeDtypeStruct((1, AW), f32),
                   jax.ShapeDtypeStruct((4, 128, 128), f32)],
        scratch_shapes=[pltpu.VMEM((tm + HALO, AW), f32)],
        compiler_params=_cp("arbitrary"),
    )(dpool, pooled, wp, scale, dep)


def _attn_bwd(qkv, qaug, kaug, attn, dattn, lse, dep):
    tq = tk = ATT_T
    n = T // tq
    it, jt = _causal_steps(True)
    nsteps = it.shape[0]

    rs = 64

    def body(it_ref, jt_ref, q_ref, k_ref, v_ref, qa_ref, ka_ref, o_ref, do_ref, lse_ref, dep_ref,
             dq_ref, dqs_ref, dk_ref, dks_ref, dv_ref, dq_acc, dk_acc, dv_acc, s_sc, dp_sc, p_sc, ds_sc):
        t = pl.program_id(1)
        i = it_ref[t]
        j = jt_ref[t]

        @pl.when(t == 0)
        def _():
            dq_acc[...] = jnp.zeros_like(dq_acc)

        @pl.when(i == j)
        def _():
            dk_acc[...] = jnp.zeros_like(dk_acc)
            dv_acc[...] = jnp.zeros_like(dv_acc)

        lane = lax.broadcasted_iota(jnp.int32, (tq, 128), 1)

        def step(on_diagonal):
            q = q_ref[...] * 0.125
            k = k_ref[...]
            v = v_ref[...]
            qa = qa_ref[...]
            ka = ka_ref[...]
            do = do_ref[...]
            dd = do.astype(f32) * o_ref[...].astype(f32)
            r0 = pl.multiple_of(i * tq, tq)
            for e in range(2):
                hm = (lane >= 64) if e else (lane < 64)
                qe = jnp.where(hm, q, qa)
                ke = jnp.where(hm, k, ka)
                doe = jnp.where(hm, do, jnp.zeros_like(do))
                delta = jnp.sum(jnp.where(hm, dd, 0.0), axis=1, keepdims=True)
                s_sc[...] = lax.dot_general(qe, ke, NT, preferred_element_type=f32)
                dp_sc[...] = lax.dot_general(doe, v, NT, preferred_element_type=f32)
                for r in range(0, tq, rs):
                    s = s_sc[r:r + rs, :] - lse_ref[r:r + rs, 64 * e:64 * e + 1]
                    if on_diagonal:
                        row = lax.broadcasted_iota(jnp.int32, (rs, tk), 0) + r
                        col = lax.broadcasted_iota(jnp.int32, (rs, tk), 1)
                        s = jnp.where(col <= row, s, NEG)
                    p = jnp.exp(s)
                    p_sc[r:r + rs, :] = p.astype(bf16)
                    ds_sc[r:r + rs, :] = (p * (dp_sc[r:r + rs, :] - delta[r:r + rs, :])).astype(bf16)
                dv_acc[...] += lax.dot_general(p_sc[...], doe, TN, preferred_element_type=f32)
                dsb = ds_sc[...]
                dk_acc[e] += lax.dot_general(dsb, qe, TN, preferred_element_type=f32)
                dq_acc[e, pl.ds(r0, tq), :] += jnp.dot(dsb, ke, preferred_element_type=f32)

        @pl.when(i > j)
        def _():
            step(False)

        @pl.when(i == j)
        def _():
            step(True)

        @pl.when(i == n - 1)
        def _():
            dk_ref[...] = jnp.where(lane < 64, dk_acc[0], dk_acc[1]).astype(bf16)
            dks_ref[...] = jnp.where(lane < 64, dk_acc[1], dk_acc[0])
            dv_ref[...] = dv_acc[...].astype(bf16)

        @pl.when(t == nsteps - 1)
        def _():
            lane_t = lax.broadcasted_iota(jnp.int32, (T, 128), 1)
            dq_ref[...] = (jnp.where(lane_t < 64, dq_acc[0], dq_acc[1]) * 0.125).astype(bf16)
            dqs_ref[...] = jnp.where(lane_t < 64, dq_acc[1], dq_acc[0])

    qmap = lambda p, t, it, jt: (it[t], p)
    grid_spec = pltpu.PrefetchScalarGridSpec(
        num_scalar_prefetch=2, grid=(PAIRS, nsteps),
        in_specs=[pl.BlockSpec((tq, 128), qmap),
                  pl.BlockSpec((tk, 128), lambda p, t, it, jt: (jt[t], PAIRS + p)),
                  pl.BlockSpec((tk, 128), lambda p, t, it, jt: (jt[t], 2 * PAIRS + p)),
                  pl.BlockSpec((tq, 128), qmap), pl.BlockSpec((tk, 128), lambda p, t, it, jt: (jt[t], p)),
                  pl.BlockSpec((tq, 128), qmap), pl.BlockSpec((tq, 128), qmap),
                  pl.BlockSpec((None, tq, 128), lambda p, t, it, jt: (p, it[t], 0)),
                  pl.BlockSpec((8, 128), lambda p, t, it, jt: (0, 0))],
        out_specs=[pl.BlockSpec((T, 128), lambda p, t, it, jt: (0, p)),
                   pl.BlockSpec((None, T, 128), lambda p, t, it, jt: (p, 0, 0)),
                   pl.BlockSpec((tk, 128), lambda p, t, it, jt: (jt[t], p)),
                   pl.BlockSpec((None, tk, 128), lambda p, t, it, jt: (p, jt[t], 0)),
                   pl.BlockSpec((tk, 128), lambda p, t, it, jt: (jt[t], p))],
        scratch_shapes=[pltpu.VMEM((2, T, 128), f32), pltpu.VMEM((2, tk, 128), f32), pltpu.VMEM((tk, 128), f32),
                        pltpu.VMEM((tq, tk), f32), pltpu.VMEM((tq, tk), f32), pltpu.VMEM((tq, tk), bf16),
                        pltpu.VMEM((tq, tk), bf16)],
    )
    return pl.pallas_call(
        body, name="fox_attn_bwd", grid_spec=grid_spec,
        out_shape=[jax.ShapeDtypeStruct((T, AW), bf16), jax.ShapeDtypeStruct((PAIRS, T, 128), f32),
                   jax.ShapeDtypeStruct((T, AW), bf16), jax.ShapeDtypeStruct((PAIRS, T, 128), f32),
                   jax.ShapeDtypeStruct((T, AW), bf16)],
        compiler_params=_cp("parallel", "arbitrary"),
    )(it, jt, qkv, qkv, qkv, qaug, kaug, attn, dattn, lse, dep)


def _fox_cumsum_bwd(dqs, dks, fl, bfp):
    tb = CUMSUM_ROWS
    nb = T // tb
    hp = lax.Precision.HIGHEST

    def body(dqs_ref, dks_ref, fl_ref, b_ref, df_ref, db_ref, carry):
        i = pl.program_id(0)

        @pl.when(i == 0)
        def _():
            carry[...] = jnp.zeros_like(carry)
            db_ref[...] = jnp.zeros_like(db_ref)

        r = lax.broadcasted_iota(jnp.int32, (128, 128), 0)
        cc = lax.broadcasted_iota(jnp.int32, (128, 128), 1)
        pick = lambda even_lane, odd_lane, p: jnp.logical_or(
            jnp.logical_and(r == even_lane, cc == 2 * p), jnp.logical_and(r == odd_lane, cc == 2 * p + 1)).astype(f32)
        dc = jnp.zeros((tb, 128), f32)
        for p in range(PAIRS):
            dc = dc + jnp.dot(dqs_ref[p], pick(64, 0, p), precision=hp, preferred_element_type=f32)
            dc = dc - jnp.dot(dks_ref[p], pick(67, 3, p), precision=hp, preferred_element_type=f32)
        rt = lax.broadcasted_iota(jnp.int32, (tb, tb), 0)
        ct = lax.broadcasted_iota(jnp.int32, (tb, tb), 1)
        utri = (ct >= rt).astype(f32)
        dl = jnp.dot(utri, dc, precision=hp, preferred_element_type=f32) + carry[0:1, :]
        carry[...] = jnp.broadcast_to(dl[0:1, :], (8, 128))
        z = fl_ref[...] + b_ref[...]
        df = dl * jax.nn.sigmoid(-z)
        df_ref[...] = df.astype(bf16)
        db_ref[...] += jnp.sum(df, axis=0, keepdims=True)

    rev = lambda i: (nb - 1 - i, 0)
    return pl.pallas_call(
        body, name="fox_cumsum_bwd", grid=(nb,),
        in_specs=[pl.BlockSpec((PAIRS, tb, 128), lambda i: (0, nb - 1 - i, 0)),
                  pl.BlockSpec((PAIRS, tb, 128), lambda i: (0, nb - 1 - i, 0)),
                  pl.BlockSpec((tb, 128), rev), _full((1, 128))],
        out_specs=[pl.BlockSpec((tb, 128), rev), _full((1, 128))],
        out_shape=[jax.ShapeDtypeStruct((T, 128), bf16), jax.ShapeDtypeStruct((1, 128), f32)],
        scratch_shapes=[pltpu.VMEM((8, 128), f32)],
        compiler_params=_cp("arbitrary"),
    )(dqs, dks, fl, bfp)


def _inproj_bwd(dq, dk, dv, du, df, wm, wf, x, dx1, g1):
    tm = 512

    def body(dq_ref, dk_ref, dv_ref, du_ref, df_ref, wm_ref, wf_ref, x_ref, dx1_ref, g_ref, dx_ref, dn_ref):
        i = pl.program_id(0)

        @pl.when(i == 0)
        def _():
            dn_ref[...] = jnp.zeros_like(dn_ref)

        dh = lax.dot_general(dq_ref[...], wm_ref[:, 0:AW], NT, preferred_element_type=f32)
        dh = dh + lax.dot_general(dk_ref[...], wm_ref[:, AW:2 * AW], NT, preferred_element_type=f32)
        dh = dh + lax.dot_general(dv_ref[...], wm_ref[:, 2 * AW:3 * AW], NT, preferred_element_type=f32)
        dh = dh + lax.dot_general(du_ref[...], wm_ref[:, 3 * AW:4 * AW], NT, preferred_element_type=f32)
        dh = dh + lax.dot_general(df_ref[...], wf_ref[...], NT, preferred_element_type=f32)
        xv = x_ref[...]
        r = lax.rsqrt(jnp.mean(xv * xv, axis=-1, keepdims=True) + EPS)
        xhat = xv * r
        dn_ref[...] += jnp.sum(dh * xhat, axis=0, keepdims=True)
        z = dh * g_ref[...]
        dx_ref[...] = dx1_ref[...] + r * (z - xhat * jnp.mean(z * xhat, axis=-1, keepdims=True))

    row = lambda i: (i, 0)
    return pl.pallas_call(
        body, name="inproj_bwd", grid=(T // tm,),
        in_specs=[pl.BlockSpec((tm, AW), row)] * 4 + [pl.BlockSpec((tm, 128), row), _full((D, 4 * AW)), _full((D, 128)),
                                                       pl.BlockSpec((tm, D), row), pl.BlockSpec((tm, D), row), _full((1, D))],
        out_specs=[pl.BlockSpec((tm, D), row), _full((1, D))],
        out_shape=[jax.ShapeDtypeStruct((T, D), f32), jax.ShapeDtypeStruct((1, D), f32)],
        compiler_params=_cp("arbitrary"),
    )(dq, dk, dv, du, df, wm, wf, x, dx1, g1)


def _adamw_math(w, g, m, v):
    m = B1 * m + (1.0 - B1) * g
    v = B2 * v + (1.0 - B2) * (g * g)
    m_hat = m / (1.0 - B1 ** STEP)
    v_hat = v / (1.0 - B2 ** STEP)
    delta = -LR * (m_hat / (jnp.sqrt(v_hat) + AEPS) + WD * w)
    return delta, m, v


def _adamw_shard(w, m, v, p_mine, p_other, name):
    rows, cols = w.shape
    tr = 256 if rows % 256 == 0 else 176

    def body(w_ref, m_ref, v_ref, a_ref, b_ref, g_ref, d_ref, nm_ref, nv_ref):
        g = a_ref[...] + b_ref[...]
        g_ref[...] = g
        d_ref[...], nm_ref[...], nv_ref[...] = _adamw_math(w_ref[...], g, m_ref[...], v_ref[...])

    spec = pl.BlockSpec((tr, cols), lambda i: (i, 0))
    return pl.pallas_call(
        body, name=name, grid=(rows // tr,), in_specs=[spec] * 5, out_specs=[spec] * 4,
        out_shape=[jax.ShapeDtypeStruct((rows, cols), f32)] * 4, compiler_params=_cp("parallel"),
    )(w, m, v, p_mine, p_other)


SMALL_SLOTS = ((0, 8, 128), (8, 16, 128), (16, 24, 128), (24, 28, 128), (32, 33, 8), (40, 552, 128))
LOSS_ROW = 39


def _adamw_small(ws, ms, vs, parts):
    n = len(ws)

    def body(*refs):
        w_refs, m_refs, v_refs, p_ref = refs[0:n], refs[n:2 * n], refs[2 * n:3 * n], refs[3 * n]
        outs = refs[3 * n + 1:]
        g_all = p_ref[0]
        for k in range(1, 8):
            g_all = g_all + p_ref[k]
        for idx, (r0, r1, lanes) in enumerate(SMALL_SLOTS):
            g = g_all[r0:r1, 0:lanes]
            d, nm, nv = _adamw_math(w_refs[idx][...], g, m_refs[idx][...], v_refs[idx][...])
            outs[idx][...] = g
            outs[n + idx][...] = d
            outs[2 * n + idx][...] = nm
            outs[3 * n + idx][...] = nv
        outs[4 * n][...] = g_all[LOSS_ROW:LOSS_ROW + 1, :]

    shapes = [jax.ShapeDtypeStruct(w.shape, f32) for w in ws]
    res = pl.pallas_call(
        body, name="adamw_small", out_shape=shapes * 4 + [jax.ShapeDtypeStruct((1, 128), f32)],
    )(*ws, *ms, *vs, parts)
    return res[:4 * n], res[4 * n]


def _sum4(recv, g, mine, name):
    _, rows, cols = recv.shape
    tr = 256 if rows % 256 == 0 else 176

    def body(mine_ref, r_ref, g_ref, o_ref):
        o_ref[...] = ((g_ref[...].astype(f32) + r_ref[0].astype(f32))
                      + (r_ref[1].astype(f32) + r_ref[2].astype(f32)))

    grid_spec = pltpu.PrefetchScalarGridSpec(
        num_scalar_prefetch=1, grid=(rows // tr,),
        in_specs=[pl.BlockSpec((3, tr, cols), lambda i, m: (0, i, 0)),
                  pl.BlockSpec((None, tr, cols), lambda i, m: (m[0], i, 0))],
        out_specs=pl.BlockSpec((tr, cols), lambda i, m: (i, 0)))
    return pl.pallas_call(
        body, name=name, grid_spec=grid_spec, out_shape=jax.ShapeDtypeStruct((rows, cols), f32),
        compiler_params=_cp("arbitrary"),
    )(mine, recv, g)


_HBM = pl.BlockSpec(memory_space=pltpu.HBM)
_SEM = pl.BlockSpec(memory_space=pltpu.SEMAPHORE)
_EFFECT = pltpu.SideEffectType.DATAFLOW_SIDE_EFFECTING


def _in_hbm(a):
    return pltpu.with_memory_space_constraint(a, pltpu.HBM)


def _mesh_pos():
    return lax.axis_index("x"), lax.axis_index("y"), lax.axis_index("c")


def _other_chips(x, y):
    return [(1 - x, y), (x, 1 - y), (1 - x, 1 - y)]


def _gather_copy(srcs, lands, send_sems, recv_sems, a, k, slot):
    x, y, c = _mesh_pos()
    cx, cy = _other_chips(x, y)[k]
    return pltpu.make_async_remote_copy(
        src_ref=srcs[a], dst_ref=lands[a].at[slot], send_sem=send_sems.at[3 * a + k], recv_sem=recv_sems.at[3 * a + k],
        device_id=(cx, cy, c), device_id_type=MESH)


def _scatter_copy(srcs, lands, send_sems, recv_sems, a, k):
    x, y, c = _mesh_pos()
    cx, cy = _other_chips(x, y)[k]
    return pltpu.make_async_remote_copy(
        src_ref=srcs[a].at[2 * cx + cy], dst_ref=lands[a].at[k], send_sem=send_sems.at[3 * a + k],
        recv_sem=recv_sems.at[3 * a + k], device_id=(cx, cy, c), device_id_type=MESH)


def _all_gather_w_in(part):
    rows = part.shape[0] // 2

    def body(src, dst, send_sems, recv_sems, loc_sem):
        x, y, c = _mesh_pos()
        mine = 2 * x + y
        chips = _other_chips(x, y)
        half = lambda ref, cc: ref.at[pl.ds(pl.multiple_of(cc * rows, rows), rows), :]

        def over_ici(k, slot):
            cx, cy = chips[k]
            return pltpu.make_async_remote_copy(
                src_ref=half(src, c), dst_ref=half(dst.at[slot], c), send_sem=send_sems.at[k], recv_sem=recv_sems.at[k],
                device_id=(cx, cy, c), device_id_type=MESH)

        def to_sibling(k, cc):
            slot = 2 * chips[k][0] + chips[k][1]
            return pltpu.make_async_remote_copy(
                src_ref=half(dst.at[slot], cc), dst_ref=half(dst.at[slot], cc), send_sem=send_sems.at[3 + k],
                recv_sem=recv_sems.at[3 + k], device_id=(x, y, 1 - c), device_id_type=MESH)

        local = pltpu.make_async_copy(src, dst.at[mine], loc_sem.at[0])
        local.start()
        first = [over_ici(k, mine) for k in range(3)]
        for cp in first:
            cp.start()
        passed = [to_sibling(k, c) for k in range(3)]
        for k in range(3):
            over_ici(k, 2 * chips[k][0] + chips[k][1]).wait_recv()
            passed[k].start()
        for k in range(3):
            to_sibling(k, 1 - c).wait_recv()
        for cp in first + passed:
            cp.wait_send()
        local.wait()

    return pl.pallas_call(
        body, name="all_gather_w_in", in_specs=[_HBM], out_specs=_HBM,
        out_shape=jax.ShapeDtypeStruct((NSH,) + part.shape, part.dtype),
        scratch_shapes=[pltpu.SemaphoreType.DMA((6,)), pltpu.SemaphoreType.DMA((6,)), pltpu.SemaphoreType.DMA((1,))],
    )(part)


def _split_start(name, srcs, lands, n_sems, plan, dep):
    n, nl = len(srcs), len(lands)

    def body(*refs):
        src_refs, land_refs = refs[:n], refs[n:n + nl]
        send_sems, recv_sems = refs[n + nl + 1], refs[n + nl + 2]
        token = refs[-1]
        sends, _ = plan(src_refs, land_refs, send_sems, recv_sems)
        for cp in sends:
            cp.start()
        token[...] = jnp.zeros_like(token)

    outs = pl.pallas_call(
        body, name=name,
        in_specs=[_HBM] * (n + nl) + [pl.BlockSpec(memory_space=pl.ANY)],
        out_specs=[_SEM, _SEM] + [_HBM] * (n + nl) + [pl.BlockSpec(memory_space=pltpu.VMEM)],
        out_shape=[pltpu.SemaphoreType.DMA((n_sems,)), pltpu.SemaphoreType.DMA((n_sems,))]
        + [pltpu.HBM(a.shape, a.dtype) for a in list(srcs) + list(lands)] + [jax.ShapeDtypeStruct((8, 128), f32)],
        input_output_aliases={i: 2 + i for i in range(n + nl)},
        compiler_params=pltpu.CompilerParams(has_side_effects=_EFFECT),
    )(*[_in_hbm(a) for a in list(srcs) + list(lands)], dep)
    return outs[0], outs[1], list(outs[2:2 + n]), list(outs[2 + n:2 + n + nl]), outs[-1]


def _split_wait(name, send_sems, recv_sems, srcs, lands, after, plan):
    n, nl = len(srcs), len(lands)

    def body(*refs):
        src_refs, land_refs = refs[:n], refs[n:n + nl]
        s_sems, r_sems = refs[n + nl], refs[n + nl + 1]
        sends, recvs = plan(src_refs, land_refs, s_sems, r_sems)
        for cp in recvs:
            cp.wait_recv()
        for cp in sends:
            cp.wait_send()

    outs = pl.pallas_call(
        body, name=name,
        in_specs=[_HBM] * (n + nl) + [_SEM, _SEM, pl.BlockSpec(memory_space=pl.ANY)],
        out_specs=[_HBM] * (n + nl),
        out_shape=[pltpu.HBM(a.shape, a.dtype) for a in list(srcs) + list(lands)],
        input_output_aliases={i: i for i in range(n + nl)},
        compiler_params=pltpu.CompilerParams(has_side_effects=_EFFECT),
    )(*srcs, *lands, send_sems, recv_sems, after)
    return list(outs[:n]), list(outs[n:])


def _gather_plan(srcs, lands, ss, rs):
    x, y, _ = _mesh_pos()
    chips = _other_chips(x, y)
    sends = [_gather_copy(srcs, lands, ss, rs, a, k, 2 * x + y) for a in range(len(srcs)) for k in range(3)]
    recvs = [_gather_copy(srcs, lands, ss, rs, a, k, 2 * chips[k][0] + chips[k][1])
             for a in range(len(srcs)) for k in range(3)]
    return sends, recvs


def _scatter_plan(srcs, lands, ss, rs):
    cps = [_scatter_copy(srcs, lands, ss, rs, a, k) for a in range(len(srcs)) for k in range(3)]
    return cps, cps


def _tail_plan(srcs, lands, ss, rs):
    x, y, c = _mesh_pos()
    me = 4 * x + 2 * y + c
    cps = [_scatter_copy(srcs[:1], lands[:1], ss, rs, 0, k) for k in range(3)]
    for f in range(1, 8):
        peer = ((x + (f >> 2)) % 2, (y + ((f >> 1) & 1)) % 2, (c + (f & 1)) % 2)
        cps.append(pltpu.make_async_remote_copy(
            src_ref=srcs[1], dst_ref=lands[1].at[me], send_sem=ss.at[2 + f], recv_sem=rs.at[2 + f],
            device_id=peer, device_id_type=MESH))
    return cps, cps


def _swap_with_sibling(parts, name):
    n = len(parts)

    def body(*refs):
        srcs, dsts = refs[:n], refs[n:2 * n]
        send_sems, recv_sems = refs[2 * n:]
        x, y, c = _mesh_pos()
        cps = [pltpu.make_async_remote_copy(src_ref=srcs[a], dst_ref=dsts[a], send_sem=send_sems.at[a],
                                            recv_sem=recv_sems.at[a], device_id=(x, y, 1 - c), device_id_type=MESH)
               for a in range(n)]
        for cp in cps:
            cp.start()
        for cp in cps:
            cp.wait_recv()
        for cp in cps:
            cp.wait_send()

    return pl.pallas_call(
        body, name=name, in_specs=[_HBM] * n, out_specs=[_HBM] * n,
        out_shape=[jax.ShapeDtypeStruct(p.shape, p.dtype) for p in parts],
        scratch_shapes=[pltpu.SemaphoreType.DMA((n,)), pltpu.SemaphoreType.DMA((n,))],
    )(*parts)


def _forward(x, tgt, wm, wf, mlp_w_fn, g1, bfp, wp, scale, g2, gf, dep):
    h, qkv, u, fl = _rms_inproj(x, g1, wm, wf, dep)
    qaug, kaug = _fox_cumsum(fl, bfp)
    attn, lse = _attn_fwd(qkv, qaug, kaug)
    pooled, pool = _pool_fwd(u, wp, scale)
    wo, wgt, wut, wd = mlp_w_fn(attn)
    x1, h2 = _outproj(x, attn, pool, wo, g2)
    loss, dgf, dx2, dx2b, gate, up, a_b = _mlp_fwd_loss(h2, x1, wgt, wut, wd, tgt, gf)
    saved = dict(h=h, qkv=qkv, fl=fl, qaug=qaug, kaug=kaug, attn=attn, lse=lse, pooled=pooled, pool=pool, x1=x1, h2=h2,
                 gate=gate, up=up, a_b=a_b, wo=wo, wgt=wgt, wut=wut, wd=wd)
    return loss, dgf, dx2, dx2b, saved


def _backward_mlp(sv, dx2, dx2b, g2):
    dgate, dup, dx1, dx1b, dg2 = _mlp_bwd(dx2b, dx2, sv["gate"], sv["up"], sv["wgt"], sv["wut"], sv["wd"], sv["x1"], g2)
    (dwd,) = _mm_tn(sv["a_b"], [dx2b], "dw_down", a_sharded=True, tk=2048)
    (dwgt,) = _mm_tn(dgate, [sv["h2"]], "dw_gate", a_sharded=True, tk=2048)
    (dwut,) = _mm_tn(dup, [sv["h2"]], "dw_up", a_sharded=True, tk=2048)
    return dx1, dx1b, dg2, (dwgt, dwut, dwd)


def _backward_outproj(sv, dx1b):
    dattn, dpool = _outproj_bwd(dx1b, sv["wo"])
    dwo_a, = _mm_tn(sv["attn"], [dx1b], "dw_out_attn", tk=2048)
    dwo_p, = _mm_tn(sv["pool"], [dx1b], "dw_out_pool", tk=2048)
    dwo = jnp.concatenate([dwo_a, dwo_p], axis=0).reshape(NSH, D // NSH, D)
    return dattn, dpool, dwo


def _backward_mixer(sv, x, dx1, dattn, dpool, wm, wf, g1, bfp, wp, scale, dep):
    du, dscale, dwp = _pool_bwd(dpool, sv["pooled"], wp, scale, dep)
    dq, dqs, dk, dks, dv = _attn_bwd(sv["qkv"], sv["qaug"], sv["kaug"], sv["attn"], dattn, sv["lse"], dep)
    df, dbf = _fox_cumsum_bwd(dqs, dks, sv["fl"], bfp)
    dx, dg1 = _inproj_bwd(dq, dk, dv, du, df, wm, wf, x, dx1, g1)
    dwq, dwk, dwv, dwu_in, dwf = _mm_tn(sv["h"], [dq, dk, dv, du, df], "dw_in", tk=1024)
    dwin = jnp.concatenate([dwq, dwk, dwv, dwf[:, 0:8], dwu_in], axis=1)
    dwin = dwin.reshape(D, NSH, IN_S).transpose(1, 0, 2)
    return dx, dg1, dscale, dwp, dbf, dwin


def kernel(x, norm1_g, w_in, b_forget, w_pool, pool_scale, w_out, norm2_g, w_gate, w_up, w_down, final_g, loss_target, m_norm1_g, m_w_in, m_b_forget, m_w_pool, m_pool_scale, m_w_out, m_norm2_g, m_w_gate, m_w_up, m_w_down, m_final_g, v_norm1_g, v_w_in, v_b_forget, v_w_pool, v_pool_scale, v_w_out, v_norm2_g, v_w_gate, v_w_up, v_w_down, v_final_g):
    mine = (2 * lax.axis_index("x") + lax.axis_index("y")).astype(jnp.int32)
    mine1 = mine.reshape(1)
    tr = lambda a: jnp.transpose(a[0])

    win4 = _all_gather_w_in(w_in[0].astype(bf16))
    later = [w_out[0].astype(bf16), tr(w_gate).astype(bf16), tr(w_up).astype(bf16), w_down[0].astype(bf16)]
    lands = [lax.dynamic_update_slice(lax.empty((NSH,) + p.shape, bf16), p[None], (mine, 0, 0)) for p in later]
    ag_send, ag_recv, later_thru, lands_thru, ag_token = _split_start("all_gather_start", later, lands, 12, _gather_plan,
                                                                      win4)
    win = win4.transpose(1, 0, 2).reshape(D, IN_W)
    wm = jnp.concatenate([win[:, 0:3 * AW], win[:, 3 * AW + 8:]], axis=1)
    wf = jnp.pad(win[:, 3 * AW:3 * AW + 8], ((0, 0), (0, 120)))
    bfp = jnp.pad(b_forget, ((0, 0), (0, 120)))
    wp = w_pool[0].astype(bf16)
    gf = final_g.reshape(1, D)

    def later_weights(after):
        _, (wo4, wgt, wut, wd) = _split_wait("all_gather_wait", ag_send, ag_recv, later_thru, lands_thru, after, _gather_plan)
        return wo4.reshape(D, D), wgt, wut, wd

    xe, tgt = x[0], loss_target[0]
    loss_v, dgf, dx2, dx2b, sv = _forward(xe, tgt, wm, wf, later_weights, norm1_g, bfp, wp, pool_scale, norm2_g, gf, ag_token)
    dx1, dx1b, dg2, mlp_grads = _backward_mlp(sv, dx2, dx2b, norm2_g)
    dattn, dpool, dwo = _backward_outproj(sv, dx1b)
    first = [dwo] + list(mlp_grads)
    first_lands = [lax.empty((3,) + g.shape[1:], bf16) for g in first]
    rs_send, rs_recv, first_thru, first_lands_thru, rs_token = _split_start("reduce_scatter_start", first, first_lands, 12,
                                                                            _scatter_plan, dattn)
    dx, dg1, dscale, dwp, dbf, dwin = _backward_mixer(sv, xe, dx1, dattn, dpool, wm, wf, norm1_g, bfp, wp, pool_scale, rs_token)

    me = (4 * lax.axis_index("x") + 2 * lax.axis_index("y") + lax.axis_index("c")).astype(jnp.int32)
    pad8 = lambda r: jnp.pad(r, ((0, 8 - r.shape[0]), (0, 0)))
    loss_rows = jnp.concatenate([dbf, jnp.zeros((6, 128), f32), loss_v[0:1, :]], axis=0)
    small = jnp.concatenate([dg1.reshape(8, 128), dg2.reshape(8, 128), dgf.reshape(8, 128), pad8(dscale.reshape(4, 128)),
                             loss_rows, dwp.reshape(512, 128)], axis=0)
    small_land = lax.dynamic_update_slice(lax.empty((8, SMALL_ROWS, 128), f32), small[None], (me, 0, 0))
    tail_send, tail_recv, tail_thru, tail_lands_thru, tail_token = _split_start(
        "tail_start", [dwin, small], [lax.empty((3,) + dwin.shape[1:], bf16), small_land], 10, _tail_plan, dx)
    first_thru, first_recv = _split_wait("reduce_scatter_wait", rs_send, rs_recv, first_thru, first_lands_thru, tail_token,
                                         _scatter_plan)
    ws = [w_in[0], w_out[0], tr(w_gate), tr(w_up), w_down[0]]
    ms = [m_w_in[0], m_w_out[0], tr(m_w_gate), tr(m_w_up), m_w_down[0]]
    vs = [v_w_in[0], v_w_out[0], tr(v_w_gate), tr(v_w_up), v_w_down[0]]
    partial = [_sum4(r, g, mine1, f"sum4_{i + 1}") for i, (r, g) in enumerate(zip(first_recv, first_thru))]
    other = _swap_with_sibling(partial, "swap_first")
    big = [_adamw_shard(ws[i + 1], ms[i + 1], vs[i + 1], partial[i], other[i], f"adamw_{i + 1}") for i in range(4)]
    (dwin_thru, _), (in_recv_land, small_all) = _split_wait("tail_wait", tail_send, tail_recv, tail_thru, tail_lands_thru,
                                                            big[3][0], _tail_plan)
    partial_in = _sum4(in_recv_land, dwin_thru, mine1, "sum4_0")
    (other_in,) = _swap_with_sibling([partial_in], "swap_in")
    big = [_adamw_shard(ws[0], ms[0], vs[0], partial_in, other_in, "adamw_0")] + big

    small_names = ["norm1_g", "norm2_g", "final_g", "pool_scale", "b_forget", "w_pool"]
    rows = lambda a, b, c, d, e, f: [a.reshape(8, 128), b.reshape(8, 128), c.reshape(8, 128), d.reshape(4, 128),
                                     e.reshape(1, 8), f.reshape(512, 128)]
    sm, loss_row = _adamw_small(rows(norm1_g, norm2_g, final_g, pool_scale, b_forget, w_pool),
                                rows(m_norm1_g, m_norm2_g, m_final_g, m_pool_scale, m_b_forget, m_w_pool),
                                rows(v_norm1_g, v_norm2_g, v_final_g, v_pool_scale, v_b_forget, v_w_pool), small_all)
    small_shape = dict(norm1_g=(1, D), norm2_g=(1, D), final_g=(D,), pool_scale=(1, AW), b_forget=(1, 8),
                       w_pool=(1, 4, 128, 128))

    order = ["norm1_g", "w_in", "b_forget", "w_pool", "pool_scale", "w_out", "norm2_g", "w_gate", "w_up", "w_down", "final_g"]
    big_idx = {"w_in": 0, "w_out": 1, "w_gate": 2, "w_up": 3, "w_down": 4}
    outs = [loss_row[0, 0], dx[None]]
    for kind in range(4):
        for name in order:
            if name in ("w_gate", "w_up"):
                outs.append(jnp.transpose(big[big_idx[name]][kind])[None])
            elif name in big_idx:
                outs.append(big[big_idx[name]][kind][None])
            else:
                outs.append(sm[6 * kind + small_names.index(name)].reshape(small_shape[name]))
    return tuple(outs)
```

```python
import functools

import jax
import jax.numpy as jnp
import numpy as np
from jax import lax
from jax.experimental import pallas as pl
from jax.experimental.pallas import tpu as pltpu

f32 = jnp.float32
bf16 = jnp.bfloat16

T = 4096
D = 1024
NSH = 4
IN_W = 2056
IN_S = IN_W // NSH
AW = 512
PAIRS = 4
FF = 2816
FS = FF // NSH
WINDOWS = (2, 4, 8, 16)
HALO = 16
EPS = 1e-6
NEG = -1e30
LR, B1, B2, AEPS, WD, STEP = 0.001, 0.9, 0.999, 1e-08, 0.01, 10
SMALL_ROWS = 552

NT = (((1,), (1,)), ((), ()))
TN = (((0,), (0,)), ((), ()))

MESH = pl.DeviceIdType.MESH


def _cp(*sem):
    return pltpu.CompilerParams(dimension_semantics=sem)


def _full(shape):
    n = len(shape)
    return pl.BlockSpec(shape, lambda *_: (0,) * n)


def _rms_inproj(x, g1, wm, wf, dep):
    tm = 512

    def body(x_ref, g_ref, wm_ref, wf_ref, dep_ref, h_ref, qkv_ref, u_ref, fl_ref):
        xv = x_ref[...]
        r = lax.rsqrt(jnp.mean(xv * xv, axis=-1, keepdims=True) + EPS)
        h = (xv * r * g_ref[...]).astype(bf16)
        h_ref[...] = h
        qkv_ref[...] = jnp.dot(h, wm_ref[:, 0:3 * AW], preferred_element_type=f32).astype(bf16)
        u_ref[...] = jnp.dot(h, wm_ref[:, 3 * AW:4 * AW], preferred_element_type=f32)
        fl_ref[...] = jnp.dot(h, wf_ref[...], preferred_element_type=f32)

    return pl.pallas_call(
        body, name="rms_inproj", grid=(T // tm,),
        in_specs=[pl.BlockSpec((tm, D), lambda i: (i, 0)), _full((1, D)), _full((D, 4 * AW)), _full((D, 128)),
                  _full((8, 128))],
        out_specs=[pl.BlockSpec((tm, D), lambda i: (i, 0)), pl.BlockSpec((tm, 3 * AW), lambda i: (i, 0)),
                   pl.BlockSpec((tm, AW), lambda i: (i, 0)), pl.BlockSpec((tm, 128), lambda i: (i, 0))],
        out_shape=[jax.ShapeDtypeStruct((T, D), bf16), jax.ShapeDtypeStruct((T, 3 * AW), bf16),
                   jax.ShapeDtypeStruct((T, AW), f32), jax.ShapeDtypeStruct((T, 128), f32)],
        compiler_params=_cp("parallel"),
    )(x, g1, wm, wf, dep)


CUMSUM_ROWS = 512
FS_CHUNKS = ((0, 256), (256, 512), (512, FS))


def _log_sigmoid(z):
    return jnp.minimum(z, 0.0) - jnp.log(1.0 + jnp.exp(-jnp.abs(z)))


def _fox_cumsum(fl, bfp):
    tb = CUMSUM_ROWS
    nb = T // tb

    def body(fl_ref, b_ref, qa_ref, ka_ref, carry):
        i = pl.program_id(0)

        @pl.when(i == 0)
        def _():
            carry[...] = jnp.zeros_like(carry)

        lf = _log_sigmoid(fl_ref[...] + b_ref[...])
        r = lax.broadcasted_iota(jnp.int32, (tb, tb), 0)
        cc = lax.broadcasted_iota(jnp.int32, (tb, tb), 1)
        ltri = (cc <= r).astype(f32)
        cb = jnp.dot(ltri, lf, precision=lax.Precision.HIGHEST, preferred_element_type=f32) + carry[0:1, :]
        carry[...] = jnp.broadcast_to(cb[tb - 1:tb, :], (8, 128))
        hi = cb.astype(bf16)
        r1 = cb - hi.astype(f32)
        mid = r1.astype(bf16)
        lo = (r1 - mid.astype(f32)).astype(bf16)
        head = lax.broadcasted_iota(jnp.int32, (128, AW), 0)
        col = lax.broadcasted_iota(jnp.int32, (128, AW), 1)
        base = 128 * (head >> 1) + 64 * (1 - (head & 1))
        place = lambda off: jnp.logical_and(col == base + off, head < 8).astype(bf16)
        mm = lambda a, off: jnp.dot(a, place(off), preferred_element_type=f32)
        cq = mm(hi, 0) + mm(mid, 1) + mm(lo, 2)
        ck = mm(hi, 3) + mm(mid, 4) + mm(lo, 5)
        within = jnp.bitwise_and(lax.broadcasted_iota(jnp.int32, (tb, AW), 1), 63)
        qa_ref[...] = jnp.where(jnp.logical_and(within >= 3, within <= 5), 1.0, cq).astype(bf16)
        ka_ref[...] = jnp.where(within <= 2, 1.0, -ck).astype(bf16)

    return pl.pallas_call(
        body, name="fox_cumsum", grid=(nb,),
        in_specs=[pl.BlockSpec((tb, 128), lambda i: (i, 0)), _full((1, 128))],
        out_specs=[pl.BlockSpec((tb, AW), lambda i: (i, 0)), pl.BlockSpec((tb, AW), lambda i: (i, 0))],
        out_shape=[jax.ShapeDtypeStruct((T, AW), bf16), jax.ShapeDtypeStruct((T, AW), bf16)],
        scratch_shapes=[pltpu.VMEM((8, 128), f32)],
        compiler_params=_cp("arbitrary"),
    )(fl, bfp)


ATT_T = 512


def _causal_steps(key_major):
    n = T // ATT_T
    if key_major:
        pairs = [(i, j) for j in range(n) for i in range(j, n)]
    else:
        pairs = [(i, j) for i in range(n) for j in range(i + 1)]
    it = np.array([p[0] for p in pairs], np.int32)
    jt = np.array([p[1] for p in pairs], np.int32)
    return jnp.asarray(it), jnp.asarray(jt)


def _attn_fwd(qkv, qaug, kaug):
    tq = tk = ATT_T
    it, jt = _causal_steps(False)
    nsteps = it.shape[0]

    rs = 64

    def body(it_ref, jt_ref, q_ref, k_ref, v_ref, qa_ref, ka_ref, o_ref, lse_ref, m_sc, acc_sc, s_sc, p_sc, alpha_sc):
        t = pl.program_id(1)
        i = it_ref[t]
        j = jt_ref[t]

        @pl.when(j == 0)
        def _():
            m_sc[...] = jnp.full_like(m_sc, NEG)
            acc_sc[...] = jnp.zeros_like(acc_sc)

        lane = lax.broadcasted_iota(jnp.int32, (tq, 128), 1)
        spare = (64, 0)

        def step(on_diagonal):
            q = q_ref[...] * 0.125
            k = k_ref[...]
            v = v_ref[...]
            qa = qa_ref[...]
            ka = ka_ref[...]
            for e in range(2):
                hm = (lane >= 64) if e else (lane < 64)
                s_sc[...] = lax.dot_general(jnp.where(hm, q, qa), jnp.where(hm, k, ka), NT, preferred_element_type=f32)
                for r in range(0, tq, rs):
                    s = s_sc[r:r + rs, :]
                    if on_diagonal:
                        row = lax.broadcasted_iota(jnp.int32, (rs, tk), 0) + r
                        col = lax.broadcasted_iota(jnp.int32, (rs, tk), 1)
                        s = jnp.where(col <= row, s, NEG)
                    m_prev = m_sc[e, r:r + rs, :]
                    m_new = jnp.maximum(m_prev, jnp.max(s, axis=1, keepdims=True))
                    p_sc[r:r + rs, :] = jnp.exp(s - jnp.tile(m_new, (1, tk // 128))).astype(bf16)
                    alpha_sc[r:r + rs, :] = jnp.exp(m_prev - m_new)
                    m_sc[e, r:r + rs, :] = m_new
                ve = jnp.where(hm, v, (lane == spare[e]).astype(bf16))
                acc_sc[e] = alpha_sc[...] * acc_sc[e] + jnp.dot(p_sc[...], ve, preferred_element_type=f32)

        @pl.when(j < i)
        def _():
            step(False)

        @pl.when(j == i)
        def _():
            step(True)
            l0 = acc_sc[0][:, spare[0]:spare[0] + 1]
            l1 = acc_sc[1][:, spare[1]:spare[1] + 1]
            o_ref[...] = jnp.where(lane < 64, acc_sc[0] / l0, acc_sc[1] / l1).astype(bf16)
            lse_ref[...] = jnp.where(lane < 64, m_sc[0] + jnp.log(l0), m_sc[1] + jnp.log(l1))

    qmap = lambda p, t, it, jt: (it[t], p)
    kmap = lambda p, t, it, jt: (jt[t], p)
    grid_spec = pltpu.PrefetchScalarGridSpec(
        num_scalar_prefetch=2, grid=(PAIRS, nsteps),
        in_specs=[pl.BlockSpec((tq, 128), qmap),
                  pl.BlockSpec((tk, 128), lambda p, t, it, jt: (jt[t], PAIRS + p)),
                  pl.BlockSpec((tk, 128), lambda p, t, it, jt: (jt[t], 2 * PAIRS + p)),
                  pl.BlockSpec((tq, 128), qmap), pl.BlockSpec((tk, 128), kmap)],
        out_specs=[pl.BlockSpec((tq, 128), qmap),
                   pl.BlockSpec((None, tq, 128), lambda p, t, it, jt: (p, it[t], 0))],
        scratch_shapes=[pltpu.VMEM((2, tq, 128), f32), pltpu.VMEM((2, tq, 128), f32), pltpu.VMEM((tq, tk), f32),
                        pltpu.VMEM((tq, tk), bf16), pltpu.VMEM((tq, 128), f32)],
    )
    return pl.pallas_call(
        body, name="fox_attn_fwd", grid_spec=grid_spec,
        out_shape=[jax.ShapeDtypeStruct((T, AW), bf16), jax.ShapeDtypeStruct((PAIRS, T, 128), f32)],
        compiler_params=_cp("parallel", "arbitrary"),
    )(it, jt, qkv, qkv, qkv, qaug, kaug)


def _pool_fwd(u, wp, scale):
    tm = 512

    def body(u_ref, wp_ref, sc_ref, pooled_ref, pool_ref, ext):
        i = pl.program_id(0)

        @pl.when(i == 0)
        def _():
            ext[0:HALO, :] = jnp.zeros((HALO, AW), f32)

        uv = u_ref[...]
        ext[HALO:HALO + tm, :] = uv
        t_idx = i * tm + lax.broadcasted_iota(jnp.int32, (tm, 1), 0)
        for g, w in enumerate(WINDOWS):
            lo, hi = 128 * g, 128 * (g + 1)
            ug = uv[:, lo:hi]
            acc = ug
            for d in range(1, w):
                acc = acc + ext[HALO - d:HALO - d + tm, lo:hi]
            cnt = jnp.minimum(t_idx + 1, w).astype(f32)
            pb = (acc / cnt - ug).astype(bf16)
            pooled_ref[:, lo:hi] = pb
            mixed = jnp.dot(pb, wp_ref[g], preferred_element_type=f32)
            pool_ref[:, lo:hi] = (mixed * sc_ref[:, lo:hi]).astype(bf16)
        ext[0:HALO, :] = uv[tm - HALO:tm, :]

    return pl.pallas_call(
        body, name="pool_fwd", grid=(T // tm,),
        in_specs=[pl.BlockSpec((tm, AW), lambda i: (i, 0)), _full((4, 128, 128)), _full((1, AW))],
        out_specs=[pl.BlockSpec((tm, AW), lambda i: (i, 0)), pl.BlockSpec((tm, AW), lambda i: (i, 0))],
        out_shape=[jax.ShapeDtypeStruct((T, AW), bf16), jax.ShapeDtypeStruct((T, AW), bf16)],
        scratch_shapes=[pltpu.VMEM((tm + HALO, AW), f32)],
        compiler_params=_cp("arbitrary"),
    )(u, wp, scale)


def _outproj(x, attn, pool, wo, g2):
    tm = 512

    def body(x_ref, a_ref, p_ref, wo_ref, g_ref, x1_ref, h2_ref):
        x1 = x_ref[...] + jnp.dot(a_ref[...], wo_ref[0:AW, :], preferred_element_type=f32)
        x1 = x1 + jnp.dot(p_ref[...], wo_ref[AW:2 * AW, :], preferred_element_type=f32)
        x1_ref[...] = x1
        r = lax.rsqrt(jnp.mean(x1 * x1, axis=-1, keepdims=True) + EPS)
        h2_ref[...] = (x1 * r * g_ref[...]).astype(bf16)

    return pl.pallas_call(
        body, name="outproj", grid=(T // tm,),
        in_specs=[pl.BlockSpec((tm, D), lambda i: (i, 0)), pl.BlockSpec((tm, AW), lambda i: (i, 0)),
                  pl.BlockSpec((tm, AW), lambda i: (i, 0)), _full((D, D)), _full((1, D))],
        out_specs=[pl.BlockSpec((tm, D), lambda i: (i, 0)), pl.BlockSpec((tm, D), lambda i: (i, 0))],
        out_shape=[jax.ShapeDtypeStruct((T, D), f32), jax.ShapeDtypeStruct((T, D), bf16)],
        compiler_params=_cp("parallel"),
    )(x, attn, pool, wo, g2)


def _mlp_fwd_loss(h2, x1, wg, wu, wd, tgt, gf):
    tm = 512

    def body(h_ref, x1_ref, wg_ref, wu_ref, wd_ref, t_ref, g_ref,
             loss_ref, dg_ref, dx_ref, dxb_ref, gate_ref, up_ref, a_ref, x2):
        i = pl.program_id(0)
        s = pl.program_id(1)

        @pl.when(jnp.logical_and(i == 0, s == 0))
        def _():
            loss_ref[...] = jnp.zeros_like(loss_ref)
            dg_ref[...] = jnp.zeros_like(dg_ref)

        h = h_ref[...]
        for c0, c1 in FS_CHUNKS:
            gate = lax.dot_general(h, wg_ref[c0:c1, :], NT, preferred_element_type=f32)
            up = lax.dot_general(h, wu_ref[c0:c1, :], NT, preferred_element_type=f32)
            gate_ref[:, c0:c1] = gate.astype(bf16)
            up_ref[:, c0:c1] = up.astype(bf16)
            a_ref[:, c0:c1] = (gate * jax.nn.sigmoid(gate) * up).astype(bf16)
        part = jnp.dot(a_ref[...], wd_ref[...], preferred_element_type=f32)

        @pl.when(s == 0)
        def _():
            x2[...] = x1_ref[...] + part

        @pl.when(s > 0)
        def _():
            x2[...] += part

        @pl.when(s == NSH - 1)
        def _():
            xv = x2[...]
            g = g_ref[...]
            r = lax.rsqrt(jnp.mean(xv * xv, axis=-1, keepdims=True) + EPS)
            xhat = xv * r
            e = xhat * g - t_ref[...]
            loss_ref[...] += 0.5 * jnp.sum(jnp.mean(e * e, axis=-1, keepdims=True))
            dy = e * (1.0 / D)
            dg_ref[...] += jnp.sum(dy * xhat, axis=0, keepdims=True)
            z = dy * g
            dx = r * (z - xhat * jnp.mean(z * xhat, axis=-1, keepdims=True))
            dx_ref[...] = dx
            dxb_ref[...] = dx.astype(bf16)

    row = lambda i, s: (i, 0)
    sl = lambda i, s: (s, i, 0)
    wsl = lambda i, s: (s, 0, 0)
    return pl.pallas_call(
        body, name="mlp_fwd_loss", grid=(T // tm, NSH),
        in_specs=[pl.BlockSpec((tm, D), row), pl.BlockSpec((tm, D), row),
                  pl.BlockSpec((None, FS, D), wsl), pl.BlockSpec((None, FS, D), wsl), pl.BlockSpec((None, FS, D), wsl),
                  pl.BlockSpec((tm, D), row), pl.BlockSpec((1, D), lambda i, s: (0, 0))],
        out_specs=[pl.BlockSpec((8, 128), lambda i, s: (0, 0)), pl.BlockSpec((1, D), lambda i, s: (0, 0)),
                   pl.BlockSpec((tm, D), row), pl.BlockSpec((tm, D), row),
                   pl.BlockSpec((None, tm, FS), sl), pl.BlockSpec((None, tm, FS), sl), pl.BlockSpec((None, tm, FS), sl)],
        out_shape=[jax.ShapeDtypeStruct((8, 128), f32), jax.ShapeDtypeStruct((1, D), f32),
                   jax.ShapeDtypeStruct((T, D), f32), jax.ShapeDtypeStruct((T, D), bf16)]
        + [jax.ShapeDtypeStruct((NSH, T, FS), bf16)] * 3,
        scratch_shapes=[pltpu.VMEM((tm, D), f32)],
        compiler_params=_cp("arbitrary", "arbitrary"),
    )(h2, x1, wg, wu, wd, tgt, gf)


def _mlp_bwd(dx2b, dx2, gate, up, wg, wu, wd, x1, g2):
    tm = 512

    def body(dxb_ref, dx_ref, gate_ref, up_ref, wg_ref, wu_ref, wd_ref, x1_ref, g_ref,
             dg_ref, du_ref, dx1_ref, dx1b_ref, dn_ref, acc):
        i = pl.program_id(0)
        s = pl.program_id(1)

        @pl.when(jnp.logical_and(i == 0, s == 0))
        def _():
            dn_ref[...] = jnp.zeros_like(dn_ref)

        dxb = dxb_ref[...]
        for c0, c1 in FS_CHUNKS:
            da = lax.dot_general(dxb, wd_ref[c0:c1, :], NT, preferred_element_type=f32)
            gate = gate_ref[:, c0:c1].astype(f32)
            upv = up_ref[:, c0:c1].astype(f32)
            sg = jax.nn.sigmoid(gate)
            silu = gate * sg
            dg_ref[:, c0:c1] = (da * upv * (sg * (1.0 + gate * (1.0 - sg)))).astype(bf16)
            du_ref[:, c0:c1] = (da * silu).astype(bf16)
        part = jnp.dot(dg_ref[...], wg_ref[...], preferred_element_type=f32)
        part = part + jnp.dot(du_ref[...], wu_ref[...], preferred_element_type=f32)

        @pl.when(s == 0)
        def _():
            acc[...] = part

        @pl.when(s > 0)
        def _():
            acc[...] += part

        @pl.when(s == NSH - 1)
        def _():
            xv = x1_ref[...]
            r = lax.rsqrt(jnp.mean(xv * xv, axis=-1, keepdims=True) + EPS)
            xhat = xv * r
            dh = acc[...]
            dn_ref[...] += jnp.sum(dh * xhat, axis=0, keepdims=True)
            z = dh * g_ref[...]
            dx1 = dx_ref[...] + r * (z - xhat * jnp.mean(z * xhat, axis=-1, keepdims=True))
            dx1_ref[...] = dx1
            dx1b_ref[...] = dx1.astype(bf16)

    row = lambda i, s: (i, 0)
    sl = lambda i, s: (s, i, 0)
    wsl = lambda i, s: (s, 0, 0)
    return pl.pallas_call(
        body, name="mlp_bwd", grid=(T // tm, NSH),
        in_specs=[pl.BlockSpec((tm, D), row), pl.BlockSpec((tm, D), row),
                  pl.BlockSpec((None, tm, FS), sl), pl.BlockSpec((None, tm, FS), sl),
                  pl.BlockSpec((None, FS, D), wsl), pl.BlockSpec((None, FS, D), wsl), pl.BlockSpec((None, FS, D), wsl),
                  pl.BlockSpec((tm, D), row), pl.BlockSpec((1, D), lambda i, s: (0, 0))],
        out_specs=[pl.BlockSpec((None, tm, FS), sl), pl.BlockSpec((None, tm, FS), sl),
                   pl.BlockSpec((tm, D), row), pl.BlockSpec((tm, D), row), pl.BlockSpec((1, D), lambda i, s: (0, 0))],
        out_shape=[jax.ShapeDtypeStruct((NSH, T, FS), bf16)] * 2
        + [jax.ShapeDtypeStruct((T, D), f32), jax.ShapeDtypeStruct((T, D), bf16), jax.ShapeDtypeStruct((1, D), f32)],
        scratch_shapes=[pltpu.VMEM((tm, D), f32)],
        compiler_params=_cp("arbitrary", "arbitrary"),
    )(dx2b, dx2, gate, up, wg, wu, wd, x1, g2)


def _mm_tn(a, bs, name, a_sharded=False, b_sharded=False, tk=512, out_dtype=bf16):
    nb = len(bs)
    sh = NSH if (a_sharded or b_sharded) else 1
    m = a.shape[-1]
    nk = T // tk

    def body(a_ref, *refs):
        kk = pl.program_id(1)
        av = a_ref[...]
        for b_ref, o_ref, acc in zip(refs[:nb], refs[nb:2 * nb], refs[2 * nb:]):
            upd = lax.dot_general(av, b_ref[...], TN, preferred_element_type=f32)

            @pl.when(kk == 0)
            def _():
                acc[...] = upd

            @pl.when(kk > 0)
            def _():
                acc[...] += upd

            @pl.when(kk == nk - 1)
            def _():
                o_ref[...] = acc[...].astype(out_dtype)

    a_spec = (pl.BlockSpec((None, tk, m), lambda s, k: (s, k, 0)) if a_sharded
              else pl.BlockSpec((tk, m), lambda s, k: (k, 0)))
    b_specs, o_specs, o_shapes, scratch = [], [], [], []
    for b in bs:
        n = b.shape[-1]
        b_specs.append(pl.BlockSpec((None, tk, n), lambda s, k: (s, k, 0)) if b_sharded
                       else pl.BlockSpec((tk, n), lambda s, k: (k, 0)))
        scratch.append(pltpu.VMEM((m, n), f32))
        if sh > 1:
            o_specs.append(pl.BlockSpec((None, m, n), lambda s, k: (s, 0, 0)))
            o_shapes.append(jax.ShapeDtypeStruct((sh, m, n), out_dtype))
        else:
            o_specs.append(pl.BlockSpec((m, n), lambda s, k: (0, 0)))
            o_shapes.append(jax.ShapeDtypeStruct((m, n), out_dtype))
    return pl.pallas_call(
        body, name=name, grid=(sh, nk), in_specs=[a_spec] + b_specs, out_specs=o_specs, out_shape=o_shapes,
        scratch_shapes=scratch, compiler_params=_cp("arbitrary", "arbitrary"),
    )(a, *bs)


def _outproj_bwd(dx1b, wo):
    tm = 512

    def body(dx_ref, wo_ref, da_ref, dp_ref):
        dx = dx_ref[...]
        da_ref[...] = lax.dot_general(dx, wo_ref[0:AW, :], NT, preferred_element_type=f32).astype(bf16)
        dp_ref[...] = lax.dot_general(dx, wo_ref[AW:2 * AW, :], NT, preferred_element_type=f32)

    return pl.pallas_call(
        body, name="outproj_bwd", grid=(T // tm,),
        in_specs=[pl.BlockSpec((tm, D), lambda i: (i, 0)), _full((D, D))],
        out_specs=[pl.BlockSpec((tm, AW), lambda i: (i, 0)), pl.BlockSpec((tm, AW), lambda i: (i, 0))],
        out_shape=[jax.ShapeDtypeStruct((T, AW), bf16), jax.ShapeDtypeStruct((T, AW), f32)],
        compiler_params=_cp("parallel"),
    )(dx1b, wo)


def _pool_bwd(dpool, pooled, wp, scale, dep):
    tm = 512
    n = T // tm

    def body(dp_ref, pb_ref, wp_ref, sc_ref, dep_ref, du_ref, dsc_ref, dwp_ref, ext):
        i = pl.program_id(0)

        @pl.when(i == 0)
        def _():
            ext[tm:tm + HALO, :] = jnp.zeros((HALO, AW), f32)
            dsc_ref[...] = jnp.zeros_like(dsc_ref)
            dwp_ref[...] = jnp.zeros_like(dwp_ref)

        t_idx = (n - 1 - i) * tm + lax.broadcasted_iota(jnp.int32, (tm, 1), 0)
        for g, w in enumerate(WINDOWS):
            lo, hi = 128 * g, 128 * (g + 1)
            pb = pb_ref[:, lo:hi]
            mixed = jnp.dot(pb, wp_ref[g], preferred_element_type=f32)
            dpo = dp_ref[:, lo:hi]
            dsc_ref[:, lo:hi] += jnp.sum(dpo * mixed, axis=0, keepdims=True)
            dmr = (dpo * sc_ref[:, lo:hi]).astype(bf16)
            dwp_ref[g] += lax.dot_general(pb, dmr, TN, preferred_element_type=f32)
            dpl = lax.dot_general(dmr, wp_ref[g], NT, preferred_element_type=f32)
            cnt = jnp.minimum(t_idx + 1, w).astype(f32)
            dpn = dpl / cnt
            ext[0:tm, lo:hi] = dpn
            acc = dpn
            for d in range(1, w):
                acc = acc + ext[d:d + tm, lo:hi]
            du_ref[:, lo:hi] = (acc - dpl).astype(bf16)
        ext[tm:tm + HALO, :] = ext[0:HALO, :]

    rev = lambda i: (n - 1 - i, 0)
    return pl.pallas_call(
        body, name="pool_bwd", grid=(n,),
        in_specs=[pl.BlockSpec((tm, AW), rev), pl.BlockSpec((tm, AW), rev), _full((4, 128, 128)), _full((1, AW)),
                  _full((8, 128))],
        out_specs=[pl.BlockSpec((tm, AW), rev), _full((1, AW)), _full((4, 128, 128))],
        out_shape=[jax.ShapeDtypeStruct((T, AW), bf16), jax.ShapeDtypeStruct((1, AW), f32),
                   jax.ShapeDtypeStruct((4, 128, 128), f32)],
        scratch_shapes=[pltpu.VMEM((tm + HALO, AW), f32)],
        compiler_params=_cp("arbitrary"),
    )(dpool, pooled, wp, scale, dep)


def _attn_bwd(qkv, qaug, kaug, attn, dattn, lse, dep):
    tq = tk = ATT_T
    n = T // tq
    it, jt = _causal_steps(True)
    nsteps = it.shape[0]

    rs = 64

    def body(it_ref, jt_ref, q_ref, k_ref, v_ref, qa_ref, ka_ref, o_ref, do_ref, lse_ref, dep_ref,
             dq_ref, dqs_ref, dk_ref, dks_ref, dv_ref, dq_acc, dk_acc, dv_acc, s_sc, dp_sc, p_sc, ds_sc):
        t = pl.program_id(1)
        i = it_ref[t]
        j = jt_ref[t]

        @pl.when(t == 0)
        def _():
            dq_acc[...] = jnp.zeros_like(dq_acc)

        @pl.when(i == j)
        def _():
            dk_acc[...] = jnp.zeros_like(dk_acc)
            dv_acc[...] = jnp.zeros_like(dv_acc)

        lane = lax.broadcasted_iota(jnp.int32, (tq, 128), 1)

        def step(on_diagonal):
            q = q_ref[...] * 0.125
            k = k_ref[...]
            v = v_ref[...]
            qa = qa_ref[...]
            ka = ka_ref[...]
            do = do_ref[...]
            dd = do.astype(f32) * o_ref[...].astype(f32)
            r0 = pl.multiple_of(i * tq, tq)
            for e in range(2):
                hm = (lane >= 64) if e else (lane < 64)
                qe = jnp.where(hm, q, qa)
                ke = jnp.where(hm, k, ka)
                doe = jnp.where(hm, do, jnp.zeros_like(do))
                delta = jnp.sum(jnp.where(hm, dd, 0.0), axis=1, keepdims=True)
                s_sc[...] = lax.dot_general(qe, ke, NT, preferred_element_type=f32)
                dp_sc[...] = lax.dot_general(doe, v, NT, preferred_element_type=f32)
                for r in range(0, tq, rs):
                    s = s_sc[r:r + rs, :] - lse_ref[r:r + rs, 64 * e:64 * e + 1]
                    if on_diagonal:
                        row = lax.broadcasted_iota(jnp.int32, (rs, tk), 0) + r
                        col = lax.broadcasted_iota(jnp.int32, (rs, tk), 1)
                        s = jnp.where(col <= row, s, NEG)
                    p = jnp.exp(s)
                    p_sc[r:r + rs, :] = p.astype(bf16)
                    ds_sc[r:r + rs, :] = (p * (dp_sc[r:r + rs, :] - delta[r:r + rs, :])).astype(bf16)
                dv_acc[...] += lax.dot_general(p_sc[...], doe, TN, preferred_element_type=f32)
                dsb = ds_sc[...]
                dk_acc[e] += lax.dot_general(dsb, qe, TN, preferred_element_type=f32)
                dq_acc[e, pl.ds(r0, tq), :] += jnp.dot(dsb, ke, preferred_element_type=f32)

        @pl.when(i > j)
        def _():
            step(False)

        @pl.when(i == j)
        def _():
            step(True)

        @pl.when(i == n - 1)
        def _():
            dk_ref[...] = jnp.where(lane < 64, dk_acc[0], dk_acc[1]).astype(bf16)
            dks_ref[...] = jnp.where(lane < 64, dk_acc[1], dk_acc[0])
            dv_ref[...] = dv_acc[...].astype(bf16)

        @pl.when(t == nsteps - 1)
        def _():
            lane_t = lax.broadcasted_iota(jnp.int32, (T, 128), 1)
            dq_ref[...] = (jnp.where(lane_t < 64, dq_acc[0], dq_acc[1]) * 0.125).astype(bf16)
            dqs_ref[...] = jnp.where(lane_t < 64, dq_acc[1], dq_acc[0])

    qmap = lambda p, t, it, jt: (it[t], p)
    grid_spec = pltpu.PrefetchScalarGridSpec(
        num_scalar_prefetch=2, grid=(PAIRS, nsteps),
        in_specs=[pl.BlockSpec((tq, 128), qmap),
                  pl.BlockSpec((tk, 128), lambda p, t, it, jt: (jt[t], PAIRS + p)),
                  pl.BlockSpec((tk, 128), lambda p, t, it, jt: (jt[t], 2 * PAIRS + p)),
                  pl.BlockSpec((tq, 128), qmap), pl.BlockSpec((tk, 128), lambda p, t, it, jt: (jt[t], p)),
                  pl.BlockSpec((tq, 128), qmap), pl.BlockSpec((tq, 128), qmap),
                  pl.BlockSpec((None, tq, 128), lambda p, t, it, jt: (p, it[t], 0)),
                  pl.BlockSpec((8, 128), lambda p, t, it, jt: (0, 0))],
        out_specs=[pl.BlockSpec((T, 128), lambda p, t, it, jt: (0, p)),
                   pl.BlockSpec((None, T, 128), lambda p, t, it, jt: (p, 0, 0)),
                   pl.BlockSpec((tk, 128), lambda p, t, it, jt: (jt[t], p)),
                   pl.BlockSpec((None, tk, 128), lambda p, t, it, jt: (p, jt[t], 0)),
                   pl.BlockSpec((tk, 128), lambda p, t, it, jt: (jt[t], p))],
        scratch_shapes=[pltpu.VMEM((2, T, 128), f32), pltpu.VMEM((2, tk, 128), f32), pltpu.VMEM((tk, 128), f32),
                        pltpu.VMEM((tq, tk), f32), pltpu.VMEM((tq, tk), f32), pltpu.VMEM((tq, tk), bf16),
                        pltpu.VMEM((tq, tk), bf16)],
    )
    return pl.pallas_call(
        body, name="fox_attn_bwd", grid_spec=grid_spec,
        out_shape=[jax.ShapeDtypeStruct((T, AW), bf16), jax.ShapeDtypeStruct((PAIRS, T, 128), f32),
                   jax.ShapeDtypeStruct((T, AW), bf16), jax.ShapeDtypeStruct((PAIRS, T, 128), f32),
                   jax.ShapeDtypeStruct((T, AW), bf16)],
        compiler_params=_cp("parallel", "arbitrary"),
    )(it, jt, qkv, qkv, qkv, qaug, kaug, attn, dattn, lse, dep)


def _fox_cumsum_bwd(dqs, dks, fl, bfp):
    tb = CUMSUM_ROWS
    nb = T // tb
    hp = lax.Precision.HIGHEST

    def body(dqs_ref, dks_ref, fl_ref, b_ref, df_ref, db_ref, carry):
        i = pl.program_id(0)

        @pl.when(i == 0)
        def _():
            carry[...] = jnp.zeros_like(carry)
            db_ref[...] = jnp.zeros_like(db_ref)

        r = lax.broadcasted_iota(jnp.int32, (128, 128), 0)
        cc = lax.broadcasted_iota(jnp.int32, (128, 128), 1)
        pick = lambda even_lane, odd_lane, p: jnp.logical_or(
            jnp.logical_and(r == even_lane, cc == 2 * p), jnp.logical_and(r == odd_lane, cc == 2 * p + 1)).astype(f32)
        dc = jnp.zeros((tb, 128), f32)
        for p in range(PAIRS):
            dc = dc + jnp.dot(dqs_ref[p], pick(64, 0, p), precision=hp, preferred_element_type=f32)
            dc = dc - jnp.dot(dks_ref[p], pick(67, 3, p), precision=hp, preferred_element_type=f32)
        rt = lax.broadcasted_iota(jnp.int32, (tb, tb), 0)
        ct = lax.broadcasted_iota(jnp.int32, (tb, tb), 1)
        utri = (ct >= rt).astype(f32)
        dl = jnp.dot(utri, dc, precision=hp, preferred_element_type=f32) + carry[0:1, :]
        carry[...] = jnp.broadcast_to(dl[0:1, :], (8, 128))
        z = fl_ref[...] + b_ref[...]
        df = dl * jax.nn.sigmoid(-z)
        df_ref[...] = df.astype(bf16)
        db_ref[...] += jnp.sum(df, axis=0, keepdims=True)

    rev = lambda i: (nb - 1 - i, 0)
    return pl.pallas_call(
        body, name="fox_cumsum_bwd", grid=(nb,),
        in_specs=[pl.BlockSpec((PAIRS, tb, 128), lambda i: (0, nb - 1 - i, 0)),
                  pl.BlockSpec((PAIRS, tb, 128), lambda i: (0, nb - 1 - i, 0)),
                  pl.BlockSpec((tb, 128), rev), _full((1, 128))],
        out_specs=[pl.BlockSpec((tb, 128), rev), _full((1, 128))],
        out_shape=[jax.ShapeDtypeStruct((T, 128), bf16), jax.ShapeDtypeStruct((1, 128), f32)],
        scratch_shapes=[pltpu.VMEM((8, 128), f32)],
        compiler_params=_cp("arbitrary"),
    )(dqs, dks, fl, bfp)


def _inproj_bwd(dq, dk, dv, du, df, wm, wf, x, dx1, g1):
    tm = 512

    def body(dq_ref, dk_ref, dv_ref, du_ref, df_ref, wm_ref, wf_ref, x_ref, dx1_ref, g_ref, dx_ref, dn_ref):
        i = pl.program_id(0)

        @pl.when(i == 0)
        def _():
            dn_ref[...] = jnp.zeros_like(dn_ref)

        dh = lax.dot_general(dq_ref[...], wm_ref[:, 0:AW], NT, preferred_element_type=f32)
        dh = dh + lax.dot_general(dk_ref[...], wm_ref[:, AW:2 * AW], NT, preferred_element_type=f32)
        dh = dh + lax.dot_general(dv_ref[...], wm_ref[:, 2 * AW:3 * AW], NT, preferred_element_type=f32)
        dh = dh + lax.dot_general(du_ref[...], wm_ref[:, 3 * AW:4 * AW], NT, preferred_element_type=f32)
        dh = dh + lax.dot_general(df_ref[...], wf_ref[...], NT, preferred_element_type=f32)
        xv = x_ref[...]
        r = lax.rsqrt(jnp.mean(xv * xv, axis=-1, keepdims=True) + EPS)
        xhat = xv * r
        dn_ref[...] += jnp.sum(dh * xhat, axis=0, keepdims=True)
        z = dh * g_ref[...]
        dx_ref[...] = dx1_ref[...] + r * (z - xhat * jnp.mean(z * xhat, axis=-1, keepdims=True))

    row = lambda i: (i, 0)
    return pl.pallas_call(
        body, name="inproj_bwd", grid=(T // tm,),
        in_specs=[pl.BlockSpec((tm, AW), row)] * 4 + [pl.BlockSpec((tm, 128), row), _full((D, 4 * AW)), _full((D, 128)),
                                                       pl.BlockSpec((tm, D), row), pl.BlockSpec((tm, D), row), _full((1, D))],
        out_specs=[pl.BlockSpec((tm, D), row), _full((1, D))],
        out_shape=[jax.ShapeDtypeStruct((T, D), f32), jax.ShapeDtypeStruct((1, D), f32)],
        compiler_params=_cp("arbitrary"),
    )(dq, dk, dv, du, df, wm, wf, x, dx1, g1)


def _adamw_math(w, g, m, v):
    m = B1 * m + (1.0 - B1) * g
    v = B2 * v + (1.0 - B2) * (g * g)
    m_hat = m / (1.0 - B1 ** STEP)
    v_hat = v / (1.0 - B2 ** STEP)
    delta = -LR * (m_hat / (jnp.sqrt(v_hat) + AEPS) + WD * w)
    return delta, m, v


def _adamw_shard(w, m, v, p_mine, p_other, name):
    rows, cols = w.shape
    tr = 256 if rows % 256 == 0 else 176

    def body(w_ref, m_ref, v_ref, a_ref, b_ref, g_ref, d_ref, nm_ref, nv_ref):
        g = a_ref[...].astype(f32) + b_ref[...].astype(f32)
        g_ref[...] = g
        d_ref[...], nm_ref[...], nv_ref[...] = _adamw_math(w_ref[...], g, m_ref[...], v_ref[...])

    spec = pl.BlockSpec((tr, cols), lambda i: (i, 0))
    return pl.pallas_call(
        body, name=name, grid=(rows // tr,), in_specs=[spec] * 5, out_specs=[spec] * 4,
        out_shape=[jax.ShapeDtypeStruct((rows, cols), f32)] * 4, compiler_params=_cp("parallel"),
    )(w, m, v, p_mine, p_other)


SMALL_SLOTS = ((0, 8, 128), (8, 16, 128), (16, 24, 128), (24, 28, 128), (32, 33, 8), (40, 552, 128))
LOSS_ROW = 39


def _adamw_small(ws, ms, vs, parts):
    n = len(ws)

    def body(*refs):
        w_refs, m_refs, v_refs, p_ref = refs[0:n], refs[n:2 * n], refs[2 * n:3 * n], refs[3 * n]
        outs = refs[3 * n + 1:]
        g_all = p_ref[0]
        for k in range(1, 8):
            g_all = g_all + p_ref[k]
        for idx, (r0, r1, lanes) in enumerate(SMALL_SLOTS):
            g = g_all[r0:r1, 0:lanes]
            d, nm, nv = _adamw_math(w_refs[idx][...], g, m_refs[idx][...], v_refs[idx][...])
            outs[idx][...] = g
            outs[n + idx][...] = d
            outs[2 * n + idx][...] = nm
            outs[3 * n + idx][...] = nv
        outs[4 * n][...] = g_all[LOSS_ROW:LOSS_ROW + 1, :]

    shapes = [jax.ShapeDtypeStruct(w.shape, f32) for w in ws]
    res = pl.pallas_call(
        body, name="adamw_small", out_shape=shapes * 4 + [jax.ShapeDtypeStruct((1, 128), f32)],
    )(*ws, *ms, *vs, parts)
    return res[:4 * n], res[4 * n]


def _sum4(recv, g, mine, name):
    _, rows, cols = recv.shape
    tr = 256 if rows % 256 == 0 else 176

    def body(mine_ref, r_ref, g_ref, o_ref):
        o_ref[...] = ((g_ref[...].astype(f32) + r_ref[0].astype(f32))
                      + (r_ref[1].astype(f32) + r_ref[2].astype(f32))).astype(bf16)

    grid_spec = pltpu.PrefetchScalarGridSpec(
        num_scalar_prefetch=1, grid=(rows // tr,),
        in_specs=[pl.BlockSpec((3, tr, cols), lambda i, m: (0, i, 0)),
                  pl.BlockSpec((None, tr, cols), lambda i, m: (m[0], i, 0))],
        out_specs=pl.BlockSpec((tr, cols), lambda i, m: (i, 0)))
    return pl.pallas_call(
        body, name=name, grid_spec=grid_spec, out_shape=jax.ShapeDtypeStruct((rows, cols), bf16),
        compiler_params=_cp("arbitrary"),
    )(mine, recv, g)


_HBM = pl.BlockSpec(memory_space=pltpu.HBM)
_SEM = pl.BlockSpec(memory_space=pltpu.SEMAPHORE)
_EFFECT = pltpu.SideEffectType.DATAFLOW_SIDE_EFFECTING


def _in_hbm(a):
    return pltpu.with_memory_space_constraint(a, pltpu.HBM)


def _mesh_pos():
    return lax.axis_index("x"), lax.axis_index("y"), lax.axis_index("c")


def _other_chips(x, y):
    return [(1 - x, y), (x, 1 - y), (1 - x, 1 - y)]


def _gather_copy(srcs, lands, send_sems, recv_sems, a, k, slot):
    x, y, c = _mesh_pos()
    cx, cy = _other_chips(x, y)[k]
    return pltpu.make_async_remote_copy(
        src_ref=srcs[a], dst_ref=lands[a].at[slot], send_sem=send_sems.at[3 * a + k], recv_sem=recv_sems.at[3 * a + k],
        device_id=(cx, cy, c), device_id_type=MESH)


def _scatter_copy(srcs, lands, send_sems, recv_sems, a, k):
    x, y, c = _mesh_pos()
    cx, cy = _other_chips(x, y)[k]
    return pltpu.make_async_remote_copy(
        src_ref=srcs[a].at[2 * cx + cy], dst_ref=lands[a].at[k], send_sem=send_sems.at[3 * a + k],
        recv_sem=recv_sems.at[3 * a + k], device_id=(cx, cy, c), device_id_type=MESH)


def _all_gather_w_in(part):
    rows = part.shape[0] // 2

    def body(src, dst, send_sems, recv_sems, loc_sem):
        x, y, c = _mesh_pos()
        mine = 2 * x + y
        chips = _other_chips(x, y)
        half = lambda ref, cc: ref.at[pl.ds(pl.multiple_of(cc * rows, rows), rows), :]

        def over_ici(k, slot):
            cx, cy = chips[k]
            return pltpu.make_async_remote_copy(
                src_ref=half(src, c), dst_ref=half(dst.at[slot], c), send_sem=send_sems.at[k], recv_sem=recv_sems.at[k],
                device_id=(cx, cy, c), device_id_type=MESH)

        def to_sibling(k, cc):
            slot = 2 * chips[k][0] + chips[k][1]
            return pltpu.make_async_remote_copy(
                src_ref=half(dst.at[slot], cc), dst_ref=half(dst.at[slot], cc), send_sem=send_sems.at[3 + k],
                recv_sem=recv_sems.at[3 + k], device_id=(x, y, 1 - c), device_id_type=MESH)

        local = pltpu.make_async_copy(src, dst.at[mine], loc_sem.at[0])
        local.start()
        first = [over_ici(k, mine) for k in range(3)]
        for cp in first:
            cp.start()
        passed = [to_sibling(k, c) for k in range(3)]
        for k in range(3):
            over_ici(k, 2 * chips[k][0] + chips[k][1]).wait_recv()
            passed[k].start()
        for k in range(3):
            to_sibling(k, 1 - c).wait_recv()
        for cp in first + passed:
            cp.wait_send()
        local.wait()

    return pl.pallas_call(
        body, name="all_gather_w_in", in_specs=[_HBM], out_specs=_HBM,
        out_shape=jax.ShapeDtypeStruct((NSH,) + part.shape, part.dtype),
        scratch_shapes=[pltpu.SemaphoreType.DMA((6,)), pltpu.SemaphoreType.DMA((6,)), pltpu.SemaphoreType.DMA((1,))],
    )(part)


def _split_start(name, srcs, lands, n_sems, plan, dep):
    n, nl = len(srcs), len(lands)

    def body(*refs):
        src_refs, land_refs = refs[:n], refs[n:n + nl]
        send_sems, recv_sems = refs[n + nl + 1], refs[n + nl + 2]
        token = refs[-1]
        sends, _ = plan(src_refs, land_refs, send_sems, recv_sems)
        for cp in sends:
            cp.start()
        token[...] = jnp.zeros_like(token)

    outs = pl.pallas_call(
        body, name=name,
        in_specs=[_HBM] * (n + nl) + [pl.BlockSpec(memory_space=pl.ANY)],
        out_specs=[_SEM, _SEM] + [_HBM] * (n + nl) + [pl.BlockSpec(memory_space=pltpu.VMEM)],
        out_shape=[pltpu.SemaphoreType.DMA((n_sems,)), pltpu.SemaphoreType.DMA((n_sems,))]
        + [pltpu.HBM(a.shape, a.dtype) for a in list(srcs) + list(lands)] + [jax.ShapeDtypeStruct((8, 128), f32)],
        input_output_aliases={i: 2 + i for i in range(n + nl)},
        compiler_params=pltpu.CompilerParams(has_side_effects=_EFFECT),
    )(*[_in_hbm(a) for a in list(srcs) + list(lands)], dep)
    return outs[0], outs[1], list(outs[2:2 + n]), list(outs[2 + n:2 + n + nl]), outs[-1]


def _split_wait(name, send_sems, recv_sems, srcs, lands, after, plan):
    n, nl = len(srcs), len(lands)

    def body(*refs):
        src_refs, land_refs = refs[:n], refs[n:n + nl]
        s_sems, r_sems = refs[n + nl], refs[n + nl + 1]
        sends, recvs = plan(src_refs, land_refs, s_sems, r_sems)
        for cp in recvs:
            cp.wait_recv()
        for cp in sends:
            cp.wait_send()

    outs = pl.pallas_call(
        body, name=name,
        in_specs=[_HBM] * (n + nl) + [_SEM, _SEM, pl.BlockSpec(memory_space=pl.ANY)],
        out_specs=[_HBM] * (n + nl),
        out_shape=[pltpu.HBM(a.shape, a.dtype) for a in list(srcs) + list(lands)],
        input_output_aliases={i: i for i in range(n + nl)},
        compiler_params=pltpu.CompilerParams(has_side_effects=_EFFECT),
    )(*srcs, *lands, send_sems, recv_sems, after)
    return list(outs[:n]), list(outs[n:])


def _gather_plan(srcs, lands, ss, rs):
    x, y, _ = _mesh_pos()
    chips = _other_chips(x, y)
    sends = [_gather_copy(srcs, lands, ss, rs, a, k, 2 * x + y) for a in range(len(srcs)) for k in range(3)]
    recvs = [_gather_copy(srcs, lands, ss, rs, a, k, 2 * chips[k][0] + chips[k][1])
             for a in range(len(srcs)) for k in range(3)]
    return sends, recvs


def _scatter_plan(srcs, lands, ss, rs):
    cps = [_scatter_copy(srcs, lands, ss, rs, a, k) for a in range(len(srcs)) for k in range(3)]
    return cps, cps


def _tail_plan(srcs, lands, ss, rs):
    x, y, c = _mesh_pos()
    me = 4 * x + 2 * y + c
    cps = [_scatter_copy(srcs[:1], lands[:1], ss, rs, 0, k) for k in range(3)]
    for f in range(1, 8):
        peer = ((x + (f >> 2)) % 2, (y + ((f >> 1) & 1)) % 2, (c + (f & 1)) % 2)
        cps.append(pltpu.make_async_remote_copy(
            src_ref=srcs[1], dst_ref=lands[1].at[me], send_sem=ss.at[2 + f], recv_sem=rs.at[2 + f],
            device_id=peer, device_id_type=MESH))
    return cps, cps


def _swap_with_sibling(parts, name):
    n = len(parts)

    def body(*refs):
        srcs, dsts = refs[:n], refs[n:2 * n]
        send_sems, recv_sems = refs[2 * n:]
        x, y, c = _mesh_pos()
        cps = [pltpu.make_async_remote_copy(src_ref=srcs[a], dst_ref=dsts[a], send_sem=send_sems.at[a],
                                            recv_sem=recv_sems.at[a], device_id=(x, y, 1 - c), device_id_type=MESH)
               for a in range(n)]
        for cp in cps:
            cp.start()
        for cp in cps:
            cp.wait_recv()
        for cp in cps:
            cp.wait_send()

    return pl.pallas_call(
        body, name=name, in_specs=[_HBM] * n, out_specs=[_HBM] * n,
        out_shape=[jax.ShapeDtypeStruct(p.shape, p.dtype) for p in parts],
        scratch_shapes=[pltpu.SemaphoreType.DMA((n,)), pltpu.SemaphoreType.DMA((n,))],
    )(*parts)


def _forward(x, tgt, wm, wf, mlp_w_fn, g1, bfp, wp, scale, g2, gf, dep):
    h, qkv, u, fl = _rms_inproj(x, g1, wm, wf, dep)
    qaug, kaug = _fox_cumsum(fl, bfp)
    attn, lse = _attn_fwd(qkv, qaug, kaug)
    pooled, pool = _pool_fwd(u, wp, scale)
    wo, wgt, wut, wd = mlp_w_fn(attn)
    x1, h2 = _outproj(x, attn, pool, wo, g2)
    loss, dgf, dx2, dx2b, gate, up, a_b = _mlp_fwd_loss(h2, x1, wgt, wut, wd, tgt, gf)
    saved = dict(h=h, qkv=qkv, fl=fl, qaug=qaug, kaug=kaug, attn=attn, lse=lse, pooled=pooled, pool=pool, x1=x1, h2=h2,
                 gate=gate, up=up, a_b=a_b, wo=wo, wgt=wgt, wut=wut, wd=wd)
    return loss, dgf, dx2, dx2b, saved


def _backward_mlp(sv, dx2, dx2b, g2):
    dgate, dup, dx1, dx1b, dg2 = _mlp_bwd(dx2b, dx2, sv["gate"], sv["up"], sv["wgt"], sv["wut"], sv["wd"], sv["x1"], g2)
    (dwd,) = _mm_tn(sv["a_b"], [dx2b], "dw_down", a_sharded=True, tk=2048)
    (dwgt,) = _mm_tn(dgate, [sv["h2"]], "dw_gate", a_sharded=True, tk=2048)
    (dwut,) = _mm_tn(dup, [sv["h2"]], "dw_up", a_sharded=True, tk=2048)
    return dx1, dx1b, dg2, (dwgt, dwut, dwd)


def _backward_outproj(sv, dx1b):
    dattn, dpool = _outproj_bwd(dx1b, sv["wo"])
    dwo_a, = _mm_tn(sv["attn"], [dx1b], "dw_out_attn", tk=2048)
    dwo_p, = _mm_tn(sv["pool"], [dx1b], "dw_out_pool", tk=2048)
    dwo = jnp.concatenate([dwo_a, dwo_p], axis=0).reshape(NSH, D // NSH, D)
    return dattn, dpool, dwo


def _backward_mixer(sv, x, dx1, dattn, dpool, wm, wf, g1, bfp, wp, scale, dep):
    du, dscale, dwp = _pool_bwd(dpool, sv["pooled"], wp, scale, dep)
    dq, dqs, dk, dks, dv = _attn_bwd(sv["qkv"], sv["qaug"], sv["kaug"], sv["attn"], dattn, sv["lse"], dep)
    df, dbf = _fox_cumsum_bwd(dqs, dks, sv["fl"], bfp)
    dx, dg1 = _inproj_bwd(dq, dk, dv, du, df, wm, wf, x, dx1, g1)
    dwq, dwk, dwv, dwu_in, dwf = _mm_tn(sv["h"], [dq, dk, dv, du, df], "dw_in", tk=1024)
    dwin = jnp.concatenate([dwq, dwk, dwv, dwf[:, 0:8], dwu_in], axis=1)
    dwin = dwin.reshape(D, NSH, IN_S).transpose(1, 0, 2)
    return dx, dg1, dscale, dwp, dbf, dwin


def kernel(x, norm1_g, w_in, b_forget, w_pool, pool_scale, w_out, norm2_g, w_gate, w_up, w_down, final_g, loss_target, m_norm1_g, m_w_in, m_b_forget, m_w_pool, m_pool_scale, m_w_out, m_norm2_g, m_w_gate, m_w_up, m_w_down, m_final_g, v_norm1_g, v_w_in, v_b_forget, v_w_pool, v_pool_scale, v_w_out, v_norm2_g, v_w_gate, v_w_up, v_w_down, v_final_g):
    mine = (2 * lax.axis_index("x") + lax.axis_index("y")).astype(jnp.int32)
    mine1 = mine.reshape(1)
    tr = lambda a: jnp.transpose(a[0])

    win4 = _all_gather_w_in(w_in[0].astype(bf16))
    later = [w_out[0].astype(bf16), tr(w_gate).astype(bf16), tr(w_up).astype(bf16), w_down[0].astype(bf16)]
    lands = [lax.dynamic_update_slice(lax.empty((NSH,) + p.shape, bf16), p[None], (mine, 0, 0)) for p in later]
    ag_send, ag_recv, later_thru, lands_thru, ag_token = _split_start("all_gather_start", later, lands, 12, _gather_plan,
                                                                      win4)
    win = win4.transpose(1, 0, 2).reshape(D, IN_W)
    wm = jnp.concatenate([win[:, 0:3 * AW], win[:, 3 * AW + 8:]], axis=1)
    wf = jnp.pad(win[:, 3 * AW:3 * AW + 8], ((0, 0), (0, 120)))
    bfp = jnp.pad(b_forget, ((0, 0), (0, 120)))
    wp = w_pool[0].astype(bf16)
    gf = final_g.reshape(1, D)

    def later_weights(after):
        _, (wo4, wgt, wut, wd) = _split_wait("all_gather_wait", ag_send, ag_recv, later_thru, lands_thru, after, _gather_plan)
        return wo4.reshape(D, D), wgt, wut, wd

    xe, tgt = x[0], loss_target[0]
    loss_v, dgf, dx2, dx2b, sv = _forward(xe, tgt, wm, wf, later_weights, norm1_g, bfp, wp, pool_scale, norm2_g, gf, ag_token)
    dx1, dx1b, dg2, mlp_grads = _backward_mlp(sv, dx2, dx2b, norm2_g)
    dattn, dpool, dwo = _backward_outproj(sv, dx1b)
    first = [dwo] + list(mlp_grads)
    first_lands = [lax.empty((3,) + g.shape[1:], bf16) for g in first]
    rs_send, rs_recv, first_thru, first_lands_thru, rs_token = _split_start("reduce_scatter_start", first, first_lands, 12,
                                                                            _scatter_plan, dattn)
    dx, dg1, dscale, dwp, dbf, dwin = _backward_mixer(sv, xe, dx1, dattn, dpool, wm, wf, norm1_g, bfp, wp, pool_scale, rs_token)

    me = (4 * lax.axis_index("x") + 2 * lax.axis_index("y") + lax.axis_index("c")).astype(jnp.int32)
    pad8 = lambda r: jnp.pad(r, ((0, 8 - r.shape[0]), (0, 0)))
    loss_rows = jnp.concatenate([dbf, jnp.zeros((6, 128), f32), loss_v[0:1, :]], axis=0)
    small = jnp.concatenate([dg1.reshape(8, 128), dg2.reshape(8, 128), dgf.reshape(8, 128), pad8(dscale.reshape(4, 128)),
                             loss_rows, dwp.reshape(512, 128)], axis=0)
    small_land = lax.dynamic_update_slice(lax.empty((8, SMALL_ROWS, 128), f32), small[None], (me, 0, 0))
    tail_send, tail_recv, tail_thru, tail_lands_thru, tail_token = _split_start(
        "tail_start", [dwin, small], [lax.empty((3,) + dwin.shape[1:], bf16), small_land], 10, _tail_plan, dx)
    first_thru, first_recv = _split_wait("reduce_scatter_wait", rs_send, rs_recv, first_thru, first_lands_thru, tail_token,
                                         _scatter_plan)
    ws = [w_in[0], w_out[0], tr(w_gate), tr(w_up), w_down[0]]
    ms = [m_w_in[0], m_w_out[0], tr(m_w_gate), tr(m_w_up), m_w_down[0]]
    vs = [v_w_in[0], v_w_out[0], tr(v_w_gate), tr(v_w_up), v_w_down[0]]
    partial = [_sum4(r, g, mine1, f"sum4_{i + 1}") for i, (r, g) in enumerate(zip(first_recv, first_thru))]
    other = _swap_with_sibling(partial, "swap_first")
    big = [_adamw_shard(ws[i + 1], ms[i + 1], vs[i + 1], partial[i], other[i], f"adamw_{i + 1}") for i in range(4)]
    (dwin_thru, _), (in_recv_land, small_all) = _split_wait("tail_wait", tail_send, tail_recv, tail_thru, tail_lands_thru,
                                                            big[3][0], _tail_plan)
    partial_in = _sum4(in_recv_land, dwin_thru, mine1, "sum4_0")
    (other_in,) = _swap_with_sibling([partial_in], "swap_in")
    big = [_adamw_shard(ws[0], ms[0], vs[0], partial_in, other_in, "adamw_0")] + big

    small_names = ["norm1_g", "norm2_g", "final_g", "pool_scale", "b_forget", "w_pool"]
    rows = lambda a, b, c, d, e, f: [a.reshape(8, 128), b.reshape(8, 128), c.reshape(8, 128), d.reshape(4, 128),
                                     e.reshape(1, 8), f.reshape(512, 128)]
    sm, loss_row = _adamw_small(rows(norm1_g, norm2_g, final_g, pool_scale, b_forget, w_pool),
                                rows(m_norm1_g, m_norm2_g, m_final_g, m_pool_scale, m_b_forget, m_w_pool),
                                rows(v_norm1_g, v_norm2_g, v_final_g, v_pool_scale, v_b_forget, v_w_pool), small_all)
    small_shape = dict(norm1_g=(1, D), norm2_g=(1, D), final_g=(D,), pool_scale=(1, AW), b_forget=(1, 8),
                       w_pool=(1, 4, 128, 128))

    order = ["norm1_g", "w_in", "b_forget", "w_pool", "pool_scale", "w_out", "norm2_g", "w_gate", "w_up", "w_down", "final_g"]
    big_idx = {"w_in": 0, "w_out": 1, "w_gate": 2, "w_up": 3, "w_down": 4}
    outs = [loss_row[0, 0], dx[None]]
    for kind in range(4):
        for name in order:
            if name in ("w_gate", "w_up"):
                outs.append(jnp.transpose(big[big_idx[name]][kind])[None])
            elif name in big_idx:
                outs.append(big[big_idx[name]][kind][None])
            else:
                outs.append(sm[6 * kind + small_names.index(name)].reshape(small_shape[name]))
    return tuple(outs)
```

```python
import functools

import jax
import jax.numpy as jnp
import numpy as np
from jax import lax
from jax.experimental import pallas as pl
from jax.experimental.pallas import tpu as pltpu

f32 = jnp.float32
bf16 = jnp.bfloat16

T = 4096
D = 1024
NSH = 4
IN_W = 2056
IN_S = IN_W // NSH
AW = 512
PAIRS = 4
FF = 2816
FS = FF // NSH
WINDOWS = (2, 4, 8, 16)
HALO = 16
EPS = 1e-6
NEG = -1e30
LR, B1, B2, AEPS, WD, STEP = 0.001, 0.9, 0.999, 1e-08, 0.01, 10
SMALL_ROWS = 552

NT = (((1,), (1,)), ((), ()))
TN = (((0,), (0,)), ((), ()))

MESH = pl.DeviceIdType.MESH


def _cp(*sem):
    return pltpu.CompilerParams(dimension_semantics=sem)


def _full(shape):
    n = len(shape)
    return pl.BlockSpec(shape, lambda *_: (0,) * n)


def _rms_inproj(x, g1, wm, wf, dep):
    tm = 512

    def body(x_ref, g_ref, wm_ref, wf_ref, dep_ref, h_ref, qkv_ref, u_ref, fl_ref):
        xv = x_ref[...]
        r = lax.rsqrt(jnp.mean(xv * xv, axis=-1, keepdims=True) + EPS)
        h = (xv * r * g_ref[...]).astype(bf16)
        h_ref[...] = h
        qkv_ref[...] = lax.dot_general(h, wm_ref[0:3 * AW, :], NT, preferred_element_type=f32).astype(bf16)
        u_ref[...] = lax.dot_general(h, wm_ref[3 * AW:4 * AW, :], NT, preferred_element_type=f32)
        fl_ref[...] = lax.dot_general(h, wf_ref[...], NT, preferred_element_type=f32)

    return pl.pallas_call(
        body, name="rms_inproj", grid=(T // tm,),
        in_specs=[pl.BlockSpec((tm, D), lambda i: (i, 0)), _full((1, D)), _full((4 * AW, D)), _full((128, D)),
                  _full((8, 128))],
        out_specs=[pl.BlockSpec((tm, D), lambda i: (i, 0)), pl.BlockSpec((tm, 3 * AW), lambda i: (i, 0)),
                   pl.BlockSpec((tm, AW), lambda i: (i, 0)), pl.BlockSpec((tm, 128), lambda i: (i, 0))],
        out_shape=[jax.ShapeDtypeStruct((T, D), bf16), jax.ShapeDtypeStruct((T, 3 * AW), bf16),
                   jax.ShapeDtypeStruct((T, AW), f32), jax.ShapeDtypeStruct((T, 128), f32)],
        compiler_params=_cp("parallel"),
    )(x, g1, wm, wf, dep)


CUMSUM_ROWS = 512
FS_CHUNKS = ((0, 256), (256, 512), (512, FS))


def _log_sigmoid(z):
    return jnp.minimum(z, 0.0) - jnp.log(1.0 + jnp.exp(-jnp.abs(z)))


def _fox_cumsum(fl, bfp):
    tb = CUMSUM_ROWS
    nb = T // tb

    def body(fl_ref, b_ref, qa_ref, ka_ref, carry):
        i = pl.program_id(0)

        @pl.when(i == 0)
        def _():
            carry[...] = jnp.zeros_like(carry)

        lf = _log_sigmoid(fl_ref[...] + b_ref[...])
        r = lax.broadcasted_iota(jnp.int32, (tb, tb), 0)
        cc = lax.broadcasted_iota(jnp.int32, (tb, tb), 1)
        ltri = (cc <= r).astype(f32)
        cb = jnp.dot(ltri, lf, precision=lax.Precision.HIGHEST, preferred_element_type=f32) + carry[0:1, :]
        carry[...] = jnp.broadcast_to(cb[tb - 1:tb, :], (8, 128))
        hi = cb.astype(bf16)
        r1 = cb - hi.astype(f32)
        mid = r1.astype(bf16)
        lo = (r1 - mid.astype(f32)).astype(bf16)
        head = lax.broadcasted_iota(jnp.int32, (128, AW), 0)
        col = lax.broadcasted_iota(jnp.int32, (128, AW), 1)
        base = 128 * (head >> 1) + 64 * (1 - (head & 1))
        place = lambda off: jnp.logical_and(col == base + off, head < 8).astype(bf16)
        mm = lambda a, off: jnp.dot(a, place(off), preferred_element_type=f32)
        cq = mm(hi, 0) + mm(mid, 1) + mm(lo, 2)
        ck = mm(hi, 3) + mm(mid, 4) + mm(lo, 5)
        within = jnp.bitwise_and(lax.broadcasted_iota(jnp.int32, (tb, AW), 1), 63)
        qa_ref[...] = jnp.where(jnp.logical_and(within >= 3, within <= 5), 1.0, cq).astype(bf16)
        ka_ref[...] = jnp.where(within <= 2, 1.0, -ck).astype(bf16)

    return pl.pallas_call(
        body, name="fox_cumsum", grid=(nb,),
        in_specs=[pl.BlockSpec((tb, 128), lambda i: (i, 0)), _full((1, 128))],
        out_specs=[pl.BlockSpec((tb, AW), lambda i: (i, 0)), pl.BlockSpec((tb, AW), lambda i: (i, 0))],
        out_shape=[jax.ShapeDtypeStruct((T, AW), bf16), jax.ShapeDtypeStruct((T, AW), bf16)],
        scratch_shapes=[pltpu.VMEM((8, 128), f32)],
        compiler_params=_cp("arbitrary"),
    )(fl, bfp)


ATT_T = 512


def _causal_steps(key_major):
    n = T // ATT_T
    if key_major:
        pairs = [(i, j) for j in range(n) for i in range(j, n)]
    else:
        pairs = [(i, j) for i in range(n) for j in range(i + 1)]
    it = np.array([p[0] for p in pairs], np.int32)
    jt = np.array([p[1] for p in pairs], np.int32)
    return jnp.asarray(it), jnp.asarray(jt)


def _attn_fwd(qkv, qaug, kaug):
    tq = tk = ATT_T
    it, jt = _causal_steps(False)
    nsteps = it.shape[0]

    rs = 64

    def body(it_ref, jt_ref, q_ref, k_ref, v_ref, qa_ref, ka_ref, o_ref, lse_ref, m_sc, acc_sc, s_sc, p_sc, alpha_sc):
        t = pl.program_id(1)
        i = it_ref[t]
        j = jt_ref[t]

        @pl.when(j == 0)
        def _():
            m_sc[...] = jnp.full_like(m_sc, NEG)
            acc_sc[...] = jnp.zeros_like(acc_sc)

        lane = lax.broadcasted_iota(jnp.int32, (tq, 128), 1)
        spare = (64, 0)

        def step(on_diagonal):
            q = q_ref[...] * 0.125
            k = k_ref[...]
            v = v_ref[...]
            qa = qa_ref[...]
            ka = ka_ref[...]
            for e in range(2):
                hm = (lane >= 64) if e else (lane < 64)
                s_sc[...] = lax.dot_general(jnp.where(hm, q, qa), jnp.where(hm, k, ka), NT, preferred_element_type=f32)
                for r in range(0, tq, rs):
                    s = s_sc[r:r + rs, :]
                    if on_diagonal:
                        row = lax.broadcasted_iota(jnp.int32, (rs, tk), 0) + r
                        col = lax.broadcasted_iota(jnp.int32, (rs, tk), 1)
                        s = jnp.where(col <= row, s, NEG)
                    m_prev = m_sc[e, r:r + rs, :]
                    m_new = jnp.maximum(m_prev, jnp.max(s, axis=1, keepdims=True))
                    p_sc[r:r + rs, :] = jnp.exp(s - jnp.tile(m_new, (1, tk // 128))).astype(bf16)
                    alpha_sc[r:r + rs, :] = jnp.exp(m_prev - m_new)
                    m_sc[e, r:r + rs, :] = m_new
                ve = jnp.where(hm, v, (lane == spare[e]).astype(bf16))
                acc_sc[e] = alpha_sc[...] * acc_sc[e] + jnp.dot(p_sc[...], ve, preferred_element_type=f32)

        @pl.when(j < i)
        def _():
            step(False)

        @pl.when(j == i)
        def _():
            step(True)
            l0 = acc_sc[0][:, spare[0]:spare[0] + 1]
            l1 = acc_sc[1][:, spare[1]:spare[1] + 1]
            o_ref[...] = jnp.where(lane < 64, acc_sc[0] / l0, acc_sc[1] / l1).astype(bf16)
            lse_ref[...] = jnp.where(lane < 64, m_sc[0] + jnp.log(l0), m_sc[1] + jnp.log(l1))

    qmap = lambda p, t, it, jt: (it[t], p)
    kmap = lambda p, t, it, jt: (jt[t], p)
    grid_spec = pltpu.PrefetchScalarGridSpec(
        num_scalar_prefetch=2, grid=(PAIRS, nsteps),
        in_specs=[pl.BlockSpec((tq, 128), qmap),
                  pl.BlockSpec((tk, 128), lambda p, t, it, jt: (jt[t], PAIRS + p)),
                  pl.BlockSpec((tk, 128), lambda p, t, it, jt: (jt[t], 2 * PAIRS + p)),
                  pl.BlockSpec((tq, 128), qmap), pl.BlockSpec((tk, 128), kmap)],
        out_specs=[pl.BlockSpec((tq, 128), qmap),
                   pl.BlockSpec((None, tq, 128), lambda p, t, it, jt: (p, it[t], 0))],
        scratch_shapes=[pltpu.VMEM((2, tq, 128), f32), pltpu.VMEM((2, tq, 128), f32), pltpu.VMEM((tq, tk), f32),
                        pltpu.VMEM((tq, tk), bf16), pltpu.VMEM((tq, 128), f32)],
    )
    return pl.pallas_call(
        body, name="fox_attn_fwd", grid_spec=grid_spec,
        out_shape=[jax.ShapeDtypeStruct((T, AW), bf16), jax.ShapeDtypeStruct((PAIRS, T, 128), f32)],
        compiler_params=_cp("parallel", "arbitrary"),
    )(it, jt, qkv, qkv, qkv, qaug, kaug)


def _pool_fwd(u, wp, scale):
    tm = 512

    def body(u_ref, wp_ref, sc_ref, pooled_ref, pool_ref, ext):
        i = pl.program_id(0)

        @pl.when(i == 0)
        def _():
            ext[0:HALO, :] = jnp.zeros((HALO, AW), f32)

        uv = u_ref[...]
        ext[HALO:HALO + tm, :] = uv
        t_idx = i * tm + lax.broadcasted_iota(jnp.int32, (tm, 1), 0)
        for g, w in enumerate(WINDOWS):
            lo, hi = 128 * g, 128 * (g + 1)
            ug = uv[:, lo:hi]
            acc = ug
            for d in range(1, w):
                acc = acc + ext[HALO - d:HALO - d + tm, lo:hi]
            cnt = jnp.minimum(t_idx + 1, w).astype(f32)
            pb = (acc / cnt - ug).astype(bf16)
            pooled_ref[:, lo:hi] = pb
            mixed = jnp.dot(pb, wp_ref[g], preferred_element_type=f32)
            pool_ref[:, lo:hi] = (mixed * sc_ref[:, lo:hi]).astype(bf16)
        ext[0:HALO, :] = uv[tm - HALO:tm, :]

    return pl.pallas_call(
        body, name="pool_fwd", grid=(T // tm,),
        in_specs=[pl.BlockSpec((tm, AW), lambda i: (i, 0)), _full((4, 128, 128)), _full((1, AW))],
        out_specs=[pl.BlockSpec((tm, AW), lambda i: (i, 0)), pl.BlockSpec((tm, AW), lambda i: (i, 0))],
        out_shape=[jax.ShapeDtypeStruct((T, AW), bf16), jax.ShapeDtypeStruct((T, AW), bf16)],
        scratch_shapes=[pltpu.VMEM((tm + HALO, AW), f32)],
        compiler_params=_cp("arbitrary"),
    )(u, wp, scale)


def _outproj(x, attn, pool, wo, g2):
    tm = 512

    def body(x_ref, a_ref, p_ref, wo_ref, g_ref, x1_ref, h2_ref):
        x1 = x_ref[...] + jnp.dot(a_ref[...], wo_ref[0:AW, :], preferred_element_type=f32)
        x1 = x1 + jnp.dot(p_ref[...], wo_ref[AW:2 * AW, :], preferred_element_type=f32)
        x1_ref[...] = x1
        r = lax.rsqrt(jnp.mean(x1 * x1, axis=-1, keepdims=True) + EPS)
        h2_ref[...] = (x1 * r * g_ref[...]).astype(bf16)

    return pl.pallas_call(
        body, name="outproj", grid=(T // tm,),
        in_specs=[pl.BlockSpec((tm, D), lambda i: (i, 0)), pl.BlockSpec((tm, AW), lambda i: (i, 0)),
                  pl.BlockSpec((tm, AW), lambda i: (i, 0)), _full((D, D)), _full((1, D))],
        out_specs=[pl.BlockSpec((tm, D), lambda i: (i, 0)), pl.BlockSpec((tm, D), lambda i: (i, 0))],
        out_shape=[jax.ShapeDtypeStruct((T, D), f32), jax.ShapeDtypeStruct((T, D), bf16)],
        compiler_params=_cp("parallel"),
    )(x, attn, pool, wo, g2)


def _mlp_fwd_loss(h2, x1, wg, wu, wd, tgt, gf):
    tm = 512

    def body(h_ref, x1_ref, wg_ref, wu_ref, wd_ref, t_ref, g_ref,
             loss_ref, dg_ref, dx_ref, dxb_ref, gate_ref, up_ref, a_ref, x2):
        i = pl.program_id(0)
        s = pl.program_id(1)

        @pl.when(jnp.logical_and(i == 0, s == 0))
        def _():
            loss_ref[...] = jnp.zeros_like(loss_ref)
            dg_ref[...] = jnp.zeros_like(dg_ref)

        h = h_ref[...]
        for c0, c1 in FS_CHUNKS:
            gate = lax.dot_general(h, wg_ref[c0:c1, :], NT, preferred_element_type=f32)
            up = lax.dot_general(h, wu_ref[c0:c1, :], NT, preferred_element_type=f32)
            gate_ref[:, c0:c1] = gate.astype(bf16)
            up_ref[:, c0:c1] = up.astype(bf16)
            a_ref[:, c0:c1] = (gate * jax.nn.sigmoid(gate) * up).astype(bf16)
        part = jnp.dot(a_ref[...], wd_ref[...], preferred_element_type=f32)

        @pl.when(s == 0)
        def _():
            x2[...] = x1_ref[...] + part

        @pl.when(s > 0)
        def _():
            x2[...] += part

        @pl.when(s == NSH - 1)
        def _():
            xv = x2[...]
            g = g_ref[...]
            r = lax.rsqrt(jnp.mean(xv * xv, axis=-1, keepdims=True) + EPS)
            xhat = xv * r
            e = xhat * g - t_ref[...]
            loss_ref[...] += 0.5 * jnp.sum(jnp.mean(e * e, axis=-1, keepdims=True))
            dy = e * (1.0 / D)
            dg_ref[...] += jnp.sum(dy * xhat, axis=0, keepdims=True)
            z = dy * g
            dx = r * (z - xhat * jnp.mean(z * xhat, axis=-1, keepdims=True))
            dx_ref[...] = dx
            dxb_ref[...] = dx.astype(bf16)

    row = lambda i, s: (i, 0)
    sl = lambda i, s: (s, i, 0)
    wsl = lambda i, s: (s, 0, 0)
    return pl.pallas_call(
        body, name="mlp_fwd_loss", grid=(T // tm, NSH),
        in_specs=[pl.BlockSpec((tm, D), row), pl.BlockSpec((tm, D), row),
                  pl.BlockSpec((None, FS, D), wsl), pl.BlockSpec((None, FS, D), wsl), pl.BlockSpec((None, FS, D), wsl),
                  pl.BlockSpec((tm, D), row), pl.BlockSpec((1, D), lambda i, s: (0, 0))],
        out_specs=[pl.BlockSpec((8, 128), lambda i, s: (0, 0)), pl.BlockSpec((1, D), lambda i, s: (0, 0)),
                   pl.BlockSpec((tm, D), row), pl.BlockSpec((tm, D), row),
                   pl.BlockSpec((None, tm, FS), sl), pl.BlockSpec((None, tm, FS), sl), pl.BlockSpec((None, tm, FS), sl)],
        out_shape=[jax.ShapeDtypeStruct((8, 128), f32), jax.ShapeDtypeStruct((1, D), f32),
                   jax.ShapeDtypeStruct((T, D), f32), jax.ShapeDtypeStruct((T, D), bf16)]
        + [jax.ShapeDtypeStruct((NSH, T, FS), bf16)] * 3,
        scratch_shapes=[pltpu.VMEM((tm, D), f32)],
        compiler_params=_cp("arbitrary", "arbitrary"),
    )(h2, x1, wg, wu, wd, tgt, gf)


def _mlp_bwd(dx2b, dx2, gate, up, wg, wu, wd, x1, g2):
    tm = 512

    def body(dxb_ref, dx_ref, gate_ref, up_ref, wg_ref, wu_ref, wd_ref, x1_ref, g_ref,
             dg_ref, du_ref, dx1_ref, dx1b_ref, dn_ref, acc):
        i = pl.program_id(0)
        s = pl.program_id(1)

        @pl.when(jnp.logical_and(i == 0, s == 0))
        def _():
            dn_ref[...] = jnp.zeros_like(dn_ref)

        dxb = dxb_ref[...]
        for c0, c1 in FS_CHUNKS:
            da = lax.dot_general(dxb, wd_ref[c0:c1, :], NT, preferred_element_type=f32)
            gate = gate_ref[:, c0:c1].astype(f32)
            upv = up_ref[:, c0:c1].astype(f32)
            sg = jax.nn.sigmoid(gate)
            silu = gate * sg
            dg_ref[:, c0:c1] = (da * upv * (sg * (1.0 + gate * (1.0 - sg)))).astype(bf16)
            du_ref[:, c0:c1] = (da * silu).astype(bf16)
        part = jnp.dot(dg_ref[...], wg_ref[...], preferred_element_type=f32)
        part = part + jnp.dot(du_ref[...], wu_ref[...], preferred_element_type=f32)

        @pl.when(s == 0)
        def _():
            acc[...] = part

        @pl.when(s > 0)
        def _():
            acc[...] += part

        @pl.when(s == NSH - 1)
        def _():
            xv = x1_ref[...]
            r = lax.rsqrt(jnp.mean(xv * xv, axis=-1, keepdims=True) + EPS)
            xhat = xv * r
            dh = acc[...]
            dn_ref[...] += jnp.sum(dh * xhat, axis=0, keepdims=True)
            z = dh * g_ref[...]
            dx1 = dx_ref[...] + r * (z - xhat * jnp.mean(z * xhat, axis=-1, keepdims=True))
            dx1_ref[...] = dx1
            dx1b_ref[...] = dx1.astype(bf16)

    row = lambda i, s: (i, 0)
    sl = lambda i, s: (s, i, 0)
    wsl = lambda i, s: (s, 0, 0)
    return pl.pallas_call(
        body, name="mlp_bwd", grid=(T // tm, NSH),
        in_specs=[pl.BlockSpec((tm, D), row), pl.BlockSpec((tm, D), row),
                  pl.BlockSpec((None, tm, FS), sl), pl.BlockSpec((None, tm, FS), sl),
                  pl.BlockSpec((None, FS, D), wsl), pl.BlockSpec((None, FS, D), wsl), pl.BlockSpec((None, FS, D), wsl),
                  pl.BlockSpec((tm, D), row), pl.BlockSpec((1, D), lambda i, s: (0, 0))],
        out_specs=[pl.BlockSpec((None, tm, FS), sl), pl.BlockSpec((None, tm, FS), sl),
                   pl.BlockSpec((tm, D), row), pl.BlockSpec((tm, D), row), pl.BlockSpec((1, D), lambda i, s: (0, 0))],
        out_shape=[jax.ShapeDtypeStruct((NSH, T, FS), bf16)] * 2
        + [jax.ShapeDtypeStruct((T, D), f32), jax.ShapeDtypeStruct((T, D), bf16), jax.ShapeDtypeStruct((1, D), f32)],
        scratch_shapes=[pltpu.VMEM((tm, D), f32)],
        compiler_params=_cp("arbitrary", "arbitrary"),
    )(dx2b, dx2, gate, up, wg, wu, wd, x1, g2)


def _mm_tn(a, bs, name, a_sharded=False, b_sharded=False, tk=512, out_dtype=bf16):
    nb = len(bs)
    sh = NSH if (a_sharded or b_sharded) else 1
    m = a.shape[-1]
    nk = T // tk

    def body(a_ref, *refs):
        kk = pl.program_id(1)
        av = a_ref[...]
        for b_ref, o_ref, acc in zip(refs[:nb], refs[nb:2 * nb], refs[2 * nb:]):
            upd = lax.dot_general(av, b_ref[...], TN, preferred_element_type=f32)

            @pl.when(kk == 0)
            def _():
                acc[...] = upd

            @pl.when(kk > 0)
            def _():
                acc[...] += upd

            @pl.when(kk == nk - 1)
            def _():
                o_ref[...] = acc[...].astype(out_dtype)

    a_spec = (pl.BlockSpec((None, tk, m), lambda s, k: (s, k, 0)) if a_sharded
              else pl.BlockSpec((tk, m), lambda s, k: (k, 0)))
    b_specs, o_specs, o_shapes, scratch = [], [], [], []
    for b in bs:
        n = b.shape[-1]
        b_specs.append(pl.BlockSpec((None, tk, n), lambda s, k: (s, k, 0)) if b_sharded
                       else pl.BlockSpec((tk, n), lambda s, k: (k, 0)))
        scratch.append(pltpu.VMEM((m, n), f32))
        if sh > 1:
            o_specs.append(pl.BlockSpec((None, m, n), lambda s, k: (s, 0, 0)))
            o_shapes.append(jax.ShapeDtypeStruct((sh, m, n), out_dtype))
        else:
            o_specs.append(pl.BlockSpec((m, n), lambda s, k: (0, 0)))
            o_shapes.append(jax.ShapeDtypeStruct((m, n), out_dtype))
    return pl.pallas_call(
        body, name=name, grid=(sh, nk), in_specs=[a_spec] + b_specs, out_specs=o_specs, out_shape=o_shapes,
        scratch_shapes=scratch, compiler_params=_cp("arbitrary", "arbitrary"),
    )(a, *bs)


def _mm_tn_rows(a_list, b, name, tk=1024, out_dtype=bf16):
    na = len(a_list)
    n = b.shape[-1]
    nk = T // tk

    def body(*refs):
        a_refs, b_ref = refs[:na], refs[na]
        o_refs, accs = refs[na + 1:2 * na + 1], refs[2 * na + 1:]
        kk = pl.program_id(0)
        bv = b_ref[...]
        for a_ref, o_ref, acc in zip(a_refs, o_refs, accs):
            upd = lax.dot_general(a_ref[...], bv, TN, preferred_element_type=f32)

            @pl.when(kk == 0)
            def _():
                acc[...] = upd

            @pl.when(kk > 0)
            def _():
                acc[...] += upd

            @pl.when(kk == nk - 1)
            def _():
                o_ref[...] = acc[...].astype(out_dtype)

    return pl.pallas_call(
        body, name=name, grid=(nk,),
        in_specs=[pl.BlockSpec((tk, a.shape[-1]), lambda k: (k, 0)) for a in a_list] + [pl.BlockSpec((tk, n), lambda k: (k, 0))],
        out_specs=[pl.BlockSpec((a.shape[-1], n), lambda k: (0, 0)) for a in a_list],
        out_shape=[jax.ShapeDtypeStruct((a.shape[-1], n), out_dtype) for a in a_list],
        scratch_shapes=[pltpu.VMEM((a.shape[-1], n), f32) for a in a_list],
        compiler_params=_cp("arbitrary"),
    )(*a_list, b)


def _outproj_bwd(dx1b, wo):
    tm = 512

    def body(dx_ref, wo_ref, da_ref, dp_ref):
        dx = dx_ref[...]
        da_ref[...] = lax.dot_general(dx, wo_ref[0:AW, :], NT, preferred_element_type=f32).astype(bf16)
        dp_ref[...] = lax.dot_general(dx, wo_ref[AW:2 * AW, :], NT, preferred_element_type=f32)

    return pl.pallas_call(
        body, name="outproj_bwd", grid=(T // tm,),
        in_specs=[pl.BlockSpec((tm, D), lambda i: (i, 0)), _full((D, D))],
        out_specs=[pl.BlockSpec((tm, AW), lambda i: (i, 0)), pl.BlockSpec((tm, AW), lambda i: (i, 0))],
        out_shape=[jax.ShapeDtypeStruct((T, AW), bf16), jax.ShapeDtypeStruct((T, AW), f32)],
        compiler_params=_cp("parallel"),
    )(dx1b, wo)


def _pool_bwd(dpool, pooled, wp, scale, dep):
    tm = 512
    n = T // tm

    def body(dp_ref, pb_ref, wp_ref, sc_ref, dep_ref, du_ref, dsc_ref, dwp_ref, ext):
        i = pl.program_id(0)

        @pl.when(i == 0)
        def _():
            ext[tm:tm + HALO, :] = jnp.zeros((HALO, AW), f32)
            dsc_ref[...] = jnp.zeros_like(dsc_ref)
            dwp_ref[...] = jnp.zeros_like(dwp_ref)

        t_idx = (n - 1 - i) * tm + lax.broadcasted_iota(jnp.int32, (tm, 1), 0)
        for g, w in enumerate(WINDOWS):
            lo, hi = 128 * g, 128 * (g + 1)
            pb = pb_ref[:, lo:hi]
            mixed = jnp.dot(pb, wp_ref[g], preferred_element_type=f32)
            dpo = dp_ref[:, lo:hi]
            dsc_ref[:, lo:hi] += jnp.sum(dpo * mixed, axis=0, keepdims=True)
            dmr = (dpo * sc_ref[:, lo:hi]).astype(bf16)
            dwp_ref[g] += lax.dot_general(pb, dmr, TN, preferred_element_type=f32)
            dpl = lax.dot_general(dmr, wp_ref[g], NT, preferred_element_type=f32)
            cnt = jnp.minimum(t_idx + 1, w).astype(f32)
            dpn = dpl / cnt
            ext[0:tm, lo:hi] = dpn
            acc = dpn
            for d in range(1, w):
                acc = acc + ext[d:d + tm, lo:hi]
            du_ref[:, lo:hi] = (acc - dpl).astype(bf16)
        ext[tm:tm + HALO, :] = ext[0:HALO, :]

    rev = lambda i: (n - 1 - i, 0)
    return pl.pallas_call(
        body, name="pool_bwd", grid=(n,),
        in_specs=[pl.BlockSpec((tm, AW), rev), pl.BlockSpec((tm, AW), rev), _full((4, 128, 128)), _full((1, AW)),
                  _full((8, 128))],
        out_specs=[pl.BlockSpec((tm, AW), rev), _full((1, AW)), _full((4, 128, 128))],
        out_shape=[jax.ShapeDtypeStruct((T, AW), bf16), jax.ShapeDtypeStruct((1, AW), f32),
                   jax.ShapeDtypeStruct((4, 128, 128), f32)],
        scratch_shapes=[pltpu.VMEM((tm + HALO, AW), f32)],
        compiler_params=_cp("arbitrary"),
    )(dpool, pooled, wp, scale, dep)


def _attn_bwd(qkv, qaug, kaug, attn, dattn, lse, dep):
    tq = tk = ATT_T
    n = T // tq
    it, jt = _causal_steps(True)
    nsteps = it.shape[0]

    rs = 64

    def body(it_ref, jt_ref, q_ref, k_ref, v_ref, qa_ref, ka_ref, o_ref, do_ref, lse_ref, dep_ref,
             dq_ref, dqs_ref, dk_ref, dks_ref, dv_ref, dq_acc, dk_acc, dv_acc, s_sc, dp_sc, p_sc, ds_sc):
        t = pl.program_id(1)
        i = it_ref[t]
        j = jt_ref[t]

        @pl.when(t == 0)
        def _():
            dq_acc[...] = jnp.zeros_like(dq_acc)

        @pl.when(i == j)
        def _():
            dk_acc[...] = jnp.zeros_like(dk_acc)
            dv_acc[...] = jnp.zeros_like(dv_acc)

        lane = lax.broadcasted_iota(jnp.int32, (tq, 128), 1)

        def step(on_diagonal):
            q = q_ref[...] * 0.125
            k = k_ref[...]
            v = v_ref[...]
            qa = qa_ref[...]
            ka = ka_ref[...]
            do = do_ref[...]
            dd = do.astype(f32) * o_ref[...].astype(f32)
            r0 = pl.multiple_of(i * tq, tq)
            for e in range(2):
                hm = (lane >= 64) if e else (lane < 64)
                qe = jnp.where(hm, q, qa)
                ke = jnp.where(hm, k, ka)
                doe = jnp.where(hm, do, jnp.zeros_like(do))
                delta = jnp.sum(jnp.where(hm, dd, 0.0), axis=1, keepdims=True)
                s_sc[...] = lax.dot_general(qe, ke, NT, preferred_element_type=f32)
                dp_sc[...] = lax.dot_general(doe, v, NT, preferred_element_type=f32)
                for r in range(0, tq, rs):
                    s = s_sc[r:r + rs, :] - lse_ref[r:r + rs, 64 * e:64 * e + 1]
                    if on_diagonal:
                        row = lax.broadcasted_iota(jnp.int32, (rs, tk), 0) + r
                        col = lax.broadcasted_iota(jnp.int32, (rs, tk), 1)
                        s = jnp.where(col <= row, s, NEG)
                    p = jnp.exp(s)
                    p_sc[r:r + rs, :] = p.astype(bf16)
                    ds_sc[r:r + rs, :] = (p * (dp_sc[r:r + rs, :] - delta[r:r + rs, :])).astype(bf16)
                dv_acc[...] += lax.dot_general(p_sc[...], doe, TN, preferred_element_type=f32)
                dsb = ds_sc[...]
                dk_acc[e] += lax.dot_general(dsb, qe, TN, preferred_element_type=f32)
                dq_acc[e, pl.ds(r0, tq), :] += jnp.dot(dsb, ke, preferred_element_type=f32)

        @pl.when(i > j)
        def _():
            step(False)

        @pl.when(i == j)
        def _():
            step(True)

        @pl.when(i == n - 1)
        def _():
            dk_ref[...] = jnp.where(lane < 64, dk_acc[0], dk_acc[1]).astype(bf16)
            dks_ref[...] = jnp.where(lane < 64, dk_acc[1], dk_acc[0])
            dv_ref[...] = dv_acc[...].astype(bf16)

        @pl.when(t == nsteps - 1)
        def _():
            lane_t = lax.broadcasted_iota(jnp.int32, (T, 128), 1)
            dq_ref[...] = (jnp.where(lane_t < 64, dq_acc[0], dq_acc[1]) * 0.125).astype(bf16)
            dqs_ref[...] = jnp.where(lane_t < 64, dq_acc[1], dq_acc[0])

    qmap = lambda p, t, it, jt: (it[t], p)
    grid_spec = pltpu.PrefetchScalarGridSpec(
        num_scalar_prefetch=2, grid=(PAIRS, nsteps),
        in_specs=[pl.BlockSpec((tq, 128), qmap),
                  pl.BlockSpec((tk, 128), lambda p, t, it, jt: (jt[t], PAIRS + p)),
                  pl.BlockSpec((tk, 128), lambda p, t, it, jt: (jt[t], 2 * PAIRS + p)),
                  pl.BlockSpec((tq, 128), qmap), pl.BlockSpec((tk, 128), lambda p, t, it, jt: (jt[t], p)),
                  pl.BlockSpec((tq, 128), qmap), pl.BlockSpec((tq, 128), qmap),
                  pl.BlockSpec((None, tq, 128), lambda p, t, it, jt: (p, it[t], 0)),
                  pl.BlockSpec((8, 128), lambda p, t, it, jt: (0, 0))],
        out_specs=[pl.BlockSpec((T, 128), lambda p, t, it, jt: (0, p)),
                   pl.BlockSpec((None, T, 128), lambda p, t, it, jt: (p, 0, 0)),
                   pl.BlockSpec((tk, 128), lambda p, t, it, jt: (jt[t], p)),
                   pl.BlockSpec((None, tk, 128), lambda p, t, it, jt: (p, jt[t], 0)),
                   pl.BlockSpec((tk, 128), lambda p, t, it, jt: (jt[t], p))],
        scratch_shapes=[pltpu.VMEM((2, T, 128), f32), pltpu.VMEM((2, tk, 128), f32), pltpu.VMEM((tk, 128), f32),
                        pltpu.VMEM((tq, tk), f32), pltpu.VMEM((tq, tk), f32), pltpu.VMEM((tq, tk), bf16),
                        pltpu.VMEM((tq, tk), bf16)],
    )
    return pl.pallas_call(
        body, name="fox_attn_bwd", grid_spec=grid_spec,
        out_shape=[jax.ShapeDtypeStruct((T, AW), bf16), jax.ShapeDtypeStruct((PAIRS, T, 128), f32),
                   jax.ShapeDtypeStruct((T, AW), bf16), jax.ShapeDtypeStruct((PAIRS, T, 128), f32),
                   jax.ShapeDtypeStruct((T, AW), bf16)],
        compiler_params=_cp("parallel", "arbitrary"),
    )(it, jt, qkv, qkv, qkv, qaug, kaug, attn, dattn, lse, dep)


def _fox_cumsum_bwd(dqs, dks, fl, bfp):
    tb = CUMSUM_ROWS
    nb = T // tb
    hp = lax.Precision.HIGHEST

    def body(dqs_ref, dks_ref, fl_ref, b_ref, df_ref, db_ref, carry):
        i = pl.program_id(0)

        @pl.when(i == 0)
        def _():
            carry[...] = jnp.zeros_like(carry)
            db_ref[...] = jnp.zeros_like(db_ref)

        r = lax.broadcasted_iota(jnp.int32, (128, 128), 0)
        cc = lax.broadcasted_iota(jnp.int32, (128, 128), 1)
        pick = lambda even_lane, odd_lane, p: jnp.logical_or(
            jnp.logical_and(r == even_lane, cc == 2 * p), jnp.logical_and(r == odd_lane, cc == 2 * p + 1)).astype(f32)
        dc = jnp.zeros((tb, 128), f32)
        for p in range(PAIRS):
            dc = dc + jnp.dot(dqs_ref[p], pick(64, 0, p), precision=hp, preferred_element_type=f32)
            dc = dc - jnp.dot(dks_ref[p], pick(67, 3, p), precision=hp, preferred_element_type=f32)
        rt = lax.broadcasted_iota(jnp.int32, (tb, tb), 0)
        ct = lax.broadcasted_iota(jnp.int32, (tb, tb), 1)
        utri = (ct >= rt).astype(f32)
        dl = jnp.dot(utri, dc, precision=hp, preferred_element_type=f32) + carry[0:1, :]
        carry[...] = jnp.broadcast_to(dl[0:1, :], (8, 128))
        z = fl_ref[...] + b_ref[...]
        df = dl * jax.nn.sigmoid(-z)
        df_ref[...] = df.astype(bf16)
        db_ref[...] += jnp.sum(df, axis=0, keepdims=True)

    rev = lambda i: (nb - 1 - i, 0)
    return pl.pallas_call(
        body, name="fox_cumsum_bwd", grid=(nb,),
        in_specs=[pl.BlockSpec((PAIRS, tb, 128), lambda i: (0, nb - 1 - i, 0)),
                  pl.BlockSpec((PAIRS, tb, 128), lambda i: (0, nb - 1 - i, 0)),
                  pl.BlockSpec((tb, 128), rev), _full((1, 128))],
        out_specs=[pl.BlockSpec((tb, 128), rev), _full((1, 128))],
        out_shape=[jax.ShapeDtypeStruct((T, 128), bf16), jax.ShapeDtypeStruct((1, 128), f32)],
        scratch_shapes=[pltpu.VMEM((8, 128), f32)],
        compiler_params=_cp("arbitrary"),
    )(dqs, dks, fl, bfp)


def _inproj_bwd(dq, dk, dv, du, df, wm, wf, x, dx1, g1):
    tm = 512

    def body(dq_ref, dk_ref, dv_ref, du_ref, df_ref, wm_ref, wf_ref, x_ref, dx1_ref, g_ref, dx_ref, dn_ref):
        i = pl.program_id(0)

        @pl.when(i == 0)
        def _():
            dn_ref[...] = jnp.zeros_like(dn_ref)

        dh = jnp.dot(dq_ref[...], wm_ref[0:AW, :], preferred_element_type=f32)
        dh = dh + jnp.dot(dk_ref[...], wm_ref[AW:2 * AW, :], preferred_element_type=f32)
        dh = dh + jnp.dot(dv_ref[...], wm_ref[2 * AW:3 * AW, :], preferred_element_type=f32)
        dh = dh + jnp.dot(du_ref[...], wm_ref[3 * AW:4 * AW, :], preferred_element_type=f32)
        dh = dh + jnp.dot(df_ref[...], wf_ref[...], preferred_element_type=f32)
        xv = x_ref[...]
        r = lax.rsqrt(jnp.mean(xv * xv, axis=-1, keepdims=True) + EPS)
        xhat = xv * r
        dn_ref[...] += jnp.sum(dh * xhat, axis=0, keepdims=True)
        z = dh * g_ref[...]
        dx_ref[...] = dx1_ref[...] + r * (z - xhat * jnp.mean(z * xhat, axis=-1, keepdims=True))

    row = lambda i: (i, 0)
    return pl.pallas_call(
        body, name="inproj_bwd", grid=(T // tm,),
        in_specs=[pl.BlockSpec((tm, AW), row)] * 4 + [pl.BlockSpec((tm, 128), row), _full((4 * AW, D)), _full((128, D)),
                                                       pl.BlockSpec((tm, D), row), pl.BlockSpec((tm, D), row), _full((1, D))],
        out_specs=[pl.BlockSpec((tm, D), row), _full((1, D))],
        out_shape=[jax.ShapeDtypeStruct((T, D), f32), jax.ShapeDtypeStruct((1, D), f32)],
        compiler_params=_cp("arbitrary"),
    )(dq, dk, dv, du, df, wm, wf, x, dx1, g1)


def _adamw_math(w, g, m, v):
    m = B1 * m + (1.0 - B1) * g
    v = B2 * v + (1.0 - B2) * (g * g)
    m_hat = m / (1.0 - B1 ** STEP)
    v_hat = v / (1.0 - B2 ** STEP)
    delta = -LR * (m_hat / (jnp.sqrt(v_hat) + AEPS) + WD * w)
    return delta, m, v


def _adamw_shard(w, m, v, p_mine, p_other, name):
    rows, cols = w.shape
    tr = 256 if rows % 256 == 0 else (176 if rows % 176 == 0 else rows)

    def body(w_ref, m_ref, v_ref, a_ref, b_ref, g_ref, d_ref, nm_ref, nv_ref):
        g = a_ref[...].astype(f32) + b_ref[...].astype(f32)
        g_ref[...] = g
        d_ref[...], nm_ref[...], nv_ref[...] = _adamw_math(w_ref[...], g, m_ref[...], v_ref[...])

    spec = pl.BlockSpec((tr, cols), lambda i: (i, 0))
    return pl.pallas_call(
        body, name=name, grid=(rows // tr,), in_specs=[spec] * 5, out_specs=[spec] * 4,
        out_shape=[jax.ShapeDtypeStruct((rows, cols), f32)] * 4, compiler_params=_cp("parallel"),
    )(w, m, v, p_mine, p_other)


SMALL_SLOTS = ((0, 8, 128), (8, 16, 128), (16, 24, 128), (24, 28, 128), (32, 33, 8), (40, 552, 128))
LOSS_ROW = 39


def _adamw_small(ws, ms, vs, parts):
    n = len(ws)

    def body(*refs):
        w_refs, m_refs, v_refs, p_ref = refs[0:n], refs[n:2 * n], refs[2 * n:3 * n], refs[3 * n]
        outs = refs[3 * n + 1:]
        g_all = p_ref[0]
        for k in range(1, 8):
            g_all = g_all + p_ref[k]
        for idx, (r0, r1, lanes) in enumerate(SMALL_SLOTS):
            g = g_all[r0:r1, 0:lanes]
            d, nm, nv = _adamw_math(w_refs[idx][...], g, m_refs[idx][...], v_refs[idx][...])
            outs[idx][...] = g
            outs[n + idx][...] = d
            outs[2 * n + idx][...] = nm
            outs[3 * n + idx][...] = nv
        outs[4 * n][...] = g_all[LOSS_ROW:LOSS_ROW + 1, :]

    shapes = [jax.ShapeDtypeStruct(w.shape, f32) for w in ws]
    res = pl.pallas_call(
        body, name="adamw_small", out_shape=shapes * 4 + [jax.ShapeDtypeStruct((1, 128), f32)],
    )(*ws, *ms, *vs, parts)
    return res[:4 * n], res[4 * n]


def _sum4(recv, g, mine, name):
    _, rows, cols = recv.shape
    tr = 256 if rows % 256 == 0 else (176 if rows % 176 == 0 else rows)

    def body(mine_ref, r_ref, g_ref, o_ref):
        o_ref[...] = ((g_ref[...].astype(f32) + r_ref[0].astype(f32))
                      + (r_ref[1].astype(f32) + r_ref[2].astype(f32))).astype(bf16)

    grid_spec = pltpu.PrefetchScalarGridSpec(
        num_scalar_prefetch=1, grid=(rows // tr,),
        in_specs=[pl.BlockSpec((3, tr, cols), lambda i, m: (0, i, 0)),
                  pl.BlockSpec((None, tr, cols), lambda i, m: (m[0], i, 0))],
        out_specs=pl.BlockSpec((tr, cols), lambda i, m: (i, 0)))
    return pl.pallas_call(
        body, name=name, grid_spec=grid_spec, out_shape=jax.ShapeDtypeStruct((rows, cols), bf16),
        compiler_params=_cp("arbitrary"),
    )(mine, recv, g)


_HBM = pl.BlockSpec(memory_space=pltpu.HBM)
_SEM = pl.BlockSpec(memory_space=pltpu.SEMAPHORE)
_EFFECT = pltpu.SideEffectType.DATAFLOW_SIDE_EFFECTING


def _in_hbm(a):
    return pltpu.with_memory_space_constraint(a, pltpu.HBM)


def _mesh_pos():
    return lax.axis_index("x"), lax.axis_index("y"), lax.axis_index("c")


def _other_chips(x, y):
    return [(1 - x, y), (x, 1 - y), (1 - x, 1 - y)]


def _gather_copy(srcs, lands, send_sems, recv_sems, a, k, slot):
    x, y, c = _mesh_pos()
    cx, cy = _other_chips(x, y)[k]
    return pltpu.make_async_remote_copy(
        src_ref=srcs[a], dst_ref=lands[a].at[slot], send_sem=send_sems.at[3 * a + k], recv_sem=recv_sems.at[3 * a + k],
        device_id=(cx, cy, c), device_id_type=MESH)


def _scatter_copy(srcs, lands, send_sems, recv_sems, a, k):
    x, y, c = _mesh_pos()
    cx, cy = _other_chips(x, y)[k]
    return pltpu.make_async_remote_copy(
        src_ref=srcs[a].at[2 * cx + cy], dst_ref=lands[a].at[k], send_sem=send_sems.at[3 * a + k],
        recv_sem=recv_sems.at[3 * a + k], device_id=(cx, cy, c), device_id_type=MESH)


def _all_gather_w_in(part):
    cols = part.shape[1] // 2

    def body(src, dst, send_sems, recv_sems, loc_sem):
        x, y, c = _mesh_pos()
        mine = 2 * x + y
        chips = _other_chips(x, y)
        half = lambda ref, cc: ref.at[:, pl.ds(pl.multiple_of(cc * cols, cols), cols)]

        def over_ici(k, slot):
            cx, cy = chips[k]
            return pltpu.make_async_remote_copy(
                src_ref=half(src, c), dst_ref=half(dst.at[slot], c), send_sem=send_sems.at[k], recv_sem=recv_sems.at[k],
                device_id=(cx, cy, c), device_id_type=MESH)

        def to_sibling(k, cc):
            slot = 2 * chips[k][0] + chips[k][1]
            return pltpu.make_async_remote_copy(
                src_ref=half(dst.at[slot], cc), dst_ref=half(dst.at[slot], cc), send_sem=send_sems.at[3 + k],
                recv_sem=recv_sems.at[3 + k], device_id=(x, y, 1 - c), device_id_type=MESH)

        local = pltpu.make_async_copy(src, dst.at[mine], loc_sem.at[0])
        local.start()
        first = [over_ici(k, mine) for k in range(3)]
        for cp in first:
            cp.start()
        passed = [to_sibling(k, c) for k in range(3)]
        for k in range(3):
            over_ici(k, 2 * chips[k][0] + chips[k][1]).wait_recv()
            passed[k].start()
        for k in range(3):
            to_sibling(k, 1 - c).wait_recv()
        for cp in first + passed:
            cp.wait_send()
        local.wait()

    return pl.pallas_call(
        body, name="all_gather_w_in", in_specs=[_HBM], out_specs=_HBM,
        out_shape=jax.ShapeDtypeStruct((NSH,) + part.shape, part.dtype),
        scratch_shapes=[pltpu.SemaphoreType.DMA((6,)), pltpu.SemaphoreType.DMA((6,)), pltpu.SemaphoreType.DMA((1,))],
    )(part)


def _split_start(name, srcs, lands, n_sems, plan, dep):
    n, nl = len(srcs), len(lands)

    def body(*refs):
        src_refs, land_refs = refs[:n], refs[n:n + nl]
        send_sems, recv_sems = refs[n + nl + 1], refs[n + nl + 2]
        token = refs[-1]
        sends, _ = plan(src_refs, land_refs, send_sems, recv_sems)
        for cp in sends:
            cp.start()
        token[...] = jnp.zeros_like(token)

    outs = pl.pallas_call(
        body, name=name,
        in_specs=[_HBM] * (n + nl) + [pl.BlockSpec(memory_space=pl.ANY)],
        out_specs=[_SEM, _SEM] + [_HBM] * (n + nl) + [pl.BlockSpec(memory_space=pltpu.VMEM)],
        out_shape=[pltpu.SemaphoreType.DMA((n_sems,)), pltpu.SemaphoreType.DMA((n_sems,))]
        + [pltpu.HBM(a.shape, a.dtype) for a in list(srcs) + list(lands)] + [jax.ShapeDtypeStruct((8, 128), f32)],
        input_output_aliases={i: 2 + i for i in range(n + nl)},
        compiler_params=pltpu.CompilerParams(has_side_effects=_EFFECT),
    )(*[_in_hbm(a) for a in list(srcs) + list(lands)], dep)
    return outs[0], outs[1], list(outs[2:2 + n]), list(outs[2 + n:2 + n + nl]), outs[-1]


def _split_wait(name, send_sems, recv_sems, srcs, lands, after, plan):
    n, nl = len(srcs), len(lands)

    def body(*refs):
        src_refs, land_refs = refs[:n], refs[n:n + nl]
        s_sems, r_sems = refs[n + nl], refs[n + nl + 1]
        sends, recvs = plan(src_refs, land_refs, s_sems, r_sems)
        for cp in recvs:
            cp.wait_recv()
        for cp in sends:
            cp.wait_send()

    outs = pl.pallas_call(
        body, name=name,
        in_specs=[_HBM] * (n + nl) + [_SEM, _SEM, pl.BlockSpec(memory_space=pl.ANY)],
        out_specs=[_HBM] * (n + nl),
        out_shape=[pltpu.HBM(a.shape, a.dtype) for a in list(srcs) + list(lands)],
        input_output_aliases={i: i for i in range(n + nl)},
        compiler_params=pltpu.CompilerParams(has_side_effects=_EFFECT),
    )(*srcs, *lands, send_sems, recv_sems, after)
    return list(outs[:n]), list(outs[n:])


def _gather_plan(srcs, lands, ss, rs):
    x, y, _ = _mesh_pos()
    chips = _other_chips(x, y)
    sends = [_gather_copy(srcs, lands, ss, rs, a, k, 2 * x + y) for a in range(len(srcs)) for k in range(3)]
    recvs = [_gather_copy(srcs, lands, ss, rs, a, k, 2 * chips[k][0] + chips[k][1])
             for a in range(len(srcs)) for k in range(3)]
    return sends, recvs


def _scatter_plan(srcs, lands, ss, rs):
    cps = [_scatter_copy(srcs, lands, ss, rs, a, k) for a in range(len(srcs)) for k in range(3)]
    return cps, cps


def _tail_plan(srcs, lands, ss, rs):
    x, y, c = _mesh_pos()
    me = 4 * x + 2 * y + c
    cps = [_scatter_copy(srcs[:1], lands[:1], ss, rs, 0, k) for k in range(3)]
    for f in range(1, 8):
        peer = ((x + (f >> 2)) % 2, (y + ((f >> 1) & 1)) % 2, (c + (f & 1)) % 2)
        cps.append(pltpu.make_async_remote_copy(
            src_ref=srcs[1], dst_ref=lands[1].at[me], send_sem=ss.at[2 + f], recv_sem=rs.at[2 + f],
            device_id=peer, device_id_type=MESH))
    return cps, cps


def _swap_with_sibling(parts, name):
    n = len(parts)

    def body(*refs):
        srcs, dsts = refs[:n], refs[n:2 * n]
        send_sems, recv_sems = refs[2 * n:]
        x, y, c = _mesh_pos()
        cps = [pltpu.make_async_remote_copy(src_ref=srcs[a], dst_ref=dsts[a], send_sem=send_sems.at[a],
                                            recv_sem=recv_sems.at[a], device_id=(x, y, 1 - c), device_id_type=MESH)
               for a in range(n)]
        for cp in cps:
            cp.start()
        for cp in cps:
            cp.wait_recv()
        for cp in cps:
            cp.wait_send()

    return pl.pallas_call(
        body, name=name, in_specs=[_HBM] * n, out_specs=[_HBM] * n,
        out_shape=[jax.ShapeDtypeStruct(p.shape, p.dtype) for p in parts],
        scratch_shapes=[pltpu.SemaphoreType.DMA((n,)), pltpu.SemaphoreType.DMA((n,))],
    )(*parts)


def _forward(x, tgt, wm, wf, mlp_w_fn, g1, bfp, wp, scale, g2, gf, dep):
    h, qkv, u, fl = _rms_inproj(x, g1, wm, wf, dep)
    qaug, kaug = _fox_cumsum(fl, bfp)
    attn, lse = _attn_fwd(qkv, qaug, kaug)
    pooled, pool = _pool_fwd(u, wp, scale)
    wo, wgt, wut, wd = mlp_w_fn(attn)
    x1, h2 = _outproj(x, attn, pool, wo, g2)
    loss, dgf, dx2, dx2b, gate, up, a_b = _mlp_fwd_loss(h2, x1, wgt, wut, wd, tgt, gf)
    saved = dict(h=h, qkv=qkv, fl=fl, qaug=qaug, kaug=kaug, attn=attn, lse=lse, pooled=pooled, pool=pool, x1=x1, h2=h2,
                 gate=gate, up=up, a_b=a_b, wo=wo, wgt=wgt, wut=wut, wd=wd)
    return loss, dgf, dx2, dx2b, saved


def _backward_mlp(sv, dx2, dx2b, g2):
    dgate, dup, dx1, dx1b, dg2 = _mlp_bwd(dx2b, dx2, sv["gate"], sv["up"], sv["wgt"], sv["wut"], sv["wd"], sv["x1"], g2)
    (dwd,) = _mm_tn(sv["a_b"], [dx2b], "dw_down", a_sharded=True, tk=2048)
    (dwgt,) = _mm_tn(dgate, [sv["h2"]], "dw_gate", a_sharded=True, tk=2048)
    (dwut,) = _mm_tn(dup, [sv["h2"]], "dw_up", a_sharded=True, tk=2048)
    return dx1, dx1b, dg2, (dwgt, dwut, dwd)


def _backward_outproj(sv, dx1b):
    dattn, dpool = _outproj_bwd(dx1b, sv["wo"])
    dwo_a, = _mm_tn(sv["attn"], [dx1b], "dw_out_attn", tk=2048)
    dwo_p, = _mm_tn(sv["pool"], [dx1b], "dw_out_pool", tk=2048)
    dwo = jnp.concatenate([dwo_a, dwo_p], axis=0).reshape(NSH, D // NSH, D)
    return dattn, dpool, dwo


def _backward_mixer(sv, x, dx1, dattn, dpool, wm, wf, g1, bfp, wp, scale, dep):
    du, dscale, dwp = _pool_bwd(dpool, sv["pooled"], wp, scale, dep)
    dq, dqs, dk, dks, dv = _attn_bwd(sv["qkv"], sv["qaug"], sv["kaug"], sv["attn"], dattn, sv["lse"], dep)
    df, dbf = _fox_cumsum_bwd(dqs, dks, sv["fl"], bfp)
    dx, dg1 = _inproj_bwd(dq, dk, dv, du, df, wm, wf, x, dx1, g1)
    dwq, dwk, dwv, dwu_in, dwf = _mm_tn_rows([dq, dk, dv, du, df], sv["h"], "dw_in")
    dwin = jnp.concatenate([dwq, dwk, dwv, dwf[0:8], dwu_in], axis=0)
    return dx, dg1, dscale, dwp, dbf, dwin.reshape(NSH, IN_S, D)


def kernel(x, norm1_g, w_in, b_forget, w_pool, pool_scale, w_out, norm2_g, w_gate, w_up, w_down, final_g, loss_target, m_norm1_g, m_w_in, m_b_forget, m_w_pool, m_pool_scale, m_w_out, m_norm2_g, m_w_gate, m_w_up, m_w_down, m_final_g, v_norm1_g, v_w_in, v_b_forget, v_w_pool, v_pool_scale, v_w_out, v_norm2_g, v_w_gate, v_w_up, v_w_down, v_final_g):
    mine = (2 * lax.axis_index("x") + lax.axis_index("y")).astype(jnp.int32)
    mine1 = mine.reshape(1)
    tr = lambda a: jnp.transpose(a[0])

    win4 = _all_gather_w_in(tr(w_in).astype(bf16))
    later = [w_out[0].astype(bf16), tr(w_gate).astype(bf16), tr(w_up).astype(bf16), w_down[0].astype(bf16)]
    lands = [lax.dynamic_update_slice(lax.empty((NSH,) + p.shape, bf16), p[None], (mine, 0, 0)) for p in later]
    ag_send, ag_recv, later_thru, lands_thru, ag_token = _split_start("all_gather_start", later, lands, 12, _gather_plan,
                                                                      win4)
    win = win4.reshape(IN_W, D)
    wm = jnp.concatenate([win[0:3 * AW], win[3 * AW + 8:]], axis=0)
    wf = jnp.pad(win[3 * AW:3 * AW + 8], ((0, 120), (0, 0)))
    bfp = jnp.pad(b_forget, ((0, 0), (0, 120)))
    wp = w_pool[0].astype(bf16)
    gf = final_g.reshape(1, D)

    def later_weights(after):
        _, (wo4, wgt, wut, wd) = _split_wait("all_gather_wait", ag_send, ag_recv, later_thru, lands_thru, after, _gather_plan)
        return wo4.reshape(D, D), wgt, wut, wd

    xe, tgt = x[0], loss_target[0]
    loss_v, dgf, dx2, dx2b, sv = _forward(xe, tgt, wm, wf, later_weights, norm1_g, bfp, wp, pool_scale, norm2_g, gf, ag_token)
    dx1, dx1b, dg2, mlp_grads = _backward_mlp(sv, dx2, dx2b, norm2_g)
    dattn, dpool, dwo = _backward_outproj(sv, dx1b)
    first = [dwo] + list(mlp_grads)
    first_lands = [lax.empty((3,) + g.shape[1:], bf16) for g in first]
    rs_send, rs_recv, first_thru, first_lands_thru, rs_token = _split_start("reduce_scatter_start", first, first_lands, 12,
                                                                            _scatter_plan, dattn)
    dx, dg1, dscale, dwp, dbf, dwin = _backward_mixer(sv, xe, dx1, dattn, dpool, wm, wf, norm1_g, bfp, wp, pool_scale, rs_token)

    me = (4 * lax.axis_index("x") + 2 * lax.axis_index("y") + lax.axis_index("c")).astype(jnp.int32)
    pad8 = lambda r: jnp.pad(r, ((0, 8 - r.shape[0]), (0, 0)))
    loss_rows = jnp.concatenate([dbf, jnp.zeros((6, 128), f32), loss_v[0:1, :]], axis=0)
    small = jnp.concatenate([dg1.reshape(8, 128), dg2.reshape(8, 128), dgf.reshape(8, 128), pad8(dscale.reshape(4, 128)),
                             loss_rows, dwp.reshape(512, 128)], axis=0)
    small_land = lax.dynamic_update_slice(lax.empty((8, SMALL_ROWS, 128), f32), small[None], (me, 0, 0))
    tail_send, tail_recv, tail_thru, tail_lands_thru, tail_token = _split_start(
        "tail_start", [dwin, small], [lax.empty((3,) + dwin.shape[1:], bf16), small_land], 10, _tail_plan, dx)
    first_thru, first_recv = _split_wait("reduce_scatter_wait", rs_send, rs_recv, first_thru, first_lands_thru, tail_token,
                                         _scatter_plan)
    ws = [tr(w_in), w_out[0], tr(w_gate), tr(w_up), w_down[0]]
    ms = [tr(m_w_in), m_w_out[0], tr(m_w_gate), tr(m_w_up), m_w_down[0]]
    vs = [tr(v_w_in), v_w_out[0], tr(v_w_gate), tr(v_w_up), v_w_down[0]]
    partial = [_sum4(r, g, mine1, f"sum4_{i + 1}") for i, (r, g) in enumerate(zip(first_recv, first_thru))]
    other = _swap_with_sibling(partial, "swap_first")
    big = [_adamw_shard(ws[i + 1], ms[i + 1], vs[i + 1], partial[i], other[i], f"adamw_{i + 1}") for i in range(4)]
    (dwin_thru, _), (in_recv_land, small_all) = _split_wait("tail_wait", tail_send, tail_recv, tail_thru, tail_lands_thru,
                                                            big[3][0], _tail_plan)
    partial_in = _sum4(in_recv_land, dwin_thru, mine1, "sum4_0")
    (other_in,) = _swap_with_sibling([partial_in], "swap_in")
    big = [_adamw_shard(ws[0], ms[0], vs[0], partial_in, other_in, "adamw_0")] + big

    small_names = ["norm1_g", "norm2_g", "final_g", "pool_scale", "b_forget", "w_pool"]
    rows = lambda a, b, c, d, e, f: [a.reshape(8, 128), b.reshape(8, 128), c.reshape(8, 128), d.reshape(4, 128),
                                     e.reshape(1, 8), f.reshape(512, 128)]
    sm, loss_row = _adamw_small(rows(norm1_g, norm2_g, final_g, pool_scale, b_forget, w_pool),
                                rows(m_norm1_g, m_norm2_g, m_final_g, m_pool_scale, m_b_forget, m_w_pool),
                                rows(v_norm1_g, v_norm2_g, v_final_g, v_pool_scale, v_b_forget, v_w_pool), small_all)
    small_shape = dict(norm1_g=(1, D), norm2_g=(1, D), final_g=(D,), pool_scale=(1, AW), b_forget=(1, 8),
                       w_pool=(1, 4, 128, 128))

    order = ["norm1_g", "w_in", "b_forget", "w_pool", "pool_scale", "w_out", "norm2_g", "w_gate", "w_up", "w_down", "final_g"]
    big_idx = {"w_in": 0, "w_out": 1, "w_gate": 2, "w_up": 3, "w_down": 4}
    outs = [loss_row[0, 0], dx[None]]
    for kind in range(4):
        for name in order:
            if name in ("w_in", "w_gate", "w_up"):
                outs.append(jnp.transpose(big[big_idx[name]][kind])[None])
            elif name in big_idx:
                outs.append(big[big_idx[name]][kind][None])
            else:
                outs.append(sm[6 * kind + small_names.index(name)].reshape(small_shape[name]))
    return tuple(outs)
```

```python
import functools

import jax
import jax.numpy as jnp
import numpy as np
from jax import lax
from jax.experimental import pallas as pl
from jax.experimental.pallas import tpu as pltpu

f32 = jnp.float32
bf16 = jnp.bfloat16

T = 4096
D = 1024
NSH = 4
IN_W = 2056
IN_S = IN_W // NSH
AW = 512
PAIRS = 4
FF = 2816
FS = FF // NSH
WINDOWS = (2, 4, 8, 16)
HALO = 16
EPS = 1e-6
NEG = -1e30
LR, B1, B2, AEPS, WD, STEP = 0.001, 0.9, 0.999, 1e-08, 0.01, 10
SMALL_ROWS = 552

NT = (((1,), (1,)), ((), ()))
TN = (((0,), (0,)), ((), ()))

MESH = pl.DeviceIdType.MESH


def _cp(*sem):
    return pltpu.CompilerParams(dimension_semantics=sem)


def _full(shape):
    n = len(shape)
    return pl.BlockSpec(shape, lambda *_: (0,) * n)


def _rms_inproj(x, g1, wm, wf, dep):
    tm = 512

    def body(x_ref, g_ref, wm_ref, wf_ref, dep_ref, h_ref, qkv_ref, u_ref, fl_ref):
        xv = x_ref[...]
        r = lax.rsqrt(jnp.mean(xv * xv, axis=-1, keepdims=True) + EPS)
        h = (xv * r * g_ref[...]).astype(bf16)
        h_ref[...] = h
        qkv_ref[...] = lax.dot_general(h, wm_ref[0:3 * AW, :], NT, preferred_element_type=f32).astype(bf16)
        u_ref[...] = lax.dot_general(h, wm_ref[3 * AW:4 * AW, :], NT, preferred_element_type=f32)
        fl_ref[...] = lax.dot_general(h, wf_ref[...], NT, preferred_element_type=f32)

    return pl.pallas_call(
        body, name="rms_inproj", grid=(T // tm,),
        in_specs=[pl.BlockSpec((tm, D), lambda i: (i, 0)), _full((1, D)), _full((4 * AW, D)), _full((128, D)),
                  _full((8, 128))],
        out_specs=[pl.BlockSpec((tm, D), lambda i: (i, 0)), pl.BlockSpec((tm, 3 * AW), lambda i: (i, 0)),
                   pl.BlockSpec((tm, AW), lambda i: (i, 0)), pl.BlockSpec((tm, 128), lambda i: (i, 0))],
        out_shape=[jax.ShapeDtypeStruct((T, D), bf16), jax.ShapeDtypeStruct((T, 3 * AW), bf16),
                   jax.ShapeDtypeStruct((T, AW), f32), jax.ShapeDtypeStruct((T, 128), f32)],
        compiler_params=_cp("parallel"),
    )(x, g1, wm, wf, dep)


CUMSUM_ROWS = 512
FS_CHUNKS = ((0, 256), (256, 512), (512, FS))


def _log_sigmoid(z):
    return jnp.minimum(z, 0.0) - jnp.log(1.0 + jnp.exp(-jnp.abs(z)))


def _fox_cumsum(fl, bfp):
    tb = CUMSUM_ROWS
    nb = T // tb

    def body(fl_ref, b_ref, qa_ref, ka_ref, carry):
        i = pl.program_id(0)

        @pl.when(i == 0)
        def _():
            carry[...] = jnp.zeros_like(carry)

        lf = _log_sigmoid(fl_ref[...] + b_ref[...])
        r = lax.broadcasted_iota(jnp.int32, (tb, tb), 0)
        cc = lax.broadcasted_iota(jnp.int32, (tb, tb), 1)
        ltri = (cc <= r).astype(f32)
        cb = jnp.dot(ltri, lf, precision=lax.Precision.HIGHEST, preferred_element_type=f32) + carry[0:1, :]
        carry[...] = jnp.broadcast_to(cb[tb - 1:tb, :], (8, 128))
        hi = cb.astype(bf16)
        r1 = cb - hi.astype(f32)
        mid = r1.astype(bf16)
        lo = (r1 - mid.astype(f32)).astype(bf16)
        head = lax.broadcasted_iota(jnp.int32, (128, AW), 0)
        col = lax.broadcasted_iota(jnp.int32, (128, AW), 1)
        base = 128 * (head >> 1) + 64 * (1 - (head & 1))
        place = lambda off: jnp.logical_and(col == base + off, head < 8).astype(bf16)
        mm = lambda a, off: jnp.dot(a, place(off), preferred_element_type=f32)
        cq = mm(hi, 0) + mm(mid, 1) + mm(lo, 2)
        ck = mm(hi, 3) + mm(mid, 4) + mm(lo, 5)
        within = jnp.bitwise_and(lax.broadcasted_iota(jnp.int32, (tb, AW), 1), 63)
        qa_ref[...] = jnp.where(jnp.logical_and(within >= 3, within <= 5), 1.0, cq).astype(bf16)
        ka_ref[...] = jnp.where(within <= 2, 1.0, -ck).astype(bf16)

    return pl.pallas_call(
        body, name="fox_cumsum", grid=(nb,),
        in_specs=[pl.BlockSpec((tb, 128), lambda i: (i, 0)), _full((1, 128))],
        out_specs=[pl.BlockSpec((tb, AW), lambda i: (i, 0)), pl.BlockSpec((tb, AW), lambda i: (i, 0))],
        out_shape=[jax.ShapeDtypeStruct((T, AW), bf16), jax.ShapeDtypeStruct((T, AW), bf16)],
        scratch_shapes=[pltpu.VMEM((8, 128), f32)],
        compiler_params=_cp("arbitrary"),
    )(fl, bfp)


ATT_T = 512


def _causal_steps(key_major):
    n = T // ATT_T
    if key_major:
        pairs = [(i, j) for j in range(n) for i in range(j, n)]
    else:
        pairs = [(i, j) for i in range(n) for j in range(i + 1)]
    it = np.array([p[0] for p in pairs], np.int32)
    jt = np.array([p[1] for p in pairs], np.int32)
    return jnp.asarray(it), jnp.asarray(jt)


def _attn_fwd(qkv, qaug, kaug):
    tq = tk = ATT_T
    it, jt = _causal_steps(False)
    nsteps = it.shape[0]

    rs = 64

    def body(it_ref, jt_ref, q_ref, k_ref, v_ref, qa_ref, ka_ref, o_ref, lse_ref, m_sc, acc_sc, s_sc, p_sc, alpha_sc):
        t = pl.program_id(1)
        i = it_ref[t]
        j = jt_ref[t]

        @pl.when(j == 0)
        def _():
            m_sc[...] = jnp.full_like(m_sc, NEG)
            acc_sc[...] = jnp.zeros_like(acc_sc)

        lane = lax.broadcasted_iota(jnp.int32, (tq, 128), 1)
        spare = (64, 0)

        def step(on_diagonal):
            q = q_ref[...] * 0.125
            k = k_ref[...]
            v = v_ref[...]
            qa = qa_ref[...]
            ka = ka_ref[...]
            for e in range(2):
                hm = (lane >= 64) if e else (lane < 64)
                s_sc[...] = lax.dot_general(jnp.where(hm, q, qa), jnp.where(hm, k, ka), NT, preferred_element_type=f32)
                for r in range(0, tq, rs):
                    s = s_sc[r:r + rs, :]
                    if on_diagonal:
                        row = lax.broadcasted_iota(jnp.int32, (rs, tk), 0) + r
                        col = lax.broadcasted_iota(jnp.int32, (rs, tk), 1)
                        s = jnp.where(col <= row, s, NEG)
                    m_prev = m_sc[e, r:r + rs, :]
                    m_new = jnp.maximum(m_prev, jnp.max(s, axis=1, keepdims=True))
                    p_sc[r:r + rs, :] = jnp.exp(s - jnp.tile(m_new, (1, tk // 128))).astype(bf16)
                    alpha_sc[r:r + rs, :] = jnp.exp(m_prev - m_new)
                    m_sc[e, r:r + rs, :] = m_new
                ve = jnp.where(hm, v, (lane == spare[e]).astype(bf16))
                acc_sc[e] = alpha_sc[...] * acc_sc[e] + jnp.dot(p_sc[...], ve, preferred_element_type=f32)

        @pl.when(j < i)
        def _():
            step(False)

        @pl.when(j == i)
        def _():
            step(True)
            l0 = acc_sc[0][:, spare[0]:spare[0] + 1]
            l1 = acc_sc[1][:, spare[1]:spare[1] + 1]
            o_ref[...] = jnp.where(lane < 64, acc_sc[0] / l0, acc_sc[1] / l1).astype(bf16)
            lse_ref[...] = jnp.where(lane < 64, m_sc[0] + jnp.log(l0), m_sc[1] + jnp.log(l1))

    qmap = lambda p, t, it, jt: (it[t], p)
    kmap = lambda p, t, it, jt: (jt[t], p)
    grid_spec = pltpu.PrefetchScalarGridSpec(
        num_scalar_prefetch=2, grid=(PAIRS, nsteps),
        in_specs=[pl.BlockSpec((tq, 128), qmap),
                  pl.BlockSpec((tk, 128), lambda p, t, it, jt: (jt[t], PAIRS + p)),
                  pl.BlockSpec((tk, 128), lambda p, t, it, jt: (jt[t], 2 * PAIRS + p)),
                  pl.BlockSpec((tq, 128), qmap), pl.BlockSpec((tk, 128), kmap)],
        out_specs=[pl.BlockSpec((tq, 128), qmap),
                   pl.BlockSpec((None, tq, 128), lambda p, t, it, jt: (p, it[t], 0))],
        scratch_shapes=[pltpu.VMEM((2, tq, 128), f32), pltpu.VMEM((2, tq, 128), f32), pltpu.VMEM((tq, tk), f32),
                        pltpu.VMEM((tq, tk), bf16), pltpu.VMEM((tq, 128), f32)],
    )
    return pl.pallas_call(
        body, name="fox_attn_fwd", grid_spec=grid_spec,
        out_shape=[jax.ShapeDtypeStruct((T, AW), bf16), jax.ShapeDtypeStruct((PAIRS, T, 128), f32)],
        compiler_params=_cp("parallel", "arbitrary"),
    )(it, jt, qkv, qkv, qkv, qaug, kaug)


def _pool_fwd(u, wp, scale):
    tm = 512

    def body(u_ref, wp_ref, sc_ref, pooled_ref, pool_ref, ext):
        i = pl.program_id(0)

        @pl.when(i == 0)
        def _():
            ext[0:HALO, :] = jnp.zeros((HALO, AW), f32)

        uv = u_ref[...]
        ext[HALO:HALO + tm, :] = uv
        t_idx = i * tm + lax.broadcasted_iota(jnp.int32, (tm, 1), 0)
        for g, w in enumerate(WINDOWS):
            lo, hi = 128 * g, 128 * (g + 1)
            ug = uv[:, lo:hi]
            acc = ug
            for d in range(1, w):
                acc = acc + ext[HALO - d:HALO - d + tm, lo:hi]
            cnt = jnp.minimum(t_idx + 1, w).astype(f32)
            pb = (acc / cnt - ug).astype(bf16)
            pooled_ref[:, lo:hi] = pb
            mixed = jnp.dot(pb, wp_ref[g], preferred_element_type=f32)
            pool_ref[:, lo:hi] = (mixed * sc_ref[:, lo:hi]).astype(bf16)
        ext[0:HALO, :] = uv[tm - HALO:tm, :]

    return pl.pallas_call(
        body, name="pool_fwd", grid=(T // tm,),
        in_specs=[pl.BlockSpec((tm, AW), lambda i: (i, 0)), _full((4, 128, 128)), _full((1, AW))],
        out_specs=[pl.BlockSpec((tm, AW), lambda i: (i, 0)), pl.BlockSpec((tm, AW), lambda i: (i, 0))],
        out_shape=[jax.ShapeDtypeStruct((T, AW), bf16), jax.ShapeDtypeStruct((T, AW), bf16)],
        scratch_shapes=[pltpu.VMEM((tm + HALO, AW), f32)],
        compiler_params=_cp("arbitrary"),
    )(u, wp, scale)


def _outproj(x, attn, pool, wo, g2):
    tm = 512

    def body(x_ref, a_ref, p_ref, wo_ref, g_ref, x1_ref, h2_ref):
        x1 = x_ref[...] + jnp.dot(a_ref[...], wo_ref[0:AW, :], preferred_element_type=f32)
        x1 = x1 + jnp.dot(p_ref[...], wo_ref[AW:2 * AW, :], preferred_element_type=f32)
        x1_ref[...] = x1
        r = lax.rsqrt(jnp.mean(x1 * x1, axis=-1, keepdims=True) + EPS)
        h2_ref[...] = (x1 * r * g_ref[...]).astype(bf16)

    return pl.pallas_call(
        body, name="outproj", grid=(T // tm,),
        in_specs=[pl.BlockSpec((tm, D), lambda i: (i, 0)), pl.BlockSpec((tm, AW), lambda i: (i, 0)),
                  pl.BlockSpec((tm, AW), lambda i: (i, 0)), _full((D, D)), _full((1, D))],
        out_specs=[pl.BlockSpec((tm, D), lambda i: (i, 0)), pl.BlockSpec((tm, D), lambda i: (i, 0))],
        out_shape=[jax.ShapeDtypeStruct((T, D), f32), jax.ShapeDtypeStruct((T, D), bf16)],
        compiler_params=_cp("parallel"),
    )(x, attn, pool, wo, g2)


def _mlp_fwd_loss(h2, x1, wg, wu, wd, tgt, gf):
    tm = 512

    def body(h_ref, x1_ref, wg_ref, wu_ref, wd_ref, t_ref, g_ref,
             loss_ref, dg_ref, dx_ref, dxb_ref, ud_ref, silu_ref, a_ref, x2):
        i = pl.program_id(0)
        s = pl.program_id(1)

        @pl.when(jnp.logical_and(i == 0, s == 0))
        def _():
            loss_ref[...] = jnp.zeros_like(loss_ref)
            dg_ref[...] = jnp.zeros_like(dg_ref)

        h = h_ref[...]
        for c0, c1 in FS_CHUNKS:
            gate = lax.dot_general(h, wg_ref[c0:c1, :], NT, preferred_element_type=f32)
            up = lax.dot_general(h, wu_ref[c0:c1, :], NT, preferred_element_type=f32)
            sg = jax.nn.sigmoid(gate)
            silu = gate * sg
            ud_ref[:, c0:c1] = (up * (sg * (1.0 + gate * (1.0 - sg)))).astype(bf16)
            silu_ref[:, c0:c1] = silu.astype(bf16)
            a_ref[:, c0:c1] = (silu * up).astype(bf16)
        part = jnp.dot(a_ref[...], wd_ref[...], preferred_element_type=f32)

        @pl.when(s == 0)
        def _():
            x2[...] = x1_ref[...] + part

        @pl.when(s > 0)
        def _():
            x2[...] += part

        @pl.when(s == NSH - 1)
        def _():
            xv = x2[...]
            g = g_ref[...]
            r = lax.rsqrt(jnp.mean(xv * xv, axis=-1, keepdims=True) + EPS)
            xhat = xv * r
            e = xhat * g - t_ref[...]
            loss_ref[...] += 0.5 * jnp.sum(jnp.mean(e * e, axis=-1, keepdims=True))
            dy = e * (1.0 / D)
            dg_ref[...] += jnp.sum(dy * xhat, axis=0, keepdims=True)
            z = dy * g
            dx = r * (z - xhat * jnp.mean(z * xhat, axis=-1, keepdims=True))
            dx_ref[...] = dx
            dxb_ref[...] = dx.astype(bf16)

    row = lambda i, s: (i, 0)
    sl = lambda i, s: (s, i, 0)
    wsl = lambda i, s: (s, 0, 0)
    return pl.pallas_call(
        body, name="mlp_fwd_loss", grid=(T // tm, NSH),
        in_specs=[pl.BlockSpec((tm, D), row), pl.BlockSpec((tm, D), row),
                  pl.BlockSpec((None, FS, D), wsl), pl.BlockSpec((None, FS, D), wsl), pl.BlockSpec((None, FS, D), wsl),
                  pl.BlockSpec((tm, D), row), pl.BlockSpec((1, D), lambda i, s: (0, 0))],
        out_specs=[pl.BlockSpec((8, 128), lambda i, s: (0, 0)), pl.BlockSpec((1, D), lambda i, s: (0, 0)),
                   pl.BlockSpec((tm, D), row), pl.BlockSpec((tm, D), row),
                   pl.BlockSpec((None, tm, FS), sl), pl.BlockSpec((None, tm, FS), sl), pl.BlockSpec((None, tm, FS), sl)],
        out_shape=[jax.ShapeDtypeStruct((8, 128), f32), jax.ShapeDtypeStruct((1, D), f32),
                   jax.ShapeDtypeStruct((T, D), f32), jax.ShapeDtypeStruct((T, D), bf16)]
        + [jax.ShapeDtypeStruct((NSH, T, FS), bf16)] * 3,
        scratch_shapes=[pltpu.VMEM((tm, D), f32)],
        compiler_params=_cp("arbitrary", "arbitrary"),
    )(h2, x1, wg, wu, wd, tgt, gf)


def _mlp_bwd(dx2b, dx2, ud, silu, wg, wu, wd, x1, g2):
    tm = 512

    def body(dxb_ref, dx_ref, ud_ref, silu_ref, wg_ref, wu_ref, wd_ref, x1_ref, g_ref,
             dg_ref, du_ref, dx1_ref, dx1b_ref, dn_ref, acc):
        i = pl.program_id(0)
        s = pl.program_id(1)

        @pl.when(jnp.logical_and(i == 0, s == 0))
        def _():
            dn_ref[...] = jnp.zeros_like(dn_ref)

        dxb = dxb_ref[...]
        for c0, c1 in FS_CHUNKS:
            da = lax.dot_general(dxb, wd_ref[c0:c1, :], NT, preferred_element_type=f32)
            dg_ref[:, c0:c1] = (da * ud_ref[:, c0:c1].astype(f32)).astype(bf16)
            du_ref[:, c0:c1] = (da * silu_ref[:, c0:c1].astype(f32)).astype(bf16)
        part = jnp.dot(dg_ref[...], wg_ref[...], preferred_element_type=f32)
        part = part + jnp.dot(du_ref[...], wu_ref[...], preferred_element_type=f32)

        @pl.when(s == 0)
        def _():
            acc[...] = part

        @pl.when(s > 0)
        def _():
            acc[...] += part

        @pl.when(s == NSH - 1)
        def _():
            xv = x1_ref[...]
            r = lax.rsqrt(jnp.mean(xv * xv, axis=-1, keepdims=True) + EPS)
            xhat = xv * r
            dh = acc[...]
            dn_ref[...] += jnp.sum(dh * xhat, axis=0, keepdims=True)
            z = dh * g_ref[...]
            dx1 = dx_ref[...] + r * (z - xhat * jnp.mean(z * xhat, axis=-1, keepdims=True))
            dx1_ref[...] = dx1
            dx1b_ref[...] = dx1.astype(bf16)

    row = lambda i, s: (i, 0)
    sl = lambda i, s: (s, i, 0)
    wsl = lambda i, s: (s, 0, 0)
    return pl.pallas_call(
        body, name="mlp_bwd", grid=(T // tm, NSH),
        in_specs=[pl.BlockSpec((tm, D), row), pl.BlockSpec((tm, D), row),
                  pl.BlockSpec((None, tm, FS), sl), pl.BlockSpec((None, tm, FS), sl),
                  pl.BlockSpec((None, FS, D), wsl), pl.BlockSpec((None, FS, D), wsl), pl.BlockSpec((None, FS, D), wsl),
                  pl.BlockSpec((tm, D), row), pl.BlockSpec((1, D), lambda i, s: (0, 0))],
        out_specs=[pl.BlockSpec((None, tm, FS), sl), pl.BlockSpec((None, tm, FS), sl),
                   pl.BlockSpec((tm, D), row), pl.BlockSpec((tm, D), row), pl.BlockSpec((1, D), lambda i, s: (0, 0))],
        out_shape=[jax.ShapeDtypeStruct((NSH, T, FS), bf16)] * 2
        + [jax.ShapeDtypeStruct((T, D), f32), jax.ShapeDtypeStruct((T, D), bf16), jax.ShapeDtypeStruct((1, D), f32)],
        scratch_shapes=[pltpu.VMEM((tm, D), f32)],
        compiler_params=_cp("arbitrary", "arbitrary"),
    )(dx2b, dx2, ud, silu, wg, wu, wd, x1, g2)


def _mm_tn(a, bs, name, a_sharded=False, b_sharded=False, tk=512, out_dtype=bf16):
    nb = len(bs)
    sh = NSH if (a_sharded or b_sharded) else 1
    m = a.shape[-1]
    nk = T // tk

    def body(a_ref, *refs):
        kk = pl.program_id(1)
        av = a_ref[...]
        for b_ref, o_ref, acc in zip(refs[:nb], refs[nb:2 * nb], refs[2 * nb:]):
            upd = lax.dot_general(av, b_ref[...], TN, preferred_element_type=f32)

            @pl.when(kk == 0)
            def _():
                acc[...] = upd

            @pl.when(kk > 0)
            def _():
                acc[...] += upd

            @pl.when(kk == nk - 1)
            def _():
                o_ref[...] = acc[...].astype(out_dtype)

    a_spec = (pl.BlockSpec((None, tk, m), lambda s, k: (s, k, 0)) if a_sharded
              else pl.BlockSpec((tk, m), lambda s, k: (k, 0)))
    b_specs, o_specs, o_shapes, scratch = [], [], [], []
    for b in bs:
        n = b.shape[-1]
        b_specs.append(pl.BlockSpec((None, tk, n), lambda s, k: (s, k, 0)) if b_sharded
                       else pl.BlockSpec((tk, n), lambda s, k: (k, 0)))
        scratch.append(pltpu.VMEM((m, n), f32))
        if sh > 1:
            o_specs.append(pl.BlockSpec((None, m, n), lambda s, k: (s, 0, 0)))
            o_shapes.append(jax.ShapeDtypeStruct((sh, m, n), out_dtype))
        else:
            o_specs.append(pl.BlockSpec((m, n), lambda s, k: (0, 0)))
            o_shapes.append(jax.ShapeDtypeStruct((m, n), out_dtype))
    return pl.pallas_call(
        body, name=name, grid=(sh, nk), in_specs=[a_spec] + b_specs, out_specs=o_specs, out_shape=o_shapes,
        scratch_shapes=scratch, compiler_params=_cp("arbitrary", "arbitrary"),
    )(a, *bs)


def _mm_tn_rows(a_list, b, name, tk=1024, out_dtype=bf16):
    na = len(a_list)
    n = b.shape[-1]
    nk = T // tk

    def body(*refs):
        a_refs, b_ref = refs[:na], refs[na]
        o_refs, accs = refs[na + 1:2 * na + 1], refs[2 * na + 1:]
        kk = pl.program_id(0)
        bv = b_ref[...]
        for a_ref, o_ref, acc in zip(a_refs, o_refs, accs):
            upd = lax.dot_general(a_ref[...], bv, TN, preferred_element_type=f32)

            @pl.when(kk == 0)
            def _():
                acc[...] = upd

            @pl.when(kk > 0)
            def _():
                acc[...] += upd

            @pl.when(kk == nk - 1)
            def _():
                o_ref[...] = acc[...].astype(out_dtype)

    return pl.pallas_call(
        body, name=name, grid=(nk,),
        in_specs=[pl.BlockSpec((tk, a.shape[-1]), lambda k: (k, 0)) for a in a_list] + [pl.BlockSpec((tk, n), lambda k: (k, 0))],
        out_specs=[pl.BlockSpec((a.shape[-1], n), lambda k: (0, 0)) for a in a_list],
        out_shape=[jax.ShapeDtypeStruct((a.shape[-1], n), out_dtype) for a in a_list],
        scratch_shapes=[pltpu.VMEM((a.shape[-1], n), f32) for a in a_list],
        compiler_params=_cp("arbitrary"),
    )(*a_list, b)


def _outproj_bwd(dx1b, wo):
    tm = 512

    def body(dx_ref, wo_ref, da_ref, dp_ref):
        dx = dx_ref[...]
        da_ref[...] = lax.dot_general(dx, wo_ref[0:AW, :], NT, preferred_element_type=f32).astype(bf16)
        dp_ref[...] = lax.dot_general(dx, wo_ref[AW:2 * AW, :], NT, preferred_element_type=f32)

    return pl.pallas_call(
        body, name="outproj_bwd", grid=(T // tm,),
        in_specs=[pl.BlockSpec((tm, D), lambda i: (i, 0)), _full((D, D))],
        out_specs=[pl.BlockSpec((tm, AW), lambda i: (i, 0)), pl.BlockSpec((tm, AW), lambda i: (i, 0))],
        out_shape=[jax.ShapeDtypeStruct((T, AW), bf16), jax.ShapeDtypeStruct((T, AW), f32)],
        compiler_params=_cp("parallel"),
    )(dx1b, wo)


def _pool_bwd(dpool, pooled, wp, scale, dep):
    tm = 512
    n = T // tm

    def body(dp_ref, pb_ref, wp_ref, sc_ref, dep_ref, du_ref, dsc_ref, dwp_ref, ext):
        i = pl.program_id(0)

        @pl.when(i == 0)
        def _():
            ext[tm:tm + HALO, :] = jnp.zeros((HALO, AW), f32)
            dsc_ref[...] = jnp.zeros_like(dsc_ref)
            dwp_ref[...] = jnp.zeros_like(dwp_ref)

        t_idx = (n - 1 - i) * tm + lax.broadcasted_iota(jnp.int32, (tm, 1), 0)
        for g, w in enumerate(WINDOWS):
            lo, hi = 128 * g, 128 * (g + 1)
            pb = pb_ref[:, lo:hi]
            mixed = jnp.dot(pb, wp_ref[g], preferred_element_type=f32)
            dpo = dp_ref[:, lo:hi]
            dsc_ref[:, lo:hi] += jnp.sum(dpo * mixed, axis=0, keepdims=True)
            dmr = (dpo * sc_ref[:, lo:hi]).astype(bf16)
            dwp_ref[g] += lax.dot_general(pb, dmr, TN, preferred_element_type=f32)
            dpl = lax.dot_general(dmr, wp_ref[g], NT, preferred_element_type=f32)
            cnt = jnp.minimum(t_idx + 1, w).astype(f32)
            dpn = dpl / cnt
            ext[0:tm, lo:hi] = dpn
            acc = dpn
            for d in range(1, w):
                acc = acc + ext[d:d + tm, lo:hi]
            du_ref[:, lo:hi] = (acc - dpl).astype(bf16)
        ext[tm:tm + HALO, :] = ext[0:HALO, :]

    rev = lambda i: (n - 1 - i, 0)
    return pl.pallas_call(
        body, name="pool_bwd", grid=(n,),
        in_specs=[pl.BlockSpec((tm, AW), rev), pl.BlockSpec((tm, AW), rev), _full((4, 128, 128)), _full((1, AW)),
                  _full((8, 128))],
        out_specs=[pl.BlockSpec((tm, AW), rev), _full((1, AW)), _full((4, 128, 128))],
        out_shape=[jax.ShapeDtypeStruct((T, AW), bf16), jax.ShapeDtypeStruct((1, AW), f32),
                   jax.ShapeDtypeStruct((4, 128, 128), f32)],
        scratch_shapes=[pltpu.VMEM((tm + HALO, AW), f32)],
        compiler_params=_cp("arbitrary"),
    )(dpool, pooled, wp, scale, dep)


def _attn_bwd(qkv, qaug, kaug, attn, dattn, lse, dep):
    tq = tk = ATT_T
    n = T // tq
    it, jt = _causal_steps(True)
    nsteps = it.shape[0]

    rs = 64

    def body(it_ref, jt_ref, q_ref, k_ref, v_ref, qa_ref, ka_ref, o_ref, do_ref, lse_ref, dep_ref,
             dq_ref, dqs_ref, dk_ref, dks_ref, dv_ref, dq_acc, dk_acc, dv_acc, s_sc, dp_sc, p_sc, ds_sc):
        t = pl.program_id(1)
        i = it_ref[t]
        j = jt_ref[t]

        @pl.when(t == 0)
        def _():
            dq_acc[...] = jnp.zeros_like(dq_acc)

        @pl.when(i == j)
        def _():
            dk_acc[...] = jnp.zeros_like(dk_acc)
            dv_acc[...] = jnp.zeros_like(dv_acc)

        lane = lax.broadcasted_iota(jnp.int32, (tq, 128), 1)

        def step(on_diagonal):
            q = q_ref[...] * 0.125
            k = k_ref[...]
            v = v_ref[...]
            qa = qa_ref[...]
            ka = ka_ref[...]
            do = do_ref[...]
            dd = do.astype(f32) * o_ref[...].astype(f32)
            r0 = pl.multiple_of(i * tq, tq)
            for e in range(2):
                hm = (lane >= 64) if e else (lane < 64)
                qe = jnp.where(hm, q, qa)
                ke = jnp.where(hm, k, ka)
                doe = jnp.where(hm, do, jnp.zeros_like(do))
                delta = jnp.sum(jnp.where(hm, dd, 0.0), axis=1, keepdims=True)
                s_sc[...] = lax.dot_general(qe, ke, NT, preferred_element_type=f32)
                dp_sc[...] = lax.dot_general(doe, v, NT, preferred_element_type=f32)
                for r in range(0, tq, rs):
                    s = s_sc[r:r + rs, :] - lse_ref[r:r + rs, 64 * e:64 * e + 1]
                    if on_diagonal:
                        row = lax.broadcasted_iota(jnp.int32, (rs, tk), 0) + r
                        col = lax.broadcasted_iota(jnp.int32, (rs, tk), 1)
                        s = jnp.where(col <= row, s, NEG)
                    p = jnp.exp(s)
                    p_sc[r:r + rs, :] = p.astype(bf16)
                    ds_sc[r:r + rs, :] = (p * (dp_sc[r:r + rs, :] - delta[r:r + rs, :])).astype(bf16)
                dv_acc[...] += lax.dot_general(p_sc[...], doe, TN, preferred_element_type=f32)
                dsb = ds_sc[...]
                dk_acc[e] += lax.dot_general(dsb, qe, TN, preferred_element_type=f32)
                dq_acc[e, pl.ds(r0, tq), :] += jnp.dot(dsb, ke, preferred_element_type=f32)

        @pl.when(i > j)
        def _():
            step(False)

        @pl.when(i == j)
        def _():
            step(True)

        @pl.when(i == n - 1)
        def _():
            dk_ref[...] = jnp.where(lane < 64, dk_acc[0], dk_acc[1]).astype(bf16)
            dks_ref[...] = jnp.where(lane < 64, dk_acc[1], dk_acc[0])
            dv_ref[...] = dv_acc[...].astype(bf16)

        @pl.when(t == nsteps - 1)
        def _():
            lane_t = lax.broadcasted_iota(jnp.int32, (T, 128), 1)
            dq_ref[...] = (jnp.where(lane_t < 64, dq_acc[0], dq_acc[1]) * 0.125).astype(bf16)
            dqs_ref[...] = jnp.where(lane_t < 64, dq_acc[1], dq_acc[0])

    qmap = lambda p, t, it, jt: (it[t], p)
    grid_spec = pltpu.PrefetchScalarGridSpec(
        num_scalar_prefetch=2, grid=(PAIRS, nsteps),
        in_specs=[pl.BlockSpec((tq, 128), qmap),
                  pl.BlockSpec((tk, 128), lambda p, t, it, jt: (jt[t], PAIRS + p)),
                  pl.BlockSpec((tk, 128), lambda p, t, it, jt: (jt[t], 2 * PAIRS + p)),
                  pl.BlockSpec((tq, 128), qmap), pl.BlockSpec((tk, 128), lambda p, t, it, jt: (jt[t], p)),
                  pl.BlockSpec((tq, 128), qmap), pl.BlockSpec((tq, 128), qmap),
                  pl.BlockSpec((None, tq, 128), lambda p, t, it, jt: (p, it[t], 0)),
                  pl.BlockSpec((8, 128), lambda p, t, it, jt: (0, 0))],
        out_specs=[pl.BlockSpec((T, 128), lambda p, t, it, jt: (0, p)),
                   pl.BlockSpec((None, T, 128), lambda p, t, it, jt: (p, 0, 0)),
                   pl.BlockSpec((tk, 128), lambda p, t, it, jt: (jt[t], p)),
                   pl.BlockSpec((None, tk, 128), lambda p, t, it, jt: (p, jt[t], 0)),
                   pl.BlockSpec((tk, 128), lambda p, t, it, jt: (jt[t], p))],
        scratch_shapes=[pltpu.VMEM((2, T, 128), f32), pltpu.VMEM((2, tk, 128), f32), pltpu.VMEM((tk, 128), f32),
                        pltpu.VMEM((tq, tk), f32), pltpu.VMEM((tq, tk), f32), pltpu.VMEM((tq, tk), bf16),
                        pltpu.VMEM((tq, tk), bf16)],
    )
    return pl.pallas_call(
        body, name="fox_attn_bwd", grid_spec=grid_spec,
        out_shape=[jax.ShapeDtypeStruct((T, AW), bf16), jax.ShapeDtypeStruct((PAIRS, T, 128), f32),
                   jax.ShapeDtypeStruct((T, AW), bf16), jax.ShapeDtypeStruct((PAIRS, T, 128), f32),
                   jax.ShapeDtypeStruct((T, AW), bf16)],
        compiler_params=_cp("parallel", "arbitrary"),
    )(it, jt, qkv, qkv, qkv, qaug, kaug, attn, dattn, lse, dep)


def _fox_cumsum_bwd(dqs, dks, fl, bfp):
    tb = CUMSUM_ROWS
    nb = T // tb
    hp = lax.Precision.HIGHEST

    def body(dqs_ref, dks_ref, fl_ref, b_ref, df_ref, db_ref, carry):
        i = pl.program_id(0)

        @pl.when(i == 0)
        def _():
            carry[...] = jnp.zeros_like(carry)
            db_ref[...] = jnp.zeros_like(db_ref)

        r = lax.broadcasted_iota(jnp.int32, (128, 128), 0)
        cc = lax.broadcasted_iota(jnp.int32, (128, 128), 1)
        pick = lambda even_lane, odd_lane, p: jnp.logical_or(
            jnp.logical_and(r == even_lane, cc == 2 * p), jnp.logical_and(r == odd_lane, cc == 2 * p + 1)).astype(f32)
        dc = jnp.zeros((tb, 128), f32)
        for p in range(PAIRS):
            dc = dc + jnp.dot(dqs_ref[p], pick(64, 0, p), precision=hp, preferred_element_type=f32)
            dc = dc - jnp.dot(dks_ref[p], pick(67, 3, p), precision=hp, preferred_element_type=f32)
        rt = lax.broadcasted_iota(jnp.int32, (tb, tb), 0)
        ct = lax.broadcasted_iota(jnp.int32, (tb, tb), 1)
        utri = (ct >= rt).astype(f32)
        dl = jnp.dot(utri, dc, precision=hp, preferred_element_type=f32) + carry[0:1, :]
        carry[...] = jnp.broadcast_to(dl[0:1, :], (8, 128))
        z = fl_ref[...] + b_ref[...]
        df = dl * jax.nn.sigmoid(-z)
        df_ref[...] = df.astype(bf16)
        db_ref[...] += jnp.sum(df, axis=0, keepdims=True)

    rev = lambda i: (nb - 1 - i, 0)
    return pl.pallas_call(
        body, name="fox_cumsum_bwd", grid=(nb,),
        in_specs=[pl.BlockSpec((PAIRS, tb, 128), lambda i: (0, nb - 1 - i, 0)),
                  pl.BlockSpec((PAIRS, tb, 128), lambda i: (0, nb - 1 - i, 0)),
                  pl.BlockSpec((tb, 128), rev), _full((1, 128))],
        out_specs=[pl.BlockSpec((tb, 128), rev), _full((1, 128))],
        out_shape=[jax.ShapeDtypeStruct((T, 128), bf16), jax.ShapeDtypeStruct((1, 128), f32)],
        scratch_shapes=[pltpu.VMEM((8, 128), f32)],
        compiler_params=_cp("arbitrary"),
    )(dqs, dks, fl, bfp)


def _inproj_bwd(dq, dk, dv, du, df, wm, wf, x, dx1, g1):
    tm = 512

    def body(dq_ref, dk_ref, dv_ref, du_ref, df_ref, wm_ref, wf_ref, x_ref, dx1_ref, g_ref, dx_ref, dn_ref):
        i = pl.program_id(0)

        @pl.when(i == 0)
        def _():
            dn_ref[...] = jnp.zeros_like(dn_ref)

        dh = jnp.dot(dq_ref[...], wm_ref[0:AW, :], preferred_element_type=f32)
        dh = dh + jnp.dot(dk_ref[...], wm_ref[AW:2 * AW, :], preferred_element_type=f32)
        dh = dh + jnp.dot(dv_ref[...], wm_ref[2 * AW:3 * AW, :], preferred_element_type=f32)
        dh = dh + jnp.dot(du_ref[...], wm_ref[3 * AW:4 * AW, :], preferred_element_type=f32)
        dh = dh + jnp.dot(df_ref[...], wf_ref[...], preferred_element_type=f32)
        xv = x_ref[...]
        r = lax.rsqrt(jnp.mean(xv * xv, axis=-1, keepdims=True) + EPS)
        xhat = xv * r
        dn_ref[...] += jnp.sum(dh * xhat, axis=0, keepdims=True)
        z = dh * g_ref[...]
        dx_ref[...] = dx1_ref[...] + r * (z - xhat * jnp.mean(z * xhat, axis=-1, keepdims=True))

    row = lambda i: (i, 0)
    return pl.pallas_call(
        body, name="inproj_bwd", grid=(T // tm,),
        in_specs=[pl.BlockSpec((tm, AW), row)] * 4 + [pl.BlockSpec((tm, 128), row), _full((4 * AW, D)), _full((128, D)),
                                                       pl.BlockSpec((tm, D), row), pl.BlockSpec((tm, D), row), _full((1, D))],
        out_specs=[pl.BlockSpec((tm, D), row), _full((1, D))],
        out_shape=[jax.ShapeDtypeStruct((T, D), f32), jax.ShapeDtypeStruct((1, D), f32)],
        compiler_params=_cp("arbitrary"),
    )(dq, dk, dv, du, df, wm, wf, x, dx1, g1)


def _adamw_math(w, g, m, v):
    m = B1 * m + (1.0 - B1) * g
    v = B2 * v + (1.0 - B2) * (g * g)
    m_hat = m / (1.0 - B1 ** STEP)
    v_hat = v / (1.0 - B2 ** STEP)
    delta = -LR * (m_hat / (jnp.sqrt(v_hat) + AEPS) + WD * w)
    return delta, m, v


def _adamw_shard(w, m, v, p_mine, p_other, name):
    rows, cols = w.shape
    tr = rows if rows <= IN_S else rows // 2

    def body(w_ref, m_ref, v_ref, a_ref, b_ref, g_ref, d_ref, nm_ref, nv_ref):
        g = a_ref[...].astype(f32) + b_ref[...].astype(f32)
        g_ref[...] = g
        d_ref[...], nm_ref[...], nv_ref[...] = _adamw_math(w_ref[...], g, m_ref[...], v_ref[...])

    spec = pl.BlockSpec((tr, cols), lambda i: (i, 0))
    return pl.pallas_call(
        body, name=name, grid=(rows // tr,), in_specs=[spec] * 5, out_specs=[spec] * 4,
        out_shape=[jax.ShapeDtypeStruct((rows, cols), f32)] * 4, compiler_params=_cp("parallel"),
    )(w, m, v, p_mine, p_other)


SMALL_SLOTS = ((0, 8, 128), (8, 16, 128), (16, 24, 128), (24, 28, 128), (32, 33, 8), (40, 552, 128))
LOSS_ROW = 39


def _adamw_small(ws, ms, vs, parts):
    n = len(ws)

    def body(*refs):
        w_refs, m_refs, v_refs, p_ref = refs[0:n], refs[n:2 * n], refs[2 * n:3 * n], refs[3 * n]
        outs = refs[3 * n + 1:]
        g_all = p_ref[0]
        for k in range(1, 8):
            g_all = g_all + p_ref[k]
        for idx, (r0, r1, lanes) in enumerate(SMALL_SLOTS):
            g = g_all[r0:r1, 0:lanes]
            d, nm, nv = _adamw_math(w_refs[idx][...], g, m_refs[idx][...], v_refs[idx][...])
            outs[idx][...] = g
            outs[n + idx][...] = d
            outs[2 * n + idx][...] = nm
            outs[3 * n + idx][...] = nv
        outs[4 * n][...] = g_all[LOSS_ROW:LOSS_ROW + 1, :]

    shapes = [jax.ShapeDtypeStruct(w.shape, f32) for w in ws]
    res = pl.pallas_call(
        body, name="adamw_small", out_shape=shapes * 4 + [jax.ShapeDtypeStruct((1, 128), f32)],
    )(*ws, *ms, *vs, parts)
    return res[:4 * n], res[4 * n]


def _sum4(recv, g, mine, name):
    _, rows, cols = recv.shape
    tr = rows if rows <= IN_S else rows // 2

    def body(mine_ref, r_ref, g_ref, o_ref):
        o_ref[...] = ((g_ref[...].astype(f32) + r_ref[0].astype(f32))
                      + (r_ref[1].astype(f32) + r_ref[2].astype(f32))).astype(bf16)

    grid_spec = pltpu.PrefetchScalarGridSpec(
        num_scalar_prefetch=1, grid=(rows // tr,),
        in_specs=[pl.BlockSpec((3, tr, cols), lambda i, m: (0, i, 0)),
                  pl.BlockSpec((None, tr, cols), lambda i, m: (m[0], i, 0))],
        out_specs=pl.BlockSpec((tr, cols), lambda i, m: (i, 0)))
    return pl.pallas_call(
        body, name=name, grid_spec=grid_spec, out_shape=jax.ShapeDtypeStruct((rows, cols), bf16),
        compiler_params=_cp("arbitrary"),
    )(mine, recv, g)


_HBM = pl.BlockSpec(memory_space=pltpu.HBM)
_SEM = pl.BlockSpec(memory_space=pltpu.SEMAPHORE)
_EFFECT = pltpu.SideEffectType.DATAFLOW_SIDE_EFFECTING


def _in_hbm(a):
    return pltpu.with_memory_space_constraint(a, pltpu.HBM)


def _mesh_pos():
    return lax.axis_index("x"), lax.axis_index("y"), lax.axis_index("c")


def _other_chips(x, y):
    return [(1 - x, y), (x, 1 - y), (1 - x, 1 - y)]


def _gather_copy(srcs, lands, send_sems, recv_sems, a, k, slot):
    x, y, c = _mesh_pos()
    cx, cy = _other_chips(x, y)[k]
    return pltpu.make_async_remote_copy(
        src_ref=srcs[a], dst_ref=lands[a].at[slot], send_sem=send_sems.at[3 * a + k], recv_sem=recv_sems.at[3 * a + k],
        device_id=(cx, cy, c), device_id_type=MESH)


def _scatter_copy(srcs, lands, send_sems, recv_sems, a, k):
    x, y, c = _mesh_pos()
    cx, cy = _other_chips(x, y)[k]
    return pltpu.make_async_remote_copy(
        src_ref=srcs[a].at[2 * cx + cy], dst_ref=lands[a].at[k], send_sem=send_sems.at[3 * a + k],
        recv_sem=recv_sems.at[3 * a + k], device_id=(cx, cy, c), device_id_type=MESH)


def _all_gather_w_in(part):
    cols = part.shape[1] // 2

    def body(src, dst, send_sems, recv_sems, loc_sem):
        x, y, c = _mesh_pos()
        mine = 2 * x + y
        chips = _other_chips(x, y)
        half = lambda ref, cc: ref.at[:, pl.ds(pl.multiple_of(cc * cols, cols), cols)]

        def over_ici(k, slot):
            cx, cy = chips[k]
            return pltpu.make_async_remote_copy(
                src_ref=half(src, c), dst_ref=half(dst.at[slot], c), send_sem=send_sems.at[k], recv_sem=recv_sems.at[k],
                device_id=(cx, cy, c), device_id_type=MESH)

        def to_sibling(k, cc):
            slot = 2 * chips[k][0] + chips[k][1]
            return pltpu.make_async_remote_copy(
                src_ref=half(dst.at[slot], cc), dst_ref=half(dst.at[slot], cc), send_sem=send_sems.at[3 + k],
                recv_sem=recv_sems.at[3 + k], device_id=(x, y, 1 - c), device_id_type=MESH)

        local = pltpu.make_async_copy(src, dst.at[mine], loc_sem.at[0])
        local.start()
        first = [over_ici(k, mine) for k in range(3)]
        for cp in first:
            cp.start()
        passed = [to_sibling(k, c) for k in range(3)]
        for k in range(3):
            over_ici(k, 2 * chips[k][0] + chips[k][1]).wait_recv()
            passed[k].start()
        for k in range(3):
            to_sibling(k, 1 - c).wait_recv()
        for cp in first + passed:
            cp.wait_send()
        local.wait()

    return pl.pallas_call(
        body, name="all_gather_w_in", in_specs=[_HBM], out_specs=_HBM,
        out_shape=jax.ShapeDtypeStruct((NSH,) + part.shape, part.dtype),
        scratch_shapes=[pltpu.SemaphoreType.DMA((6,)), pltpu.SemaphoreType.DMA((6,)), pltpu.SemaphoreType.DMA((1,))],
    )(part)


def _split_start(name, srcs, lands, n_sems, plan, dep):
    n, nl = len(srcs), len(lands)

    def body(*refs):
        src_refs, land_refs = refs[:n], refs[n:n + nl]
        send_sems, recv_sems = refs[n + nl + 1], refs[n + nl + 2]
        token = refs[-1]
        sends, _ = plan(src_refs, land_refs, send_sems, recv_sems)
        for cp in sends:
            cp.start()
        token[...] = jnp.zeros_like(token)

    outs = pl.pallas_call(
        body, name=name,
        in_specs=[_HBM] * (n + nl) + [pl.BlockSpec(memory_space=pl.ANY)],
        out_specs=[_SEM, _SEM] + [_HBM] * (n + nl) + [pl.BlockSpec(memory_space=pltpu.VMEM)],
        out_shape=[pltpu.SemaphoreType.DMA((n_sems,)), pltpu.SemaphoreType.DMA((n_sems,))]
        + [pltpu.HBM(a.shape, a.dtype) for a in list(srcs) + list(lands)] + [jax.ShapeDtypeStruct((8, 128), f32)],
        input_output_aliases={i: 2 + i for i in range(n + nl)},
        compiler_params=pltpu.CompilerParams(has_side_effects=_EFFECT),
    )(*[_in_hbm(a) for a in list(srcs) + list(lands)], dep)
    return outs[0], outs[1], list(outs[2:2 + n]), list(outs[2 + n:2 + n + nl]), outs[-1]


def _split_wait(name, send_sems, recv_sems, srcs, lands, after, plan):
    n, nl = len(srcs), len(lands)

    def body(*refs):
        src_refs, land_refs = refs[:n], refs[n:n + nl]
        s_sems, r_sems = refs[n + nl], refs[n + nl + 1]
        sends, recvs = plan(src_refs, land_refs, s_sems, r_sems)
        for cp in recvs:
            cp.wait_recv()
        for cp in sends:
            cp.wait_send()

    outs = pl.pallas_call(
        body, name=name,
        in_specs=[_HBM] * (n + nl) + [_SEM, _SEM, pl.BlockSpec(memory_space=pl.ANY)],
        out_specs=[_HBM] * (n + nl),
        out_shape=[pltpu.HBM(a.shape, a.dtype) for a in list(srcs) + list(lands)],
        input_output_aliases={i: i for i in range(n + nl)},
        compiler_params=pltpu.CompilerParams(has_side_effects=_EFFECT),
    )(*srcs, *lands, send_sems, recv_sems, after)
    return list(outs[:n]), list(outs[n:])


def _gather_plan(srcs, lands, ss, rs):
    x, y, _ = _mesh_pos()
    chips = _other_chips(x, y)
    sends = [_gather_copy(srcs, lands, ss, rs, a, k, 2 * x + y) for a in range(len(srcs)) for k in range(3)]
    recvs = [_gather_copy(srcs, lands, ss, rs, a, k, 2 * chips[k][0] + chips[k][1])
             for a in range(len(srcs)) for k in range(3)]
    return sends, recvs


def _scatter_plan(srcs, lands, ss, rs):
    cps = [_scatter_copy(srcs, lands, ss, rs, a, k) for a in range(len(srcs)) for k in range(3)]
    return cps, cps


def _tail_plan(srcs, lands, ss, rs):
    x, y, c = _mesh_pos()
    me = 4 * x + 2 * y + c
    cps = [_scatter_copy(srcs[:1], lands[:1], ss, rs, 0, k) for k in range(3)]
    for f in range(1, 8):
        peer = ((x + (f >> 2)) % 2, (y + ((f >> 1) & 1)) % 2, (c + (f & 1)) % 2)
        cps.append(pltpu.make_async_remote_copy(
            src_ref=srcs[1], dst_ref=lands[1].at[me], send_sem=ss.at[2 + f], recv_sem=rs.at[2 + f],
            device_id=peer, device_id_type=MESH))
    return cps, cps


def _swap_with_sibling(parts, name):
    n = len(parts)

    def body(*refs):
        srcs, dsts = refs[:n], refs[n:2 * n]
        send_sems, recv_sems = refs[2 * n:]
        x, y, c = _mesh_pos()
        cps = [pltpu.make_async_remote_copy(src_ref=srcs[a], dst_ref=dsts[a], send_sem=send_sems.at[a],
                                            recv_sem=recv_sems.at[a], device_id=(x, y, 1 - c), device_id_type=MESH)
               for a in range(n)]
        for cp in cps:
            cp.start()
        for cp in cps:
            cp.wait_recv()
        for cp in cps:
            cp.wait_send()

    return pl.pallas_call(
        body, name=name, in_specs=[_HBM] * n, out_specs=[_HBM] * n,
        out_shape=[jax.ShapeDtypeStruct(p.shape, p.dtype) for p in parts],
        scratch_shapes=[pltpu.SemaphoreType.DMA((n,)), pltpu.SemaphoreType.DMA((n,))],
    )(*parts)


def _forward(x, tgt, wm, wf, mlp_w_fn, g1, bfp, wp, scale, g2, gf, dep):
    h, qkv, u, fl = _rms_inproj(x, g1, wm, wf, dep)
    qaug, kaug = _fox_cumsum(fl, bfp)
    attn, lse = _attn_fwd(qkv, qaug, kaug)
    pooled, pool = _pool_fwd(u, wp, scale)
    wo, wgt, wut, wd = mlp_w_fn(attn)
    x1, h2 = _outproj(x, attn, pool, wo, g2)
    loss, dgf, dx2, dx2b, ud, silu, a_b = _mlp_fwd_loss(h2, x1, wgt, wut, wd, tgt, gf)
    saved = dict(h=h, qkv=qkv, fl=fl, qaug=qaug, kaug=kaug, attn=attn, lse=lse, pooled=pooled, pool=pool, x1=x1, h2=h2,
                 ud=ud, silu=silu, a_b=a_b, wo=wo, wgt=wgt, wut=wut, wd=wd)
    return loss, dgf, dx2, dx2b, saved


def _backward_mlp(sv, dx2, dx2b, g2):
    dgate, dup, dx1, dx1b, dg2 = _mlp_bwd(dx2b, dx2, sv["ud"], sv["silu"], sv["wgt"], sv["wut"], sv["wd"], sv["x1"], g2)
    (dwd,) = _mm_tn(sv["a_b"], [dx2b], "dw_down", a_sharded=True, tk=T)
    (dwgt,) = _mm_tn(dgate, [sv["h2"]], "dw_gate", a_sharded=True, tk=T)
    (dwut,) = _mm_tn(dup, [sv["h2"]], "dw_up", a_sharded=True, tk=T)
    return dx1, dx1b, dg2, (dwgt, dwut, dwd)


def _backward_outproj(sv, dx1b):
    dattn, dpool = _outproj_bwd(dx1b, sv["wo"])
    dwo_a, = _mm_tn(sv["attn"], [dx1b], "dw_out_attn", tk=2048)
    dwo_p, = _mm_tn(sv["pool"], [dx1b], "dw_out_pool", tk=2048)
    dwo = jnp.concatenate([dwo_a, dwo_p], axis=0).reshape(NSH, D // NSH, D)
    return dattn, dpool, dwo


def _backward_mixer(sv, x, dx1, dattn, dpool, wm, wf, g1, bfp, wp, scale, dep):
    du, dscale, dwp = _pool_bwd(dpool, sv["pooled"], wp, scale, dep)
    dq, dqs, dk, dks, dv = _attn_bwd(sv["qkv"], sv["qaug"], sv["kaug"], sv["attn"], dattn, sv["lse"], dep)
    df, dbf = _fox_cumsum_bwd(dqs, dks, sv["fl"], bfp)
    dx, dg1 = _inproj_bwd(dq, dk, dv, du, df, wm, wf, x, dx1, g1)
    dwq, dwk, dwv, dwu_in, dwf = _mm_tn_rows([dq, dk, dv, du, df], sv["h"], "dw_in")
    dwin = jnp.concatenate([dwq, dwk, dwv, dwf[0:8], dwu_in], axis=0)
    return dx, dg1, dscale, dwp, dbf, dwin.reshape(NSH, IN_S, D)


def kernel(x, norm1_g, w_in, b_forget, w_pool, pool_scale, w_out, norm2_g, w_gate, w_up, w_down, final_g, loss_target, m_norm1_g, m_w_in, m_b_forget, m_w_pool, m_pool_scale, m_w_out, m_norm2_g, m_w_gate, m_w_up, m_w_down, m_final_g, v_norm1_g, v_w_in, v_b_forget, v_w_pool, v_pool_scale, v_w_out, v_norm2_g, v_w_gate, v_w_up, v_w_down, v_final_g):
    mine = (2 * lax.axis_index("x") + lax.axis_index("y")).astype(jnp.int32)
    mine1 = mine.reshape(1)
    tr = lambda a: jnp.transpose(a[0])

    win4 = _all_gather_w_in(tr(w_in).astype(bf16))
    later = [w_out[0].astype(bf16), tr(w_gate).astype(bf16), tr(w_up).astype(bf16), w_down[0].astype(bf16)]
    lands = [lax.dynamic_update_slice(lax.empty((NSH,) + p.shape, bf16), p[None], (mine, 0, 0)) for p in later]
    ag_send, ag_recv, later_thru, lands_thru, ag_token = _split_start("all_gather_start", later, lands, 12, _gather_plan,
                                                                      win4)
    win = win4.reshape(IN_W, D)
    wm = jnp.concatenate([win[0:3 * AW], win[3 * AW + 8:]], axis=0)
    wf = jnp.pad(win[3 * AW:3 * AW + 8], ((0, 120), (0, 0)))
    bfp = jnp.pad(b_forget, ((0, 0), (0, 120)))
    wp = w_pool[0].astype(bf16)
    gf = final_g.reshape(1, D)

    def later_weights(after):
        _, (wo4, wgt, wut, wd) = _split_wait("all_gather_wait", ag_send, ag_recv, later_thru, lands_thru, after, _gather_plan)
        return wo4.reshape(D, D), wgt, wut, wd

    xe, tgt = x[0], loss_target[0]
    loss_v, dgf, dx2, dx2b, sv = _forward(xe, tgt, wm, wf, later_weights, norm1_g, bfp, wp, pool_scale, norm2_g, gf, ag_token)
    dx1, dx1b, dg2, mlp_grads = _backward_mlp(sv, dx2, dx2b, norm2_g)
    dattn, dpool, dwo = _backward_outproj(sv, dx1b)
    first = [dwo] + list(mlp_grads)
    first_lands = [lax.empty((3,) + g.shape[1:], bf16) for g in first]
    rs_send, rs_recv, first_thru, first_lands_thru, rs_token = _split_start("reduce_scatter_start", first, first_lands, 12,
                                                                            _scatter_plan, dattn)
    dx, dg1, dscale, dwp, dbf, dwin = _backward_mixer(sv, xe, dx1, dattn, dpool, wm, wf, norm1_g, bfp, wp, pool_scale, rs_token)

    me = (4 * lax.axis_index("x") + 2 * lax.axis_index("y") + lax.axis_index("c")).astype(jnp.int32)
    pad8 = lambda r: jnp.pad(r, ((0, 8 - r.shape[0]), (0, 0)))
    loss_rows = jnp.concatenate([dbf, jnp.zeros((6, 128), f32), loss_v[0:1, :]], axis=0)
    small = jnp.concatenate([dg1.reshape(8, 128), dg2.reshape(8, 128), dgf.reshape(8, 128), pad8(dscale.reshape(4, 128)),
                             loss_rows, dwp.reshape(512, 128)], axis=0)
    small_land = lax.dynamic_update_slice(lax.empty((8, SMALL_ROWS, 128), f32), small[None], (me, 0, 0))
    tail_send, tail_recv, tail_thru, tail_lands_thru, tail_token = _split_start(
        "tail_start", [dwin, small], [lax.empty((3,) + dwin.shape[1:], bf16), small_land], 10, _tail_plan, dx)
    first_thru, first_recv = _split_wait("reduce_scatter_wait", rs_send, rs_recv, first_thru, first_lands_thru, tail_token,
                                         _scatter_plan)
    ws = [tr(w_in), w_out[0], tr(w_gate), tr(w_up), w_down[0]]
    ms = [tr(m_w_in), m_w_out[0], tr(m_w_gate), tr(m_w_up), m_w_down[0]]
    vs = [tr(v_w_in), v_w_out[0], tr(v_w_gate), tr(v_w_up), v_w_down[0]]
    partial = [_sum4(r, g, mine1, f"sum4_{i + 1}") for i, (r, g) in enumerate(zip(first_recv, first_thru))]
    other = _swap_with_sibling(partial, "swap_first")
    big = [_adamw_shard(ws[i + 1], ms[i + 1], vs[i + 1], partial[i], other[i], f"adamw_{i + 1}") for i in range(4)]
    (dwin_thru, _), (in_recv_land, small_all) = _split_wait("tail_wait", tail_send, tail_recv, tail_thru, tail_lands_thru,
                                                            big[3][0], _tail_plan)
    partial_in = _sum4(in_recv_land, dwin_thru, mine1, "sum4_0")
    (other_in,) = _swap_with_sibling([partial_in], "swap_in")
    big = [_adamw_shard(ws[0], ms[0], vs[0], partial_in, other_in, "adamw_0")] + big

    small_names = ["norm1_g", "norm2_g", "final_g", "pool_scale", "b_forget", "w_pool"]
    rows = lambda a, b, c, d, e, f: [a.reshape(8, 128), b.reshape(8, 128), c.reshape(8, 128), d.reshape(4, 128),
                                     e.reshape(1, 8), f.reshape(512, 128)]
    sm, loss_row = _adamw_small(rows(norm1_g, norm2_g, final_g, pool_scale, b_forget, w_pool),
                                rows(m_norm1_g, m_norm2_g, m_final_g, m_pool_scale, m_b_forget, m_w_pool),
                                rows(v_norm1_g, v_norm2_g, v_final_g, v_pool_scale, v_b_forget, v_w_pool), small_all)
    small_shape = dict(norm1_g=(1, D), norm2_g=(1, D), final_g=(D,), pool_scale=(1, AW), b_forget=(1, 8),
                       w_pool=(1, 4, 128, 128))

    order = ["norm1_g", "w_in", "b_forget", "w_pool", "pool_scale", "w_out", "norm2_g", "w_gate", "w_up", "w_down", "final_g"]
    big_idx = {"w_in": 0, "w_out": 1, "w_gate": 2, "w_up": 3, "w_down": 4}
    outs = [loss_row[0, 0], dx[None]]
    for kind in range(4):
        for name in order:
            if name in ("w_in", "w_gate", "w_up"):
                outs.append(jnp.transpose(big[big_idx[name]][kind])[None])
            elif name in big_idx:
                outs.append(big[big_idx[name]][kind][None])
            else:
                outs.append(sm[6 * kind + small_names.index(name)].reshape(small_shape[name]))
    return tuple(outs)
```

```python
import functools

import jax
import jax.numpy as jnp
import numpy as np
from jax import lax
from jax.experimental import pallas as pl
from jax.experimental.pallas import tpu as pltpu

f32 = jnp.float32
bf16 = jnp.bfloat16

T = 4096
D = 1024
NSH = 4
IN_W = 2056
IN_S = IN_W // NSH
AW = 512
PAIRS = 4
FF = 2816
FS = FF // NSH
WINDOWS = (2, 4, 8, 16)
HALO = 16
EPS = 1e-6
NEG = -1e30
LR, B1, B2, AEPS, WD, STEP = 0.001, 0.9, 0.999, 1e-08, 0.01, 10
SMALL_ROWS = 552

NT = (((1,), (1,)), ((), ()))
TN = (((0,), (0,)), ((), ()))

MESH = pl.DeviceIdType.MESH


def _cp(*sem):
    return pltpu.CompilerParams(dimension_semantics=sem)


def _full(shape):
    n = len(shape)
    return pl.BlockSpec(shape, lambda *_: (0,) * n)


def _rms_inproj(x, g1, wm, wf, dep):
    tm = 512

    def body(x_ref, g_ref, wm_ref, wf_ref, dep_ref, h_ref, qkv_ref, u_ref, fl_ref):
        xv = x_ref[...]
        r = lax.rsqrt(jnp.mean(xv * xv, axis=-1, keepdims=True) + EPS)
        h = (xv * r * g_ref[...]).astype(bf16)
        h_ref[...] = h
        qkv_ref[...] = lax.dot_general(h, wm_ref[0:3 * AW, :], NT, preferred_element_type=f32).astype(bf16)
        u_ref[...] = lax.dot_general(h, wm_ref[3 * AW:4 * AW, :], NT, preferred_element_type=f32)
        fl_ref[...] = lax.dot_general(h, wf_ref[...], NT, preferred_element_type=f32)

    return pl.pallas_call(
        body, name="rms_inproj", grid=(T // tm,),
        in_specs=[pl.BlockSpec((tm, D), lambda i: (i, 0)), _full((1, D)), _full((4 * AW, D)), _full((128, D)),
                  _full((8, 128))],
        out_specs=[pl.BlockSpec((tm, D), lambda i: (i, 0)), pl.BlockSpec((tm, 3 * AW), lambda i: (i, 0)),
                   pl.BlockSpec((tm, AW), lambda i: (i, 0)), pl.BlockSpec((tm, 128), lambda i: (i, 0))],
        out_shape=[jax.ShapeDtypeStruct((T, D), bf16), jax.ShapeDtypeStruct((T, 3 * AW), bf16),
                   jax.ShapeDtypeStruct((T, AW), f32), jax.ShapeDtypeStruct((T, 128), f32)],
        compiler_params=_cp("parallel"),
    )(x, g1, wm, wf, dep)


CUMSUM_ROWS = 512
FS_CHUNKS = ((0, 256), (256, 512), (512, FS))


def _log_sigmoid(z):
    return jnp.minimum(z, 0.0) - jnp.log(1.0 + jnp.exp(-jnp.abs(z)))


def _fox_cumsum(fl, bfp):
    tb = CUMSUM_ROWS
    nb = T // tb

    def body(fl_ref, b_ref, qa_ref, ka_ref, carry):
        i = pl.program_id(0)

        @pl.when(i == 0)
        def _():
            carry[...] = jnp.zeros_like(carry)

        lf = _log_sigmoid(fl_ref[...] + b_ref[...])
        r = lax.broadcasted_iota(jnp.int32, (tb, tb), 0)
        cc = lax.broadcasted_iota(jnp.int32, (tb, tb), 1)
        ltri = (cc <= r).astype(f32)
        cb = jnp.dot(ltri, lf, precision=lax.Precision.HIGHEST, preferred_element_type=f32) + carry[0:1, :]
        carry[...] = jnp.broadcast_to(cb[tb - 1:tb, :], (8, 128))
        hi = cb.astype(bf16)
        r1 = cb - hi.astype(f32)
        mid = r1.astype(bf16)
        lo = (r1 - mid.astype(f32)).astype(bf16)
        head = lax.broadcasted_iota(jnp.int32, (128, AW), 0)
        col = lax.broadcasted_iota(jnp.int32, (128, AW), 1)
        base = 128 * (head >> 1) + 64 * (1 - (head & 1))
        place = lambda off: jnp.logical_and(col == base + off, head < 8).astype(bf16)
        mm = lambda a, off: jnp.dot(a, place(off), preferred_element_type=f32)
        cq = mm(hi, 0) + mm(mid, 1) + mm(lo, 2)
        ck = mm(hi, 3) + mm(mid, 4) + mm(lo, 5)
        within = jnp.bitwise_and(lax.broadcasted_iota(jnp.int32, (tb, AW), 1), 63)
        qa_ref[...] = jnp.where(jnp.logical_and(within >= 3, within <= 5), 1.0, cq).astype(bf16)
        ka_ref[...] = jnp.where(within <= 2, 1.0, -ck).astype(bf16)

    return pl.pallas_call(
        body, name="fox_cumsum", grid=(nb,),
        in_specs=[pl.BlockSpec((tb, 128), lambda i: (i, 0)), _full((1, 128))],
        out_specs=[pl.BlockSpec((tb, AW), lambda i: (i, 0)), pl.BlockSpec((tb, AW), lambda i: (i, 0))],
        out_shape=[jax.ShapeDtypeStruct((T, AW), bf16), jax.ShapeDtypeStruct((T, AW), bf16)],
        scratch_shapes=[pltpu.VMEM((8, 128), f32)],
        compiler_params=_cp("arbitrary"),
    )(fl, bfp)


ATT_T = 512


def _causal_steps(key_major):
    n = T // ATT_T
    if key_major:
        pairs = [(i, j) for j in range(n) for i in range(j, n)]
    else:
        pairs = [(i, j) for i in range(n) for j in range(i + 1)]
    it = np.array([p[0] for p in pairs], np.int32)
    jt = np.array([p[1] for p in pairs], np.int32)
    return jnp.asarray(it), jnp.asarray(jt)


def _attn_fwd(qkv, qaug, kaug):
    tq = tk = ATT_T
    it, jt = _causal_steps(False)
    nsteps = it.shape[0]

    rs = 64

    def body(it_ref, jt_ref, q_ref, k_ref, v_ref, qa_ref, ka_ref, o_ref, lse_ref, m_sc, acc_sc, s_sc, p_sc, alpha_sc):
        t = pl.program_id(1)
        i = it_ref[t]
        j = jt_ref[t]

        @pl.when(j == 0)
        def _():
            m_sc[...] = jnp.full_like(m_sc, NEG)
            acc_sc[...] = jnp.zeros_like(acc_sc)

        lane = lax.broadcasted_iota(jnp.int32, (tq, 128), 1)
        spare = (64, 0)

        def step(on_diagonal):
            q = q_ref[...] * 0.125
            k = k_ref[...]
            v = v_ref[...]
            qa = qa_ref[...]
            ka = ka_ref[...]
            for e in range(2):
                hm = (lane >= 64) if e else (lane < 64)
                s_sc[e] = lax.dot_general(jnp.where(hm, q, qa), jnp.where(hm, k, ka), NT, preferred_element_type=f32)
            for e in range(2):
                hm = (lane >= 64) if e else (lane < 64)
                for r in range(0, tq, rs):
                    s = s_sc[e, r:r + rs, :]
                    if on_diagonal:
                        row = lax.broadcasted_iota(jnp.int32, (rs, tk), 0) + r
                        col = lax.broadcasted_iota(jnp.int32, (rs, tk), 1)
                        s = jnp.where(col <= row, s, NEG)
                    m_prev = m_sc[e, r:r + rs, :]
                    m_new = jnp.maximum(m_prev, jnp.max(s, axis=1, keepdims=True))
                    p_sc[e, r:r + rs, :] = jnp.exp(s - jnp.tile(m_new, (1, tk // 128))).astype(bf16)
                    alpha_sc[e, r:r + rs, :] = jnp.exp(m_prev - m_new)
                    m_sc[e, r:r + rs, :] = m_new
            for e in range(2):
                hm = (lane >= 64) if e else (lane < 64)
                ve = jnp.where(hm, v, (lane == spare[e]).astype(bf16))
                acc_sc[e] = alpha_sc[e] * acc_sc[e] + jnp.dot(p_sc[e], ve, preferred_element_type=f32)

        @pl.when(j < i)
        def _():
            step(False)

        @pl.when(j == i)
        def _():
            step(True)
            l0 = acc_sc[0][:, spare[0]:spare[0] + 1]
            l1 = acc_sc[1][:, spare[1]:spare[1] + 1]
            o_ref[...] = jnp.where(lane < 64, acc_sc[0] / l0, acc_sc[1] / l1).astype(bf16)
            lse_ref[...] = jnp.where(lane < 64, m_sc[0] + jnp.log(l0), m_sc[1] + jnp.log(l1))

    qmap = lambda p, t, it, jt: (it[t], p)
    kmap = lambda p, t, it, jt: (jt[t], p)
    grid_spec = pltpu.PrefetchScalarGridSpec(
        num_scalar_prefetch=2, grid=(PAIRS, nsteps),
        in_specs=[pl.BlockSpec((tq, 128), qmap),
                  pl.BlockSpec((tk, 128), lambda p, t, it, jt: (jt[t], PAIRS + p)),
                  pl.BlockSpec((tk, 128), lambda p, t, it, jt: (jt[t], 2 * PAIRS + p)),
                  pl.BlockSpec((tq, 128), qmap), pl.BlockSpec((tk, 128), kmap)],
        out_specs=[pl.BlockSpec((tq, 128), qmap),
                   pl.BlockSpec((None, tq, 128), lambda p, t, it, jt: (p, it[t], 0))],
        scratch_shapes=[pltpu.VMEM((2, tq, 128), f32), pltpu.VMEM((2, tq, 128), f32), pltpu.VMEM((2, tq, tk), f32),
                        pltpu.VMEM((2, tq, tk), bf16), pltpu.VMEM((2, tq, 128), f32)],
    )
    return pl.pallas_call(
        body, name="fox_attn_fwd", grid_spec=grid_spec,
        out_shape=[jax.ShapeDtypeStruct((T, AW), bf16), jax.ShapeDtypeStruct((PAIRS, T, 128), f32)],
        compiler_params=_cp("parallel", "arbitrary"),
    )(it, jt, qkv, qkv, qkv, qaug, kaug)


def _pool_fwd(u, wp, scale):
    tm = 512

    def body(u_ref, wp_ref, sc_ref, pooled_ref, pool_ref, ext):
        i = pl.program_id(0)

        @pl.when(i == 0)
        def _():
            ext[0:HALO, :] = jnp.zeros((HALO, AW), f32)

        uv = u_ref[...]
        ext[HALO:HALO + tm, :] = uv
        t_idx = i * tm + lax.broadcasted_iota(jnp.int32, (tm, 1), 0)
        for g, w in enumerate(WINDOWS):
            lo, hi = 128 * g, 128 * (g + 1)
            ug = uv[:, lo:hi]
            acc = ug
            for d in range(1, w):
                acc = acc + ext[HALO - d:HALO - d + tm, lo:hi]
            cnt = jnp.minimum(t_idx + 1, w).astype(f32)
            pb = (acc / cnt - ug).astype(bf16)
            pooled_ref[:, lo:hi] = pb
            mixed = jnp.dot(pb, wp_ref[g], preferred_element_type=f32)
            pool_ref[:, lo:hi] = (mixed * sc_ref[:, lo:hi]).astype(bf16)
        ext[0:HALO, :] = uv[tm - HALO:tm, :]

    return pl.pallas_call(
        body, name="pool_fwd", grid=(T // tm,),
        in_specs=[pl.BlockSpec((tm, AW), lambda i: (i, 0)), _full((4, 128, 128)), _full((1, AW))],
        out_specs=[pl.BlockSpec((tm, AW), lambda i: (i, 0)), pl.BlockSpec((tm, AW), lambda i: (i, 0))],
        out_shape=[jax.ShapeDtypeStruct((T, AW), bf16), jax.ShapeDtypeStruct((T, AW), bf16)],
        scratch_shapes=[pltpu.VMEM((tm + HALO, AW), f32)],
        compiler_params=_cp("arbitrary"),
    )(u, wp, scale)


def _outproj(x, attn, pool, wo, g2):
    tm = 512

    def body(x_ref, a_ref, p_ref, wo_ref, g_ref, x1_ref, h2_ref):
        x1 = x_ref[...] + jnp.dot(a_ref[...], wo_ref[0:AW, :], preferred_element_type=f32)
        x1 = x1 + jnp.dot(p_ref[...], wo_ref[AW:2 * AW, :], preferred_element_type=f32)
        x1_ref[...] = x1
        r = lax.rsqrt(jnp.mean(x1 * x1, axis=-1, keepdims=True) + EPS)
        h2_ref[...] = (x1 * r * g_ref[...]).astype(bf16)

    return pl.pallas_call(
        body, name="outproj", grid=(T // tm,),
        in_specs=[pl.BlockSpec((tm, D), lambda i: (i, 0)), pl.BlockSpec((tm, AW), lambda i: (i, 0)),
                  pl.BlockSpec((tm, AW), lambda i: (i, 0)), _full((D, D)), _full((1, D))],
        out_specs=[pl.BlockSpec((tm, D), lambda i: (i, 0)), pl.BlockSpec((tm, D), lambda i: (i, 0))],
        out_shape=[jax.ShapeDtypeStruct((T, D), f32), jax.ShapeDtypeStruct((T, D), bf16)],
        compiler_params=_cp("parallel"),
    )(x, attn, pool, wo, g2)


def _mlp_fwd_loss(h2, x1, wg, wu, wd, tgt, gf):
    tm = 512

    def body(h_ref, x1_ref, wg_ref, wu_ref, wd_ref, t_ref, g_ref,
             loss_ref, dg_ref, dx_ref, dxb_ref, ud_ref, silu_ref, a_ref, x2):
        i = pl.program_id(0)
        s = pl.program_id(1)

        @pl.when(jnp.logical_and(i == 0, s == 0))
        def _():
            loss_ref[...] = jnp.zeros_like(loss_ref)
            dg_ref[...] = jnp.zeros_like(dg_ref)

        h = h_ref[...]
        for c0, c1 in FS_CHUNKS:
            gate = lax.dot_general(h, wg_ref[c0:c1, :], NT, preferred_element_type=f32)
            up = lax.dot_general(h, wu_ref[c0:c1, :], NT, preferred_element_type=f32)
            sg = jax.nn.sigmoid(gate)
            silu = gate * sg
            ud_ref[:, c0:c1] = (up * (sg * (1.0 + gate * (1.0 - sg)))).astype(bf16)
            silu_ref[:, c0:c1] = silu.astype(bf16)
            a_ref[:, c0:c1] = (silu * up).astype(bf16)
        part = jnp.dot(a_ref[...], wd_ref[...], preferred_element_type=f32)

        @pl.when(s == 0)
        def _():
            x2[...] = x1_ref[...] + part

        @pl.when(s > 0)
        def _():
            x2[...] += part

        @pl.when(s == NSH - 1)
        def _():
            xv = x2[...]
            g = g_ref[...]
            r = lax.rsqrt(jnp.mean(xv * xv, axis=-1, keepdims=True) + EPS)
            xhat = xv * r
            e = xhat * g - t_ref[...]
            loss_ref[...] += 0.5 * jnp.sum(jnp.mean(e * e, axis=-1, keepdims=True))
            dy = e * (1.0 / D)
            dg_ref[...] += jnp.sum(dy * xhat, axis=0, keepdims=True)
            z = dy * g
            dx = r * (z - xhat * jnp.mean(z * xhat, axis=-1, keepdims=True))
            dx_ref[...] = dx
            dxb_ref[...] = dx.astype(bf16)

    row = lambda i, s: (i, 0)
    sl = lambda i, s: (s, i, 0)
    wsl = lambda i, s: (s, 0, 0)
    return pl.pallas_call(
        body, name="mlp_fwd_loss", grid=(T // tm, NSH),
        in_specs=[pl.BlockSpec((tm, D), row), pl.BlockSpec((tm, D), row),
                  pl.BlockSpec((None, FS, D), wsl), pl.BlockSpec((None, FS, D), wsl), pl.BlockSpec((None, FS, D), wsl),
                  pl.BlockSpec((tm, D), row), pl.BlockSpec((1, D), lambda i, s: (0, 0))],
        out_specs=[pl.BlockSpec((8, 128), lambda i, s: (0, 0)), pl.BlockSpec((1, D), lambda i, s: (0, 0)),
                   pl.BlockSpec((tm, D), row), pl.BlockSpec((tm, D), row),
                   pl.BlockSpec((None, tm, FS), sl), pl.BlockSpec((None, tm, FS), sl), pl.BlockSpec((None, tm, FS), sl)],
        out_shape=[jax.ShapeDtypeStruct((8, 128), f32), jax.ShapeDtypeStruct((1, D), f32),
                   jax.ShapeDtypeStruct((T, D), f32), jax.ShapeDtypeStruct((T, D), bf16)]
        + [jax.ShapeDtypeStruct((NSH, T, FS), bf16)] * 3,
        scratch_shapes=[pltpu.VMEM((tm, D), f32)],
        compiler_params=_cp("arbitrary", "arbitrary"),
    )(h2, x1, wg, wu, wd, tgt, gf)


def _mlp_bwd(dx2b, dx2, ud, silu, wg, wu, wd, x1, g2):
    tm = 512

    def body(dxb_ref, dx_ref, ud_ref, silu_ref, wg_ref, wu_ref, wd_ref, x1_ref, g_ref,
             dg_ref, du_ref, dx1_ref, dx1b_ref, dn_ref, acc):
        i = pl.program_id(0)
        s = pl.program_id(1)

        @pl.when(jnp.logical_and(i == 0, s == 0))
        def _():
            dn_ref[...] = jnp.zeros_like(dn_ref)

        dxb = dxb_ref[...]
        for c0, c1 in FS_CHUNKS:
            da = lax.dot_general(dxb, wd_ref[c0:c1, :], NT, preferred_element_type=f32)
            dg_ref[:, c0:c1] = (da * ud_ref[:, c0:c1].astype(f32)).astype(bf16)
            du_ref[:, c0:c1] = (da * silu_ref[:, c0:c1].astype(f32)).astype(bf16)
        part = jnp.dot(dg_ref[...], wg_ref[...], preferred_element_type=f32)
        part = part + jnp.dot(du_ref[...], wu_ref[...], preferred_element_type=f32)

        @pl.when(s == 0)
        def _():
            acc[...] = part

        @pl.when(s > 0)
        def _():
            acc[...] += part

        @pl.when(s == NSH - 1)
        def _():
            xv = x1_ref[...]
            r = lax.rsqrt(jnp.mean(xv * xv, axis=-1, keepdims=True) + EPS)
            xhat = xv * r
            dh = acc[...]
            dn_ref[...] += jnp.sum(dh * xhat, axis=0, keepdims=True)
            z = dh * g_ref[...]
            dx1 = dx_ref[...] + r * (z - xhat * jnp.mean(z * xhat, axis=-1, keepdims=True))
            dx1_ref[...] = dx1
            dx1b_ref[...] = dx1.astype(bf16)

    row = lambda i, s: (i, 0)
    sl = lambda i, s: (s, i, 0)
    wsl = lambda i, s: (s, 0, 0)
    return pl.pallas_call(
        body, name="mlp_bwd", grid=(T // tm, NSH),
        in_specs=[pl.BlockSpec((tm, D), row), pl.BlockSpec((tm, D), row),
                  pl.BlockSpec((None, tm, FS), sl), pl.BlockSpec((None, tm, FS), sl),
                  pl.BlockSpec((None, FS, D), wsl), pl.BlockSpec((None, FS, D), wsl), pl.BlockSpec((None, FS, D), wsl),
                  pl.BlockSpec((tm, D), row), pl.BlockSpec((1, D), lambda i, s: (0, 0))],
        out_specs=[pl.BlockSpec((None, tm, FS), sl), pl.BlockSpec((None, tm, FS), sl),
                   pl.BlockSpec((tm, D), row), pl.BlockSpec((tm, D), row), pl.BlockSpec((1, D), lambda i, s: (0, 0))],
        out_shape=[jax.ShapeDtypeStruct((NSH, T, FS), bf16)] * 2
        + [jax.ShapeDtypeStruct((T, D), f32), jax.ShapeDtypeStruct((T, D), bf16), jax.ShapeDtypeStruct((1, D), f32)],
        scratch_shapes=[pltpu.VMEM((tm, D), f32)],
        compiler_params=_cp("arbitrary", "arbitrary"),
    )(dx2b, dx2, ud, silu, wg, wu, wd, x1, g2)


def _mm_tn(a, bs, name, a_sharded=False, b_sharded=False, tk=512, out_dtype=bf16):
    nb = len(bs)
    sh = NSH if (a_sharded or b_sharded) else 1
    m = a.shape[-1]
    nk = T // tk

    def body(a_ref, *refs):
        kk = pl.program_id(1)
        av = a_ref[...]
        for b_ref, o_ref, acc in zip(refs[:nb], refs[nb:2 * nb], refs[2 * nb:]):
            upd = lax.dot_general(av, b_ref[...], TN, preferred_element_type=f32)

            @pl.when(kk == 0)
            def _():
                acc[...] = upd

            @pl.when(kk > 0)
            def _():
                acc[...] += upd

            @pl.when(kk == nk - 1)
            def _():
                o_ref[...] = acc[...].astype(out_dtype)

    a_spec = (pl.BlockSpec((None, tk, m), lambda s, k: (s, k, 0)) if a_sharded
              else pl.BlockSpec((tk, m), lambda s, k: (k, 0)))
    b_specs, o_specs, o_shapes, scratch = [], [], [], []
    for b in bs:
        n = b.shape[-1]
        b_specs.append(pl.BlockSpec((None, tk, n), lambda s, k: (s, k, 0)) if b_sharded
                       else pl.BlockSpec((tk, n), lambda s, k: (k, 0)))
        scratch.append(pltpu.VMEM((m, n), f32))
        if sh > 1:
            o_specs.append(pl.BlockSpec((None, m, n), lambda s, k: (s, 0, 0)))
            o_shapes.append(jax.ShapeDtypeStruct((sh, m, n), out_dtype))
        else:
            o_specs.append(pl.BlockSpec((m, n), lambda s, k: (0, 0)))
            o_shapes.append(jax.ShapeDtypeStruct((m, n), out_dtype))
    return pl.pallas_call(
        body, name=name, grid=(sh, nk), in_specs=[a_spec] + b_specs, out_specs=o_specs, out_shape=o_shapes,
        scratch_shapes=scratch, compiler_params=_cp("arbitrary", "arbitrary"),
    )(a, *bs)


def _mm_tn_rows(a_list, b, name, tk=1024, out_dtype=bf16):
    na = len(a_list)
    n = b.shape[-1]
    nk = T // tk

    def body(*refs):
        a_refs, b_ref = refs[:na], refs[na]
        o_refs, accs = refs[na + 1:2 * na + 1], refs[2 * na + 1:]
        kk = pl.program_id(0)
        bv = b_ref[...]
        for a_ref, o_ref, acc in zip(a_refs, o_refs, accs):
            upd = lax.dot_general(a_ref[...], bv, TN, preferred_element_type=f32)

            @pl.when(kk == 0)
            def _():
                acc[...] = upd

            @pl.when(kk > 0)
            def _():
                acc[...] += upd

            @pl.when(kk == nk - 1)
            def _():
                o_ref[...] = acc[...].astype(out_dtype)

    return pl.pallas_call(
        body, name=name, grid=(nk,),
        in_specs=[pl.BlockSpec((tk, a.shape[-1]), lambda k: (k, 0)) for a in a_list] + [pl.BlockSpec((tk, n), lambda k: (k, 0))],
        out_specs=[pl.BlockSpec((a.shape[-1], n), lambda k: (0, 0)) for a in a_list],
        out_shape=[jax.ShapeDtypeStruct((a.shape[-1], n), out_dtype) for a in a_list],
        scratch_shapes=[pltpu.VMEM((a.shape[-1], n), f32) for a in a_list],
        compiler_params=_cp("arbitrary"),
    )(*a_list, b)


def _outproj_bwd(dx1b, wo):
    tm = 512

    def body(dx_ref, wo_ref, da_ref, dp_ref):
        dx = dx_ref[...]
        da_ref[...] = lax.dot_general(dx, wo_ref[0:AW, :], NT, preferred_element_type=f32).astype(bf16)
        dp_ref[...] = lax.dot_general(dx, wo_ref[AW:2 * AW, :], NT, preferred_element_type=f32)

    return pl.pallas_call(
        body, name="outproj_bwd", grid=(T // tm,),
        in_specs=[pl.BlockSpec((tm, D), lambda i: (i, 0)), _full((D, D))],
        out_specs=[pl.BlockSpec((tm, AW), lambda i: (i, 0)), pl.BlockSpec((tm, AW), lambda i: (i, 0))],
        out_shape=[jax.ShapeDtypeStruct((T, AW), bf16), jax.ShapeDtypeStruct((T, AW), f32)],
        compiler_params=_cp("parallel"),
    )(dx1b, wo)


def _pool_bwd(dpool, pooled, wp, scale, dep):
    tm = 512
    n = T // tm

    def body(dp_ref, pb_ref, wp_ref, sc_ref, dep_ref, du_ref, dsc_ref, dwp_ref, ext):
        i = pl.program_id(0)

        @pl.when(i == 0)
        def _():
            ext[tm:tm + HALO, :] = jnp.zeros((HALO, AW), f32)
            dsc_ref[...] = jnp.zeros_like(dsc_ref)
            dwp_ref[...] = jnp.zeros_like(dwp_ref)

        t_idx = (n - 1 - i) * tm + lax.broadcasted_iota(jnp.int32, (tm, 1), 0)
        for g, w in enumerate(WINDOWS):
            lo, hi = 128 * g, 128 * (g + 1)
            pb = pb_ref[:, lo:hi]
            mixed = jnp.dot(pb, wp_ref[g], preferred_element_type=f32)
            dpo = dp_ref[:, lo:hi]
            dsc_ref[:, lo:hi] += jnp.sum(dpo * mixed, axis=0, keepdims=True)
            dmr = (dpo * sc_ref[:, lo:hi]).astype(bf16)
            dwp_ref[g] += lax.dot_general(pb, dmr, TN, preferred_element_type=f32)
            dpl = lax.dot_general(dmr, wp_ref[g], NT, preferred_element_type=f32)
            cnt = jnp.minimum(t_idx + 1, w).astype(f32)
            dpn = dpl / cnt
            ext[0:tm, lo:hi] = dpn
            acc = dpn
            for d in range(1, w):
                acc = acc + ext[d:d + tm, lo:hi]
            du_ref[:, lo:hi] = (acc - dpl).astype(bf16)
        ext[tm:tm + HALO, :] = ext[0:HALO, :]

    rev = lambda i: (n - 1 - i, 0)
    return pl.pallas_call(
        body, name="pool_bwd", grid=(n,),
        in_specs=[pl.BlockSpec((tm, AW), rev), pl.BlockSpec((tm, AW), rev), _full((4, 128, 128)), _full((1, AW)),
                  _full((8, 128))],
        out_specs=[pl.BlockSpec((tm, AW), rev), _full((1, AW)), _full((4, 128, 128))],
        out_shape=[jax.ShapeDtypeStruct((T, AW), bf16), jax.ShapeDtypeStruct((1, AW), f32),
                   jax.ShapeDtypeStruct((4, 128, 128), f32)],
        scratch_shapes=[pltpu.VMEM((tm + HALO, AW), f32)],
        compiler_params=_cp("arbitrary"),
    )(dpool, pooled, wp, scale, dep)


def _attn_bwd(qkv, qaug, kaug, attn, dattn, lse, dep):
    tq = tk = ATT_T
    n = T // tq
    it, jt = _causal_steps(True)
    nsteps = it.shape[0]

    rs = 64

    def body(it_ref, jt_ref, q_ref, k_ref, v_ref, qa_ref, ka_ref, o_ref, do_ref, lse_ref, dep_ref,
             dq_ref, dqs_ref, dk_ref, dks_ref, dv_ref, dq_acc, dk_acc, dv_acc, s_sc, dp_sc, p_sc, ds_sc):
        t = pl.program_id(1)
        i = it_ref[t]
        j = jt_ref[t]

        @pl.when(t == 0)
        def _():
            dq_acc[...] = jnp.zeros_like(dq_acc)

        @pl.when(i == j)
        def _():
            dk_acc[...] = jnp.zeros_like(dk_acc)
            dv_acc[...] = jnp.zeros_like(dv_acc)

        lane = lax.broadcasted_iota(jnp.int32, (tq, 128), 1)

        def step(on_diagonal):
            q = q_ref[...] * 0.125
            k = k_ref[...]
            v = v_ref[...]
            qa = qa_ref[...]
            ka = ka_ref[...]
            do = do_ref[...]
            dd = do.astype(f32) * o_ref[...].astype(f32)
            r0 = pl.multiple_of(i * tq, tq)
            qes, kes, does, deltas = [], [], [], []
            for e in range(2):
                hm = (lane >= 64) if e else (lane < 64)
                qes.append(jnp.where(hm, q, qa))
                kes.append(jnp.where(hm, k, ka))
                does.append(jnp.where(hm, do, jnp.zeros_like(do)))
                deltas.append(jnp.sum(jnp.where(hm, dd, 0.0), axis=1, keepdims=True))
                s_sc[e] = lax.dot_general(qes[e], kes[e], NT, preferred_element_type=f32)
                dp_sc[e] = lax.dot_general(does[e], v, NT, preferred_element_type=f32)
            for e in range(2):
                for r in range(0, tq, rs):
                    s = s_sc[e, r:r + rs, :] - lse_ref[r:r + rs, 64 * e:64 * e + 1]
                    if on_diagonal:
                        row = lax.broadcasted_iota(jnp.int32, (rs, tk), 0) + r
                        col = lax.broadcasted_iota(jnp.int32, (rs, tk), 1)
                        s = jnp.where(col <= row, s, NEG)
                    p = jnp.exp(s)
                    p_sc[e, r:r + rs, :] = p.astype(bf16)
                    ds_sc[e, r:r + rs, :] = (p * (dp_sc[e, r:r + rs, :] - deltas[e][r:r + rs, :])).astype(bf16)
                dv_acc[...] += lax.dot_general(does[e], p_sc[e], TN, preferred_element_type=f32)
                dsb = ds_sc[e]
                dk_acc[e] += lax.dot_general(qes[e], dsb, TN, preferred_element_type=f32)
                dq_acc[e, pl.ds(r0, tq), :] += jnp.dot(dsb, kes[e], preferred_element_type=f32)

        @pl.when(i > j)
        def _():
            step(False)

        @pl.when(i == j)
        def _():
            step(True)

        @pl.when(i == n - 1)
        def _():
            dk0 = dk_acc[0].T
            dk1 = dk_acc[1].T
            dk_ref[...] = jnp.where(lane < 64, dk0, dk1).astype(bf16)
            dks_ref[...] = jnp.where(lane < 64, dk1, dk0)
            dv_ref[...] = dv_acc[...].T.astype(bf16)

        @pl.when(t == nsteps - 1)
        def _():
            lane_t = lax.broadcasted_iota(jnp.int32, (T, 128), 1)
            dq_ref[...] = (jnp.where(lane_t < 64, dq_acc[0], dq_acc[1]) * 0.125).astype(bf16)
            dqs_ref[...] = jnp.where(lane_t < 64, dq_acc[1], dq_acc[0])

    qmap = lambda p, t, it, jt: (it[t], p)
    grid_spec = pltpu.PrefetchScalarGridSpec(
        num_scalar_prefetch=2, grid=(PAIRS, nsteps),
        in_specs=[pl.BlockSpec((tq, 128), qmap),
                  pl.BlockSpec((tk, 128), lambda p, t, it, jt: (jt[t], PAIRS + p)),
                  pl.BlockSpec((tk, 128), lambda p, t, it, jt: (jt[t], 2 * PAIRS + p)),
                  pl.BlockSpec((tq, 128), qmap), pl.BlockSpec((tk, 128), lambda p, t, it, jt: (jt[t], p)),
                  pl.BlockSpec((tq, 128), qmap), pl.BlockSpec((tq, 128), qmap),
                  pl.BlockSpec((None, tq, 128), lambda p, t, it, jt: (p, it[t], 0)),
                  pl.BlockSpec((8, 128), lambda p, t, it, jt: (0, 0))],
        out_specs=[pl.BlockSpec((T, 128), lambda p, t, it, jt: (0, p)),
                   pl.BlockSpec((None, T, 128), lambda p, t, it, jt: (p, 0, 0)),
                   pl.BlockSpec((tk, 128), lambda p, t, it, jt: (jt[t], p)),
                   pl.BlockSpec((None, tk, 128), lambda p, t, it, jt: (p, jt[t], 0)),
                   pl.BlockSpec((tk, 128), lambda p, t, it, jt: (jt[t], p))],
        scratch_shapes=[pltpu.VMEM((2, T, 128), f32), pltpu.VMEM((2, 128, tk), f32), pltpu.VMEM((128, tk), f32),
                        pltpu.VMEM((2, tq, tk), f32), pltpu.VMEM((2, tq, tk), f32), pltpu.VMEM((2, tq, tk), bf16),
                        pltpu.VMEM((2, tq, tk), bf16)],
    )
    return pl.pallas_call(
        body, name="fox_attn_bwd", grid_spec=grid_spec,
        out_shape=[jax.ShapeDtypeStruct((T, AW), bf16), jax.ShapeDtypeStruct((PAIRS, T, 128), f32),
                   jax.ShapeDtypeStruct((T, AW), bf16), jax.ShapeDtypeStruct((PAIRS, T, 128), f32),
                   jax.ShapeDtypeStruct((T, AW), bf16)],
        compiler_params=_cp("parallel", "arbitrary"),
    )(it, jt, qkv, qkv, qkv, qaug, kaug, attn, dattn, lse, dep)


def _fox_cumsum_bwd(dqs, dks, fl, bfp):
    tb = CUMSUM_ROWS
    nb = T // tb
    hp = lax.Precision.HIGHEST

    def body(dqs_ref, dks_ref, fl_ref, b_ref, df_ref, db_ref, carry):
        i = pl.program_id(0)

        @pl.when(i == 0)
        def _():
            carry[...] = jnp.zeros_like(carry)
            db_ref[...] = jnp.zeros_like(db_ref)

        r = lax.broadcasted_iota(jnp.int32, (128, 128), 0)
        cc = lax.broadcasted_iota(jnp.int32, (128, 128), 1)
        pick = lambda even_lane, odd_lane, p: jnp.logical_or(
            jnp.logical_and(r == even_lane, cc == 2 * p), jnp.logical_and(r == odd_lane, cc == 2 * p + 1)).astype(f32)
        dc = jnp.zeros((tb, 128), f32)
        for p in range(PAIRS):
            dc = dc + jnp.dot(dqs_ref[p], pick(64, 0, p), precision=hp, preferred_element_type=f32)
            dc = dc - jnp.dot(dks_ref[p], pick(67, 3, p), precision=hp, preferred_element_type=f32)
        rt = lax.broadcasted_iota(jnp.int32, (tb, tb), 0)
        ct = lax.broadcasted_iota(jnp.int32, (tb, tb), 1)
        utri = (ct >= rt).astype(f32)
        dl = jnp.dot(utri, dc, precision=hp, preferred_element_type=f32) + carry[0:1, :]
        carry[...] = jnp.broadcast_to(dl[0:1, :], (8, 128))
        z = fl_ref[...] + b_ref[...]
        df = dl * jax.nn.sigmoid(-z)
        df_ref[...] = df.astype(bf16)
        db_ref[...] += jnp.sum(df, axis=0, keepdims=True)

    rev = lambda i: (nb - 1 - i, 0)
    return pl.pallas_call(
        body, name="fox_cumsum_bwd", grid=(nb,),
        in_specs=[pl.BlockSpec((PAIRS, tb, 128), lambda i: (0, nb - 1 - i, 0)),
                  pl.BlockSpec((PAIRS, tb, 128), lambda i: (0, nb - 1 - i, 0)),
                  pl.BlockSpec((tb, 128), rev), _full((1, 128))],
        out_specs=[pl.BlockSpec((tb, 128), rev), _full((1, 128))],
        out_shape=[jax.ShapeDtypeStruct((T, 128), bf16), jax.ShapeDtypeStruct((1, 128), f32)],
        scratch_shapes=[pltpu.VMEM((8, 128), f32)],
        compiler_params=_cp("arbitrary"),
    )(dqs, dks, fl, bfp)


def _inproj_bwd(dq, dk, dv, du, df, wm, wf, x, dx1, g1):
    tm = 512

    def body(dq_ref, dk_ref, dv_ref, du_ref, df_ref, wm_ref, wf_ref, x_ref, dx1_ref, g_ref, dx_ref, dn_ref):
        i = pl.program_id(0)

        @pl.when(i == 0)
        def _():
            dn_ref[...] = jnp.zeros_like(dn_ref)

        dh = jnp.dot(dq_ref[...], wm_ref[0:AW, :], preferred_element_type=f32)
        dh = dh + jnp.dot(dk_ref[...], wm_ref[AW:2 * AW, :], preferred_element_type=f32)
        dh = dh + jnp.dot(dv_ref[...], wm_ref[2 * AW:3 * AW, :], preferred_element_type=f32)
        dh = dh + jnp.dot(du_ref[...], wm_ref[3 * AW:4 * AW, :], preferred_element_type=f32)
        dh = dh + jnp.dot(df_ref[...], wf_ref[...], preferred_element_type=f32)
        xv = x_ref[...]
        r = lax.rsqrt(jnp.mean(xv * xv, axis=-1, keepdims=True) + EPS)
        xhat = xv * r
        dn_ref[...] += jnp.sum(dh * xhat, axis=0, keepdims=True)
        z = dh * g_ref[...]
        dx_ref[...] = dx1_ref[...] + r * (z - xhat * jnp.mean(z * xhat, axis=-1, keepdims=True))

    row = lambda i: (i, 0)
    return pl.pallas_call(
        body, name="inproj_bwd", grid=(T // tm,),
        in_specs=[pl.BlockSpec((tm, AW), row)] * 4 + [pl.BlockSpec((tm, 128), row), _full((4 * AW, D)), _full((128, D)),
                                                       pl.BlockSpec((tm, D), row), pl.BlockSpec((tm, D), row), _full((1, D))],
        out_specs=[pl.BlockSpec((tm, D), row), _full((1, D))],
        out_shape=[jax.ShapeDtypeStruct((T, D), f32), jax.ShapeDtypeStruct((1, D), f32)],
        compiler_params=_cp("arbitrary"),
    )(dq, dk, dv, du, df, wm, wf, x, dx1, g1)


def _adamw_math(w, g, m, v):
    m = B1 * m + (1.0 - B1) * g
    v = B2 * v + (1.0 - B2) * (g * g)
    m_hat = m / (1.0 - B1 ** STEP)
    v_hat = v / (1.0 - B2 ** STEP)
    delta = -LR * (m_hat / (jnp.sqrt(v_hat) + AEPS) + WD * w)
    return delta, m, v


def _adamw_shard(w, m, v, p_mine, p_other, name):
    rows, cols = w.shape
    tr = rows if rows <= IN_S else rows // 2

    def body(w_ref, m_ref, v_ref, a_ref, b_ref, g_ref, d_ref, nm_ref, nv_ref):
        g = a_ref[...].astype(f32) + b_ref[...].astype(f32)
        g_ref[...] = g
        d_ref[...], nm_ref[...], nv_ref[...] = _adamw_math(w_ref[...], g, m_ref[...], v_ref[...])

    spec = pl.BlockSpec((tr, cols), lambda i: (i, 0))
    return pl.pallas_call(
        body, name=name, grid=(rows // tr,), in_specs=[spec] * 5, out_specs=[spec] * 4,
        out_shape=[jax.ShapeDtypeStruct((rows, cols), f32)] * 4, compiler_params=_cp("parallel"),
    )(w, m, v, p_mine, p_other)


SMALL_SLOTS = ((0, 8, 128), (8, 16, 128), (16, 24, 128), (24, 28, 128), (32, 33, 8), (40, 552, 128))
LOSS_ROW = 39


def _adamw_small(ws, ms, vs, parts):
    n = len(ws)

    def body(*refs):
        w_refs, m_refs, v_refs, p_ref = refs[0:n], refs[n:2 * n], refs[2 * n:3 * n], refs[3 * n]
        outs = refs[3 * n + 1:]
        g_all = p_ref[0]
        for k in range(1, 8):
            g_all = g_all + p_ref[k]
        for idx, (r0, r1, lanes) in enumerate(SMALL_SLOTS):
            g = g_all[r0:r1, 0:lanes]
            d, nm, nv = _adamw_math(w_refs[idx][...], g, m_refs[idx][...], v_refs[idx][...])
            outs[idx][...] = g
            outs[n + idx][...] = d
            outs[2 * n + idx][...] = nm
            outs[3 * n + idx][...] = nv
        outs[4 * n][...] = g_all[LOSS_ROW:LOSS_ROW + 1, :]

    shapes = [jax.ShapeDtypeStruct(w.shape, f32) for w in ws]
    res = pl.pallas_call(
        body, name="adamw_small", out_shape=shapes * 4 + [jax.ShapeDtypeStruct((1, 128), f32)],
    )(*ws, *ms, *vs, parts)
    return res[:4 * n], res[4 * n]


def _sum4(recv, g, mine, name):
    _, rows, cols = recv.shape
    tr = rows if rows <= IN_S else rows // 2

    def body(mine_ref, r_ref, g_ref, o_ref):
        o_ref[...] = ((g_ref[...].astype(f32) + r_ref[0].astype(f32))
                      + (r_ref[1].astype(f32) + r_ref[2].astype(f32))).astype(bf16)

    grid_spec = pltpu.PrefetchScalarGridSpec(
        num_scalar_prefetch=1, grid=(rows // tr,),
        in_specs=[pl.BlockSpec((3, tr, cols), lambda i, m: (0, i, 0)),
                  pl.BlockSpec((None, tr, cols), lambda i, m: (m[0], i, 0))],
        out_specs=pl.BlockSpec((tr, cols), lambda i, m: (i, 0)))
    return pl.pallas_call(
        body, name=name, grid_spec=grid_spec, out_shape=jax.ShapeDtypeStruct((rows, cols), bf16),
        compiler_params=_cp("arbitrary"),
    )(mine, recv, g)


_HBM = pl.BlockSpec(memory_space=pltpu.HBM)
_SEM = pl.BlockSpec(memory_space=pltpu.SEMAPHORE)
_EFFECT = pltpu.SideEffectType.DATAFLOW_SIDE_EFFECTING


def _in_hbm(a):
    return pltpu.with_memory_space_constraint(a, pltpu.HBM)


def _mesh_pos():
    return lax.axis_index("x"), lax.axis_index("y"), lax.axis_index("c")


def _other_chips(x, y):
    return [(1 - x, y), (x, 1 - y), (1 - x, 1 - y)]


def _gather_copy(srcs, lands, send_sems, recv_sems, a, k, slot):
    x, y, c = _mesh_pos()
    cx, cy = _other_chips(x, y)[k]
    return pltpu.make_async_remote_copy(
        src_ref=srcs[a], dst_ref=lands[a].at[slot], send_sem=send_sems.at[3 * a + k], recv_sem=recv_sems.at[3 * a + k],
        device_id=(cx, cy, c), device_id_type=MESH)


def _scatter_copy(srcs, lands, send_sems, recv_sems, a, k):
    x, y, c = _mesh_pos()
    cx, cy = _other_chips(x, y)[k]
    return pltpu.make_async_remote_copy(
        src_ref=srcs[a].at[2 * cx + cy], dst_ref=lands[a].at[k], send_sem=send_sems.at[3 * a + k],
        recv_sem=recv_sems.at[3 * a + k], device_id=(cx, cy, c), device_id_type=MESH)


def _all_gather_w_in(part):
    cols = part.shape[1] // 2

    def body(src, dst, send_sems, recv_sems, loc_sem):
        x, y, c = _mesh_pos()
        mine = 2 * x + y
        chips = _other_chips(x, y)
        half = lambda ref, cc: ref.at[:, pl.ds(pl.multiple_of(cc * cols, cols), cols)]

        def over_ici(k, slot):
            cx, cy = chips[k]
            return pltpu.make_async_remote_copy(
                src_ref=half(src, c), dst_ref=half(dst.at[slot], c), send_sem=send_sems.at[k], recv_sem=recv_sems.at[k],
                device_id=(cx, cy, c), device_id_type=MESH)

        def to_sibling(k, cc):
            slot = 2 * chips[k][0] + chips[k][1]
            return pltpu.make_async_remote_copy(
                src_ref=half(dst.at[slot], cc), dst_ref=half(dst.at[slot], cc), send_sem=send_sems.at[3 + k],
                recv_sem=recv_sems.at[3 + k], device_id=(x, y, 1 - c), device_id_type=MESH)

        local = pltpu.make_async_copy(src, dst.at[mine], loc_sem.at[0])
        local.start()
        first = [over_ici(k, mine) for k in range(3)]
        for cp in first:
            cp.start()
        passed = [to_sibling(k, c) for k in range(3)]
        for k in range(3):
            over_ici(k, 2 * chips[k][0] + chips[k][1]).wait_recv()
            passed[k].start()
        for k in range(3):
            to_sibling(k, 1 - c).wait_recv()
        for cp in first + passed:
            cp.wait_send()
        local.wait()

    return pl.pallas_call(
        body, name="all_gather_w_in", in_specs=[_HBM], out_specs=_HBM,
        out_shape=jax.ShapeDtypeStruct((NSH,) + part.shape, part.dtype),
        scratch_shapes=[pltpu.SemaphoreType.DMA((6,)), pltpu.SemaphoreType.DMA((6,)), pltpu.SemaphoreType.DMA((1,))],
    )(part)


def _split_start(name, srcs, lands, n_sems, plan, dep):
    n, nl = len(srcs), len(lands)

    def body(*refs):
        src_refs, land_refs = refs[:n], refs[n:n + nl]
        send_sems, recv_sems = refs[n + nl + 1], refs[n + nl + 2]
        token = refs[-1]
        sends, _ = plan(src_refs, land_refs, send_sems, recv_sems)
        for cp in sends:
            cp.start()
        token[...] = jnp.zeros_like(token)

    outs = pl.pallas_call(
        body, name=name,
        in_specs=[_HBM] * (n + nl) + [pl.BlockSpec(memory_space=pl.ANY)],
        out_specs=[_SEM, _SEM] + [_HBM] * (n + nl) + [pl.BlockSpec(memory_space=pltpu.VMEM)],
        out_shape=[pltpu.SemaphoreType.DMA((n_sems,)), pltpu.SemaphoreType.DMA((n_sems,))]
        + [pltpu.HBM(a.shape, a.dtype) for a in list(srcs) + list(lands)] + [jax.ShapeDtypeStruct((8, 128), f32)],
        input_output_aliases={i: 2 + i for i in range(n + nl)},
        compiler_params=pltpu.CompilerParams(has_side_effects=_EFFECT),
    )(*[_in_hbm(a) for a in list(srcs) + list(lands)], dep)
    return outs[0], outs[1], list(outs[2:2 + n]), list(outs[2 + n:2 + n + nl]), outs[-1]


def _split_wait(name, send_sems, recv_sems, srcs, lands, after, plan):
    n, nl = len(srcs), len(lands)

    def body(*refs):
        src_refs, land_refs = refs[:n], refs[n:n + nl]
        s_sems, r_sems = refs[n + nl], refs[n + nl + 1]
        sends, recvs = plan(src_refs, land_refs, s_sems, r_sems)
        for cp in recvs:
            cp.wait_recv()
        for cp in sends:
            cp.wait_send()

    outs = pl.pallas_call(
        body, name=name,
        in_specs=[_HBM] * (n + nl) + [_SEM, _SEM, pl.BlockSpec(memory_space=pl.ANY)],
        out_specs=[_HBM] * (n + nl),
        out_shape=[pltpu.HBM(a.shape, a.dtype) for a in list(srcs) + list(lands)],
        input_output_aliases={i: i for i in range(n + nl)},
        compiler_params=pltpu.CompilerParams(has_side_effects=_EFFECT),
    )(*srcs, *lands, send_sems, recv_sems, after)
    return list(outs[:n]), list(outs[n:])


def _gather_plan(srcs, lands, ss, rs):
    x, y, _ = _mesh_pos()
    chips = _other_chips(x, y)
    sends = [_gather_copy(srcs, lands, ss, rs, a, k, 2 * x + y) for a in range(len(srcs)) for k in range(3)]
    recvs = [_gather_copy(srcs, lands, ss, rs, a, k, 2 * chips[k][0] + chips[k][1])
             for a in range(len(srcs)) for k in range(3)]
    return sends, recvs


def _scatter_plan(srcs, lands, ss, rs):
    cps = [_scatter_copy(srcs, lands, ss, rs, a, k) for a in range(len(srcs)) for k in range(3)]
    return cps, cps


def _tail_plan(srcs, lands, ss, rs):
    x, y, c = _mesh_pos()
    me = 4 * x + 2 * y + c
    cps = [_scatter_copy(srcs[:1], lands[:1], ss, rs, 0, k) for k in range(3)]
    for f in range(1, 8):
        peer = ((x + (f >> 2)) % 2, (y + ((f >> 1) & 1)) % 2, (c + (f & 1)) % 2)
        cps.append(pltpu.make_async_remote_copy(
            src_ref=srcs[1], dst_ref=lands[1].at[me], send_sem=ss.at[2 + f], recv_sem=rs.at[2 + f],
            device_id=peer, device_id_type=MESH))
    return cps, cps


def _swap_with_sibling(parts, name):
    n = len(parts)

    def body(*refs):
        srcs, dsts = refs[:n], refs[n:2 * n]
        send_sems, recv_sems = refs[2 * n:]
        x, y, c = _mesh_pos()
        cps = [pltpu.make_async_remote_copy(src_ref=srcs[a], dst_ref=dsts[a], send_sem=send_sems.at[a],
                                            recv_sem=recv_sems.at[a], device_id=(x, y, 1 - c), device_id_type=MESH)
               for a in range(n)]
        for cp in cps:
            cp.start()
        for cp in cps:
            cp.wait_recv()
        for cp in cps:
            cp.wait_send()

    return pl.pallas_call(
        body, name=name, in_specs=[_HBM] * n, out_specs=[_HBM] * n,
        out_shape=[jax.ShapeDtypeStruct(p.shape, p.dtype) for p in parts],
        scratch_shapes=[pltpu.SemaphoreType.DMA((n,)), pltpu.SemaphoreType.DMA((n,))],
    )(*parts)


def _forward(x, tgt, wm, wf, mlp_w_fn, g1, bfp, wp, scale, g2, gf, dep):
    h, qkv, u, fl = _rms_inproj(x, g1, wm, wf, dep)
    qaug, kaug = _fox_cumsum(fl, bfp)
    attn, lse = _attn_fwd(qkv, qaug, kaug)
    pooled, pool = _pool_fwd(u, wp, scale)
    wo, wgt, wut, wd = mlp_w_fn(attn)
    x1, h2 = _outproj(x, attn, pool, wo, g2)
    loss, dgf, dx2, dx2b, ud, silu, a_b = _mlp_fwd_loss(h2, x1, wgt, wut, wd, tgt, gf)
    saved = dict(h=h, qkv=qkv, fl=fl, qaug=qaug, kaug=kaug, attn=attn, lse=lse, pooled=pooled, pool=pool, x1=x1, h2=h2,
                 ud=ud, silu=silu, a_b=a_b, wo=wo, wgt=wgt, wut=wut, wd=wd)
    return loss, dgf, dx2, dx2b, saved


def _backward_mlp(sv, dx2, dx2b, g2):
    dgate, dup, dx1, dx1b, dg2 = _mlp_bwd(dx2b, dx2, sv["ud"], sv["silu"], sv["wgt"], sv["wut"], sv["wd"], sv["x1"], g2)
    (dwd,) = _mm_tn(sv["a_b"], [dx2b], "dw_down", a_sharded=True, tk=T)
    (dwgt,) = _mm_tn(dgate, [sv["h2"]], "dw_gate", a_sharded=True, tk=T)
    (dwut,) = _mm_tn(dup, [sv["h2"]], "dw_up", a_sharded=True, tk=T)
    return dx1, dx1b, dg2, (dwgt, dwut, dwd)


def _backward_outproj(sv, dx1b):
    dattn, dpool = _outproj_bwd(dx1b, sv["wo"])
    dwo_a, = _mm_tn(sv["attn"], [dx1b], "dw_out_attn", tk=2048)
    dwo_p, = _mm_tn(sv["pool"], [dx1b], "dw_out_pool", tk=2048)
    dwo = jnp.concatenate([dwo_a, dwo_p], axis=0).reshape(NSH, D // NSH, D)
    return dattn, dpool, dwo


def _backward_mixer(sv, x, dx1, dattn, dpool, wm, wf, g1, bfp, wp, scale, dep):
    du, dscale, dwp = _pool_bwd(dpool, sv["pooled"], wp, scale, dep)
    dq, dqs, dk, dks, dv = _attn_bwd(sv["qkv"], sv["qaug"], sv["kaug"], sv["attn"], dattn, sv["lse"], dep)
    df, dbf = _fox_cumsum_bwd(dqs, dks, sv["fl"], bfp)
    dx, dg1 = _inproj_bwd(dq, dk, dv, du, df, wm, wf, x, dx1, g1)
    dwq, dwk, dwv, dwu_in, dwf = _mm_tn_rows([dq, dk, dv, du, df], sv["h"], "dw_in")
    dwin = jnp.concatenate([dwq, dwk, dwv, dwf[0:8], dwu_in], axis=0)
    return dx, dg1, dscale, dwp, dbf, dwin.reshape(NSH, IN_S, D)


def kernel(x, norm1_g, w_in, b_forget, w_pool, pool_scale, w_out, norm2_g, w_gate, w_up, w_down, final_g, loss_target, m_norm1_g, m_w_in, m_b_forget, m_w_pool, m_pool_scale, m_w_out, m_norm2_g, m_w_gate, m_w_up, m_w_down, m_final_g, v_norm1_g, v_w_in, v_b_forget, v_w_pool, v_pool_scale, v_w_out, v_norm2_g, v_w_gate, v_w_up, v_w_down, v_final_g):
    mine = (2 * lax.axis_index("x") + lax.axis_index("y")).astype(jnp.int32)
    mine1 = mine.reshape(1)
    tr = lambda a: jnp.transpose(a[0])

    win4 = _all_gather_w_in(tr(w_in).astype(bf16))
    later = [w_out[0].astype(bf16), tr(w_gate).astype(bf16), tr(w_up).astype(bf16), w_down[0].astype(bf16)]
    lands = [lax.dynamic_update_slice(lax.empty((NSH,) + p.shape, bf16), p[None], (mine, 0, 0)) for p in later]
    ag_send, ag_recv, later_thru, lands_thru, ag_token = _split_start("all_gather_start", later, lands, 12, _gather_plan,
                                                                      win4)
    win = win4.reshape(IN_W, D)
    wm = jnp.concatenate([win[0:3 * AW], win[3 * AW + 8:]], axis=0)
    wf = jnp.pad(win[3 * AW:3 * AW + 8], ((0, 120), (0, 0)))
    bfp = jnp.pad(b_forget, ((0, 0), (0, 120)))
    wp = w_pool[0].astype(bf16)
    gf = final_g.reshape(1, D)

    def later_weights(after):
        _, (wo4, wgt, wut, wd) = _split_wait("all_gather_wait", ag_send, ag_recv, later_thru, lands_thru, after, _gather_plan)
        return wo4.reshape(D, D), wgt, wut, wd

    xe, tgt = x[0], loss_target[0]
    loss_v, dgf, dx2, dx2b, sv = _forward(xe, tgt, wm, wf, later_weights, norm1_g, bfp, wp, pool_scale, norm2_g, gf, ag_token)
    dx1, dx1b, dg2, mlp_grads = _backward_mlp(sv, dx2, dx2b, norm2_g)
    dattn, dpool, dwo = _backward_outproj(sv, dx1b)
    first = [dwo] + list(mlp_grads)
    first_lands = [lax.empty((3,) + g.shape[1:], bf16) for g in first]
    rs_send, rs_recv, first_thru, first_lands_thru, rs_token = _split_start("reduce_scatter_start", first, first_lands, 12,
                                                                            _scatter_plan, dattn)
    dx, dg1, dscale, dwp, dbf, dwin = _backward_mixer(sv, xe, dx1, dattn, dpool, wm, wf, norm1_g, bfp, wp, pool_scale, rs_token)

    me = (4 * lax.axis_index("x") + 2 * lax.axis_index("y") + lax.axis_index("c")).astype(jnp.int32)
    pad8 = lambda r: jnp.pad(r, ((0, 8 - r.shape[0]), (0, 0)))
    loss_rows = jnp.concatenate([dbf, jnp.zeros((6, 128), f32), loss_v[0:1, :]], axis=0)
    small = jnp.concatenate([dg1.reshape(8, 128), dg2.reshape(8, 128), dgf.reshape(8, 128), pad8(dscale.reshape(4, 128)),
                             loss_rows, dwp.reshape(512, 128)], axis=0)
    small_land = lax.dynamic_update_slice(lax.empty((8, SMALL_ROWS, 128), f32), small[None], (me, 0, 0))
    tail_send, tail_recv, tail_thru, tail_lands_thru, tail_token = _split_start(
        "tail_start", [dwin, small], [lax.empty((3,) + dwin.shape[1:], bf16), small_land], 10, _tail_plan, dx)
    first_thru, first_recv = _split_wait("reduce_scatter_wait", rs_send, rs_recv, first_thru, first_lands_thru, tail_token,
                                         _scatter_plan)
    ws = [tr(w_in), w_out[0], tr(w_gate), tr(w_up), w_down[0]]
    ms = [tr(m_w_in), m_w_out[0], tr(m_w_gate), tr(m_w_up), m_w_down[0]]
    vs = [tr(v_w_in), v_w_out[0], tr(v_w_gate), tr(v_w_up), v_w_down[0]]
    partial = [_sum4(r, g, mine1, f"sum4_{i + 1}") for i, (r, g) in enumerate(zip(first_recv, first_thru))]
    other = _swap_with_sibling(partial, "swap_first")
    big = [_adamw_shard(ws[i + 1], ms[i + 1], vs[i + 1], partial[i], other[i], f"adamw_{i + 1}") for i in range(4)]
    (dwin_thru, _), (in_recv_land, small_all) = _split_wait("tail_wait", tail_send, tail_recv, tail_thru, tail_lands_thru,
                                                            big[3][0], _tail_plan)
    partial_in = _sum4(in_recv_land, dwin_thru, mine1, "sum4_0")
    (other_in,) = _swap_with_sibling([partial_in], "swap_in")
    big = [_adamw_shard(ws[0], ms[0], vs[0], partial_in, other_in, "adamw_0")] + big

    small_names = ["norm1_g", "norm2_g", "final_g", "pool_scale", "b_forget", "w_pool"]
    rows = lambda a, b, c, d, e, f: [a.reshape(8, 128), b.reshape(8, 128), c.reshape(8, 128), d.reshape(4, 128),
                                     e.reshape(1, 8), f.reshape(512, 128)]
    sm, loss_row = _adamw_small(rows(norm1_g, norm2_g, final_g, pool_scale, b_forget, w_pool),
                                rows(m_norm1_g, m_norm2_g, m_final_g, m_pool_scale, m_b_forget, m_w_pool),
                                rows(v_norm1_g, v_norm2_g, v_final_g, v_pool_scale, v_b_forget, v_w_pool), small_all)
    small_shape = dict(norm1_g=(1, D), norm2_g=(1, D), final_g=(D,), pool_scale=(1, AW), b_forget=(1, 8),
                       w_pool=(1, 4, 128, 128))

    order = ["norm1_g", "w_in", "b_forget", "w_pool", "pool_scale", "w_out", "norm2_g", "w_gate", "w_up", "w_down", "final_g"]
    big_idx = {"w_in": 0, "w_out": 1, "w_gate": 2, "w_up": 3, "w_down": 4}
    outs = [loss_row[0, 0], dx[None]]
    for kind in range(4):
        for name in order:
            if name in ("w_in", "w_gate", "w_up"):
                outs.append(jnp.transpose(big[big_idx[name]][kind])[None])
            elif name in big_idx:
                outs.append(big[big_idx[name]][kind][None])
            else:
                outs.append(sm[6 * kind + small_names.index(name)].reshape(small_shape[name]))
    return tuple(outs)
```

```python
import functools

import jax
import jax.numpy as jnp
import numpy as np
from jax import lax
from jax.experimental import pallas as pl
from jax.experimental.pallas import tpu as pltpu

f32 = jnp.float32
bf16 = jnp.bfloat16

T = 4096
D = 1024
NSH = 4
IN_W = 2056
IN_S = IN_W // NSH
AW = 512
PAIRS = 4
FF = 2816
FS = FF // NSH
WINDOWS = (2, 4, 8, 16)
HALO = 16
EPS = 1e-6
NEG = -1e30
LR, B1, B2, AEPS, WD, STEP = 0.001, 0.9, 0.999, 1e-08, 0.01, 10
SMALL_ROWS = 552

NT = (((1,), (1,)), ((), ()))
TN = (((0,), (0,)), ((), ()))

MESH = pl.DeviceIdType.MESH


def _cp(*sem):
    return pltpu.CompilerParams(dimension_semantics=sem)


def _full(shape):
    n = len(shape)
    return pl.BlockSpec(shape, lambda *_: (0,) * n)


def _rms_inproj(x, g1, wm, wf, dep):
    tm = 512

    def body(x_ref, g_ref, wm_ref, wf_ref, dep_ref, h_ref, qkv_ref, u_ref, fl_ref):
        xv = x_ref[...]
        r = lax.rsqrt(jnp.mean(xv * xv, axis=-1, keepdims=True) + EPS)
        h = (xv * r * g_ref[...]).astype(bf16)
        h_ref[...] = h
        qkv_ref[...] = lax.dot_general(h, wm_ref[0:3 * AW, :], NT, preferred_element_type=f32).astype(bf16)
        u_ref[...] = lax.dot_general(h, wm_ref[3 * AW:4 * AW, :], NT, preferred_element_type=f32)
        fl_ref[...] = lax.dot_general(h, wf_ref[...], NT, preferred_element_type=f32)

    return pl.pallas_call(
        body, name="rms_inproj", grid=(T // tm,),
        in_specs=[pl.BlockSpec((tm, D), lambda i: (i, 0)), _full((1, D)), _full((4 * AW, D)), _full((128, D)),
                  _full((8, 128))],
        out_specs=[pl.BlockSpec((tm, D), lambda i: (i, 0)), pl.BlockSpec((tm, 3 * AW), lambda i: (i, 0)),
                   pl.BlockSpec((tm, AW), lambda i: (i, 0)), pl.BlockSpec((tm, 128), lambda i: (i, 0))],
        out_shape=[jax.ShapeDtypeStruct((T, D), bf16), jax.ShapeDtypeStruct((T, 3 * AW), bf16),
                   jax.ShapeDtypeStruct((T, AW), f32), jax.ShapeDtypeStruct((T, 128), f32)],
        compiler_params=_cp("parallel"),
    )(x, g1, wm, wf, dep)


CUMSUM_ROWS = 512
FS_CHUNKS = ((0, 256), (256, 512), (512, FS))


def _log_sigmoid(z):
    return jnp.minimum(z, 0.0) - jnp.log(1.0 + jnp.exp(-jnp.abs(z)))


def _fox_cumsum(fl, bfp):
    tb = CUMSUM_ROWS
    nb = T // tb

    def body(fl_ref, b_ref, qa_ref, ka_ref, carry):
        i = pl.program_id(0)

        @pl.when(i == 0)
        def _():
            carry[...] = jnp.zeros_like(carry)

        lf = _log_sigmoid(fl_ref[...] + b_ref[...])
        r = lax.broadcasted_iota(jnp.int32, (tb, tb), 0)
        cc = lax.broadcasted_iota(jnp.int32, (tb, tb), 1)
        ltri = (cc <= r).astype(f32)
        cb = jnp.dot(ltri, lf, precision=lax.Precision.HIGHEST, preferred_element_type=f32) + carry[0:1, :]
        carry[...] = jnp.broadcast_to(cb[tb - 1:tb, :], (8, 128))
        hi = cb.astype(bf16)
        r1 = cb - hi.astype(f32)
        mid = r1.astype(bf16)
        lo = (r1 - mid.astype(f32)).astype(bf16)
        head = lax.broadcasted_iota(jnp.int32, (128, AW), 0)
        col = lax.broadcasted_iota(jnp.int32, (128, AW), 1)
        base = 128 * (head >> 1) + 64 * (1 - (head & 1))
        place = lambda off: jnp.logical_and(col == base + off, head < 8).astype(bf16)
        mm = lambda a, off: jnp.dot(a, place(off), preferred_element_type=f32)
        cq = mm(hi, 0) + mm(mid, 1) + mm(lo, 2)
        ck = mm(hi, 3) + mm(mid, 4) + mm(lo, 5)
        within = jnp.bitwise_and(lax.broadcasted_iota(jnp.int32, (tb, AW), 1), 63)
        qa_ref[...] = jnp.where(jnp.logical_and(within >= 3, within <= 5), 1.0, cq).astype(bf16)
        ka_ref[...] = jnp.where(within <= 2, 1.0, -ck).astype(bf16)

    return pl.pallas_call(
        body, name="fox_cumsum", grid=(nb,),
        in_specs=[pl.BlockSpec((tb, 128), lambda i: (i, 0)), _full((1, 128))],
        out_specs=[pl.BlockSpec((tb, AW), lambda i: (i, 0)), pl.BlockSpec((tb, AW), lambda i: (i, 0))],
        out_shape=[jax.ShapeDtypeStruct((T, AW), bf16), jax.ShapeDtypeStruct((T, AW), bf16)],
        scratch_shapes=[pltpu.VMEM((8, 128), f32)],
        compiler_params=_cp("arbitrary"),
    )(fl, bfp)


ATT_T = 512


def _causal_steps(key_major):
    n = T // ATT_T
    if key_major:
        pairs = [(i, j) for j in range(n) for i in range(j, n)]
    else:
        pairs = [(i, j) for i in range(n) for j in range(i + 1)]
    it = np.array([p[0] for p in pairs], np.int32)
    jt = np.array([p[1] for p in pairs], np.int32)
    return jnp.asarray(it), jnp.asarray(jt)


def _attn_fwd(qkv, qaug, kaug):
    tq = tk = ATT_T
    it, jt = _causal_steps(False)
    nsteps = it.shape[0]

    rs = 64

    def body(it_ref, jt_ref, q_ref, k_ref, v_ref, qa_ref, ka_ref, o_ref, lse_ref, m_sc, acc_sc, s_sc, p_sc, alpha_sc):
        t = pl.program_id(1)
        i = it_ref[t]
        j = jt_ref[t]

        @pl.when(j == 0)
        def _():
            m_sc[...] = jnp.full_like(m_sc, NEG)
            acc_sc[...] = jnp.zeros_like(acc_sc)

        lane = lax.broadcasted_iota(jnp.int32, (tq, 128), 1)
        spare = (64, 0)

        def step(on_diagonal):
            q = q_ref[...] * 0.125
            k = k_ref[...]
            v = v_ref[...]
            qa = qa_ref[...]
            ka = ka_ref[...]
            for e in range(2):
                hm = (lane >= 64) if e else (lane < 64)
                s_sc[e] = lax.dot_general(jnp.where(hm, q, qa), jnp.where(hm, k, ka), NT, preferred_element_type=f32)
            for e in range(2):
                hm = (lane >= 64) if e else (lane < 64)
                for r in range(0, tq, rs):
                    s = s_sc[e, r:r + rs, :]
                    if on_diagonal:
                        row = lax.broadcasted_iota(jnp.int32, (rs, tk), 0) + r
                        col = lax.broadcasted_iota(jnp.int32, (rs, tk), 1)
                        s = jnp.where(col <= row, s, NEG)
                    m_prev = m_sc[e, r:r + rs, :]
                    m_new = jnp.maximum(m_prev, jnp.max(s, axis=1, keepdims=True))
                    p_sc[e, r:r + rs, :] = jnp.exp(s - jnp.tile(m_new, (1, tk // 128))).astype(bf16)
                    alpha_sc[e, r:r + rs, :] = jnp.exp(m_prev - m_new)
                    m_sc[e, r:r + rs, :] = m_new
            for e in range(2):
                hm = (lane >= 64) if e else (lane < 64)
                ve = jnp.where(hm, v, (lane == spare[e]).astype(bf16))
                acc_sc[e] = alpha_sc[e] * acc_sc[e] + jnp.dot(p_sc[e], ve, preferred_element_type=f32)

        @pl.when(j < i)
        def _():
            step(False)

        @pl.when(j == i)
        def _():
            step(True)
            l0 = acc_sc[0][:, spare[0]:spare[0] + 1]
            l1 = acc_sc[1][:, spare[1]:spare[1] + 1]
            o_ref[...] = jnp.where(lane < 64, acc_sc[0] / l0, acc_sc[1] / l1).astype(bf16)
            lse_ref[...] = jnp.where(lane < 64, m_sc[0] + jnp.log(l0), m_sc[1] + jnp.log(l1))

    qmap = lambda p, t, it, jt: (it[t], p)
    kmap = lambda p, t, it, jt: (jt[t], p)
    grid_spec = pltpu.PrefetchScalarGridSpec(
        num_scalar_prefetch=2, grid=(PAIRS, nsteps),
        in_specs=[pl.BlockSpec((tq, 128), qmap),
                  pl.BlockSpec((tk, 128), lambda p, t, it, jt: (jt[t], PAIRS + p)),
                  pl.BlockSpec((tk, 128), lambda p, t, it, jt: (jt[t], 2 * PAIRS + p)),
                  pl.BlockSpec((tq, 128), qmap), pl.BlockSpec((tk, 128), kmap)],
        out_specs=[pl.BlockSpec((tq, 128), qmap),
                   pl.BlockSpec((None, tq, 128), lambda p, t, it, jt: (p, it[t], 0))],
        scratch_shapes=[pltpu.VMEM((2, tq, 128), f32), pltpu.VMEM((2, tq, 128), f32), pltpu.VMEM((2, tq, tk), f32),
                        pltpu.VMEM((2, tq, tk), bf16), pltpu.VMEM((2, tq, 128), f32)],
    )
    return pl.pallas_call(
        body, name="fox_attn_fwd", grid_spec=grid_spec,
        out_shape=[jax.ShapeDtypeStruct((T, AW), bf16), jax.ShapeDtypeStruct((PAIRS, T, 128), f32)],
        compiler_params=_cp("parallel", "arbitrary"),
    )(it, jt, qkv, qkv, qkv, qaug, kaug)


def _pool_fwd(u, wp, scale):
    tm = 512

    def body(u_ref, wp_ref, sc_ref, pooled_ref, pool_ref, ext):
        i = pl.program_id(0)

        @pl.when(i == 0)
        def _():
            ext[0:HALO, :] = jnp.zeros((HALO, AW), f32)

        uv = u_ref[...]
        ext[HALO:HALO + tm, :] = uv
        t_idx = i * tm + lax.broadcasted_iota(jnp.int32, (tm, 1), 0)
        for g, w in enumerate(WINDOWS):
            lo, hi = 128 * g, 128 * (g + 1)
            ug = uv[:, lo:hi]
            acc = ug
            for d in range(1, w):
                acc = acc + ext[HALO - d:HALO - d + tm, lo:hi]
            cnt = jnp.minimum(t_idx + 1, w).astype(f32)
            pb = (acc / cnt - ug).astype(bf16)
            pooled_ref[:, lo:hi] = pb
            mixed = jnp.dot(pb, wp_ref[g], preferred_element_type=f32)
            pool_ref[:, lo:hi] = (mixed * sc_ref[:, lo:hi]).astype(bf16)
        ext[0:HALO, :] = uv[tm - HALO:tm, :]

    return pl.pallas_call(
        body, name="pool_fwd", grid=(T // tm,),
        in_specs=[pl.BlockSpec((tm, AW), lambda i: (i, 0)), _full((4, 128, 128)), _full((1, AW))],
        out_specs=[pl.BlockSpec((tm, AW), lambda i: (i, 0)), pl.BlockSpec((tm, AW), lambda i: (i, 0))],
        out_shape=[jax.ShapeDtypeStruct((T, AW), bf16), jax.ShapeDtypeStruct((T, AW), bf16)],
        scratch_shapes=[pltpu.VMEM((tm + HALO, AW), f32)],
        compiler_params=_cp("arbitrary"),
    )(u, wp, scale)


def _outproj(x, attn, pool, wo, g2):
    tm = 512

    def body(x_ref, a_ref, p_ref, wo_ref, g_ref, x1_ref, h2_ref):
        x1 = x_ref[...] + jnp.dot(a_ref[...], wo_ref[0:AW, :], preferred_element_type=f32)
        x1 = x1 + jnp.dot(p_ref[...], wo_ref[AW:2 * AW, :], preferred_element_type=f32)
        x1_ref[...] = x1
        r = lax.rsqrt(jnp.mean(x1 * x1, axis=-1, keepdims=True) + EPS)
        h2_ref[...] = (x1 * r * g_ref[...]).astype(bf16)

    return pl.pallas_call(
        body, name="outproj", grid=(T // tm,),
        in_specs=[pl.BlockSpec((tm, D), lambda i: (i, 0)), pl.BlockSpec((tm, AW), lambda i: (i, 0)),
                  pl.BlockSpec((tm, AW), lambda i: (i, 0)), _full((D, D)), _full((1, D))],
        out_specs=[pl.BlockSpec((tm, D), lambda i: (i, 0)), pl.BlockSpec((tm, D), lambda i: (i, 0))],
        out_shape=[jax.ShapeDtypeStruct((T, D), f32), jax.ShapeDtypeStruct((T, D), bf16)],
        compiler_params=_cp("parallel"),
    )(x, attn, pool, wo, g2)


def _mlp_fwd_loss(h2, x1, wg, wu, wd, tgt, gf):
    tm = 512

    def body(h_ref, x1_ref, wg_ref, wu_ref, wd_ref, t_ref, g_ref,
             loss_ref, dg_ref, dx_ref, dxb_ref, ud_ref, silu_ref, a_ref, x2):
        i = pl.program_id(0)
        s = pl.program_id(1)

        @pl.when(jnp.logical_and(i == 0, s == 0))
        def _():
            loss_ref[...] = jnp.zeros_like(loss_ref)
            dg_ref[...] = jnp.zeros_like(dg_ref)

        h = h_ref[...]
        gus = [(lax.dot_general(h, wg_ref[c0:c1, :], NT, preferred_element_type=f32),
                lax.dot_general(h, wu_ref[c0:c1, :], NT, preferred_element_type=f32)) for c0, c1 in FS_CHUNKS]
        for (c0, c1), (gate, up) in zip(FS_CHUNKS, gus):
            sg = jax.nn.sigmoid(gate)
            silu = gate * sg
            ud_ref[:, c0:c1] = (up * (sg * (1.0 + gate * (1.0 - sg)))).astype(bf16)
            silu_ref[:, c0:c1] = silu.astype(bf16)
            a_ref[:, c0:c1] = (silu * up).astype(bf16)
        part = jnp.dot(a_ref[...], wd_ref[...], preferred_element_type=f32)

        @pl.when(s == 0)
        def _():
            x2[...] = x1_ref[...] + part

        @pl.when(s > 0)
        def _():
            x2[...] += part

        @pl.when(s == NSH - 1)
        def _():
            xv = x2[...]
            g = g_ref[...]
            r = lax.rsqrt(jnp.mean(xv * xv, axis=-1, keepdims=True) + EPS)
            xhat = xv * r
            e = xhat * g - t_ref[...]
            loss_ref[...] += 0.5 * jnp.sum(jnp.mean(e * e, axis=-1, keepdims=True))
            dy = e * (1.0 / D)
            dg_ref[...] += jnp.sum(dy * xhat, axis=0, keepdims=True)
            z = dy * g
            dx = r * (z - xhat * jnp.mean(z * xhat, axis=-1, keepdims=True))
            dx_ref[...] = dx
            dxb_ref[...] = dx.astype(bf16)

    row = lambda i, s: (i, 0)
    sl = lambda i, s: (s, i, 0)
    wsl = lambda i, s: (s, 0, 0)
    return pl.pallas_call(
        body, name="mlp_fwd_loss", grid=(T // tm, NSH),
        in_specs=[pl.BlockSpec((tm, D), row), pl.BlockSpec((tm, D), row),
                  pl.BlockSpec((None, FS, D), wsl), pl.BlockSpec((None, FS, D), wsl), pl.BlockSpec((None, FS, D), wsl),
                  pl.BlockSpec((tm, D), row), pl.BlockSpec((1, D), lambda i, s: (0, 0))],
        out_specs=[pl.BlockSpec((8, 128), lambda i, s: (0, 0)), pl.BlockSpec((1, D), lambda i, s: (0, 0)),
                   pl.BlockSpec((tm, D), row), pl.BlockSpec((tm, D), row),
                   pl.BlockSpec((None, tm, FS), sl), pl.BlockSpec((None, tm, FS), sl), pl.BlockSpec((None, tm, FS), sl)],
        out_shape=[jax.ShapeDtypeStruct((8, 128), f32), jax.ShapeDtypeStruct((1, D), f32),
                   jax.ShapeDtypeStruct((T, D), f32), jax.ShapeDtypeStruct((T, D), bf16)]
        + [jax.ShapeDtypeStruct((NSH, T, FS), bf16)] * 3,
        scratch_shapes=[pltpu.VMEM((tm, D), f32)],
        compiler_params=_cp("arbitrary", "arbitrary"),
    )(h2, x1, wg, wu, wd, tgt, gf)


def _mlp_bwd(dx2b, dx2, ud, silu, wg, wu, wd, x1, g2):
    tm = 512

    def body(dxb_ref, dx_ref, ud_ref, silu_ref, wg_ref, wu_ref, wd_ref, x1_ref, g_ref,
             dg_ref, du_ref, dx1_ref, dx1b_ref, dn_ref, acc):
        i = pl.program_id(0)
        s = pl.program_id(1)

        @pl.when(jnp.logical_and(i == 0, s == 0))
        def _():
            dn_ref[...] = jnp.zeros_like(dn_ref)

        dxb = dxb_ref[...]
        das = [lax.dot_general(dxb, wd_ref[c0:c1, :], NT, preferred_element_type=f32) for c0, c1 in FS_CHUNKS]
        for (c0, c1), da in zip(FS_CHUNKS, das):
            dg_ref[:, c0:c1] = (da * ud_ref[:, c0:c1].astype(f32)).astype(bf16)
            du_ref[:, c0:c1] = (da * silu_ref[:, c0:c1].astype(f32)).astype(bf16)
        part = jnp.dot(dg_ref[...], wg_ref[...], preferred_element_type=f32)
        part = part + jnp.dot(du_ref[...], wu_ref[...], preferred_element_type=f32)

        @pl.when(s == 0)
        def _():
            acc[...] = part

        @pl.when(s > 0)
        def _():
            acc[...] += part

        @pl.when(s == NSH - 1)
        def _():
            xv = x1_ref[...]
            r = lax.rsqrt(jnp.mean(xv * xv, axis=-1, keepdims=True) + EPS)
            xhat = xv * r
            dh = acc[...]
            dn_ref[...] += jnp.sum(dh * xhat, axis=0, keepdims=True)
            z = dh * g_ref[...]
            dx1 = dx_ref[...] + r * (z - xhat * jnp.mean(z * xhat, axis=-1, keepdims=True))
            dx1_ref[...] = dx1
            dx1b_ref[...] = dx1.astype(bf16)

    row = lambda i, s: (i, 0)
    sl = lambda i, s: (s, i, 0)
    wsl = lambda i, s: (s, 0, 0)
    return pl.pallas_call(
        body, name="mlp_bwd", grid=(T // tm, NSH),
        in_specs=[pl.BlockSpec((tm, D), row), pl.BlockSpec((tm, D), row),
                  pl.BlockSpec((None, tm, FS), sl), pl.BlockSpec((None, tm, FS), sl),
                  pl.BlockSpec((None, FS, D), wsl), pl.BlockSpec((None, FS, D), wsl), pl.BlockSpec((None, FS, D), wsl),
                  pl.BlockSpec((tm, D), row), pl.BlockSpec((1, D), lambda i, s: (0, 0))],
        out_specs=[pl.BlockSpec((None, tm, FS), sl), pl.BlockSpec((None, tm, FS), sl),
                   pl.BlockSpec((tm, D), row), pl.BlockSpec((tm, D), row), pl.BlockSpec((1, D), lambda i, s: (0, 0))],
        out_shape=[jax.ShapeDtypeStruct((NSH, T, FS), bf16)] * 2
        + [jax.ShapeDtypeStruct((T, D), f32), jax.ShapeDtypeStruct((T, D), bf16), jax.ShapeDtypeStruct((1, D), f32)],
        scratch_shapes=[pltpu.VMEM((tm, D), f32)],
        compiler_params=_cp("arbitrary", "arbitrary"),
    )(dx2b, dx2, ud, silu, wg, wu, wd, x1, g2)


def _mm_tn(a, bs, name, a_sharded=False, b_sharded=False, tk=512, out_dtype=bf16):
    nb = len(bs)
    sh = NSH if (a_sharded or b_sharded) else 1
    m = a.shape[-1]
    nk = T // tk

    def body(a_ref, *refs):
        kk = pl.program_id(1)
        av = a_ref[...]
        for b_ref, o_ref, acc in zip(refs[:nb], refs[nb:2 * nb], refs[2 * nb:]):
            upd = lax.dot_general(av, b_ref[...], TN, preferred_element_type=f32)

            @pl.when(kk == 0)
            def _():
                acc[...] = upd

            @pl.when(kk > 0)
            def _():
                acc[...] += upd

            @pl.when(kk == nk - 1)
            def _():
                o_ref[...] = acc[...].astype(out_dtype)

    a_spec = (pl.BlockSpec((None, tk, m), lambda s, k: (s, k, 0)) if a_sharded
              else pl.BlockSpec((tk, m), lambda s, k: (k, 0)))
    b_specs, o_specs, o_shapes, scratch = [], [], [], []
    for b in bs:
        n = b.shape[-1]
        b_specs.append(pl.BlockSpec((None, tk, n), lambda s, k: (s, k, 0)) if b_sharded
                       else pl.BlockSpec((tk, n), lambda s, k: (k, 0)))
        scratch.append(pltpu.VMEM((m, n), f32))
        if sh > 1:
            o_specs.append(pl.BlockSpec((None, m, n), lambda s, k: (s, 0, 0)))
            o_shapes.append(jax.ShapeDtypeStruct((sh, m, n), out_dtype))
        else:
            o_specs.append(pl.BlockSpec((m, n), lambda s, k: (0, 0)))
            o_shapes.append(jax.ShapeDtypeStruct((m, n), out_dtype))
    return pl.pallas_call(
        body, name=name, grid=(sh, nk), in_specs=[a_spec] + b_specs, out_specs=o_specs, out_shape=o_shapes,
        scratch_shapes=scratch, compiler_params=_cp("arbitrary", "arbitrary"),
    )(a, *bs)


def _mm_tn_rows(a_list, b, name, tk=1024, out_dtype=bf16):
    na = len(a_list)
    n = b.shape[-1]
    nk = T // tk

    def body(*refs):
        a_refs, b_ref = refs[:na], refs[na]
        o_refs, accs = refs[na + 1:2 * na + 1], refs[2 * na + 1:]
        kk = pl.program_id(0)
        bv = b_ref[...]
        for a_ref, o_ref, acc in zip(a_refs, o_refs, accs):
            upd = lax.dot_general(a_ref[...], bv, TN, preferred_element_type=f32)

            @pl.when(kk == 0)
            def _():
                acc[...] = upd

            @pl.when(kk > 0)
            def _():
                acc[...] += upd

            @pl.when(kk == nk - 1)
            def _():
                o_ref[...] = acc[...].astype(out_dtype)

    return pl.pallas_call(
        body, name=name, grid=(nk,),
        in_specs=[pl.BlockSpec((tk, a.shape[-1]), lambda k: (k, 0)) for a in a_list] + [pl.BlockSpec((tk, n), lambda k: (k, 0))],
        out_specs=[pl.BlockSpec((a.shape[-1], n), lambda k: (0, 0)) for a in a_list],
        out_shape=[jax.ShapeDtypeStruct((a.shape[-1], n), out_dtype) for a in a_list],
        scratch_shapes=[pltpu.VMEM((a.shape[-1], n), f32) for a in a_list],
        compiler_params=_cp("arbitrary"),
    )(*a_list, b)


def _outproj_bwd(dx1b, wo):
    tm = 512

    def body(dx_ref, wo_ref, da_ref, dp_ref):
        dx = dx_ref[...]
        da_ref[...] = lax.dot_general(dx, wo_ref[0:AW, :], NT, preferred_element_type=f32).astype(bf16)
        dp_ref[...] = lax.dot_general(dx, wo_ref[AW:2 * AW, :], NT, preferred_element_type=f32)

    return pl.pallas_call(
        body, name="outproj_bwd", grid=(T // tm,),
        in_specs=[pl.BlockSpec((tm, D), lambda i: (i, 0)), _full((D, D))],
        out_specs=[pl.BlockSpec((tm, AW), lambda i: (i, 0)), pl.BlockSpec((tm, AW), lambda i: (i, 0))],
        out_shape=[jax.ShapeDtypeStruct((T, AW), bf16), jax.ShapeDtypeStruct((T, AW), f32)],
        compiler_params=_cp("parallel"),
    )(dx1b, wo)


def _pool_bwd(dpool, pooled, wp, scale, dep):
    tm = 512
    n = T // tm

    def body(dp_ref, pb_ref, wp_ref, sc_ref, dep_ref, du_ref, dsc_ref, dwp_ref, ext):
        i = pl.program_id(0)

        @pl.when(i == 0)
        def _():
            ext[tm:tm + HALO, :] = jnp.zeros((HALO, AW), f32)
            dsc_ref[...] = jnp.zeros_like(dsc_ref)
            dwp_ref[...] = jnp.zeros_like(dwp_ref)

        t_idx = (n - 1 - i) * tm + lax.broadcasted_iota(jnp.int32, (tm, 1), 0)
        for g, w in enumerate(WINDOWS):
            lo, hi = 128 * g, 128 * (g + 1)
            pb = pb_ref[:, lo:hi]
            mixed = jnp.dot(pb, wp_ref[g], preferred_element_type=f32)
            dpo = dp_ref[:, lo:hi]
            dsc_ref[:, lo:hi] += jnp.sum(dpo * mixed, axis=0, keepdims=True)
            dmr = (dpo * sc_ref[:, lo:hi]).astype(bf16)
            dwp_ref[g] += lax.dot_general(pb, dmr, TN, preferred_element_type=f32)
            dpl = lax.dot_general(dmr, wp_ref[g], NT, preferred_element_type=f32)
            cnt = jnp.minimum(t_idx + 1, w).astype(f32)
            dpn = dpl / cnt
            ext[0:tm, lo:hi] = dpn
            acc = dpn
            for d in range(1, w):
                acc = acc + ext[d:d + tm, lo:hi]
            du_ref[:, lo:hi] = (acc - dpl).astype(bf16)
        ext[tm:tm + HALO, :] = ext[0:HALO, :]

    rev = lambda i: (n - 1 - i, 0)
    return pl.pallas_call(
        body, name="pool_bwd", grid=(n,),
        in_specs=[pl.BlockSpec((tm, AW), rev), pl.BlockSpec((tm, AW), rev), _full((4, 128, 128)), _full((1, AW)),
                  _full((8, 128))],
        out_specs=[pl.BlockSpec((tm, AW), rev), _full((1, AW)), _full((4, 128, 128))],
        out_shape=[jax.ShapeDtypeStruct((T, AW), bf16), jax.ShapeDtypeStruct((1, AW), f32),
                   jax.ShapeDtypeStruct((4, 128, 128), f32)],
        scratch_shapes=[pltpu.VMEM((tm + HALO, AW), f32)],
        compiler_params=_cp("arbitrary"),
    )(dpool, pooled, wp, scale, dep)


def _attn_bwd(qkv, qaug, kaug, attn, dattn, lse, dep):
    tq = tk = ATT_T
    n = T // tq
    it, jt = _causal_steps(True)
    nsteps = it.shape[0]

    rs = 64

    def body(it_ref, jt_ref, q_ref, k_ref, v_ref, qa_ref, ka_ref, o_ref, do_ref, lse_ref, dep_ref,
             dq_ref, dqs_ref, dk_ref, dks_ref, dv_ref, dq_acc, dk_acc, dv_acc, s_sc, dp_sc, p_sc, ds_sc):
        t = pl.program_id(1)
        i = it_ref[t]
        j = jt_ref[t]

        @pl.when(t == 0)
        def _():
            dq_acc[...] = jnp.zeros_like(dq_acc)

        @pl.when(i == j)
        def _():
            dk_acc[...] = jnp.zeros_like(dk_acc)
            dv_acc[...] = jnp.zeros_like(dv_acc)

        lane = lax.broadcasted_iota(jnp.int32, (tq, 128), 1)

        def step(on_diagonal):
            q = q_ref[...] * 0.125
            k = k_ref[...]
            v = v_ref[...]
            qa = qa_ref[...]
            ka = ka_ref[...]
            do = do_ref[...]
            dd = do.astype(f32) * o_ref[...].astype(f32)
            r0 = pl.multiple_of(i * tq, tq)
            qes, kes, does, deltas = [], [], [], []
            for e in range(2):
                hm = (lane >= 64) if e else (lane < 64)
                qes.append(jnp.where(hm, q, qa))
                kes.append(jnp.where(hm, k, ka))
                does.append(jnp.where(hm, do, jnp.zeros_like(do)))
                deltas.append(jnp.sum(jnp.where(hm, dd, 0.0), axis=1, keepdims=True))
                s_sc[e] = lax.dot_general(qes[e], kes[e], NT, preferred_element_type=f32)
                dp_sc[e] = lax.dot_general(does[e], v, NT, preferred_element_type=f32)
            for e in range(2):
                for r in range(0, tq, rs):
                    s = s_sc[e, r:r + rs, :] - lse_ref[r:r + rs, 64 * e:64 * e + 1]
                    if on_diagonal:
                        row = lax.broadcasted_iota(jnp.int32, (rs, tk), 0) + r
                        col = lax.broadcasted_iota(jnp.int32, (rs, tk), 1)
                        s = jnp.where(col <= row, s, NEG)
                    p = jnp.exp(s)
                    p_sc[e, r:r + rs, :] = p.astype(bf16)
                    ds_sc[e, r:r + rs, :] = (p * (dp_sc[e, r:r + rs, :] - deltas[e][r:r + rs, :])).astype(bf16)
                dv_acc[...] += lax.dot_general(does[e], p_sc[e], TN, preferred_element_type=f32)
                dsb = ds_sc[e]
                dk_acc[e] += lax.dot_general(qes[e], dsb, TN, preferred_element_type=f32)
                dq_acc[e, pl.ds(r0, tq), :] += jnp.dot(dsb, kes[e], preferred_element_type=f32)

        @pl.when(i > j)
        def _():
            step(False)

        @pl.when(i == j)
        def _():
            step(True)

        @pl.when(i == n - 1)
        def _():
            dk0 = dk_acc[0].T
            dk1 = dk_acc[1].T
            dk_ref[...] = jnp.where(lane < 64, dk0, dk1).astype(bf16)
            dks_ref[...] = jnp.where(lane < 64, dk1, dk0)
            dv_ref[...] = dv_acc[...].T.astype(bf16)

        @pl.when(t == nsteps - 1)
        def _():
            lane_t = lax.broadcasted_iota(jnp.int32, (T, 128), 1)
            dq_ref[...] = (jnp.where(lane_t < 64, dq_acc[0], dq_acc[1]) * 0.125).astype(bf16)
            dqs_ref[...] = jnp.where(lane_t < 64, dq_acc[1], dq_acc[0])

    qmap = lambda p, t, it, jt: (it[t], p)
    grid_spec = pltpu.PrefetchScalarGridSpec(
        num_scalar_prefetch=2, grid=(PAIRS, nsteps),
        in_specs=[pl.BlockSpec((tq, 128), qmap),
                  pl.BlockSpec((tk, 128), lambda p, t, it, jt: (jt[t], PAIRS + p)),
                  pl.BlockSpec((tk, 128), lambda p, t, it, jt: (jt[t], 2 * PAIRS + p)),
                  pl.BlockSpec((tq, 128), qmap), pl.BlockSpec((tk, 128), lambda p, t, it, jt: (jt[t], p)),
                  pl.BlockSpec((tq, 128), qmap), pl.BlockSpec((tq, 128), qmap),
                  pl.BlockSpec((None, tq, 128), lambda p, t, it, jt: (p, it[t], 0)),
                  pl.BlockSpec((8, 128), lambda p, t, it, jt: (0, 0))],
        out_specs=[pl.BlockSpec((T, 128), lambda p, t, it, jt: (0, p)),
                   pl.BlockSpec((None, T, 128), lambda p, t, it, jt: (p, 0, 0)),
                   pl.BlockSpec((tk, 128), lambda p, t, it, jt: (jt[t], p)),
                   pl.BlockSpec((None, tk, 128), lambda p, t, it, jt: (p, jt[t], 0)),
                   pl.BlockSpec((tk, 128), lambda p, t, it, jt: (jt[t], p))],
        scratch_shapes=[pltpu.VMEM((2, T, 128), f32), pltpu.VMEM((2, 128, tk), f32), pltpu.VMEM((128, tk), f32),
                        pltpu.VMEM((2, tq, tk), f32), pltpu.VMEM((2, tq, tk), f32), pltpu.VMEM((2, tq, tk), bf16),
                        pltpu.VMEM((2, tq, tk), bf16)],
    )
    return pl.pallas_call(
        body, name="fox_attn_bwd", grid_spec=grid_spec,
        out_shape=[jax.ShapeDtypeStruct((T, AW), bf16), jax.ShapeDtypeStruct((PAIRS, T, 128), f32),
                   jax.ShapeDtypeStruct((T, AW), bf16), jax.ShapeDtypeStruct((PAIRS, T, 128), f32),
                   jax.ShapeDtypeStruct((T, AW), bf16)],
        compiler_params=_cp("parallel", "arbitrary"),
    )(it, jt, qkv, qkv, qkv, qaug, kaug, attn, dattn, lse, dep)


def _fox_cumsum_bwd(dqs, dks, fl, bfp):
    tb = CUMSUM_ROWS
    nb = T // tb
    hp = lax.Precision.HIGHEST

    def body(dqs_ref, dks_ref, fl_ref, b_ref, df_ref, db_ref, carry):
        i = pl.program_id(0)

        @pl.when(i == 0)
        def _():
            carry[...] = jnp.zeros_like(carry)
            db_ref[...] = jnp.zeros_like(db_ref)

        r = lax.broadcasted_iota(jnp.int32, (128, 128), 0)
        cc = lax.broadcasted_iota(jnp.int32, (128, 128), 1)
        pick = lambda even_lane, odd_lane, p: jnp.logical_or(
            jnp.logical_and(r == even_lane, cc == 2 * p), jnp.logical_and(r == odd_lane, cc == 2 * p + 1)).astype(f32)
        dc = jnp.zeros((tb, 128), f32)
        for p in range(PAIRS):
            dc = dc + jnp.dot(dqs_ref[p], pick(64, 0, p), precision=hp, preferred_element_type=f32)
            dc = dc - jnp.dot(dks_ref[p], pick(67, 3, p), precision=hp, preferred_element_type=f32)
        rt = lax.broadcasted_iota(jnp.int32, (tb, tb), 0)
        ct = lax.broadcasted_iota(jnp.int32, (tb, tb), 1)
        utri = (ct >= rt).astype(f32)
        dl = jnp.dot(utri, dc, precision=hp, preferred_element_type=f32) + carry[0:1, :]
        carry[...] = jnp.broadcast_to(dl[0:1, :], (8, 128))
        z = fl_ref[...] + b_ref[...]
        df = dl * jax.nn.sigmoid(-z)
        df_ref[...] = df.astype(bf16)
        db_ref[...] += jnp.sum(df, axis=0, keepdims=True)

    rev = lambda i: (nb - 1 - i, 0)
    return pl.pallas_call(
        body, name="fox_cumsum_bwd", grid=(nb,),
        in_specs=[pl.BlockSpec((PAIRS, tb, 128), lambda i: (0, nb - 1 - i, 0)),
                  pl.BlockSpec((PAIRS, tb, 128), lambda i: (0, nb - 1 - i, 0)),
                  pl.BlockSpec((tb, 128), rev), _full((1, 128))],
        out_specs=[pl.BlockSpec((tb, 128), rev), _full((1, 128))],
        out_shape=[jax.ShapeDtypeStruct((T, 128), bf16), jax.ShapeDtypeStruct((1, 128), f32)],
        scratch_shapes=[pltpu.VMEM((8, 128), f32)],
        compiler_params=_cp("arbitrary"),
    )(dqs, dks, fl, bfp)


def _inproj_bwd(dq, dk, dv, du, df, wm, wf, x, dx1, g1):
    tm = 512

    def body(dq_ref, dk_ref, dv_ref, du_ref, df_ref, wm_ref, wf_ref, x_ref, dx1_ref, g_ref, dx_ref, dn_ref):
        i = pl.program_id(0)

        @pl.when(i == 0)
        def _():
            dn_ref[...] = jnp.zeros_like(dn_ref)

        dh = jnp.dot(dq_ref[...], wm_ref[0:AW, :], preferred_element_type=f32)
        dh = dh + jnp.dot(dk_ref[...], wm_ref[AW:2 * AW, :], preferred_element_type=f32)
        dh = dh + jnp.dot(dv_ref[...], wm_ref[2 * AW:3 * AW, :], preferred_element_type=f32)
        dh = dh + jnp.dot(du_ref[...], wm_ref[3 * AW:4 * AW, :], preferred_element_type=f32)
        dh = dh + jnp.dot(df_ref[...], wf_ref[...], preferred_element_type=f32)
        xv = x_ref[...]
        r = lax.rsqrt(jnp.mean(xv * xv, axis=-1, keepdims=True) + EPS)
        xhat = xv * r
        dn_ref[...] += jnp.sum(dh * xhat, axis=0, keepdims=True)
        z = dh * g_ref[...]
        dx_ref[...] = dx1_ref[...] + r * (z - xhat * jnp.mean(z * xhat, axis=-1, keepdims=True))

    row = lambda i: (i, 0)
    return pl.pallas_call(
        body, name="inproj_bwd", grid=(T // tm,),
        in_specs=[pl.BlockSpec((tm, AW), row)] * 4 + [pl.BlockSpec((tm, 128), row), _full((4 * AW, D)), _full((128, D)),
                                                       pl.BlockSpec((tm, D), row), pl.BlockSpec((tm, D), row), _full((1, D))],
        out_specs=[pl.BlockSpec((tm, D), row), _full((1, D))],
        out_shape=[jax.ShapeDtypeStruct((T, D), f32), jax.ShapeDtypeStruct((1, D), f32)],
        compiler_params=_cp("arbitrary"),
    )(dq, dk, dv, du, df, wm, wf, x, dx1, g1)


def _adamw_math(w, g, m, v):
    m = B1 * m + (1.0 - B1) * g
    v = B2 * v + (1.0 - B2) * (g * g)
    m_hat = m / (1.0 - B1 ** STEP)
    v_hat = v / (1.0 - B2 ** STEP)
    delta = -LR * (m_hat / (jnp.sqrt(v_hat) + AEPS) + WD * w)
    return delta, m, v


def _adamw_shard(w, m, v, p_mine, p_other, name):
    rows, cols = w.shape
    tr = rows if rows <= IN_S else rows // 2

    def body(w_ref, m_ref, v_ref, a_ref, b_ref, g_ref, d_ref, nm_ref, nv_ref):
        g = a_ref[...].astype(f32) + b_ref[...].astype(f32)
        g_ref[...] = g
        d_ref[...], nm_ref[...], nv_ref[...] = _adamw_math(w_ref[...], g, m_ref[...], v_ref[...])

    spec = pl.BlockSpec((tr, cols), lambda i: (i, 0))
    return pl.pallas_call(
        body, name=name, grid=(rows // tr,), in_specs=[spec] * 5, out_specs=[spec] * 4,
        out_shape=[jax.ShapeDtypeStruct((rows, cols), f32)] * 4, compiler_params=_cp("parallel"),
    )(w, m, v, p_mine, p_other)


SMALL_SLOTS = ((0, 8, 128), (8, 16, 128), (16, 24, 128), (24, 28, 128), (32, 33, 8), (40, 552, 128))
LOSS_ROW = 39


def _adamw_small(ws, ms, vs, parts):
    n = len(ws)

    def body(*refs):
        w_refs, m_refs, v_refs, p_ref = refs[0:n], refs[n:2 * n], refs[2 * n:3 * n], refs[3 * n]
        outs = refs[3 * n + 1:]
        g_all = p_ref[0]
        for k in range(1, 8):
            g_all = g_all + p_ref[k]
        for idx, (r0, r1, lanes) in enumerate(SMALL_SLOTS):
            g = g_all[r0:r1, 0:lanes]
            d, nm, nv = _adamw_math(w_refs[idx][...], g, m_refs[idx][...], v_refs[idx][...])
            outs[idx][...] = g
            outs[n + idx][...] = d
            outs[2 * n + idx][...] = nm
            outs[3 * n + idx][...] = nv
        outs[4 * n][...] = g_all[LOSS_ROW:LOSS_ROW + 1, :]

    shapes = [jax.ShapeDtypeStruct(w.shape, f32) for w in ws]
    res = pl.pallas_call(
        body, name="adamw_small", out_shape=shapes * 4 + [jax.ShapeDtypeStruct((1, 128), f32)],
    )(*ws, *ms, *vs, parts)
    return res[:4 * n], res[4 * n]


def _sum4(recv, g, mine, name):
    _, rows, cols = recv.shape
    tr = rows if rows <= IN_S else rows // 2

    def body(mine_ref, r_ref, g_ref, o_ref):
        o_ref[...] = ((g_ref[...].astype(f32) + r_ref[0].astype(f32))
                      + (r_ref[1].astype(f32) + r_ref[2].astype(f32))).astype(bf16)

    grid_spec = pltpu.PrefetchScalarGridSpec(
        num_scalar_prefetch=1, grid=(rows // tr,),
        in_specs=[pl.BlockSpec((3, tr, cols), lambda i, m: (0, i, 0)),
                  pl.BlockSpec((None, tr, cols), lambda i, m: (m[0], i, 0))],
        out_specs=pl.BlockSpec((tr, cols), lambda i, m: (i, 0)))
    return pl.pallas_call(
        body, name=name, grid_spec=grid_spec, out_shape=jax.ShapeDtypeStruct((rows, cols), bf16),
        compiler_params=_cp("arbitrary"),
    )(mine, recv, g)


_HBM = pl.BlockSpec(memory_space=pltpu.HBM)
_SEM = pl.BlockSpec(memory_space=pltpu.SEMAPHORE)
_EFFECT = pltpu.SideEffectType.DATAFLOW_SIDE_EFFECTING


def _in_hbm(a):
    return pltpu.with_memory_space_constraint(a, pltpu.HBM)


def _mesh_pos():
    return lax.axis_index("x"), lax.axis_index("y"), lax.axis_index("c")


def _other_chips(x, y):
    return [(1 - x, y), (x, 1 - y), (1 - x, 1 - y)]


def _gather_copy(srcs, lands, send_sems, recv_sems, a, k, slot):
    x, y, c = _mesh_pos()
    cx, cy = _other_chips(x, y)[k]
    return pltpu.make_async_remote_copy(
        src_ref=srcs[a], dst_ref=lands[a].at[slot], send_sem=send_sems.at[3 * a + k], recv_sem=recv_sems.at[3 * a + k],
        device_id=(cx, cy, c), device_id_type=MESH)


def _scatter_copy(srcs, lands, send_sems, recv_sems, a, k):
    x, y, c = _mesh_pos()
    cx, cy = _other_chips(x, y)[k]
    return pltpu.make_async_remote_copy(
        src_ref=srcs[a].at[2 * cx + cy], dst_ref=lands[a].at[k], send_sem=send_sems.at[3 * a + k],
        recv_sem=recv_sems.at[3 * a + k], device_id=(cx, cy, c), device_id_type=MESH)


def _all_gather_w_in(part):
    cols = part.shape[1] // 2

    def body(src, dst, send_sems, recv_sems, loc_sem):
        x, y, c = _mesh_pos()
        mine = 2 * x + y
        chips = _other_chips(x, y)
        half = lambda ref, cc: ref.at[:, pl.ds(pl.multiple_of(cc * cols, cols), cols)]

        def over_ici(k, slot):
            cx, cy = chips[k]
            return pltpu.make_async_remote_copy(
                src_ref=half(src, c), dst_ref=half(dst.at[slot], c), send_sem=send_sems.at[k], recv_sem=recv_sems.at[k],
                device_id=(cx, cy, c), device_id_type=MESH)

        def to_sibling(k, cc):
            slot = 2 * chips[k][0] + chips[k][1]
            return pltpu.make_async_remote_copy(
                src_ref=half(dst.at[slot], cc), dst_ref=half(dst.at[slot], cc), send_sem=send_sems.at[3 + k],
                recv_sem=recv_sems.at[3 + k], device_id=(x, y, 1 - c), device_id_type=MESH)

        local = pltpu.make_async_copy(src, dst.at[mine], loc_sem.at[0])
        local.start()
        first = [over_ici(k, mine) for k in range(3)]
        for cp in first:
            cp.start()
        passed = [to_sibling(k, c) for k in range(3)]
        for k in range(3):
            over_ici(k, 2 * chips[k][0] + chips[k][1]).wait_recv()
            passed[k].start()
        for k in range(3):
            to_sibling(k, 1 - c).wait_recv()
        for cp in first + passed:
            cp.wait_send()
        local.wait()

    return pl.pallas_call(
        body, name="all_gather_w_in", in_specs=[_HBM], out_specs=_HBM,
        out_shape=jax.ShapeDtypeStruct((NSH,) + part.shape, part.dtype),
        scratch_shapes=[pltpu.SemaphoreType.DMA((6,)), pltpu.SemaphoreType.DMA((6,)), pltpu.SemaphoreType.DMA((1,))],
    )(part)


def _split_start(name, srcs, lands, n_sems, plan, dep):
    n, nl = len(srcs), len(lands)

    def body(*refs):
        src_refs, land_refs = refs[:n], refs[n:n + nl]
        send_sems, recv_sems = refs[n + nl + 1], refs[n + nl + 2]
        token = refs[-1]
        sends, _ = plan(src_refs, land_refs, send_sems, recv_sems)
        for cp in sends:
            cp.start()
        token[...] = jnp.zeros_like(token)

    outs = pl.pallas_call(
        body, name=name,
        in_specs=[_HBM] * (n + nl) + [pl.BlockSpec(memory_space=pl.ANY)],
        out_specs=[_SEM, _SEM] + [_HBM] * (n + nl) + [pl.BlockSpec(memory_space=pltpu.VMEM)],
        out_shape=[pltpu.SemaphoreType.DMA((n_sems,)), pltpu.SemaphoreType.DMA((n_sems,))]
        + [pltpu.HBM(a.shape, a.dtype) for a in list(srcs) + list(lands)] + [jax.ShapeDtypeStruct((8, 128), f32)],
        input_output_aliases={i: 2 + i for i in range(n + nl)},
        compiler_params=pltpu.CompilerParams(has_side_effects=_EFFECT),
    )(*[_in_hbm(a) for a in list(srcs) + list(lands)], dep)
    return outs[0], outs[1], list(outs[2:2 + n]), list(outs[2 + n:2 + n + nl]), outs[-1]


def _split_wait(name, send_sems, recv_sems, srcs, lands, after, plan):
    n, nl = len(srcs), len(lands)

    def body(*refs):
        src_refs, land_refs = refs[:n], refs[n:n + nl]
        s_sems, r_sems = refs[n + nl], refs[n + nl + 1]
        sends, recvs = plan(src_refs, land_refs, s_sems, r_sems)
        for cp in recvs:
            cp.wait_recv()
        for cp in sends:
            cp.wait_send()

    outs = pl.pallas_call(
        body, name=name,
        in_specs=[_HBM] * (n + nl) + [_SEM, _SEM, pl.BlockSpec(memory_space=pl.ANY)],
        out_specs=[_HBM] * (n + nl),
        out_shape=[pltpu.HBM(a.shape, a.dtype) for a in list(srcs) + list(lands)],
        input_output_aliases={i: i for i in range(n + nl)},
        compiler_params=pltpu.CompilerParams(has_side_effects=_EFFECT),
    )(*srcs, *lands, send_sems, recv_sems, after)
    return list(outs[:n]), list(outs[n:])


def _gather_plan(srcs, lands, ss, rs):
    x, y, _ = _mesh_pos()
    chips = _other_chips(x, y)
    sends = [_gather_copy(srcs, lands, ss, rs, a, k, 2 * x + y) for a in range(len(srcs)) for k in range(3)]
    recvs = [_gather_copy(srcs, lands, ss, rs, a, k, 2 * chips[k][0] + chips[k][1])
             for a in range(len(srcs)) for k in range(3)]
    return sends, recvs


def _scatter_plan(srcs, lands, ss, rs):
    cps = [_scatter_copy(srcs, lands, ss, rs, a, k) for a in range(len(srcs)) for k in range(3)]
    return cps, cps


def _tail_plan(srcs, lands, ss, rs):
    x, y, c = _mesh_pos()
    me = 4 * x + 2 * y + c
    cps = [_scatter_copy(srcs[:1], lands[:1], ss, rs, 0, k) for k in range(3)]
    for f in range(1, 8):
        peer = ((x + (f >> 2)) % 2, (y + ((f >> 1) & 1)) % 2, (c + (f & 1)) % 2)
        cps.append(pltpu.make_async_remote_copy(
            src_ref=srcs[1], dst_ref=lands[1].at[me], send_sem=ss.at[2 + f], recv_sem=rs.at[2 + f],
            device_id=peer, device_id_type=MESH))
    return cps, cps


def _swap_with_sibling(parts, name):
    n = len(parts)

    def body(*refs):
        srcs, dsts = refs[:n], refs[n:2 * n]
        send_sems, recv_sems = refs[2 * n:]
        x, y, c = _mesh_pos()
        cps = [pltpu.make_async_remote_copy(src_ref=srcs[a], dst_ref=dsts[a], send_sem=send_sems.at[a],
                                            recv_sem=recv_sems.at[a], device_id=(x, y, 1 - c), device_id_type=MESH)
               for a in range(n)]
        for cp in cps:
            cp.start()
        for cp in cps:
            cp.wait_recv()
        for cp in cps:
            cp.wait_send()

    return pl.pallas_call(
        body, name=name, in_specs=[_HBM] * n, out_specs=[_HBM] * n,
        out_shape=[jax.ShapeDtypeStruct(p.shape, p.dtype) for p in parts],
        scratch_shapes=[pltpu.SemaphoreType.DMA((n,)), pltpu.SemaphoreType.DMA((n,))],
    )(*parts)


def _forward(x, tgt, wm, wf, mlp_w_fn, g1, bfp, wp, scale, g2, gf, dep):
    h, qkv, u, fl = _rms_inproj(x, g1, wm, wf, dep)
    qaug, kaug = _fox_cumsum(fl, bfp)
    attn, lse = _attn_fwd(qkv, qaug, kaug)
    pooled, pool = _pool_fwd(u, wp, scale)
    wo, wgt, wut, wd = mlp_w_fn(attn)
    x1, h2 = _outproj(x, attn, pool, wo, g2)
    loss, dgf, dx2, dx2b, ud, silu, a_b = _mlp_fwd_loss(h2, x1, wgt, wut, wd, tgt, gf)
    saved = dict(h=h, qkv=qkv, fl=fl, qaug=qaug, kaug=kaug, attn=attn, lse=lse, pooled=pooled, pool=pool, x1=x1, h2=h2,
                 ud=ud, silu=silu, a_b=a_b, wo=wo, wgt=wgt, wut=wut, wd=wd)
    return loss, dgf, dx2, dx2b, saved


def _backward_mlp(sv, dx2, dx2b, g2):
    dgate, dup, dx1, dx1b, dg2 = _mlp_bwd(dx2b, dx2, sv["ud"], sv["silu"], sv["wgt"], sv["wut"], sv["wd"], sv["x1"], g2)
    (dwd,) = _mm_tn(sv["a_b"], [dx2b], "dw_down", a_sharded=True, tk=T)
    (dwgt,) = _mm_tn(dgate, [sv["h2"]], "dw_gate", a_sharded=True, tk=T)
    (dwut,) = _mm_tn(dup, [sv["h2"]], "dw_up", a_sharded=True, tk=T)
    return dx1, dx1b, dg2, (dwgt, dwut, dwd)


def _backward_outproj(sv, dx1b):
    dattn, dpool = _outproj_bwd(dx1b, sv["wo"])
    dwo_a, = _mm_tn(sv["attn"], [dx1b], "dw_out_attn", tk=2048)
    dwo_p, = _mm_tn(sv["pool"], [dx1b], "dw_out_pool", tk=2048)
    dwo = jnp.concatenate([dwo_a, dwo_p], axis=0).reshape(NSH, D // NSH, D)
    return dattn, dpool, dwo


def _backward_mixer(sv, x, dx1, dattn, dpool, wm, wf, g1, bfp, wp, scale, dep):
    du, dscale, dwp = _pool_bwd(dpool, sv["pooled"], wp, scale, dep)
    dq, dqs, dk, dks, dv = _attn_bwd(sv["qkv"], sv["qaug"], sv["kaug"], sv["attn"], dattn, sv["lse"], dep)
    df, dbf = _fox_cumsum_bwd(dqs, dks, sv["fl"], bfp)
    dx, dg1 = _inproj_bwd(dq, dk, dv, du, df, wm, wf, x, dx1, g1)
    dwq, dwk, dwv, dwu_in, dwf = _mm_tn_rows([dq, dk, dv, du, df], sv["h"], "dw_in")
    dwin = jnp.concatenate([dwq, dwk, dwv, dwf[0:8], dwu_in], axis=0)
    return dx, dg1, dscale, dwp, dbf, dwin.reshape(NSH, IN_S, D)


def kernel(x, norm1_g, w_in, b_forget, w_pool, pool_scale, w_out, norm2_g, w_gate, w_up, w_down, final_g, loss_target, m_norm1_g, m_w_in, m_b_forget, m_w_pool, m_pool_scale, m_w_out, m_norm2_g, m_w_gate, m_w_up, m_w_down, m_final_g, v_norm1_g, v_w_in, v_b_forget, v_w_pool, v_pool_scale, v_w_out, v_norm2_g, v_w_gate, v_w_up, v_w_down, v_final_g):
    mine = (2 * lax.axis_index("x") + lax.axis_index("y")).astype(jnp.int32)
    mine1 = mine.reshape(1)
    tr = lambda a: jnp.transpose(a[0])

    win4 = _all_gather_w_in(tr(w_in).astype(bf16))
    later = [w_out[0].astype(bf16), tr(w_gate).astype(bf16), tr(w_up).astype(bf16), w_down[0].astype(bf16)]
    lands = [lax.dynamic_update_slice(lax.empty((NSH,) + p.shape, bf16), p[None], (mine, 0, 0)) for p in later]
    ag_send, ag_recv, later_thru, lands_thru, ag_token = _split_start("all_gather_start", later, lands, 12, _gather_plan,
                                                                      win4)
    win = win4.reshape(IN_W, D)
    wm = jnp.concatenate([win[0:3 * AW], win[3 * AW + 8:]], axis=0)
    wf = jnp.pad(win[3 * AW:3 * AW + 8], ((0, 120), (0, 0)))
    bfp = jnp.pad(b_forget, ((0, 0), (0, 120)))
    wp = w_pool[0].astype(bf16)
    gf = final_g.reshape(1, D)

    def later_weights(after):
        _, (wo4, wgt, wut, wd) = _split_wait("all_gather_wait", ag_send, ag_recv, later_thru, lands_thru, after, _gather_plan)
        return wo4.reshape(D, D), wgt, wut, wd

    xe, tgt = x[0], loss_target[0]
    loss_v, dgf, dx2, dx2b, sv = _forward(xe, tgt, wm, wf, later_weights, norm1_g, bfp, wp, pool_scale, norm2_g, gf, ag_token)
    dx1, dx1b, dg2, mlp_grads = _backward_mlp(sv, dx2, dx2b, norm2_g)
    dattn, dpool, dwo = _backward_outproj(sv, dx1b)
    first = [dwo] + list(mlp_grads)
    first_lands = [lax.empty((3,) + g.shape[1:], bf16) for g in first]
    rs_send, rs_recv, first_thru, first_lands_thru, rs_token = _split_start("reduce_scatter_start", first, first_lands, 12,
                                                                            _scatter_plan, dattn)
    dx, dg1, dscale, dwp, dbf, dwin = _backward_mixer(sv, xe, dx1, dattn, dpool, wm, wf, norm1_g, bfp, wp, pool_scale, rs_token)

    me = (4 * lax.axis_index("x") + 2 * lax.axis_index("y") + lax.axis_index("c")).astype(jnp.int32)
    pad8 = lambda r: jnp.pad(r, ((0, 8 - r.shape[0]), (0, 0)))
    loss_rows = jnp.concatenate([dbf, jnp.zeros((6, 128), f32), loss_v[0:1, :]], axis=0)
    small = jnp.concatenate([dg1.reshape(8, 128), dg2.reshape(8, 128), dgf.reshape(8, 128), pad8(dscale.reshape(4, 128)),
                             loss_rows, dwp.reshape(512, 128)], axis=0)
    small_land = lax.dynamic_update_slice(lax.empty((8, SMALL_ROWS, 128), f32), small[None], (me, 0, 0))
    tail_send, tail_recv, tail_thru, tail_lands_thru, tail_token = _split_start(
        "tail_start", [dwin, small], [lax.empty((3,) + dwin.shape[1:], bf16), small_land], 10, _tail_plan, dx)
    first_thru, first_recv = _split_wait("reduce_scatter_wait", rs_send, rs_recv, first_thru, first_lands_thru, tail_token,
                                         _scatter_plan)
    ws = [tr(w_in), w_out[0], tr(w_gate), tr(w_up), w_down[0]]
    ms = [tr(m_w_in), m_w_out[0], tr(m_w_gate), tr(m_w_up), m_w_down[0]]
    vs = [tr(v_w_in), v_w_out[0], tr(v_w_gate), tr(v_w_up), v_w_down[0]]
    partial = [_sum4(r, g, mine1, f"sum4_{i + 1}") for i, (r, g) in enumerate(zip(first_recv, first_thru))]
    other = _swap_with_sibling(partial, "swap_first")
    big = [_adamw_shard(ws[i + 1], ms[i + 1], vs[i + 1], partial[i], other[i], f"adamw_{i + 1}") for i in range(4)]
    (dwin_thru, _), (in_recv_land, small_all) = _split_wait("tail_wait", tail_send, tail_recv, tail_thru, tail_lands_thru,
                                                            big[3][0], _tail_plan)
    partial_in = _sum4(in_recv_land, dwin_thru, mine1, "sum4_0")
    (other_in,) = _swap_with_sibling([partial_in], "swap_in")
    big = [_adamw_shard(ws[0], ms[0], vs[0], partial_in, other_in, "adamw_0")] + big

    small_names = ["norm1_g", "norm2_g", "final_g", "pool_scale", "b_forget", "w_pool"]
    rows = lambda a, b, c, d, e, f: [a.reshape(8, 128), b.reshape(8, 128), c.reshape(8, 128), d.reshape(4, 128),
                                     e.reshape(1, 8), f.reshape(512, 128)]
    sm, loss_row = _adamw_small(rows(norm1_g, norm2_g, final_g, pool_scale, b_forget, w_pool),
                                rows(m_norm1_g, m_norm2_g, m_final_g, m_pool_scale, m_b_forget, m_w_pool),
                                rows(v_norm1_g, v_norm2_g, v_final_g, v_pool_scale, v_b_forget, v_w_pool), small_all)
    small_shape = dict(norm1_g=(1, D), norm2_g=(1, D), final_g=(D,), pool_scale=(1, AW), b_forget=(1, 8),
                       w_pool=(1, 4, 128, 128))

    order = ["norm1_g", "w_in", "b_forget", "w_pool", "pool_scale", "w_out", "norm2_g", "w_gate", "w_up", "w_down", "final_g"]
    big_idx = {"w_in": 0, "w_out": 1, "w_gate": 2, "w_up": 3, "w_down": 4}
    outs = [loss_row[0, 0], dx[None]]
    for kind in range(4):
        for name in order:
            if name in ("w_in", "w_gate", "w_up"):
                outs.append(jnp.transpose(big[big_idx[name]][kind])[None])
            elif name in big_idx:
                outs.append(big[big_idx[name]][kind][None])
            else:
                outs.append(sm[6 * kind + small_names.index(name)].reshape(small_shape[name]))
    return tuple(outs)
```

```python
import functools

import jax
import jax.numpy as jnp
import numpy as np
from jax import lax
from jax.experimental import pallas as pl
from jax.experimental.pallas import tpu as pltpu

f32 = jnp.float32
bf16 = jnp.bfloat16

T = 4096
D = 1024
NSH = 4
IN_W = 2056
IN_S = IN_W // NSH
AW = 512
PAIRS = 4
FF = 2816
FS = FF // NSH
WINDOWS = (2, 4, 8, 16)
HALO = 16
EPS = 1e-6
NEG = -1e30
LR, B1, B2, AEPS, WD, STEP = 0.001, 0.9, 0.999, 1e-08, 0.01, 10
SMALL_ROWS = 552

NT = (((1,), (1,)), ((), ()))
TN = (((0,), (0,)), ((), ()))

MESH = pl.DeviceIdType.MESH


def _cp(*sem):
    return pltpu.CompilerParams(dimension_semantics=sem)


def _full(shape):
    n = len(shape)
    return pl.BlockSpec(shape, lambda *_: (0,) * n)


def _resident(shape):
    n = len(shape)
    return pl.BlockSpec(shape, lambda *_: (0,) * n, pipeline_mode=pl.Buffered(1))


def _rms_inproj(x, g1, wm, wf, dep):
    tm = 512

    def body(x_ref, g_ref, wm_ref, wf_ref, dep_ref, h_ref, qkv_ref, u_ref, fl_ref):
        xv = x_ref[...]
        r = lax.rsqrt(jnp.mean(xv * xv, axis=-1, keepdims=True) + EPS)
        h = (xv * r * g_ref[...]).astype(bf16)
        h_ref[...] = h
        qkv_ref[...] = lax.dot_general(h, wm_ref[0:3 * AW, :], NT, preferred_element_type=f32).astype(bf16)
        u_ref[...] = lax.dot_general(h, wm_ref[3 * AW:4 * AW, :], NT, preferred_element_type=f32)
        fl_ref[...] = lax.dot_general(h, wf_ref[...], NT, preferred_element_type=f32)

    return pl.pallas_call(
        body, name="rms_inproj", grid=(T // tm,),
        in_specs=[pl.BlockSpec((tm, D), lambda i: (i, 0)), _full((1, D)), _full((4 * AW, D)), _full((128, D)),
                  _full((8, 128))],
        out_specs=[pl.BlockSpec((tm, D), lambda i: (i, 0)), pl.BlockSpec((tm, 3 * AW), lambda i: (i, 0)),
                   pl.BlockSpec((tm, AW), lambda i: (i, 0)), pl.BlockSpec((tm, 128), lambda i: (i, 0))],
        out_shape=[jax.ShapeDtypeStruct((T, D), bf16), jax.ShapeDtypeStruct((T, 3 * AW), bf16),
                   jax.ShapeDtypeStruct((T, AW), f32), jax.ShapeDtypeStruct((T, 128), f32)],
        compiler_params=_cp("parallel"),
    )(x, g1, wm, wf, dep)


CUMSUM_ROWS = 512
FS_CHUNKS = ((0, 256), (256, 512), (512, FS))


def _log_sigmoid(z):
    return jnp.minimum(z, 0.0) - jnp.log(1.0 + jnp.exp(-jnp.abs(z)))


def _fox_cumsum(fl, bfp):
    tb = CUMSUM_ROWS
    nb = T // tb

    def body(fl_ref, b_ref, qa_ref, ka_ref, carry):
        i = pl.program_id(0)

        @pl.when(i == 0)
        def _():
            carry[...] = jnp.zeros_like(carry)

        lf = _log_sigmoid(fl_ref[...] + b_ref[...])
        r = lax.broadcasted_iota(jnp.int32, (tb, tb), 0)
        cc = lax.broadcasted_iota(jnp.int32, (tb, tb), 1)
        ltri = (cc <= r).astype(f32)
        cb = jnp.dot(ltri, lf, precision=lax.Precision.HIGHEST, preferred_element_type=f32) + carry[0:1, :]
        carry[...] = jnp.broadcast_to(cb[tb - 1:tb, :], (8, 128))
        hi = cb.astype(bf16)
        r1 = cb - hi.astype(f32)
        mid = r1.astype(bf16)
        lo = (r1 - mid.astype(f32)).astype(bf16)
        head = lax.broadcasted_iota(jnp.int32, (128, AW), 0)
        col = lax.broadcasted_iota(jnp.int32, (128, AW), 1)
        base = 128 * (head >> 1) + 64 * (1 - (head & 1))
        place = lambda off: jnp.logical_and(col == base + off, head < 8).astype(bf16)
        mm = lambda a, off: jnp.dot(a, place(off), preferred_element_type=f32)
        cq = mm(hi, 0) + mm(mid, 1) + mm(lo, 2)
        ck = mm(hi, 3) + mm(mid, 4) + mm(lo, 5)
        within = jnp.bitwise_and(lax.broadcasted_iota(jnp.int32, (tb, AW), 1), 63)
        qa_ref[...] = jnp.where(jnp.logical_and(within >= 3, within <= 5), 1.0, cq).astype(bf16)
        ka_ref[...] = jnp.where(within <= 2, 1.0, -ck).astype(bf16)

    return pl.pallas_call(
        body, name="fox_cumsum", grid=(nb,),
        in_specs=[pl.BlockSpec((tb, 128), lambda i: (i, 0)), _full((1, 128))],
        out_specs=[pl.BlockSpec((tb, AW), lambda i: (i, 0)), pl.BlockSpec((tb, AW), lambda i: (i, 0))],
        out_shape=[jax.ShapeDtypeStruct((T, AW), bf16), jax.ShapeDtypeStruct((T, AW), bf16)],
        scratch_shapes=[pltpu.VMEM((8, 128), f32)],
        compiler_params=_cp("arbitrary"),
    )(fl, bfp)


ATT_T = 512


def _causal_steps(key_major):
    n = T // ATT_T
    if key_major:
        pairs = [(i, j) for j in range(n) for i in range(j, n)]
    else:
        pairs = [(i, j) for i in range(n) for j in range(i + 1)]
    it = np.array([p[0] for p in pairs], np.int32)
    jt = np.array([p[1] for p in pairs], np.int32)
    return jnp.asarray(it), jnp.asarray(jt)


def _attn_fwd(qkv, qaug, kaug):
    tq = tk = ATT_T
    it, jt = _causal_steps(False)
    nsteps = it.shape[0]

    rs = 64

    def body(it_ref, jt_ref, q_ref, k_ref, v_ref, qa_ref, ka_ref, o_ref, lse_ref, m_sc, acc_sc, s_sc, p_sc, alpha_sc):
        t = pl.program_id(1)
        i = it_ref[t]
        j = jt_ref[t]

        @pl.when(j == 0)
        def _():
            m_sc[...] = jnp.full_like(m_sc, NEG)
            acc_sc[...] = jnp.zeros_like(acc_sc)

        lane = lax.broadcasted_iota(jnp.int32, (tq, 128), 1)
        spare = (64, 0)

        def step(on_diagonal):
            q = q_ref[...] * 0.125
            k = k_ref[...]
            v = v_ref[...]
            qa = qa_ref[...]
            ka = ka_ref[...]
            for e in range(2):
                hm = (lane >= 64) if e else (lane < 64)
                s_sc[e] = lax.dot_general(jnp.where(hm, q, qa), jnp.where(hm, k, ka), NT, preferred_element_type=f32)
            for e in range(2):
                hm = (lane >= 64) if e else (lane < 64)
                for r in range(0, tq, rs):
                    s = s_sc[e, r:r + rs, :]
                    if on_diagonal:
                        row = lax.broadcasted_iota(jnp.int32, (rs, tk), 0) + r
                        col = lax.broadcasted_iota(jnp.int32, (rs, tk), 1)
                        s = jnp.where(col <= row, s, NEG)
                    m_prev = m_sc[e, r:r + rs, :]
                    m_new = jnp.maximum(m_prev, jnp.max(s, axis=1, keepdims=True))
                    p_sc[e, r:r + rs, :] = jnp.exp(s - jnp.tile(m_new, (1, tk // 128))).astype(bf16)
                    alpha_sc[e, r:r + rs, :] = jnp.exp(m_prev - m_new)
                    m_sc[e, r:r + rs, :] = m_new
            for e in range(2):
                hm = (lane >= 64) if e else (lane < 64)
                ve = jnp.where(hm, v, (lane == spare[e]).astype(bf16))
                acc_sc[e] = alpha_sc[e] * acc_sc[e] + jnp.dot(p_sc[e], ve, preferred_element_type=f32)

        @pl.when(j < i)
        def _():
            step(False)

        @pl.when(j == i)
        def _():
            step(True)
            l0 = acc_sc[0][:, spare[0]:spare[0] + 1]
            l1 = acc_sc[1][:, spare[1]:spare[1] + 1]
            o_ref[...] = jnp.where(lane < 64, acc_sc[0] / l0, acc_sc[1] / l1).astype(bf16)
            lse_ref[...] = jnp.where(lane < 64, m_sc[0] + jnp.log(l0), m_sc[1] + jnp.log(l1))

    qmap = lambda p, t, it, jt: (it[t], p)
    kmap = lambda p, t, it, jt: (jt[t], p)
    grid_spec = pltpu.PrefetchScalarGridSpec(
        num_scalar_prefetch=2, grid=(PAIRS, nsteps),
        in_specs=[pl.BlockSpec((tq, 128), qmap),
                  pl.BlockSpec((tk, 128), lambda p, t, it, jt: (jt[t], PAIRS + p)),
                  pl.BlockSpec((tk, 128), lambda p, t, it, jt: (jt[t], 2 * PAIRS + p)),
                  pl.BlockSpec((tq, 128), qmap), pl.BlockSpec((tk, 128), kmap)],
        out_specs=[pl.BlockSpec((tq, 128), qmap),
                   pl.BlockSpec((None, tq, 128), lambda p, t, it, jt: (p, it[t], 0))],
        scratch_shapes=[pltpu.VMEM((2, tq, 128), f32), pltpu.VMEM((2, tq, 128), f32), pltpu.VMEM((2, tq, tk), f32),
                        pltpu.VMEM((2, tq, tk), bf16), pltpu.VMEM((2, tq, 128), f32)],
    )
    return pl.pallas_call(
        body, name="fox_attn_fwd", grid_spec=grid_spec,
        out_shape=[jax.ShapeDtypeStruct((T, AW), bf16), jax.ShapeDtypeStruct((PAIRS, T, 128), f32)],
        compiler_params=_cp("parallel", "arbitrary"),
    )(it, jt, qkv, qkv, qkv, qaug, kaug)


def _pool_fwd(u, wp, scale):
    tm = 512

    def body(u_ref, wp_ref, sc_ref, pooled_ref, pool_ref, ext):
        i = pl.program_id(0)

        @pl.when(i == 0)
        def _():
            ext[0:HALO, :] = jnp.zeros((HALO, AW), f32)

        uv = u_ref[...]
        ext[HALO:HALO + tm, :] = uv
        t_idx = i * tm + lax.broadcasted_iota(jnp.int32, (tm, 1), 0)
        for g, w in enumerate(WINDOWS):
            lo, hi = 128 * g, 128 * (g + 1)
            ug = uv[:, lo:hi]
            acc = ug
            for d in range(1, w):
                acc = acc + ext[HALO - d:HALO - d + tm, lo:hi]
            cnt = jnp.minimum(t_idx + 1, w).astype(f32)
            pb = (acc / cnt - ug).astype(bf16)
            pooled_ref[:, lo:hi] = pb
            mixed = jnp.dot(pb, wp_ref[g], preferred_element_type=f32)
            pool_ref[:, lo:hi] = (mixed * sc_ref[:, lo:hi]).astype(bf16)
        ext[0:HALO, :] = uv[tm - HALO:tm, :]

    return pl.pallas_call(
        body, name="pool_fwd", grid=(T // tm,),
        in_specs=[pl.BlockSpec((tm, AW), lambda i: (i, 0)), _full((4, 128, 128)), _full((1, AW))],
        out_specs=[pl.BlockSpec((tm, AW), lambda i: (i, 0)), pl.BlockSpec((tm, AW), lambda i: (i, 0))],
        out_shape=[jax.ShapeDtypeStruct((T, AW), bf16), jax.ShapeDtypeStruct((T, AW), bf16)],
        scratch_shapes=[pltpu.VMEM((tm + HALO, AW), f32)],
        compiler_params=_cp("arbitrary"),
    )(u, wp, scale)


def _outproj(x, attn, pool, wo, g2):
    tm = 512

    def body(x_ref, a_ref, p_ref, wo_ref, g_ref, x1_ref, h2_ref):
        x1 = x_ref[...] + jnp.dot(a_ref[...], wo_ref[0:AW, :], preferred_element_type=f32)
        x1 = x1 + jnp.dot(p_ref[...], wo_ref[AW:2 * AW, :], preferred_element_type=f32)
        x1_ref[...] = x1
        r = lax.rsqrt(jnp.mean(x1 * x1, axis=-1, keepdims=True) + EPS)
        h2_ref[...] = (x1 * r * g_ref[...]).astype(bf16)

    return pl.pallas_call(
        body, name="outproj", grid=(T // tm,),
        in_specs=[pl.BlockSpec((tm, D), lambda i: (i, 0)), pl.BlockSpec((tm, AW), lambda i: (i, 0)),
                  pl.BlockSpec((tm, AW), lambda i: (i, 0)), _full((D, D)), _full((1, D))],
        out_specs=[pl.BlockSpec((tm, D), lambda i: (i, 0)), pl.BlockSpec((tm, D), lambda i: (i, 0))],
        out_shape=[jax.ShapeDtypeStruct((T, D), f32), jax.ShapeDtypeStruct((T, D), bf16)],
        compiler_params=_cp("parallel"),
    )(x, attn, pool, wo, g2)


def _mlp_fwd_loss(h2, x1, wg, wu, wd, tgt, gf):
    tm = 512

    def body(h_ref, x1_ref, wg_ref, wu_ref, wd_ref, t_ref, g_ref,
             loss_ref, dg_ref, dx_ref, dxb_ref, ud_ref, silu_ref, a_ref, x2):
        i = pl.program_id(0)
        s = pl.program_id(1)

        @pl.when(jnp.logical_and(i == 0, s == 0))
        def _():
            loss_ref[...] = jnp.zeros_like(loss_ref)
            dg_ref[...] = jnp.zeros_like(dg_ref)

        h = h_ref[...]
        gus = [(lax.dot_general(h, wg_ref[s, c0:c1, :], NT, preferred_element_type=f32),
                lax.dot_general(h, wu_ref[s, c0:c1, :], NT, preferred_element_type=f32)) for c0, c1 in FS_CHUNKS]
        for (c0, c1), (gate, up) in zip(FS_CHUNKS, gus):
            sg = jax.nn.sigmoid(gate)
            silu = gate * sg
            ud_ref[:, c0:c1] = (up * (sg * (1.0 + gate * (1.0 - sg)))).astype(bf16)
            silu_ref[:, c0:c1] = silu.astype(bf16)
            a_ref[:, c0:c1] = (silu * up).astype(bf16)
        part = jnp.dot(a_ref[...], wd_ref[s], preferred_element_type=f32)

        @pl.when(s == 0)
        def _():
            x2[...] = x1_ref[...] + part

        @pl.when(s > 0)
        def _():
            x2[...] += part

        @pl.when(s == NSH - 1)
        def _():
            xv = x2[...]
            g = g_ref[...]
            r = lax.rsqrt(jnp.mean(xv * xv, axis=-1, keepdims=True) + EPS)
            xhat = xv * r
            e = xhat * g - t_ref[...]
            loss_ref[...] += 0.5 * jnp.sum(jnp.mean(e * e, axis=-1, keepdims=True))
            dy = e * (1.0 / D)
            dg_ref[...] += jnp.sum(dy * xhat, axis=0, keepdims=True)
            z = dy * g
            dx = r * (z - xhat * jnp.mean(z * xhat, axis=-1, keepdims=True))
            dx_ref[...] = dx
            dxb_ref[...] = dx.astype(bf16)

    row = lambda i, s: (i, 0)
    sl = lambda i, s: (s, i, 0)
    wsl = lambda i, s: (s, 0, 0)
    return pl.pallas_call(
        body, name="mlp_fwd_loss", grid=(T // tm, NSH),
        in_specs=[pl.BlockSpec((tm, D), row), pl.BlockSpec((tm, D), row),
                  _resident((NSH, FS, D)), _resident((NSH, FS, D)), _resident((NSH, FS, D)),
                  pl.BlockSpec((tm, D), row), pl.BlockSpec((1, D), lambda i, s: (0, 0))],
        out_specs=[pl.BlockSpec((8, 128), lambda i, s: (0, 0)), pl.BlockSpec((1, D), lambda i, s: (0, 0)),
                   pl.BlockSpec((tm, D), row), pl.BlockSpec((tm, D), row),
                   pl.BlockSpec((None, tm, FS), sl), pl.BlockSpec((None, tm, FS), sl), pl.BlockSpec((None, tm, FS), sl)],
        out_shape=[jax.ShapeDtypeStruct((8, 128), f32), jax.ShapeDtypeStruct((1, D), f32),
                   jax.ShapeDtypeStruct((T, D), f32), jax.ShapeDtypeStruct((T, D), bf16)]
        + [jax.ShapeDtypeStruct((NSH, T, FS), bf16)] * 3,
        scratch_shapes=[pltpu.VMEM((tm, D), f32)],
        compiler_params=_cp("arbitrary", "arbitrary"),
    )(h2, x1, wg, wu, wd, tgt, gf)


def _mlp_bwd(dx2b, dx2, ud, silu, wg, wu, wd, x1, g2):
    tm = 512

    def body(dxb_ref, dx_ref, ud_ref, silu_ref, wg_ref, wu_ref, wd_ref, x1_ref, g_ref,
             dg_ref, du_ref, dx1_ref, dx1b_ref, dn_ref, acc):
        i = pl.program_id(0)
        s = pl.program_id(1)

        @pl.when(jnp.logical_and(i == 0, s == 0))
        def _():
            dn_ref[...] = jnp.zeros_like(dn_ref)

        dxb = dxb_ref[...]
        das = [lax.dot_general(dxb, wd_ref[s, c0:c1, :], NT, preferred_element_type=f32) for c0, c1 in FS_CHUNKS]
        for (c0, c1), da in zip(FS_CHUNKS, das):
            dg_ref[:, c0:c1] = (da * ud_ref[:, c0:c1].astype(f32)).astype(bf16)
            du_ref[:, c0:c1] = (da * silu_ref[:, c0:c1].astype(f32)).astype(bf16)
        part = jnp.dot(dg_ref[...], wg_ref[s], preferred_element_type=f32)
        part = part + jnp.dot(du_ref[...], wu_ref[s], preferred_element_type=f32)

        @pl.when(s == 0)
        def _():
            acc[...] = part

        @pl.when(s > 0)
        def _():
            acc[...] += part

        @pl.when(s == NSH - 1)
        def _():
            xv = x1_ref[...]
            r = lax.rsqrt(jnp.mean(xv * xv, axis=-1, keepdims=True) + EPS)
            xhat = xv * r
            dh = acc[...]
            dn_ref[...] += jnp.sum(dh * xhat, axis=0, keepdims=True)
            z = dh * g_ref[...]
            dx1 = dx_ref[...] + r * (z - xhat * jnp.mean(z * xhat, axis=-1, keepdims=True))
            dx1_ref[...] = dx1
            dx1b_ref[...] = dx1.astype(bf16)

    row = lambda i, s: (i, 0)
    sl = lambda i, s: (s, i, 0)
    wsl = lambda i, s: (s, 0, 0)
    return pl.pallas_call(
        body, name="mlp_bwd", grid=(T // tm, NSH),
        in_specs=[pl.BlockSpec((tm, D), row), pl.BlockSpec((tm, D), row),
                  pl.BlockSpec((None, tm, FS), sl), pl.BlockSpec((None, tm, FS), sl),
                  _resident((NSH, FS, D)), _resident((NSH, FS, D)), _resident((NSH, FS, D)),
                  pl.BlockSpec((tm, D), row), pl.BlockSpec((1, D), lambda i, s: (0, 0))],
        out_specs=[pl.BlockSpec((None, tm, FS), sl), pl.BlockSpec((None, tm, FS), sl),
                   pl.BlockSpec((tm, D), row), pl.BlockSpec((tm, D), row), pl.BlockSpec((1, D), lambda i, s: (0, 0))],
        out_shape=[jax.ShapeDtypeStruct((NSH, T, FS), bf16)] * 2
        + [jax.ShapeDtypeStruct((T, D), f32), jax.ShapeDtypeStruct((T, D), bf16), jax.ShapeDtypeStruct((1, D), f32)],
        scratch_shapes=[pltpu.VMEM((tm, D), f32)],
        compiler_params=_cp("arbitrary", "arbitrary"),
    )(dx2b, dx2, ud, silu, wg, wu, wd, x1, g2)


def _mm_tn(a, bs, name, a_sharded=False, b_sharded=False, tk=512, out_dtype=bf16):
    nb = len(bs)
    sh = NSH if (a_sharded or b_sharded) else 1
    m = a.shape[-1]
    nk = T // tk

    def body(a_ref, *refs):
        kk = pl.program_id(1)
        av = a_ref[...]
        for b_ref, o_ref, acc in zip(refs[:nb], refs[nb:2 * nb], refs[2 * nb:]):
            upd = lax.dot_general(av, b_ref[...], TN, preferred_element_type=f32)

            @pl.when(kk == 0)
            def _():
                acc[...] = upd

            @pl.when(kk > 0)
            def _():
                acc[...] += upd

            @pl.when(kk == nk - 1)
            def _():
                o_ref[...] = acc[...].astype(out_dtype)

    a_spec = (pl.BlockSpec((None, tk, m), lambda s, k: (s, k, 0)) if a_sharded
              else pl.BlockSpec((tk, m), lambda s, k: (k, 0)))
    b_specs, o_specs, o_shapes, scratch = [], [], [], []
    for b in bs:
        n = b.shape[-1]
        b_specs.append(pl.BlockSpec((None, tk, n), lambda s, k: (s, k, 0)) if b_sharded
                       else pl.BlockSpec((tk, n), lambda s, k: (k, 0)))
        scratch.append(pltpu.VMEM((m, n), f32))
        if sh > 1:
            o_specs.append(pl.BlockSpec((None, m, n), lambda s, k: (s, 0, 0)))
            o_shapes.append(jax.ShapeDtypeStruct((sh, m, n), out_dtype))
        else:
            o_specs.append(pl.BlockSpec((m, n), lambda s, k: (0, 0)))
            o_shapes.append(jax.ShapeDtypeStruct((m, n), out_dtype))
    return pl.pallas_call(
        body, name=name, grid=(sh, nk), in_specs=[a_spec] + b_specs, out_specs=o_specs, out_shape=o_shapes,
        scratch_shapes=scratch, compiler_params=_cp("arbitrary", "arbitrary"),
    )(a, *bs)


def _mm_tn_rows(a_list, b, name, tk=1024, out_dtype=bf16):
    na = len(a_list)
    n = b.shape[-1]
    nk = T // tk

    def body(*refs):
        a_refs, b_ref = refs[:na], refs[na]
        o_refs, accs = refs[na + 1:2 * na + 1], refs[2 * na + 1:]
        kk = pl.program_id(0)
        bv = b_ref[...]
        for a_ref, o_ref, acc in zip(a_refs, o_refs, accs):
            upd = lax.dot_general(a_ref[...], bv, TN, preferred_element_type=f32)

            @pl.when(kk == 0)
            def _():
                acc[...] = upd

            @pl.when(kk > 0)
            def _():
                acc[...] += upd

            @pl.when(kk == nk - 1)
            def _():
                o_ref[...] = acc[...].astype(out_dtype)

    return pl.pallas_call(
        body, name=name, grid=(nk,),
        in_specs=[pl.BlockSpec((tk, a.shape[-1]), lambda k: (k, 0)) for a in a_list] + [pl.BlockSpec((tk, n), lambda k: (k, 0))],
        out_specs=[pl.BlockSpec((a.shape[-1], n), lambda k: (0, 0)) for a in a_list],
        out_shape=[jax.ShapeDtypeStruct((a.shape[-1], n), out_dtype) for a in a_list],
        scratch_shapes=[pltpu.VMEM((a.shape[-1], n), f32) for a in a_list],
        compiler_params=_cp("arbitrary"),
    )(*a_list, b)


def _outproj_bwd(dx1b, wo):
    tm = 512

    def body(dx_ref, wo_ref, da_ref, dp_ref):
        dx = dx_ref[...]
        da_ref[...] = lax.dot_general(dx, wo_ref[0:AW, :], NT, preferred_element_type=f32).astype(bf16)
        dp_ref[...] = lax.dot_general(dx, wo_ref[AW:2 * AW, :], NT, preferred_element_type=f32)

    return pl.pallas_call(
        body, name="outproj_bwd", grid=(T // tm,),
        in_specs=[pl.BlockSpec((tm, D), lambda i: (i, 0)), _full((D, D))],
        out_specs=[pl.BlockSpec((tm, AW), lambda i: (i, 0)), pl.BlockSpec((tm, AW), lambda i: (i, 0))],
        out_shape=[jax.ShapeDtypeStruct((T, AW), bf16), jax.ShapeDtypeStruct((T, AW), f32)],
        compiler_params=_cp("parallel"),
    )(dx1b, wo)


def _pool_bwd(dpool, pooled, wp, scale, dep):
    tm = 512
    n = T // tm

    def body(dp_ref, pb_ref, wp_ref, sc_ref, dep_ref, du_ref, dsc_ref, dwp_ref, ext):
        i = pl.program_id(0)

        @pl.when(i == 0)
        def _():
            ext[tm:tm + HALO, :] = jnp.zeros((HALO, AW), f32)
            dsc_ref[...] = jnp.zeros_like(dsc_ref)
            dwp_ref[...] = jnp.zeros_like(dwp_ref)

        t_idx = (n - 1 - i) * tm + lax.broadcasted_iota(jnp.int32, (tm, 1), 0)
        for g, w in enumerate(WINDOWS):
            lo, hi = 128 * g, 128 * (g + 1)
            pb = pb_ref[:, lo:hi]
            mixed = jnp.dot(pb, wp_ref[g], preferred_element_type=f32)
            dpo = dp_ref[:, lo:hi]
            dsc_ref[:, lo:hi] += jnp.sum(dpo * mixed, axis=0, keepdims=True)
            dmr = (dpo * sc_ref[:, lo:hi]).astype(bf16)
            dwp_ref[g] += lax.dot_general(pb, dmr, TN, preferred_element_type=f32)
            dpl = lax.dot_general(dmr, wp_ref[g], NT, preferred_element_type=f32)
            cnt = jnp.minimum(t_idx + 1, w).astype(f32)
            dpn = dpl / cnt
            ext[0:tm, lo:hi] = dpn
            acc = dpn
            for d in range(1, w):
                acc = acc + ext[d:d + tm, lo:hi]
            du_ref[:, lo:hi] = (acc - dpl).astype(bf16)
        ext[tm:tm + HALO, :] = ext[0:HALO, :]

    rev = lambda i: (n - 1 - i, 0)
    return pl.pallas_call(
        body, name="pool_bwd", grid=(n,),
        in_specs=[pl.BlockSpec((tm, AW), rev), pl.BlockSpec((tm, AW), rev), _full((4, 128, 128)), _full((1, AW)),
                  _full((8, 128))],
        out_specs=[pl.BlockSpec((tm, AW), rev), _full((1, AW)), _full((4, 128, 128))],
        out_shape=[jax.ShapeDtypeStruct((T, AW), bf16), jax.ShapeDtypeStruct((1, AW), f32),
                   jax.ShapeDtypeStruct((4, 128, 128), f32)],
        scratch_shapes=[pltpu.VMEM((tm + HALO, AW), f32)],
        compiler_params=_cp("arbitrary"),
    )(dpool, pooled, wp, scale, dep)


def _attn_bwd(qkv, qaug, kaug, attn, dattn, lse, dep):
    tq = tk = ATT_T
    n = T // tq
    it, jt = _causal_steps(True)
    nsteps = it.shape[0]

    rs = 64

    def body(it_ref, jt_ref, q_ref, k_ref, v_ref, qa_ref, ka_ref, o_ref, do_ref, lse_ref, dep_ref,
             dq_ref, dqs_ref, dk_ref, dks_ref, dv_ref, dq_acc, dk_acc, dv_acc, s_sc, dp_sc, p_sc, ds_sc):
        t = pl.program_id(1)
        i = it_ref[t]
        j = jt_ref[t]

        @pl.when(t == 0)
        def _():
            dq_acc[...] = jnp.zeros_like(dq_acc)

        @pl.when(i == j)
        def _():
            dk_acc[...] = jnp.zeros_like(dk_acc)
            dv_acc[...] = jnp.zeros_like(dv_acc)

        lane = lax.broadcasted_iota(jnp.int32, (tq, 128), 1)

        def step(on_diagonal):
            q = q_ref[...] * 0.125
            k = k_ref[...]
            v = v_ref[...]
            qa = qa_ref[...]
            ka = ka_ref[...]
            do = do_ref[...]
            dd = do.astype(f32) * o_ref[...].astype(f32)
            r0 = pl.multiple_of(i * tq, tq)
            qes, kes, does, deltas = [], [], [], []
            for e in range(2):
                hm = (lane >= 64) if e else (lane < 64)
                qes.append(jnp.where(hm, q, qa))
                kes.append(jnp.where(hm, k, ka))
                does.append(jnp.where(hm, do, jnp.zeros_like(do)))
                deltas.append(jnp.sum(jnp.where(hm, dd, 0.0), axis=1, keepdims=True))
                s_sc[e] = lax.dot_general(qes[e], kes[e], NT, preferred_element_type=f32)
                dp_sc[e] = lax.dot_general(does[e], v, NT, preferred_element_type=f32)
            for e in range(2):
                for r in range(0, tq, rs):
                    s = s_sc[e, r:r + rs, :] - lse_ref[r:r + rs, 64 * e:64 * e + 1]
                    if on_diagonal:
                        row = lax.broadcasted_iota(jnp.int32, (rs, tk), 0) + r
                        col = lax.broadcasted_iota(jnp.int32, (rs, tk), 1)
                        s = jnp.where(col <= row, s, NEG)
                    p = jnp.exp(s)
                    p_sc[e, r:r + rs, :] = p.astype(bf16)
                    ds_sc[e, r:r + rs, :] = (p * (dp_sc[e, r:r + rs, :] - deltas[e][r:r + rs, :])).astype(bf16)
                dv_acc[...] += lax.dot_general(does[e], p_sc[e], TN, preferred_element_type=f32)
                dsb = ds_sc[e]
                dk_acc[e] += lax.dot_general(qes[e], dsb, TN, preferred_element_type=f32)
                dq_acc[e, pl.ds(r0, tq), :] += jnp.dot(dsb, kes[e], preferred_element_type=f32)

        @pl.when(i > j)
        def _():
            step(False)

        @pl.when(i == j)
        def _():
            step(True)

        @pl.when(i == n - 1)
        def _():
            dk0 = dk_acc[0].T
            dk1 = dk_acc[1].T
            dk_ref[...] = jnp.where(lane < 64, dk0, dk1).astype(bf16)
            dks_ref[...] = jnp.where(lane < 64, dk1, dk0)
            dv_ref[...] = dv_acc[...].T.astype(bf16)

        @pl.when(t == nsteps - 1)
        def _():
            lane_t = lax.broadcasted_iota(jnp.int32, (T, 128), 1)
            dq_ref[...] = (jnp.where(lane_t < 64, dq_acc[0], dq_acc[1]) * 0.125).astype(bf16)
            dqs_ref[...] = jnp.where(lane_t < 64, dq_acc[1], dq_acc[0])

    qmap = lambda p, t, it, jt: (it[t], p)
    grid_spec = pltpu.PrefetchScalarGridSpec(
        num_scalar_prefetch=2, grid=(PAIRS, nsteps),
        in_specs=[pl.BlockSpec((tq, 128), qmap),
                  pl.BlockSpec((tk, 128), lambda p, t, it, jt: (jt[t], PAIRS + p)),
                  pl.BlockSpec((tk, 128), lambda p, t, it, jt: (jt[t], 2 * PAIRS + p)),
                  pl.BlockSpec((tq, 128), qmap), pl.BlockSpec((tk, 128), lambda p, t, it, jt: (jt[t], p)),
                  pl.BlockSpec((tq, 128), qmap), pl.BlockSpec((tq, 128), qmap),
                  pl.BlockSpec((None, tq, 128), lambda p, t, it, jt: (p, it[t], 0)),
                  pl.BlockSpec((8, 128), lambda p, t, it, jt: (0, 0))],
        out_specs=[pl.BlockSpec((T, 128), lambda p, t, it, jt: (0, p)),
                   pl.BlockSpec((None, T, 128), lambda p, t, it, jt: (p, 0, 0)),
                   pl.BlockSpec((tk, 128), lambda p, t, it, jt: (jt[t], p)),
                   pl.BlockSpec((None, tk, 128), lambda p, t, it, jt: (p, jt[t], 0)),
                   pl.BlockSpec((tk, 128), lambda p, t, it, jt: (jt[t], p))],
        scratch_shapes=[pltpu.VMEM((2, T, 128), f32), pltpu.VMEM((2, 128, tk), f32), pltpu.VMEM((128, tk), f32),
                        pltpu.VMEM((2, tq, tk), f32), pltpu.VMEM((2, tq, tk), f32), pltpu.VMEM((2, tq, tk), bf16),
                        pltpu.VMEM((2, tq, tk), bf16)],
    )
    return pl.pallas_call(
        body, name="fox_attn_bwd", grid_spec=grid_spec,
        out_shape=[jax.ShapeDtypeStruct((T, AW), bf16), jax.ShapeDtypeStruct((PAIRS, T, 128), f32),
                   jax.ShapeDtypeStruct((T, AW), bf16), jax.ShapeDtypeStruct((PAIRS, T, 128), f32),
                   jax.ShapeDtypeStruct((T, AW), bf16)],
        compiler_params=_cp("parallel", "arbitrary"),
    )(it, jt, qkv, qkv, qkv, qaug, kaug, attn, dattn, lse, dep)


def _fox_cumsum_bwd(dqs, dks, fl, bfp):
    tb = CUMSUM_ROWS
    nb = T // tb
    hp = lax.Precision.HIGHEST

    def body(dqs_ref, dks_ref, fl_ref, b_ref, df_ref, db_ref, carry):
        i = pl.program_id(0)

        @pl.when(i == 0)
        def _():
            carry[...] = jnp.zeros_like(carry)
            db_ref[...] = jnp.zeros_like(db_ref)

        r = lax.broadcasted_iota(jnp.int32, (128, 128), 0)
        cc = lax.broadcasted_iota(jnp.int32, (128, 128), 1)
        pick = lambda even_lane, odd_lane, p: jnp.logical_or(
            jnp.logical_and(r == even_lane, cc == 2 * p), jnp.logical_and(r == odd_lane, cc == 2 * p + 1)).astype(f32)
        dc = jnp.zeros((tb, 128), f32)
        for p in range(PAIRS):
            dc = dc + jnp.dot(dqs_ref[p], pick(64, 0, p), precision=hp, preferred_element_type=f32)
            dc = dc - jnp.dot(dks_ref[p], pick(67, 3, p), precision=hp, preferred_element_type=f32)
        rt = lax.broadcasted_iota(jnp.int32, (tb, tb), 0)
        ct = lax.broadcasted_iota(jnp.int32, (tb, tb), 1)
        utri = (ct >= rt).astype(f32)
        dl = jnp.dot(utri, dc, precision=hp, preferred_element_type=f32) + carry[0:1, :]
        carry[...] = jnp.broadcast_to(dl[0:1, :], (8, 128))
        z = fl_ref[...] + b_ref[...]
        df = dl * jax.nn.sigmoid(-z)
        df_ref[...] = df.astype(bf16)
        db_ref[...] += jnp.sum(df, axis=0, keepdims=True)

    rev = lambda i: (nb - 1 - i, 0)
    return pl.pallas_call(
        body, name="fox_cumsum_bwd", grid=(nb,),
        in_specs=[pl.BlockSpec((PAIRS, tb, 128), lambda i: (0, nb - 1 - i, 0)),
                  pl.BlockSpec((PAIRS, tb, 128), lambda i: (0, nb - 1 - i, 0)),
                  pl.BlockSpec((tb, 128), rev), _full((1, 128))],
        out_specs=[pl.BlockSpec((tb, 128), rev), _full((1, 128))],
        out_shape=[jax.ShapeDtypeStruct((T, 128), bf16), jax.ShapeDtypeStruct((1, 128), f32)],
        scratch_shapes=[pltpu.VMEM((8, 128), f32)],
        compiler_params=_cp("arbitrary"),
    )(dqs, dks, fl, bfp)


def _inproj_bwd(dq, dk, dv, du, df, wm, wf, x, dx1, g1):
    tm = 512

    def body(dq_ref, dk_ref, dv_ref, du_ref, df_ref, wm_ref, wf_ref, x_ref, dx1_ref, g_ref, dx_ref, dn_ref):
        i = pl.program_id(0)

        @pl.when(i == 0)
        def _():
            dn_ref[...] = jnp.zeros_like(dn_ref)

        dh = jnp.dot(dq_ref[...], wm_ref[0:AW, :], preferred_element_type=f32)
        dh = dh + jnp.dot(dk_ref[...], wm_ref[AW:2 * AW, :], preferred_element_type=f32)
        dh = dh + jnp.dot(dv_ref[...], wm_ref[2 * AW:3 * AW, :], preferred_element_type=f32)
        dh = dh + jnp.dot(du_ref[...], wm_ref[3 * AW:4 * AW, :], preferred_element_type=f32)
        dh = dh + jnp.dot(df_ref[...], wf_ref[...], preferred_element_type=f32)
        xv = x_ref[...]
        r = lax.rsqrt(jnp.mean(xv * xv, axis=-1, keepdims=True) + EPS)
        xhat = xv * r
        dn_ref[...] += jnp.sum(dh * xhat, axis=0, keepdims=True)
        z = dh * g_ref[...]
        dx_ref[...] = dx1_ref[...] + r * (z - xhat * jnp.mean(z * xhat, axis=-1, keepdims=True))

    row = lambda i: (i, 0)
    return pl.pallas_call(
        body, name="inproj_bwd", grid=(T // tm,),
        in_specs=[pl.BlockSpec((tm, AW), row)] * 4 + [pl.BlockSpec((tm, 128), row), _full((4 * AW, D)), _full((128, D)),
                                                       pl.BlockSpec((tm, D), row), pl.BlockSpec((tm, D), row), _full((1, D))],
        out_specs=[pl.BlockSpec((tm, D), row), _full((1, D))],
        out_shape=[jax.ShapeDtypeStruct((T, D), f32), jax.ShapeDtypeStruct((1, D), f32)],
        compiler_params=_cp("arbitrary"),
    )(dq, dk, dv, du, df, wm, wf, x, dx1, g1)


def _adamw_math(w, g, m, v):
    m = B1 * m + (1.0 - B1) * g
    v = B2 * v + (1.0 - B2) * (g * g)
    m_hat = m / (1.0 - B1 ** STEP)
    v_hat = v / (1.0 - B2 ** STEP)
    delta = -LR * (m_hat / (jnp.sqrt(v_hat) + AEPS) + WD * w)
    return delta, m, v


def _adamw_shard(w, m, v, p_mine, p_other, name):
    rows, cols = w.shape
    tr = rows if rows <= IN_S else rows // 2

    def body(w_ref, m_ref, v_ref, a_ref, b_ref, g_ref, d_ref, nm_ref, nv_ref):
        g = a_ref[...].astype(f32) + b_ref[...].astype(f32)
        g_ref[...] = g
        d_ref[...], nm_ref[...], nv_ref[...] = _adamw_math(w_ref[...], g, m_ref[...], v_ref[...])

    spec = pl.BlockSpec((tr, cols), lambda i: (i, 0))
    return pl.pallas_call(
        body, name=name, grid=(rows // tr,), in_specs=[spec] * 5, out_specs=[spec] * 4,
        out_shape=[jax.ShapeDtypeStruct((rows, cols), f32)] * 4, compiler_params=_cp("parallel"),
    )(w, m, v, p_mine, p_other)


SMALL_SLOTS = ((0, 8, 128), (8, 16, 128), (16, 24, 128), (24, 28, 128), (32, 33, 8), (40, 552, 128))
LOSS_ROW = 39


def _adamw_small(ws, ms, vs, parts):
    n = len(ws)

    def body(*refs):
        w_refs, m_refs, v_refs, p_ref = refs[0:n], refs[n:2 * n], refs[2 * n:3 * n], refs[3 * n]
        outs = refs[3 * n + 1:]
        g_all = p_ref[0]
        for k in range(1, 8):
            g_all = g_all + p_ref[k]
        for idx, (r0, r1, lanes) in enumerate(SMALL_SLOTS):
            g = g_all[r0:r1, 0:lanes]
            d, nm, nv = _adamw_math(w_refs[idx][...], g, m_refs[idx][...], v_refs[idx][...])
            outs[idx][...] = g
            outs[n + idx][...] = d
            outs[2 * n + idx][...] = nm
            outs[3 * n + idx][...] = nv
        outs[4 * n][...] = g_all[LOSS_ROW:LOSS_ROW + 1, :]

    shapes = [jax.ShapeDtypeStruct(w.shape, f32) for w in ws]
    res = pl.pallas_call(
        body, name="adamw_small", out_shape=shapes * 4 + [jax.ShapeDtypeStruct((1, 128), f32)],
    )(*ws, *ms, *vs, parts)
    return res[:4 * n], res[4 * n]


def _sum4(recv, g, mine, name):
    _, rows, cols = recv.shape
    tr = rows if rows <= IN_S else rows // 2

    def body(mine_ref, r_ref, g_ref, o_ref):
        o_ref[...] = ((g_ref[...].astype(f32) + r_ref[0].astype(f32))
                      + (r_ref[1].astype(f32) + r_ref[2].astype(f32))).astype(bf16)

    grid_spec = pltpu.PrefetchScalarGridSpec(
        num_scalar_prefetch=1, grid=(rows // tr,),
        in_specs=[pl.BlockSpec((3, tr, cols), lambda i, m: (0, i, 0)),
                  pl.BlockSpec((None, tr, cols), lambda i, m: (m[0], i, 0))],
        out_specs=pl.BlockSpec((tr, cols), lambda i, m: (i, 0)))
    return pl.pallas_call(
        body, name=name, grid_spec=grid_spec, out_shape=jax.ShapeDtypeStruct((rows, cols), bf16),
        compiler_params=_cp("arbitrary"),
    )(mine, recv, g)


_HBM = pl.BlockSpec(memory_space=pltpu.HBM)
_SEM = pl.BlockSpec(memory_space=pltpu.SEMAPHORE)
_EFFECT = pltpu.SideEffectType.DATAFLOW_SIDE_EFFECTING


def _in_hbm(a):
    return pltpu.with_memory_space_constraint(a, pltpu.HBM)


def _mesh_pos():
    return lax.axis_index("x"), lax.axis_index("y"), lax.axis_index("c")


def _other_chips(x, y):
    return [(1 - x, y), (x, 1 - y), (1 - x, 1 - y)]


def _gather_copy(srcs, lands, send_sems, recv_sems, a, k, slot):
    x, y, c = _mesh_pos()
    cx, cy = _other_chips(x, y)[k]
    return pltpu.make_async_remote_copy(
        src_ref=srcs[a], dst_ref=lands[a].at[slot], send_sem=send_sems.at[3 * a + k], recv_sem=recv_sems.at[3 * a + k],
        device_id=(cx, cy, c), device_id_type=MESH)


def _scatter_copy(srcs, lands, send_sems, recv_sems, a, k):
    x, y, c = _mesh_pos()
    cx, cy = _other_chips(x, y)[k]
    return pltpu.make_async_remote_copy(
        src_ref=srcs[a].at[2 * cx + cy], dst_ref=lands[a].at[k], send_sem=send_sems.at[3 * a + k],
        recv_sem=recv_sems.at[3 * a + k], device_id=(cx, cy, c), device_id_type=MESH)


def _all_gather_w_in(part):
    cols = part.shape[1] // 2

    def body(src, dst, send_sems, recv_sems, loc_sem):
        x, y, c = _mesh_pos()
        mine = 2 * x + y
        chips = _other_chips(x, y)
        half = lambda ref, cc: ref.at[:, pl.ds(pl.multiple_of(cc * cols, cols), cols)]

        def over_ici(k, slot):
            cx, cy = chips[k]
            return pltpu.make_async_remote_copy(
                src_ref=half(src, c), dst_ref=half(dst.at[slot], c), send_sem=send_sems.at[k], recv_sem=recv_sems.at[k],
                device_id=(cx, cy, c), device_id_type=MESH)

        def to_sibling(k, cc):
            slot = 2 * chips[k][0] + chips[k][1]
            return pltpu.make_async_remote_copy(
                src_ref=half(dst.at[slot], cc), dst_ref=half(dst.at[slot], cc), send_sem=send_sems.at[3 + k],
                recv_sem=recv_sems.at[3 + k], device_id=(x, y, 1 - c), device_id_type=MESH)

        local = pltpu.make_async_copy(src, dst.at[mine], loc_sem.at[0])
        local.start()
        first = [over_ici(k, mine) for k in range(3)]
        for cp in first:
            cp.start()
        passed = [to_sibling(k, c) for k in range(3)]
        for k in range(3):
            over_ici(k, 2 * chips[k][0] + chips[k][1]).wait_recv()
            passed[k].start()
        for k in range(3):
            to_sibling(k, 1 - c).wait_recv()
        for cp in first + passed:
            cp.wait_send()
        local.wait()

    return pl.pallas_call(
        body, name="all_gather_w_in", in_specs=[_HBM], out_specs=_HBM,
        out_shape=jax.ShapeDtypeStruct((NSH,) + part.shape, part.dtype),
        scratch_shapes=[pltpu.SemaphoreType.DMA((6,)), pltpu.SemaphoreType.DMA((6,)), pltpu.SemaphoreType.DMA((1,))],
    )(part)


def _split_start(name, srcs, lands, n_sems, plan, dep):
    n, nl = len(srcs), len(lands)

    def body(*refs):
        src_refs, land_refs = refs[:n], refs[n:n + nl]
        send_sems, recv_sems = refs[n + nl + 1], refs[n + nl + 2]
        token = refs[-1]
        sends, _ = plan(src_refs, land_refs, send_sems, recv_sems)
        for cp in sends:
            cp.start()
        token[...] = jnp.zeros_like(token)

    outs = pl.pallas_call(
        body, name=name,
        in_specs=[_HBM] * (n + nl) + [pl.BlockSpec(memory_space=pl.ANY)],
        out_specs=[_SEM, _SEM] + [_HBM] * (n + nl) + [pl.BlockSpec(memory_space=pltpu.VMEM)],
        out_shape=[pltpu.SemaphoreType.DMA((n_sems,)), pltpu.SemaphoreType.DMA((n_sems,))]
        + [pltpu.HBM(a.shape, a.dtype) for a in list(srcs) + list(lands)] + [jax.ShapeDtypeStruct((8, 128), f32)],
        input_output_aliases={i: 2 + i for i in range(n + nl)},
        compiler_params=pltpu.CompilerParams(has_side_effects=_EFFECT),
    )(*[_in_hbm(a) for a in list(srcs) + list(lands)], dep)
    return outs[0], outs[1], list(outs[2:2 + n]), list(outs[2 + n:2 + n + nl]), outs[-1]


def _split_wait(name, send_sems, recv_sems, srcs, lands, after, plan):
    n, nl = len(srcs), len(lands)

    def body(*refs):
        src_refs, land_refs = refs[:n], refs[n:n + nl]
        s_sems, r_sems = refs[n + nl], refs[n + nl + 1]
        sends, recvs = plan(src_refs, land_refs, s_sems, r_sems)
        for cp in recvs:
            cp.wait_recv()
        for cp in sends:
            cp.wait_send()

    outs = pl.pallas_call(
        body, name=name,
        in_specs=[_HBM] * (n + nl) + [_SEM, _SEM, pl.BlockSpec(memory_space=pl.ANY)],
        out_specs=[_HBM] * (n + nl),
        out_shape=[pltpu.HBM(a.shape, a.dtype) for a in list(srcs) + list(lands)],
        input_output_aliases={i: i for i in range(n + nl)},
        compiler_params=pltpu.CompilerParams(has_side_effects=_EFFECT),
    )(*srcs, *lands, send_sems, recv_sems, after)
    return list(outs[:n]), list(outs[n:])


def _gather_plan(srcs, lands, ss, rs):
    x, y, _ = _mesh_pos()
    chips = _other_chips(x, y)
    sends = [_gather_copy(srcs, lands, ss, rs, a, k, 2 * x + y) for a in range(len(srcs)) for k in range(3)]
    recvs = [_gather_copy(srcs, lands, ss, rs, a, k, 2 * chips[k][0] + chips[k][1])
             for a in range(len(srcs)) for k in range(3)]
    return sends, recvs


def _scatter_plan(srcs, lands, ss, rs):
    cps = [_scatter_copy(srcs, lands, ss, rs, a, k) for a in range(len(srcs)) for k in range(3)]
    return cps, cps


def _tail_plan(srcs, lands, ss, rs):
    x, y, c = _mesh_pos()
    me = 4 * x + 2 * y + c
    cps = [_scatter_copy(srcs[:1], lands[:1], ss, rs, 0, k) for k in range(3)]
    for f in range(1, 8):
        peer = ((x + (f >> 2)) % 2, (y + ((f >> 1) & 1)) % 2, (c + (f & 1)) % 2)
        cps.append(pltpu.make_async_remote_copy(
            src_ref=srcs[1], dst_ref=lands[1].at[me], send_sem=ss.at[2 + f], recv_sem=rs.at[2 + f],
            device_id=peer, device_id_type=MESH))
    return cps, cps


def _swap_with_sibling(parts, name):
    n = len(parts)

    def body(*refs):
        srcs, dsts = refs[:n], refs[n:2 * n]
        send_sems, recv_sems = refs[2 * n:]
        x, y, c = _mesh_pos()
        cps = [pltpu.make_async_remote_copy(src_ref=srcs[a], dst_ref=dsts[a], send_sem=send_sems.at[a],
                                            recv_sem=recv_sems.at[a], device_id=(x, y, 1 - c), device_id_type=MESH)
               for a in range(n)]
        for cp in cps:
            cp.start()
        for cp in cps:
            cp.wait_recv()
        for cp in cps:
            cp.wait_send()

    return pl.pallas_call(
        body, name=name, in_specs=[_HBM] * n, out_specs=[_HBM] * n,
        out_shape=[jax.ShapeDtypeStruct(p.shape, p.dtype) for p in parts],
        scratch_shapes=[pltpu.SemaphoreType.DMA((n,)), pltpu.SemaphoreType.DMA((n,))],
    )(*parts)


def _forward(x, tgt, wm, wf, mlp_w_fn, g1, bfp, wp, scale, g2, gf, dep):
    h, qkv, u, fl = _rms_inproj(x, g1, wm, wf, dep)
    qaug, kaug = _fox_cumsum(fl, bfp)
    attn, lse = _attn_fwd(qkv, qaug, kaug)
    pooled, pool = _pool_fwd(u, wp, scale)
    wo, wgt, wut, wd = mlp_w_fn(attn)
    x1, h2 = _outproj(x, attn, pool, wo, g2)
    loss, dgf, dx2, dx2b, ud, silu, a_b = _mlp_fwd_loss(h2, x1, wgt, wut, wd, tgt, gf)
    saved = dict(h=h, qkv=qkv, fl=fl, qaug=qaug, kaug=kaug, attn=attn, lse=lse, pooled=pooled, pool=pool, x1=x1, h2=h2,
                 ud=ud, silu=silu, a_b=a_b, wo=wo, wgt=wgt, wut=wut, wd=wd)
    return loss, dgf, dx2, dx2b, saved


def _backward_mlp(sv, dx2, dx2b, g2):
    dgate, dup, dx1, dx1b, dg2 = _mlp_bwd(dx2b, dx2, sv["ud"], sv["silu"], sv["wgt"], sv["wut"], sv["wd"], sv["x1"], g2)
    (dwd,) = _mm_tn(sv["a_b"], [dx2b], "dw_down", a_sharded=True, tk=T)
    (dwgt,) = _mm_tn(dgate, [sv["h2"]], "dw_gate", a_sharded=True, tk=T)
    (dwut,) = _mm_tn(dup, [sv["h2"]], "dw_up", a_sharded=True, tk=T)
    return dx1, dx1b, dg2, (dwgt, dwut, dwd)


def _backward_outproj(sv, dx1b):
    dattn, dpool = _outproj_bwd(dx1b, sv["wo"])
    dwo_a, = _mm_tn(sv["attn"], [dx1b], "dw_out_attn", tk=2048)
    dwo_p, = _mm_tn(sv["pool"], [dx1b], "dw_out_pool", tk=2048)
    dwo = jnp.concatenate([dwo_a, dwo_p], axis=0).reshape(NSH, D // NSH, D)
    return dattn, dpool, dwo


def _backward_mixer(sv, x, dx1, dattn, dpool, wm, wf, g1, bfp, wp, scale, dep):
    du, dscale, dwp = _pool_bwd(dpool, sv["pooled"], wp, scale, dep)
    dq, dqs, dk, dks, dv = _attn_bwd(sv["qkv"], sv["qaug"], sv["kaug"], sv["attn"], dattn, sv["lse"], dep)
    df, dbf = _fox_cumsum_bwd(dqs, dks, sv["fl"], bfp)
    dx, dg1 = _inproj_bwd(dq, dk, dv, du, df, wm, wf, x, dx1, g1)
    dwq, dwk, dwv, dwu_in, dwf = _mm_tn_rows([dq, dk, dv, du, df], sv["h"], "dw_in")
    dwin = jnp.concatenate([dwq, dwk, dwv, dwf[0:8], dwu_in], axis=0)
    return dx, dg1, dscale, dwp, dbf, dwin.reshape(NSH, IN_S, D)


def kernel(x, norm1_g, w_in, b_forget, w_pool, pool_scale, w_out, norm2_g, w_gate, w_up, w_down, final_g, loss_target, m_norm1_g, m_w_in, m_b_forget, m_w_pool, m_pool_scale, m_w_out, m_norm2_g, m_w_gate, m_w_up, m_w_down, m_final_g, v_norm1_g, v_w_in, v_b_forget, v_w_pool, v_pool_scale, v_w_out, v_norm2_g, v_w_gate, v_w_up, v_w_down, v_final_g):
    mine = (2 * lax.axis_index("x") + lax.axis_index("y")).astype(jnp.int32)
    mine1 = mine.reshape(1)
    tr = lambda a: jnp.transpose(a[0])

    win4 = _all_gather_w_in(tr(w_in).astype(bf16))
    later = [w_out[0].astype(bf16), tr(w_gate).astype(bf16), tr(w_up).astype(bf16), w_down[0].astype(bf16)]
    lands = [lax.dynamic_update_slice(lax.empty((NSH,) + p.shape, bf16), p[None], (mine, 0, 0)) for p in later]
    ag_send, ag_recv, later_thru, lands_thru, ag_token = _split_start("all_gather_start", later, lands, 12, _gather_plan,
                                                                      win4)
    win = win4.reshape(IN_W, D)
    wm = jnp.concatenate([win[0:3 * AW], win[3 * AW + 8:]], axis=0)
    wf = jnp.pad(win[3 * AW:3 * AW + 8], ((0, 120), (0, 0)))
    bfp = jnp.pad(b_forget, ((0, 0), (0, 120)))
    wp = w_pool[0].astype(bf16)
    gf = final_g.reshape(1, D)

    def later_weights(after):
        _, (wo4, wgt, wut, wd) = _split_wait("all_gather_wait", ag_send, ag_recv, later_thru, lands_thru, after, _gather_plan)
        return wo4.reshape(D, D), wgt, wut, wd

    xe, tgt = x[0], loss_target[0]
    loss_v, dgf, dx2, dx2b, sv = _forward(xe, tgt, wm, wf, later_weights, norm1_g, bfp, wp, pool_scale, norm2_g, gf, ag_token)
    dx1, dx1b, dg2, mlp_grads = _backward_mlp(sv, dx2, dx2b, norm2_g)
    dattn, dpool, dwo = _backward_outproj(sv, dx1b)
    first = [dwo] + list(mlp_grads)
    first_lands = [lax.empty((3,) + g.shape[1:], bf16) for g in first]
    rs_send, rs_recv, first_thru, first_lands_thru, rs_token = _split_start("reduce_scatter_start", first, first_lands, 12,
                                                                            _scatter_plan, dattn)
    dx, dg1, dscale, dwp, dbf, dwin = _backward_mixer(sv, xe, dx1, dattn, dpool, wm, wf, norm1_g, bfp, wp, pool_scale, rs_token)

    me = (4 * lax.axis_index("x") + 2 * lax.axis_index("y") + lax.axis_index("c")).astype(jnp.int32)
    pad8 = lambda r: jnp.pad(r, ((0, 8 - r.shape[0]), (0, 0)))
    loss_rows = jnp.concatenate([dbf, jnp.zeros((6, 128), f32), loss_v[0:1, :]], axis=0)
    small = jnp.concatenate([dg1.reshape(8, 128), dg2.reshape(8, 128), dgf.reshape(8, 128), pad8(dscale.reshape(4, 128)),
                             loss_rows, dwp.reshape(512, 128)], axis=0)
    small_land = lax.dynamic_update_slice(lax.empty((8, SMALL_ROWS, 128), f32), small[None], (me, 0, 0))
    tail_send, tail_recv, tail_thru, tail_lands_thru, tail_token = _split_start(
        "tail_start", [dwin, small], [lax.empty((3,) + dwin.shape[1:], bf16), small_land], 10, _tail_plan, dx)
    first_thru, first_recv = _split_wait("reduce_scatter_wait", rs_send, rs_recv, first_thru, first_lands_thru, tail_token,
                                         _scatter_plan)
    ws = [tr(w_in), w_out[0], tr(w_gate), tr(w_up), w_down[0]]
    ms = [tr(m_w_in), m_w_out[0], tr(m_w_gate), tr(m_w_up), m_w_down[0]]
    vs = [tr(v_w_in), v_w_out[0], tr(v_w_gate), tr(v_w_up), v_w_down[0]]
    partial = [_sum4(r, g, mine1, f"sum4_{i + 1}") for i, (r, g) in enumerate(zip(first_recv, first_thru))]
    other = _swap_with_sibling(partial, "swap_first")
    big = [_adamw_shard(ws[i + 1], ms[i + 1], vs[i + 1], partial[i], other[i], f"adamw_{i + 1}") for i in range(4)]
    (dwin_thru, _), (in_recv_land, small_all) = _split_wait("tail_wait", tail_send, tail_recv, tail_thru, tail_lands_thru,
                                                            big[3][0], _tail_plan)
    partial_in = _sum4(in_recv_land, dwin_thru, mine1, "sum4_0")
    (other_in,) = _swap_with_sibling([partial_in], "swap_in")
    big = [_adamw_shard(ws[0], ms[0], vs[0], partial_in, other_in, "adamw_0")] + big

    small_names = ["norm1_g", "norm2_g", "final_g", "pool_scale", "b_forget", "w_pool"]
    rows = lambda a, b, c, d, e, f: [a.reshape(8, 128), b.reshape(8, 128), c.reshape(8, 128), d.reshape(4, 128),
                                     e.reshape(1, 8), f.reshape(512, 128)]
    sm, loss_row = _adamw_small(rows(norm1_g, norm2_g, final_g, pool_scale, b_forget, w_pool),
                                rows(m_norm1_g, m_norm2_g, m_final_g, m_pool_scale, m_b_forget, m_w_pool),
                                rows(v_norm1_g, v_norm2_g, v_final_g, v_pool_scale, v_b_forget, v_w_pool), small_all)
    small_shape = dict(norm1_g=(1, D), norm2_g=(1, D), final_g=(D,), pool_scale=(1, AW), b_forget=(1, 8),
                       w_pool=(1, 4, 128, 128))

    order = ["norm1_g", "w_in", "b_forget", "w_pool", "pool_scale", "w_out", "norm2_g", "w_gate", "w_up", "w_down", "final_g"]
    big_idx = {"w_in": 0, "w_out": 1, "w_gate": 2, "w_up": 3, "w_down": 4}
    outs = [loss_row[0, 0], dx[None]]
    for kind in range(4):
        for name in order:
            if name in ("w_in", "w_gate", "w_up"):
                outs.append(jnp.transpose(big[big_idx[name]][kind])[None])
            elif name in big_idx:
                outs.append(big[big_idx[name]][kind][None])
            else:
                outs.append(sm[6 * kind + small_names.index(name)].reshape(small_shape[name]))
    return tuple(outs)
```

```python
import jax
import jax.numpy as jnp
import numpy as np
from jax import lax
from jax.experimental import pallas as pl
from jax.experimental.pallas import tpu as pltpu

f32 = jnp.float32
bf16 = jnp.bfloat16

T = 4096
D = 1024
NSH = 4
IN_W = 2056
IN_S = IN_W // NSH
AW = 512
PAIRS = 4
FF = 2816
FS = FF // NSH
WINDOWS = (2, 4, 8, 16)
HALO = 16
EPS = 1e-6
NEG = -1e30
LR, B1, B2, AEPS, WD, STEP = 0.001, 0.9, 0.999, 1e-08, 0.01, 10
SMALL_ROWS = 40

NT = (((1,), (1,)), ((), ()))
TN = (((0,), (0,)), ((), ()))

MESH = pl.DeviceIdType.MESH


def _cp(*sem):
    return pltpu.CompilerParams(dimension_semantics=sem)


def _full(shape):
    n = len(shape)
    return pl.BlockSpec(shape, lambda *_: (0,) * n)


def _resident(shape):
    n = len(shape)
    return pl.BlockSpec(shape, lambda *_: (0,) * n, pipeline_mode=pl.Buffered(1))


def _rms_inproj(x, g1, wm, wf, dep):
    tm = 512

    def body(x_ref, g_ref, wm_ref, wf_ref, dep_ref, h_ref, qkv_ref, u_ref, fl_ref):
        xv = x_ref[...]
        r = lax.rsqrt(jnp.mean(xv * xv, axis=-1, keepdims=True) + EPS)
        h = (xv * r * g_ref[...]).astype(bf16)
        h_ref[...] = h
        qkv_ref[...] = lax.dot_general(h, wm_ref[0:3 * AW, :], NT, preferred_element_type=f32).astype(bf16)
        u_ref[...] = lax.dot_general(h, wm_ref[3 * AW:4 * AW, :], NT, preferred_element_type=f32)
        fl_ref[...] = lax.dot_general(h, wf_ref[...], NT, preferred_element_type=f32)

    return pl.pallas_call(
        body, name="rms_inproj", grid=(T // tm,),
        in_specs=[pl.BlockSpec((tm, D), lambda i: (i, 0)), _full((1, D)), _full((4 * AW, D)), _full((128, D)),
                  _full((8, 128))],
        out_specs=[pl.BlockSpec((tm, D), lambda i: (i, 0)), pl.BlockSpec((tm, 3 * AW), lambda i: (i, 0)),
                   pl.BlockSpec((tm, AW), lambda i: (i, 0)), pl.BlockSpec((tm, 128), lambda i: (i, 0))],
        out_shape=[jax.ShapeDtypeStruct((T, D), bf16), jax.ShapeDtypeStruct((T, 3 * AW), bf16),
                   jax.ShapeDtypeStruct((T, AW), f32), jax.ShapeDtypeStruct((T, 128), f32)],
        compiler_params=_cp("parallel"),
    )(x, g1, wm, wf, dep)


CUMSUM_ROWS = 512
FS_CHUNKS = ((0, 256), (256, 512), (512, FS))


def _log_sigmoid(z):
    return jnp.minimum(z, 0.0) - jnp.log(1.0 + jnp.exp(-jnp.abs(z)))


def _fox_cumsum(fl, bfp):
    tb = CUMSUM_ROWS
    nb = T // tb

    def body(fl_ref, b_ref, qa_ref, ka_ref, carry):
        i = pl.program_id(0)

        @pl.when(i == 0)
        def _():
            carry[...] = jnp.zeros_like(carry)

        lf = _log_sigmoid(fl_ref[...] + b_ref[...])
        r = lax.broadcasted_iota(jnp.int32, (tb, tb), 0)
        cc = lax.broadcasted_iota(jnp.int32, (tb, tb), 1)
        ltri = (cc <= r).astype(f32)
        cb = jnp.dot(ltri, lf, precision=lax.Precision.HIGHEST, preferred_element_type=f32) + carry[0:1, :]
        carry[...] = jnp.broadcast_to(cb[tb - 1:tb, :], (8, 128))
        hi = cb.astype(bf16)
        r1 = cb - hi.astype(f32)
        mid = r1.astype(bf16)
        lo = (r1 - mid.astype(f32)).astype(bf16)
        head = lax.broadcasted_iota(jnp.int32, (128, AW), 0)
        col = lax.broadcasted_iota(jnp.int32, (128, AW), 1)
        base = 128 * (head >> 1) + 64 * (1 - (head & 1))
        place = lambda off: jnp.logical_and(col == base + off, head < 8).astype(bf16)
        mm = lambda a, off: jnp.dot(a, place(off), preferred_element_type=f32)
        cq = mm(hi, 0) + mm(mid, 1) + mm(lo, 2)
        ck = mm(hi, 3) + mm(mid, 4) + mm(lo, 5)
        within = jnp.bitwise_and(lax.broadcasted_iota(jnp.int32, (tb, AW), 1), 63)
        qa_ref[...] = jnp.where(jnp.logical_and(within >= 3, within <= 5), 1.0, cq).astype(bf16)
        ka_ref[...] = jnp.where(within <= 2, 1.0, -ck).astype(bf16)

    return pl.pallas_call(
        body, name="fox_cumsum", grid=(nb,),
        in_specs=[pl.BlockSpec((tb, 128), lambda i: (i, 0)), _full((1, 128))],
        out_specs=[pl.BlockSpec((tb, AW), lambda i: (i, 0)), pl.BlockSpec((tb, AW), lambda i: (i, 0))],
        out_shape=[jax.ShapeDtypeStruct((T, AW), bf16), jax.ShapeDtypeStruct((T, AW), bf16)],
        scratch_shapes=[pltpu.VMEM((8, 128), f32)],
        compiler_params=_cp("arbitrary"),
    )(fl, bfp)


ATT_T = 512


def _causal_steps(key_major):
    n = T // ATT_T
    if key_major:
        pairs = [(i, j) for j in range(n) for i in range(j, n)]
    else:
        pairs = [(i, j) for i in range(n) for j in range(i + 1)]
    it = np.array([p[0] for p in pairs], np.int32)
    jt = np.array([p[1] for p in pairs], np.int32)
    return jnp.asarray(it), jnp.asarray(jt)


def _attn_fwd(qkv, qaug, kaug):
    tq = tk = ATT_T
    it, jt = _causal_steps(False)
    nsteps = it.shape[0]

    rs = 64

    def body(it_ref, jt_ref, q_ref, k_ref, v_ref, qa_ref, ka_ref, o_ref, lse_ref, m_sc, acc_sc, s_sc, p_sc, alpha_sc):
        t = pl.program_id(1)
        i = it_ref[t]
        j = jt_ref[t]

        @pl.when(j == 0)
        def _():
            m_sc[...] = jnp.full_like(m_sc, NEG)
            acc_sc[...] = jnp.zeros_like(acc_sc)

        lane = lax.broadcasted_iota(jnp.int32, (tq, 128), 1)
        spare = (64, 0)

        def step(on_diagonal):
            q = q_ref[...] * 0.125
            k = k_ref[...]
            v = v_ref[...]
            qa = qa_ref[...]
            ka = ka_ref[...]
            for e in range(2):
                hm = (lane >= 64) if e else (lane < 64)
                s_sc[e] = lax.dot_general(jnp.where(hm, q, qa), jnp.where(hm, k, ka), NT, preferred_element_type=f32)
            for e in range(2):
                hm = (lane >= 64) if e else (lane < 64)
                for r in range(0, tq, rs):
                    s = s_sc[e, r:r + rs, :]
                    if on_diagonal:
                        row = lax.broadcasted_iota(jnp.int32, (rs, tk), 0) + r
                        col = lax.broadcasted_iota(jnp.int32, (rs, tk), 1)
                        s = jnp.where(col <= row, s, NEG)
                    m_prev = m_sc[e, r:r + rs, :]
                    m_new = jnp.maximum(m_prev, jnp.max(s, axis=1, keepdims=True))
                    p_sc[e, r:r + rs, :] = jnp.exp(s - jnp.tile(m_new, (1, tk // 128))).astype(bf16)
                    alpha_sc[e, r:r + rs, :] = jnp.exp(m_prev - m_new)
                    m_sc[e, r:r + rs, :] = m_new
            for e in range(2):
                hm = (lane >= 64) if e else (lane < 64)
                ve = jnp.where(hm, v, (lane == spare[e]).astype(bf16))
                acc_sc[e] = alpha_sc[e] * acc_sc[e] + jnp.dot(p_sc[e], ve, preferred_element_type=f32)

        @pl.when(j < i)
        def _():
            step(False)

        @pl.when(j == i)
        def _():
            step(True)
            l0 = acc_sc[0][:, spare[0]:spare[0] + 1]
            l1 = acc_sc[1][:, spare[1]:spare[1] + 1]
            o_ref[...] = jnp.where(lane < 64, acc_sc[0] / l0, acc_sc[1] / l1).astype(bf16)
            lse_ref[...] = jnp.where(lane < 64, m_sc[0] + jnp.log(l0), m_sc[1] + jnp.log(l1))

    qmap = lambda p, t, it, jt: (it[t], p)
    kmap = lambda p, t, it, jt: (jt[t], p)
    grid_spec = pltpu.PrefetchScalarGridSpec(
        num_scalar_prefetch=2, grid=(PAIRS, nsteps),
        in_specs=[pl.BlockSpec((tq, 128), qmap),
                  pl.BlockSpec((tk, 128), lambda p, t, it, jt: (jt[t], PAIRS + p)),
                  pl.BlockSpec((tk, 128), lambda p, t, it, jt: (jt[t], 2 * PAIRS + p)),
                  pl.BlockSpec((tq, 128), qmap), pl.BlockSpec((tk, 128), kmap)],
        out_specs=[pl.BlockSpec((tq, 128), qmap),
                   pl.BlockSpec((None, tq, 128), lambda p, t, it, jt: (p, it[t], 0))],
        scratch_shapes=[pltpu.VMEM((2, tq, 128), f32), pltpu.VMEM((2, tq, 128), f32), pltpu.VMEM((2, tq, tk), f32),
                        pltpu.VMEM((2, tq, tk), bf16), pltpu.VMEM((2, tq, 128), f32)],
    )
    return pl.pallas_call(
        body, name="fox_attn_fwd", grid_spec=grid_spec,
        out_shape=[jax.ShapeDtypeStruct((T, AW), bf16), jax.ShapeDtypeStruct((PAIRS, T, 128), f32)],
        compiler_params=_cp("parallel", "arbitrary"),
    )(it, jt, qkv, qkv, qkv, qaug, kaug)


def _pool_fwd(u, wp, scale):
    tm = 512

    def body(u_ref, wp_ref, sc_ref, pooled_ref, pool_ref, ext):
        i = pl.program_id(0)

        @pl.when(i == 0)
        def _():
            ext[0:HALO, :] = jnp.zeros((HALO, AW), f32)

        uv = u_ref[...]
        ext[HALO:HALO + tm, :] = uv
        t_idx = i * tm + lax.broadcasted_iota(jnp.int32, (tm, 1), 0)
        for g, w in enumerate(WINDOWS):
            lo, hi = 128 * g, 128 * (g + 1)
            ug = uv[:, lo:hi]
            acc = ug
            for d in range(1, w):
                acc = acc + ext[HALO - d:HALO - d + tm, lo:hi]
            cnt = jnp.minimum(t_idx + 1, w).astype(f32)
            pb = (acc / cnt - ug).astype(bf16)
            pooled_ref[:, lo:hi] = pb
            mixed = jnp.dot(pb, wp_ref[g], preferred_element_type=f32)
            pool_ref[:, lo:hi] = (mixed * sc_ref[:, lo:hi]).astype(bf16)
        ext[0:HALO, :] = uv[tm - HALO:tm, :]

    return pl.pallas_call(
        body, name="pool_fwd", grid=(T // tm,),
        in_specs=[pl.BlockSpec((tm, AW), lambda i: (i, 0)), _full((4, 128, 128)), _full((1, AW))],
        out_specs=[pl.BlockSpec((tm, AW), lambda i: (i, 0)), pl.BlockSpec((tm, AW), lambda i: (i, 0))],
        out_shape=[jax.ShapeDtypeStruct((T, AW), bf16), jax.ShapeDtypeStruct((T, AW), bf16)],
        scratch_shapes=[pltpu.VMEM((tm + HALO, AW), f32)],
        compiler_params=_cp("arbitrary"),
    )(u, wp, scale)


def _outproj(x, attn, pool, wo, g2):
    tm = 512

    def body(x_ref, a_ref, p_ref, wo_ref, g_ref, x1_ref, h2_ref):
        x1 = x_ref[...] + jnp.dot(a_ref[...], wo_ref[0:AW, :], preferred_element_type=f32)
        x1 = x1 + jnp.dot(p_ref[...], wo_ref[AW:2 * AW, :], preferred_element_type=f32)
        x1_ref[...] = x1
        r = lax.rsqrt(jnp.mean(x1 * x1, axis=-1, keepdims=True) + EPS)
        h2_ref[...] = (x1 * r * g_ref[...]).astype(bf16)

    return pl.pallas_call(
        body, name="outproj", grid=(T // tm,),
        in_specs=[pl.BlockSpec((tm, D), lambda i: (i, 0)), pl.BlockSpec((tm, AW), lambda i: (i, 0)),
                  pl.BlockSpec((tm, AW), lambda i: (i, 0)), _full((D, D)), _full((1, D))],
        out_specs=[pl.BlockSpec((tm, D), lambda i: (i, 0)), pl.BlockSpec((tm, D), lambda i: (i, 0))],
        out_shape=[jax.ShapeDtypeStruct((T, D), f32), jax.ShapeDtypeStruct((T, D), bf16)],
        compiler_params=_cp("parallel"),
    )(x, attn, pool, wo, g2)


def _mlp_fwd_loss(h2, x1, wg, wu, wd, tgt, gf):
    tm = 512

    def body(h_ref, x1_ref, wg_ref, wu_ref, wd_ref, t_ref, g_ref,
             loss_ref, dg_ref, dx_ref, dxb_ref, ud_ref, silu_ref, a_ref, x2):
        i = pl.program_id(0)
        s = pl.program_id(1)

        @pl.when(jnp.logical_and(i == 0, s == 0))
        def _():
            loss_ref[...] = jnp.zeros_like(loss_ref)
            dg_ref[...] = jnp.zeros_like(dg_ref)

        h = h_ref[...]
        gus = [(lax.dot_general(h, wg_ref[s, c0:c1, :], NT, preferred_element_type=f32),
                lax.dot_general(h, wu_ref[s, c0:c1, :], NT, preferred_element_type=f32)) for c0, c1 in FS_CHUNKS]
        for (c0, c1), (gate, up) in zip(FS_CHUNKS, gus):
            sg = jax.nn.sigmoid(gate)
            silu = gate * sg
            ud_ref[:, c0:c1] = (up * (sg * (1.0 + gate * (1.0 - sg)))).astype(bf16)
            silu_ref[:, c0:c1] = silu.astype(bf16)
            a_ref[:, c0:c1] = (silu * up).astype(bf16)
        part = jnp.dot(a_ref[...], wd_ref[s], preferred_element_type=f32)

        @pl.when(s == 0)
        def _():
            x2[...] = x1_ref[...] + part

        @pl.when(s > 0)
        def _():
            x2[...] += part

        @pl.when(s == NSH - 1)
        def _():
            xv = x2[...]
            g = g_ref[...]
            r = lax.rsqrt(jnp.mean(xv * xv, axis=-1, keepdims=True) + EPS)
            xhat = xv * r
            e = xhat * g - t_ref[...]
            loss_ref[...] += 0.5 * jnp.sum(jnp.mean(e * e, axis=-1, keepdims=True))
            dy = e * (1.0 / D)
            dg_ref[...] += jnp.sum(dy * xhat, axis=0, keepdims=True)
            z = dy * g
            dx = r * (z - xhat * jnp.mean(z * xhat, axis=-1, keepdims=True))
            dx_ref[...] = dx
            dxb_ref[...] = dx.astype(bf16)

    row = lambda i, s: (i, 0)
    sl = lambda i, s: (s, i, 0)
    wsl = lambda i, s: (s, 0, 0)
    return pl.pallas_call(
        body, name="mlp_fwd_loss", grid=(T // tm, NSH),
        in_specs=[pl.BlockSpec((tm, D), row), pl.BlockSpec((tm, D), row),
                  _resident((NSH, FS, D)), _resident((NSH, FS, D)), _resident((NSH, FS, D)),
                  pl.BlockSpec((tm, D), row), pl.BlockSpec((1, D), lambda i, s: (0, 0))],
        out_specs=[pl.BlockSpec((8, 128), lambda i, s: (0, 0)), pl.BlockSpec((1, D), lambda i, s: (0, 0)),
                   pl.BlockSpec((tm, D), row), pl.BlockSpec((tm, D), row),
                   pl.BlockSpec((None, tm, FS), sl), pl.BlockSpec((None, tm, FS), sl), pl.BlockSpec((None, tm, FS), sl)],
        out_shape=[jax.ShapeDtypeStruct((8, 128), f32), jax.ShapeDtypeStruct((1, D), f32),
                   jax.ShapeDtypeStruct((T, D), f32), jax.ShapeDtypeStruct((T, D), bf16)]
        + [jax.ShapeDtypeStruct((NSH, T, FS), bf16)] * 3,
        scratch_shapes=[pltpu.VMEM((tm, D), f32)],
        compiler_params=_cp("arbitrary", "arbitrary"),
    )(h2, x1, wg, wu, wd, tgt, gf)


def _mlp_bwd(dx2b, dx2, ud, silu, wg, wu, wd, x1, g2):
    tm = 512

    def body(dxb_ref, dx_ref, ud_ref, silu_ref, wg_ref, wu_ref, wd_ref, x1_ref, g_ref,
             dg_ref, du_ref, dx1_ref, dx1b_ref, dn_ref, acc):
        i = pl.program_id(0)
        s = pl.program_id(1)

        @pl.when(jnp.logical_and(i == 0, s == 0))
        def _():
            dn_ref[...] = jnp.zeros_like(dn_ref)

        dxb = dxb_ref[...]
        das = [lax.dot_general(dxb, wd_ref[s, c0:c1, :], NT, preferred_element_type=f32) for c0, c1 in FS_CHUNKS]
        for (c0, c1), da in zip(FS_CHUNKS, das):
            dg_ref[:, c0:c1] = (da * ud_ref[:, c0:c1].astype(f32)).astype(bf16)
            du_ref[:, c0:c1] = (da * silu_ref[:, c0:c1].astype(f32)).astype(bf16)
        part = jnp.dot(dg_ref[...], wg_ref[s], preferred_element_type=f32)
        part = part + jnp.dot(du_ref[...], wu_ref[s], preferred_element_type=f32)

        @pl.when(s == 0)
        def _():
            acc[...] = part

        @pl.when(s > 0)
        def _():
            acc[...] += part

        @pl.when(s == NSH - 1)
        def _():
            xv = x1_ref[...]
            r = lax.rsqrt(jnp.mean(xv * xv, axis=-1, keepdims=True) + EPS)
            xhat = xv * r
            dh = acc[...]
            dn_ref[...] += jnp.sum(dh * xhat, axis=0, keepdims=True)
            z = dh * g_ref[...]
            dx1 = dx_ref[...] + r * (z - xhat * jnp.mean(z * xhat, axis=-1, keepdims=True))
            dx1_ref[...] = dx1
            dx1b_ref[...] = dx1.astype(bf16)

    row = lambda i, s: (i, 0)
    sl = lambda i, s: (s, i, 0)
    wsl = lambda i, s: (s, 0, 0)
    return pl.pallas_call(
        body, name="mlp_bwd", grid=(T // tm, NSH),
        in_specs=[pl.BlockSpec((tm, D), row), pl.BlockSpec((tm, D), row),
                  pl.BlockSpec((None, tm, FS), sl), pl.BlockSpec((None, tm, FS), sl),
                  _resident((NSH, FS, D)), _resident((NSH, FS, D)), _resident((NSH, FS, D)),
                  pl.BlockSpec((tm, D), row), pl.BlockSpec((1, D), lambda i, s: (0, 0))],
        out_specs=[pl.BlockSpec((None, tm, FS), sl), pl.BlockSpec((None, tm, FS), sl),
                   pl.BlockSpec((tm, D), row), pl.BlockSpec((tm, D), row), pl.BlockSpec((1, D), lambda i, s: (0, 0))],
        out_shape=[jax.ShapeDtypeStruct((NSH, T, FS), bf16)] * 2
        + [jax.ShapeDtypeStruct((T, D), f32), jax.ShapeDtypeStruct((T, D), bf16), jax.ShapeDtypeStruct((1, D), f32)],
        scratch_shapes=[pltpu.VMEM((tm, D), f32)],
        compiler_params=_cp("arbitrary", "arbitrary"),
    )(dx2b, dx2, ud, silu, wg, wu, wd, x1, g2)


def _mm_tn(a, bs, name, a_sharded=False, b_sharded=False, tk=512, out_dtype=bf16):
    nb = len(bs)
    sh = NSH if (a_sharded or b_sharded) else 1
    m = a.shape[-1]
    nk = T // tk

    def body(a_ref, *refs):
        kk = pl.program_id(1)
        av = a_ref[...]
        for b_ref, o_ref, acc in zip(refs[:nb], refs[nb:2 * nb], refs[2 * nb:]):
            upd = lax.dot_general(av, b_ref[...], TN, preferred_element_type=f32)

            @pl.when(kk == 0)
            def _():
                acc[...] = upd

            @pl.when(kk > 0)
            def _():
                acc[...] += upd

            @pl.when(kk == nk - 1)
            def _():
                o_ref[...] = acc[...].astype(out_dtype)

    a_spec = (pl.BlockSpec((None, tk, m), lambda s, k: (s, k, 0)) if a_sharded
              else pl.BlockSpec((tk, m), lambda s, k: (k, 0)))
    b_specs, o_specs, o_shapes, scratch = [], [], [], []
    for b in bs:
        n = b.shape[-1]
        b_specs.append(pl.BlockSpec((None, tk, n), lambda s, k: (s, k, 0)) if b_sharded
                       else pl.BlockSpec((tk, n), lambda s, k: (k, 0)))
        scratch.append(pltpu.VMEM((m, n), f32))
        if sh > 1:
            o_specs.append(pl.BlockSpec((None, m, n), lambda s, k: (s, 0, 0)))
            o_shapes.append(jax.ShapeDtypeStruct((sh, m, n), out_dtype))
        else:
            o_specs.append(pl.BlockSpec((m, n), lambda s, k: (0, 0)))
            o_shapes.append(jax.ShapeDtypeStruct((m, n), out_dtype))
    return pl.pallas_call(
        body, name=name, grid=(sh, nk), in_specs=[a_spec] + b_specs, out_specs=o_specs, out_shape=o_shapes,
        scratch_shapes=scratch, compiler_params=_cp("arbitrary", "arbitrary"),
    )(a, *bs)


def _mm_tn_rows(a_list, b, name, tk=1024, out_dtype=bf16):
    na = len(a_list)
    n = b.shape[-1]
    nk = T // tk

    def body(*refs):
        a_refs, b_ref = refs[:na], refs[na]
        o_refs, accs = refs[na + 1:2 * na + 1], refs[2 * na + 1:]
        kk = pl.program_id(0)
        bv = b_ref[...]
        for a_ref, o_ref, acc in zip(a_refs, o_refs, accs):
            upd = lax.dot_general(a_ref[...], bv, TN, preferred_element_type=f32)

            @pl.when(kk == 0)
            def _():
                acc[...] = upd

            @pl.when(kk > 0)
            def _():
                acc[...] += upd

            @pl.when(kk == nk - 1)
            def _():
                o_ref[...] = acc[...].astype(out_dtype)

    return pl.pallas_call(
        body, name=name, grid=(nk,),
        in_specs=[pl.BlockSpec((tk, a.shape[-1]), lambda k: (k, 0)) for a in a_list] + [pl.BlockSpec((tk, n), lambda k: (k, 0))],
        out_specs=[pl.BlockSpec((a.shape[-1], n), lambda k: (0, 0)) for a in a_list],
        out_shape=[jax.ShapeDtypeStruct((a.shape[-1], n), out_dtype) for a in a_list],
        scratch_shapes=[pltpu.VMEM((a.shape[-1], n), f32) for a in a_list],
        compiler_params=_cp("arbitrary"),
    )(*a_list, b)


def _outproj_bwd(dx1b, wo):
    tm = 512

    def body(dx_ref, wo_ref, da_ref, dp_ref):
        dx = dx_ref[...]
        da_ref[...] = lax.dot_general(dx, wo_ref[0:AW, :], NT, preferred_element_type=f32).astype(bf16)
        dp_ref[...] = lax.dot_general(dx, wo_ref[AW:2 * AW, :], NT, preferred_element_type=f32)

    return pl.pallas_call(
        body, name="outproj_bwd", grid=(T // tm,),
        in_specs=[pl.BlockSpec((tm, D), lambda i: (i, 0)), _full((D, D))],
        out_specs=[pl.BlockSpec((tm, AW), lambda i: (i, 0)), pl.BlockSpec((tm, AW), lambda i: (i, 0))],
        out_shape=[jax.ShapeDtypeStruct((T, AW), bf16), jax.ShapeDtypeStruct((T, AW), f32)],
        compiler_params=_cp("parallel"),
    )(dx1b, wo)


def _pool_bwd(dpool, pooled, wp, scale):
    tm = 512
    n = T // tm

    def body(dp_ref, pb_ref, wp_ref, sc_ref, du_ref, dsc_ref, dwp_ref, ext):
        i = pl.program_id(0)

        @pl.when(i == 0)
        def _():
            ext[tm:tm + HALO, :] = jnp.zeros((HALO, AW), f32)
            dsc_ref[...] = jnp.zeros_like(dsc_ref)
            dwp_ref[...] = jnp.zeros_like(dwp_ref)

        t_idx = (n - 1 - i) * tm + lax.broadcasted_iota(jnp.int32, (tm, 1), 0)
        for g, w in enumerate(WINDOWS):
            lo, hi = 128 * g, 128 * (g + 1)
            pb = pb_ref[:, lo:hi]
            mixed = jnp.dot(pb, wp_ref[g], preferred_element_type=f32)
            dpo = dp_ref[:, lo:hi]
            dsc_ref[:, lo:hi] += jnp.sum(dpo * mixed, axis=0, keepdims=True)
            dmr = (dpo * sc_ref[:, lo:hi]).astype(bf16)
            dwp_ref[g] += lax.dot_general(pb, dmr, TN, preferred_element_type=f32)
            dpl = lax.dot_general(dmr, wp_ref[g], NT, preferred_element_type=f32)
            cnt = jnp.minimum(t_idx + 1, w).astype(f32)
            dpn = dpl / cnt
            ext[0:tm, lo:hi] = dpn
            acc = dpn
            for d in range(1, w):
                acc = acc + ext[d:d + tm, lo:hi]
            du_ref[:, lo:hi] = (acc - dpl).astype(bf16)
        ext[tm:tm + HALO, :] = ext[0:HALO, :]

    rev = lambda i: (n - 1 - i, 0)
    return pl.pallas_call(
        body, name="pool_bwd", grid=(n,),
        in_specs=[pl.BlockSpec((tm, AW), rev), pl.BlockSpec((tm, AW), rev), _full((4, 128, 128)), _full((1, AW))],
        out_specs=[pl.BlockSpec((tm, AW), rev), _full((1, AW)), _full((4, 128, 128))],
        out_shape=[jax.ShapeDtypeStruct((T, AW), bf16), jax.ShapeDtypeStruct((1, AW), f32),
                   jax.ShapeDtypeStruct((4, 128, 128), f32)],
        scratch_shapes=[pltpu.VMEM((tm + HALO, AW), f32)],
        compiler_params=_cp("arbitrary"),
    )(dpool, pooled, wp, scale)


def _attn_bwd(qkv, qaug, kaug, attn, dattn, lse, dep):
    tq = tk = ATT_T
    n = T // tq
    it, jt = _causal_steps(True)
    nsteps = it.shape[0]

    rs = 64

    def body(it_ref, jt_ref, q_ref, k_ref, v_ref, qa_ref, ka_ref, o_ref, do_ref, lse_ref, dep_ref,
             dq_ref, dqs_ref, dk_ref, dks_ref, dv_ref, dq_acc, dk_acc, dv_acc, s_sc, dp_sc, p_sc, ds_sc):
        t = pl.program_id(1)
        i = it_ref[t]
        j = jt_ref[t]

        @pl.when(t == 0)
        def _():
            dq_acc[...] = jnp.zeros_like(dq_acc)

        @pl.when(i == j)
        def _():
            dk_acc[...] = jnp.zeros_like(dk_acc)
            dv_acc[...] = jnp.zeros_like(dv_acc)

        lane = lax.broadcasted_iota(jnp.int32, (tq, 128), 1)

        def step(on_diagonal):
            q = q_ref[...] * 0.125
            k = k_ref[...]
            v = v_ref[...]
            qa = qa_ref[...]
            ka = ka_ref[...]
            do = do_ref[...]
            dd = do.astype(f32) * o_ref[...].astype(f32)
            r0 = pl.multiple_of(i * tq, tq)
            qes, kes, does, deltas = [], [], [], []
            for e in range(2):
                hm = (lane >= 64) if e else (lane < 64)
                qes.append(jnp.where(hm, q, qa))
                kes.append(jnp.where(hm, k, ka))
                does.append(jnp.where(hm, do, jnp.zeros_like(do)))
                deltas.append(jnp.sum(jnp.where(hm, dd, 0.0), axis=1, keepdims=True))
                s_sc[e] = lax.dot_general(qes[e], kes[e], NT, preferred_element_type=f32)
                dp_sc[e] = lax.dot_general(does[e], v, NT, preferred_element_type=f32)
            for e in range(2):
                for r in range(0, tq, rs):
                    s = s_sc[e, r:r + rs, :] - lse_ref[r:r + rs, 64 * e:64 * e + 1]
                    if on_diagonal:
                        row = lax.broadcasted_iota(jnp.int32, (rs, tk), 0) + r
                        col = lax.broadcasted_iota(jnp.int32, (rs, tk), 1)
                        s = jnp.where(col <= row, s, NEG)
                    p = jnp.exp(s)
                    p_sc[e, r:r + rs, :] = p.astype(bf16)
                    ds_sc[e, r:r + rs, :] = (p * (dp_sc[e, r:r + rs, :] - deltas[e][r:r + rs, :])).astype(bf16)
                dv_acc[...] += lax.dot_general(does[e], p_sc[e], TN, preferred_element_type=f32)
                dsb = ds_sc[e]
                dk_acc[e] += lax.dot_general(qes[e], dsb, TN, preferred_element_type=f32)
                dq_acc[e, pl.ds(r0, tq), :] += jnp.dot(dsb, kes[e], preferred_element_type=f32)

        @pl.when(i > j)
        def _():
            step(False)

        @pl.when(i == j)
        def _():
            step(True)

        @pl.when(i == n - 1)
        def _():
            dk0 = dk_acc[0].T
            dk1 = dk_acc[1].T
            dk_ref[...] = jnp.where(lane < 64, dk0, dk1).astype(bf16)
            dks_ref[...] = jnp.where(lane < 64, dk1, dk0)
            dv_ref[...] = dv_acc[...].T.astype(bf16)

        @pl.when(t == nsteps - 1)
        def _():
            lane_t = lax.broadcasted_iota(jnp.int32, (T, 128), 1)
            dq_ref[...] = (jnp.where(lane_t < 64, dq_acc[0], dq_acc[1]) * 0.125).astype(bf16)
            dqs_ref[...] = jnp.where(lane_t < 64, dq_acc[1], dq_acc[0])

    qmap = lambda p, t, it, jt: (it[t], p)
    grid_spec = pltpu.PrefetchScalarGridSpec(
        num_scalar_prefetch=2, grid=(PAIRS, nsteps),
        in_specs=[pl.BlockSpec((tq, 128), qmap),
                  pl.BlockSpec((tk, 128), lambda p, t, it, jt: (jt[t], PAIRS + p)),
                  pl.BlockSpec((tk, 128), lambda p, t, it, jt: (jt[t], 2 * PAIRS + p)),
                  pl.BlockSpec((tq, 128), qmap), pl.BlockSpec((tk, 128), lambda p, t, it, jt: (jt[t], p)),
                  pl.BlockSpec((tq, 128), qmap), pl.BlockSpec((tq, 128), qmap),
                  pl.BlockSpec((None, tq, 128), lambda p, t, it, jt: (p, it[t], 0)),
                  pl.BlockSpec((8, 128), lambda p, t, it, jt: (0, 0))],
        out_specs=[pl.BlockSpec((T, 128), lambda p, t, it, jt: (0, p)),
                   pl.BlockSpec((None, T, 128), lambda p, t, it, jt: (p, 0, 0)),
                   pl.BlockSpec((tk, 128), lambda p, t, it, jt: (jt[t], p)),
                   pl.BlockSpec((None, tk, 128), lambda p, t, it, jt: (p, jt[t], 0)),
                   pl.BlockSpec((tk, 128), lambda p, t, it, jt: (jt[t], p))],
        scratch_shapes=[pltpu.VMEM((2, T, 128), f32), pltpu.VMEM((2, 128, tk), f32), pltpu.VMEM((128, tk), f32),
                        pltpu.VMEM((2, tq, tk), f32), pltpu.VMEM((2, tq, tk), f32), pltpu.VMEM((2, tq, tk), bf16),
                        pltpu.VMEM((2, tq, tk), bf16)],
    )
    return pl.pallas_call(
        body, name="fox_attn_bwd", grid_spec=grid_spec,
        out_shape=[jax.ShapeDtypeStruct((T, AW), bf16), jax.ShapeDtypeStruct((PAIRS, T, 128), f32),
                   jax.ShapeDtypeStruct((T, AW), bf16), jax.ShapeDtypeStruct((PAIRS, T, 128), f32),
                   jax.ShapeDtypeStruct((T, AW), bf16)],
        compiler_params=_cp("parallel", "arbitrary"),
    )(it, jt, qkv, qkv, qkv, qaug, kaug, attn, dattn, lse, dep)


def _fox_cumsum_bwd(dqs, dks, fl, bfp):
    tb = CUMSUM_ROWS
    nb = T // tb
    hp = lax.Precision.HIGHEST

    def body(dqs_ref, dks_ref, fl_ref, b_ref, df_ref, db_ref, carry):
        i = pl.program_id(0)

        @pl.when(i == 0)
        def _():
            carry[...] = jnp.zeros_like(carry)
            db_ref[...] = jnp.zeros_like(db_ref)

        r = lax.broadcasted_iota(jnp.int32, (128, 128), 0)
        cc = lax.broadcasted_iota(jnp.int32, (128, 128), 1)
        pick = lambda even_lane, odd_lane, p: jnp.logical_or(
            jnp.logical_and(r == even_lane, cc == 2 * p), jnp.logical_and(r == odd_lane, cc == 2 * p + 1)).astype(f32)
        dc = jnp.zeros((tb, 128), f32)
        for p in range(PAIRS):
            dc = dc + jnp.dot(dqs_ref[p], pick(64, 0, p), precision=hp, preferred_element_type=f32)
            dc = dc - jnp.dot(dks_ref[p], pick(67, 3, p), precision=hp, preferred_element_type=f32)
        rt = lax.broadcasted_iota(jnp.int32, (tb, tb), 0)
        ct = lax.broadcasted_iota(jnp.int32, (tb, tb), 1)
        utri = (ct >= rt).astype(f32)
        dl = jnp.dot(utri, dc, precision=hp, preferred_element_type=f32) + carry[0:1, :]
        carry[...] = jnp.broadcast_to(dl[0:1, :], (8, 128))
        z = fl_ref[...] + b_ref[...]
        df = dl * jax.nn.sigmoid(-z)
        df_ref[...] = df.astype(bf16)
        db_ref[...] += jnp.sum(df, axis=0, keepdims=True)

    rev = lambda i: (nb - 1 - i, 0)
    return pl.pallas_call(
        body, name="fox_cumsum_bwd", grid=(nb,),
        in_specs=[pl.BlockSpec((PAIRS, tb, 128), lambda i: (0, nb - 1 - i, 0)),
                  pl.BlockSpec((PAIRS, tb, 128), lambda i: (0, nb - 1 - i, 0)),
                  pl.BlockSpec((tb, 128), rev), _full((1, 128))],
        out_specs=[pl.BlockSpec((tb, 128), rev), _full((1, 128))],
        out_shape=[jax.ShapeDtypeStruct((T, 128), bf16), jax.ShapeDtypeStruct((1, 128), f32)],
        scratch_shapes=[pltpu.VMEM((8, 128), f32)],
        compiler_params=_cp("arbitrary"),
    )(dqs, dks, fl, bfp)


def _inproj_bwd(dq, dk, dv, du, df, wm, wf, x, dx1, g1):
    tm = 512

    def body(dq_ref, dk_ref, dv_ref, du_ref, df_ref, wm_ref, wf_ref, x_ref, dx1_ref, g_ref, dx_ref, dn_ref):
        i = pl.program_id(0)

        @pl.when(i == 0)
        def _():
            dn_ref[...] = jnp.zeros_like(dn_ref)

        dh = jnp.dot(dq_ref[...], wm_ref[0:AW, :], preferred_element_type=f32)
        dh = dh + jnp.dot(dk_ref[...], wm_ref[AW:2 * AW, :], preferred_element_type=f32)
        dh = dh + jnp.dot(dv_ref[...], wm_ref[2 * AW:3 * AW, :], preferred_element_type=f32)
        dh = dh + jnp.dot(du_ref[...], wm_ref[3 * AW:4 * AW, :], preferred_element_type=f32)
        dh = dh + jnp.dot(df_ref[...], wf_ref[...], preferred_element_type=f32)
        xv = x_ref[...]
        r = lax.rsqrt(jnp.mean(xv * xv, axis=-1, keepdims=True) + EPS)
        xhat = xv * r
        dn_ref[...] += jnp.sum(dh * xhat, axis=0, keepdims=True)
        z = dh * g_ref[...]
        dx_ref[...] = dx1_ref[...] + r * (z - xhat * jnp.mean(z * xhat, axis=-1, keepdims=True))

    row = lambda i: (i, 0)
    return pl.pallas_call(
        body, name="inproj_bwd", grid=(T // tm,),
        in_specs=[pl.BlockSpec((tm, AW), row)] * 4 + [pl.BlockSpec((tm, 128), row), _full((4 * AW, D)), _full((128, D)),
                                                       pl.BlockSpec((tm, D), row), pl.BlockSpec((tm, D), row), _full((1, D))],
        out_specs=[pl.BlockSpec((tm, D), row), _full((1, D))],
        out_shape=[jax.ShapeDtypeStruct((T, D), f32), jax.ShapeDtypeStruct((1, D), f32)],
        compiler_params=_cp("arbitrary"),
    )(dq, dk, dv, du, df, wm, wf, x, dx1, g1)


def _adamw_math(w, g, m, v):
    m = B1 * m + (1.0 - B1) * g
    v = B2 * v + (1.0 - B2) * (g * g)
    m_hat = m / (1.0 - B1 ** STEP)
    v_hat = v / (1.0 - B2 ** STEP)
    delta = -LR * (m_hat / (jnp.sqrt(v_hat) + AEPS) + WD * w)
    return delta, m, v


def _adamw_shard(w, m, v, p_mine, p_other, name):
    rows, cols = w.shape
    tr = rows if rows <= IN_S else rows // 2

    def body(w_ref, m_ref, v_ref, a_ref, b_ref, g_ref, d_ref, nm_ref, nv_ref):
        g = a_ref[...].astype(f32) + b_ref[...].astype(f32)
        g_ref[...] = g
        d_ref[...], nm_ref[...], nv_ref[...] = _adamw_math(w_ref[...], g, m_ref[...], v_ref[...])

    spec = pl.BlockSpec((tr, cols), lambda i: (i, 0))
    return pl.pallas_call(
        body, name=name, grid=(rows // tr,), in_specs=[spec] * 5, out_specs=[spec] * 4,
        out_shape=[jax.ShapeDtypeStruct((rows, cols), f32)] * 4, compiler_params=_cp("parallel"),
    )(w, m, v, p_mine, p_other)


SMALL_SLOTS = ((0, 8, 128), (8, 16, 128), (16, 24, 128), (24, 28, 128), (32, 33, 8))
LOSS_ROW = 39


def _adamw_small(ws, ms, vs, parts, parts_wp):
    n = len(ws)

    def body(*refs):
        w_refs, m_refs, v_refs = refs[0:n], refs[n:2 * n], refs[2 * n:3 * n]
        p_ref, pw_ref = refs[3 * n], refs[3 * n + 1]
        outs = refs[3 * n + 2:]
        g_all = p_ref[0]
        g_wp = pw_ref[0]
        for k in range(1, 8):
            g_all = g_all + p_ref[k]
            g_wp = g_wp + pw_ref[k]
        grads = [g_all[r0:r1, 0:lanes] for r0, r1, lanes in SMALL_SLOTS] + [g_wp]
        for idx, g in enumerate(grads):
            d, nm, nv = _adamw_math(w_refs[idx][...], g, m_refs[idx][...], v_refs[idx][...])
            outs[idx][...] = g
            outs[n + idx][...] = d
            outs[2 * n + idx][...] = nm
            outs[3 * n + idx][...] = nv
        outs[4 * n][...] = g_all[LOSS_ROW:LOSS_ROW + 1, :]

    shapes = [jax.ShapeDtypeStruct(w.shape, f32) for w in ws]
    res = pl.pallas_call(
        body, name="adamw_small", out_shape=shapes * 4 + [jax.ShapeDtypeStruct((1, 128), f32)],
    )(*ws, *ms, *vs, parts, parts_wp)
    return res[:4 * n], res[4 * n]


def _sum4(recv, g, mine, name):
    _, rows, cols = recv.shape
    tr = rows if rows <= IN_S else rows // 2

    def body(mine_ref, r_ref, g_ref, o_ref):
        o_ref[...] = ((g_ref[...].astype(f32) + r_ref[0].astype(f32))
                      + (r_ref[1].astype(f32) + r_ref[2].astype(f32))).astype(bf16)

    grid_spec = pltpu.PrefetchScalarGridSpec(
        num_scalar_prefetch=1, grid=(rows // tr,),
        in_specs=[pl.BlockSpec((3, tr, cols), lambda i, m: (0, i, 0)),
                  pl.BlockSpec((None, tr, cols), lambda i, m: (m[0], i, 0))],
        out_specs=pl.BlockSpec((tr, cols), lambda i, m: (i, 0)))
    return pl.pallas_call(
        body, name=name, grid_spec=grid_spec, out_shape=jax.ShapeDtypeStruct((rows, cols), bf16),
        compiler_params=_cp("arbitrary"),
    )(mine, recv, g)


_HBM = pl.BlockSpec(memory_space=pltpu.HBM)
_SEM = pl.BlockSpec(memory_space=pltpu.SEMAPHORE)
_EFFECT = pltpu.SideEffectType.DATAFLOW_SIDE_EFFECTING


def _in_hbm(a):
    return pltpu.with_memory_space_constraint(a, pltpu.HBM)


def _mesh_pos():
    return lax.axis_index("x"), lax.axis_index("y"), lax.axis_index("c")


def _other_chips(x, y):
    return [(1 - x, y), (x, 1 - y), (1 - x, 1 - y)]


def _gather_copy(srcs, lands, send_sems, recv_sems, a, k, slot):
    x, y, c = _mesh_pos()
    cx, cy = _other_chips(x, y)[k]
    return pltpu.make_async_remote_copy(
        src_ref=srcs[a], dst_ref=lands[a].at[slot], send_sem=send_sems.at[3 * a + k], recv_sem=recv_sems.at[3 * a + k],
        device_id=(cx, cy, c), device_id_type=MESH)


def _scatter_copy(srcs, lands, send_sems, recv_sems, a, k):
    x, y, c = _mesh_pos()
    cx, cy = _other_chips(x, y)[k]
    return pltpu.make_async_remote_copy(
        src_ref=srcs[a].at[2 * cx + cy], dst_ref=lands[a].at[k], send_sem=send_sems.at[3 * a + k],
        recv_sem=recv_sems.at[3 * a + k], device_id=(cx, cy, c), device_id_type=MESH)


def _all_gather_w_in(part):
    cols = part.shape[1] // 2

    def body(src, dst, send_sems, recv_sems, loc_sem):
        x, y, c = _mesh_pos()
        mine = 2 * x + y
        chips = _other_chips(x, y)
        half = lambda ref, cc: ref.at[:, pl.ds(pl.multiple_of(cc * cols, cols), cols)]

        def over_ici(k, slot):
            cx, cy = chips[k]
            return pltpu.make_async_remote_copy(
                src_ref=half(src, c), dst_ref=half(dst.at[slot], c), send_sem=send_sems.at[k], recv_sem=recv_sems.at[k],
                device_id=(cx, cy, c), device_id_type=MESH)

        def to_sibling(k, cc):
            slot = 2 * chips[k][0] + chips[k][1]
            return pltpu.make_async_remote_copy(
                src_ref=half(dst.at[slot], cc), dst_ref=half(dst.at[slot], cc), send_sem=send_sems.at[3 + k],
                recv_sem=recv_sems.at[3 + k], device_id=(x, y, 1 - c), device_id_type=MESH)

        local = pltpu.make_async_copy(src, dst.at[mine], loc_sem.at[0])
        local.start()
        first = [over_ici(k, mine) for k in range(3)]
        for cp in first:
            cp.start()
        passed = [to_sibling(k, c) for k in range(3)]
        for k in range(3):
            over_ici(k, 2 * chips[k][0] + chips[k][1]).wait_recv()
            passed[k].start()
        for k in range(3):
            to_sibling(k, 1 - c).wait_recv()
        for cp in first + passed:
            cp.wait_send()
        local.wait()

    return pl.pallas_call(
        body, name="all_gather_w_in", in_specs=[_HBM], out_specs=_HBM,
        out_shape=jax.ShapeDtypeStruct((NSH,) + part.shape, part.dtype),
        scratch_shapes=[pltpu.SemaphoreType.DMA((6,)), pltpu.SemaphoreType.DMA((6,)), pltpu.SemaphoreType.DMA((1,))],
    )(part)


def _split_start(name, srcs, lands, n_sems, plan, dep):
    n, nl = len(srcs), len(lands)

    def body(*refs):
        src_refs, land_refs = refs[:n], refs[n:n + nl]
        send_sems, recv_sems = refs[n + nl + 1], refs[n + nl + 2]
        token = refs[-1]
        sends, _ = plan(src_refs, land_refs, send_sems, recv_sems)
        for cp in sends:
            cp.start()
        token[...] = jnp.zeros_like(token)

    outs = pl.pallas_call(
        body, name=name,
        in_specs=[_HBM] * (n + nl) + [pl.BlockSpec(memory_space=pl.ANY)],
        out_specs=[_SEM, _SEM] + [_HBM] * (n + nl) + [pl.BlockSpec(memory_space=pltpu.VMEM)],
        out_shape=[pltpu.SemaphoreType.DMA((n_sems,)), pltpu.SemaphoreType.DMA((n_sems,))]
        + [pltpu.HBM(a.shape, a.dtype) for a in list(srcs) + list(lands)] + [jax.ShapeDtypeStruct((8, 128), f32)],
        input_output_aliases={i: 2 + i for i in range(n + nl)},
        compiler_params=pltpu.CompilerParams(has_side_effects=_EFFECT),
    )(*[_in_hbm(a) for a in list(srcs) + list(lands)], dep)
    return outs[0], outs[1], list(outs[2:2 + n]), list(outs[2 + n:2 + n + nl]), outs[-1]


def _split_wait(name, send_sems, recv_sems, srcs, lands, after, plan):
    n, nl = len(srcs), len(lands)

    def body(*refs):
        src_refs, land_refs = refs[:n], refs[n:n + nl]
        s_sems, r_sems = refs[n + nl], refs[n + nl + 1]
        sends, recvs = plan(src_refs, land_refs, s_sems, r_sems)
        for cp in recvs:
            cp.wait_recv()
        for cp in sends:
            cp.wait_send()

    outs = pl.pallas_call(
        body, name=name,
        in_specs=[_HBM] * (n + nl) + [_SEM, _SEM, pl.BlockSpec(memory_space=pl.ANY)],
        out_specs=[_HBM] * (n + nl),
        out_shape=[pltpu.HBM(a.shape, a.dtype) for a in list(srcs) + list(lands)],
        input_output_aliases={i: i for i in range(n + nl)},
        compiler_params=pltpu.CompilerParams(has_side_effects=_EFFECT),
    )(*srcs, *lands, send_sems, recv_sems, after)
    return list(outs[:n]), list(outs[n:])


def _gather_plan(srcs, lands, ss, rs):
    x, y, _ = _mesh_pos()
    chips = _other_chips(x, y)
    sends = [_gather_copy(srcs, lands, ss, rs, a, k, 2 * x + y) for a in range(len(srcs)) for k in range(3)]
    recvs = [_gather_copy(srcs, lands, ss, rs, a, k, 2 * chips[k][0] + chips[k][1])
             for a in range(len(srcs)) for k in range(3)]
    return sends, recvs


def _scatter_plan(srcs, lands, ss, rs):
    cps = [_scatter_copy(srcs, lands, ss, rs, a, k) for a in range(len(srcs)) for k in range(3)]
    return cps, cps


def _scatter_and_spread_plan(srcs, lands, ss, rs):
    x, y, c = _mesh_pos()
    me = 4 * x + 2 * y + c
    n = len(srcs) - 1
    cps = [_scatter_copy(srcs[:n], lands[:n], ss, rs, a, k) for a in range(n) for k in range(3)]
    for f in range(1, 8):
        peer = ((x + (f >> 2)) % 2, (y + ((f >> 1) & 1)) % 2, (c + (f & 1)) % 2)
        cps.append(pltpu.make_async_remote_copy(
            src_ref=srcs[n], dst_ref=lands[n].at[me], send_sem=ss.at[3 * n - 1 + f], recv_sem=rs.at[3 * n - 1 + f],
            device_id=peer, device_id_type=MESH))
    return cps, cps


def _swap_with_sibling(parts, name):
    n = len(parts)

    def body(*refs):
        srcs, dsts = refs[:n], refs[n:2 * n]
        send_sems, recv_sems = refs[2 * n:]
        x, y, c = _mesh_pos()
        cps = [pltpu.make_async_remote_copy(src_ref=srcs[a], dst_ref=dsts[a], send_sem=send_sems.at[a],
                                            recv_sem=recv_sems.at[a], device_id=(x, y, 1 - c), device_id_type=MESH)
               for a in range(n)]
        for cp in cps:
            cp.start()
        for cp in cps:
            cp.wait_recv()
        for cp in cps:
            cp.wait_send()

    return pl.pallas_call(
        body, name=name, in_specs=[_HBM] * n, out_specs=[_HBM] * n,
        out_shape=[jax.ShapeDtypeStruct(p.shape, p.dtype) for p in parts],
        scratch_shapes=[pltpu.SemaphoreType.DMA((n,)), pltpu.SemaphoreType.DMA((n,))],
    )(*parts)


def _forward(x, tgt, wm, wf, mlp_w_fn, g1, bfp, wp, scale, g2, gf, dep):
    h, qkv, u, fl = _rms_inproj(x, g1, wm, wf, dep)
    qaug, kaug = _fox_cumsum(fl, bfp)
    attn, lse = _attn_fwd(qkv, qaug, kaug)
    pooled, pool = _pool_fwd(u, wp, scale)
    wo, wgt, wut, wd = mlp_w_fn(attn)
    x1, h2 = _outproj(x, attn, pool, wo, g2)
    loss, dgf, dx2, dx2b, ud, silu, a_b = _mlp_fwd_loss(h2, x1, wgt, wut, wd, tgt, gf)
    saved = dict(h=h, qkv=qkv, fl=fl, qaug=qaug, kaug=kaug, attn=attn, lse=lse, pooled=pooled, pool=pool, x1=x1, h2=h2,
                 ud=ud, silu=silu, a_b=a_b, wo=wo, wgt=wgt, wut=wut, wd=wd)
    return loss, dgf, dx2, dx2b, saved


def _backward_mlp(sv, dx2, dx2b, g2):
    dgate, dup, dx1, dx1b, dg2 = _mlp_bwd(dx2b, dx2, sv["ud"], sv["silu"], sv["wgt"], sv["wut"], sv["wd"], sv["x1"], g2)
    (dwd,) = _mm_tn(sv["a_b"], [dx2b], "dw_down", a_sharded=True, tk=T)
    (dwgt,) = _mm_tn(dgate, [sv["h2"]], "dw_gate", a_sharded=True, tk=T)
    (dwut,) = _mm_tn(dup, [sv["h2"]], "dw_up", a_sharded=True, tk=T)
    return dx1, dx1b, dg2, (dwgt, dwut, dwd)


def _backward_outproj_pool(sv, dx1b, wp, scale):
    dattn, dpool = _outproj_bwd(dx1b, sv["wo"])
    dwo_a, = _mm_tn(sv["attn"], [dx1b], "dw_out_attn", tk=2048)
    dwo_p, = _mm_tn(sv["pool"], [dx1b], "dw_out_pool", tk=2048)
    dwo = jnp.concatenate([dwo_a, dwo_p], axis=0).reshape(NSH, D // NSH, D)
    du, dscale, dwp = _pool_bwd(dpool, sv["pooled"], wp, scale)
    return dattn, dwo, du, dscale, dwp


def _backward_attn_inproj(sv, x, dx1, dattn, du, wm, wf, g1, bfp, dep):
    dq, dqs, dk, dks, dv = _attn_bwd(sv["qkv"], sv["qaug"], sv["kaug"], sv["attn"], dattn, sv["lse"], dep)
    df, dbf = _fox_cumsum_bwd(dqs, dks, sv["fl"], bfp)
    dx, dg1 = _inproj_bwd(dq, dk, dv, du, df, wm, wf, x, dx1, g1)
    dwq, dwk, dwv, dwu_in, dwf = _mm_tn_rows([dq, dk, dv, du, df], sv["h"], "dw_in")
    dwin = jnp.concatenate([dwq, dwk, dwv, dwf[0:8], dwu_in], axis=0)
    return dx, dg1, dbf, dwin.reshape(NSH, IN_S, D)


def kernel(x, norm1_g, w_in, b_forget, w_pool, pool_scale, w_out, norm2_g, w_gate, w_up, w_down, final_g, loss_target, m_norm1_g, m_w_in, m_b_forget, m_w_pool, m_pool_scale, m_w_out, m_norm2_g, m_w_gate, m_w_up, m_w_down, m_final_g, v_norm1_g, v_w_in, v_b_forget, v_w_pool, v_pool_scale, v_w_out, v_norm2_g, v_w_gate, v_w_up, v_w_down, v_final_g):
    mine = (2 * lax.axis_index("x") + lax.axis_index("y")).astype(jnp.int32)
    mine1 = mine.reshape(1)
    tr = lambda a: jnp.transpose(a[0])

    win4 = _all_gather_w_in(tr(w_in).astype(bf16))
    later = [w_out[0].astype(bf16), tr(w_gate).astype(bf16), tr(w_up).astype(bf16), w_down[0].astype(bf16)]
    lands = [lax.dynamic_update_slice(lax.empty((NSH,) + p.shape, bf16), p[None], (mine, 0, 0)) for p in later]
    ag_send, ag_recv, later_thru, lands_thru, ag_token = _split_start("all_gather_start", later, lands, 12, _gather_plan,
                                                                      win4)
    win = win4.reshape(IN_W, D)
    wm = jnp.concatenate([win[0:3 * AW], win[3 * AW + 8:]], axis=0)
    wf = jnp.pad(win[3 * AW:3 * AW + 8], ((0, 120), (0, 0)))
    bfp = jnp.pad(b_forget, ((0, 0), (0, 120)))
    wp = w_pool[0].astype(bf16)
    gf = final_g.reshape(1, D)

    def later_weights(after):
        _, (wo4, wgt, wut, wd) = _split_wait("all_gather_wait", ag_send, ag_recv, later_thru, lands_thru, after, _gather_plan)
        return wo4.reshape(D, D), wgt, wut, wd

    xe, tgt = x[0], loss_target[0]
    loss_v, dgf, dx2, dx2b, sv = _forward(xe, tgt, wm, wf, later_weights, norm1_g, bfp, wp, pool_scale, norm2_g, gf, ag_token)
    dx1, dx1b, dg2, mlp_grads = _backward_mlp(sv, dx2, dx2b, norm2_g)
    dattn, dwo, du, dscale, dwp = _backward_outproj_pool(sv, dx1b, wp, pool_scale)
    me = (4 * lax.axis_index("x") + 2 * lax.axis_index("y") + lax.axis_index("c")).astype(jnp.int32)
    dwp = dwp.reshape(512, 128)
    first = [dwo] + list(mlp_grads) + [dwp]
    first_lands = [lax.empty((3,) + g.shape[1:], bf16) for g in first[:4]]
    first_lands.append(lax.dynamic_update_slice(lax.empty((8, 512, 128), f32), dwp[None], (me, 0, 0)))
    rs_send, rs_recv, first_thru, first_lands_thru, rs_token = _split_start(
        "reduce_scatter_start", first, first_lands, 19, _scatter_and_spread_plan, du)
    dx, dg1, dbf, dwin = _backward_attn_inproj(sv, xe, dx1, dattn, du, wm, wf, norm1_g, bfp, rs_token)

    pad8 = lambda r: jnp.pad(r, ((0, 8 - r.shape[0]), (0, 0)))
    loss_rows = jnp.concatenate([dbf, jnp.zeros((6, 128), f32), loss_v[0:1, :]], axis=0)
    small = jnp.concatenate([dg1.reshape(8, 128), dg2.reshape(8, 128), dgf.reshape(8, 128), pad8(dscale.reshape(4, 128)),
                             loss_rows], axis=0)
    small_land = lax.dynamic_update_slice(lax.empty((8, SMALL_ROWS, 128), f32), small[None], (me, 0, 0))
    tail_send, tail_recv, tail_thru, tail_lands_thru, tail_token = _split_start(
        "tail_start", [dwin, small], [lax.empty((3,) + dwin.shape[1:], bf16), small_land], 10, _scatter_and_spread_plan,
        dx)
    first_thru, first_recv = _split_wait("reduce_scatter_wait", rs_send, rs_recv, first_thru, first_lands_thru, tail_token,
                                         _scatter_and_spread_plan)
    wp_all = first_recv[4]
    ws = [tr(w_in), w_out[0], tr(w_gate), tr(w_up), w_down[0]]
    ms = [tr(m_w_in), m_w_out[0], tr(m_w_gate), tr(m_w_up), m_w_down[0]]
    vs = [tr(v_w_in), v_w_out[0], tr(v_w_gate), tr(v_w_up), v_w_down[0]]
    partial = [_sum4(r, g, mine1, f"sum4_{i + 1}") for i, (r, g) in enumerate(zip(first_recv[:4], first_thru[:4]))]
    other = _swap_with_sibling(partial, "swap_first")
    big = [_adamw_shard(ws[i + 1], ms[i + 1], vs[i + 1], partial[i], other[i], f"adamw_{i + 1}") for i in range(4)]
    (dwin_thru, _), (in_recv_land, small_all) = _split_wait("tail_wait", tail_send, tail_recv, tail_thru, tail_lands_thru,
                                                            big[3][0], _scatter_and_spread_plan)
    partial_in = _sum4(in_recv_land, dwin_thru, mine1, "sum4_0")
    (other_in,) = _swap_with_sibling([partial_in], "swap_in")
    big = [_adamw_shard(ws[0], ms[0], vs[0], partial_in, other_in, "adamw_0")] + big

    small_names = ["norm1_g", "norm2_g", "final_g", "pool_scale", "b_forget", "w_pool"]
    rows = lambda a, b, c, d, e, f: [a.reshape(8, 128), b.reshape(8, 128), c.reshape(8, 128), d.reshape(4, 128),
                                     e.reshape(1, 8), f.reshape(512, 128)]
    sm, loss_row = _adamw_small(rows(norm1_g, norm2_g, final_g, pool_scale, b_forget, w_pool),
                                rows(m_norm1_g, m_norm2_g, m_final_g, m_pool_scale, m_b_forget, m_w_pool),
                                rows(v_norm1_g, v_norm2_g, v_final_g, v_pool_scale, v_b_forget, v_w_pool), small_all, wp_all)
    small_shape = dict(norm1_g=(1, D), norm2_g=(1, D), final_g=(D,), pool_scale=(1, AW), b_forget=(1, 8),
                       w_pool=(1, 4, 128, 128))

    order = ["norm1_g", "w_in", "b_forget", "w_pool", "pool_scale", "w_out", "norm2_g", "w_gate", "w_up", "w_down", "final_g"]
    big_idx = {"w_in": 0, "w_out": 1, "w_gate": 2, "w_up": 3, "w_down": 4}
    outs = [loss_row[0, 0], dx[None]]
    for kind in range(4):
        for name in order:
            if name in ("w_in", "w_gate", "w_up"):
                outs.append(jnp.transpose(big[big_idx[name]][kind])[None])
            elif name in big_idx:
                outs.append(big[big_idx[name]][kind][None])
            else:
                outs.append(sm[6 * kind + small_names.index(name)].reshape(small_shape[name]))
    return tuple(outs)
```

```python
import jax
import jax.numpy as jnp
import numpy as np
from jax import lax
from jax.experimental import pallas as pl
from jax.experimental.pallas import tpu as pltpu

f32 = jnp.float32
bf16 = jnp.bfloat16

T = 4096
D = 1024
NSH = 4
IN_W = 2056
IN_S = IN_W // NSH
AW = 512
PAIRS = 4
FF = 2816
FS = FF // NSH
WINDOWS = (2, 4, 8, 16)
HALO = 16
EPS = 1e-6
NEG = -1e30
LR, B1, B2, AEPS, WD, STEP = 0.001, 0.9, 0.999, 1e-08, 0.01, 10
SMALL_ROWS = 40

NT = (((1,), (1,)), ((), ()))
TN = (((0,), (0,)), ((), ()))

MESH = pl.DeviceIdType.MESH


def _cp(*sem):
    return pltpu.CompilerParams(dimension_semantics=sem)


def _full(shape):
    n = len(shape)
    return pl.BlockSpec(shape, lambda *_: (0,) * n)


def _resident(shape):
    n = len(shape)
    return pl.BlockSpec(shape, lambda *_: (0,) * n, pipeline_mode=pl.Buffered(1))


def _rms_inproj(x, g1, wm, wf, dep):
    tm = 1024

    def body(x_ref, g_ref, wm_ref, wf_ref, dep_ref, h_ref, qkv_ref, u_ref, fl_ref):
        xv = x_ref[...]
        r = lax.rsqrt(jnp.mean(xv * xv, axis=-1, keepdims=True) + EPS)
        h = (xv * r * g_ref[...]).astype(bf16)
        h_ref[...] = h
        qkv_ref[...] = lax.dot_general(h, wm_ref[0:3 * AW, :], NT, preferred_element_type=f32).astype(bf16)
        u_ref[...] = lax.dot_general(h, wm_ref[3 * AW:4 * AW, :], NT, preferred_element_type=f32)
        fl_ref[...] = lax.dot_general(h, wf_ref[...], NT, preferred_element_type=f32)

    return pl.pallas_call(
        body, name="rms_inproj", grid=(T // tm,),
        in_specs=[pl.BlockSpec((tm, D), lambda i: (i, 0)), _full((1, D)), _full((4 * AW, D)), _full((128, D)),
                  _full((8, 128))],
        out_specs=[pl.BlockSpec((tm, D), lambda i: (i, 0)), pl.BlockSpec((tm, 3 * AW), lambda i: (i, 0)),
                   pl.BlockSpec((tm, AW), lambda i: (i, 0)), pl.BlockSpec((tm, 128), lambda i: (i, 0))],
        out_shape=[jax.ShapeDtypeStruct((T, D), bf16), jax.ShapeDtypeStruct((T, 3 * AW), bf16),
                   jax.ShapeDtypeStruct((T, AW), f32), jax.ShapeDtypeStruct((T, 128), f32)],
        compiler_params=_cp("parallel"),
    )(x, g1, wm, wf, dep)


CUMSUM_ROWS = 512
FS_CHUNKS = ((0, 256), (256, 512), (512, FS))


def _log_sigmoid(z):
    return jnp.minimum(z, 0.0) - jnp.log(1.0 + jnp.exp(-jnp.abs(z)))


def _fox_cumsum(fl, bfp):
    tb = CUMSUM_ROWS
    nb = T // tb

    def body(fl_ref, b_ref, qa_ref, ka_ref, carry):
        i = pl.program_id(0)

        @pl.when(i == 0)
        def _():
            carry[...] = jnp.zeros_like(carry)

        lf = _log_sigmoid(fl_ref[...] + b_ref[...])
        r = lax.broadcasted_iota(jnp.int32, (tb, tb), 0)
        cc = lax.broadcasted_iota(jnp.int32, (tb, tb), 1)
        ltri = (cc <= r).astype(f32)
        cb = jnp.dot(ltri, lf, precision=lax.Precision.HIGHEST, preferred_element_type=f32) + carry[0:1, :]
        carry[...] = jnp.broadcast_to(cb[tb - 1:tb, :], (8, 128))
        hi = cb.astype(bf16)
        r1 = cb - hi.astype(f32)
        mid = r1.astype(bf16)
        lo = (r1 - mid.astype(f32)).astype(bf16)
        head = lax.broadcasted_iota(jnp.int32, (128, AW), 0)
        col = lax.broadcasted_iota(jnp.int32, (128, AW), 1)
        base = 128 * (head >> 1) + 64 * (1 - (head & 1))
        place = lambda off: jnp.logical_and(col == base + off, head < 8).astype(bf16)
        mm = lambda a, off: jnp.dot(a, place(off), preferred_element_type=f32)
        cq = mm(hi, 0) + mm(mid, 1) + mm(lo, 2)
        ck = mm(hi, 3) + mm(mid, 4) + mm(lo, 5)
        within = jnp.bitwise_and(lax.broadcasted_iota(jnp.int32, (tb, AW), 1), 63)
        qa_ref[...] = jnp.where(jnp.logical_and(within >= 3, within <= 5), 1.0, cq).astype(bf16)
        ka_ref[...] = jnp.where(within <= 2, 1.0, -ck).astype(bf16)

    return pl.pallas_call(
        body, name="fox_cumsum", grid=(nb,),
        in_specs=[pl.BlockSpec((tb, 128), lambda i: (i, 0)), _full((1, 128))],
        out_specs=[pl.BlockSpec((tb, AW), lambda i: (i, 0)), pl.BlockSpec((tb, AW), lambda i: (i, 0))],
        out_shape=[jax.ShapeDtypeStruct((T, AW), bf16), jax.ShapeDtypeStruct((T, AW), bf16)],
        scratch_shapes=[pltpu.VMEM((8, 128), f32)],
        compiler_params=_cp("arbitrary"),
    )(fl, bfp)


ATT_T = 512


def _causal_steps(key_major):
    n = T // ATT_T
    if key_major:
        pairs = [(i, j) for j in range(n) for i in range(j, n)]
    else:
        pairs = [(i, j) for i in range(n) for j in range(i + 1)]
    it = np.array([p[0] for p in pairs], np.int32)
    jt = np.array([p[1] for p in pairs], np.int32)
    return jnp.asarray(it), jnp.asarray(jt)


def _attn_fwd(qkv, qaug, kaug):
    tq = tk = ATT_T
    it, jt = _causal_steps(False)
    nsteps = it.shape[0]

    rs = 64

    def body(it_ref, jt_ref, q_ref, k_ref, v_ref, qa_ref, ka_ref, o_ref, lse_ref, m_sc, acc_sc, s_sc, p_sc, alpha_sc):
        t = pl.program_id(1)
        i = it_ref[t]
        j = jt_ref[t]

        @pl.when(j == 0)
        def _():
            m_sc[...] = jnp.full_like(m_sc, NEG)
            acc_sc[...] = jnp.zeros_like(acc_sc)

        lane = lax.broadcasted_iota(jnp.int32, (tq, 128), 1)
        spare = (64, 0)

        def step(on_diagonal):
            q = q_ref[...] * 0.125
            k = k_ref[...]
            v = v_ref[...]
            qa = qa_ref[...]
            ka = ka_ref[...]
            for e in range(2):
                hm = (lane >= 64) if e else (lane < 64)
                s_sc[e] = lax.dot_general(jnp.where(hm, q, qa), jnp.where(hm, k, ka), NT, preferred_element_type=f32)
            for e in range(2):
                hm = (lane >= 64) if e else (lane < 64)
                for r in range(0, tq, rs):
                    s = s_sc[e, r:r + rs, :]
                    if on_diagonal:
                        row = lax.broadcasted_iota(jnp.int32, (rs, tk), 0) + r
                        col = lax.broadcasted_iota(jnp.int32, (rs, tk), 1)
                        s = jnp.where(col <= row, s, NEG)
                    m_prev = m_sc[e, r:r + rs, :]
                    m_new = jnp.maximum(m_prev, jnp.max(s, axis=1, keepdims=True))
                    p_sc[e, r:r + rs, :] = jnp.exp(s - jnp.tile(m_new, (1, tk // 128))).astype(bf16)
                    alpha_sc[e, r:r + rs, :] = jnp.exp(m_prev - m_new)
                    m_sc[e, r:r + rs, :] = m_new
            for e in range(2):
                hm = (lane >= 64) if e else (lane < 64)
                ve = jnp.where(hm, v, (lane == spare[e]).astype(bf16))
                acc_sc[e] = alpha_sc[e] * acc_sc[e] + jnp.dot(p_sc[e], ve, preferred_element_type=f32)

        @pl.when(j < i)
        def _():
            step(False)

        @pl.when(j == i)
        def _():
            step(True)
            l0 = acc_sc[0][:, spare[0]:spare[0] + 1]
            l1 = acc_sc[1][:, spare[1]:spare[1] + 1]
            o_ref[...] = jnp.where(lane < 64, acc_sc[0] / l0, acc_sc[1] / l1).astype(bf16)
            lse_ref[...] = jnp.where(lane < 64, m_sc[0] + jnp.log(l0), m_sc[1] + jnp.log(l1))

    qmap = lambda p, t, it, jt: (it[t], p)
    kmap = lambda p, t, it, jt: (jt[t], p)
    grid_spec = pltpu.PrefetchScalarGridSpec(
        num_scalar_prefetch=2, grid=(PAIRS, nsteps),
        in_specs=[pl.BlockSpec((tq, 128), qmap),
                  pl.BlockSpec((tk, 128), lambda p, t, it, jt: (jt[t], PAIRS + p)),
                  pl.BlockSpec((tk, 128), lambda p, t, it, jt: (jt[t], 2 * PAIRS + p)),
                  pl.BlockSpec((tq, 128), qmap), pl.BlockSpec((tk, 128), kmap)],
        out_specs=[pl.BlockSpec((tq, 128), qmap),
                   pl.BlockSpec((None, tq, 128), lambda p, t, it, jt: (p, it[t], 0))],
        scratch_shapes=[pltpu.VMEM((2, tq, 128), f32), pltpu.VMEM((2, tq, 128), f32), pltpu.VMEM((2, tq, tk), f32),
                        pltpu.VMEM((2, tq, tk), bf16), pltpu.VMEM((2, tq, 128), f32)],
    )
    return pl.pallas_call(
        body, name="fox_attn_fwd", grid_spec=grid_spec,
        out_shape=[jax.ShapeDtypeStruct((T, AW), bf16), jax.ShapeDtypeStruct((PAIRS, T, 128), f32)],
        compiler_params=_cp("parallel", "arbitrary"),
    )(it, jt, qkv, qkv, qkv, qaug, kaug)


def _pool_fwd(u, wp, scale):
    tm = 512

    def body(u_ref, wp_ref, sc_ref, pooled_ref, pool_ref, ext):
        i = pl.program_id(0)

        @pl.when(i == 0)
        def _():
            ext[0:HALO, :] = jnp.zeros((HALO, AW), f32)

        uv = u_ref[...]
        ext[HALO:HALO + tm, :] = uv
        t_idx = i * tm + lax.broadcasted_iota(jnp.int32, (tm, 1), 0)
        for g, w in enumerate(WINDOWS):
            lo, hi = 128 * g, 128 * (g + 1)
            ug = uv[:, lo:hi]
            acc = ug
            for d in range(1, w):
                acc = acc + ext[HALO - d:HALO - d + tm, lo:hi]
            cnt = jnp.minimum(t_idx + 1, w).astype(f32)
            pb = (acc / cnt - ug).astype(bf16)
            pooled_ref[:, lo:hi] = pb
            mixed = jnp.dot(pb, wp_ref[g], preferred_element_type=f32)
            pool_ref[:, lo:hi] = (mixed * sc_ref[:, lo:hi]).astype(bf16)
        ext[0:HALO, :] = uv[tm - HALO:tm, :]

    return pl.pallas_call(
        body, name="pool_fwd", grid=(T // tm,),
        in_specs=[pl.BlockSpec((tm, AW), lambda i: (i, 0)), _full((4, 128, 128)), _full((1, AW))],
        out_specs=[pl.BlockSpec((tm, AW), lambda i: (i, 0)), pl.BlockSpec((tm, AW), lambda i: (i, 0))],
        out_shape=[jax.ShapeDtypeStruct((T, AW), bf16), jax.ShapeDtypeStruct((T, AW), bf16)],
        scratch_shapes=[pltpu.VMEM((tm + HALO, AW), f32)],
        compiler_params=_cp("arbitrary"),
    )(u, wp, scale)


def _outproj(x, attn, pool, wo, g2):
    tm = 1024

    def body(x_ref, a_ref, p_ref, wo_ref, g_ref, x1_ref, h2_ref):
        x1 = x_ref[...] + jnp.dot(a_ref[...], wo_ref[0:AW, :], preferred_element_type=f32)
        x1 = x1 + jnp.dot(p_ref[...], wo_ref[AW:2 * AW, :], preferred_element_type=f32)
        x1_ref[...] = x1
        r = lax.rsqrt(jnp.mean(x1 * x1, axis=-1, keepdims=True) + EPS)
        h2_ref[...] = (x1 * r * g_ref[...]).astype(bf16)

    return pl.pallas_call(
        body, name="outproj", grid=(T // tm,),
        in_specs=[pl.BlockSpec((tm, D), lambda i: (i, 0)), pl.BlockSpec((tm, AW), lambda i: (i, 0)),
                  pl.BlockSpec((tm, AW), lambda i: (i, 0)), _full((D, D)), _full((1, D))],
        out_specs=[pl.BlockSpec((tm, D), lambda i: (i, 0)), pl.BlockSpec((tm, D), lambda i: (i, 0))],
        out_shape=[jax.ShapeDtypeStruct((T, D), f32), jax.ShapeDtypeStruct((T, D), bf16)],
        compiler_params=_cp("parallel"),
    )(x, attn, pool, wo, g2)


def _mlp_fwd_loss(h2, x1, wg, wu, wd, tgt, gf):
    tm = 512

    def body(h_ref, x1_ref, wg_ref, wu_ref, wd_ref, t_ref, g_ref,
             loss_ref, dg_ref, dx_ref, dxb_ref, ud_ref, silu_ref, a_ref, x2):
        i = pl.program_id(0)
        s = pl.program_id(1)

        @pl.when(jnp.logical_and(i == 0, s == 0))
        def _():
            loss_ref[...] = jnp.zeros_like(loss_ref)
            dg_ref[...] = jnp.zeros_like(dg_ref)

        h = h_ref[...]
        gus = [(lax.dot_general(h, wg_ref[s, c0:c1, :], NT, preferred_element_type=f32),
                lax.dot_general(h, wu_ref[s, c0:c1, :], NT, preferred_element_type=f32)) for c0, c1 in FS_CHUNKS]
        for (c0, c1), (gate, up) in zip(FS_CHUNKS, gus):
            sg = jax.nn.sigmoid(gate)
            silu = gate * sg
            ud_ref[:, c0:c1] = (up * (sg * (1.0 + gate * (1.0 - sg)))).astype(bf16)
            silu_ref[:, c0:c1] = silu.astype(bf16)
            a_ref[:, c0:c1] = (silu * up).astype(bf16)
        part = jnp.dot(a_ref[...], wd_ref[s], preferred_element_type=f32)

        @pl.when(s == 0)
        def _():
            x2[...] = x1_ref[...] + part

        @pl.when(s > 0)
        def _():
            x2[...] += part

        @pl.when(s == NSH - 1)
        def _():
            xv = x2[...]
            g = g_ref[...]
            r = lax.rsqrt(jnp.mean(xv * xv, axis=-1, keepdims=True) + EPS)
            xhat = xv * r
            e = xhat * g - t_ref[...]
            loss_ref[...] += 0.5 * jnp.sum(jnp.mean(e * e, axis=-1, keepdims=True))
            dy = e * (1.0 / D)
            dg_ref[...] += jnp.sum(dy * xhat, axis=0, keepdims=True)
            z = dy * g
            dx = r * (z - xhat * jnp.mean(z * xhat, axis=-1, keepdims=True))
            dx_ref[...] = dx
            dxb_ref[...] = dx.astype(bf16)

    row = lambda i, s: (i, 0)
    sl = lambda i, s: (s, i, 0)
    wsl = lambda i, s: (s, 0, 0)
    return pl.pallas_call(
        body, name="mlp_fwd_loss", grid=(T // tm, NSH),
        in_specs=[pl.BlockSpec((tm, D), row), pl.BlockSpec((tm, D), row),
                  _resident((NSH, FS, D)), _resident((NSH, FS, D)), _resident((NSH, FS, D)),
                  pl.BlockSpec((tm, D), row), pl.BlockSpec((1, D), lambda i, s: (0, 0))],
        out_specs=[pl.BlockSpec((8, 128), lambda i, s: (0, 0)), pl.BlockSpec((1, D), lambda i, s: (0, 0)),
                   pl.BlockSpec((tm, D), row), pl.BlockSpec((tm, D), row),
                   pl.BlockSpec((None, tm, FS), sl), pl.BlockSpec((None, tm, FS), sl), pl.BlockSpec((None, tm, FS), sl)],
        out_shape=[jax.ShapeDtypeStruct((8, 128), f32), jax.ShapeDtypeStruct((1, D), f32),
                   jax.ShapeDtypeStruct((T, D), f32), jax.ShapeDtypeStruct((T, D), bf16)]
        + [jax.ShapeDtypeStruct((NSH, T, FS), bf16)] * 3,
        scratch_shapes=[pltpu.VMEM((tm, D), f32)],
        compiler_params=_cp("arbitrary", "arbitrary"),
    )(h2, x1, wg, wu, wd, tgt, gf)


def _mlp_bwd(dx2b, dx2, ud, silu, wg, wu, wd, x1, g2):
    tm = 512

    def body(dxb_ref, dx_ref, ud_ref, silu_ref, wg_ref, wu_ref, wd_ref, x1_ref, g_ref,
             dg_ref, du_ref, dx1_ref, dx1b_ref, dn_ref, acc):
        i = pl.program_id(0)
        s = pl.program_id(1)

        @pl.when(jnp.logical_and(i == 0, s == 0))
        def _():
            dn_ref[...] = jnp.zeros_like(dn_ref)

        dxb = dxb_ref[...]
        das = [lax.dot_general(dxb, wd_ref[s, c0:c1, :], NT, preferred_element_type=f32) for c0, c1 in FS_CHUNKS]
        for (c0, c1), da in zip(FS_CHUNKS, das):
            dg_ref[:, c0:c1] = (da * ud_ref[:, c0:c1].astype(f32)).astype(bf16)
            du_ref[:, c0:c1] = (da * silu_ref[:, c0:c1].astype(f32)).astype(bf16)
        part = jnp.dot(dg_ref[...], wg_ref[s], preferred_element_type=f32)
        part = part + jnp.dot(du_ref[...], wu_ref[s], preferred_element_type=f32)

        @pl.when(s == 0)
        def _():
            acc[...] = part

        @pl.when(s > 0)
        def _():
            acc[...] += part

        @pl.when(s == NSH - 1)
        def _():
            xv = x1_ref[...]
            r = lax.rsqrt(jnp.mean(xv * xv, axis=-1, keepdims=True) + EPS)
            xhat = xv * r
            dh = acc[...]
            dn_ref[...] += jnp.sum(dh * xhat, axis=0, keepdims=True)
            z = dh * g_ref[...]
            dx1 = dx_ref[...] + r * (z - xhat * jnp.mean(z * xhat, axis=-1, keepdims=True))
            dx1_ref[...] = dx1
            dx1b_ref[...] = dx1.astype(bf16)

    row = lambda i, s: (i, 0)
    sl = lambda i, s: (s, i, 0)
    wsl = lambda i, s: (s, 0, 0)
    return pl.pallas_call(
        body, name="mlp_bwd", grid=(T // tm, NSH),
        in_specs=[pl.BlockSpec((tm, D), row), pl.BlockSpec((tm, D), row),
                  pl.BlockSpec((None, tm, FS), sl), pl.BlockSpec((None, tm, FS), sl),
                  _resident((NSH, FS, D)), _resident((NSH, FS, D)), _resident((NSH, FS, D)),
                  pl.BlockSpec((tm, D), row), pl.BlockSpec((1, D), lambda i, s: (0, 0))],
        out_specs=[pl.BlockSpec((None, tm, FS), sl), pl.BlockSpec((None, tm, FS), sl),
                   pl.BlockSpec((tm, D), row), pl.BlockSpec((tm, D), row), pl.BlockSpec((1, D), lambda i, s: (0, 0))],
        out_shape=[jax.ShapeDtypeStruct((NSH, T, FS), bf16)] * 2
        + [jax.ShapeDtypeStruct((T, D), f32), jax.ShapeDtypeStruct((T, D), bf16), jax.ShapeDtypeStruct((1, D), f32)],
        scratch_shapes=[pltpu.VMEM((tm, D), f32)],
        compiler_params=_cp("arbitrary", "arbitrary"),
    )(dx2b, dx2, ud, silu, wg, wu, wd, x1, g2)


def _mm_tn(a, bs, name, a_sharded=False, b_sharded=False, tk=512, out_dtype=bf16):
    nb = len(bs)
    sh = NSH if (a_sharded or b_sharded) else 1
    m = a.shape[-1]
    nk = T // tk

    def body(a_ref, *refs):
        kk = pl.program_id(1)
        av = a_ref[...]
        for b_ref, o_ref, acc in zip(refs[:nb], refs[nb:2 * nb], refs[2 * nb:]):
            upd = lax.dot_general(av, b_ref[...], TN, preferred_element_type=f32)

            @pl.when(kk == 0)
            def _():
                acc[...] = upd

            @pl.when(kk > 0)
            def _():
                acc[...] += upd

            @pl.when(kk == nk - 1)
            def _():
                o_ref[...] = acc[...].astype(out_dtype)

    a_spec = (pl.BlockSpec((None, tk, m), lambda s, k: (s, k, 0)) if a_sharded
              else pl.BlockSpec((tk, m), lambda s, k: (k, 0)))
    b_specs, o_specs, o_shapes, scratch = [], [], [], []
    for b in bs:
        n = b.shape[-1]
        b_specs.append(pl.BlockSpec((None, tk, n), lambda s, k: (s, k, 0)) if b_sharded
                       else pl.BlockSpec((tk, n), lambda s, k: (k, 0)))
        scratch.append(pltpu.VMEM((m, n), f32))
        if sh > 1:
            o_specs.append(pl.BlockSpec((None, m, n), lambda s, k: (s, 0, 0)))
            o_shapes.append(jax.ShapeDtypeStruct((sh, m, n), out_dtype))
        else:
            o_specs.append(pl.BlockSpec((m, n), lambda s, k: (0, 0)))
            o_shapes.append(jax.ShapeDtypeStruct((m, n), out_dtype))
    return pl.pallas_call(
        body, name=name, grid=(sh, nk), in_specs=[a_spec] + b_specs, out_specs=o_specs, out_shape=o_shapes,
        scratch_shapes=scratch, compiler_params=_cp("arbitrary", "arbitrary"),
    )(a, *bs)


def _mm_tn_rows(a_list, b, name, tk=1024, out_dtype=bf16):
    na = len(a_list)
    n = b.shape[-1]
    nk = T // tk

    def body(*refs):
        a_refs, b_ref = refs[:na], refs[na]
        o_refs, accs = refs[na + 1:2 * na + 1], refs[2 * na + 1:]
        kk = pl.program_id(0)
        bv = b_ref[...]
        for a_ref, o_ref, acc in zip(a_refs, o_refs, accs):
            upd = lax.dot_general(a_ref[...], bv, TN, preferred_element_type=f32)

            @pl.when(kk == 0)
            def _():
                acc[...] = upd

            @pl.when(kk > 0)
            def _():
                acc[...] += upd

            @pl.when(kk == nk - 1)
            def _():
                o_ref[...] = acc[...].astype(out_dtype)

    return pl.pallas_call(
        body, name=name, grid=(nk,),
        in_specs=[pl.BlockSpec((tk, a.shape[-1]), lambda k: (k, 0)) for a in a_list] + [pl.BlockSpec((tk, n), lambda k: (k, 0))],
        out_specs=[pl.BlockSpec((a.shape[-1], n), lambda k: (0, 0)) for a in a_list],
        out_shape=[jax.ShapeDtypeStruct((a.shape[-1], n), out_dtype) for a in a_list],
        scratch_shapes=[pltpu.VMEM((a.shape[-1], n), f32) for a in a_list],
        compiler_params=_cp("arbitrary"),
    )(*a_list, b)


def _outproj_bwd(dx1b, wo):
    tm = 1024

    def body(dx_ref, wo_ref, da_ref, dp_ref):
        dx = dx_ref[...]
        da_ref[...] = lax.dot_general(dx, wo_ref[0:AW, :], NT, preferred_element_type=f32).astype(bf16)
        dp_ref[...] = lax.dot_general(dx, wo_ref[AW:2 * AW, :], NT, preferred_element_type=f32)

    return pl.pallas_call(
        body, name="outproj_bwd", grid=(T // tm,),
        in_specs=[pl.BlockSpec((tm, D), lambda i: (i, 0)), _full((D, D))],
        out_specs=[pl.BlockSpec((tm, AW), lambda i: (i, 0)), pl.BlockSpec((tm, AW), lambda i: (i, 0))],
        out_shape=[jax.ShapeDtypeStruct((T, AW), bf16), jax.ShapeDtypeStruct((T, AW), f32)],
        compiler_params=_cp("parallel"),
    )(dx1b, wo)


def _pool_bwd(dpool, pooled, wp, scale):
    tm = 512
    n = T // tm

    def body(dp_ref, pb_ref, wp_ref, sc_ref, du_ref, dsc_ref, dwp_ref, ext):
        i = pl.program_id(0)

        @pl.when(i == 0)
        def _():
            ext[tm:tm + HALO, :] = jnp.zeros((HALO, AW), f32)
            dsc_ref[...] = jnp.zeros_like(dsc_ref)
            dwp_ref[...] = jnp.zeros_like(dwp_ref)

        t_idx = (n - 1 - i) * tm + lax.broadcasted_iota(jnp.int32, (tm, 1), 0)
        for g, w in enumerate(WINDOWS):
            lo, hi = 128 * g, 128 * (g + 1)
            pb = pb_ref[:, lo:hi]
            mixed = jnp.dot(pb, wp_ref[g], preferred_element_type=f32)
            dpo = dp_ref[:, lo:hi]
            dsc_ref[:, lo:hi] += jnp.sum(dpo * mixed, axis=0, keepdims=True)
            dmr = (dpo * sc_ref[:, lo:hi]).astype(bf16)
            dwp_ref[g] += lax.dot_general(pb, dmr, TN, preferred_element_type=f32)
            dpl = lax.dot_general(dmr, wp_ref[g], NT, preferred_element_type=f32)
            cnt = jnp.minimum(t_idx + 1, w).astype(f32)
            dpn = dpl / cnt
            ext[0:tm, lo:hi] = dpn
            acc = dpn
            for d in range(1, w):
                acc = acc + ext[d:d + tm, lo:hi]
            du_ref[:, lo:hi] = (acc - dpl).astype(bf16)
        ext[tm:tm + HALO, :] = ext[0:HALO, :]

    rev = lambda i: (n - 1 - i, 0)
    return pl.pallas_call(
        body, name="pool_bwd", grid=(n,),
        in_specs=[pl.BlockSpec((tm, AW), rev), pl.BlockSpec((tm, AW), rev), _full((4, 128, 128)), _full((1, AW))],
        out_specs=[pl.BlockSpec((tm, AW), rev), _full((1, AW)), _full((4, 128, 128))],
        out_shape=[jax.ShapeDtypeStruct((T, AW), bf16), jax.ShapeDtypeStruct((1, AW), f32),
                   jax.ShapeDtypeStruct((4, 128, 128), f32)],
        scratch_shapes=[pltpu.VMEM((tm + HALO, AW), f32)],
        compiler_params=_cp("arbitrary"),
    )(dpool, pooled, wp, scale)


def _attn_bwd(qkv, qaug, kaug, attn, dattn, lse, dep):
    tq = tk = ATT_T
    n = T // tq
    it, jt = _causal_steps(True)
    nsteps = it.shape[0]

    rs = 64

    def body(it_ref, jt_ref, q_ref, k_ref, v_ref, qa_ref, ka_ref, o_ref, do_ref, lse_ref, dep_ref,
             dq_ref, dqs_ref, dk_ref, dks_ref, dv_ref, dq_acc, dk_acc, dv_acc, s_sc, dp_sc, p_sc, ds_sc):
        t = pl.program_id(1)
        i = it_ref[t]
        j = jt_ref[t]

        @pl.when(t == 0)
        def _():
            dq_acc[...] = jnp.zeros_like(dq_acc)

        @pl.when(i == j)
        def _():
            dk_acc[...] = jnp.zeros_like(dk_acc)
            dv_acc[...] = jnp.zeros_like(dv_acc)

        lane = lax.broadcasted_iota(jnp.int32, (tq, 128), 1)

        def step(on_diagonal):
            q = q_ref[...] * 0.125
            k = k_ref[...]
            v = v_ref[...]
            qa = qa_ref[...]
            ka = ka_ref[...]
            do = do_ref[...]
            dd = do.astype(f32) * o_ref[...].astype(f32)
            r0 = pl.multiple_of(i * tq, tq)
            qes, kes, does, deltas = [], [], [], []
            for e in range(2):
                hm = (lane >= 64) if e else (lane < 64)
                qes.append(jnp.where(hm, q, qa))
                kes.append(jnp.where(hm, k, ka))
                does.append(jnp.where(hm, do, jnp.zeros_like(do)))
                deltas.append(jnp.sum(jnp.where(hm, dd, 0.0), axis=1, keepdims=True))
                s_sc[e] = lax.dot_general(qes[e], kes[e], NT, preferred_element_type=f32)
                dp_sc[e] = lax.dot_general(does[e], v, NT, preferred_element_type=f32)
            for e in range(2):
                for r in range(0, tq, rs):
                    s = s_sc[e, r:r + rs, :] - lse_ref[r:r + rs, 64 * e:64 * e + 1]
                    if on_diagonal:
                        row = lax.broadcasted_iota(jnp.int32, (rs, tk), 0) + r
                        col = lax.broadcasted_iota(jnp.int32, (rs, tk), 1)
                        s = jnp.where(col <= row, s, NEG)
                    p = jnp.exp(s)
                    p_sc[e, r:r + rs, :] = p.astype(bf16)
                    ds_sc[e, r:r + rs, :] = (p * (dp_sc[e, r:r + rs, :] - deltas[e][r:r + rs, :])).astype(bf16)
                dv_acc[...] += lax.dot_general(does[e], p_sc[e], TN, preferred_element_type=f32)
                dsb = ds_sc[e]
                dk_acc[e] += lax.dot_general(qes[e], dsb, TN, preferred_element_type=f32)
                dq_acc[e, pl.ds(r0, tq), :] += jnp.dot(dsb, kes[e], preferred_element_type=f32)

        @pl.when(i > j)
        def _():
            step(False)

        @pl.when(i == j)
        def _():
            step(True)

        @pl.when(i == n - 1)
        def _():
            dk0 = dk_acc[0].T
            dk1 = dk_acc[1].T
            dk_ref[...] = jnp.where(lane < 64, dk0, dk1).astype(bf16)
            dks_ref[...] = jnp.where(lane < 64, dk1, dk0)
            dv_ref[...] = dv_acc[...].T.astype(bf16)

        @pl.when(t == nsteps - 1)
        def _():
            lane_t = lax.broadcasted_iota(jnp.int32, (T, 128), 1)
            dq_ref[...] = (jnp.where(lane_t < 64, dq_acc[0], dq_acc[1]) * 0.125).astype(bf16)
            dqs_ref[...] = jnp.where(lane_t < 64, dq_acc[1], dq_acc[0])

    qmap = lambda p, t, it, jt: (it[t], p)
    grid_spec = pltpu.PrefetchScalarGridSpec(
        num_scalar_prefetch=2, grid=(PAIRS, nsteps),
        in_specs=[pl.BlockSpec((tq, 128), qmap),
                  pl.BlockSpec((tk, 128), lambda p, t, it, jt: (jt[t], PAIRS + p)),
                  pl.BlockSpec((tk, 128), lambda p, t, it, jt: (jt[t], 2 * PAIRS + p)),
                  pl.BlockSpec((tq, 128), qmap), pl.BlockSpec((tk, 128), lambda p, t, it, jt: (jt[t], p)),
                  pl.BlockSpec((tq, 128), qmap), pl.BlockSpec((tq, 128), qmap),
                  pl.BlockSpec((None, tq, 128), lambda p, t, it, jt: (p, it[t], 0)),
                  pl.BlockSpec((8, 128), lambda p, t, it, jt: (0, 0))],
        out_specs=[pl.BlockSpec((T, 128), lambda p, t, it, jt: (0, p)),
                   pl.BlockSpec((None, T, 128), lambda p, t, it, jt: (p, 0, 0)),
                   pl.BlockSpec((tk, 128), lambda p, t, it, jt: (jt[t], p)),
                   pl.BlockSpec((None, tk, 128), lambda p, t, it, jt: (p, jt[t], 0)),
                   pl.BlockSpec((tk, 128), lambda p, t, it, jt: (jt[t], p))],
        scratch_shapes=[pltpu.VMEM((2, T, 128), f32), pltpu.VMEM((2, 128, tk), f32), pltpu.VMEM((128, tk), f32),
                        pltpu.VMEM((2, tq, tk), f32), pltpu.VMEM((2, tq, tk), f32), pltpu.VMEM((2, tq, tk), bf16),
                        pltpu.VMEM((2, tq, tk), bf16)],
    )
    return pl.pallas_call(
        body, name="fox_attn_bwd", grid_spec=grid_spec,
        out_shape=[jax.ShapeDtypeStruct((T, AW), bf16), jax.ShapeDtypeStruct((PAIRS, T, 128), f32),
                   jax.ShapeDtypeStruct((T, AW), bf16), jax.ShapeDtypeStruct((PAIRS, T, 128), f32),
                   jax.ShapeDtypeStruct((T, AW), bf16)],
        compiler_params=_cp("parallel", "arbitrary"),
    )(it, jt, qkv, qkv, qkv, qaug, kaug, attn, dattn, lse, dep)


def _fox_cumsum_bwd(dqs, dks, fl, bfp):
    tb = CUMSUM_ROWS
    nb = T // tb
    hp = lax.Precision.HIGHEST

    def body(dqs_ref, dks_ref, fl_ref, b_ref, df_ref, db_ref, carry):
        i = pl.program_id(0)

        @pl.when(i == 0)
        def _():
            carry[...] = jnp.zeros_like(carry)
            db_ref[...] = jnp.zeros_like(db_ref)

        r = lax.broadcasted_iota(jnp.int32, (128, 128), 0)
        cc = lax.broadcasted_iota(jnp.int32, (128, 128), 1)
        pick = lambda even_lane, odd_lane, p: jnp.logical_or(
            jnp.logical_and(r == even_lane, cc == 2 * p), jnp.logical_and(r == odd_lane, cc == 2 * p + 1)).astype(f32)
        dc = jnp.zeros((tb, 128), f32)
        for p in range(PAIRS):
            dc = dc + jnp.dot(dqs_ref[p], pick(64, 0, p), precision=hp, preferred_element_type=f32)
            dc = dc - jnp.dot(dks_ref[p], pick(67, 3, p), precision=hp, preferred_element_type=f32)
        rt = lax.broadcasted_iota(jnp.int32, (tb, tb), 0)
        ct = lax.broadcasted_iota(jnp.int32, (tb, tb), 1)
        utri = (ct >= rt).astype(f32)
        dl = jnp.dot(utri, dc, precision=hp, preferred_element_type=f32) + carry[0:1, :]
        carry[...] = jnp.broadcast_to(dl[0:1, :], (8, 128))
        z = fl_ref[...] + b_ref[...]
        df = dl * jax.nn.sigmoid(-z)
        df_ref[...] = df.astype(bf16)
        db_ref[...] += jnp.sum(df, axis=0, keepdims=True)

    rev = lambda i: (nb - 1 - i, 0)
    return pl.pallas_call(
        body, name="fox_cumsum_bwd", grid=(nb,),
        in_specs=[pl.BlockSpec((PAIRS, tb, 128), lambda i: (0, nb - 1 - i, 0)),
                  pl.BlockSpec((PAIRS, tb, 128), lambda i: (0, nb - 1 - i, 0)),
                  pl.BlockSpec((tb, 128), rev), _full((1, 128))],
        out_specs=[pl.BlockSpec((tb, 128), rev), _full((1, 128))],
        out_shape=[jax.ShapeDtypeStruct((T, 128), bf16), jax.ShapeDtypeStruct((1, 128), f32)],
        scratch_shapes=[pltpu.VMEM((8, 128), f32)],
        compiler_params=_cp("arbitrary"),
    )(dqs, dks, fl, bfp)


def _inproj_bwd(dq, dk, dv, du, df, wm, wf, x, dx1, g1):
    tm = 1024

    def body(dq_ref, dk_ref, dv_ref, du_ref, df_ref, wm_ref, wf_ref, x_ref, dx1_ref, g_ref, dx_ref, dn_ref):
        i = pl.program_id(0)

        @pl.when(i == 0)
        def _():
            dn_ref[...] = jnp.zeros_like(dn_ref)

        dh = jnp.dot(dq_ref[...], wm_ref[0:AW, :], preferred_element_type=f32)
        dh = dh + jnp.dot(dk_ref[...], wm_ref[AW:2 * AW, :], preferred_element_type=f32)
        dh = dh + jnp.dot(dv_ref[...], wm_ref[2 * AW:3 * AW, :], preferred_element_type=f32)
        dh = dh + jnp.dot(du_ref[...], wm_ref[3 * AW:4 * AW, :], preferred_element_type=f32)
        dh = dh + jnp.dot(df_ref[...], wf_ref[...], preferred_element_type=f32)
        xv = x_ref[...]
        r = lax.rsqrt(jnp.mean(xv * xv, axis=-1, keepdims=True) + EPS)
        xhat = xv * r
        dn_ref[...] += jnp.sum(dh * xhat, axis=0, keepdims=True)
        z = dh * g_ref[...]
        dx_ref[...] = dx1_ref[...] + r * (z - xhat * jnp.mean(z * xhat, axis=-1, keepdims=True))

    row = lambda i: (i, 0)
    return pl.pallas_call(
        body, name="inproj_bwd", grid=(T // tm,),
        in_specs=[pl.BlockSpec((tm, AW), row)] * 4 + [pl.BlockSpec((tm, 128), row), _full((4 * AW, D)), _full((128, D)),
                                                       pl.BlockSpec((tm, D), row), pl.BlockSpec((tm, D), row), _full((1, D))],
        out_specs=[pl.BlockSpec((tm, D), row), _full((1, D))],
        out_shape=[jax.ShapeDtypeStruct((T, D), f32), jax.ShapeDtypeStruct((1, D), f32)],
        compiler_params=_cp("arbitrary"),
    )(dq, dk, dv, du, df, wm, wf, x, dx1, g1)


def _adamw_math(w, g, m, v):
    m = B1 * m + (1.0 - B1) * g
    v = B2 * v + (1.0 - B2) * (g * g)
    m_hat = m / (1.0 - B1 ** STEP)
    v_hat = v / (1.0 - B2 ** STEP)
    delta = -LR * (m_hat / (jnp.sqrt(v_hat) + AEPS) + WD * w)
    return delta, m, v


def _adamw_shard(w, m, v, p_mine, p_other, name):
    rows, cols = w.shape
    tr = rows if rows <= IN_S else rows // 2

    def body(w_ref, m_ref, v_ref, a_ref, b_ref, g_ref, d_ref, nm_ref, nv_ref):
        g = a_ref[...].astype(f32) + b_ref[...].astype(f32)
        g_ref[...] = g
        d_ref[...], nm_ref[...], nv_ref[...] = _adamw_math(w_ref[...], g, m_ref[...], v_ref[...])

    spec = pl.BlockSpec((tr, cols), lambda i: (i, 0))
    return pl.pallas_call(
        body, name=name, grid=(rows // tr,), in_specs=[spec] * 5, out_specs=[spec] * 4,
        out_shape=[jax.ShapeDtypeStruct((rows, cols), f32)] * 4, compiler_params=_cp("parallel"),
    )(w, m, v, p_mine, p_other)


SMALL_SLOTS = ((0, 8, 128), (8, 16, 128), (16, 24, 128), (24, 28, 128), (32, 33, 8))
LOSS_ROW = 39


def _adamw_small(ws, ms, vs, parts, parts_wp):
    n = len(ws)

    def body(*refs):
        w_refs, m_refs, v_refs = refs[0:n], refs[n:2 * n], refs[2 * n:3 * n]
        p_ref, pw_ref = refs[3 * n], refs[3 * n + 1]
        outs = refs[3 * n + 2:]
        g_all = p_ref[0]
        g_wp = pw_ref[0]
        for k in range(1, 8):
            g_all = g_all + p_ref[k]
            g_wp = g_wp + pw_ref[k]
        grads = [g_all[r0:r1, 0:lanes] for r0, r1, lanes in SMALL_SLOTS] + [g_wp]
        for idx, g in enumerate(grads):
            d, nm, nv = _adamw_math(w_refs[idx][...], g, m_refs[idx][...], v_refs[idx][...])
            outs[idx][...] = g
            outs[n + idx][...] = d
            outs[2 * n + idx][...] = nm
            outs[3 * n + idx][...] = nv
        outs[4 * n][...] = g_all[LOSS_ROW:LOSS_ROW + 1, :]

    shapes = [jax.ShapeDtypeStruct(w.shape, f32) for w in ws]
    res = pl.pallas_call(
        body, name="adamw_small", out_shape=shapes * 4 + [jax.ShapeDtypeStruct((1, 128), f32)],
    )(*ws, *ms, *vs, parts, parts_wp)
    return res[:4 * n], res[4 * n]


def _sum4(recv, g, mine, name):
    _, rows, cols = recv.shape
    tr = rows if rows <= IN_S else rows // 2

    def body(mine_ref, r_ref, g_ref, o_ref):
        o_ref[...] = ((g_ref[...].astype(f32) + r_ref[0].astype(f32))
                      + (r_ref[1].astype(f32) + r_ref[2].astype(f32))).astype(bf16)

    grid_spec = pltpu.PrefetchScalarGridSpec(
        num_scalar_prefetch=1, grid=(rows // tr,),
        in_specs=[pl.BlockSpec((3, tr, cols), lambda i, m: (0, i, 0)),
                  pl.BlockSpec((None, tr, cols), lambda i, m: (m[0], i, 0))],
        out_specs=pl.BlockSpec((tr, cols), lambda i, m: (i, 0)))
    return pl.pallas_call(
        body, name=name, grid_spec=grid_spec, out_shape=jax.ShapeDtypeStruct((rows, cols), bf16),
        compiler_params=_cp("arbitrary"),
    )(mine, recv, g)


_HBM = pl.BlockSpec(memory_space=pltpu.HBM)
_SEM = pl.BlockSpec(memory_space=pltpu.SEMAPHORE)
_EFFECT = pltpu.SideEffectType.DATAFLOW_SIDE_EFFECTING


def _in_hbm(a):
    return pltpu.with_memory_space_constraint(a, pltpu.HBM)


def _mesh_pos():
    return lax.axis_index("x"), lax.axis_index("y"), lax.axis_index("c")


def _other_chips(x, y):
    return [(1 - x, y), (x, 1 - y), (1 - x, 1 - y)]


def _gather_copy(srcs, lands, send_sems, recv_sems, a, k, slot):
    x, y, c = _mesh_pos()
    cx, cy = _other_chips(x, y)[k]
    return pltpu.make_async_remote_copy(
        src_ref=srcs[a], dst_ref=lands[a].at[slot], send_sem=send_sems.at[3 * a + k], recv_sem=recv_sems.at[3 * a + k],
        device_id=(cx, cy, c), device_id_type=MESH)


def _scatter_copy(srcs, lands, send_sems, recv_sems, a, k):
    x, y, c = _mesh_pos()
    cx, cy = _other_chips(x, y)[k]
    return pltpu.make_async_remote_copy(
        src_ref=srcs[a].at[2 * cx + cy], dst_ref=lands[a].at[k], send_sem=send_sems.at[3 * a + k],
        recv_sem=recv_sems.at[3 * a + k], device_id=(cx, cy, c), device_id_type=MESH)


def _all_gather_w_in(part):
    cols = part.shape[1] // 2

    def body(src, dst, send_sems, recv_sems, loc_sem):
        x, y, c = _mesh_pos()
        mine = 2 * x + y
        chips = _other_chips(x, y)
        half = lambda ref, cc: ref.at[:, pl.ds(pl.multiple_of(cc * cols, cols), cols)]

        def over_ici(k, slot):
            cx, cy = chips[k]
            return pltpu.make_async_remote_copy(
                src_ref=half(src, c), dst_ref=half(dst.at[slot], c), send_sem=send_sems.at[k], recv_sem=recv_sems.at[k],
                device_id=(cx, cy, c), device_id_type=MESH)

        def to_sibling(k, cc):
            slot = 2 * chips[k][0] + chips[k][1]
            return pltpu.make_async_remote_copy(
                src_ref=half(dst.at[slot], cc), dst_ref=half(dst.at[slot], cc), send_sem=send_sems.at[3 + k],
                recv_sem=recv_sems.at[3 + k], device_id=(x, y, 1 - c), device_id_type=MESH)

        local = pltpu.make_async_copy(src, dst.at[mine], loc_sem.at[0])
        local.start()
        first = [over_ici(k, mine) for k in range(3)]
        for cp in first:
            cp.start()
        passed = [to_sibling(k, c) for k in range(3)]
        for k in range(3):
            over_ici(k, 2 * chips[k][0] + chips[k][1]).wait_recv()
            passed[k].start()
        for k in range(3):
            to_sibling(k, 1 - c).wait_recv()
        for cp in first + passed:
            cp.wait_send()
        local.wait()

    return pl.pallas_call(
        body, name="all_gather_w_in", in_specs=[_HBM], out_specs=_HBM,
        out_shape=jax.ShapeDtypeStruct((NSH,) + part.shape, part.dtype),
        scratch_shapes=[pltpu.SemaphoreType.DMA((6,)), pltpu.SemaphoreType.DMA((6,)), pltpu.SemaphoreType.DMA((1,))],
    )(part)


def _split_start(name, srcs, lands, n_sems, plan, dep):
    n, nl = len(srcs), len(lands)

    def body(*refs):
        src_refs, land_refs = refs[:n], refs[n:n + nl]
        send_sems, recv_sems = refs[n + nl + 1], refs[n + nl + 2]
        token = refs[-1]
        sends, _ = plan(src_refs, land_refs, send_sems, recv_sems)
        for cp in sends:
            cp.start()
        token[...] = jnp.zeros_like(token)

    outs = pl.pallas_call(
        body, name=name,
        in_specs=[_HBM] * (n + nl) + [pl.BlockSpec(memory_space=pl.ANY)],
        out_specs=[_SEM, _SEM] + [_HBM] * (n + nl) + [pl.BlockSpec(memory_space=pltpu.VMEM)],
        out_shape=[pltpu.SemaphoreType.DMA((n_sems,)), pltpu.SemaphoreType.DMA((n_sems,))]
        + [pltpu.HBM(a.shape, a.dtype) for a in list(srcs) + list(lands)] + [jax.ShapeDtypeStruct((8, 128), f32)],
        input_output_aliases={i: 2 + i for i in range(n + nl)},
        compiler_params=pltpu.CompilerParams(has_side_effects=_EFFECT),
    )(*[_in_hbm(a) for a in list(srcs) + list(lands)], dep)
    return outs[0], outs[1], list(outs[2:2 + n]), list(outs[2 + n:2 + n + nl]), outs[-1]


def _split_wait(name, send_sems, recv_sems, srcs, lands, after, plan):
    n, nl = len(srcs), len(lands)

    def body(*refs):
        src_refs, land_refs = refs[:n], refs[n:n + nl]
        s_sems, r_sems = refs[n + nl], refs[n + nl + 1]
        sends, recvs = plan(src_refs, land_refs, s_sems, r_sems)
        for cp in recvs:
            cp.wait_recv()
        for cp in sends:
            cp.wait_send()

    outs = pl.pallas_call(
        body, name=name,
        in_specs=[_HBM] * (n + nl) + [_SEM, _SEM, pl.BlockSpec(memory_space=pl.ANY)],
        out_specs=[_HBM] * (n + nl),
        out_shape=[pltpu.HBM(a.shape, a.dtype) for a in list(srcs) + list(lands)],
        input_output_aliases={i: i for i in range(n + nl)},
        compiler_params=pltpu.CompilerParams(has_side_effects=_EFFECT),
    )(*srcs, *lands, send_sems, recv_sems, after)
    return list(outs[:n]), list(outs[n:])


def _gather_plan(srcs, lands, ss, rs):
    x, y, _ = _mesh_pos()
    chips = _other_chips(x, y)
    sends = [_gather_copy(srcs, lands, ss, rs, a, k, 2 * x + y) for a in range(len(srcs)) for k in range(3)]
    recvs = [_gather_copy(srcs, lands, ss, rs, a, k, 2 * chips[k][0] + chips[k][1])
             for a in range(len(srcs)) for k in range(3)]
    return sends, recvs


def _scatter_plan(srcs, lands, ss, rs):
    cps = [_scatter_copy(srcs, lands, ss, rs, a, k) for a in range(len(srcs)) for k in range(3)]
    return cps, cps


def _scatter_and_spread_plan(srcs, lands, ss, rs):
    x, y, c = _mesh_pos()
    me = 4 * x + 2 * y + c
    n = len(srcs) - 1
    cps = [_scatter_copy(srcs[:n], lands[:n], ss, rs, a, k) for a in range(n) for k in range(3)]
    for f in range(1, 8):
        peer = ((x + (f >> 2)) % 2, (y + ((f >> 1) & 1)) % 2, (c + (f & 1)) % 2)
        cps.append(pltpu.make_async_remote_copy(
            src_ref=srcs[n], dst_ref=lands[n].at[me], send_sem=ss.at[3 * n - 1 + f], recv_sem=rs.at[3 * n - 1 + f],
            device_id=peer, device_id_type=MESH))
    return cps, cps


def _swap_with_sibling(parts, name):
    n = len(parts)

    def body(*refs):
        srcs, dsts = refs[:n], refs[n:2 * n]
        send_sems, recv_sems = refs[2 * n:]
        x, y, c = _mesh_pos()
        cps = [pltpu.make_async_remote_copy(src_ref=srcs[a], dst_ref=dsts[a], send_sem=send_sems.at[a],
                                            recv_sem=recv_sems.at[a], device_id=(x, y, 1 - c), device_id_type=MESH)
               for a in range(n)]
        for cp in cps:
            cp.start()
        for cp in cps:
            cp.wait_recv()
        for cp in cps:
            cp.wait_send()

    return pl.pallas_call(
        body, name=name, in_specs=[_HBM] * n, out_specs=[_HBM] * n,
        out_shape=[jax.ShapeDtypeStruct(p.shape, p.dtype) for p in parts],
        scratch_shapes=[pltpu.SemaphoreType.DMA((n,)), pltpu.SemaphoreType.DMA((n,))],
    )(*parts)


def _forward(x, tgt, wm, wf, mlp_w_fn, g1, bfp, wp, scale, g2, gf, dep):
    h, qkv, u, fl = _rms_inproj(x, g1, wm, wf, dep)
    qaug, kaug = _fox_cumsum(fl, bfp)
    attn, lse = _attn_fwd(qkv, qaug, kaug)
    pooled, pool = _pool_fwd(u, wp, scale)
    wo, wgt, wut, wd = mlp_w_fn(attn)
    x1, h2 = _outproj(x, attn, pool, wo, g2)
    loss, dgf, dx2, dx2b, ud, silu, a_b = _mlp_fwd_loss(h2, x1, wgt, wut, wd, tgt, gf)
    saved = dict(h=h, qkv=qkv, fl=fl, qaug=qaug, kaug=kaug, attn=attn, lse=lse, pooled=pooled, pool=pool, x1=x1, h2=h2,
                 ud=ud, silu=silu, a_b=a_b, wo=wo, wgt=wgt, wut=wut, wd=wd)
    return loss, dgf, dx2, dx2b, saved


def _backward_mlp(sv, dx2, dx2b, g2):
    dgate, dup, dx1, dx1b, dg2 = _mlp_bwd(dx2b, dx2, sv["ud"], sv["silu"], sv["wgt"], sv["wut"], sv["wd"], sv["x1"], g2)
    (dwd,) = _mm_tn(sv["a_b"], [dx2b], "dw_down", a_sharded=True, tk=T)
    (dwgt,) = _mm_tn(dgate, [sv["h2"]], "dw_gate", a_sharded=True, tk=T)
    (dwut,) = _mm_tn(dup, [sv["h2"]], "dw_up", a_sharded=True, tk=T)
    return dx1, dx1b, dg2, (dwgt, dwut, dwd)


def _backward_outproj_pool(sv, dx1b, wp, scale):
    dattn, dpool = _outproj_bwd(dx1b, sv["wo"])
    dwo_a, = _mm_tn(sv["attn"], [dx1b], "dw_out_attn", tk=2048)
    dwo_p, = _mm_tn(sv["pool"], [dx1b], "dw_out_pool", tk=2048)
    dwo = jnp.concatenate([dwo_a, dwo_p], axis=0).reshape(NSH, D // NSH, D)
    du, dscale, dwp = _pool_bwd(dpool, sv["pooled"], wp, scale)
    return dattn, dwo, du, dscale, dwp


def _backward_attn_inproj(sv, x, dx1, dattn, du, wm, wf, g1, bfp, dep):
    dq, dqs, dk, dks, dv = _attn_bwd(sv["qkv"], sv["qaug"], sv["kaug"], sv["attn"], dattn, sv["lse"], dep)
    df, dbf = _fox_cumsum_bwd(dqs, dks, sv["fl"], bfp)
    dx, dg1 = _inproj_bwd(dq, dk, dv, du, df, wm, wf, x, dx1, g1)
    dwq, dwk, dwv, dwu_in, dwf = _mm_tn_rows([dq, dk, dv, du, df], sv["h"], "dw_in")
    dwin = jnp.concatenate([dwq, dwk, dwv, dwf[0:8], dwu_in], axis=0)
    return dx, dg1, dbf, dwin.reshape(NSH, IN_S, D)


def kernel(x, norm1_g, w_in, b_forget, w_pool, pool_scale, w_out, norm2_g, w_gate, w_up, w_down, final_g, loss_target, m_norm1_g, m_w_in, m_b_forget, m_w_pool, m_pool_scale, m_w_out, m_norm2_g, m_w_gate, m_w_up, m_w_down, m_final_g, v_norm1_g, v_w_in, v_b_forget, v_w_pool, v_pool_scale, v_w_out, v_norm2_g, v_w_gate, v_w_up, v_w_down, v_final_g):
    mine = (2 * lax.axis_index("x") + lax.axis_index("y")).astype(jnp.int32)
    mine1 = mine.reshape(1)
    tr = lambda a: jnp.transpose(a[0])

    win4 = _all_gather_w_in(tr(w_in).astype(bf16))
    later = [w_out[0].astype(bf16), tr(w_gate).astype(bf16), tr(w_up).astype(bf16), w_down[0].astype(bf16)]
    lands = [lax.dynamic_update_slice(lax.empty((NSH,) + p.shape, bf16), p[None], (mine, 0, 0)) for p in later]
    ag_send, ag_recv, later_thru, lands_thru, ag_token = _split_start("all_gather_start", later, lands, 12, _gather_plan,
                                                                      win4)
    win = win4.reshape(IN_W, D)
    wm = jnp.concatenate([win[0:3 * AW], win[3 * AW + 8:]], axis=0)
    wf = jnp.pad(win[3 * AW:3 * AW + 8], ((0, 120), (0, 0)))
    bfp = jnp.pad(b_forget, ((0, 0), (0, 120)))
    wp = w_pool[0].astype(bf16)
    gf = final_g.reshape(1, D)

    def later_weights(after):
        _, (wo4, wgt, wut, wd) = _split_wait("all_gather_wait", ag_send, ag_recv, later_thru, lands_thru, after, _gather_plan)
        return wo4.reshape(D, D), wgt, wut, wd

    xe, tgt = x[0], loss_target[0]
    loss_v, dgf, dx2, dx2b, sv = _forward(xe, tgt, wm, wf, later_weights, norm1_g, bfp, wp, pool_scale, norm2_g, gf, ag_token)
    dx1, dx1b, dg2, mlp_grads = _backward_mlp(sv, dx2, dx2b, norm2_g)
    dattn, dwo, du, dscale, dwp = _backward_outproj_pool(sv, dx1b, wp, pool_scale)
    me = (4 * lax.axis_index("x") + 2 * lax.axis_index("y") + lax.axis_index("c")).astype(jnp.int32)
    dwp = dwp.reshape(512, 128)
    first = [dwo] + list(mlp_grads) + [dwp]
    first_lands = [lax.empty((3,) + g.shape[1:], bf16) for g in first[:4]]
    first_lands.append(lax.dynamic_update_slice(lax.empty((8, 512, 128), f32), dwp[None], (me, 0, 0)))
    rs_send, rs_recv, first_thru, first_lands_thru, rs_token = _split_start(
        "reduce_scatter_start", first, first_lands, 19, _scatter_and_spread_plan, du)
    dx, dg1, dbf, dwin = _backward_attn_inproj(sv, xe, dx1, dattn, du, wm, wf, norm1_g, bfp, rs_token)

    pad8 = lambda r: jnp.pad(r, ((0, 8 - r.shape[0]), (0, 0)))
    loss_rows = jnp.concatenate([dbf, jnp.zeros((6, 128), f32), loss_v[0:1, :]], axis=0)
    small = jnp.concatenate([dg1.reshape(8, 128), dg2.reshape(8, 128), dgf.reshape(8, 128), pad8(dscale.reshape(4, 128)),
                             loss_rows], axis=0)
    small_land = lax.dynamic_update_slice(lax.empty((8, SMALL_ROWS, 128), f32), small[None], (me, 0, 0))
    tail_send, tail_recv, tail_thru, tail_lands_thru, tail_token = _split_start(
        "tail_start", [dwin, small], [lax.empty((3,) + dwin.shape[1:], bf16), small_land], 10, _scatter_and_spread_plan,
        dx)
    first_thru, first_recv = _split_wait("reduce_scatter_wait", rs_send, rs_recv, first_thru, first_lands_thru, tail_token,
                                         _scatter_and_spread_plan)
    wp_all = first_recv[4]
    ws = [tr(w_in), w_out[0], tr(w_gate), tr(w_up), w_down[0]]
    ms = [tr(m_w_in), m_w_out[0], tr(m_w_gate), tr(m_w_up), m_w_down[0]]
    vs = [tr(v_w_in), v_w_out[0], tr(v_w_gate), tr(v_w_up), v_w_down[0]]
    partial = [_sum4(r, g, mine1, f"sum4_{i + 1}") for i, (r, g) in enumerate(zip(first_recv[:4], first_thru[:4]))]
    other = _swap_with_sibling(partial, "swap_first")
    big = [_adamw_shard(ws[i + 1], ms[i + 1], vs[i + 1], partial[i], other[i], f"adamw_{i + 1}") for i in range(4)]
    (dwin_thru, _), (in_recv_land, small_all) = _split_wait("tail_wait", tail_send, tail_recv, tail_thru, tail_lands_thru,
                                                            big[3][0], _scatter_and_spread_plan)
    partial_in = _sum4(in_recv_land, dwin_thru, mine1, "sum4_0")
    (other_in,) = _swap_with_sibling([partial_in], "swap_in")
    big = [_adamw_shard(ws[0], ms[0], vs[0], partial_in, other_in, "adamw_0")] + big

    small_names = ["norm1_g", "norm2_g", "final_g", "pool_scale", "b_forget", "w_pool"]
    rows = lambda a, b, c, d, e, f: [a.reshape(8, 128), b.reshape(8, 128), c.reshape(8, 128), d.reshape(4, 128),
                                     e.reshape(1, 8), f.reshape(512, 128)]
    sm, loss_row = _adamw_small(rows(norm1_g, norm2_g, final_g, pool_scale, b_forget, w_pool),
                                rows(m_norm1_g, m_norm2_g, m_final_g, m_pool_scale, m_b_forget, m_w_pool),
                                rows(v_norm1_g, v_norm2_g, v_final_g, v_pool_scale, v_b_forget, v_w_pool), small_all, wp_all)
    small_shape = dict(norm1_g=(1, D), norm2_g=(1, D), final_g=(D,), pool_scale=(1, AW), b_forget=(1, 8),
                       w_pool=(1, 4, 128, 128))

    order = ["norm1_g", "w_in", "b_forget", "w_pool", "pool_scale", "w_out", "norm2_g", "w_gate", "w_up", "w_down", "final_g"]
    big_idx = {"w_in": 0, "w_out": 1, "w_gate": 2, "w_up": 3, "w_down": 4}
    outs = [loss_row[0, 0], dx[None]]
    for kind in range(4):
        for name in order:
            if name in ("w_in", "w_gate", "w_up"):
                outs.append(jnp.transpose(big[big_idx[name]][kind])[None])
            elif name in big_idx:
                outs.append(big[big_idx[name]][kind][None])
            else:
                outs.append(sm[6 * kind + small_names.index(name)].reshape(small_shape[name]))
    return tuple(outs)
```

```python
import jax
import jax.numpy as jnp
import numpy as np
from jax import lax
from jax.experimental import pallas as pl
from jax.experimental.pallas import tpu as pltpu

f32 = jnp.float32
bf16 = jnp.bfloat16

T = 4096
D = 1024
NSH = 4
IN_W = 2056
IN_S = IN_W // NSH
AW = 512
PAIRS = 4
FF = 2816
FS = FF // NSH
WINDOWS = (2, 4, 8, 16)
HALO = 16
EPS = 1e-6
NEG = -1e30
LR, B1, B2, AEPS, WD, STEP = 0.001, 0.9, 0.999, 1e-08, 0.01, 10
SMALL_ROWS = 40

NT = (((1,), (1,)), ((), ()))
TN = (((0,), (0,)), ((), ()))

MESH = pl.DeviceIdType.MESH


def _cp(*sem):
    return pltpu.CompilerParams(dimension_semantics=sem)


def _full(shape):
    n = len(shape)
    return pl.BlockSpec(shape, lambda *_: (0,) * n)


def _resident(shape):
    n = len(shape)
    return pl.BlockSpec(shape, lambda *_: (0,) * n, pipeline_mode=pl.Buffered(1))


def _rms_inproj(x, g1, wm, wf, dep):
    tm = 512

    def body(x_ref, g_ref, wm_ref, wf_ref, dep_ref, h_ref, qkv_ref, u_ref, fl_ref):
        xv = x_ref[...]
        r = lax.rsqrt(jnp.mean(xv * xv, axis=-1, keepdims=True) + EPS)
        h = (xv * r * g_ref[...]).astype(bf16)
        h_ref[...] = h
        qkv_ref[...] = lax.dot_general(h, wm_ref[0:3 * AW, :], NT, preferred_element_type=f32).astype(bf16)
        u_ref[...] = lax.dot_general(h, wm_ref[3 * AW:4 * AW, :], NT, preferred_element_type=f32)
        fl_ref[...] = lax.dot_general(h, wf_ref[...], NT, preferred_element_type=f32)

    return pl.pallas_call(
        body, name="rms_inproj", grid=(T // tm,),
        in_specs=[pl.BlockSpec((tm, D), lambda i: (i, 0)), _full((1, D)), _full((4 * AW, D)), _full((128, D)),
                  _full((8, 128))],
        out_specs=[pl.BlockSpec((tm, D), lambda i: (i, 0)), pl.BlockSpec((tm, 3 * AW), lambda i: (i, 0)),
                   pl.BlockSpec((tm, AW), lambda i: (i, 0)), pl.BlockSpec((tm, 128), lambda i: (i, 0))],
        out_shape=[jax.ShapeDtypeStruct((T, D), bf16), jax.ShapeDtypeStruct((T, 3 * AW), bf16),
                   jax.ShapeDtypeStruct((T, AW), f32), jax.ShapeDtypeStruct((T, 128), f32)],
        compiler_params=_cp("parallel"),
    )(x, g1, wm, wf, dep)


CUMSUM_ROWS = 512
FS_CHUNKS = ((0, 256), (256, 512), (512, FS))


def _log_sigmoid(z):
    return jnp.minimum(z, 0.0) - jnp.log(1.0 + jnp.exp(-jnp.abs(z)))


def _fox_cumsum(fl, bfp):
    tb = CUMSUM_ROWS
    nb = T // tb

    def body(fl_ref, b_ref, qa_ref, ka_ref, carry):
        i = pl.program_id(0)

        @pl.when(i == 0)
        def _():
            carry[...] = jnp.zeros_like(carry)

        lf = _log_sigmoid(fl_ref[...] + b_ref[...])
        r = lax.broadcasted_iota(jnp.int32, (tb, tb), 0)
        cc = lax.broadcasted_iota(jnp.int32, (tb, tb), 1)
        ltri = (cc <= r).astype(f32)
        cb = jnp.dot(ltri, lf, precision=lax.Precision.HIGHEST, preferred_element_type=f32) + carry[0:1, :]
        carry[...] = jnp.broadcast_to(cb[tb - 1:tb, :], (8, 128))
        hi = cb.astype(bf16)
        r1 = cb - hi.astype(f32)
        mid = r1.astype(bf16)
        lo = (r1 - mid.astype(f32)).astype(bf16)
        head = lax.broadcasted_iota(jnp.int32, (128, AW), 0)
        col = lax.broadcasted_iota(jnp.int32, (128, AW), 1)
        base = 128 * (head >> 1) + 64 * (1 - (head & 1))
        place = lambda off: jnp.logical_and(col == base + off, head < 8).astype(bf16)
        mm = lambda a, off: jnp.dot(a, place(off), preferred_element_type=f32)
        cq = mm(hi, 0) + mm(mid, 1) + mm(lo, 2)
        ck = mm(hi, 3) + mm(mid, 4) + mm(lo, 5)
        within = jnp.bitwise_and(lax.broadcasted_iota(jnp.int32, (tb, AW), 1), 63)
        qa_ref[...] = jnp.where(jnp.logical_and(within >= 3, within <= 5), 1.0, cq).astype(bf16)
        ka_ref[...] = jnp.where(within <= 2, 1.0, -ck).astype(bf16)

    return pl.pallas_call(
        body, name="fox_cumsum", grid=(nb,),
        in_specs=[pl.BlockSpec((tb, 128), lambda i: (i, 0)), _full((1, 128))],
        out_specs=[pl.BlockSpec((tb, AW), lambda i: (i, 0)), pl.BlockSpec((tb, AW), lambda i: (i, 0))],
        out_shape=[jax.ShapeDtypeStruct((T, AW), bf16), jax.ShapeDtypeStruct((T, AW), bf16)],
        scratch_shapes=[pltpu.VMEM((8, 128), f32)],
        compiler_params=_cp("arbitrary"),
    )(fl, bfp)


ATT_T = 512


def _causal_steps(key_major):
    n = T // ATT_T
    if key_major:
        pairs = [(i, j) for j in range(n) for i in range(j, n)]
    else:
        pairs = [(i, j) for i in range(n) for j in range(i + 1)]
    it = np.array([p[0] for p in pairs], np.int32)
    jt = np.array([p[1] for p in pairs], np.int32)
    return jnp.asarray(it), jnp.asarray(jt)


def _row_blocks(tq, tk, on_diagonal):
    return ((0, tq // 2, tk // 2), (tq // 2, tq, tk)) if on_diagonal else ((0, tq, tk),)


def _attn_fwd(qkv, qaug, kaug):
    tq = tk = ATT_T
    it, jt = _causal_steps(False)
    nsteps = it.shape[0]

    rs = 64

    def body(it_ref, jt_ref, q_ref, k_ref, v_ref, qa_ref, ka_ref, o_ref, lse_ref, m_sc, acc_sc, s_sc, p_sc, alpha_sc):
        t = pl.program_id(1)
        i = it_ref[t]
        j = jt_ref[t]

        @pl.when(j == 0)
        def _():
            m_sc[...] = jnp.full_like(m_sc, NEG)
            acc_sc[...] = jnp.zeros_like(acc_sc)

        lane = lax.broadcasted_iota(jnp.int32, (tq, 128), 1)
        spare = (64, 0)

        def step(on_diagonal):
            q = q_ref[...] * 0.125
            k = k_ref[...]
            v = v_ref[...]
            qa = qa_ref[...]
            ka = ka_ref[...]
            blocks = _row_blocks(tq, tk, on_diagonal)
            for e in range(2):
                hm = (lane >= 64) if e else (lane < 64)
                qe = jnp.where(hm, q, qa)
                ke = jnp.where(hm, k, ka)
                for r0, r1, nc in blocks:
                    s_sc[e, r0:r1, 0:nc] = lax.dot_general(qe[r0:r1], ke[0:nc], NT, preferred_element_type=f32)
            for e in range(2):
                for r0, r1, nc in blocks:
                    for r in range(r0, r1, rs):
                        s = s_sc[e, r:r + rs, 0:nc]
                        if on_diagonal:
                            row = lax.broadcasted_iota(jnp.int32, (rs, nc), 0) + r
                            col = lax.broadcasted_iota(jnp.int32, (rs, nc), 1)
                            s = jnp.where(col <= row, s, NEG)
                        m_prev = m_sc[e, r:r + rs, :]
                        m_new = jnp.maximum(m_prev, jnp.max(s, axis=1, keepdims=True))
                        p_sc[e, r:r + rs, 0:nc] = jnp.exp(s - jnp.tile(m_new, (1, nc // 128))).astype(bf16)
                        alpha_sc[e, r:r + rs, :] = jnp.exp(m_prev - m_new)
                        m_sc[e, r:r + rs, :] = m_new
            for e in range(2):
                hm = (lane >= 64) if e else (lane < 64)
                ve = jnp.where(hm, v, (lane == spare[e]).astype(bf16))
                for r0, r1, nc in blocks:
                    acc_sc[e, r0:r1] = (alpha_sc[e, r0:r1] * acc_sc[e, r0:r1]
                                        + jnp.dot(p_sc[e, r0:r1, 0:nc], ve[0:nc], preferred_element_type=f32))

        @pl.when(j < i)
        def _():
            step(False)

        @pl.when(j == i)
        def _():
            step(True)
            l0 = acc_sc[0][:, spare[0]:spare[0] + 1]
            l1 = acc_sc[1][:, spare[1]:spare[1] + 1]
            o_ref[...] = jnp.where(lane < 64, acc_sc[0] / l0, acc_sc[1] / l1).astype(bf16)
            lse_ref[...] = jnp.where(lane < 64, m_sc[0] + jnp.log(l0), m_sc[1] + jnp.log(l1))

    qmap = lambda p, t, it, jt: (it[t], p)
    kmap = lambda p, t, it, jt: (jt[t], p)
    grid_spec = pltpu.PrefetchScalarGridSpec(
        num_scalar_prefetch=2, grid=(PAIRS, nsteps),
        in_specs=[pl.BlockSpec((tq, 128), qmap),
                  pl.BlockSpec((tk, 128), lambda p, t, it, jt: (jt[t], PAIRS + p)),
                  pl.BlockSpec((tk, 128), lambda p, t, it, jt: (jt[t], 2 * PAIRS + p)),
                  pl.BlockSpec((tq, 128), qmap), pl.BlockSpec((tk, 128), kmap)],
        out_specs=[pl.BlockSpec((tq, 128), qmap),
                   pl.BlockSpec((None, tq, 128), lambda p, t, it, jt: (p, it[t], 0))],
        scratch_shapes=[pltpu.VMEM((2, tq, 128), f32), pltpu.VMEM((2, tq, 128), f32), pltpu.VMEM((2, tq, tk), f32),
                        pltpu.VMEM((2, tq, tk), bf16), pltpu.VMEM((2, tq, 128), f32)],
    )
    return pl.pallas_call(
        body, name="fox_attn_fwd", grid_spec=grid_spec,
        out_shape=[jax.ShapeDtypeStruct((T, AW), bf16), jax.ShapeDtypeStruct((PAIRS, T, 128), f32)],
        compiler_params=_cp("parallel", "arbitrary"),
    )(it, jt, qkv, qkv, qkv, qaug, kaug)


def _pool_fwd(u, wp, scale):
    tm = 512

    def body(u_ref, wp_ref, sc_ref, pooled_ref, pool_ref, ext):
        i = pl.program_id(0)

        @pl.when(i == 0)
        def _():
            ext[0:HALO, :] = jnp.zeros((HALO, AW), f32)

        uv = u_ref[...]
        ext[HALO:HALO + tm, :] = uv
        t_idx = i * tm + lax.broadcasted_iota(jnp.int32, (tm, 1), 0)
        for g, w in enumerate(WINDOWS):
            lo, hi = 128 * g, 128 * (g + 1)
            ug = uv[:, lo:hi]
            acc = ug
            for d in range(1, w):
                acc = acc + ext[HALO - d:HALO - d + tm, lo:hi]
            cnt = jnp.minimum(t_idx + 1, w).astype(f32)
            pb = (acc / cnt - ug).astype(bf16)
            pooled_ref[:, lo:hi] = pb
            mixed = jnp.dot(pb, wp_ref[g], preferred_element_type=f32)
            pool_ref[:, lo:hi] = (mixed * sc_ref[:, lo:hi]).astype(bf16)
        ext[0:HALO, :] = uv[tm - HALO:tm, :]

    return pl.pallas_call(
        body, name="pool_fwd", grid=(T // tm,),
        in_specs=[pl.BlockSpec((tm, AW), lambda i: (i, 0)), _full((4, 128, 128)), _full((1, AW))],
        out_specs=[pl.BlockSpec((tm, AW), lambda i: (i, 0)), pl.BlockSpec((tm, AW), lambda i: (i, 0))],
        out_shape=[jax.ShapeDtypeStruct((T, AW), bf16), jax.ShapeDtypeStruct((T, AW), bf16)],
        scratch_shapes=[pltpu.VMEM((tm + HALO, AW), f32)],
        compiler_params=_cp("arbitrary"),
    )(u, wp, scale)


def _outproj(x, attn, pool, wo, g2):
    tm = 1024

    def body(x_ref, a_ref, p_ref, wo_ref, g_ref, x1_ref, h2_ref):
        x1 = x_ref[...] + jnp.dot(a_ref[...], wo_ref[0:AW, :], preferred_element_type=f32)
        x1 = x1 + jnp.dot(p_ref[...], wo_ref[AW:2 * AW, :], preferred_element_type=f32)
        x1_ref[...] = x1
        r = lax.rsqrt(jnp.mean(x1 * x1, axis=-1, keepdims=True) + EPS)
        h2_ref[...] = (x1 * r * g_ref[...]).astype(bf16)

    return pl.pallas_call(
        body, name="outproj", grid=(T // tm,),
        in_specs=[pl.BlockSpec((tm, D), lambda i: (i, 0)), pl.BlockSpec((tm, AW), lambda i: (i, 0)),
                  pl.BlockSpec((tm, AW), lambda i: (i, 0)), _full((D, D)), _full((1, D))],
        out_specs=[pl.BlockSpec((tm, D), lambda i: (i, 0)), pl.BlockSpec((tm, D), lambda i: (i, 0))],
        out_shape=[jax.ShapeDtypeStruct((T, D), f32), jax.ShapeDtypeStruct((T, D), bf16)],
        compiler_params=_cp("parallel"),
    )(x, attn, pool, wo, g2)


def _mlp_fwd_loss(h2, x1, wg, wu, wd, tgt, gf):
    tm = 512

    def body(h_ref, x1_ref, wg_ref, wu_ref, wd_ref, t_ref, g_ref,
             loss_ref, dg_ref, dx_ref, dxb_ref, ud_ref, silu_ref, a_ref, x2):
        i = pl.program_id(0)
        s = pl.program_id(1)

        @pl.when(jnp.logical_and(i == 0, s == 0))
        def _():
            loss_ref[...] = jnp.zeros_like(loss_ref)
            dg_ref[...] = jnp.zeros_like(dg_ref)

        h = h_ref[...]
        gus = [(lax.dot_general(h, wg_ref[s, c0:c1, :], NT, preferred_element_type=f32),
                lax.dot_general(h, wu_ref[s, c0:c1, :], NT, preferred_element_type=f32)) for c0, c1 in FS_CHUNKS]
        for (c0, c1), (gate, up) in zip(FS_CHUNKS, gus):
            sg = jax.nn.sigmoid(gate)
            silu = gate * sg
            ud_ref[:, c0:c1] = (up * (sg * (1.0 + gate * (1.0 - sg)))).astype(bf16)
            silu_ref[:, c0:c1] = silu.astype(bf16)
            a_ref[:, c0:c1] = (silu * up).astype(bf16)
        part = jnp.dot(a_ref[...], wd_ref[s], preferred_element_type=f32)

        @pl.when(s == 0)
        def _():
            x2[...] = x1_ref[...] + part

        @pl.when(s > 0)
        def _():
            x2[...] += part

        @pl.when(s == NSH - 1)
        def _():
            xv = x2[...]
            g = g_ref[...]
            r = lax.rsqrt(jnp.mean(xv * xv, axis=-1, keepdims=True) + EPS)
            xhat = xv * r
            e = xhat * g - t_ref[...]
            loss_ref[...] += 0.5 * jnp.sum(jnp.mean(e * e, axis=-1, keepdims=True))
            dy = e * (1.0 / D)
            dg_ref[...] += jnp.sum(dy * xhat, axis=0, keepdims=True)
            z = dy * g
            dx = r * (z - xhat * jnp.mean(z * xhat, axis=-1, keepdims=True))
            dx_ref[...] = dx
            dxb_ref[...] = dx.astype(bf16)

    row = lambda i, s: (i, 0)
    sl = lambda i, s: (s, i, 0)
    wsl = lambda i, s: (s, 0, 0)
    return pl.pallas_call(
        body, name="mlp_fwd_loss", grid=(T // tm, NSH),
        in_specs=[pl.BlockSpec((tm, D), row), pl.BlockSpec((tm, D), row),
                  _resident((NSH, FS, D)), _resident((NSH, FS, D)), _resident((NSH, FS, D)),
                  pl.BlockSpec((tm, D), row), pl.BlockSpec((1, D), lambda i, s: (0, 0))],
        out_specs=[pl.BlockSpec((8, 128), lambda i, s: (0, 0)), pl.BlockSpec((1, D), lambda i, s: (0, 0)),
                   pl.BlockSpec((tm, D), row), pl.BlockSpec((tm, D), row),
                   pl.BlockSpec((None, tm, FS), sl), pl.BlockSpec((None, tm, FS), sl), pl.BlockSpec((None, tm, FS), sl)],
        out_shape=[jax.ShapeDtypeStruct((8, 128), f32), jax.ShapeDtypeStruct((1, D), f32),
                   jax.ShapeDtypeStruct((T, D), f32), jax.ShapeDtypeStruct((T, D), bf16)]
        + [jax.ShapeDtypeStruct((NSH, T, FS), bf16)] * 3,
        scratch_shapes=[pltpu.VMEM((tm, D), f32)],
        compiler_params=_cp("arbitrary", "arbitrary"),
    )(h2, x1, wg, wu, wd, tgt, gf)


def _mlp_bwd(dx2b, dx2, ud, silu, wg, wu, wd, x1, g2):
    tm = 512

    def body(dxb_ref, dx_ref, ud_ref, silu_ref, wg_ref, wu_ref, wd_ref, x1_ref, g_ref,
             dg_ref, du_ref, dx1_ref, dx1b_ref, dn_ref, acc):
        i = pl.program_id(0)
        s = pl.program_id(1)

        @pl.when(jnp.logical_and(i == 0, s == 0))
        def _():
            dn_ref[...] = jnp.zeros_like(dn_ref)

        dxb = dxb_ref[...]
        das = [lax.dot_general(dxb, wd_ref[s, c0:c1, :], NT, preferred_element_type=f32) for c0, c1 in FS_CHUNKS]
        for (c0, c1), da in zip(FS_CHUNKS, das):
            dg_ref[:, c0:c1] = (da * ud_ref[:, c0:c1].astype(f32)).astype(bf16)
            du_ref[:, c0:c1] = (da * silu_ref[:, c0:c1].astype(f32)).astype(bf16)
        part = jnp.dot(dg_ref[...], wg_ref[s], preferred_element_type=f32)
        part = part + jnp.dot(du_ref[...], wu_ref[s], preferred_element_type=f32)

        @pl.when(s == 0)
        def _():
            acc[...] = part

        @pl.when(s > 0)
        def _():
            acc[...] += part

        @pl.when(s == NSH - 1)
        def _():
            xv = x1_ref[...]
            r = lax.rsqrt(jnp.mean(xv * xv, axis=-1, keepdims=True) + EPS)
            xhat = xv * r
            dh = acc[...]
            dn_ref[...] += jnp.sum(dh * xhat, axis=0, keepdims=True)
            z = dh * g_ref[...]
            dx1 = dx_ref[...] + r * (z - xhat * jnp.mean(z * xhat, axis=-1, keepdims=True))
            dx1_ref[...] = dx1
            dx1b_ref[...] = dx1.astype(bf16)

    row = lambda i, s: (i, 0)
    sl = lambda i, s: (s, i, 0)
    wsl = lambda i, s: (s, 0, 0)
    return pl.pallas_call(
        body, name="mlp_bwd", grid=(T // tm, NSH),
        in_specs=[pl.BlockSpec((tm, D), row), pl.BlockSpec((tm, D), row),
                  pl.BlockSpec((None, tm, FS), sl), pl.BlockSpec((None, tm, FS), sl),
                  _resident((NSH, FS, D)), _resident((NSH, FS, D)), _resident((NSH, FS, D)),
                  pl.BlockSpec((tm, D), row), pl.BlockSpec((1, D), lambda i, s: (0, 0))],
        out_specs=[pl.BlockSpec((None, tm, FS), sl), pl.BlockSpec((None, tm, FS), sl),
                   pl.BlockSpec((tm, D), row), pl.BlockSpec((tm, D), row), pl.BlockSpec((1, D), lambda i, s: (0, 0))],
        out_shape=[jax.ShapeDtypeStruct((NSH, T, FS), bf16)] * 2
        + [jax.ShapeDtypeStruct((T, D), f32), jax.ShapeDtypeStruct((T, D), bf16), jax.ShapeDtypeStruct((1, D), f32)],
        scratch_shapes=[pltpu.VMEM((tm, D), f32)],
        compiler_params=_cp("arbitrary", "arbitrary"),
    )(dx2b, dx2, ud, silu, wg, wu, wd, x1, g2)


def _mm_tn(a, bs, name, a_sharded=False, b_sharded=False, tk=512, out_dtype=bf16):
    nb = len(bs)
    sh = NSH if (a_sharded or b_sharded) else 1
    m = a.shape[-1]
    nk = T // tk

    def body(a_ref, *refs):
        kk = pl.program_id(1)
        av = a_ref[...]
        for b_ref, o_ref, acc in zip(refs[:nb], refs[nb:2 * nb], refs[2 * nb:]):
            upd = lax.dot_general(av, b_ref[...], TN, preferred_element_type=f32)

            @pl.when(kk == 0)
            def _():
                acc[...] = upd

            @pl.when(kk > 0)
            def _():
                acc[...] += upd

            @pl.when(kk == nk - 1)
            def _():
                o_ref[...] = acc[...].astype(out_dtype)

    a_spec = (pl.BlockSpec((None, tk, m), lambda s, k: (s, k, 0)) if a_sharded
              else pl.BlockSpec((tk, m), lambda s, k: (k, 0)))
    b_specs, o_specs, o_shapes, scratch = [], [], [], []
    for b in bs:
        n = b.shape[-1]
        b_specs.append(pl.BlockSpec((None, tk, n), lambda s, k: (s, k, 0)) if b_sharded
                       else pl.BlockSpec((tk, n), lambda s, k: (k, 0)))
        scratch.append(pltpu.VMEM((m, n), f32))
        if sh > 1:
            o_specs.append(pl.BlockSpec((None, m, n), lambda s, k: (s, 0, 0)))
            o_shapes.append(jax.ShapeDtypeStruct((sh, m, n), out_dtype))
        else:
            o_specs.append(pl.BlockSpec((m, n), lambda s, k: (0, 0)))
            o_shapes.append(jax.ShapeDtypeStruct((m, n), out_dtype))
    return pl.pallas_call(
        body, name=name, grid=(sh, nk), in_specs=[a_spec] + b_specs, out_specs=o_specs, out_shape=o_shapes,
        scratch_shapes=scratch, compiler_params=_cp("arbitrary", "arbitrary"),
    )(a, *bs)


def _mm_tn_rows(a_list, b, name, tk=1024, out_dtype=bf16):
    na = len(a_list)
    n = b.shape[-1]
    nk = T // tk

    def body(*refs):
        a_refs, b_ref = refs[:na], refs[na]
        o_refs, accs = refs[na + 1:2 * na + 1], refs[2 * na + 1:]
        kk = pl.program_id(0)
        bv = b_ref[...]
        for a_ref, o_ref, acc in zip(a_refs, o_refs, accs):
            upd = lax.dot_general(a_ref[...], bv, TN, preferred_element_type=f32)

            @pl.when(kk == 0)
            def _():
                acc[...] = upd

            @pl.when(kk > 0)
            def _():
                acc[...] += upd

            @pl.when(kk == nk - 1)
            def _():
                o_ref[...] = acc[...].astype(out_dtype)

    return pl.pallas_call(
        body, name=name, grid=(nk,),
        in_specs=[pl.BlockSpec((tk, a.shape[-1]), lambda k: (k, 0)) for a in a_list] + [pl.BlockSpec((tk, n), lambda k: (k, 0))],
        out_specs=[pl.BlockSpec((a.shape[-1], n), lambda k: (0, 0)) for a in a_list],
        out_shape=[jax.ShapeDtypeStruct((a.shape[-1], n), out_dtype) for a in a_list],
        scratch_shapes=[pltpu.VMEM((a.shape[-1], n), f32) for a in a_list],
        compiler_params=_cp("arbitrary"),
    )(*a_list, b)


def _outproj_bwd(dx1b, wo):
    tm = 1024

    def body(dx_ref, wo_ref, da_ref, dp_ref):
        dx = dx_ref[...]
        da_ref[...] = lax.dot_general(dx, wo_ref[0:AW, :], NT, preferred_element_type=f32).astype(bf16)
        dp_ref[...] = lax.dot_general(dx, wo_ref[AW:2 * AW, :], NT, preferred_element_type=f32)

    return pl.pallas_call(
        body, name="outproj_bwd", grid=(T // tm,),
        in_specs=[pl.BlockSpec((tm, D), lambda i: (i, 0)), _full((D, D))],
        out_specs=[pl.BlockSpec((tm, AW), lambda i: (i, 0)), pl.BlockSpec((tm, AW), lambda i: (i, 0))],
        out_shape=[jax.ShapeDtypeStruct((T, AW), bf16), jax.ShapeDtypeStruct((T, AW), f32)],
        compiler_params=_cp("parallel"),
    )(dx1b, wo)


def _pool_bwd(dpool, pooled, wp, scale):
    tm = 512
    n = T // tm

    def body(dp_ref, pb_ref, wp_ref, sc_ref, du_ref, dsc_ref, dwp_ref, ext):
        i = pl.program_id(0)

        @pl.when(i == 0)
        def _():
            ext[tm:tm + HALO, :] = jnp.zeros((HALO, AW), f32)
            dsc_ref[...] = jnp.zeros_like(dsc_ref)
            dwp_ref[...] = jnp.zeros_like(dwp_ref)

        t_idx = (n - 1 - i) * tm + lax.broadcasted_iota(jnp.int32, (tm, 1), 0)
        for g, w in enumerate(WINDOWS):
            lo, hi = 128 * g, 128 * (g + 1)
            pb = pb_ref[:, lo:hi]
            mixed = jnp.dot(pb, wp_ref[g], preferred_element_type=f32)
            dpo = dp_ref[:, lo:hi]
            dsc_ref[:, lo:hi] += jnp.sum(dpo * mixed, axis=0, keepdims=True)
            dmr = (dpo * sc_ref[:, lo:hi]).astype(bf16)
            dwp_ref[g] += lax.dot_general(pb, dmr, TN, preferred_element_type=f32)
            dpl = lax.dot_general(dmr, wp_ref[g], NT, preferred_element_type=f32)
            cnt = jnp.minimum(t_idx + 1, w).astype(f32)
            dpn = dpl / cnt
            ext[0:tm, lo:hi] = dpn
            acc = dpn
            for d in range(1, w):
                acc = acc + ext[d:d + tm, lo:hi]
            du_ref[:, lo:hi] = (acc - dpl).astype(bf16)
        ext[tm:tm + HALO, :] = ext[0:HALO, :]

    rev = lambda i: (n - 1 - i, 0)
    return pl.pallas_call(
        body, name="pool_bwd", grid=(n,),
        in_specs=[pl.BlockSpec((tm, AW), rev), pl.BlockSpec((tm, AW), rev), _full((4, 128, 128)), _full((1, AW))],
        out_specs=[pl.BlockSpec((tm, AW), rev), _full((1, AW)), _full((4, 128, 128))],
        out_shape=[jax.ShapeDtypeStruct((T, AW), bf16), jax.ShapeDtypeStruct((1, AW), f32),
                   jax.ShapeDtypeStruct((4, 128, 128), f32)],
        scratch_shapes=[pltpu.VMEM((tm + HALO, AW), f32)],
        compiler_params=_cp("arbitrary"),
    )(dpool, pooled, wp, scale)


def _attn_bwd(qkv, qaug, kaug, attn, dattn, lse, dep):
    tq = tk = ATT_T
    n = T // tq
    it, jt = _causal_steps(True)
    nsteps = it.shape[0]

    rs = 64

    def body(it_ref, jt_ref, q_ref, k_ref, v_ref, qa_ref, ka_ref, o_ref, do_ref, lse_ref, dep_ref,
             dq_ref, dqs_ref, dk_ref, dks_ref, dv_ref, dq_acc, dk_acc, dv_acc, s_sc, dp_sc, p_sc, ds_sc):
        t = pl.program_id(1)
        i = it_ref[t]
        j = jt_ref[t]

        @pl.when(t == 0)
        def _():
            dq_acc[...] = jnp.zeros_like(dq_acc)

        @pl.when(i == j)
        def _():
            dk_acc[...] = jnp.zeros_like(dk_acc)
            dv_acc[...] = jnp.zeros_like(dv_acc)

        lane = lax.broadcasted_iota(jnp.int32, (tq, 128), 1)

        def step(on_diagonal):
            q = q_ref[...] * 0.125
            k = k_ref[...]
            v = v_ref[...]
            qa = qa_ref[...]
            ka = ka_ref[...]
            do = do_ref[...]
            dd = do.astype(f32) * o_ref[...].astype(f32)
            blocks = _row_blocks(tq, tk, on_diagonal)
            qes, kes, does, deltas = [], [], [], []
            for e in range(2):
                hm = (lane >= 64) if e else (lane < 64)
                qes.append(jnp.where(hm, q, qa))
                kes.append(jnp.where(hm, k, ka))
                does.append(jnp.where(hm, do, jnp.zeros_like(do)))
                deltas.append(jnp.sum(jnp.where(hm, dd, 0.0), axis=1, keepdims=True))
                for r0, r1, nc in blocks:
                    s_sc[e, r0:r1, 0:nc] = lax.dot_general(qes[e][r0:r1], kes[e][0:nc], NT, preferred_element_type=f32)
                    dp_sc[e, r0:r1, 0:nc] = lax.dot_general(does[e][r0:r1], v[0:nc], NT, preferred_element_type=f32)
            for e in range(2):
                for r0, r1, nc in blocks:
                    for r in range(r0, r1, rs):
                        s = s_sc[e, r:r + rs, 0:nc] - lse_ref[r:r + rs, 64 * e:64 * e + 1]
                        if on_diagonal:
                            row = lax.broadcasted_iota(jnp.int32, (rs, nc), 0) + r
                            col = lax.broadcasted_iota(jnp.int32, (rs, nc), 1)
                            s = jnp.where(col <= row, s, NEG)
                        p = jnp.exp(s)
                        p_sc[e, r:r + rs, 0:nc] = p.astype(bf16)
                        ds_sc[e, r:r + rs, 0:nc] = (p * (dp_sc[e, r:r + rs, 0:nc] - deltas[e][r:r + rs, :])).astype(bf16)
                for r0, r1, nc in blocks:
                    dv_acc[:, 0:nc] += lax.dot_general(does[e][r0:r1], p_sc[e, r0:r1, 0:nc], TN, preferred_element_type=f32)
                    dsb = ds_sc[e, r0:r1, 0:nc]
                    dk_acc[e, :, 0:nc] += lax.dot_general(qes[e][r0:r1], dsb, TN, preferred_element_type=f32)
                    rq = pl.multiple_of(i * tq + r0, r1 - r0)
                    dq_acc[e, pl.ds(rq, r1 - r0), :] += jnp.dot(dsb, kes[e][0:nc], preferred_element_type=f32)

        @pl.when(i > j)
        def _():
            step(False)

        @pl.when(i == j)
        def _():
            step(True)

        @pl.when(i == n - 1)
        def _():
            dk0 = dk_acc[0].T
            dk1 = dk_acc[1].T
            dk_ref[...] = jnp.where(lane < 64, dk0, dk1).astype(bf16)
            dks_ref[...] = jnp.where(lane < 64, dk1, dk0)
            dv_ref[...] = dv_acc[...].T.astype(bf16)

        @pl.when(t == nsteps - 1)
        def _():
            lane_t = lax.broadcasted_iota(jnp.int32, (T, 128), 1)
            dq_ref[...] = (jnp.where(lane_t < 64, dq_acc[0], dq_acc[1]) * 0.125).astype(bf16)
            dqs_ref[...] = jnp.where(lane_t < 64, dq_acc[1], dq_acc[0])

    qmap = lambda p, t, it, jt: (it[t], p)
    grid_spec = pltpu.PrefetchScalarGridSpec(
        num_scalar_prefetch=2, grid=(PAIRS, nsteps),
        in_specs=[pl.BlockSpec((tq, 128), qmap),
                  pl.BlockSpec((tk, 128), lambda p, t, it, jt: (jt[t], PAIRS + p)),
                  pl.BlockSpec((tk, 128), lambda p, t, it, jt: (jt[t], 2 * PAIRS + p)),
                  pl.BlockSpec((tq, 128), qmap), pl.BlockSpec((tk, 128), lambda p, t, it, jt: (jt[t], p)),
                  pl.BlockSpec((tq, 128), qmap), pl.BlockSpec((tq, 128), qmap),
                  pl.BlockSpec((None, tq, 128), lambda p, t, it, jt: (p, it[t], 0)),
                  pl.BlockSpec((8, 128), lambda p, t, it, jt: (0, 0))],
        out_specs=[pl.BlockSpec((T, 128), lambda p, t, it, jt: (0, p)),
                   pl.BlockSpec((None, T, 128), lambda p, t, it, jt: (p, 0, 0)),
                   pl.BlockSpec((tk, 128), lambda p, t, it, jt: (jt[t], p)),
                   pl.BlockSpec((None, tk, 128), lambda p, t, it, jt: (p, jt[t], 0)),
                   pl.BlockSpec((tk, 128), lambda p, t, it, jt: (jt[t], p))],
        scratch_shapes=[pltpu.VMEM((2, T, 128), f32), pltpu.VMEM((2, 128, tk), f32), pltpu.VMEM((128, tk), f32),
                        pltpu.VMEM((2, tq, tk), f32), pltpu.VMEM((2, tq, tk), f32), pltpu.VMEM((2, tq, tk), bf16),
                        pltpu.VMEM((2, tq, tk), bf16)],
    )
    return pl.pallas_call(
        body, name="fox_attn_bwd", grid_spec=grid_spec,
        out_shape=[jax.ShapeDtypeStruct((T, AW), bf16), jax.ShapeDtypeStruct((PAIRS, T, 128), f32),
                   jax.ShapeDtypeStruct((T, AW), bf16), jax.ShapeDtypeStruct((PAIRS, T, 128), f32),
                   jax.ShapeDtypeStruct((T, AW), bf16)],
        compiler_params=_cp("parallel", "arbitrary"),
    )(it, jt, qkv, qkv, qkv, qaug, kaug, attn, dattn, lse, dep)


def _fox_cumsum_bwd(dqs, dks, fl, bfp):
    tb = CUMSUM_ROWS
    nb = T // tb
    hp = lax.Precision.HIGHEST

    def body(dqs_ref, dks_ref, fl_ref, b_ref, df_ref, db_ref, carry):
        i = pl.program_id(0)

        @pl.when(i == 0)
        def _():
            carry[...] = jnp.zeros_like(carry)
            db_ref[...] = jnp.zeros_like(db_ref)

        r = lax.broadcasted_iota(jnp.int32, (128, 128), 0)
        cc = lax.broadcasted_iota(jnp.int32, (128, 128), 1)
        pick = lambda even_lane, odd_lane, p: jnp.logical_or(
            jnp.logical_and(r == even_lane, cc == 2 * p), jnp.logical_and(r == odd_lane, cc == 2 * p + 1)).astype(f32)
        dc = jnp.zeros((tb, 128), f32)
        for p in range(PAIRS):
            dc = dc + jnp.dot(dqs_ref[p], pick(64, 0, p), precision=hp, preferred_element_type=f32)
            dc = dc - jnp.dot(dks_ref[p], pick(67, 3, p), precision=hp, preferred_element_type=f32)
        rt = lax.broadcasted_iota(jnp.int32, (tb, tb), 0)
        ct = lax.broadcasted_iota(jnp.int32, (tb, tb), 1)
        utri = (ct >= rt).astype(f32)
        dl = jnp.dot(utri, dc, precision=hp, preferred_element_type=f32) + carry[0:1, :]
        carry[...] = jnp.broadcast_to(dl[0:1, :], (8, 128))
        z = fl_ref[...] + b_ref[...]
        df = dl * jax.nn.sigmoid(-z)
        df_ref[...] = df.astype(bf16)
        db_ref[...] += jnp.sum(df, axis=0, keepdims=True)

    rev = lambda i: (nb - 1 - i, 0)
    return pl.pallas_call(
        body, name="fox_cumsum_bwd", grid=(nb,),
        in_specs=[pl.BlockSpec((PAIRS, tb, 128), lambda i: (0, nb - 1 - i, 0)),
                  pl.BlockSpec((PAIRS, tb, 128), lambda i: (0, nb - 1 - i, 0)),
                  pl.BlockSpec((tb, 128), rev), _full((1, 128))],
        out_specs=[pl.BlockSpec((tb, 128), rev), _full((1, 128))],
        out_shape=[jax.ShapeDtypeStruct((T, 128), bf16), jax.ShapeDtypeStruct((1, 128), f32)],
        scratch_shapes=[pltpu.VMEM((8, 128), f32)],
        compiler_params=_cp("arbitrary"),
    )(dqs, dks, fl, bfp)


def _inproj_bwd(dq, dk, dv, du, df, wm, wf, x, dx1, g1):
    tm = 512

    def body(dq_ref, dk_ref, dv_ref, du_ref, df_ref, wm_ref, wf_ref, x_ref, dx1_ref, g_ref, dx_ref, dn_ref):
        i = pl.program_id(0)

        @pl.when(i == 0)
        def _():
            dn_ref[...] = jnp.zeros_like(dn_ref)

        dh = jnp.dot(dq_ref[...], wm_ref[0:AW, :], preferred_element_type=f32)
        dh = dh + jnp.dot(dk_ref[...], wm_ref[AW:2 * AW, :], preferred_element_type=f32)
        dh = dh + jnp.dot(dv_ref[...], wm_ref[2 * AW:3 * AW, :], preferred_element_type=f32)
        dh = dh + jnp.dot(du_ref[...], wm_ref[3 * AW:4 * AW, :], preferred_element_type=f32)
        dh = dh + jnp.dot(df_ref[...], wf_ref[...], preferred_element_type=f32)
        xv = x_ref[...]
        r = lax.rsqrt(jnp.mean(xv * xv, axis=-1, keepdims=True) + EPS)
        xhat = xv * r
        dn_ref[...] += jnp.sum(dh * xhat, axis=0, keepdims=True)
        z = dh * g_ref[...]
        dx_ref[...] = dx1_ref[...] + r * (z - xhat * jnp.mean(z * xhat, axis=-1, keepdims=True))

    row = lambda i: (i, 0)
    return pl.pallas_call(
        body, name="inproj_bwd", grid=(T // tm,),
        in_specs=[pl.BlockSpec((tm, AW), row)] * 4 + [pl.BlockSpec((tm, 128), row), _full((4 * AW, D)), _full((128, D)),
                                                       pl.BlockSpec((tm, D), row), pl.BlockSpec((tm, D), row), _full((1, D))],
        out_specs=[pl.BlockSpec((tm, D), row), _full((1, D))],
        out_shape=[jax.ShapeDtypeStruct((T, D), f32), jax.ShapeDtypeStruct((1, D), f32)],
        compiler_params=_cp("arbitrary"),
    )(dq, dk, dv, du, df, wm, wf, x, dx1, g1)


def _adamw_math(w, g, m, v):
    m = B1 * m + (1.0 - B1) * g
    v = B2 * v + (1.0 - B2) * (g * g)
    m_hat = m / (1.0 - B1 ** STEP)
    v_hat = v / (1.0 - B2 ** STEP)
    delta = -LR * (m_hat / (jnp.sqrt(v_hat) + AEPS) + WD * w)
    return delta, m, v


def _adamw_shard(w, m, v, p_mine, p_other, name):
    rows, cols = w.shape
    tr = rows if rows <= IN_S else rows // 2

    def body(w_ref, m_ref, v_ref, a_ref, b_ref, g_ref, d_ref, nm_ref, nv_ref):
        g = a_ref[...].astype(f32) + b_ref[...].astype(f32)
        g_ref[...] = g
        d_ref[...], nm_ref[...], nv_ref[...] = _adamw_math(w_ref[...], g, m_ref[...], v_ref[...])

    spec = pl.BlockSpec((tr, cols), lambda i: (i, 0))
    return pl.pallas_call(
        body, name=name, grid=(rows // tr,), in_specs=[spec] * 5, out_specs=[spec] * 4,
        out_shape=[jax.ShapeDtypeStruct((rows, cols), f32)] * 4, compiler_params=_cp("parallel"),
    )(w, m, v, p_mine, p_other)


SMALL_SLOTS = ((0, 8, 128), (8, 16, 128), (16, 24, 128), (24, 28, 128), (32, 33, 8))
LOSS_ROW = 39


def _adamw_small(ws, ms, vs, parts, parts_wp):
    n = len(ws)

    def body(*refs):
        w_refs, m_refs, v_refs = refs[0:n], refs[n:2 * n], refs[2 * n:3 * n]
        p_ref, pw_ref = refs[3 * n], refs[3 * n + 1]
        outs = refs[3 * n + 2:]
        g_all = p_ref[0]
        g_wp = pw_ref[0]
        for k in range(1, 8):
            g_all = g_all + p_ref[k]
            g_wp = g_wp + pw_ref[k]
        grads = [g_all[r0:r1, 0:lanes] for r0, r1, lanes in SMALL_SLOTS] + [g_wp]
        for idx, g in enumerate(grads):
            d, nm, nv = _adamw_math(w_refs[idx][...], g, m_refs[idx][...], v_refs[idx][...])
            outs[idx][...] = g
            outs[n + idx][...] = d
            outs[2 * n + idx][...] = nm
            outs[3 * n + idx][...] = nv
        outs[4 * n][...] = g_all[LOSS_ROW:LOSS_ROW + 1, :]

    shapes = [jax.ShapeDtypeStruct(w.shape, f32) for w in ws]
    res = pl.pallas_call(
        body, name="adamw_small", out_shape=shapes * 4 + [jax.ShapeDtypeStruct((1, 128), f32)],
    )(*ws, *ms, *vs, parts, parts_wp)
    return res[:4 * n], res[4 * n]


def _sum4(recv, g, mine, name):
    _, rows, cols = recv.shape
    tr = rows if rows <= IN_S else rows // 2

    def body(mine_ref, r_ref, g_ref, o_ref):
        o_ref[...] = ((g_ref[...].astype(f32) + r_ref[0].astype(f32))
                      + (r_ref[1].astype(f32) + r_ref[2].astype(f32))).astype(bf16)

    grid_spec = pltpu.PrefetchScalarGridSpec(
        num_scalar_prefetch=1, grid=(rows // tr,),
        in_specs=[pl.BlockSpec((3, tr, cols), lambda i, m: (0, i, 0)),
                  pl.BlockSpec((None, tr, cols), lambda i, m: (m[0], i, 0))],
        out_specs=pl.BlockSpec((tr, cols), lambda i, m: (i, 0)))
    return pl.pallas_call(
        body, name=name, grid_spec=grid_spec, out_shape=jax.ShapeDtypeStruct((rows, cols), bf16),
        compiler_params=_cp("arbitrary"),
    )(mine, recv, g)


_HBM = pl.BlockSpec(memory_space=pltpu.HBM)
_SEM = pl.BlockSpec(memory_space=pltpu.SEMAPHORE)
_EFFECT = pltpu.SideEffectType.DATAFLOW_SIDE_EFFECTING


def _in_hbm(a):
    return pltpu.with_memory_space_constraint(a, pltpu.HBM)


def _mesh_pos():
    return lax.axis_index("x"), lax.axis_index("y"), lax.axis_index("c")


def _other_chips(x, y):
    return [(1 - x, y), (x, 1 - y), (1 - x, 1 - y)]


def _gather_copy(srcs, lands, send_sems, recv_sems, a, k, slot):
    x, y, c = _mesh_pos()
    cx, cy = _other_chips(x, y)[k]
    return pltpu.make_async_remote_copy(
        src_ref=srcs[a], dst_ref=lands[a].at[slot], send_sem=send_sems.at[3 * a + k], recv_sem=recv_sems.at[3 * a + k],
        device_id=(cx, cy, c), device_id_type=MESH)


def _scatter_copy(srcs, lands, send_sems, recv_sems, a, k):
    x, y, c = _mesh_pos()
    cx, cy = _other_chips(x, y)[k]
    return pltpu.make_async_remote_copy(
        src_ref=srcs[a].at[2 * cx + cy], dst_ref=lands[a].at[k], send_sem=send_sems.at[3 * a + k],
        recv_sem=recv_sems.at[3 * a + k], device_id=(cx, cy, c), device_id_type=MESH)


def _all_gather_w_in(part):
    cols = part.shape[1] // 2

    def body(src, dst, send_sems, recv_sems, loc_sem):
        x, y, c = _mesh_pos()
        mine = 2 * x + y
        chips = _other_chips(x, y)
        half = lambda ref, cc: ref.at[:, pl.ds(pl.multiple_of(cc * cols, cols), cols)]

        def over_ici(k, slot):
            cx, cy = chips[k]
            return pltpu.make_async_remote_copy(
                src_ref=half(src, c), dst_ref=half(dst.at[slot], c), send_sem=send_sems.at[k], recv_sem=recv_sems.at[k],
                device_id=(cx, cy, c), device_id_type=MESH)

        def to_sibling(k, cc):
            slot = 2 * chips[k][0] + chips[k][1]
            return pltpu.make_async_remote_copy(
                src_ref=half(dst.at[slot], cc), dst_ref=half(dst.at[slot], cc), send_sem=send_sems.at[3 + k],
                recv_sem=recv_sems.at[3 + k], device_id=(x, y, 1 - c), device_id_type=MESH)

        local = pltpu.make_async_copy(src, dst.at[mine], loc_sem.at[0])
        local.start()
        first = [over_ici(k, mine) for k in range(3)]
        for cp in first:
            cp.start()
        passed = [to_sibling(k, c) for k in range(3)]
        for k in range(3):
            over_ici(k, 2 * chips[k][0] + chips[k][1]).wait_recv()
            passed[k].start()
        for k in range(3):
            to_sibling(k, 1 - c).wait_recv()
        for cp in first + passed:
            cp.wait_send()
        local.wait()

    return pl.pallas_call(
        body, name="all_gather_w_in", in_specs=[_HBM], out_specs=_HBM,
        out_shape=jax.ShapeDtypeStruct((NSH,) + part.shape, part.dtype),
        scratch_shapes=[pltpu.SemaphoreType.DMA((6,)), pltpu.SemaphoreType.DMA((6,)), pltpu.SemaphoreType.DMA((1,))],
    )(part)


def _split_start(name, srcs, lands, n_sems, plan, dep):
    n, nl = len(srcs), len(lands)

    def body(*refs):
        src_refs, land_refs = refs[:n], refs[n:n + nl]
        send_sems, recv_sems = refs[n + nl + 1], refs[n + nl + 2]
        token = refs[-1]
        sends, _ = plan(src_refs, land_refs, send_sems, recv_sems)
        for cp in sends:
            cp.start()
        token[...] = jnp.zeros_like(token)

    outs = pl.pallas_call(
        body, name=name,
        in_specs=[_HBM] * (n + nl) + [pl.BlockSpec(memory_space=pl.ANY)],
        out_specs=[_SEM, _SEM] + [_HBM] * (n + nl) + [pl.BlockSpec(memory_space=pltpu.VMEM)],
        out_shape=[pltpu.SemaphoreType.DMA((n_sems,)), pltpu.SemaphoreType.DMA((n_sems,))]
        + [pltpu.HBM(a.shape, a.dtype) for a in list(srcs) + list(lands)] + [jax.ShapeDtypeStruct((8, 128), f32)],
        input_output_aliases={i: 2 + i for i in range(n + nl)},
        compiler_params=pltpu.CompilerParams(has_side_effects=_EFFECT),
    )(*[_in_hbm(a) for a in list(srcs) + list(lands)], dep)
    return outs[0], outs[1], list(outs[2:2 + n]), list(outs[2 + n:2 + n + nl]), outs[-1]


def _split_wait(name, send_sems, recv_sems, srcs, lands, after, plan):
    n, nl = len(srcs), len(lands)

    def body(*refs):
        src_refs, land_refs = refs[:n], refs[n:n + nl]
        s_sems, r_sems = refs[n + nl], refs[n + nl + 1]
        sends, recvs = plan(src_refs, land_refs, s_sems, r_sems)
        for cp in recvs:
            cp.wait_recv()
        for cp in sends:
            cp.wait_send()

    outs = pl.pallas_call(
        body, name=name,
        in_specs=[_HBM] * (n + nl) + [_SEM, _SEM, pl.BlockSpec(memory_space=pl.ANY)],
        out_specs=[_HBM] * (n + nl),
        out_shape=[pltpu.HBM(a.shape, a.dtype) for a in list(srcs) + list(lands)],
        input_output_aliases={i: i for i in range(n + nl)},
        compiler_params=pltpu.CompilerParams(has_side_effects=_EFFECT),
    )(*srcs, *lands, send_sems, recv_sems, after)
    return list(outs[:n]), list(outs[n:])


def _gather_plan(srcs, lands, ss, rs):
    x, y, _ = _mesh_pos()
    chips = _other_chips(x, y)
    sends = [_gather_copy(srcs, lands, ss, rs, a, k, 2 * x + y) for a in range(len(srcs)) for k in range(3)]
    recvs = [_gather_copy(srcs, lands, ss, rs, a, k, 2 * chips[k][0] + chips[k][1])
             for a in range(len(srcs)) for k in range(3)]
    return sends, recvs


def _scatter_plan(srcs, lands, ss, rs):
    cps = [_scatter_copy(srcs, lands, ss, rs, a, k) for a in range(len(srcs)) for k in range(3)]
    return cps, cps


def _scatter_and_spread_plan(srcs, lands, ss, rs):
    x, y, c = _mesh_pos()
    me = 4 * x + 2 * y + c
    n = len(srcs) - 1
    cps = [_scatter_copy(srcs[:n], lands[:n], ss, rs, a, k) for a in range(n) for k in range(3)]
    for f in range(1, 8):
        peer = ((x + (f >> 2)) % 2, (y + ((f >> 1) & 1)) % 2, (c + (f & 1)) % 2)
        cps.append(pltpu.make_async_remote_copy(
            src_ref=srcs[n], dst_ref=lands[n].at[me], send_sem=ss.at[3 * n - 1 + f], recv_sem=rs.at[3 * n - 1 + f],
            device_id=peer, device_id_type=MESH))
    return cps, cps


def _swap_with_sibling(parts, name):
    n = len(parts)

    def body(*refs):
        srcs, dsts = refs[:n], refs[n:2 * n]
        send_sems, recv_sems = refs[2 * n:]
        x, y, c = _mesh_pos()
        cps = [pltpu.make_async_remote_copy(src_ref=srcs[a], dst_ref=dsts[a], send_sem=send_sems.at[a],
                                            recv_sem=recv_sems.at[a], device_id=(x, y, 1 - c), device_id_type=MESH)
               for a in range(n)]
        for cp in cps:
            cp.start()
        for cp in cps:
            cp.wait_recv()
        for cp in cps:
            cp.wait_send()

    return pl.pallas_call(
        body, name=name, in_specs=[_HBM] * n, out_specs=[_HBM] * n,
        out_shape=[jax.ShapeDtypeStruct(p.shape, p.dtype) for p in parts],
        scratch_shapes=[pltpu.SemaphoreType.DMA((n,)), pltpu.SemaphoreType.DMA((n,))],
    )(*parts)


def _forward(x, tgt, wm, wf, mlp_w_fn, g1, bfp, wp, scale, g2, gf, dep):
    h, qkv, u, fl = _rms_inproj(x, g1, wm, wf, dep)
    qaug, kaug = _fox_cumsum(fl, bfp)
    attn, lse = _attn_fwd(qkv, qaug, kaug)
    pooled, pool = _pool_fwd(u, wp, scale)
    wo, wgt, wut, wd = mlp_w_fn(attn)
    x1, h2 = _outproj(x, attn, pool, wo, g2)
    loss, dgf, dx2, dx2b, ud, silu, a_b = _mlp_fwd_loss(h2, x1, wgt, wut, wd, tgt, gf)
    saved = dict(h=h, qkv=qkv, fl=fl, qaug=qaug, kaug=kaug, attn=attn, lse=lse, pooled=pooled, pool=pool, x1=x1, h2=h2,
                 ud=ud, silu=silu, a_b=a_b, wo=wo, wgt=wgt, wut=wut, wd=wd)
    return loss, dgf, dx2, dx2b, saved


def _backward_mlp(sv, dx2, dx2b, g2):
    dgate, dup, dx1, dx1b, dg2 = _mlp_bwd(dx2b, dx2, sv["ud"], sv["silu"], sv["wgt"], sv["wut"], sv["wd"], sv["x1"], g2)
    (dwd,) = _mm_tn(sv["a_b"], [dx2b], "dw_down", a_sharded=True, tk=T)
    (dwgt,) = _mm_tn(dgate, [sv["h2"]], "dw_gate", a_sharded=True, tk=T)
    (dwut,) = _mm_tn(dup, [sv["h2"]], "dw_up", a_sharded=True, tk=T)
    return dx1, dx1b, dg2, (dwgt, dwut, dwd)


def _backward_outproj_pool(sv, dx1b, wp, scale):
    dattn, dpool = _outproj_bwd(dx1b, sv["wo"])
    dwo_a, = _mm_tn(sv["attn"], [dx1b], "dw_out_attn", tk=2048)
    dwo_p, = _mm_tn(sv["pool"], [dx1b], "dw_out_pool", tk=2048)
    dwo = jnp.concatenate([dwo_a, dwo_p], axis=0).reshape(NSH, D // NSH, D)
    du, dscale, dwp = _pool_bwd(dpool, sv["pooled"], wp, scale)
    return dattn, dwo, du, dscale, dwp


def _backward_attn_inproj(sv, x, dx1, dattn, du, wm, wf, g1, bfp, dep):
    dq, dqs, dk, dks, dv = _attn_bwd(sv["qkv"], sv["qaug"], sv["kaug"], sv["attn"], dattn, sv["lse"], dep)
    df, dbf = _fox_cumsum_bwd(dqs, dks, sv["fl"], bfp)
    dx, dg1 = _inproj_bwd(dq, dk, dv, du, df, wm, wf, x, dx1, g1)
    dwq, dwk, dwv, dwu_in, dwf = _mm_tn_rows([dq, dk, dv, du, df], sv["h"], "dw_in")
    dwin = jnp.concatenate([dwq, dwk, dwv, dwf[0:8], dwu_in], axis=0)
    return dx, dg1, dbf, dwin.reshape(NSH, IN_S, D)


def kernel(x, norm1_g, w_in, b_forget, w_pool, pool_scale, w_out, norm2_g, w_gate, w_up, w_down, final_g, loss_target, m_norm1_g, m_w_in, m_b_forget, m_w_pool, m_pool_scale, m_w_out, m_norm2_g, m_w_gate, m_w_up, m_w_down, m_final_g, v_norm1_g, v_w_in, v_b_forget, v_w_pool, v_pool_scale, v_w_out, v_norm2_g, v_w_gate, v_w_up, v_w_down, v_final_g):
    mine = (2 * lax.axis_index("x") + lax.axis_index("y")).astype(jnp.int32)
    mine1 = mine.reshape(1)
    tr = lambda a: jnp.transpose(a[0])

    win4 = _all_gather_w_in(tr(w_in).astype(bf16))
    later = [w_out[0].astype(bf16), tr(w_gate).astype(bf16), tr(w_up).astype(bf16), w_down[0].astype(bf16)]
    lands = [lax.dynamic_update_slice(lax.empty((NSH,) + p.shape, bf16), p[None], (mine, 0, 0)) for p in later]
    ag_send, ag_recv, later_thru, lands_thru, ag_token = _split_start("all_gather_start", later, lands, 12, _gather_plan,
                                                                      win4)
    win = win4.reshape(IN_W, D)
    wm = jnp.concatenate([win[0:3 * AW], win[3 * AW + 8:]], axis=0)
    wf = jnp.pad(win[3 * AW:3 * AW + 8], ((0, 120), (0, 0)))
    bfp = jnp.pad(b_forget, ((0, 0), (0, 120)))
    wp = w_pool[0].astype(bf16)
    gf = final_g.reshape(1, D)

    def later_weights(after):
        _, (wo4, wgt, wut, wd) = _split_wait("all_gather_wait", ag_send, ag_recv, later_thru, lands_thru, after, _gather_plan)
        return wo4.reshape(D, D), wgt, wut, wd

    xe, tgt = x[0], loss_target[0]
    loss_v, dgf, dx2, dx2b, sv = _forward(xe, tgt, wm, wf, later_weights, norm1_g, bfp, wp, pool_scale, norm2_g, gf, ag_token)
    dx1, dx1b, dg2, mlp_grads = _backward_mlp(sv, dx2, dx2b, norm2_g)
    dattn, dwo, du, dscale, dwp = _backward_outproj_pool(sv, dx1b, wp, pool_scale)
    me = (4 * lax.axis_index("x") + 2 * lax.axis_index("y") + lax.axis_index("c")).astype(jnp.int32)
    dwp = dwp.reshape(512, 128)
    first = [dwo] + list(mlp_grads) + [dwp]
    first_lands = [lax.empty((3,) + g.shape[1:], bf16) for g in first[:4]]
    first_lands.append(lax.dynamic_update_slice(lax.empty((8, 512, 128), f32), dwp[None], (me, 0, 0)))
    rs_send, rs_recv, first_thru, first_lands_thru, rs_token = _split_start(
        "reduce_scatter_start", first, first_lands, 19, _scatter_and_spread_plan, du)
    dx, dg1, dbf, dwin = _backward_attn_inproj(sv, xe, dx1, dattn, du, wm, wf, norm1_g, bfp, rs_token)

    pad8 = lambda r: jnp.pad(r, ((0, 8 - r.shape[0]), (0, 0)))
    loss_rows = jnp.concatenate([dbf, jnp.zeros((6, 128), f32), loss_v[0:1, :]], axis=0)
    small = jnp.concatenate([dg1.reshape(8, 128), dg2.reshape(8, 128), dgf.reshape(8, 128), pad8(dscale.reshape(4, 128)),
                             loss_rows], axis=0)
    small_land = lax.dynamic_update_slice(lax.empty((8, SMALL_ROWS, 128), f32), small[None], (me, 0, 0))
    tail_send, tail_recv, tail_thru, tail_lands_thru, tail_token = _split_start(
        "tail_start", [dwin, small], [lax.empty((3,) + dwin.shape[1:], bf16), small_land], 10, _scatter_and_spread_plan,
        dx)
    first_thru, first_recv = _split_wait("reduce_scatter_wait", rs_send, rs_recv, first_thru, first_lands_thru, tail_token,
                                         _scatter_and_spread_plan)
    wp_all = first_recv[4]
    ws = [tr(w_in), w_out[0], tr(w_gate), tr(w_up), w_down[0]]
    ms = [tr(m_w_in), m_w_out[0], tr(m_w_gate), tr(m_w_up), m_w_down[0]]
    vs = [tr(v_w_in), v_w_out[0], tr(v_w_gate), tr(v_w_up), v_w_down[0]]
    partial = [_sum4(r, g, mine1, f"sum4_{i + 1}") for i, (r, g) in enumerate(zip(first_recv[:4], first_thru[:4]))]
    other = _swap_with_sibling(partial, "swap_first")
    big = [_adamw_shard(ws[i + 1], ms[i + 1], vs[i + 1], partial[i], other[i], f"adamw_{i + 1}") for i in range(4)]
    (dwin_thru, _), (in_recv_land, small_all) = _split_wait("tail_wait", tail_send, tail_recv, tail_thru, tail_lands_thru,
                                                            big[3][0], _scatter_and_spread_plan)
    partial_in = _sum4(in_recv_land, dwin_thru, mine1, "sum4_0")
    (other_in,) = _swap_with_sibling([partial_in], "swap_in")
    big = [_adamw_shard(ws[0], ms[0], vs[0], partial_in, other_in, "adamw_0")] + big

    small_names = ["norm1_g", "norm2_g", "final_g", "pool_scale", "b_forget", "w_pool"]
    rows = lambda a, b, c, d, e, f: [a.reshape(8, 128), b.reshape(8, 128), c.reshape(8, 128), d.reshape(4, 128),
                                     e.reshape(1, 8), f.reshape(512, 128)]
    sm, loss_row = _adamw_small(rows(norm1_g, norm2_g, final_g, pool_scale, b_forget, w_pool),
                                rows(m_norm1_g, m_norm2_g, m_final_g, m_pool_scale, m_b_forget, m_w_pool),
                                rows(v_norm1_g, v_norm2_g, v_final_g, v_pool_scale, v_b_forget, v_w_pool), small_all, wp_all)
    small_shape = dict(norm1_g=(1, D), norm2_g=(1, D), final_g=(D,), pool_scale=(1, AW), b_forget=(1, 8),
                       w_pool=(1, 4, 128, 128))

    order = ["norm1_g", "w_in", "b_forget", "w_pool", "pool_scale", "w_out", "norm2_g", "w_gate", "w_up", "w_down", "final_g"]
    big_idx = {"w_in": 0, "w_out": 1, "w_gate": 2, "w_up": 3, "w_down": 4}
    outs = [loss_row[0, 0], dx[None]]
    for kind in range(4):
        for name in order:
            if name in ("w_in", "w_gate", "w_up"):
                outs.append(jnp.transpose(big[big_idx[name]][kind])[None])
            elif name in big_idx:
                outs.append(big[big_idx[name]][kind][None])
            else:
                outs.append(sm[6 * kind + small_names.index(name)].reshape(small_shape[name]))
    return tuple(outs)
```

```python
import jax
import jax.numpy as jnp
import numpy as np
from jax import lax
from jax.experimental import pallas as pl
from jax.experimental.pallas import tpu as pltpu

f32 = jnp.float32
bf16 = jnp.bfloat16

T = 4096
D = 1024
NSH = 4
IN_W = 2056
IN_S = IN_W // NSH
AW = 512
PAIRS = 4
SPARE = (64, 0)
ROW_SUM_LANE, COL_SUM_LANE = 0, 3
FF = 2816
FS = FF // NSH
WINDOWS = (2, 4, 8, 16)
HALO = 16
EPS = 1e-6
NEG = -1e30
LR, B1, B2, AEPS, WD, STEP = 0.001, 0.9, 0.999, 1e-08, 0.01, 10
SMALL_ROWS = 40

NT = (((1,), (1,)), ((), ()))
TN = (((0,), (0,)), ((), ()))

MESH = pl.DeviceIdType.MESH


def _cp(*sem):
    return pltpu.CompilerParams(dimension_semantics=sem)


def _full(shape):
    n = len(shape)
    return pl.BlockSpec(shape, lambda *_: (0,) * n)


def _resident(shape):
    n = len(shape)
    return pl.BlockSpec(shape, lambda *_: (0,) * n, pipeline_mode=pl.Buffered(1))


def _rms_inproj(x, g1, wm, wf, dep):
    tm = 512

    def body(x_ref, g_ref, wm_ref, wf_ref, dep_ref, h_ref, qkv_ref, u_ref, fl_ref):
        xv = x_ref[...]
        r = lax.rsqrt(jnp.mean(xv * xv, axis=-1, keepdims=True) + EPS)
        h = (xv * r * g_ref[...]).astype(bf16)
        h_ref[...] = h
        qkv_ref[...] = lax.dot_general(h, wm_ref[0:3 * AW, :], NT, preferred_element_type=f32).astype(bf16)
        u_ref[...] = lax.dot_general(h, wm_ref[3 * AW:4 * AW, :], NT, preferred_element_type=f32)
        fl_ref[...] = lax.dot_general(h, wf_ref[...], NT, preferred_element_type=f32)

    return pl.pallas_call(
        body, name="rms_inproj", grid=(T // tm,),
        in_specs=[pl.BlockSpec((tm, D), lambda i: (i, 0)), _full((1, D)), _full((4 * AW, D)), _full((128, D)),
                  _full((8, 128))],
        out_specs=[pl.BlockSpec((tm, D), lambda i: (i, 0)), pl.BlockSpec((tm, 3 * AW), lambda i: (i, 0)),
                   pl.BlockSpec((tm, AW), lambda i: (i, 0)), pl.BlockSpec((tm, 128), lambda i: (i, 0))],
        out_shape=[jax.ShapeDtypeStruct((T, D), bf16), jax.ShapeDtypeStruct((T, 3 * AW), bf16),
                   jax.ShapeDtypeStruct((T, AW), f32), jax.ShapeDtypeStruct((T, 128), f32)],
        compiler_params=_cp("parallel"),
    )(x, g1, wm, wf, dep)


CUMSUM_ROWS = 512
FS_CHUNKS = ((0, 256), (256, 512), (512, FS))


def _log_sigmoid(z):
    return jnp.minimum(z, 0.0) - jnp.log(1.0 + jnp.exp(-jnp.abs(z)))


def _split3(x):
    hi = x.astype(bf16)
    r1 = x - hi.astype(f32)
    mid = r1.astype(bf16)
    return hi, mid, (r1 - mid.astype(f32)).astype(bf16)


def _dot01(sel, x, sel_first):
    parts = _split3(x)
    if sel_first:
        return sum(jnp.dot(sel, p, preferred_element_type=f32) for p in parts)
    return sum(jnp.dot(p, sel, preferred_element_type=f32) for p in parts)


def _fox_cumsum(fl, bfp):
    tb = CUMSUM_ROWS
    nb = T // tb

    def body(fl_ref, b_ref, qa_ref, ka_ref, carry):
        i = pl.program_id(0)

        @pl.when(i == 0)
        def _():
            carry[...] = jnp.zeros_like(carry)

        lf = _log_sigmoid(fl_ref[...] + b_ref[...])
        r = lax.broadcasted_iota(jnp.int32, (tb, tb), 0)
        cc = lax.broadcasted_iota(jnp.int32, (tb, tb), 1)
        ltri = (cc <= r).astype(bf16)
        cb = _dot01(ltri, lf, True) + carry[0:1, :]
        carry[...] = jnp.broadcast_to(cb[tb - 1:tb, :], (8, 128))
        hi, mid, lo = _split3(cb)
        head = lax.broadcasted_iota(jnp.int32, (128, AW), 0)
        col = lax.broadcasted_iota(jnp.int32, (128, AW), 1)
        base = 128 * (head >> 1) + jnp.where((head & 1) == 0, SPARE[0], SPARE[1])
        place = lambda off: jnp.logical_and(col == base + off, head < 8).astype(bf16)
        mm = lambda a, off: jnp.dot(a, place(off), preferred_element_type=f32)
        cq = mm(hi, 0) + mm(mid, 1) + mm(lo, 2)
        ck = mm(hi, 3) + mm(mid, 4) + mm(lo, 5)
        within = jnp.bitwise_and(lax.broadcasted_iota(jnp.int32, (tb, AW), 1), 63)
        qa_ref[...] = jnp.where(jnp.logical_and(within >= 3, within <= 5), 1.0, cq).astype(bf16)
        ka_ref[...] = jnp.where(within <= 2, 1.0, -ck).astype(bf16)

    return pl.pallas_call(
        body, name="fox_cumsum", grid=(nb,),
        in_specs=[pl.BlockSpec((tb, 128), lambda i: (i, 0)), _full((1, 128))],
        out_specs=[pl.BlockSpec((tb, AW), lambda i: (i, 0)), pl.BlockSpec((tb, AW), lambda i: (i, 0))],
        out_shape=[jax.ShapeDtypeStruct((T, AW), bf16), jax.ShapeDtypeStruct((T, AW), bf16)],
        scratch_shapes=[pltpu.VMEM((8, 128), f32)],
        compiler_params=_cp("arbitrary"),
    )(fl, bfp)


ATT_T = 512


def _causal_steps(key_major):
    n = T // ATT_T
    if key_major:
        pairs = [(i, j) for j in range(n) for i in range(j, n)]
    else:
        pairs = [(i, j) for i in range(n) for j in range(i + 1)]
    it = np.array([p[0] for p in pairs], np.int32)
    jt = np.array([p[1] for p in pairs], np.int32)
    return jnp.asarray(it), jnp.asarray(jt)


def _row_blocks(tq, tk, on_diagonal):
    return ((0, tq // 2, tk // 2), (tq // 2, tq, tk)) if on_diagonal else ((0, tq, tk),)


def _attn_fwd(qkv, qaug, kaug):
    tq = tk = ATT_T
    it, jt = _causal_steps(False)
    nsteps = it.shape[0]

    rs = 64

    def body(it_ref, jt_ref, q_ref, k_ref, v_ref, qa_ref, ka_ref, o_ref, lse_ref, m_sc, acc_sc, s_sc, p_sc, alpha_sc):
        t = pl.program_id(1)
        i = it_ref[t]
        j = jt_ref[t]

        @pl.when(j == 0)
        def _():
            m_sc[...] = jnp.full_like(m_sc, NEG)
            acc_sc[...] = jnp.zeros_like(acc_sc)

        lane = lax.broadcasted_iota(jnp.int32, (tq, 128), 1)
        spare = SPARE

        def step(on_diagonal):
            q = q_ref[...] * 0.125
            k = k_ref[...]
            v = v_ref[...]
            qa = qa_ref[...]
            ka = ka_ref[...]
            blocks = _row_blocks(tq, tk, on_diagonal)
            for e in range(2):
                hm = (lane >= 64) if e else (lane < 64)
                qe = jnp.where(hm, q, qa)
                ke = jnp.where(hm, k, ka)
                for r0, r1, nc in blocks:
                    s_sc[e, r0:r1, 0:nc] = lax.dot_general(qe[r0:r1], ke[0:nc], NT, preferred_element_type=f32)
            for e in range(2):
                for r0, r1, nc in blocks:
                    for r in range(r0, r1, rs):
                        s = s_sc[e, r:r + rs, 0:nc]
                        if on_diagonal:
                            row = lax.broadcasted_iota(jnp.int32, (rs, nc), 0) + r
                            col = lax.broadcasted_iota(jnp.int32, (rs, nc), 1)
                            s = jnp.where(col <= row, s, NEG)
                        m_prev = m_sc[e, r:r + rs, :]
                        m_new = jnp.maximum(m_prev, jnp.max(s, axis=1, keepdims=True))
                        p_sc[e, r:r + rs, 0:nc] = jnp.exp(s - jnp.tile(m_new, (1, nc // 128))).astype(bf16)
                        alpha_sc[e, r:r + rs, :] = jnp.exp(m_prev - m_new)
                        m_sc[e, r:r + rs, :] = m_new
            for e in range(2):
                hm = (lane >= 64) if e else (lane < 64)
                ve = jnp.where(hm, v, (lane == spare[e]).astype(bf16))
                for r0, r1, nc in blocks:
                    acc_sc[e, r0:r1] = (alpha_sc[e, r0:r1] * acc_sc[e, r0:r1]
                                        + jnp.dot(p_sc[e, r0:r1, 0:nc], ve[0:nc], preferred_element_type=f32))

        @pl.when(j < i)
        def _():
            step(False)

        @pl.when(j == i)
        def _():
            step(True)
            l0 = acc_sc[0][:, spare[0]:spare[0] + 1]
            l1 = acc_sc[1][:, spare[1]:spare[1] + 1]
            o_ref[...] = jnp.where(lane < 64, acc_sc[0] / l0, acc_sc[1] / l1).astype(bf16)
            lse_ref[...] = jnp.where(lane < 64, m_sc[0] + jnp.log(l0), m_sc[1] + jnp.log(l1))

    qmap = lambda p, t, it, jt: (it[t], p)
    kmap = lambda p, t, it, jt: (jt[t], p)
    grid_spec = pltpu.PrefetchScalarGridSpec(
        num_scalar_prefetch=2, grid=(PAIRS, nsteps),
        in_specs=[pl.BlockSpec((tq, 128), qmap),
                  pl.BlockSpec((tk, 128), lambda p, t, it, jt: (jt[t], PAIRS + p)),
                  pl.BlockSpec((tk, 128), lambda p, t, it, jt: (jt[t], 2 * PAIRS + p)),
                  pl.BlockSpec((tq, 128), qmap), pl.BlockSpec((tk, 128), kmap)],
        out_specs=[pl.BlockSpec((tq, 128), qmap),
                   pl.BlockSpec((None, tq, 128), lambda p, t, it, jt: (p, it[t], 0))],
        scratch_shapes=[pltpu.VMEM((2, tq, 128), f32), pltpu.VMEM((2, tq, 128), f32), pltpu.VMEM((2, tq, tk), f32),
                        pltpu.VMEM((2, tq, tk), bf16), pltpu.VMEM((2, tq, 128), f32)],
    )
    return pl.pallas_call(
        body, name="fox_attn_fwd", grid_spec=grid_spec,
        out_shape=[jax.ShapeDtypeStruct((T, AW), bf16), jax.ShapeDtypeStruct((PAIRS, T, 128), f32)],
        compiler_params=_cp("parallel", "arbitrary"),
    )(it, jt, qkv, qkv, qkv, qaug, kaug)


def _pool_fwd(u, wp, scale):
    tm = 1024

    def body(u_ref, wp_ref, sc_ref, pooled_ref, pool_ref, ext):
        i = pl.program_id(0)

        @pl.when(i == 0)
        def _():
            ext[0:HALO, :] = jnp.zeros((HALO, AW), f32)

        uv = u_ref[...]
        ext[HALO:HALO + tm, :] = uv
        t_idx = i * tm + lax.broadcasted_iota(jnp.int32, (tm, 1), 0)
        for g, w in enumerate(WINDOWS):
            lo, hi = 128 * g, 128 * (g + 1)
            ug = uv[:, lo:hi]
            acc = ug
            for d in range(1, w):
                acc = acc + ext[HALO - d:HALO - d + tm, lo:hi]
            cnt = jnp.minimum(t_idx + 1, w).astype(f32)
            pb = (acc / cnt - ug).astype(bf16)
            pooled_ref[:, lo:hi] = pb
            mixed = jnp.dot(pb, wp_ref[g], preferred_element_type=f32)
            pool_ref[:, lo:hi] = (mixed * sc_ref[:, lo:hi]).astype(bf16)
        ext[0:HALO, :] = uv[tm - HALO:tm, :]

    return pl.pallas_call(
        body, name="pool_fwd", grid=(T // tm,),
        in_specs=[pl.BlockSpec((tm, AW), lambda i: (i, 0)), _full((4, 128, 128)), _full((1, AW))],
        out_specs=[pl.BlockSpec((tm, AW), lambda i: (i, 0)), pl.BlockSpec((tm, AW), lambda i: (i, 0))],
        out_shape=[jax.ShapeDtypeStruct((T, AW), bf16), jax.ShapeDtypeStruct((T, AW), bf16)],
        scratch_shapes=[pltpu.VMEM((tm + HALO, AW), f32)],
        compiler_params=_cp("arbitrary"),
    )(u, wp, scale)


def _outproj(x, attn, pool, wo, g2):
    tm = 1024

    def body(x_ref, a_ref, p_ref, wo_ref, g_ref, x1_ref, h2_ref):
        x1 = x_ref[...] + jnp.dot(a_ref[...], wo_ref[0:AW, :], preferred_element_type=f32)
        x1 = x1 + jnp.dot(p_ref[...], wo_ref[AW:2 * AW, :], preferred_element_type=f32)
        x1_ref[...] = x1
        r = lax.rsqrt(jnp.mean(x1 * x1, axis=-1, keepdims=True) + EPS)
        h2_ref[...] = (x1 * r * g_ref[...]).astype(bf16)

    return pl.pallas_call(
        body, name="outproj", grid=(T // tm,),
        in_specs=[pl.BlockSpec((tm, D), lambda i: (i, 0)), pl.BlockSpec((tm, AW), lambda i: (i, 0)),
                  pl.BlockSpec((tm, AW), lambda i: (i, 0)), _full((D, D)), _full((1, D))],
        out_specs=[pl.BlockSpec((tm, D), lambda i: (i, 0)), pl.BlockSpec((tm, D), lambda i: (i, 0))],
        out_shape=[jax.ShapeDtypeStruct((T, D), f32), jax.ShapeDtypeStruct((T, D), bf16)],
        compiler_params=_cp("parallel"),
    )(x, attn, pool, wo, g2)


def _mlp_fwd_loss(h2, x1, wg, wu, wd, tgt, gf):
    tm = 512

    def body(h_ref, x1_ref, wg_ref, wu_ref, wd_ref, t_ref, g_ref,
             loss_ref, dg_ref, dx_ref, dxb_ref, ud_ref, silu_ref, a_ref, x2):
        i = pl.program_id(0)
        s = pl.program_id(1)

        @pl.when(jnp.logical_and(i == 0, s == 0))
        def _():
            loss_ref[...] = jnp.zeros_like(loss_ref)
            dg_ref[...] = jnp.zeros_like(dg_ref)

        h = h_ref[...]
        gus = [(lax.dot_general(h, wg_ref[s, c0:c1, :], NT, preferred_element_type=f32),
                lax.dot_general(h, wu_ref[s, c0:c1, :], NT, preferred_element_type=f32)) for c0, c1 in FS_CHUNKS]
        for (c0, c1), (gate, up) in zip(FS_CHUNKS, gus):
            sg = jax.nn.sigmoid(gate)
            silu = gate * sg
            ud_ref[:, c0:c1] = (up * (sg * (1.0 + gate * (1.0 - sg)))).astype(bf16)
            silu_ref[:, c0:c1] = silu.astype(bf16)
            a_ref[:, c0:c1] = (silu * up).astype(bf16)
        part = jnp.dot(a_ref[...], wd_ref[s], preferred_element_type=f32)

        @pl.when(s == 0)
        def _():
            x2[...] = x1_ref[...] + part

        @pl.when(s > 0)
        def _():
            x2[...] += part

        @pl.when(s == NSH - 1)
        def _():
            xv = x2[...]
            g = g_ref[...]
            r = lax.rsqrt(jnp.mean(xv * xv, axis=-1, keepdims=True) + EPS)
            xhat = xv * r
            e = xhat * g - t_ref[...]
            loss_ref[...] += 0.5 * jnp.sum(jnp.mean(e * e, axis=-1, keepdims=True))
            dy = e * (1.0 / D)
            dg_ref[...] += jnp.sum(dy * xhat, axis=0, keepdims=True)
            z = dy * g
            dx = r * (z - xhat * jnp.mean(z * xhat, axis=-1, keepdims=True))
            dx_ref[...] = dx
            dxb_ref[...] = dx.astype(bf16)

    row = lambda i, s: (i, 0)
    sl = lambda i, s: (s, i, 0)
    wsl = lambda i, s: (s, 0, 0)
    return pl.pallas_call(
        body, name="mlp_fwd_loss", grid=(T // tm, NSH),
        in_specs=[pl.BlockSpec((tm, D), row), pl.BlockSpec((tm, D), row),
                  _resident((NSH, FS, D)), _resident((NSH, FS, D)), _resident((NSH, FS, D)),
                  pl.BlockSpec((tm, D), row), pl.BlockSpec((1, D), lambda i, s: (0, 0))],
        out_specs=[pl.BlockSpec((8, 128), lambda i, s: (0, 0)), pl.BlockSpec((1, D), lambda i, s: (0, 0)),
                   pl.BlockSpec((tm, D), row), pl.BlockSpec((tm, D), row),
                   pl.BlockSpec((None, tm, FS), sl), pl.BlockSpec((None, tm, FS), sl), pl.BlockSpec((None, tm, FS), sl)],
        out_shape=[jax.ShapeDtypeStruct((8, 128), f32), jax.ShapeDtypeStruct((1, D), f32),
                   jax.ShapeDtypeStruct((T, D), f32), jax.ShapeDtypeStruct((T, D), bf16)]
        + [jax.ShapeDtypeStruct((NSH, T, FS), bf16)] * 3,
        scratch_shapes=[pltpu.VMEM((tm, D), f32)],
        compiler_params=_cp("arbitrary", "arbitrary"),
    )(h2, x1, wg, wu, wd, tgt, gf)


def _mlp_bwd(dx2b, dx2, ud, silu, wg, wu, wd, x1, g2):
    tm = 512

    def body(dxb_ref, dx_ref, ud_ref, silu_ref, wg_ref, wu_ref, wd_ref, x1_ref, g_ref,
             dg_ref, du_ref, dx1_ref, dx1b_ref, dn_ref, acc):
        i = pl.program_id(0)
        s = pl.program_id(1)

        @pl.when(jnp.logical_and(i == 0, s == 0))
        def _():
            dn_ref[...] = jnp.zeros_like(dn_ref)

        dxb = dxb_ref[...]
        das = [lax.dot_general(dxb, wd_ref[s, c0:c1, :], NT, preferred_element_type=f32) for c0, c1 in FS_CHUNKS]
        for (c0, c1), da in zip(FS_CHUNKS, das):
            dg_ref[:, c0:c1] = (da * ud_ref[:, c0:c1].astype(f32)).astype(bf16)
            du_ref[:, c0:c1] = (da * silu_ref[:, c0:c1].astype(f32)).astype(bf16)
        part = jnp.dot(dg_ref[...], wg_ref[s], preferred_element_type=f32)
        part = part + jnp.dot(du_ref[...], wu_ref[s], preferred_element_type=f32)

        @pl.when(s == 0)
        def _():
            acc[...] = part

        @pl.when(s > 0)
        def _():
            acc[...] += part

        @pl.when(s == NSH - 1)
        def _():
            xv = x1_ref[...]
            r = lax.rsqrt(jnp.mean(xv * xv, axis=-1, keepdims=True) + EPS)
            xhat = xv * r
            dh = acc[...]
            dn_ref[...] += jnp.sum(dh * xhat, axis=0, keepdims=True)
            z = dh * g_ref[...]
            dx1 = dx_ref[...] + r * (z - xhat * jnp.mean(z * xhat, axis=-1, keepdims=True))
            dx1_ref[...] = dx1
            dx1b_ref[...] = dx1.astype(bf16)

    row = lambda i, s: (i, 0)
    sl = lambda i, s: (s, i, 0)
    wsl = lambda i, s: (s, 0, 0)
    return pl.pallas_call(
        body, name="mlp_bwd", grid=(T // tm, NSH),
        in_specs=[pl.BlockSpec((tm, D), row), pl.BlockSpec((tm, D), row),
                  pl.BlockSpec((None, tm, FS), sl), pl.BlockSpec((None, tm, FS), sl),
                  _resident((NSH, FS, D)), _resident((NSH, FS, D)), _resident((NSH, FS, D)),
                  pl.BlockSpec((tm, D), row), pl.BlockSpec((1, D), lambda i, s: (0, 0))],
        out_specs=[pl.BlockSpec((None, tm, FS), sl), pl.BlockSpec((None, tm, FS), sl),
                   pl.BlockSpec((tm, D), row), pl.BlockSpec((tm, D), row), pl.BlockSpec((1, D), lambda i, s: (0, 0))],
        out_shape=[jax.ShapeDtypeStruct((NSH, T, FS), bf16)] * 2
        + [jax.ShapeDtypeStruct((T, D), f32), jax.ShapeDtypeStruct((T, D), bf16), jax.ShapeDtypeStruct((1, D), f32)],
        scratch_shapes=[pltpu.VMEM((tm, D), f32)],
        compiler_params=_cp("arbitrary", "arbitrary"),
    )(dx2b, dx2, ud, silu, wg, wu, wd, x1, g2)


def _mm_tn(a, bs, name, a_sharded=False, b_sharded=False, tk=512, out_dtype=bf16):
    nb = len(bs)
    sh = NSH if (a_sharded or b_sharded) else 1
    m = a.shape[-1]
    nk = T // tk

    def body(a_ref, *refs):
        kk = pl.program_id(1)
        av = a_ref[...]
        for b_ref, o_ref, acc in zip(refs[:nb], refs[nb:2 * nb], refs[2 * nb:]):
            upd = lax.dot_general(av, b_ref[...], TN, preferred_element_type=f32)

            @pl.when(kk == 0)
            def _():
                acc[...] = upd

            @pl.when(kk > 0)
            def _():
                acc[...] += upd

            @pl.when(kk == nk - 1)
            def _():
                o_ref[...] = acc[...].astype(out_dtype)

    a_spec = (pl.BlockSpec((None, tk, m), lambda s, k: (s, k, 0)) if a_sharded
              else pl.BlockSpec((tk, m), lambda s, k: (k, 0)))
    b_specs, o_specs, o_shapes, scratch = [], [], [], []
    for b in bs:
        n = b.shape[-1]
        b_specs.append(pl.BlockSpec((None, tk, n), lambda s, k: (s, k, 0)) if b_sharded
                       else pl.BlockSpec((tk, n), lambda s, k: (k, 0)))
        scratch.append(pltpu.VMEM((m, n), f32))
        if sh > 1:
            o_specs.append(pl.BlockSpec((None, m, n), lambda s, k: (s, 0, 0)))
            o_shapes.append(jax.ShapeDtypeStruct((sh, m, n), out_dtype))
        else:
            o_specs.append(pl.BlockSpec((m, n), lambda s, k: (0, 0)))
            o_shapes.append(jax.ShapeDtypeStruct((m, n), out_dtype))
    return pl.pallas_call(
        body, name=name, grid=(sh, nk), in_specs=[a_spec] + b_specs, out_specs=o_specs, out_shape=o_shapes,
        scratch_shapes=scratch, compiler_params=_cp("arbitrary", "arbitrary"),
    )(a, *bs)


def _mm_tn_rows(a_list, b, name, tk=1024, out_dtype=bf16):
    na = len(a_list)
    n = b.shape[-1]
    nk = T // tk

    def body(*refs):
        a_refs, b_ref = refs[:na], refs[na]
        o_refs, accs = refs[na + 1:2 * na + 1], refs[2 * na + 1:]
        kk = pl.program_id(0)
        bv = b_ref[...]
        for a_ref, o_ref, acc in zip(a_refs, o_refs, accs):
            upd = lax.dot_general(a_ref[...], bv, TN, preferred_element_type=f32)

            @pl.when(kk == 0)
            def _():
                acc[...] = upd

            @pl.when(kk > 0)
            def _():
                acc[...] += upd

            @pl.when(kk == nk - 1)
            def _():
                o_ref[...] = acc[...].astype(out_dtype)

    return pl.pallas_call(
        body, name=name, grid=(nk,),
        in_specs=[pl.BlockSpec((tk, a.shape[-1]), lambda k: (k, 0)) for a in a_list] + [pl.BlockSpec((tk, n), lambda k: (k, 0))],
        out_specs=[pl.BlockSpec((a.shape[-1], n), lambda k: (0, 0)) for a in a_list],
        out_shape=[jax.ShapeDtypeStruct((a.shape[-1], n), out_dtype) for a in a_list],
        scratch_shapes=[pltpu.VMEM((a.shape[-1], n), f32) for a in a_list],
        compiler_params=_cp("arbitrary"),
    )(*a_list, b)


def _outproj_bwd(dx1b, wo):
    tm = 1024

    def body(dx_ref, wo_ref, da_ref, dp_ref):
        dx = dx_ref[...]
        da_ref[...] = lax.dot_general(dx, wo_ref[0:AW, :], NT, preferred_element_type=f32).astype(bf16)
        dp_ref[...] = lax.dot_general(dx, wo_ref[AW:2 * AW, :], NT, preferred_element_type=f32)

    return pl.pallas_call(
        body, name="outproj_bwd", grid=(T // tm,),
        in_specs=[pl.BlockSpec((tm, D), lambda i: (i, 0)), _full((D, D))],
        out_specs=[pl.BlockSpec((tm, AW), lambda i: (i, 0)), pl.BlockSpec((tm, AW), lambda i: (i, 0))],
        out_shape=[jax.ShapeDtypeStruct((T, AW), bf16), jax.ShapeDtypeStruct((T, AW), f32)],
        compiler_params=_cp("parallel"),
    )(dx1b, wo)


def _pool_bwd(dpool, pooled, wp, scale):
    tm = 1024
    n = T // tm

    def body(dp_ref, pb_ref, wp_ref, sc_ref, du_ref, dsc_ref, dwp_ref, ext):
        i = pl.program_id(0)

        @pl.when(i == 0)
        def _():
            ext[tm:tm + HALO, :] = jnp.zeros((HALO, AW), f32)
            dsc_ref[...] = jnp.zeros_like(dsc_ref)
            dwp_ref[...] = jnp.zeros_like(dwp_ref)

        t_idx = (n - 1 - i) * tm + lax.broadcasted_iota(jnp.int32, (tm, 1), 0)
        for g, w in enumerate(WINDOWS):
            lo, hi = 128 * g, 128 * (g + 1)
            pb = pb_ref[:, lo:hi]
            mixed = jnp.dot(pb, wp_ref[g], preferred_element_type=f32)
            dpo = dp_ref[:, lo:hi]
            dsc_ref[:, lo:hi] += jnp.sum(dpo * mixed, axis=0, keepdims=True)
            dmr = (dpo * sc_ref[:, lo:hi]).astype(bf16)
            dwp_ref[g] += lax.dot_general(pb, dmr, TN, preferred_element_type=f32)
            dpl = lax.dot_general(dmr, wp_ref[g], NT, preferred_element_type=f32)
            cnt = jnp.minimum(t_idx + 1, w).astype(f32)
            dpn = dpl / cnt
            ext[0:tm, lo:hi] = dpn
            acc = dpn
            for d in range(1, w):
                acc = acc + ext[d:d + tm, lo:hi]
            du_ref[:, lo:hi] = (acc - dpl).astype(bf16)
        ext[tm:tm + HALO, :] = ext[0:HALO, :]

    rev = lambda i: (n - 1 - i, 0)
    return pl.pallas_call(
        body, name="pool_bwd", grid=(n,),
        in_specs=[pl.BlockSpec((tm, AW), rev), pl.BlockSpec((tm, AW), rev), _full((4, 128, 128)), _full((1, AW))],
        out_specs=[pl.BlockSpec((tm, AW), rev), _full((1, AW)), _full((4, 128, 128))],
        out_shape=[jax.ShapeDtypeStruct((T, AW), bf16), jax.ShapeDtypeStruct((1, AW), f32),
                   jax.ShapeDtypeStruct((4, 128, 128), f32)],
        scratch_shapes=[pltpu.VMEM((tm + HALO, AW), f32)],
        compiler_params=_cp("arbitrary"),
    )(dpool, pooled, wp, scale)


def _attn_bwd(qkv, qaug, kaug, attn, dattn, lse, dep):
    tq = tk = ATT_T
    n = T // tq
    it, jt = _causal_steps(True)
    nsteps = it.shape[0]

    rs = 64

    def body(it_ref, jt_ref, q_ref, k_ref, v_ref, qa_ref, ka_ref, o_ref, do_ref, lse_ref, dep_ref,
             dq_ref, dqs_ref, dk_ref, dks_ref, dv_ref, dq_acc, dk_acc, dv_acc, s_sc, dp_sc, p_sc, ds_sc):
        t = pl.program_id(1)
        i = it_ref[t]
        j = jt_ref[t]

        @pl.when(t == 0)
        def _():
            dq_acc[...] = jnp.zeros_like(dq_acc)

        @pl.when(i == j)
        def _():
            dk_acc[...] = jnp.zeros_like(dk_acc)
            dv_acc[...] = jnp.zeros_like(dv_acc)

        lane = lax.broadcasted_iota(jnp.int32, (tq, 128), 1)

        def step(on_diagonal):
            q = q_ref[...] * 0.125
            k = k_ref[...]
            v = v_ref[...]
            qa = qa_ref[...]
            ka = ka_ref[...]
            do = do_ref[...]
            dd = do.astype(f32) * o_ref[...].astype(f32)
            blocks = _row_blocks(tq, tk, on_diagonal)
            qes, kes, does, deltas = [], [], [], []
            for e in range(2):
                hm = (lane >= 64) if e else (lane < 64)
                qes.append(jnp.where(hm, q, qa))
                kes.append(jnp.where(hm, k, ka))
                does.append(jnp.where(hm, do, jnp.zeros_like(do)))
                deltas.append(jnp.sum(jnp.where(hm, dd, 0.0), axis=1, keepdims=True))
                for r0, r1, nc in blocks:
                    s_sc[e, r0:r1, 0:nc] = lax.dot_general(qes[e][r0:r1], kes[e][0:nc], NT, preferred_element_type=f32)
                    dp_sc[e, r0:r1, 0:nc] = lax.dot_general(does[e][r0:r1], v[0:nc], NT, preferred_element_type=f32)
            for e in range(2):
                for r0, r1, nc in blocks:
                    for r in range(r0, r1, rs):
                        s = s_sc[e, r:r + rs, 0:nc] - lse_ref[r:r + rs, 64 * e:64 * e + 1]
                        if on_diagonal:
                            row = lax.broadcasted_iota(jnp.int32, (rs, nc), 0) + r
                            col = lax.broadcasted_iota(jnp.int32, (rs, nc), 1)
                            s = jnp.where(col <= row, s, NEG)
                        p = jnp.exp(s)
                        p_sc[e, r:r + rs, 0:nc] = p.astype(bf16)
                        ds_sc[e, r:r + rs, 0:nc] = (p * (dp_sc[e, r:r + rs, 0:nc] - deltas[e][r:r + rs, :])).astype(bf16)
                for r0, r1, nc in blocks:
                    dv_acc[:, 0:nc] += lax.dot_general(does[e][r0:r1], p_sc[e, r0:r1, 0:nc], TN, preferred_element_type=f32)
                    dsb = ds_sc[e, r0:r1, 0:nc]
                    dk_acc[e, :, 0:nc] += lax.dot_general(qes[e][r0:r1], dsb, TN, preferred_element_type=f32)
                    rq = pl.multiple_of(i * tq + r0, r1 - r0)
                    dq_acc[e, pl.ds(rq, r1 - r0), :] += jnp.dot(dsb, kes[e][0:nc], preferred_element_type=f32)

        @pl.when(i > j)
        def _():
            step(False)

        @pl.when(i == j)
        def _():
            step(True)

        @pl.when(i == n - 1)
        def _():
            dk0 = dk_acc[0].T
            dk1 = dk_acc[1].T
            dk_ref[...] = jnp.where(lane < 64, dk0, dk1).astype(bf16)
            dks_ref[...] = jnp.where(lane < 64, dk1, dk0)
            dv_ref[...] = dv_acc[...].T.astype(bf16)

        @pl.when(t == nsteps - 1)
        def _():
            lane_t = lax.broadcasted_iota(jnp.int32, (T, 128), 1)
            dq_ref[...] = (jnp.where(lane_t < 64, dq_acc[0], dq_acc[1]) * 0.125).astype(bf16)
            dqs_ref[...] = jnp.where(lane_t < 64, dq_acc[1], dq_acc[0])

    qmap = lambda p, t, it, jt: (it[t], p)
    grid_spec = pltpu.PrefetchScalarGridSpec(
        num_scalar_prefetch=2, grid=(PAIRS, nsteps),
        in_specs=[pl.BlockSpec((tq, 128), qmap),
                  pl.BlockSpec((tk, 128), lambda p, t, it, jt: (jt[t], PAIRS + p)),
                  pl.BlockSpec((tk, 128), lambda p, t, it, jt: (jt[t], 2 * PAIRS + p)),
                  pl.BlockSpec((tq, 128), qmap), pl.BlockSpec((tk, 128), lambda p, t, it, jt: (jt[t], p)),
                  pl.BlockSpec((tq, 128), qmap), pl.BlockSpec((tq, 128), qmap),
                  pl.BlockSpec((None, tq, 128), lambda p, t, it, jt: (p, it[t], 0)),
                  pl.BlockSpec((8, 128), lambda p, t, it, jt: (0, 0))],
        out_specs=[pl.BlockSpec((T, 128), lambda p, t, it, jt: (0, p)),
                   pl.BlockSpec((None, T, 128), lambda p, t, it, jt: (p, 0, 0)),
                   pl.BlockSpec((tk, 128), lambda p, t, it, jt: (jt[t], p)),
                   pl.BlockSpec((None, tk, 128), lambda p, t, it, jt: (p, jt[t], 0)),
                   pl.BlockSpec((tk, 128), lambda p, t, it, jt: (jt[t], p))],
        scratch_shapes=[pltpu.VMEM((2, T, 128), f32), pltpu.VMEM((2, 128, tk), f32), pltpu.VMEM((128, tk), f32),
                        pltpu.VMEM((2, tq, tk), f32), pltpu.VMEM((2, tq, tk), f32), pltpu.VMEM((2, tq, tk), bf16),
                        pltpu.VMEM((2, tq, tk), bf16)],
    )
    return pl.pallas_call(
        body, name="fox_attn_bwd", grid_spec=grid_spec,
        out_shape=[jax.ShapeDtypeStruct((T, AW), bf16), jax.ShapeDtypeStruct((PAIRS, T, 128), f32),
                   jax.ShapeDtypeStruct((T, AW), bf16), jax.ShapeDtypeStruct((PAIRS, T, 128), f32),
                   jax.ShapeDtypeStruct((T, AW), bf16)],
        compiler_params=_cp("parallel", "arbitrary"),
    )(it, jt, qkv, qkv, qkv, qaug, kaug, attn, dattn, lse, dep)


def _fox_cumsum_bwd(dqs, dks, fl, bfp):
    tb = CUMSUM_ROWS
    nb = T // tb

    def body(dqs_ref, dks_ref, fl_ref, b_ref, df_ref, db_ref, carry):
        i = pl.program_id(0)

        @pl.when(i == 0)
        def _():
            carry[...] = jnp.zeros_like(carry)
            db_ref[...] = jnp.zeros_like(db_ref)

        r = lax.broadcasted_iota(jnp.int32, (128, 128), 0)
        cc = lax.broadcasted_iota(jnp.int32, (128, 128), 1)
        pick = lambda even_lane, odd_lane, p: jnp.logical_or(
            jnp.logical_and(r == even_lane, cc == 2 * p), jnp.logical_and(r == odd_lane, cc == 2 * p + 1)).astype(bf16)
        dc = jnp.zeros((tb, 128), f32)
        for p in range(PAIRS):
            rows_at = pick(SPARE[0] + ROW_SUM_LANE, SPARE[1] + ROW_SUM_LANE, p)
            cols_at = pick(SPARE[0] + COL_SUM_LANE, SPARE[1] + COL_SUM_LANE, p)
            dc = dc + _dot01(rows_at, dqs_ref[p], False) - _dot01(cols_at, dks_ref[p], False)
        rt = lax.broadcasted_iota(jnp.int32, (tb, tb), 0)
        ct = lax.broadcasted_iota(jnp.int32, (tb, tb), 1)
        utri = (ct >= rt).astype(bf16)
        dl = _dot01(utri, dc, True) + carry[0:1, :]
        carry[...] = jnp.broadcast_to(dl[0:1, :], (8, 128))
        z = fl_ref[...] + b_ref[...]
        df = dl * jax.nn.sigmoid(-z)
        df_ref[...] = df.astype(bf16)
        db_ref[...] += jnp.sum(df, axis=0, keepdims=True)

    rev = lambda i: (nb - 1 - i, 0)
    return pl.pallas_call(
        body, name="fox_cumsum_bwd", grid=(nb,),
        in_specs=[pl.BlockSpec((PAIRS, tb, 128), lambda i: (0, nb - 1 - i, 0)),
                  pl.BlockSpec((PAIRS, tb, 128), lambda i: (0, nb - 1 - i, 0)),
                  pl.BlockSpec((tb, 128), rev), _full((1, 128))],
        out_specs=[pl.BlockSpec((tb, 128), rev), _full((1, 128))],
        out_shape=[jax.ShapeDtypeStruct((T, 128), bf16), jax.ShapeDtypeStruct((1, 128), f32)],
        scratch_shapes=[pltpu.VMEM((8, 128), f32)],
        compiler_params=_cp("arbitrary"),
    )(dqs, dks, fl, bfp)


def _inproj_bwd(dq, dk, dv, du, df, wm, wf, x, dx1, g1):
    tm = 512

    def body(dq_ref, dk_ref, dv_ref, du_ref, df_ref, wm_ref, wf_ref, x_ref, dx1_ref, g_ref, dx_ref, dn_ref):
        i = pl.program_id(0)

        @pl.when(i == 0)
        def _():
            dn_ref[...] = jnp.zeros_like(dn_ref)

        dh = jnp.dot(dq_ref[...], wm_ref[0:AW, :], preferred_element_type=f32)
        dh = dh + jnp.dot(dk_ref[...], wm_ref[AW:2 * AW, :], preferred_element_type=f32)
        dh = dh + jnp.dot(dv_ref[...], wm_ref[2 * AW:3 * AW, :], preferred_element_type=f32)
        dh = dh + jnp.dot(du_ref[...], wm_ref[3 * AW:4 * AW, :], preferred_element_type=f32)
        dh = dh + jnp.dot(df_ref[...], wf_ref[...], preferred_element_type=f32)
        xv = x_ref[...]
        r = lax.rsqrt(jnp.mean(xv * xv, axis=-1, keepdims=True) + EPS)
        xhat = xv * r
        dn_ref[...] += jnp.sum(dh * xhat, axis=0, keepdims=True)
        z = dh * g_ref[...]
        dx_ref[...] = dx1_ref[...] + r * (z - xhat * jnp.mean(z * xhat, axis=-1, keepdims=True))

    row = lambda i: (i, 0)
    return pl.pallas_call(
        body, name="inproj_bwd", grid=(T // tm,),
        in_specs=[pl.BlockSpec((tm, AW), row)] * 4 + [pl.BlockSpec((tm, 128), row), _full((4 * AW, D)), _full((128, D)),
                                                       pl.BlockSpec((tm, D), row), pl.BlockSpec((tm, D), row), _full((1, D))],
        out_specs=[pl.BlockSpec((tm, D), row), _full((1, D))],
        out_shape=[jax.ShapeDtypeStruct((T, D), f32), jax.ShapeDtypeStruct((1, D), f32)],
        compiler_params=_cp("arbitrary"),
    )(dq, dk, dv, du, df, wm, wf, x, dx1, g1)


def _adamw_math(w, g, m, v):
    m = B1 * m + (1.0 - B1) * g
    v = B2 * v + (1.0 - B2) * (g * g)
    m_hat = m / (1.0 - B1 ** STEP)
    v_hat = v / (1.0 - B2 ** STEP)
    delta = -LR * (m_hat / (jnp.sqrt(v_hat) + AEPS) + WD * w)
    return delta, m, v


def _adamw_shard(w, m, v, p_mine, p_other, name):
    rows, cols = w.shape
    tr = rows if rows <= IN_S else rows // 2

    def body(w_ref, m_ref, v_ref, a_ref, b_ref, g_ref, d_ref, nm_ref, nv_ref):
        g = a_ref[...].astype(f32) + b_ref[...].astype(f32)
        g_ref[...] = g
        d_ref[...], nm_ref[...], nv_ref[...] = _adamw_math(w_ref[...], g, m_ref[...], v_ref[...])

    spec = pl.BlockSpec((tr, cols), lambda i: (i, 0))
    return pl.pallas_call(
        body, name=name, grid=(rows // tr,), in_specs=[spec] * 5, out_specs=[spec] * 4,
        out_shape=[jax.ShapeDtypeStruct((rows, cols), f32)] * 4, compiler_params=_cp("parallel"),
    )(w, m, v, p_mine, p_other)


SMALL_SLOTS = ((0, 8, 128), (8, 16, 128), (16, 24, 128), (24, 28, 128), (32, 33, 8))
LOSS_ROW = 39


def _adamw_small(ws, ms, vs, parts, parts_wp):
    n = len(ws)

    def body(*refs):
        w_refs, m_refs, v_refs = refs[0:n], refs[n:2 * n], refs[2 * n:3 * n]
        p_ref, pw_ref = refs[3 * n], refs[3 * n + 1]
        outs = refs[3 * n + 2:]
        g_all = p_ref[0]
        g_wp = pw_ref[0]
        for k in range(1, 8):
            g_all = g_all + p_ref[k]
            g_wp = g_wp + pw_ref[k]
        grads = [g_all[r0:r1, 0:lanes] for r0, r1, lanes in SMALL_SLOTS] + [g_wp]
        for idx, g in enumerate(grads):
            d, nm, nv = _adamw_math(w_refs[idx][...], g, m_refs[idx][...], v_refs[idx][...])
            outs[idx][...] = g
            outs[n + idx][...] = d
            outs[2 * n + idx][...] = nm
            outs[3 * n + idx][...] = nv
        outs[4 * n][...] = g_all[LOSS_ROW:LOSS_ROW + 1, :]

    shapes = [jax.ShapeDtypeStruct(w.shape, f32) for w in ws]
    res = pl.pallas_call(
        body, name="adamw_small", out_shape=shapes * 4 + [jax.ShapeDtypeStruct((1, 128), f32)],
    )(*ws, *ms, *vs, parts, parts_wp)
    return res[:4 * n], res[4 * n]


def _sum4(recv, g, mine, name):
    _, rows, cols = recv.shape
    tr = rows if rows <= IN_S else rows // 2

    def body(mine_ref, r_ref, g_ref, o_ref):
        o_ref[...] = ((g_ref[...].astype(f32) + r_ref[0].astype(f32))
                      + (r_ref[1].astype(f32) + r_ref[2].astype(f32))).astype(bf16)

    grid_spec = pltpu.PrefetchScalarGridSpec(
        num_scalar_prefetch=1, grid=(rows // tr,),
        in_specs=[pl.BlockSpec((3, tr, cols), lambda i, m: (0, i, 0)),
                  pl.BlockSpec((None, tr, cols), lambda i, m: (m[0], i, 0))],
        out_specs=pl.BlockSpec((tr, cols), lambda i, m: (i, 0)))
    return pl.pallas_call(
        body, name=name, grid_spec=grid_spec, out_shape=jax.ShapeDtypeStruct((rows, cols), bf16),
        compiler_params=_cp("arbitrary"),
    )(mine, recv, g)


_HBM = pl.BlockSpec(memory_space=pltpu.HBM)
_SEM = pl.BlockSpec(memory_space=pltpu.SEMAPHORE)
_EFFECT = pltpu.SideEffectType.DATAFLOW_SIDE_EFFECTING


def _in_hbm(a):
    return pltpu.with_memory_space_constraint(a, pltpu.HBM)


def _mesh_pos():
    return lax.axis_index("x"), lax.axis_index("y"), lax.axis_index("c")


def _other_chips(x, y):
    return [(1 - x, y), (x, 1 - y), (1 - x, 1 - y)]


def _gather_copy(srcs, lands, send_sems, recv_sems, a, k, slot):
    x, y, c = _mesh_pos()
    cx, cy = _other_chips(x, y)[k]
    return pltpu.make_async_remote_copy(
        src_ref=srcs[a], dst_ref=lands[a].at[slot], send_sem=send_sems.at[3 * a + k], recv_sem=recv_sems.at[3 * a + k],
        device_id=(cx, cy, c), device_id_type=MESH)


def _scatter_copy(srcs, lands, send_sems, recv_sems, a, k):
    x, y, c = _mesh_pos()
    cx, cy = _other_chips(x, y)[k]
    return pltpu.make_async_remote_copy(
        src_ref=srcs[a].at[2 * cx + cy], dst_ref=lands[a].at[k], send_sem=send_sems.at[3 * a + k],
        recv_sem=recv_sems.at[3 * a + k], device_id=(cx, cy, c), device_id_type=MESH)


def _all_gather_w_in(part):
    cols = part.shape[1] // 2

    def body(src, dst, send_sems, recv_sems, loc_sem):
        x, y, c = _mesh_pos()
        mine = 2 * x + y
        chips = _other_chips(x, y)
        half = lambda ref, cc: ref.at[:, pl.ds(pl.multiple_of(cc * cols, cols), cols)]

        def over_ici(k, slot):
            cx, cy = chips[k]
            return pltpu.make_async_remote_copy(
                src_ref=half(src, c), dst_ref=half(dst.at[slot], c), send_sem=send_sems.at[k], recv_sem=recv_sems.at[k],
                device_id=(cx, cy, c), device_id_type=MESH)

        def to_sibling(k, cc):
            slot = 2 * chips[k][0] + chips[k][1]
            return pltpu.make_async_remote_copy(
                src_ref=half(dst.at[slot], cc), dst_ref=half(dst.at[slot], cc), send_sem=send_sems.at[3 + k],
                recv_sem=recv_sems.at[3 + k], device_id=(x, y, 1 - c), device_id_type=MESH)

        local = pltpu.make_async_copy(src, dst.at[mine], loc_sem.at[0])
        local.start()
        first = [over_ici(k, mine) for k in range(3)]
        for cp in first:
            cp.start()
        passed = [to_sibling(k, c) for k in range(3)]
        for k in range(3):
            over_ici(k, 2 * chips[k][0] + chips[k][1]).wait_recv()
            passed[k].start()
        for k in range(3):
            to_sibling(k, 1 - c).wait_recv()
        for cp in first + passed:
            cp.wait_send()
        local.wait()

    return pl.pallas_call(
        body, name="all_gather_w_in", in_specs=[_HBM], out_specs=_HBM,
        out_shape=jax.ShapeDtypeStruct((NSH,) + part.shape, part.dtype),
        scratch_shapes=[pltpu.SemaphoreType.DMA((6,)), pltpu.SemaphoreType.DMA((6,)), pltpu.SemaphoreType.DMA((1,))],
    )(part)


def _split_start(name, srcs, lands, n_sems, plan, dep):
    n, nl = len(srcs), len(lands)

    def body(*refs):
        src_refs, land_refs = refs[:n], refs[n:n + nl]
        send_sems, recv_sems = refs[n + nl + 1], refs[n + nl + 2]
        token = refs[-1]
        sends, _ = plan(src_refs, land_refs, send_sems, recv_sems)
        for cp in sends:
            cp.start()
        token[...] = jnp.zeros_like(token)

    outs = pl.pallas_call(
        body, name=name,
        in_specs=[_HBM] * (n + nl) + [pl.BlockSpec(memory_space=pl.ANY)],
        out_specs=[_SEM, _SEM] + [_HBM] * (n + nl) + [pl.BlockSpec(memory_space=pltpu.VMEM)],
        out_shape=[pltpu.SemaphoreType.DMA((n_sems,)), pltpu.SemaphoreType.DMA((n_sems,))]
        + [pltpu.HBM(a.shape, a.dtype) for a in list(srcs) + list(lands)] + [jax.ShapeDtypeStruct((8, 128), f32)],
        input_output_aliases={i: 2 + i for i in range(n + nl)},
        compiler_params=pltpu.CompilerParams(has_side_effects=_EFFECT),
    )(*[_in_hbm(a) for a in list(srcs) + list(lands)], dep)
    return outs[0], outs[1], list(outs[2:2 + n]), list(outs[2 + n:2 + n + nl]), outs[-1]


def _split_wait(name, send_sems, recv_sems, srcs, lands, after, plan):
    n, nl = len(srcs), len(lands)

    def body(*refs):
        src_refs, land_refs = refs[:n], refs[n:n + nl]
        s_sems, r_sems = refs[n + nl], refs[n + nl + 1]
        sends, recvs = plan(src_refs, land_refs, s_sems, r_sems)
        for cp in recvs:
            cp.wait_recv()
        for cp in sends:
            cp.wait_send()

    outs = pl.pallas_call(
        body, name=name,
        in_specs=[_HBM] * (n + nl) + [_SEM, _SEM, pl.BlockSpec(memory_space=pl.ANY)],
        out_specs=[_HBM] * (n + nl),
        out_shape=[pltpu.HBM(a.shape, a.dtype) for a in list(srcs) + list(lands)],
        input_output_aliases={i: i for i in range(n + nl)},
        compiler_params=pltpu.CompilerParams(has_side_effects=_EFFECT),
    )(*srcs, *lands, send_sems, recv_sems, after)
    return list(outs[:n]), list(outs[n:])


def _gather_plan(srcs, lands, ss, rs):
    x, y, _ = _mesh_pos()
    chips = _other_chips(x, y)
    sends = [_gather_copy(srcs, lands, ss, rs, a, k, 2 * x + y) for a in range(len(srcs)) for k in range(3)]
    recvs = [_gather_copy(srcs, lands, ss, rs, a, k, 2 * chips[k][0] + chips[k][1])
             for a in range(len(srcs)) for k in range(3)]
    return sends, recvs


def _scatter_plan(srcs, lands, ss, rs):
    cps = [_scatter_copy(srcs, lands, ss, rs, a, k) for a in range(len(srcs)) for k in range(3)]
    return cps, cps


def _scatter_and_spread_plan(srcs, lands, ss, rs):
    x, y, c = _mesh_pos()
    me = 4 * x + 2 * y + c
    n = len(srcs) - 1
    cps = [_scatter_copy(srcs[:n], lands[:n], ss, rs, a, k) for a in range(n) for k in range(3)]
    for f in range(1, 8):
        peer = ((x + (f >> 2)) % 2, (y + ((f >> 1) & 1)) % 2, (c + (f & 1)) % 2)
        cps.append(pltpu.make_async_remote_copy(
            src_ref=srcs[n], dst_ref=lands[n].at[me], send_sem=ss.at[3 * n - 1 + f], recv_sem=rs.at[3 * n - 1 + f],
            device_id=peer, device_id_type=MESH))
    return cps, cps


def _swap_with_sibling(parts, name):
    n = len(parts)

    def body(*refs):
        srcs, dsts = refs[:n], refs[n:2 * n]
        send_sems, recv_sems = refs[2 * n:]
        x, y, c = _mesh_pos()
        cps = [pltpu.make_async_remote_copy(src_ref=srcs[a], dst_ref=dsts[a], send_sem=send_sems.at[a],
                                            recv_sem=recv_sems.at[a], device_id=(x, y, 1 - c), device_id_type=MESH)
               for a in range(n)]
        for cp in cps:
            cp.start()
        for cp in cps:
            cp.wait_recv()
        for cp in cps:
            cp.wait_send()

    return pl.pallas_call(
        body, name=name, in_specs=[_HBM] * n, out_specs=[_HBM] * n,
        out_shape=[jax.ShapeDtypeStruct(p.shape, p.dtype) for p in parts],
        scratch_shapes=[pltpu.SemaphoreType.DMA((n,)), pltpu.SemaphoreType.DMA((n,))],
    )(*parts)


def _forward(x, tgt, wm, wf, mlp_w_fn, g1, bfp, wp, scale, g2, gf, dep):
    h, qkv, u, fl = _rms_inproj(x, g1, wm, wf, dep)
    qaug, kaug = _fox_cumsum(fl, bfp)
    attn, lse = _attn_fwd(qkv, qaug, kaug)
    pooled, pool = _pool_fwd(u, wp, scale)
    wo, wgt, wut, wd = mlp_w_fn(attn)
    x1, h2 = _outproj(x, attn, pool, wo, g2)
    loss, dgf, dx2, dx2b, ud, silu, a_b = _mlp_fwd_loss(h2, x1, wgt, wut, wd, tgt, gf)
    saved = dict(h=h, qkv=qkv, fl=fl, qaug=qaug, kaug=kaug, attn=attn, lse=lse, pooled=pooled, pool=pool, x1=x1, h2=h2,
                 ud=ud, silu=silu, a_b=a_b, wo=wo, wgt=wgt, wut=wut, wd=wd)
    return loss, dgf, dx2, dx2b, saved


def _backward_mlp(sv, dx2, dx2b, g2):
    dgate, dup, dx1, dx1b, dg2 = _mlp_bwd(dx2b, dx2, sv["ud"], sv["silu"], sv["wgt"], sv["wut"], sv["wd"], sv["x1"], g2)
    (dwd,) = _mm_tn(sv["a_b"], [dx2b], "dw_down", a_sharded=True, tk=T)
    (dwgt,) = _mm_tn(dgate, [sv["h2"]], "dw_gate", a_sharded=True, tk=T)
    (dwut,) = _mm_tn(dup, [sv["h2"]], "dw_up", a_sharded=True, tk=T)
    return dx1, dx1b, dg2, (dwgt, dwut, dwd)


def _backward_outproj_pool(sv, dx1b, wp, scale):
    dattn, dpool = _outproj_bwd(dx1b, sv["wo"])
    dwo_a, = _mm_tn(sv["attn"], [dx1b], "dw_out_attn", tk=2048)
    dwo_p, = _mm_tn(sv["pool"], [dx1b], "dw_out_pool", tk=2048)
    dwo = jnp.concatenate([dwo_a, dwo_p], axis=0).reshape(NSH, D // NSH, D)
    du, dscale, dwp = _pool_bwd(dpool, sv["pooled"], wp, scale)
    return dattn, dwo, du, dscale, dwp


def _backward_attn_inproj(sv, x, dx1, dattn, du, wm, wf, g1, bfp, dep):
    dq, dqs, dk, dks, dv = _attn_bwd(sv["qkv"], sv["qaug"], sv["kaug"], sv["attn"], dattn, sv["lse"], dep)
    df, dbf = _fox_cumsum_bwd(dqs, dks, sv["fl"], bfp)
    dx, dg1 = _inproj_bwd(dq, dk, dv, du, df, wm, wf, x, dx1, g1)
    dwq, dwk, dwv, dwu_in, dwf = _mm_tn_rows([dq, dk, dv, du, df], sv["h"], "dw_in")
    dwin = jnp.concatenate([dwq, dwk, dwv, dwf[0:8], dwu_in], axis=0)
    return dx, dg1, dbf, dwin.reshape(NSH, IN_S, D)


def kernel(x, norm1_g, w_in, b_forget, w_pool, pool_scale, w_out, norm2_g, w_gate, w_up, w_down, final_g, loss_target, m_norm1_g, m_w_in, m_b_forget, m_w_pool, m_pool_scale, m_w_out, m_norm2_g, m_w_gate, m_w_up, m_w_down, m_final_g, v_norm1_g, v_w_in, v_b_forget, v_w_pool, v_pool_scale, v_w_out, v_norm2_g, v_w_gate, v_w_up, v_w_down, v_final_g):
    mine = (2 * lax.axis_index("x") + lax.axis_index("y")).astype(jnp.int32)
    mine1 = mine.reshape(1)
    tr = lambda a: jnp.transpose(a[0])

    win4 = _all_gather_w_in(tr(w_in).astype(bf16))
    later = [w_out[0].astype(bf16), tr(w_gate).astype(bf16), tr(w_up).astype(bf16), w_down[0].astype(bf16)]
    lands = [lax.dynamic_update_slice(lax.empty((NSH,) + p.shape, bf16), p[None], (mine, 0, 0)) for p in later]
    ag_send, ag_recv, later_thru, lands_thru, ag_token = _split_start("all_gather_start", later, lands, 12, _gather_plan,
                                                                      win4)
    win = win4.reshape(IN_W, D)
    wm = jnp.concatenate([win[0:3 * AW], win[3 * AW + 8:]], axis=0)
    wf = jnp.pad(win[3 * AW:3 * AW + 8], ((0, 120), (0, 0)))
    bfp = jnp.pad(b_forget, ((0, 0), (0, 120)))
    wp = w_pool[0].astype(bf16)
    gf = final_g.reshape(1, D)

    def later_weights(after):
        _, (wo4, wgt, wut, wd) = _split_wait("all_gather_wait", ag_send, ag_recv, later_thru, lands_thru, after, _gather_plan)
        return wo4.reshape(D, D), wgt, wut, wd

    xe, tgt = x[0], loss_target[0]
    loss_v, dgf, dx2, dx2b, sv = _forward(xe, tgt, wm, wf, later_weights, norm1_g, bfp, wp, pool_scale, norm2_g, gf, ag_token)
    dx1, dx1b, dg2, mlp_grads = _backward_mlp(sv, dx2, dx2b, norm2_g)
    dattn, dwo, du, dscale, dwp = _backward_outproj_pool(sv, dx1b, wp, pool_scale)
    me = (4 * lax.axis_index("x") + 2 * lax.axis_index("y") + lax.axis_index("c")).astype(jnp.int32)
    dwp = dwp.reshape(512, 128)
    first = [dwo] + list(mlp_grads) + [dwp]
    first_lands = [lax.empty((3,) + g.shape[1:], bf16) for g in first[:4]]
    first_lands.append(lax.dynamic_update_slice(lax.empty((8, 512, 128), f32), dwp[None], (me, 0, 0)))
    rs_send, rs_recv, first_thru, first_lands_thru, rs_token = _split_start(
        "reduce_scatter_start", first, first_lands, 19, _scatter_and_spread_plan, du)
    dx, dg1, dbf, dwin = _backward_attn_inproj(sv, xe, dx1, dattn, du, wm, wf, norm1_g, bfp, rs_token)

    pad8 = lambda r: jnp.pad(r, ((0, 8 - r.shape[0]), (0, 0)))
    loss_rows = jnp.concatenate([dbf, jnp.zeros((6, 128), f32), loss_v[0:1, :]], axis=0)
    small = jnp.concatenate([dg1.reshape(8, 128), dg2.reshape(8, 128), dgf.reshape(8, 128), pad8(dscale.reshape(4, 128)),
                             loss_rows], axis=0)
    small_land = lax.dynamic_update_slice(lax.empty((8, SMALL_ROWS, 128), f32), small[None], (me, 0, 0))
    tail_send, tail_recv, tail_thru, tail_lands_thru, tail_token = _split_start(
        "tail_start", [dwin, small], [lax.empty((3,) + dwin.shape[1:], bf16), small_land], 10, _scatter_and_spread_plan,
        dx)
    first_thru, first_recv = _split_wait("reduce_scatter_wait", rs_send, rs_recv, first_thru, first_lands_thru, tail_token,
                                         _scatter_and_spread_plan)
    wp_all = first_recv[4]
    ws = [tr(w_in), w_out[0], tr(w_gate), tr(w_up), w_down[0]]
    ms = [tr(m_w_in), m_w_out[0], tr(m_w_gate), tr(m_w_up), m_w_down[0]]
    vs = [tr(v_w_in), v_w_out[0], tr(v_w_gate), tr(v_w_up), v_w_down[0]]
    partial = [_sum4(r, g, mine1, f"sum4_{i + 1}") for i, (r, g) in enumerate(zip(first_recv[:4], first_thru[:4]))]
    other = _swap_with_sibling(partial, "swap_first")
    big = [_adamw_shard(ws[i + 1], ms[i + 1], vs[i + 1], partial[i], other[i], f"adamw_{i + 1}") for i in range(4)]
    (dwin_thru, _), (in_recv_land, small_all) = _split_wait("tail_wait", tail_send, tail_recv, tail_thru, tail_lands_thru,
                                                            big[3][0], _scatter_and_spread_plan)
    partial_in = _sum4(in_recv_land, dwin_thru, mine1, "sum4_0")
    (other_in,) = _swap_with_sibling([partial_in], "swap_in")
    big = [_adamw_shard(ws[0], ms[0], vs[0], partial_in, other_in, "adamw_0")] + big

    small_names = ["norm1_g", "norm2_g", "final_g", "pool_scale", "b_forget", "w_pool"]
    rows = lambda a, b, c, d, e, f: [a.reshape(8, 128), b.reshape(8, 128), c.reshape(8, 128), d.reshape(4, 128),
                                     e.reshape(1, 8), f.reshape(512, 128)]
    sm, loss_row = _adamw_small(rows(norm1_g, norm2_g, final_g, pool_scale, b_forget, w_pool),
                                rows(m_norm1_g, m_norm2_g, m_final_g, m_pool_scale, m_b_forget, m_w_pool),
                                rows(v_norm1_g, v_norm2_g, v_final_g, v_pool_scale, v_b_forget, v_w_pool), small_all, wp_all)
    small_shape = dict(norm1_g=(1, D), norm2_g=(1, D), final_g=(D,), pool_scale=(1, AW), b_forget=(1, 8),
                       w_pool=(1, 4, 128, 128))

    order = ["norm1_g", "w_in", "b_forget", "w_pool", "pool_scale", "w_out", "norm2_g", "w_gate", "w_up", "w_down", "final_g"]
    big_idx = {"w_in": 0, "w_out": 1, "w_gate": 2, "w_up": 3, "w_down": 4}
    outs = [loss_row[0, 0], dx[None]]
    for kind in range(4):
        for name in order:
            if name in ("w_in", "w_gate", "w_up"):
                outs.append(jnp.transpose(big[big_idx[name]][kind])[None])
            elif name in big_idx:
                outs.append(big[big_idx[name]][kind][None])
            else:
                outs.append(sm[6 * kind + small_names.index(name)].reshape(small_shape[name]))
    return tuple(outs)
```

```python
import jax
import jax.numpy as jnp
import numpy as np
from jax import lax
from jax.experimental import pallas as pl
from jax.experimental.pallas import tpu as pltpu

f32 = jnp.float32
bf16 = jnp.bfloat16

T = 4096
D = 1024
NSH = 4
IN_W = 2056
IN_S = IN_W // NSH
AW = 512
PAIRS = 4
SPARE = (64, 0)
ROW_SUM_LANE, COL_SUM_LANE = 0, 3
FF = 2816
FS = FF // NSH
WINDOWS = (2, 4, 8, 16)
HALO = 16
EPS = 1e-6
NEG = -1e30
LR, B1, B2, AEPS, WD, STEP = 0.001, 0.9, 0.999, 1e-08, 0.01, 10
SMALL_ROWS = 40

NT = (((1,), (1,)), ((), ()))
TN = (((0,), (0,)), ((), ()))

MESH = pl.DeviceIdType.MESH


def _cp(*sem):
    return pltpu.CompilerParams(dimension_semantics=sem)


def _full(shape):
    n = len(shape)
    return pl.BlockSpec(shape, lambda *_: (0,) * n)


def _resident(shape):
    n = len(shape)
    return pl.BlockSpec(shape, lambda *_: (0,) * n, pipeline_mode=pl.Buffered(1))


W_ROWS = 4 * AW + 128


def _rms_inproj(x, g1, w, dep):
    tm = 512

    def body(x_ref, g_ref, w_ref, dep_ref, h_ref, qkv_ref, u_ref, fl_ref):
        xv = x_ref[...]
        r = lax.rsqrt(jnp.mean(xv * xv, axis=-1, keepdims=True) + EPS)
        h = (xv * r * g_ref[...]).astype(bf16)
        h_ref[...] = h
        qkv_ref[...] = lax.dot_general(h, w_ref[0:3 * AW, :], NT, preferred_element_type=f32).astype(bf16)
        u_ref[...] = lax.dot_general(h, w_ref[3 * AW:4 * AW, :], NT, preferred_element_type=f32)
        fl_ref[...] = lax.dot_general(h, w_ref[4 * AW:W_ROWS, :], NT, preferred_element_type=f32)

    return pl.pallas_call(
        body, name="rms_inproj", grid=(T // tm,),
        in_specs=[pl.BlockSpec((tm, D), lambda i: (i, 0)), _full((1, D)), _full((W_ROWS, D)), _full((8, 128))],
        out_specs=[pl.BlockSpec((tm, D), lambda i: (i, 0)), pl.BlockSpec((tm, 3 * AW), lambda i: (i, 0)),
                   pl.BlockSpec((tm, AW), lambda i: (i, 0)), pl.BlockSpec((tm, 128), lambda i: (i, 0))],
        out_shape=[jax.ShapeDtypeStruct((T, D), bf16), jax.ShapeDtypeStruct((T, 3 * AW), bf16),
                   jax.ShapeDtypeStruct((T, AW), f32), jax.ShapeDtypeStruct((T, 128), f32)],
        compiler_params=_cp("parallel"),
    )(x, g1, w, dep)


CUMSUM_ROWS = 512
FS_CHUNKS = ((0, 256), (256, 512), (512, FS))


def _log_sigmoid(z):
    return jnp.minimum(z, 0.0) - jnp.log(1.0 + jnp.exp(-jnp.abs(z)))


def _split3(x):
    hi = x.astype(bf16)
    r1 = x - hi.astype(f32)
    mid = r1.astype(bf16)
    return hi, mid, (r1 - mid.astype(f32)).astype(bf16)


def _dot01(sel, x, sel_first):
    parts = _split3(x)
    if sel_first:
        return sum(jnp.dot(sel, p, preferred_element_type=f32) for p in parts)
    return sum(jnp.dot(p, sel, preferred_element_type=f32) for p in parts)


def _fox_cumsum(fl, bfp):
    tb = CUMSUM_ROWS
    nb = T // tb

    def body(fl_ref, b_ref, qa_ref, ka_ref, carry):
        i = pl.program_id(0)

        @pl.when(i == 0)
        def _():
            carry[...] = jnp.zeros_like(carry)

        lf = _log_sigmoid(fl_ref[...] + b_ref[...])
        r = lax.broadcasted_iota(jnp.int32, (tb, tb), 0)
        cc = lax.broadcasted_iota(jnp.int32, (tb, tb), 1)
        ltri = (cc <= r).astype(bf16)
        cb = _dot01(ltri, lf, True) + carry[0:1, :]
        carry[...] = jnp.broadcast_to(cb[tb - 1:tb, :], (8, 128))
        hi, mid, lo = _split3(cb)
        head = lax.broadcasted_iota(jnp.int32, (128, AW), 0)
        col = lax.broadcasted_iota(jnp.int32, (128, AW), 1)
        base = 128 * (head >> 1) + jnp.where((head & 1) == 0, SPARE[0], SPARE[1])
        place = lambda off: jnp.logical_and(col == base + off, head < 8).astype(bf16)
        mm = lambda a, off: jnp.dot(a, place(off), preferred_element_type=f32)
        cq = mm(hi, 0) + mm(mid, 1) + mm(lo, 2)
        ck = mm(hi, 3) + mm(mid, 4) + mm(lo, 5)
        within = jnp.bitwise_and(lax.broadcasted_iota(jnp.int32, (tb, AW), 1), 63)
        qa_ref[...] = jnp.where(jnp.logical_and(within >= 3, within <= 5), 1.0, cq).astype(bf16)
        ka_ref[...] = jnp.where(within <= 2, 1.0, -ck).astype(bf16)

    return pl.pallas_call(
        body, name="fox_cumsum", grid=(nb,),
        in_specs=[pl.BlockSpec((tb, 128), lambda i: (i, 0)), _full((1, 128))],
        out_specs=[pl.BlockSpec((tb, AW), lambda i: (i, 0)), pl.BlockSpec((tb, AW), lambda i: (i, 0))],
        out_shape=[jax.ShapeDtypeStruct((T, AW), bf16), jax.ShapeDtypeStruct((T, AW), bf16)],
        scratch_shapes=[pltpu.VMEM((8, 128), f32)],
        compiler_params=_cp("arbitrary"),
    )(fl, bfp)


ATT_T = 512


def _causal_steps(key_major):
    n = T // ATT_T
    if key_major:
        pairs = [(i, j) for j in range(n) for i in range(j, n)]
    else:
        pairs = [(i, j) for i in range(n) for j in range(i + 1)]
    it = np.array([p[0] for p in pairs], np.int32)
    jt = np.array([p[1] for p in pairs], np.int32)
    return jnp.asarray(it), jnp.asarray(jt)


def _row_blocks(tq, tk, on_diagonal):
    return ((0, tq // 2, tk // 2), (tq // 2, tq, tk)) if on_diagonal else ((0, tq, tk),)


def _attn_fwd(qkv, qaug, kaug):
    tq = tk = ATT_T
    it, jt = _causal_steps(False)
    nsteps = it.shape[0]

    rs = 64

    def body(it_ref, jt_ref, q_ref, k_ref, v_ref, qa_ref, ka_ref, o_ref, lse_ref, m_sc, acc_sc, s_sc, p_sc, alpha_sc):
        t = pl.program_id(1)
        i = it_ref[t]
        j = jt_ref[t]

        @pl.when(j == 0)
        def _():
            m_sc[...] = jnp.full_like(m_sc, NEG)
            acc_sc[...] = jnp.zeros_like(acc_sc)

        lane = lax.broadcasted_iota(jnp.int32, (tq, 128), 1)
        spare = SPARE

        def step(on_diagonal):
            q = q_ref[...] * 0.125
            k = k_ref[...]
            v = v_ref[...]
            qa = qa_ref[...]
            ka = ka_ref[...]
            blocks = _row_blocks(tq, tk, on_diagonal)
            for e in range(2):
                hm = (lane >= 64) if e else (lane < 64)
                qe = jnp.where(hm, q, qa)
                ke = jnp.where(hm, k, ka)
                for r0, r1, nc in blocks:
                    s_sc[e, r0:r1, 0:nc] = lax.dot_general(qe[r0:r1], ke[0:nc], NT, preferred_element_type=f32)
            for e in range(2):
                for r0, r1, nc in blocks:
                    for r in range(r0, r1, rs):
                        s = s_sc[e, r:r + rs, 0:nc]
                        if on_diagonal:
                            row = lax.broadcasted_iota(jnp.int32, (rs, nc), 0) + r
                            col = lax.broadcasted_iota(jnp.int32, (rs, nc), 1)
                            s = jnp.where(col <= row, s, NEG)
                        m_prev = m_sc[e, r:r + rs, :]
                        m_new = jnp.maximum(m_prev, jnp.max(s, axis=1, keepdims=True))
                        p_sc[e, r:r + rs, 0:nc] = jnp.exp(s - jnp.tile(m_new, (1, nc // 128))).astype(bf16)
                        alpha_sc[e, r:r + rs, :] = jnp.exp(m_prev - m_new)
                        m_sc[e, r:r + rs, :] = m_new
            for e in range(2):
                hm = (lane >= 64) if e else (lane < 64)
                ve = jnp.where(hm, v, (lane == spare[e]).astype(bf16))
                for r0, r1, nc in blocks:
                    acc_sc[e, r0:r1] = (alpha_sc[e, r0:r1] * acc_sc[e, r0:r1]
                                        + jnp.dot(p_sc[e, r0:r1, 0:nc], ve[0:nc], preferred_element_type=f32))

        @pl.when(j < i)
        def _():
            step(False)

        @pl.when(j == i)
        def _():
            step(True)
            l0 = acc_sc[0][:, spare[0]:spare[0] + 1]
            l1 = acc_sc[1][:, spare[1]:spare[1] + 1]
            o_ref[...] = jnp.where(lane < 64, acc_sc[0] / l0, acc_sc[1] / l1).astype(bf16)
            lse_ref[...] = jnp.where(lane < 64, m_sc[0] + jnp.log(l0), m_sc[1] + jnp.log(l1))

    qmap = lambda p, t, it, jt: (it[t], p)
    kmap = lambda p, t, it, jt: (jt[t], p)
    grid_spec = pltpu.PrefetchScalarGridSpec(
        num_scalar_prefetch=2, grid=(PAIRS, nsteps),
        in_specs=[pl.BlockSpec((tq, 128), qmap),
                  pl.BlockSpec((tk, 128), lambda p, t, it, jt: (jt[t], PAIRS + p)),
                  pl.BlockSpec((tk, 128), lambda p, t, it, jt: (jt[t], 2 * PAIRS + p)),
                  pl.BlockSpec((tq, 128), qmap), pl.BlockSpec((tk, 128), kmap)],
        out_specs=[pl.BlockSpec((tq, 128), qmap),
                   pl.BlockSpec((None, tq, 128), lambda p, t, it, jt: (p, it[t], 0))],
        scratch_shapes=[pltpu.VMEM((2, tq, 128), f32), pltpu.VMEM((2, tq, 128), f32), pltpu.VMEM((2, tq, tk), f32),
                        pltpu.VMEM((2, tq, tk), bf16), pltpu.VMEM((2, tq, 128), f32)],
    )
    return pl.pallas_call(
        body, name="fox_attn_fwd", grid_spec=grid_spec,
        out_shape=[jax.ShapeDtypeStruct((T, AW), bf16), jax.ShapeDtypeStruct((PAIRS, T, 128), f32)],
        compiler_params=_cp("parallel", "arbitrary"),
    )(it, jt, qkv, qkv, qkv, qaug, kaug)


def _pool_fwd(u, wp, scale):
    tm = 1024

    def body(u_ref, wp_ref, sc_ref, pooled_ref, pool_ref, ext):
        i = pl.program_id(0)

        @pl.when(i == 0)
        def _():
            ext[0:HALO, :] = jnp.zeros((HALO, AW), f32)

        uv = u_ref[...]
        ext[HALO:HALO + tm, :] = uv
        t_idx = i * tm + lax.broadcasted_iota(jnp.int32, (tm, 1), 0)
        for g, w in enumerate(WINDOWS):
            lo, hi = 128 * g, 128 * (g + 1)
            ug = uv[:, lo:hi]
            acc = ug
            for d in range(1, w):
                acc = acc + ext[HALO - d:HALO - d + tm, lo:hi]
            cnt = jnp.minimum(t_idx + 1, w).astype(f32)
            pb = (acc / cnt - ug).astype(bf16)
            pooled_ref[:, lo:hi] = pb
            mixed = jnp.dot(pb, wp_ref[g], preferred_element_type=f32)
            pool_ref[:, lo:hi] = (mixed * sc_ref[:, lo:hi]).astype(bf16)
        ext[0:HALO, :] = uv[tm - HALO:tm, :]

    return pl.pallas_call(
        body, name="pool_fwd", grid=(T // tm,),
        in_specs=[pl.BlockSpec((tm, AW), lambda i: (i, 0)), _full((4, 128, 128)), _full((1, AW))],
        out_specs=[pl.BlockSpec((tm, AW), lambda i: (i, 0)), pl.BlockSpec((tm, AW), lambda i: (i, 0))],
        out_shape=[jax.ShapeDtypeStruct((T, AW), bf16), jax.ShapeDtypeStruct((T, AW), bf16)],
        scratch_shapes=[pltpu.VMEM((tm + HALO, AW), f32)],
        compiler_params=_cp("arbitrary"),
    )(u, wp, scale)


def _outproj(x, attn, pool, wo, g2):
    tm = 1024

    def body(x_ref, a_ref, p_ref, wo_ref, g_ref, x1_ref, h2_ref):
        mixed = jnp.concatenate([a_ref[...], p_ref[...]], axis=1)
        x1 = x_ref[...] + jnp.dot(mixed, wo_ref[...], preferred_element_type=f32)
        x1_ref[...] = x1
        r = lax.rsqrt(jnp.mean(x1 * x1, axis=-1, keepdims=True) + EPS)
        h2_ref[...] = (x1 * r * g_ref[...]).astype(bf16)

    return pl.pallas_call(
        body, name="outproj", grid=(T // tm,),
        in_specs=[pl.BlockSpec((tm, D), lambda i: (i, 0)), pl.BlockSpec((tm, AW), lambda i: (i, 0)),
                  pl.BlockSpec((tm, AW), lambda i: (i, 0)), _full((D, D)), _full((1, D))],
        out_specs=[pl.BlockSpec((tm, D), lambda i: (i, 0)), pl.BlockSpec((tm, D), lambda i: (i, 0))],
        out_shape=[jax.ShapeDtypeStruct((T, D), f32), jax.ShapeDtypeStruct((T, D), bf16)],
        compiler_params=_cp("parallel"),
    )(x, attn, pool, wo, g2)


def _mlp_fwd_loss(h2, x1, wg, wu, wd, tgt, gf):
    tm = 512

    def body(h_ref, x1_ref, wg_ref, wu_ref, wd_ref, t_ref, g_ref,
             loss_ref, dg_ref, dx_ref, dxb_ref, ud_ref, silu_ref, a_ref, x2):
        i = pl.program_id(0)
        s = pl.program_id(1)

        @pl.when(jnp.logical_and(i == 0, s == 0))
        def _():
            loss_ref[...] = jnp.zeros_like(loss_ref)
            dg_ref[...] = jnp.zeros_like(dg_ref)

        h = h_ref[...]
        gus = [(lax.dot_general(h, wg_ref[s, c0:c1, :], NT, preferred_element_type=f32),
                lax.dot_general(h, wu_ref[s, c0:c1, :], NT, preferred_element_type=f32)) for c0, c1 in FS_CHUNKS]
        for (c0, c1), (gate, up) in zip(FS_CHUNKS, gus):
            sg = jax.nn.sigmoid(gate)
            silu = gate * sg
            ud_ref[:, c0:c1] = (up * (sg * (1.0 + gate * (1.0 - sg)))).astype(bf16)
            silu_ref[:, c0:c1] = silu.astype(bf16)
            a_ref[:, c0:c1] = (silu * up).astype(bf16)
        part = jnp.dot(a_ref[...], wd_ref[s], preferred_element_type=f32)

        @pl.when(s == 0)
        def _():
            x2[...] = x1_ref[...] + part

        @pl.when(s > 0)
        def _():
            x2[...] += part

        @pl.when(s == NSH - 1)
        def _():
            xv = x2[...]
            g = g_ref[...]
            r = lax.rsqrt(jnp.mean(xv * xv, axis=-1, keepdims=True) + EPS)
            xhat = xv * r
            e = xhat * g - t_ref[...]
            loss_ref[...] += 0.5 * jnp.sum(jnp.mean(e * e, axis=-1, keepdims=True))
            dy = e * (1.0 / D)
            dg_ref[...] += jnp.sum(dy * xhat, axis=0, keepdims=True)
            z = dy * g
            dx = r * (z - xhat * jnp.mean(z * xhat, axis=-1, keepdims=True))
            dx_ref[...] = dx
            dxb_ref[...] = dx.astype(bf16)

    row = lambda i, s: (i, 0)
    sl = lambda i, s: (s, i, 0)
    wsl = lambda i, s: (s, 0, 0)
    return pl.pallas_call(
        body, name="mlp_fwd_loss", grid=(T // tm, NSH),
        in_specs=[pl.BlockSpec((tm, D), row), pl.BlockSpec((tm, D), row),
                  _resident((NSH, FS, D)), _resident((NSH, FS, D)), _resident((NSH, FS, D)),
                  pl.BlockSpec((tm, D), row), pl.BlockSpec((1, D), lambda i, s: (0, 0))],
        out_specs=[pl.BlockSpec((8, 128), lambda i, s: (0, 0)), pl.BlockSpec((1, D), lambda i, s: (0, 0)),
                   pl.BlockSpec((tm, D), row), pl.BlockSpec((tm, D), row),
                   pl.BlockSpec((None, tm, FS), sl), pl.BlockSpec((None, tm, FS), sl), pl.BlockSpec((None, tm, FS), sl)],
        out_shape=[jax.ShapeDtypeStruct((8, 128), f32), jax.ShapeDtypeStruct((1, D), f32),
                   jax.ShapeDtypeStruct((T, D), f32), jax.ShapeDtypeStruct((T, D), bf16)]
        + [jax.ShapeDtypeStruct((NSH, T, FS), bf16)] * 3,
        scratch_shapes=[pltpu.VMEM((tm, D), f32)],
        compiler_params=_cp("arbitrary", "arbitrary"),
    )(h2, x1, wg, wu, wd, tgt, gf)


def _mlp_bwd(dx2b, dx2, ud, silu, wg, wu, wd, x1, g2):
    tm = 512

    def body(dxb_ref, dx_ref, ud_ref, silu_ref, wg_ref, wu_ref, wd_ref, x1_ref, g_ref,
             dg_ref, du_ref, dx1_ref, dx1b_ref, dn_ref, acc):
        i = pl.program_id(0)
        s = pl.program_id(1)

        @pl.when(jnp.logical_and(i == 0, s == 0))
        def _():
            dn_ref[...] = jnp.zeros_like(dn_ref)

        dxb = dxb_ref[...]
        das = [lax.dot_general(dxb, wd_ref[s, c0:c1, :], NT, preferred_element_type=f32) for c0, c1 in FS_CHUNKS]
        for (c0, c1), da in zip(FS_CHUNKS, das):
            dg_ref[:, c0:c1] = (da * ud_ref[:, c0:c1].astype(f32)).astype(bf16)
            du_ref[:, c0:c1] = (da * silu_ref[:, c0:c1].astype(f32)).astype(bf16)
        part = jnp.dot(dg_ref[...], wg_ref[s], preferred_element_type=f32)
        part = part + jnp.dot(du_ref[...], wu_ref[s], preferred_element_type=f32)

        @pl.when(s == 0)
        def _():
            acc[...] = part

        @pl.when(s > 0)
        def _():
            acc[...] += part

        @pl.when(s == NSH - 1)
        def _():
            xv = x1_ref[...]
            r = lax.rsqrt(jnp.mean(xv * xv, axis=-1, keepdims=True) + EPS)
            xhat = xv * r
            dh = acc[...]
            dn_ref[...] += jnp.sum(dh * xhat, axis=0, keepdims=True)
            z = dh * g_ref[...]
            dx1 = dx_ref[...] + r * (z - xhat * jnp.mean(z * xhat, axis=-1, keepdims=True))
            dx1_ref[...] = dx1
            dx1b_ref[...] = dx1.astype(bf16)

    row = lambda i, s: (i, 0)
    sl = lambda i, s: (s, i, 0)
    wsl = lambda i, s: (s, 0, 0)
    return pl.pallas_call(
        body, name="mlp_bwd", grid=(T // tm, NSH),
        in_specs=[pl.BlockSpec((tm, D), row), pl.BlockSpec((tm, D), row),
                  pl.BlockSpec((None, tm, FS), sl), pl.BlockSpec((None, tm, FS), sl),
                  _resident((NSH, FS, D)), _resident((NSH, FS, D)), _resident((NSH, FS, D)),
                  pl.BlockSpec((tm, D), row), pl.BlockSpec((1, D), lambda i, s: (0, 0))],
        out_specs=[pl.BlockSpec((None, tm, FS), sl), pl.BlockSpec((None, tm, FS), sl),
                   pl.BlockSpec((tm, D), row), pl.BlockSpec((tm, D), row), pl.BlockSpec((1, D), lambda i, s: (0, 0))],
        out_shape=[jax.ShapeDtypeStruct((NSH, T, FS), bf16)] * 2
        + [jax.ShapeDtypeStruct((T, D), f32), jax.ShapeDtypeStruct((T, D), bf16), jax.ShapeDtypeStruct((1, D), f32)],
        scratch_shapes=[pltpu.VMEM((tm, D), f32)],
        compiler_params=_cp("arbitrary", "arbitrary"),
    )(dx2b, dx2, ud, silu, wg, wu, wd, x1, g2)


def _mm_tn(a, bs, name, a_sharded=False, b_sharded=False, tk=512, out_dtype=bf16):
    nb = len(bs)
    sh = NSH if (a_sharded or b_sharded) else 1
    m = a.shape[-1]
    nk = T // tk

    def body(a_ref, *refs):
        kk = pl.program_id(1)
        av = a_ref[...]
        for b_ref, o_ref, acc in zip(refs[:nb], refs[nb:2 * nb], refs[2 * nb:]):
            upd = lax.dot_general(av, b_ref[...], TN, preferred_element_type=f32)

            @pl.when(kk == 0)
            def _():
                acc[...] = upd

            @pl.when(kk > 0)
            def _():
                acc[...] += upd

            @pl.when(kk == nk - 1)
            def _():
                o_ref[...] = acc[...].astype(out_dtype)

    a_spec = (pl.BlockSpec((None, tk, m), lambda s, k: (s, k, 0)) if a_sharded
              else pl.BlockSpec((tk, m), lambda s, k: (k, 0)))
    b_specs, o_specs, o_shapes, scratch = [], [], [], []
    for b in bs:
        n = b.shape[-1]
        b_specs.append(pl.BlockSpec((None, tk, n), lambda s, k: (s, k, 0)) if b_sharded
                       else pl.BlockSpec((tk, n), lambda s, k: (k, 0)))
        scratch.append(pltpu.VMEM((m, n), f32))
        if sh > 1:
            o_specs.append(pl.BlockSpec((None, m, n), lambda s, k: (s, 0, 0)))
            o_shapes.append(jax.ShapeDtypeStruct((sh, m, n), out_dtype))
        else:
            o_specs.append(pl.BlockSpec((m, n), lambda s, k: (0, 0)))
            o_shapes.append(jax.ShapeDtypeStruct((m, n), out_dtype))
    return pl.pallas_call(
        body, name=name, grid=(sh, nk), in_specs=[a_spec] + b_specs, out_specs=o_specs, out_shape=o_shapes,
        scratch_shapes=scratch, compiler_params=_cp("arbitrary", "arbitrary"),
    )(a, *bs)


def _mm_tn_rows(a_list, b, name, tk=1024, out_dtype=bf16):
    na = len(a_list)
    n = b.shape[-1]
    nk = T // tk

    def body(*refs):
        a_refs, b_ref = refs[:na], refs[na]
        o_refs, accs = refs[na + 1:2 * na + 1], refs[2 * na + 1:]
        kk = pl.program_id(0)
        bv = b_ref[...]
        for a_ref, o_ref, acc in zip(a_refs, o_refs, accs):
            upd = lax.dot_general(a_ref[...], bv, TN, preferred_element_type=f32)

            @pl.when(kk == 0)
            def _():
                acc[...] = upd

            @pl.when(kk > 0)
            def _():
                acc[...] += upd

            @pl.when(kk == nk - 1)
            def _():
                o_ref[...] = acc[...].astype(out_dtype)

    return pl.pallas_call(
        body, name=name, grid=(nk,),
        in_specs=[pl.BlockSpec((tk, a.shape[-1]), lambda k: (k, 0)) for a in a_list] + [pl.BlockSpec((tk, n), lambda k: (k, 0))],
        out_specs=[pl.BlockSpec((a.shape[-1], n), lambda k: (0, 0)) for a in a_list],
        out_shape=[jax.ShapeDtypeStruct((a.shape[-1], n), out_dtype) for a in a_list],
        scratch_shapes=[pltpu.VMEM((a.shape[-1], n), f32) for a in a_list],
        compiler_params=_cp("arbitrary"),
    )(*a_list, b)


def _outproj_bwd(dx1b, wo):
    tm = 1024

    def body(dx_ref, wo_ref, da_ref, dp_ref):
        dx = dx_ref[...]
        da_ref[...] = lax.dot_general(dx, wo_ref[0:AW, :], NT, preferred_element_type=f32).astype(bf16)
        dp_ref[...] = lax.dot_general(dx, wo_ref[AW:2 * AW, :], NT, preferred_element_type=f32)

    return pl.pallas_call(
        body, name="outproj_bwd", grid=(T // tm,),
        in_specs=[pl.BlockSpec((tm, D), lambda i: (i, 0)), _full((D, D))],
        out_specs=[pl.BlockSpec((tm, AW), lambda i: (i, 0)), pl.BlockSpec((tm, AW), lambda i: (i, 0))],
        out_shape=[jax.ShapeDtypeStruct((T, AW), bf16), jax.ShapeDtypeStruct((T, AW), f32)],
        compiler_params=_cp("parallel"),
    )(dx1b, wo)


def _pool_bwd(dpool, pooled, wp, scale):
    tm = 1024
    n = T // tm

    def body(dp_ref, pb_ref, wp_ref, sc_ref, du_ref, dsc_ref, dwp_ref, ext):
        i = pl.program_id(0)

        @pl.when(i == 0)
        def _():
            ext[tm:tm + HALO, :] = jnp.zeros((HALO, AW), f32)
            dsc_ref[...] = jnp.zeros_like(dsc_ref)
            dwp_ref[...] = jnp.zeros_like(dwp_ref)

        t_idx = (n - 1 - i) * tm + lax.broadcasted_iota(jnp.int32, (tm, 1), 0)
        for g, w in enumerate(WINDOWS):
            lo, hi = 128 * g, 128 * (g + 1)
            pb = pb_ref[:, lo:hi]
            mixed = jnp.dot(pb, wp_ref[g], preferred_element_type=f32)
            dpo = dp_ref[:, lo:hi]
            dsc_ref[:, lo:hi] += jnp.sum(dpo * mixed, axis=0, keepdims=True)
            dmr = (dpo * sc_ref[:, lo:hi]).astype(bf16)
            dwp_ref[g] += lax.dot_general(pb, dmr, TN, preferred_element_type=f32)
            dpl = lax.dot_general(dmr, wp_ref[g], NT, preferred_element_type=f32)
            cnt = jnp.minimum(t_idx + 1, w).astype(f32)
            dpn = dpl / cnt
            ext[0:tm, lo:hi] = dpn
            acc = dpn
            for d in range(1, w):
                acc = acc + ext[d:d + tm, lo:hi]
            du_ref[:, lo:hi] = (acc - dpl).astype(bf16)
        ext[tm:tm + HALO, :] = ext[0:HALO, :]

    rev = lambda i: (n - 1 - i, 0)
    return pl.pallas_call(
        body, name="pool_bwd", grid=(n,),
        in_specs=[pl.BlockSpec((tm, AW), rev), pl.BlockSpec((tm, AW), rev), _full((4, 128, 128)), _full((1, AW))],
        out_specs=[pl.BlockSpec((tm, AW), rev), _full((1, AW)), _full((4, 128, 128))],
        out_shape=[jax.ShapeDtypeStruct((T, AW), bf16), jax.ShapeDtypeStruct((1, AW), f32),
                   jax.ShapeDtypeStruct((4, 128, 128), f32)],
        scratch_shapes=[pltpu.VMEM((tm + HALO, AW), f32)],
        compiler_params=_cp("arbitrary"),
    )(dpool, pooled, wp, scale)


def _attn_bwd(qkv, qaug, kaug, attn, dattn, lse, dep):
    tq = tk = ATT_T
    n = T // tq
    it, jt = _causal_steps(True)
    nsteps = it.shape[0]

    rs = 64

    def body(it_ref, jt_ref, q_ref, k_ref, v_ref, qa_ref, ka_ref, o_ref, do_ref, lse_ref, dep_ref,
             dq_ref, dqs_ref, dk_ref, dks_ref, dv_ref, dq_acc, dk_acc, dv_acc, s_sc, dp_sc, p_sc, ds_sc):
        t = pl.program_id(1)
        i = it_ref[t]
        j = jt_ref[t]

        @pl.when(t == 0)
        def _():
            dq_acc[...] = jnp.zeros_like(dq_acc)

        @pl.when(i == j)
        def _():
            dk_acc[...] = jnp.zeros_like(dk_acc)
            dv_acc[...] = jnp.zeros_like(dv_acc)

        lane = lax.broadcasted_iota(jnp.int32, (tq, 128), 1)

        def step(on_diagonal):
            q = q_ref[...] * 0.125
            k = k_ref[...]
            v = v_ref[...]
            qa = qa_ref[...]
            ka = ka_ref[...]
            do = do_ref[...]
            dd = do.astype(f32) * o_ref[...].astype(f32)
            blocks = _row_blocks(tq, tk, on_diagonal)
            qes, kes, does, deltas = [], [], [], []
            for e in range(2):
                hm = (lane >= 64) if e else (lane < 64)
                qes.append(jnp.where(hm, q, qa))
                kes.append(jnp.where(hm, k, ka))
                does.append(jnp.where(hm, do, jnp.zeros_like(do)))
                deltas.append(jnp.sum(jnp.where(hm, dd, 0.0), axis=1, keepdims=True))
                for r0, r1, nc in blocks:
                    s_sc[e, r0:r1, 0:nc] = lax.dot_general(qes[e][r0:r1], kes[e][0:nc], NT, preferred_element_type=f32)
                    dp_sc[e, r0:r1, 0:nc] = lax.dot_general(does[e][r0:r1], v[0:nc], NT, preferred_element_type=f32)
            for e in range(2):
                for r0, r1, nc in blocks:
                    for r in range(r0, r1, rs):
                        s = s_sc[e, r:r + rs, 0:nc] - lse_ref[r:r + rs, 64 * e:64 * e + 1]
                        if on_diagonal:
                            row = lax.broadcasted_iota(jnp.int32, (rs, nc), 0) + r
                            col = lax.broadcasted_iota(jnp.int32, (rs, nc), 1)
                            s = jnp.where(col <= row, s, NEG)
                        p = jnp.exp(s)
                        p_sc[e, r:r + rs, 0:nc] = p.astype(bf16)
                        ds_sc[e, r:r + rs, 0:nc] = (p * (dp_sc[e, r:r + rs, 0:nc] - deltas[e][r:r + rs, :])).astype(bf16)
                for r0, r1, nc in blocks:
                    dv_acc[:, 0:nc] += lax.dot_general(does[e][r0:r1], p_sc[e, r0:r1, 0:nc], TN, preferred_element_type=f32)
                    dsb = ds_sc[e, r0:r1, 0:nc]
                    dk_acc[e, :, 0:nc] += lax.dot_general(qes[e][r0:r1], dsb, TN, preferred_element_type=f32)
                    rq = pl.multiple_of(i * tq + r0, r1 - r0)
                    dq_acc[e, pl.ds(rq, r1 - r0), :] += jnp.dot(dsb, kes[e][0:nc], preferred_element_type=f32)

        @pl.when(i > j)
        def _():
            step(False)

        @pl.when(i == j)
        def _():
            step(True)

        @pl.when(i == n - 1)
        def _():
            dk0 = dk_acc[0].T
            dk1 = dk_acc[1].T
            dk_ref[...] = jnp.where(lane < 64, dk0, dk1).astype(bf16)
            dks_ref[...] = jnp.where(lane < 64, dk1, dk0)
            dv_ref[...] = dv_acc[...].T.astype(bf16)

        @pl.when(t == nsteps - 1)
        def _():
            lane_t = lax.broadcasted_iota(jnp.int32, (T, 128), 1)
            dq_ref[...] = (jnp.where(lane_t < 64, dq_acc[0], dq_acc[1]) * 0.125).astype(bf16)
            dqs_ref[...] = jnp.where(lane_t < 64, dq_acc[1], dq_acc[0])

    qmap = lambda p, t, it, jt: (it[t], p)
    grid_spec = pltpu.PrefetchScalarGridSpec(
        num_scalar_prefetch=2, grid=(PAIRS, nsteps),
        in_specs=[pl.BlockSpec((tq, 128), qmap),
                  pl.BlockSpec((tk, 128), lambda p, t, it, jt: (jt[t], PAIRS + p)),
                  pl.BlockSpec((tk, 128), lambda p, t, it, jt: (jt[t], 2 * PAIRS + p)),
                  pl.BlockSpec((tq, 128), qmap), pl.BlockSpec((tk, 128), lambda p, t, it, jt: (jt[t], p)),
                  pl.BlockSpec((tq, 128), qmap), pl.BlockSpec((tq, 128), qmap),
                  pl.BlockSpec((None, tq, 128), lambda p, t, it, jt: (p, it[t], 0)),
                  pl.BlockSpec((8, 128), lambda p, t, it, jt: (0, 0))],
        out_specs=[pl.BlockSpec((T, 128), lambda p, t, it, jt: (0, p)),
                   pl.BlockSpec((None, T, 128), lambda p, t, it, jt: (p, 0, 0)),
                   pl.BlockSpec((tk, 128), lambda p, t, it, jt: (jt[t], p)),
                   pl.BlockSpec((None, tk, 128), lambda p, t, it, jt: (p, jt[t], 0)),
                   pl.BlockSpec((tk, 128), lambda p, t, it, jt: (jt[t], p))],
        scratch_shapes=[pltpu.VMEM((2, T, 128), f32), pltpu.VMEM((2, 128, tk), f32), pltpu.VMEM((128, tk), f32),
                        pltpu.VMEM((2, tq, tk), f32), pltpu.VMEM((2, tq, tk), f32), pltpu.VMEM((2, tq, tk), bf16),
                        pltpu.VMEM((2, tq, tk), bf16)],
    )
    return pl.pallas_call(
        body, name="fox_attn_bwd", grid_spec=grid_spec,
        out_shape=[jax.ShapeDtypeStruct((T, AW), bf16), jax.ShapeDtypeStruct((PAIRS, T, 128), f32),
                   jax.ShapeDtypeStruct((T, AW), bf16), jax.ShapeDtypeStruct((PAIRS, T, 128), f32),
                   jax.ShapeDtypeStruct((T, AW), bf16)],
        compiler_params=_cp("parallel", "arbitrary"),
    )(it, jt, qkv, qkv, qkv, qaug, kaug, attn, dattn, lse, dep)


def _fox_cumsum_bwd(dqs, dks, fl, bfp):
    tb = CUMSUM_ROWS
    nb = T // tb

    def body(dqs_ref, dks_ref, fl_ref, b_ref, df_ref, db_ref, carry):
        i = pl.program_id(0)

        @pl.when(i == 0)
        def _():
            carry[...] = jnp.zeros_like(carry)
            db_ref[...] = jnp.zeros_like(db_ref)

        r = lax.broadcasted_iota(jnp.int32, (128, 128), 0)
        cc = lax.broadcasted_iota(jnp.int32, (128, 128), 1)
        pick = lambda even_lane, odd_lane, p: jnp.logical_or(
            jnp.logical_and(r == even_lane, cc == 2 * p), jnp.logical_and(r == odd_lane, cc == 2 * p + 1)).astype(bf16)
        dc = jnp.zeros((tb, 128), f32)
        for p in range(PAIRS):
            rows_at = pick(SPARE[0] + ROW_SUM_LANE, SPARE[1] + ROW_SUM_LANE, p)
            cols_at = pick(SPARE[0] + COL_SUM_LANE, SPARE[1] + COL_SUM_LANE, p)
            dc = dc + _dot01(rows_at, dqs_ref[p], False) - _dot01(cols_at, dks_ref[p], False)
        rt = lax.broadcasted_iota(jnp.int32, (tb, tb), 0)
        ct = lax.broadcasted_iota(jnp.int32, (tb, tb), 1)
        utri = (ct >= rt).astype(bf16)
        dl = _dot01(utri, dc, True) + carry[0:1, :]
        carry[...] = jnp.broadcast_to(dl[0:1, :], (8, 128))
        z = fl_ref[...] + b_ref[...]
        df = dl * jax.nn.sigmoid(-z)
        df_ref[...] = df.astype(bf16)
        db_ref[...] += jnp.sum(df, axis=0, keepdims=True)

    rev = lambda i: (nb - 1 - i, 0)
    return pl.pallas_call(
        body, name="fox_cumsum_bwd", grid=(nb,),
        in_specs=[pl.BlockSpec((PAIRS, tb, 128), lambda i: (0, nb - 1 - i, 0)),
                  pl.BlockSpec((PAIRS, tb, 128), lambda i: (0, nb - 1 - i, 0)),
                  pl.BlockSpec((tb, 128), rev), _full((1, 128))],
        out_specs=[pl.BlockSpec((tb, 128), rev), _full((1, 128))],
        out_shape=[jax.ShapeDtypeStruct((T, 128), bf16), jax.ShapeDtypeStruct((1, 128), f32)],
        scratch_shapes=[pltpu.VMEM((8, 128), f32)],
        compiler_params=_cp("arbitrary"),
    )(dqs, dks, fl, bfp)


def _inproj_bwd(dq, dk, dv, du, df, w, x, dx1, g1):
    tm = 512

    def body(dq_ref, dk_ref, dv_ref, du_ref, df_ref, w_ref, x_ref, dx1_ref, g_ref, dx_ref, dn_ref):
        i = pl.program_id(0)

        @pl.when(i == 0)
        def _():
            dn_ref[...] = jnp.zeros_like(dn_ref)

        dproj = jnp.concatenate([dq_ref[...], dk_ref[...], dv_ref[...], du_ref[...], df_ref[...]], axis=1)
        dh = jnp.dot(dproj, w_ref[...], preferred_element_type=f32)
        xv = x_ref[...]
        r = lax.rsqrt(jnp.mean(xv * xv, axis=-1, keepdims=True) + EPS)
        xhat = xv * r
        dn_ref[...] += jnp.sum(dh * xhat, axis=0, keepdims=True)
        z = dh * g_ref[...]
        dx_ref[...] = dx1_ref[...] + r * (z - xhat * jnp.mean(z * xhat, axis=-1, keepdims=True))

    row = lambda i: (i, 0)
    return pl.pallas_call(
        body, name="inproj_bwd", grid=(T // tm,),
        in_specs=[pl.BlockSpec((tm, AW), row)] * 4 + [pl.BlockSpec((tm, 128), row), _full((W_ROWS, D)),
                                                       pl.BlockSpec((tm, D), row), pl.BlockSpec((tm, D), row), _full((1, D))],
        out_specs=[pl.BlockSpec((tm, D), row), _full((1, D))],
        out_shape=[jax.ShapeDtypeStruct((T, D), f32), jax.ShapeDtypeStruct((1, D), f32)],
        compiler_params=_cp("arbitrary"),
    )(dq, dk, dv, du, df, w, x, dx1, g1)


def _adamw_math(w, g, m, v):
    m = B1 * m + (1.0 - B1) * g
    v = B2 * v + (1.0 - B2) * (g * g)
    m_hat = m / (1.0 - B1 ** STEP)
    v_hat = v / (1.0 - B2 ** STEP)
    delta = -LR * (m_hat / (jnp.sqrt(v_hat) + AEPS) + WD * w)
    return delta, m, v


def _adamw_shard(w, m, v, p_mine, p_other, name):
    rows, cols = w.shape
    tr = rows if rows <= IN_S else rows // 2

    def body(w_ref, m_ref, v_ref, a_ref, b_ref, g_ref, d_ref, nm_ref, nv_ref):
        g = a_ref[...].astype(f32) + b_ref[...].astype(f32)
        g_ref[...] = g
        d_ref[...], nm_ref[...], nv_ref[...] = _adamw_math(w_ref[...], g, m_ref[...], v_ref[...])

    spec = pl.BlockSpec((tr, cols), lambda i: (i, 0))
    return pl.pallas_call(
        body, name=name, grid=(rows // tr,), in_specs=[spec] * 5, out_specs=[spec] * 4,
        out_shape=[jax.ShapeDtypeStruct((rows, cols), f32)] * 4, compiler_params=_cp("parallel"),
    )(w, m, v, p_mine, p_other)


SMALL_SLOTS = ((0, 8, 128), (8, 16, 128), (16, 24, 128), (24, 28, 128), (32, 33, 8))
LOSS_ROW = 39


def _adamw_small(ws, ms, vs, parts, parts_wp):
    n = len(ws)

    def body(*refs):
        w_refs, m_refs, v_refs = refs[0:n], refs[n:2 * n], refs[2 * n:3 * n]
        p_ref, pw_ref = refs[3 * n], refs[3 * n + 1]
        outs = refs[3 * n + 2:]
        g_all = p_ref[0]
        g_wp = pw_ref[0]
        for k in range(1, 8):
            g_all = g_all + p_ref[k]
            g_wp = g_wp + pw_ref[k]
        grads = [g_all[r0:r1, 0:lanes] for r0, r1, lanes in SMALL_SLOTS] + [g_wp]
        for idx, g in enumerate(grads):
            d, nm, nv = _adamw_math(w_refs[idx][...], g, m_refs[idx][...], v_refs[idx][...])
            outs[idx][...] = g
            outs[n + idx][...] = d
            outs[2 * n + idx][...] = nm
            outs[3 * n + idx][...] = nv
        outs[4 * n][...] = g_all[LOSS_ROW:LOSS_ROW + 1, :]

    shapes = [jax.ShapeDtypeStruct(w.shape, f32) for w in ws]
    res = pl.pallas_call(
        body, name="adamw_small", out_shape=shapes * 4 + [jax.ShapeDtypeStruct((1, 128), f32)],
    )(*ws, *ms, *vs, parts, parts_wp)
    return res[:4 * n], res[4 * n]


def _sum4(recv, g, mine, name):
    _, rows, cols = recv.shape
    tr = rows if rows <= IN_S else rows // 2

    def body(mine_ref, r_ref, g_ref, o_ref):
        o_ref[...] = ((g_ref[...].astype(f32) + r_ref[0].astype(f32))
                      + (r_ref[1].astype(f32) + r_ref[2].astype(f32))).astype(bf16)

    grid_spec = pltpu.PrefetchScalarGridSpec(
        num_scalar_prefetch=1, grid=(rows // tr,),
        in_specs=[pl.BlockSpec((3, tr, cols), lambda i, m: (0, i, 0)),
                  pl.BlockSpec((None, tr, cols), lambda i, m: (m[0], i, 0))],
        out_specs=pl.BlockSpec((tr, cols), lambda i, m: (i, 0)))
    return pl.pallas_call(
        body, name=name, grid_spec=grid_spec, out_shape=jax.ShapeDtypeStruct((rows, cols), bf16),
        compiler_params=_cp("arbitrary"),
    )(mine, recv, g)


_HBM = pl.BlockSpec(memory_space=pltpu.HBM)
_SEM = pl.BlockSpec(memory_space=pltpu.SEMAPHORE)
_EFFECT = pltpu.SideEffectType.DATAFLOW_SIDE_EFFECTING


def _in_hbm(a):
    return pltpu.with_memory_space_constraint(a, pltpu.HBM)


def _mesh_pos():
    return lax.axis_index("x"), lax.axis_index("y"), lax.axis_index("c")


def _other_chips(x, y):
    return [(1 - x, y), (x, 1 - y), (1 - x, 1 - y)]


def _gather_copy(srcs, lands, send_sems, recv_sems, a, k, slot):
    x, y, c = _mesh_pos()
    cx, cy = _other_chips(x, y)[k]
    return pltpu.make_async_remote_copy(
        src_ref=srcs[a], dst_ref=lands[a].at[slot], send_sem=send_sems.at[3 * a + k], recv_sem=recv_sems.at[3 * a + k],
        device_id=(cx, cy, c), device_id_type=MESH)


def _scatter_copy(srcs, lands, send_sems, recv_sems, a, k):
    x, y, c = _mesh_pos()
    cx, cy = _other_chips(x, y)[k]
    return pltpu.make_async_remote_copy(
        src_ref=srcs[a].at[2 * cx + cy], dst_ref=lands[a].at[k], send_sem=send_sems.at[3 * a + k],
        recv_sem=recv_sems.at[3 * a + k], device_id=(cx, cy, c), device_id_type=MESH)


def _all_gather_w_in(part):
    cols = part.shape[1] // 2

    def body(src, dst, send_sems, recv_sems, loc_sem):
        x, y, c = _mesh_pos()
        mine = 2 * x + y
        chips = _other_chips(x, y)
        half = lambda ref, cc: ref.at[:, pl.ds(pl.multiple_of(cc * cols, cols), cols)]

        def over_ici(k, slot):
            cx, cy = chips[k]
            return pltpu.make_async_remote_copy(
                src_ref=half(src, c), dst_ref=half(dst.at[slot], c), send_sem=send_sems.at[k], recv_sem=recv_sems.at[k],
                device_id=(cx, cy, c), device_id_type=MESH)

        def to_sibling(k, cc):
            slot = 2 * chips[k][0] + chips[k][1]
            return pltpu.make_async_remote_copy(
                src_ref=half(dst.at[slot], cc), dst_ref=half(dst.at[slot], cc), send_sem=send_sems.at[3 + k],
                recv_sem=recv_sems.at[3 + k], device_id=(x, y, 1 - c), device_id_type=MESH)

        local = pltpu.make_async_copy(src, dst.at[mine], loc_sem.at[0])
        local.start()
        first = [over_ici(k, mine) for k in range(3)]
        for cp in first:
            cp.start()
        passed = [to_sibling(k, c) for k in range(3)]
        for k in range(3):
            over_ici(k, 2 * chips[k][0] + chips[k][1]).wait_recv()
            passed[k].start()
        for k in range(3):
            to_sibling(k, 1 - c).wait_recv()
        for cp in first + passed:
            cp.wait_send()
        local.wait()

    return pl.pallas_call(
        body, name="all_gather_w_in", in_specs=[_HBM], out_specs=_HBM,
        out_shape=jax.ShapeDtypeStruct((NSH,) + part.shape, part.dtype),
        scratch_shapes=[pltpu.SemaphoreType.DMA((6,)), pltpu.SemaphoreType.DMA((6,)), pltpu.SemaphoreType.DMA((1,))],
    )(part)


def _split_start(name, srcs, lands, n_sems, plan, dep):
    n, nl = len(srcs), len(lands)

    def body(*refs):
        src_refs, land_refs = refs[:n], refs[n:n + nl]
        send_sems, recv_sems = refs[n + nl + 1], refs[n + nl + 2]
        token = refs[-1]
        sends, _ = plan(src_refs, land_refs, send_sems, recv_sems)
        for cp in sends:
            cp.start()
        token[...] = jnp.zeros_like(token)

    outs = pl.pallas_call(
        body, name=name,
        in_specs=[_HBM] * (n + nl) + [pl.BlockSpec(memory_space=pl.ANY)],
        out_specs=[_SEM, _SEM] + [_HBM] * (n + nl) + [pl.BlockSpec(memory_space=pltpu.VMEM)],
        out_shape=[pltpu.SemaphoreType.DMA((n_sems,)), pltpu.SemaphoreType.DMA((n_sems,))]
        + [pltpu.HBM(a.shape, a.dtype) for a in list(srcs) + list(lands)] + [jax.ShapeDtypeStruct((8, 128), f32)],
        input_output_aliases={i: 2 + i for i in range(n + nl)},
        compiler_params=pltpu.CompilerParams(has_side_effects=_EFFECT),
    )(*[_in_hbm(a) for a in list(srcs) + list(lands)], dep)
    return outs[0], outs[1], list(outs[2:2 + n]), list(outs[2 + n:2 + n + nl]), outs[-1]


def _split_wait(name, send_sems, recv_sems, srcs, lands, after, plan):
    n, nl = len(srcs), len(lands)

    def body(*refs):
        src_refs, land_refs = refs[:n], refs[n:n + nl]
        s_sems, r_sems = refs[n + nl], refs[n + nl + 1]
        sends, recvs = plan(src_refs, land_refs, s_sems, r_sems)
        for cp in recvs:
            cp.wait_recv()
        for cp in sends:
            cp.wait_send()

    outs = pl.pallas_call(
        body, name=name,
        in_specs=[_HBM] * (n + nl) + [_SEM, _SEM, pl.BlockSpec(memory_space=pl.ANY)],
        out_specs=[_HBM] * (n + nl),
        out_shape=[pltpu.HBM(a.shape, a.dtype) for a in list(srcs) + list(lands)],
        input_output_aliases={i: i for i in range(n + nl)},
        compiler_params=pltpu.CompilerParams(has_side_effects=_EFFECT),
    )(*srcs, *lands, send_sems, recv_sems, after)
    return list(outs[:n]), list(outs[n:])


def _gather_plan(srcs, lands, ss, rs):
    x, y, _ = _mesh_pos()
    chips = _other_chips(x, y)
    sends = [_gather_copy(srcs, lands, ss, rs, a, k, 2 * x + y) for a in range(len(srcs)) for k in range(3)]
    recvs = [_gather_copy(srcs, lands, ss, rs, a, k, 2 * chips[k][0] + chips[k][1])
             for a in range(len(srcs)) for k in range(3)]
    return sends, recvs


def _scatter_plan(srcs, lands, ss, rs):
    cps = [_scatter_copy(srcs, lands, ss, rs, a, k) for a in range(len(srcs)) for k in range(3)]
    return cps, cps


def _scatter_and_spread_plan(srcs, lands, ss, rs):
    x, y, c = _mesh_pos()
    me = 4 * x + 2 * y + c
    n = len(srcs) - 1
    cps = [_scatter_copy(srcs[:n], lands[:n], ss, rs, a, k) for a in range(n) for k in range(3)]
    for f in range(1, 8):
        peer = ((x + (f >> 2)) % 2, (y + ((f >> 1) & 1)) % 2, (c + (f & 1)) % 2)
        cps.append(pltpu.make_async_remote_copy(
            src_ref=srcs[n], dst_ref=lands[n].at[me], send_sem=ss.at[3 * n - 1 + f], recv_sem=rs.at[3 * n - 1 + f],
            device_id=peer, device_id_type=MESH))
    return cps, cps


def _swap_with_sibling(parts, name):
    n = len(parts)

    def body(*refs):
        srcs, dsts = refs[:n], refs[n:2 * n]
        send_sems, recv_sems = refs[2 * n:]
        x, y, c = _mesh_pos()
        cps = [pltpu.make_async_remote_copy(src_ref=srcs[a], dst_ref=dsts[a], send_sem=send_sems.at[a],
                                            recv_sem=recv_sems.at[a], device_id=(x, y, 1 - c), device_id_type=MESH)
               for a in range(n)]
        for cp in cps:
            cp.start()
        for cp in cps:
            cp.wait_recv()
        for cp in cps:
            cp.wait_send()

    return pl.pallas_call(
        body, name=name, in_specs=[_HBM] * n, out_specs=[_HBM] * n,
        out_shape=[jax.ShapeDtypeStruct(p.shape, p.dtype) for p in parts],
        scratch_shapes=[pltpu.SemaphoreType.DMA((n,)), pltpu.SemaphoreType.DMA((n,))],
    )(*parts)


def _forward(x, tgt, w_in_t, mlp_w_fn, g1, bfp, wp, scale, g2, gf, dep):
    h, qkv, u, fl = _rms_inproj(x, g1, w_in_t, dep)
    qaug, kaug = _fox_cumsum(fl, bfp)
    attn, lse = _attn_fwd(qkv, qaug, kaug)
    pooled, pool = _pool_fwd(u, wp, scale)
    wo, wgt, wut, wd = mlp_w_fn(attn)
    x1, h2 = _outproj(x, attn, pool, wo, g2)
    loss, dgf, dx2, dx2b, ud, silu, a_b = _mlp_fwd_loss(h2, x1, wgt, wut, wd, tgt, gf)
    saved = dict(h=h, qkv=qkv, fl=fl, qaug=qaug, kaug=kaug, attn=attn, lse=lse, pooled=pooled, pool=pool, x1=x1, h2=h2,
                 ud=ud, silu=silu, a_b=a_b, wo=wo, wgt=wgt, wut=wut, wd=wd)
    return loss, dgf, dx2, dx2b, saved


def _backward_mlp(sv, dx2, dx2b, g2):
    dgate, dup, dx1, dx1b, dg2 = _mlp_bwd(dx2b, dx2, sv["ud"], sv["silu"], sv["wgt"], sv["wut"], sv["wd"], sv["x1"], g2)
    (dwd,) = _mm_tn(sv["a_b"], [dx2b], "dw_down", a_sharded=True, tk=T)
    (dwgt,) = _mm_tn(dgate, [sv["h2"]], "dw_gate", a_sharded=True, tk=T)
    (dwut,) = _mm_tn(dup, [sv["h2"]], "dw_up", a_sharded=True, tk=T)
    return dx1, dx1b, dg2, (dwgt, dwut, dwd)


def _backward_outproj_pool(sv, dx1b, wp, scale):
    dattn, dpool = _outproj_bwd(dx1b, sv["wo"])
    dwo_a, = _mm_tn(sv["attn"], [dx1b], "dw_out_attn", tk=2048)
    dwo_p, = _mm_tn(sv["pool"], [dx1b], "dw_out_pool", tk=2048)
    dwo = jnp.concatenate([dwo_a, dwo_p], axis=0).reshape(NSH, D // NSH, D)
    du, dscale, dwp = _pool_bwd(dpool, sv["pooled"], wp, scale)
    return dattn, dwo, du, dscale, dwp


def _backward_attn_inproj(sv, x, dx1, dattn, du, w_in_t, g1, bfp, dep):
    dq, dqs, dk, dks, dv = _attn_bwd(sv["qkv"], sv["qaug"], sv["kaug"], sv["attn"], dattn, sv["lse"], dep)
    df, dbf = _fox_cumsum_bwd(dqs, dks, sv["fl"], bfp)
    dx, dg1 = _inproj_bwd(dq, dk, dv, du, df, w_in_t, x, dx1, g1)
    dwq, dwk, dwv, dwu_in, dwf = _mm_tn_rows([dq, dk, dv, du, df], sv["h"], "dw_in")
    dwin = jnp.concatenate([dwq, dwk, dwv, dwf[0:8], dwu_in], axis=0)
    return dx, dg1, dbf, dwin.reshape(NSH, IN_S, D)


def kernel(x, norm1_g, w_in, b_forget, w_pool, pool_scale, w_out, norm2_g, w_gate, w_up, w_down, final_g, loss_target, m_norm1_g, m_w_in, m_b_forget, m_w_pool, m_pool_scale, m_w_out, m_norm2_g, m_w_gate, m_w_up, m_w_down, m_final_g, v_norm1_g, v_w_in, v_b_forget, v_w_pool, v_pool_scale, v_w_out, v_norm2_g, v_w_gate, v_w_up, v_w_down, v_final_g):
    mine = (2 * lax.axis_index("x") + lax.axis_index("y")).astype(jnp.int32)
    mine1 = mine.reshape(1)
    tr = lambda a: jnp.transpose(a[0])

    win4 = _all_gather_w_in(tr(w_in).astype(bf16))
    later = [w_out[0].astype(bf16), tr(w_gate).astype(bf16), tr(w_up).astype(bf16), w_down[0].astype(bf16)]
    lands = [lax.dynamic_update_slice(lax.empty((NSH,) + p.shape, bf16), p[None], (mine, 0, 0)) for p in later]
    ag_send, ag_recv, later_thru, lands_thru, ag_token = _split_start("all_gather_start", later, lands, 12, _gather_plan,
                                                                      win4)
    win = win4.reshape(IN_W, D)
    w_in_t = jnp.concatenate([win[0:3 * AW], win[3 * AW + 8:], win[3 * AW:3 * AW + 8], jnp.zeros((120, D), bf16)], axis=0)
    bfp = jnp.pad(b_forget, ((0, 0), (0, 120)))
    wp = w_pool[0].astype(bf16)
    gf = final_g.reshape(1, D)

    def later_weights(after):
        _, (wo4, wgt, wut, wd) = _split_wait("all_gather_wait", ag_send, ag_recv, later_thru, lands_thru, after, _gather_plan)
        return wo4.reshape(D, D), wgt, wut, wd

    xe, tgt = x[0], loss_target[0]
    loss_v, dgf, dx2, dx2b, sv = _forward(xe, tgt, w_in_t, later_weights, norm1_g, bfp, wp, pool_scale, norm2_g, gf, ag_token)
    dx1, dx1b, dg2, mlp_grads = _backward_mlp(sv, dx2, dx2b, norm2_g)
    dattn, dwo, du, dscale, dwp = _backward_outproj_pool(sv, dx1b, wp, pool_scale)
    me = (4 * lax.axis_index("x") + 2 * lax.axis_index("y") + lax.axis_index("c")).astype(jnp.int32)
    dwp = dwp.reshape(512, 128)
    first = [dwo] + list(mlp_grads) + [dwp]
    first_lands = [lax.empty((3,) + g.shape[1:], bf16) for g in first[:4]]
    first_lands.append(lax.dynamic_update_slice(lax.empty((8, 512, 128), f32), dwp[None], (me, 0, 0)))
    rs_send, rs_recv, first_thru, first_lands_thru, rs_token = _split_start(
        "reduce_scatter_start", first, first_lands, 19, _scatter_and_spread_plan, du)
    dx, dg1, dbf, dwin = _backward_attn_inproj(sv, xe, dx1, dattn, du, w_in_t, norm1_g, bfp, rs_token)

    pad8 = lambda r: jnp.pad(r, ((0, 8 - r.shape[0]), (0, 0)))
    loss_rows = jnp.concatenate([dbf, jnp.zeros((6, 128), f32), loss_v[0:1, :]], axis=0)
    small = jnp.concatenate([dg1.reshape(8, 128), dg2.reshape(8, 128), dgf.reshape(8, 128), pad8(dscale.reshape(4, 128)),
                             loss_rows], axis=0)
    small_land = lax.dynamic_update_slice(lax.empty((8, SMALL_ROWS, 128), f32), small[None], (me, 0, 0))
    tail_send, tail_recv, tail_thru, tail_lands_thru, tail_token = _split_start(
        "tail_start", [dwin, small], [lax.empty((3,) + dwin.shape[1:], bf16), small_land], 10, _scatter_and_spread_plan,
        dx)
    first_thru, first_recv = _split_wait("reduce_scatter_wait", rs_send, rs_recv, first_thru, first_lands_thru, tail_token,
                                         _scatter_and_spread_plan)
    wp_all = first_recv[4]
    ws = [tr(w_in), w_out[0], tr(w_gate), tr(w_up), w_down[0]]
    ms = [tr(m_w_in), m_w_out[0], tr(m_w_gate), tr(m_w_up), m_w_down[0]]
    vs = [tr(v_w_in), v_w_out[0], tr(v_w_gate), tr(v_w_up), v_w_down[0]]
    partial = [_sum4(r, g, mine1, f"sum4_{i + 1}") for i, (r, g) in enumerate(zip(first_recv[:4], first_thru[:4]))]
    other = _swap_with_sibling(partial, "swap_first")
    big = [_adamw_shard(ws[i + 1], ms[i + 1], vs[i + 1], partial[i], other[i], f"adamw_{i + 1}") for i in range(4)]
    (dwin_thru, _), (in_recv_land, small_all) = _split_wait("tail_wait", tail_send, tail_recv, tail_thru, tail_lands_thru,
                                                            big[3][0], _scatter_and_spread_plan)
    partial_in = _sum4(in_recv_land, dwin_thru, mine1, "sum4_0")
    (other_in,) = _swap_with_sibling([partial_in], "swap_in")
    big = [_adamw_shard(ws[0], ms[0], vs[0], partial_in, other_in, "adamw_0")] + big

    small_names = ["norm1_g", "norm2_g", "final_g", "pool_scale", "b_forget", "w_pool"]
    rows = lambda a, b, c, d, e, f: [a.reshape(8, 128), b.reshape(8, 128), c.reshape(8, 128), d.reshape(4, 128),
                                     e.reshape(1, 8), f.reshape(512, 128)]
    sm, loss_row = _adamw_small(rows(norm1_g, norm2_g, final_g, pool_scale, b_forget, w_pool),
                                rows(m_norm1_g, m_norm2_g, m_final_g, m_pool_scale, m_b_forget, m_w_pool),
                                rows(v_norm1_g, v_norm2_g, v_final_g, v_pool_scale, v_b_forget, v_w_pool), small_all, wp_all)
    small_shape = dict(norm1_g=(1, D), norm2_g=(1, D), final_g=(D,), pool_scale=(1, AW), b_forget=(1, 8),
                       w_pool=(1, 4, 128, 128))

    order = ["norm1_g", "w_in", "b_forget", "w_pool", "pool_scale", "w_out", "norm2_g", "w_gate", "w_up", "w_down", "final_g"]
    big_idx = {"w_in": 0, "w_out": 1, "w_gate": 2, "w_up": 3, "w_down": 4}
    outs = [loss_row[0, 0], dx[None]]
    for kind in range(4):
        for name in order:
            if name in ("w_in", "w_gate", "w_up"):
                outs.append(jnp.transpose(big[big_idx[name]][kind])[None])
            elif name in big_idx:
                outs.append(big[big_idx[name]][kind][None])
            else:
                outs.append(sm[6 * kind + small_names.index(name)].reshape(small_shape[name]))
    return tuple(outs)
```

```python
import jax
import jax.numpy as jnp
import numpy as np
from jax import lax
from jax.experimental import pallas as pl
from jax.experimental.pallas import tpu as pltpu

f32 = jnp.float32
bf16 = jnp.bfloat16

T = 4096
D = 1024
NSH = 4
IN_W = 2056
IN_S = IN_W // NSH
AW = 512
PAIRS = 4
SPARE = (64, 0)
ROW_SUM_LANE, COL_SUM_LANE = 0, 3
FF = 2816
FS = FF // NSH
WINDOWS = (2, 4, 8, 16)
HALO = 16
EPS = 1e-6
NEG = -1e30
LR, B1, B2, AEPS, WD, STEP = 0.001, 0.9, 0.999, 1e-08, 0.01, 10
SMALL_ROWS = 40

NT = (((1,), (1,)), ((), ()))
TN = (((0,), (0,)), ((), ()))

MESH = pl.DeviceIdType.MESH


def _cp(*sem):
    return pltpu.CompilerParams(dimension_semantics=sem)


def _full(shape):
    n = len(shape)
    return pl.BlockSpec(shape, lambda *_: (0,) * n)


def _resident(shape):
    n = len(shape)
    return pl.BlockSpec(shape, lambda *_: (0,) * n, pipeline_mode=pl.Buffered(1))


W_ROWS = 4 * AW + 128


def _rms_inproj(x, g1, w, dep):
    tm = 512

    def body(x_ref, g_ref, w_ref, dep_ref, h_ref, qkv_ref, u_ref, fl_ref):
        xv = x_ref[...]
        r = lax.rsqrt(jnp.mean(xv * xv, axis=-1, keepdims=True) + EPS)
        h = (xv * r * g_ref[...]).astype(bf16)
        h_ref[...] = h
        qkv_ref[...] = lax.dot_general(h, w_ref[0:3 * AW, :], NT, preferred_element_type=f32).astype(bf16)
        u_ref[...] = lax.dot_general(h, w_ref[3 * AW:4 * AW, :], NT, preferred_element_type=f32)
        fl_ref[...] = lax.dot_general(h, w_ref[4 * AW:W_ROWS, :], NT, preferred_element_type=f32)

    return pl.pallas_call(
        body, name="rms_inproj", grid=(T // tm,),
        in_specs=[pl.BlockSpec((tm, D), lambda i: (i, 0)), _full((1, D)), _full((W_ROWS, D)), _full((8, 128))],
        out_specs=[pl.BlockSpec((tm, D), lambda i: (i, 0)), pl.BlockSpec((tm, 3 * AW), lambda i: (i, 0)),
                   pl.BlockSpec((tm, AW), lambda i: (i, 0)), pl.BlockSpec((tm, 128), lambda i: (i, 0))],
        out_shape=[jax.ShapeDtypeStruct((T, D), bf16), jax.ShapeDtypeStruct((T, 3 * AW), bf16),
                   jax.ShapeDtypeStruct((T, AW), f32), jax.ShapeDtypeStruct((T, 128), f32)],
        compiler_params=_cp("parallel"),
    )(x, g1, w, dep)


CUMSUM_ROWS = 512
FS_CHUNKS = ((0, 256), (256, 512), (512, FS))


def _log_sigmoid(z):
    return jnp.minimum(z, 0.0) - jnp.log(1.0 + jnp.exp(-jnp.abs(z)))


def _split3(x):
    hi = x.astype(bf16)
    r1 = x - hi.astype(f32)
    mid = r1.astype(bf16)
    return hi, mid, (r1 - mid.astype(f32)).astype(bf16)


def _dot01(sel, x, sel_first):
    parts = _split3(x)
    if sel_first:
        return sum(jnp.dot(sel, p, preferred_element_type=f32) for p in parts)
    return sum(jnp.dot(p, sel, preferred_element_type=f32) for p in parts)


def _fox_cumsum(fl, bfp):
    tb = CUMSUM_ROWS
    nb = T // tb

    def body(fl_ref, b_ref, qa_ref, ka_ref, carry):
        i = pl.program_id(0)

        @pl.when(i == 0)
        def _():
            carry[...] = jnp.zeros_like(carry)

        lf = _log_sigmoid(fl_ref[...] + b_ref[...])
        r = lax.broadcasted_iota(jnp.int32, (tb, tb), 0)
        cc = lax.broadcasted_iota(jnp.int32, (tb, tb), 1)
        ltri = (cc <= r).astype(bf16)
        cb = _dot01(ltri, lf, True) + carry[0:1, :]
        carry[...] = jnp.broadcast_to(cb[tb - 1:tb, :], (8, 128))
        hi, mid, lo = _split3(cb * LOG2E)
        head = lax.broadcasted_iota(jnp.int32, (128, AW), 0)
        col = lax.broadcasted_iota(jnp.int32, (128, AW), 1)
        base = 128 * (head >> 1) + jnp.where((head & 1) == 0, SPARE[0], SPARE[1])
        place = lambda off: jnp.logical_and(col == base + off, head < 8).astype(bf16)
        mm = lambda a, off: jnp.dot(a, place(off), preferred_element_type=f32)
        cq = mm(hi, 0) + mm(mid, 1) + mm(lo, 2)
        ck = mm(hi, 3) + mm(mid, 4) + mm(lo, 5)
        within = jnp.bitwise_and(lax.broadcasted_iota(jnp.int32, (tb, AW), 1), 63)
        qa_ref[...] = jnp.where(jnp.logical_and(within >= 3, within <= 5), 1.0, cq).astype(bf16)
        ka_ref[...] = jnp.where(within <= 2, 1.0, -ck).astype(bf16)

    return pl.pallas_call(
        body, name="fox_cumsum", grid=(nb,),
        in_specs=[pl.BlockSpec((tb, 128), lambda i: (i, 0)), _full((1, 128))],
        out_specs=[pl.BlockSpec((tb, AW), lambda i: (i, 0)), pl.BlockSpec((tb, AW), lambda i: (i, 0))],
        out_shape=[jax.ShapeDtypeStruct((T, AW), bf16), jax.ShapeDtypeStruct((T, AW), bf16)],
        scratch_shapes=[pltpu.VMEM((8, 128), f32)],
        compiler_params=_cp("arbitrary"),
    )(fl, bfp)


ATT_T = 512
LOG2E = 1.4426950408889634
Q_SCALE = 0.125 * LOG2E


def _causal_steps(key_major):
    n = T // ATT_T
    if key_major:
        pairs = [(i, j) for j in range(n) for i in range(j, n)]
    else:
        pairs = [(i, j) for i in range(n) for j in range(i + 1)]
    it = np.array([p[0] for p in pairs], np.int32)
    jt = np.array([p[1] for p in pairs], np.int32)
    return jnp.asarray(it), jnp.asarray(jt)


def _row_blocks(tq, tk, on_diagonal):
    return ((0, tq // 2, tk // 2), (tq // 2, tq, tk)) if on_diagonal else ((0, tq, tk),)


def _attn_fwd(qkv, qaug, kaug):
    tq = tk = ATT_T
    it, jt = _causal_steps(False)
    nsteps = it.shape[0]

    rs = 64

    def body(it_ref, jt_ref, q_ref, k_ref, v_ref, qa_ref, ka_ref, o_ref, lse_ref, m_sc, acc_sc, s_sc, p_sc, alpha_sc):
        t = pl.program_id(1)
        i = it_ref[t]
        j = jt_ref[t]

        @pl.when(j == 0)
        def _():
            m_sc[...] = jnp.full_like(m_sc, NEG)
            acc_sc[...] = jnp.zeros_like(acc_sc)

        lane = lax.broadcasted_iota(jnp.int32, (tq, 128), 1)
        spare = SPARE

        def step(on_diagonal):
            q = (q_ref[...].astype(f32) * Q_SCALE).astype(bf16)
            k = k_ref[...]
            v = v_ref[...]
            qa = qa_ref[...]
            ka = ka_ref[...]
            blocks = _row_blocks(tq, tk, on_diagonal)
            for e in range(2):
                hm = (lane >= 64) if e else (lane < 64)
                qe = jnp.where(hm, q, qa)
                ke = jnp.where(hm, k, ka)
                for r0, r1, nc in blocks:
                    s_sc[e, r0:r1, 0:nc] = lax.dot_general(qe[r0:r1], ke[0:nc], NT, preferred_element_type=f32)
            for e in range(2):
                for r0, r1, nc in blocks:
                    for r in range(r0, r1, rs):
                        s = s_sc[e, r:r + rs, 0:nc]
                        if on_diagonal:
                            row = lax.broadcasted_iota(jnp.int32, (rs, nc), 0) + r
                            col = lax.broadcasted_iota(jnp.int32, (rs, nc), 1)
                            s = jnp.where(col <= row, s, NEG)
                        m_prev = m_sc[e, r:r + rs, :]
                        m_new = jnp.maximum(m_prev, jnp.max(s, axis=1, keepdims=True))
                        p_sc[e, r:r + rs, 0:nc] = jnp.exp2(s - jnp.tile(m_new, (1, nc // 128))).astype(bf16)
                        alpha_sc[e, r:r + rs, :] = jnp.exp2(m_prev - m_new)
                        m_sc[e, r:r + rs, :] = m_new
            for e in range(2):
                hm = (lane >= 64) if e else (lane < 64)
                ve = jnp.where(hm, v, (lane == spare[e]).astype(bf16))
                for r0, r1, nc in blocks:
                    acc_sc[e, r0:r1] = (alpha_sc[e, r0:r1] * acc_sc[e, r0:r1]
                                        + jnp.dot(p_sc[e, r0:r1, 0:nc], ve[0:nc], preferred_element_type=f32))

        @pl.when(j < i)
        def _():
            step(False)

        @pl.when(j == i)
        def _():
            step(True)
            l0 = acc_sc[0][:, spare[0]:spare[0] + 1]
            l1 = acc_sc[1][:, spare[1]:spare[1] + 1]
            o_ref[...] = jnp.where(lane < 64, acc_sc[0] / l0, acc_sc[1] / l1).astype(bf16)
            lse_ref[...] = jnp.where(lane < 64, m_sc[0] + jnp.log2(l0), m_sc[1] + jnp.log2(l1))

    qmap = lambda p, t, it, jt: (it[t], p)
    kmap = lambda p, t, it, jt: (jt[t], p)
    grid_spec = pltpu.PrefetchScalarGridSpec(
        num_scalar_prefetch=2, grid=(PAIRS, nsteps),
        in_specs=[pl.BlockSpec((tq, 128), qmap),
                  pl.BlockSpec((tk, 128), lambda p, t, it, jt: (jt[t], PAIRS + p)),
                  pl.BlockSpec((tk, 128), lambda p, t, it, jt: (jt[t], 2 * PAIRS + p)),
                  pl.BlockSpec((tq, 128), qmap), pl.BlockSpec((tk, 128), kmap)],
        out_specs=[pl.BlockSpec((tq, 128), qmap),
                   pl.BlockSpec((None, tq, 128), lambda p, t, it, jt: (p, it[t], 0))],
        scratch_shapes=[pltpu.VMEM((2, tq, 128), f32), pltpu.VMEM((2, tq, 128), f32), pltpu.VMEM((2, tq, tk), f32),
                        pltpu.VMEM((2, tq, tk), bf16), pltpu.VMEM((2, tq, 128), f32)],
    )
    return pl.pallas_call(
        body, name="fox_attn_fwd", grid_spec=grid_spec,
        out_shape=[jax.ShapeDtypeStruct((T, AW), bf16), jax.ShapeDtypeStruct((PAIRS, T, 128), f32)],
        compiler_params=_cp("parallel", "arbitrary"),
    )(it, jt, qkv, qkv, qkv, qaug, kaug)


def _pool_fwd(u, wp, scale):
    tm = 1024

    def body(u_ref, wp_ref, sc_ref, pooled_ref, pool_ref, ext):
        i = pl.program_id(0)

        @pl.when(i == 0)
        def _():
            ext[0:HALO, :] = jnp.zeros((HALO, AW), f32)

        uv = u_ref[...]
        ext[HALO:HALO + tm, :] = uv
        t_idx = i * tm + lax.broadcasted_iota(jnp.int32, (tm, 1), 0)
        for g, w in enumerate(WINDOWS):
            lo, hi = 128 * g, 128 * (g + 1)
            ug = uv[:, lo:hi]
            acc = ug
            for d in range(1, w):
                acc = acc + ext[HALO - d:HALO - d + tm, lo:hi]
            cnt = jnp.minimum(t_idx + 1, w).astype(f32)
            pb = (acc / cnt - ug).astype(bf16)
            pooled_ref[:, lo:hi] = pb
            mixed = jnp.dot(pb, wp_ref[g], preferred_element_type=f32)
            pool_ref[:, lo:hi] = (mixed * sc_ref[:, lo:hi]).astype(bf16)
        ext[0:HALO, :] = uv[tm - HALO:tm, :]

    return pl.pallas_call(
        body, name="pool_fwd", grid=(T // tm,),
        in_specs=[pl.BlockSpec((tm, AW), lambda i: (i, 0)), _full((4, 128, 128)), _full((1, AW))],
        out_specs=[pl.BlockSpec((tm, AW), lambda i: (i, 0)), pl.BlockSpec((tm, AW), lambda i: (i, 0))],
        out_shape=[jax.ShapeDtypeStruct((T, AW), bf16), jax.ShapeDtypeStruct((T, AW), bf16)],
        scratch_shapes=[pltpu.VMEM((tm + HALO, AW), f32)],
        compiler_params=_cp("arbitrary"),
    )(u, wp, scale)


def _outproj(x, attn, pool, wo, g2):
    tm = 1024

    def body(x_ref, a_ref, p_ref, wo_ref, g_ref, x1_ref, h2_ref):
        mixed = jnp.concatenate([a_ref[...], p_ref[...]], axis=1)
        x1 = x_ref[...] + jnp.dot(mixed, wo_ref[...], preferred_element_type=f32)
        x1_ref[...] = x1
        r = lax.rsqrt(jnp.mean(x1 * x1, axis=-1, keepdims=True) + EPS)
        h2_ref[...] = (x1 * r * g_ref[...]).astype(bf16)

    return pl.pallas_call(
        body, name="outproj", grid=(T // tm,),
        in_specs=[pl.BlockSpec((tm, D), lambda i: (i, 0)), pl.BlockSpec((tm, AW), lambda i: (i, 0)),
                  pl.BlockSpec((tm, AW), lambda i: (i, 0)), _full((D, D)), _full((1, D))],
        out_specs=[pl.BlockSpec((tm, D), lambda i: (i, 0)), pl.BlockSpec((tm, D), lambda i: (i, 0))],
        out_shape=[jax.ShapeDtypeStruct((T, D), f32), jax.ShapeDtypeStruct((T, D), bf16)],
        compiler_params=_cp("parallel"),
    )(x, attn, pool, wo, g2)


def _mlp_fwd_loss(h2, x1, wg, wu, wd, tgt, gf):
    tm = 512

    def body(h_ref, x1_ref, wg_ref, wu_ref, wd_ref, t_ref, g_ref,
             loss_ref, dg_ref, dx_ref, dxb_ref, ud_ref, silu_ref, a_ref, x2):
        i = pl.program_id(0)
        s = pl.program_id(1)

        @pl.when(jnp.logical_and(i == 0, s == 0))
        def _():
            loss_ref[...] = jnp.zeros_like(loss_ref)
            dg_ref[...] = jnp.zeros_like(dg_ref)

        h = h_ref[...]
        gus = [(lax.dot_general(h, wg_ref[s, c0:c1, :], NT, preferred_element_type=f32),
                lax.dot_general(h, wu_ref[s, c0:c1, :], NT, preferred_element_type=f32)) for c0, c1 in FS_CHUNKS]
        for (c0, c1), (gate, up) in zip(FS_CHUNKS, gus):
            sg = jax.nn.sigmoid(gate)
            silu = gate * sg
            ud_ref[:, c0:c1] = (up * (sg * (1.0 + gate * (1.0 - sg)))).astype(bf16)
            silu_ref[:, c0:c1] = silu.astype(bf16)
            a_ref[:, c0:c1] = (silu * up).astype(bf16)
        part = jnp.dot(a_ref[...], wd_ref[s], preferred_element_type=f32)

        @pl.when(s == 0)
        def _():
            x2[...] = x1_ref[...] + part

        @pl.when(s > 0)
        def _():
            x2[...] += part

        @pl.when(s == NSH - 1)
        def _():
            xv = x2[...]
            g = g_ref[...]
            r = lax.rsqrt(jnp.mean(xv * xv, axis=-1, keepdims=True) + EPS)
            xhat = xv * r
            e = xhat * g - t_ref[...]
            loss_ref[...] += 0.5 * jnp.sum(jnp.mean(e * e, axis=-1, keepdims=True))
            dy = e * (1.0 / D)
            dg_ref[...] += jnp.sum(dy * xhat, axis=0, keepdims=True)
            z = dy * g
            dx = r * (z - xhat * jnp.mean(z * xhat, axis=-1, keepdims=True))
            dx_ref[...] = dx
            dxb_ref[...] = dx.astype(bf16)

    row = lambda i, s: (i, 0)
    sl = lambda i, s: (s, i, 0)
    wsl = lambda i, s: (s, 0, 0)
    return pl.pallas_call(
        body, name="mlp_fwd_loss", grid=(T // tm, NSH),
        in_specs=[pl.BlockSpec((tm, D), row), pl.BlockSpec((tm, D), row),
                  _resident((NSH, FS, D)), _resident((NSH, FS, D)), _resident((NSH, FS, D)),
                  pl.BlockSpec((tm, D), row), pl.BlockSpec((1, D), lambda i, s: (0, 0))],
        out_specs=[pl.BlockSpec((8, 128), lambda i, s: (0, 0)), pl.BlockSpec((1, D), lambda i, s: (0, 0)),
                   pl.BlockSpec((tm, D), row), pl.BlockSpec((tm, D), row),
                   pl.BlockSpec((None, tm, FS), sl), pl.BlockSpec((None, tm, FS), sl), pl.BlockSpec((None, tm, FS), sl)],
        out_shape=[jax.ShapeDtypeStruct((8, 128), f32), jax.ShapeDtypeStruct((1, D), f32),
                   jax.ShapeDtypeStruct((T, D), f32), jax.ShapeDtypeStruct((T, D), bf16)]
        + [jax.ShapeDtypeStruct((NSH, T, FS), bf16)] * 3,
        scratch_shapes=[pltpu.VMEM((tm, D), f32)],
        compiler_params=_cp("arbitrary", "arbitrary"),
    )(h2, x1, wg, wu, wd, tgt, gf)


def _mlp_bwd(dx2b, dx2, ud, silu, wg, wu, wd, x1, g2):
    tm = 512

    def body(dxb_ref, dx_ref, ud_ref, silu_ref, wg_ref, wu_ref, wd_ref, x1_ref, g_ref,
             dg_ref, du_ref, dx1_ref, dx1b_ref, dn_ref, acc):
        i = pl.program_id(0)
        s = pl.program_id(1)

        @pl.when(jnp.logical_and(i == 0, s == 0))
        def _():
            dn_ref[...] = jnp.zeros_like(dn_ref)

        dxb = dxb_ref[...]
        das = [lax.dot_general(dxb, wd_ref[s, c0:c1, :], NT, preferred_element_type=f32) for c0, c1 in FS_CHUNKS]
        for (c0, c1), da in zip(FS_CHUNKS, das):
            dg_ref[:, c0:c1] = (da * ud_ref[:, c0:c1].astype(f32)).astype(bf16)
            du_ref[:, c0:c1] = (da * silu_ref[:, c0:c1].astype(f32)).astype(bf16)
        part = jnp.dot(dg_ref[...], wg_ref[s], preferred_element_type=f32)
        part = part + jnp.dot(du_ref[...], wu_ref[s], preferred_element_type=f32)

        @pl.when(s == 0)
        def _():
            acc[...] = part

        @pl.when(s > 0)
        def _():
            acc[...] += part

        @pl.when(s == NSH - 1)
        def _():
            xv = x1_ref[...]
            r = lax.rsqrt(jnp.mean(xv * xv, axis=-1, keepdims=True) + EPS)
            xhat = xv * r
            dh = acc[...]
            dn_ref[...] += jnp.sum(dh * xhat, axis=0, keepdims=True)
            z = dh * g_ref[...]
            dx1 = dx_ref[...] + r * (z - xhat * jnp.mean(z * xhat, axis=-1, keepdims=True))
            dx1_ref[...] = dx1
            dx1b_ref[...] = dx1.astype(bf16)

    row = lambda i, s: (i, 0)
    sl = lambda i, s: (s, i, 0)
    wsl = lambda i, s: (s, 0, 0)
    return pl.pallas_call(
        body, name="mlp_bwd", grid=(T // tm, NSH),
        in_specs=[pl.BlockSpec((tm, D), row), pl.BlockSpec((tm, D), row),
                  pl.BlockSpec((None, tm, FS), sl), pl.BlockSpec((None, tm, FS), sl),
                  _resident((NSH, FS, D)), _resident((NSH, FS, D)), _resident((NSH, FS, D)),
                  pl.BlockSpec((tm, D), row), pl.BlockSpec((1, D), lambda i, s: (0, 0))],
        out_specs=[pl.BlockSpec((None, tm, FS), sl), pl.BlockSpec((None, tm, FS), sl),
                   pl.BlockSpec((tm, D), row), pl.BlockSpec((tm, D), row), pl.BlockSpec((1, D), lambda i, s: (0, 0))],
        out_shape=[jax.ShapeDtypeStruct((NSH, T, FS), bf16)] * 2
        + [jax.ShapeDtypeStruct((T, D), f32), jax.ShapeDtypeStruct((T, D), bf16), jax.ShapeDtypeStruct((1, D), f32)],
        scratch_shapes=[pltpu.VMEM((tm, D), f32)],
        compiler_params=_cp("arbitrary", "arbitrary"),
    )(dx2b, dx2, ud, silu, wg, wu, wd, x1, g2)


def _mm_tn(a, bs, name, a_sharded=False, b_sharded=False, tk=512, out_dtype=bf16):
    nb = len(bs)
    sh = NSH if (a_sharded or b_sharded) else 1
    m = a.shape[-1]
    nk = T // tk

    def body(a_ref, *refs):
        kk = pl.program_id(1)
        av = a_ref[...]
        for b_ref, o_ref, acc in zip(refs[:nb], refs[nb:2 * nb], refs[2 * nb:]):
            upd = lax.dot_general(av, b_ref[...], TN, preferred_element_type=f32)

            @pl.when(kk == 0)
            def _():
                acc[...] = upd

            @pl.when(kk > 0)
            def _():
                acc[...] += upd

            @pl.when(kk == nk - 1)
            def _():
                o_ref[...] = acc[...].astype(out_dtype)

    a_spec = (pl.BlockSpec((None, tk, m), lambda s, k: (s, k, 0)) if a_sharded
              else pl.BlockSpec((tk, m), lambda s, k: (k, 0)))
    b_specs, o_specs, o_shapes, scratch = [], [], [], []
    for b in bs:
        n = b.shape[-1]
        b_specs.append(pl.BlockSpec((None, tk, n), lambda s, k: (s, k, 0)) if b_sharded
                       else pl.BlockSpec((tk, n), lambda s, k: (k, 0)))
        scratch.append(pltpu.VMEM((m, n), f32))
        if sh > 1:
            o_specs.append(pl.BlockSpec((None, m, n), lambda s, k: (s, 0, 0)))
            o_shapes.append(jax.ShapeDtypeStruct((sh, m, n), out_dtype))
        else:
            o_specs.append(pl.BlockSpec((m, n), lambda s, k: (0, 0)))
            o_shapes.append(jax.ShapeDtypeStruct((m, n), out_dtype))
    return pl.pallas_call(
        body, name=name, grid=(sh, nk), in_specs=[a_spec] + b_specs, out_specs=o_specs, out_shape=o_shapes,
        scratch_shapes=scratch, compiler_params=_cp("arbitrary", "arbitrary"),
    )(a, *bs)


def _mm_tn_rows(a_list, b, name, tk=1024, out_dtype=bf16):
    na = len(a_list)
    n = b.shape[-1]
    nk = T // tk

    def body(*refs):
        a_refs, b_ref = refs[:na], refs[na]
        o_refs, accs = refs[na + 1:2 * na + 1], refs[2 * na + 1:]
        kk = pl.program_id(0)
        bv = b_ref[...]
        for a_ref, o_ref, acc in zip(a_refs, o_refs, accs):
            upd = lax.dot_general(a_ref[...], bv, TN, preferred_element_type=f32)

            @pl.when(kk == 0)
            def _():
                acc[...] = upd

            @pl.when(kk > 0)
            def _():
                acc[...] += upd

            @pl.when(kk == nk - 1)
            def _():
                o_ref[...] = acc[...].astype(out_dtype)

    return pl.pallas_call(
        body, name=name, grid=(nk,),
        in_specs=[pl.BlockSpec((tk, a.shape[-1]), lambda k: (k, 0)) for a in a_list] + [pl.BlockSpec((tk, n), lambda k: (k, 0))],
        out_specs=[pl.BlockSpec((a.shape[-1], n), lambda k: (0, 0)) for a in a_list],
        out_shape=[jax.ShapeDtypeStruct((a.shape[-1], n), out_dtype) for a in a_list],
        scratch_shapes=[pltpu.VMEM((a.shape[-1], n), f32) for a in a_list],
        compiler_params=_cp("arbitrary"),
    )(*a_list, b)


def _outproj_bwd(dx1b, wo):
    tm = 1024

    def body(dx_ref, wo_ref, da_ref, dp_ref):
        dx = dx_ref[...]
        da_ref[...] = lax.dot_general(dx, wo_ref[0:AW, :], NT, preferred_element_type=f32).astype(bf16)
        dp_ref[...] = lax.dot_general(dx, wo_ref[AW:2 * AW, :], NT, preferred_element_type=f32)

    return pl.pallas_call(
        body, name="outproj_bwd", grid=(T // tm,),
        in_specs=[pl.BlockSpec((tm, D), lambda i: (i, 0)), _full((D, D))],
        out_specs=[pl.BlockSpec((tm, AW), lambda i: (i, 0)), pl.BlockSpec((tm, AW), lambda i: (i, 0))],
        out_shape=[jax.ShapeDtypeStruct((T, AW), bf16), jax.ShapeDtypeStruct((T, AW), f32)],
        compiler_params=_cp("parallel"),
    )(dx1b, wo)


def _pool_bwd(dpool, pooled, wp, scale):
    tm = 1024
    n = T // tm

    def body(dp_ref, pb_ref, wp_ref, sc_ref, du_ref, dsc_ref, dwp_ref, ext):
        i = pl.program_id(0)

        @pl.when(i == 0)
        def _():
            ext[tm:tm + HALO, :] = jnp.zeros((HALO, AW), f32)
            dsc_ref[...] = jnp.zeros_like(dsc_ref)
            dwp_ref[...] = jnp.zeros_like(dwp_ref)

        t_idx = (n - 1 - i) * tm + lax.broadcasted_iota(jnp.int32, (tm, 1), 0)
        for g, w in enumerate(WINDOWS):
            lo, hi = 128 * g, 128 * (g + 1)
            pb = pb_ref[:, lo:hi]
            mixed = jnp.dot(pb, wp_ref[g], preferred_element_type=f32)
            dpo = dp_ref[:, lo:hi]
            dsc_ref[:, lo:hi] += jnp.sum(dpo * mixed, axis=0, keepdims=True)
            dmr = (dpo * sc_ref[:, lo:hi]).astype(bf16)
            dwp_ref[g] += lax.dot_general(pb, dmr, TN, preferred_element_type=f32)
            dpl = lax.dot_general(dmr, wp_ref[g], NT, preferred_element_type=f32)
            cnt = jnp.minimum(t_idx + 1, w).astype(f32)
            dpn = dpl / cnt
            ext[0:tm, lo:hi] = dpn
            acc = dpn
            for d in range(1, w):
                acc = acc + ext[d:d + tm, lo:hi]
            du_ref[:, lo:hi] = (acc - dpl).astype(bf16)
        ext[tm:tm + HALO, :] = ext[0:HALO, :]

    rev = lambda i: (n - 1 - i, 0)
    return pl.pallas_call(
        body, name="pool_bwd", grid=(n,),
        in_specs=[pl.BlockSpec((tm, AW), rev), pl.BlockSpec((tm, AW), rev), _full((4, 128, 128)), _full((1, AW))],
        out_specs=[pl.BlockSpec((tm, AW), rev), _full((1, AW)), _full((4, 128, 128))],
        out_shape=[jax.ShapeDtypeStruct((T, AW), bf16), jax.ShapeDtypeStruct((1, AW), f32),
                   jax.ShapeDtypeStruct((4, 128, 128), f32)],
        scratch_shapes=[pltpu.VMEM((tm + HALO, AW), f32)],
        compiler_params=_cp("arbitrary"),
    )(dpool, pooled, wp, scale)


def _attn_bwd(qkv, qaug, kaug, attn, dattn, lse, dep):
    tq = tk = ATT_T
    n = T // tq
    it, jt = _causal_steps(True)
    nsteps = it.shape[0]

    rs = 64

    def body(it_ref, jt_ref, q_ref, k_ref, v_ref, qa_ref, ka_ref, o_ref, do_ref, lse_ref, dep_ref,
             dq_ref, dqs_ref, dk_ref, dks_ref, dv_ref, dq_acc, dk_acc, dv_acc, s_sc, dp_sc, p_sc, ds_sc):
        t = pl.program_id(1)
        i = it_ref[t]
        j = jt_ref[t]

        @pl.when(t == 0)
        def _():
            dq_acc[...] = jnp.zeros_like(dq_acc)

        @pl.when(i == j)
        def _():
            dk_acc[...] = jnp.zeros_like(dk_acc)
            dv_acc[...] = jnp.zeros_like(dv_acc)

        lane = lax.broadcasted_iota(jnp.int32, (tq, 128), 1)

        def step(on_diagonal):
            q = (q_ref[...].astype(f32) * Q_SCALE).astype(bf16)
            k = k_ref[...]
            v = v_ref[...]
            qa = qa_ref[...]
            ka = ka_ref[...]
            do = do_ref[...]
            dd = do.astype(f32) * o_ref[...].astype(f32)
            blocks = _row_blocks(tq, tk, on_diagonal)
            qes, kes, does, deltas = [], [], [], []
            for e in range(2):
                hm = (lane >= 64) if e else (lane < 64)
                qes.append(jnp.where(hm, q, qa))
                kes.append(jnp.where(hm, k, ka))
                does.append(jnp.where(hm, do, jnp.zeros_like(do)))
                deltas.append(jnp.sum(jnp.where(hm, dd, 0.0), axis=1, keepdims=True))
                for r0, r1, nc in blocks:
                    s_sc[e, r0:r1, 0:nc] = lax.dot_general(qes[e][r0:r1], kes[e][0:nc], NT, preferred_element_type=f32)
                    dp_sc[e, r0:r1, 0:nc] = lax.dot_general(does[e][r0:r1], v[0:nc], NT, preferred_element_type=f32)
            for e in range(2):
                for r0, r1, nc in blocks:
                    for r in range(r0, r1, rs):
                        s = s_sc[e, r:r + rs, 0:nc] - lse_ref[r:r + rs, 64 * e:64 * e + 1]
                        if on_diagonal:
                            row = lax.broadcasted_iota(jnp.int32, (rs, nc), 0) + r
                            col = lax.broadcasted_iota(jnp.int32, (rs, nc), 1)
                            s = jnp.where(col <= row, s, NEG)
                        p = jnp.exp2(s)
                        p_sc[e, r:r + rs, 0:nc] = p.astype(bf16)
                        ds_sc[e, r:r + rs, 0:nc] = (p * (dp_sc[e, r:r + rs, 0:nc] - deltas[e][r:r + rs, :])).astype(bf16)
                for r0, r1, nc in blocks:
                    dv_acc[:, 0:nc] += lax.dot_general(does[e][r0:r1], p_sc[e, r0:r1, 0:nc], TN, preferred_element_type=f32)
                    dsb = ds_sc[e, r0:r1, 0:nc]
                    dk_acc[e, :, 0:nc] += lax.dot_general(qes[e][r0:r1], dsb, TN, preferred_element_type=f32)
                    rq = pl.multiple_of(i * tq + r0, r1 - r0)
                    dq_acc[e, pl.ds(rq, r1 - r0), :] += jnp.dot(dsb, kes[e][0:nc], preferred_element_type=f32)

        @pl.when(i > j)
        def _():
            step(False)

        @pl.when(i == j)
        def _():
            step(True)

        @pl.when(i == n - 1)
        def _():
            dk0 = dk_acc[0].T
            dk1 = dk_acc[1].T
            dk_ref[...] = (jnp.where(lane < 64, dk0, dk1) * (1.0 / LOG2E)).astype(bf16)
            dks_ref[...] = jnp.where(lane < 64, dk1, dk0)
            dv_ref[...] = dv_acc[...].T.astype(bf16)

        @pl.when(t == nsteps - 1)
        def _():
            lane_t = lax.broadcasted_iota(jnp.int32, (T, 128), 1)
            dq_ref[...] = (jnp.where(lane_t < 64, dq_acc[0], dq_acc[1]) * 0.125).astype(bf16)
            dqs_ref[...] = jnp.where(lane_t < 64, dq_acc[1], dq_acc[0])

    qmap = lambda p, t, it, jt: (it[t], p)
    grid_spec = pltpu.PrefetchScalarGridSpec(
        num_scalar_prefetch=2, grid=(PAIRS, nsteps),
        in_specs=[pl.BlockSpec((tq, 128), qmap),
                  pl.BlockSpec((tk, 128), lambda p, t, it, jt: (jt[t], PAIRS + p)),
                  pl.BlockSpec((tk, 128), lambda p, t, it, jt: (jt[t], 2 * PAIRS + p)),
                  pl.BlockSpec((tq, 128), qmap), pl.BlockSpec((tk, 128), lambda p, t, it, jt: (jt[t], p)),
                  pl.BlockSpec((tq, 128), qmap), pl.BlockSpec((tq, 128), qmap),
                  pl.BlockSpec((None, tq, 128), lambda p, t, it, jt: (p, it[t], 0)),
                  pl.BlockSpec((8, 128), lambda p, t, it, jt: (0, 0))],
        out_specs=[pl.BlockSpec((T, 128), lambda p, t, it, jt: (0, p)),
                   pl.BlockSpec((None, T, 128), lambda p, t, it, jt: (p, 0, 0)),
                   pl.BlockSpec((tk, 128), lambda p, t, it, jt: (jt[t], p)),
                   pl.BlockSpec((None, tk, 128), lambda p, t, it, jt: (p, jt[t], 0)),
                   pl.BlockSpec((tk, 128), lambda p, t, it, jt: (jt[t], p))],
        scratch_shapes=[pltpu.VMEM((2, T, 128), f32), pltpu.VMEM((2, 128, tk), f32), pltpu.VMEM((128, tk), f32),
                        pltpu.VMEM((2, tq, tk), f32), pltpu.VMEM((2, tq, tk), f32), pltpu.VMEM((2, tq, tk), bf16),
                        pltpu.VMEM((2, tq, tk), bf16)],
    )
    return pl.pallas_call(
        body, name="fox_attn_bwd", grid_spec=grid_spec,
        out_shape=[jax.ShapeDtypeStruct((T, AW), bf16), jax.ShapeDtypeStruct((PAIRS, T, 128), f32),
                   jax.ShapeDtypeStruct((T, AW), bf16), jax.ShapeDtypeStruct((PAIRS, T, 128), f32),
                   jax.ShapeDtypeStruct((T, AW), bf16)],
        compiler_params=_cp("parallel", "arbitrary"),
    )(it, jt, qkv, qkv, qkv, qaug, kaug, attn, dattn, lse, dep)


def _fox_cumsum_bwd(dqs, dks, fl, bfp):
    tb = CUMSUM_ROWS
    nb = T // tb

    def body(dqs_ref, dks_ref, fl_ref, b_ref, df_ref, db_ref, carry):
        i = pl.program_id(0)

        @pl.when(i == 0)
        def _():
            carry[...] = jnp.zeros_like(carry)
            db_ref[...] = jnp.zeros_like(db_ref)

        r = lax.broadcasted_iota(jnp.int32, (128, 128), 0)
        cc = lax.broadcasted_iota(jnp.int32, (128, 128), 1)
        pick = lambda even_lane, odd_lane, p: jnp.logical_or(
            jnp.logical_and(r == even_lane, cc == 2 * p), jnp.logical_and(r == odd_lane, cc == 2 * p + 1)).astype(bf16)
        dc = jnp.zeros((tb, 128), f32)
        for p in range(PAIRS):
            rows_at = pick(SPARE[0] + ROW_SUM_LANE, SPARE[1] + ROW_SUM_LANE, p)
            cols_at = pick(SPARE[0] + COL_SUM_LANE, SPARE[1] + COL_SUM_LANE, p)
            dc = dc + _dot01(rows_at, dqs_ref[p], False) - _dot01(cols_at, dks_ref[p], False)
        rt = lax.broadcasted_iota(jnp.int32, (tb, tb), 0)
        ct = lax.broadcasted_iota(jnp.int32, (tb, tb), 1)
        utri = (ct >= rt).astype(bf16)
        dl = _dot01(utri, dc, True) + carry[0:1, :]
        carry[...] = jnp.broadcast_to(dl[0:1, :], (8, 128))
        z = fl_ref[...] + b_ref[...]
        df = dl * jax.nn.sigmoid(-z)
        df_ref[...] = df.astype(bf16)
        db_ref[...] += jnp.sum(df, axis=0, keepdims=True)

    rev = lambda i: (nb - 1 - i, 0)
    return pl.pallas_call(
        body, name="fox_cumsum_bwd", grid=(nb,),
        in_specs=[pl.BlockSpec((PAIRS, tb, 128), lambda i: (0, nb - 1 - i, 0)),
                  pl.BlockSpec((PAIRS, tb, 128), lambda i: (0, nb - 1 - i, 0)),
                  pl.BlockSpec((tb, 128), rev), _full((1, 128))],
        out_specs=[pl.BlockSpec((tb, 128), rev), _full((1, 128))],
        out_shape=[jax.ShapeDtypeStruct((T, 128), bf16), jax.ShapeDtypeStruct((1, 128), f32)],
        scratch_shapes=[pltpu.VMEM((8, 128), f32)],
        compiler_params=_cp("arbitrary"),
    )(dqs, dks, fl, bfp)


def _inproj_bwd(dq, dk, dv, du, df, w, x, dx1, g1):
    tm = 512

    def body(dq_ref, dk_ref, dv_ref, du_ref, df_ref, w_ref, x_ref, dx1_ref, g_ref, dx_ref, dn_ref):
        i = pl.program_id(0)

        @pl.when(i == 0)
        def _():
            dn_ref[...] = jnp.zeros_like(dn_ref)

        dproj = jnp.concatenate([dq_ref[...], dk_ref[...], dv_ref[...], du_ref[...], df_ref[...]], axis=1)
        dh = jnp.dot(dproj, w_ref[...], preferred_element_type=f32)
        xv = x_ref[...]
        r = lax.rsqrt(jnp.mean(xv * xv, axis=-1, keepdims=True) + EPS)
        xhat = xv * r
        dn_ref[...] += jnp.sum(dh * xhat, axis=0, keepdims=True)
        z = dh * g_ref[...]
        dx_ref[...] = dx1_ref[...] + r * (z - xhat * jnp.mean(z * xhat, axis=-1, keepdims=True))

    row = lambda i: (i, 0)
    return pl.pallas_call(
        body, name="inproj_bwd", grid=(T // tm,),
        in_specs=[pl.BlockSpec((tm, AW), row)] * 4 + [pl.BlockSpec((tm, 128), row), _full((W_ROWS, D)),
                                                       pl.BlockSpec((tm, D), row), pl.BlockSpec((tm, D), row), _full((1, D))],
        out_specs=[pl.BlockSpec((tm, D), row), _full((1, D))],
        out_shape=[jax.ShapeDtypeStruct((T, D), f32), jax.ShapeDtypeStruct((1, D), f32)],
        compiler_params=_cp("arbitrary"),
    )(dq, dk, dv, du, df, w, x, dx1, g1)


def _adamw_math(w, g, m, v):
    m = B1 * m + (1.0 - B1) * g
    v = B2 * v + (1.0 - B2) * (g * g)
    m_hat = m / (1.0 - B1 ** STEP)
    v_hat = v / (1.0 - B2 ** STEP)
    delta = -LR * (m_hat / (jnp.sqrt(v_hat) + AEPS) + WD * w)
    return delta, m, v


def _adamw_shard(w, m, v, p_mine, p_other, name):
    rows, cols = w.shape
    tr = rows if rows <= IN_S else rows // 2

    def body(w_ref, m_ref, v_ref, a_ref, b_ref, g_ref, d_ref, nm_ref, nv_ref):
        g = a_ref[...].astype(f32) + b_ref[...].astype(f32)
        g_ref[...] = g
        d_ref[...], nm_ref[...], nv_ref[...] = _adamw_math(w_ref[...], g, m_ref[...], v_ref[...])

    spec = pl.BlockSpec((tr, cols), lambda i: (i, 0))
    return pl.pallas_call(
        body, name=name, grid=(rows // tr,), in_specs=[spec] * 5, out_specs=[spec] * 4,
        out_shape=[jax.ShapeDtypeStruct((rows, cols), f32)] * 4, compiler_params=_cp("parallel"),
    )(w, m, v, p_mine, p_other)


SMALL_SLOTS = ((0, 8, 128), (8, 16, 128), (16, 24, 128), (24, 28, 128), (32, 33, 8))
LOSS_ROW = 39


def _adamw_small(ws, ms, vs, parts, parts_wp):
    n = len(ws)

    def body(*refs):
        w_refs, m_refs, v_refs = refs[0:n], refs[n:2 * n], refs[2 * n:3 * n]
        p_ref, pw_ref = refs[3 * n], refs[3 * n + 1]
        outs = refs[3 * n + 2:]
        g_all = p_ref[0]
        g_wp = pw_ref[0]
        for k in range(1, 8):
            g_all = g_all + p_ref[k]
            g_wp = g_wp + pw_ref[k]
        grads = [g_all[r0:r1, 0:lanes] for r0, r1, lanes in SMALL_SLOTS] + [g_wp]
        for idx, g in enumerate(grads):
            d, nm, nv = _adamw_math(w_refs[idx][...], g, m_refs[idx][...], v_refs[idx][...])
            outs[idx][...] = g
            outs[n + idx][...] = d
            outs[2 * n + idx][...] = nm
            outs[3 * n + idx][...] = nv
        outs[4 * n][...] = g_all[LOSS_ROW:LOSS_ROW + 1, :]

    shapes = [jax.ShapeDtypeStruct(w.shape, f32) for w in ws]
    res = pl.pallas_call(
        body, name="adamw_small", out_shape=shapes * 4 + [jax.ShapeDtypeStruct((1, 128), f32)],
    )(*ws, *ms, *vs, parts, parts_wp)
    return res[:4 * n], res[4 * n]


def _sum4(recv, g, mine, name):
    _, rows, cols = recv.shape
    tr = rows if rows <= IN_S else rows // 2

    def body(mine_ref, r_ref, g_ref, o_ref):
        o_ref[...] = ((g_ref[...].astype(f32) + r_ref[0].astype(f32))
                      + (r_ref[1].astype(f32) + r_ref[2].astype(f32))).astype(bf16)

    grid_spec = pltpu.PrefetchScalarGridSpec(
        num_scalar_prefetch=1, grid=(rows // tr,),
        in_specs=[pl.BlockSpec((3, tr, cols), lambda i, m: (0, i, 0)),
                  pl.BlockSpec((None, tr, cols), lambda i, m: (m[0], i, 0))],
        out_specs=pl.BlockSpec((tr, cols), lambda i, m: (i, 0)))
    return pl.pallas_call(
        body, name=name, grid_spec=grid_spec, out_shape=jax.ShapeDtypeStruct((rows, cols), bf16),
        compiler_params=_cp("arbitrary"),
    )(mine, recv, g)


_HBM = pl.BlockSpec(memory_space=pltpu.HBM)
_SEM = pl.BlockSpec(memory_space=pltpu.SEMAPHORE)
_EFFECT = pltpu.SideEffectType.DATAFLOW_SIDE_EFFECTING


def _in_hbm(a):
    return pltpu.with_memory_space_constraint(a, pltpu.HBM)


def _mesh_pos():
    return lax.axis_index("x"), lax.axis_index("y"), lax.axis_index("c")


def _other_chips(x, y):
    return [(1 - x, y), (x, 1 - y), (1 - x, 1 - y)]


def _gather_copy(srcs, lands, send_sems, recv_sems, a, k, slot):
    x, y, c = _mesh_pos()
    cx, cy = _other_chips(x, y)[k]
    return pltpu.make_async_remote_copy(
        src_ref=srcs[a], dst_ref=lands[a].at[slot], send_sem=send_sems.at[3 * a + k], recv_sem=recv_sems.at[3 * a + k],
        device_id=(cx, cy, c), device_id_type=MESH)


def _scatter_copy(srcs, lands, send_sems, recv_sems, a, k):
    x, y, c = _mesh_pos()
    cx, cy = _other_chips(x, y)[k]
    return pltpu.make_async_remote_copy(
        src_ref=srcs[a].at[2 * cx + cy], dst_ref=lands[a].at[k], send_sem=send_sems.at[3 * a + k],
        recv_sem=recv_sems.at[3 * a + k], device_id=(cx, cy, c), device_id_type=MESH)


def _all_gather_w_in(part):
    cols = part.shape[1] // 2

    def body(src, dst, send_sems, recv_sems, loc_sem):
        x, y, c = _mesh_pos()
        mine = 2 * x + y
        chips = _other_chips(x, y)
        half = lambda ref, cc: ref.at[:, pl.ds(pl.multiple_of(cc * cols, cols), cols)]

        def over_ici(k, slot):
            cx, cy = chips[k]
            return pltpu.make_async_remote_copy(
                src_ref=half(src, c), dst_ref=half(dst.at[slot], c), send_sem=send_sems.at[k], recv_sem=recv_sems.at[k],
                device_id=(cx, cy, c), device_id_type=MESH)

        def to_sibling(k, cc):
            slot = 2 * chips[k][0] + chips[k][1]
            return pltpu.make_async_remote_copy(
                src_ref=half(dst.at[slot], cc), dst_ref=half(dst.at[slot], cc), send_sem=send_sems.at[3 + k],
                recv_sem=recv_sems.at[3 + k], device_id=(x, y, 1 - c), device_id_type=MESH)

        local = pltpu.make_async_copy(src, dst.at[mine], loc_sem.at[0])
        local.start()
        first = [over_ici(k, mine) for k in range(3)]
        for cp in first:
            cp.start()
        passed = [to_sibling(k, c) for k in range(3)]
        for k in range(3):
            over_ici(k, 2 * chips[k][0] + chips[k][1]).wait_recv()
            passed[k].start()
        for k in range(3):
            to_sibling(k, 1 - c).wait_recv()
        for cp in first + passed:
            cp.wait_send()
        local.wait()

    return pl.pallas_call(
        body, name="all_gather_w_in", in_specs=[_HBM], out_specs=_HBM,
        out_shape=jax.ShapeDtypeStruct((NSH,) + part.shape, part.dtype),
        scratch_shapes=[pltpu.SemaphoreType.DMA((6,)), pltpu.SemaphoreType.DMA((6,)), pltpu.SemaphoreType.DMA((1,))],
    )(part)


def _split_start(name, srcs, lands, n_sems, plan, dep):
    n, nl = len(srcs), len(lands)

    def body(*refs):
        src_refs, land_refs = refs[:n], refs[n:n + nl]
        send_sems, recv_sems = refs[n + nl + 1], refs[n + nl + 2]
        token = refs[-1]
        sends, _ = plan(src_refs, land_refs, send_sems, recv_sems)
        for cp in sends:
            cp.start()
        token[...] = jnp.zeros_like(token)

    outs = pl.pallas_call(
        body, name=name,
        in_specs=[_HBM] * (n + nl) + [pl.BlockSpec(memory_space=pl.ANY)],
        out_specs=[_SEM, _SEM] + [_HBM] * (n + nl) + [pl.BlockSpec(memory_space=pltpu.VMEM)],
        out_shape=[pltpu.SemaphoreType.DMA((n_sems,)), pltpu.SemaphoreType.DMA((n_sems,))]
        + [pltpu.HBM(a.shape, a.dtype) for a in list(srcs) + list(lands)] + [jax.ShapeDtypeStruct((8, 128), f32)],
        input_output_aliases={i: 2 + i for i in range(n + nl)},
        compiler_params=pltpu.CompilerParams(has_side_effects=_EFFECT),
    )(*[_in_hbm(a) for a in list(srcs) + list(lands)], dep)
    return outs[0], outs[1], list(outs[2:2 + n]), list(outs[2 + n:2 + n + nl]), outs[-1]


def _split_wait(name, send_sems, recv_sems, srcs, lands, after, plan):
    n, nl = len(srcs), len(lands)

    def body(*refs):
        src_refs, land_refs = refs[:n], refs[n:n + nl]
        s_sems, r_sems = refs[n + nl], refs[n + nl + 1]
        sends, recvs = plan(src_refs, land_refs, s_sems, r_sems)
        for cp in recvs:
            cp.wait_recv()
        for cp in sends:
            cp.wait_send()

    outs = pl.pallas_call(
        body, name=name,
        in_specs=[_HBM] * (n + nl) + [_SEM, _SEM, pl.BlockSpec(memory_space=pl.ANY)],
        out_specs=[_HBM] * (n + nl),
        out_shape=[pltpu.HBM(a.shape, a.dtype) for a in list(srcs) + list(lands)],
        input_output_aliases={i: i for i in range(n + nl)},
        compiler_params=pltpu.CompilerParams(has_side_effects=_EFFECT),
    )(*srcs, *lands, send_sems, recv_sems, after)
    return list(outs[:n]), list(outs[n:])


def _gather_plan(srcs, lands, ss, rs):
    x, y, _ = _mesh_pos()
    chips = _other_chips(x, y)
    sends = [_gather_copy(srcs, lands, ss, rs, a, k, 2 * x + y) for a in range(len(srcs)) for k in range(3)]
    recvs = [_gather_copy(srcs, lands, ss, rs, a, k, 2 * chips[k][0] + chips[k][1])
             for a in range(len(srcs)) for k in range(3)]
    return sends, recvs


def _scatter_plan(srcs, lands, ss, rs):
    cps = [_scatter_copy(srcs, lands, ss, rs, a, k) for a in range(len(srcs)) for k in range(3)]
    return cps, cps


def _scatter_and_spread_plan(srcs, lands, ss, rs):
    x, y, c = _mesh_pos()
    me = 4 * x + 2 * y + c
    n = len(srcs) - 1
    cps = [_scatter_copy(srcs[:n], lands[:n], ss, rs, a, k) for a in range(n) for k in range(3)]
    for f in range(1, 8):
        peer = ((x + (f >> 2)) % 2, (y + ((f >> 1) & 1)) % 2, (c + (f & 1)) % 2)
        cps.append(pltpu.make_async_remote_copy(
            src_ref=srcs[n], dst_ref=lands[n].at[me], send_sem=ss.at[3 * n - 1 + f], recv_sem=rs.at[3 * n - 1 + f],
            device_id=peer, device_id_type=MESH))
    return cps, cps


def _swap_with_sibling(parts, name):
    n = len(parts)

    def body(*refs):
        srcs, dsts = refs[:n], refs[n:2 * n]
        send_sems, recv_sems = refs[2 * n:]
        x, y, c = _mesh_pos()
        cps = [pltpu.make_async_remote_copy(src_ref=srcs[a], dst_ref=dsts[a], send_sem=send_sems.at[a],
                                            recv_sem=recv_sems.at[a], device_id=(x, y, 1 - c), device_id_type=MESH)
               for a in range(n)]
        for cp in cps:
            cp.start()
        for cp in cps:
            cp.wait_recv()
        for cp in cps:
            cp.wait_send()

    return pl.pallas_call(
        body, name=name, in_specs=[_HBM] * n, out_specs=[_HBM] * n,
        out_shape=[jax.ShapeDtypeStruct(p.shape, p.dtype) for p in parts],
        scratch_shapes=[pltpu.SemaphoreType.DMA((n,)), pltpu.SemaphoreType.DMA((n,))],
    )(*parts)


def _forward(x, tgt, w_in_t, mlp_w_fn, g1, bfp, wp, scale, g2, gf, dep):
    h, qkv, u, fl = _rms_inproj(x, g1, w_in_t, dep)
    qaug, kaug = _fox_cumsum(fl, bfp)
    attn, lse = _attn_fwd(qkv, qaug, kaug)
    pooled, pool = _pool_fwd(u, wp, scale)
    wo, wgt, wut, wd = mlp_w_fn(attn)
    x1, h2 = _outproj(x, attn, pool, wo, g2)
    loss, dgf, dx2, dx2b, ud, silu, a_b = _mlp_fwd_loss(h2, x1, wgt, wut, wd, tgt, gf)
    saved = dict(h=h, qkv=qkv, fl=fl, qaug=qaug, kaug=kaug, attn=attn, lse=lse, pooled=pooled, pool=pool, x1=x1, h2=h2,
                 ud=ud, silu=silu, a_b=a_b, wo=wo, wgt=wgt, wut=wut, wd=wd)
    return loss, dgf, dx2, dx2b, saved


def _backward_mlp(sv, dx2, dx2b, g2):
    dgate, dup, dx1, dx1b, dg2 = _mlp_bwd(dx2b, dx2, sv["ud"], sv["silu"], sv["wgt"], sv["wut"], sv["wd"], sv["x1"], g2)
    (dwd,) = _mm_tn(sv["a_b"], [dx2b], "dw_down", a_sharded=True, tk=T)
    (dwgt,) = _mm_tn(dgate, [sv["h2"]], "dw_gate", a_sharded=True, tk=T)
    (dwut,) = _mm_tn(dup, [sv["h2"]], "dw_up", a_sharded=True, tk=T)
    return dx1, dx1b, dg2, (dwgt, dwut, dwd)


def _backward_outproj_pool(sv, dx1b, wp, scale):
    dattn, dpool = _outproj_bwd(dx1b, sv["wo"])
    dwo_a, = _mm_tn(sv["attn"], [dx1b], "dw_out_attn", tk=2048)
    dwo_p, = _mm_tn(sv["pool"], [dx1b], "dw_out_pool", tk=2048)
    dwo = jnp.concatenate([dwo_a, dwo_p], axis=0).reshape(NSH, D // NSH, D)
    du, dscale, dwp = _pool_bwd(dpool, sv["pooled"], wp, scale)
    return dattn, dwo, du, dscale, dwp


def _backward_attn_inproj(sv, x, dx1, dattn, du, w_in_t, g1, bfp, dep):
    dq, dqs, dk, dks, dv = _attn_bwd(sv["qkv"], sv["qaug"], sv["kaug"], sv["attn"], dattn, sv["lse"], dep)
    df, dbf = _fox_cumsum_bwd(dqs, dks, sv["fl"], bfp)
    dx, dg1 = _inproj_bwd(dq, dk, dv, du, df, w_in_t, x, dx1, g1)
    dwq, dwk, dwv, dwu_in, dwf = _mm_tn_rows([dq, dk, dv, du, df], sv["h"], "dw_in")
    dwin = jnp.concatenate([dwq, dwk, dwv, dwf[0:8], dwu_in], axis=0)
    return dx, dg1, dbf, dwin.reshape(NSH, IN_S, D)


def kernel(x, norm1_g, w_in, b_forget, w_pool, pool_scale, w_out, norm2_g, w_gate, w_up, w_down, final_g, loss_target, m_norm1_g, m_w_in, m_b_forget, m_w_pool, m_pool_scale, m_w_out, m_norm2_g, m_w_gate, m_w_up, m_w_down, m_final_g, v_norm1_g, v_w_in, v_b_forget, v_w_pool, v_pool_scale, v_w_out, v_norm2_g, v_w_gate, v_w_up, v_w_down, v_final_g):
    mine = (2 * lax.axis_index("x") + lax.axis_index("y")).astype(jnp.int32)
    mine1 = mine.reshape(1)
    tr = lambda a: jnp.transpose(a[0])

    win4 = _all_gather_w_in(tr(w_in).astype(bf16))
    later = [w_out[0].astype(bf16), tr(w_gate).astype(bf16), tr(w_up).astype(bf16), w_down[0].astype(bf16)]
    lands = [lax.dynamic_update_slice(lax.empty((NSH,) + p.shape, bf16), p[None], (mine, 0, 0)) for p in later]
    ag_send, ag_recv, later_thru, lands_thru, ag_token = _split_start("all_gather_start", later, lands, 12, _gather_plan,
                                                                      win4)
    win = win4.reshape(IN_W, D)
    w_in_t = jnp.concatenate([win[0:3 * AW], win[3 * AW + 8:], win[3 * AW:3 * AW + 8], jnp.zeros((120, D), bf16)], axis=0)
    bfp = jnp.pad(b_forget, ((0, 0), (0, 120)))
    wp = w_pool[0].astype(bf16)
    gf = final_g.reshape(1, D)

    def later_weights(after):
        _, (wo4, wgt, wut, wd) = _split_wait("all_gather_wait", ag_send, ag_recv, later_thru, lands_thru, after, _gather_plan)
        return wo4.reshape(D, D), wgt, wut, wd

    xe, tgt = x[0], loss_target[0]
    loss_v, dgf, dx2, dx2b, sv = _forward(xe, tgt, w_in_t, later_weights, norm1_g, bfp, wp, pool_scale, norm2_g, gf, ag_token)
    dx1, dx1b, dg2, mlp_grads = _backward_mlp(sv, dx2, dx2b, norm2_g)
    dattn, dwo, du, dscale, dwp = _backward_outproj_pool(sv, dx1b, wp, pool_scale)
    me = (4 * lax.axis_index("x") + 2 * lax.axis_index("y") + lax.axis_index("c")).astype(jnp.int32)
    dwp = dwp.reshape(512, 128)
    first = [dwo] + list(mlp_grads) + [dwp]
    first_lands = [lax.empty((3,) + g.shape[1:], bf16) for g in first[:4]]
    first_lands.append(lax.dynamic_update_slice(lax.empty((8, 512, 128), f32), dwp[None], (me, 0, 0)))
    rs_send, rs_recv, first_thru, first_lands_thru, rs_token = _split_start(
        "reduce_scatter_start", first, first_lands, 19, _scatter_and_spread_plan, du)
    dx, dg1, dbf, dwin = _backward_attn_inproj(sv, xe, dx1, dattn, du, w_in_t, norm1_g, bfp, rs_token)

    pad8 = lambda r: jnp.pad(r, ((0, 8 - r.shape[0]), (0, 0)))
    loss_rows = jnp.concatenate([dbf, jnp.zeros((6, 128), f32), loss_v[0:1, :]], axis=0)
    small = jnp.concatenate([dg1.reshape(8, 128), dg2.reshape(8, 128), dgf.reshape(8, 128), pad8(dscale.reshape(4, 128)),
                             loss_rows], axis=0)
    small_land = lax.dynamic_update_slice(lax.empty((8, SMALL_ROWS, 128), f32), small[None], (me, 0, 0))
    tail_send, tail_recv, tail_thru, tail_lands_thru, tail_token = _split_start(
        "tail_start", [dwin, small], [lax.empty((3,) + dwin.shape[1:], bf16), small_land], 10, _scatter_and_spread_plan,
        dx)
    first_thru, first_recv = _split_wait("reduce_scatter_wait", rs_send, rs_recv, first_thru, first_lands_thru, tail_token,
                                         _scatter_and_spread_plan)
    wp_all = first_recv[4]
    ws = [tr(w_in), w_out[0], tr(w_gate), tr(w_up), w_down[0]]
    ms = [tr(m_w_in), m_w_out[0], tr(m_w_gate), tr(m_w_up), m_w_down[0]]
    vs = [tr(v_w_in), v_w_out[0], tr(v_w_gate), tr(v_w_up), v_w_down[0]]
    partial = [_sum4(r, g, mine1, f"sum4_{i + 1}") for i, (r, g) in enumerate(zip(first_recv[:4], first_thru[:4]))]
    other = _swap_with_sibling(partial, "swap_first")
    big = [_adamw_shard(ws[i + 1], ms[i + 1], vs[i + 1], partial[i], other[i], f"adamw_{i + 1}") for i in range(4)]
    (dwin_thru, _), (in_recv_land, small_all) = _split_wait("tail_wait", tail_send, tail_recv, tail_thru, tail_lands_thru,
                                                            big[3][0], _scatter_and_spread_plan)
    partial_in = _sum4(in_recv_land, dwin_thru, mine1, "sum4_0")
    (other_in,) = _swap_with_sibling([partial_in], "swap_in")
    big = [_adamw_shard(ws[0], ms[0], vs[0], partial_in, other_in, "adamw_0")] + big

    small_names = ["norm1_g", "norm2_g", "final_g", "pool_scale", "b_forget", "w_pool"]
    rows = lambda a, b, c, d, e, f: [a.reshape(8, 128), b.reshape(8, 128), c.reshape(8, 128), d.reshape(4, 128),
                                     e.reshape(1, 8), f.reshape(512, 128)]
    sm, loss_row = _adamw_small(rows(norm1_g, norm2_g, final_g, pool_scale, b_forget, w_pool),
                                rows(m_norm1_g, m_norm2_g, m_final_g, m_pool_scale, m_b_forget, m_w_pool),
                                rows(v_norm1_g, v_norm2_g, v_final_g, v_pool_scale, v_b_forget, v_w_pool), small_all, wp_all)
    small_shape = dict(norm1_g=(1, D), norm2_g=(1, D), final_g=(D,), pool_scale=(1, AW), b_forget=(1, 8),
                       w_pool=(1, 4, 128, 128))

    order = ["norm1_g", "w_in", "b_forget", "w_pool", "pool_scale", "w_out", "norm2_g", "w_gate", "w_up", "w_down", "final_g"]
    big_idx = {"w_in": 0, "w_out": 1, "w_gate": 2, "w_up": 3, "w_down": 4}
    outs = [loss_row[0, 0], dx[None]]
    for kind in range(4):
        for name in order:
            if name in ("w_in", "w_gate", "w_up"):
                outs.append(jnp.transpose(big[big_idx[name]][kind])[None])
            elif name in big_idx:
                outs.append(big[big_idx[name]][kind][None])
            else:
                outs.append(sm[6 * kind + small_names.index(name)].reshape(small_shape[name]))
    return tuple(outs)
```

```python
import jax
import jax.numpy as jnp
import numpy as np
from jax import lax
from jax.experimental import pallas as pl
from jax.experimental.pallas import tpu as pltpu

f32 = jnp.float32
bf16 = jnp.bfloat16

T = 4096
D = 1024
NSH = 4
IN_W = 2056
IN_S = IN_W // NSH
AW = 512
PAIRS = 4
SPARE = (64, 0)
ROW_SUM_LANE, COL_SUM_LANE = 0, 3
FF = 2816
FS = FF // NSH
WINDOWS = (2, 4, 8, 16)
HALO = 16
EPS = 1e-6
NEG = -1e30
LR, B1, B2, AEPS, WD, STEP = 0.001, 0.9, 0.999, 1e-08, 0.01, 10
SMALL_ROWS = 40

NT = (((1,), (1,)), ((), ()))
TN = (((0,), (0,)), ((), ()))

MESH = pl.DeviceIdType.MESH


def _cp(*sem):
    return pltpu.CompilerParams(dimension_semantics=sem)


def _full(shape):
    n = len(shape)
    return pl.BlockSpec(shape, lambda *_: (0,) * n)


def _resident(shape):
    n = len(shape)
    return pl.BlockSpec(shape, lambda *_: (0,) * n, pipeline_mode=pl.Buffered(1))


W_ROWS = 4 * AW + 128


def _rms_inproj(x, g1, w, dep):
    tm = 512

    def body(x_ref, g_ref, w_ref, dep_ref, h_ref, qkv_ref, u_ref, fl_ref):
        xv = x_ref[...]
        r = lax.rsqrt(jnp.mean(xv * xv, axis=-1, keepdims=True) + EPS)
        h = (xv * r * g_ref[...]).astype(bf16)
        h_ref[...] = h
        qkv_ref[...] = lax.dot_general(h, w_ref[0:3 * AW, :], NT, preferred_element_type=f32).astype(bf16)
        u_ref[...] = lax.dot_general(h, w_ref[3 * AW:4 * AW, :], NT, preferred_element_type=f32)
        fl_ref[...] = lax.dot_general(h, w_ref[4 * AW:W_ROWS, :], NT, preferred_element_type=f32)

    return pl.pallas_call(
        body, name="rms_inproj", grid=(T // tm,),
        in_specs=[pl.BlockSpec((tm, D), lambda i: (i, 0)), _full((1, D)), _full((W_ROWS, D)), _full((8, 128))],
        out_specs=[pl.BlockSpec((tm, D), lambda i: (i, 0)), pl.BlockSpec((tm, 3 * AW), lambda i: (i, 0)),
                   pl.BlockSpec((tm, AW), lambda i: (i, 0)), pl.BlockSpec((tm, 128), lambda i: (i, 0))],
        out_shape=[jax.ShapeDtypeStruct((T, D), bf16), jax.ShapeDtypeStruct((T, 3 * AW), bf16),
                   jax.ShapeDtypeStruct((T, AW), f32), jax.ShapeDtypeStruct((T, 128), f32)],
        compiler_params=_cp("parallel"),
    )(x, g1, w, dep)


CUMSUM_ROWS = 512
FS_CHUNKS = ((0, 256), (256, 512), (512, FS))


def _log_sigmoid(z):
    return jnp.minimum(z, 0.0) - jnp.log(1.0 + jnp.exp(-jnp.abs(z)))


def _split3(x):
    hi = x.astype(bf16)
    r1 = x - hi.astype(f32)
    mid = r1.astype(bf16)
    return hi, mid, (r1 - mid.astype(f32)).astype(bf16)


def _dot01(sel, x, sel_first):
    parts = _split3(x)
    if sel_first:
        return sum(jnp.dot(sel, p, preferred_element_type=f32) for p in parts)
    return sum(jnp.dot(p, sel, preferred_element_type=f32) for p in parts)


def _fox_cumsum(fl, bfp):
    tb = CUMSUM_ROWS
    nb = T // tb

    def body(fl_ref, b_ref, qa_ref, ka_ref, carry):
        i = pl.program_id(0)

        @pl.when(i == 0)
        def _():
            carry[...] = jnp.zeros_like(carry)

        lf = _log_sigmoid(fl_ref[...] + b_ref[...])
        r = lax.broadcasted_iota(jnp.int32, (tb, tb), 0)
        cc = lax.broadcasted_iota(jnp.int32, (tb, tb), 1)
        ltri = (cc <= r).astype(bf16)
        cb = _dot01(ltri, lf, True) + carry[0:1, :]
        carry[...] = jnp.broadcast_to(cb[tb - 1:tb, :], (8, 128))
        hi, mid, lo = _split3(cb * LOG2E)
        head = lax.broadcasted_iota(jnp.int32, (128, AW), 0)
        col = lax.broadcasted_iota(jnp.int32, (128, AW), 1)
        base = 128 * (head >> 1) + jnp.where((head & 1) == 0, SPARE[0], SPARE[1])
        place = lambda off: jnp.logical_and(col == base + off, head < 8).astype(bf16)
        mm = lambda a, off: jnp.dot(a, place(off), preferred_element_type=f32)
        cq = mm(hi, 0) + mm(mid, 1) + mm(lo, 2)
        ck = mm(hi, 3) + mm(mid, 4) + mm(lo, 5)
        within = jnp.bitwise_and(lax.broadcasted_iota(jnp.int32, (tb, AW), 1), 63)
        qa_ref[...] = jnp.where(jnp.logical_and(within >= 3, within <= 5), 1.0, cq).astype(bf16)
        ka_ref[...] = jnp.where(within <= 2, 1.0, -ck).astype(bf16)

    return pl.pallas_call(
        body, name="fox_cumsum", grid=(nb,),
        in_specs=[pl.BlockSpec((tb, 128), lambda i: (i, 0)), _full((1, 128))],
        out_specs=[pl.BlockSpec((tb, AW), lambda i: (i, 0)), pl.BlockSpec((tb, AW), lambda i: (i, 0))],
        out_shape=[jax.ShapeDtypeStruct((T, AW), bf16), jax.ShapeDtypeStruct((T, AW), bf16)],
        scratch_shapes=[pltpu.VMEM((8, 128), f32)],
        compiler_params=_cp("arbitrary"),
    )(fl, bfp)


ATT_T = 512
LOG2E = 1.4426950408889634
Q_SCALE = 0.125 * LOG2E


def _causal_steps(key_major):
    n = T // ATT_T
    if key_major:
        pairs = [(i, j) for j in range(n) for i in range(j, n)]
    else:
        pairs = [(i, j) for i in range(n) for j in range(i + 1)]
    it = np.array([p[0] for p in pairs], np.int32)
    jt = np.array([p[1] for p in pairs], np.int32)
    return jnp.asarray(it), jnp.asarray(jt)


def _row_blocks(tq, tk, on_diagonal):
    return ((0, tq // 2, tk // 2), (tq // 2, tq, tk)) if on_diagonal else ((0, tq, tk),)


def _attn_fwd(qkv, qaug, kaug):
    tq = tk = ATT_T
    it, jt = _causal_steps(False)
    nsteps = it.shape[0]

    rs = 64

    def body(it_ref, jt_ref, q_ref, k_ref, v_ref, qa_ref, ka_ref, o_ref, lse_ref, m_sc, acc_sc, s_sc, p_sc, alpha_sc):
        t = pl.program_id(1)
        i = it_ref[t]
        j = jt_ref[t]

        @pl.when(j == 0)
        def _():
            m_sc[...] = jnp.full_like(m_sc, NEG)
            acc_sc[...] = jnp.zeros_like(acc_sc)

        lane = lax.broadcasted_iota(jnp.int32, (tq, 128), 1)
        spare = SPARE

        def step(on_diagonal):
            q = (q_ref[...].astype(f32) * Q_SCALE).astype(bf16)
            k = k_ref[...]
            v = v_ref[...]
            qa = qa_ref[...]
            ka = ka_ref[...]
            blocks = _row_blocks(tq, tk, on_diagonal)
            for e in range(2):
                hm = (lane >= 64) if e else (lane < 64)
                qe = jnp.where(hm, q, qa)
                ke = jnp.where(hm, k, ka)
                for r0, r1, nc in blocks:
                    s_sc[e, r0:r1, 0:nc] = lax.dot_general(qe[r0:r1], ke[0:nc], NT, preferred_element_type=f32)
            for e in range(2):
                for r0, r1, nc in blocks:
                    for r in range(r0, r1, rs):
                        s = s_sc[e, r:r + rs, 0:nc]
                        if on_diagonal:
                            row = lax.broadcasted_iota(jnp.int32, (rs, nc), 0) + r
                            col = lax.broadcasted_iota(jnp.int32, (rs, nc), 1)
                            s = jnp.where(col <= row, s, NEG)
                        m_prev = m_sc[e, r:r + rs, :]
                        m_new = jnp.maximum(m_prev, jnp.max(s, axis=1, keepdims=True))
                        p_sc[e, r:r + rs, 0:nc] = jnp.exp2(s - jnp.tile(m_new, (1, nc // 128))).astype(bf16)
                        alpha_sc[e, r:r + rs, :] = jnp.exp2(m_prev - m_new)
                        m_sc[e, r:r + rs, :] = m_new
            for e in range(2):
                hm = (lane >= 64) if e else (lane < 64)
                ve = jnp.where(hm, v, (lane == spare[e]).astype(bf16))
                for r0, r1, nc in blocks:
                    acc_sc[e, r0:r1] = (alpha_sc[e, r0:r1] * acc_sc[e, r0:r1]
                                        + jnp.dot(p_sc[e, r0:r1, 0:nc], ve[0:nc], preferred_element_type=f32))

        @pl.when(j < i)
        def _():
            step(False)

        @pl.when(j == i)
        def _():
            step(True)
            l0 = acc_sc[0][:, spare[0]:spare[0] + 1]
            l1 = acc_sc[1][:, spare[1]:spare[1] + 1]
            o_ref[...] = jnp.where(lane < 64, acc_sc[0] / l0, acc_sc[1] / l1).astype(bf16)
            lse_ref[...] = jnp.where(lane < 64, m_sc[0] + jnp.log2(l0), m_sc[1] + jnp.log2(l1))

    qmap = lambda p, t, it, jt: (it[t], p)
    kmap = lambda p, t, it, jt: (jt[t], p)
    grid_spec = pltpu.PrefetchScalarGridSpec(
        num_scalar_prefetch=2, grid=(PAIRS, nsteps),
        in_specs=[pl.BlockSpec((tq, 128), qmap),
                  pl.BlockSpec((tk, 128), lambda p, t, it, jt: (jt[t], PAIRS + p)),
                  pl.BlockSpec((tk, 128), lambda p, t, it, jt: (jt[t], 2 * PAIRS + p)),
                  pl.BlockSpec((tq, 128), qmap), pl.BlockSpec((tk, 128), kmap)],
        out_specs=[pl.BlockSpec((tq, 128), qmap),
                   pl.BlockSpec((None, tq, 128), lambda p, t, it, jt: (p, it[t], 0))],
        scratch_shapes=[pltpu.VMEM((2, tq, 128), f32), pltpu.VMEM((2, tq, 128), f32), pltpu.VMEM((2, tq, tk), f32),
                        pltpu.VMEM((2, tq, tk), bf16), pltpu.VMEM((2, tq, 128), f32)],
    )
    return pl.pallas_call(
        body, name="fox_attn_fwd", grid_spec=grid_spec,
        out_shape=[jax.ShapeDtypeStruct((T, AW), bf16), jax.ShapeDtypeStruct((PAIRS, T, 128), f32)],
        compiler_params=_cp("parallel", "arbitrary"),
    )(it, jt, qkv, qkv, qkv, qaug, kaug)


def _pool_fwd(u, wp, scale):
    tm = 1024

    def body(u_ref, wp_ref, sc_ref, pooled_ref, pool_ref, ext):
        i = pl.program_id(0)

        @pl.when(i == 0)
        def _():
            ext[0:HALO, :] = jnp.zeros((HALO, AW), f32)

        uv = u_ref[...]
        ext[HALO:HALO + tm, :] = uv
        t_idx = i * tm + lax.broadcasted_iota(jnp.int32, (tm, 1), 0)
        for g, w in enumerate(WINDOWS):
            lo, hi = 128 * g, 128 * (g + 1)
            ug = uv[:, lo:hi]
            acc = ug
            for d in range(1, w):
                acc = acc + ext[HALO - d:HALO - d + tm, lo:hi]
            cnt = jnp.minimum(t_idx + 1, w).astype(f32)
            pb = (acc / cnt - ug).astype(bf16)
            pooled_ref[:, lo:hi] = pb
            mixed = jnp.dot(pb, wp_ref[g], preferred_element_type=f32)
            pool_ref[:, lo:hi] = (mixed * sc_ref[:, lo:hi]).astype(bf16)
        ext[0:HALO, :] = uv[tm - HALO:tm, :]

    return pl.pallas_call(
        body, name="pool_fwd", grid=(T // tm,),
        in_specs=[pl.BlockSpec((tm, AW), lambda i: (i, 0)), _full((4, 128, 128)), _full((1, AW))],
        out_specs=[pl.BlockSpec((tm, AW), lambda i: (i, 0)), pl.BlockSpec((tm, AW), lambda i: (i, 0))],
        out_shape=[jax.ShapeDtypeStruct((T, AW), bf16), jax.ShapeDtypeStruct((T, AW), bf16)],
        scratch_shapes=[pltpu.VMEM((tm + HALO, AW), f32)],
        compiler_params=_cp("arbitrary"),
    )(u, wp, scale)


def _outproj(x, attn, pool, wo, g2):
    tm = 1024

    def body(x_ref, a_ref, p_ref, wo_ref, g_ref, x1_ref, h2_ref):
        mixed = jnp.concatenate([a_ref[...], p_ref[...]], axis=1)
        x1 = x_ref[...] + jnp.dot(mixed, wo_ref[...], preferred_element_type=f32)
        x1_ref[...] = x1
        r = lax.rsqrt(jnp.mean(x1 * x1, axis=-1, keepdims=True) + EPS)
        h2_ref[...] = (x1 * r * g_ref[...]).astype(bf16)

    return pl.pallas_call(
        body, name="outproj", grid=(T // tm,),
        in_specs=[pl.BlockSpec((tm, D), lambda i: (i, 0)), pl.BlockSpec((tm, AW), lambda i: (i, 0)),
                  pl.BlockSpec((tm, AW), lambda i: (i, 0)), _full((D, D)), _full((1, D))],
        out_specs=[pl.BlockSpec((tm, D), lambda i: (i, 0)), pl.BlockSpec((tm, D), lambda i: (i, 0))],
        out_shape=[jax.ShapeDtypeStruct((T, D), f32), jax.ShapeDtypeStruct((T, D), bf16)],
        compiler_params=_cp("parallel"),
    )(x, attn, pool, wo, g2)


def _mlp_fwd_loss(h2, x1, wg, wu, wd, tgt, gf):
    tm = 512

    def body(h_ref, x1_ref, wg_ref, wu_ref, wd_ref, t_ref, g_ref,
             loss_ref, dg_ref, dx_ref, dxb_ref, ud_ref, silu_ref, a_ref, x2):
        i = pl.program_id(0)
        s = pl.program_id(1)

        @pl.when(jnp.logical_and(i == 0, s == 0))
        def _():
            loss_ref[...] = jnp.zeros_like(loss_ref)
            dg_ref[...] = jnp.zeros_like(dg_ref)

        h = h_ref[...]
        gus = [(lax.dot_general(h, wg_ref[s, c0:c1, :], NT, preferred_element_type=f32),
                lax.dot_general(h, wu_ref[s, c0:c1, :], NT, preferred_element_type=f32)) for c0, c1 in FS_CHUNKS]
        for (c0, c1), (gate, up) in zip(FS_CHUNKS, gus):
            sg = jax.nn.sigmoid(gate)
            silu = gate * sg
            ud_ref[:, c0:c1] = (up * (sg * (1.0 + gate * (1.0 - sg)))).astype(bf16)
            silu_ref[:, c0:c1] = silu.astype(bf16)
            a_ref[:, c0:c1] = (silu * up).astype(bf16)
        part = jnp.dot(a_ref[...], wd_ref[s], preferred_element_type=f32)

        @pl.when(s == 0)
        def _():
            x2[...] = x1_ref[...] + part

        @pl.when(s > 0)
        def _():
            x2[...] += part

        @pl.when(s == NSH - 1)
        def _():
            xv = x2[...]
            g = g_ref[...]
            r = lax.rsqrt(jnp.mean(xv * xv, axis=-1, keepdims=True) + EPS)
            xhat = xv * r
            e = xhat * g - t_ref[...]
            loss_ref[...] += 0.5 * jnp.sum(jnp.mean(e * e, axis=-1, keepdims=True))
            dy = e * (1.0 / D)
            dg_ref[...] += jnp.sum(dy * xhat, axis=0, keepdims=True)
            z = dy * g
            dx = r * (z - xhat * jnp.mean(z * xhat, axis=-1, keepdims=True))
            dx_ref[...] = dx
            dxb_ref[...] = dx.astype(bf16)

    row = lambda i, s: (i, 0)
    sl = lambda i, s: (s, i, 0)
    wsl = lambda i, s: (s, 0, 0)
    return pl.pallas_call(
        body, name="mlp_fwd_loss", grid=(T // tm, NSH),
        in_specs=[pl.BlockSpec((tm, D), row), pl.BlockSpec((tm, D), row),
                  _resident((NSH, FS, D)), _resident((NSH, FS, D)), _resident((NSH, FS, D)),
                  pl.BlockSpec((tm, D), row), pl.BlockSpec((1, D), lambda i, s: (0, 0))],
        out_specs=[pl.BlockSpec((8, 128), lambda i, s: (0, 0)), pl.BlockSpec((1, D), lambda i, s: (0, 0)),
                   pl.BlockSpec((tm, D), row), pl.BlockSpec((tm, D), row),
                   pl.BlockSpec((None, tm, FS), sl), pl.BlockSpec((None, tm, FS), sl), pl.BlockSpec((None, tm, FS), sl)],
        out_shape=[jax.ShapeDtypeStruct((8, 128), f32), jax.ShapeDtypeStruct((1, D), f32),
                   jax.ShapeDtypeStruct((T, D), f32), jax.ShapeDtypeStruct((T, D), bf16)]
        + [jax.ShapeDtypeStruct((NSH, T, FS), bf16)] * 3,
        scratch_shapes=[pltpu.VMEM((tm, D), f32)],
        compiler_params=_cp("arbitrary", "arbitrary"),
    )(h2, x1, wg, wu, wd, tgt, gf)


def _mlp_bwd(dx2b, dx2, ud, silu, wg, wu, wd, x1, g2):
    tm = 512

    def body(dxb_ref, dx_ref, ud_ref, silu_ref, wg_ref, wu_ref, wd_ref, x1_ref, g_ref,
             dg_ref, du_ref, dx1_ref, dx1b_ref, dn_ref, acc):
        i = pl.program_id(0)
        s = pl.program_id(1)

        @pl.when(jnp.logical_and(i == 0, s == 0))
        def _():
            dn_ref[...] = jnp.zeros_like(dn_ref)

        dxb = dxb_ref[...]
        das = [lax.dot_general(dxb, wd_ref[s, c0:c1, :], NT, preferred_element_type=f32) for c0, c1 in FS_CHUNKS]
        for (c0, c1), da in zip(FS_CHUNKS, das):
            dg_ref[:, c0:c1] = (da * ud_ref[:, c0:c1].astype(f32)).astype(bf16)
            du_ref[:, c0:c1] = (da * silu_ref[:, c0:c1].astype(f32)).astype(bf16)
        part = jnp.dot(dg_ref[...], wg_ref[s], preferred_element_type=f32)
        part = part + jnp.dot(du_ref[...], wu_ref[s], preferred_element_type=f32)

        @pl.when(s == 0)
        def _():
            acc[...] = part

        @pl.when(s > 0)
        def _():
            acc[...] += part

        @pl.when(s == NSH - 1)
        def _():
            xv = x1_ref[...]
            r = lax.rsqrt(jnp.mean(xv * xv, axis=-1, keepdims=True) + EPS)
            xhat = xv * r
            dh = acc[...]
            dn_ref[...] += jnp.sum(dh * xhat, axis=0, keepdims=True)
            z = dh * g_ref[...]
            dx1 = dx_ref[...] + r * (z - xhat * jnp.mean(z * xhat, axis=-1, keepdims=True))
            dx1_ref[...] = dx1
            dx1b_ref[...] = dx1.astype(bf16)

    row = lambda i, s: (i, 0)
    sl = lambda i, s: (s, i, 0)
    wsl = lambda i, s: (s, 0, 0)
    return pl.pallas_call(
        body, name="mlp_bwd", grid=(T // tm, NSH),
        in_specs=[pl.BlockSpec((tm, D), row), pl.BlockSpec((tm, D), row),
                  pl.BlockSpec((None, tm, FS), sl), pl.BlockSpec((None, tm, FS), sl),
                  _resident((NSH, FS, D)), _resident((NSH, FS, D)), _resident((NSH, FS, D)),
                  pl.BlockSpec((tm, D), row), pl.BlockSpec((1, D), lambda i, s: (0, 0))],
        out_specs=[pl.BlockSpec((None, tm, FS), sl), pl.BlockSpec((None, tm, FS), sl),
                   pl.BlockSpec((tm, D), row), pl.BlockSpec((tm, D), row), pl.BlockSpec((1, D), lambda i, s: (0, 0))],
        out_shape=[jax.ShapeDtypeStruct((NSH, T, FS), bf16)] * 2
        + [jax.ShapeDtypeStruct((T, D), f32), jax.ShapeDtypeStruct((T, D), bf16), jax.ShapeDtypeStruct((1, D), f32)],
        scratch_shapes=[pltpu.VMEM((tm, D), f32)],
        compiler_params=_cp("arbitrary", "arbitrary"),
    )(dx2b, dx2, ud, silu, wg, wu, wd, x1, g2)


def _mm_tn(a, bs, name, a_sharded=False, b_sharded=False, tk=512, out_dtype=bf16):
    nb = len(bs)
    sh = NSH if (a_sharded or b_sharded) else 1
    m = a.shape[-1]
    nk = T // tk

    def body(a_ref, *refs):
        kk = pl.program_id(1)
        av = a_ref[...]
        for b_ref, o_ref, acc in zip(refs[:nb], refs[nb:2 * nb], refs[2 * nb:]):
            upd = lax.dot_general(av, b_ref[...], TN, preferred_element_type=f32)

            @pl.when(kk == 0)
            def _():
                acc[...] = upd

            @pl.when(kk > 0)
            def _():
                acc[...] += upd

            @pl.when(kk == nk - 1)
            def _():
                o_ref[...] = acc[...].astype(out_dtype)

    a_spec = (pl.BlockSpec((None, tk, m), lambda s, k: (s, k, 0)) if a_sharded
              else pl.BlockSpec((tk, m), lambda s, k: (k, 0)))
    b_specs, o_specs, o_shapes, scratch = [], [], [], []
    for b in bs:
        n = b.shape[-1]
        b_specs.append(pl.BlockSpec((None, tk, n), lambda s, k: (s, k, 0)) if b_sharded
                       else pl.BlockSpec((tk, n), lambda s, k: (k, 0)))
        scratch.append(pltpu.VMEM((m, n), f32))
        if sh > 1:
            o_specs.append(pl.BlockSpec((None, m, n), lambda s, k: (s, 0, 0)))
            o_shapes.append(jax.ShapeDtypeStruct((sh, m, n), out_dtype))
        else:
            o_specs.append(pl.BlockSpec((m, n), lambda s, k: (0, 0)))
            o_shapes.append(jax.ShapeDtypeStruct((m, n), out_dtype))
    return pl.pallas_call(
        body, name=name, grid=(sh, nk), in_specs=[a_spec] + b_specs, out_specs=o_specs, out_shape=o_shapes,
        scratch_shapes=scratch, compiler_params=_cp("arbitrary", "arbitrary"),
    )(a, *bs)


def _mm_tn_rows(a_list, b, name, tk=1024, out_dtype=bf16):
    na = len(a_list)
    n = b.shape[-1]
    nk = T // tk

    def body(*refs):
        a_refs, b_ref = refs[:na], refs[na]
        o_refs, accs = refs[na + 1:2 * na + 1], refs[2 * na + 1:]
        kk = pl.program_id(0)
        bv = b_ref[...]
        for a_ref, o_ref, acc in zip(a_refs, o_refs, accs):
            upd = lax.dot_general(a_ref[...], bv, TN, preferred_element_type=f32)

            @pl.when(kk == 0)
            def _():
                acc[...] = upd

            @pl.when(kk > 0)
            def _():
                acc[...] += upd

            @pl.when(kk == nk - 1)
            def _():
                o_ref[...] = acc[...].astype(out_dtype)

    return pl.pallas_call(
        body, name=name, grid=(nk,),
        in_specs=[pl.BlockSpec((tk, a.shape[-1]), lambda k: (k, 0)) for a in a_list] + [pl.BlockSpec((tk, n), lambda k: (k, 0))],
        out_specs=[pl.BlockSpec((a.shape[-1], n), lambda k: (0, 0)) for a in a_list],
        out_shape=[jax.ShapeDtypeStruct((a.shape[-1], n), out_dtype) for a in a_list],
        scratch_shapes=[pltpu.VMEM((a.shape[-1], n), f32) for a in a_list],
        compiler_params=_cp("arbitrary"),
    )(*a_list, b)


def _outproj_bwd(dx1b, wo):
    tm = 1024

    def body(dx_ref, wo_ref, da_ref, dp_ref):
        dx = dx_ref[...]
        da_ref[...] = lax.dot_general(dx, wo_ref[0:AW, :], NT, preferred_element_type=f32).astype(bf16)
        dp_ref[...] = lax.dot_general(dx, wo_ref[AW:2 * AW, :], NT, preferred_element_type=f32)

    return pl.pallas_call(
        body, name="outproj_bwd", grid=(T // tm,),
        in_specs=[pl.BlockSpec((tm, D), lambda i: (i, 0)), _full((D, D))],
        out_specs=[pl.BlockSpec((tm, AW), lambda i: (i, 0)), pl.BlockSpec((tm, AW), lambda i: (i, 0))],
        out_shape=[jax.ShapeDtypeStruct((T, AW), bf16), jax.ShapeDtypeStruct((T, AW), f32)],
        compiler_params=_cp("parallel"),
    )(dx1b, wo)


def _pool_bwd(dpool, pooled, wp, scale):
    tm = 1024
    n = T // tm

    def body(dp_ref, pb_ref, wp_ref, sc_ref, du_ref, dsc_ref, dwp_ref, ext):
        i = pl.program_id(0)

        @pl.when(i == 0)
        def _():
            ext[tm:tm + HALO, :] = jnp.zeros((HALO, AW), f32)
            dsc_ref[...] = jnp.zeros_like(dsc_ref)
            dwp_ref[...] = jnp.zeros_like(dwp_ref)

        t_idx = (n - 1 - i) * tm + lax.broadcasted_iota(jnp.int32, (tm, 1), 0)
        for g, w in enumerate(WINDOWS):
            lo, hi = 128 * g, 128 * (g + 1)
            pb = pb_ref[:, lo:hi]
            mixed = jnp.dot(pb, wp_ref[g], preferred_element_type=f32)
            dpo = dp_ref[:, lo:hi]
            dsc_ref[:, lo:hi] += jnp.sum(dpo * mixed, axis=0, keepdims=True)
            dmr = (dpo * sc_ref[:, lo:hi]).astype(bf16)
            dwp_ref[g] += lax.dot_general(pb, dmr, TN, preferred_element_type=f32)
            dpl = lax.dot_general(dmr, wp_ref[g], NT, preferred_element_type=f32)
            cnt = jnp.minimum(t_idx + 1, w).astype(f32)
            dpn = dpl / cnt
            ext[0:tm, lo:hi] = dpn
            acc = dpn
            for d in range(1, w):
                acc = acc + ext[d:d + tm, lo:hi]
            du_ref[:, lo:hi] = (acc - dpl).astype(bf16)
        ext[tm:tm + HALO, :] = ext[0:HALO, :]

    rev = lambda i: (n - 1 - i, 0)
    return pl.pallas_call(
        body, name="pool_bwd", grid=(n,),
        in_specs=[pl.BlockSpec((tm, AW), rev), pl.BlockSpec((tm, AW), rev), _full((4, 128, 128)), _full((1, AW))],
        out_specs=[pl.BlockSpec((tm, AW), rev), _full((1, AW)), _full((4, 128, 128))],
        out_shape=[jax.ShapeDtypeStruct((T, AW), bf16), jax.ShapeDtypeStruct((1, AW), f32),
                   jax.ShapeDtypeStruct((4, 128, 128), f32)],
        scratch_shapes=[pltpu.VMEM((tm + HALO, AW), f32)],
        compiler_params=_cp("arbitrary"),
    )(dpool, pooled, wp, scale)


def _attn_bwd(qkv, qaug, kaug, attn, dattn, lse, dep):
    tq = tk = ATT_T
    n = T // tq
    it, jt = _causal_steps(True)
    nsteps = it.shape[0]

    rs = 64

    def body(it_ref, jt_ref, q_ref, k_ref, v_ref, qa_ref, ka_ref, o_ref, do_ref, lse_ref, dep_ref,
             dq_ref, dqs_ref, dk_ref, dks_ref, dv_ref, dq_acc, dk_acc, dv_acc, s_sc, dp_sc, p_sc, ds_sc):
        t = pl.program_id(1)
        i = it_ref[t]
        j = jt_ref[t]

        @pl.when(t == 0)
        def _():
            dq_acc[...] = jnp.zeros_like(dq_acc)

        @pl.when(i == j)
        def _():
            dk_acc[...] = jnp.zeros_like(dk_acc)
            dv_acc[...] = jnp.zeros_like(dv_acc)

        lane = lax.broadcasted_iota(jnp.int32, (tq, 128), 1)

        def step(on_diagonal):
            q = (q_ref[...].astype(f32) * Q_SCALE).astype(bf16)
            k = k_ref[...]
            v = v_ref[...]
            qa = qa_ref[...]
            ka = ka_ref[...]
            do = do_ref[...]
            dd = do.astype(f32) * o_ref[...].astype(f32)
            blocks = _row_blocks(tq, tk, on_diagonal)
            qes, kes, does, deltas = [], [], [], []
            for e in range(2):
                hm = (lane >= 64) if e else (lane < 64)
                qes.append(jnp.where(hm, q, qa))
                kes.append(jnp.where(hm, k, ka))
                does.append(jnp.where(hm, do, jnp.zeros_like(do)))
                deltas.append(jnp.sum(jnp.where(hm, dd, 0.0), axis=1, keepdims=True))
                for r0, r1, nc in blocks:
                    s_sc[e, r0:r1, 0:nc] = lax.dot_general(qes[e][r0:r1], kes[e][0:nc], NT, preferred_element_type=f32)
                    dp_sc[e, r0:r1, 0:nc] = lax.dot_general(does[e][r0:r1], v[0:nc], NT, preferred_element_type=f32)
            for e in range(2):
                for r0, r1, nc in blocks:
                    for r in range(r0, r1, rs):
                        s = s_sc[e, r:r + rs, 0:nc] - lse_ref[r:r + rs, 64 * e:64 * e + 1]
                        if on_diagonal:
                            row = lax.broadcasted_iota(jnp.int32, (rs, nc), 0) + r
                            col = lax.broadcasted_iota(jnp.int32, (rs, nc), 1)
                            s = jnp.where(col <= row, s, NEG)
                        p = jnp.exp2(s)
                        p_sc[e, r:r + rs, 0:nc] = p.astype(bf16)
                        ds_sc[e, r:r + rs, 0:nc] = (p * (dp_sc[e, r:r + rs, 0:nc] - deltas[e][r:r + rs, :])).astype(bf16)
                for r0, r1, nc in blocks:
                    dv_acc[:, 0:nc] += lax.dot_general(does[e][r0:r1], p_sc[e, r0:r1, 0:nc], TN, preferred_element_type=f32)
                    dsb = ds_sc[e, r0:r1, 0:nc]
                    dk_acc[e, :, 0:nc] += lax.dot_general(qes[e][r0:r1], dsb, TN, preferred_element_type=f32)
                    rq = pl.multiple_of(i * tq + r0, r1 - r0)
                    dq_acc[e, pl.ds(rq, r1 - r0), :] += jnp.dot(dsb, kes[e][0:nc], preferred_element_type=f32)

        @pl.when(i > j)
        def _():
            step(False)

        @pl.when(i == j)
        def _():
            step(True)

        @pl.when(i == n - 1)
        def _():
            dk0 = dk_acc[0].T
            dk1 = dk_acc[1].T
            dk_ref[...] = (jnp.where(lane < 64, dk0, dk1) * (1.0 / LOG2E)).astype(bf16)
            dks_ref[...] = jnp.where(lane < 64, dk1, dk0)
            dv_ref[...] = dv_acc[...].T.astype(bf16)

        @pl.when(t == nsteps - 1)
        def _():
            lane_t = lax.broadcasted_iota(jnp.int32, (T, 128), 1)
            dq_ref[...] = (jnp.where(lane_t < 64, dq_acc[0], dq_acc[1]) * 0.125).astype(bf16)
            dqs_ref[...] = jnp.where(lane_t < 64, dq_acc[1], dq_acc[0])

    qmap = lambda p, t, it, jt: (it[t], p)
    grid_spec = pltpu.PrefetchScalarGridSpec(
        num_scalar_prefetch=2, grid=(PAIRS, nsteps),
        in_specs=[pl.BlockSpec((tq, 128), qmap),
                  pl.BlockSpec((tk, 128), lambda p, t, it, jt: (jt[t], PAIRS + p)),
                  pl.BlockSpec((tk, 128), lambda p, t, it, jt: (jt[t], 2 * PAIRS + p)),
                  pl.BlockSpec((tq, 128), qmap), pl.BlockSpec((tk, 128), lambda p, t, it, jt: (jt[t], p)),
                  pl.BlockSpec((tq, 128), qmap), pl.BlockSpec((tq, 128), qmap),
                  pl.BlockSpec((None, tq, 128), lambda p, t, it, jt: (p, it[t], 0)),
                  pl.BlockSpec((8, 128), lambda p, t, it, jt: (0, 0))],
        out_specs=[pl.BlockSpec((T, 128), lambda p, t, it, jt: (0, p)),
                   pl.BlockSpec((None, T, 128), lambda p, t, it, jt: (p, 0, 0)),
                   pl.BlockSpec((tk, 128), lambda p, t, it, jt: (jt[t], p)),
                   pl.BlockSpec((None, tk, 128), lambda p, t, it, jt: (p, jt[t], 0)),
                   pl.BlockSpec((tk, 128), lambda p, t, it, jt: (jt[t], p))],
        scratch_shapes=[pltpu.VMEM((2, T, 128), f32), pltpu.VMEM((2, 128, tk), f32), pltpu.VMEM((128, tk), f32),
                        pltpu.VMEM((2, tq, tk), f32), pltpu.VMEM((2, tq, tk), f32), pltpu.VMEM((2, tq, tk), bf16),
                        pltpu.VMEM((2, tq, tk), bf16)],
    )
    return pl.pallas_call(
        body, name="fox_attn_bwd", grid_spec=grid_spec,
        out_shape=[jax.ShapeDtypeStruct((T, AW), bf16), jax.ShapeDtypeStruct((PAIRS, T, 128), f32),
                   jax.ShapeDtypeStruct((T, AW), bf16), jax.ShapeDtypeStruct((PAIRS, T, 128), f32),
                   jax.ShapeDtypeStruct((T, AW), bf16)],
        compiler_params=_cp("parallel", "arbitrary"),
    )(it, jt, qkv, qkv, qkv, qaug, kaug, attn, dattn, lse, dep)


def _fox_cumsum_bwd(dqs, dks, fl, bfp):
    tb = CUMSUM_ROWS
    nb = T // tb

    def body(dqs_ref, dks_ref, fl_ref, b_ref, df_ref, db_ref, carry):
        i = pl.program_id(0)

        @pl.when(i == 0)
        def _():
            carry[...] = jnp.zeros_like(carry)
            db_ref[...] = jnp.zeros_like(db_ref)

        r = lax.broadcasted_iota(jnp.int32, (128, 128), 0)
        cc = lax.broadcasted_iota(jnp.int32, (128, 128), 1)
        pick = lambda even_lane, odd_lane, p: jnp.logical_or(
            jnp.logical_and(r == even_lane, cc == 2 * p), jnp.logical_and(r == odd_lane, cc == 2 * p + 1)).astype(bf16)
        dc = jnp.zeros((tb, 128), f32)
        for p in range(PAIRS):
            rows_at = pick(SPARE[0] + ROW_SUM_LANE, SPARE[1] + ROW_SUM_LANE, p)
            cols_at = pick(SPARE[0] + COL_SUM_LANE, SPARE[1] + COL_SUM_LANE, p)
            dc = dc + _dot01(rows_at, dqs_ref[p], False) - _dot01(cols_at, dks_ref[p], False)
        rt = lax.broadcasted_iota(jnp.int32, (tb, tb), 0)
        ct = lax.broadcasted_iota(jnp.int32, (tb, tb), 1)
        utri = (ct >= rt).astype(bf16)
        dl = _dot01(utri, dc, True) + carry[0:1, :]
        carry[...] = jnp.broadcast_to(dl[0:1, :], (8, 128))
        z = fl_ref[...] + b_ref[...]
        df = dl * jax.nn.sigmoid(-z)
        df_ref[...] = df.astype(bf16)
        db_ref[...] += jnp.sum(df, axis=0, keepdims=True)

    rev = lambda i: (nb - 1 - i, 0)
    return pl.pallas_call(
        body, name="fox_cumsum_bwd", grid=(nb,),
        in_specs=[pl.BlockSpec((PAIRS, tb, 128), lambda i: (0, nb - 1 - i, 0)),
                  pl.BlockSpec((PAIRS, tb, 128), lambda i: (0, nb - 1 - i, 0)),
                  pl.BlockSpec((tb, 128), rev), _full((1, 128))],
        out_specs=[pl.BlockSpec((tb, 128), rev), _full((1, 128))],
        out_shape=[jax.ShapeDtypeStruct((T, 128), bf16), jax.ShapeDtypeStruct((1, 128), f32)],
        scratch_shapes=[pltpu.VMEM((8, 128), f32)],
        compiler_params=_cp("arbitrary"),
    )(dqs, dks, fl, bfp)


def _inproj_bwd(dq, dk, dv, du, df, w, x, dx1, g1):
    tm = 512

    def body(dq_ref, dk_ref, dv_ref, du_ref, df_ref, w_ref, x_ref, dx1_ref, g_ref, dx_ref, dn_ref):
        i = pl.program_id(0)

        @pl.when(i == 0)
        def _():
            dn_ref[...] = jnp.zeros_like(dn_ref)

        dproj = jnp.concatenate([dq_ref[...], dk_ref[...], dv_ref[...], du_ref[...], df_ref[...]], axis=1)
        dh = jnp.dot(dproj, w_ref[...], preferred_element_type=f32)
        xv = x_ref[...]
        r = lax.rsqrt(jnp.mean(xv * xv, axis=-1, keepdims=True) + EPS)
        xhat = xv * r
        dn_ref[...] += jnp.sum(dh * xhat, axis=0, keepdims=True)
        z = dh * g_ref[...]
        dx_ref[...] = dx1_ref[...] + r * (z - xhat * jnp.mean(z * xhat, axis=-1, keepdims=True))

    row = lambda i: (i, 0)
    return pl.pallas_call(
        body, name="inproj_bwd", grid=(T // tm,),
        in_specs=[pl.BlockSpec((tm, AW), row)] * 4 + [pl.BlockSpec((tm, 128), row), _full((W_ROWS, D)),
                                                       pl.BlockSpec((tm, D), row), pl.BlockSpec((tm, D), row), _full((1, D))],
        out_specs=[pl.BlockSpec((tm, D), row), _full((1, D))],
        out_shape=[jax.ShapeDtypeStruct((T, D), f32), jax.ShapeDtypeStruct((1, D), f32)],
        compiler_params=_cp("arbitrary"),
    )(dq, dk, dv, du, df, w, x, dx1, g1)


def _adamw_math(w, g, m, v):
    m = B1 * m + (1.0 - B1) * g
    v = B2 * v + (1.0 - B2) * (g * g)
    m_hat = m / (1.0 - B1 ** STEP)
    v_hat = v / (1.0 - B2 ** STEP)
    delta = -LR * (m_hat / (jnp.sqrt(v_hat) + AEPS) + WD * w)
    return delta, m, v


def _adamw_shard(w, m, v, p_mine, p_other, name):
    rows, rest = w.shape[0], tuple(w.shape[1:])
    tr = rows if rows <= IN_S else rows // 2

    def body(w_ref, m_ref, v_ref, a_ref, b_ref, g_ref, d_ref, nm_ref, nv_ref):
        g = a_ref[...].astype(f32) + b_ref[...].astype(f32)
        g_ref[...] = g
        d_ref[...], nm_ref[...], nv_ref[...] = _adamw_math(w_ref[...], g, m_ref[...], v_ref[...])

    spec = pl.BlockSpec((tr,) + rest, lambda i: (i,) + (0,) * len(rest))
    return pl.pallas_call(
        body, name=name, grid=(rows // tr,), in_specs=[spec] * 5, out_specs=[spec] * 4,
        out_shape=[jax.ShapeDtypeStruct(w.shape, f32)] * 4, compiler_params=_cp("parallel"),
    )(w, m, v, p_mine, p_other)


SMALL_SLOTS = ((0, 8, 128), (8, 16, 128), (16, 24, 128), (24, 28, 128), (32, 33, 8))
LOSS_ROW = 39


def _adamw_small(ws, ms, vs, parts, parts_wp):
    n = len(ws)

    def body(*refs):
        w_refs, m_refs, v_refs = refs[0:n], refs[n:2 * n], refs[2 * n:3 * n]
        p_ref, pw_ref = refs[3 * n], refs[3 * n + 1]
        outs = refs[3 * n + 2:]
        g_all = p_ref[0]
        g_wp = pw_ref[0]
        for k in range(1, 8):
            g_all = g_all + p_ref[k]
            g_wp = g_wp + pw_ref[k]
        grads = [g_all[r0:r1, 0:lanes] for r0, r1, lanes in SMALL_SLOTS] + [g_wp]
        for idx, g in enumerate(grads):
            d, nm, nv = _adamw_math(w_refs[idx][...], g, m_refs[idx][...], v_refs[idx][...])
            outs[idx][...] = g
            outs[n + idx][...] = d
            outs[2 * n + idx][...] = nm
            outs[3 * n + idx][...] = nv
        outs[4 * n][...] = g_all[LOSS_ROW:LOSS_ROW + 1, :]

    shapes = [jax.ShapeDtypeStruct(w.shape, f32) for w in ws]
    res = pl.pallas_call(
        body, name="adamw_small", out_shape=shapes * 4 + [jax.ShapeDtypeStruct((1, 128), f32)],
    )(*ws, *ms, *vs, parts, parts_wp)
    return res[:4 * n], res[4 * n]


def _sum4(recv, g, mine, name):
    _, rows, cols = recv.shape
    tr = rows if rows <= IN_S else rows // 2

    def body(mine_ref, r_ref, g_ref, o_ref):
        o_ref[...] = ((g_ref[...].astype(f32) + r_ref[0].astype(f32))
                      + (r_ref[1].astype(f32) + r_ref[2].astype(f32))).astype(bf16)

    grid_spec = pltpu.PrefetchScalarGridSpec(
        num_scalar_prefetch=1, grid=(rows // tr,),
        in_specs=[pl.BlockSpec((3, tr, cols), lambda i, m: (0, i, 0)),
                  pl.BlockSpec((None, tr, cols), lambda i, m: (m[0], i, 0))],
        out_specs=pl.BlockSpec((tr, cols), lambda i, m: (i, 0)))
    return pl.pallas_call(
        body, name=name, grid_spec=grid_spec, out_shape=jax.ShapeDtypeStruct((rows, cols), bf16),
        compiler_params=_cp("arbitrary"),
    )(mine, recv, g)


_HBM = pl.BlockSpec(memory_space=pltpu.HBM)
_SEM = pl.BlockSpec(memory_space=pltpu.SEMAPHORE)
_EFFECT = pltpu.SideEffectType.DATAFLOW_SIDE_EFFECTING


def _in_hbm(a):
    return pltpu.with_memory_space_constraint(a, pltpu.HBM)


def _mesh_pos():
    return lax.axis_index("x"), lax.axis_index("y"), lax.axis_index("c")


def _other_chips(x, y):
    return [(1 - x, y), (x, 1 - y), (1 - x, 1 - y)]


def _gather_copy(srcs, lands, send_sems, recv_sems, a, k, slot):
    x, y, c = _mesh_pos()
    cx, cy = _other_chips(x, y)[k]
    return pltpu.make_async_remote_copy(
        src_ref=srcs[a], dst_ref=lands[a].at[slot], send_sem=send_sems.at[3 * a + k], recv_sem=recv_sems.at[3 * a + k],
        device_id=(cx, cy, c), device_id_type=MESH)


def _scatter_copy(srcs, lands, send_sems, recv_sems, a, k):
    x, y, c = _mesh_pos()
    cx, cy = _other_chips(x, y)[k]
    return pltpu.make_async_remote_copy(
        src_ref=srcs[a].at[2 * cx + cy], dst_ref=lands[a].at[k], send_sem=send_sems.at[3 * a + k],
        recv_sem=recv_sems.at[3 * a + k], device_id=(cx, cy, c), device_id_type=MESH)


def _all_gather_w_in(part):
    cols = part.shape[1] // 2

    def body(src, dst, send_sems, recv_sems, loc_sem):
        x, y, c = _mesh_pos()
        mine = 2 * x + y
        chips = _other_chips(x, y)
        half = lambda ref, cc: ref.at[:, pl.ds(pl.multiple_of(cc * cols, cols), cols)]

        def over_ici(k, slot):
            cx, cy = chips[k]
            return pltpu.make_async_remote_copy(
                src_ref=half(src, c), dst_ref=half(dst.at[slot], c), send_sem=send_sems.at[k], recv_sem=recv_sems.at[k],
                device_id=(cx, cy, c), device_id_type=MESH)

        def to_sibling(k, cc):
            slot = 2 * chips[k][0] + chips[k][1]
            return pltpu.make_async_remote_copy(
                src_ref=half(dst.at[slot], cc), dst_ref=half(dst.at[slot], cc), send_sem=send_sems.at[3 + k],
                recv_sem=recv_sems.at[3 + k], device_id=(x, y, 1 - c), device_id_type=MESH)

        local = pltpu.make_async_copy(src, dst.at[mine], loc_sem.at[0])
        local.start()
        first = [over_ici(k, mine) for k in range(3)]
        for cp in first:
            cp.start()
        passed = [to_sibling(k, c) for k in range(3)]
        for k in range(3):
            over_ici(k, 2 * chips[k][0] + chips[k][1]).wait_recv()
            passed[k].start()
        for k in range(3):
            to_sibling(k, 1 - c).wait_recv()
        for cp in first + passed:
            cp.wait_send()
        local.wait()

    return pl.pallas_call(
        body, name="all_gather_w_in", in_specs=[_HBM], out_specs=_HBM,
        out_shape=jax.ShapeDtypeStruct((NSH,) + part.shape, part.dtype),
        scratch_shapes=[pltpu.SemaphoreType.DMA((6,)), pltpu.SemaphoreType.DMA((6,)), pltpu.SemaphoreType.DMA((1,))],
    )(part)


def _split_start(name, srcs, lands, n_sems, plan, dep):
    n, nl = len(srcs), len(lands)

    def body(*refs):
        src_refs, land_refs = refs[:n], refs[n:n + nl]
        send_sems, recv_sems = refs[n + nl + 1], refs[n + nl + 2]
        token = refs[-1]
        sends, _ = plan(src_refs, land_refs, send_sems, recv_sems)
        for cp in sends:
            cp.start()
        token[...] = jnp.zeros_like(token)

    outs = pl.pallas_call(
        body, name=name,
        in_specs=[_HBM] * (n + nl) + [pl.BlockSpec(memory_space=pl.ANY)],
        out_specs=[_SEM, _SEM] + [_HBM] * (n + nl) + [pl.BlockSpec(memory_space=pltpu.VMEM)],
        out_shape=[pltpu.SemaphoreType.DMA((n_sems,)), pltpu.SemaphoreType.DMA((n_sems,))]
        + [pltpu.HBM(a.shape, a.dtype) for a in list(srcs) + list(lands)] + [jax.ShapeDtypeStruct((8, 128), f32)],
        input_output_aliases={i: 2 + i for i in range(n + nl)},
        compiler_params=pltpu.CompilerParams(has_side_effects=_EFFECT),
    )(*[_in_hbm(a) for a in list(srcs) + list(lands)], dep)
    return outs[0], outs[1], list(outs[2:2 + n]), list(outs[2 + n:2 + n + nl]), outs[-1]


def _split_wait(name, send_sems, recv_sems, srcs, lands, after, plan):
    n, nl = len(srcs), len(lands)

    def body(*refs):
        src_refs, land_refs = refs[:n], refs[n:n + nl]
        s_sems, r_sems = refs[n + nl], refs[n + nl + 1]
        sends, recvs = plan(src_refs, land_refs, s_sems, r_sems)
        for cp in recvs:
            cp.wait_recv()
        for cp in sends:
            cp.wait_send()

    outs = pl.pallas_call(
        body, name=name,
        in_specs=[_HBM] * (n + nl) + [_SEM, _SEM, pl.BlockSpec(memory_space=pl.ANY)],
        out_specs=[_HBM] * (n + nl),
        out_shape=[pltpu.HBM(a.shape, a.dtype) for a in list(srcs) + list(lands)],
        input_output_aliases={i: i for i in range(n + nl)},
        compiler_params=pltpu.CompilerParams(has_side_effects=_EFFECT),
    )(*srcs, *lands, send_sems, recv_sems, after)
    return list(outs[:n]), list(outs[n:])


def _gather_plan(srcs, lands, ss, rs):
    x, y, _ = _mesh_pos()
    chips = _other_chips(x, y)
    sends = [_gather_copy(srcs, lands, ss, rs, a, k, 2 * x + y) for a in range(len(srcs)) for k in range(3)]
    recvs = [_gather_copy(srcs, lands, ss, rs, a, k, 2 * chips[k][0] + chips[k][1])
             for a in range(len(srcs)) for k in range(3)]
    return sends, recvs


def _scatter_plan(srcs, lands, ss, rs):
    cps = [_scatter_copy(srcs, lands, ss, rs, a, k) for a in range(len(srcs)) for k in range(3)]
    return cps, cps


def _scatter_and_spread_plan(srcs, lands, ss, rs):
    x, y, c = _mesh_pos()
    me = 4 * x + 2 * y + c
    n = len(srcs) - 1
    cps = [_scatter_copy(srcs[:n], lands[:n], ss, rs, a, k) for a in range(n) for k in range(3)]
    for f in range(1, 8):
        peer = ((x + (f >> 2)) % 2, (y + ((f >> 1) & 1)) % 2, (c + (f & 1)) % 2)
        cps.append(pltpu.make_async_remote_copy(
            src_ref=srcs[n], dst_ref=lands[n].at[me], send_sem=ss.at[3 * n - 1 + f], recv_sem=rs.at[3 * n - 1 + f],
            device_id=peer, device_id_type=MESH))
    return cps, cps


def _swap_with_sibling(parts, name):
    n = len(parts)

    def body(*refs):
        srcs, dsts = refs[:n], refs[n:2 * n]
        send_sems, recv_sems = refs[2 * n:]
        x, y, c = _mesh_pos()
        cps = [pltpu.make_async_remote_copy(src_ref=srcs[a], dst_ref=dsts[a], send_sem=send_sems.at[a],
                                            recv_sem=recv_sems.at[a], device_id=(x, y, 1 - c), device_id_type=MESH)
               for a in range(n)]
        for cp in cps:
            cp.start()
        for cp in cps:
            cp.wait_recv()
        for cp in cps:
            cp.wait_send()

    return pl.pallas_call(
        body, name=name, in_specs=[_HBM] * n, out_specs=[_HBM] * n,
        out_shape=[jax.ShapeDtypeStruct(p.shape, p.dtype) for p in parts],
        scratch_shapes=[pltpu.SemaphoreType.DMA((n,)), pltpu.SemaphoreType.DMA((n,))],
    )(*parts)


def _forward(x, tgt, w_in_t, mlp_w_fn, g1, bfp, wp, scale, g2, gf, dep):
    h, qkv, u, fl = _rms_inproj(x, g1, w_in_t, dep)
    qaug, kaug = _fox_cumsum(fl, bfp)
    attn, lse = _attn_fwd(qkv, qaug, kaug)
    pooled, pool = _pool_fwd(u, wp, scale)
    wo, wgt, wut, wd = mlp_w_fn(attn)
    x1, h2 = _outproj(x, attn, pool, wo, g2)
    loss, dgf, dx2, dx2b, ud, silu, a_b = _mlp_fwd_loss(h2, x1, wgt, wut, wd, tgt, gf)
    saved = dict(h=h, qkv=qkv, fl=fl, qaug=qaug, kaug=kaug, attn=attn, lse=lse, pooled=pooled, pool=pool, x1=x1, h2=h2,
                 ud=ud, silu=silu, a_b=a_b, wo=wo, wgt=wgt, wut=wut, wd=wd)
    return loss, dgf, dx2, dx2b, saved


def _backward_mlp(sv, dx2, dx2b, g2):
    dgate, dup, dx1, dx1b, dg2 = _mlp_bwd(dx2b, dx2, sv["ud"], sv["silu"], sv["wgt"], sv["wut"], sv["wd"], sv["x1"], g2)
    (dwd,) = _mm_tn(sv["a_b"], [dx2b], "dw_down", a_sharded=True, tk=T)
    (dwgt,) = _mm_tn(dgate, [sv["h2"]], "dw_gate", a_sharded=True, tk=T)
    (dwut,) = _mm_tn(dup, [sv["h2"]], "dw_up", a_sharded=True, tk=T)
    return dx1, dx1b, dg2, (dwgt, dwut, dwd)


def _backward_outproj_pool(sv, dx1b, wp, scale):
    dattn, dpool = _outproj_bwd(dx1b, sv["wo"])
    dwo_a, = _mm_tn(sv["attn"], [dx1b], "dw_out_attn", tk=2048)
    dwo_p, = _mm_tn(sv["pool"], [dx1b], "dw_out_pool", tk=2048)
    dwo = jnp.concatenate([dwo_a, dwo_p], axis=0).reshape(NSH, D // NSH, D)
    du, dscale, dwp = _pool_bwd(dpool, sv["pooled"], wp, scale)
    return dattn, dwo, du, dscale, dwp


def _backward_attn_inproj(sv, x, dx1, dattn, du, w_in_t, g1, bfp, dep):
    dq, dqs, dk, dks, dv = _attn_bwd(sv["qkv"], sv["qaug"], sv["kaug"], sv["attn"], dattn, sv["lse"], dep)
    df, dbf = _fox_cumsum_bwd(dqs, dks, sv["fl"], bfp)
    dx, dg1 = _inproj_bwd(dq, dk, dv, du, df, w_in_t, x, dx1, g1)
    dwq, dwk, dwv, dwu_in, dwf = _mm_tn_rows([dq, dk, dv, du, df], sv["h"], "dw_in")
    dwin = jnp.concatenate([dwq, dwk, dwv, dwf[0:8], dwu_in], axis=0)
    return dx, dg1, dbf, dwin.reshape(NSH, IN_S, D)


def kernel(x, norm1_g, w_in, b_forget, w_pool, pool_scale, w_out, norm2_g, w_gate, w_up, w_down, final_g, loss_target, m_norm1_g, m_w_in, m_b_forget, m_w_pool, m_pool_scale, m_w_out, m_norm2_g, m_w_gate, m_w_up, m_w_down, m_final_g, v_norm1_g, v_w_in, v_b_forget, v_w_pool, v_pool_scale, v_w_out, v_norm2_g, v_w_gate, v_w_up, v_w_down, v_final_g):
    mine = (2 * lax.axis_index("x") + lax.axis_index("y")).astype(jnp.int32)
    mine1 = mine.reshape(1)
    tr = lambda a: jnp.transpose(a[0])

    win4 = _all_gather_w_in(tr(w_in).astype(bf16))
    later = [w_out[0].astype(bf16), tr(w_gate).astype(bf16), tr(w_up).astype(bf16), w_down[0].astype(bf16)]
    lands = [lax.dynamic_update_slice(lax.empty((NSH,) + p.shape, bf16), p[None], (mine, 0, 0)) for p in later]
    ag_send, ag_recv, later_thru, lands_thru, ag_token = _split_start("all_gather_start", later, lands, 12, _gather_plan,
                                                                      win4)
    win = win4.reshape(IN_W, D)
    w_in_t = jnp.concatenate([win[0:3 * AW], win[3 * AW + 8:], win[3 * AW:3 * AW + 8], jnp.zeros((120, D), bf16)], axis=0)
    bfp = jnp.pad(b_forget, ((0, 0), (0, 120)))
    wp = w_pool[0].astype(bf16)
    gf = final_g.reshape(1, D)

    def later_weights(after):
        _, (wo4, wgt, wut, wd) = _split_wait("all_gather_wait", ag_send, ag_recv, later_thru, lands_thru, after, _gather_plan)
        return wo4.reshape(D, D), wgt, wut, wd

    xe, tgt = x[0], loss_target[0]
    loss_v, dgf, dx2, dx2b, sv = _forward(xe, tgt, w_in_t, later_weights, norm1_g, bfp, wp, pool_scale, norm2_g, gf, ag_token)
    dx1, dx1b, dg2, mlp_grads = _backward_mlp(sv, dx2, dx2b, norm2_g)
    dattn, dwo, du, dscale, dwp = _backward_outproj_pool(sv, dx1b, wp, pool_scale)
    me = (4 * lax.axis_index("x") + 2 * lax.axis_index("y") + lax.axis_index("c")).astype(jnp.int32)
    dwp = dwp.reshape(512, 128)
    first = [dwo] + list(mlp_grads) + [dwp]
    first_lands = [lax.empty((3,) + g.shape[1:], bf16) for g in first[:4]]
    first_lands.append(lax.dynamic_update_slice(lax.empty((8, 512, 128), f32), dwp[None], (me, 0, 0)))
    rs_send, rs_recv, first_thru, first_lands_thru, rs_token = _split_start(
        "reduce_scatter_start", first, first_lands, 19, _scatter_and_spread_plan, du)
    dx, dg1, dbf, dwin = _backward_attn_inproj(sv, xe, dx1, dattn, du, w_in_t, norm1_g, bfp, rs_token)

    pad8 = lambda r: jnp.pad(r, ((0, 8 - r.shape[0]), (0, 0)))
    loss_rows = jnp.concatenate([dbf, jnp.zeros((6, 128), f32), loss_v[0:1, :]], axis=0)
    small = jnp.concatenate([dg1.reshape(8, 128), dg2.reshape(8, 128), dgf.reshape(8, 128), pad8(dscale.reshape(4, 128)),
                             loss_rows], axis=0)
    small_land = lax.dynamic_update_slice(lax.empty((8, SMALL_ROWS, 128), f32), small[None], (me, 0, 0))
    tail_send, tail_recv, tail_thru, tail_lands_thru, tail_token = _split_start(
        "tail_start", [dwin, small], [lax.empty((3,) + dwin.shape[1:], bf16), small_land], 10, _scatter_and_spread_plan,
        dx)
    first_thru, first_recv = _split_wait("reduce_scatter_wait", rs_send, rs_recv, first_thru, first_lands_thru, tail_token,
                                         _scatter_and_spread_plan)
    wp_all = first_recv[4]
    tr3 = lambda a: jnp.transpose(a, (2, 0, 1))
    ws = [tr3(w_in), w_out[0], tr(w_gate), tr(w_up), w_down[0]]
    ms = [tr3(m_w_in), m_w_out[0], tr(m_w_gate), tr(m_w_up), m_w_down[0]]
    vs = [tr3(v_w_in), v_w_out[0], tr(v_w_gate), tr(v_w_up), v_w_down[0]]
    partial = [_sum4(r, g, mine1, f"sum4_{i + 1}") for i, (r, g) in enumerate(zip(first_recv[:4], first_thru[:4]))]
    other = _swap_with_sibling(partial, "swap_first")
    big = [_adamw_shard(ws[i + 1], ms[i + 1], vs[i + 1], partial[i], other[i], f"adamw_{i + 1}") for i in range(4)]
    (dwin_thru, _), (in_recv_land, small_all) = _split_wait("tail_wait", tail_send, tail_recv, tail_thru, tail_lands_thru,
                                                            big[3][0], _scatter_and_spread_plan)
    partial_in = _sum4(in_recv_land, dwin_thru, mine1, "sum4_0")
    (other_in,) = _swap_with_sibling([partial_in], "swap_in")
    big = [_adamw_shard(ws[0], ms[0], vs[0], partial_in.reshape(IN_S, 1, D), other_in.reshape(IN_S, 1, D), "adamw_0")] + big

    small_names = ["norm1_g", "norm2_g", "final_g", "pool_scale", "b_forget", "w_pool"]
    rows = lambda a, b, c, d, e, f: [a.reshape(8, 128), b.reshape(8, 128), c.reshape(8, 128), d.reshape(4, 128),
                                     e.reshape(1, 8), f.reshape(512, 128)]
    sm, loss_row = _adamw_small(rows(norm1_g, norm2_g, final_g, pool_scale, b_forget, w_pool),
                                rows(m_norm1_g, m_norm2_g, m_final_g, m_pool_scale, m_b_forget, m_w_pool),
                                rows(v_norm1_g, v_norm2_g, v_final_g, v_pool_scale, v_b_forget, v_w_pool), small_all, wp_all)
    small_shape = dict(norm1_g=(1, D), norm2_g=(1, D), final_g=(D,), pool_scale=(1, AW), b_forget=(1, 8),
                       w_pool=(1, 4, 128, 128))

    order = ["norm1_g", "w_in", "b_forget", "w_pool", "pool_scale", "w_out", "norm2_g", "w_gate", "w_up", "w_down", "final_g"]
    big_idx = {"w_in": 0, "w_out": 1, "w_gate": 2, "w_up": 3, "w_down": 4}
    outs = [loss_row[0, 0], dx[None]]
    for kind in range(4):
        for name in order:
            if name == "w_in":
                outs.append(jnp.transpose(big[0][kind], (1, 2, 0)))
            elif name in ("w_gate", "w_up"):
                outs.append(jnp.transpose(big[big_idx[name]][kind])[None])
            elif name in big_idx:
                outs.append(big[big_idx[name]][kind][None])
            else:
                outs.append(sm[6 * kind + small_names.index(name)].reshape(small_shape[name]))
    return tuple(outs)
```

```python
import jax
import jax.numpy as jnp
import numpy as np
from jax import lax
from jax.experimental import pallas as pl
from jax.experimental.pallas import tpu as pltpu

f32 = jnp.float32
bf16 = jnp.bfloat16

T = 4096
D = 1024
NSH = 4
IN_W = 2056
IN_S = IN_W // NSH
AW = 512
PAIRS = 4
SPARE = (64, 0)
ROW_SUM_LANE, COL_SUM_LANE = 0, 3
FF = 2816
FS = FF // NSH
WINDOWS = (2, 4, 8, 16)
HALO = 16
EPS = 1e-6
NEG = -1e30
LR, B1, B2, AEPS, WD, STEP = 0.001, 0.9, 0.999, 1e-08, 0.01, 10
SMALL_ROWS = 40

NT = (((1,), (1,)), ((), ()))
TN = (((0,), (0,)), ((), ()))

MESH = pl.DeviceIdType.MESH


def _cp(*sem):
    return pltpu.CompilerParams(dimension_semantics=sem)


def _full(shape):
    n = len(shape)
    return pl.BlockSpec(shape, lambda *_: (0,) * n)


def _resident(shape):
    n = len(shape)
    return pl.BlockSpec(shape, lambda *_: (0,) * n, pipeline_mode=pl.Buffered(1))


W_ROWS = 4 * AW + 128


def _rms_inproj(x, g1, w, dep):
    tm = 512

    def body(x_ref, g_ref, w_ref, dep_ref, h_ref, qkv_ref, u_ref, fl_ref):
        xv = x_ref[...]
        r = lax.rsqrt(jnp.mean(xv * xv, axis=-1, keepdims=True) + EPS)
        h = (xv * r * g_ref[...]).astype(bf16)
        h_ref[...] = h
        qkv_ref[...] = lax.dot_general(h, w_ref[0:3 * AW, :], NT, preferred_element_type=f32).astype(bf16)
        u_ref[...] = lax.dot_general(h, w_ref[3 * AW:4 * AW, :], NT, preferred_element_type=f32)
        fl_ref[...] = lax.dot_general(h, w_ref[4 * AW:W_ROWS, :], NT, preferred_element_type=f32)

    return pl.pallas_call(
        body, name="rms_inproj", grid=(T // tm,),
        in_specs=[pl.BlockSpec((tm, D), lambda i: (i, 0)), _full((1, D)), _full((W_ROWS, D)), _full((8, 128))],
        out_specs=[pl.BlockSpec((tm, D), lambda i: (i, 0)), pl.BlockSpec((tm, 3 * AW), lambda i: (i, 0)),
                   pl.BlockSpec((tm, AW), lambda i: (i, 0)), pl.BlockSpec((tm, 128), lambda i: (i, 0))],
        out_shape=[jax.ShapeDtypeStruct((T, D), bf16), jax.ShapeDtypeStruct((T, 3 * AW), bf16),
                   jax.ShapeDtypeStruct((T, AW), f32), jax.ShapeDtypeStruct((T, 128), f32)],
        compiler_params=_cp("parallel"),
    )(x, g1, w, dep)


CUMSUM_ROWS = 512
FS_CHUNKS = ((0, 256), (256, 512), (512, FS))


def _log_sigmoid(z):
    return jnp.minimum(z, 0.0) - jnp.log(1.0 + jnp.exp(-jnp.abs(z)))


def _split3(x):
    hi = x.astype(bf16)
    r1 = x - hi.astype(f32)
    mid = r1.astype(bf16)
    return hi, mid, (r1 - mid.astype(f32)).astype(bf16)


def _dot01(sel, x, sel_first):
    parts = _split3(x)
    if sel_first:
        return sum(jnp.dot(sel, p, preferred_element_type=f32) for p in parts)
    return sum(jnp.dot(p, sel, preferred_element_type=f32) for p in parts)


def _fox_cumsum(fl, bfp):
    tb = CUMSUM_ROWS
    nb = T // tb

    def body(fl_ref, b_ref, qa_ref, ka_ref, carry):
        i = pl.program_id(0)

        @pl.when(i == 0)
        def _():
            carry[...] = jnp.zeros_like(carry)

        lf = _log_sigmoid(fl_ref[...] + b_ref[...])
        r = lax.broadcasted_iota(jnp.int32, (tb, tb), 0)
        cc = lax.broadcasted_iota(jnp.int32, (tb, tb), 1)
        ltri = (cc <= r).astype(bf16)
        cb = _dot01(ltri, lf, True) + carry[0:1, :]
        carry[...] = jnp.broadcast_to(cb[tb - 1:tb, :], (8, 128))
        hi, mid, lo = _split3(cb * LOG2E)
        head = lax.broadcasted_iota(jnp.int32, (128, AW), 0)
        col = lax.broadcasted_iota(jnp.int32, (128, AW), 1)
        base = 128 * (head >> 1) + jnp.where((head & 1) == 0, SPARE[0], SPARE[1])
        place = lambda off: jnp.logical_and(col == base + off, head < 8).astype(bf16)
        mm = lambda a, off: jnp.dot(a, place(off), preferred_element_type=f32)
        cq = mm(hi, 0) + mm(mid, 1) + mm(lo, 2)
        ck = mm(hi, 3) + mm(mid, 4) + mm(lo, 5)
        within = jnp.bitwise_and(lax.broadcasted_iota(jnp.int32, (tb, AW), 1), 63)
        qa_ref[...] = jnp.where(jnp.logical_and(within >= 3, within <= 5), 1.0, cq).astype(bf16)
        ka_ref[...] = jnp.where(within <= 2, 1.0, -ck).astype(bf16)

    return pl.pallas_call(
        body, name="fox_cumsum", grid=(nb,),
        in_specs=[pl.BlockSpec((tb, 128), lambda i: (i, 0)), _full((1, 128))],
        out_specs=[pl.BlockSpec((tb, AW), lambda i: (i, 0)), pl.BlockSpec((tb, AW), lambda i: (i, 0))],
        out_shape=[jax.ShapeDtypeStruct((T, AW), bf16), jax.ShapeDtypeStruct((T, AW), bf16)],
        scratch_shapes=[pltpu.VMEM((8, 128), f32)],
        compiler_params=_cp("arbitrary"),
    )(fl, bfp)


ATT_T = 512
LOG2E = 1.4426950408889634
Q_SCALE = 0.125 * LOG2E


def _causal_steps(key_major):
    n = T // ATT_T
    if key_major:
        pairs = [(i, j) for j in range(n) for i in range(j, n)]
    else:
        pairs = [(i, j) for i in range(n) for j in range(i + 1)]
    it = np.array([p[0] for p in pairs], np.int32)
    jt = np.array([p[1] for p in pairs], np.int32)
    return jnp.asarray(it), jnp.asarray(jt)


def _row_blocks(tq, tk, on_diagonal):
    return ((0, tq // 2, tk // 2), (tq // 2, tq, tk)) if on_diagonal else ((0, tq, tk),)


def _attn_fwd(qkv, qaug, kaug):
    tq = tk = ATT_T
    it, jt = _causal_steps(False)
    nsteps = it.shape[0]

    rs = 64

    def body(it_ref, jt_ref, q_ref, k_ref, v_ref, qa_ref, ka_ref, o_ref, lse_ref, m_sc, acc_sc, s_sc, p_sc, alpha_sc):
        t = pl.program_id(1)
        i = it_ref[t]
        j = jt_ref[t]

        @pl.when(j == 0)
        def _():
            m_sc[...] = jnp.full_like(m_sc, NEG)
            acc_sc[...] = jnp.zeros_like(acc_sc)

        lane = lax.broadcasted_iota(jnp.int32, (tq, 128), 1)
        spare = SPARE

        def step(on_diagonal):
            q = (q_ref[...].astype(f32) * Q_SCALE).astype(bf16)
            k = k_ref[...]
            v = v_ref[...]
            qa = qa_ref[...]
            ka = ka_ref[...]
            blocks = _row_blocks(tq, tk, on_diagonal)
            for e in range(2):
                hm = (lane >= 64) if e else (lane < 64)
                qe = jnp.where(hm, q, qa)
                ke = jnp.where(hm, k, ka)
                for r0, r1, nc in blocks:
                    s_sc[e, r0:r1, 0:nc] = lax.dot_general(qe[r0:r1], ke[0:nc], NT, preferred_element_type=f32)
            for e in range(2):
                for r0, r1, nc in blocks:
                    for r in range(r0, r1, rs):
                        s = s_sc[e, r:r + rs, 0:nc]
                        if on_diagonal:
                            row = lax.broadcasted_iota(jnp.int32, (rs, nc), 0) + r
                            col = lax.broadcasted_iota(jnp.int32, (rs, nc), 1)
                            s = jnp.where(col <= row, s, NEG)
                        m_prev = m_sc[e, r:r + rs, :]
                        m_new = jnp.maximum(m_prev, jnp.max(s, axis=1, keepdims=True))
                        p_sc[e, r:r + rs, 0:nc] = jnp.exp2(s - jnp.tile(m_new, (1, nc // 128))).astype(bf16)
                        alpha_sc[e, r:r + rs, :] = jnp.exp2(m_prev - m_new)
                        m_sc[e, r:r + rs, :] = m_new
            for e in range(2):
                hm = (lane >= 64) if e else (lane < 64)
                ve = jnp.where(hm, v, (lane == spare[e]).astype(bf16))
                for r0, r1, nc in blocks:
                    acc_sc[e, r0:r1] = (alpha_sc[e, r0:r1] * acc_sc[e, r0:r1]
                                        + jnp.dot(p_sc[e, r0:r1, 0:nc], ve[0:nc], preferred_element_type=f32))

        @pl.when(j < i)
        def _():
            step(False)

        @pl.when(j == i)
        def _():
            step(True)
            l0 = acc_sc[0][:, spare[0]:spare[0] + 1]
            l1 = acc_sc[1][:, spare[1]:spare[1] + 1]
            o_ref[...] = jnp.where(lane < 64, acc_sc[0] / l0, acc_sc[1] / l1).astype(bf16)
            lse_ref[...] = jnp.where(lane < 64, m_sc[0] + jnp.log2(l0), m_sc[1] + jnp.log2(l1))

    qmap = lambda p, t, it, jt: (it[t], p)
    kmap = lambda p, t, it, jt: (jt[t], p)
    grid_spec = pltpu.PrefetchScalarGridSpec(
        num_scalar_prefetch=2, grid=(PAIRS, nsteps),
        in_specs=[pl.BlockSpec((tq, 128), qmap),
                  pl.BlockSpec((tk, 128), lambda p, t, it, jt: (jt[t], PAIRS + p)),
                  pl.BlockSpec((tk, 128), lambda p, t, it, jt: (jt[t], 2 * PAIRS + p)),
                  pl.BlockSpec((tq, 128), qmap), pl.BlockSpec((tk, 128), kmap)],
        out_specs=[pl.BlockSpec((tq, 128), qmap),
                   pl.BlockSpec((None, tq, 128), lambda p, t, it, jt: (p, it[t], 0))],
        scratch_shapes=[pltpu.VMEM((2, tq, 128), f32), pltpu.VMEM((2, tq, 128), f32), pltpu.VMEM((2, tq, tk), f32),
                        pltpu.VMEM((2, tq, tk), bf16), pltpu.VMEM((2, tq, 128), f32)],
    )
    return pl.pallas_call(
        body, name="fox_attn_fwd", grid_spec=grid_spec,
        out_shape=[jax.ShapeDtypeStruct((T, AW), bf16), jax.ShapeDtypeStruct((PAIRS, T, 128), f32)],
        compiler_params=_cp("parallel", "arbitrary"),
    )(it, jt, qkv, qkv, qkv, qaug, kaug)


def _pool_fwd(u, wp, scale):
    tm = 1024

    def body(u_ref, wp_ref, sc_ref, pooled_ref, pool_ref, ext):
        i = pl.program_id(0)

        @pl.when(i == 0)
        def _():
            ext[0:HALO, :] = jnp.zeros((HALO, AW), f32)

        uv = u_ref[...]
        ext[HALO:HALO + tm, :] = uv
        t_idx = i * tm + lax.broadcasted_iota(jnp.int32, (tm, 1), 0)
        for g, w in enumerate(WINDOWS):
            lo, hi = 128 * g, 128 * (g + 1)
            ug = uv[:, lo:hi]
            acc = ug
            for d in range(1, w):
                acc = acc + ext[HALO - d:HALO - d + tm, lo:hi]
            cnt = jnp.minimum(t_idx + 1, w).astype(f32)
            pb = (acc / cnt - ug).astype(bf16)
            pooled_ref[:, lo:hi] = pb
            mixed = jnp.dot(pb, wp_ref[g], preferred_element_type=f32)
            pool_ref[:, lo:hi] = (mixed * sc_ref[:, lo:hi]).astype(bf16)
        ext[0:HALO, :] = uv[tm - HALO:tm, :]

    return pl.pallas_call(
        body, name="pool_fwd", grid=(T // tm,),
        in_specs=[pl.BlockSpec((tm, AW), lambda i: (i, 0)), _full((4, 128, 128)), _full((1, AW))],
        out_specs=[pl.BlockSpec((tm, AW), lambda i: (i, 0)), pl.BlockSpec((tm, AW), lambda i: (i, 0))],
        out_shape=[jax.ShapeDtypeStruct((T, AW), bf16), jax.ShapeDtypeStruct((T, AW), bf16)],
        scratch_shapes=[pltpu.VMEM((tm + HALO, AW), f32)],
        compiler_params=_cp("arbitrary"),
    )(u, wp, scale)


def _outproj(x, attn, pool, wo, g2):
    tm = 1024

    def body(x_ref, a_ref, p_ref, wo_ref, g_ref, x1_ref, h2_ref):
        mixed = jnp.concatenate([a_ref[...], p_ref[...]], axis=1)
        x1 = x_ref[...] + jnp.dot(mixed, wo_ref[...], preferred_element_type=f32)
        x1_ref[...] = x1
        r = lax.rsqrt(jnp.mean(x1 * x1, axis=-1, keepdims=True) + EPS)
        h2_ref[...] = (x1 * r * g_ref[...]).astype(bf16)

    return pl.pallas_call(
        body, name="outproj", grid=(T // tm,),
        in_specs=[pl.BlockSpec((tm, D), lambda i: (i, 0)), pl.BlockSpec((tm, AW), lambda i: (i, 0)),
                  pl.BlockSpec((tm, AW), lambda i: (i, 0)), _full((D, D)), _full((1, D))],
        out_specs=[pl.BlockSpec((tm, D), lambda i: (i, 0)), pl.BlockSpec((tm, D), lambda i: (i, 0))],
        out_shape=[jax.ShapeDtypeStruct((T, D), f32), jax.ShapeDtypeStruct((T, D), bf16)],
        compiler_params=_cp("parallel"),
    )(x, attn, pool, wo, g2)


def _mlp_fwd_loss(h2, x1, wg, wu, wd, tgt, gf):
    tm = 512

    def body(h_ref, x1_ref, wg_ref, wu_ref, wd_ref, t_ref, g_ref,
             loss_ref, dg_ref, dx_ref, dxb_ref, ud_ref, silu_ref, a_ref, x2):
        i = pl.program_id(0)
        s = pl.program_id(1)

        @pl.when(jnp.logical_and(i == 0, s == 0))
        def _():
            loss_ref[...] = jnp.zeros_like(loss_ref)
            dg_ref[...] = jnp.zeros_like(dg_ref)

        h = h_ref[...]
        gus = [(lax.dot_general(h, wg_ref[s, c0:c1, :], NT, preferred_element_type=f32),
                lax.dot_general(h, wu_ref[s, c0:c1, :], NT, preferred_element_type=f32)) for c0, c1 in FS_CHUNKS]
        for (c0, c1), (gate, up) in zip(FS_CHUNKS, gus):
            sg = jax.nn.sigmoid(gate)
            silu = gate * sg
            ud_ref[:, c0:c1] = (up * (sg * (1.0 + gate * (1.0 - sg)))).astype(bf16)
            silu_ref[:, c0:c1] = silu.astype(bf16)
            a_ref[:, c0:c1] = (silu * up).astype(bf16)
        part = jnp.dot(a_ref[...], wd_ref[s], preferred_element_type=f32)

        @pl.when(s == 0)
        def _():
            x2[...] = x1_ref[...] + part

        @pl.when(s > 0)
        def _():
            x2[...] += part

        @pl.when(s == NSH - 1)
        def _():
            xv = x2[...]
            g = g_ref[...]
            r = lax.rsqrt(jnp.mean(xv * xv, axis=-1, keepdims=True) + EPS)
            xhat = xv * r
            e = xhat * g - t_ref[...]
            loss_ref[...] += 0.5 * jnp.sum(jnp.mean(e * e, axis=-1, keepdims=True))
            dy = e * (1.0 / D)
            dg_ref[...] += jnp.sum(dy * xhat, axis=0, keepdims=True)
            z = dy * g
            dx = r * (z - xhat * jnp.mean(z * xhat, axis=-1, keepdims=True))
            dx_ref[...] = dx
            dxb_ref[...] = dx.astype(bf16)

    row = lambda i, s: (i, 0)
    sl = lambda i, s: (s, i, 0)
    wsl = lambda i, s: (s, 0, 0)
    return pl.pallas_call(
        body, name="mlp_fwd_loss", grid=(T // tm, NSH),
        in_specs=[pl.BlockSpec((tm, D), row), pl.BlockSpec((tm, D), row),
                  _resident((NSH, FS, D)), _resident((NSH, FS, D)), _resident((NSH, FS, D)),
                  pl.BlockSpec((tm, D), row), pl.BlockSpec((1, D), lambda i, s: (0, 0))],
        out_specs=[pl.BlockSpec((8, 128), lambda i, s: (0, 0)), pl.BlockSpec((1, D), lambda i, s: (0, 0)),
                   pl.BlockSpec((tm, D), row), pl.BlockSpec((tm, D), row),
                   pl.BlockSpec((None, tm, FS), sl), pl.BlockSpec((None, tm, FS), sl), pl.BlockSpec((None, tm, FS), sl)],
        out_shape=[jax.ShapeDtypeStruct((8, 128), f32), jax.ShapeDtypeStruct((1, D), f32),
                   jax.ShapeDtypeStruct((T, D), f32), jax.ShapeDtypeStruct((T, D), bf16)]
        + [jax.ShapeDtypeStruct((NSH, T, FS), bf16)] * 3,
        scratch_shapes=[pltpu.VMEM((tm, D), f32)],
        compiler_params=_cp("arbitrary", "arbitrary"),
    )(h2, x1, wg, wu, wd, tgt, gf)


def _mlp_bwd(dx2b, dx2, ud, silu, wg, wu, wd, x1, g2):
    tm = 512

    def body(dxb_ref, dx_ref, ud_ref, silu_ref, wg_ref, wu_ref, wd_ref, x1_ref, g_ref,
             dg_ref, du_ref, dx1_ref, dx1b_ref, dn_ref, acc):
        i = pl.program_id(0)
        s = pl.program_id(1)

        @pl.when(jnp.logical_and(i == 0, s == 0))
        def _():
            dn_ref[...] = jnp.zeros_like(dn_ref)

        dxb = dxb_ref[...]
        das = [lax.dot_general(dxb, wd_ref[s, c0:c1, :], NT, preferred_element_type=f32) for c0, c1 in FS_CHUNKS]
        for (c0, c1), da in zip(FS_CHUNKS, das):
            dg_ref[:, c0:c1] = (da * ud_ref[:, c0:c1].astype(f32)).astype(bf16)
            du_ref[:, c0:c1] = (da * silu_ref[:, c0:c1].astype(f32)).astype(bf16)
        part = jnp.dot(dg_ref[...], wg_ref[s], preferred_element_type=f32)
        part = part + jnp.dot(du_ref[...], wu_ref[s], preferred_element_type=f32)

        @pl.when(s == 0)
        def _():
            acc[...] = part

        @pl.when(s > 0)
        def _():
            acc[...] += part

        @pl.when(s == NSH - 1)
        def _():
            xv = x1_ref[...]
            r = lax.rsqrt(jnp.mean(xv * xv, axis=-1, keepdims=True) + EPS)
            xhat = xv * r
            dh = acc[...]
            dn_ref[...] += jnp.sum(dh * xhat, axis=0, keepdims=True)
            z = dh * g_ref[...]
            dx1 = dx_ref[...] + r * (z - xhat * jnp.mean(z * xhat, axis=-1, keepdims=True))
            dx1_ref[...] = dx1
            dx1b_ref[...] = dx1.astype(bf16)

    row = lambda i, s: (i, 0)
    sl = lambda i, s: (s, i, 0)
    wsl = lambda i, s: (s, 0, 0)
    return pl.pallas_call(
        body, name="mlp_bwd", grid=(T // tm, NSH),
        in_specs=[pl.BlockSpec((tm, D), row), pl.BlockSpec((tm, D), row),
                  pl.BlockSpec((None, tm, FS), sl), pl.BlockSpec((None, tm, FS), sl),
                  _resident((NSH, FS, D)), _resident((NSH, FS, D)), _resident((NSH, FS, D)),
                  pl.BlockSpec((tm, D), row), pl.BlockSpec((1, D), lambda i, s: (0, 0))],
        out_specs=[pl.BlockSpec((None, tm, FS), sl), pl.BlockSpec((None, tm, FS), sl),
                   pl.BlockSpec((tm, D), row), pl.BlockSpec((tm, D), row), pl.BlockSpec((1, D), lambda i, s: (0, 0))],
        out_shape=[jax.ShapeDtypeStruct((NSH, T, FS), bf16)] * 2
        + [jax.ShapeDtypeStruct((T, D), f32), jax.ShapeDtypeStruct((T, D), bf16), jax.ShapeDtypeStruct((1, D), f32)],
        scratch_shapes=[pltpu.VMEM((tm, D), f32)],
        compiler_params=_cp("arbitrary", "arbitrary"),
    )(dx2b, dx2, ud, silu, wg, wu, wd, x1, g2)


def _mm_tn(a, bs, name, a_sharded=False, b_sharded=False, tk=512, out_dtype=bf16):
    nb = len(bs)
    sh = NSH if (a_sharded or b_sharded) else 1
    m = a.shape[-1]
    nk = T // tk

    def body(a_ref, *refs):
        kk = pl.program_id(1)
        av = a_ref[...]
        for b_ref, o_ref, acc in zip(refs[:nb], refs[nb:2 * nb], refs[2 * nb:]):
            upd = lax.dot_general(av, b_ref[...], TN, preferred_element_type=f32)

            @pl.when(kk == 0)
            def _():
                acc[...] = upd

            @pl.when(kk > 0)
            def _():
                acc[...] += upd

            @pl.when(kk == nk - 1)
            def _():
                o_ref[...] = acc[...].astype(out_dtype)

    a_spec = (pl.BlockSpec((None, tk, m), lambda s, k: (s, k, 0)) if a_sharded
              else pl.BlockSpec((tk, m), lambda s, k: (k, 0)))
    b_specs, o_specs, o_shapes, scratch = [], [], [], []
    for b in bs:
        n = b.shape[-1]
        b_specs.append(pl.BlockSpec((None, tk, n), lambda s, k: (s, k, 0)) if b_sharded
                       else pl.BlockSpec((tk, n), lambda s, k: (k, 0)))
        scratch.append(pltpu.VMEM((m, n), f32))
        if sh > 1:
            o_specs.append(pl.BlockSpec((None, m, n), lambda s, k: (s, 0, 0)))
            o_shapes.append(jax.ShapeDtypeStruct((sh, m, n), out_dtype))
        else:
            o_specs.append(pl.BlockSpec((m, n), lambda s, k: (0, 0)))
            o_shapes.append(jax.ShapeDtypeStruct((m, n), out_dtype))
    return pl.pallas_call(
        body, name=name, grid=(sh, nk), in_specs=[a_spec] + b_specs, out_specs=o_specs, out_shape=o_shapes,
        scratch_shapes=scratch, compiler_params=_cp("arbitrary", "arbitrary"),
    )(a, *bs)


def _mm_tn_rows(a_list, b, name, tk=1024, out_dtype=bf16):
    na = len(a_list)
    n = b.shape[-1]
    nk = T // tk

    def body(*refs):
        a_refs, b_ref = refs[:na], refs[na]
        o_refs, accs = refs[na + 1:2 * na + 1], refs[2 * na + 1:]
        kk = pl.program_id(0)
        bv = b_ref[...]
        for a_ref, o_ref, acc in zip(a_refs, o_refs, accs):
            upd = lax.dot_general(a_ref[...], bv, TN, preferred_element_type=f32)

            @pl.when(kk == 0)
            def _():
                acc[...] = upd

            @pl.when(kk > 0)
            def _():
                acc[...] += upd

            @pl.when(kk == nk - 1)
            def _():
                o_ref[...] = acc[...].astype(out_dtype)

    return pl.pallas_call(
        body, name=name, grid=(nk,),
        in_specs=[pl.BlockSpec((tk, a.shape[-1]), lambda k: (k, 0)) for a in a_list] + [pl.BlockSpec((tk, n), lambda k: (k, 0))],
        out_specs=[pl.BlockSpec((a.shape[-1], n), lambda k: (0, 0)) for a in a_list],
        out_shape=[jax.ShapeDtypeStruct((a.shape[-1], n), out_dtype) for a in a_list],
        scratch_shapes=[pltpu.VMEM((a.shape[-1], n), f32) for a in a_list],
        compiler_params=_cp("arbitrary"),
    )(*a_list, b)


def _outproj_bwd(dx1b, wo):
    tm = 1024

    def body(dx_ref, wo_ref, da_ref, dp_ref):
        dx = dx_ref[...]
        da_ref[...] = lax.dot_general(dx, wo_ref[0:AW, :], NT, preferred_element_type=f32).astype(bf16)
        dp_ref[...] = lax.dot_general(dx, wo_ref[AW:2 * AW, :], NT, preferred_element_type=f32)

    return pl.pallas_call(
        body, name="outproj_bwd", grid=(T // tm,),
        in_specs=[pl.BlockSpec((tm, D), lambda i: (i, 0)), _full((D, D))],
        out_specs=[pl.BlockSpec((tm, AW), lambda i: (i, 0)), pl.BlockSpec((tm, AW), lambda i: (i, 0))],
        out_shape=[jax.ShapeDtypeStruct((T, AW), bf16), jax.ShapeDtypeStruct((T, AW), f32)],
        compiler_params=_cp("parallel"),
    )(dx1b, wo)


def _pool_bwd(dpool, pooled, wp, scale):
    tm = 1024
    n = T // tm

    def body(dp_ref, pb_ref, wp_ref, sc_ref, du_ref, dsc_ref, dwp_ref, ext):
        i = pl.program_id(0)

        @pl.when(i == 0)
        def _():
            ext[tm:tm + HALO, :] = jnp.zeros((HALO, AW), f32)
            dsc_ref[...] = jnp.zeros_like(dsc_ref)
            dwp_ref[...] = jnp.zeros_like(dwp_ref)

        t_idx = (n - 1 - i) * tm + lax.broadcasted_iota(jnp.int32, (tm, 1), 0)
        for g, w in enumerate(WINDOWS):
            lo, hi = 128 * g, 128 * (g + 1)
            pb = pb_ref[:, lo:hi]
            mixed = jnp.dot(pb, wp_ref[g], preferred_element_type=f32)
            dpo = dp_ref[:, lo:hi]
            dsc_ref[:, lo:hi] += jnp.sum(dpo * mixed, axis=0, keepdims=True)
            dmr = (dpo * sc_ref[:, lo:hi]).astype(bf16)
            dwp_ref[g] += lax.dot_general(pb, dmr, TN, preferred_element_type=f32)
            dpl = lax.dot_general(dmr, wp_ref[g], NT, preferred_element_type=f32)
            cnt = jnp.minimum(t_idx + 1, w).astype(f32)
            dpn = dpl / cnt
            ext[0:tm, lo:hi] = dpn
            acc = dpn
            for d in range(1, w):
                acc = acc + ext[d:d + tm, lo:hi]
            du_ref[:, lo:hi] = (acc - dpl).astype(bf16)
        ext[tm:tm + HALO, :] = ext[0:HALO, :]

    rev = lambda i: (n - 1 - i, 0)
    return pl.pallas_call(
        body, name="pool_bwd", grid=(n,),
        in_specs=[pl.BlockSpec((tm, AW), rev), pl.BlockSpec((tm, AW), rev), _full((4, 128, 128)), _full((1, AW))],
        out_specs=[pl.BlockSpec((tm, AW), rev), _full((1, AW)), _full((4, 128, 128))],
        out_shape=[jax.ShapeDtypeStruct((T, AW), bf16), jax.ShapeDtypeStruct((1, AW), f32),
                   jax.ShapeDtypeStruct((4, 128, 128), f32)],
        scratch_shapes=[pltpu.VMEM((tm + HALO, AW), f32)],
        compiler_params=_cp("arbitrary"),
    )(dpool, pooled, wp, scale)


def _attn_bwd(qkv, qaug, kaug, attn, dattn, lse, dep):
    tq = tk = ATT_T
    n = T // tq
    it, jt = _causal_steps(True)
    nsteps = it.shape[0]

    rs = 64

    def body(it_ref, jt_ref, q_ref, k_ref, v_ref, qa_ref, ka_ref, o_ref, do_ref, lse_ref, dep_ref,
             dq_ref, dqs_ref, dk_ref, dks_ref, dv_ref, dq_acc, dk_acc, dv_acc, s_sc, dp_sc, p_sc, ds_sc):
        t = pl.program_id(1)
        i = it_ref[t]
        j = jt_ref[t]

        @pl.when(t == 0)
        def _():
            dq_acc[...] = jnp.zeros_like(dq_acc)

        @pl.when(i == j)
        def _():
            dk_acc[...] = jnp.zeros_like(dk_acc)
            dv_acc[...] = jnp.zeros_like(dv_acc)

        lane = lax.broadcasted_iota(jnp.int32, (tq, 128), 1)

        def step(on_diagonal):
            q = (q_ref[...].astype(f32) * Q_SCALE).astype(bf16)
            k = k_ref[...]
            v = v_ref[...]
            qa = qa_ref[...]
            ka = ka_ref[...]
            do = do_ref[...]
            dd = do.astype(f32) * o_ref[...].astype(f32)
            blocks = _row_blocks(tq, tk, on_diagonal)
            qes, kes, does, deltas = [], [], [], []
            for e in range(2):
                hm = (lane >= 64) if e else (lane < 64)
                qes.append(jnp.where(hm, q, qa))
                kes.append(jnp.where(hm, k, ka))
                does.append(jnp.where(hm, do, jnp.zeros_like(do)))
                deltas.append(jnp.sum(jnp.where(hm, dd, 0.0), axis=1, keepdims=True))
                for r0, r1, nc in blocks:
                    s_sc[e, r0:r1, 0:nc] = lax.dot_general(qes[e][r0:r1], kes[e][0:nc], NT, preferred_element_type=f32)
                    dp_sc[e, r0:r1, 0:nc] = lax.dot_general(does[e][r0:r1], v[0:nc], NT, preferred_element_type=f32)
            for e in range(2):
                for r0, r1, nc in blocks:
                    for r in range(r0, r1, rs):
                        s = s_sc[e, r:r + rs, 0:nc] - lse_ref[r:r + rs, 64 * e:64 * e + 1]
                        if on_diagonal:
                            row = lax.broadcasted_iota(jnp.int32, (rs, nc), 0) + r
                            col = lax.broadcasted_iota(jnp.int32, (rs, nc), 1)
                            s = jnp.where(col <= row, s, NEG)
                        p = jnp.exp2(s)
                        p_sc[e, r:r + rs, 0:nc] = p.astype(bf16)
                        ds_sc[e, r:r + rs, 0:nc] = (p * (dp_sc[e, r:r + rs, 0:nc] - deltas[e][r:r + rs, :])).astype(bf16)
                for r0, r1, nc in blocks:
                    dv_acc[:, 0:nc] += lax.dot_general(does[e][r0:r1], p_sc[e, r0:r1, 0:nc], TN, preferred_element_type=f32)
                    dsb = ds_sc[e, r0:r1, 0:nc]
                    dk_acc[e, :, 0:nc] += lax.dot_general(qes[e][r0:r1], dsb, TN, preferred_element_type=f32)
                    rq = pl.multiple_of(i * tq + r0, r1 - r0)
                    dq_acc[e, pl.ds(rq, r1 - r0), :] += jnp.dot(dsb, kes[e][0:nc], preferred_element_type=f32)

        @pl.when(i > j)
        def _():
            step(False)

        @pl.when(i == j)
        def _():
            step(True)

        @pl.when(i == n - 1)
        def _():
            dk0 = dk_acc[0].T
            dk1 = dk_acc[1].T
            dk_ref[...] = (jnp.where(lane < 64, dk0, dk1) * (1.0 / LOG2E)).astype(bf16)
            dks_ref[...] = jnp.where(lane < 64, dk1, dk0)
            dv_ref[...] = dv_acc[...].T.astype(bf16)

        @pl.when(t == nsteps - 1)
        def _():
            lane_t = lax.broadcasted_iota(jnp.int32, (T, 128), 1)
            dq_ref[...] = (jnp.where(lane_t < 64, dq_acc[0], dq_acc[1]) * 0.125).astype(bf16)
            dqs_ref[...] = jnp.where(lane_t < 64, dq_acc[1], dq_acc[0])

    qmap = lambda p, t, it, jt: (it[t], p)
    grid_spec = pltpu.PrefetchScalarGridSpec(
        num_scalar_prefetch=2, grid=(PAIRS, nsteps),
        in_specs=[pl.BlockSpec((tq, 128), qmap),
                  pl.BlockSpec((tk, 128), lambda p, t, it, jt: (jt[t], PAIRS + p)),
                  pl.BlockSpec((tk, 128), lambda p, t, it, jt: (jt[t], 2 * PAIRS + p)),
                  pl.BlockSpec((tq, 128), qmap), pl.BlockSpec((tk, 128), lambda p, t, it, jt: (jt[t], p)),
                  pl.BlockSpec((tq, 128), qmap), pl.BlockSpec((tq, 128), qmap),
                  pl.BlockSpec((None, tq, 128), lambda p, t, it, jt: (p, it[t], 0)),
                  pl.BlockSpec((8, 128), lambda p, t, it, jt: (0, 0))],
        out_specs=[pl.BlockSpec((T, 128), lambda p, t, it, jt: (0, p)),
                   pl.BlockSpec((None, T, 128), lambda p, t, it, jt: (p, 0, 0)),
                   pl.BlockSpec((tk, 128), lambda p, t, it, jt: (jt[t], p)),
                   pl.BlockSpec((None, tk, 128), lambda p, t, it, jt: (p, jt[t], 0)),
                   pl.BlockSpec((tk, 128), lambda p, t, it, jt: (jt[t], p))],
        scratch_shapes=[pltpu.VMEM((2, T, 128), f32), pltpu.VMEM((2, 128, tk), f32), pltpu.VMEM((128, tk), f32),
                        pltpu.VMEM((2, tq, tk), f32), pltpu.VMEM((2, tq, tk), f32), pltpu.VMEM((2, tq, tk), bf16),
                        pltpu.VMEM((2, tq, tk), bf16)],
    )
    return pl.pallas_call(
        body, name="fox_attn_bwd", grid_spec=grid_spec,
        out_shape=[jax.ShapeDtypeStruct((T, AW), bf16), jax.ShapeDtypeStruct((PAIRS, T, 128), f32),
                   jax.ShapeDtypeStruct((T, AW), bf16), jax.ShapeDtypeStruct((PAIRS, T, 128), f32),
                   jax.ShapeDtypeStruct((T, AW), bf16)],
        compiler_params=_cp("parallel", "arbitrary"),
    )(it, jt, qkv, qkv, qkv, qaug, kaug, attn, dattn, lse, dep)


def _fox_cumsum_bwd(dqs, dks, fl, bfp):
    tb = CUMSUM_ROWS
    nb = T // tb

    def body(dqs_ref, dks_ref, fl_ref, b_ref, df_ref, db_ref, carry):
        i = pl.program_id(0)

        @pl.when(i == 0)
        def _():
            carry[...] = jnp.zeros_like(carry)
            db_ref[...] = jnp.zeros_like(db_ref)

        r = lax.broadcasted_iota(jnp.int32, (128, 128), 0)
        cc = lax.broadcasted_iota(jnp.int32, (128, 128), 1)
        pick = lambda even_lane, odd_lane, p: jnp.logical_or(
            jnp.logical_and(r == even_lane, cc == 2 * p), jnp.logical_and(r == odd_lane, cc == 2 * p + 1)).astype(bf16)
        dc = jnp.zeros((tb, 128), f32)
        for p in range(PAIRS):
            rows_at = pick(SPARE[0] + ROW_SUM_LANE, SPARE[1] + ROW_SUM_LANE, p)
            cols_at = pick(SPARE[0] + COL_SUM_LANE, SPARE[1] + COL_SUM_LANE, p)
            dc = dc + _dot01(rows_at, dqs_ref[p], False) - _dot01(cols_at, dks_ref[p], False)
        rt = lax.broadcasted_iota(jnp.int32, (tb, tb), 0)
        ct = lax.broadcasted_iota(jnp.int32, (tb, tb), 1)
        utri = (ct >= rt).astype(bf16)
        dl = _dot01(utri, dc, True) + carry[0:1, :]
        carry[...] = jnp.broadcast_to(dl[0:1, :], (8, 128))
        z = fl_ref[...] + b_ref[...]
        df = dl * jax.nn.sigmoid(-z)
        df_ref[...] = df.astype(bf16)
        db_ref[...] += jnp.sum(df, axis=0, keepdims=True)

    rev = lambda i: (nb - 1 - i, 0)
    return pl.pallas_call(
        body, name="fox_cumsum_bwd", grid=(nb,),
        in_specs=[pl.BlockSpec((PAIRS, tb, 128), lambda i: (0, nb - 1 - i, 0)),
                  pl.BlockSpec((PAIRS, tb, 128), lambda i: (0, nb - 1 - i, 0)),
                  pl.BlockSpec((tb, 128), rev), _full((1, 128))],
        out_specs=[pl.BlockSpec((tb, 128), rev), _full((1, 128))],
        out_shape=[jax.ShapeDtypeStruct((T, 128), bf16), jax.ShapeDtypeStruct((1, 128), f32)],
        scratch_shapes=[pltpu.VMEM((8, 128), f32)],
        compiler_params=_cp("arbitrary"),
    )(dqs, dks, fl, bfp)


def _inproj_bwd(dq, dk, dv, du, df, w, x, dx1, g1):
    tm = 512

    def body(dq_ref, dk_ref, dv_ref, du_ref, df_ref, w_ref, x_ref, dx1_ref, g_ref, dx_ref, dn_ref):
        i = pl.program_id(0)

        @pl.when(i == 0)
        def _():
            dn_ref[...] = jnp.zeros_like(dn_ref)

        dproj = jnp.concatenate([dq_ref[...], dk_ref[...], dv_ref[...], du_ref[...], df_ref[...]], axis=1)
        dh = jnp.dot(dproj, w_ref[...], preferred_element_type=f32)
        xv = x_ref[...]
        r = lax.rsqrt(jnp.mean(xv * xv, axis=-1, keepdims=True) + EPS)
        xhat = xv * r
        dn_ref[...] += jnp.sum(dh * xhat, axis=0, keepdims=True)
        z = dh * g_ref[...]
        dx_ref[...] = dx1_ref[...] + r * (z - xhat * jnp.mean(z * xhat, axis=-1, keepdims=True))

    row = lambda i: (i, 0)
    return pl.pallas_call(
        body, name="inproj_bwd", grid=(T // tm,),
        in_specs=[pl.BlockSpec((tm, AW), row)] * 4 + [pl.BlockSpec((tm, 128), row), _full((W_ROWS, D)),
                                                       pl.BlockSpec((tm, D), row), pl.BlockSpec((tm, D), row), _full((1, D))],
        out_specs=[pl.BlockSpec((tm, D), row), _full((1, D))],
        out_shape=[jax.ShapeDtypeStruct((T, D), f32), jax.ShapeDtypeStruct((1, D), f32)],
        compiler_params=_cp("arbitrary"),
    )(dq, dk, dv, du, df, w, x, dx1, g1)


def _adamw_math(w, g, m, v):
    m = B1 * m + (1.0 - B1) * g
    v = B2 * v + (1.0 - B2) * (g * g)
    m_hat = m / (1.0 - B1 ** STEP)
    v_hat = v / (1.0 - B2 ** STEP)
    delta = -LR * (m_hat / (jnp.sqrt(v_hat) + AEPS) + WD * w)
    return delta, m, v


def _adamw_shard(w, m, v, p_mine, p_other, name):
    rows, rest = w.shape[0], tuple(w.shape[1:])
    tr = rows if rows <= IN_S else rows // 2

    def body(w_ref, m_ref, v_ref, a_ref, b_ref, g_ref, d_ref, nm_ref, nv_ref):
        g = (a_ref[...].astype(f32) + b_ref[...].astype(f32)).reshape(w_ref.shape)
        g_ref[...] = g
        d_ref[...], nm_ref[...], nv_ref[...] = _adamw_math(w_ref[...], g, m_ref[...], v_ref[...])

    spec = pl.BlockSpec((tr,) + rest, lambda i: (i,) + (0,) * len(rest))
    pspec = pl.BlockSpec((tr, p_mine.shape[1]), lambda i: (i, 0))
    return pl.pallas_call(
        body, name=name, grid=(rows // tr,), in_specs=[spec] * 3 + [pspec] * 2, out_specs=[spec] * 4,
        out_shape=[jax.ShapeDtypeStruct(w.shape, f32)] * 4, compiler_params=_cp("parallel"),
    )(w, m, v, p_mine, p_other)


SMALL_SLOTS = ((0, 8, 128), (8, 16, 128), (16, 24, 128), (24, 28, 128), (32, 33, 8))
LOSS_ROW = 39


def _adamw_small(ws, ms, vs, parts, parts_wp):
    n = len(ws)

    def body(*refs):
        w_refs, m_refs, v_refs = refs[0:n], refs[n:2 * n], refs[2 * n:3 * n]
        p_ref, pw_ref = refs[3 * n], refs[3 * n + 1]
        outs = refs[3 * n + 2:]
        g_all = p_ref[0]
        g_wp = pw_ref[0]
        for k in range(1, 8):
            g_all = g_all + p_ref[k]
            g_wp = g_wp + pw_ref[k]
        grads = [g_all[r0:r1, 0:lanes] for r0, r1, lanes in SMALL_SLOTS] + [g_wp]
        for idx, g in enumerate(grads):
            d, nm, nv = _adamw_math(w_refs[idx][...], g, m_refs[idx][...], v_refs[idx][...])
            outs[idx][...] = g
            outs[n + idx][...] = d
            outs[2 * n + idx][...] = nm
            outs[3 * n + idx][...] = nv
        outs[4 * n][...] = g_all[LOSS_ROW:LOSS_ROW + 1, :]

    shapes = [jax.ShapeDtypeStruct(w.shape, f32) for w in ws]
    res = pl.pallas_call(
        body, name="adamw_small", out_shape=shapes * 4 + [jax.ShapeDtypeStruct((1, 128), f32)],
    )(*ws, *ms, *vs, parts, parts_wp)
    return res[:4 * n], res[4 * n]


def _sum4(recv, g, mine, name):
    _, rows, cols = recv.shape
    tr = rows if rows <= IN_S else rows // 2

    def body(mine_ref, r_ref, g_ref, o_ref):
        o_ref[...] = ((g_ref[...].astype(f32) + r_ref[0].astype(f32))
                      + (r_ref[1].astype(f32) + r_ref[2].astype(f32))).astype(bf16)

    grid_spec = pltpu.PrefetchScalarGridSpec(
        num_scalar_prefetch=1, grid=(rows // tr,),
        in_specs=[pl.BlockSpec((3, tr, cols), lambda i, m: (0, i, 0)),
                  pl.BlockSpec((None, tr, cols), lambda i, m: (m[0], i, 0))],
        out_specs=pl.BlockSpec((tr, cols), lambda i, m: (i, 0)))
    return pl.pallas_call(
        body, name=name, grid_spec=grid_spec, out_shape=jax.ShapeDtypeStruct((rows, cols), bf16),
        compiler_params=_cp("arbitrary"),
    )(mine, recv, g)


_HBM = pl.BlockSpec(memory_space=pltpu.HBM)
_SEM = pl.BlockSpec(memory_space=pltpu.SEMAPHORE)
_EFFECT = pltpu.SideEffectType.DATAFLOW_SIDE_EFFECTING


def _in_hbm(a):
    return pltpu.with_memory_space_constraint(a, pltpu.HBM)


def _mesh_pos():
    return lax.axis_index("x"), lax.axis_index("y"), lax.axis_index("c")


def _other_chips(x, y):
    return [(1 - x, y), (x, 1 - y), (1 - x, 1 - y)]


def _gather_copy(srcs, lands, send_sems, recv_sems, a, k, slot):
    x, y, c = _mesh_pos()
    cx, cy = _other_chips(x, y)[k]
    return pltpu.make_async_remote_copy(
        src_ref=srcs[a], dst_ref=lands[a].at[slot], send_sem=send_sems.at[3 * a + k], recv_sem=recv_sems.at[3 * a + k],
        device_id=(cx, cy, c), device_id_type=MESH)


def _scatter_copy(srcs, lands, send_sems, recv_sems, a, k):
    x, y, c = _mesh_pos()
    cx, cy = _other_chips(x, y)[k]
    return pltpu.make_async_remote_copy(
        src_ref=srcs[a].at[2 * cx + cy], dst_ref=lands[a].at[k], send_sem=send_sems.at[3 * a + k],
        recv_sem=recv_sems.at[3 * a + k], device_id=(cx, cy, c), device_id_type=MESH)


def _all_gather_w_in(part):
    cols = part.shape[1] // 2

    def body(src, dst, send_sems, recv_sems, loc_sem):
        x, y, c = _mesh_pos()
        mine = 2 * x + y
        chips = _other_chips(x, y)
        half = lambda ref, cc: ref.at[:, pl.ds(pl.multiple_of(cc * cols, cols), cols)]

        def over_ici(k, slot):
            cx, cy = chips[k]
            return pltpu.make_async_remote_copy(
                src_ref=half(src, c), dst_ref=half(dst.at[slot], c), send_sem=send_sems.at[k], recv_sem=recv_sems.at[k],
                device_id=(cx, cy, c), device_id_type=MESH)

        def to_sibling(k, cc):
            slot = 2 * chips[k][0] + chips[k][1]
            return pltpu.make_async_remote_copy(
                src_ref=half(dst.at[slot], cc), dst_ref=half(dst.at[slot], cc), send_sem=send_sems.at[3 + k],
                recv_sem=recv_sems.at[3 + k], device_id=(x, y, 1 - c), device_id_type=MESH)

        local = pltpu.make_async_copy(src, dst.at[mine], loc_sem.at[0])
        local.start()
        first = [over_ici(k, mine) for k in range(3)]
        for cp in first:
            cp.start()
        passed = [to_sibling(k, c) for k in range(3)]
        for k in range(3):
            over_ici(k, 2 * chips[k][0] + chips[k][1]).wait_recv()
            passed[k].start()
        for k in range(3):
            to_sibling(k, 1 - c).wait_recv()
        for cp in first + passed:
            cp.wait_send()
        local.wait()

    return pl.pallas_call(
        body, name="all_gather_w_in", in_specs=[_HBM], out_specs=_HBM,
        out_shape=jax.ShapeDtypeStruct((NSH,) + part.shape, part.dtype),
        scratch_shapes=[pltpu.SemaphoreType.DMA((6,)), pltpu.SemaphoreType.DMA((6,)), pltpu.SemaphoreType.DMA((1,))],
    )(part)


def _split_start(name, srcs, lands, n_sems, plan, dep):
    n, nl = len(srcs), len(lands)

    def body(*refs):
        src_refs, land_refs = refs[:n], refs[n:n + nl]
        send_sems, recv_sems = refs[n + nl + 1], refs[n + nl + 2]
        token = refs[-1]
        sends, _ = plan(src_refs, land_refs, send_sems, recv_sems)
        for cp in sends:
            cp.start()
        token[...] = jnp.zeros_like(token)

    outs = pl.pallas_call(
        body, name=name,
        in_specs=[_HBM] * (n + nl) + [pl.BlockSpec(memory_space=pl.ANY)],
        out_specs=[_SEM, _SEM] + [_HBM] * (n + nl) + [pl.BlockSpec(memory_space=pltpu.VMEM)],
        out_shape=[pltpu.SemaphoreType.DMA((n_sems,)), pltpu.SemaphoreType.DMA((n_sems,))]
        + [pltpu.HBM(a.shape, a.dtype) for a in list(srcs) + list(lands)] + [jax.ShapeDtypeStruct((8, 128), f32)],
        input_output_aliases={i: 2 + i for i in range(n + nl)},
        compiler_params=pltpu.CompilerParams(has_side_effects=_EFFECT),
    )(*[_in_hbm(a) for a in list(srcs) + list(lands)], dep)
    return outs[0], outs[1], list(outs[2:2 + n]), list(outs[2 + n:2 + n + nl]), outs[-1]


def _split_wait(name, send_sems, recv_sems, srcs, lands, after, plan):
    n, nl = len(srcs), len(lands)

    def body(*refs):
        src_refs, land_refs = refs[:n], refs[n:n + nl]
        s_sems, r_sems = refs[n + nl], refs[n + nl + 1]
        sends, recvs = plan(src_refs, land_refs, s_sems, r_sems)
        for cp in recvs:
            cp.wait_recv()
        for cp in sends:
            cp.wait_send()

    outs = pl.pallas_call(
        body, name=name,
        in_specs=[_HBM] * (n + nl) + [_SEM, _SEM, pl.BlockSpec(memory_space=pl.ANY)],
        out_specs=[_HBM] * (n + nl),
        out_shape=[pltpu.HBM(a.shape, a.dtype) for a in list(srcs) + list(lands)],
        input_output_aliases={i: i for i in range(n + nl)},
        compiler_params=pltpu.CompilerParams(has_side_effects=_EFFECT),
    )(*srcs, *lands, send_sems, recv_sems, after)
    return list(outs[:n]), list(outs[n:])


def _gather_plan(srcs, lands, ss, rs):
    x, y, _ = _mesh_pos()
    chips = _other_chips(x, y)
    sends = [_gather_copy(srcs, lands, ss, rs, a, k, 2 * x + y) for a in range(len(srcs)) for k in range(3)]
    recvs = [_gather_copy(srcs, lands, ss, rs, a, k, 2 * chips[k][0] + chips[k][1])
             for a in range(len(srcs)) for k in range(3)]
    return sends, recvs


def _scatter_plan(srcs, lands, ss, rs):
    cps = [_scatter_copy(srcs, lands, ss, rs, a, k) for a in range(len(srcs)) for k in range(3)]
    return cps, cps


def _scatter_and_spread_plan(srcs, lands, ss, rs):
    x, y, c = _mesh_pos()
    me = 4 * x + 2 * y + c
    n = len(srcs) - 1
    cps = [_scatter_copy(srcs[:n], lands[:n], ss, rs, a, k) for a in range(n) for k in range(3)]
    for f in range(1, 8):
        peer = ((x + (f >> 2)) % 2, (y + ((f >> 1) & 1)) % 2, (c + (f & 1)) % 2)
        cps.append(pltpu.make_async_remote_copy(
            src_ref=srcs[n], dst_ref=lands[n].at[me], send_sem=ss.at[3 * n - 1 + f], recv_sem=rs.at[3 * n - 1 + f],
            device_id=peer, device_id_type=MESH))
    return cps, cps


def _swap_with_sibling(parts, name):
    n = len(parts)

    def body(*refs):
        srcs, dsts = refs[:n], refs[n:2 * n]
        send_sems, recv_sems = refs[2 * n:]
        x, y, c = _mesh_pos()
        cps = [pltpu.make_async_remote_copy(src_ref=srcs[a], dst_ref=dsts[a], send_sem=send_sems.at[a],
                                            recv_sem=recv_sems.at[a], device_id=(x, y, 1 - c), device_id_type=MESH)
               for a in range(n)]
        for cp in cps:
            cp.start()
        for cp in cps:
            cp.wait_recv()
        for cp in cps:
            cp.wait_send()

    return pl.pallas_call(
        body, name=name, in_specs=[_HBM] * n, out_specs=[_HBM] * n,
        out_shape=[jax.ShapeDtypeStruct(p.shape, p.dtype) for p in parts],
        scratch_shapes=[pltpu.SemaphoreType.DMA((n,)), pltpu.SemaphoreType.DMA((n,))],
    )(*parts)


def _forward(x, tgt, w_in_t, mlp_w_fn, g1, bfp, wp, scale, g2, gf, dep):
    h, qkv, u, fl = _rms_inproj(x, g1, w_in_t, dep)
    qaug, kaug = _fox_cumsum(fl, bfp)
    attn, lse = _attn_fwd(qkv, qaug, kaug)
    pooled, pool = _pool_fwd(u, wp, scale)
    wo, wgt, wut, wd = mlp_w_fn(attn)
    x1, h2 = _outproj(x, attn, pool, wo, g2)
    loss, dgf, dx2, dx2b, ud, silu, a_b = _mlp_fwd_loss(h2, x1, wgt, wut, wd, tgt, gf)
    saved = dict(h=h, qkv=qkv, fl=fl, qaug=qaug, kaug=kaug, attn=attn, lse=lse, pooled=pooled, pool=pool, x1=x1, h2=h2,
                 ud=ud, silu=silu, a_b=a_b, wo=wo, wgt=wgt, wut=wut, wd=wd)
    return loss, dgf, dx2, dx2b, saved


def _backward_mlp(sv, dx2, dx2b, g2):
    dgate, dup, dx1, dx1b, dg2 = _mlp_bwd(dx2b, dx2, sv["ud"], sv["silu"], sv["wgt"], sv["wut"], sv["wd"], sv["x1"], g2)
    (dwd,) = _mm_tn(sv["a_b"], [dx2b], "dw_down", a_sharded=True, tk=T)
    (dwgt,) = _mm_tn(dgate, [sv["h2"]], "dw_gate", a_sharded=True, tk=T)
    (dwut,) = _mm_tn(dup, [sv["h2"]], "dw_up", a_sharded=True, tk=T)
    return dx1, dx1b, dg2, (dwgt, dwut, dwd)


def _backward_outproj_pool(sv, dx1b, wp, scale):
    dattn, dpool = _outproj_bwd(dx1b, sv["wo"])
    dwo_a, = _mm_tn(sv["attn"], [dx1b], "dw_out_attn", tk=2048)
    dwo_p, = _mm_tn(sv["pool"], [dx1b], "dw_out_pool", tk=2048)
    dwo = jnp.concatenate([dwo_a, dwo_p], axis=0).reshape(NSH, D // NSH, D)
    du, dscale, dwp = _pool_bwd(dpool, sv["pooled"], wp, scale)
    return dattn, dwo, du, dscale, dwp


def _backward_attn_inproj(sv, x, dx1, dattn, du, w_in_t, g1, bfp, dep):
    dq, dqs, dk, dks, dv = _attn_bwd(sv["qkv"], sv["qaug"], sv["kaug"], sv["attn"], dattn, sv["lse"], dep)
    df, dbf = _fox_cumsum_bwd(dqs, dks, sv["fl"], bfp)
    dx, dg1 = _inproj_bwd(dq, dk, dv, du, df, w_in_t, x, dx1, g1)
    dwq, dwk, dwv, dwu_in, dwf = _mm_tn_rows([dq, dk, dv, du, df], sv["h"], "dw_in")
    dwin = jnp.concatenate([dwq, dwk, dwv, dwf[0:8], dwu_in], axis=0)
    return dx, dg1, dbf, dwin.reshape(NSH, IN_S, D)


def kernel(x, norm1_g, w_in, b_forget, w_pool, pool_scale, w_out, norm2_g, w_gate, w_up, w_down, final_g, loss_target, m_norm1_g, m_w_in, m_b_forget, m_w_pool, m_pool_scale, m_w_out, m_norm2_g, m_w_gate, m_w_up, m_w_down, m_final_g, v_norm1_g, v_w_in, v_b_forget, v_w_pool, v_pool_scale, v_w_out, v_norm2_g, v_w_gate, v_w_up, v_w_down, v_final_g):
    mine = (2 * lax.axis_index("x") + lax.axis_index("y")).astype(jnp.int32)
    mine1 = mine.reshape(1)
    tr = lambda a: jnp.transpose(a[0])

    win4 = _all_gather_w_in(tr(w_in).astype(bf16))
    later = [w_out[0].astype(bf16), tr(w_gate).astype(bf16), tr(w_up).astype(bf16), w_down[0].astype(bf16)]
    lands = [lax.dynamic_update_slice(lax.empty((NSH,) + p.shape, bf16), p[None], (mine, 0, 0)) for p in later]
    ag_send, ag_recv, later_thru, lands_thru, ag_token = _split_start("all_gather_start", later, lands, 12, _gather_plan,
                                                                      win4)
    win = win4.reshape(IN_W, D)
    w_in_t = jnp.concatenate([win[0:3 * AW], win[3 * AW + 8:], win[3 * AW:3 * AW + 8], jnp.zeros((120, D), bf16)], axis=0)
    bfp = jnp.pad(b_forget, ((0, 0), (0, 120)))
    wp = w_pool[0].astype(bf16)
    gf = final_g.reshape(1, D)

    def later_weights(after):
        _, (wo4, wgt, wut, wd) = _split_wait("all_gather_wait", ag_send, ag_recv, later_thru, lands_thru, after, _gather_plan)
        return wo4.reshape(D, D), wgt, wut, wd

    xe, tgt = x[0], loss_target[0]
    loss_v, dgf, dx2, dx2b, sv = _forward(xe, tgt, w_in_t, later_weights, norm1_g, bfp, wp, pool_scale, norm2_g, gf, ag_token)
    dx1, dx1b, dg2, mlp_grads = _backward_mlp(sv, dx2, dx2b, norm2_g)
    dattn, dwo, du, dscale, dwp = _backward_outproj_pool(sv, dx1b, wp, pool_scale)
    me = (4 * lax.axis_index("x") + 2 * lax.axis_index("y") + lax.axis_index("c")).astype(jnp.int32)
    dwp = dwp.reshape(512, 128)
    first = [dwo] + list(mlp_grads) + [dwp]
    first_lands = [lax.empty((3,) + g.shape[1:], bf16) for g in first[:4]]
    first_lands.append(lax.dynamic_update_slice(lax.empty((8, 512, 128), f32), dwp[None], (me, 0, 0)))
    rs_send, rs_recv, first_thru, first_lands_thru, rs_token = _split_start(
        "reduce_scatter_start", first, first_lands, 19, _scatter_and_spread_plan, du)
    dx, dg1, dbf, dwin = _backward_attn_inproj(sv, xe, dx1, dattn, du, w_in_t, norm1_g, bfp, rs_token)

    pad8 = lambda r: jnp.pad(r, ((0, 8 - r.shape[0]), (0, 0)))
    loss_rows = jnp.concatenate([dbf, jnp.zeros((6, 128), f32), loss_v[0:1, :]], axis=0)
    small = jnp.concatenate([dg1.reshape(8, 128), dg2.reshape(8, 128), dgf.reshape(8, 128), pad8(dscale.reshape(4, 128)),
                             loss_rows], axis=0)
    small_land = lax.dynamic_update_slice(lax.empty((8, SMALL_ROWS, 128), f32), small[None], (me, 0, 0))
    tail_send, tail_recv, tail_thru, tail_lands_thru, tail_token = _split_start(
        "tail_start", [dwin, small], [lax.empty((3,) + dwin.shape[1:], bf16), small_land], 10, _scatter_and_spread_plan,
        dx)
    first_thru, first_recv = _split_wait("reduce_scatter_wait", rs_send, rs_recv, first_thru, first_lands_thru, tail_token,
                                         _scatter_and_spread_plan)
    wp_all = first_recv[4]
    tr3 = lambda a: jnp.transpose(a, (2, 0, 1))
    ws = [tr3(w_in), w_out[0], tr(w_gate), tr(w_up), w_down[0]]
    ms = [tr3(m_w_in), m_w_out[0], tr(m_w_gate), tr(m_w_up), m_w_down[0]]
    vs = [tr3(v_w_in), v_w_out[0], tr(v_w_gate), tr(v_w_up), v_w_down[0]]
    partial = [_sum4(r, g, mine1, f"sum4_{i + 1}") for i, (r, g) in enumerate(zip(first_recv[:4], first_thru[:4]))]
    other = _swap_with_sibling(partial, "swap_first")
    big = [_adamw_shard(ws[i + 1], ms[i + 1], vs[i + 1], partial[i], other[i], f"adamw_{i + 1}") for i in range(4)]
    (dwin_thru, _), (in_recv_land, small_all) = _split_wait("tail_wait", tail_send, tail_recv, tail_thru, tail_lands_thru,
                                                            big[3][0], _scatter_and_spread_plan)
    partial_in = _sum4(in_recv_land, dwin_thru, mine1, "sum4_0")
    (other_in,) = _swap_with_sibling([partial_in], "swap_in")
    big = [_adamw_shard(ws[0], ms[0], vs[0], partial_in, other_in, "adamw_0")] + big

    small_names = ["norm1_g", "norm2_g", "final_g", "pool_scale", "b_forget", "w_pool"]
    rows = lambda a, b, c, d, e, f: [a.reshape(8, 128), b.reshape(8, 128), c.reshape(8, 128), d.reshape(4, 128),
                                     e.reshape(1, 8), f.reshape(512, 128)]
    sm, loss_row = _adamw_small(rows(norm1_g, norm2_g, final_g, pool_scale, b_forget, w_pool),
                                rows(m_norm1_g, m_norm2_g, m_final_g, m_pool_scale, m_b_forget, m_w_pool),
                                rows(v_norm1_g, v_norm2_g, v_final_g, v_pool_scale, v_b_forget, v_w_pool), small_all, wp_all)
    small_shape = dict(norm1_g=(1, D), norm2_g=(1, D), final_g=(D,), pool_scale=(1, AW), b_forget=(1, 8),
                       w_pool=(1, 4, 128, 128))

    order = ["norm1_g", "w_in", "b_forget", "w_pool", "pool_scale", "w_out", "norm2_g", "w_gate", "w_up", "w_down", "final_g"]
    big_idx = {"w_in": 0, "w_out": 1, "w_gate": 2, "w_up": 3, "w_down": 4}
    outs = [loss_row[0, 0], dx[None]]
    for kind in range(4):
        for name in order:
            if name == "w_in":
                outs.append(jnp.transpose(big[0][kind], (1, 2, 0)))
            elif name in ("w_gate", "w_up"):
                outs.append(jnp.transpose(big[big_idx[name]][kind])[None])
            elif name in big_idx:
                outs.append(big[big_idx[name]][kind][None])
            else:
                outs.append(sm[6 * kind + small_names.index(name)].reshape(small_shape[name]))
    return tuple(outs)
```

```python
import jax
import jax.numpy as jnp
import numpy as np
from jax import lax
from jax.experimental import pallas as pl
from jax.experimental.pallas import tpu as pltpu

f32 = jnp.float32
bf16 = jnp.bfloat16

T = 4096
D = 1024
NSH = 4
IN_W = 2056
IN_S = IN_W // NSH
AW = 512
PAIRS = 4
SPARE = (64, 0)
ROW_SUM_LANE, COL_SUM_LANE = 0, 3
FF = 2816
FS = FF // NSH
WINDOWS = (2, 4, 8, 16)
HALO = 16
EPS = 1e-6
NEG = -1e30
LR, B1, B2, AEPS, WD, STEP = 0.001, 0.9, 0.999, 1e-08, 0.01, 10
SMALL_ROWS = 40

NT = (((1,), (1,)), ((), ()))
TN = (((0,), (0,)), ((), ()))

MESH = pl.DeviceIdType.MESH


def _cp(*sem):
    return pltpu.CompilerParams(dimension_semantics=sem)


def _full(shape):
    n = len(shape)
    return pl.BlockSpec(shape, lambda *_: (0,) * n)


def _resident(shape):
    n = len(shape)
    return pl.BlockSpec(shape, lambda *_: (0,) * n, pipeline_mode=pl.Buffered(1))


W_ROWS = 4 * AW + 128


def _rms_inproj(x, g1, w, dep):
    tm = 512

    def body(x_ref, g_ref, w_ref, dep_ref, h_ref, qkv_ref, u_ref, fl_ref):
        xv = x_ref[...]
        r = lax.rsqrt(jnp.mean(xv * xv, axis=-1, keepdims=True) + EPS)
        h = (xv * r * g_ref[...]).astype(bf16)
        h_ref[...] = h
        qkv_ref[...] = lax.dot_general(h, w_ref[0:3 * AW, :], NT, preferred_element_type=f32).astype(bf16)
        u_ref[...] = lax.dot_general(h, w_ref[3 * AW:4 * AW, :], NT, preferred_element_type=f32)
        fl_ref[...] = lax.dot_general(h, w_ref[4 * AW:W_ROWS, :], NT, preferred_element_type=f32)

    return pl.pallas_call(
        body, name="rms_inproj", grid=(T // tm,),
        in_specs=[pl.BlockSpec((tm, D), lambda i: (i, 0)), _full((1, D)), _full((W_ROWS, D)), _full((8, 128))],
        out_specs=[pl.BlockSpec((tm, D), lambda i: (i, 0)), pl.BlockSpec((tm, 3 * AW), lambda i: (i, 0)),
                   pl.BlockSpec((tm, AW), lambda i: (i, 0)), pl.BlockSpec((tm, 128), lambda i: (i, 0))],
        out_shape=[jax.ShapeDtypeStruct((T, D), bf16), jax.ShapeDtypeStruct((T, 3 * AW), bf16),
                   jax.ShapeDtypeStruct((T, AW), f32), jax.ShapeDtypeStruct((T, 128), f32)],
        compiler_params=_cp("parallel"),
    )(x, g1, w, dep)


CUMSUM_ROWS = 512
FS_CHUNKS = ((0, 256), (256, 512), (512, FS))


def _log_sigmoid(z):
    return jnp.minimum(z, 0.0) - jnp.log(1.0 + jnp.exp(-jnp.abs(z)))


def _split3(x):
    hi = x.astype(bf16)
    r1 = x - hi.astype(f32)
    mid = r1.astype(bf16)
    return hi, mid, (r1 - mid.astype(f32)).astype(bf16)


def _dot01(sel, x, sel_first):
    parts = _split3(x)
    if sel_first:
        return sum(jnp.dot(sel, p, preferred_element_type=f32) for p in parts)
    return sum(jnp.dot(p, sel, preferred_element_type=f32) for p in parts)


def _fox_cumsum(fl, bfp):
    tb = CUMSUM_ROWS
    nb = T // tb

    def body(fl_ref, b_ref, qa_ref, ka_ref, carry):
        i = pl.program_id(0)

        @pl.when(i == 0)
        def _():
            carry[...] = jnp.zeros_like(carry)

        lf = _log_sigmoid(fl_ref[...] + b_ref[...])
        r = lax.broadcasted_iota(jnp.int32, (tb, tb), 0)
        cc = lax.broadcasted_iota(jnp.int32, (tb, tb), 1)
        ltri = (cc <= r).astype(bf16)
        cb = _dot01(ltri, lf, True) + carry[0:1, :]
        carry[...] = jnp.broadcast_to(cb[tb - 1:tb, :], (8, 128))
        hi, mid, lo = _split3(cb * LOG2E)
        head = lax.broadcasted_iota(jnp.int32, (128, AW), 0)
        col = lax.broadcasted_iota(jnp.int32, (128, AW), 1)
        base = 128 * (head >> 1) + jnp.where((head & 1) == 0, SPARE[0], SPARE[1])
        place = lambda off: jnp.logical_and(col == base + off, head < 8).astype(bf16)
        mm = lambda a, off: jnp.dot(a, place(off), preferred_element_type=f32)
        cq = mm(hi, 0) + mm(mid, 1) + mm(lo, 2)
        ck = mm(hi, 3) + mm(mid, 4) + mm(lo, 5)
        within = jnp.bitwise_and(lax.broadcasted_iota(jnp.int32, (tb, AW), 1), 63)
        qa_ref[...] = jnp.where(jnp.logical_and(within >= 3, within <= 5), 1.0, cq).astype(bf16)
        ka_ref[...] = jnp.where(within <= 2, 1.0, -ck).astype(bf16)

    return pl.pallas_call(
        body, name="fox_cumsum", grid=(nb,),
        in_specs=[pl.BlockSpec((tb, 128), lambda i: (i, 0)), _full((1, 128))],
        out_specs=[pl.BlockSpec((tb, AW), lambda i: (i, 0)), pl.BlockSpec((tb, AW), lambda i: (i, 0))],
        out_shape=[jax.ShapeDtypeStruct((T, AW), bf16), jax.ShapeDtypeStruct((T, AW), bf16)],
        scratch_shapes=[pltpu.VMEM((8, 128), f32)],
        compiler_params=_cp("arbitrary"),
    )(fl, bfp)


ATT_T = 512
LOG2E = 1.4426950408889634
Q_SCALE = 0.125 * LOG2E


def _causal_steps(key_major):
    n = T // ATT_T
    if key_major:
        pairs = [(i, j) for j in range(n) for i in range(j, n)]
    else:
        pairs = [(i, j) for i in range(n) for j in range(i + 1)]
    it = np.array([p[0] for p in pairs], np.int32)
    jt = np.array([p[1] for p in pairs], np.int32)
    return jnp.asarray(it), jnp.asarray(jt)


def _row_blocks(tq, tk, on_diagonal):
    return ((0, tq // 2, tk // 2), (tq // 2, tq, tk)) if on_diagonal else ((0, tq, tk),)


def _attn_fwd(qkv, qaug, kaug):
    tq = tk = ATT_T
    it, jt = _causal_steps(False)
    nsteps = it.shape[0]

    rs = 64

    def body(it_ref, jt_ref, q_ref, k_ref, v_ref, qa_ref, ka_ref, o_ref, lse_ref, m_sc, acc_sc, s_sc, p_sc, alpha_sc):
        t = pl.program_id(1)
        i = it_ref[t]
        j = jt_ref[t]

        @pl.when(j == 0)
        def _():
            m_sc[...] = jnp.full_like(m_sc, NEG)
            acc_sc[...] = jnp.zeros_like(acc_sc)

        lane = lax.broadcasted_iota(jnp.int32, (tq, 128), 1)
        spare = SPARE

        def step(on_diagonal):
            q = (q_ref[...].astype(f32) * Q_SCALE).astype(bf16)
            k = k_ref[...]
            v = v_ref[...]
            qa = qa_ref[...]
            ka = ka_ref[...]
            blocks = _row_blocks(tq, tk, on_diagonal)
            for e in range(2):
                hm = (lane >= 64) if e else (lane < 64)
                qe = jnp.where(hm, q, qa)
                ke = jnp.where(hm, k, ka)
                for r0, r1, nc in blocks:
                    s_sc[e, r0:r1, 0:nc] = lax.dot_general(qe[r0:r1], ke[0:nc], NT, preferred_element_type=f32)
            for e in range(2):
                for r0, r1, nc in blocks:
                    for r in range(r0, r1, rs):
                        s = s_sc[e, r:r + rs, 0:nc]
                        if on_diagonal:
                            row = lax.broadcasted_iota(jnp.int32, (rs, nc), 0) + r
                            col = lax.broadcasted_iota(jnp.int32, (rs, nc), 1)
                            s = jnp.where(col <= row, s, NEG)
                        m_prev = m_sc[e, r:r + rs, :]
                        m_new = jnp.maximum(m_prev, jnp.max(s, axis=1, keepdims=True))
                        p_sc[e, r:r + rs, 0:nc] = jnp.exp2(s - jnp.tile(m_new, (1, nc // 128))).astype(bf16)
                        alpha_sc[e, r:r + rs, :] = jnp.exp2(m_prev - m_new)
                        m_sc[e, r:r + rs, :] = m_new
            for e in range(2):
                hm = (lane >= 64) if e else (lane < 64)
                ve = jnp.where(hm, v, (lane == spare[e]).astype(bf16))
                for r0, r1, nc in blocks:
                    acc_sc[e, r0:r1] = (alpha_sc[e, r0:r1] * acc_sc[e, r0:r1]
                                        + jnp.dot(p_sc[e, r0:r1, 0:nc], ve[0:nc], preferred_element_type=f32))

        @pl.when(j < i)
        def _():
            step(False)

        @pl.when(j == i)
        def _():
            step(True)
            l0 = acc_sc[0][:, spare[0]:spare[0] + 1]
            l1 = acc_sc[1][:, spare[1]:spare[1] + 1]
            o_ref[...] = jnp.where(lane < 64, acc_sc[0] / l0, acc_sc[1] / l1).astype(bf16)
            lse_ref[...] = jnp.where(lane < 64, m_sc[0] + jnp.log2(l0), m_sc[1] + jnp.log2(l1))

    qmap = lambda p, t, it, jt: (it[t], p)
    kmap = lambda p, t, it, jt: (jt[t], p)
    grid_spec = pltpu.PrefetchScalarGridSpec(
        num_scalar_prefetch=2, grid=(PAIRS, nsteps),
        in_specs=[pl.BlockSpec((tq, 128), qmap),
                  pl.BlockSpec((tk, 128), lambda p, t, it, jt: (jt[t], PAIRS + p)),
                  pl.BlockSpec((tk, 128), lambda p, t, it, jt: (jt[t], 2 * PAIRS + p)),
                  pl.BlockSpec((tq, 128), qmap), pl.BlockSpec((tk, 128), kmap)],
        out_specs=[pl.BlockSpec((tq, 128), qmap),
                   pl.BlockSpec((None, tq, 128), lambda p, t, it, jt: (p, it[t], 0))],
        scratch_shapes=[pltpu.VMEM((2, tq, 128), f32), pltpu.VMEM((2, tq, 128), f32), pltpu.VMEM((2, tq, tk), f32),
                        pltpu.VMEM((2, tq, tk), bf16), pltpu.VMEM((2, tq, 128), f32)],
    )
    return pl.pallas_call(
        body, name="fox_attn_fwd", grid_spec=grid_spec,
        out_shape=[jax.ShapeDtypeStruct((T, AW), bf16), jax.ShapeDtypeStruct((PAIRS, T, 128), f32)],
        compiler_params=_cp("parallel", "arbitrary"),
    )(it, jt, qkv, qkv, qkv, qaug, kaug)


def _pool_fwd(u, wp, scale):
    tm = 1024

    def body(u_ref, wp_ref, sc_ref, pooled_ref, pool_ref, ext):
        i = pl.program_id(0)

        @pl.when(i == 0)
        def _():
            ext[0:HALO, :] = jnp.zeros((HALO, AW), f32)

        uv = u_ref[...]
        ext[HALO:HALO + tm, :] = uv
        t_idx = i * tm + lax.broadcasted_iota(jnp.int32, (tm, 1), 0)
        for g, w in enumerate(WINDOWS):
            lo, hi = 128 * g, 128 * (g + 1)
            ug = uv[:, lo:hi]
            acc = ug
            for d in range(1, w):
                acc = acc + ext[HALO - d:HALO - d + tm, lo:hi]
            cnt = jnp.minimum(t_idx + 1, w).astype(f32)
            pb = (acc / cnt - ug).astype(bf16)
            pooled_ref[:, lo:hi] = pb
            mixed = jnp.dot(pb, wp_ref[g], preferred_element_type=f32)
            pool_ref[:, lo:hi] = (mixed * sc_ref[:, lo:hi]).astype(bf16)
        ext[0:HALO, :] = uv[tm - HALO:tm, :]

    return pl.pallas_call(
        body, name="pool_fwd", grid=(T // tm,),
        in_specs=[pl.BlockSpec((tm, AW), lambda i: (i, 0)), _full((4, 128, 128)), _full((1, AW))],
        out_specs=[pl.BlockSpec((tm, AW), lambda i: (i, 0)), pl.BlockSpec((tm, AW), lambda i: (i, 0))],
        out_shape=[jax.ShapeDtypeStruct((T, AW), bf16), jax.ShapeDtypeStruct((T, AW), bf16)],
        scratch_shapes=[pltpu.VMEM((tm + HALO, AW), f32)],
        compiler_params=_cp("arbitrary"),
    )(u, wp, scale)


def _outproj(x, attn, pool, wo, g2):
    tm = 1024

    def body(x_ref, a_ref, p_ref, wo_ref, g_ref, x1_ref, h2_ref):
        mixed = jnp.concatenate([a_ref[...], p_ref[...]], axis=1)
        x1 = x_ref[...] + jnp.dot(mixed, wo_ref[...], preferred_element_type=f32)
        x1_ref[...] = x1
        r = lax.rsqrt(jnp.mean(x1 * x1, axis=-1, keepdims=True) + EPS)
        h2_ref[...] = (x1 * r * g_ref[...]).astype(bf16)

    return pl.pallas_call(
        body, name="outproj", grid=(T // tm,),
        in_specs=[pl.BlockSpec((tm, D), lambda i: (i, 0)), pl.BlockSpec((tm, AW), lambda i: (i, 0)),
                  pl.BlockSpec((tm, AW), lambda i: (i, 0)), _full((D, D)), _full((1, D))],
        out_specs=[pl.BlockSpec((tm, D), lambda i: (i, 0)), pl.BlockSpec((tm, D), lambda i: (i, 0))],
        out_shape=[jax.ShapeDtypeStruct((T, D), f32), jax.ShapeDtypeStruct((T, D), bf16)],
        compiler_params=_cp("parallel"),
    )(x, attn, pool, wo, g2)


def _mlp_fwd_loss(h2, x1, wg, wu, wd, tgt, gf):
    tm = 512

    def body(h_ref, x1_ref, wg_ref, wu_ref, wd_ref, t_ref, g_ref,
             loss_ref, dg_ref, dx_ref, dxb_ref, ud_ref, silu_ref, a_ref, x2):
        i = pl.program_id(0)
        s = pl.program_id(1)

        @pl.when(jnp.logical_and(i == 0, s == 0))
        def _():
            loss_ref[...] = jnp.zeros_like(loss_ref)
            dg_ref[...] = jnp.zeros_like(dg_ref)

        h = h_ref[...]
        gus = [(lax.dot_general(h, wg_ref[s, c0:c1, :], NT, preferred_element_type=f32),
                lax.dot_general(h, wu_ref[s, c0:c1, :], NT, preferred_element_type=f32)) for c0, c1 in FS_CHUNKS]
        for (c0, c1), (gate, up) in zip(FS_CHUNKS, gus):
            sg = jax.nn.sigmoid(gate)
            silu = gate * sg
            ud_ref[:, c0:c1] = (up * (sg * (1.0 + gate * (1.0 - sg)))).astype(bf16)
            silu_ref[:, c0:c1] = silu.astype(bf16)
            a_ref[:, c0:c1] = (silu * up).astype(bf16)
        part = jnp.dot(a_ref[...], wd_ref[s], preferred_element_type=f32)

        @pl.when(s == 0)
        def _():
            x2[...] = x1_ref[...] + part

        @pl.when(s > 0)
        def _():
            x2[...] += part

        @pl.when(s == NSH - 1)
        def _():
            xv = x2[...]
            g = g_ref[...]
            r = lax.rsqrt(jnp.mean(xv * xv, axis=-1, keepdims=True) + EPS)
            xhat = xv * r
            e = xhat * g - t_ref[...]
            loss_ref[...] += 0.5 * jnp.sum(jnp.mean(e * e, axis=-1, keepdims=True))
            dy = e * (1.0 / D)
            dg_ref[...] += jnp.sum(dy * xhat, axis=0, keepdims=True)
            z = dy * g
            dx = r * (z - xhat * jnp.mean(z * xhat, axis=-1, keepdims=True))
            dx_ref[...] = dx
            dxb_ref[...] = dx.astype(bf16)

    row = lambda i, s: (i, 0)
    sl = lambda i, s: (s, i, 0)
    wsl = lambda i, s: (s, 0, 0)
    return pl.pallas_call(
        body, name="mlp_fwd_loss", grid=(T // tm, NSH),
        in_specs=[pl.BlockSpec((tm, D), row), pl.BlockSpec((tm, D), row),
                  _resident((NSH, FS, D)), _resident((NSH, FS, D)), _resident((NSH, FS, D)),
                  pl.BlockSpec((tm, D), row), pl.BlockSpec((1, D), lambda i, s: (0, 0))],
        out_specs=[pl.BlockSpec((8, 128), lambda i, s: (0, 0)), pl.BlockSpec((1, D), lambda i, s: (0, 0)),
                   pl.BlockSpec((tm, D), row), pl.BlockSpec((tm, D), row),
                   pl.BlockSpec((None, tm, FS), sl), pl.BlockSpec((None, tm, FS), sl), pl.BlockSpec((None, tm, FS), sl)],
        out_shape=[jax.ShapeDtypeStruct((8, 128), f32), jax.ShapeDtypeStruct((1, D), f32),
                   jax.ShapeDtypeStruct((T, D), f32), jax.ShapeDtypeStruct((T, D), bf16)]
        + [jax.ShapeDtypeStruct((NSH, T, FS), bf16)] * 3,
        scratch_shapes=[pltpu.VMEM((tm, D), f32)],
        compiler_params=_cp("arbitrary", "arbitrary"),
    )(h2, x1, wg, wu, wd, tgt, gf)


def _mlp_bwd(dx2b, dx2, ud, silu, wg, wu, wd, x1, g2):
    tm = 512

    def body(dxb_ref, dx_ref, ud_ref, silu_ref, wg_ref, wu_ref, wd_ref, x1_ref, g_ref,
             dg_ref, du_ref, dx1_ref, dx1b_ref, dn_ref, acc):
        i = pl.program_id(0)
        s = pl.program_id(1)

        @pl.when(jnp.logical_and(i == 0, s == 0))
        def _():
            dn_ref[...] = jnp.zeros_like(dn_ref)

        dxb = dxb_ref[...]
        das = [lax.dot_general(dxb, wd_ref[s, c0:c1, :], NT, preferred_element_type=f32) for c0, c1 in FS_CHUNKS]
        for (c0, c1), da in zip(FS_CHUNKS, das):
            dg_ref[:, c0:c1] = (da * ud_ref[:, c0:c1].astype(f32)).astype(bf16)
            du_ref[:, c0:c1] = (da * silu_ref[:, c0:c1].astype(f32)).astype(bf16)
        part = jnp.dot(dg_ref[...], wg_ref[s], preferred_element_type=f32)
        part = part + jnp.dot(du_ref[...], wu_ref[s], preferred_element_type=f32)

        @pl.when(s == 0)
        def _():
            acc[...] = part

        @pl.when(s > 0)
        def _():
            acc[...] += part

        @pl.when(s == NSH - 1)
        def _():
            xv = x1_ref[...]
            r = lax.rsqrt(jnp.mean(xv * xv, axis=-1, keepdims=True) + EPS)
            xhat = xv * r
            dh = acc[...]
            dn_ref[...] += jnp.sum(dh * xhat, axis=0, keepdims=True)
            z = dh * g_ref[...]
            dx1 = dx_ref[...] + r * (z - xhat * jnp.mean(z * xhat, axis=-1, keepdims=True))
            dx1_ref[...] = dx1
            dx1b_ref[...] = dx1.astype(bf16)

    row = lambda i, s: (i, 0)
    sl = lambda i, s: (s, i, 0)
    wsl = lambda i, s: (s, 0, 0)
    return pl.pallas_call(
        body, name="mlp_bwd", grid=(T // tm, NSH),
        in_specs=[pl.BlockSpec((tm, D), row), pl.BlockSpec((tm, D), row),
                  pl.BlockSpec((None, tm, FS), sl), pl.BlockSpec((None, tm, FS), sl),
                  _resident((NSH, FS, D)), _resident((NSH, FS, D)), _resident((NSH, FS, D)),
                  pl.BlockSpec((tm, D), row), pl.BlockSpec((1, D), lambda i, s: (0, 0))],
        out_specs=[pl.BlockSpec((None, tm, FS), sl), pl.BlockSpec((None, tm, FS), sl),
                   pl.BlockSpec((tm, D), row), pl.BlockSpec((tm, D), row), pl.BlockSpec((1, D), lambda i, s: (0, 0))],
        out_shape=[jax.ShapeDtypeStruct((NSH, T, FS), bf16)] * 2
        + [jax.ShapeDtypeStruct((T, D), f32), jax.ShapeDtypeStruct((T, D), bf16), jax.ShapeDtypeStruct((1, D), f32)],
        scratch_shapes=[pltpu.VMEM((tm, D), f32)],
        compiler_params=_cp("arbitrary", "arbitrary"),
    )(dx2b, dx2, ud, silu, wg, wu, wd, x1, g2)


def _mm_tn(a, bs, name, a_sharded=False, b_sharded=False, tk=512, out_dtype=bf16):
    nb = len(bs)
    sh = NSH if (a_sharded or b_sharded) else 1
    m = a.shape[-1]
    nk = T // tk

    def body(a_ref, *refs):
        kk = pl.program_id(1)
        av = a_ref[...]
        for b_ref, o_ref, acc in zip(refs[:nb], refs[nb:2 * nb], refs[2 * nb:]):
            upd = lax.dot_general(av, b_ref[...], TN, preferred_element_type=f32)

            @pl.when(kk == 0)
            def _():
                acc[...] = upd

            @pl.when(kk > 0)
            def _():
                acc[...] += upd

            @pl.when(kk == nk - 1)
            def _():
                o_ref[...] = acc[...].astype(out_dtype)

    a_spec = (pl.BlockSpec((None, tk, m), lambda s, k: (s, k, 0)) if a_sharded
              else pl.BlockSpec((tk, m), lambda s, k: (k, 0)))
    b_specs, o_specs, o_shapes, scratch = [], [], [], []
    for b in bs:
        n = b.shape[-1]
        b_specs.append(pl.BlockSpec((None, tk, n), lambda s, k: (s, k, 0)) if b_sharded
                       else pl.BlockSpec((tk, n), lambda s, k: (k, 0)))
        scratch.append(pltpu.VMEM((m, n), f32))
        if sh > 1:
            o_specs.append(pl.BlockSpec((None, m, n), lambda s, k: (s, 0, 0)))
            o_shapes.append(jax.ShapeDtypeStruct((sh, m, n), out_dtype))
        else:
            o_specs.append(pl.BlockSpec((m, n), lambda s, k: (0, 0)))
            o_shapes.append(jax.ShapeDtypeStruct((m, n), out_dtype))
    return pl.pallas_call(
        body, name=name, grid=(sh, nk), in_specs=[a_spec] + b_specs, out_specs=o_specs, out_shape=o_shapes,
        scratch_shapes=scratch, compiler_params=_cp("arbitrary", "arbitrary"),
    )(a, *bs)


def _mm_tn_rows(a_list, b, name, tk=1024, out_dtype=bf16, stacked=False):
    na = len(a_list)
    n = b.shape[-1]
    nk = T // tk
    ms = [a.shape[-1] for a in a_list]
    starts = [sum(ms[:k]) for k in range(na)] if stacked else [0] * na
    out_rows = [sum(ms)] if stacked else ms

    def body(*refs):
        a_refs, b_ref = refs[:na], refs[na]
        no = len(out_rows)
        o_refs, accs = refs[na + 1:na + 1 + no], refs[na + 1 + no:]
        kk = pl.program_id(0)
        bv = b_ref[...]
        for k, a_ref in enumerate(a_refs):
            o_ref, acc = (o_refs[0], accs[0]) if stacked else (o_refs[k], accs[k])
            rows = pl.ds(starts[k], ms[k])
            upd = lax.dot_general(a_ref[...], bv, TN, preferred_element_type=f32)

            @pl.when(kk == 0)
            def _():
                acc[rows, :] = upd

            @pl.when(kk > 0)
            def _():
                acc[rows, :] += upd

            @pl.when(kk == nk - 1)
            def _():
                o_ref[rows, :] = acc[rows, :].astype(out_dtype)

    return pl.pallas_call(
        body, name=name, grid=(nk,),
        in_specs=[pl.BlockSpec((tk, m), lambda k: (k, 0)) for m in ms] + [pl.BlockSpec((tk, n), lambda k: (k, 0))],
        out_specs=[pl.BlockSpec((r, n), lambda k: (0, 0)) for r in out_rows],
        out_shape=[jax.ShapeDtypeStruct((r, n), out_dtype) for r in out_rows],
        scratch_shapes=[pltpu.VMEM((r, n), f32) for r in out_rows],
        compiler_params=_cp("arbitrary"),
    )(*a_list, b)


def _outproj_bwd(dx1b, wo):
    tm = 1024

    def body(dx_ref, wo_ref, da_ref, dp_ref):
        dx = dx_ref[...]
        da_ref[...] = lax.dot_general(dx, wo_ref[0:AW, :], NT, preferred_element_type=f32).astype(bf16)
        dp_ref[...] = lax.dot_general(dx, wo_ref[AW:2 * AW, :], NT, preferred_element_type=f32)

    return pl.pallas_call(
        body, name="outproj_bwd", grid=(T // tm,),
        in_specs=[pl.BlockSpec((tm, D), lambda i: (i, 0)), _full((D, D))],
        out_specs=[pl.BlockSpec((tm, AW), lambda i: (i, 0)), pl.BlockSpec((tm, AW), lambda i: (i, 0))],
        out_shape=[jax.ShapeDtypeStruct((T, AW), bf16), jax.ShapeDtypeStruct((T, AW), f32)],
        compiler_params=_cp("parallel"),
    )(dx1b, wo)


def _pool_bwd(dpool, pooled, wp, scale):
    tm = 1024
    n = T // tm

    def body(dp_ref, pb_ref, wp_ref, sc_ref, du_ref, dsc_ref, dwp_ref, ext):
        i = pl.program_id(0)

        @pl.when(i == 0)
        def _():
            ext[tm:tm + HALO, :] = jnp.zeros((HALO, AW), f32)
            dsc_ref[...] = jnp.zeros_like(dsc_ref)
            dwp_ref[...] = jnp.zeros_like(dwp_ref)

        t_idx = (n - 1 - i) * tm + lax.broadcasted_iota(jnp.int32, (tm, 1), 0)
        for g, w in enumerate(WINDOWS):
            lo, hi = 128 * g, 128 * (g + 1)
            pb = pb_ref[:, lo:hi]
            mixed = jnp.dot(pb, wp_ref[g], preferred_element_type=f32)
            dpo = dp_ref[:, lo:hi]
            dsc_ref[:, lo:hi] += jnp.sum(dpo * mixed, axis=0, keepdims=True)
            dmr = (dpo * sc_ref[:, lo:hi]).astype(bf16)
            dwp_ref[g] += lax.dot_general(pb, dmr, TN, preferred_element_type=f32)
            dpl = lax.dot_general(dmr, wp_ref[g], NT, preferred_element_type=f32)
            cnt = jnp.minimum(t_idx + 1, w).astype(f32)
            dpn = dpl / cnt
            ext[0:tm, lo:hi] = dpn
            acc = dpn
            for d in range(1, w):
                acc = acc + ext[d:d + tm, lo:hi]
            du_ref[:, lo:hi] = (acc - dpl).astype(bf16)
        ext[tm:tm + HALO, :] = ext[0:HALO, :]

    rev = lambda i: (n - 1 - i, 0)
    return pl.pallas_call(
        body, name="pool_bwd", grid=(n,),
        in_specs=[pl.BlockSpec((tm, AW), rev), pl.BlockSpec((tm, AW), rev), _full((4, 128, 128)), _full((1, AW))],
        out_specs=[pl.BlockSpec((tm, AW), rev), _full((1, AW)), _full((4, 128, 128))],
        out_shape=[jax.ShapeDtypeStruct((T, AW), bf16), jax.ShapeDtypeStruct((1, AW), f32),
                   jax.ShapeDtypeStruct((4, 128, 128), f32)],
        scratch_shapes=[pltpu.VMEM((tm + HALO, AW), f32)],
        compiler_params=_cp("arbitrary"),
    )(dpool, pooled, wp, scale)


def _attn_bwd(qkv, qaug, kaug, attn, dattn, lse, dep):
    tq = tk = ATT_T
    n = T // tq
    it, jt = _causal_steps(True)
    nsteps = it.shape[0]

    rs = 64

    def body(it_ref, jt_ref, q_ref, k_ref, v_ref, qa_ref, ka_ref, o_ref, do_ref, lse_ref, dep_ref,
             dq_ref, dqs_ref, dk_ref, dks_ref, dv_ref, dq_acc, dk_acc, dv_acc, s_sc, dp_sc, p_sc, ds_sc):
        t = pl.program_id(1)
        i = it_ref[t]
        j = jt_ref[t]

        @pl.when(t == 0)
        def _():
            dq_acc[...] = jnp.zeros_like(dq_acc)

        @pl.when(i == j)
        def _():
            dk_acc[...] = jnp.zeros_like(dk_acc)
            dv_acc[...] = jnp.zeros_like(dv_acc)

        lane = lax.broadcasted_iota(jnp.int32, (tq, 128), 1)

        def step(on_diagonal):
            q = (q_ref[...].astype(f32) * Q_SCALE).astype(bf16)
            k = k_ref[...]
            v = v_ref[...]
            qa = qa_ref[...]
            ka = ka_ref[...]
            do = do_ref[...]
            dd = do.astype(f32) * o_ref[...].astype(f32)
            blocks = _row_blocks(tq, tk, on_diagonal)
            qes, kes, does, deltas = [], [], [], []
            for e in range(2):
                hm = (lane >= 64) if e else (lane < 64)
                qes.append(jnp.where(hm, q, qa))
                kes.append(jnp.where(hm, k, ka))
                does.append(jnp.where(hm, do, jnp.zeros_like(do)))
                deltas.append(jnp.sum(jnp.where(hm, dd, 0.0), axis=1, keepdims=True))
                for r0, r1, nc in blocks:
                    s_sc[e, r0:r1, 0:nc] = lax.dot_general(qes[e][r0:r1], kes[e][0:nc], NT, preferred_element_type=f32)
                    dp_sc[e, r0:r1, 0:nc] = lax.dot_general(does[e][r0:r1], v[0:nc], NT, preferred_element_type=f32)
            for e in range(2):
                for r0, r1, nc in blocks:
                    for r in range(r0, r1, rs):
                        s = s_sc[e, r:r + rs, 0:nc] - lse_ref[r:r + rs, 64 * e:64 * e + 1]
                        if on_diagonal:
                            row = lax.broadcasted_iota(jnp.int32, (rs, nc), 0) + r
                            col = lax.broadcasted_iota(jnp.int32, (rs, nc), 1)
                            s = jnp.where(col <= row, s, NEG)
                        p = jnp.exp2(s)
                        p_sc[e, r:r + rs, 0:nc] = p.astype(bf16)
                        ds_sc[e, r:r + rs, 0:nc] = (p * (dp_sc[e, r:r + rs, 0:nc] - deltas[e][r:r + rs, :])).astype(bf16)
                for r0, r1, nc in blocks:
                    dv_acc[:, 0:nc] += lax.dot_general(does[e][r0:r1], p_sc[e, r0:r1, 0:nc], TN, preferred_element_type=f32)
                    dsb = ds_sc[e, r0:r1, 0:nc]
                    dk_acc[e, :, 0:nc] += lax.dot_general(qes[e][r0:r1], dsb, TN, preferred_element_type=f32)
                    rq = pl.multiple_of(i * tq + r0, r1 - r0)
                    dq_acc[e, pl.ds(rq, r1 - r0), :] += jnp.dot(dsb, kes[e][0:nc], preferred_element_type=f32)

        @pl.when(i > j)
        def _():
            step(False)

        @pl.when(i == j)
        def _():
            step(True)

        @pl.when(i == n - 1)
        def _():
            dk0 = dk_acc[0].T
            dk1 = dk_acc[1].T
            dk_ref[...] = (jnp.where(lane < 64, dk0, dk1) * (1.0 / LOG2E)).astype(bf16)
            dks_ref[...] = jnp.where(lane < 64, dk1, dk0)
            dv_ref[...] = dv_acc[...].T.astype(bf16)

        @pl.when(t == nsteps - 1)
        def _():
            lane_t = lax.broadcasted_iota(jnp.int32, (T, 128), 1)
            dq_ref[...] = (jnp.where(lane_t < 64, dq_acc[0], dq_acc[1]) * 0.125).astype(bf16)
            dqs_ref[...] = jnp.where(lane_t < 64, dq_acc[1], dq_acc[0])

    qmap = lambda p, t, it, jt: (it[t], p)
    grid_spec = pltpu.PrefetchScalarGridSpec(
        num_scalar_prefetch=2, grid=(PAIRS, nsteps),
        in_specs=[pl.BlockSpec((tq, 128), qmap),
                  pl.BlockSpec((tk, 128), lambda p, t, it, jt: (jt[t], PAIRS + p)),
                  pl.BlockSpec((tk, 128), lambda p, t, it, jt: (jt[t], 2 * PAIRS + p)),
                  pl.BlockSpec((tq, 128), qmap), pl.BlockSpec((tk, 128), lambda p, t, it, jt: (jt[t], p)),
                  pl.BlockSpec((tq, 128), qmap), pl.BlockSpec((tq, 128), qmap),
                  pl.BlockSpec((None, tq, 128), lambda p, t, it, jt: (p, it[t], 0)),
                  pl.BlockSpec((8, 128), lambda p, t, it, jt: (0, 0))],
        out_specs=[pl.BlockSpec((T, 128), lambda p, t, it, jt: (0, p)),
                   pl.BlockSpec((None, T, 128), lambda p, t, it, jt: (p, 0, 0)),
                   pl.BlockSpec((tk, 128), lambda p, t, it, jt: (jt[t], p)),
                   pl.BlockSpec((None, tk, 128), lambda p, t, it, jt: (p, jt[t], 0)),
                   pl.BlockSpec((tk, 128), lambda p, t, it, jt: (jt[t], p))],
        scratch_shapes=[pltpu.VMEM((2, T, 128), f32), pltpu.VMEM((2, 128, tk), f32), pltpu.VMEM((128, tk), f32),
                        pltpu.VMEM((2, tq, tk), f32), pltpu.VMEM((2, tq, tk), f32), pltpu.VMEM((2, tq, tk), bf16),
                        pltpu.VMEM((2, tq, tk), bf16)],
    )
    return pl.pallas_call(
        body, name="fox_attn_bwd", grid_spec=grid_spec,
        out_shape=[jax.ShapeDtypeStruct((T, AW), bf16), jax.ShapeDtypeStruct((PAIRS, T, 128), f32),
                   jax.ShapeDtypeStruct((T, AW), bf16), jax.ShapeDtypeStruct((PAIRS, T, 128), f32),
                   jax.ShapeDtypeStruct((T, AW), bf16)],
        compiler_params=_cp("parallel", "arbitrary"),
    )(it, jt, qkv, qkv, qkv, qaug, kaug, attn, dattn, lse, dep)


def _fox_cumsum_bwd(dqs, dks, fl, bfp):
    tb = CUMSUM_ROWS
    nb = T // tb

    def body(dqs_ref, dks_ref, fl_ref, b_ref, df_ref, db_ref, carry):
        i = pl.program_id(0)

        @pl.when(i == 0)
        def _():
            carry[...] = jnp.zeros_like(carry)
            db_ref[...] = jnp.zeros_like(db_ref)

        r = lax.broadcasted_iota(jnp.int32, (128, 128), 0)
        cc = lax.broadcasted_iota(jnp.int32, (128, 128), 1)
        pick = lambda even_lane, odd_lane, p: jnp.logical_or(
            jnp.logical_and(r == even_lane, cc == 2 * p), jnp.logical_and(r == odd_lane, cc == 2 * p + 1)).astype(bf16)
        dc = jnp.zeros((tb, 128), f32)
        for p in range(PAIRS):
            rows_at = pick(SPARE[0] + ROW_SUM_LANE, SPARE[1] + ROW_SUM_LANE, p)
            cols_at = pick(SPARE[0] + COL_SUM_LANE, SPARE[1] + COL_SUM_LANE, p)
            dc = dc + _dot01(rows_at, dqs_ref[p], False) - _dot01(cols_at, dks_ref[p], False)
        rt = lax.broadcasted_iota(jnp.int32, (tb, tb), 0)
        ct = lax.broadcasted_iota(jnp.int32, (tb, tb), 1)
        utri = (ct >= rt).astype(bf16)
        dl = _dot01(utri, dc, True) + carry[0:1, :]
        carry[...] = jnp.broadcast_to(dl[0:1, :], (8, 128))
        z = fl_ref[...] + b_ref[...]
        df = dl * jax.nn.sigmoid(-z)
        df_ref[...] = df.astype(bf16)
        db_ref[...] += jnp.sum(df, axis=0, keepdims=True)

    rev = lambda i: (nb - 1 - i, 0)
    return pl.pallas_call(
        body, name="fox_cumsum_bwd", grid=(nb,),
        in_specs=[pl.BlockSpec((PAIRS, tb, 128), lambda i: (0, nb - 1 - i, 0)),
                  pl.BlockSpec((PAIRS, tb, 128), lambda i: (0, nb - 1 - i, 0)),
                  pl.BlockSpec((tb, 128), rev), _full((1, 128))],
        out_specs=[pl.BlockSpec((tb, 128), rev), _full((1, 128))],
        out_shape=[jax.ShapeDtypeStruct((T, 128), bf16), jax.ShapeDtypeStruct((1, 128), f32)],
        scratch_shapes=[pltpu.VMEM((8, 128), f32)],
        compiler_params=_cp("arbitrary"),
    )(dqs, dks, fl, bfp)


def _inproj_bwd(dq, dk, dv, du, df, w, x, dx1, g1):
    tm = 512

    def body(dq_ref, dk_ref, dv_ref, du_ref, df_ref, w_ref, x_ref, dx1_ref, g_ref, dx_ref, dn_ref):
        i = pl.program_id(0)

        @pl.when(i == 0)
        def _():
            dn_ref[...] = jnp.zeros_like(dn_ref)

        dproj = jnp.concatenate([dq_ref[...], dk_ref[...], dv_ref[...], du_ref[...], df_ref[...]], axis=1)
        dh = jnp.dot(dproj, w_ref[...], preferred_element_type=f32)
        xv = x_ref[...]
        r = lax.rsqrt(jnp.mean(xv * xv, axis=-1, keepdims=True) + EPS)
        xhat = xv * r
        dn_ref[...] += jnp.sum(dh * xhat, axis=0, keepdims=True)
        z = dh * g_ref[...]
        dx_ref[...] = dx1_ref[...] + r * (z - xhat * jnp.mean(z * xhat, axis=-1, keepdims=True))

    row = lambda i: (i, 0)
    return pl.pallas_call(
        body, name="inproj_bwd", grid=(T // tm,),
        in_specs=[pl.BlockSpec((tm, AW), row)] * 4 + [pl.BlockSpec((tm, 128), row), _full((W_ROWS, D)),
                                                       pl.BlockSpec((tm, D), row), pl.BlockSpec((tm, D), row), _full((1, D))],
        out_specs=[pl.BlockSpec((tm, D), row), _full((1, D))],
        out_shape=[jax.ShapeDtypeStruct((T, D), f32), jax.ShapeDtypeStruct((1, D), f32)],
        compiler_params=_cp("arbitrary"),
    )(dq, dk, dv, du, df, w, x, dx1, g1)


def _adamw_math(w, g, m, v):
    m = B1 * m + (1.0 - B1) * g
    v = B2 * v + (1.0 - B2) * (g * g)
    m_hat = m / (1.0 - B1 ** STEP)
    v_hat = v / (1.0 - B2 ** STEP)
    delta = -LR * (m_hat / (jnp.sqrt(v_hat) + AEPS) + WD * w)
    return delta, m, v


def _adamw_shard(w, m, v, p_mine, p_other, name):
    rows, rest = w.shape[0], tuple(w.shape[1:])
    tr = rows if rows <= IN_S else rows // 2

    def body(w_ref, m_ref, v_ref, a_ref, b_ref, g_ref, d_ref, nm_ref, nv_ref):
        g = (a_ref[...].astype(f32) + b_ref[...].astype(f32)).reshape(w_ref.shape)
        g_ref[...] = g
        d_ref[...], nm_ref[...], nv_ref[...] = _adamw_math(w_ref[...], g, m_ref[...], v_ref[...])

    spec = pl.BlockSpec((tr,) + rest, lambda i: (i,) + (0,) * len(rest))
    pspec = pl.BlockSpec((tr, p_mine.shape[1]), lambda i: (i, 0))
    return pl.pallas_call(
        body, name=name, grid=(rows // tr,), in_specs=[spec] * 3 + [pspec] * 2, out_specs=[spec] * 4,
        out_shape=[jax.ShapeDtypeStruct(w.shape, f32)] * 4, compiler_params=_cp("parallel"),
    )(w, m, v, p_mine, p_other)


SMALL_SLOTS = ((0, 8, 128), (8, 16, 128), (16, 24, 128), (24, 28, 128), (32, 33, 8))
LOSS_ROW = 39


def _adamw_small(ws, ms, vs, parts, parts_wp):
    n = len(ws)

    def body(*refs):
        w_refs, m_refs, v_refs = refs[0:n], refs[n:2 * n], refs[2 * n:3 * n]
        p_ref, pw_ref = refs[3 * n], refs[3 * n + 1]
        outs = refs[3 * n + 2:]
        g_all = p_ref[0]
        g_wp = pw_ref[0]
        for k in range(1, 8):
            g_all = g_all + p_ref[k]
            g_wp = g_wp + pw_ref[k]
        grads = [g_all[r0:r1, 0:lanes] for r0, r1, lanes in SMALL_SLOTS] + [g_wp]
        for idx, g in enumerate(grads):
            d, nm, nv = _adamw_math(w_refs[idx][...], g, m_refs[idx][...], v_refs[idx][...])
            outs[idx][...] = g
            outs[n + idx][...] = d
            outs[2 * n + idx][...] = nm
            outs[3 * n + idx][...] = nv
        outs[4 * n][...] = g_all[LOSS_ROW:LOSS_ROW + 1, :]

    shapes = [jax.ShapeDtypeStruct(w.shape, f32) for w in ws]
    res = pl.pallas_call(
        body, name="adamw_small", out_shape=shapes * 4 + [jax.ShapeDtypeStruct((1, 128), f32)],
    )(*ws, *ms, *vs, parts, parts_wp)
    return res[:4 * n], res[4 * n]


def _sum4(recv, g, mine, name):
    _, rows, cols = recv.shape
    tr = rows if rows <= IN_S else rows // 2

    def body(mine_ref, r_ref, g_ref, o_ref):
        o_ref[...] = ((g_ref[...].astype(f32) + r_ref[0].astype(f32))
                      + (r_ref[1].astype(f32) + r_ref[2].astype(f32))).astype(bf16)

    grid_spec = pltpu.PrefetchScalarGridSpec(
        num_scalar_prefetch=1, grid=(rows // tr,),
        in_specs=[pl.BlockSpec((3, tr, cols), lambda i, m: (0, i, 0)),
                  pl.BlockSpec((None, tr, cols), lambda i, m: (m[0], i, 0))],
        out_specs=pl.BlockSpec((tr, cols), lambda i, m: (i, 0)))
    return pl.pallas_call(
        body, name=name, grid_spec=grid_spec, out_shape=jax.ShapeDtypeStruct((rows, cols), bf16),
        compiler_params=_cp("arbitrary"),
    )(mine, recv, g)


_HBM = pl.BlockSpec(memory_space=pltpu.HBM)
_SEM = pl.BlockSpec(memory_space=pltpu.SEMAPHORE)
_EFFECT = pltpu.SideEffectType.DATAFLOW_SIDE_EFFECTING


def _in_hbm(a):
    return pltpu.with_memory_space_constraint(a, pltpu.HBM)


def _mesh_pos():
    return lax.axis_index("x"), lax.axis_index("y"), lax.axis_index("c")


def _other_chips(x, y):
    return [(1 - x, y), (x, 1 - y), (1 - x, 1 - y)]


def _gather_copy(srcs, lands, send_sems, recv_sems, a, k, slot):
    x, y, c = _mesh_pos()
    cx, cy = _other_chips(x, y)[k]
    return pltpu.make_async_remote_copy(
        src_ref=srcs[a], dst_ref=lands[a].at[slot], send_sem=send_sems.at[3 * a + k], recv_sem=recv_sems.at[3 * a + k],
        device_id=(cx, cy, c), device_id_type=MESH)


def _scatter_copy(srcs, lands, send_sems, recv_sems, a, k):
    x, y, c = _mesh_pos()
    cx, cy = _other_chips(x, y)[k]
    return pltpu.make_async_remote_copy(
        src_ref=srcs[a].at[2 * cx + cy], dst_ref=lands[a].at[k], send_sem=send_sems.at[3 * a + k],
        recv_sem=recv_sems.at[3 * a + k], device_id=(cx, cy, c), device_id_type=MESH)


def _all_gather_w_in(part):
    cols = part.shape[1] // 2

    def body(src, dst, send_sems, recv_sems, loc_sem):
        x, y, c = _mesh_pos()
        mine = 2 * x + y
        chips = _other_chips(x, y)
        half = lambda ref, cc: ref.at[:, pl.ds(pl.multiple_of(cc * cols, cols), cols)]

        def over_ici(k, slot):
            cx, cy = chips[k]
            return pltpu.make_async_remote_copy(
                src_ref=half(src, c), dst_ref=half(dst.at[slot], c), send_sem=send_sems.at[k], recv_sem=recv_sems.at[k],
                device_id=(cx, cy, c), device_id_type=MESH)

        def to_sibling(k, cc):
            slot = 2 * chips[k][0] + chips[k][1]
            return pltpu.make_async_remote_copy(
                src_ref=half(dst.at[slot], cc), dst_ref=half(dst.at[slot], cc), send_sem=send_sems.at[3 + k],
                recv_sem=recv_sems.at[3 + k], device_id=(x, y, 1 - c), device_id_type=MESH)

        local = pltpu.make_async_copy(src, dst.at[mine], loc_sem.at[0])
        local.start()
        first = [over_ici(k, mine) for k in range(3)]
        for cp in first:
            cp.start()
        passed = [to_sibling(k, c) for k in range(3)]
        for k in range(3):
            over_ici(k, 2 * chips[k][0] + chips[k][1]).wait_recv()
            passed[k].start()
        for k in range(3):
            to_sibling(k, 1 - c).wait_recv()
        for cp in first + passed:
            cp.wait_send()
        local.wait()

    return pl.pallas_call(
        body, name="all_gather_w_in", in_specs=[_HBM], out_specs=_HBM,
        out_shape=jax.ShapeDtypeStruct((NSH,) + part.shape, part.dtype),
        scratch_shapes=[pltpu.SemaphoreType.DMA((6,)), pltpu.SemaphoreType.DMA((6,)), pltpu.SemaphoreType.DMA((1,))],
    )(part)


def _split_start(name, srcs, lands, n_sems, plan, dep):
    n, nl = len(srcs), len(lands)

    def body(*refs):
        src_refs, land_refs = refs[:n], refs[n:n + nl]
        send_sems, recv_sems = refs[n + nl + 1], refs[n + nl + 2]
        token = refs[-1]
        sends, _ = plan(src_refs, land_refs, send_sems, recv_sems)
        for cp in sends:
            cp.start()
        token[...] = jnp.zeros_like(token)

    outs = pl.pallas_call(
        body, name=name,
        in_specs=[_HBM] * (n + nl) + [pl.BlockSpec(memory_space=pl.ANY)],
        out_specs=[_SEM, _SEM] + [_HBM] * (n + nl) + [pl.BlockSpec(memory_space=pltpu.VMEM)],
        out_shape=[pltpu.SemaphoreType.DMA((n_sems,)), pltpu.SemaphoreType.DMA((n_sems,))]
        + [pltpu.HBM(a.shape, a.dtype) for a in list(srcs) + list(lands)] + [jax.ShapeDtypeStruct((8, 128), f32)],
        input_output_aliases={i: 2 + i for i in range(n + nl)},
        compiler_params=pltpu.CompilerParams(has_side_effects=_EFFECT),
    )(*[_in_hbm(a) for a in list(srcs) + list(lands)], dep)
    return outs[0], outs[1], list(outs[2:2 + n]), list(outs[2 + n:2 + n + nl]), outs[-1]


def _split_wait(name, send_sems, recv_sems, srcs, lands, after, plan):
    n, nl = len(srcs), len(lands)

    def body(*refs):
        src_refs, land_refs = refs[:n], refs[n:n + nl]
        s_sems, r_sems = refs[n + nl], refs[n + nl + 1]
        sends, recvs = plan(src_refs, land_refs, s_sems, r_sems)
        for cp in recvs:
            cp.wait_recv()
        for cp in sends:
            cp.wait_send()

    outs = pl.pallas_call(
        body, name=name,
        in_specs=[_HBM] * (n + nl) + [_SEM, _SEM, pl.BlockSpec(memory_space=pl.ANY)],
        out_specs=[_HBM] * (n + nl),
        out_shape=[pltpu.HBM(a.shape, a.dtype) for a in list(srcs) + list(lands)],
        input_output_aliases={i: i for i in range(n + nl)},
        compiler_params=pltpu.CompilerParams(has_side_effects=_EFFECT),
    )(*srcs, *lands, send_sems, recv_sems, after)
    return list(outs[:n]), list(outs[n:])


def _gather_plan(srcs, lands, ss, rs):
    x, y, _ = _mesh_pos()
    chips = _other_chips(x, y)
    sends = [_gather_copy(srcs, lands, ss, rs, a, k, 2 * x + y) for a in range(len(srcs)) for k in range(3)]
    recvs = [_gather_copy(srcs, lands, ss, rs, a, k, 2 * chips[k][0] + chips[k][1])
             for a in range(len(srcs)) for k in range(3)]
    return sends, recvs


def _scatter_plan(srcs, lands, ss, rs):
    cps = [_scatter_copy(srcs, lands, ss, rs, a, k) for a in range(len(srcs)) for k in range(3)]
    return cps, cps


def _scatter_and_spread_plan(srcs, lands, ss, rs):
    x, y, c = _mesh_pos()
    me = 4 * x + 2 * y + c
    n = len(srcs) - 1
    cps = [_scatter_copy(srcs[:n], lands[:n], ss, rs, a, k) for a in range(n) for k in range(3)]
    for f in range(1, 8):
        peer = ((x + (f >> 2)) % 2, (y + ((f >> 1) & 1)) % 2, (c + (f & 1)) % 2)
        cps.append(pltpu.make_async_remote_copy(
            src_ref=srcs[n], dst_ref=lands[n].at[me], send_sem=ss.at[3 * n - 1 + f], recv_sem=rs.at[3 * n - 1 + f],
            device_id=peer, device_id_type=MESH))
    return cps, cps


def _swap_with_sibling(parts, name):
    n = len(parts)

    def body(*refs):
        srcs, dsts = refs[:n], refs[n:2 * n]
        send_sems, recv_sems = refs[2 * n:]
        x, y, c = _mesh_pos()
        cps = [pltpu.make_async_remote_copy(src_ref=srcs[a], dst_ref=dsts[a], send_sem=send_sems.at[a],
                                            recv_sem=recv_sems.at[a], device_id=(x, y, 1 - c), device_id_type=MESH)
               for a in range(n)]
        for cp in cps:
            cp.start()
        for cp in cps:
            cp.wait_recv()
        for cp in cps:
            cp.wait_send()

    return pl.pallas_call(
        body, name=name, in_specs=[_HBM] * n, out_specs=[_HBM] * n,
        out_shape=[jax.ShapeDtypeStruct(p.shape, p.dtype) for p in parts],
        scratch_shapes=[pltpu.SemaphoreType.DMA((n,)), pltpu.SemaphoreType.DMA((n,))],
    )(*parts)


def _forward(x, tgt, w_in_t, mlp_w_fn, g1, bfp, wp, scale, g2, gf, dep):
    h, qkv, u, fl = _rms_inproj(x, g1, w_in_t, dep)
    qaug, kaug = _fox_cumsum(fl, bfp)
    attn, lse = _attn_fwd(qkv, qaug, kaug)
    pooled, pool = _pool_fwd(u, wp, scale)
    wo, wgt, wut, wd = mlp_w_fn(attn)
    x1, h2 = _outproj(x, attn, pool, wo, g2)
    loss, dgf, dx2, dx2b, ud, silu, a_b = _mlp_fwd_loss(h2, x1, wgt, wut, wd, tgt, gf)
    saved = dict(h=h, qkv=qkv, fl=fl, qaug=qaug, kaug=kaug, attn=attn, lse=lse, pooled=pooled, pool=pool, x1=x1, h2=h2,
                 ud=ud, silu=silu, a_b=a_b, wo=wo, wgt=wgt, wut=wut, wd=wd)
    return loss, dgf, dx2, dx2b, saved


def _backward_mlp(sv, dx2, dx2b, g2):
    dgate, dup, dx1, dx1b, dg2 = _mlp_bwd(dx2b, dx2, sv["ud"], sv["silu"], sv["wgt"], sv["wut"], sv["wd"], sv["x1"], g2)
    (dwd,) = _mm_tn(sv["a_b"], [dx2b], "dw_down", a_sharded=True, tk=T)
    (dwgt,) = _mm_tn(dgate, [sv["h2"]], "dw_gate", a_sharded=True, tk=T)
    (dwut,) = _mm_tn(dup, [sv["h2"]], "dw_up", a_sharded=True, tk=T)
    return dx1, dx1b, dg2, (dwgt, dwut, dwd)


def _backward_outproj_pool(sv, dx1b, wp, scale):
    dattn, dpool = _outproj_bwd(dx1b, sv["wo"])
    (dwo,) = _mm_tn_rows([sv["attn"], sv["pool"]], dx1b, "dw_out", tk=2048, stacked=True)
    dwo = dwo.reshape(NSH, D // NSH, D)
    du, dscale, dwp = _pool_bwd(dpool, sv["pooled"], wp, scale)
    return dattn, dwo, du, dscale, dwp


def _backward_attn_inproj(sv, x, dx1, dattn, du, w_in_t, g1, bfp, dep):
    dq, dqs, dk, dks, dv = _attn_bwd(sv["qkv"], sv["qaug"], sv["kaug"], sv["attn"], dattn, sv["lse"], dep)
    df, dbf = _fox_cumsum_bwd(dqs, dks, sv["fl"], bfp)
    dx, dg1 = _inproj_bwd(dq, dk, dv, du, df, w_in_t, x, dx1, g1)
    dwq, dwk, dwv, dwu_in, dwf = _mm_tn_rows([dq, dk, dv, du, df], sv["h"], "dw_in")
    dwin = jnp.concatenate([dwq, dwk, dwv, dwf[0:8], dwu_in], axis=0)
    return dx, dg1, dbf, dwin.reshape(NSH, IN_S, D)


def kernel(x, norm1_g, w_in, b_forget, w_pool, pool_scale, w_out, norm2_g, w_gate, w_up, w_down, final_g, loss_target, m_norm1_g, m_w_in, m_b_forget, m_w_pool, m_pool_scale, m_w_out, m_norm2_g, m_w_gate, m_w_up, m_w_down, m_final_g, v_norm1_g, v_w_in, v_b_forget, v_w_pool, v_pool_scale, v_w_out, v_norm2_g, v_w_gate, v_w_up, v_w_down, v_final_g):
    mine = (2 * lax.axis_index("x") + lax.axis_index("y")).astype(jnp.int32)
    mine1 = mine.reshape(1)
    tr = lambda a: jnp.transpose(a[0])

    win4 = _all_gather_w_in(tr(w_in).astype(bf16))
    later = [w_out[0].astype(bf16), tr(w_gate).astype(bf16), tr(w_up).astype(bf16), w_down[0].astype(bf16)]
    lands = [lax.dynamic_update_slice(lax.empty((NSH,) + p.shape, bf16), p[None], (mine, 0, 0)) for p in later]
    ag_send, ag_recv, later_thru, lands_thru, ag_token = _split_start("all_gather_start", later, lands, 12, _gather_plan,
                                                                      win4)
    win = win4.reshape(IN_W, D)
    w_in_t = jnp.concatenate([win[0:3 * AW], win[3 * AW + 8:], win[3 * AW:3 * AW + 8], jnp.zeros((120, D), bf16)], axis=0)
    bfp = jnp.pad(b_forget, ((0, 0), (0, 120)))
    wp = w_pool[0].astype(bf16)
    gf = final_g.reshape(1, D)

    def later_weights(after):
        _, (wo4, wgt, wut, wd) = _split_wait("all_gather_wait", ag_send, ag_recv, later_thru, lands_thru, after, _gather_plan)
        return wo4.reshape(D, D), wgt, wut, wd

    xe, tgt = x[0], loss_target[0]
    loss_v, dgf, dx2, dx2b, sv = _forward(xe, tgt, w_in_t, later_weights, norm1_g, bfp, wp, pool_scale, norm2_g, gf, ag_token)
    dx1, dx1b, dg2, mlp_grads = _backward_mlp(sv, dx2, dx2b, norm2_g)
    dattn, dwo, du, dscale, dwp = _backward_outproj_pool(sv, dx1b, wp, pool_scale)
    me = (4 * lax.axis_index("x") + 2 * lax.axis_index("y") + lax.axis_index("c")).astype(jnp.int32)
    dwp = dwp.reshape(512, 128)
    first = [dwo] + list(mlp_grads) + [dwp]
    first_lands = [lax.empty((3,) + g.shape[1:], bf16) for g in first[:4]]
    first_lands.append(lax.dynamic_update_slice(lax.empty((8, 512, 128), f32), dwp[None], (me, 0, 0)))
    rs_send, rs_recv, first_thru, first_lands_thru, rs_token = _split_start(
        "reduce_scatter_start", first, first_lands, 19, _scatter_and_spread_plan, du)
    dx, dg1, dbf, dwin = _backward_attn_inproj(sv, xe, dx1, dattn, du, w_in_t, norm1_g, bfp, rs_token)

    pad8 = lambda r: jnp.pad(r, ((0, 8 - r.shape[0]), (0, 0)))
    loss_rows = jnp.concatenate([dbf, jnp.zeros((6, 128), f32), loss_v[0:1, :]], axis=0)
    small = jnp.concatenate([dg1.reshape(8, 128), dg2.reshape(8, 128), dgf.reshape(8, 128), pad8(dscale.reshape(4, 128)),
                             loss_rows], axis=0)
    small_land = lax.dynamic_update_slice(lax.empty((8, SMALL_ROWS, 128), f32), small[None], (me, 0, 0))
    tail_send, tail_recv, tail_thru, tail_lands_thru, tail_token = _split_start(
        "tail_start", [dwin, small], [lax.empty((3,) + dwin.shape[1:], bf16), small_land], 10, _scatter_and_spread_plan,
        dx)
    first_thru, first_recv = _split_wait("reduce_scatter_wait", rs_send, rs_recv, first_thru, first_lands_thru, tail_token,
                                         _scatter_and_spread_plan)
    wp_all = first_recv[4]
    tr3 = lambda a: jnp.transpose(a, (2, 0, 1))
    ws = [tr3(w_in), w_out[0], tr(w_gate), tr(w_up), w_down[0]]
    ms = [tr3(m_w_in), m_w_out[0], tr(m_w_gate), tr(m_w_up), m_w_down[0]]
    vs = [tr3(v_w_in), v_w_out[0], tr(v_w_gate), tr(v_w_up), v_w_down[0]]
    partial = [_sum4(r, g, mine1, f"sum4_{i + 1}") for i, (r, g) in enumerate(zip(first_recv[:4], first_thru[:4]))]
    other = _swap_with_sibling(partial, "swap_first")
    big = [_adamw_shard(ws[i + 1], ms[i + 1], vs[i + 1], partial[i], other[i], f"adamw_{i + 1}") for i in range(4)]
    (dwin_thru, _), (in_recv_land, small_all) = _split_wait("tail_wait", tail_send, tail_recv, tail_thru, tail_lands_thru,
                                                            big[3][0], _scatter_and_spread_plan)
    partial_in = _sum4(in_recv_land, dwin_thru, mine1, "sum4_0")
    (other_in,) = _swap_with_sibling([partial_in], "swap_in")
    big = [_adamw_shard(ws[0], ms[0], vs[0], partial_in, other_in, "adamw_0")] + big

    small_names = ["norm1_g", "norm2_g", "final_g", "pool_scale", "b_forget", "w_pool"]
    rows = lambda a, b, c, d, e, f: [a.reshape(8, 128), b.reshape(8, 128), c.reshape(8, 128), d.reshape(4, 128),
                                     e.reshape(1, 8), f.reshape(512, 128)]
    sm, loss_row = _adamw_small(rows(norm1_g, norm2_g, final_g, pool_scale, b_forget, w_pool),
                                rows(m_norm1_g, m_norm2_g, m_final_g, m_pool_scale, m_b_forget, m_w_pool),
                                rows(v_norm1_g, v_norm2_g, v_final_g, v_pool_scale, v_b_forget, v_w_pool), small_all, wp_all)
    small_shape = dict(norm1_g=(1, D), norm2_g=(1, D), final_g=(D,), pool_scale=(1, AW), b_forget=(1, 8),
                       w_pool=(1, 4, 128, 128))

    order = ["norm1_g", "w_in", "b_forget", "w_pool", "pool_scale", "w_out", "norm2_g", "w_gate", "w_up", "w_down", "final_g"]
    big_idx = {"w_in": 0, "w_out": 1, "w_gate": 2, "w_up": 3, "w_down": 4}
    outs = [loss_row[0, 0], dx[None]]
    for kind in range(4):
        for name in order:
            if name == "w_in":
                outs.append(jnp.transpose(big[0][kind], (1, 2, 0)))
            elif name in ("w_gate", "w_up"):
                outs.append(jnp.transpose(big[big_idx[name]][kind])[None])
            elif name in big_idx:
                outs.append(big[big_idx[name]][kind][None])
            else:
                outs.append(sm[6 * kind + small_names.index(name)].reshape(small_shape[name]))
    return tuple(outs)
```

```python
import jax
import jax.numpy as jnp
import numpy as np
from jax import lax
from jax.experimental import pallas as pl
from jax.experimental.pallas import tpu as pltpu

f32 = jnp.float32
bf16 = jnp.bfloat16

T = 4096
D = 1024
NSH = 4
IN_W = 2056
IN_S = IN_W // NSH
AW = 512
PAIRS = 4
SPARE = (64, 0)
ROW_SUM_LANE, COL_SUM_LANE = 0, 3
FF = 2816
FS = FF // NSH
WINDOWS = (2, 4, 8, 16)
HALO = 16
EPS = 1e-6
NEG = -1e30
LR, B1, B2, AEPS, WD, STEP = 0.001, 0.9, 0.999, 1e-08, 0.01, 10
SMALL_ROWS = 40

NT = (((1,), (1,)), ((), ()))
TN = (((0,), (0,)), ((), ()))

MESH = pl.DeviceIdType.MESH


def _cp(*sem):
    return pltpu.CompilerParams(dimension_semantics=sem)


def _full(shape):
    n = len(shape)
    return pl.BlockSpec(shape, lambda *_: (0,) * n)


def _resident(shape):
    n = len(shape)
    return pl.BlockSpec(shape, lambda *_: (0,) * n, pipeline_mode=pl.Buffered(1))


W_ROWS = 4 * AW + 128


def _rms_inproj(x, g1, w, dep):
    tm = 512

    def body(x_ref, g_ref, w_ref, dep_ref, h_ref, qkv_ref, u_ref, fl_ref):
        xv = x_ref[...]
        r = lax.rsqrt(jnp.mean(xv * xv, axis=-1, keepdims=True) + EPS)
        h = (xv * r * g_ref[...]).astype(bf16)
        h_ref[...] = h
        qkv_ref[...] = lax.dot_general(h, w_ref[0:3 * AW, :], NT, preferred_element_type=f32).astype(bf16)
        u_ref[...] = lax.dot_general(h, w_ref[3 * AW:4 * AW, :], NT, preferred_element_type=f32)
        fl_ref[...] = lax.dot_general(h, w_ref[4 * AW:W_ROWS, :], NT, preferred_element_type=f32)

    return pl.pallas_call(
        body, name="rms_inproj", grid=(T // tm,),
        in_specs=[pl.BlockSpec((tm, D), lambda i: (i, 0)), _full((1, D)), _full((W_ROWS, D)), _full((8, 128))],
        out_specs=[pl.BlockSpec((tm, D), lambda i: (i, 0)), pl.BlockSpec((tm, 3 * AW), lambda i: (i, 0)),
                   pl.BlockSpec((tm, AW), lambda i: (i, 0)), pl.BlockSpec((tm, 128), lambda i: (i, 0))],
        out_shape=[jax.ShapeDtypeStruct((T, D), bf16), jax.ShapeDtypeStruct((T, 3 * AW), bf16),
                   jax.ShapeDtypeStruct((T, AW), f32), jax.ShapeDtypeStruct((T, 128), f32)],
        compiler_params=_cp("parallel"),
    )(x, g1, w, dep)


CUMSUM_ROWS = 512
FS_CHUNKS = ((0, 256), (256, 512), (512, FS))


def _log_sigmoid(z):
    return jnp.minimum(z, 0.0) - jnp.log(1.0 + jnp.exp(-jnp.abs(z)))


def _split3(x):
    hi = x.astype(bf16)
    r1 = x - hi.astype(f32)
    mid = r1.astype(bf16)
    return hi, mid, (r1 - mid.astype(f32)).astype(bf16)


def _dot01(sel, x, sel_first):
    parts = _split3(x)
    if sel_first:
        return sum(jnp.dot(sel, p, preferred_element_type=f32) for p in parts)
    return sum(jnp.dot(p, sel, preferred_element_type=f32) for p in parts)


def _fox_cumsum(fl, bfp):
    tb = CUMSUM_ROWS
    nb = T // tb

    def body(fl_ref, b_ref, qa_ref, ka_ref, carry):
        i = pl.program_id(0)

        @pl.when(i == 0)
        def _():
            carry[...] = jnp.zeros_like(carry)

        lf = _log_sigmoid(fl_ref[...] + b_ref[...])
        r = lax.broadcasted_iota(jnp.int32, (tb, tb), 0)
        cc = lax.broadcasted_iota(jnp.int32, (tb, tb), 1)
        ltri = (cc <= r).astype(bf16)
        cb = _dot01(ltri, lf, True) + carry[0:1, :]
        carry[...] = jnp.broadcast_to(cb[tb - 1:tb, :], (8, 128))
        hi, mid, lo = _split3(cb * LOG2E)
        head = lax.broadcasted_iota(jnp.int32, (128, AW), 0)
        col = lax.broadcasted_iota(jnp.int32, (128, AW), 1)
        base = 128 * (head >> 1) + jnp.where((head & 1) == 0, SPARE[0], SPARE[1])
        place = lambda off: jnp.logical_and(col == base + off, head < 8).astype(bf16)
        mm = lambda a, off: jnp.dot(a, place(off), preferred_element_type=f32)
        cq = mm(hi, 0) + mm(mid, 1) + mm(lo, 2)
        ck = mm(hi, 3) + mm(mid, 4) + mm(lo, 5)
        within = jnp.bitwise_and(lax.broadcasted_iota(jnp.int32, (tb, AW), 1), 63)
        qa_ref[...] = jnp.where(jnp.logical_and(within >= 3, within <= 5), 1.0, cq).astype(bf16)
        ka_ref[...] = jnp.where(within <= 2, 1.0, -ck).astype(bf16)

    return pl.pallas_call(
        body, name="fox_cumsum", grid=(nb,),
        in_specs=[pl.BlockSpec((tb, 128), lambda i: (i, 0)), _full((1, 128))],
        out_specs=[pl.BlockSpec((tb, AW), lambda i: (i, 0)), pl.BlockSpec((tb, AW), lambda i: (i, 0))],
        out_shape=[jax.ShapeDtypeStruct((T, AW), bf16), jax.ShapeDtypeStruct((T, AW), bf16)],
        scratch_shapes=[pltpu.VMEM((8, 128), f32)],
        compiler_params=_cp("arbitrary"),
    )(fl, bfp)


ATT_T = 512
LOG2E = 1.4426950408889634
Q_SCALE = 0.125 * LOG2E


def _causal_steps(key_major):
    n = T // ATT_T
    if key_major:
        pairs = [(i, j) for j in range(n) for i in range(j, n)]
    else:
        pairs = [(i, j) for i in range(n) for j in range(i + 1)]
    it = np.array([p[0] for p in pairs], np.int32)
    jt = np.array([p[1] for p in pairs], np.int32)
    return jnp.asarray(it), jnp.asarray(jt)


def _row_blocks(tq, tk, on_diagonal):
    return ((0, tq // 2, tk // 2), (tq // 2, tq, tk)) if on_diagonal else ((0, tq, tk),)


def _attn_fwd(qkv, qaug, kaug):
    tq = tk = ATT_T
    it, jt = _causal_steps(False)
    nsteps = it.shape[0]

    rs = 64

    def body(it_ref, jt_ref, q_ref, k_ref, v_ref, qa_ref, ka_ref, o_ref, lse_ref, m_sc, acc_sc, s_sc, p_sc, alpha_sc):
        t = pl.program_id(1)
        i = it_ref[t]
        j = jt_ref[t]

        @pl.when(j == 0)
        def _():
            m_sc[...] = jnp.full_like(m_sc, NEG)
            acc_sc[...] = jnp.zeros_like(acc_sc)

        lane = lax.broadcasted_iota(jnp.int32, (tq, 128), 1)
        spare = SPARE

        def step(on_diagonal):
            q = (q_ref[...].astype(f32) * Q_SCALE).astype(bf16)
            k = k_ref[...]
            v = v_ref[...]
            qa = qa_ref[...]
            ka = ka_ref[...]
            blocks = _row_blocks(tq, tk, on_diagonal)
            for e in range(2):
                hm = (lane >= 64) if e else (lane < 64)
                qe = jnp.where(hm, q, qa)
                ke = jnp.where(hm, k, ka)
                for r0, r1, nc in blocks:
                    s_sc[e, r0:r1, 0:nc] = lax.dot_general(qe[r0:r1], ke[0:nc], NT, preferred_element_type=f32)
            for e in range(2):
                for r0, r1, nc in blocks:
                    for r in range(r0, r1, rs):
                        s = s_sc[e, r:r + rs, 0:nc]
                        if on_diagonal:
                            row = lax.broadcasted_iota(jnp.int32, (rs, nc), 0) + r
                            col = lax.broadcasted_iota(jnp.int32, (rs, nc), 1)
                            s = jnp.where(col <= row, s, NEG)
                        m_prev = m_sc[e, r:r + rs, :]
                        m_new = jnp.maximum(m_prev, jnp.max(s, axis=1, keepdims=True))
                        p_sc[e, r:r + rs, 0:nc] = jnp.exp2(s - jnp.tile(m_new, (1, nc // 128))).astype(bf16)
                        alpha_sc[e, r:r + rs, :] = jnp.exp2(m_prev - m_new)
                        m_sc[e, r:r + rs, :] = m_new
            for e in range(2):
                hm = (lane >= 64) if e else (lane < 64)
                ve = jnp.where(hm, v, (lane == spare[e]).astype(bf16))
                for r0, r1, nc in blocks:
                    acc_sc[e, r0:r1] = (alpha_sc[e, r0:r1] * acc_sc[e, r0:r1]
                                        + jnp.dot(p_sc[e, r0:r1, 0:nc], ve[0:nc], preferred_element_type=f32))

        @pl.when(j < i)
        def _():
            step(False)

        @pl.when(j == i)
        def _():
            step(True)
            l0 = acc_sc[0][:, spare[0]:spare[0] + 1]
            l1 = acc_sc[1][:, spare[1]:spare[1] + 1]
            o_ref[...] = jnp.where(lane < 64, acc_sc[0] / l0, acc_sc[1] / l1).astype(bf16)
            lse_ref[...] = jnp.where(lane < 64, m_sc[0] + jnp.log2(l0), m_sc[1] + jnp.log2(l1))

    qmap = lambda p, t, it, jt: (it[t], p)
    kmap = lambda p, t, it, jt: (jt[t], p)
    grid_spec = pltpu.PrefetchScalarGridSpec(
        num_scalar_prefetch=2, grid=(PAIRS, nsteps),
        in_specs=[pl.BlockSpec((tq, 128), qmap),
                  pl.BlockSpec((tk, 128), lambda p, t, it, jt: (jt[t], PAIRS + p)),
                  pl.BlockSpec((tk, 128), lambda p, t, it, jt: (jt[t], 2 * PAIRS + p)),
                  pl.BlockSpec((tq, 128), qmap), pl.BlockSpec((tk, 128), kmap)],
        out_specs=[pl.BlockSpec((tq, 128), qmap),
                   pl.BlockSpec((None, tq, 128), lambda p, t, it, jt: (p, it[t], 0))],
        scratch_shapes=[pltpu.VMEM((2, tq, 128), f32), pltpu.VMEM((2, tq, 128), f32), pltpu.VMEM((2, tq, tk), f32),
                        pltpu.VMEM((2, tq, tk), bf16), pltpu.VMEM((2, tq, 128), f32)],
    )
    return pl.pallas_call(
        body, name="fox_attn_fwd", grid_spec=grid_spec,
        out_shape=[jax.ShapeDtypeStruct((T, AW), bf16), jax.ShapeDtypeStruct((PAIRS, T, 128), f32)],
        compiler_params=_cp("parallel", "arbitrary"),
    )(it, jt, qkv, qkv, qkv, qaug, kaug)


def _pool_fwd(u, wp, scale):
    tm = 1024

    def body(u_ref, wp_ref, sc_ref, pooled_ref, pool_ref, ext):
        i = pl.program_id(0)

        @pl.when(i == 0)
        def _():
            ext[0:HALO, :] = jnp.zeros((HALO, AW), f32)

        uv = u_ref[...]
        ext[HALO:HALO + tm, :] = uv
        t_idx = i * tm + lax.broadcasted_iota(jnp.int32, (tm, 1), 0)
        for g, w in enumerate(WINDOWS):
            lo, hi = 128 * g, 128 * (g + 1)
            ug = uv[:, lo:hi]
            acc = ug
            for d in range(1, w):
                acc = acc + ext[HALO - d:HALO - d + tm, lo:hi]
            cnt = jnp.minimum(t_idx + 1, w).astype(f32)
            pb = (acc / cnt - ug).astype(bf16)
            pooled_ref[:, lo:hi] = pb
            mixed = jnp.dot(pb, wp_ref[g], preferred_element_type=f32)
            pool_ref[:, lo:hi] = (mixed * sc_ref[:, lo:hi]).astype(bf16)
        ext[0:HALO, :] = uv[tm - HALO:tm, :]

    return pl.pallas_call(
        body, name="pool_fwd", grid=(T // tm,),
        in_specs=[pl.BlockSpec((tm, AW), lambda i: (i, 0)), _full((4, 128, 128)), _full((1, AW))],
        out_specs=[pl.BlockSpec((tm, AW), lambda i: (i, 0)), pl.BlockSpec((tm, AW), lambda i: (i, 0))],
        out_shape=[jax.ShapeDtypeStruct((T, AW), bf16), jax.ShapeDtypeStruct((T, AW), bf16)],
        scratch_shapes=[pltpu.VMEM((tm + HALO, AW), f32)],
        compiler_params=_cp("arbitrary"),
    )(u, wp, scale)


def _outproj(x, attn, pool, wo, g2):
    tm = 1024

    def body(x_ref, a_ref, p_ref, wo_ref, g_ref, x1_ref, h2_ref):
        mixed = jnp.concatenate([a_ref[...], p_ref[...]], axis=1)
        x1 = x_ref[...] + jnp.dot(mixed, wo_ref[...], preferred_element_type=f32)
        x1_ref[...] = x1
        r = lax.rsqrt(jnp.mean(x1 * x1, axis=-1, keepdims=True) + EPS)
        h2_ref[...] = (x1 * r * g_ref[...]).astype(bf16)

    return pl.pallas_call(
        body, name="outproj", grid=(T // tm,),
        in_specs=[pl.BlockSpec((tm, D), lambda i: (i, 0)), pl.BlockSpec((tm, AW), lambda i: (i, 0)),
                  pl.BlockSpec((tm, AW), lambda i: (i, 0)), _full((D, D)), _full((1, D))],
        out_specs=[pl.BlockSpec((tm, D), lambda i: (i, 0)), pl.BlockSpec((tm, D), lambda i: (i, 0))],
        out_shape=[jax.ShapeDtypeStruct((T, D), f32), jax.ShapeDtypeStruct((T, D), bf16)],
        compiler_params=_cp("parallel"),
    )(x, attn, pool, wo, g2)


def _mlp_fwd_loss(h2, x1, wg, wu, wd, tgt, gf):
    tm = 512

    def body(h_ref, x1_ref, wg_ref, wu_ref, wd_ref, t_ref, g_ref,
             loss_ref, dg_ref, dx_ref, dxb_ref, ud_ref, silu_ref, a_ref, x2):
        i = pl.program_id(0)
        s = pl.program_id(1)

        @pl.when(jnp.logical_and(i == 0, s == 0))
        def _():
            loss_ref[...] = jnp.zeros_like(loss_ref)
            dg_ref[...] = jnp.zeros_like(dg_ref)

        h = h_ref[...]
        gus = [(lax.dot_general(h, wg_ref[s, c0:c1, :], NT, preferred_element_type=f32),
                lax.dot_general(h, wu_ref[s, c0:c1, :], NT, preferred_element_type=f32)) for c0, c1 in FS_CHUNKS]
        for (c0, c1), (gate, up) in zip(FS_CHUNKS, gus):
            sg = jax.nn.sigmoid(gate)
            silu = gate * sg
            ud_ref[:, c0:c1] = (up * (sg * (1.0 + gate * (1.0 - sg)))).astype(bf16)
            silu_ref[:, c0:c1] = silu.astype(bf16)
            a_ref[:, c0:c1] = (silu * up).astype(bf16)
        part = jnp.dot(a_ref[...], wd_ref[s], preferred_element_type=f32)

        @pl.when(s == 0)
        def _():
            x2[...] = x1_ref[...] + part

        @pl.when(s > 0)
        def _():
            x2[...] += part

        @pl.when(s == NSH - 1)
        def _():
            xv = x2[...]
            g = g_ref[...]
            r = lax.rsqrt(jnp.mean(xv * xv, axis=-1, keepdims=True) + EPS)
            xhat = xv * r
            e = xhat * g - t_ref[...]
            loss_ref[...] += 0.5 * jnp.sum(jnp.mean(e * e, axis=-1, keepdims=True))
            dy = e * (1.0 / D)
            dg_ref[...] += jnp.sum(dy * xhat, axis=0, keepdims=True)
            z = dy * g
            dx = r * (z - xhat * jnp.mean(z * xhat, axis=-1, keepdims=True))
            dx_ref[...] = dx
            dxb_ref[...] = dx.astype(bf16)

    row = lambda i, s: (i, 0)
    sl = lambda i, s: (s, i, 0)
    wsl = lambda i, s: (s, 0, 0)
    return pl.pallas_call(
        body, name="mlp_fwd_loss", grid=(T // tm, NSH),
        in_specs=[pl.BlockSpec((tm, D), row), pl.BlockSpec((tm, D), row),
                  _resident((NSH, FS, D)), _resident((NSH, FS, D)), _resident((NSH, FS, D)),
                  pl.BlockSpec((tm, D), row), pl.BlockSpec((1, D), lambda i, s: (0, 0))],
        out_specs=[pl.BlockSpec((8, 128), lambda i, s: (0, 0)), pl.BlockSpec((1, D), lambda i, s: (0, 0)),
                   pl.BlockSpec((tm, D), row), pl.BlockSpec((tm, D), row),
                   pl.BlockSpec((None, tm, FS), sl), pl.BlockSpec((None, tm, FS), sl), pl.BlockSpec((None, tm, FS), sl)],
        out_shape=[jax.ShapeDtypeStruct((8, 128), f32), jax.ShapeDtypeStruct((1, D), f32),
                   jax.ShapeDtypeStruct((T, D), f32), jax.ShapeDtypeStruct((T, D), bf16)]
        + [jax.ShapeDtypeStruct((NSH, T, FS), bf16)] * 3,
        scratch_shapes=[pltpu.VMEM((tm, D), f32)],
        compiler_params=_cp("arbitrary", "arbitrary"),
    )(h2, x1, wg, wu, wd, tgt, gf)


def _mlp_bwd(dx2b, dx2, ud, silu, wg, wu, wd, x1, g2):
    tm = 512

    def body(dxb_ref, dx_ref, ud_ref, silu_ref, wg_ref, wu_ref, wd_ref, x1_ref, g_ref,
             dg_ref, du_ref, dx1_ref, dx1b_ref, dn_ref, acc):
        i = pl.program_id(0)
        s = pl.program_id(1)

        @pl.when(jnp.logical_and(i == 0, s == 0))
        def _():
            dn_ref[...] = jnp.zeros_like(dn_ref)

        dxb = dxb_ref[...]
        das = [lax.dot_general(dxb, wd_ref[s, c0:c1, :], NT, preferred_element_type=f32) for c0, c1 in FS_CHUNKS]
        for (c0, c1), da in zip(FS_CHUNKS, das):
            dg_ref[:, c0:c1] = (da * ud_ref[:, c0:c1].astype(f32)).astype(bf16)
            du_ref[:, c0:c1] = (da * silu_ref[:, c0:c1].astype(f32)).astype(bf16)
        part = jnp.dot(dg_ref[...], wg_ref[s], preferred_element_type=f32)
        part = part + jnp.dot(du_ref[...], wu_ref[s], preferred_element_type=f32)

        @pl.when(s == 0)
        def _():
            acc[...] = part

        @pl.when(s > 0)
        def _():
            acc[...] += part

        @pl.when(s == NSH - 1)
        def _():
            xv = x1_ref[...]
            r = lax.rsqrt(jnp.mean(xv * xv, axis=-1, keepdims=True) + EPS)
            xhat = xv * r
            dh = acc[...]
            dn_ref[...] += jnp.sum(dh * xhat, axis=0, keepdims=True)
            z = dh * g_ref[...]
            dx1 = dx_ref[...] + r * (z - xhat * jnp.mean(z * xhat, axis=-1, keepdims=True))
            dx1_ref[...] = dx1
            dx1b_ref[...] = dx1.astype(bf16)

    row = lambda i, s: (i, 0)
    sl = lambda i, s: (s, i, 0)
    wsl = lambda i, s: (s, 0, 0)
    return pl.pallas_call(
        body, name="mlp_bwd", grid=(T // tm, NSH),
        in_specs=[pl.BlockSpec((tm, D), row), pl.BlockSpec((tm, D), row),
                  pl.BlockSpec((None, tm, FS), sl), pl.BlockSpec((None, tm, FS), sl),
                  _resident((NSH, FS, D)), _resident((NSH, FS, D)), _resident((NSH, FS, D)),
                  pl.BlockSpec((tm, D), row), pl.BlockSpec((1, D), lambda i, s: (0, 0))],
        out_specs=[pl.BlockSpec((None, tm, FS), sl), pl.BlockSpec((None, tm, FS), sl),
                   pl.BlockSpec((tm, D), row), pl.BlockSpec((tm, D), row), pl.BlockSpec((1, D), lambda i, s: (0, 0))],
        out_shape=[jax.ShapeDtypeStruct((NSH, T, FS), bf16)] * 2
        + [jax.ShapeDtypeStruct((T, D), f32), jax.ShapeDtypeStruct((T, D), bf16), jax.ShapeDtypeStruct((1, D), f32)],
        scratch_shapes=[pltpu.VMEM((tm, D), f32)],
        compiler_params=_cp("arbitrary", "arbitrary"),
    )(dx2b, dx2, ud, silu, wg, wu, wd, x1, g2)


def _mm_tn(a, bs, name, a_sharded=False, b_sharded=False, tk=512, out_dtype=bf16):
    nb = len(bs)
    sh = NSH if (a_sharded or b_sharded) else 1
    m = a.shape[-1]
    nk = T // tk

    def body(a_ref, *refs):
        kk = pl.program_id(1)
        av = a_ref[...]
        for b_ref, o_ref, acc in zip(refs[:nb], refs[nb:2 * nb], refs[2 * nb:]):
            upd = lax.dot_general(av, b_ref[...], TN, preferred_element_type=f32)

            @pl.when(kk == 0)
            def _():
                acc[...] = upd

            @pl.when(kk > 0)
            def _():
                acc[...] += upd

            @pl.when(kk == nk - 1)
            def _():
                o_ref[...] = acc[...].astype(out_dtype)

    a_spec = (pl.BlockSpec((None, tk, m), lambda s, k: (s, k, 0)) if a_sharded
              else pl.BlockSpec((tk, m), lambda s, k: (k, 0)))
    b_specs, o_specs, o_shapes, scratch = [], [], [], []
    for b in bs:
        n = b.shape[-1]
        b_specs.append(pl.BlockSpec((None, tk, n), lambda s, k: (s, k, 0)) if b_sharded
                       else pl.BlockSpec((tk, n), lambda s, k: (k, 0)))
        scratch.append(pltpu.VMEM((m, n), f32))
        if sh > 1:
            o_specs.append(pl.BlockSpec((None, m, n), lambda s, k: (s, 0, 0)))
            o_shapes.append(jax.ShapeDtypeStruct((sh, m, n), out_dtype))
        else:
            o_specs.append(pl.BlockSpec((m, n), lambda s, k: (0, 0)))
            o_shapes.append(jax.ShapeDtypeStruct((m, n), out_dtype))
    return pl.pallas_call(
        body, name=name, grid=(sh, nk), in_specs=[a_spec] + b_specs, out_specs=o_specs, out_shape=o_shapes,
        scratch_shapes=scratch, compiler_params=_cp("arbitrary", "arbitrary"),
    )(a, *bs)


def _mm_tn_rows(a_list, b, name, tk=1024, out_dtype=bf16, stacked=False):
    na = len(a_list)
    n = b.shape[-1]
    nk = T // tk
    ms = [a.shape[-1] for a in a_list]
    starts = [sum(ms[:k]) for k in range(na)] if stacked else [0] * na
    out_rows = [sum(ms)] if stacked else ms

    def body(*refs):
        a_refs, b_ref = refs[:na], refs[na]
        no = len(out_rows)
        o_refs, accs = refs[na + 1:na + 1 + no], refs[na + 1 + no:]
        kk = pl.program_id(0)
        bv = b_ref[...]
        for k, a_ref in enumerate(a_refs):
            o_ref, acc = (o_refs[0], accs[0]) if stacked else (o_refs[k], accs[k])
            rows = pl.ds(starts[k], ms[k])
            upd = lax.dot_general(a_ref[...], bv, TN, preferred_element_type=f32)

            @pl.when(kk == 0)
            def _():
                acc[rows, :] = upd

            @pl.when(kk > 0)
            def _():
                acc[rows, :] += upd

            @pl.when(kk == nk - 1)
            def _():
                o_ref[rows, :] = acc[rows, :].astype(out_dtype)

    return pl.pallas_call(
        body, name=name, grid=(nk,),
        in_specs=[pl.BlockSpec((tk, m), lambda k: (k, 0)) for m in ms] + [pl.BlockSpec((tk, n), lambda k: (k, 0))],
        out_specs=[pl.BlockSpec((r, n), lambda k: (0, 0)) for r in out_rows],
        out_shape=[jax.ShapeDtypeStruct((r, n), out_dtype) for r in out_rows],
        scratch_shapes=[pltpu.VMEM((r, n), f32) for r in out_rows],
        compiler_params=_cp("arbitrary"),
    )(*a_list, b)


def _outproj_bwd(dx1b, wo):
    tm = 1024

    def body(dx_ref, wo_ref, da_ref, dp_ref):
        dx = dx_ref[...]
        da_ref[...] = lax.dot_general(dx, wo_ref[0:AW, :], NT, preferred_element_type=f32).astype(bf16)
        dp_ref[...] = lax.dot_general(dx, wo_ref[AW:2 * AW, :], NT, preferred_element_type=f32)

    return pl.pallas_call(
        body, name="outproj_bwd", grid=(T // tm,),
        in_specs=[pl.BlockSpec((tm, D), lambda i: (i, 0)), _full((D, D))],
        out_specs=[pl.BlockSpec((tm, AW), lambda i: (i, 0)), pl.BlockSpec((tm, AW), lambda i: (i, 0))],
        out_shape=[jax.ShapeDtypeStruct((T, AW), bf16), jax.ShapeDtypeStruct((T, AW), f32)],
        compiler_params=_cp("parallel"),
    )(dx1b, wo)


def _pool_bwd(dpool, pooled, wp, scale):
    tm = 1024
    n = T // tm

    def body(dp_ref, pb_ref, wp_ref, sc_ref, du_ref, dsc_ref, dwp_ref, ext):
        i = pl.program_id(0)

        @pl.when(i == 0)
        def _():
            ext[tm:tm + HALO, :] = jnp.zeros((HALO, AW), f32)
            dsc_ref[...] = jnp.zeros_like(dsc_ref)
            dwp_ref[...] = jnp.zeros_like(dwp_ref)

        t_idx = (n - 1 - i) * tm + lax.broadcasted_iota(jnp.int32, (tm, 1), 0)
        for g, w in enumerate(WINDOWS):
            lo, hi = 128 * g, 128 * (g + 1)
            pb = pb_ref[:, lo:hi]
            mixed = jnp.dot(pb, wp_ref[g], preferred_element_type=f32)
            dpo = dp_ref[:, lo:hi]
            dsc_ref[:, lo:hi] += jnp.sum(dpo * mixed, axis=0, keepdims=True)
            dmr = (dpo * sc_ref[:, lo:hi]).astype(bf16)
            dwp_ref[g] += lax.dot_general(pb, dmr, TN, preferred_element_type=f32)
            dpl = lax.dot_general(dmr, wp_ref[g], NT, preferred_element_type=f32)
            cnt = jnp.minimum(t_idx + 1, w).astype(f32)
            dpn = dpl / cnt
            ext[0:tm, lo:hi] = dpn
            acc = dpn
            for d in range(1, w):
                acc = acc + ext[d:d + tm, lo:hi]
            du_ref[:, lo:hi] = (acc - dpl).astype(bf16)
        ext[tm:tm + HALO, :] = ext[0:HALO, :]

    rev = lambda i: (n - 1 - i, 0)
    return pl.pallas_call(
        body, name="pool_bwd", grid=(n,),
        in_specs=[pl.BlockSpec((tm, AW), rev), pl.BlockSpec((tm, AW), rev), _full((4, 128, 128)), _full((1, AW))],
        out_specs=[pl.BlockSpec((tm, AW), rev), _full((1, AW)), _full((4, 128, 128))],
        out_shape=[jax.ShapeDtypeStruct((T, AW), bf16), jax.ShapeDtypeStruct((1, AW), f32),
                   jax.ShapeDtypeStruct((4, 128, 128), f32)],
        scratch_shapes=[pltpu.VMEM((tm + HALO, AW), f32)],
        compiler_params=_cp("arbitrary"),
    )(dpool, pooled, wp, scale)


def _attn_bwd(qkv, qaug, kaug, attn, dattn, lse, dep):
    tq = tk = ATT_T
    n = T // tq
    it, jt = _causal_steps(True)
    nsteps = it.shape[0]

    rs = 64

    def body(it_ref, jt_ref, q_ref, k_ref, v_ref, qa_ref, ka_ref, o_ref, do_ref, lse_ref, dep_ref,
             dq_ref, dqs_ref, dk_ref, dks_ref, dv_ref, dq_acc, dk_acc, dv_acc, s_sc, dp_sc, p_sc, ds_sc):
        t = pl.program_id(1)
        i = it_ref[t]
        j = jt_ref[t]

        @pl.when(t == 0)
        def _():
            dq_acc[...] = jnp.zeros_like(dq_acc)

        @pl.when(i == j)
        def _():
            dk_acc[...] = jnp.zeros_like(dk_acc)
            dv_acc[...] = jnp.zeros_like(dv_acc)

        lane = lax.broadcasted_iota(jnp.int32, (tq, 128), 1)

        def step(on_diagonal):
            q = (q_ref[...].astype(f32) * Q_SCALE).astype(bf16)
            k = k_ref[...]
            v = v_ref[...]
            qa = qa_ref[...]
            ka = ka_ref[...]
            do = do_ref[...]
            dd = do.astype(f32) * o_ref[...].astype(f32)
            blocks = _row_blocks(tq, tk, on_diagonal)
            qes, kes, does, deltas = [], [], [], []
            for e in range(2):
                hm = (lane >= 64) if e else (lane < 64)
                qes.append(jnp.where(hm, q, qa))
                kes.append(jnp.where(hm, k, ka))
                does.append(jnp.where(hm, do, jnp.zeros_like(do)))
                deltas.append(jnp.sum(jnp.where(hm, dd, 0.0), axis=1, keepdims=True))
                for r0, r1, nc in blocks:
                    s_sc[e, r0:r1, 0:nc] = lax.dot_general(qes[e][r0:r1], kes[e][0:nc], NT, preferred_element_type=f32)
                    dp_sc[e, r0:r1, 0:nc] = lax.dot_general(does[e][r0:r1], v[0:nc], NT, preferred_element_type=f32)
            for e in range(2):
                for r0, r1, nc in blocks:
                    for r in range(r0, r1, rs):
                        s = s_sc[e, r:r + rs, 0:nc] - lse_ref[r:r + rs, 64 * e:64 * e + 1]
                        if on_diagonal:
                            row = lax.broadcasted_iota(jnp.int32, (rs, nc), 0) + r
                            col = lax.broadcasted_iota(jnp.int32, (rs, nc), 1)
                            s = jnp.where(col <= row, s, NEG)
                        p = jnp.exp2(s)
                        p_sc[e, r:r + rs, 0:nc] = p.astype(bf16)
                        ds_sc[e, r:r + rs, 0:nc] = (p * (dp_sc[e, r:r + rs, 0:nc] - deltas[e][r:r + rs, :])).astype(bf16)
                for r0, r1, nc in blocks:
                    dv_acc[:, 0:nc] += lax.dot_general(does[e][r0:r1], p_sc[e, r0:r1, 0:nc], TN, preferred_element_type=f32)
                    dsb = ds_sc[e, r0:r1, 0:nc]
                    dk_acc[e, :, 0:nc] += lax.dot_general(qes[e][r0:r1], dsb, TN, preferred_element_type=f32)
                    rq = pl.multiple_of(i * tq + r0, r1 - r0)
                    dq_acc[e, pl.ds(rq, r1 - r0), :] += jnp.dot(dsb, kes[e][0:nc], preferred_element_type=f32)

        @pl.when(i > j)
        def _():
            step(False)

        @pl.when(i == j)
        def _():
            step(True)

        @pl.when(i == n - 1)
        def _():
            dk0 = dk_acc[0].T
            dk1 = dk_acc[1].T
            dk_ref[...] = (jnp.where(lane < 64, dk0, dk1) * (1.0 / LOG2E)).astype(bf16)
            dks_ref[...] = jnp.where(lane < 64, dk1, dk0)
            dv_ref[...] = dv_acc[...].T.astype(bf16)

        @pl.when(t == nsteps - 1)
        def _():
            lane_t = lax.broadcasted_iota(jnp.int32, (T, 128), 1)
            dq_ref[...] = (jnp.where(lane_t < 64, dq_acc[0], dq_acc[1]) * 0.125).astype(bf16)
            dqs_ref[...] = jnp.where(lane_t < 64, dq_acc[1], dq_acc[0])

    qmap = lambda p, t, it, jt: (it[t], p)
    grid_spec = pltpu.PrefetchScalarGridSpec(
        num_scalar_prefetch=2, grid=(PAIRS, nsteps),
        in_specs=[pl.BlockSpec((tq, 128), qmap),
                  pl.BlockSpec((tk, 128), lambda p, t, it, jt: (jt[t], PAIRS + p)),
                  pl.BlockSpec((tk, 128), lambda p, t, it, jt: (jt[t], 2 * PAIRS + p)),
                  pl.BlockSpec((tq, 128), qmap), pl.BlockSpec((tk, 128), lambda p, t, it, jt: (jt[t], p)),
                  pl.BlockSpec((tq, 128), qmap), pl.BlockSpec((tq, 128), qmap),
                  pl.BlockSpec((None, tq, 128), lambda p, t, it, jt: (p, it[t], 0)),
                  pl.BlockSpec((8, 128), lambda p, t, it, jt: (0, 0))],
        out_specs=[pl.BlockSpec((T, 128), lambda p, t, it, jt: (0, p)),
                   pl.BlockSpec((None, T, 128), lambda p, t, it, jt: (p, 0, 0)),
                   pl.BlockSpec((tk, 128), lambda p, t, it, jt: (jt[t], p)),
                   pl.BlockSpec((None, tk, 128), lambda p, t, it, jt: (p, jt[t], 0)),
                   pl.BlockSpec((tk, 128), lambda p, t, it, jt: (jt[t], p))],
        scratch_shapes=[pltpu.VMEM((2, T, 128), f32), pltpu.VMEM((2, 128, tk), f32), pltpu.VMEM((128, tk), f32),
                        pltpu.VMEM((2, tq, tk), f32), pltpu.VMEM((2, tq, tk), f32), pltpu.VMEM((2, tq, tk), bf16),
                        pltpu.VMEM((2, tq, tk), bf16)],
    )
    return pl.pallas_call(
        body, name="fox_attn_bwd", grid_spec=grid_spec,
        out_shape=[jax.ShapeDtypeStruct((T, AW), bf16), jax.ShapeDtypeStruct((PAIRS, T, 128), f32),
                   jax.ShapeDtypeStruct((T, AW), bf16), jax.ShapeDtypeStruct((PAIRS, T, 128), f32),
                   jax.ShapeDtypeStruct((T, AW), bf16)],
        compiler_params=_cp("parallel", "arbitrary"),
    )(it, jt, qkv, qkv, qkv, qaug, kaug, attn, dattn, lse, dep)


def _fox_cumsum_bwd(dqs, dks, fl, bfp):
    tb = CUMSUM_ROWS
    nb = T // tb

    def body(dqs_ref, dks_ref, fl_ref, b_ref, df_ref, db_ref, carry):
        i = pl.program_id(0)

        @pl.when(i == 0)
        def _():
            carry[...] = jnp.zeros_like(carry)
            db_ref[...] = jnp.zeros_like(db_ref)

        r = lax.broadcasted_iota(jnp.int32, (128, 128), 0)
        cc = lax.broadcasted_iota(jnp.int32, (128, 128), 1)
        pick = lambda even_lane, odd_lane, p: jnp.logical_or(
            jnp.logical_and(r == even_lane, cc == 2 * p), jnp.logical_and(r == odd_lane, cc == 2 * p + 1)).astype(bf16)
        dc = jnp.zeros((tb, 128), f32)
        for p in range(PAIRS):
            rows_at = pick(SPARE[0] + ROW_SUM_LANE, SPARE[1] + ROW_SUM_LANE, p)
            cols_at = pick(SPARE[0] + COL_SUM_LANE, SPARE[1] + COL_SUM_LANE, p)
            dc = dc + _dot01(rows_at, dqs_ref[p], False) - _dot01(cols_at, dks_ref[p], False)
        rt = lax.broadcasted_iota(jnp.int32, (tb, tb), 0)
        ct = lax.broadcasted_iota(jnp.int32, (tb, tb), 1)
        utri = (ct >= rt).astype(bf16)
        dl = _dot01(utri, dc, True) + carry[0:1, :]
        carry[...] = jnp.broadcast_to(dl[0:1, :], (8, 128))
        z = fl_ref[...] + b_ref[...]
        df = dl * jax.nn.sigmoid(-z)
        df_ref[...] = df.astype(bf16)
        db_ref[...] += jnp.sum(df, axis=0, keepdims=True)

    rev = lambda i: (nb - 1 - i, 0)
    return pl.pallas_call(
        body, name="fox_cumsum_bwd", grid=(nb,),
        in_specs=[pl.BlockSpec((PAIRS, tb, 128), lambda i: (0, nb - 1 - i, 0)),
                  pl.BlockSpec((PAIRS, tb, 128), lambda i: (0, nb - 1 - i, 0)),
                  pl.BlockSpec((tb, 128), rev), _full((1, 128))],
        out_specs=[pl.BlockSpec((tb, 128), rev), _full((1, 128))],
        out_shape=[jax.ShapeDtypeStruct((T, 128), bf16), jax.ShapeDtypeStruct((1, 128), f32)],
        scratch_shapes=[pltpu.VMEM((8, 128), f32)],
        compiler_params=_cp("arbitrary"),
    )(dqs, dks, fl, bfp)


def _inproj_bwd(dq, dk, dv, du, df, w, x, dx1, g1):
    tm = 512

    def body(dq_ref, dk_ref, dv_ref, du_ref, df_ref, w_ref, x_ref, dx1_ref, g_ref, dx_ref, dn_ref):
        i = pl.program_id(0)

        @pl.when(i == 0)
        def _():
            dn_ref[...] = jnp.zeros_like(dn_ref)

        dproj = jnp.concatenate([dq_ref[...], dk_ref[...], dv_ref[...], du_ref[...], df_ref[...]], axis=1)
        dh = jnp.dot(dproj, w_ref[...], preferred_element_type=f32)
        xv = x_ref[...]
        r = lax.rsqrt(jnp.mean(xv * xv, axis=-1, keepdims=True) + EPS)
        xhat = xv * r
        dn_ref[...] += jnp.sum(dh * xhat, axis=0, keepdims=True)
        z = dh * g_ref[...]
        dx_ref[...] = dx1_ref[...] + r * (z - xhat * jnp.mean(z * xhat, axis=-1, keepdims=True))

    row = lambda i: (i, 0)
    return pl.pallas_call(
        body, name="inproj_bwd", grid=(T // tm,),
        in_specs=[pl.BlockSpec((tm, AW), row)] * 4 + [pl.BlockSpec((tm, 128), row), _full((W_ROWS, D)),
                                                       pl.BlockSpec((tm, D), row), pl.BlockSpec((tm, D), row), _full((1, D))],
        out_specs=[pl.BlockSpec((tm, D), row), _full((1, D))],
        out_shape=[jax.ShapeDtypeStruct((T, D), f32), jax.ShapeDtypeStruct((1, D), f32)],
        compiler_params=_cp("arbitrary"),
    )(dq, dk, dv, du, df, w, x, dx1, g1)


def _adamw_math(w, g, m, v):
    m = B1 * m + (1.0 - B1) * g
    v = B2 * v + (1.0 - B2) * (g * g)
    m_hat = m / (1.0 - B1 ** STEP)
    v_hat = v / (1.0 - B2 ** STEP)
    delta = -LR * (m_hat / (jnp.sqrt(v_hat) + AEPS) + WD * w)
    return delta, m, v


SHARD_STEPS = 4


def _adamw_shards(ws, ms, vs, ps_mine, ps_other, name, steps):
    n = len(ws)

    def body(*refs):
        ins, outs = refs[:5 * n], refs[5 * n:]
        for k in range(n):
            w_ref, m_ref, v_ref, a_ref, b_ref = ins[5 * k:5 * k + 5]
            g_ref, d_ref, nm_ref, nv_ref = outs[4 * k:4 * k + 4]
            g = (a_ref[...].astype(f32) + b_ref[...].astype(f32)).reshape(w_ref.shape)
            g_ref[...] = g
            d_ref[...], nm_ref[...], nv_ref[...] = _adamw_math(w_ref[...], g, m_ref[...], v_ref[...])

    in_specs, out_specs, out_shape = [], [], []
    for w, p in zip(ws, ps_mine):
        rest = tuple(w.shape[1:])
        tr = w.shape[0] // steps
        assert tr * steps == w.shape[0]
        spec = pl.BlockSpec((tr,) + rest, lambda i, _n=len(rest): (i,) + (0,) * _n)
        pspec = pl.BlockSpec((tr, p.shape[1]), lambda i: (i, 0))
        in_specs += [spec] * 3 + [pspec] * 2
        out_specs += [spec] * 4
        out_shape += [jax.ShapeDtypeStruct(w.shape, f32)] * 4
    args = [a for k in range(n) for a in (ws[k], ms[k], vs[k], ps_mine[k], ps_other[k])]
    res = pl.pallas_call(
        body, name=name, grid=(steps,), in_specs=in_specs, out_specs=out_specs, out_shape=out_shape,
        compiler_params=_cp("parallel"),
    )(*args)
    return [res[4 * k:4 * k + 4] for k in range(n)]


SMALL_SLOTS = ((0, 8, 128), (8, 16, 128), (16, 24, 128), (24, 28, 128), (32, 33, 8))
LOSS_ROW = 39


def _adamw_small(ws, ms, vs, parts, parts_wp):
    n = len(ws)

    def body(*refs):
        w_refs, m_refs, v_refs = refs[0:n], refs[n:2 * n], refs[2 * n:3 * n]
        p_ref, pw_ref = refs[3 * n], refs[3 * n + 1]
        outs = refs[3 * n + 2:]
        g_all = p_ref[0]
        g_wp = pw_ref[0]
        for k in range(1, 8):
            g_all = g_all + p_ref[k]
            g_wp = g_wp + pw_ref[k]
        grads = [g_all[r0:r1, 0:lanes] for r0, r1, lanes in SMALL_SLOTS] + [g_wp]
        for idx, g in enumerate(grads):
            d, nm, nv = _adamw_math(w_refs[idx][...], g, m_refs[idx][...], v_refs[idx][...])
            outs[idx][...] = g
            outs[n + idx][...] = d
            outs[2 * n + idx][...] = nm
            outs[3 * n + idx][...] = nv
        outs[4 * n][...] = g_all[LOSS_ROW:LOSS_ROW + 1, :]

    shapes = [jax.ShapeDtypeStruct(w.shape, f32) for w in ws]
    res = pl.pallas_call(
        body, name="adamw_small", out_shape=shapes * 4 + [jax.ShapeDtypeStruct((1, 128), f32)],
    )(*ws, *ms, *vs, parts, parts_wp)
    return res[:4 * n], res[4 * n]


def _sum4(recvs, gs, mine, name, steps):
    n = len(recvs)

    def body(mine_ref, *refs):
        for r_ref, g_ref, o_ref in zip(refs[:n], refs[n:2 * n], refs[2 * n:]):
            o_ref[...] = ((g_ref[...].astype(f32) + r_ref[0].astype(f32))
                          + (r_ref[1].astype(f32) + r_ref[2].astype(f32))).astype(bf16)

    r_specs, g_specs, o_specs, shapes = [], [], [], []
    for recv in recvs:
        _, rows, cols = recv.shape
        tr = rows // steps
        assert tr * steps == rows
        r_specs.append(pl.BlockSpec((3, tr, cols), lambda i, m: (0, i, 0)))
        g_specs.append(pl.BlockSpec((None, tr, cols), lambda i, m: (m[0], i, 0)))
        o_specs.append(pl.BlockSpec((tr, cols), lambda i, m: (i, 0)))
        shapes.append(jax.ShapeDtypeStruct((rows, cols), bf16))
    grid_spec = pltpu.PrefetchScalarGridSpec(num_scalar_prefetch=1, grid=(steps,), in_specs=r_specs + g_specs,
                                             out_specs=o_specs)
    return pl.pallas_call(
        body, name=name, grid_spec=grid_spec, out_shape=shapes, compiler_params=_cp("arbitrary"),
    )(mine, *recvs, *gs)


_HBM = pl.BlockSpec(memory_space=pltpu.HBM)
_SEM = pl.BlockSpec(memory_space=pltpu.SEMAPHORE)
_EFFECT = pltpu.SideEffectType.DATAFLOW_SIDE_EFFECTING


def _in_hbm(a):
    return pltpu.with_memory_space_constraint(a, pltpu.HBM)


def _mesh_pos():
    return lax.axis_index("x"), lax.axis_index("y"), lax.axis_index("c")


def _other_chips(x, y):
    return [(1 - x, y), (x, 1 - y), (1 - x, 1 - y)]


def _gather_copy(srcs, lands, send_sems, recv_sems, a, k, slot):
    x, y, c = _mesh_pos()
    cx, cy = _other_chips(x, y)[k]
    return pltpu.make_async_remote_copy(
        src_ref=srcs[a], dst_ref=lands[a].at[slot], send_sem=send_sems.at[3 * a + k], recv_sem=recv_sems.at[3 * a + k],
        device_id=(cx, cy, c), device_id_type=MESH)


def _scatter_copy(srcs, lands, send_sems, recv_sems, a, k):
    x, y, c = _mesh_pos()
    cx, cy = _other_chips(x, y)[k]
    return pltpu.make_async_remote_copy(
        src_ref=srcs[a].at[2 * cx + cy], dst_ref=lands[a].at[k], send_sem=send_sems.at[3 * a + k],
        recv_sem=recv_sems.at[3 * a + k], device_id=(cx, cy, c), device_id_type=MESH)


def _all_gather_w_in(part):
    cols = part.shape[1] // 2

    def body(src, dst, send_sems, recv_sems, loc_sem):
        x, y, c = _mesh_pos()
        mine = 2 * x + y
        chips = _other_chips(x, y)
        half = lambda ref, cc: ref.at[:, pl.ds(pl.multiple_of(cc * cols, cols), cols)]

        def over_ici(k, slot):
            cx, cy = chips[k]
            return pltpu.make_async_remote_copy(
                src_ref=half(src, c), dst_ref=half(dst.at[slot], c), send_sem=send_sems.at[k], recv_sem=recv_sems.at[k],
                device_id=(cx, cy, c), device_id_type=MESH)

        def to_sibling(k, cc):
            slot = 2 * chips[k][0] + chips[k][1]
            return pltpu.make_async_remote_copy(
                src_ref=half(dst.at[slot], cc), dst_ref=half(dst.at[slot], cc), send_sem=send_sems.at[3 + k],
                recv_sem=recv_sems.at[3 + k], device_id=(x, y, 1 - c), device_id_type=MESH)

        local = pltpu.make_async_copy(src, dst.at[mine], loc_sem.at[0])
        local.start()
        first = [over_ici(k, mine) for k in range(3)]
        for cp in first:
            cp.start()
        passed = [to_sibling(k, c) for k in range(3)]
        for k in range(3):
            over_ici(k, 2 * chips[k][0] + chips[k][1]).wait_recv()
            passed[k].start()
        for k in range(3):
            to_sibling(k, 1 - c).wait_recv()
        for cp in first + passed:
            cp.wait_send()
        local.wait()

    return pl.pallas_call(
        body, name="all_gather_w_in", in_specs=[_HBM], out_specs=_HBM,
        out_shape=jax.ShapeDtypeStruct((NSH,) + part.shape, part.dtype),
        scratch_shapes=[pltpu.SemaphoreType.DMA((6,)), pltpu.SemaphoreType.DMA((6,)), pltpu.SemaphoreType.DMA((1,))],
    )(part)


def _split_start(name, srcs, lands, n_sems, plan, dep):
    n, nl = len(srcs), len(lands)

    def body(*refs):
        src_refs, land_refs = refs[:n], refs[n:n + nl]
        send_sems, recv_sems = refs[n + nl + 1], refs[n + nl + 2]
        token = refs[-1]
        sends, _ = plan(src_refs, land_refs, send_sems, recv_sems)
        for cp in sends:
            cp.start()
        token[...] = jnp.zeros_like(token)

    outs = pl.pallas_call(
        body, name=name,
        in_specs=[_HBM] * (n + nl) + [pl.BlockSpec(memory_space=pl.ANY)],
        out_specs=[_SEM, _SEM] + [_HBM] * (n + nl) + [pl.BlockSpec(memory_space=pltpu.VMEM)],
        out_shape=[pltpu.SemaphoreType.DMA((n_sems,)), pltpu.SemaphoreType.DMA((n_sems,))]
        + [pltpu.HBM(a.shape, a.dtype) for a in list(srcs) + list(lands)] + [jax.ShapeDtypeStruct((8, 128), f32)],
        input_output_aliases={i: 2 + i for i in range(n + nl)},
        compiler_params=pltpu.CompilerParams(has_side_effects=_EFFECT),
    )(*[_in_hbm(a) for a in list(srcs) + list(lands)], dep)
    return outs[0], outs[1], list(outs[2:2 + n]), list(outs[2 + n:2 + n + nl]), outs[-1]


def _split_wait(name, send_sems, recv_sems, srcs, lands, after, plan):
    n, nl = len(srcs), len(lands)

    def body(*refs):
        src_refs, land_refs = refs[:n], refs[n:n + nl]
        s_sems, r_sems = refs[n + nl], refs[n + nl + 1]
        sends, recvs = plan(src_refs, land_refs, s_sems, r_sems)
        for cp in recvs:
            cp.wait_recv()
        for cp in sends:
            cp.wait_send()

    outs = pl.pallas_call(
        body, name=name,
        in_specs=[_HBM] * (n + nl) + [_SEM, _SEM, pl.BlockSpec(memory_space=pl.ANY)],
        out_specs=[_HBM] * (n + nl),
        out_shape=[pltpu.HBM(a.shape, a.dtype) for a in list(srcs) + list(lands)],
        input_output_aliases={i: i for i in range(n + nl)},
        compiler_params=pltpu.CompilerParams(has_side_effects=_EFFECT),
    )(*srcs, *lands, send_sems, recv_sems, after)
    return list(outs[:n]), list(outs[n:])


def _gather_plan(srcs, lands, ss, rs):
    x, y, _ = _mesh_pos()
    chips = _other_chips(x, y)
    sends = [_gather_copy(srcs, lands, ss, rs, a, k, 2 * x + y) for a in range(len(srcs)) for k in range(3)]
    recvs = [_gather_copy(srcs, lands, ss, rs, a, k, 2 * chips[k][0] + chips[k][1])
             for a in range(len(srcs)) for k in range(3)]
    return sends, recvs


def _scatter_and_spread_plan(srcs, lands, ss, rs):
    x, y, c = _mesh_pos()
    me = 4 * x + 2 * y + c
    n = len(srcs) - 1
    cps = [_scatter_copy(srcs[:n], lands[:n], ss, rs, a, k) for a in range(n) for k in range(3)]
    for f in range(1, 8):
        peer = ((x + (f >> 2)) % 2, (y + ((f >> 1) & 1)) % 2, (c + (f & 1)) % 2)
        cps.append(pltpu.make_async_remote_copy(
            src_ref=srcs[n], dst_ref=lands[n].at[me], send_sem=ss.at[3 * n - 1 + f], recv_sem=rs.at[3 * n - 1 + f],
            device_id=peer, device_id_type=MESH))
    return cps, cps


def _swap_with_sibling(parts, name):
    n = len(parts)

    def body(*refs):
        srcs, dsts = refs[:n], refs[n:2 * n]
        send_sems, recv_sems = refs[2 * n:]
        x, y, c = _mesh_pos()
        cps = [pltpu.make_async_remote_copy(src_ref=srcs[a], dst_ref=dsts[a], send_sem=send_sems.at[a],
                                            recv_sem=recv_sems.at[a], device_id=(x, y, 1 - c), device_id_type=MESH)
               for a in range(n)]
        for cp in cps:
            cp.start()
        for cp in cps:
            cp.wait_recv()
        for cp in cps:
            cp.wait_send()

    return pl.pallas_call(
        body, name=name, in_specs=[_HBM] * n, out_specs=[_HBM] * n,
        out_shape=[jax.ShapeDtypeStruct(p.shape, p.dtype) for p in parts],
        scratch_shapes=[pltpu.SemaphoreType.DMA((n,)), pltpu.SemaphoreType.DMA((n,))],
    )(*parts)


def _forward(x, tgt, w_in_t, mlp_w_fn, g1, bfp, wp, scale, g2, gf, dep):
    h, qkv, u, fl = _rms_inproj(x, g1, w_in_t, dep)
    qaug, kaug = _fox_cumsum(fl, bfp)
    attn, lse = _attn_fwd(qkv, qaug, kaug)
    pooled, pool = _pool_fwd(u, wp, scale)
    wo, wgt, wut, wd = mlp_w_fn(attn)
    x1, h2 = _outproj(x, attn, pool, wo, g2)
    loss, dgf, dx2, dx2b, ud, silu, a_b = _mlp_fwd_loss(h2, x1, wgt, wut, wd, tgt, gf)
    saved = dict(h=h, qkv=qkv, fl=fl, qaug=qaug, kaug=kaug, attn=attn, lse=lse, pooled=pooled, pool=pool, x1=x1, h2=h2,
                 ud=ud, silu=silu, a_b=a_b, wo=wo, wgt=wgt, wut=wut, wd=wd)
    return loss, dgf, dx2, dx2b, saved


def _backward_mlp(sv, dx2, dx2b, g2):
    dgate, dup, dx1, dx1b, dg2 = _mlp_bwd(dx2b, dx2, sv["ud"], sv["silu"], sv["wgt"], sv["wut"], sv["wd"], sv["x1"], g2)
    (dwd,) = _mm_tn(sv["a_b"], [dx2b], "dw_down", a_sharded=True, tk=T)
    (dwgt,) = _mm_tn(dgate, [sv["h2"]], "dw_gate", a_sharded=True, tk=T)
    (dwut,) = _mm_tn(dup, [sv["h2"]], "dw_up", a_sharded=True, tk=T)
    return dx1, dx1b, dg2, (dwgt, dwut, dwd)


def _backward_outproj_pool(sv, dx1b, wp, scale):
    dattn, dpool = _outproj_bwd(dx1b, sv["wo"])
    (dwo,) = _mm_tn_rows([sv["attn"], sv["pool"]], dx1b, "dw_out", tk=2048, stacked=True)
    dwo = dwo.reshape(NSH, D // NSH, D)
    du, dscale, dwp = _pool_bwd(dpool, sv["pooled"], wp, scale)
    return dattn, dwo, du, dscale, dwp


def _backward_attn_inproj(sv, x, dx1, dattn, du, w_in_t, g1, bfp, dep):
    dq, dqs, dk, dks, dv = _attn_bwd(sv["qkv"], sv["qaug"], sv["kaug"], sv["attn"], dattn, sv["lse"], dep)
    df, dbf = _fox_cumsum_bwd(dqs, dks, sv["fl"], bfp)
    dx, dg1 = _inproj_bwd(dq, dk, dv, du, df, w_in_t, x, dx1, g1)
    dwq, dwk, dwv, dwu_in, dwf = _mm_tn_rows([dq, dk, dv, du, df], sv["h"], "dw_in")
    dwin = jnp.concatenate([dwq, dwk, dwv, dwf[0:8], dwu_in], axis=0)
    return dx, dg1, dbf, dwin.reshape(NSH, IN_S, D)


def kernel(x, norm1_g, w_in, b_forget, w_pool, pool_scale, w_out, norm2_g, w_gate, w_up, w_down, final_g, loss_target, m_norm1_g, m_w_in, m_b_forget, m_w_pool, m_pool_scale, m_w_out, m_norm2_g, m_w_gate, m_w_up, m_w_down, m_final_g, v_norm1_g, v_w_in, v_b_forget, v_w_pool, v_pool_scale, v_w_out, v_norm2_g, v_w_gate, v_w_up, v_w_down, v_final_g):
    mine = (2 * lax.axis_index("x") + lax.axis_index("y")).astype(jnp.int32)
    mine1 = mine.reshape(1)
    tr = lambda a: jnp.transpose(a[0])

    win4 = _all_gather_w_in(tr(w_in).astype(bf16))
    later = [w_out[0].astype(bf16), tr(w_gate).astype(bf16), tr(w_up).astype(bf16), w_down[0].astype(bf16)]
    lands = [lax.dynamic_update_slice(lax.empty((NSH,) + p.shape, bf16), p[None], (mine, 0, 0)) for p in later]
    ag_send, ag_recv, later_thru, lands_thru, ag_token = _split_start("all_gather_start", later, lands, 12, _gather_plan,
                                                                      win4)
    win = win4.reshape(IN_W, D)
    w_in_t = jnp.concatenate([win[0:3 * AW], win[3 * AW + 8:], win[3 * AW:3 * AW + 8], jnp.zeros((120, D), bf16)], axis=0)
    bfp = jnp.pad(b_forget, ((0, 0), (0, 120)))
    wp = w_pool[0].astype(bf16)
    gf = final_g.reshape(1, D)

    def later_weights(after):
        _, (wo4, wgt, wut, wd) = _split_wait("all_gather_wait", ag_send, ag_recv, later_thru, lands_thru, after, _gather_plan)
        return wo4.reshape(D, D), wgt, wut, wd

    xe, tgt = x[0], loss_target[0]
    loss_v, dgf, dx2, dx2b, sv = _forward(xe, tgt, w_in_t, later_weights, norm1_g, bfp, wp, pool_scale, norm2_g, gf, ag_token)
    dx1, dx1b, dg2, mlp_grads = _backward_mlp(sv, dx2, dx2b, norm2_g)
    dattn, dwo, du, dscale, dwp = _backward_outproj_pool(sv, dx1b, wp, pool_scale)
    me = (4 * lax.axis_index("x") + 2 * lax.axis_index("y") + lax.axis_index("c")).astype(jnp.int32)
    dwp = dwp.reshape(512, 128)
    first = [dwo] + list(mlp_grads) + [dwp]
    first_lands = [lax.empty((3,) + g.shape[1:], bf16) for g in first[:4]]
    first_lands.append(lax.dynamic_update_slice(lax.empty((8, 512, 128), f32), dwp[None], (me, 0, 0)))
    rs_send, rs_recv, first_thru, first_lands_thru, rs_token = _split_start(
        "reduce_scatter_start", first, first_lands, 19, _scatter_and_spread_plan, du)
    dx, dg1, dbf, dwin = _backward_attn_inproj(sv, xe, dx1, dattn, du, w_in_t, norm1_g, bfp, rs_token)

    pad8 = lambda r: jnp.pad(r, ((0, 8 - r.shape[0]), (0, 0)))
    loss_rows = jnp.concatenate([dbf, jnp.zeros((6, 128), f32), loss_v[0:1, :]], axis=0)
    small = jnp.concatenate([dg1.reshape(8, 128), dg2.reshape(8, 128), dgf.reshape(8, 128), pad8(dscale.reshape(4, 128)),
                             loss_rows], axis=0)
    small_land = lax.dynamic_update_slice(lax.empty((8, SMALL_ROWS, 128), f32), small[None], (me, 0, 0))
    tail_send, tail_recv, tail_thru, tail_lands_thru, tail_token = _split_start(
        "tail_start", [dwin, small], [lax.empty((3,) + dwin.shape[1:], bf16), small_land], 10, _scatter_and_spread_plan,
        dx)
    first_thru, first_recv = _split_wait("reduce_scatter_wait", rs_send, rs_recv, first_thru, first_lands_thru, tail_token,
                                         _scatter_and_spread_plan)
    wp_all = first_recv[4]
    tr3 = lambda a: jnp.transpose(a, (2, 0, 1))
    ws = [tr3(w_in), w_out[0], tr(w_gate), tr(w_up), w_down[0]]
    ms = [tr3(m_w_in), m_w_out[0], tr(m_w_gate), tr(m_w_up), m_w_down[0]]
    vs = [tr3(v_w_in), v_w_out[0], tr(v_w_gate), tr(v_w_up), v_w_down[0]]
    partial = _sum4(first_recv[:4], first_thru[:4], mine1, "sum4_first", SHARD_STEPS)
    other = _swap_with_sibling(partial, "swap_first")
    big = _adamw_shards(ws[1:], ms[1:], vs[1:], partial, other, "adamw_first", SHARD_STEPS)
    (dwin_thru, _), (in_recv_land, small_all) = _split_wait("tail_wait", tail_send, tail_recv, tail_thru, tail_lands_thru,
                                                            big[3][0], _scatter_and_spread_plan)
    partial_in = _sum4([in_recv_land], [dwin_thru], mine1, "sum4_in", 1)
    other_in = _swap_with_sibling(partial_in, "swap_in")
    big = _adamw_shards(ws[:1], ms[:1], vs[:1], partial_in, other_in, "adamw_in", 1) + big

    small_names = ["norm1_g", "norm2_g", "final_g", "pool_scale", "b_forget", "w_pool"]
    rows = lambda a, b, c, d, e, f: [a.reshape(8, 128), b.reshape(8, 128), c.reshape(8, 128), d.reshape(4, 128),
                                     e.reshape(1, 8), f.reshape(512, 128)]
    sm, loss_row = _adamw_small(rows(norm1_g, norm2_g, final_g, pool_scale, b_forget, w_pool),
                                rows(m_norm1_g, m_norm2_g, m_final_g, m_pool_scale, m_b_forget, m_w_pool),
                                rows(v_norm1_g, v_norm2_g, v_final_g, v_pool_scale, v_b_forget, v_w_pool), small_all, wp_all)
    small_shape = dict(norm1_g=(1, D), norm2_g=(1, D), final_g=(D,), pool_scale=(1, AW), b_forget=(1, 8),
                       w_pool=(1, 4, 128, 128))

    order = ["norm1_g", "w_in", "b_forget", "w_pool", "pool_scale", "w_out", "norm2_g", "w_gate", "w_up", "w_down", "final_g"]
    big_idx = {"w_in": 0, "w_out": 1, "w_gate": 2, "w_up": 3, "w_down": 4}
    outs = [loss_row[0, 0], dx[None]]
    for kind in range(4):
        for name in order:
            if name == "w_in":
                outs.append(jnp.transpose(big[0][kind], (1, 2, 0)))
            elif name in ("w_gate", "w_up"):
                outs.append(jnp.transpose(big[big_idx[name]][kind])[None])
            elif name in big_idx:
                outs.append(big[big_idx[name]][kind][None])
            else:
                outs.append(sm[6 * kind + small_names.index(name)].reshape(small_shape[name]))
    return tuple(outs)
```

```python
import jax
import jax.numpy as jnp
import numpy as np
from jax import lax
from jax.experimental import pallas as pl
from jax.experimental.pallas import tpu as pltpu

f32 = jnp.float32
bf16 = jnp.bfloat16

T = 4096
D = 1024
NSH = 4
IN_W = 2056
IN_S = IN_W // NSH
AW = 512
PAIRS = 4
SPARE = (64, 0)
ROW_SUM_LANE, COL_SUM_LANE = 0, 3
FF = 2816
FS = FF // NSH
WINDOWS = (2, 4, 8, 16)
HALO = 16
EPS = 1e-6
NEG = -1e30
LR, B1, B2, AEPS, WD, STEP = 0.001, 0.9, 0.999, 1e-08, 0.01, 10
SMALL_ROWS = 40

NT = (((1,), (1,)), ((), ()))
TN = (((0,), (0,)), ((), ()))

MESH = pl.DeviceIdType.MESH


def _cp(*sem):
    return pltpu.CompilerParams(dimension_semantics=sem)


def _full(shape):
    n = len(shape)
    return pl.BlockSpec(shape, lambda *_: (0,) * n)


def _resident(shape):
    n = len(shape)
    return pl.BlockSpec(shape, lambda *_: (0,) * n, pipeline_mode=pl.Buffered(1))


W_ROWS = 4 * AW + 128


def _rms_inproj(x, g1, w, dep):
    tm = 512

    def body(x_ref, g_ref, w_ref, dep_ref, h_ref, qkv_ref, u_ref, fl_ref):
        xv = x_ref[...]
        r = lax.rsqrt(jnp.mean(xv * xv, axis=-1, keepdims=True) + EPS)
        h = (xv * r * g_ref[...]).astype(bf16)
        h_ref[...] = h
        qkv_ref[...] = lax.dot_general(h, w_ref[0:3 * AW, :], NT, preferred_element_type=f32).astype(bf16)
        u_ref[...] = lax.dot_general(h, w_ref[3 * AW:4 * AW, :], NT, preferred_element_type=f32)
        fl_ref[...] = lax.dot_general(h, w_ref[4 * AW:W_ROWS, :], NT, preferred_element_type=f32)

    return pl.pallas_call(
        body, name="rms_inproj", grid=(T // tm,),
        in_specs=[pl.BlockSpec((tm, D), lambda i: (i, 0)), _full((1, D)), _full((W_ROWS, D)), _full((8, 128))],
        out_specs=[pl.BlockSpec((tm, D), lambda i: (i, 0)), pl.BlockSpec((tm, 3 * AW), lambda i: (i, 0)),
                   pl.BlockSpec((tm, AW), lambda i: (i, 0)), pl.BlockSpec((tm, 128), lambda i: (i, 0))],
        out_shape=[jax.ShapeDtypeStruct((T, D), bf16), jax.ShapeDtypeStruct((T, 3 * AW), bf16),
                   jax.ShapeDtypeStruct((T, AW), f32), jax.ShapeDtypeStruct((T, 128), f32)],
        compiler_params=_cp("parallel"),
    )(x, g1, w, dep)


CUMSUM_ROWS = 512
FS_CHUNKS = ((0, 256), (256, 512), (512, FS))


def _log_sigmoid(z):
    return jnp.minimum(z, 0.0) - jnp.log(1.0 + jnp.exp(-jnp.abs(z)))


def _split3(x):
    hi = x.astype(bf16)
    r1 = x - hi.astype(f32)
    mid = r1.astype(bf16)
    return hi, mid, (r1 - mid.astype(f32)).astype(bf16)


def _dot01(sel, x, sel_first):
    parts = _split3(x)
    if sel_first:
        return sum(jnp.dot(sel, p, preferred_element_type=f32) for p in parts)
    return sum(jnp.dot(p, sel, preferred_element_type=f32) for p in parts)


def _fox_cumsum(fl, bfp):
    tb = CUMSUM_ROWS
    nb = T // tb

    def body(fl_ref, b_ref, qa_ref, ka_ref, carry):
        i = pl.program_id(0)

        @pl.when(i == 0)
        def _():
            carry[...] = jnp.zeros_like(carry)

        lf = _log_sigmoid(fl_ref[...] + b_ref[...])
        r = lax.broadcasted_iota(jnp.int32, (tb, tb), 0)
        cc = lax.broadcasted_iota(jnp.int32, (tb, tb), 1)
        ltri = (cc <= r).astype(bf16)
        cb = _dot01(ltri, lf, True) + carry[0:1, :]
        carry[...] = jnp.broadcast_to(cb[tb - 1:tb, :], (8, 128))
        hi, mid, lo = _split3(cb * LOG2E)
        head = lax.broadcasted_iota(jnp.int32, (128, AW), 0)
        col = lax.broadcasted_iota(jnp.int32, (128, AW), 1)
        base = 128 * (head >> 1) + jnp.where((head & 1) == 0, SPARE[0], SPARE[1])
        place = lambda off: jnp.logical_and(col == base + off, head < 8).astype(bf16)
        mm = lambda a, off: jnp.dot(a, place(off), preferred_element_type=f32)
        cq = mm(hi, 0) + mm(mid, 1) + mm(lo, 2)
        ck = mm(hi, 3) + mm(mid, 4) + mm(lo, 5)
        within = jnp.bitwise_and(lax.broadcasted_iota(jnp.int32, (tb, AW), 1), 63)
        qa_ref[...] = jnp.where(jnp.logical_and(within >= 3, within <= 5), 1.0, cq).astype(bf16)
        ka_ref[...] = jnp.where(within <= 2, 1.0, -ck).astype(bf16)

    return pl.pallas_call(
        body, name="fox_cumsum", grid=(nb,),
        in_specs=[pl.BlockSpec((tb, 128), lambda i: (i, 0)), _full((1, 128))],
        out_specs=[pl.BlockSpec((tb, AW), lambda i: (i, 0)), pl.BlockSpec((tb, AW), lambda i: (i, 0))],
        out_shape=[jax.ShapeDtypeStruct((T, AW), bf16), jax.ShapeDtypeStruct((T, AW), bf16)],
        scratch_shapes=[pltpu.VMEM((8, 128), f32)],
        compiler_params=_cp("arbitrary"),
    )(fl, bfp)


ATT_T = 512
LOG2E = 1.4426950408889634
Q_SCALE = 0.125 * LOG2E


def _causal_steps(key_major):
    n = T // ATT_T
    if key_major:
        pairs = [(i, j) for j in range(n) for i in range(j, n)]
    else:
        pairs = [(i, j) for i in range(n) for j in range(i + 1)]
    it = np.array([p[0] for p in pairs], np.int32)
    jt = np.array([p[1] for p in pairs], np.int32)
    return jnp.asarray(it), jnp.asarray(jt)


def _row_blocks(tq, tk, on_diagonal):
    return ((0, tq // 2, tk // 2), (tq // 2, tq, tk)) if on_diagonal else ((0, tq, tk),)


def _attn_fwd(qkv, qaug, kaug):
    tq = tk = ATT_T
    it, jt = _causal_steps(False)
    nsteps = it.shape[0]

    rs = 64

    def body(it_ref, jt_ref, q_ref, k_ref, v_ref, qa_ref, ka_ref, o_ref, lse_ref, m_sc, acc_sc, s_sc, p_sc, alpha_sc):
        t = pl.program_id(1)
        i = it_ref[t]
        j = jt_ref[t]

        @pl.when(j == 0)
        def _():
            m_sc[...] = jnp.full_like(m_sc, NEG)
            acc_sc[...] = jnp.zeros_like(acc_sc)

        lane = lax.broadcasted_iota(jnp.int32, (tq, 128), 1)
        spare = SPARE

        def step(on_diagonal):
            q = (q_ref[...].astype(f32) * Q_SCALE).astype(bf16)
            k = k_ref[...]
            v = v_ref[...]
            qa = qa_ref[...]
            ka = ka_ref[...]
            blocks = _row_blocks(tq, tk, on_diagonal)
            for e in range(2):
                hm = (lane >= 64) if e else (lane < 64)
                qe = jnp.where(hm, q, qa)
                ke = jnp.where(hm, k, ka)
                for r0, r1, nc in blocks:
                    s_sc[e, r0:r1, 0:nc] = lax.dot_general(qe[r0:r1], ke[0:nc], NT, preferred_element_type=f32)
            for e in range(2):
                for r0, r1, nc in blocks:
                    for r in range(r0, r1, rs):
                        s = s_sc[e, r:r + rs, 0:nc]
                        if on_diagonal:
                            row = lax.broadcasted_iota(jnp.int32, (rs, nc), 0) + r
                            col = lax.broadcasted_iota(jnp.int32, (rs, nc), 1)
                            s = jnp.where(col <= row, s, NEG)
                        m_prev = m_sc[e, r:r + rs, :]
                        m_new = jnp.maximum(m_prev, jnp.max(s, axis=1, keepdims=True))
                        p_sc[e, r:r + rs, 0:nc] = jnp.exp2(s - jnp.tile(m_new, (1, nc // 128))).astype(bf16)
                        alpha_sc[e, r:r + rs, :] = jnp.exp2(m_prev - m_new)
                        m_sc[e, r:r + rs, :] = m_new
            for e in range(2):
                hm = (lane >= 64) if e else (lane < 64)
                ve = jnp.where(hm, v, (lane == spare[e]).astype(bf16))
                for r0, r1, nc in blocks:
                    acc_sc[e, r0:r1] = (alpha_sc[e, r0:r1] * acc_sc[e, r0:r1]
                                        + jnp.dot(p_sc[e, r0:r1, 0:nc], ve[0:nc], preferred_element_type=f32))

        @pl.when(j < i)
        def _():
            step(False)

        @pl.when(j == i)
        def _():
            step(True)
            l0 = acc_sc[0][:, spare[0]:spare[0] + 1]
            l1 = acc_sc[1][:, spare[1]:spare[1] + 1]
            o_ref[...] = jnp.where(lane < 64, acc_sc[0] / l0, acc_sc[1] / l1).astype(bf16)
            lse_ref[...] = jnp.where(lane < 64, m_sc[0] + jnp.log2(l0), m_sc[1] + jnp.log2(l1))

    qmap = lambda p, t, it, jt: (it[t], p)
    kmap = lambda p, t, it, jt: (jt[t], p)
    grid_spec = pltpu.PrefetchScalarGridSpec(
        num_scalar_prefetch=2, grid=(PAIRS, nsteps),
        in_specs=[pl.BlockSpec((tq, 128), qmap),
                  pl.BlockSpec((tk, 128), lambda p, t, it, jt: (jt[t], PAIRS + p)),
                  pl.BlockSpec((tk, 128), lambda p, t, it, jt: (jt[t], 2 * PAIRS + p)),
                  pl.BlockSpec((tq, 128), qmap), pl.BlockSpec((tk, 128), kmap)],
        out_specs=[pl.BlockSpec((tq, 128), qmap),
                   pl.BlockSpec((None, tq, 128), lambda p, t, it, jt: (p, it[t], 0))],
        scratch_shapes=[pltpu.VMEM((2, tq, 128), f32), pltpu.VMEM((2, tq, 128), f32), pltpu.VMEM((2, tq, tk), f32),
                        pltpu.VMEM((2, tq, tk), bf16), pltpu.VMEM((2, tq, 128), f32)],
    )
    return pl.pallas_call(
        body, name="fox_attn_fwd", grid_spec=grid_spec,
        out_shape=[jax.ShapeDtypeStruct((T, AW), bf16), jax.ShapeDtypeStruct((PAIRS, T, 128), f32)],
        compiler_params=_cp("parallel", "arbitrary"),
    )(it, jt, qkv, qkv, qkv, qaug, kaug)


def _pool_fwd(u, wp, scale):
    tm = 1024

    def body(u_ref, wp_ref, sc_ref, pooled_ref, pool_ref, ext):
        i = pl.program_id(0)

        @pl.when(i == 0)
        def _():
            ext[0:HALO, :] = jnp.zeros((HALO, AW), f32)

        uv = u_ref[...]
        ext[HALO:HALO + tm, :] = uv
        t_idx = i * tm + lax.broadcasted_iota(jnp.int32, (tm, 1), 0)
        for g, w in enumerate(WINDOWS):
            lo, hi = 128 * g, 128 * (g + 1)
            ug = uv[:, lo:hi]
            acc = ug
            for d in range(1, w):
                acc = acc + ext[HALO - d:HALO - d + tm, lo:hi]
            cnt = jnp.minimum(t_idx + 1, w).astype(f32)
            pb = (acc / cnt - ug).astype(bf16)
            pooled_ref[:, lo:hi] = pb
            mixed = jnp.dot(pb, wp_ref[g], preferred_element_type=f32)
            pool_ref[:, lo:hi] = (mixed * sc_ref[:, lo:hi]).astype(bf16)
        ext[0:HALO, :] = uv[tm - HALO:tm, :]

    return pl.pallas_call(
        body, name="pool_fwd", grid=(T // tm,),
        in_specs=[pl.BlockSpec((tm, AW), lambda i: (i, 0)), _full((4, 128, 128)), _full((1, AW))],
        out_specs=[pl.BlockSpec((tm, AW), lambda i: (i, 0)), pl.BlockSpec((tm, AW), lambda i: (i, 0))],
        out_shape=[jax.ShapeDtypeStruct((T, AW), bf16), jax.ShapeDtypeStruct((T, AW), bf16)],
        scratch_shapes=[pltpu.VMEM((tm + HALO, AW), f32)],
        compiler_params=_cp("arbitrary"),
    )(u, wp, scale)


def _outproj(x, attn, pool, wo, g2):
    tm = 1024

    def body(x_ref, a_ref, p_ref, wo_ref, g_ref, x1_ref, h2_ref):
        mixed = jnp.concatenate([a_ref[...], p_ref[...]], axis=1)
        x1 = x_ref[...] + jnp.dot(mixed, wo_ref[...], preferred_element_type=f32)
        x1_ref[...] = x1
        r = lax.rsqrt(jnp.mean(x1 * x1, axis=-1, keepdims=True) + EPS)
        h2_ref[...] = (x1 * r * g_ref[...]).astype(bf16)

    return pl.pallas_call(
        body, name="outproj", grid=(T // tm,),
        in_specs=[pl.BlockSpec((tm, D), lambda i: (i, 0)), pl.BlockSpec((tm, AW), lambda i: (i, 0)),
                  pl.BlockSpec((tm, AW), lambda i: (i, 0)), _full((D, D)), _full((1, D))],
        out_specs=[pl.BlockSpec((tm, D), lambda i: (i, 0)), pl.BlockSpec((tm, D), lambda i: (i, 0))],
        out_shape=[jax.ShapeDtypeStruct((T, D), f32), jax.ShapeDtypeStruct((T, D), bf16)],
        compiler_params=_cp("parallel"),
    )(x, attn, pool, wo, g2)


def _mlp_fwd_loss(h2, x1, wg, wu, wd, tgt, gf):
    tm = 512

    def body(h_ref, x1_ref, wg_ref, wu_ref, wd_ref, t_ref, g_ref,
             loss_ref, dg_ref, dx_ref, dxb_ref, ud_ref, silu_ref, a_ref, x2):
        i = pl.program_id(0)
        s = pl.program_id(1)

        @pl.when(jnp.logical_and(i == 0, s == 0))
        def _():
            loss_ref[...] = jnp.zeros_like(loss_ref)
            dg_ref[...] = jnp.zeros_like(dg_ref)

        h = h_ref[...]
        gus = [(lax.dot_general(h, wg_ref[s, c0:c1, :], NT, preferred_element_type=f32),
                lax.dot_general(h, wu_ref[s, c0:c1, :], NT, preferred_element_type=f32)) for c0, c1 in FS_CHUNKS]
        for (c0, c1), (gate, up) in zip(FS_CHUNKS, gus):
            sg = jax.nn.sigmoid(gate)
            silu = gate * sg
            ud_ref[:, c0:c1] = (up * (sg * (1.0 + gate * (1.0 - sg)))).astype(bf16)
            silu_ref[:, c0:c1] = silu.astype(bf16)
            a_ref[:, c0:c1] = (silu * up).astype(bf16)
        part = jnp.dot(a_ref[...], wd_ref[s], preferred_element_type=f32)

        @pl.when(s == 0)
        def _():
            x2[...] = x1_ref[...] + part

        @pl.when(s > 0)
        def _():
            x2[...] += part

        @pl.when(s == NSH - 1)
        def _():
            xv = x2[...]
            g = g_ref[...]
            r = lax.rsqrt(jnp.mean(xv * xv, axis=-1, keepdims=True) + EPS)
            xhat = xv * r
            e = xhat * g - t_ref[...]
            loss_ref[...] += 0.5 * jnp.sum(jnp.mean(e * e, axis=-1, keepdims=True))
            dy = e * (1.0 / D)
            dg_ref[...] += jnp.sum(dy * xhat, axis=0, keepdims=True)
            z = dy * g
            dx = r * (z - xhat * jnp.mean(z * xhat, axis=-1, keepdims=True))
            dx_ref[...] = dx
            dxb_ref[...] = dx.astype(bf16)

    row = lambda i, s: (i, 0)
    sl = lambda i, s: (s, i, 0)
    wsl = lambda i, s: (s, 0, 0)
    return pl.pallas_call(
        body, name="mlp_fwd_loss", grid=(T // tm, NSH),
        in_specs=[pl.BlockSpec((tm, D), row), pl.BlockSpec((tm, D), row),
                  _resident((NSH, FS, D)), _resident((NSH, FS, D)), _resident((NSH, FS, D)),
                  pl.BlockSpec((tm, D), row), pl.BlockSpec((1, D), lambda i, s: (0, 0))],
        out_specs=[pl.BlockSpec((8, 128), lambda i, s: (0, 0)), pl.BlockSpec((1, D), lambda i, s: (0, 0)),
                   pl.BlockSpec((tm, D), row), pl.BlockSpec((tm, D), row),
                   pl.BlockSpec((None, tm, FS), sl), pl.BlockSpec((None, tm, FS), sl), pl.BlockSpec((None, tm, FS), sl)],
        out_shape=[jax.ShapeDtypeStruct((8, 128), f32), jax.ShapeDtypeStruct((1, D), f32),
                   jax.ShapeDtypeStruct((T, D), f32), jax.ShapeDtypeStruct((T, D), bf16)]
        + [jax.ShapeDtypeStruct((NSH, T, FS), bf16)] * 3,
        scratch_shapes=[pltpu.VMEM((tm, D), f32)],
        compiler_params=_cp("arbitrary", "arbitrary"),
    )(h2, x1, wg, wu, wd, tgt, gf)


def _mlp_bwd(dx2b, dx2, ud, silu, wg, wu, wd, x1, g2):
    tm = 512

    def body(dxb_ref, dx_ref, ud_ref, silu_ref, wg_ref, wu_ref, wd_ref, x1_ref, g_ref,
             dg_ref, du_ref, dx1_ref, dx1b_ref, dn_ref, acc):
        i = pl.program_id(0)
        s = pl.program_id(1)

        @pl.when(jnp.logical_and(i == 0, s == 0))
        def _():
            dn_ref[...] = jnp.zeros_like(dn_ref)

        dxb = dxb_ref[...]
        das = [lax.dot_general(dxb, wd_ref[s, c0:c1, :], NT, preferred_element_type=f32) for c0, c1 in FS_CHUNKS]
        for (c0, c1), da in zip(FS_CHUNKS, das):
            dg_ref[:, c0:c1] = (da * ud_ref[:, c0:c1].astype(f32)).astype(bf16)
            du_ref[:, c0:c1] = (da * silu_ref[:, c0:c1].astype(f32)).astype(bf16)
        part = jnp.dot(dg_ref[...], wg_ref[s], preferred_element_type=f32)
        part = part + jnp.dot(du_ref[...], wu_ref[s], preferred_element_type=f32)

        @pl.when(s == 0)
        def _():
            acc[...] = part

        @pl.when(s > 0)
        def _():
            acc[...] += part

        @pl.when(s == NSH - 1)
        def _():
            xv = x1_ref[...]
            r = lax.rsqrt(jnp.mean(xv * xv, axis=-1, keepdims=True) + EPS)
            xhat = xv * r
            dh = acc[...]
            dn_ref[...] += jnp.sum(dh * xhat, axis=0, keepdims=True)
            z = dh * g_ref[...]
            dx1 = dx_ref[...] + r * (z - xhat * jnp.mean(z * xhat, axis=-1, keepdims=True))
            dx1_ref[...] = dx1
            dx1b_ref[...] = dx1.astype(bf16)

    row = lambda i, s: (i, 0)
    sl = lambda i, s: (s, i, 0)
    wsl = lambda i, s: (s, 0, 0)
    return pl.pallas_call(
        body, name="mlp_bwd", grid=(T // tm, NSH),
        in_specs=[pl.BlockSpec((tm, D), row), pl.BlockSpec((tm, D), row),
                  pl.BlockSpec((None, tm, FS), sl), pl.BlockSpec((None, tm, FS), sl),
                  _resident((NSH, FS, D)), _resident((NSH, FS, D)), _resident((NSH, FS, D)),
                  pl.BlockSpec((tm, D), row), pl.BlockSpec((1, D), lambda i, s: (0, 0))],
        out_specs=[pl.BlockSpec((None, tm, FS), sl), pl.BlockSpec((None, tm, FS), sl),
                   pl.BlockSpec((tm, D), row), pl.BlockSpec((tm, D), row), pl.BlockSpec((1, D), lambda i, s: (0, 0))],
        out_shape=[jax.ShapeDtypeStruct((NSH, T, FS), bf16)] * 2
        + [jax.ShapeDtypeStruct((T, D), f32), jax.ShapeDtypeStruct((T, D), bf16), jax.ShapeDtypeStruct((1, D), f32)],
        scratch_shapes=[pltpu.VMEM((tm, D), f32)],
        compiler_params=_cp("arbitrary", "arbitrary"),
    )(dx2b, dx2, ud, silu, wg, wu, wd, x1, g2)


def _mm_tn(a, bs, name, a_sharded=False, b_sharded=False, tk=512, out_dtype=bf16):
    nb = len(bs)
    sh = NSH if (a_sharded or b_sharded) else 1
    m = a.shape[-1]
    nk = T // tk

    def body(a_ref, *refs):
        kk = pl.program_id(1)
        av = a_ref[...]
        for b_ref, o_ref, acc in zip(refs[:nb], refs[nb:2 * nb], refs[2 * nb:]):
            upd = lax.dot_general(av, b_ref[...], TN, preferred_element_type=f32)

            @pl.when(kk == 0)
            def _():
                acc[...] = upd

            @pl.when(kk > 0)
            def _():
                acc[...] += upd

            @pl.when(kk == nk - 1)
            def _():
                o_ref[...] = acc[...].astype(out_dtype)

    a_spec = (pl.BlockSpec((None, tk, m), lambda s, k: (s, k, 0)) if a_sharded
              else pl.BlockSpec((tk, m), lambda s, k: (k, 0)))
    b_specs, o_specs, o_shapes, scratch = [], [], [], []
    for b in bs:
        n = b.shape[-1]
        b_specs.append(pl.BlockSpec((None, tk, n), lambda s, k: (s, k, 0)) if b_sharded
                       else pl.BlockSpec((tk, n), lambda s, k: (k, 0)))
        scratch.append(pltpu.VMEM((m, n), f32))
        if sh > 1:
            o_specs.append(pl.BlockSpec((None, m, n), lambda s, k: (s, 0, 0)))
            o_shapes.append(jax.ShapeDtypeStruct((sh, m, n), out_dtype))
        else:
            o_specs.append(pl.BlockSpec((m, n), lambda s, k: (0, 0)))
            o_shapes.append(jax.ShapeDtypeStruct((m, n), out_dtype))
    return pl.pallas_call(
        body, name=name, grid=(sh, nk), in_specs=[a_spec] + b_specs, out_specs=o_specs, out_shape=o_shapes,
        scratch_shapes=scratch, compiler_params=_cp("arbitrary", "arbitrary"),
    )(a, *bs)


def _mm_tn_rows(a_list, b, name, tk=1024, out_dtype=bf16, stacked=False):
    na = len(a_list)
    n = b.shape[-1]
    nk = T // tk
    ms = [a.shape[-1] for a in a_list]
    starts = [sum(ms[:k]) for k in range(na)] if stacked else [0] * na
    out_rows = [sum(ms)] if stacked else ms

    def body(*refs):
        a_refs, b_ref = refs[:na], refs[na]
        no = len(out_rows)
        o_refs, accs = refs[na + 1:na + 1 + no], refs[na + 1 + no:]
        kk = pl.program_id(0)
        bv = b_ref[...]
        for k, a_ref in enumerate(a_refs):
            o_ref, acc = (o_refs[0], accs[0]) if stacked else (o_refs[k], accs[k])
            rows = pl.ds(starts[k], ms[k])
            upd = lax.dot_general(a_ref[...], bv, TN, preferred_element_type=f32)

            @pl.when(kk == 0)
            def _():
                acc[rows, :] = upd

            @pl.when(kk > 0)
            def _():
                acc[rows, :] += upd

            @pl.when(kk == nk - 1)
            def _():
                o_ref[rows, :] = acc[rows, :].astype(out_dtype)

    return pl.pallas_call(
        body, name=name, grid=(nk,),
        in_specs=[pl.BlockSpec((tk, m), lambda k: (k, 0)) for m in ms] + [pl.BlockSpec((tk, n), lambda k: (k, 0))],
        out_specs=[pl.BlockSpec((r, n), lambda k: (0, 0)) for r in out_rows],
        out_shape=[jax.ShapeDtypeStruct((r, n), out_dtype) for r in out_rows],
        scratch_shapes=[pltpu.VMEM((r, n), f32) for r in out_rows],
        compiler_params=_cp("arbitrary"),
    )(*a_list, b)


def _outproj_bwd(dx1b, wo):
    tm = 1024

    def body(dx_ref, wo_ref, da_ref, dp_ref):
        dx = dx_ref[...]
        da_ref[...] = lax.dot_general(dx, wo_ref[0:AW, :], NT, preferred_element_type=f32).astype(bf16)
        dp_ref[...] = lax.dot_general(dx, wo_ref[AW:2 * AW, :], NT, preferred_element_type=f32)

    return pl.pallas_call(
        body, name="outproj_bwd", grid=(T // tm,),
        in_specs=[pl.BlockSpec((tm, D), lambda i: (i, 0)), _full((D, D))],
        out_specs=[pl.BlockSpec((tm, AW), lambda i: (i, 0)), pl.BlockSpec((tm, AW), lambda i: (i, 0))],
        out_shape=[jax.ShapeDtypeStruct((T, AW), bf16), jax.ShapeDtypeStruct((T, AW), f32)],
        compiler_params=_cp("parallel"),
    )(dx1b, wo)


def _pool_bwd(dpool, pooled, wp, scale):
    tm = 1024
    n = T // tm

    def body(dp_ref, pb_ref, wp_ref, sc_ref, du_ref, dsc_ref, dwp_ref, ext):
        i = pl.program_id(0)

        @pl.when(i == 0)
        def _():
            ext[tm:tm + HALO, :] = jnp.zeros((HALO, AW), f32)
            dsc_ref[...] = jnp.zeros_like(dsc_ref)
            dwp_ref[...] = jnp.zeros_like(dwp_ref)

        t_idx = (n - 1 - i) * tm + lax.broadcasted_iota(jnp.int32, (tm, 1), 0)
        for g, w in enumerate(WINDOWS):
            lo, hi = 128 * g, 128 * (g + 1)
            pb = pb_ref[:, lo:hi]
            mixed = jnp.dot(pb, wp_ref[g], preferred_element_type=f32)
            dpo = dp_ref[:, lo:hi]
            dsc_ref[:, lo:hi] += jnp.sum(dpo * mixed, axis=0, keepdims=True)
            dmr = (dpo * sc_ref[:, lo:hi]).astype(bf16)
            dwp_ref[g] += lax.dot_general(pb, dmr, TN, preferred_element_type=f32)
            dpl = lax.dot_general(dmr, wp_ref[g], NT, preferred_element_type=f32)
            cnt = jnp.minimum(t_idx + 1, w).astype(f32)
            dpn = dpl / cnt
            ext[0:tm, lo:hi] = dpn
            acc = dpn
            for d in range(1, w):
                acc = acc + ext[d:d + tm, lo:hi]
            du_ref[:, lo:hi] = (acc - dpl).astype(bf16)
        ext[tm:tm + HALO, :] = ext[0:HALO, :]

    rev = lambda i: (n - 1 - i, 0)
    return pl.pallas_call(
        body, name="pool_bwd", grid=(n,),
        in_specs=[pl.BlockSpec((tm, AW), rev), pl.BlockSpec((tm, AW), rev), _full((4, 128, 128)), _full((1, AW))],
        out_specs=[pl.BlockSpec((tm, AW), rev), _full((1, AW)), _full((4, 128, 128))],
        out_shape=[jax.ShapeDtypeStruct((T, AW), bf16), jax.ShapeDtypeStruct((1, AW), f32),
                   jax.ShapeDtypeStruct((4, 128, 128), f32)],
        scratch_shapes=[pltpu.VMEM((tm + HALO, AW), f32)],
        compiler_params=_cp("arbitrary"),
    )(dpool, pooled, wp, scale)


def _attn_bwd(qkv, qaug, kaug, attn, dattn, lse, dep):
    tq = tk = ATT_T
    n = T // tq
    it, jt = _causal_steps(True)
    nsteps = it.shape[0]

    rs = 64

    def body(it_ref, jt_ref, q_ref, k_ref, v_ref, qa_ref, ka_ref, o_ref, do_ref, lse_ref, dep_ref,
             dq_ref, dqs_ref, dk_ref, dks_ref, dv_ref, dq_acc, dk_acc, dv_acc, s_sc, dp_sc, p_sc, ds_sc):
        t = pl.program_id(1)
        i = it_ref[t]
        j = jt_ref[t]

        @pl.when(t == 0)
        def _():
            dq_acc[...] = jnp.zeros_like(dq_acc)

        @pl.when(i == j)
        def _():
            dk_acc[...] = jnp.zeros_like(dk_acc)
            dv_acc[...] = jnp.zeros_like(dv_acc)

        lane = lax.broadcasted_iota(jnp.int32, (tq, 128), 1)

        def step(on_diagonal):
            q = (q_ref[...].astype(f32) * Q_SCALE).astype(bf16)
            k = k_ref[...]
            v = v_ref[...]
            qa = qa_ref[...]
            ka = ka_ref[...]
            do = do_ref[...]
            dd = do.astype(f32) * o_ref[...].astype(f32)
            blocks = _row_blocks(tq, tk, on_diagonal)
            qes, kes, does, deltas = [], [], [], []
            for e in range(2):
                hm = (lane >= 64) if e else (lane < 64)
                qes.append(jnp.where(hm, q, qa))
                kes.append(jnp.where(hm, k, ka))
                does.append(jnp.where(hm, do, jnp.zeros_like(do)))
                deltas.append(jnp.sum(jnp.where(hm, dd, 0.0), axis=1, keepdims=True))
                for r0, r1, nc in blocks:
                    s_sc[e, r0:r1, 0:nc] = lax.dot_general(qes[e][r0:r1], kes[e][0:nc], NT, preferred_element_type=f32)
                    dp_sc[e, r0:r1, 0:nc] = lax.dot_general(does[e][r0:r1], v[0:nc], NT, preferred_element_type=f32)
            for e in range(2):
                for r0, r1, nc in blocks:
                    for r in range(r0, r1, rs):
                        s = s_sc[e, r:r + rs, 0:nc] - lse_ref[r:r + rs, 64 * e:64 * e + 1]
                        if on_diagonal:
                            row = lax.broadcasted_iota(jnp.int32, (rs, nc), 0) + r
                            col = lax.broadcasted_iota(jnp.int32, (rs, nc), 1)
                            s = jnp.where(col <= row, s, NEG)
                        p = jnp.exp2(s)
                        p_sc[e, r:r + rs, 0:nc] = p.astype(bf16)
                        ds_sc[e, r:r + rs, 0:nc] = (p * (dp_sc[e, r:r + rs, 0:nc] - deltas[e][r:r + rs, :])).astype(bf16)
                for r0, r1, nc in blocks:
                    dv_acc[:, 0:nc] += lax.dot_general(does[e][r0:r1], p_sc[e, r0:r1, 0:nc], TN, preferred_element_type=f32)
                    dsb = ds_sc[e, r0:r1, 0:nc]
                    dk_acc[e, :, 0:nc] += lax.dot_general(qes[e][r0:r1], dsb, TN, preferred_element_type=f32)
                    rq = pl.multiple_of(i * tq + r0, r1 - r0)
                    dq_acc[e, pl.ds(rq, r1 - r0), :] += jnp.dot(dsb, kes[e][0:nc], preferred_element_type=f32)

        @pl.when(i > j)
        def _():
            step(False)

        @pl.when(i == j)
        def _():
            step(True)

        @pl.when(i == n - 1)
        def _():
            dk0 = dk_acc[0].T
            dk1 = dk_acc[1].T
            dk_ref[...] = (jnp.where(lane < 64, dk0, dk1) * (1.0 / LOG2E)).astype(bf16)
            dks_ref[...] = jnp.where(lane < 64, dk1, dk0)
            dv_ref[...] = dv_acc[...].T.astype(bf16)

        @pl.when(t == nsteps - 1)
        def _():
            lane_t = lax.broadcasted_iota(jnp.int32, (T, 128), 1)
            dq_ref[...] = (jnp.where(lane_t < 64, dq_acc[0], dq_acc[1]) * 0.125).astype(bf16)
            dqs_ref[...] = jnp.where(lane_t < 64, dq_acc[1], dq_acc[0])

    qmap = lambda p, t, it, jt: (it[t], p)
    grid_spec = pltpu.PrefetchScalarGridSpec(
        num_scalar_prefetch=2, grid=(PAIRS, nsteps),
        in_specs=[pl.BlockSpec((tq, 128), qmap),
                  pl.BlockSpec((tk, 128), lambda p, t, it, jt: (jt[t], PAIRS + p)),
                  pl.BlockSpec((tk, 128), lambda p, t, it, jt: (jt[t], 2 * PAIRS + p)),
                  pl.BlockSpec((tq, 128), qmap), pl.BlockSpec((tk, 128), lambda p, t, it, jt: (jt[t], p)),
                  pl.BlockSpec((tq, 128), qmap), pl.BlockSpec((tq, 128), qmap),
                  pl.BlockSpec((None, tq, 128), lambda p, t, it, jt: (p, it[t], 0)),
                  pl.BlockSpec((8, 128), lambda p, t, it, jt: (0, 0))],
        out_specs=[pl.BlockSpec((T, 128), lambda p, t, it, jt: (0, p)),
                   pl.BlockSpec((None, T, 128), lambda p, t, it, jt: (p, 0, 0)),
                   pl.BlockSpec((tk, 128), lambda p, t, it, jt: (jt[t], p)),
                   pl.BlockSpec((None, tk, 128), lambda p, t, it, jt: (p, jt[t], 0)),
                   pl.BlockSpec((tk, 128), lambda p, t, it, jt: (jt[t], p))],
        scratch_shapes=[pltpu.VMEM((2, T, 128), f32), pltpu.VMEM((2, 128, tk), f32), pltpu.VMEM((128, tk), f32),
                        pltpu.VMEM((2, tq, tk), f32), pltpu.VMEM((2, tq, tk), f32), pltpu.VMEM((2, tq, tk), bf16),
                        pltpu.VMEM((2, tq, tk), bf16)],
    )
    return pl.pallas_call(
        body, name="fox_attn_bwd", grid_spec=grid_spec,
        out_shape=[jax.ShapeDtypeStruct((T, AW), bf16), jax.ShapeDtypeStruct((PAIRS, T, 128), f32),
                   jax.ShapeDtypeStruct((T, AW), bf16), jax.ShapeDtypeStruct((PAIRS, T, 128), f32),
                   jax.ShapeDtypeStruct((T, AW), bf16)],
        compiler_params=_cp("parallel", "arbitrary"),
    )(it, jt, qkv, qkv, qkv, qaug, kaug, attn, dattn, lse, dep)


def _fox_cumsum_bwd(dqs, dks, fl, bfp):
    tb = CUMSUM_ROWS
    nb = T // tb

    def body(dqs_ref, dks_ref, fl_ref, b_ref, df_ref, db_ref, carry):
        i = pl.program_id(0)

        @pl.when(i == 0)
        def _():
            carry[...] = jnp.zeros_like(carry)
            db_ref[...] = jnp.zeros_like(db_ref)

        r = lax.broadcasted_iota(jnp.int32, (128, 128), 0)
        cc = lax.broadcasted_iota(jnp.int32, (128, 128), 1)
        pick = lambda even_lane, odd_lane, p: jnp.logical_or(
            jnp.logical_and(r == even_lane, cc == 2 * p), jnp.logical_and(r == odd_lane, cc == 2 * p + 1)).astype(bf16)
        dc = jnp.zeros((tb, 128), f32)
        for p in range(PAIRS):
            rows_at = pick(SPARE[0] + ROW_SUM_LANE, SPARE[1] + ROW_SUM_LANE, p)
            cols_at = pick(SPARE[0] + COL_SUM_LANE, SPARE[1] + COL_SUM_LANE, p)
            dc = dc + _dot01(rows_at, dqs_ref[p], False) - _dot01(cols_at, dks_ref[p], False)
        rt = lax.broadcasted_iota(jnp.int32, (tb, tb), 0)
        ct = lax.broadcasted_iota(jnp.int32, (tb, tb), 1)
        utri = (ct >= rt).astype(bf16)
        dl = _dot01(utri, dc, True) + carry[0:1, :]
        carry[...] = jnp.broadcast_to(dl[0:1, :], (8, 128))
        z = fl_ref[...] + b_ref[...]
        df = dl * jax.nn.sigmoid(-z)
        df_ref[...] = df.astype(bf16)
        db_ref[...] += jnp.sum(df, axis=0, keepdims=True)

    rev = lambda i: (nb - 1 - i, 0)
    return pl.pallas_call(
        body, name="fox_cumsum_bwd", grid=(nb,),
        in_specs=[pl.BlockSpec((PAIRS, tb, 128), lambda i: (0, nb - 1 - i, 0)),
                  pl.BlockSpec((PAIRS, tb, 128), lambda i: (0, nb - 1 - i, 0)),
                  pl.BlockSpec((tb, 128), rev), _full((1, 128))],
        out_specs=[pl.BlockSpec((tb, 128), rev), _full((1, 128))],
        out_shape=[jax.ShapeDtypeStruct((T, 128), bf16), jax.ShapeDtypeStruct((1, 128), f32)],
        scratch_shapes=[pltpu.VMEM((8, 128), f32)],
        compiler_params=_cp("arbitrary"),
    )(dqs, dks, fl, bfp)


def _inproj_bwd(dq, dk, dv, du, df, w, x, dx1, g1):
    tm = 512

    def body(dq_ref, dk_ref, dv_ref, du_ref, df_ref, w_ref, x_ref, dx1_ref, g_ref, dx_ref, dn_ref):
        i = pl.program_id(0)

        @pl.when(i == 0)
        def _():
            dn_ref[...] = jnp.zeros_like(dn_ref)

        dproj = jnp.concatenate([dq_ref[...], dk_ref[...], dv_ref[...], du_ref[...], df_ref[...]], axis=1)
        dh = jnp.dot(dproj, w_ref[...], preferred_element_type=f32)
        xv = x_ref[...]
        r = lax.rsqrt(jnp.mean(xv * xv, axis=-1, keepdims=True) + EPS)
        xhat = xv * r
        dn_ref[...] += jnp.sum(dh * xhat, axis=0, keepdims=True)
        z = dh * g_ref[...]
        dx_ref[...] = dx1_ref[...] + r * (z - xhat * jnp.mean(z * xhat, axis=-1, keepdims=True))

    row = lambda i: (i, 0)
    return pl.pallas_call(
        body, name="inproj_bwd", grid=(T // tm,),
        in_specs=[pl.BlockSpec((tm, AW), row)] * 4 + [pl.BlockSpec((tm, 128), row), _full((W_ROWS, D)),
                                                       pl.BlockSpec((tm, D), row), pl.BlockSpec((tm, D), row), _full((1, D))],
        out_specs=[pl.BlockSpec((tm, D), row), _full((1, D))],
        out_shape=[jax.ShapeDtypeStruct((T, D), f32), jax.ShapeDtypeStruct((1, D), f32)],
        compiler_params=_cp("arbitrary"),
    )(dq, dk, dv, du, df, w, x, dx1, g1)


def _adamw_math(w, g, m, v):
    m = B1 * m + (1.0 - B1) * g
    v = B2 * v + (1.0 - B2) * (g * g)
    m_hat = m / (1.0 - B1 ** STEP)
    v_hat = v / (1.0 - B2 ** STEP)
    delta = -LR * (m_hat / (jnp.sqrt(v_hat) + AEPS) + WD * w)
    return delta, m, v


SHARD_STEPS = 4


def _adamw_shards(ws, ms, vs, ps_mine, ps_other, name, steps):
    n = len(ws)

    def body(*refs):
        ins, outs = refs[:5 * n], refs[5 * n:]
        for k in range(n):
            w_ref, m_ref, v_ref, a_ref, b_ref = ins[5 * k:5 * k + 5]
            g_ref, d_ref, nm_ref, nv_ref = outs[4 * k:4 * k + 4]
            g = (a_ref[...].astype(f32) + b_ref[...].astype(f32)).reshape(w_ref.shape)
            g_ref[...] = g
            d_ref[...], nm_ref[...], nv_ref[...] = _adamw_math(w_ref[...], g, m_ref[...], v_ref[...])

    in_specs, out_specs, out_shape = [], [], []
    for w, p in zip(ws, ps_mine):
        rest = tuple(w.shape[1:])
        tr = w.shape[0] // steps
        assert tr * steps == w.shape[0]
        spec = pl.BlockSpec((tr,) + rest, lambda i, _n=len(rest): (i,) + (0,) * _n)
        pspec = pl.BlockSpec((tr, p.shape[1]), lambda i: (i, 0))
        in_specs += [spec] * 3 + [pspec] * 2
        out_specs += [spec] * 4
        out_shape += [jax.ShapeDtypeStruct(w.shape, f32)] * 4
    args = [a for k in range(n) for a in (ws[k], ms[k], vs[k], ps_mine[k], ps_other[k])]
    res = pl.pallas_call(
        body, name=name, grid=(steps,), in_specs=in_specs, out_specs=out_specs, out_shape=out_shape,
        compiler_params=_cp("parallel"),
    )(*args)
    return [res[4 * k:4 * k + 4] for k in range(n)]


SMALL_SLOTS = ((0, 8, 128), (8, 16, 128), (16, 24, 128), (24, 28, 128), (32, 33, 8))
LOSS_ROW = 39


def _adamw_small(ws, ms, vs, parts, parts_wp):
    n = len(ws)

    def body(*refs):
        w_refs, m_refs, v_refs = refs[0:n], refs[n:2 * n], refs[2 * n:3 * n]
        p_ref, pw_ref = refs[3 * n], refs[3 * n + 1]
        outs = refs[3 * n + 2:]
        g_all = p_ref[0]
        g_wp = pw_ref[0]
        for k in range(1, 8):
            g_all = g_all + p_ref[k]
            g_wp = g_wp + pw_ref[k]
        grads = [g_all[r0:r1, 0:lanes] for r0, r1, lanes in SMALL_SLOTS] + [g_wp]
        for idx, g in enumerate(grads):
            d, nm, nv = _adamw_math(w_refs[idx][...], g, m_refs[idx][...], v_refs[idx][...])
            outs[idx][...] = g
            outs[n + idx][...] = d
            outs[2 * n + idx][...] = nm
            outs[3 * n + idx][...] = nv
        outs[4 * n][...] = g_all[LOSS_ROW:LOSS_ROW + 1, :]

    shapes = [jax.ShapeDtypeStruct(w.shape, f32) for w in ws]
    res = pl.pallas_call(
        body, name="adamw_small", out_shape=shapes * 4 + [jax.ShapeDtypeStruct((1, 128), f32)],
    )(*ws, *ms, *vs, parts, parts_wp)
    return res[:4 * n], res[4 * n]


def _sum4(recvs, gs, mine, name, steps):
    n = len(recvs)

    def body(mine_ref, *refs):
        for r_ref, g_ref, o_ref in zip(refs[:n], refs[n:2 * n], refs[2 * n:]):
            o_ref[...] = ((g_ref[...].astype(f32) + r_ref[0].astype(f32))
                          + (r_ref[1].astype(f32) + r_ref[2].astype(f32))).astype(bf16)

    r_specs, g_specs, o_specs, shapes = [], [], [], []
    for recv in recvs:
        _, rows, cols = recv.shape
        tr = rows // steps
        assert tr * steps == rows
        r_specs.append(pl.BlockSpec((3, tr, cols), lambda i, m: (0, i, 0)))
        g_specs.append(pl.BlockSpec((None, tr, cols), lambda i, m: (m[0], i, 0)))
        o_specs.append(pl.BlockSpec((tr, cols), lambda i, m: (i, 0)))
        shapes.append(jax.ShapeDtypeStruct((rows, cols), bf16))
    grid_spec = pltpu.PrefetchScalarGridSpec(num_scalar_prefetch=1, grid=(steps,), in_specs=r_specs + g_specs,
                                             out_specs=o_specs)
    return pl.pallas_call(
        body, name=name, grid_spec=grid_spec, out_shape=shapes, compiler_params=_cp("arbitrary"),
    )(mine, *recvs, *gs)


_HBM = pl.BlockSpec(memory_space=pltpu.HBM)
_SEM = pl.BlockSpec(memory_space=pltpu.SEMAPHORE)
_EFFECT = pltpu.SideEffectType.DATAFLOW_SIDE_EFFECTING


def _in_hbm(a):
    return pltpu.with_memory_space_constraint(a, pltpu.HBM)


def _mesh_pos():
    return lax.axis_index("x"), lax.axis_index("y"), lax.axis_index("c")


def _other_chips(x, y):
    return [(1 - x, y), (x, 1 - y), (1 - x, 1 - y)]


def _gather_copy(srcs, lands, send_sems, recv_sems, a, k, slot):
    x, y, c = _mesh_pos()
    cx, cy = _other_chips(x, y)[k]
    return pltpu.make_async_remote_copy(
        src_ref=srcs[a], dst_ref=lands[a].at[slot], send_sem=send_sems.at[3 * a + k], recv_sem=recv_sems.at[3 * a + k],
        device_id=(cx, cy, c), device_id_type=MESH)


def _scatter_copy(srcs, lands, send_sems, recv_sems, a, k):
    x, y, c = _mesh_pos()
    cx, cy = _other_chips(x, y)[k]
    return pltpu.make_async_remote_copy(
        src_ref=srcs[a].at[2 * cx + cy], dst_ref=lands[a].at[k], send_sem=send_sems.at[3 * a + k],
        recv_sem=recv_sems.at[3 * a + k], device_id=(cx, cy, c), device_id_type=MESH)


def _all_gather_w_in(part):
    cols = part.shape[1] // 2

    def body(src, dst, send_sems, recv_sems, loc_sem):
        x, y, c = _mesh_pos()
        mine = 2 * x + y
        chips = _other_chips(x, y)
        half = lambda ref, cc: ref.at[:, pl.ds(pl.multiple_of(cc * cols, cols), cols)]

        def over_ici(k, slot):
            cx, cy = chips[k]
            return pltpu.make_async_remote_copy(
                src_ref=half(src, c), dst_ref=half(dst.at[slot], c), send_sem=send_sems.at[k], recv_sem=recv_sems.at[k],
                device_id=(cx, cy, c), device_id_type=MESH)

        def to_sibling(k, cc):
            slot = 2 * chips[k][0] + chips[k][1]
            return pltpu.make_async_remote_copy(
                src_ref=half(dst.at[slot], cc), dst_ref=half(dst.at[slot], cc), send_sem=send_sems.at[3 + k],
                recv_sem=recv_sems.at[3 + k], device_id=(x, y, 1 - c), device_id_type=MESH)

        local = pltpu.make_async_copy(src, dst.at[mine], loc_sem.at[0])
        local.start()
        first = [over_ici(k, mine) for k in range(3)]
        for cp in first:
            cp.start()
        passed = [to_sibling(k, c) for k in range(3)]
        for k in range(3):
            over_ici(k, 2 * chips[k][0] + chips[k][1]).wait_recv()
            passed[k].start()
        for k in range(3):
            to_sibling(k, 1 - c).wait_recv()
        for cp in first + passed:
            cp.wait_send()
        local.wait()

    return pl.pallas_call(
        body, name="all_gather_w_in", in_specs=[_HBM], out_specs=_HBM,
        out_shape=jax.ShapeDtypeStruct((NSH,) + part.shape, part.dtype),
        scratch_shapes=[pltpu.SemaphoreType.DMA((6,)), pltpu.SemaphoreType.DMA((6,)), pltpu.SemaphoreType.DMA((1,))],
    )(part)


def _split_start(name, srcs, lands, n_sems, plan, dep):
    n, nl = len(srcs), len(lands)

    def body(*refs):
        src_refs, land_refs = refs[:n], refs[n:n + nl]
        send_sems, recv_sems = refs[n + nl + 1], refs[n + nl + 2]
        token = refs[-1]
        sends, _, own = plan(src_refs, land_refs, send_sems, recv_sems)
        for cp in own + sends:
            cp.start()
        token[...] = jnp.zeros_like(token)

    outs = pl.pallas_call(
        body, name=name,
        in_specs=[_HBM] * (n + nl) + [pl.BlockSpec(memory_space=pl.ANY)],
        out_specs=[_SEM, _SEM] + [_HBM] * (n + nl) + [pl.BlockSpec(memory_space=pltpu.VMEM)],
        out_shape=[pltpu.SemaphoreType.DMA((n_sems,)), pltpu.SemaphoreType.DMA((n_sems,))]
        + [pltpu.HBM(a.shape, a.dtype) for a in list(srcs) + list(lands)] + [jax.ShapeDtypeStruct((8, 128), f32)],
        input_output_aliases={i: 2 + i for i in range(n + nl)},
        compiler_params=pltpu.CompilerParams(has_side_effects=_EFFECT),
    )(*[_in_hbm(a) for a in list(srcs) + list(lands)], dep)
    return outs[0], outs[1], list(outs[2:2 + n]), list(outs[2 + n:2 + n + nl]), outs[-1]


def _split_wait(name, send_sems, recv_sems, srcs, lands, after, plan):
    n, nl = len(srcs), len(lands)

    def body(*refs):
        src_refs, land_refs = refs[:n], refs[n:n + nl]
        s_sems, r_sems = refs[n + nl], refs[n + nl + 1]
        sends, recvs, own = plan(src_refs, land_refs, s_sems, r_sems)
        for cp in own:
            cp.wait()
        for cp in recvs:
            cp.wait_recv()
        for cp in sends:
            cp.wait_send()

    outs = pl.pallas_call(
        body, name=name,
        in_specs=[_HBM] * (n + nl) + [_SEM, _SEM, pl.BlockSpec(memory_space=pl.ANY)],
        out_specs=[_HBM] * (n + nl),
        out_shape=[pltpu.HBM(a.shape, a.dtype) for a in list(srcs) + list(lands)],
        input_output_aliases={i: i for i in range(n + nl)},
        compiler_params=pltpu.CompilerParams(has_side_effects=_EFFECT),
    )(*srcs, *lands, send_sems, recv_sems, after)
    return list(outs[:n]), list(outs[n:])


def _gather_plan(srcs, lands, ss, rs):
    x, y, _ = _mesh_pos()
    chips = _other_chips(x, y)
    sends = [_gather_copy(srcs, lands, ss, rs, a, k, 2 * x + y) for a in range(len(srcs)) for k in range(3)]
    recvs = [_gather_copy(srcs, lands, ss, rs, a, k, 2 * chips[k][0] + chips[k][1])
             for a in range(len(srcs)) for k in range(3)]
    own = [pltpu.make_async_copy(srcs[a], lands[a].at[2 * x + y], rs.at[3 * len(srcs) + a]) for a in range(len(srcs))]
    return sends, recvs, own


def _scatter_and_spread_plan(srcs, lands, ss, rs):
    x, y, c = _mesh_pos()
    me = 4 * x + 2 * y + c
    n = len(srcs) - 1
    cps = [_scatter_copy(srcs[:n], lands[:n], ss, rs, a, k) for a in range(n) for k in range(3)]
    for f in range(1, 8):
        peer = ((x + (f >> 2)) % 2, (y + ((f >> 1) & 1)) % 2, (c + (f & 1)) % 2)
        cps.append(pltpu.make_async_remote_copy(
            src_ref=srcs[n], dst_ref=lands[n].at[me], send_sem=ss.at[3 * n - 1 + f], recv_sem=rs.at[3 * n - 1 + f],
            device_id=peer, device_id_type=MESH))
    own = [pltpu.make_async_copy(srcs[n], lands[n].at[me], rs.at[3 * n + 7])]
    return cps, cps, own


def _swap_with_sibling(parts, name):
    n = len(parts)

    def body(*refs):
        srcs, dsts = refs[:n], refs[n:2 * n]
        send_sems, recv_sems = refs[2 * n:]
        x, y, c = _mesh_pos()
        cps = [pltpu.make_async_remote_copy(src_ref=srcs[a], dst_ref=dsts[a], send_sem=send_sems.at[a],
                                            recv_sem=recv_sems.at[a], device_id=(x, y, 1 - c), device_id_type=MESH)
               for a in range(n)]
        for cp in cps:
            cp.start()
        for cp in cps:
            cp.wait_recv()
        for cp in cps:
            cp.wait_send()

    return pl.pallas_call(
        body, name=name, in_specs=[_HBM] * n, out_specs=[_HBM] * n,
        out_shape=[jax.ShapeDtypeStruct(p.shape, p.dtype) for p in parts],
        scratch_shapes=[pltpu.SemaphoreType.DMA((n,)), pltpu.SemaphoreType.DMA((n,))],
    )(*parts)


def _forward(x, tgt, w_in_t, mlp_w_fn, g1, bfp, wp, scale, g2, gf, dep):
    h, qkv, u, fl = _rms_inproj(x, g1, w_in_t, dep)
    qaug, kaug = _fox_cumsum(fl, bfp)
    attn, lse = _attn_fwd(qkv, qaug, kaug)
    pooled, pool = _pool_fwd(u, wp, scale)
    wo, wgt, wut, wd = mlp_w_fn(attn)
    x1, h2 = _outproj(x, attn, pool, wo, g2)
    loss, dgf, dx2, dx2b, ud, silu, a_b = _mlp_fwd_loss(h2, x1, wgt, wut, wd, tgt, gf)
    saved = dict(h=h, qkv=qkv, fl=fl, qaug=qaug, kaug=kaug, attn=attn, lse=lse, pooled=pooled, pool=pool, x1=x1, h2=h2,
                 ud=ud, silu=silu, a_b=a_b, wo=wo, wgt=wgt, wut=wut, wd=wd)
    return loss, dgf, dx2, dx2b, saved


def _backward_mlp(sv, dx2, dx2b, g2):
    dgate, dup, dx1, dx1b, dg2 = _mlp_bwd(dx2b, dx2, sv["ud"], sv["silu"], sv["wgt"], sv["wut"], sv["wd"], sv["x1"], g2)
    (dwd,) = _mm_tn(sv["a_b"], [dx2b], "dw_down", a_sharded=True, tk=T)
    (dwgt,) = _mm_tn(dgate, [sv["h2"]], "dw_gate", a_sharded=True, tk=T)
    (dwut,) = _mm_tn(dup, [sv["h2"]], "dw_up", a_sharded=True, tk=T)
    return dx1, dx1b, dg2, (dwgt, dwut, dwd)


def _backward_outproj_pool(sv, dx1b, wp, scale):
    dattn, dpool = _outproj_bwd(dx1b, sv["wo"])
    (dwo,) = _mm_tn_rows([sv["attn"], sv["pool"]], dx1b, "dw_out", tk=2048, stacked=True)
    dwo = dwo.reshape(NSH, D // NSH, D)
    du, dscale, dwp = _pool_bwd(dpool, sv["pooled"], wp, scale)
    return dattn, dwo, du, dscale, dwp


def _backward_attn_inproj(sv, x, dx1, dattn, du, w_in_t, g1, bfp, dep):
    dq, dqs, dk, dks, dv = _attn_bwd(sv["qkv"], sv["qaug"], sv["kaug"], sv["attn"], dattn, sv["lse"], dep)
    df, dbf = _fox_cumsum_bwd(dqs, dks, sv["fl"], bfp)
    dx, dg1 = _inproj_bwd(dq, dk, dv, du, df, w_in_t, x, dx1, g1)
    dwq, dwk, dwv, dwu_in, dwf = _mm_tn_rows([dq, dk, dv, du, df], sv["h"], "dw_in")
    dwin = jnp.concatenate([dwq, dwk, dwv, dwf[0:8], dwu_in], axis=0)
    return dx, dg1, dbf, dwin.reshape(NSH, IN_S, D)


def kernel(x, norm1_g, w_in, b_forget, w_pool, pool_scale, w_out, norm2_g, w_gate, w_up, w_down, final_g, loss_target, m_norm1_g, m_w_in, m_b_forget, m_w_pool, m_pool_scale, m_w_out, m_norm2_g, m_w_gate, m_w_up, m_w_down, m_final_g, v_norm1_g, v_w_in, v_b_forget, v_w_pool, v_pool_scale, v_w_out, v_norm2_g, v_w_gate, v_w_up, v_w_down, v_final_g):
    mine = (2 * lax.axis_index("x") + lax.axis_index("y")).astype(jnp.int32)
    mine1 = mine.reshape(1)
    tr = lambda a: jnp.transpose(a[0])

    win4 = _all_gather_w_in(tr(w_in).astype(bf16))
    later = [w_out[0].astype(bf16), tr(w_gate).astype(bf16), tr(w_up).astype(bf16), w_down[0].astype(bf16)]
    lands = [lax.empty((NSH,) + p.shape, bf16) for p in later]
    ag_send, ag_recv, later_thru, lands_thru, ag_token = _split_start("all_gather_start", later, lands, 16, _gather_plan,
                                                                      win4)
    win = win4.reshape(IN_W, D)
    w_in_t = jnp.concatenate([win[0:3 * AW], win[3 * AW + 8:], win[3 * AW:3 * AW + 8], jnp.zeros((120, D), bf16)], axis=0)
    bfp = jnp.pad(b_forget, ((0, 0), (0, 120)))
    wp = w_pool[0].astype(bf16)
    gf = final_g.reshape(1, D)

    def later_weights(after):
        _, (wo4, wgt, wut, wd) = _split_wait("all_gather_wait", ag_send, ag_recv, later_thru, lands_thru, after, _gather_plan)
        return wo4.reshape(D, D), wgt, wut, wd

    xe, tgt = x[0], loss_target[0]
    loss_v, dgf, dx2, dx2b, sv = _forward(xe, tgt, w_in_t, later_weights, norm1_g, bfp, wp, pool_scale, norm2_g, gf, ag_token)
    dx1, dx1b, dg2, mlp_grads = _backward_mlp(sv, dx2, dx2b, norm2_g)
    dattn, dwo, du, dscale, dwp = _backward_outproj_pool(sv, dx1b, wp, pool_scale)
    first = [dwo] + list(mlp_grads) + [dwp.reshape(512, 128)]
    first_lands = [lax.empty((3,) + g.shape[1:], bf16) for g in first[:4]] + [lax.empty((8, 512, 128), f32)]
    rs_send, rs_recv, first_thru, first_lands_thru, rs_token = _split_start(
        "reduce_scatter_start", first, first_lands, 20, _scatter_and_spread_plan, du)
    dx, dg1, dbf, dwin = _backward_attn_inproj(sv, xe, dx1, dattn, du, w_in_t, norm1_g, bfp, rs_token)

    pad8 = lambda r: jnp.pad(r, ((0, 8 - r.shape[0]), (0, 0)))
    loss_rows = jnp.concatenate([dbf, jnp.zeros((6, 128), f32), loss_v[0:1, :]], axis=0)
    small = jnp.concatenate([dg1.reshape(8, 128), dg2.reshape(8, 128), dgf.reshape(8, 128), pad8(dscale.reshape(4, 128)),
                             loss_rows], axis=0)
    tail_send, tail_recv, tail_thru, tail_lands_thru, tail_token = _split_start(
        "tail_start", [dwin, small], [lax.empty((3,) + dwin.shape[1:], bf16), lax.empty((8, SMALL_ROWS, 128), f32)], 11,
        _scatter_and_spread_plan, dx)
    first_thru, first_recv = _split_wait("reduce_scatter_wait", rs_send, rs_recv, first_thru, first_lands_thru, tail_token,
                                         _scatter_and_spread_plan)
    wp_all = first_recv[4]
    tr3 = lambda a: jnp.transpose(a, (2, 0, 1))
    ws = [tr3(w_in), w_out[0], tr(w_gate), tr(w_up), w_down[0]]
    ms = [tr3(m_w_in), m_w_out[0], tr(m_w_gate), tr(m_w_up), m_w_down[0]]
    vs = [tr3(v_w_in), v_w_out[0], tr(v_w_gate), tr(v_w_up), v_w_down[0]]
    partial = _sum4(first_recv[:4], first_thru[:4], mine1, "sum4_first", SHARD_STEPS)
    other = _swap_with_sibling(partial, "swap_first")
    big = _adamw_shards(ws[1:], ms[1:], vs[1:], partial, other, "adamw_first", SHARD_STEPS)
    (dwin_thru, _), (in_recv_land, small_all) = _split_wait("tail_wait", tail_send, tail_recv, tail_thru, tail_lands_thru,
                                                            big[3][0], _scatter_and_spread_plan)
    partial_in = _sum4([in_recv_land], [dwin_thru], mine1, "sum4_in", 1)
    other_in = _swap_with_sibling(partial_in, "swap_in")
    big = _adamw_shards(ws[:1], ms[:1], vs[:1], partial_in, other_in, "adamw_in", 1) + big

    small_names = ["norm1_g", "norm2_g", "final_g", "pool_scale", "b_forget", "w_pool"]
    rows = lambda a, b, c, d, e, f: [a.reshape(8, 128), b.reshape(8, 128), c.reshape(8, 128), d.reshape(4, 128),
                                     e.reshape(1, 8), f.reshape(512, 128)]
    sm, loss_row = _adamw_small(rows(norm1_g, norm2_g, final_g, pool_scale, b_forget, w_pool),
                                rows(m_norm1_g, m_norm2_g, m_final_g, m_pool_scale, m_b_forget, m_w_pool),
                                rows(v_norm1_g, v_norm2_g, v_final_g, v_pool_scale, v_b_forget, v_w_pool), small_all, wp_all)
    small_shape = dict(norm1_g=(1, D), norm2_g=(1, D), final_g=(D,), pool_scale=(1, AW), b_forget=(1, 8),
                       w_pool=(1, 4, 128, 128))

    order = ["norm1_g", "w_in", "b_forget", "w_pool", "pool_scale", "w_out", "norm2_g", "w_gate", "w_up", "w_down", "final_g"]
    big_idx = {"w_in": 0, "w_out": 1, "w_gate": 2, "w_up": 3, "w_down": 4}
    outs = [loss_row[0, 0], dx[None]]
    for kind in range(4):
        for name in order:
            if name == "w_in":
                outs.append(jnp.transpose(big[0][kind], (1, 2, 0)))
            elif name in ("w_gate", "w_up"):
                outs.append(jnp.transpose(big[big_idx[name]][kind])[None])
            elif name in big_idx:
                outs.append(big[big_idx[name]][kind][None])
            else:
                outs.append(sm[6 * kind + small_names.index(name)].reshape(small_shape[name]))
    return tuple(outs)
```

```python
import jax
import jax.numpy as jnp
import numpy as np
from jax import lax
from jax.experimental import pallas as pl
from jax.experimental.pallas import tpu as pltpu

f32 = jnp.float32
bf16 = jnp.bfloat16

T = 4096
D = 1024
NSH = 4
IN_W = 2056
IN_S = IN_W // NSH
AW = 512
PAIRS = 4
SPARE = (64, 0)
ROW_SUM_LANE, COL_SUM_LANE = 0, 3
FF = 2816
FS = FF // NSH
WINDOWS = (2, 4, 8, 16)
HALO = 16
EPS = 1e-6
NEG = -1e30
LR, B1, B2, AEPS, WD, STEP = 0.001, 0.9, 0.999, 1e-08, 0.01, 10
SMALL_ROWS = 40

NT = (((1,), (1,)), ((), ()))
TN = (((0,), (0,)), ((), ()))

MESH = pl.DeviceIdType.MESH


def _cp(*sem):
    return pltpu.CompilerParams(dimension_semantics=sem)


def _full(shape):
    n = len(shape)
    return pl.BlockSpec(shape, lambda *_: (0,) * n)


def _resident(shape):
    n = len(shape)
    return pl.BlockSpec(shape, lambda *_: (0,) * n, pipeline_mode=pl.Buffered(1))


W_ROWS = 4 * AW + 128


def _rms_inproj(x, g1, w, dep):
    tm = 512

    def body(x_ref, g_ref, w_ref, dep_ref, h_ref, qkv_ref, u_ref, fl_ref):
        xv = x_ref[...]
        r = lax.rsqrt(jnp.mean(xv * xv, axis=-1, keepdims=True) + EPS)
        h = (xv * r * g_ref[...]).astype(bf16)
        h_ref[...] = h
        qkv_ref[...] = lax.dot_general(h, w_ref[0:3 * AW, :], NT, preferred_element_type=f32).astype(bf16)
        u_ref[...] = lax.dot_general(h, w_ref[3 * AW:4 * AW, :], NT, preferred_element_type=f32)
        fl_ref[...] = lax.dot_general(h, w_ref[4 * AW:W_ROWS, :], NT, preferred_element_type=f32)

    return pl.pallas_call(
        body, name="rms_inproj", grid=(T // tm,),
        in_specs=[pl.BlockSpec((tm, D), lambda i: (i, 0)), _full((1, D)), _full((W_ROWS, D)), _full((8, 128))],
        out_specs=[pl.BlockSpec((tm, D), lambda i: (i, 0)), pl.BlockSpec((tm, 3 * AW), lambda i: (i, 0)),
                   pl.BlockSpec((tm, AW), lambda i: (i, 0)), pl.BlockSpec((tm, 128), lambda i: (i, 0))],
        out_shape=[jax.ShapeDtypeStruct((T, D), bf16), jax.ShapeDtypeStruct((T, 3 * AW), bf16),
                   jax.ShapeDtypeStruct((T, AW), f32), jax.ShapeDtypeStruct((T, 128), f32)],
        compiler_params=_cp("parallel"),
    )(x, g1, w, dep)


CUMSUM_ROWS = 512
FS_CHUNKS = ((0, 256), (256, 512), (512, FS))


def _log_sigmoid(z):
    return jnp.minimum(z, 0.0) - jnp.log(1.0 + jnp.exp(-jnp.abs(z)))


def _split3(x):
    hi = x.astype(bf16)
    r1 = x - hi.astype(f32)
    mid = r1.astype(bf16)
    return hi, mid, (r1 - mid.astype(f32)).astype(bf16)


def _dot01(sel, x, sel_first):
    parts = _split3(x)
    if sel_first:
        return sum(jnp.dot(sel, p, preferred_element_type=f32) for p in parts)
    return sum(jnp.dot(p, sel, preferred_element_type=f32) for p in parts)


def _fox_cumsum(fl, bfp):
    tb = CUMSUM_ROWS
    nb = T // tb

    def body(fl_ref, b_ref, qa_ref, ka_ref, carry):
        i = pl.program_id(0)

        @pl.when(i == 0)
        def _():
            carry[...] = jnp.zeros_like(carry)

        lf = _log_sigmoid(fl_ref[...] + b_ref[...])
        r = lax.broadcasted_iota(jnp.int32, (tb, tb), 0)
        cc = lax.broadcasted_iota(jnp.int32, (tb, tb), 1)
        ltri = (cc <= r).astype(bf16)
        cb = _dot01(ltri, lf, True) + carry[0:1, :]
        carry[...] = jnp.broadcast_to(cb[tb - 1:tb, :], (8, 128))
        hi, mid, lo = _split3(cb * LOG2E)
        head = lax.broadcasted_iota(jnp.int32, (128, AW), 0)
        col = lax.broadcasted_iota(jnp.int32, (128, AW), 1)
        base = 128 * (head >> 1) + jnp.where((head & 1) == 0, SPARE[0], SPARE[1])
        place = lambda off: jnp.logical_and(col == base + off, head < 8).astype(bf16)
        mm = lambda a, off: jnp.dot(a, place(off), preferred_element_type=f32)
        cq = mm(hi, 0) + mm(mid, 1) + mm(lo, 2)
        ck = mm(hi, 3) + mm(mid, 4) + mm(lo, 5)
        within = jnp.bitwise_and(lax.broadcasted_iota(jnp.int32, (tb, AW), 1), 63)
        qa_ref[...] = jnp.where(jnp.logical_and(within >= 3, within <= 5), 1.0, cq).astype(bf16)
        ka_ref[...] = jnp.where(within <= 2, 1.0, -ck).astype(bf16)

    return pl.pallas_call(
        body, name="fox_cumsum", grid=(nb,),
        in_specs=[pl.BlockSpec((tb, 128), lambda i: (i, 0)), _full((1, 128))],
        out_specs=[pl.BlockSpec((tb, AW), lambda i: (i, 0)), pl.BlockSpec((tb, AW), lambda i: (i, 0))],
        out_shape=[jax.ShapeDtypeStruct((T, AW), bf16), jax.ShapeDtypeStruct((T, AW), bf16)],
        scratch_shapes=[pltpu.VMEM((8, 128), f32)],
        compiler_params=_cp("arbitrary"),
    )(fl, bfp)


ATT_T = 512
LOG2E = 1.4426950408889634
Q_SCALE = 0.125 * LOG2E


def _causal_steps(key_major):
    n = T // ATT_T
    if key_major:
        pairs = [(i, j) for j in range(n) for i in range(j, n)]
    else:
        pairs = [(i, j) for i in range(n) for j in range(i + 1)]
    it = np.array([p[0] for p in pairs], np.int32)
    jt = np.array([p[1] for p in pairs], np.int32)
    return jnp.asarray(it), jnp.asarray(jt)


def _row_blocks(tq, tk, on_diagonal):
    return ((0, tq // 2, tk // 2), (tq // 2, tq, tk)) if on_diagonal else ((0, tq, tk),)


def _attn_fwd(qkv, qaug, kaug):
    tq = tk = ATT_T
    it, jt = _causal_steps(False)
    nsteps = it.shape[0]

    rs = 64

    def body(it_ref, jt_ref, q_ref, k_ref, v_ref, qa_ref, ka_ref, o_ref, lse_ref, m_sc, acc_sc, s_sc, p_sc, alpha_sc):
        t = pl.program_id(1)
        i = it_ref[t]
        j = jt_ref[t]

        @pl.when(j == 0)
        def _():
            m_sc[...] = jnp.full_like(m_sc, NEG)
            acc_sc[...] = jnp.zeros_like(acc_sc)

        lane = lax.broadcasted_iota(jnp.int32, (tq, 128), 1)
        spare = SPARE

        def step(on_diagonal):
            q = (q_ref[...].astype(f32) * Q_SCALE).astype(bf16)
            k = k_ref[...]
            v = v_ref[...]
            qa = qa_ref[...]
            ka = ka_ref[...]
            blocks = _row_blocks(tq, tk, on_diagonal)
            for e in range(2):
                hm = (lane >= 64) if e else (lane < 64)
                qe = jnp.where(hm, q, qa)
                ke = jnp.where(hm, k, ka)
                for r0, r1, nc in blocks:
                    s_sc[e, r0:r1, 0:nc] = lax.dot_general(qe[r0:r1], ke[0:nc], NT, preferred_element_type=f32)
            for e in range(2):
                for r0, r1, nc in blocks:
                    for r in range(r0, r1, rs):
                        s = s_sc[e, r:r + rs, 0:nc]
                        if on_diagonal:
                            row = lax.broadcasted_iota(jnp.int32, (rs, nc), 0) + r
                            col = lax.broadcasted_iota(jnp.int32, (rs, nc), 1)
                            s = jnp.where(col <= row, s, NEG)
                        m_prev = m_sc[e, r:r + rs, :]
                        m_new = jnp.maximum(m_prev, jnp.max(s, axis=1, keepdims=True))
                        p_sc[e, r:r + rs, 0:nc] = jnp.exp2(s - jnp.tile(m_new, (1, nc // 128))).astype(bf16)
                        alpha_sc[e, r:r + rs, :] = jnp.exp2(m_prev - m_new)
                        m_sc[e, r:r + rs, :] = m_new
            for e in range(2):
                hm = (lane >= 64) if e else (lane < 64)
                ve = jnp.where(hm, v, (lane == spare[e]).astype(bf16))
                for r0, r1, nc in blocks:
                    acc_sc[e, r0:r1] = (alpha_sc[e, r0:r1] * acc_sc[e, r0:r1]
                                        + jnp.dot(p_sc[e, r0:r1, 0:nc], ve[0:nc], preferred_element_type=f32))

        @pl.when(j < i)
        def _():
            step(False)

        @pl.when(j == i)
        def _():
            step(True)
            l0 = acc_sc[0][:, spare[0]:spare[0] + 1]
            l1 = acc_sc[1][:, spare[1]:spare[1] + 1]
            o_ref[...] = jnp.where(lane < 64, acc_sc[0] / l0, acc_sc[1] / l1).astype(bf16)
            lse_ref[...] = jnp.where(lane < 64, m_sc[0] + jnp.log2(l0), m_sc[1] + jnp.log2(l1))

    qmap = lambda p, t, it, jt: (it[t], p)
    kmap = lambda p, t, it, jt: (jt[t], p)
    grid_spec = pltpu.PrefetchScalarGridSpec(
        num_scalar_prefetch=2, grid=(PAIRS, nsteps),
        in_specs=[pl.BlockSpec((tq, 128), qmap),
                  pl.BlockSpec((tk, 128), lambda p, t, it, jt: (jt[t], PAIRS + p)),
                  pl.BlockSpec((tk, 128), lambda p, t, it, jt: (jt[t], 2 * PAIRS + p)),
                  pl.BlockSpec((tq, 128), qmap), pl.BlockSpec((tk, 128), kmap)],
        out_specs=[pl.BlockSpec((tq, 128), qmap),
                   pl.BlockSpec((None, tq, 128), lambda p, t, it, jt: (p, it[t], 0))],
        scratch_shapes=[pltpu.VMEM((2, tq, 128), f32), pltpu.VMEM((2, tq, 128), f32), pltpu.VMEM((2, tq, tk), f32),
                        pltpu.VMEM((2, tq, tk), bf16), pltpu.VMEM((2, tq, 128), f32)],
    )
    return pl.pallas_call(
        body, name="fox_attn_fwd", grid_spec=grid_spec,
        out_shape=[jax.ShapeDtypeStruct((T, AW), bf16), jax.ShapeDtypeStruct((PAIRS, T, 128), f32)],
        compiler_params=_cp("parallel", "arbitrary"),
    )(it, jt, qkv, qkv, qkv, qaug, kaug)


def _pool_fwd(u, wp, scale):
    tm = 1024

    def body(u_ref, wp_ref, sc_ref, pooled_ref, pool_ref, ext):
        i = pl.program_id(0)

        @pl.when(i == 0)
        def _():
            ext[0:HALO, :] = jnp.zeros((HALO, AW), f32)

        uv = u_ref[...]
        ext[HALO:HALO + tm, :] = uv
        t_idx = i * tm + lax.broadcasted_iota(jnp.int32, (tm, 1), 0)
        for g, w in enumerate(WINDOWS):
            lo, hi = 128 * g, 128 * (g + 1)
            ug = uv[:, lo:hi]
            acc = ug
            for d in range(1, w):
                acc = acc + ext[HALO - d:HALO - d + tm, lo:hi]
            cnt = jnp.minimum(t_idx + 1, w).astype(f32)
            pb = (acc / cnt - ug).astype(bf16)
            pooled_ref[:, lo:hi] = pb
            mixed = jnp.dot(pb, wp_ref[g], preferred_element_type=f32)
            pool_ref[:, lo:hi] = (mixed * sc_ref[:, lo:hi]).astype(bf16)
        ext[0:HALO, :] = uv[tm - HALO:tm, :]

    return pl.pallas_call(
        body, name="pool_fwd", grid=(T // tm,),
        in_specs=[pl.BlockSpec((tm, AW), lambda i: (i, 0)), _full((4, 128, 128)), _full((1, AW))],
        out_specs=[pl.BlockSpec((tm, AW), lambda i: (i, 0)), pl.BlockSpec((tm, AW), lambda i: (i, 0))],
        out_shape=[jax.ShapeDtypeStruct((T, AW), bf16), jax.ShapeDtypeStruct((T, AW), bf16)],
        scratch_shapes=[pltpu.VMEM((tm + HALO, AW), f32)],
        compiler_params=_cp("arbitrary"),
    )(u, wp, scale)


def _outproj(x, attn, pool, wo, g2):
    tm = 1024

    def body(x_ref, a_ref, p_ref, wo_ref, g_ref, x1_ref, h2_ref):
        mixed = jnp.concatenate([a_ref[...], p_ref[...]], axis=1)
        x1 = x_ref[...] + jnp.dot(mixed, wo_ref[...], preferred_element_type=f32)
        x1_ref[...] = x1
        r = lax.rsqrt(jnp.mean(x1 * x1, axis=-1, keepdims=True) + EPS)
        h2_ref[...] = (x1 * r * g_ref[...]).astype(bf16)

    return pl.pallas_call(
        body, name="outproj", grid=(T // tm,),
        in_specs=[pl.BlockSpec((tm, D), lambda i: (i, 0)), pl.BlockSpec((tm, AW), lambda i: (i, 0)),
                  pl.BlockSpec((tm, AW), lambda i: (i, 0)), _full((D, D)), _full((1, D))],
        out_specs=[pl.BlockSpec((tm, D), lambda i: (i, 0)), pl.BlockSpec((tm, D), lambda i: (i, 0))],
        out_shape=[jax.ShapeDtypeStruct((T, D), f32), jax.ShapeDtypeStruct((T, D), bf16)],
        compiler_params=_cp("parallel"),
    )(x, attn, pool, wo, g2)


def _mlp_fwd_loss(h2, x1, wg, wu, wd, tgt, gf):
    tm = 512

    def body(h_ref, x1_ref, wg_ref, wu_ref, wd_ref, t_ref, g_ref,
             loss_ref, dg_ref, dx_ref, dxb_ref, ud_ref, silu_ref, a_ref, x2):
        i = pl.program_id(0)
        s = pl.program_id(1)

        @pl.when(jnp.logical_and(i == 0, s == 0))
        def _():
            loss_ref[...] = jnp.zeros_like(loss_ref)
            dg_ref[...] = jnp.zeros_like(dg_ref)

        h = h_ref[...]
        gus = [(lax.dot_general(h, wg_ref[s, c0:c1, :], NT, preferred_element_type=f32),
                lax.dot_general(h, wu_ref[s, c0:c1, :], NT, preferred_element_type=f32)) for c0, c1 in FS_CHUNKS]
        for (c0, c1), (gate, up) in zip(FS_CHUNKS, gus):
            sg = jax.nn.sigmoid(gate)
            silu = gate * sg
            ud_ref[:, c0:c1] = (up * (sg * (1.0 + gate * (1.0 - sg)))).astype(bf16)
            silu_ref[:, c0:c1] = silu.astype(bf16)
            a_ref[:, c0:c1] = (silu * up).astype(bf16)
        part = jnp.dot(a_ref[...], wd_ref[s], preferred_element_type=f32)

        @pl.when(s == 0)
        def _():
            x2[...] = x1_ref[...] + part

        @pl.when(s > 0)
        def _():
            x2[...] += part

        @pl.when(s == NSH - 1)
        def _():
            xv = x2[...]
            g = g_ref[...]
            r = lax.rsqrt(jnp.mean(xv * xv, axis=-1, keepdims=True) + EPS)
            xhat = xv * r
            e = xhat * g - t_ref[...]
            loss_ref[...] += 0.5 * jnp.sum(jnp.mean(e * e, axis=-1, keepdims=True))
            dy = e * (1.0 / D)
            dg_ref[...] += jnp.sum(dy * xhat, axis=0, keepdims=True)
            z = dy * g
            dx = r * (z - xhat * jnp.mean(z * xhat, axis=-1, keepdims=True))
            dx_ref[...] = dx
            dxb_ref[...] = dx.astype(bf16)

    row = lambda i, s: (i, 0)
    sl = lambda i, s: (s, i, 0)
    wsl = lambda i, s: (s, 0, 0)
    return pl.pallas_call(
        body, name="mlp_fwd_loss", grid=(T // tm, NSH),
        in_specs=[pl.BlockSpec((tm, D), row), pl.BlockSpec((tm, D), row),
                  _resident((NSH, FS, D)), _resident((NSH, FS, D)), _resident((NSH, FS, D)),
                  pl.BlockSpec((tm, D), row), pl.BlockSpec((1, D), lambda i, s: (0, 0))],
        out_specs=[pl.BlockSpec((8, 128), lambda i, s: (0, 0)), pl.BlockSpec((1, D), lambda i, s: (0, 0)),
                   pl.BlockSpec((tm, D), row), pl.BlockSpec((tm, D), row),
                   pl.BlockSpec((None, tm, FS), sl), pl.BlockSpec((None, tm, FS), sl), pl.BlockSpec((None, tm, FS), sl)],
        out_shape=[jax.ShapeDtypeStruct((8, 128), f32), jax.ShapeDtypeStruct((1, D), f32),
                   jax.ShapeDtypeStruct((T, D), f32), jax.ShapeDtypeStruct((T, D), bf16)]
        + [jax.ShapeDtypeStruct((NSH, T, FS), bf16)] * 3,
        scratch_shapes=[pltpu.VMEM((tm, D), f32)],
        compiler_params=_cp("arbitrary", "arbitrary"),
    )(h2, x1, wg, wu, wd, tgt, gf)


def _mlp_bwd(dx2b, dx2, ud, silu, wg, wu, wd, x1, g2):
    tm = 512

    def body(dxb_ref, dx_ref, ud_ref, silu_ref, wg_ref, wu_ref, wd_ref, x1_ref, g_ref,
             dg_ref, du_ref, dx1_ref, dx1b_ref, dn_ref, acc):
        i = pl.program_id(0)
        s = pl.program_id(1)

        @pl.when(jnp.logical_and(i == 0, s == 0))
        def _():
            dn_ref[...] = jnp.zeros_like(dn_ref)

        dxb = dxb_ref[...]
        das = [lax.dot_general(dxb, wd_ref[s, c0:c1, :], NT, preferred_element_type=f32) for c0, c1 in FS_CHUNKS]
        for (c0, c1), da in zip(FS_CHUNKS, das):
            dg_ref[:, c0:c1] = (da * ud_ref[:, c0:c1].astype(f32)).astype(bf16)
            du_ref[:, c0:c1] = (da * silu_ref[:, c0:c1].astype(f32)).astype(bf16)
        part = jnp.dot(dg_ref[...], wg_ref[s], preferred_element_type=f32)
        part = part + jnp.dot(du_ref[...], wu_ref[s], preferred_element_type=f32)

        @pl.when(s == 0)
        def _():
            acc[...] = part

        @pl.when(s > 0)
        def _():
            acc[...] += part

        @pl.when(s == NSH - 1)
        def _():
            xv = x1_ref[...]
            r = lax.rsqrt(jnp.mean(xv * xv, axis=-1, keepdims=True) + EPS)
            xhat = xv * r
            dh = acc[...]
            dn_ref[...] += jnp.sum(dh * xhat, axis=0, keepdims=True)
            z = dh * g_ref[...]
            dx1 = dx_ref[...] + r * (z - xhat * jnp.mean(z * xhat, axis=-1, keepdims=True))
            dx1_ref[...] = dx1
            dx1b_ref[...] = dx1.astype(bf16)

    row = lambda i, s: (i, 0)
    sl = lambda i, s: (s, i, 0)
    wsl = lambda i, s: (s, 0, 0)
    return pl.pallas_call(
        body, name="mlp_bwd", grid=(T // tm, NSH),
        in_specs=[pl.BlockSpec((tm, D), row), pl.BlockSpec((tm, D), row),
                  pl.BlockSpec((None, tm, FS), sl), pl.BlockSpec((None, tm, FS), sl),
                  _resident((NSH, FS, D)), _resident((NSH, FS, D)), _resident((NSH, FS, D)),
                  pl.BlockSpec((tm, D), row), pl.BlockSpec((1, D), lambda i, s: (0, 0))],
        out_specs=[pl.BlockSpec((None, tm, FS), sl), pl.BlockSpec((None, tm, FS), sl),
                   pl.BlockSpec((tm, D), row), pl.BlockSpec((tm, D), row), pl.BlockSpec((1, D), lambda i, s: (0, 0))],
        out_shape=[jax.ShapeDtypeStruct((NSH, T, FS), bf16)] * 2
        + [jax.ShapeDtypeStruct((T, D), f32), jax.ShapeDtypeStruct((T, D), bf16), jax.ShapeDtypeStruct((1, D), f32)],
        scratch_shapes=[pltpu.VMEM((tm, D), f32)],
        compiler_params=_cp("arbitrary", "arbitrary"),
    )(dx2b, dx2, ud, silu, wg, wu, wd, x1, g2)


def _mm_tn(a, bs, name, a_sharded=False, b_sharded=False, tk=512, out_dtype=bf16):
    nb = len(bs)
    sh = NSH if (a_sharded or b_sharded) else 1
    m = a.shape[-1]
    nk = T // tk

    def body(a_ref, *refs):
        kk = pl.program_id(1)
        av = a_ref[...]
        for b_ref, o_ref, acc in zip(refs[:nb], refs[nb:2 * nb], refs[2 * nb:]):
            upd = lax.dot_general(av, b_ref[...], TN, preferred_element_type=f32)

            @pl.when(kk == 0)
            def _():
                acc[...] = upd

            @pl.when(kk > 0)
            def _():
                acc[...] += upd

            @pl.when(kk == nk - 1)
            def _():
                o_ref[...] = acc[...].astype(out_dtype)

    a_spec = (pl.BlockSpec((None, tk, m), lambda s, k: (s, k, 0)) if a_sharded
              else pl.BlockSpec((tk, m), lambda s, k: (k, 0)))
    b_specs, o_specs, o_shapes, scratch = [], [], [], []
    for b in bs:
        n = b.shape[-1]
        b_specs.append(pl.BlockSpec((None, tk, n), lambda s, k: (s, k, 0)) if b_sharded
                       else pl.BlockSpec((tk, n), lambda s, k: (k, 0)))
        scratch.append(pltpu.VMEM((m, n), f32))
        if sh > 1:
            o_specs.append(pl.BlockSpec((None, m, n), lambda s, k: (s, 0, 0)))
            o_shapes.append(jax.ShapeDtypeStruct((sh, m, n), out_dtype))
        else:
            o_specs.append(pl.BlockSpec((m, n), lambda s, k: (0, 0)))
            o_shapes.append(jax.ShapeDtypeStruct((m, n), out_dtype))
    return pl.pallas_call(
        body, name=name, grid=(sh, nk), in_specs=[a_spec] + b_specs, out_specs=o_specs, out_shape=o_shapes,
        scratch_shapes=scratch, compiler_params=_cp("arbitrary", "arbitrary"),
    )(a, *bs)


def _mm_tn_rows(a_list, b, name, tk=1024, out_dtype=bf16, stacked=False):
    na = len(a_list)
    n = b.shape[-1]
    nk = T // tk
    ms = [a.shape[-1] for a in a_list]
    starts = [sum(ms[:k]) for k in range(na)] if stacked else [0] * na
    out_rows = [sum(ms)] if stacked else ms

    def body(*refs):
        a_refs, b_ref = refs[:na], refs[na]
        no = len(out_rows)
        o_refs, accs = refs[na + 1:na + 1 + no], refs[na + 1 + no:]
        kk = pl.program_id(0)
        bv = b_ref[...]
        for k, a_ref in enumerate(a_refs):
            o_ref, acc = (o_refs[0], accs[0]) if stacked else (o_refs[k], accs[k])
            rows = pl.ds(starts[k], ms[k])
            upd = lax.dot_general(a_ref[...], bv, TN, preferred_element_type=f32)

            @pl.when(kk == 0)
            def _():
                acc[rows, :] = upd

            @pl.when(kk > 0)
            def _():
                acc[rows, :] += upd

            @pl.when(kk == nk - 1)
            def _():
                o_ref[rows, :] = acc[rows, :].astype(out_dtype)

    return pl.pallas_call(
        body, name=name, grid=(nk,),
        in_specs=[pl.BlockSpec((tk, m), lambda k: (k, 0)) for m in ms] + [pl.BlockSpec((tk, n), lambda k: (k, 0))],
        out_specs=[pl.BlockSpec((r, n), lambda k: (0, 0)) for r in out_rows],
        out_shape=[jax.ShapeDtypeStruct((r, n), out_dtype) for r in out_rows],
        scratch_shapes=[pltpu.VMEM((r, n), f32) for r in out_rows],
        compiler_params=_cp("arbitrary"),
    )(*a_list, b)


def _outproj_bwd(dx1b, wo):
    tm = 1024

    def body(dx_ref, wo_ref, da_ref, dp_ref):
        dx = dx_ref[...]
        da_ref[...] = lax.dot_general(dx, wo_ref[0:AW, :], NT, preferred_element_type=f32).astype(bf16)
        dp_ref[...] = lax.dot_general(dx, wo_ref[AW:2 * AW, :], NT, preferred_element_type=f32)

    return pl.pallas_call(
        body, name="outproj_bwd", grid=(T // tm,),
        in_specs=[pl.BlockSpec((tm, D), lambda i: (i, 0)), _full((D, D))],
        out_specs=[pl.BlockSpec((tm, AW), lambda i: (i, 0)), pl.BlockSpec((tm, AW), lambda i: (i, 0))],
        out_shape=[jax.ShapeDtypeStruct((T, AW), bf16), jax.ShapeDtypeStruct((T, AW), f32)],
        compiler_params=_cp("parallel"),
    )(dx1b, wo)


def _pool_bwd(dpool, pooled, wp, scale):
    tm = 1024
    n = T // tm

    def body(dp_ref, pb_ref, wp_ref, sc_ref, du_ref, dsc_ref, dwp_ref, ext):
        i = pl.program_id(0)

        @pl.when(i == 0)
        def _():
            ext[tm:tm + HALO, :] = jnp.zeros((HALO, AW), f32)
            dsc_ref[...] = jnp.zeros_like(dsc_ref)
            dwp_ref[...] = jnp.zeros_like(dwp_ref)

        t_idx = (n - 1 - i) * tm + lax.broadcasted_iota(jnp.int32, (tm, 1), 0)
        for g, w in enumerate(WINDOWS):
            lo, hi = 128 * g, 128 * (g + 1)
            pb = pb_ref[:, lo:hi]
            mixed = jnp.dot(pb, wp_ref[g], preferred_element_type=f32)
            dpo = dp_ref[:, lo:hi]
            dsc_ref[:, lo:hi] += jnp.sum(dpo * mixed, axis=0, keepdims=True)
            dmr = (dpo * sc_ref[:, lo:hi]).astype(bf16)
            dwp_ref[g] += lax.dot_general(pb, dmr, TN, preferred_element_type=f32)
            dpl = lax.dot_general(dmr, wp_ref[g], NT, preferred_element_type=f32)
            cnt = jnp.minimum(t_idx + 1, w).astype(f32)
            dpn = dpl / cnt
            ext[0:tm, lo:hi] = dpn
            acc = dpn
            for d in range(1, w):
                acc = acc + ext[d:d + tm, lo:hi]
            du_ref[:, lo:hi] = (acc - dpl).astype(bf16)
        ext[tm:tm + HALO, :] = ext[0:HALO, :]

    rev = lambda i: (n - 1 - i, 0)
    return pl.pallas_call(
        body, name="pool_bwd", grid=(n,),
        in_specs=[pl.BlockSpec((tm, AW), rev), pl.BlockSpec((tm, AW), rev), _full((4, 128, 128)), _full((1, AW))],
        out_specs=[pl.BlockSpec((tm, AW), rev), _full((1, AW)), _full((4, 128, 128))],
        out_shape=[jax.ShapeDtypeStruct((T, AW), bf16), jax.ShapeDtypeStruct((1, AW), f32),
                   jax.ShapeDtypeStruct((4, 128, 128), f32)],
        scratch_shapes=[pltpu.VMEM((tm + HALO, AW), f32)],
        compiler_params=_cp("arbitrary"),
    )(dpool, pooled, wp, scale)


def _attn_bwd(qkv, qaug, kaug, attn, dattn, lse, dep):
    tq = tk = ATT_T
    n = T // tq
    it, jt = _causal_steps(True)
    nsteps = it.shape[0]

    rs = 64

    def body(it_ref, jt_ref, q_ref, k_ref, v_ref, qa_ref, ka_ref, o_ref, do_ref, lse_ref, dep_ref,
             dq_ref, dqs_ref, dk_ref, dks_ref, dv_ref, dq_acc, dk_acc, dv_acc, s_sc, dp_sc, p_sc, ds_sc):
        t = pl.program_id(1)
        i = it_ref[t]
        j = jt_ref[t]

        @pl.when(t == 0)
        def _():
            dq_acc[...] = jnp.zeros_like(dq_acc)

        @pl.when(i == j)
        def _():
            dk_acc[...] = jnp.zeros_like(dk_acc)
            dv_acc[...] = jnp.zeros_like(dv_acc)

        lane = lax.broadcasted_iota(jnp.int32, (tq, 128), 1)

        def step(on_diagonal):
            q = (q_ref[...].astype(f32) * Q_SCALE).astype(bf16)
            k = k_ref[...]
            v = v_ref[...]
            qa = qa_ref[...]
            ka = ka_ref[...]
            do = do_ref[...]
            dd = do.astype(f32) * o_ref[...].astype(f32)
            blocks = _row_blocks(tq, tk, on_diagonal)
            qes, kes, does, deltas = [], [], [], []
            for e in range(2):
                hm = (lane >= 64) if e else (lane < 64)
                qes.append(jnp.where(hm, q, qa))
                kes.append(jnp.where(hm, k, ka))
                does.append(jnp.where(hm, do, jnp.zeros_like(do)))
                deltas.append(jnp.sum(jnp.where(hm, dd, 0.0), axis=1, keepdims=True))
                for r0, r1, nc in blocks:
                    s_sc[e, r0:r1, 0:nc] = lax.dot_general(qes[e][r0:r1], kes[e][0:nc], NT, preferred_element_type=f32)
                    dp_sc[e, r0:r1, 0:nc] = lax.dot_general(does[e][r0:r1], v[0:nc], NT, preferred_element_type=f32)
            for e in range(2):
                for r0, r1, nc in blocks:
                    for r in range(r0, r1, rs):
                        s = s_sc[e, r:r + rs, 0:nc] - lse_ref[r:r + rs, 64 * e:64 * e + 1]
                        if on_diagonal:
                            row = lax.broadcasted_iota(jnp.int32, (rs, nc), 0) + r
                            col = lax.broadcasted_iota(jnp.int32, (rs, nc), 1)
                            s = jnp.where(col <= row, s, NEG)
                        p = jnp.exp2(s)
                        p_sc[e, r:r + rs, 0:nc] = p.astype(bf16)
                        ds_sc[e, r:r + rs, 0:nc] = (p * (dp_sc[e, r:r + rs, 0:nc] - deltas[e][r:r + rs, :])).astype(bf16)
                for r0, r1, nc in blocks:
                    dv_acc[:, 0:nc] += lax.dot_general(does[e][r0:r1], p_sc[e, r0:r1, 0:nc], TN, preferred_element_type=f32)
                    dsb = ds_sc[e, r0:r1, 0:nc]
                    dk_acc[e, :, 0:nc] += lax.dot_general(qes[e][r0:r1], dsb, TN, preferred_element_type=f32)
                    rq = pl.multiple_of(i * tq + r0, r1 - r0)
                    dq_acc[e, pl.ds(rq, r1 - r0), :] += jnp.dot(dsb, kes[e][0:nc], preferred_element_type=f32)

        @pl.when(i > j)
        def _():
            step(False)

        @pl.when(i == j)
        def _():
            step(True)

        @pl.when(i == n - 1)
        def _():
            dk0 = dk_acc[0].T
            dk1 = dk_acc[1].T
            dk_ref[...] = (jnp.where(lane < 64, dk0, dk1) * (1.0 / LOG2E)).astype(bf16)
            dks_ref[...] = jnp.where(lane < 64, dk1, dk0)
            dv_ref[...] = dv_acc[...].T.astype(bf16)

        @pl.when(t == nsteps - 1)
        def _():
            lane_t = lax.broadcasted_iota(jnp.int32, (T, 128), 1)
            dq_ref[...] = (jnp.where(lane_t < 64, dq_acc[0], dq_acc[1]) * 0.125).astype(bf16)
            dqs_ref[...] = jnp.where(lane_t < 64, dq_acc[1], dq_acc[0])

    qmap = lambda p, t, it, jt: (it[t], p)
    grid_spec = pltpu.PrefetchScalarGridSpec(
        num_scalar_prefetch=2, grid=(PAIRS, nsteps),
        in_specs=[pl.BlockSpec((tq, 128), qmap),
                  pl.BlockSpec((tk, 128), lambda p, t, it, jt: (jt[t], PAIRS + p)),
                  pl.BlockSpec((tk, 128), lambda p, t, it, jt: (jt[t], 2 * PAIRS + p)),
                  pl.BlockSpec((tq, 128), qmap), pl.BlockSpec((tk, 128), lambda p, t, it, jt: (jt[t], p)),
                  pl.BlockSpec((tq, 128), qmap), pl.BlockSpec((tq, 128), qmap),
                  pl.BlockSpec((None, tq, 128), lambda p, t, it, jt: (p, it[t], 0)),
                  pl.BlockSpec((8, 128), lambda p, t, it, jt: (0, 0))],
        out_specs=[pl.BlockSpec((T, 128), lambda p, t, it, jt: (0, p)),
                   pl.BlockSpec((None, T, 128), lambda p, t, it, jt: (p, 0, 0)),
                   pl.BlockSpec((tk, 128), lambda p, t, it, jt: (jt[t], p)),
                   pl.BlockSpec((None, tk, 128), lambda p, t, it, jt: (p, jt[t], 0)),
                   pl.BlockSpec((tk, 128), lambda p, t, it, jt: (jt[t], p))],
        scratch_shapes=[pltpu.VMEM((2, T, 128), f32), pltpu.VMEM((2, 128, tk), f32), pltpu.VMEM((128, tk), f32),
                        pltpu.VMEM((2, tq, tk), f32), pltpu.VMEM((2, tq, tk), f32), pltpu.VMEM((2, tq, tk), bf16),
                        pltpu.VMEM((2, tq, tk), bf16)],
    )
    return pl.pallas_call(
        body, name="fox_attn_bwd", grid_spec=grid_spec,
        out_shape=[jax.ShapeDtypeStruct((T, AW), bf16), jax.ShapeDtypeStruct((PAIRS, T, 128), f32),
                   jax.ShapeDtypeStruct((T, AW), bf16), jax.ShapeDtypeStruct((PAIRS, T, 128), f32),
                   jax.ShapeDtypeStruct((T, AW), bf16)],
        compiler_params=_cp("parallel", "arbitrary"),
    )(it, jt, qkv, qkv, qkv, qaug, kaug, attn, dattn, lse, dep)


def _fox_cumsum_bwd(dqs, dks, fl, bfp):
    tb = CUMSUM_ROWS
    nb = T // tb

    def body(dqs_ref, dks_ref, fl_ref, b_ref, df_ref, db_ref, carry):
        i = pl.program_id(0)

        @pl.when(i == 0)
        def _():
            carry[...] = jnp.zeros_like(carry)
            db_ref[...] = jnp.zeros_like(db_ref)

        r = lax.broadcasted_iota(jnp.int32, (128, 128), 0)
        cc = lax.broadcasted_iota(jnp.int32, (128, 128), 1)
        pick = lambda even_lane, odd_lane, p: jnp.logical_or(
            jnp.logical_and(r == even_lane, cc == 2 * p), jnp.logical_and(r == odd_lane, cc == 2 * p + 1)).astype(bf16)
        dc = jnp.zeros((tb, 128), f32)
        for p in range(PAIRS):
            rows_at = pick(SPARE[0] + ROW_SUM_LANE, SPARE[1] + ROW_SUM_LANE, p)
            cols_at = pick(SPARE[0] + COL_SUM_LANE, SPARE[1] + COL_SUM_LANE, p)
            dc = dc + _dot01(rows_at, dqs_ref[p], False) - _dot01(cols_at, dks_ref[p], False)
        rt = lax.broadcasted_iota(jnp.int32, (tb, tb), 0)
        ct = lax.broadcasted_iota(jnp.int32, (tb, tb), 1)
        utri = (ct >= rt).astype(bf16)
        dl = _dot01(utri, dc, True) + carry[0:1, :]
        carry[...] = jnp.broadcast_to(dl[0:1, :], (8, 128))
        z = fl_ref[...] + b_ref[...]
        df = dl * jax.nn.sigmoid(-z)
        df_ref[...] = df.astype(bf16)
        db_ref[...] += jnp.sum(df, axis=0, keepdims=True)

    rev = lambda i: (nb - 1 - i, 0)
    return pl.pallas_call(
        body, name="fox_cumsum_bwd", grid=(nb,),
        in_specs=[pl.BlockSpec((PAIRS, tb, 128), lambda i: (0, nb - 1 - i, 0)),
                  pl.BlockSpec((PAIRS, tb, 128), lambda i: (0, nb - 1 - i, 0)),
                  pl.BlockSpec((tb, 128), rev), _full((1, 128))],
        out_specs=[pl.BlockSpec((tb, 128), rev), _full((1, 128))],
        out_shape=[jax.ShapeDtypeStruct((T, 128), bf16), jax.ShapeDtypeStruct((1, 128), f32)],
        scratch_shapes=[pltpu.VMEM((8, 128), f32)],
        compiler_params=_cp("arbitrary"),
    )(dqs, dks, fl, bfp)


def _inproj_bwd(dq, dk, dv, du, df, w, x, dx1, g1):
    tm = 512

    def body(dq_ref, dk_ref, dv_ref, du_ref, df_ref, w_ref, x_ref, dx1_ref, g_ref, dx_ref, dn_ref):
        i = pl.program_id(0)

        @pl.when(i == 0)
        def _():
            dn_ref[...] = jnp.zeros_like(dn_ref)

        dproj = jnp.concatenate([dq_ref[...], dk_ref[...], dv_ref[...], du_ref[...], df_ref[...]], axis=1)
        dh = jnp.dot(dproj, w_ref[...], preferred_element_type=f32)
        xv = x_ref[...]
        r = lax.rsqrt(jnp.mean(xv * xv, axis=-1, keepdims=True) + EPS)
        xhat = xv * r
        dn_ref[...] += jnp.sum(dh * xhat, axis=0, keepdims=True)
        z = dh * g_ref[...]
        dx_ref[...] = dx1_ref[...] + r * (z - xhat * jnp.mean(z * xhat, axis=-1, keepdims=True))

    row = lambda i: (i, 0)
    return pl.pallas_call(
        body, name="inproj_bwd", grid=(T // tm,),
        in_specs=[pl.BlockSpec((tm, AW), row)] * 4 + [pl.BlockSpec((tm, 128), row), _full((W_ROWS, D)),
                                                       pl.BlockSpec((tm, D), row), pl.BlockSpec((tm, D), row), _full((1, D))],
        out_specs=[pl.BlockSpec((tm, D), row), _full((1, D))],
        out_shape=[jax.ShapeDtypeStruct((T, D), f32), jax.ShapeDtypeStruct((1, D), f32)],
        compiler_params=_cp("arbitrary"),
    )(dq, dk, dv, du, df, w, x, dx1, g1)


def _adamw_math(w, g, m, v):
    m = B1 * m + (1.0 - B1) * g
    v = B2 * v + (1.0 - B2) * (g * g)
    m_hat = m / (1.0 - B1 ** STEP)
    v_hat = v / (1.0 - B2 ** STEP)
    delta = -LR * (m_hat / (jnp.sqrt(v_hat) + AEPS) + WD * w)
    return delta, m, v


SHARD_STEPS = 4


def _adamw_shards(ws, ms, vs, ps_mine, ps_other, name, steps):
    n = len(ws)

    def body(*refs):
        ins, outs = refs[:5 * n], refs[5 * n:]
        for k in range(n):
            w_ref, m_ref, v_ref, a_ref, b_ref = ins[5 * k:5 * k + 5]
            g_ref, d_ref, nm_ref, nv_ref = outs[4 * k:4 * k + 4]
            g = (a_ref[...].astype(f32) + b_ref[...].astype(f32)).reshape(w_ref.shape)
            g_ref[...] = g
            d_ref[...], nm_ref[...], nv_ref[...] = _adamw_math(w_ref[...], g, m_ref[...], v_ref[...])

    in_specs, out_specs, out_shape = [], [], []
    for w, p in zip(ws, ps_mine):
        rest = tuple(w.shape[1:])
        tr = w.shape[0] // steps
        assert tr * steps == w.shape[0]
        spec = pl.BlockSpec((tr,) + rest, lambda i, _n=len(rest): (i,) + (0,) * _n)
        pspec = pl.BlockSpec((tr, p.shape[1]), lambda i: (i, 0))
        in_specs += [spec] * 3 + [pspec] * 2
        out_specs += [spec] * 4
        out_shape += [jax.ShapeDtypeStruct(w.shape, f32)] * 4
    args = [a for k in range(n) for a in (ws[k], ms[k], vs[k], ps_mine[k], ps_other[k])]
    res = pl.pallas_call(
        body, name=name, grid=(steps,), in_specs=in_specs, out_specs=out_specs, out_shape=out_shape,
        compiler_params=_cp("parallel"),
    )(*args)
    return [res[4 * k:4 * k + 4] for k in range(n)]


SMALL_SLOTS = ((0, 8, 128), (8, 16, 128), (16, 24, 128), (24, 28, 128), (32, 33, 8))
LOSS_ROW = 39


def _adamw_small(ws, ms, vs, parts, parts_wp):
    n = len(ws)

    def body(*refs):
        w_refs, m_refs, v_refs = refs[0:n], refs[n:2 * n], refs[2 * n:3 * n]
        p_ref, pw_ref = refs[3 * n], refs[3 * n + 1]
        outs = refs[3 * n + 2:]
        g_all = p_ref[0]
        g_wp = pw_ref[0]
        for k in range(1, 8):
            g_all = g_all + p_ref[k]
            g_wp = g_wp + pw_ref[k]
        grads = [g_all[r0:r1, 0:lanes] for r0, r1, lanes in SMALL_SLOTS] + [g_wp]
        for idx, g in enumerate(grads):
            d, nm, nv = _adamw_math(w_refs[idx][...], g, m_refs[idx][...], v_refs[idx][...])
            outs[idx][...] = g
            outs[n + idx][...] = d
            outs[2 * n + idx][...] = nm
            outs[3 * n + idx][...] = nv
        outs[4 * n][...] = g_all[LOSS_ROW:LOSS_ROW + 1, :]

    shapes = [jax.ShapeDtypeStruct(w.shape, f32) for w in ws]
    res = pl.pallas_call(
        body, name="adamw_small", out_shape=shapes * 4 + [jax.ShapeDtypeStruct((1, 128), f32)],
    )(*ws, *ms, *vs, parts, parts_wp)
    return res[:4 * n], res[4 * n]


def _sum4(recvs, gs, mine, name, steps):
    n = len(recvs)

    def body(mine_ref, *refs):
        for r_ref, g_ref, o_ref in zip(refs[:n], refs[n:2 * n], refs[2 * n:]):
            o_ref[...] = ((g_ref[...].astype(f32) + r_ref[0].astype(f32))
                          + (r_ref[1].astype(f32) + r_ref[2].astype(f32))).astype(bf16)

    r_specs, g_specs, o_specs, shapes = [], [], [], []
    for recv in recvs:
        _, rows, cols = recv.shape
        tr = rows // steps
        assert tr * steps == rows
        r_specs.append(pl.BlockSpec((3, tr, cols), lambda i, m: (0, i, 0)))
        g_specs.append(pl.BlockSpec((None, tr, cols), lambda i, m: (m[0], i, 0)))
        o_specs.append(pl.BlockSpec((tr, cols), lambda i, m: (i, 0)))
        shapes.append(jax.ShapeDtypeStruct((rows, cols), bf16))
    grid_spec = pltpu.PrefetchScalarGridSpec(num_scalar_prefetch=1, grid=(steps,), in_specs=r_specs + g_specs,
                                             out_specs=o_specs)
    return pl.pallas_call(
        body, name=name, grid_spec=grid_spec, out_shape=shapes, compiler_params=_cp("arbitrary"),
    )(mine, *recvs, *gs)


_HBM = pl.BlockSpec(memory_space=pltpu.HBM)
_SEM = pl.BlockSpec(memory_space=pltpu.SEMAPHORE)
_EFFECT = pltpu.SideEffectType.DATAFLOW_SIDE_EFFECTING


def _in_hbm(a):
    return pltpu.with_memory_space_constraint(a, pltpu.HBM)


def _mesh_pos():
    return lax.axis_index("x"), lax.axis_index("y"), lax.axis_index("c")


def _other_chips(x, y):
    return [(1 - x, y), (x, 1 - y), (1 - x, 1 - y)]


def _gather_copy(srcs, lands, send_sems, recv_sems, a, k, slot):
    x, y, c = _mesh_pos()
    cx, cy = _other_chips(x, y)[k]
    return pltpu.make_async_remote_copy(
        src_ref=srcs[a], dst_ref=lands[a].at[slot], send_sem=send_sems.at[3 * a + k], recv_sem=recv_sems.at[3 * a + k],
        device_id=(cx, cy, c), device_id_type=MESH)


def _scatter_copy(srcs, lands, send_sems, recv_sems, a, k):
    x, y, c = _mesh_pos()
    cx, cy = _other_chips(x, y)[k]
    return pltpu.make_async_remote_copy(
        src_ref=srcs[a].at[2 * cx + cy], dst_ref=lands[a].at[k], send_sem=send_sems.at[3 * a + k],
        recv_sem=recv_sems.at[3 * a + k], device_id=(cx, cy, c), device_id_type=MESH)


def _all_gather_w_in(part):
    cols = part.shape[1] // 2

    def body(src, dst, send_sems, recv_sems, loc_sem):
        x, y, c = _mesh_pos()
        mine = 2 * x + y
        chips = _other_chips(x, y)
        half = lambda ref, cc: ref.at[:, pl.ds(pl.multiple_of(cc * cols, cols), cols)]

        def over_ici(k, slot):
            cx, cy = chips[k]
            return pltpu.make_async_remote_copy(
                src_ref=half(src, c), dst_ref=half(dst.at[slot], c), send_sem=send_sems.at[k], recv_sem=recv_sems.at[k],
                device_id=(cx, cy, c), device_id_type=MESH)

        def to_sibling(k, cc):
            slot = 2 * chips[k][0] + chips[k][1]
            return pltpu.make_async_remote_copy(
                src_ref=half(dst.at[slot], cc), dst_ref=half(dst.at[slot], cc), send_sem=send_sems.at[3 + k],
                recv_sem=recv_sems.at[3 + k], device_id=(x, y, 1 - c), device_id_type=MESH)

        local = pltpu.make_async_copy(src, dst.at[mine], loc_sem.at[0])
        local.start()
        first = [over_ici(k, mine) for k in range(3)]
        for cp in first:
            cp.start()
        passed = [to_sibling(k, c) for k in range(3)]
        for k in range(3):
            over_ici(k, 2 * chips[k][0] + chips[k][1]).wait_recv()
            passed[k].start()
        for k in range(3):
            to_sibling(k, 1 - c).wait_recv()
        for cp in first + passed:
            cp.wait_send()
        local.wait()

    return pl.pallas_call(
        body, name="all_gather_w_in", in_specs=[_HBM], out_specs=_HBM,
        out_shape=jax.ShapeDtypeStruct((NSH,) + part.shape, part.dtype),
        scratch_shapes=[pltpu.SemaphoreType.DMA((6,)), pltpu.SemaphoreType.DMA((6,)), pltpu.SemaphoreType.DMA((1,))],
    )(part)


def _split_start(name, srcs, lands, n_sems, plan, dep):
    n, nl = len(srcs), len(lands)

    def body(*refs):
        src_refs, land_refs = refs[:n], refs[n:n + nl]
        send_sems, recv_sems = refs[n + nl + 1], refs[n + nl + 2]
        token = refs[-1]
        sends, _, own = plan(src_refs, land_refs, send_sems, recv_sems)
        for cp in own + sends:
            cp.start()
        token[...] = jnp.zeros_like(token)

    outs = pl.pallas_call(
        body, name=name,
        in_specs=[_HBM] * (n + nl) + [pl.BlockSpec(memory_space=pl.ANY)],
        out_specs=[_SEM, _SEM] + [_HBM] * (n + nl) + [pl.BlockSpec(memory_space=pltpu.VMEM)],
        out_shape=[pltpu.SemaphoreType.DMA((n_sems,)), pltpu.SemaphoreType.DMA((n_sems,))]
        + [pltpu.HBM(a.shape, a.dtype) for a in list(srcs) + list(lands)] + [jax.ShapeDtypeStruct((8, 128), f32)],
        input_output_aliases={i: 2 + i for i in range(n + nl)},
        compiler_params=pltpu.CompilerParams(has_side_effects=_EFFECT),
    )(*[_in_hbm(a) for a in list(srcs) + list(lands)], dep)
    return outs[0], outs[1], list(outs[2:2 + n]), list(outs[2 + n:2 + n + nl]), outs[-1]


def _split_wait(name, send_sems, recv_sems, srcs, lands, after, plan):
    n, nl = len(srcs), len(lands)

    def body(*refs):
        src_refs, land_refs = refs[:n], refs[n:n + nl]
        s_sems, r_sems = refs[n + nl], refs[n + nl + 1]
        sends, recvs, own = plan(src_refs, land_refs, s_sems, r_sems)
        for cp in own:
            cp.wait()
        for cp in recvs:
            cp.wait_recv()
        for cp in sends:
            cp.wait_send()

    outs = pl.pallas_call(
        body, name=name,
        in_specs=[_HBM] * (n + nl) + [_SEM, _SEM, pl.BlockSpec(memory_space=pl.ANY)],
        out_specs=[_HBM] * (n + nl),
        out_shape=[pltpu.HBM(a.shape, a.dtype) for a in list(srcs) + list(lands)],
        input_output_aliases={i: i for i in range(n + nl)},
        compiler_params=pltpu.CompilerParams(has_side_effects=_EFFECT),
    )(*srcs, *lands, send_sems, recv_sems, after)
    return list(outs[:n]), list(outs[n:])


def _gather_plan(srcs, lands, ss, rs):
    x, y, _ = _mesh_pos()
    chips = _other_chips(x, y)
    sends = [_gather_copy(srcs, lands, ss, rs, a, k, 2 * x + y) for a in range(len(srcs)) for k in range(3)]
    recvs = [_gather_copy(srcs, lands, ss, rs, a, k, 2 * chips[k][0] + chips[k][1])
             for a in range(len(srcs)) for k in range(3)]
    own = [pltpu.make_async_copy(srcs[a], lands[a].at[2 * x + y], rs.at[3 * len(srcs) + a]) for a in range(len(srcs))]
    return sends, recvs, own


def _scatter_and_spread_plan(srcs, lands, ss, rs):
    x, y, c = _mesh_pos()
    me = 4 * x + 2 * y + c
    n = len(srcs) - 1
    cps = [_scatter_copy(srcs[:n], lands[:n], ss, rs, a, k) for a in range(n) for k in range(3)]
    for f in range(1, 8):
        peer = ((x + (f >> 2)) % 2, (y + ((f >> 1) & 1)) % 2, (c + (f & 1)) % 2)
        cps.append(pltpu.make_async_remote_copy(
            src_ref=srcs[n], dst_ref=lands[n].at[me], send_sem=ss.at[3 * n - 1 + f], recv_sem=rs.at[3 * n - 1 + f],
            device_id=peer, device_id_type=MESH))
    own = [pltpu.make_async_copy(srcs[n], lands[n].at[me], rs.at[3 * n + 7])]
    return cps, cps, own


def _swap_with_sibling(parts, name):
    n = len(parts)

    def body(*refs):
        srcs, dsts = refs[:n], refs[n:2 * n]
        send_sems, recv_sems = refs[2 * n:]
        x, y, c = _mesh_pos()
        cps = [pltpu.make_async_remote_copy(src_ref=srcs[a], dst_ref=dsts[a], send_sem=send_sems.at[a],
                                            recv_sem=recv_sems.at[a], device_id=(x, y, 1 - c), device_id_type=MESH)
               for a in range(n)]
        for cp in cps:
            cp.start()
        for cp in cps:
            cp.wait_recv()
        for cp in cps:
            cp.wait_send()

    return pl.pallas_call(
        body, name=name, in_specs=[_HBM] * n, out_specs=[_HBM] * n,
        out_shape=[jax.ShapeDtypeStruct(p.shape, p.dtype) for p in parts],
        scratch_shapes=[pltpu.SemaphoreType.DMA((n,)), pltpu.SemaphoreType.DMA((n,))],
    )(*parts)


def _forward(x, tgt, w_in_t, mlp_w_fn, g1, bfp, wp, scale, g2, gf, dep):
    h, qkv, u, fl = _rms_inproj(x, g1, w_in_t, dep)
    qaug, kaug = _fox_cumsum(fl, bfp)
    attn, lse = _attn_fwd(qkv, qaug, kaug)
    pooled, pool = _pool_fwd(u, wp, scale)
    wo, wgt, wut, wd = mlp_w_fn(attn)
    x1, h2 = _outproj(x, attn, pool, wo, g2)
    loss, dgf, dx2, dx2b, ud, silu, a_b = _mlp_fwd_loss(h2, x1, wgt, wut, wd, tgt, gf)
    saved = dict(h=h, qkv=qkv, fl=fl, qaug=qaug, kaug=kaug, attn=attn, lse=lse, pooled=pooled, pool=pool, x1=x1, h2=h2,
                 ud=ud, silu=silu, a_b=a_b, wo=wo, wgt=wgt, wut=wut, wd=wd)
    return loss, dgf, dx2, dx2b, saved


def _backward_mlp(sv, dx2, dx2b, g2):
    dgate, dup, dx1, dx1b, dg2 = _mlp_bwd(dx2b, dx2, sv["ud"], sv["silu"], sv["wgt"], sv["wut"], sv["wd"], sv["x1"], g2)
    (dwd,) = _mm_tn(sv["a_b"], [dx2b], "dw_down", a_sharded=True, tk=T)
    (dwgt,) = _mm_tn(dgate, [sv["h2"]], "dw_gate", a_sharded=True, tk=T)
    (dwut,) = _mm_tn(dup, [sv["h2"]], "dw_up", a_sharded=True, tk=T)
    return dx1, dx1b, dg2, (dwgt, dwut, dwd)


def _backward_outproj_pool(sv, dx1b, wp, scale):
    dattn, dpool = _outproj_bwd(dx1b, sv["wo"])
    (dwo,) = _mm_tn_rows([sv["attn"], sv["pool"]], dx1b, "dw_out", tk=2048, stacked=True)
    dwo = dwo.reshape(NSH, D // NSH, D)
    du, dscale, dwp = _pool_bwd(dpool, sv["pooled"], wp, scale)
    return dattn, dwo, du, dscale, dwp


def _backward_attn_inproj(sv, x, dx1, dattn, du, w_in_t, g1, bfp, dep):
    dq, dqs, dk, dks, dv = _attn_bwd(sv["qkv"], sv["qaug"], sv["kaug"], sv["attn"], dattn, sv["lse"], dep)
    df, dbf = _fox_cumsum_bwd(dqs, dks, sv["fl"], bfp)
    dx, dg1 = _inproj_bwd(dq, dk, dv, du, df, w_in_t, x, dx1, g1)
    dwq, dwk, dwv, dwu_in, dwf = _mm_tn_rows([dq, dk, dv, du, df], sv["h"], "dw_in")
    dwin = jnp.stack([jnp.concatenate([dwq, dwk[0:2]], axis=0), jnp.concatenate([dwk[2:], dwv[0:4]], axis=0),
                      jnp.concatenate([dwv[4:], dwf[0:6]], axis=0), jnp.concatenate([dwf[6:8], dwu_in], axis=0)])
    return dx, dg1, dbf, dwin


def kernel(x, norm1_g, w_in, b_forget, w_pool, pool_scale, w_out, norm2_g, w_gate, w_up, w_down, final_g, loss_target, m_norm1_g, m_w_in, m_b_forget, m_w_pool, m_pool_scale, m_w_out, m_norm2_g, m_w_gate, m_w_up, m_w_down, m_final_g, v_norm1_g, v_w_in, v_b_forget, v_w_pool, v_pool_scale, v_w_out, v_norm2_g, v_w_gate, v_w_up, v_w_down, v_final_g):
    mine = (2 * lax.axis_index("x") + lax.axis_index("y")).astype(jnp.int32)
    mine1 = mine.reshape(1)
    tr = lambda a: jnp.transpose(a[0])

    win4 = _all_gather_w_in(tr(w_in).astype(bf16))
    later = [w_out[0].astype(bf16), tr(w_gate).astype(bf16), tr(w_up).astype(bf16), w_down[0].astype(bf16)]
    lands = [lax.empty((NSH,) + p.shape, bf16) for p in later]
    ag_send, ag_recv, later_thru, lands_thru, ag_token = _split_start("all_gather_start", later, lands, 16, _gather_plan,
                                                                      win4)
    win = win4.reshape(IN_W, D)
    w_in_t = jnp.concatenate([win[0:3 * AW], win[3 * AW + 8:], win[3 * AW:3 * AW + 8], jnp.zeros((120, D), bf16)], axis=0)
    bfp = jnp.pad(b_forget, ((0, 0), (0, 120)))
    wp = w_pool[0].astype(bf16)
    gf = final_g.reshape(1, D)

    def later_weights(after):
        _, (wo4, wgt, wut, wd) = _split_wait("all_gather_wait", ag_send, ag_recv, later_thru, lands_thru, after, _gather_plan)
        return wo4.reshape(D, D), wgt, wut, wd

    xe, tgt = x[0], loss_target[0]
    loss_v, dgf, dx2, dx2b, sv = _forward(xe, tgt, w_in_t, later_weights, norm1_g, bfp, wp, pool_scale, norm2_g, gf, ag_token)
    dx1, dx1b, dg2, mlp_grads = _backward_mlp(sv, dx2, dx2b, norm2_g)
    dattn, dwo, du, dscale, dwp = _backward_outproj_pool(sv, dx1b, wp, pool_scale)
    first = [dwo] + list(mlp_grads) + [dwp.reshape(512, 128)]
    first_lands = [lax.empty((3,) + g.shape[1:], bf16) for g in first[:4]] + [lax.empty((8, 512, 128), f32)]
    rs_send, rs_recv, first_thru, first_lands_thru, rs_token = _split_start(
        "reduce_scatter_start", first, first_lands, 20, _scatter_and_spread_plan, du)
    dx, dg1, dbf, dwin = _backward_attn_inproj(sv, xe, dx1, dattn, du, w_in_t, norm1_g, bfp, rs_token)

    pad8 = lambda r: jnp.pad(r, ((0, 8 - r.shape[0]), (0, 0)))
    loss_rows = jnp.concatenate([dbf, jnp.zeros((6, 128), f32), loss_v[0:1, :]], axis=0)
    small = jnp.concatenate([dg1.reshape(8, 128), dg2.reshape(8, 128), dgf.reshape(8, 128), pad8(dscale.reshape(4, 128)),
                             loss_rows], axis=0)
    tail_send, tail_recv, tail_thru, tail_lands_thru, tail_token = _split_start(
        "tail_start", [dwin, small], [lax.empty((3,) + dwin.shape[1:], bf16), lax.empty((8, SMALL_ROWS, 128), f32)], 11,
        _scatter_and_spread_plan, dx)
    first_thru, first_recv = _split_wait("reduce_scatter_wait", rs_send, rs_recv, first_thru, first_lands_thru, tail_token,
                                         _scatter_and_spread_plan)
    wp_all = first_recv[4]
    tr3 = lambda a: jnp.transpose(a, (2, 0, 1))
    ws = [tr3(w_in), w_out[0], tr(w_gate), tr(w_up), w_down[0]]
    ms = [tr3(m_w_in), m_w_out[0], tr(m_w_gate), tr(m_w_up), m_w_down[0]]
    vs = [tr3(v_w_in), v_w_out[0], tr(v_w_gate), tr(v_w_up), v_w_down[0]]
    partial = _sum4(first_recv[:4], first_thru[:4], mine1, "sum4_first", SHARD_STEPS)
    other = _swap_with_sibling(partial, "swap_first")
    big = _adamw_shards(ws[1:], ms[1:], vs[1:], partial, other, "adamw_first", SHARD_STEPS)
    (dwin_thru, _), (in_recv_land, small_all) = _split_wait("tail_wait", tail_send, tail_recv, tail_thru, tail_lands_thru,
                                                            big[3][0], _scatter_and_spread_plan)
    partial_in = _sum4([in_recv_land], [dwin_thru], mine1, "sum4_in", 1)
    other_in = _swap_with_sibling(partial_in, "swap_in")
    big = _adamw_shards(ws[:1], ms[:1], vs[:1], partial_in, other_in, "adamw_in", 1) + big

    small_names = ["norm1_g", "norm2_g", "final_g", "pool_scale", "b_forget", "w_pool"]
    rows = lambda a, b, c, d, e, f: [a.reshape(8, 128), b.reshape(8, 128), c.reshape(8, 128), d.reshape(4, 128),
                                     e.reshape(1, 8), f.reshape(512, 128)]
    sm, loss_row = _adamw_small(rows(norm1_g, norm2_g, final_g, pool_scale, b_forget, w_pool),
                                rows(m_norm1_g, m_norm2_g, m_final_g, m_pool_scale, m_b_forget, m_w_pool),
                                rows(v_norm1_g, v_norm2_g, v_final_g, v_pool_scale, v_b_forget, v_w_pool), small_all, wp_all)
    small_shape = dict(norm1_g=(1, D), norm2_g=(1, D), final_g=(D,), pool_scale=(1, AW), b_forget=(1, 8),
                       w_pool=(1, 4, 128, 128))

    order = ["norm1_g", "w_in", "b_forget", "w_pool", "pool_scale", "w_out", "norm2_g", "w_gate", "w_up", "w_down", "final_g"]
    big_idx = {"w_in": 0, "w_out": 1, "w_gate": 2, "w_up": 3, "w_down": 4}
    outs = [loss_row[0, 0], dx[None]]
    for kind in range(4):
        for name in order:
            if name == "w_in":
                outs.append(jnp.transpose(big[0][kind], (1, 2, 0)))
            elif name in ("w_gate", "w_up"):
                outs.append(jnp.transpose(big[big_idx[name]][kind])[None])
            elif name in big_idx:
                outs.append(big[big_idx[name]][kind][None])
            else:
                outs.append(sm[6 * kind + small_names.index(name)].reshape(small_shape[name]))
    return tuple(outs)
```

```python
import jax
import jax.numpy as jnp
import numpy as np
from jax import lax
from jax.experimental import pallas as pl
from jax.experimental.pallas import tpu as pltpu

f32 = jnp.float32
bf16 = jnp.bfloat16

T = 4096
D = 1024
NSH = 4
IN_W = 2056
IN_S = IN_W // NSH
AW = 512
PAIRS = 4
SPARE = (64, 0)
ROW_SUM_LANE, COL_SUM_LANE = 0, 3
FF = 2816
FS = FF // NSH
WINDOWS = (2, 4, 8, 16)
HALO = 16
EPS = 1e-6
NEG = -1e30
LR, B1, B2, AEPS, WD, STEP = 0.001, 0.9, 0.999, 1e-08, 0.01, 10
SMALL_ROWS = 40

NT = (((1,), (1,)), ((), ()))
TN = (((0,), (0,)), ((), ()))

MESH = pl.DeviceIdType.MESH


def _cp(*sem):
    return pltpu.CompilerParams(dimension_semantics=sem)


def _full(shape):
    n = len(shape)
    return pl.BlockSpec(shape, lambda *_: (0,) * n)


def _resident(shape):
    n = len(shape)
    return pl.BlockSpec(shape, lambda *_: (0,) * n, pipeline_mode=pl.Buffered(1))


W_ROWS = 4 * AW + 128


def _rms_inproj(x, g1, w, dep):
    tm = 512

    def body(x_ref, g_ref, w_ref, dep_ref, h_ref, qkv_ref, u_ref, fl_ref):
        xv = x_ref[...]
        r = lax.rsqrt(jnp.mean(xv * xv, axis=-1, keepdims=True) + EPS)
        h = (xv * r * g_ref[...]).astype(bf16)
        h_ref[...] = h
        qkv_ref[...] = lax.dot_general(h, w_ref[0:3 * AW, :], NT, preferred_element_type=f32).astype(bf16)
        u_ref[...] = lax.dot_general(h, w_ref[3 * AW:4 * AW, :], NT, preferred_element_type=f32)
        fl_ref[...] = lax.dot_general(h, w_ref[4 * AW:W_ROWS, :], NT, preferred_element_type=f32)

    return pl.pallas_call(
        body, name="rms_inproj", grid=(T // tm,),
        in_specs=[pl.BlockSpec((tm, D), lambda i: (i, 0)), _full((1, D)), _full((W_ROWS, D)), _full((8, 128))],
        out_specs=[pl.BlockSpec((tm, D), lambda i: (i, 0)), pl.BlockSpec((tm, 3 * AW), lambda i: (i, 0)),
                   pl.BlockSpec((tm, AW), lambda i: (i, 0)), pl.BlockSpec((tm, 128), lambda i: (i, 0))],
        out_shape=[jax.ShapeDtypeStruct((T, D), bf16), jax.ShapeDtypeStruct((T, 3 * AW), bf16),
                   jax.ShapeDtypeStruct((T, AW), f32), jax.ShapeDtypeStruct((T, 128), f32)],
        compiler_params=_cp("parallel"),
    )(x, g1, w, dep)


CUMSUM_ROWS = 512
FS_CHUNKS = ((0, 256), (256, 512), (512, FS))


def _log_sigmoid(z):
    return jnp.minimum(z, 0.0) - jnp.log(1.0 + jnp.exp(-jnp.abs(z)))


def _split3(x):
    hi = x.astype(bf16)
    r1 = x - hi.astype(f32)
    mid = r1.astype(bf16)
    return hi, mid, (r1 - mid.astype(f32)).astype(bf16)


def _dot01(sel, x, sel_first):
    parts = _split3(x)
    if sel_first:
        return sum(jnp.dot(sel, p, preferred_element_type=f32) for p in parts)
    return sum(jnp.dot(p, sel, preferred_element_type=f32) for p in parts)


def _fox_cumsum(fl, bfp):
    tb = CUMSUM_ROWS
    nb = T // tb

    def body(fl_ref, b_ref, qa_ref, ka_ref, carry):
        i = pl.program_id(0)

        @pl.when(i == 0)
        def _():
            carry[...] = jnp.zeros_like(carry)

        lf = _log_sigmoid(fl_ref[...] + b_ref[...])
        r = lax.broadcasted_iota(jnp.int32, (tb, tb), 0)
        cc = lax.broadcasted_iota(jnp.int32, (tb, tb), 1)
        ltri = (cc <= r).astype(bf16)
        cb = _dot01(ltri, lf, True) + carry[0:1, :]
        carry[...] = jnp.broadcast_to(cb[tb - 1:tb, :], (8, 128))
        hi, mid, lo = _split3(cb * LOG2E)
        lane = lax.broadcasted_iota(jnp.int32, (tb, 128), 1)
        terms = jnp.where(lane < 8, hi.astype(f32),
                          jnp.where(lane < 16, pltpu.roll(mid.astype(f32), 8, 1), pltpu.roll(lo.astype(f32), 16, 1)))
        terms = terms.astype(bf16)
        src = lax.broadcasted_iota(jnp.int32, (128, AW), 0)
        col = lax.broadcasted_iota(jnp.int32, (128, AW), 1)
        head, term = src & 7, src >> 3
        base = 128 * (head >> 1) + jnp.where((head & 1) == 0, SPARE[0], SPARE[1])
        place = lambda off: jnp.logical_and(col == base + off + term, term < 3).astype(bf16)
        cq = jnp.dot(terms, place(0), preferred_element_type=f32)
        ck = jnp.dot(terms, place(3), preferred_element_type=f32)
        within = jnp.bitwise_and(lax.broadcasted_iota(jnp.int32, (tb, AW), 1), 63)
        qa_ref[...] = jnp.where(jnp.logical_and(within >= 3, within <= 5), 1.0, cq).astype(bf16)
        ka_ref[...] = jnp.where(within <= 2, 1.0, -ck).astype(bf16)

    return pl.pallas_call(
        body, name="fox_cumsum", grid=(nb,),
        in_specs=[pl.BlockSpec((tb, 128), lambda i: (i, 0)), _full((1, 128))],
        out_specs=[pl.BlockSpec((tb, AW), lambda i: (i, 0)), pl.BlockSpec((tb, AW), lambda i: (i, 0))],
        out_shape=[jax.ShapeDtypeStruct((T, AW), bf16), jax.ShapeDtypeStruct((T, AW), bf16)],
        scratch_shapes=[pltpu.VMEM((8, 128), f32)],
        compiler_params=_cp("arbitrary"),
    )(fl, bfp)


ATT_T = 512
LOG2E = 1.4426950408889634
Q_SCALE = 0.125 * LOG2E


def _causal_steps(key_major):
    n = T // ATT_T
    if key_major:
        pairs = [(i, j) for j in range(n) for i in range(j, n)]
    else:
        pairs = [(i, j) for i in range(n) for j in range(i + 1)]
    it = np.array([p[0] for p in pairs], np.int32)
    jt = np.array([p[1] for p in pairs], np.int32)
    return jnp.asarray(it), jnp.asarray(jt)


def _row_blocks(tq, tk, on_diagonal):
    return ((0, tq // 2, tk // 2), (tq // 2, tq, tk)) if on_diagonal else ((0, tq, tk),)


def _attn_fwd(qkv, qaug, kaug):
    tq = tk = ATT_T
    it, jt = _causal_steps(False)
    nsteps = it.shape[0]

    rs = 64

    def body(it_ref, jt_ref, q_ref, k_ref, v_ref, qa_ref, ka_ref, o_ref, lse_ref, m_sc, acc_sc, s_sc, p_sc, alpha_sc):
        t = pl.program_id(1)
        i = it_ref[t]
        j = jt_ref[t]

        @pl.when(j == 0)
        def _():
            m_sc[...] = jnp.full_like(m_sc, NEG)
            acc_sc[...] = jnp.zeros_like(acc_sc)

        lane = lax.broadcasted_iota(jnp.int32, (tq, 128), 1)
        spare = SPARE

        def step(on_diagonal):
            q = (q_ref[...].astype(f32) * Q_SCALE).astype(bf16)
            k = k_ref[...]
            v = v_ref[...]
            qa = qa_ref[...]
            ka = ka_ref[...]
            blocks = _row_blocks(tq, tk, on_diagonal)
            for e in range(2):
                hm = (lane >= 64) if e else (lane < 64)
                qe = jnp.where(hm, q, qa)
                ke = jnp.where(hm, k, ka)
                for r0, r1, nc in blocks:
                    s_sc[e, r0:r1, 0:nc] = lax.dot_general(qe[r0:r1], ke[0:nc], NT, preferred_element_type=f32)
            for e in range(2):
                for r0, r1, nc in blocks:
                    for r in range(r0, r1, rs):
                        s = s_sc[e, r:r + rs, 0:nc]
                        if on_diagonal:
                            row = lax.broadcasted_iota(jnp.int32, (rs, nc), 0) + r
                            col = lax.broadcasted_iota(jnp.int32, (rs, nc), 1)
                            s = jnp.where(col <= row, s, NEG)
                        m_prev = m_sc[e, r:r + rs, :]
                        m_new = jnp.maximum(m_prev, jnp.max(s, axis=1, keepdims=True))
                        p_sc[e, r:r + rs, 0:nc] = jnp.exp2(s - jnp.tile(m_new, (1, nc // 128))).astype(bf16)
                        alpha_sc[e, r:r + rs, :] = jnp.exp2(m_prev - m_new)
                        m_sc[e, r:r + rs, :] = m_new
            for e in range(2):
                hm = (lane >= 64) if e else (lane < 64)
                ve = jnp.where(hm, v, (lane == spare[e]).astype(bf16))
                for r0, r1, nc in blocks:
                    acc_sc[e, r0:r1] = (alpha_sc[e, r0:r1] * acc_sc[e, r0:r1]
                                        + jnp.dot(p_sc[e, r0:r1, 0:nc], ve[0:nc], preferred_element_type=f32))

        @pl.when(j < i)
        def _():
            step(False)

        @pl.when(j == i)
        def _():
            step(True)
            l0 = acc_sc[0][:, spare[0]:spare[0] + 1]
            l1 = acc_sc[1][:, spare[1]:spare[1] + 1]
            o_ref[...] = jnp.where(lane < 64, acc_sc[0] / l0, acc_sc[1] / l1).astype(bf16)
            lse_ref[...] = jnp.where(lane < 64, m_sc[0] + jnp.log2(l0), m_sc[1] + jnp.log2(l1))

    qmap = lambda p, t, it, jt: (it[t], p)
    kmap = lambda p, t, it, jt: (jt[t], p)
    grid_spec = pltpu.PrefetchScalarGridSpec(
        num_scalar_prefetch=2, grid=(PAIRS, nsteps),
        in_specs=[pl.BlockSpec((tq, 128), qmap),
                  pl.BlockSpec((tk, 128), lambda p, t, it, jt: (jt[t], PAIRS + p)),
                  pl.BlockSpec((tk, 128), lambda p, t, it, jt: (jt[t], 2 * PAIRS + p)),
                  pl.BlockSpec((tq, 128), qmap), pl.BlockSpec((tk, 128), kmap)],
        out_specs=[pl.BlockSpec((tq, 128), qmap),
                   pl.BlockSpec((None, tq, 128), lambda p, t, it, jt: (p, it[t], 0))],
        scratch_shapes=[pltpu.VMEM((2, tq, 128), f32), pltpu.VMEM((2, tq, 128), f32), pltpu.VMEM((2, tq, tk), f32),
                        pltpu.VMEM((2, tq, tk), bf16), pltpu.VMEM((2, tq, 128), f32)],
    )
    return pl.pallas_call(
        body, name="fox_attn_fwd", grid_spec=grid_spec,
        out_shape=[jax.ShapeDtypeStruct((T, AW), bf16), jax.ShapeDtypeStruct((PAIRS, T, 128), f32)],
        compiler_params=_cp("parallel", "arbitrary"),
    )(it, jt, qkv, qkv, qkv, qaug, kaug)


def _pool_fwd(u, wp, scale):
    tm = 1024

    def body(u_ref, wp_ref, sc_ref, pooled_ref, pool_ref, ext):
        i = pl.program_id(0)

        @pl.when(i == 0)
        def _():
            ext[0:HALO, :] = jnp.zeros((HALO, AW), f32)

        uv = u_ref[...]
        ext[HALO:HALO + tm, :] = uv
        t_idx = i * tm + lax.broadcasted_iota(jnp.int32, (tm, 1), 0)
        for g, w in enumerate(WINDOWS):
            lo, hi = 128 * g, 128 * (g + 1)
            ug = uv[:, lo:hi]
            acc = ug
            for d in range(1, w):
                acc = acc + ext[HALO - d:HALO - d + tm, lo:hi]
            cnt = jnp.minimum(t_idx + 1, w).astype(f32)
            pb = (acc / cnt - ug).astype(bf16)
            pooled_ref[:, lo:hi] = pb
            mixed = jnp.dot(pb, wp_ref[g], preferred_element_type=f32)
            pool_ref[:, lo:hi] = (mixed * sc_ref[:, lo:hi]).astype(bf16)
        ext[0:HALO, :] = uv[tm - HALO:tm, :]

    return pl.pallas_call(
        body, name="pool_fwd", grid=(T // tm,),
        in_specs=[pl.BlockSpec((tm, AW), lambda i: (i, 0)), _full((4, 128, 128)), _full((1, AW))],
        out_specs=[pl.BlockSpec((tm, AW), lambda i: (i, 0)), pl.BlockSpec((tm, AW), lambda i: (i, 0))],
        out_shape=[jax.ShapeDtypeStruct((T, AW), bf16), jax.ShapeDtypeStruct((T, AW), bf16)],
        scratch_shapes=[pltpu.VMEM((tm + HALO, AW), f32)],
        compiler_params=_cp("arbitrary"),
    )(u, wp, scale)


def _outproj(x, attn, pool, wo, g2):
    tm = 1024

    def body(x_ref, a_ref, p_ref, wo_ref, g_ref, x1_ref, h2_ref):
        mixed = jnp.concatenate([a_ref[...], p_ref[...]], axis=1)
        x1 = x_ref[...] + jnp.dot(mixed, wo_ref[...], preferred_element_type=f32)
        x1_ref[...] = x1
        r = lax.rsqrt(jnp.mean(x1 * x1, axis=-1, keepdims=True) + EPS)
        h2_ref[...] = (x1 * r * g_ref[...]).astype(bf16)

    return pl.pallas_call(
        body, name="outproj", grid=(T // tm,),
        in_specs=[pl.BlockSpec((tm, D), lambda i: (i, 0)), pl.BlockSpec((tm, AW), lambda i: (i, 0)),
                  pl.BlockSpec((tm, AW), lambda i: (i, 0)), _full((D, D)), _full((1, D))],
        out_specs=[pl.BlockSpec((tm, D), lambda i: (i, 0)), pl.BlockSpec((tm, D), lambda i: (i, 0))],
        out_shape=[jax.ShapeDtypeStruct((T, D), f32), jax.ShapeDtypeStruct((T, D), bf16)],
        compiler_params=_cp("parallel"),
    )(x, attn, pool, wo, g2)


def _mlp_fwd_loss(h2, x1, wg, wu, wd, tgt, gf):
    tm = 512

    def body(h_ref, x1_ref, wg_ref, wu_ref, wd_ref, t_ref, g_ref,
             loss_ref, dg_ref, dx_ref, dxb_ref, ud_ref, silu_ref, a_ref, x2):
        i = pl.program_id(0)
        s = pl.program_id(1)

        @pl.when(jnp.logical_and(i == 0, s == 0))
        def _():
            loss_ref[...] = jnp.zeros_like(loss_ref)
            dg_ref[...] = jnp.zeros_like(dg_ref)

        h = h_ref[...]
        gus = [(lax.dot_general(h, wg_ref[s, c0:c1, :], NT, preferred_element_type=f32),
                lax.dot_general(h, wu_ref[s, c0:c1, :], NT, preferred_element_type=f32)) for c0, c1 in FS_CHUNKS]
        for (c0, c1), (gate, up) in zip(FS_CHUNKS, gus):
            sg = jax.nn.sigmoid(gate)
            silu = gate * sg
            ud_ref[:, c0:c1] = (up * (sg * (1.0 + gate * (1.0 - sg)))).astype(bf16)
            silu_ref[:, c0:c1] = silu.astype(bf16)
            a_ref[:, c0:c1] = (silu * up).astype(bf16)
        part = jnp.dot(a_ref[...], wd_ref[s], preferred_element_type=f32)

        @pl.when(s == 0)
        def _():
            x2[...] = x1_ref[...] + part

        @pl.when(s > 0)
        def _():
            x2[...] += part

        @pl.when(s == NSH - 1)
        def _():
            xv = x2[...]
            g = g_ref[...]
            r = lax.rsqrt(jnp.mean(xv * xv, axis=-1, keepdims=True) + EPS)
            xhat = xv * r
            e = xhat * g - t_ref[...]
            loss_ref[...] += 0.5 * jnp.sum(jnp.mean(e * e, axis=-1, keepdims=True))
            dy = e * (1.0 / D)
            dg_ref[...] += jnp.sum(dy * xhat, axis=0, keepdims=True)
            z = dy * g
            dx = r * (z - xhat * jnp.mean(z * xhat, axis=-1, keepdims=True))
            dx_ref[...] = dx
            dxb_ref[...] = dx.astype(bf16)

    row = lambda i, s: (i, 0)
    sl = lambda i, s: (s, i, 0)
    wsl = lambda i, s: (s, 0, 0)
    return pl.pallas_call(
        body, name="mlp_fwd_loss", grid=(T // tm, NSH),
        in_specs=[pl.BlockSpec((tm, D), row), pl.BlockSpec((tm, D), row),
                  _resident((NSH, FS, D)), _resident((NSH, FS, D)), _resident((NSH, FS, D)),
                  pl.BlockSpec((tm, D), row), pl.BlockSpec((1, D), lambda i, s: (0, 0))],
        out_specs=[pl.BlockSpec((8, 128), lambda i, s: (0, 0)), pl.BlockSpec((1, D), lambda i, s: (0, 0)),
                   pl.BlockSpec((tm, D), row), pl.BlockSpec((tm, D), row),
                   pl.BlockSpec((None, tm, FS), sl), pl.BlockSpec((None, tm, FS), sl), pl.BlockSpec((None, tm, FS), sl)],
        out_shape=[jax.ShapeDtypeStruct((8, 128), f32), jax.ShapeDtypeStruct((1, D), f32),
                   jax.ShapeDtypeStruct((T, D), f32), jax.ShapeDtypeStruct((T, D), bf16)]
        + [jax.ShapeDtypeStruct((NSH, T, FS), bf16)] * 3,
        scratch_shapes=[pltpu.VMEM((tm, D), f32)],
        compiler_params=_cp("arbitrary", "arbitrary"),
    )(h2, x1, wg, wu, wd, tgt, gf)


def _mlp_bwd(dx2b, dx2, ud, silu, wg, wu, wd, x1, g2):
    tm = 512

    def body(dxb_ref, dx_ref, ud_ref, silu_ref, wg_ref, wu_ref, wd_ref, x1_ref, g_ref,
             dg_ref, du_ref, dx1_ref, dx1b_ref, dn_ref, acc):
        i = pl.program_id(0)
        s = pl.program_id(1)

        @pl.when(jnp.logical_and(i == 0, s == 0))
        def _():
            dn_ref[...] = jnp.zeros_like(dn_ref)

        dxb = dxb_ref[...]
        das = [lax.dot_general(dxb, wd_ref[s, c0:c1, :], NT, preferred_element_type=f32) for c0, c1 in FS_CHUNKS]
        for (c0, c1), da in zip(FS_CHUNKS, das):
            dg_ref[:, c0:c1] = (da * ud_ref[:, c0:c1].astype(f32)).astype(bf16)
            du_ref[:, c0:c1] = (da * silu_ref[:, c0:c1].astype(f32)).astype(bf16)
        part = jnp.dot(dg_ref[...], wg_ref[s], preferred_element_type=f32)
        part = part + jnp.dot(du_ref[...], wu_ref[s], preferred_element_type=f32)

        @pl.when(s == 0)
        def _():
            acc[...] = part

        @pl.when(s > 0)
        def _():
            acc[...] += part

        @pl.when(s == NSH - 1)
        def _():
            xv = x1_ref[...]
            r = lax.rsqrt(jnp.mean(xv * xv, axis=-1, keepdims=True) + EPS)
            xhat = xv * r
            dh = acc[...]
            dn_ref[...] += jnp.sum(dh * xhat, axis=0, keepdims=True)
            z = dh * g_ref[...]
            dx1 = dx_ref[...] + r * (z - xhat * jnp.mean(z * xhat, axis=-1, keepdims=True))
            dx1_ref[...] = dx1
            dx1b_ref[...] = dx1.astype(bf16)

    row = lambda i, s: (i, 0)
    sl = lambda i, s: (s, i, 0)
    wsl = lambda i, s: (s, 0, 0)
    return pl.pallas_call(
        body, name="mlp_bwd", grid=(T // tm, NSH),
        in_specs=[pl.BlockSpec((tm, D), row), pl.BlockSpec((tm, D), row),
                  pl.BlockSpec((None, tm, FS), sl), pl.BlockSpec((None, tm, FS), sl),
                  _resident((NSH, FS, D)), _resident((NSH, FS, D)), _resident((NSH, FS, D)),
                  pl.BlockSpec((tm, D), row), pl.BlockSpec((1, D), lambda i, s: (0, 0))],
        out_specs=[pl.BlockSpec((None, tm, FS), sl), pl.BlockSpec((None, tm, FS), sl),
                   pl.BlockSpec((tm, D), row), pl.BlockSpec((tm, D), row), pl.BlockSpec((1, D), lambda i, s: (0, 0))],
        out_shape=[jax.ShapeDtypeStruct((NSH, T, FS), bf16)] * 2
        + [jax.ShapeDtypeStruct((T, D), f32), jax.ShapeDtypeStruct((T, D), bf16), jax.ShapeDtypeStruct((1, D), f32)],
        scratch_shapes=[pltpu.VMEM((tm, D), f32)],
        compiler_params=_cp("arbitrary", "arbitrary"),
    )(dx2b, dx2, ud, silu, wg, wu, wd, x1, g2)


def _mm_tn(a, bs, name, a_sharded=False, b_sharded=False, tk=512, out_dtype=bf16):
    nb = len(bs)
    sh = NSH if (a_sharded or b_sharded) else 1
    m = a.shape[-1]
    nk = T // tk

    def body(a_ref, *refs):
        kk = pl.program_id(1)
        av = a_ref[...]
        for b_ref, o_ref, acc in zip(refs[:nb], refs[nb:2 * nb], refs[2 * nb:]):
            upd = lax.dot_general(av, b_ref[...], TN, preferred_element_type=f32)

            @pl.when(kk == 0)
            def _():
                acc[...] = upd

            @pl.when(kk > 0)
            def _():
                acc[...] += upd

            @pl.when(kk == nk - 1)
            def _():
                o_ref[...] = acc[...].astype(out_dtype)

    a_spec = (pl.BlockSpec((None, tk, m), lambda s, k: (s, k, 0)) if a_sharded
              else pl.BlockSpec((tk, m), lambda s, k: (k, 0)))
    b_specs, o_specs, o_shapes, scratch = [], [], [], []
    for b in bs:
        n = b.shape[-1]
        b_specs.append(pl.BlockSpec((None, tk, n), lambda s, k: (s, k, 0)) if b_sharded
                       else pl.BlockSpec((tk, n), lambda s, k: (k, 0)))
        scratch.append(pltpu.VMEM((m, n), f32))
        if sh > 1:
            o_specs.append(pl.BlockSpec((None, m, n), lambda s, k: (s, 0, 0)))
            o_shapes.append(jax.ShapeDtypeStruct((sh, m, n), out_dtype))
        else:
            o_specs.append(pl.BlockSpec((m, n), lambda s, k: (0, 0)))
            o_shapes.append(jax.ShapeDtypeStruct((m, n), out_dtype))
    return pl.pallas_call(
        body, name=name, grid=(sh, nk), in_specs=[a_spec] + b_specs, out_specs=o_specs, out_shape=o_shapes,
        scratch_shapes=scratch, compiler_params=_cp("arbitrary", "arbitrary"),
    )(a, *bs)


def _mm_tn_rows(a_list, b, name, tk=1024, out_dtype=bf16, stacked=False):
    na = len(a_list)
    n = b.shape[-1]
    nk = T // tk
    ms = [a.shape[-1] for a in a_list]
    starts = [sum(ms[:k]) for k in range(na)] if stacked else [0] * na
    out_rows = [sum(ms)] if stacked else ms

    def body(*refs):
        a_refs, b_ref = refs[:na], refs[na]
        no = len(out_rows)
        o_refs, accs = refs[na + 1:na + 1 + no], refs[na + 1 + no:]
        kk = pl.program_id(0)
        bv = b_ref[...]
        for k, a_ref in enumerate(a_refs):
            o_ref, acc = (o_refs[0], accs[0]) if stacked else (o_refs[k], accs[k])
            rows = pl.ds(starts[k], ms[k])
            upd = lax.dot_general(a_ref[...], bv, TN, preferred_element_type=f32)

            @pl.when(kk == 0)
            def _():
                acc[rows, :] = upd

            @pl.when(kk > 0)
            def _():
                acc[rows, :] += upd

            @pl.when(kk == nk - 1)
            def _():
                o_ref[rows, :] = acc[rows, :].astype(out_dtype)

    return pl.pallas_call(
        body, name=name, grid=(nk,),
        in_specs=[pl.BlockSpec((tk, m), lambda k: (k, 0)) for m in ms] + [pl.BlockSpec((tk, n), lambda k: (k, 0))],
        out_specs=[pl.BlockSpec((r, n), lambda k: (0, 0)) for r in out_rows],
        out_shape=[jax.ShapeDtypeStruct((r, n), out_dtype) for r in out_rows],
        scratch_shapes=[pltpu.VMEM((r, n), f32) for r in out_rows],
        compiler_params=_cp("arbitrary"),
    )(*a_list, b)


def _outproj_bwd(dx1b, wo):
    tm = 1024

    def body(dx_ref, wo_ref, da_ref, dp_ref):
        dx = dx_ref[...]
        da_ref[...] = lax.dot_general(dx, wo_ref[0:AW, :], NT, preferred_element_type=f32).astype(bf16)
        dp_ref[...] = lax.dot_general(dx, wo_ref[AW:2 * AW, :], NT, preferred_element_type=f32)

    return pl.pallas_call(
        body, name="outproj_bwd", grid=(T // tm,),
        in_specs=[pl.BlockSpec((tm, D), lambda i: (i, 0)), _full((D, D))],
        out_specs=[pl.BlockSpec((tm, AW), lambda i: (i, 0)), pl.BlockSpec((tm, AW), lambda i: (i, 0))],
        out_shape=[jax.ShapeDtypeStruct((T, AW), bf16), jax.ShapeDtypeStruct((T, AW), f32)],
        compiler_params=_cp("parallel"),
    )(dx1b, wo)


def _pool_bwd(dpool, pooled, wp, scale):
    tm = 1024
    n = T // tm

    def body(dp_ref, pb_ref, wp_ref, sc_ref, du_ref, dsc_ref, dwp_ref, ext):
        i = pl.program_id(0)

        @pl.when(i == 0)
        def _():
            ext[tm:tm + HALO, :] = jnp.zeros((HALO, AW), f32)
            dsc_ref[...] = jnp.zeros_like(dsc_ref)
            dwp_ref[...] = jnp.zeros_like(dwp_ref)

        t_idx = (n - 1 - i) * tm + lax.broadcasted_iota(jnp.int32, (tm, 1), 0)
        for g, w in enumerate(WINDOWS):
            lo, hi = 128 * g, 128 * (g + 1)
            pb = pb_ref[:, lo:hi]
            mixed = jnp.dot(pb, wp_ref[g], preferred_element_type=f32)
            dpo = dp_ref[:, lo:hi]
            dsc_ref[:, lo:hi] += jnp.sum(dpo * mixed, axis=0, keepdims=True)
            dmr = (dpo * sc_ref[:, lo:hi]).astype(bf16)
            dwp_ref[g] += lax.dot_general(pb, dmr, TN, preferred_element_type=f32)
            dpl = lax.dot_general(dmr, wp_ref[g], NT, preferred_element_type=f32)
            cnt = jnp.minimum(t_idx + 1, w).astype(f32)
            dpn = dpl / cnt
            ext[0:tm, lo:hi] = dpn
            acc = dpn
            for d in range(1, w):
                acc = acc + ext[d:d + tm, lo:hi]
            du_ref[:, lo:hi] = (acc - dpl).astype(bf16)
        ext[tm:tm + HALO, :] = ext[0:HALO, :]

    rev = lambda i: (n - 1 - i, 0)
    return pl.pallas_call(
        body, name="pool_bwd", grid=(n,),
        in_specs=[pl.BlockSpec((tm, AW), rev), pl.BlockSpec((tm, AW), rev), _full((4, 128, 128)), _full((1, AW))],
        out_specs=[pl.BlockSpec((tm, AW), rev), _full((1, AW)), _full((4, 128, 128))],
        out_shape=[jax.ShapeDtypeStruct((T, AW), bf16), jax.ShapeDtypeStruct((1, AW), f32),
                   jax.ShapeDtypeStruct((4, 128, 128), f32)],
        scratch_shapes=[pltpu.VMEM((tm + HALO, AW), f32)],
        compiler_params=_cp("arbitrary"),
    )(dpool, pooled, wp, scale)


def _attn_bwd(qkv, qaug, kaug, attn, dattn, lse, dep):
    tq = tk = ATT_T
    n = T // tq
    it, jt = _causal_steps(True)
    nsteps = it.shape[0]

    rs = 64

    def body(it_ref, jt_ref, q_ref, k_ref, v_ref, qa_ref, ka_ref, o_ref, do_ref, lse_ref, dep_ref,
             dq_ref, dqs_ref, dk_ref, dks_ref, dv_ref, dq_acc, dk_acc, dv_acc, s_sc, dp_sc, p_sc, ds_sc):
        t = pl.program_id(1)
        i = it_ref[t]
        j = jt_ref[t]

        @pl.when(t == 0)
        def _():
            dq_acc[...] = jnp.zeros_like(dq_acc)

        @pl.when(i == j)
        def _():
            dk_acc[...] = jnp.zeros_like(dk_acc)
            dv_acc[...] = jnp.zeros_like(dv_acc)

        lane = lax.broadcasted_iota(jnp.int32, (tq, 128), 1)

        def step(on_diagonal):
            q = (q_ref[...].astype(f32) * Q_SCALE).astype(bf16)
            k = k_ref[...]
            v = v_ref[...]
            qa = qa_ref[...]
            ka = ka_ref[...]
            do = do_ref[...]
            dd = do.astype(f32) * o_ref[...].astype(f32)
            blocks = _row_blocks(tq, tk, on_diagonal)
            qes, kes, does, deltas = [], [], [], []
            for e in range(2):
                hm = (lane >= 64) if e else (lane < 64)
                qes.append(jnp.where(hm, q, qa))
                kes.append(jnp.where(hm, k, ka))
                does.append(jnp.where(hm, do, jnp.zeros_like(do)))
                deltas.append(jnp.sum(jnp.where(hm, dd, 0.0), axis=1, keepdims=True))
                for r0, r1, nc in blocks:
                    s_sc[e, r0:r1, 0:nc] = lax.dot_general(qes[e][r0:r1], kes[e][0:nc], NT, preferred_element_type=f32)
                    dp_sc[e, r0:r1, 0:nc] = lax.dot_general(does[e][r0:r1], v[0:nc], NT, preferred_element_type=f32)
            for e in range(2):
                for r0, r1, nc in blocks:
                    for r in range(r0, r1, rs):
                        s = s_sc[e, r:r + rs, 0:nc] - lse_ref[r:r + rs, 64 * e:64 * e + 1]
                        if on_diagonal:
                            row = lax.broadcasted_iota(jnp.int32, (rs, nc), 0) + r
                            col = lax.broadcasted_iota(jnp.int32, (rs, nc), 1)
                            s = jnp.where(col <= row, s, NEG)
                        p = jnp.exp2(s)
                        p_sc[e, r:r + rs, 0:nc] = p.astype(bf16)
                        ds_sc[e, r:r + rs, 0:nc] = (p * (dp_sc[e, r:r + rs, 0:nc] - deltas[e][r:r + rs, :])).astype(bf16)
                for r0, r1, nc in blocks:
                    dv_acc[:, 0:nc] += lax.dot_general(does[e][r0:r1], p_sc[e, r0:r1, 0:nc], TN, preferred_element_type=f32)
                    dsb = ds_sc[e, r0:r1, 0:nc]
                    dk_acc[e, :, 0:nc] += lax.dot_general(qes[e][r0:r1], dsb, TN, preferred_element_type=f32)
                    rq = pl.multiple_of(i * tq + r0, r1 - r0)
                    dq_acc[e, pl.ds(rq, r1 - r0), :] += jnp.dot(dsb, kes[e][0:nc], preferred_element_type=f32)

        @pl.when(i > j)
        def _():
            step(False)

        @pl.when(i == j)
        def _():
            step(True)

        @pl.when(i == n - 1)
        def _():
            dk0 = dk_acc[0].T
            dk1 = dk_acc[1].T
            dk_ref[...] = (jnp.where(lane < 64, dk0, dk1) * (1.0 / LOG2E)).astype(bf16)
            dks_ref[...] = jnp.where(lane < 64, dk1, dk0)
            dv_ref[...] = dv_acc[...].T.astype(bf16)

        @pl.when(t == nsteps - 1)
        def _():
            lane_t = lax.broadcasted_iota(jnp.int32, (T, 128), 1)
            dq_ref[...] = (jnp.where(lane_t < 64, dq_acc[0], dq_acc[1]) * 0.125).astype(bf16)
            dqs_ref[...] = jnp.where(lane_t < 64, dq_acc[1], dq_acc[0])

    qmap = lambda p, t, it, jt: (it[t], p)
    grid_spec = pltpu.PrefetchScalarGridSpec(
        num_scalar_prefetch=2, grid=(PAIRS, nsteps),
        in_specs=[pl.BlockSpec((tq, 128), qmap),
                  pl.BlockSpec((tk, 128), lambda p, t, it, jt: (jt[t], PAIRS + p)),
                  pl.BlockSpec((tk, 128), lambda p, t, it, jt: (jt[t], 2 * PAIRS + p)),
                  pl.BlockSpec((tq, 128), qmap), pl.BlockSpec((tk, 128), lambda p, t, it, jt: (jt[t], p)),
                  pl.BlockSpec((tq, 128), qmap), pl.BlockSpec((tq, 128), qmap),
                  pl.BlockSpec((None, tq, 128), lambda p, t, it, jt: (p, it[t], 0)),
                  pl.BlockSpec((8, 128), lambda p, t, it, jt: (0, 0))],
        out_specs=[pl.BlockSpec((T, 128), lambda p, t, it, jt: (0, p)),
                   pl.BlockSpec((None, T, 128), lambda p, t, it, jt: (p, 0, 0)),
                   pl.BlockSpec((tk, 128), lambda p, t, it, jt: (jt[t], p)),
                   pl.BlockSpec((None, tk, 128), lambda p, t, it, jt: (p, jt[t], 0)),
                   pl.BlockSpec((tk, 128), lambda p, t, it, jt: (jt[t], p))],
        scratch_shapes=[pltpu.VMEM((2, T, 128), f32), pltpu.VMEM((2, 128, tk), f32), pltpu.VMEM((128, tk), f32),
                        pltpu.VMEM((2, tq, tk), f32), pltpu.VMEM((2, tq, tk), f32), pltpu.VMEM((2, tq, tk), bf16),
                        pltpu.VMEM((2, tq, tk), bf16)],
    )
    return pl.pallas_call(
        body, name="fox_attn_bwd", grid_spec=grid_spec,
        out_shape=[jax.ShapeDtypeStruct((T, AW), bf16), jax.ShapeDtypeStruct((PAIRS, T, 128), f32),
                   jax.ShapeDtypeStruct((T, AW), bf16), jax.ShapeDtypeStruct((PAIRS, T, 128), f32),
                   jax.ShapeDtypeStruct((T, AW), bf16)],
        compiler_params=_cp("parallel", "arbitrary"),
    )(it, jt, qkv, qkv, qkv, qaug, kaug, attn, dattn, lse, dep)


def _fox_cumsum_bwd(dqs, dks, fl, bfp):
    tb = CUMSUM_ROWS
    nb = T // tb

    def body(dqs_ref, dks_ref, fl_ref, b_ref, df_ref, db_ref, carry):
        i = pl.program_id(0)

        @pl.when(i == 0)
        def _():
            carry[...] = jnp.zeros_like(carry)
            db_ref[...] = jnp.zeros_like(db_ref)

        r = lax.broadcasted_iota(jnp.int32, (128, 128), 0)
        cc = lax.broadcasted_iota(jnp.int32, (128, 128), 1)
        lane = lax.broadcasted_iota(jnp.int32, (tb, 128), 1)
        even_at, odd_at = SPARE[0] + ROW_SUM_LANE, SPARE[1] + ROW_SUM_LANE
        both = jnp.zeros((tb, 128), f32)
        pick = jnp.zeros((128, 128), jnp.bool_)
        for p in range(PAIRS):
            diff = dqs_ref[p] - pltpu.roll(dks_ref[p], 128 - (COL_SUM_LANE - ROW_SUM_LANE), 1)
            moved = pltpu.roll(diff, p, 1) if p else diff
            both = jnp.where(jnp.logical_or(lane == even_at + p, lane == odd_at + p), moved, both)
            pick = jnp.logical_or(pick, jnp.logical_or(jnp.logical_and(r == even_at + p, cc == 2 * p),
                                                       jnp.logical_and(r == odd_at + p, cc == 2 * p + 1)))
        dc = _dot01(pick.astype(bf16), both, False)
        rt = lax.broadcasted_iota(jnp.int32, (tb, tb), 0)
        ct = lax.broadcasted_iota(jnp.int32, (tb, tb), 1)
        utri = (ct >= rt).astype(bf16)
        dl = _dot01(utri, dc, True) + carry[0:1, :]
        carry[...] = jnp.broadcast_to(dl[0:1, :], (8, 128))
        z = fl_ref[...] + b_ref[...]
        df = dl * jax.nn.sigmoid(-z)
        df_ref[...] = df.astype(bf16)
        db_ref[...] += jnp.sum(df, axis=0, keepdims=True)

    rev = lambda i: (nb - 1 - i, 0)
    return pl.pallas_call(
        body, name="fox_cumsum_bwd", grid=(nb,),
        in_specs=[pl.BlockSpec((PAIRS, tb, 128), lambda i: (0, nb - 1 - i, 0)),
                  pl.BlockSpec((PAIRS, tb, 128), lambda i: (0, nb - 1 - i, 0)),
                  pl.BlockSpec((tb, 128), rev), _full((1, 128))],
        out_specs=[pl.BlockSpec((tb, 128), rev), _full((1, 128))],
        out_shape=[jax.ShapeDtypeStruct((T, 128), bf16), jax.ShapeDtypeStruct((1, 128), f32)],
        scratch_shapes=[pltpu.VMEM((8, 128), f32)],
        compiler_params=_cp("arbitrary"),
    )(dqs, dks, fl, bfp)


def _inproj_bwd(dq, dk, dv, du, df, w, x, dx1, g1):
    tm = 512

    def body(dq_ref, dk_ref, dv_ref, du_ref, df_ref, w_ref, x_ref, dx1_ref, g_ref, dx_ref, dn_ref):
        i = pl.program_id(0)

        @pl.when(i == 0)
        def _():
            dn_ref[...] = jnp.zeros_like(dn_ref)

        dproj = jnp.concatenate([dq_ref[...], dk_ref[...], dv_ref[...], du_ref[...], df_ref[...]], axis=1)
        dh = jnp.dot(dproj, w_ref[...], preferred_element_type=f32)
        xv = x_ref[...]
        r = lax.rsqrt(jnp.mean(xv * xv, axis=-1, keepdims=True) + EPS)
        xhat = xv * r
        dn_ref[...] += jnp.sum(dh * xhat, axis=0, keepdims=True)
        z = dh * g_ref[...]
        dx_ref[...] = dx1_ref[...] + r * (z - xhat * jnp.mean(z * xhat, axis=-1, keepdims=True))

    row = lambda i: (i, 0)
    return pl.pallas_call(
        body, name="inproj_bwd", grid=(T // tm,),
        in_specs=[pl.BlockSpec((tm, AW), row)] * 4 + [pl.BlockSpec((tm, 128), row), _full((W_ROWS, D)),
                                                       pl.BlockSpec((tm, D), row), pl.BlockSpec((tm, D), row), _full((1, D))],
        out_specs=[pl.BlockSpec((tm, D), row), _full((1, D))],
        out_shape=[jax.ShapeDtypeStruct((T, D), f32), jax.ShapeDtypeStruct((1, D), f32)],
        compiler_params=_cp("arbitrary"),
    )(dq, dk, dv, du, df, w, x, dx1, g1)


def _adamw_math(w, g, m, v):
    m = B1 * m + (1.0 - B1) * g
    v = B2 * v + (1.0 - B2) * (g * g)
    m_hat = m / (1.0 - B1 ** STEP)
    v_hat = v / (1.0 - B2 ** STEP)
    delta = -LR * (m_hat / (jnp.sqrt(v_hat) + AEPS) + WD * w)
    return delta, m, v


SHARD_STEPS = 4


def _adamw_shards(ws, ms, vs, ps_mine, ps_other, name, steps):
    n = len(ws)

    def body(*refs):
        ins, outs = refs[:5 * n], refs[5 * n:]
        for k in range(n):
            w_ref, m_ref, v_ref, a_ref, b_ref = ins[5 * k:5 * k + 5]
            g_ref, d_ref, nm_ref, nv_ref = outs[4 * k:4 * k + 4]
            g = (a_ref[...].astype(f32) + b_ref[...].astype(f32)).reshape(w_ref.shape)
            g_ref[...] = g
            d_ref[...], nm_ref[...], nv_ref[...] = _adamw_math(w_ref[...], g, m_ref[...], v_ref[...])

    in_specs, out_specs, out_shape = [], [], []
    for w, p in zip(ws, ps_mine):
        rest = tuple(w.shape[1:])
        tr = w.shape[0] // steps
        assert tr * steps == w.shape[0]
        spec = pl.BlockSpec((tr,) + rest, lambda i, _n=len(rest): (i,) + (0,) * _n)
        pspec = pl.BlockSpec((tr, p.shape[1]), lambda i: (i, 0))
        in_specs += [spec] * 3 + [pspec] * 2
        out_specs += [spec] * 4
        out_shape += [jax.ShapeDtypeStruct(w.shape, f32)] * 4
    args = [a for k in range(n) for a in (ws[k], ms[k], vs[k], ps_mine[k], ps_other[k])]
    res = pl.pallas_call(
        body, name=name, grid=(steps,), in_specs=in_specs, out_specs=out_specs, out_shape=out_shape,
        compiler_params=_cp("parallel"),
    )(*args)
    return [res[4 * k:4 * k + 4] for k in range(n)]


SMALL_SLOTS = ((0, 8, 128), (8, 16, 128), (16, 24, 128), (24, 28, 128), (32, 33, 8))
LOSS_ROW = 39


def _adamw_small(ws, ms, vs, parts, parts_wp):
    n = len(ws)

    def body(*refs):
        w_refs, m_refs, v_refs = refs[0:n], refs[n:2 * n], refs[2 * n:3 * n]
        p_ref, pw_ref = refs[3 * n], refs[3 * n + 1]
        outs = refs[3 * n + 2:]
        g_all = p_ref[0]
        g_wp = pw_ref[0]
        for k in range(1, 8):
            g_all = g_all + p_ref[k]
            g_wp = g_wp + pw_ref[k]
        grads = [g_all[r0:r1, 0:lanes] for r0, r1, lanes in SMALL_SLOTS] + [g_wp]
        for idx, g in enumerate(grads):
            d, nm, nv = _adamw_math(w_refs[idx][...], g, m_refs[idx][...], v_refs[idx][...])
            outs[idx][...] = g
            outs[n + idx][...] = d
            outs[2 * n + idx][...] = nm
            outs[3 * n + idx][...] = nv
        outs[4 * n][...] = g_all[LOSS_ROW:LOSS_ROW + 1, :]

    shapes = [jax.ShapeDtypeStruct(w.shape, f32) for w in ws]
    res = pl.pallas_call(
        body, name="adamw_small", out_shape=shapes * 4 + [jax.ShapeDtypeStruct((1, 128), f32)],
    )(*ws, *ms, *vs, parts, parts_wp)
    return res[:4 * n], res[4 * n]


def _sum4(recvs, gs, mine, name, steps):
    n = len(recvs)

    def body(mine_ref, *refs):
        for r_ref, g_ref, o_ref in zip(refs[:n], refs[n:2 * n], refs[2 * n:]):
            o_ref[...] = ((g_ref[...].astype(f32) + r_ref[0].astype(f32))
                          + (r_ref[1].astype(f32) + r_ref[2].astype(f32))).astype(bf16)

    r_specs, g_specs, o_specs, shapes = [], [], [], []
    for recv in recvs:
        _, rows, cols = recv.shape
        tr = rows // steps
        assert tr * steps == rows
        r_specs.append(pl.BlockSpec((3, tr, cols), lambda i, m: (0, i, 0)))
        g_specs.append(pl.BlockSpec((None, tr, cols), lambda i, m: (m[0], i, 0)))
        o_specs.append(pl.BlockSpec((tr, cols), lambda i, m: (i, 0)))
        shapes.append(jax.ShapeDtypeStruct((rows, cols), bf16))
    grid_spec = pltpu.PrefetchScalarGridSpec(num_scalar_prefetch=1, grid=(steps,), in_specs=r_specs + g_specs,
                                             out_specs=o_specs)
    return pl.pallas_call(
        body, name=name, grid_spec=grid_spec, out_shape=shapes, compiler_params=_cp("arbitrary"),
    )(mine, *recvs, *gs)


_HBM = pl.BlockSpec(memory_space=pltpu.HBM)
_SEM = pl.BlockSpec(memory_space=pltpu.SEMAPHORE)
_EFFECT = pltpu.SideEffectType.DATAFLOW_SIDE_EFFECTING


def _in_hbm(a):
    return pltpu.with_memory_space_constraint(a, pltpu.HBM)


def _mesh_pos():
    return lax.axis_index("x"), lax.axis_index("y"), lax.axis_index("c")


def _other_chips(x, y):
    return [(1 - x, y), (x, 1 - y), (1 - x, 1 - y)]


def _gather_copy(srcs, lands, send_sems, recv_sems, a, k, slot):
    x, y, c = _mesh_pos()
    cx, cy = _other_chips(x, y)[k]
    return pltpu.make_async_remote_copy(
        src_ref=srcs[a], dst_ref=lands[a].at[slot], send_sem=send_sems.at[3 * a + k], recv_sem=recv_sems.at[3 * a + k],
        device_id=(cx, cy, c), device_id_type=MESH)


def _scatter_copy(srcs, lands, send_sems, recv_sems, a, k):
    x, y, c = _mesh_pos()
    cx, cy = _other_chips(x, y)[k]
    return pltpu.make_async_remote_copy(
        src_ref=srcs[a].at[2 * cx + cy], dst_ref=lands[a].at[k], send_sem=send_sems.at[3 * a + k],
        recv_sem=recv_sems.at[3 * a + k], device_id=(cx, cy, c), device_id_type=MESH)


def _all_gather_w_in(part):
    cols = part.shape[1] // 2

    def body(src, dst, send_sems, recv_sems, loc_sem):
        x, y, c = _mesh_pos()
        mine = 2 * x + y
        chips = _other_chips(x, y)
        half = lambda ref, cc: ref.at[:, pl.ds(pl.multiple_of(cc * cols, cols), cols)]

        def over_ici(k, slot):
            cx, cy = chips[k]
            return pltpu.make_async_remote_copy(
                src_ref=half(src, c), dst_ref=half(dst.at[slot], c), send_sem=send_sems.at[k], recv_sem=recv_sems.at[k],
                device_id=(cx, cy, c), device_id_type=MESH)

        def to_sibling(k, cc):
            slot = 2 * chips[k][0] + chips[k][1]
            return pltpu.make_async_remote_copy(
                src_ref=half(dst.at[slot], cc), dst_ref=half(dst.at[slot], cc), send_sem=send_sems.at[3 + k],
                recv_sem=recv_sems.at[3 + k], device_id=(x, y, 1 - c), device_id_type=MESH)

        local = pltpu.make_async_copy(src, dst.at[mine], loc_sem.at[0])
        local.start()
        first = [over_ici(k, mine) for k in range(3)]
        for cp in first:
            cp.start()
        passed = [to_sibling(k, c) for k in range(3)]
        for k in range(3):
            over_ici(k, 2 * chips[k][0] + chips[k][1]).wait_recv()
            passed[k].start()
        for k in range(3):
            to_sibling(k, 1 - c).wait_recv()
        for cp in first + passed:
            cp.wait_send()
        local.wait()

    return pl.pallas_call(
        body, name="all_gather_w_in", in_specs=[_HBM], out_specs=_HBM,
        out_shape=jax.ShapeDtypeStruct((NSH,) + part.shape, part.dtype),
        scratch_shapes=[pltpu.SemaphoreType.DMA((6,)), pltpu.SemaphoreType.DMA((6,)), pltpu.SemaphoreType.DMA((1,))],
    )(part)


def _split_start(name, srcs, lands, n_sems, plan, dep):
    n, nl = len(srcs), len(lands)

    def body(*refs):
        src_refs, land_refs = refs[:n], refs[n:n + nl]
        send_sems, recv_sems = refs[n + nl + 1], refs[n + nl + 2]
        token = refs[-1]
        sends, _, own = plan(src_refs, land_refs, send_sems, recv_sems)
        for cp in own + sends:
            cp.start()
        token[...] = jnp.zeros_like(token)

    outs = pl.pallas_call(
        body, name=name,
        in_specs=[_HBM] * (n + nl) + [pl.BlockSpec(memory_space=pl.ANY)],
        out_specs=[_SEM, _SEM] + [_HBM] * (n + nl) + [pl.BlockSpec(memory_space=pltpu.VMEM)],
        out_shape=[pltpu.SemaphoreType.DMA((n_sems,)), pltpu.SemaphoreType.DMA((n_sems,))]
        + [pltpu.HBM(a.shape, a.dtype) for a in list(srcs) + list(lands)] + [jax.ShapeDtypeStruct((8, 128), f32)],
        input_output_aliases={i: 2 + i for i in range(n + nl)},
        compiler_params=pltpu.CompilerParams(has_side_effects=_EFFECT),
    )(*[_in_hbm(a) for a in list(srcs) + list(lands)], dep)
    return outs[0], outs[1], list(outs[2:2 + n]), list(outs[2 + n:2 + n + nl]), outs[-1]


def _split_wait(name, send_sems, recv_sems, srcs, lands, after, plan):
    n, nl = len(srcs), len(lands)

    def body(*refs):
        src_refs, land_refs = refs[:n], refs[n:n + nl]
        s_sems, r_sems = refs[n + nl], refs[n + nl + 1]
        sends, recvs, own = plan(src_refs, land_refs, s_sems, r_sems)
        for cp in own:
            cp.wait()
        for cp in recvs:
            cp.wait_recv()
        for cp in sends:
            cp.wait_send()

    outs = pl.pallas_call(
        body, name=name,
        in_specs=[_HBM] * (n + nl) + [_SEM, _SEM, pl.BlockSpec(memory_space=pl.ANY)],
        out_specs=[_HBM] * (n + nl),
        out_shape=[pltpu.HBM(a.shape, a.dtype) for a in list(srcs) + list(lands)],
        input_output_aliases={i: i for i in range(n + nl)},
        compiler_params=pltpu.CompilerParams(has_side_effects=_EFFECT),
    )(*srcs, *lands, send_sems, recv_sems, after)
    return list(outs[:n]), list(outs[n:])


def _gather_plan(srcs, lands, ss, rs):
    x, y, _ = _mesh_pos()
    chips = _other_chips(x, y)
    sends = [_gather_copy(srcs, lands, ss, rs, a, k, 2 * x + y) for a in range(len(srcs)) for k in range(3)]
    recvs = [_gather_copy(srcs, lands, ss, rs, a, k, 2 * chips[k][0] + chips[k][1])
             for a in range(len(srcs)) for k in range(3)]
    own = [pltpu.make_async_copy(srcs[a], lands[a].at[2 * x + y], rs.at[3 * len(srcs) + a]) for a in range(len(srcs))]
    return sends, recvs, own


def _scatter_and_spread_plan(srcs, lands, ss, rs):
    x, y, c = _mesh_pos()
    me = 4 * x + 2 * y + c
    n = len(srcs) - 1
    cps = [_scatter_copy(srcs[:n], lands[:n], ss, rs, a, k) for a in range(n) for k in range(3)]
    for f in range(1, 8):
        peer = ((x + (f >> 2)) % 2, (y + ((f >> 1) & 1)) % 2, (c + (f & 1)) % 2)
        cps.append(pltpu.make_async_remote_copy(
            src_ref=srcs[n], dst_ref=lands[n].at[me], send_sem=ss.at[3 * n - 1 + f], recv_sem=rs.at[3 * n - 1 + f],
            device_id=peer, device_id_type=MESH))
    own = [pltpu.make_async_copy(srcs[n], lands[n].at[me], rs.at[3 * n + 7])]
    return cps, cps, own


def _swap_with_sibling(parts, name):
    n = len(parts)

    def body(*refs):
        srcs, dsts = refs[:n], refs[n:2 * n]
        send_sems, recv_sems = refs[2 * n:]
        x, y, c = _mesh_pos()
        cps = [pltpu.make_async_remote_copy(src_ref=srcs[a], dst_ref=dsts[a], send_sem=send_sems.at[a],
                                            recv_sem=recv_sems.at[a], device_id=(x, y, 1 - c), device_id_type=MESH)
               for a in range(n)]
        for cp in cps:
            cp.start()
        for cp in cps:
            cp.wait_recv()
        for cp in cps:
            cp.wait_send()

    return pl.pallas_call(
        body, name=name, in_specs=[_HBM] * n, out_specs=[_HBM] * n,
        out_shape=[jax.ShapeDtypeStruct(p.shape, p.dtype) for p in parts],
        scratch_shapes=[pltpu.SemaphoreType.DMA((n,)), pltpu.SemaphoreType.DMA((n,))],
    )(*parts)


def _forward(x, tgt, w_in_t, mlp_w_fn, g1, bfp, wp, scale, g2, gf, dep):
    h, qkv, u, fl = _rms_inproj(x, g1, w_in_t, dep)
    qaug, kaug = _fox_cumsum(fl, bfp)
    attn, lse = _attn_fwd(qkv, qaug, kaug)
    pooled, pool = _pool_fwd(u, wp, scale)
    wo, wgt, wut, wd = mlp_w_fn(attn)
    x1, h2 = _outproj(x, attn, pool, wo, g2)
    loss, dgf, dx2, dx2b, ud, silu, a_b = _mlp_fwd_loss(h2, x1, wgt, wut, wd, tgt, gf)
    saved = dict(h=h, qkv=qkv, fl=fl, qaug=qaug, kaug=kaug, attn=attn, lse=lse, pooled=pooled, pool=pool, x1=x1, h2=h2,
                 ud=ud, silu=silu, a_b=a_b, wo=wo, wgt=wgt, wut=wut, wd=wd)
    return loss, dgf, dx2, dx2b, saved


def _backward_mlp(sv, dx2, dx2b, g2):
    dgate, dup, dx1, dx1b, dg2 = _mlp_bwd(dx2b, dx2, sv["ud"], sv["silu"], sv["wgt"], sv["wut"], sv["wd"], sv["x1"], g2)
    (dwd,) = _mm_tn(sv["a_b"], [dx2b], "dw_down", a_sharded=True, tk=T)
    (dwgt,) = _mm_tn(dgate, [sv["h2"]], "dw_gate", a_sharded=True, tk=T)
    (dwut,) = _mm_tn(dup, [sv["h2"]], "dw_up", a_sharded=True, tk=T)
    return dx1, dx1b, dg2, (dwgt, dwut, dwd)


def _backward_outproj_pool(sv, dx1b, wp, scale):
    dattn, dpool = _outproj_bwd(dx1b, sv["wo"])
    (dwo,) = _mm_tn_rows([sv["attn"], sv["pool"]], dx1b, "dw_out", tk=2048, stacked=True)
    dwo = dwo.reshape(NSH, D // NSH, D)
    du, dscale, dwp = _pool_bwd(dpool, sv["pooled"], wp, scale)
    return dattn, dwo, du, dscale, dwp


def _backward_attn_inproj(sv, x, dx1, dattn, du, w_in_t, g1, bfp, dep):
    dq, dqs, dk, dks, dv = _attn_bwd(sv["qkv"], sv["qaug"], sv["kaug"], sv["attn"], dattn, sv["lse"], dep)
    df, dbf = _fox_cumsum_bwd(dqs, dks, sv["fl"], bfp)
    dx, dg1 = _inproj_bwd(dq, dk, dv, du, df, w_in_t, x, dx1, g1)
    dwq, dwk, dwv, dwu_in, dwf = _mm_tn_rows([dq, dk, dv, du, df], sv["h"], "dw_in")
    dwin = jnp.concatenate([dwq, dwk, dwv, dwf[0:8], dwu_in], axis=0)
    return dx, dg1, dbf, dwin.reshape(NSH, IN_S, D)


def kernel(x, norm1_g, w_in, b_forget, w_pool, pool_scale, w_out, norm2_g, w_gate, w_up, w_down, final_g, loss_target, m_norm1_g, m_w_in, m_b_forget, m_w_pool, m_pool_scale, m_w_out, m_norm2_g, m_w_gate, m_w_up, m_w_down, m_final_g, v_norm1_g, v_w_in, v_b_forget, v_w_pool, v_pool_scale, v_w_out, v_norm2_g, v_w_gate, v_w_up, v_w_down, v_final_g):
    mine = (2 * lax.axis_index("x") + lax.axis_index("y")).astype(jnp.int32)
    mine1 = mine.reshape(1)
    tr = lambda a: jnp.transpose(a[0])

    win4 = _all_gather_w_in(tr(w_in).astype(bf16))
    later = [w_out[0].astype(bf16), tr(w_gate).astype(bf16), tr(w_up).astype(bf16), w_down[0].astype(bf16)]
    lands = [lax.empty((NSH,) + p.shape, bf16) for p in later]
    ag_send, ag_recv, later_thru, lands_thru, ag_token = _split_start("all_gather_start", later, lands, 16, _gather_plan,
                                                                      win4)
    win = win4.reshape(IN_W, D)
    w_in_t = jnp.concatenate([win[0:3 * AW], win[3 * AW + 8:], win[3 * AW:3 * AW + 8], jnp.zeros((120, D), bf16)], axis=0)
    bfp = jnp.pad(b_forget, ((0, 0), (0, 120)))
    wp = w_pool[0].astype(bf16)
    gf = final_g.reshape(1, D)

    def later_weights(after):
        _, (wo4, wgt, wut, wd) = _split_wait("all_gather_wait", ag_send, ag_recv, later_thru, lands_thru, after, _gather_plan)
        return wo4.reshape(D, D), wgt, wut, wd

    xe, tgt = x[0], loss_target[0]
    loss_v, dgf, dx2, dx2b, sv = _forward(xe, tgt, w_in_t, later_weights, norm1_g, bfp, wp, pool_scale, norm2_g, gf, ag_token)
    dx1, dx1b, dg2, mlp_grads = _backward_mlp(sv, dx2, dx2b, norm2_g)
    dattn, dwo, du, dscale, dwp = _backward_outproj_pool(sv, dx1b, wp, pool_scale)
    first = [dwo] + list(mlp_grads) + [dwp.reshape(512, 128)]
    first_lands = [lax.empty((3,) + g.shape[1:], bf16) for g in first[:4]] + [lax.empty((8, 512, 128), f32)]
    rs_send, rs_recv, first_thru, first_lands_thru, rs_token = _split_start(
        "reduce_scatter_start", first, first_lands, 20, _scatter_and_spread_plan, du)
    dx, dg1, dbf, dwin = _backward_attn_inproj(sv, xe, dx1, dattn, du, w_in_t, norm1_g, bfp, rs_token)

    pad8 = lambda r: jnp.pad(r, ((0, 8 - r.shape[0]), (0, 0)))
    loss_rows = jnp.concatenate([dbf, jnp.zeros((6, 128), f32), loss_v[0:1, :]], axis=0)
    small = jnp.concatenate([dg1.reshape(8, 128), dg2.reshape(8, 128), dgf.reshape(8, 128), pad8(dscale.reshape(4, 128)),
                             loss_rows], axis=0)
    tail_send, tail_recv, tail_thru, tail_lands_thru, tail_token = _split_start(
        "tail_start", [dwin, small], [lax.empty((3,) + dwin.shape[1:], bf16), lax.empty((8, SMALL_ROWS, 128), f32)], 11,
        _scatter_and_spread_plan, dx)
    first_thru, first_recv = _split_wait("reduce_scatter_wait", rs_send, rs_recv, first_thru, first_lands_thru, tail_token,
                                         _scatter_and_spread_plan)
    wp_all = first_recv[4]
    tr3 = lambda a: jnp.transpose(a, (2, 0, 1))
    ws = [tr3(w_in), w_out[0], tr(w_gate), tr(w_up), w_down[0]]
    ms = [tr3(m_w_in), m_w_out[0], tr(m_w_gate), tr(m_w_up), m_w_down[0]]
    vs = [tr3(v_w_in), v_w_out[0], tr(v_w_gate), tr(v_w_up), v_w_down[0]]
    partial = _sum4(first_recv[:4], first_thru[:4], mine1, "sum4_first", SHARD_STEPS)
    other = _swap_with_sibling(partial, "swap_first")
    big = _adamw_shards(ws[1:], ms[1:], vs[1:], partial, other, "adamw_first", SHARD_STEPS)
    (dwin_thru, _), (in_recv_land, small_all) = _split_wait("tail_wait", tail_send, tail_recv, tail_thru, tail_lands_thru,
                                                            big[3][0], _scatter_and_spread_plan)
    partial_in = _sum4([in_recv_land], [dwin_thru], mine1, "sum4_in", 1)
    other_in = _swap_with_sibling(partial_in, "swap_in")
    big = _adamw_shards(ws[:1], ms[:1], vs[:1], partial_in, other_in, "adamw_in", 1) + big

    small_names = ["norm1_g", "norm2_g", "final_g", "pool_scale", "b_forget", "w_pool"]
    rows = lambda a, b, c, d, e, f: [a.reshape(8, 128), b.reshape(8, 128), c.reshape(8, 128), d.reshape(4, 128),
                                     e.reshape(1, 8), f.reshape(512, 128)]
    sm, loss_row = _adamw_small(rows(norm1_g, norm2_g, final_g, pool_scale, b_forget, w_pool),
                                rows(m_norm1_g, m_norm2_g, m_final_g, m_pool_scale, m_b_forget, m_w_pool),
                                rows(v_norm1_g, v_norm2_g, v_final_g, v_pool_scale, v_b_forget, v_w_pool), small_all, wp_all)
    small_shape = dict(norm1_g=(1, D), norm2_g=(1, D), final_g=(D,), pool_scale=(1, AW), b_forget=(1, 8),
                       w_pool=(1, 4, 128, 128))

    order = ["norm1_g", "w_in", "b_forget", "w_pool", "pool_scale", "w_out", "norm2_g", "w_gate", "w_up", "w_down", "final_g"]
    big_idx = {"w_in": 0, "w_out": 1, "w_gate": 2, "w_up": 3, "w_down": 4}
    outs = [loss_row[0, 0], dx[None]]
    for kind in range(4):
        for name in order:
            if name == "w_in":
                outs.append(jnp.transpose(big[0][kind], (1, 2, 0)))
            elif name in ("w_gate", "w_up"):
                outs.append(jnp.transpose(big[big_idx[name]][kind])[None])
            elif name in big_idx:
                outs.append(big[big_idx[name]][kind][None])
            else:
                outs.append(sm[6 * kind + small_names.index(name)].reshape(small_shape[name]))
    return tuple(outs)
```

```python
import jax
import jax.numpy as jnp
import numpy as np
from jax import lax
from jax.experimental import pallas as pl
from jax.experimental.pallas import tpu as pltpu

f32 = jnp.float32
bf16 = jnp.bfloat16

T = 4096
D = 1024
NSH = 4
IN_W = 2056
IN_S = IN_W // NSH
AW = 512
PAIRS = 4
SPARE = (64, 0)
ROW_SUM_LANE, COL_SUM_LANE = 0, 3
FF = 2816
FS = FF // NSH
WINDOWS = (2, 4, 8, 16)
HALO = 32
EPS = 1e-6
NEG = -1e30
LR, B1, B2, AEPS, WD, STEP = 0.001, 0.9, 0.999, 1e-08, 0.01, 10
SMALL_ROWS = 40

NT = (((1,), (1,)), ((), ()))
TN = (((0,), (0,)), ((), ()))

MESH = pl.DeviceIdType.MESH


def _cp(*sem):
    return pltpu.CompilerParams(dimension_semantics=sem)


def _full(shape):
    n = len(shape)
    return pl.BlockSpec(shape, lambda *_: (0,) * n)


def _resident(shape):
    n = len(shape)
    return pl.BlockSpec(shape, lambda *_: (0,) * n, pipeline_mode=pl.Buffered(1))


W_ROWS = 4 * AW + 128


def _rms_inproj(x, g1, w, dep):
    tm = 512

    def body(x_ref, g_ref, w_ref, dep_ref, h_ref, qkv_ref, u_ref, fl_ref):
        xv = x_ref[...]
        r = lax.rsqrt(jnp.mean(xv * xv, axis=-1, keepdims=True) + EPS)
        h = (xv * r * g_ref[...]).astype(bf16)
        h_ref[...] = h
        qkv_ref[...] = lax.dot_general(h, w_ref[0:3 * AW, :], NT, preferred_element_type=f32).astype(bf16)
        u_ref[...] = lax.dot_general(h, w_ref[3 * AW:4 * AW, :], NT, preferred_element_type=f32)
        fl_ref[...] = lax.dot_general(h, w_ref[4 * AW:W_ROWS, :], NT, preferred_element_type=f32)

    return pl.pallas_call(
        body, name="rms_inproj", grid=(T // tm,),
        in_specs=[pl.BlockSpec((tm, D), lambda i: (i, 0)), _full((1, D)), _full((W_ROWS, D)), _full((8, 128))],
        out_specs=[pl.BlockSpec((tm, D), lambda i: (i, 0)), pl.BlockSpec((tm, 3 * AW), lambda i: (i, 0)),
                   pl.BlockSpec((tm, AW), lambda i: (i, 0)), pl.BlockSpec((tm, 128), lambda i: (i, 0))],
        out_shape=[jax.ShapeDtypeStruct((T, D), bf16), jax.ShapeDtypeStruct((T, 3 * AW), bf16),
                   jax.ShapeDtypeStruct((T, AW), f32), jax.ShapeDtypeStruct((T, 128), f32)],
        compiler_params=_cp("parallel"),
    )(x, g1, w, dep)


CUMSUM_ROWS = 512
FS_CHUNKS = ((0, 256), (256, 512), (512, FS))


def _log_sigmoid(z):
    return jnp.minimum(z, 0.0) - jnp.log(1.0 + jnp.exp(-jnp.abs(z)))


def _split3(x):
    hi = x.astype(bf16)
    r1 = x - hi.astype(f32)
    mid = r1.astype(bf16)
    return hi, mid, (r1 - mid.astype(f32)).astype(bf16)


def _dot01(sel, x, sel_first):
    parts = _split3(x)
    if sel_first:
        return sum(jnp.dot(sel, p, preferred_element_type=f32) for p in parts)
    return sum(jnp.dot(p, sel, preferred_element_type=f32) for p in parts)


def _fox_cumsum(fl, bfp):
    tb = CUMSUM_ROWS
    nb = T // tb

    def body(fl_ref, b_ref, qa_ref, ka_ref, carry):
        i = pl.program_id(0)

        @pl.when(i == 0)
        def _():
            carry[...] = jnp.zeros_like(carry)

        lf = _log_sigmoid(fl_ref[...] + b_ref[...])
        r = lax.broadcasted_iota(jnp.int32, (tb, tb), 0)
        cc = lax.broadcasted_iota(jnp.int32, (tb, tb), 1)
        ltri = (cc <= r).astype(bf16)
        cb = _dot01(ltri, lf, True) + carry[0:1, :]
        carry[...] = jnp.broadcast_to(cb[tb - 1:tb, :], (8, 128))
        hi, mid, lo = _split3(cb * LOG2E)
        lane = lax.broadcasted_iota(jnp.int32, (tb, 128), 1)
        terms = jnp.where(lane < 8, hi.astype(f32),
                          jnp.where(lane < 16, pltpu.roll(mid.astype(f32), 8, 1), pltpu.roll(lo.astype(f32), 16, 1)))
        terms = terms.astype(bf16)
        src = lax.broadcasted_iota(jnp.int32, (128, AW), 0)
        col = lax.broadcasted_iota(jnp.int32, (128, AW), 1)
        head, term = src & 7, src >> 3
        base = 128 * (head >> 1) + jnp.where((head & 1) == 0, SPARE[0], SPARE[1])
        place = lambda off: jnp.logical_and(col == base + off + term, term < 3).astype(bf16)
        cq = jnp.dot(terms, place(0), preferred_element_type=f32)
        ck = jnp.dot(terms, place(3), preferred_element_type=f32)
        within = jnp.bitwise_and(lax.broadcasted_iota(jnp.int32, (tb, AW), 1), 63)
        qa_ref[...] = jnp.where(jnp.logical_and(within >= 3, within <= 5), 1.0, cq).astype(bf16)
        ka_ref[...] = jnp.where(within <= 2, 1.0, -ck).astype(bf16)

    return pl.pallas_call(
        body, name="fox_cumsum", grid=(nb,),
        in_specs=[pl.BlockSpec((tb, 128), lambda i: (i, 0)), _full((1, 128))],
        out_specs=[pl.BlockSpec((tb, AW), lambda i: (i, 0)), pl.BlockSpec((tb, AW), lambda i: (i, 0))],
        out_shape=[jax.ShapeDtypeStruct((T, AW), bf16), jax.ShapeDtypeStruct((T, AW), bf16)],
        scratch_shapes=[pltpu.VMEM((8, 128), f32)],
        compiler_params=_cp("arbitrary"),
    )(fl, bfp)


ATT_T = 512
LOG2E = 1.4426950408889634
Q_SCALE = 0.125 * LOG2E


def _causal_steps(key_major):
    n = T // ATT_T
    if key_major:
        pairs = [(i, j) for j in range(n) for i in range(j, n)]
    else:
        pairs = [(i, j) for i in range(n) for j in range(i + 1)]
    it = np.array([p[0] for p in pairs], np.int32)
    jt = np.array([p[1] for p in pairs], np.int32)
    return jnp.asarray(it), jnp.asarray(jt)


def _row_blocks(tq, tk, on_diagonal):
    return ((0, tq // 2, tk // 2), (tq // 2, tq, tk)) if on_diagonal else ((0, tq, tk),)


def _attn_fwd(qkv, qaug, kaug):
    tq = tk = ATT_T
    it, jt = _causal_steps(False)
    nsteps = it.shape[0]

    rs = 64

    def body(it_ref, jt_ref, q_ref, k_ref, v_ref, qa_ref, ka_ref, o_ref, lse_ref, m_sc, acc_sc, s_sc, p_sc, alpha_sc):
        t = pl.program_id(1)
        i = it_ref[t]
        j = jt_ref[t]

        @pl.when(j == 0)
        def _():
            m_sc[...] = jnp.full_like(m_sc, NEG)
            acc_sc[...] = jnp.zeros_like(acc_sc)

        lane = lax.broadcasted_iota(jnp.int32, (tq, 128), 1)
        spare = SPARE

        def step(on_diagonal):
            q = (q_ref[...].astype(f32) * Q_SCALE).astype(bf16)
            k = k_ref[...]
            v = v_ref[...]
            qa = qa_ref[...]
            ka = ka_ref[...]
            blocks = _row_blocks(tq, tk, on_diagonal)
            for e in range(2):
                hm = (lane >= 64) if e else (lane < 64)
                qe = jnp.where(hm, q, qa)
                ke = jnp.where(hm, k, ka)
                for r0, r1, nc in blocks:
                    s_sc[e, r0:r1, 0:nc] = lax.dot_general(qe[r0:r1], ke[0:nc], NT, preferred_element_type=f32)
            for e in range(2):
                for r0, r1, nc in blocks:
                    for r in range(r0, r1, rs):
                        s = s_sc[e, r:r + rs, 0:nc]
                        if on_diagonal:
                            row = lax.broadcasted_iota(jnp.int32, (rs, nc), 0) + r
                            col = lax.broadcasted_iota(jnp.int32, (rs, nc), 1)
                            s = jnp.where(col <= row, s, NEG)
                        m_prev = m_sc[e, r:r + rs, :]
                        m_new = jnp.maximum(m_prev, jnp.max(s, axis=1, keepdims=True))
                        p_sc[e, r:r + rs, 0:nc] = jnp.exp2(s - jnp.tile(m_new, (1, nc // 128))).astype(bf16)
                        alpha_sc[e, r:r + rs, :] = jnp.exp2(m_prev - m_new)
                        m_sc[e, r:r + rs, :] = m_new
            for e in range(2):
                hm = (lane >= 64) if e else (lane < 64)
                ve = jnp.where(hm, v, (lane == spare[e]).astype(bf16))
                for r0, r1, nc in blocks:
                    acc_sc[e, r0:r1] = (alpha_sc[e, r0:r1] * acc_sc[e, r0:r1]
                                        + jnp.dot(p_sc[e, r0:r1, 0:nc], ve[0:nc], preferred_element_type=f32))

        @pl.when(j < i)
        def _():
            step(False)

        @pl.when(j == i)
        def _():
            step(True)
            l0 = acc_sc[0][:, spare[0]:spare[0] + 1]
            l1 = acc_sc[1][:, spare[1]:spare[1] + 1]
            o_ref[...] = jnp.where(lane < 64, acc_sc[0] / l0, acc_sc[1] / l1).astype(bf16)
            lse_ref[...] = jnp.where(lane < 64, m_sc[0] + jnp.log2(l0), m_sc[1] + jnp.log2(l1))

    qmap = lambda p, t, it, jt: (it[t], p)
    kmap = lambda p, t, it, jt: (jt[t], p)
    grid_spec = pltpu.PrefetchScalarGridSpec(
        num_scalar_prefetch=2, grid=(PAIRS, nsteps),
        in_specs=[pl.BlockSpec((tq, 128), qmap),
                  pl.BlockSpec((tk, 128), lambda p, t, it, jt: (jt[t], PAIRS + p)),
                  pl.BlockSpec((tk, 128), lambda p, t, it, jt: (jt[t], 2 * PAIRS + p)),
                  pl.BlockSpec((tq, 128), qmap), pl.BlockSpec((tk, 128), kmap)],
        out_specs=[pl.BlockSpec((tq, 128), qmap),
                   pl.BlockSpec((None, tq, 128), lambda p, t, it, jt: (p, it[t], 0))],
        scratch_shapes=[pltpu.VMEM((2, tq, 128), f32), pltpu.VMEM((2, tq, 128), f32), pltpu.VMEM((2, tq, tk), f32),
                        pltpu.VMEM((2, tq, tk), bf16), pltpu.VMEM((2, tq, 128), f32)],
    )
    return pl.pallas_call(
        body, name="fox_attn_fwd", grid_spec=grid_spec,
        out_shape=[jax.ShapeDtypeStruct((T, AW), bf16), jax.ShapeDtypeStruct((PAIRS, T, 128), f32)],
        compiler_params=_cp("parallel", "arbitrary"),
    )(it, jt, qkv, qkv, qkv, qaug, kaug)


def _pool_fwd(u, wp, scale):
    tm = 1024

    def body(u_ref, wp_ref, sc_ref, pooled_ref, pool_ref, ext, lv):
        i = pl.program_id(0)

        @pl.when(i == 0)
        def _():
            ext[0:HALO, :] = jnp.zeros((HALO, AW), f32)

        uv = u_ref[...]
        ext[HALO:HALO + tm, :] = uv
        t_idx = i * tm + lax.broadcasted_iota(jnp.int32, (tm, 1), 0)
        for g, w in enumerate(WINDOWS):
            lo, hi = 128 * g, 128 * (g + 1)
            ug = uv[:, lo:hi]
            steps = w.bit_length() - 1
            assert 1 << steps == w and 8 * steps <= HALO
            for k in range(1, steps + 1):
                back = 1 << (k - 1)
                first = 8 * k if k < steps else HALO
                rd = (lambda a, b: ext[a:b, lo:hi]) if k == 1 else (lambda a, b, s=k % 2: lv[s, a:b, :])
                cur = rd(first, tm + HALO) + rd(first - back, tm + HALO - back)
                if k < steps:
                    lv[(k + 1) % 2, first:tm + HALO, :] = cur
            acc = cur
            cnt = jnp.minimum(t_idx + 1, w).astype(f32)
            pb = (acc / cnt - ug).astype(bf16)
            pooled_ref[:, lo:hi] = pb
            mixed = jnp.dot(pb, wp_ref[g], preferred_element_type=f32)
            pool_ref[:, lo:hi] = (mixed * sc_ref[:, lo:hi]).astype(bf16)
        ext[0:HALO, :] = uv[tm - HALO:tm, :]

    return pl.pallas_call(
        body, name="pool_fwd", grid=(T // tm,),
        in_specs=[pl.BlockSpec((tm, AW), lambda i: (i, 0)), _full((4, 128, 128)), _full((1, AW))],
        out_specs=[pl.BlockSpec((tm, AW), lambda i: (i, 0)), pl.BlockSpec((tm, AW), lambda i: (i, 0))],
        out_shape=[jax.ShapeDtypeStruct((T, AW), bf16), jax.ShapeDtypeStruct((T, AW), bf16)],
        scratch_shapes=[pltpu.VMEM((tm + HALO, AW), f32), pltpu.VMEM((2, tm + HALO, 128), f32)],
        compiler_params=_cp("arbitrary"),
    )(u, wp, scale)


def _outproj(x, attn, pool, wo, g2):
    tm = 1024

    def body(x_ref, a_ref, p_ref, wo_ref, g_ref, x1_ref, h2_ref):
        mixed = jnp.concatenate([a_ref[...], p_ref[...]], axis=1)
        x1 = x_ref[...] + jnp.dot(mixed, wo_ref[...], preferred_element_type=f32)
        x1_ref[...] = x1
        r = lax.rsqrt(jnp.mean(x1 * x1, axis=-1, keepdims=True) + EPS)
        h2_ref[...] = (x1 * r * g_ref[...]).astype(bf16)

    return pl.pallas_call(
        body, name="outproj", grid=(T // tm,),
        in_specs=[pl.BlockSpec((tm, D), lambda i: (i, 0)), pl.BlockSpec((tm, AW), lambda i: (i, 0)),
                  pl.BlockSpec((tm, AW), lambda i: (i, 0)), _full((D, D)), _full((1, D))],
        out_specs=[pl.BlockSpec((tm, D), lambda i: (i, 0)), pl.BlockSpec((tm, D), lambda i: (i, 0))],
        out_shape=[jax.ShapeDtypeStruct((T, D), f32), jax.ShapeDtypeStruct((T, D), bf16)],
        compiler_params=_cp("parallel"),
    )(x, attn, pool, wo, g2)


def _mlp_fwd_loss(h2, x1, wg, wu, wd, tgt, gf):
    tm = 512

    def body(h_ref, x1_ref, wg_ref, wu_ref, wd_ref, t_ref, g_ref,
             loss_ref, dg_ref, dx_ref, dxb_ref, ud_ref, silu_ref, a_ref, x2):
        i = pl.program_id(0)
        s = pl.program_id(1)

        @pl.when(jnp.logical_and(i == 0, s == 0))
        def _():
            loss_ref[...] = jnp.zeros_like(loss_ref)
            dg_ref[...] = jnp.zeros_like(dg_ref)

        h = h_ref[...]
        gus = [(lax.dot_general(h, wg_ref[s, c0:c1, :], NT, preferred_element_type=f32),
                lax.dot_general(h, wu_ref[s, c0:c1, :], NT, preferred_element_type=f32)) for c0, c1 in FS_CHUNKS]
        for (c0, c1), (gate, up) in zip(FS_CHUNKS, gus):
            sg = jax.nn.sigmoid(gate)
            silu = gate * sg
            ud_ref[:, c0:c1] = (up * (sg * (1.0 + gate * (1.0 - sg)))).astype(bf16)
            silu_ref[:, c0:c1] = silu.astype(bf16)
            a_ref[:, c0:c1] = (silu * up).astype(bf16)
        part = jnp.dot(a_ref[...], wd_ref[s], preferred_element_type=f32)

        @pl.when(s == 0)
        def _():
            x2[...] = x1_ref[...] + part

        @pl.when(s > 0)
        def _():
            x2[...] += part

        @pl.when(s == NSH - 1)
        def _():
            xv = x2[...]
            g = g_ref[...]
            r = lax.rsqrt(jnp.mean(xv * xv, axis=-1, keepdims=True) + EPS)
            xhat = xv * r
            e = xhat * g - t_ref[...]
            loss_ref[...] += 0.5 * jnp.sum(jnp.mean(e * e, axis=-1, keepdims=True))
            dy = e * (1.0 / D)
            dg_ref[...] += jnp.sum(dy * xhat, axis=0, keepdims=True)
            z = dy * g
            dx = r * (z - xhat * jnp.mean(z * xhat, axis=-1, keepdims=True))
            dx_ref[...] = dx
            dxb_ref[...] = dx.astype(bf16)

    row = lambda i, s: (i, 0)
    sl = lambda i, s: (s, i, 0)
    wsl = lambda i, s: (s, 0, 0)
    return pl.pallas_call(
        body, name="mlp_fwd_loss", grid=(T // tm, NSH),
        in_specs=[pl.BlockSpec((tm, D), row), pl.BlockSpec((tm, D), row),
                  _resident((NSH, FS, D)), _resident((NSH, FS, D)), _resident((NSH, FS, D)),
                  pl.BlockSpec((tm, D), row), pl.BlockSpec((1, D), lambda i, s: (0, 0))],
        out_specs=[pl.BlockSpec((8, 128), lambda i, s: (0, 0)), pl.BlockSpec((1, D), lambda i, s: (0, 0)),
                   pl.BlockSpec((tm, D), row), pl.BlockSpec((tm, D), row),
                   pl.BlockSpec((None, tm, FS), sl), pl.BlockSpec((None, tm, FS), sl), pl.BlockSpec((None, tm, FS), sl)],
        out_shape=[jax.ShapeDtypeStruct((8, 128), f32), jax.ShapeDtypeStruct((1, D), f32),
                   jax.ShapeDtypeStruct((T, D), f32), jax.ShapeDtypeStruct((T, D), bf16)]
        + [jax.ShapeDtypeStruct((NSH, T, FS), bf16)] * 3,
        scratch_shapes=[pltpu.VMEM((tm, D), f32)],
        compiler_params=_cp("arbitrary", "arbitrary"),
    )(h2, x1, wg, wu, wd, tgt, gf)


def _mlp_bwd(dx2b, dx2, ud, silu, wg, wu, wd, x1, g2):
    tm = 512

    def body(dxb_ref, dx_ref, ud_ref, silu_ref, wg_ref, wu_ref, wd_ref, x1_ref, g_ref,
             dg_ref, du_ref, dx1_ref, dx1b_ref, dn_ref, acc):
        i = pl.program_id(0)
        s = pl.program_id(1)

        @pl.when(jnp.logical_and(i == 0, s == 0))
        def _():
            dn_ref[...] = jnp.zeros_like(dn_ref)

        dxb = dxb_ref[...]
        das = [lax.dot_general(dxb, wd_ref[s, c0:c1, :], NT, preferred_element_type=f32) for c0, c1 in FS_CHUNKS]
        for (c0, c1), da in zip(FS_CHUNKS, das):
            dg_ref[:, c0:c1] = (da * ud_ref[:, c0:c1].astype(f32)).astype(bf16)
            du_ref[:, c0:c1] = (da * silu_ref[:, c0:c1].astype(f32)).astype(bf16)
        part = jnp.dot(dg_ref[...], wg_ref[s], preferred_element_type=f32)
        part = part + jnp.dot(du_ref[...], wu_ref[s], preferred_element_type=f32)

        @pl.when(s == 0)
        def _():
            acc[...] = part

        @pl.when(s > 0)
        def _():
            acc[...] += part

        @pl.when(s == NSH - 1)
        def _():
            xv = x1_ref[...]
            r = lax.rsqrt(jnp.mean(xv * xv, axis=-1, keepdims=True) + EPS)
            xhat = xv * r
            dh = acc[...]
            dn_ref[...] += jnp.sum(dh * xhat, axis=0, keepdims=True)
            z = dh * g_ref[...]
            dx1 = dx_ref[...] + r * (z - xhat * jnp.mean(z * xhat, axis=-1, keepdims=True))
            dx1_ref[...] = dx1
            dx1b_ref[...] = dx1.astype(bf16)

    row = lambda i, s: (i, 0)
    sl = lambda i, s: (s, i, 0)
    wsl = lambda i, s: (s, 0, 0)
    return pl.pallas_call(
        body, name="mlp_bwd", grid=(T // tm, NSH),
        in_specs=[pl.BlockSpec((tm, D), row), pl.BlockSpec((tm, D), row),
                  pl.BlockSpec((None, tm, FS), sl), pl.BlockSpec((None, tm, FS), sl),
                  _resident((NSH, FS, D)), _resident((NSH, FS, D)), _resident((NSH, FS, D)),
                  pl.BlockSpec((tm, D), row), pl.BlockSpec((1, D), lambda i, s: (0, 0))],
        out_specs=[pl.BlockSpec((None, tm, FS), sl), pl.BlockSpec((None, tm, FS), sl),
                   pl.BlockSpec((tm, D), row), pl.BlockSpec((tm, D), row), pl.BlockSpec((1, D), lambda i, s: (0, 0))],
        out_shape=[jax.ShapeDtypeStruct((NSH, T, FS), bf16)] * 2
        + [jax.ShapeDtypeStruct((T, D), f32), jax.ShapeDtypeStruct((T, D), bf16), jax.ShapeDtypeStruct((1, D), f32)],
        scratch_shapes=[pltpu.VMEM((tm, D), f32)],
        compiler_params=_cp("arbitrary", "arbitrary"),
    )(dx2b, dx2, ud, silu, wg, wu, wd, x1, g2)


def _mm_tn(a, bs, name, a_sharded=False, b_sharded=False, tk=512, out_dtype=bf16):
    nb = len(bs)
    sh = NSH if (a_sharded or b_sharded) else 1
    m = a.shape[-1]
    nk = T // tk

    def body(a_ref, *refs):
        kk = pl.program_id(1)
        av = a_ref[...]
        for b_ref, o_ref, acc in zip(refs[:nb], refs[nb:2 * nb], refs[2 * nb:]):
            upd = lax.dot_general(av, b_ref[...], TN, preferred_element_type=f32)

            @pl.when(kk == 0)
            def _():
                acc[...] = upd

            @pl.when(kk > 0)
            def _():
                acc[...] += upd

            @pl.when(kk == nk - 1)
            def _():
                o_ref[...] = acc[...].astype(out_dtype)

    a_spec = (pl.BlockSpec((None, tk, m), lambda s, k: (s, k, 0)) if a_sharded
              else pl.BlockSpec((tk, m), lambda s, k: (k, 0)))
    b_specs, o_specs, o_shapes, scratch = [], [], [], []
    for b in bs:
        n = b.shape[-1]
        b_specs.append(pl.BlockSpec((None, tk, n), lambda s, k: (s, k, 0)) if b_sharded
                       else pl.BlockSpec((tk, n), lambda s, k: (k, 0)))
        scratch.append(pltpu.VMEM((m, n), f32))
        if sh > 1:
            o_specs.append(pl.BlockSpec((None, m, n), lambda s, k: (s, 0, 0)))
            o_shapes.append(jax.ShapeDtypeStruct((sh, m, n), out_dtype))
        else:
            o_specs.append(pl.BlockSpec((m, n), lambda s, k: (0, 0)))
            o_shapes.append(jax.ShapeDtypeStruct((m, n), out_dtype))
    return pl.pallas_call(
        body, name=name, grid=(sh, nk), in_specs=[a_spec] + b_specs, out_specs=o_specs, out_shape=o_shapes,
        scratch_shapes=scratch, compiler_params=_cp("arbitrary", "arbitrary"),
    )(a, *bs)


def _mm_tn_rows(a_list, b, name, tk=1024, out_dtype=bf16, stacked=False):
    na = len(a_list)
    n = b.shape[-1]
    nk = T // tk
    ms = [a.shape[-1] for a in a_list]
    starts = [sum(ms[:k]) for k in range(na)] if stacked else [0] * na
    out_rows = [sum(ms)] if stacked else ms

    def body(*refs):
        a_refs, b_ref = refs[:na], refs[na]
        no = len(out_rows)
        o_refs, accs = refs[na + 1:na + 1 + no], refs[na + 1 + no:]
        kk = pl.program_id(0)
        bv = b_ref[...]
        for k, a_ref in enumerate(a_refs):
            o_ref, acc = (o_refs[0], accs[0]) if stacked else (o_refs[k], accs[k])
            rows = pl.ds(starts[k], ms[k])
            upd = lax.dot_general(a_ref[...], bv, TN, preferred_element_type=f32)

            @pl.when(kk == 0)
            def _():
                acc[rows, :] = upd

            @pl.when(kk > 0)
            def _():
                acc[rows, :] += upd

            @pl.when(kk == nk - 1)
            def _():
                o_ref[rows, :] = acc[rows, :].astype(out_dtype)

    return pl.pallas_call(
        body, name=name, grid=(nk,),
        in_specs=[pl.BlockSpec((tk, m), lambda k: (k, 0)) for m in ms] + [pl.BlockSpec((tk, n), lambda k: (k, 0))],
        out_specs=[pl.BlockSpec((r, n), lambda k: (0, 0)) for r in out_rows],
        out_shape=[jax.ShapeDtypeStruct((r, n), out_dtype) for r in out_rows],
        scratch_shapes=[pltpu.VMEM((r, n), f32) for r in out_rows],
        compiler_params=_cp("arbitrary"),
    )(*a_list, b)


def _outproj_bwd(dx1b, wo):
    tm = 1024

    def body(dx_ref, wo_ref, da_ref, dp_ref):
        dx = dx_ref[...]
        da_ref[...] = lax.dot_general(dx, wo_ref[0:AW, :], NT, preferred_element_type=f32).astype(bf16)
        dp_ref[...] = lax.dot_general(dx, wo_ref[AW:2 * AW, :], NT, preferred_element_type=f32)

    return pl.pallas_call(
        body, name="outproj_bwd", grid=(T // tm,),
        in_specs=[pl.BlockSpec((tm, D), lambda i: (i, 0)), _full((D, D))],
        out_specs=[pl.BlockSpec((tm, AW), lambda i: (i, 0)), pl.BlockSpec((tm, AW), lambda i: (i, 0))],
        out_shape=[jax.ShapeDtypeStruct((T, AW), bf16), jax.ShapeDtypeStruct((T, AW), f32)],
        compiler_params=_cp("parallel"),
    )(dx1b, wo)


def _pool_bwd(dpool, pooled, wp, scale):
    tm = 1024
    n = T // tm

    def body(dp_ref, pb_ref, wp_ref, sc_ref, du_ref, dsc_ref, dwp_ref, ext, lv):
        i = pl.program_id(0)

        @pl.when(i == 0)
        def _():
            ext[tm:tm + HALO, :] = jnp.zeros((HALO, AW), f32)
            dsc_ref[...] = jnp.zeros_like(dsc_ref)
            dwp_ref[...] = jnp.zeros_like(dwp_ref)

        t_idx = (n - 1 - i) * tm + lax.broadcasted_iota(jnp.int32, (tm, 1), 0)
        for g, w in enumerate(WINDOWS):
            lo, hi = 128 * g, 128 * (g + 1)
            pb = pb_ref[:, lo:hi]
            mixed = jnp.dot(pb, wp_ref[g], preferred_element_type=f32)
            dpo = dp_ref[:, lo:hi]
            dsc_ref[:, lo:hi] += jnp.sum(dpo * mixed, axis=0, keepdims=True)
            dmr = (dpo * sc_ref[:, lo:hi]).astype(bf16)
            dwp_ref[g] += lax.dot_general(pb, dmr, TN, preferred_element_type=f32)
            dpl = lax.dot_general(dmr, wp_ref[g], NT, preferred_element_type=f32)
            cnt = jnp.minimum(t_idx + 1, w).astype(f32)
            dpn = dpl / cnt
            ext[0:tm, lo:hi] = dpn
            steps = w.bit_length() - 1
            assert 1 << steps == w and 8 * steps <= HALO
            for k in range(1, steps + 1):
                ahead = 1 << (k - 1)
                last = tm + HALO - 8 * k if k < steps else tm
                rd = (lambda a, b: ext[a:b, lo:hi]) if k == 1 else (lambda a, b, s=k % 2: lv[s, a:b, :])
                cur = rd(0, last) + rd(ahead, last + ahead)
                if k < steps:
                    lv[(k + 1) % 2, 0:last, :] = cur
            du_ref[:, lo:hi] = (cur - dpl).astype(bf16)
        ext[tm:tm + HALO, :] = ext[0:HALO, :]

    rev = lambda i: (n - 1 - i, 0)
    return pl.pallas_call(
        body, name="pool_bwd", grid=(n,),
        in_specs=[pl.BlockSpec((tm, AW), rev), pl.BlockSpec((tm, AW), rev), _full((4, 128, 128)), _full((1, AW))],
        out_specs=[pl.BlockSpec((tm, AW), rev), _full((1, AW)), _full((4, 128, 128))],
        out_shape=[jax.ShapeDtypeStruct((T, AW), bf16), jax.ShapeDtypeStruct((1, AW), f32),
                   jax.ShapeDtypeStruct((4, 128, 128), f32)],
        scratch_shapes=[pltpu.VMEM((tm + HALO, AW), f32), pltpu.VMEM((2, tm + HALO, 128), f32)],
        compiler_params=_cp("arbitrary"),
    )(dpool, pooled, wp, scale)


def _attn_bwd(qkv, qaug, kaug, attn, dattn, lse, dep):
    tq = tk = ATT_T
    n = T // tq
    it, jt = _causal_steps(True)
    nsteps = it.shape[0]

    rs = 64

    def body(it_ref, jt_ref, q_ref, k_ref, v_ref, qa_ref, ka_ref, o_ref, do_ref, lse_ref, dep_ref,
             dq_ref, dqs_ref, dk_ref, dks_ref, dv_ref, dq_acc, dk_acc, dv_acc, s_sc, dp_sc, p_sc, ds_sc):
        t = pl.program_id(1)
        i = it_ref[t]
        j = jt_ref[t]

        @pl.when(t == 0)
        def _():
            dq_acc[...] = jnp.zeros_like(dq_acc)

        @pl.when(i == j)
        def _():
            dk_acc[...] = jnp.zeros_like(dk_acc)
            dv_acc[...] = jnp.zeros_like(dv_acc)

        lane = lax.broadcasted_iota(jnp.int32, (tq, 128), 1)

        def step(on_diagonal):
            q = (q_ref[...].astype(f32) * Q_SCALE).astype(bf16)
            k = k_ref[...]
            v = v_ref[...]
            qa = qa_ref[...]
            ka = ka_ref[...]
            do = do_ref[...]
            dd = do.astype(f32) * o_ref[...].astype(f32)
            blocks = _row_blocks(tq, tk, on_diagonal)
            qes, kes, does, deltas = [], [], [], []
            for e in range(2):
                hm = (lane >= 64) if e else (lane < 64)
                qes.append(jnp.where(hm, q, qa))
                kes.append(jnp.where(hm, k, ka))
                does.append(jnp.where(hm, do, jnp.zeros_like(do)))
                deltas.append(jnp.sum(jnp.where(hm, dd, 0.0), axis=1, keepdims=True))
                for r0, r1, nc in blocks:
                    s_sc[e, r0:r1, 0:nc] = lax.dot_general(qes[e][r0:r1], kes[e][0:nc], NT, preferred_element_type=f32)
                    dp_sc[e, r0:r1, 0:nc] = lax.dot_general(does[e][r0:r1], v[0:nc], NT, preferred_element_type=f32)
            for e in range(2):
                for r0, r1, nc in blocks:
                    for r in range(r0, r1, rs):
                        s = s_sc[e, r:r + rs, 0:nc] - lse_ref[r:r + rs, 64 * e:64 * e + 1]
                        if on_diagonal:
                            row = lax.broadcasted_iota(jnp.int32, (rs, nc), 0) + r
                            col = lax.broadcasted_iota(jnp.int32, (rs, nc), 1)
                            s = jnp.where(col <= row, s, NEG)
                        p = jnp.exp2(s)
                        p_sc[e, r:r + rs, 0:nc] = p.astype(bf16)
                        ds_sc[e, r:r + rs, 0:nc] = (p * (dp_sc[e, r:r + rs, 0:nc] - deltas[e][r:r + rs, :])).astype(bf16)
                for r0, r1, nc in blocks:
                    dv_acc[:, 0:nc] += lax.dot_general(does[e][r0:r1], p_sc[e, r0:r1, 0:nc], TN, preferred_element_type=f32)
                    dsb = ds_sc[e, r0:r1, 0:nc]
                    dk_acc[e, :, 0:nc] += lax.dot_general(qes[e][r0:r1], dsb, TN, preferred_element_type=f32)
                    rq = pl.multiple_of(i * tq + r0, r1 - r0)
                    dq_acc[e, pl.ds(rq, r1 - r0), :] += jnp.dot(dsb, kes[e][0:nc], preferred_element_type=f32)

        @pl.when(i > j)
        def _():
            step(False)

        @pl.when(i == j)
        def _():
            step(True)

        @pl.when(i == n - 1)
        def _():
            dk0 = dk_acc[0].T
            dk1 = dk_acc[1].T
            dk_ref[...] = (jnp.where(lane < 64, dk0, dk1) * (1.0 / LOG2E)).astype(bf16)
            dks_ref[...] = jnp.where(lane < 64, dk1, dk0)
            dv_ref[...] = dv_acc[...].T.astype(bf16)

        @pl.when(t == nsteps - 1)
        def _():
            lane_t = lax.broadcasted_iota(jnp.int32, (T, 128), 1)
            dq_ref[...] = (jnp.where(lane_t < 64, dq_acc[0], dq_acc[1]) * 0.125).astype(bf16)
            dqs_ref[...] = jnp.where(lane_t < 64, dq_acc[1], dq_acc[0])

    qmap = lambda p, t, it, jt: (it[t], p)
    grid_spec = pltpu.PrefetchScalarGridSpec(
        num_scalar_prefetch=2, grid=(PAIRS, nsteps),
        in_specs=[pl.BlockSpec((tq, 128), qmap),
                  pl.BlockSpec((tk, 128), lambda p, t, it, jt: (jt[t], PAIRS + p)),
                  pl.BlockSpec((tk, 128), lambda p, t, it, jt: (jt[t], 2 * PAIRS + p)),
                  pl.BlockSpec((tq, 128), qmap), pl.BlockSpec((tk, 128), lambda p, t, it, jt: (jt[t], p)),
                  pl.BlockSpec((tq, 128), qmap), pl.BlockSpec((tq, 128), qmap),
                  pl.BlockSpec((None, tq, 128), lambda p, t, it, jt: (p, it[t], 0)),
                  pl.BlockSpec((8, 128), lambda p, t, it, jt: (0, 0))],
        out_specs=[pl.BlockSpec((T, 128), lambda p, t, it, jt: (0, p)),
                   pl.BlockSpec((None, T, 128), lambda p, t, it, jt: (p, 0, 0)),
                   pl.BlockSpec((tk, 128), lambda p, t, it, jt: (jt[t], p)),
                   pl.BlockSpec((None, tk, 128), lambda p, t, it, jt: (p, jt[t], 0)),
                   pl.BlockSpec((tk, 128), lambda p, t, it, jt: (jt[t], p))],
        scratch_shapes=[pltpu.VMEM((2, T, 128), f32), pltpu.VMEM((2, 128, tk), f32), pltpu.VMEM((128, tk), f32),
                        pltpu.VMEM((2, tq, tk), f32), pltpu.VMEM((2, tq, tk), f32), pltpu.VMEM((2, tq, tk), bf16),
                        pltpu.VMEM((2, tq, tk), bf16)],
    )
    return pl.pallas_call(
        body, name="fox_attn_bwd", grid_spec=grid_spec,
        out_shape=[jax.ShapeDtypeStruct((T, AW), bf16), jax.ShapeDtypeStruct((PAIRS, T, 128), f32),
                   jax.ShapeDtypeStruct((T, AW), bf16), jax.ShapeDtypeStruct((PAIRS, T, 128), f32),
                   jax.ShapeDtypeStruct((T, AW), bf16)],
        compiler_params=_cp("parallel", "arbitrary"),
    )(it, jt, qkv, qkv, qkv, qaug, kaug, attn, dattn, lse, dep)


def _fox_cumsum_bwd(dqs, dks, fl, bfp):
    tb = CUMSUM_ROWS
    nb = T // tb

    def body(dqs_ref, dks_ref, fl_ref, b_ref, df_ref, db_ref, carry):
        i = pl.program_id(0)

        @pl.when(i == 0)
        def _():
            carry[...] = jnp.zeros_like(carry)
            db_ref[...] = jnp.zeros_like(db_ref)

        r = lax.broadcasted_iota(jnp.int32, (128, 128), 0)
        cc = lax.broadcasted_iota(jnp.int32, (128, 128), 1)
        lane = lax.broadcasted_iota(jnp.int32, (tb, 128), 1)
        even_at, odd_at = SPARE[0] + ROW_SUM_LANE, SPARE[1] + ROW_SUM_LANE
        both = jnp.zeros((tb, 128), f32)
        pick = jnp.zeros((128, 128), jnp.bool_)
        for p in range(PAIRS):
            diff = dqs_ref[p] - pltpu.roll(dks_ref[p], 128 - (COL_SUM_LANE - ROW_SUM_LANE), 1)
            moved = pltpu.roll(diff, p, 1) if p else diff
            both = jnp.where(jnp.logical_or(lane == even_at + p, lane == odd_at + p), moved, both)
            pick = jnp.logical_or(pick, jnp.logical_or(jnp.logical_and(r == even_at + p, cc == 2 * p),
                                                       jnp.logical_and(r == odd_at + p, cc == 2 * p + 1)))
        dc = _dot01(pick.astype(bf16), both, False)
        rt = lax.broadcasted_iota(jnp.int32, (tb, tb), 0)
        ct = lax.broadcasted_iota(jnp.int32, (tb, tb), 1)
        utri = (ct >= rt).astype(bf16)
        dl = _dot01(utri, dc, True) + carry[0:1, :]
        carry[...] = jnp.broadcast_to(dl[0:1, :], (8, 128))
        z = fl_ref[...] + b_ref[...]
        df = dl * jax.nn.sigmoid(-z)
        df_ref[...] = df.astype(bf16)
        db_ref[...] += jnp.sum(df, axis=0, keepdims=True)

    rev = lambda i: (nb - 1 - i, 0)
    return pl.pallas_call(
        body, name="fox_cumsum_bwd", grid=(nb,),
        in_specs=[pl.BlockSpec((PAIRS, tb, 128), lambda i: (0, nb - 1 - i, 0)),
                  pl.BlockSpec((PAIRS, tb, 128), lambda i: (0, nb - 1 - i, 0)),
                  pl.BlockSpec((tb, 128), rev), _full((1, 128))],
        out_specs=[pl.BlockSpec((tb, 128), rev), _full((1, 128))],
        out_shape=[jax.ShapeDtypeStruct((T, 128), bf16), jax.ShapeDtypeStruct((1, 128), f32)],
        scratch_shapes=[pltpu.VMEM((8, 128), f32)],
        compiler_params=_cp("arbitrary"),
    )(dqs, dks, fl, bfp)


def _inproj_bwd(dq, dk, dv, du, df, w, x, dx1, g1):
    tm = 512

    def body(dq_ref, dk_ref, dv_ref, du_ref, df_ref, w_ref, x_ref, dx1_ref, g_ref, dx_ref, dn_ref):
        i = pl.program_id(0)

        @pl.when(i == 0)
        def _():
            dn_ref[...] = jnp.zeros_like(dn_ref)

        dproj = jnp.concatenate([dq_ref[...], dk_ref[...], dv_ref[...], du_ref[...], df_ref[...]], axis=1)
        dh = jnp.dot(dproj, w_ref[...], preferred_element_type=f32)
        xv = x_ref[...]
        r = lax.rsqrt(jnp.mean(xv * xv, axis=-1, keepdims=True) + EPS)
        xhat = xv * r
        dn_ref[...] += jnp.sum(dh * xhat, axis=0, keepdims=True)
        z = dh * g_ref[...]
        dx_ref[...] = dx1_ref[...] + r * (z - xhat * jnp.mean(z * xhat, axis=-1, keepdims=True))

    row = lambda i: (i, 0)
    return pl.pallas_call(
        body, name="inproj_bwd", grid=(T // tm,),
        in_specs=[pl.BlockSpec((tm, AW), row)] * 4 + [pl.BlockSpec((tm, 128), row), _full((W_ROWS, D)),
                                                       pl.BlockSpec((tm, D), row), pl.BlockSpec((tm, D), row), _full((1, D))],
        out_specs=[pl.BlockSpec((tm, D), row), _full((1, D))],
        out_shape=[jax.ShapeDtypeStruct((T, D), f32), jax.ShapeDtypeStruct((1, D), f32)],
        compiler_params=_cp("arbitrary"),
    )(dq, dk, dv, du, df, w, x, dx1, g1)


def _adamw_math(w, g, m, v):
    m = B1 * m + (1.0 - B1) * g
    v = B2 * v + (1.0 - B2) * (g * g)
    m_hat = m / (1.0 - B1 ** STEP)
    v_hat = v / (1.0 - B2 ** STEP)
    delta = -LR * (m_hat / (jnp.sqrt(v_hat) + AEPS) + WD * w)
    return delta, m, v


SHARD_STEPS = 4


def _adamw_shards(ws, ms, vs, ps_mine, ps_other, name, steps):
    n = len(ws)

    def body(*refs):
        ins, outs = refs[:5 * n], refs[5 * n:]
        for k in range(n):
            w_ref, m_ref, v_ref, a_ref, b_ref = ins[5 * k:5 * k + 5]
            g_ref, d_ref, nm_ref, nv_ref = outs[4 * k:4 * k + 4]
            g = (a_ref[...].astype(f32) + b_ref[...].astype(f32)).reshape(w_ref.shape)
            g_ref[...] = g
            d_ref[...], nm_ref[...], nv_ref[...] = _adamw_math(w_ref[...], g, m_ref[...], v_ref[...])

    in_specs, out_specs, out_shape = [], [], []
    for w, p in zip(ws, ps_mine):
        rest = tuple(w.shape[1:])
        tr = w.shape[0] // steps
        assert tr * steps == w.shape[0]
        spec = pl.BlockSpec((tr,) + rest, lambda i, _n=len(rest): (i,) + (0,) * _n)
        pspec = pl.BlockSpec((tr, p.shape[1]), lambda i: (i, 0))
        in_specs += [spec] * 3 + [pspec] * 2
        out_specs += [spec] * 4
        out_shape += [jax.ShapeDtypeStruct(w.shape, f32)] * 4
    args = [a for k in range(n) for a in (ws[k], ms[k], vs[k], ps_mine[k], ps_other[k])]
    res = pl.pallas_call(
        body, name=name, grid=(steps,), in_specs=in_specs, out_specs=out_specs, out_shape=out_shape,
        compiler_params=_cp("parallel"),
    )(*args)
    return [res[4 * k:4 * k + 4] for k in range(n)]


SMALL_SLOTS = ((0, 8, 128), (8, 16, 128), (16, 24, 128), (24, 28, 128), (32, 33, 8))
LOSS_ROW = 39


def _adamw_small(ws, ms, vs, parts, parts_wp):
    n = len(ws)

    def body(*refs):
        w_refs, m_refs, v_refs = refs[0:n], refs[n:2 * n], refs[2 * n:3 * n]
        p_ref, pw_ref = refs[3 * n], refs[3 * n + 1]
        outs = refs[3 * n + 2:]
        g_all = p_ref[0]
        g_wp = pw_ref[0]
        for k in range(1, 8):
            g_all = g_all + p_ref[k]
            g_wp = g_wp + pw_ref[k]
        grads = [g_all[r0:r1, 0:lanes] for r0, r1, lanes in SMALL_SLOTS] + [g_wp]
        for idx, g in enumerate(grads):
            d, nm, nv = _adamw_math(w_refs[idx][...], g, m_refs[idx][...], v_refs[idx][...])
            outs[idx][...] = g
            outs[n + idx][...] = d
            outs[2 * n + idx][...] = nm
            outs[3 * n + idx][...] = nv
        outs[4 * n][...] = g_all[LOSS_ROW:LOSS_ROW + 1, :]

    shapes = [jax.ShapeDtypeStruct(w.shape, f32) for w in ws]
    res = pl.pallas_call(
        body, name="adamw_small", out_shape=shapes * 4 + [jax.ShapeDtypeStruct((1, 128), f32)],
    )(*ws, *ms, *vs, parts, parts_wp)
    return res[:4 * n], res[4 * n]


def _sum4(recvs, gs, mine, name, steps):
    n = len(recvs)

    def body(mine_ref, *refs):
        for r_ref, g_ref, o_ref in zip(refs[:n], refs[n:2 * n], refs[2 * n:]):
            o_ref[...] = ((g_ref[...].astype(f32) + r_ref[0].astype(f32))
                          + (r_ref[1].astype(f32) + r_ref[2].astype(f32))).astype(bf16)

    r_specs, g_specs, o_specs, shapes = [], [], [], []
    for recv in recvs:
        _, rows, cols = recv.shape
        tr = rows // steps
        assert tr * steps == rows
        r_specs.append(pl.BlockSpec((3, tr, cols), lambda i, m: (0, i, 0)))
        g_specs.append(pl.BlockSpec((None, tr, cols), lambda i, m: (m[0], i, 0)))
        o_specs.append(pl.BlockSpec((tr, cols), lambda i, m: (i, 0)))
        shapes.append(jax.ShapeDtypeStruct((rows, cols), bf16))
    grid_spec = pltpu.PrefetchScalarGridSpec(num_scalar_prefetch=1, grid=(steps,), in_specs=r_specs + g_specs,
                                             out_specs=o_specs)
    return pl.pallas_call(
        body, name=name, grid_spec=grid_spec, out_shape=shapes, compiler_params=_cp("arbitrary"),
    )(mine, *recvs, *gs)


_HBM = pl.BlockSpec(memory_space=pltpu.HBM)
_SEM = pl.BlockSpec(memory_space=pltpu.SEMAPHORE)
_EFFECT = pltpu.SideEffectType.DATAFLOW_SIDE_EFFECTING


def _in_hbm(a):
    return pltpu.with_memory_space_constraint(a, pltpu.HBM)


def _mesh_pos():
    return lax.axis_index("x"), lax.axis_index("y"), lax.axis_index("c")


def _other_chips(x, y):
    return [(1 - x, y), (x, 1 - y), (1 - x, 1 - y)]


def _gather_copy(srcs, lands, send_sems, recv_sems, a, k, slot):
    x, y, c = _mesh_pos()
    cx, cy = _other_chips(x, y)[k]
    return pltpu.make_async_remote_copy(
        src_ref=srcs[a], dst_ref=lands[a].at[slot], send_sem=send_sems.at[3 * a + k], recv_sem=recv_sems.at[3 * a + k],
        device_id=(cx, cy, c), device_id_type=MESH)


def _scatter_copy(srcs, lands, send_sems, recv_sems, a, k):
    x, y, c = _mesh_pos()
    cx, cy = _other_chips(x, y)[k]
    return pltpu.make_async_remote_copy(
        src_ref=srcs[a].at[2 * cx + cy], dst_ref=lands[a].at[k], send_sem=send_sems.at[3 * a + k],
        recv_sem=recv_sems.at[3 * a + k], device_id=(cx, cy, c), device_id_type=MESH)


def _all_gather_w_in(part):
    cols = part.shape[1] // 2

    def body(src, dst, send_sems, recv_sems, loc_sem):
        x, y, c = _mesh_pos()
        mine = 2 * x + y
        chips = _other_chips(x, y)
        half = lambda ref, cc: ref.at[:, pl.ds(pl.multiple_of(cc * cols, cols), cols)]

        def over_ici(k, slot):
            cx, cy = chips[k]
            return pltpu.make_async_remote_copy(
                src_ref=half(src, c), dst_ref=half(dst.at[slot], c), send_sem=send_sems.at[k], recv_sem=recv_sems.at[k],
                device_id=(cx, cy, c), device_id_type=MESH)

        def to_sibling(k, cc):
            slot = 2 * chips[k][0] + chips[k][1]
            return pltpu.make_async_remote_copy(
                src_ref=half(dst.at[slot], cc), dst_ref=half(dst.at[slot], cc), send_sem=send_sems.at[3 + k],
                recv_sem=recv_sems.at[3 + k], device_id=(x, y, 1 - c), device_id_type=MESH)

        local = pltpu.make_async_copy(src, dst.at[mine], loc_sem.at[0])
        local.start()
        first = [over_ici(k, mine) for k in range(3)]
        for cp in first:
            cp.start()
        passed = [to_sibling(k, c) for k in range(3)]
        for k in range(3):
            over_ici(k, 2 * chips[k][0] + chips[k][1]).wait_recv()
            passed[k].start()
        for k in range(3):
            to_sibling(k, 1 - c).wait_recv()
        for cp in first + passed:
            cp.wait_send()
        local.wait()

    return pl.pallas_call(
        body, name="all_gather_w_in", in_specs=[_HBM], out_specs=_HBM,
        out_shape=jax.ShapeDtypeStruct((NSH,) + part.shape, part.dtype),
        scratch_shapes=[pltpu.SemaphoreType.DMA((6,)), pltpu.SemaphoreType.DMA((6,)), pltpu.SemaphoreType.DMA((1,))],
    )(part)


def _split_start(name, srcs, lands, n_sems, plan, dep):
    n, nl = len(srcs), len(lands)

    def body(*refs):
        src_refs, land_refs = refs[:n], refs[n:n + nl]
        send_sems, recv_sems = refs[n + nl + 1], refs[n + nl + 2]
        token = refs[-1]
        sends, _, own = plan(src_refs, land_refs, send_sems, recv_sems)
        for cp in own + sends:
            cp.start()
        token[...] = jnp.zeros_like(token)

    outs = pl.pallas_call(
        body, name=name,
        in_specs=[_HBM] * (n + nl) + [pl.BlockSpec(memory_space=pl.ANY)],
        out_specs=[_SEM, _SEM] + [_HBM] * (n + nl) + [pl.BlockSpec(memory_space=pltpu.VMEM)],
        out_shape=[pltpu.SemaphoreType.DMA((n_sems,)), pltpu.SemaphoreType.DMA((n_sems,))]
        + [pltpu.HBM(a.shape, a.dtype) for a in list(srcs) + list(lands)] + [jax.ShapeDtypeStruct((8, 128), f32)],
        input_output_aliases={i: 2 + i for i in range(n + nl)},
        compiler_params=pltpu.CompilerParams(has_side_effects=_EFFECT),
    )(*[_in_hbm(a) for a in list(srcs) + list(lands)], dep)
    return outs[0], outs[1], list(outs[2:2 + n]), list(outs[2 + n:2 + n + nl]), outs[-1]


def _split_wait(name, send_sems, recv_sems, srcs, lands, after, plan):
    n, nl = len(srcs), len(lands)

    def body(*refs):
        src_refs, land_refs = refs[:n], refs[n:n + nl]
        s_sems, r_sems = refs[n + nl], refs[n + nl + 1]
        sends, recvs, own = plan(src_refs, land_refs, s_sems, r_sems)
        for cp in own:
            cp.wait()
        for cp in recvs:
            cp.wait_recv()
        for cp in sends:
            cp.wait_send()

    outs = pl.pallas_call(
        body, name=name,
        in_specs=[_HBM] * (n + nl) + [_SEM, _SEM, pl.BlockSpec(memory_space=pl.ANY)],
        out_specs=[_HBM] * (n + nl),
        out_shape=[pltpu.HBM(a.shape, a.dtype) for a in list(srcs) + list(lands)],
        input_output_aliases={i: i for i in range(n + nl)},
        compiler_params=pltpu.CompilerParams(has_side_effects=_EFFECT),
    )(*srcs, *lands, send_sems, recv_sems, after)
    return list(outs[:n]), list(outs[n:])


def _gather_plan(srcs, lands, ss, rs):
    x, y, _ = _mesh_pos()
    chips = _other_chips(x, y)
    sends = [_gather_copy(srcs, lands, ss, rs, a, k, 2 * x + y) for a in range(len(srcs)) for k in range(3)]
    recvs = [_gather_copy(srcs, lands, ss, rs, a, k, 2 * chips[k][0] + chips[k][1])
             for a in range(len(srcs)) for k in range(3)]
    own = [pltpu.make_async_copy(srcs[a], lands[a].at[2 * x + y], rs.at[3 * len(srcs) + a]) for a in range(len(srcs))]
    return sends, recvs, own


def _scatter_and_spread_plan(srcs, lands, ss, rs):
    x, y, c = _mesh_pos()
    me = 4 * x + 2 * y + c
    n = len(srcs) - 1
    cps = [_scatter_copy(srcs[:n], lands[:n], ss, rs, a, k) for a in range(n) for k in range(3)]
    for f in range(1, 8):
        peer = ((x + (f >> 2)) % 2, (y + ((f >> 1) & 1)) % 2, (c + (f & 1)) % 2)
        cps.append(pltpu.make_async_remote_copy(
            src_ref=srcs[n], dst_ref=lands[n].at[me], send_sem=ss.at[3 * n - 1 + f], recv_sem=rs.at[3 * n - 1 + f],
            device_id=peer, device_id_type=MESH))
    own = [pltpu.make_async_copy(srcs[n], lands[n].at[me], rs.at[3 * n + 7])]
    return cps, cps, own


def _swap_with_sibling(parts, name):
    n = len(parts)

    def body(*refs):
        srcs, dsts = refs[:n], refs[n:2 * n]
        send_sems, recv_sems = refs[2 * n:]
        x, y, c = _mesh_pos()
        cps = [pltpu.make_async_remote_copy(src_ref=srcs[a], dst_ref=dsts[a], send_sem=send_sems.at[a],
                                            recv_sem=recv_sems.at[a], device_id=(x, y, 1 - c), device_id_type=MESH)
               for a in range(n)]
        for cp in cps:
            cp.start()
        for cp in cps:
            cp.wait_recv()
        for cp in cps:
            cp.wait_send()

    return pl.pallas_call(
        body, name=name, in_specs=[_HBM] * n, out_specs=[_HBM] * n,
        out_shape=[jax.ShapeDtypeStruct(p.shape, p.dtype) for p in parts],
        scratch_shapes=[pltpu.SemaphoreType.DMA((n,)), pltpu.SemaphoreType.DMA((n,))],
    )(*parts)


def _forward(x, tgt, w_in_t, mlp_w_fn, g1, bfp, wp, scale, g2, gf, dep):
    h, qkv, u, fl = _rms_inproj(x, g1, w_in_t, dep)
    qaug, kaug = _fox_cumsum(fl, bfp)
    attn, lse = _attn_fwd(qkv, qaug, kaug)
    pooled, pool = _pool_fwd(u, wp, scale)
    wo, wgt, wut, wd = mlp_w_fn(attn)
    x1, h2 = _outproj(x, attn, pool, wo, g2)
    loss, dgf, dx2, dx2b, ud, silu, a_b = _mlp_fwd_loss(h2, x1, wgt, wut, wd, tgt, gf)
    saved = dict(h=h, qkv=qkv, fl=fl, qaug=qaug, kaug=kaug, attn=attn, lse=lse, pooled=pooled, pool=pool, x1=x1, h2=h2,
                 ud=ud, silu=silu, a_b=a_b, wo=wo, wgt=wgt, wut=wut, wd=wd)
    return loss, dgf, dx2, dx2b, saved


def _backward_mlp(sv, dx2, dx2b, g2):
    dgate, dup, dx1, dx1b, dg2 = _mlp_bwd(dx2b, dx2, sv["ud"], sv["silu"], sv["wgt"], sv["wut"], sv["wd"], sv["x1"], g2)
    (dwd,) = _mm_tn(sv["a_b"], [dx2b], "dw_down", a_sharded=True, tk=T)
    (dwgt,) = _mm_tn(dgate, [sv["h2"]], "dw_gate", a_sharded=True, tk=T)
    (dwut,) = _mm_tn(dup, [sv["h2"]], "dw_up", a_sharded=True, tk=T)
    return dx1, dx1b, dg2, (dwgt, dwut, dwd)


def _backward_outproj_pool(sv, dx1b, wp, scale):
    dattn, dpool = _outproj_bwd(dx1b, sv["wo"])
    (dwo,) = _mm_tn_rows([sv["attn"], sv["pool"]], dx1b, "dw_out", tk=2048, stacked=True)
    dwo = dwo.reshape(NSH, D // NSH, D)
    du, dscale, dwp = _pool_bwd(dpool, sv["pooled"], wp, scale)
    return dattn, dwo, du, dscale, dwp


def _backward_attn_inproj(sv, x, dx1, dattn, du, w_in_t, g1, bfp, dep):
    dq, dqs, dk, dks, dv = _attn_bwd(sv["qkv"], sv["qaug"], sv["kaug"], sv["attn"], dattn, sv["lse"], dep)
    df, dbf = _fox_cumsum_bwd(dqs, dks, sv["fl"], bfp)
    dx, dg1 = _inproj_bwd(dq, dk, dv, du, df, w_in_t, x, dx1, g1)
    dwq, dwk, dwv, dwu_in, dwf = _mm_tn_rows([dq, dk, dv, du, df], sv["h"], "dw_in")
    dwin = jnp.concatenate([dwq, dwk, dwv, dwf[0:8], dwu_in], axis=0)
    return dx, dg1, dbf, dwin.reshape(NSH, IN_S, D)


def kernel(x, norm1_g, w_in, b_forget, w_pool, pool_scale, w_out, norm2_g, w_gate, w_up, w_down, final_g, loss_target, m_norm1_g, m_w_in, m_b_forget, m_w_pool, m_pool_scale, m_w_out, m_norm2_g, m_w_gate, m_w_up, m_w_down, m_final_g, v_norm1_g, v_w_in, v_b_forget, v_w_pool, v_pool_scale, v_w_out, v_norm2_g, v_w_gate, v_w_up, v_w_down, v_final_g):
    mine = (2 * lax.axis_index("x") + lax.axis_index("y")).astype(jnp.int32)
    mine1 = mine.reshape(1)
    tr = lambda a: jnp.transpose(a[0])

    win4 = _all_gather_w_in(tr(w_in).astype(bf16))
    later = [w_out[0].astype(bf16), tr(w_gate).astype(bf16), tr(w_up).astype(bf16), w_down[0].astype(bf16)]
    lands = [lax.empty((NSH,) + p.shape, bf16) for p in later]
    ag_send, ag_recv, later_thru, lands_thru, ag_token = _split_start("all_gather_start", later, lands, 16, _gather_plan,
                                                                      win4)
    win = win4.reshape(IN_W, D)
    w_in_t = jnp.concatenate([win[0:3 * AW], win[3 * AW + 8:], win[3 * AW:3 * AW + 8], jnp.zeros((120, D), bf16)], axis=0)
    bfp = jnp.pad(b_forget, ((0, 0), (0, 120)))
    wp = w_pool[0].astype(bf16)
    gf = final_g.reshape(1, D)

    def later_weights(after):
        _, (wo4, wgt, wut, wd) = _split_wait("all_gather_wait", ag_send, ag_recv, later_thru, lands_thru, after, _gather_plan)
        return wo4.reshape(D, D), wgt, wut, wd

    xe, tgt = x[0], loss_target[0]
    loss_v, dgf, dx2, dx2b, sv = _forward(xe, tgt, w_in_t, later_weights, norm1_g, bfp, wp, pool_scale, norm2_g, gf, ag_token)
    dx1, dx1b, dg2, mlp_grads = _backward_mlp(sv, dx2, dx2b, norm2_g)
    dattn, dwo, du, dscale, dwp = _backward_outproj_pool(sv, dx1b, wp, pool_scale)
    first = [dwo] + list(mlp_grads) + [dwp.reshape(512, 128)]
    first_lands = [lax.empty((3,) + g.shape[1:], bf16) for g in first[:4]] + [lax.empty((8, 512, 128), f32)]
    rs_send, rs_recv, first_thru, first_lands_thru, rs_token = _split_start(
        "reduce_scatter_start", first, first_lands, 20, _scatter_and_spread_plan, du)
    dx, dg1, dbf, dwin = _backward_attn_inproj(sv, xe, dx1, dattn, du, w_in_t, norm1_g, bfp, rs_token)

    pad8 = lambda r: jnp.pad(r, ((0, 8 - r.shape[0]), (0, 0)))
    loss_rows = jnp.concatenate([dbf, jnp.zeros((6, 128), f32), loss_v[0:1, :]], axis=0)
    small = jnp.concatenate([dg1.reshape(8, 128), dg2.reshape(8, 128), dgf.reshape(8, 128), pad8(dscale.reshape(4, 128)),
                             loss_rows], axis=0)
    tail_send, tail_recv, tail_thru, tail_lands_thru, tail_token = _split_start(
        "tail_start", [dwin, small], [lax.empty((3,) + dwin.shape[1:], bf16), lax.empty((8, SMALL_ROWS, 128), f32)], 11,
        _scatter_and_spread_plan, dx)
    first_thru, first_recv = _split_wait("reduce_scatter_wait", rs_send, rs_recv, first_thru, first_lands_thru, tail_token,
                                         _scatter_and_spread_plan)
    wp_all = first_recv[4]
    tr3 = lambda a: jnp.transpose(a, (2, 0, 1))
    ws = [tr3(w_in), w_out[0], tr(w_gate), tr(w_up), w_down[0]]
    ms = [tr3(m_w_in), m_w_out[0], tr(m_w_gate), tr(m_w_up), m_w_down[0]]
    vs = [tr3(v_w_in), v_w_out[0], tr(v_w_gate), tr(v_w_up), v_w_down[0]]
    partial = _sum4(first_recv[:4], first_thru[:4], mine1, "sum4_first", SHARD_STEPS)
    other = _swap_with_sibling(partial, "swap_first")
    big = _adamw_shards(ws[1:], ms[1:], vs[1:], partial, other, "adamw_first", SHARD_STEPS)
    (dwin_thru, _), (in_recv_land, small_all) = _split_wait("tail_wait", tail_send, tail_recv, tail_thru, tail_lands_thru,
                                                            big[3][0], _scatter_and_spread_plan)
    partial_in = _sum4([in_recv_land], [dwin_thru], mine1, "sum4_in", 1)
    other_in = _swap_with_sibling(partial_in, "swap_in")
    big = _adamw_shards(ws[:1], ms[:1], vs[:1], partial_in, other_in, "adamw_in", 1) + big

    small_names = ["norm1_g", "norm2_g", "final_g", "pool_scale", "b_forget", "w_pool"]
    rows = lambda a, b, c, d, e, f: [a.reshape(8, 128), b.reshape(8, 128), c.reshape(8, 128), d.reshape(4, 128),
                                     e.reshape(1, 8), f.reshape(512, 128)]
    sm, loss_row = _adamw_small(rows(norm1_g, norm2_g, final_g, pool_scale, b_forget, w_pool),
                                rows(m_norm1_g, m_norm2_g, m_final_g, m_pool_scale, m_b_forget, m_w_pool),
                                rows(v_norm1_g, v_norm2_g, v_final_g, v_pool_scale, v_b_forget, v_w_pool), small_all, wp_all)
    small_shape = dict(norm1_g=(1, D), norm2_g=(1, D), final_g=(D,), pool_scale=(1, AW), b_forget=(1, 8),
                       w_pool=(1, 4, 128, 128))

    order = ["norm1_g", "w_in", "b_forget", "w_pool", "pool_scale", "w_out", "norm2_g", "w_gate", "w_up", "w_down", "final_g"]
    big_idx = {"w_in": 0, "w_out": 1, "w_gate": 2, "w_up": 3, "w_down": 4}
    outs = [loss_row[0, 0], dx[None]]
    for kind in range(4):
        for name in order:
            if name == "w_in":
                outs.append(jnp.transpose(big[0][kind], (1, 2, 0)))
            elif name in ("w_gate", "w_up"):
                outs.append(jnp.transpose(big[big_idx[name]][kind])[None])
            elif name in big_idx:
                outs.append(big[big_idx[name]][kind][None])
            else:
                outs.append(sm[6 * kind + small_names.index(name)].reshape(small_shape[name]))
    return tuple(outs)
```

```python
import jax
import jax.numpy as jnp
import numpy as np
from jax import lax
from jax.experimental import pallas as pl
from jax.experimental.pallas import tpu as pltpu

f32 = jnp.float32
bf16 = jnp.bfloat16

T = 4096
D = 1024
NSH = 4
IN_W = 2056
IN_S = IN_W // NSH
AW = 512
PAIRS = 4
SPARE = (64, 0)
ROW_SUM_LANE, COL_SUM_LANE = 0, 3
FF = 2816
FS = FF // NSH
WINDOWS = (2, 4, 8, 16)
HALO = 32
EPS = 1e-6
NEG = -1e30
LR, B1, B2, AEPS, WD, STEP = 0.001, 0.9, 0.999, 1e-08, 0.01, 10
SMALL_ROWS = 40

NT = (((1,), (1,)), ((), ()))
TN = (((0,), (0,)), ((), ()))

MESH = pl.DeviceIdType.MESH


def _cp(*sem):
    return pltpu.CompilerParams(dimension_semantics=sem)


def _full(shape):
    n = len(shape)
    return pl.BlockSpec(shape, lambda *_: (0,) * n)


def _resident(shape):
    n = len(shape)
    return pl.BlockSpec(shape, lambda *_: (0,) * n, pipeline_mode=pl.Buffered(1))


W_ROWS = 4 * AW + 128


def _rms_inproj(x, g1, w, dep):
    tm = 512

    def body(x_ref, g_ref, w_ref, dep_ref, h_ref, qkv_ref, u_ref, fl_ref):
        xv = x_ref[...]
        r = lax.rsqrt(jnp.mean(xv * xv, axis=-1, keepdims=True) + EPS)
        h = (xv * r * g_ref[...]).astype(bf16)
        h_ref[...] = h
        qkv_ref[...] = lax.dot_general(h, w_ref[0:3 * AW, :], NT, preferred_element_type=f32).astype(bf16)
        u_ref[...] = lax.dot_general(h, w_ref[3 * AW:4 * AW, :], NT, preferred_element_type=f32)
        fl_ref[...] = lax.dot_general(h, w_ref[4 * AW:W_ROWS, :], NT, preferred_element_type=f32)

    return pl.pallas_call(
        body, name="rms_inproj", grid=(T // tm,),
        in_specs=[pl.BlockSpec((tm, D), lambda i: (i, 0)), _full((1, D)), _full((W_ROWS, D)), _full((8, 128))],
        out_specs=[pl.BlockSpec((tm, D), lambda i: (i, 0)), pl.BlockSpec((tm, 3 * AW), lambda i: (i, 0)),
                   pl.BlockSpec((tm, AW), lambda i: (i, 0)), pl.BlockSpec((tm, 128), lambda i: (i, 0))],
        out_shape=[jax.ShapeDtypeStruct((T, D), bf16), jax.ShapeDtypeStruct((T, 3 * AW), bf16),
                   jax.ShapeDtypeStruct((T, AW), f32), jax.ShapeDtypeStruct((T, 128), f32)],
        compiler_params=_cp("parallel"),
    )(x, g1, w, dep)


CUMSUM_ROWS = 512
FS_CHUNKS = ((0, 256), (256, 512), (512, FS))


def _log_sigmoid(z):
    return jnp.minimum(z, 0.0) - jnp.log(1.0 + jnp.exp(-jnp.abs(z)))


def _split3(x):
    hi = x.astype(bf16)
    r1 = x - hi.astype(f32)
    mid = r1.astype(bf16)
    return hi, mid, (r1 - mid.astype(f32)).astype(bf16)


def _dot01(sel, x, sel_first):
    parts = _split3(x)
    if sel_first:
        return sum(jnp.dot(sel, p, preferred_element_type=f32) for p in parts)
    return sum(jnp.dot(p, sel, preferred_element_type=f32) for p in parts)


def _fox_cumsum(fl, bfp):
    tb = CUMSUM_ROWS
    nb = T // tb

    def body(fl_ref, b_ref, qa_ref, ka_ref, carry):
        i = pl.program_id(0)

        @pl.when(i == 0)
        def _():
            carry[...] = jnp.zeros_like(carry)

        lf = _log_sigmoid(fl_ref[...] + b_ref[...])
        r = lax.broadcasted_iota(jnp.int32, (tb, tb), 0)
        cc = lax.broadcasted_iota(jnp.int32, (tb, tb), 1)
        ltri = (cc <= r).astype(bf16)
        cb = _dot01(ltri, lf, True) + carry[0:1, :]
        carry[...] = jnp.broadcast_to(cb[tb - 1:tb, :], (8, 128))
        hi, mid, lo = _split3(cb * LOG2E)
        lane = lax.broadcasted_iota(jnp.int32, (tb, 128), 1)
        terms = jnp.where(lane < 8, hi.astype(f32),
                          jnp.where(lane < 16, pltpu.roll(mid.astype(f32), 8, 1), pltpu.roll(lo.astype(f32), 16, 1)))
        terms = terms.astype(bf16)
        src = lax.broadcasted_iota(jnp.int32, (128, AW), 0)
        col = lax.broadcasted_iota(jnp.int32, (128, AW), 1)
        head, term = src & 7, src >> 3
        base = 128 * (head >> 1) + jnp.where((head & 1) == 0, SPARE[0], SPARE[1])
        place = lambda off: jnp.logical_and(col == base + off + term, term < 3).astype(bf16)
        cq = jnp.dot(terms, place(0), preferred_element_type=f32)
        ck = jnp.dot(terms, place(3), preferred_element_type=f32)
        within = jnp.bitwise_and(lax.broadcasted_iota(jnp.int32, (tb, AW), 1), 63)
        qa_ref[...] = jnp.where(jnp.logical_and(within >= 3, within <= 5), 1.0, cq).astype(bf16)
        ka_ref[...] = jnp.where(within <= 2, 1.0, -ck).astype(bf16)

    return pl.pallas_call(
        body, name="fox_cumsum", grid=(nb,),
        in_specs=[pl.BlockSpec((tb, 128), lambda i: (i, 0)), _full((1, 128))],
        out_specs=[pl.BlockSpec((tb, AW), lambda i: (i, 0)), pl.BlockSpec((tb, AW), lambda i: (i, 0))],
        out_shape=[jax.ShapeDtypeStruct((T, AW), bf16), jax.ShapeDtypeStruct((T, AW), bf16)],
        scratch_shapes=[pltpu.VMEM((8, 128), f32)],
        compiler_params=_cp("arbitrary"),
    )(fl, bfp)


ATT_T = 512
LOG2E = 1.4426950408889634
Q_SCALE = 0.125 * LOG2E


def _causal_steps(key_major):
    n = T // ATT_T
    if key_major:
        pairs = [(i, j) for j in range(n) for i in range(j, n)]
    else:
        pairs = [(i, j) for i in range(n) for j in range(i + 1)]
    it = np.array([p[0] for p in pairs], np.int32)
    jt = np.array([p[1] for p in pairs], np.int32)
    return jnp.asarray(it), jnp.asarray(jt)


def _row_blocks(tq, tk, on_diagonal):
    return ((0, tq // 2, tk // 2), (tq // 2, tq, tk)) if on_diagonal else ((0, tq, tk),)


def _attn_fwd(qkv, qaug, kaug):
    tq = tk = ATT_T
    it, jt = _causal_steps(False)
    nsteps = it.shape[0]

    rs = 64

    def body(it_ref, jt_ref, q_ref, k_ref, v_ref, qa_ref, ka_ref, o_ref, lse_ref, m_sc, acc_sc, s_sc, p_sc, alpha_sc):
        t = pl.program_id(1)
        i = it_ref[t]
        j = jt_ref[t]

        @pl.when(j == 0)
        def _():
            m_sc[...] = jnp.full_like(m_sc, NEG)
            acc_sc[...] = jnp.zeros_like(acc_sc)

        lane = lax.broadcasted_iota(jnp.int32, (tq, 128), 1)
        spare = SPARE

        def step(on_diagonal):
            q = (q_ref[...].astype(f32) * Q_SCALE).astype(bf16)
            k = k_ref[...]
            v = v_ref[...]
            qa = qa_ref[...]
            ka = ka_ref[...]
            blocks = _row_blocks(tq, tk, on_diagonal)
            for e in range(2):
                hm = (lane >= 64) if e else (lane < 64)
                qe = jnp.where(hm, q, qa)
                ke = jnp.where(hm, k, ka)
                for r0, r1, nc in blocks:
                    s_sc[e, r0:r1, 0:nc] = lax.dot_general(qe[r0:r1], ke[0:nc], NT, preferred_element_type=f32)
            for e in range(2):
                for r0, r1, nc in blocks:
                    for r in range(r0, r1, rs):
                        s = s_sc[e, r:r + rs, 0:nc]
                        if on_diagonal:
                            row = lax.broadcasted_iota(jnp.int32, (rs, nc), 0) + r
                            col = lax.broadcasted_iota(jnp.int32, (rs, nc), 1)
                            s = jnp.where(col <= row, s, NEG)
                        m_prev = m_sc[e, r:r + rs, :]
                        m_new = jnp.maximum(m_prev, jnp.max(s, axis=1, keepdims=True))
                        p_sc[e, r:r + rs, 0:nc] = jnp.exp2(s - jnp.tile(m_new, (1, nc // 128))).astype(bf16)
                        alpha_sc[e, r:r + rs, :] = jnp.exp2(m_prev - m_new)
                        m_sc[e, r:r + rs, :] = m_new
            for e in range(2):
                hm = (lane >= 64) if e else (lane < 64)
                ve = jnp.where(hm, v, (lane == spare[e]).astype(bf16))
                for r0, r1, nc in blocks:
                    acc_sc[e, r0:r1] = (alpha_sc[e, r0:r1] * acc_sc[e, r0:r1]
                                        + jnp.dot(p_sc[e, r0:r1, 0:nc], ve[0:nc], preferred_element_type=f32))

        @pl.when(j < i)
        def _():
            step(False)

        @pl.when(j == i)
        def _():
            step(True)
            l0 = acc_sc[0][:, spare[0]:spare[0] + 1]
            l1 = acc_sc[1][:, spare[1]:spare[1] + 1]
            o_ref[...] = jnp.where(lane < 64, acc_sc[0] / l0, acc_sc[1] / l1).astype(bf16)
            lse_ref[...] = jnp.where(lane < 64, m_sc[0] + jnp.log2(l0), m_sc[1] + jnp.log2(l1))

    qmap = lambda p, t, it, jt: (it[t], p)
    kmap = lambda p, t, it, jt: (jt[t], p)
    grid_spec = pltpu.PrefetchScalarGridSpec(
        num_scalar_prefetch=2, grid=(PAIRS, nsteps),
        in_specs=[pl.BlockSpec((tq, 128), qmap),
                  pl.BlockSpec((tk, 128), lambda p, t, it, jt: (jt[t], PAIRS + p)),
                  pl.BlockSpec((tk, 128), lambda p, t, it, jt: (jt[t], 2 * PAIRS + p)),
                  pl.BlockSpec((tq, 128), qmap), pl.BlockSpec((tk, 128), kmap)],
        out_specs=[pl.BlockSpec((tq, 128), qmap),
                   pl.BlockSpec((None, tq, 128), lambda p, t, it, jt: (p, it[t], 0))],
        scratch_shapes=[pltpu.VMEM((2, tq, 128), f32), pltpu.VMEM((2, tq, 128), f32), pltpu.VMEM((2, tq, tk), f32),
                        pltpu.VMEM((2, tq, tk), bf16), pltpu.VMEM((2, tq, 128), f32)],
    )
    return pl.pallas_call(
        body, name="fox_attn_fwd", grid_spec=grid_spec,
        out_shape=[jax.ShapeDtypeStruct((T, AW), bf16), jax.ShapeDtypeStruct((PAIRS, T, 128), f32)],
        compiler_params=_cp("parallel", "arbitrary"),
    )(it, jt, qkv, qkv, qkv, qaug, kaug)


def _pool_fwd(u, wp, scale):
    tm = 1024

    def body(u_ref, wp_ref, sc_ref, pooled_ref, pool_ref, ext, lv):
        i = pl.program_id(0)

        @pl.when(i == 0)
        def _():
            ext[0:HALO, :] = jnp.zeros((HALO, AW), f32)

        uv = u_ref[...]
        ext[HALO:HALO + tm, :] = uv
        t_idx = i * tm + lax.broadcasted_iota(jnp.int32, (tm, 1), 0)
        for g, w in enumerate(WINDOWS):
            lo, hi = 128 * g, 128 * (g + 1)
            ug = uv[:, lo:hi]
            steps = w.bit_length() - 1
            assert 1 << steps == w and 8 * steps <= HALO
            for k in range(1, steps + 1):
                back = 1 << (k - 1)
                first = 8 * k if k < steps else HALO
                rd = (lambda a, b: ext[a:b, lo:hi]) if k == 1 else (lambda a, b, s=k % 2: lv[s, a:b, :])
                cur = rd(first, tm + HALO) + rd(first - back, tm + HALO - back)
                if k < steps:
                    lv[(k + 1) % 2, first:tm + HALO, :] = cur
            acc = cur
            cnt = jnp.minimum(t_idx + 1, w).astype(f32)
            pb = (acc / cnt - ug).astype(bf16)
            pooled_ref[:, lo:hi] = pb
            mixed = jnp.dot(pb, wp_ref[g], preferred_element_type=f32)
            pool_ref[:, lo:hi] = (mixed * sc_ref[:, lo:hi]).astype(bf16)
        ext[0:HALO, :] = uv[tm - HALO:tm, :]

    return pl.pallas_call(
        body, name="pool_fwd", grid=(T // tm,),
        in_specs=[pl.BlockSpec((tm, AW), lambda i: (i, 0)), _full((4, 128, 128)), _full((1, AW))],
        out_specs=[pl.BlockSpec((tm, AW), lambda i: (i, 0)), pl.BlockSpec((tm, AW), lambda i: (i, 0))],
        out_shape=[jax.ShapeDtypeStruct((T, AW), bf16), jax.ShapeDtypeStruct((T, AW), bf16)],
        scratch_shapes=[pltpu.VMEM((tm + HALO, AW), f32), pltpu.VMEM((2, tm + HALO, 128), f32)],
        compiler_params=_cp("arbitrary"),
    )(u, wp, scale)


def _outproj(x, attn, pool, wo, g2):
    tm = 1024

    def body(x_ref, a_ref, p_ref, wo_ref, g_ref, x1_ref, h2_ref):
        mixed = jnp.concatenate([a_ref[...], p_ref[...]], axis=1)
        x1 = x_ref[...] + jnp.dot(mixed, wo_ref[...], preferred_element_type=f32)
        x1_ref[...] = x1
        r = lax.rsqrt(jnp.mean(x1 * x1, axis=-1, keepdims=True) + EPS)
        h2_ref[...] = (x1 * r * g_ref[...]).astype(bf16)

    return pl.pallas_call(
        body, name="outproj", grid=(T // tm,),
        in_specs=[pl.BlockSpec((tm, D), lambda i: (i, 0)), pl.BlockSpec((tm, AW), lambda i: (i, 0)),
                  pl.BlockSpec((tm, AW), lambda i: (i, 0)), _full((D, D)), _full((1, D))],
        out_specs=[pl.BlockSpec((tm, D), lambda i: (i, 0)), pl.BlockSpec((tm, D), lambda i: (i, 0))],
        out_shape=[jax.ShapeDtypeStruct((T, D), f32), jax.ShapeDtypeStruct((T, D), bf16)],
        compiler_params=_cp("parallel"),
    )(x, attn, pool, wo, g2)


def _mlp_fwd_loss(h2, x1, wg, wu, wd, tgt, gf):
    tm = 512

    def body(h_ref, x1_ref, wg_ref, wu_ref, wd_ref, t_ref, g_ref,
             loss_ref, dg_ref, dx_ref, dxb_ref, ud_ref, silu_ref, a_ref, x2):
        i = pl.program_id(0)
        s = pl.program_id(1)

        @pl.when(jnp.logical_and(i == 0, s == 0))
        def _():
            loss_ref[...] = jnp.zeros_like(loss_ref)
            dg_ref[...] = jnp.zeros_like(dg_ref)

        h = h_ref[...]
        gus = [(lax.dot_general(h, wg_ref[s, c0:c1, :], NT, preferred_element_type=f32),
                lax.dot_general(h, wu_ref[s, c0:c1, :], NT, preferred_element_type=f32)) for c0, c1 in FS_CHUNKS]
        for (c0, c1), (gate, up) in zip(FS_CHUNKS, gus):
            sg = jax.nn.sigmoid(gate)
            silu = gate * sg
            ud_ref[:, c0:c1] = (up * (sg * (1.0 + gate * (1.0 - sg)))).astype(bf16)
            silu_ref[:, c0:c1] = silu.astype(bf16)
            a_ref[:, c0:c1] = (silu * up).astype(bf16)
        part = jnp.dot(a_ref[...], wd_ref[s], preferred_element_type=f32)

        @pl.when(s == 0)
        def _():
            x2[...] = x1_ref[...] + part

        @pl.when(s > 0)
        def _():
            x2[...] += part

        @pl.when(s == NSH - 1)
        def _():
            xv = x2[...]
            g = g_ref[...]
            r = lax.rsqrt(jnp.mean(xv * xv, axis=-1, keepdims=True) + EPS)
            xhat = xv * r
            e = xhat * g - t_ref[...]
            loss_ref[...] += 0.5 * jnp.sum(jnp.mean(e * e, axis=-1, keepdims=True))
            dy = e * (1.0 / D)
            dg_ref[...] += jnp.sum(dy * xhat, axis=0, keepdims=True)
            z = dy * g
            dx = r * (z - xhat * jnp.mean(z * xhat, axis=-1, keepdims=True))
            dx_ref[...] = dx
            dxb_ref[...] = dx.astype(bf16)

    row = lambda i, s: (i, 0)
    sl = lambda i, s: (s, i, 0)
    wsl = lambda i, s: (s, 0, 0)
    return pl.pallas_call(
        body, name="mlp_fwd_loss", grid=(T // tm, NSH),
        in_specs=[pl.BlockSpec((tm, D), row), pl.BlockSpec((tm, D), row),
                  _resident((NSH, FS, D)), _resident((NSH, FS, D)), _resident((NSH, FS, D)),
                  pl.BlockSpec((tm, D), row), pl.BlockSpec((1, D), lambda i, s: (0, 0))],
        out_specs=[pl.BlockSpec((8, 128), lambda i, s: (0, 0)), pl.BlockSpec((1, D), lambda i, s: (0, 0)),
                   pl.BlockSpec((tm, D), row), pl.BlockSpec((tm, D), row),
                   pl.BlockSpec((None, tm, FS), sl), pl.BlockSpec((None, tm, FS), sl), pl.BlockSpec((None, tm, FS), sl)],
        out_shape=[jax.ShapeDtypeStruct((8, 128), f32), jax.ShapeDtypeStruct((1, D), f32),
                   jax.ShapeDtypeStruct((T, D), f32), jax.ShapeDtypeStruct((T, D), bf16)]
        + [jax.ShapeDtypeStruct((NSH, T, FS), bf16)] * 3,
        scratch_shapes=[pltpu.VMEM((tm, D), f32)],
        compiler_params=_cp("arbitrary", "arbitrary"),
    )(h2, x1, wg, wu, wd, tgt, gf)


def _mlp_bwd(dx2b, dx2, ud, silu, wg, wu, wd, x1, g2):
    tm = 512

    def body(dxb_ref, dx_ref, ud_ref, silu_ref, wg_ref, wu_ref, wd_ref, x1_ref, g_ref,
             dg_ref, du_ref, dx1_ref, dx1b_ref, dn_ref, acc):
        i = pl.program_id(0)
        s = pl.program_id(1)

        @pl.when(jnp.logical_and(i == 0, s == 0))
        def _():
            dn_ref[...] = jnp.zeros_like(dn_ref)

        dxb = dxb_ref[...]
        das = [lax.dot_general(dxb, wd_ref[s, c0:c1, :], NT, preferred_element_type=f32) for c0, c1 in FS_CHUNKS]
        for (c0, c1), da in zip(FS_CHUNKS, das):
            dg_ref[:, c0:c1] = (da * ud_ref[:, c0:c1].astype(f32)).astype(bf16)
            du_ref[:, c0:c1] = (da * silu_ref[:, c0:c1].astype(f32)).astype(bf16)
        part = jnp.dot(dg_ref[...], wg_ref[s], preferred_element_type=f32)
        part = part + jnp.dot(du_ref[...], wu_ref[s], preferred_element_type=f32)

        @pl.when(s == 0)
        def _():
            acc[...] = part

        @pl.when(s > 0)
        def _():
            acc[...] += part

        @pl.when(s == NSH - 1)
        def _():
            xv = x1_ref[...]
            r = lax.rsqrt(jnp.mean(xv * xv, axis=-1, keepdims=True) + EPS)
            xhat = xv * r
            dh = acc[...]
            dn_ref[...] += jnp.sum(dh * xhat, axis=0, keepdims=True)
            z = dh * g_ref[...]
            dx1 = dx_ref[...] + r * (z - xhat * jnp.mean(z * xhat, axis=-1, keepdims=True))
            dx1_ref[...] = dx1
            dx1b_ref[...] = dx1.astype(bf16)

    row = lambda i, s: (i, 0)
    sl = lambda i, s: (s, i, 0)
    wsl = lambda i, s: (s, 0, 0)
    return pl.pallas_call(
        body, name="mlp_bwd", grid=(T // tm, NSH),
        in_specs=[pl.BlockSpec((tm, D), row), pl.BlockSpec((tm, D), row),
                  pl.BlockSpec((None, tm, FS), sl), pl.BlockSpec((None, tm, FS), sl),
                  _resident((NSH, FS, D)), _resident((NSH, FS, D)), _resident((NSH, FS, D)),
                  pl.BlockSpec((tm, D), row), pl.BlockSpec((1, D), lambda i, s: (0, 0))],
        out_specs=[pl.BlockSpec((None, tm, FS), sl), pl.BlockSpec((None, tm, FS), sl),
                   pl.BlockSpec((tm, D), row), pl.BlockSpec((tm, D), row), pl.BlockSpec((1, D), lambda i, s: (0, 0))],
        out_shape=[jax.ShapeDtypeStruct((NSH, T, FS), bf16)] * 2
        + [jax.ShapeDtypeStruct((T, D), f32), jax.ShapeDtypeStruct((T, D), bf16), jax.ShapeDtypeStruct((1, D), f32)],
        scratch_shapes=[pltpu.VMEM((tm, D), f32)],
        compiler_params=_cp("arbitrary", "arbitrary"),
    )(dx2b, dx2, ud, silu, wg, wu, wd, x1, g2)


def _mm_tn(a, bs, name, a_sharded=False, b_sharded=False, tk=512, out_dtype=bf16):
    nb = len(bs)
    sh = NSH if (a_sharded or b_sharded) else 1
    m = a.shape[-1]
    nk = T // tk

    def body(a_ref, *refs):
        kk = pl.program_id(1)
        av = a_ref[...]
        for b_ref, o_ref, acc in zip(refs[:nb], refs[nb:2 * nb], refs[2 * nb:]):
            upd = lax.dot_general(av, b_ref[...], TN, preferred_element_type=f32)

            @pl.when(kk == 0)
            def _():
                acc[...] = upd

            @pl.when(kk > 0)
            def _():
                acc[...] += upd

            @pl.when(kk == nk - 1)
            def _():
                o_ref[...] = acc[...].astype(out_dtype)

    a_spec = (pl.BlockSpec((None, tk, m), lambda s, k: (s, k, 0)) if a_sharded
              else pl.BlockSpec((tk, m), lambda s, k: (k, 0)))
    b_specs, o_specs, o_shapes, scratch = [], [], [], []
    for b in bs:
        n = b.shape[-1]
        b_specs.append(pl.BlockSpec((None, tk, n), lambda s, k: (s, k, 0)) if b_sharded
                       else pl.BlockSpec((tk, n), lambda s, k: (k, 0)))
        scratch.append(pltpu.VMEM((m, n), f32))
        if sh > 1:
            o_specs.append(pl.BlockSpec((None, m, n), lambda s, k: (s, 0, 0)))
            o_shapes.append(jax.ShapeDtypeStruct((sh, m, n), out_dtype))
        else:
            o_specs.append(pl.BlockSpec((m, n), lambda s, k: (0, 0)))
            o_shapes.append(jax.ShapeDtypeStruct((m, n), out_dtype))
    return pl.pallas_call(
        body, name=name, grid=(sh, nk), in_specs=[a_spec] + b_specs, out_specs=o_specs, out_shape=o_shapes,
        scratch_shapes=scratch, compiler_params=_cp("arbitrary", "arbitrary"),
    )(a, *bs)


def _dw_mlp(a_list, b_of, bs):
    na, steps = len(a_list), len(a_list) * NSH
    first_use = [b_of.index(k) * NSH for k in range(len(bs))]
    assert all(b_of[j] <= b_of[j + 1] for j in range(na - 1))

    def body(*refs):
        a_refs, b_refs, o_refs = refs[:na], refs[na:na + len(bs)], refs[na + len(bs):2 * na + len(bs)]
        abuf, bbuf, asem, bsem = refs[2 * na + len(bs):]
        s = pl.program_id(0)

        def fetch_a(step, start):
            slot = lax.rem(step, 2)
            for j in range(na):
                @pl.when(lax.div(step, NSH) == j)
                def _():
                    cp = pltpu.make_async_copy(a_refs[j].at[lax.rem(step, NSH)], abuf.at[slot], asem.at[slot])
                    cp.start() if start else cp.wait()

        b_copies = [pltpu.make_async_copy(b_refs[k], bbuf.at[k], bsem.at[k]) for k in range(len(bs))]

        @pl.when(s == 0)
        def _():
            fetch_a(s, True)
            b_copies[0].start()

        @pl.when(s + 1 < steps)
        def _():
            fetch_a(s + 1, True)

        for k in range(1, len(bs)):
            @pl.when(s == 1)
            def _():
                b_copies[k].start()

        fetch_a(s, False)
        for k in range(len(bs)):
            @pl.when(s == max(first_use[k], 0 if k == 0 else 2))
            def _():
                b_copies[k].wait()

        which_b = sum((s >= first_use[k]).astype(jnp.int32) for k in range(1, len(bs)))
        upd = lax.dot_general(abuf[lax.rem(s, 2)], bbuf[which_b], TN, preferred_element_type=f32).astype(bf16)
        for j in range(na):
            @pl.when(lax.div(s, NSH) == j)
            def _():
                o_refs[j][...] = upd

    def out_spec(j):
        return pl.BlockSpec((None, FS, D), lambda s: (jnp.clip(s - NSH * j, 0, NSH - 1), 0, 0))

    return pl.pallas_call(
        body, name="dw_mlp", grid=(steps,),
        in_specs=[pl.BlockSpec(memory_space=pl.ANY)] * (na + len(bs)),
        out_specs=[out_spec(j) for j in range(na)],
        out_shape=[jax.ShapeDtypeStruct((NSH, FS, D), bf16)] * na,
        scratch_shapes=[pltpu.VMEM((2, T, FS), bf16), pltpu.VMEM((len(bs), T, D), bf16),
                        pltpu.SemaphoreType.DMA((2,)), pltpu.SemaphoreType.DMA((len(bs),))],
        compiler_params=_cp("arbitrary"),
    )(*a_list, *bs)


def _mm_tn_rows(a_list, b, name, tk=1024, out_dtype=bf16, stacked=False):
    na = len(a_list)
    n = b.shape[-1]
    nk = T // tk
    ms = [a.shape[-1] for a in a_list]
    starts = [sum(ms[:k]) for k in range(na)] if stacked else [0] * na
    out_rows = [sum(ms)] if stacked else ms

    def body(*refs):
        a_refs, b_ref = refs[:na], refs[na]
        no = len(out_rows)
        o_refs, accs = refs[na + 1:na + 1 + no], refs[na + 1 + no:]
        kk = pl.program_id(0)
        bv = b_ref[...]
        for k, a_ref in enumerate(a_refs):
            o_ref, acc = (o_refs[0], accs[0]) if stacked else (o_refs[k], accs[k])
            rows = pl.ds(starts[k], ms[k])
            upd = lax.dot_general(a_ref[...], bv, TN, preferred_element_type=f32)

            @pl.when(kk == 0)
            def _():
                acc[rows, :] = upd

            @pl.when(kk > 0)
            def _():
                acc[rows, :] += upd

            @pl.when(kk == nk - 1)
            def _():
                o_ref[rows, :] = acc[rows, :].astype(out_dtype)

    return pl.pallas_call(
        body, name=name, grid=(nk,),
        in_specs=[pl.BlockSpec((tk, m), lambda k: (k, 0)) for m in ms] + [pl.BlockSpec((tk, n), lambda k: (k, 0))],
        out_specs=[pl.BlockSpec((r, n), lambda k: (0, 0)) for r in out_rows],
        out_shape=[jax.ShapeDtypeStruct((r, n), out_dtype) for r in out_rows],
        scratch_shapes=[pltpu.VMEM((r, n), f32) for r in out_rows],
        compiler_params=_cp("arbitrary"),
    )(*a_list, b)


def _outproj_bwd(dx1b, wo):
    tm = 1024

    def body(dx_ref, wo_ref, da_ref, dp_ref):
        dx = dx_ref[...]
        da_ref[...] = lax.dot_general(dx, wo_ref[0:AW, :], NT, preferred_element_type=f32).astype(bf16)
        dp_ref[...] = lax.dot_general(dx, wo_ref[AW:2 * AW, :], NT, preferred_element_type=f32)

    return pl.pallas_call(
        body, name="outproj_bwd", grid=(T // tm,),
        in_specs=[pl.BlockSpec((tm, D), lambda i: (i, 0)), _full((D, D))],
        out_specs=[pl.BlockSpec((tm, AW), lambda i: (i, 0)), pl.BlockSpec((tm, AW), lambda i: (i, 0))],
        out_shape=[jax.ShapeDtypeStruct((T, AW), bf16), jax.ShapeDtypeStruct((T, AW), f32)],
        compiler_params=_cp("parallel"),
    )(dx1b, wo)


def _pool_bwd(dpool, pooled, wp, scale):
    tm = 1024
    n = T // tm

    def body(dp_ref, pb_ref, wp_ref, sc_ref, du_ref, dsc_ref, dwp_ref, ext, lv):
        i = pl.program_id(0)

        @pl.when(i == 0)
        def _():
            ext[tm:tm + HALO, :] = jnp.zeros((HALO, AW), f32)
            dsc_ref[...] = jnp.zeros_like(dsc_ref)
            dwp_ref[...] = jnp.zeros_like(dwp_ref)

        t_idx = (n - 1 - i) * tm + lax.broadcasted_iota(jnp.int32, (tm, 1), 0)
        for g, w in enumerate(WINDOWS):
            lo, hi = 128 * g, 128 * (g + 1)
            pb = pb_ref[:, lo:hi]
            mixed = jnp.dot(pb, wp_ref[g], preferred_element_type=f32)
            dpo = dp_ref[:, lo:hi]
            dsc_ref[:, lo:hi] += jnp.sum(dpo * mixed, axis=0, keepdims=True)
            dmr = (dpo * sc_ref[:, lo:hi]).astype(bf16)
            dwp_ref[g] += lax.dot_general(pb, dmr, TN, preferred_element_type=f32)
            dpl = lax.dot_general(dmr, wp_ref[g], NT, preferred_element_type=f32)
            cnt = jnp.minimum(t_idx + 1, w).astype(f32)
            dpn = dpl / cnt
            ext[0:tm, lo:hi] = dpn
            steps = w.bit_length() - 1
            assert 1 << steps == w and 8 * steps <= HALO
            for k in range(1, steps + 1):
                ahead = 1 << (k - 1)
                last = tm + HALO - 8 * k if k < steps else tm
                rd = (lambda a, b: ext[a:b, lo:hi]) if k == 1 else (lambda a, b, s=k % 2: lv[s, a:b, :])
                cur = rd(0, last) + rd(ahead, last + ahead)
                if k < steps:
                    lv[(k + 1) % 2, 0:last, :] = cur
            du_ref[:, lo:hi] = (cur - dpl).astype(bf16)
        ext[tm:tm + HALO, :] = ext[0:HALO, :]

    rev = lambda i: (n - 1 - i, 0)
    return pl.pallas_call(
        body, name="pool_bwd", grid=(n,),
        in_specs=[pl.BlockSpec((tm, AW), rev), pl.BlockSpec((tm, AW), rev), _full((4, 128, 128)), _full((1, AW))],
        out_specs=[pl.BlockSpec((tm, AW), rev), _full((1, AW)), _full((4, 128, 128))],
        out_shape=[jax.ShapeDtypeStruct((T, AW), bf16), jax.ShapeDtypeStruct((1, AW), f32),
                   jax.ShapeDtypeStruct((4, 128, 128), f32)],
        scratch_shapes=[pltpu.VMEM((tm + HALO, AW), f32), pltpu.VMEM((2, tm + HALO, 128), f32)],
        compiler_params=_cp("arbitrary"),
    )(dpool, pooled, wp, scale)


def _attn_bwd(qkv, qaug, kaug, attn, dattn, lse, dep):
    tq = tk = ATT_T
    n = T // tq
    it, jt = _causal_steps(True)
    nsteps = it.shape[0]

    rs = 64

    def body(it_ref, jt_ref, q_ref, k_ref, v_ref, qa_ref, ka_ref, o_ref, do_ref, lse_ref, dep_ref,
             dq_ref, dqs_ref, dk_ref, dks_ref, dv_ref, dq_acc, dk_acc, dv_acc, s_sc, dp_sc, p_sc, ds_sc):
        t = pl.program_id(1)
        i = it_ref[t]
        j = jt_ref[t]

        @pl.when(t == 0)
        def _():
            dq_acc[...] = jnp.zeros_like(dq_acc)

        @pl.when(i == j)
        def _():
            dk_acc[...] = jnp.zeros_like(dk_acc)
            dv_acc[...] = jnp.zeros_like(dv_acc)

        lane = lax.broadcasted_iota(jnp.int32, (tq, 128), 1)

        def step(on_diagonal):
            q = (q_ref[...].astype(f32) * Q_SCALE).astype(bf16)
            k = k_ref[...]
            v = v_ref[...]
            qa = qa_ref[...]
            ka = ka_ref[...]
            do = do_ref[...]
            dd = do.astype(f32) * o_ref[...].astype(f32)
            blocks = _row_blocks(tq, tk, on_diagonal)
            qes, kes, does, deltas = [], [], [], []
            for e in range(2):
                hm = (lane >= 64) if e else (lane < 64)
                qes.append(jnp.where(hm, q, qa))
                kes.append(jnp.where(hm, k, ka))
                does.append(jnp.where(hm, do, jnp.zeros_like(do)))
                deltas.append(jnp.sum(jnp.where(hm, dd, 0.0), axis=1, keepdims=True))
                for r0, r1, nc in blocks:
                    s_sc[e, r0:r1, 0:nc] = lax.dot_general(qes[e][r0:r1], kes[e][0:nc], NT, preferred_element_type=f32)
                    dp_sc[e, r0:r1, 0:nc] = lax.dot_general(does[e][r0:r1], v[0:nc], NT, preferred_element_type=f32)
            for e in range(2):
                for r0, r1, nc in blocks:
                    for r in range(r0, r1, rs):
                        s = s_sc[e, r:r + rs, 0:nc] - lse_ref[r:r + rs, 64 * e:64 * e + 1]
                        if on_diagonal:
                            row = lax.broadcasted_iota(jnp.int32, (rs, nc), 0) + r
                            col = lax.broadcasted_iota(jnp.int32, (rs, nc), 1)
                            s = jnp.where(col <= row, s, NEG)
                        p = jnp.exp2(s)
                        p_sc[e, r:r + rs, 0:nc] = p.astype(bf16)
                        ds_sc[e, r:r + rs, 0:nc] = (p * (dp_sc[e, r:r + rs, 0:nc] - deltas[e][r:r + rs, :])).astype(bf16)
                for r0, r1, nc in blocks:
                    dv_acc[:, 0:nc] += lax.dot_general(does[e][r0:r1], p_sc[e, r0:r1, 0:nc], TN, preferred_element_type=f32)
                    dsb = ds_sc[e, r0:r1, 0:nc]
                    dk_acc[e, :, 0:nc] += lax.dot_general(qes[e][r0:r1], dsb, TN, preferred_element_type=f32)
                    rq = pl.multiple_of(i * tq + r0, r1 - r0)
                    dq_acc[e, pl.ds(rq, r1 - r0), :] += jnp.dot(dsb, kes[e][0:nc], preferred_element_type=f32)

        @pl.when(i > j)
        def _():
            step(False)

        @pl.when(i == j)
        def _():
            step(True)

        @pl.when(i == n - 1)
        def _():
            dk0 = dk_acc[0].T
            dk1 = dk_acc[1].T
            dk_ref[...] = (jnp.where(lane < 64, dk0, dk1) * (1.0 / LOG2E)).astype(bf16)
            dks_ref[...] = jnp.where(lane < 64, dk1, dk0)
            dv_ref[...] = dv_acc[...].T.astype(bf16)

        @pl.when(t == nsteps - 1)
        def _():
            lane_t = lax.broadcasted_iota(jnp.int32, (T, 128), 1)
            dq_ref[...] = (jnp.where(lane_t < 64, dq_acc[0], dq_acc[1]) * 0.125).astype(bf16)
            dqs_ref[...] = jnp.where(lane_t < 64, dq_acc[1], dq_acc[0])

    qmap = lambda p, t, it, jt: (it[t], p)
    grid_spec = pltpu.PrefetchScalarGridSpec(
        num_scalar_prefetch=2, grid=(PAIRS, nsteps),
        in_specs=[pl.BlockSpec((tq, 128), qmap),
                  pl.BlockSpec((tk, 128), lambda p, t, it, jt: (jt[t], PAIRS + p)),
                  pl.BlockSpec((tk, 128), lambda p, t, it, jt: (jt[t], 2 * PAIRS + p)),
                  pl.BlockSpec((tq, 128), qmap), pl.BlockSpec((tk, 128), lambda p, t, it, jt: (jt[t], p)),
                  pl.BlockSpec((tq, 128), qmap), pl.BlockSpec((tq, 128), qmap),
                  pl.BlockSpec((None, tq, 128), lambda p, t, it, jt: (p, it[t], 0)),
                  pl.BlockSpec((8, 128), lambda p, t, it, jt: (0, 0))],
        out_specs=[pl.BlockSpec((T, 128), lambda p, t, it, jt: (0, p)),
                   pl.BlockSpec((None, T, 128), lambda p, t, it, jt: (p, 0, 0)),
                   pl.BlockSpec((tk, 128), lambda p, t, it, jt: (jt[t], p)),
                   pl.BlockSpec((None, tk, 128), lambda p, t, it, jt: (p, jt[t], 0)),
                   pl.BlockSpec((tk, 128), lambda p, t, it, jt: (jt[t], p))],
        scratch_shapes=[pltpu.VMEM((2, T, 128), f32), pltpu.VMEM((2, 128, tk), f32), pltpu.VMEM((128, tk), f32),
                        pltpu.VMEM((2, tq, tk), f32), pltpu.VMEM((2, tq, tk), f32), pltpu.VMEM((2, tq, tk), bf16),
                        pltpu.VMEM((2, tq, tk), bf16)],
    )
    return pl.pallas_call(
        body, name="fox_attn_bwd", grid_spec=grid_spec,
        out_shape=[jax.ShapeDtypeStruct((T, AW), bf16), jax.ShapeDtypeStruct((PAIRS, T, 128), f32),
                   jax.ShapeDtypeStruct((T, AW), bf16), jax.ShapeDtypeStruct((PAIRS, T, 128), f32),
                   jax.ShapeDtypeStruct((T, AW), bf16)],
        compiler_params=_cp("parallel", "arbitrary"),
    )(it, jt, qkv, qkv, qkv, qaug, kaug, attn, dattn, lse, dep)


def _fox_cumsum_bwd(dqs, dks, fl, bfp):
    tb = CUMSUM_ROWS
    nb = T // tb

    def body(dqs_ref, dks_ref, fl_ref, b_ref, df_ref, db_ref, carry):
        i = pl.program_id(0)

        @pl.when(i == 0)
        def _():
            carry[...] = jnp.zeros_like(carry)
            db_ref[...] = jnp.zeros_like(db_ref)

        r = lax.broadcasted_iota(jnp.int32, (128, 128), 0)
        cc = lax.broadcasted_iota(jnp.int32, (128, 128), 1)
        lane = lax.broadcasted_iota(jnp.int32, (tb, 128), 1)
        even_at, odd_at = SPARE[0] + ROW_SUM_LANE, SPARE[1] + ROW_SUM_LANE
        both = jnp.zeros((tb, 128), f32)
        pick = jnp.zeros((128, 128), jnp.bool_)
        for p in range(PAIRS):
            diff = dqs_ref[p] - pltpu.roll(dks_ref[p], 128 - (COL_SUM_LANE - ROW_SUM_LANE), 1)
            moved = pltpu.roll(diff, p, 1) if p else diff
            both = jnp.where(jnp.logical_or(lane == even_at + p, lane == odd_at + p), moved, both)
            pick = jnp.logical_or(pick, jnp.logical_or(jnp.logical_and(r == even_at + p, cc == 2 * p),
                                                       jnp.logical_and(r == odd_at + p, cc == 2 * p + 1)))
        dc = _dot01(pick.astype(bf16), both, False)
        rt = lax.broadcasted_iota(jnp.int32, (tb, tb), 0)
        ct = lax.broadcasted_iota(jnp.int32, (tb, tb), 1)
        utri = (ct >= rt).astype(bf16)
        dl = _dot01(utri, dc, True) + carry[0:1, :]
        carry[...] = jnp.broadcast_to(dl[0:1, :], (8, 128))
        z = fl_ref[...] + b_ref[...]
        df = dl * jax.nn.sigmoid(-z)
        df_ref[...] = df.astype(bf16)
        db_ref[...] += jnp.sum(df, axis=0, keepdims=True)

    rev = lambda i: (nb - 1 - i, 0)
    return pl.pallas_call(
        body, name="fox_cumsum_bwd", grid=(nb,),
        in_specs=[pl.BlockSpec((PAIRS, tb, 128), lambda i: (0, nb - 1 - i, 0)),
                  pl.BlockSpec((PAIRS, tb, 128), lambda i: (0, nb - 1 - i, 0)),
                  pl.BlockSpec((tb, 128), rev), _full((1, 128))],
        out_specs=[pl.BlockSpec((tb, 128), rev), _full((1, 128))],
        out_shape=[jax.ShapeDtypeStruct((T, 128), bf16), jax.ShapeDtypeStruct((1, 128), f32)],
        scratch_shapes=[pltpu.VMEM((8, 128), f32)],
        compiler_params=_cp("arbitrary"),
    )(dqs, dks, fl, bfp)


def _inproj_bwd(dq, dk, dv, du, df, w, x, dx1, g1):
    tm = 512

    def body(dq_ref, dk_ref, dv_ref, du_ref, df_ref, w_ref, x_ref, dx1_ref, g_ref, dx_ref, dn_ref):
        i = pl.program_id(0)

        @pl.when(i == 0)
        def _():
            dn_ref[...] = jnp.zeros_like(dn_ref)

        dproj = jnp.concatenate([dq_ref[...], dk_ref[...], dv_ref[...], du_ref[...], df_ref[...]], axis=1)
        dh = jnp.dot(dproj, w_ref[...], preferred_element_type=f32)
        xv = x_ref[...]
        r = lax.rsqrt(jnp.mean(xv * xv, axis=-1, keepdims=True) + EPS)
        xhat = xv * r
        dn_ref[...] += jnp.sum(dh * xhat, axis=0, keepdims=True)
        z = dh * g_ref[...]
        dx_ref[...] = dx1_ref[...] + r * (z - xhat * jnp.mean(z * xhat, axis=-1, keepdims=True))

    row = lambda i: (i, 0)
    return pl.pallas_call(
        body, name="inproj_bwd", grid=(T // tm,),
        in_specs=[pl.BlockSpec((tm, AW), row)] * 4 + [pl.BlockSpec((tm, 128), row), _full((W_ROWS, D)),
                                                       pl.BlockSpec((tm, D), row), pl.BlockSpec((tm, D), row), _full((1, D))],
        out_specs=[pl.BlockSpec((tm, D), row), _full((1, D))],
        out_shape=[jax.ShapeDtypeStruct((T, D), f32), jax.ShapeDtypeStruct((1, D), f32)],
        compiler_params=_cp("arbitrary"),
    )(dq, dk, dv, du, df, w, x, dx1, g1)


def _adamw_math(w, g, m, v):
    m = B1 * m + (1.0 - B1) * g
    v = B2 * v + (1.0 - B2) * (g * g)
    m_hat = m / (1.0 - B1 ** STEP)
    v_hat = v / (1.0 - B2 ** STEP)
    delta = -LR * (m_hat / (jnp.sqrt(v_hat) + AEPS) + WD * w)
    return delta, m, v


SHARD_STEPS = 4


def _adamw_shards(ws, ms, vs, ps_mine, ps_other, name, steps):
    n = len(ws)

    def body(*refs):
        ins, outs = refs[:5 * n], refs[5 * n:]
        for k in range(n):
            w_ref, m_ref, v_ref, a_ref, b_ref = ins[5 * k:5 * k + 5]
            g_ref, d_ref, nm_ref, nv_ref = outs[4 * k:4 * k + 4]
            g = (a_ref[...].astype(f32) + b_ref[...].astype(f32)).reshape(w_ref.shape)
            g_ref[...] = g
            d_ref[...], nm_ref[...], nv_ref[...] = _adamw_math(w_ref[...], g, m_ref[...], v_ref[...])

    in_specs, out_specs, out_shape = [], [], []
    for w, p in zip(ws, ps_mine):
        rest = tuple(w.shape[1:])
        tr = w.shape[0] // steps
        assert tr * steps == w.shape[0]
        spec = pl.BlockSpec((tr,) + rest, lambda i, _n=len(rest): (i,) + (0,) * _n)
        pspec = pl.BlockSpec((tr, p.shape[1]), lambda i: (i, 0))
        in_specs += [spec] * 3 + [pspec] * 2
        out_specs += [spec] * 4
        out_shape += [jax.ShapeDtypeStruct(w.shape, f32)] * 4
    args = [a for k in range(n) for a in (ws[k], ms[k], vs[k], ps_mine[k], ps_other[k])]
    res = pl.pallas_call(
        body, name=name, grid=(steps,), in_specs=in_specs, out_specs=out_specs, out_shape=out_shape,
        compiler_params=_cp("parallel"),
    )(*args)
    return [res[4 * k:4 * k + 4] for k in range(n)]


SMALL_SLOTS = ((0, 8, 128), (8, 16, 128), (16, 24, 128), (24, 28, 128), (32, 33, 8))
LOSS_ROW = 39


def _adamw_small(ws, ms, vs, parts, parts_wp):
    n = len(ws)

    def body(*refs):
        w_refs, m_refs, v_refs = refs[0:n], refs[n:2 * n], refs[2 * n:3 * n]
        p_ref, pw_ref = refs[3 * n], refs[3 * n + 1]
        outs = refs[3 * n + 2:]
        g_all = p_ref[0]
        g_wp = pw_ref[0]
        for k in range(1, 8):
            g_all = g_all + p_ref[k]
            g_wp = g_wp + pw_ref[k]
        grads = [g_all[r0:r1, 0:lanes] for r0, r1, lanes in SMALL_SLOTS] + [g_wp]
        for idx, g in enumerate(grads):
            d, nm, nv = _adamw_math(w_refs[idx][...], g, m_refs[idx][...], v_refs[idx][...])
            outs[idx][...] = g
            outs[n + idx][...] = d
            outs[2 * n + idx][...] = nm
            outs[3 * n + idx][...] = nv
        outs[4 * n][...] = g_all[LOSS_ROW:LOSS_ROW + 1, :]

    shapes = [jax.ShapeDtypeStruct(w.shape, f32) for w in ws]
    res = pl.pallas_call(
        body, name="adamw_small", out_shape=shapes * 4 + [jax.ShapeDtypeStruct((1, 128), f32)],
    )(*ws, *ms, *vs, parts, parts_wp)
    return res[:4 * n], res[4 * n]


def _sum4(recvs, gs, mine, name, steps):
    n = len(recvs)

    def body(mine_ref, *refs):
        for r_ref, g_ref, o_ref in zip(refs[:n], refs[n:2 * n], refs[2 * n:]):
            o_ref[...] = ((g_ref[...].astype(f32) + r_ref[0].astype(f32))
                          + (r_ref[1].astype(f32) + r_ref[2].astype(f32))).astype(bf16)

    r_specs, g_specs, o_specs, shapes = [], [], [], []
    for recv in recvs:
        _, rows, cols = recv.shape
        tr = rows // steps
        assert tr * steps == rows
        r_specs.append(pl.BlockSpec((3, tr, cols), lambda i, m: (0, i, 0)))
        g_specs.append(pl.BlockSpec((None, tr, cols), lambda i, m: (m[0], i, 0)))
        o_specs.append(pl.BlockSpec((tr, cols), lambda i, m: (i, 0)))
        shapes.append(jax.ShapeDtypeStruct((rows, cols), bf16))
    grid_spec = pltpu.PrefetchScalarGridSpec(num_scalar_prefetch=1, grid=(steps,), in_specs=r_specs + g_specs,
                                             out_specs=o_specs)
    return pl.pallas_call(
        body, name=name, grid_spec=grid_spec, out_shape=shapes, compiler_params=_cp("arbitrary"),
    )(mine, *recvs, *gs)


_HBM = pl.BlockSpec(memory_space=pltpu.HBM)
_SEM = pl.BlockSpec(memory_space=pltpu.SEMAPHORE)
_EFFECT = pltpu.SideEffectType.DATAFLOW_SIDE_EFFECTING


def _in_hbm(a):
    return pltpu.with_memory_space_constraint(a, pltpu.HBM)


def _mesh_pos():
    return lax.axis_index("x"), lax.axis_index("y"), lax.axis_index("c")


def _other_chips(x, y):
    return [(1 - x, y), (x, 1 - y), (1 - x, 1 - y)]


def _gather_copy(srcs, lands, send_sems, recv_sems, a, k, slot):
    x, y, c = _mesh_pos()
    cx, cy = _other_chips(x, y)[k]
    return pltpu.make_async_remote_copy(
        src_ref=srcs[a], dst_ref=lands[a].at[slot], send_sem=send_sems.at[3 * a + k], recv_sem=recv_sems.at[3 * a + k],
        device_id=(cx, cy, c), device_id_type=MESH)


def _scatter_copy(srcs, lands, send_sems, recv_sems, a, k):
    x, y, c = _mesh_pos()
    cx, cy = _other_chips(x, y)[k]
    return pltpu.make_async_remote_copy(
        src_ref=srcs[a].at[2 * cx + cy], dst_ref=lands[a].at[k], send_sem=send_sems.at[3 * a + k],
        recv_sem=recv_sems.at[3 * a + k], device_id=(cx, cy, c), device_id_type=MESH)


def _all_gather_w_in(part):
    cols = part.shape[1] // 2

    def body(src, dst, send_sems, recv_sems, loc_sem):
        x, y, c = _mesh_pos()
        mine = 2 * x + y
        chips = _other_chips(x, y)
        half = lambda ref, cc: ref.at[:, pl.ds(pl.multiple_of(cc * cols, cols), cols)]

        def over_ici(k, slot):
            cx, cy = chips[k]
            return pltpu.make_async_remote_copy(
                src_ref=half(src, c), dst_ref=half(dst.at[slot], c), send_sem=send_sems.at[k], recv_sem=recv_sems.at[k],
                device_id=(cx, cy, c), device_id_type=MESH)

        def to_sibling(k, cc):
            slot = 2 * chips[k][0] + chips[k][1]
            return pltpu.make_async_remote_copy(
                src_ref=half(dst.at[slot], cc), dst_ref=half(dst.at[slot], cc), send_sem=send_sems.at[3 + k],
                recv_sem=recv_sems.at[3 + k], device_id=(x, y, 1 - c), device_id_type=MESH)

        local = pltpu.make_async_copy(src, dst.at[mine], loc_sem.at[0])
        local.start()
        first = [over_ici(k, mine) for k in range(3)]
        for cp in first:
            cp.start()
        passed = [to_sibling(k, c) for k in range(3)]
        for k in range(3):
            over_ici(k, 2 * chips[k][0] + chips[k][1]).wait_recv()
            passed[k].start()
        for k in range(3):
            to_sibling(k, 1 - c).wait_recv()
        for cp in first + passed:
            cp.wait_send()
        local.wait()

    return pl.pallas_call(
        body, name="all_gather_w_in", in_specs=[_HBM], out_specs=_HBM,
        out_shape=jax.ShapeDtypeStruct((NSH,) + part.shape, part.dtype),
        scratch_shapes=[pltpu.SemaphoreType.DMA((6,)), pltpu.SemaphoreType.DMA((6,)), pltpu.SemaphoreType.DMA((1,))],
    )(part)


def _split_start(name, srcs, lands, n_sems, plan, dep):
    n, nl = len(srcs), len(lands)

    def body(*refs):
        src_refs, land_refs = refs[:n], refs[n:n + nl]
        send_sems, recv_sems = refs[n + nl + 1], refs[n + nl + 2]
        token = refs[-1]
        sends, _, own = plan(src_refs, land_refs, send_sems, recv_sems)
        for cp in own + sends:
            cp.start()
        token[...] = jnp.zeros_like(token)

    outs = pl.pallas_call(
        body, name=name,
        in_specs=[_HBM] * (n + nl) + [pl.BlockSpec(memory_space=pl.ANY)],
        out_specs=[_SEM, _SEM] + [_HBM] * (n + nl) + [pl.BlockSpec(memory_space=pltpu.VMEM)],
        out_shape=[pltpu.SemaphoreType.DMA((n_sems,)), pltpu.SemaphoreType.DMA((n_sems,))]
        + [pltpu.HBM(a.shape, a.dtype) for a in list(srcs) + list(lands)] + [jax.ShapeDtypeStruct((8, 128), f32)],
        input_output_aliases={i: 2 + i for i in range(n + nl)},
        compiler_params=pltpu.CompilerParams(has_side_effects=_EFFECT),
    )(*[_in_hbm(a) for a in list(srcs) + list(lands)], dep)
    return outs[0], outs[1], list(outs[2:2 + n]), list(outs[2 + n:2 + n + nl]), outs[-1]


def _split_wait(name, send_sems, recv_sems, srcs, lands, after, plan):
    n, nl = len(srcs), len(lands)

    def body(*refs):
        src_refs, land_refs = refs[:n], refs[n:n + nl]
        s_sems, r_sems = refs[n + nl], refs[n + nl + 1]
        sends, recvs, own = plan(src_refs, land_refs, s_sems, r_sems)
        for cp in own:
            cp.wait()
        for cp in recvs:
            cp.wait_recv()
        for cp in sends:
            cp.wait_send()

    outs = pl.pallas_call(
        body, name=name,
        in_specs=[_HBM] * (n + nl) + [_SEM, _SEM, pl.BlockSpec(memory_space=pl.ANY)],
        out_specs=[_HBM] * (n + nl),
        out_shape=[pltpu.HBM(a.shape, a.dtype) for a in list(srcs) + list(lands)],
        input_output_aliases={i: i for i in range(n + nl)},
        compiler_params=pltpu.CompilerParams(has_side_effects=_EFFECT),
    )(*srcs, *lands, send_sems, recv_sems, after)
    return list(outs[:n]), list(outs[n:])


def _gather_plan(srcs, lands, ss, rs):
    x, y, _ = _mesh_pos()
    chips = _other_chips(x, y)
    sends = [_gather_copy(srcs, lands, ss, rs, a, k, 2 * x + y) for a in range(len(srcs)) for k in range(3)]
    recvs = [_gather_copy(srcs, lands, ss, rs, a, k, 2 * chips[k][0] + chips[k][1])
             for a in range(len(srcs)) for k in range(3)]
    own = [pltpu.make_async_copy(srcs[a], lands[a].at[2 * x + y], rs.at[3 * len(srcs) + a]) for a in range(len(srcs))]
    return sends, recvs, own


def _scatter_and_spread_plan(srcs, lands, ss, rs):
    x, y, c = _mesh_pos()
    me = 4 * x + 2 * y + c
    n = len(srcs) - 1
    cps = [_scatter_copy(srcs[:n], lands[:n], ss, rs, a, k) for a in range(n) for k in range(3)]
    for f in range(1, 8):
        peer = ((x + (f >> 2)) % 2, (y + ((f >> 1) & 1)) % 2, (c + (f & 1)) % 2)
        cps.append(pltpu.make_async_remote_copy(
            src_ref=srcs[n], dst_ref=lands[n].at[me], send_sem=ss.at[3 * n - 1 + f], recv_sem=rs.at[3 * n - 1 + f],
            device_id=peer, device_id_type=MESH))
    own = [pltpu.make_async_copy(srcs[n], lands[n].at[me], rs.at[3 * n + 7])]
    return cps, cps, own


def _swap_with_sibling(parts, name):
    n = len(parts)

    def body(*refs):
        srcs, dsts = refs[:n], refs[n:2 * n]
        send_sems, recv_sems = refs[2 * n:]
        x, y, c = _mesh_pos()
        cps = [pltpu.make_async_remote_copy(src_ref=srcs[a], dst_ref=dsts[a], send_sem=send_sems.at[a],
                                            recv_sem=recv_sems.at[a], device_id=(x, y, 1 - c), device_id_type=MESH)
               for a in range(n)]
        for cp in cps:
            cp.start()
        for cp in cps:
            cp.wait_recv()
        for cp in cps:
            cp.wait_send()

    return pl.pallas_call(
        body, name=name, in_specs=[_HBM] * n, out_specs=[_HBM] * n,
        out_shape=[jax.ShapeDtypeStruct(p.shape, p.dtype) for p in parts],
        scratch_shapes=[pltpu.SemaphoreType.DMA((n,)), pltpu.SemaphoreType.DMA((n,))],
    )(*parts)


def _forward(x, tgt, w_in_t, mlp_w_fn, g1, bfp, wp, scale, g2, gf, dep):
    h, qkv, u, fl = _rms_inproj(x, g1, w_in_t, dep)
    qaug, kaug = _fox_cumsum(fl, bfp)
    attn, lse = _attn_fwd(qkv, qaug, kaug)
    pooled, pool = _pool_fwd(u, wp, scale)
    wo, wgt, wut, wd = mlp_w_fn(attn)
    x1, h2 = _outproj(x, attn, pool, wo, g2)
    loss, dgf, dx2, dx2b, ud, silu, a_b = _mlp_fwd_loss(h2, x1, wgt, wut, wd, tgt, gf)
    saved = dict(h=h, qkv=qkv, fl=fl, qaug=qaug, kaug=kaug, attn=attn, lse=lse, pooled=pooled, pool=pool, x1=x1, h2=h2,
                 ud=ud, silu=silu, a_b=a_b, wo=wo, wgt=wgt, wut=wut, wd=wd)
    return loss, dgf, dx2, dx2b, saved


def _backward_mlp(sv, dx2, dx2b, g2):
    dgate, dup, dx1, dx1b, dg2 = _mlp_bwd(dx2b, dx2, sv["ud"], sv["silu"], sv["wgt"], sv["wut"], sv["wd"], sv["x1"], g2)
    dwd, dwgt, dwut = _dw_mlp([sv["a_b"], dgate, dup], [0, 1, 1], [dx2b, sv["h2"]])
    return dx1, dx1b, dg2, (dwgt, dwut, dwd)


def _backward_outproj_pool(sv, dx1b, wp, scale):
    dattn, dpool = _outproj_bwd(dx1b, sv["wo"])
    (dwo,) = _mm_tn_rows([sv["attn"], sv["pool"]], dx1b, "dw_out", tk=2048, stacked=True)
    dwo = dwo.reshape(NSH, D // NSH, D)
    du, dscale, dwp = _pool_bwd(dpool, sv["pooled"], wp, scale)
    return dattn, dwo, du, dscale, dwp


def _backward_attn_inproj(sv, x, dx1, dattn, du, w_in_t, g1, bfp, dep):
    dq, dqs, dk, dks, dv = _attn_bwd(sv["qkv"], sv["qaug"], sv["kaug"], sv["attn"], dattn, sv["lse"], dep)
    df, dbf = _fox_cumsum_bwd(dqs, dks, sv["fl"], bfp)
    dx, dg1 = _inproj_bwd(dq, dk, dv, du, df, w_in_t, x, dx1, g1)
    dwq, dwk, dwv, dwu_in, dwf = _mm_tn_rows([dq, dk, dv, du, df], sv["h"], "dw_in")
    dwin = jnp.concatenate([dwq, dwk, dwv, dwf[0:8], dwu_in], axis=0)
    return dx, dg1, dbf, dwin.reshape(NSH, IN_S, D)


def kernel(x, norm1_g, w_in, b_forget, w_pool, pool_scale, w_out, norm2_g, w_gate, w_up, w_down, final_g, loss_target, m_norm1_g, m_w_in, m_b_forget, m_w_pool, m_pool_scale, m_w_out, m_norm2_g, m_w_gate, m_w_up, m_w_down, m_final_g, v_norm1_g, v_w_in, v_b_forget, v_w_pool, v_pool_scale, v_w_out, v_norm2_g, v_w_gate, v_w_up, v_w_down, v_final_g):
    mine = (2 * lax.axis_index("x") + lax.axis_index("y")).astype(jnp.int32)
    mine1 = mine.reshape(1)
    tr = lambda a: jnp.transpose(a[0])

    win4 = _all_gather_w_in(tr(w_in).astype(bf16))
    later = [w_out[0].astype(bf16), tr(w_gate).astype(bf16), tr(w_up).astype(bf16), w_down[0].astype(bf16)]
    lands = [lax.empty((NSH,) + p.shape, bf16) for p in later]
    ag_send, ag_recv, later_thru, lands_thru, ag_token = _split_start("all_gather_start", later, lands, 16, _gather_plan,
                                                                      win4)
    win = win4.reshape(IN_W, D)
    w_in_t = jnp.concatenate([win[0:3 * AW], win[3 * AW + 8:], win[3 * AW:3 * AW + 8], jnp.zeros((120, D), bf16)], axis=0)
    bfp = jnp.pad(b_forget, ((0, 0), (0, 120)))
    wp = w_pool[0].astype(bf16)
    gf = final_g.reshape(1, D)

    def later_weights(after):
        _, (wo4, wgt, wut, wd) = _split_wait("all_gather_wait", ag_send, ag_recv, later_thru, lands_thru, after, _gather_plan)
        return wo4.reshape(D, D), wgt, wut, wd

    xe, tgt = x[0], loss_target[0]
    loss_v, dgf, dx2, dx2b, sv = _forward(xe, tgt, w_in_t, later_weights, norm1_g, bfp, wp, pool_scale, norm2_g, gf, ag_token)
    dx1, dx1b, dg2, mlp_grads = _backward_mlp(sv, dx2, dx2b, norm2_g)
    dattn, dwo, du, dscale, dwp = _backward_outproj_pool(sv, dx1b, wp, pool_scale)
    first = [dwo] + list(mlp_grads) + [dwp.reshape(512, 128)]
    first_lands = [lax.empty((3,) + g.shape[1:], bf16) for g in first[:4]] + [lax.empty((8, 512, 128), f32)]
    rs_send, rs_recv, first_thru, first_lands_thru, rs_token = _split_start(
        "reduce_scatter_start", first, first_lands, 20, _scatter_and_spread_plan, du)
    dx, dg1, dbf, dwin = _backward_attn_inproj(sv, xe, dx1, dattn, du, w_in_t, norm1_g, bfp, rs_token)

    pad8 = lambda r: jnp.pad(r, ((0, 8 - r.shape[0]), (0, 0)))
    loss_rows = jnp.concatenate([dbf, jnp.zeros((6, 128), f32), loss_v[0:1, :]], axis=0)
    small = jnp.concatenate([dg1.reshape(8, 128), dg2.reshape(8, 128), dgf.reshape(8, 128), pad8(dscale.reshape(4, 128)),
                             loss_rows], axis=0)
    tail_send, tail_recv, tail_thru, tail_lands_thru, tail_token = _split_start(
        "tail_start", [dwin, small], [lax.empty((3,) + dwin.shape[1:], bf16), lax.empty((8, SMALL_ROWS, 128), f32)], 11,
        _scatter_and_spread_plan, dx)
    first_thru, first_recv = _split_wait("reduce_scatter_wait", rs_send, rs_recv, first_thru, first_lands_thru, tail_token,
                                         _scatter_and_spread_plan)
    wp_all = first_recv[4]
    tr3 = lambda a: jnp.transpose(a, (2, 0, 1))
    ws = [tr3(w_in), w_out[0], tr(w_gate), tr(w_up), w_down[0]]
    ms = [tr3(m_w_in), m_w_out[0], tr(m_w_gate), tr(m_w_up), m_w_down[0]]
    vs = [tr3(v_w_in), v_w_out[0], tr(v_w_gate), tr(v_w_up), v_w_down[0]]
    partial = _sum4(first_recv[:4], first_thru[:4], mine1, "sum4_first", SHARD_STEPS)
    other = _swap_with_sibling(partial, "swap_first")
    big = _adamw_shards(ws[1:], ms[1:], vs[1:], partial, other, "adamw_first", SHARD_STEPS)
    (dwin_thru, _), (in_recv_land, small_all) = _split_wait("tail_wait", tail_send, tail_recv, tail_thru, tail_lands_thru,
                                                            big[3][0], _scatter_and_spread_plan)
    partial_in = _sum4([in_recv_land], [dwin_thru], mine1, "sum4_in", 1)
    other_in = _swap_with_sibling(partial_in, "swap_in")
    big = _adamw_shards(ws[:1], ms[:1], vs[:1], partial_in, other_in, "adamw_in", 1) + big

    small_names = ["norm1_g", "norm2_g", "final_g", "pool_scale", "b_forget", "w_pool"]
    rows = lambda a, b, c, d, e, f: [a.reshape(8, 128), b.reshape(8, 128), c.reshape(8, 128), d.reshape(4, 128),
                                     e.reshape(1, 8), f.reshape(512, 128)]
    sm, loss_row = _adamw_small(rows(norm1_g, norm2_g, final_g, pool_scale, b_forget, w_pool),
                                rows(m_norm1_g, m_norm2_g, m_final_g, m_pool_scale, m_b_forget, m_w_pool),
                                rows(v_norm1_g, v_norm2_g, v_final_g, v_pool_scale, v_b_forget, v_w_pool), small_all, wp_all)
    small_shape = dict(norm1_g=(1, D), norm2_g=(1, D), final_g=(D,), pool_scale=(1, AW), b_forget=(1, 8),
                       w_pool=(1, 4, 128, 128))

    order = ["norm1_g", "w_in", "b_forget", "w_pool", "pool_scale", "w_out", "norm2_g", "w_gate", "w_up", "w_down", "final_g"]
    big_idx = {"w_in": 0, "w_out": 1, "w_gate": 2, "w_up": 3, "w_down": 4}
    outs = [loss_row[0, 0], dx[None]]
    for kind in range(4):
        for name in order:
            if name == "w_in":
                outs.append(jnp.transpose(big[0][kind], (1, 2, 0)))
            elif name in ("w_gate", "w_up"):
                outs.append(jnp.transpose(big[big_idx[name]][kind])[None])
            elif name in big_idx:
                outs.append(big[big_idx[name]][kind][None])
            else:
                outs.append(sm[6 * kind + small_names.index(name)].reshape(small_shape[name]))
    return tuple(outs)
```

```python
import jax
import jax.numpy as jnp
import numpy as np
from jax import lax
from jax.experimental import pallas as pl
from jax.experimental.pallas import tpu as pltpu

f32 = jnp.float32
bf16 = jnp.bfloat16

T = 4096
D = 1024
NSH = 4
IN_W = 2056
IN_S = IN_W // NSH
AW = 512
PAIRS = 4
SPARE = (64, 0)
ROW_SUM_LANE, COL_SUM_LANE = 0, 3
FF = 2816
FS = FF // NSH
WINDOWS = (2, 4, 8, 16)
HALO = 32
EPS = 1e-6
NEG = -1e30
LR, B1, B2, AEPS, WD, STEP = 0.001, 0.9, 0.999, 1e-08, 0.01, 10
SMALL_ROWS = 40

NT = (((1,), (1,)), ((), ()))
TN = (((0,), (0,)), ((), ()))

MESH = pl.DeviceIdType.MESH


def _cp(*sem):
    return pltpu.CompilerParams(dimension_semantics=sem)


def _full(shape):
    n = len(shape)
    return pl.BlockSpec(shape, lambda *_: (0,) * n)


def _resident(shape):
    n = len(shape)
    return pl.BlockSpec(shape, lambda *_: (0,) * n, pipeline_mode=pl.Buffered(1))


W_ROWS = 4 * AW + 128


def _rms_inproj(x, g1, w, dep):
    tm = 512

    def body(x_ref, g_ref, w_ref, dep_ref, h_ref, qkv_ref, u_ref, fl_ref):
        xv = x_ref[...]
        r = lax.rsqrt(jnp.mean(xv * xv, axis=-1, keepdims=True) + EPS)
        h = (xv * r * g_ref[...]).astype(bf16)
        h_ref[...] = h
        qkv_ref[...] = lax.dot_general(h, w_ref[0:3 * AW, :], NT, preferred_element_type=f32).astype(bf16)
        u_ref[...] = lax.dot_general(h, w_ref[3 * AW:4 * AW, :], NT, preferred_element_type=f32)
        fl_ref[...] = lax.dot_general(h, w_ref[4 * AW:W_ROWS, :], NT, preferred_element_type=f32)

    return pl.pallas_call(
        body, name="rms_inproj", grid=(T // tm,),
        in_specs=[pl.BlockSpec((tm, D), lambda i: (i, 0)), _full((1, D)), _full((W_ROWS, D)), _full((8, 128))],
        out_specs=[pl.BlockSpec((tm, D), lambda i: (i, 0)), pl.BlockSpec((tm, 3 * AW), lambda i: (i, 0)),
                   pl.BlockSpec((tm, AW), lambda i: (i, 0)), pl.BlockSpec((tm, 128), lambda i: (i, 0))],
        out_shape=[jax.ShapeDtypeStruct((T, D), bf16), jax.ShapeDtypeStruct((T, 3 * AW), bf16),
                   jax.ShapeDtypeStruct((T, AW), f32), jax.ShapeDtypeStruct((T, 128), f32)],
        compiler_params=_cp("parallel"),
    )(x, g1, w, dep)


CUMSUM_ROWS = 512
FS_CHUNKS = ((0, 256), (256, 512), (512, FS))


def _log_sigmoid(z):
    return jnp.minimum(z, 0.0) - jnp.log(1.0 + jnp.exp(-jnp.abs(z)))


def _split3(x):
    hi = x.astype(bf16)
    r1 = x - hi.astype(f32)
    mid = r1.astype(bf16)
    return hi, mid, (r1 - mid.astype(f32)).astype(bf16)


def _dot01(sel, x, sel_first):
    parts = _split3(x)
    if sel_first:
        return sum(jnp.dot(sel, p, preferred_element_type=f32) for p in parts)
    return sum(jnp.dot(p, sel, preferred_element_type=f32) for p in parts)


def _fox_cumsum(fl, bfp):
    tb = CUMSUM_ROWS
    nb = T // tb

    def body(fl_ref, b_ref, qa_ref, ka_ref, carry):
        i = pl.program_id(0)

        @pl.when(i == 0)
        def _():
            carry[...] = jnp.zeros_like(carry)

        lf = _log_sigmoid(fl_ref[...] + b_ref[...])
        r = lax.broadcasted_iota(jnp.int32, (tb, tb), 0)
        cc = lax.broadcasted_iota(jnp.int32, (tb, tb), 1)
        ltri = (cc <= r).astype(bf16)
        cb = _dot01(ltri, lf, True) + carry[0:1, :]
        carry[...] = jnp.broadcast_to(cb[tb - 1:tb, :], (8, 128))
        hi, mid, lo = _split3(cb * LOG2E)
        lane = lax.broadcasted_iota(jnp.int32, (tb, 128), 1)
        terms = jnp.where(lane < 8, hi.astype(f32),
                          jnp.where(lane < 16, pltpu.roll(mid.astype(f32), 8, 1), pltpu.roll(lo.astype(f32), 16, 1)))
        terms = terms.astype(bf16)
        src = lax.broadcasted_iota(jnp.int32, (128, AW), 0)
        col = lax.broadcasted_iota(jnp.int32, (128, AW), 1)
        head, term = src & 7, src >> 3
        base = 128 * (head >> 1) + jnp.where((head & 1) == 0, SPARE[0], SPARE[1])
        place = lambda off: jnp.logical_and(col == base + off + term, term < 3).astype(bf16)
        cq = jnp.dot(terms, place(0), preferred_element_type=f32)
        ck = jnp.dot(terms, place(3), preferred_element_type=f32)
        within = jnp.bitwise_and(lax.broadcasted_iota(jnp.int32, (tb, AW), 1), 63)
        qa_ref[...] = jnp.where(jnp.logical_and(within >= 3, within <= 5), 1.0, cq).astype(bf16)
        ka_ref[...] = jnp.where(within <= 2, 1.0, -ck).astype(bf16)

    return pl.pallas_call(
        body, name="fox_cumsum", grid=(nb,),
        in_specs=[pl.BlockSpec((tb, 128), lambda i: (i, 0)), _full((1, 128))],
        out_specs=[pl.BlockSpec((tb, AW), lambda i: (i, 0)), pl.BlockSpec((tb, AW), lambda i: (i, 0))],
        out_shape=[jax.ShapeDtypeStruct((T, AW), bf16), jax.ShapeDtypeStruct((T, AW), bf16)],
        scratch_shapes=[pltpu.VMEM((8, 128), f32)],
        compiler_params=_cp("arbitrary"),
    )(fl, bfp)


ATT_T = 512
LOG2E = 1.4426950408889634
Q_SCALE = 0.125 * LOG2E


def _causal_steps(key_major):
    n = T // ATT_T
    if key_major:
        pairs = [(i, j) for j in range(n) for i in range(j, n)]
    else:
        pairs = [(i, j) for i in range(n) for j in range(i + 1)]
    it = np.array([p[0] for p in pairs], np.int32)
    jt = np.array([p[1] for p in pairs], np.int32)
    return jnp.asarray(it), jnp.asarray(jt)


def _row_blocks(tq, tk, on_diagonal):
    return ((0, tq // 2, tk // 2), (tq // 2, tq, tk)) if on_diagonal else ((0, tq, tk),)


def _attn_fwd(qkv, qaug, kaug):
    tq = tk = ATT_T
    it, jt = _causal_steps(False)
    nsteps = it.shape[0]

    rs = 64

    def body(it_ref, jt_ref, q_ref, k_ref, v_ref, qa_ref, ka_ref, o_ref, lse_ref, m_sc, acc_sc, s_sc, p_sc, alpha_sc):
        t = pl.program_id(1)
        i = it_ref[t]
        j = jt_ref[t]

        @pl.when(j == 0)
        def _():
            m_sc[...] = jnp.full_like(m_sc, NEG)
            acc_sc[...] = jnp.zeros_like(acc_sc)

        lane = lax.broadcasted_iota(jnp.int32, (tq, 128), 1)
        spare = SPARE

        def step(on_diagonal):
            q = (q_ref[...].astype(f32) * Q_SCALE).astype(bf16)
            k = k_ref[...]
            v = v_ref[...]
            qa = qa_ref[...]
            ka = ka_ref[...]
            blocks = _row_blocks(tq, tk, on_diagonal)
            for e in range(2):
                hm = (lane >= 64) if e else (lane < 64)
                qe = jnp.where(hm, q, qa)
                ke = jnp.where(hm, k, ka)
                for r0, r1, nc in blocks:
                    s_sc[e, r0:r1, 0:nc] = lax.dot_general(qe[r0:r1], ke[0:nc], NT, preferred_element_type=f32)
            for e in range(2):
                for r0, r1, nc in blocks:
                    for r in range(r0, r1, rs):
                        s = s_sc[e, r:r + rs, 0:nc]
                        if on_diagonal:
                            row = lax.broadcasted_iota(jnp.int32, (rs, nc), 0) + r
                            col = lax.broadcasted_iota(jnp.int32, (rs, nc), 1)
                            s = jnp.where(col <= row, s, NEG)
                        m_prev = m_sc[e, r:r + rs, :]
                        m_new = jnp.maximum(m_prev, jnp.max(s, axis=1, keepdims=True))
                        p_sc[e, r:r + rs, 0:nc] = jnp.exp2(s - jnp.tile(m_new, (1, nc // 128))).astype(bf16)
                        alpha_sc[e, r:r + rs, :] = jnp.exp2(m_prev - m_new)
                        m_sc[e, r:r + rs, :] = m_new
            for e in range(2):
                hm = (lane >= 64) if e else (lane < 64)
                ve = jnp.where(hm, v, (lane == spare[e]).astype(bf16))
                for r0, r1, nc in blocks:
                    acc_sc[e, r0:r1] = (alpha_sc[e, r0:r1] * acc_sc[e, r0:r1]
                                        + jnp.dot(p_sc[e, r0:r1, 0:nc], ve[0:nc], preferred_element_type=f32))

        @pl.when(j < i)
        def _():
            step(False)

        @pl.when(j == i)
        def _():
            step(True)
            l0 = acc_sc[0][:, spare[0]:spare[0] + 1]
            l1 = acc_sc[1][:, spare[1]:spare[1] + 1]
            o_ref[...] = jnp.where(lane < 64, acc_sc[0] / l0, acc_sc[1] / l1).astype(bf16)
            lse_ref[...] = jnp.where(lane < 64, m_sc[0] + jnp.log2(l0), m_sc[1] + jnp.log2(l1))

    qmap = lambda p, t, it, jt: (it[t], p)
    kmap = lambda p, t, it, jt: (jt[t], p)
    grid_spec = pltpu.PrefetchScalarGridSpec(
        num_scalar_prefetch=2, grid=(PAIRS, nsteps),
        in_specs=[pl.BlockSpec((tq, 128), qmap),
                  pl.BlockSpec((tk, 128), lambda p, t, it, jt: (jt[t], PAIRS + p)),
                  pl.BlockSpec((tk, 128), lambda p, t, it, jt: (jt[t], 2 * PAIRS + p)),
                  pl.BlockSpec((tq, 128), qmap), pl.BlockSpec((tk, 128), kmap)],
        out_specs=[pl.BlockSpec((tq, 128), qmap),
                   pl.BlockSpec((None, tq, 128), lambda p, t, it, jt: (p, it[t], 0))],
        scratch_shapes=[pltpu.VMEM((2, tq, 128), f32), pltpu.VMEM((2, tq, 128), f32), pltpu.VMEM((2, tq, tk), f32),
                        pltpu.VMEM((2, tq, tk), bf16), pltpu.VMEM((2, tq, 128), f32)],
    )
    return pl.pallas_call(
        body, name="fox_attn_fwd", grid_spec=grid_spec,
        out_shape=[jax.ShapeDtypeStruct((T, AW), bf16), jax.ShapeDtypeStruct((PAIRS, T, 128), f32)],
        compiler_params=_cp("parallel", "arbitrary"),
    )(it, jt, qkv, qkv, qkv, qaug, kaug)


def _pool_fwd(u, wp, scale):
    tm = 1024

    def body(u_ref, wp_ref, sc_ref, pooled_ref, pool_ref, ext, lv):
        i = pl.program_id(0)

        @pl.when(i == 0)
        def _():
            ext[0:HALO, :] = jnp.zeros((HALO, AW), f32)

        uv = u_ref[...]
        ext[HALO:HALO + tm, :] = uv
        t_idx = i * tm + lax.broadcasted_iota(jnp.int32, (tm, 1), 0)
        for g, w in enumerate(WINDOWS):
            lo, hi = 128 * g, 128 * (g + 1)
            ug = uv[:, lo:hi]
            steps = w.bit_length() - 1
            assert 1 << steps == w and 8 * steps <= HALO
            for k in range(1, steps + 1):
                back = 1 << (k - 1)
                first = 8 * k if k < steps else HALO
                rd = (lambda a, b: ext[a:b, lo:hi]) if k == 1 else (lambda a, b, s=k % 2: lv[s, a:b, :])
                cur = rd(first, tm + HALO) + rd(first - back, tm + HALO - back)
                if k < steps:
                    lv[(k + 1) % 2, first:tm + HALO, :] = cur
            acc = cur
            cnt = jnp.minimum(t_idx + 1, w).astype(f32)
            pb = (acc / cnt - ug).astype(bf16)
            pooled_ref[:, lo:hi] = pb
            mixed = jnp.dot(pb, wp_ref[g], preferred_element_type=f32)
            pool_ref[:, lo:hi] = (mixed * sc_ref[:, lo:hi]).astype(bf16)
        ext[0:HALO, :] = uv[tm - HALO:tm, :]

    return pl.pallas_call(
        body, name="pool_fwd", grid=(T // tm,),
        in_specs=[pl.BlockSpec((tm, AW), lambda i: (i, 0)), _full((4, 128, 128)), _full((1, AW))],
        out_specs=[pl.BlockSpec((tm, AW), lambda i: (i, 0)), pl.BlockSpec((tm, AW), lambda i: (i, 0))],
        out_shape=[jax.ShapeDtypeStruct((T, AW), bf16), jax.ShapeDtypeStruct((T, AW), bf16)],
        scratch_shapes=[pltpu.VMEM((tm + HALO, AW), f32), pltpu.VMEM((2, tm + HALO, 128), f32)],
        compiler_params=_cp("arbitrary"),
    )(u, wp, scale)


def _outproj(x, attn, pool, wo, g2):
    tm = 1024

    def body(x_ref, a_ref, p_ref, wo_ref, g_ref, x1_ref, h2_ref):
        mixed = jnp.concatenate([a_ref[...], p_ref[...]], axis=1)
        x1 = x_ref[...] + jnp.dot(mixed, wo_ref[...], preferred_element_type=f32)
        x1_ref[...] = x1
        r = lax.rsqrt(jnp.mean(x1 * x1, axis=-1, keepdims=True) + EPS)
        h2_ref[...] = (x1 * r * g_ref[...]).astype(bf16)

    return pl.pallas_call(
        body, name="outproj", grid=(T // tm,),
        in_specs=[pl.BlockSpec((tm, D), lambda i: (i, 0)), pl.BlockSpec((tm, AW), lambda i: (i, 0)),
                  pl.BlockSpec((tm, AW), lambda i: (i, 0)), _full((D, D)), _full((1, D))],
        out_specs=[pl.BlockSpec((tm, D), lambda i: (i, 0)), pl.BlockSpec((tm, D), lambda i: (i, 0))],
        out_shape=[jax.ShapeDtypeStruct((T, D), f32), jax.ShapeDtypeStruct((T, D), bf16)],
        compiler_params=_cp("parallel"),
    )(x, attn, pool, wo, g2)


def _mlp_fwd_loss(h2, x1, wg, wu, wd, tgt, gf):
    tm = 512

    def body(h_ref, x1_ref, wg_ref, wu_ref, wd_ref, t_ref, g_ref,
             loss_ref, dg_ref, dx_ref, dxb_ref, ud_ref, silu_ref, a_ref, x2):
        i = pl.program_id(0)
        s = pl.program_id(1)

        @pl.when(jnp.logical_and(i == 0, s == 0))
        def _():
            loss_ref[...] = jnp.zeros_like(loss_ref)
            dg_ref[...] = jnp.zeros_like(dg_ref)

        h = h_ref[...]
        gus = [(lax.dot_general(h, wg_ref[s, c0:c1, :], NT, preferred_element_type=f32),
                lax.dot_general(h, wu_ref[s, c0:c1, :], NT, preferred_element_type=f32)) for c0, c1 in FS_CHUNKS]
        for (c0, c1), (gate, up) in zip(FS_CHUNKS, gus):
            sg = jax.nn.sigmoid(gate)
            silu = gate * sg
            ud_ref[:, c0:c1] = (up * (sg * (1.0 + gate * (1.0 - sg)))).astype(bf16)
            silu_ref[:, c0:c1] = silu.astype(bf16)
            a_ref[:, c0:c1] = (silu * up).astype(bf16)
        part = jnp.dot(a_ref[...], wd_ref[s], preferred_element_type=f32)

        @pl.when(s == 0)
        def _():
            x2[...] = x1_ref[...] + part

        @pl.when(s > 0)
        def _():
            x2[...] += part

        @pl.when(s == NSH - 1)
        def _():
            xv = x2[...]
            g = g_ref[...]
            r = lax.rsqrt(jnp.mean(xv * xv, axis=-1, keepdims=True) + EPS)
            xhat = xv * r
            e = xhat * g - t_ref[...]
            loss_ref[...] += 0.5 * jnp.sum(jnp.mean(e * e, axis=-1, keepdims=True))
            dy = e * (1.0 / D)
            dg_ref[...] += jnp.sum(dy * xhat, axis=0, keepdims=True)
            z = dy * g
            dx = r * (z - xhat * jnp.mean(z * xhat, axis=-1, keepdims=True))
            dx_ref[...] = dx
            dxb_ref[...] = dx.astype(bf16)

    row = lambda i, s: (i, 0)
    sl = lambda i, s: (s, i, 0)
    wsl = lambda i, s: (s, 0, 0)
    return pl.pallas_call(
        body, name="mlp_fwd_loss", grid=(T // tm, NSH),
        in_specs=[pl.BlockSpec((tm, D), row), pl.BlockSpec((tm, D), row),
                  _resident((NSH, FS, D)), _resident((NSH, FS, D)), _resident((NSH, FS, D)),
                  pl.BlockSpec((tm, D), row), pl.BlockSpec((1, D), lambda i, s: (0, 0))],
        out_specs=[pl.BlockSpec((8, 128), lambda i, s: (0, 0)), pl.BlockSpec((1, D), lambda i, s: (0, 0)),
                   pl.BlockSpec((tm, D), row), pl.BlockSpec((tm, D), row),
                   pl.BlockSpec((None, tm, FS), sl), pl.BlockSpec((None, tm, FS), sl), pl.BlockSpec((None, tm, FS), sl)],
        out_shape=[jax.ShapeDtypeStruct((8, 128), f32), jax.ShapeDtypeStruct((1, D), f32),
                   jax.ShapeDtypeStruct((T, D), f32), jax.ShapeDtypeStruct((T, D), bf16)]
        + [jax.ShapeDtypeStruct((NSH, T, FS), bf16)] * 3,
        scratch_shapes=[pltpu.VMEM((tm, D), f32)],
        compiler_params=_cp("arbitrary", "arbitrary"),
    )(h2, x1, wg, wu, wd, tgt, gf)


def _mlp_bwd(dx2b, dx2, ud, silu, wg, wu, wd, x1, g2):
    tm = 512

    def body(dxb_ref, dx_ref, ud_ref, silu_ref, wg_ref, wu_ref, wd_ref, x1_ref, g_ref,
             dg_ref, du_ref, dx1_ref, dx1b_ref, dn_ref, acc):
        i = pl.program_id(0)
        s = pl.program_id(1)

        @pl.when(jnp.logical_and(i == 0, s == 0))
        def _():
            dn_ref[...] = jnp.zeros_like(dn_ref)

        dxb = dxb_ref[...]
        das = [lax.dot_general(dxb, wd_ref[s, c0:c1, :], NT, preferred_element_type=f32) for c0, c1 in FS_CHUNKS]
        for (c0, c1), da in zip(FS_CHUNKS, das):
            dg_ref[:, c0:c1] = (da * ud_ref[:, c0:c1].astype(f32)).astype(bf16)
            du_ref[:, c0:c1] = (da * silu_ref[:, c0:c1].astype(f32)).astype(bf16)
        part = jnp.dot(dg_ref[...], wg_ref[s], preferred_element_type=f32)
        part = part + jnp.dot(du_ref[...], wu_ref[s], preferred_element_type=f32)

        @pl.when(s == 0)
        def _():
            acc[...] = part

        @pl.when(s > 0)
        def _():
            acc[...] += part

        @pl.when(s == NSH - 1)
        def _():
            xv = x1_ref[...]
            r = lax.rsqrt(jnp.mean(xv * xv, axis=-1, keepdims=True) + EPS)
            xhat = xv * r
            dh = acc[...]
            dn_ref[...] += jnp.sum(dh * xhat, axis=0, keepdims=True)
            z = dh * g_ref[...]
            dx1 = dx_ref[...] + r * (z - xhat * jnp.mean(z * xhat, axis=-1, keepdims=True))
            dx1_ref[...] = dx1
            dx1b_ref[...] = dx1.astype(bf16)

    row = lambda i, s: (i, 0)
    sl = lambda i, s: (s, i, 0)
    wsl = lambda i, s: (s, 0, 0)
    return pl.pallas_call(
        body, name="mlp_bwd", grid=(T // tm, NSH),
        in_specs=[pl.BlockSpec((tm, D), row), pl.BlockSpec((tm, D), row),
                  pl.BlockSpec((None, tm, FS), sl), pl.BlockSpec((None, tm, FS), sl),
                  _resident((NSH, FS, D)), _resident((NSH, FS, D)), _resident((NSH, FS, D)),
                  pl.BlockSpec((tm, D), row), pl.BlockSpec((1, D), lambda i, s: (0, 0))],
        out_specs=[pl.BlockSpec((None, tm, FS), sl), pl.BlockSpec((None, tm, FS), sl),
                   pl.BlockSpec((tm, D), row), pl.BlockSpec((tm, D), row), pl.BlockSpec((1, D), lambda i, s: (0, 0))],
        out_shape=[jax.ShapeDtypeStruct((NSH, T, FS), bf16)] * 2
        + [jax.ShapeDtypeStruct((T, D), f32), jax.ShapeDtypeStruct((T, D), bf16), jax.ShapeDtypeStruct((1, D), f32)],
        scratch_shapes=[pltpu.VMEM((tm, D), f32)],
        compiler_params=_cp("arbitrary", "arbitrary"),
    )(dx2b, dx2, ud, silu, wg, wu, wd, x1, g2)


def _dw_mlp(a_list, b_of, bs):
    na, steps = len(a_list), len(a_list) * NSH
    first_use = [b_of.index(k) * NSH for k in range(len(bs))]
    assert all(b_of[j] <= b_of[j + 1] for j in range(na - 1))

    def body(*refs):
        a_refs, b_refs, o_refs = refs[:na], refs[na:na + len(bs)], refs[na + len(bs):2 * na + len(bs)]
        abuf, bbuf, asem, bsem = refs[2 * na + len(bs):]
        s = pl.program_id(0)

        def fetch_a(step, start):
            slot = lax.rem(step, 2)
            for j in range(na):
                @pl.when(lax.div(step, NSH) == j)
                def _():
                    cp = pltpu.make_async_copy(a_refs[j].at[lax.rem(step, NSH)], abuf.at[slot], asem.at[slot])
                    cp.start() if start else cp.wait()

        b_copies = [pltpu.make_async_copy(b_refs[k], bbuf.at[k], bsem.at[k]) for k in range(len(bs))]

        @pl.when(s == 0)
        def _():
            fetch_a(s, True)
            b_copies[0].start()

        @pl.when(s + 1 < steps)
        def _():
            fetch_a(s + 1, True)

        for k in range(1, len(bs)):
            @pl.when(s == 1)
            def _():
                b_copies[k].start()

        fetch_a(s, False)
        for k in range(len(bs)):
            @pl.when(s == max(first_use[k], 0 if k == 0 else 2))
            def _():
                b_copies[k].wait()

        which_b = sum((s >= first_use[k]).astype(jnp.int32) for k in range(1, len(bs)))
        upd = lax.dot_general(abuf[lax.rem(s, 2)], bbuf[which_b], TN, preferred_element_type=f32).astype(bf16)
        for j in range(na):
            @pl.when(lax.div(s, NSH) == j)
            def _():
                o_refs[j][...] = upd

    def out_spec(j):
        return pl.BlockSpec((None, FS, D), lambda s: (jnp.clip(s - NSH * j, 0, NSH - 1), 0, 0))

    return pl.pallas_call(
        body, name="dw_mlp", grid=(steps,),
        in_specs=[pl.BlockSpec(memory_space=pl.ANY)] * (na + len(bs)),
        out_specs=[out_spec(j) for j in range(na)],
        out_shape=[jax.ShapeDtypeStruct((NSH, FS, D), bf16)] * na,
        scratch_shapes=[pltpu.VMEM((2, T, FS), bf16), pltpu.VMEM((len(bs), T, D), bf16),
                        pltpu.SemaphoreType.DMA((2,)), pltpu.SemaphoreType.DMA((len(bs),))],
        compiler_params=_cp("arbitrary"),
    )(*a_list, *bs)


def _mm_tn_rows(a_list, b, name, tk=1024, out_dtype=bf16, stacked=False):
    na = len(a_list)
    n = b.shape[-1]
    nk = T // tk
    ms = [a.shape[-1] for a in a_list]
    starts = [sum(ms[:k]) for k in range(na)] if stacked else [0] * na
    out_rows = [sum(ms)] if stacked else ms

    def body(*refs):
        a_refs, b_ref = refs[:na], refs[na]
        no = len(out_rows)
        o_refs, accs = refs[na + 1:na + 1 + no], refs[na + 1 + no:]
        kk = pl.program_id(0)
        bv = b_ref[...]
        for k, a_ref in enumerate(a_refs):
            o_ref, acc = (o_refs[0], accs[0]) if stacked else (o_refs[k], accs[k])
            rows = pl.ds(starts[k], ms[k])
            upd = lax.dot_general(a_ref[...], bv, TN, preferred_element_type=f32)

            @pl.when(kk == 0)
            def _():
                acc[rows, :] = upd

            @pl.when(kk > 0)
            def _():
                acc[rows, :] += upd

            @pl.when(kk == nk - 1)
            def _():
                o_ref[rows, :] = acc[rows, :].astype(out_dtype)

    return pl.pallas_call(
        body, name=name, grid=(nk,),
        in_specs=[pl.BlockSpec((tk, m), lambda k: (k, 0)) for m in ms] + [pl.BlockSpec((tk, n), lambda k: (k, 0))],
        out_specs=[pl.BlockSpec((r, n), lambda k: (0, 0)) for r in out_rows],
        out_shape=[jax.ShapeDtypeStruct((r, n), out_dtype) for r in out_rows],
        scratch_shapes=[pltpu.VMEM((r, n), f32) for r in out_rows],
        compiler_params=_cp("arbitrary"),
    )(*a_list, b)


def _outproj_bwd(dx1b, wo):
    tm = 1024

    def body(dx_ref, wo_ref, da_ref, dp_ref):
        dx = dx_ref[...]
        da_ref[...] = lax.dot_general(dx, wo_ref[0:AW, :], NT, preferred_element_type=f32).astype(bf16)
        dp_ref[...] = lax.dot_general(dx, wo_ref[AW:2 * AW, :], NT, preferred_element_type=f32)

    return pl.pallas_call(
        body, name="outproj_bwd", grid=(T // tm,),
        in_specs=[pl.BlockSpec((tm, D), lambda i: (i, 0)), _full((D, D))],
        out_specs=[pl.BlockSpec((tm, AW), lambda i: (i, 0)), pl.BlockSpec((tm, AW), lambda i: (i, 0))],
        out_shape=[jax.ShapeDtypeStruct((T, AW), bf16), jax.ShapeDtypeStruct((T, AW), f32)],
        compiler_params=_cp("parallel"),
    )(dx1b, wo)


def _pool_bwd(dpool, pooled, wp, scale):
    tm = 1024
    n = T // tm

    def body(dp_ref, pb_ref, wp_ref, sc_ref, du_ref, dsc_ref, dwp_ref, ext, lv):
        i = pl.program_id(0)

        @pl.when(i == 0)
        def _():
            ext[tm:tm + HALO, :] = jnp.zeros((HALO, AW), f32)
            dsc_ref[...] = jnp.zeros_like(dsc_ref)
            dwp_ref[...] = jnp.zeros_like(dwp_ref)

        t_idx = (n - 1 - i) * tm + lax.broadcasted_iota(jnp.int32, (tm, 1), 0)
        for g, w in enumerate(WINDOWS):
            lo, hi = 128 * g, 128 * (g + 1)
            pb = pb_ref[:, lo:hi]
            mixed = jnp.dot(pb, wp_ref[g], preferred_element_type=f32)
            dpo = dp_ref[:, lo:hi]
            dsc_ref[:, lo:hi] += jnp.sum(dpo * mixed, axis=0, keepdims=True)
            dmr = (dpo * sc_ref[:, lo:hi]).astype(bf16)
            dwp_ref[g] += lax.dot_general(pb, dmr, TN, preferred_element_type=f32)
            dpl = lax.dot_general(dmr, wp_ref[g], NT, preferred_element_type=f32)
            cnt = jnp.minimum(t_idx + 1, w).astype(f32)
            dpn = dpl / cnt
            ext[0:tm, lo:hi] = dpn
            steps = w.bit_length() - 1
            assert 1 << steps == w and 8 * steps <= HALO
            for k in range(1, steps + 1):
                ahead = 1 << (k - 1)
                last = tm + HALO - 8 * k if k < steps else tm
                rd = (lambda a, b: ext[a:b, lo:hi]) if k == 1 else (lambda a, b, s=k % 2: lv[s, a:b, :])
                cur = rd(0, last) + rd(ahead, last + ahead)
                if k < steps:
                    lv[(k + 1) % 2, 0:last, :] = cur
            du_ref[:, lo:hi] = (cur - dpl).astype(bf16)
        ext[tm:tm + HALO, :] = ext[0:HALO, :]

    rev = lambda i: (n - 1 - i, 0)
    return pl.pallas_call(
        body, name="pool_bwd", grid=(n,),
        in_specs=[pl.BlockSpec((tm, AW), rev), pl.BlockSpec((tm, AW), rev), _full((4, 128, 128)), _full((1, AW))],
        out_specs=[pl.BlockSpec((tm, AW), rev), _full((1, AW)), _full((4, 128, 128))],
        out_shape=[jax.ShapeDtypeStruct((T, AW), bf16), jax.ShapeDtypeStruct((1, AW), f32),
                   jax.ShapeDtypeStruct((4, 128, 128), f32)],
        scratch_shapes=[pltpu.VMEM((tm + HALO, AW), f32), pltpu.VMEM((2, tm + HALO, 128), f32)],
        compiler_params=_cp("arbitrary"),
    )(dpool, pooled, wp, scale)


def _attn_bwd(qkv, qaug, kaug, attn, dattn, lse, dep):
    tq = tk = ATT_T
    n = T // tq
    it, jt = _causal_steps(True)
    nsteps = it.shape[0]

    rs = 64

    def body(it_ref, jt_ref, q_ref, k_ref, v_ref, qa_ref, ka_ref, o_ref, do_ref, lse_ref, dep_ref,
             dq_ref, dqs_ref, dk_ref, dks_ref, dv_ref, dq_acc, dk_acc, dv_acc, s_sc, dp_sc, p_sc, ds_sc):
        t = pl.program_id(1)
        i = it_ref[t]
        j = jt_ref[t]

        @pl.when(t == 0)
        def _():
            dq_acc[...] = jnp.zeros_like(dq_acc)

        @pl.when(i == j)
        def _():
            dk_acc[...] = jnp.zeros_like(dk_acc)
            dv_acc[...] = jnp.zeros_like(dv_acc)

        lane = lax.broadcasted_iota(jnp.int32, (tq, 128), 1)

        def step(on_diagonal):
            q = (q_ref[...].astype(f32) * Q_SCALE).astype(bf16)
            k = k_ref[...]
            v = v_ref[...]
            qa = qa_ref[...]
            ka = ka_ref[...]
            do = do_ref[...]
            dd = do.astype(f32) * o_ref[...].astype(f32)
            blocks = _row_blocks(tq, tk, on_diagonal)
            qes, kes, does, deltas = [], [], [], []
            for e in range(2):
                hm = (lane >= 64) if e else (lane < 64)
                qes.append(jnp.where(hm, q, qa))
                kes.append(jnp.where(hm, k, ka))
                does.append(jnp.where(hm, do, jnp.zeros_like(do)))
                deltas.append(jnp.sum(jnp.where(hm, dd, 0.0), axis=1, keepdims=True))
                for r0, r1, nc in blocks:
                    s_sc[e, r0:r1, 0:nc] = lax.dot_general(qes[e][r0:r1], kes[e][0:nc], NT, preferred_element_type=f32)
                    dp_sc[e, r0:r1, 0:nc] = lax.dot_general(does[e][r0:r1], v[0:nc], NT, preferred_element_type=f32)
            for e in range(2):
                for r0, r1, nc in blocks:
                    for r in range(r0, r1, rs):
                        s = s_sc[e, r:r + rs, 0:nc] - lse_ref[r:r + rs, 64 * e:64 * e + 1]
                        if on_diagonal:
                            row = lax.broadcasted_iota(jnp.int32, (rs, nc), 0) + r
                            col = lax.broadcasted_iota(jnp.int32, (rs, nc), 1)
                            s = jnp.where(col <= row, s, NEG)
                        p = jnp.exp2(s)
                        p_sc[e, r:r + rs, 0:nc] = p.astype(bf16)
                        ds_sc[e, r:r + rs, 0:nc] = (p * (dp_sc[e, r:r + rs, 0:nc] - deltas[e][r:r + rs, :])).astype(bf16)
                for r0, r1, nc in blocks:
                    dv_acc[:, 0:nc] += lax.dot_general(does[e][r0:r1], p_sc[e, r0:r1, 0:nc], TN, preferred_element_type=f32)
                    dsb = ds_sc[e, r0:r1, 0:nc]
                    dk_acc[e, :, 0:nc] += lax.dot_general(qes[e][r0:r1], dsb, TN, preferred_element_type=f32)
                    rq = pl.multiple_of(i * tq + r0, r1 - r0)
                    dq_acc[e, pl.ds(rq, r1 - r0), :] += jnp.dot(dsb, kes[e][0:nc], preferred_element_type=f32)

        @pl.when(i > j)
        def _():
            step(False)

        @pl.when(i == j)
        def _():
            step(True)

        @pl.when(i == n - 1)
        def _():
            dk0 = dk_acc[0].T
            dk1 = dk_acc[1].T
            dk_ref[...] = (jnp.where(lane < 64, dk0, dk1) * (1.0 / LOG2E)).astype(bf16)
            dks_ref[...] = jnp.where(lane < 64, dk1, dk0)
            dv_ref[...] = dv_acc[...].T.astype(bf16)

        @pl.when(t == nsteps - 1)
        def _():
            lane_t = lax.broadcasted_iota(jnp.int32, (T, 128), 1)
            dq_ref[...] = (jnp.where(lane_t < 64, dq_acc[0], dq_acc[1]) * 0.125).astype(bf16)
            dqs_ref[...] = jnp.where(lane_t < 64, dq_acc[1], dq_acc[0])

    qmap = lambda p, t, it, jt: (it[t], p)
    grid_spec = pltpu.PrefetchScalarGridSpec(
        num_scalar_prefetch=2, grid=(PAIRS, nsteps),
        in_specs=[pl.BlockSpec((tq, 128), qmap),
                  pl.BlockSpec((tk, 128), lambda p, t, it, jt: (jt[t], PAIRS + p)),
                  pl.BlockSpec((tk, 128), lambda p, t, it, jt: (jt[t], 2 * PAIRS + p)),
                  pl.BlockSpec((tq, 128), qmap), pl.BlockSpec((tk, 128), lambda p, t, it, jt: (jt[t], p)),
                  pl.BlockSpec((tq, 128), qmap), pl.BlockSpec((tq, 128), qmap),
                  pl.BlockSpec((None, tq, 128), lambda p, t, it, jt: (p, it[t], 0)),
                  pl.BlockSpec((8, 128), lambda p, t, it, jt: (0, 0))],
        out_specs=[pl.BlockSpec((T, 128), lambda p, t, it, jt: (0, p)),
                   pl.BlockSpec((None, T, 128), lambda p, t, it, jt: (p, 0, 0)),
                   pl.BlockSpec((tk, 128), lambda p, t, it, jt: (jt[t], p)),
                   pl.BlockSpec((None, tk, 128), lambda p, t, it, jt: (p, jt[t], 0)),
                   pl.BlockSpec((tk, 128), lambda p, t, it, jt: (jt[t], p))],
        scratch_shapes=[pltpu.VMEM((2, T, 128), f32), pltpu.VMEM((2, 128, tk), f32), pltpu.VMEM((128, tk), f32),
                        pltpu.VMEM((2, tq, tk), f32), pltpu.VMEM((2, tq, tk), f32), pltpu.VMEM((2, tq, tk), bf16),
                        pltpu.VMEM((2, tq, tk), bf16)],
    )
    return pl.pallas_call(
        body, name="fox_attn_bwd", grid_spec=grid_spec,
        out_shape=[jax.ShapeDtypeStruct((T, AW), bf16), jax.ShapeDtypeStruct((PAIRS, T, 128), f32),
                   jax.ShapeDtypeStruct((T, AW), bf16), jax.ShapeDtypeStruct((PAIRS, T, 128), f32),
                   jax.ShapeDtypeStruct((T, AW), bf16)],
        compiler_params=_cp("parallel", "arbitrary"),
    )(it, jt, qkv, qkv, qkv, qaug, kaug, attn, dattn, lse, dep)


def _fox_cumsum_bwd(dqs, dks, fl, bfp):
    tb = CUMSUM_ROWS
    nb = T // tb

    def body(dqs_ref, dks_ref, fl_ref, b_ref, df_ref, db_ref, carry):
        i = pl.program_id(0)

        @pl.when(i == 0)
        def _():
            carry[...] = jnp.zeros_like(carry)
            db_ref[...] = jnp.zeros_like(db_ref)

        r = lax.broadcasted_iota(jnp.int32, (128, 128), 0)
        cc = lax.broadcasted_iota(jnp.int32, (128, 128), 1)
        lane = lax.broadcasted_iota(jnp.int32, (tb, 128), 1)
        even_at, odd_at = SPARE[0] + ROW_SUM_LANE, SPARE[1] + ROW_SUM_LANE
        both = jnp.zeros((tb, 128), f32)
        pick = jnp.zeros((128, 128), jnp.bool_)
        for p in range(PAIRS):
            diff = dqs_ref[p] - pltpu.roll(dks_ref[p], 128 - (COL_SUM_LANE - ROW_SUM_LANE), 1)
            moved = pltpu.roll(diff, p, 1) if p else diff
            both = jnp.where(jnp.logical_or(lane == even_at + p, lane == odd_at + p), moved, both)
            pick = jnp.logical_or(pick, jnp.logical_or(jnp.logical_and(r == even_at + p, cc == 2 * p),
                                                       jnp.logical_and(r == odd_at + p, cc == 2 * p + 1)))
        dc = _dot01(pick.astype(bf16), both, False)
        rt = lax.broadcasted_iota(jnp.int32, (tb, tb), 0)
        ct = lax.broadcasted_iota(jnp.int32, (tb, tb), 1)
        utri = (ct >= rt).astype(bf16)
        dl = _dot01(utri, dc, True) + carry[0:1, :]
        carry[...] = jnp.broadcast_to(dl[0:1, :], (8, 128))
        z = fl_ref[...] + b_ref[...]
        df = dl * jax.nn.sigmoid(-z)
        df_ref[...] = df.astype(bf16)
        db_ref[...] += jnp.sum(df, axis=0, keepdims=True)

    rev = lambda i: (nb - 1 - i, 0)
    return pl.pallas_call(
        body, name="fox_cumsum_bwd", grid=(nb,),
        in_specs=[pl.BlockSpec((PAIRS, tb, 128), lambda i: (0, nb - 1 - i, 0)),
                  pl.BlockSpec((PAIRS, tb, 128), lambda i: (0, nb - 1 - i, 0)),
                  pl.BlockSpec((tb, 128), rev), _full((1, 128))],
        out_specs=[pl.BlockSpec((tb, 128), rev), _full((1, 128))],
        out_shape=[jax.ShapeDtypeStruct((T, 128), bf16), jax.ShapeDtypeStruct((1, 128), f32)],
        scratch_shapes=[pltpu.VMEM((8, 128), f32)],
        compiler_params=_cp("arbitrary"),
    )(dqs, dks, fl, bfp)


def _inproj_bwd(dq, dk, dv, du, df, w, x, dx1, g1):
    tm = 512

    def body(dq_ref, dk_ref, dv_ref, du_ref, df_ref, w_ref, x_ref, dx1_ref, g_ref, dx_ref, dn_ref):
        i = pl.program_id(0)

        @pl.when(i == 0)
        def _():
            dn_ref[...] = jnp.zeros_like(dn_ref)

        dproj = jnp.concatenate([dq_ref[...], dk_ref[...], dv_ref[...], du_ref[...], df_ref[...]], axis=1)
        dh = jnp.dot(dproj, w_ref[...], preferred_element_type=f32)
        xv = x_ref[...]
        r = lax.rsqrt(jnp.mean(xv * xv, axis=-1, keepdims=True) + EPS)
        xhat = xv * r
        dn_ref[...] += jnp.sum(dh * xhat, axis=0, keepdims=True)
        z = dh * g_ref[...]
        dx_ref[...] = dx1_ref[...] + r * (z - xhat * jnp.mean(z * xhat, axis=-1, keepdims=True))

    row = lambda i: (i, 0)
    return pl.pallas_call(
        body, name="inproj_bwd", grid=(T // tm,),
        in_specs=[pl.BlockSpec((tm, AW), row)] * 4 + [pl.BlockSpec((tm, 128), row), _full((W_ROWS, D)),
                                                       pl.BlockSpec((tm, D), row), pl.BlockSpec((tm, D), row), _full((1, D))],
        out_specs=[pl.BlockSpec((tm, D), row), _full((1, D))],
        out_shape=[jax.ShapeDtypeStruct((T, D), f32), jax.ShapeDtypeStruct((1, D), f32)],
        compiler_params=_cp("arbitrary"),
    )(dq, dk, dv, du, df, w, x, dx1, g1)


def _adamw_math(w, g, m, v):
    m = B1 * m + (1.0 - B1) * g
    v = B2 * v + (1.0 - B2) * (g * g)
    m_hat = m / (1.0 - B1 ** STEP)
    v_hat = v / (1.0 - B2 ** STEP)
    delta = -LR * (m_hat / (jnp.sqrt(v_hat) + AEPS) + WD * w)
    return delta, m, v


SHARD_STEPS = 4


def _adamw_shards(ws, ms, vs, ps_mine, ps_other, name, steps):
    n = len(ws)

    def body(*refs):
        ins, outs = refs[:5 * n], refs[5 * n:]
        for k in range(n):
            w_ref, m_ref, v_ref, a_ref, b_ref = ins[5 * k:5 * k + 5]
            g_ref, d_ref, nm_ref, nv_ref = outs[4 * k:4 * k + 4]
            g = (a_ref[...].astype(f32) + b_ref[...].astype(f32)).reshape(w_ref.shape)
            g_ref[...] = g
            d_ref[...], nm_ref[...], nv_ref[...] = _adamw_math(w_ref[...], g, m_ref[...], v_ref[...])

    in_specs, out_specs, out_shape = [], [], []
    for w, p in zip(ws, ps_mine):
        rest = tuple(w.shape[1:])
        tr = w.shape[0] // steps
        assert tr * steps == w.shape[0]
        spec = pl.BlockSpec((tr,) + rest, lambda i, _n=len(rest): (i,) + (0,) * _n)
        pspec = pl.BlockSpec((tr, p.shape[1]), lambda i: (i, 0))
        in_specs += [spec] * 3 + [pspec] * 2
        out_specs += [spec] * 4
        out_shape += [jax.ShapeDtypeStruct(w.shape, f32)] * 4
    args = [a for k in range(n) for a in (ws[k], ms[k], vs[k], ps_mine[k], ps_other[k])]
    res = pl.pallas_call(
        body, name=name, grid=(steps,), in_specs=in_specs, out_specs=out_specs, out_shape=out_shape,
        compiler_params=_cp("parallel"),
    )(*args)
    return [res[4 * k:4 * k + 4] for k in range(n)]


SMALL_SLOTS = ((0, 8, 128), (8, 16, 128), (16, 24, 128), (24, 28, 128), (32, 33, 8))
LOSS_ROW = 39


def _adamw_small(ws, ms, vs, parts, parts_wp):
    n = len(ws)

    def body(*refs):
        w_refs, m_refs, v_refs = refs[0:n], refs[n:2 * n], refs[2 * n:3 * n]
        p_ref, pw_ref = refs[3 * n], refs[3 * n + 1]
        outs = refs[3 * n + 2:]
        g_all = p_ref[0]
        g_wp = pw_ref[0]
        for k in range(1, 8):
            g_all = g_all + p_ref[k]
            g_wp = g_wp + pw_ref[k]
        grads = [g_all[r0:r1, 0:lanes] for r0, r1, lanes in SMALL_SLOTS] + [g_wp]
        for idx, g in enumerate(grads):
            d, nm, nv = _adamw_math(w_refs[idx][...], g, m_refs[idx][...], v_refs[idx][...])
            outs[idx][...] = g
            outs[n + idx][...] = d
            outs[2 * n + idx][...] = nm
            outs[3 * n + idx][...] = nv
        outs[4 * n][...] = g_all[LOSS_ROW:LOSS_ROW + 1, :]

    shapes = [jax.ShapeDtypeStruct(w.shape, f32) for w in ws]
    res = pl.pallas_call(
        body, name="adamw_small", out_shape=shapes * 4 + [jax.ShapeDtypeStruct((1, 128), f32)],
    )(*ws, *ms, *vs, parts, parts_wp)
    return res[:4 * n], res[4 * n]


def _sum4(recvs, gs, mine, name, steps):
    n = len(recvs)

    def body(mine_ref, *refs):
        for r_ref, g_ref, o_ref in zip(refs[:n], refs[n:2 * n], refs[2 * n:]):
            o_ref[...] = ((g_ref[...].astype(f32) + r_ref[0].astype(f32))
                          + (r_ref[1].astype(f32) + r_ref[2].astype(f32))).astype(bf16)

    r_specs, g_specs, o_specs, shapes = [], [], [], []
    for recv in recvs:
        _, rows, cols = recv.shape
        tr = rows // steps
        assert tr * steps == rows
        r_specs.append(pl.BlockSpec((3, tr, cols), lambda i, m: (0, i, 0)))
        g_specs.append(pl.BlockSpec((None, tr, cols), lambda i, m: (m[0], i, 0)))
        o_specs.append(pl.BlockSpec((tr, cols), lambda i, m: (i, 0)))
        shapes.append(jax.ShapeDtypeStruct((rows, cols), bf16))
    grid_spec = pltpu.PrefetchScalarGridSpec(num_scalar_prefetch=1, grid=(steps,), in_specs=r_specs + g_specs,
                                             out_specs=o_specs)
    return pl.pallas_call(
        body, name=name, grid_spec=grid_spec, out_shape=shapes, compiler_params=_cp("arbitrary"),
    )(mine, *recvs, *gs)


_HBM = pl.BlockSpec(memory_space=pltpu.HBM)
_SEM = pl.BlockSpec(memory_space=pltpu.SEMAPHORE)
_EFFECT = pltpu.SideEffectType.DATAFLOW_SIDE_EFFECTING


def _in_hbm(a):
    return pltpu.with_memory_space_constraint(a, pltpu.HBM)


def _mesh_pos():
    return lax.axis_index("x"), lax.axis_index("y"), lax.axis_index("c")


def _other_chips(x, y):
    return [(1 - x, y), (x, 1 - y), (1 - x, 1 - y)]


def _gather_copy(srcs, lands, send_sems, recv_sems, a, k, slot):
    x, y, c = _mesh_pos()
    cx, cy = _other_chips(x, y)[k]
    return pltpu.make_async_remote_copy(
        src_ref=srcs[a], dst_ref=lands[a].at[slot], send_sem=send_sems.at[3 * a + k], recv_sem=recv_sems.at[3 * a + k],
        device_id=(cx, cy, c), device_id_type=MESH)


def _scatter_copy(srcs, lands, send_sems, recv_sems, a, k):
    x, y, c = _mesh_pos()
    cx, cy = _other_chips(x, y)[k]
    return pltpu.make_async_remote_copy(
        src_ref=srcs[a].at[2 * cx + cy], dst_ref=lands[a].at[k], send_sem=send_sems.at[3 * a + k],
        recv_sem=recv_sems.at[3 * a + k], device_id=(cx, cy, c), device_id_type=MESH)


def _all_gather_w_in(part):
    cols = part.shape[1] // 2

    def body(src, dst, send_sems, recv_sems, loc_sem):
        x, y, c = _mesh_pos()
        mine = 2 * x + y
        chips = _other_chips(x, y)
        half = lambda ref, cc: ref.at[:, pl.ds(pl.multiple_of(cc * cols, cols), cols)]

        def over_ici(k, slot):
            cx, cy = chips[k]
            return pltpu.make_async_remote_copy(
                src_ref=half(src, c), dst_ref=half(dst.at[slot], c), send_sem=send_sems.at[k], recv_sem=recv_sems.at[k],
                device_id=(cx, cy, c), device_id_type=MESH)

        def to_sibling(k, cc):
            slot = 2 * chips[k][0] + chips[k][1]
            return pltpu.make_async_remote_copy(
                src_ref=half(dst.at[slot], cc), dst_ref=half(dst.at[slot], cc), send_sem=send_sems.at[3 + k],
                recv_sem=recv_sems.at[3 + k], device_id=(x, y, 1 - c), device_id_type=MESH)

        local = pltpu.make_async_copy(src, dst.at[mine], loc_sem.at[0])
        local.start()
        first = [over_ici(k, mine) for k in range(3)]
        for cp in first:
            cp.start()
        passed = [to_sibling(k, c) for k in range(3)]
        for k in range(3):
            over_ici(k, 2 * chips[k][0] + chips[k][1]).wait_recv()
            passed[k].start()
        for k in range(3):
            to_sibling(k, 1 - c).wait_recv()
        for cp in first + passed:
            cp.wait_send()
        local.wait()

    return pl.pallas_call(
        body, name="all_gather_w_in", in_specs=[_HBM], out_specs=_HBM,
        out_shape=jax.ShapeDtypeStruct((NSH,) + part.shape, part.dtype),
        scratch_shapes=[pltpu.SemaphoreType.DMA((6,)), pltpu.SemaphoreType.DMA((6,)), pltpu.SemaphoreType.DMA((1,))],
    )(part)


def _split_start(name, srcs, lands, n_sems, plan, dep):
    n, nl = len(srcs), len(lands)

    def body(*refs):
        src_refs, land_refs = refs[:n], refs[n:n + nl]
        send_sems, recv_sems = refs[n + nl + 1], refs[n + nl + 2]
        token = refs[-1]
        sends, _, own = plan(src_refs, land_refs, send_sems, recv_sems)
        for cp in own + sends:
            cp.start()
        token[...] = jnp.zeros_like(token)

    outs = pl.pallas_call(
        body, name=name,
        in_specs=[_HBM] * (n + nl) + [pl.BlockSpec(memory_space=pl.ANY)],
        out_specs=[_SEM, _SEM] + [_HBM] * (n + nl) + [pl.BlockSpec(memory_space=pltpu.VMEM)],
        out_shape=[pltpu.SemaphoreType.DMA((n_sems,)), pltpu.SemaphoreType.DMA((n_sems,))]
        + [pltpu.HBM(a.shape, a.dtype) for a in list(srcs) + list(lands)] + [jax.ShapeDtypeStruct((8, 128), f32)],
        input_output_aliases={i: 2 + i for i in range(n + nl)},
        compiler_params=pltpu.CompilerParams(has_side_effects=_EFFECT),
    )(*[_in_hbm(a) for a in list(srcs) + list(lands)], dep)
    return outs[0], outs[1], list(outs[2:2 + n]), list(outs[2 + n:2 + n + nl]), outs[-1]


def _split_wait(name, send_sems, recv_sems, srcs, lands, after, plan):
    n, nl = len(srcs), len(lands)

    def body(*refs):
        src_refs, land_refs = refs[:n], refs[n:n + nl]
        s_sems, r_sems = refs[n + nl], refs[n + nl + 1]
        sends, recvs, own = plan(src_refs, land_refs, s_sems, r_sems)
        for cp in own:
            cp.wait()
        for cp in recvs:
            cp.wait_recv()
        for cp in sends:
            cp.wait_send()

    outs = pl.pallas_call(
        body, name=name,
        in_specs=[_HBM] * (n + nl) + [_SEM, _SEM, pl.BlockSpec(memory_space=pl.ANY)],
        out_specs=[_HBM] * (n + nl),
        out_shape=[pltpu.HBM(a.shape, a.dtype) for a in list(srcs) + list(lands)],
        input_output_aliases={i: i for i in range(n + nl)},
        compiler_params=pltpu.CompilerParams(has_side_effects=_EFFECT),
    )(*srcs, *lands, send_sems, recv_sems, after)
    return list(outs[:n]), list(outs[n:])


def _gather_plan(srcs, lands, ss, rs):
    x, y, _ = _mesh_pos()
    chips = _other_chips(x, y)
    sends = [_gather_copy(srcs, lands, ss, rs, a, k, 2 * x + y) for a in range(len(srcs)) for k in range(3)]
    recvs = [_gather_copy(srcs, lands, ss, rs, a, k, 2 * chips[k][0] + chips[k][1])
             for a in range(len(srcs)) for k in range(3)]
    own = [pltpu.make_async_copy(srcs[a], lands[a].at[2 * x + y], rs.at[3 * len(srcs) + a]) for a in range(len(srcs))]
    return sends, recvs, own


def _scatter_and_spread_plan(srcs, lands, ss, rs):
    x, y, c = _mesh_pos()
    me = 4 * x + 2 * y + c
    n = len(srcs) - 1
    cps = [_scatter_copy(srcs[:n], lands[:n], ss, rs, a, k) for a in range(n) for k in range(3)]
    for f in range(1, 8):
        peer = ((x + (f >> 2)) % 2, (y + ((f >> 1) & 1)) % 2, (c + (f & 1)) % 2)
        cps.append(pltpu.make_async_remote_copy(
            src_ref=srcs[n], dst_ref=lands[n].at[me], send_sem=ss.at[3 * n - 1 + f], recv_sem=rs.at[3 * n - 1 + f],
            device_id=peer, device_id_type=MESH))
    own = [pltpu.make_async_copy(srcs[n], lands[n].at[me], rs.at[3 * n + 7])]
    return cps, cps, own


def _swap_with_sibling(parts, name):
    n = len(parts)

    def body(*refs):
        srcs, dsts = refs[:n], refs[n:2 * n]
        send_sems, recv_sems = refs[2 * n:]
        x, y, c = _mesh_pos()
        cps = [pltpu.make_async_remote_copy(src_ref=srcs[a], dst_ref=dsts[a], send_sem=send_sems.at[a],
                                            recv_sem=recv_sems.at[a], device_id=(x, y, 1 - c), device_id_type=MESH)
               for a in range(n)]
        for cp in cps:
            cp.start()
        for cp in cps:
            cp.wait_recv()
        for cp in cps:
            cp.wait_send()

    return pl.pallas_call(
        body, name=name, in_specs=[_HBM] * n, out_specs=[_HBM] * n,
        out_shape=[jax.ShapeDtypeStruct(p.shape, p.dtype) for p in parts],
        scratch_shapes=[pltpu.SemaphoreType.DMA((n,)), pltpu.SemaphoreType.DMA((n,))],
    )(*parts)


def _forward(x, tgt, w_in_t, mlp_w_fn, g1, bfp, wp, scale, g2, gf, dep):
    h, qkv, u, fl = _rms_inproj(x, g1, w_in_t, dep)
    qaug, kaug = _fox_cumsum(fl, bfp)
    attn, lse = _attn_fwd(qkv, qaug, kaug)
    pooled, pool = _pool_fwd(u, wp, scale)
    wo, wgt, wut, wd = mlp_w_fn(attn)
    x1, h2 = _outproj(x, attn, pool, wo, g2)
    loss, dgf, dx2, dx2b, ud, silu, a_b = _mlp_fwd_loss(h2, x1, wgt, wut, wd, tgt, gf)
    saved = dict(h=h, qkv=qkv, fl=fl, qaug=qaug, kaug=kaug, attn=attn, lse=lse, pooled=pooled, pool=pool, x1=x1, h2=h2,
                 ud=ud, silu=silu, a_b=a_b, wo=wo, wgt=wgt, wut=wut, wd=wd)
    return loss, dgf, dx2, dx2b, saved


def _backward_mlp(sv, dx2, dx2b, g2):
    dgate, dup, dx1, dx1b, dg2 = _mlp_bwd(dx2b, dx2, sv["ud"], sv["silu"], sv["wgt"], sv["wut"], sv["wd"], sv["x1"], g2)
    dwd, dwgt, dwut = _dw_mlp([sv["a_b"], dgate, dup], [0, 1, 1], [dx2b, sv["h2"]])
    return dx1, dx1b, dg2, (dwgt, dwut, dwd)


def _backward_outproj_pool(sv, dx1b, wp, scale):
    dattn, dpool = _outproj_bwd(dx1b, sv["wo"])
    (dwo,) = _mm_tn_rows([sv["attn"], sv["pool"]], dx1b, "dw_out", tk=T, stacked=True)
    dwo = dwo.reshape(NSH, D // NSH, D)
    du, dscale, dwp = _pool_bwd(dpool, sv["pooled"], wp, scale)
    return dattn, dwo, du, dscale, dwp


def _backward_attn_inproj(sv, x, dx1, dattn, du, w_in_t, g1, bfp, dep):
    dq, dqs, dk, dks, dv = _attn_bwd(sv["qkv"], sv["qaug"], sv["kaug"], sv["attn"], dattn, sv["lse"], dep)
    df, dbf = _fox_cumsum_bwd(dqs, dks, sv["fl"], bfp)
    dx, dg1 = _inproj_bwd(dq, dk, dv, du, df, w_in_t, x, dx1, g1)
    dwq, dwk, dwv, dwu_in, dwf = _mm_tn_rows([dq, dk, dv, du, df], sv["h"], "dw_in")
    dwin = jnp.concatenate([dwq, dwk, dwv, dwf[0:8], dwu_in], axis=0)
    return dx, dg1, dbf, dwin.reshape(NSH, IN_S, D)


def kernel(x, norm1_g, w_in, b_forget, w_pool, pool_scale, w_out, norm2_g, w_gate, w_up, w_down, final_g, loss_target, m_norm1_g, m_w_in, m_b_forget, m_w_pool, m_pool_scale, m_w_out, m_norm2_g, m_w_gate, m_w_up, m_w_down, m_final_g, v_norm1_g, v_w_in, v_b_forget, v_w_pool, v_pool_scale, v_w_out, v_norm2_g, v_w_gate, v_w_up, v_w_down, v_final_g):
    mine = (2 * lax.axis_index("x") + lax.axis_index("y")).astype(jnp.int32)
    mine1 = mine.reshape(1)
    tr = lambda a: jnp.transpose(a[0])

    win4 = _all_gather_w_in(tr(w_in).astype(bf16))
    later = [w_out[0].astype(bf16), tr(w_gate).astype(bf16), tr(w_up).astype(bf16), w_down[0].astype(bf16)]
    lands = [lax.empty((NSH,) + p.shape, bf16) for p in later]
    ag_send, ag_recv, later_thru, lands_thru, ag_token = _split_start("all_gather_start", later, lands, 16, _gather_plan,
                                                                      win4)
    win = win4.reshape(IN_W, D)
    w_in_t = jnp.concatenate([win[0:3 * AW], win[3 * AW + 8:], win[3 * AW:3 * AW + 8], jnp.zeros((120, D), bf16)], axis=0)
    bfp = jnp.pad(b_forget, ((0, 0), (0, 120)))
    wp = w_pool[0].astype(bf16)
    gf = final_g.reshape(1, D)

    def later_weights(after):
        _, (wo4, wgt, wut, wd) = _split_wait("all_gather_wait", ag_send, ag_recv, later_thru, lands_thru, after, _gather_plan)
        return wo4.reshape(D, D), wgt, wut, wd

    xe, tgt = x[0], loss_target[0]
    loss_v, dgf, dx2, dx2b, sv = _forward(xe, tgt, w_in_t, later_weights, norm1_g, bfp, wp, pool_scale, norm2_g, gf, ag_token)
    dx1, dx1b, dg2, mlp_grads = _backward_mlp(sv, dx2, dx2b, norm2_g)
    dattn, dwo, du, dscale, dwp = _backward_outproj_pool(sv, dx1b, wp, pool_scale)
    first = [dwo] + list(mlp_grads) + [dwp.reshape(512, 128)]
    first_lands = [lax.empty((3,) + g.shape[1:], bf16) for g in first[:4]] + [lax.empty((8, 512, 128), f32)]
    rs_send, rs_recv, first_thru, first_lands_thru, rs_token = _split_start(
        "reduce_scatter_start", first, first_lands, 20, _scatter_and_spread_plan, du)
    dx, dg1, dbf, dwin = _backward_attn_inproj(sv, xe, dx1, dattn, du, w_in_t, norm1_g, bfp, rs_token)

    pad8 = lambda r: jnp.pad(r, ((0, 8 - r.shape[0]), (0, 0)))
    loss_rows = jnp.concatenate([dbf, jnp.zeros((6, 128), f32), loss_v[0:1, :]], axis=0)
    small = jnp.concatenate([dg1.reshape(8, 128), dg2.reshape(8, 128), dgf.reshape(8, 128), pad8(dscale.reshape(4, 128)),
                             loss_rows], axis=0)
    tail_send, tail_recv, tail_thru, tail_lands_thru, tail_token = _split_start(
        "tail_start", [dwin, small], [lax.empty((3,) + dwin.shape[1:], bf16), lax.empty((8, SMALL_ROWS, 128), f32)], 11,
        _scatter_and_spread_plan, dx)
    first_thru, first_recv = _split_wait("reduce_scatter_wait", rs_send, rs_recv, first_thru, first_lands_thru, tail_token,
                                         _scatter_and_spread_plan)
    wp_all = first_recv[4]
    tr3 = lambda a: jnp.transpose(a, (2, 0, 1))
    ws = [tr3(w_in), w_out[0], tr(w_gate), tr(w_up), w_down[0]]
    ms = [tr3(m_w_in), m_w_out[0], tr(m_w_gate), tr(m_w_up), m_w_down[0]]
    vs = [tr3(v_w_in), v_w_out[0], tr(v_w_gate), tr(v_w_up), v_w_down[0]]
    partial = _sum4(first_recv[:4], first_thru[:4], mine1, "sum4_first", SHARD_STEPS)
    other = _swap_with_sibling(partial, "swap_first")
    big = _adamw_shards(ws[1:], ms[1:], vs[1:], partial, other, "adamw_first", SHARD_STEPS)
    (dwin_thru, _), (in_recv_land, small_all) = _split_wait("tail_wait", tail_send, tail_recv, tail_thru, tail_lands_thru,
                                                            big[3][0], _scatter_and_spread_plan)
    partial_in = _sum4([in_recv_land], [dwin_thru], mine1, "sum4_in", 1)
    other_in = _swap_with_sibling(partial_in, "swap_in")
    big = _adamw_shards(ws[:1], ms[:1], vs[:1], partial_in, other_in, "adamw_in", 1) + big

    small_names = ["norm1_g", "norm2_g", "final_g", "pool_scale", "b_forget", "w_pool"]
    rows = lambda a, b, c, d, e, f: [a.reshape(8, 128), b.reshape(8, 128), c.reshape(8, 128), d.reshape(4, 128),
                                     e.reshape(1, 8), f.reshape(512, 128)]
    sm, loss_row = _adamw_small(rows(norm1_g, norm2_g, final_g, pool_scale, b_forget, w_pool),
                                rows(m_norm1_g, m_norm2_g, m_final_g, m_pool_scale, m_b_forget, m_w_pool),
                                rows(v_norm1_g, v_norm2_g, v_final_g, v_pool_scale, v_b_forget, v_w_pool), small_all, wp_all)
    small_shape = dict(norm1_g=(1, D), norm2_g=(1, D), final_g=(D,), pool_scale=(1, AW), b_forget=(1, 8),
                       w_pool=(1, 4, 128, 128))

    order = ["norm1_g", "w_in", "b_forget", "w_pool", "pool_scale", "w_out", "norm2_g", "w_gate", "w_up", "w_down", "final_g"]
    big_idx = {"w_in": 0, "w_out": 1, "w_gate": 2, "w_up": 3, "w_down": 4}
    outs = [loss_row[0, 0], dx[None]]
    for kind in range(4):
        for name in order:
            if name == "w_in":
                outs.append(jnp.transpose(big[0][kind], (1, 2, 0)))
            elif name in ("w_gate", "w_up"):
                outs.append(jnp.transpose(big[big_idx[name]][kind])[None])
            elif name in big_idx:
                outs.append(big[big_idx[name]][kind][None])
            else:
                outs.append(sm[6 * kind + small_names.index(name)].reshape(small_shape[name]))
    return tuple(outs)
```
